```python
import jax, jax.numpy as jnp
from jax import lax
import numpy as np

D_MODEL = 2048
BATCH = 8
SEQ = 2048
DEPTH = 1

MLA_HEADS = 8
MLA_Q_RANK = 512
MLA_KV_RANK = 512
MLA_NOPE_DIM = 128
MLA_ROPE_DIM = 64
MLA_V_DIM = 128
MLA_WIDTH = MLA_HEADS * MLA_V_DIM
MLA_IN = MLA_Q_RANK + MLA_KV_RANK + MLA_ROPE_DIM
ROPE_THETA = 10000.0
Q_BLOCK = 128

RWKV_HEAD = 64
RWKV_HEADS = 16
RWKV_WIDTH = RWKV_HEADS * RWKV_HEAD
DECAY_RANK = 96
ICLR_RANK = 96
RWKV_SIZES = [RWKV_WIDTH, RWKV_WIDTH, RWKV_WIDTH, DECAY_RANK, DECAY_RANK, ICLR_RANK, ICLR_RANK]
RWKV_IN = sum(RWKV_SIZES)
GN_EPS = 64e-5
NORM_EPS = 1e-6

IN_SIZES = [MLA_IN, RWKV_IN, MLA_WIDTH, RWKV_WIDTH, D_MODEL, D_MODEL]
D_IN = sum(IN_SIZES)

kernel_name = "hybrid_mla_rwkv7_gated_encoder_block"


def _offsets(sizes):
    return [int(o) for o in np.cumsum(sizes)[:-1]]


def rms_norm(x, g, eps=NORM_EPS):
    xf = x.astype(jnp.float32)
    y = xf * lax.rsqrt(jnp.mean(xf * xf, axis=-1, keepdims=True) + eps)
    return (y * g.astype(jnp.float32)).astype(x.dtype)


def apply_rotary(t, cos, sin):
    tf = t.astype(jnp.float32)
    t1, t2 = jnp.split(tf, 2, axis=-1)
    return jnp.concatenate([t1 * cos - t2 * sin, t1 * sin + t2 * cos], axis=-1).astype(t.dtype)


def centred_shift(p):
    pad = jnp.pad(p, ((0, 0), (1, 1), (0, 0)))
    return 0.5 * (pad[:, :-2] + pad[:, 2:])


def mla_attention(q_nope, q_rope, k_nope, k_rope, v):
    B, S, H, _ = q_nope.shape
    nb = S // Q_BLOCK
    scale = (MLA_NOPE_DIM + MLA_ROPE_DIM) ** -0.5
    qn = q_nope.reshape(B, nb, Q_BLOCK, H, MLA_NOPE_DIM).transpose(1, 0, 2, 3, 4)
    qr = q_rope.reshape(B, nb, Q_BLOCK, H, MLA_ROPE_DIM).transpose(1, 0, 2, 3, 4)

    def block(args):
        qn_b, qr_b = args
        s = (jnp.einsum('bqhd,bkhd->bhqk', qn_b, k_nope, preferred_element_type=jnp.float32)
             + jnp.einsum('bqhr,bkr->bhqk', qr_b, k_rope, preferred_element_type=jnp.float32))
        p = jax.nn.softmax(s * scale, axis=-1)
        return jnp.einsum('bhqk,bkhd->bqhd', p.astype(v.dtype), v)

    o = lax.map(block, (qn, qr))
    return o.transpose(1, 0, 2, 3, 4).reshape(B, S, H * MLA_V_DIM)


def rwkv7_bidir_scan(r, w_f, w_b, k_f, k_b, v, kk, a_f, a_b):
    B, S, H, N = r.shape

    def tm(fwd, bwd):
        return jnp.stack([fwd, bwd[:, ::-1]], axis=0).transpose(2, 0, 1, 3, 4)

    xs = (tm(r, r), tm(w_f, w_b), tm(k_f, k_b), tm(v, v), tm(-kk, -kk), tm(kk * a_f, kk * a_b))

    def step(st, inp):
        r_t, w_t, k_t, v_t, a_t, b_t = inp
        sa = jnp.einsum('dbhij,dbhj->dbhi', st, a_t)
        st = st * w_t[..., None, :] + sa[..., None] * b_t[..., None, :] + v_t[..., None] * k_t[..., None, :]
        y = jnp.einsum('dbhij,dbhj->dbhi', st, r_t)
        return st, y

    s0 = jnp.zeros((2, B, H, N, N), jnp.float32)
    _, ys = lax.scan(step, s0, xs)
    y = ys[:, 0] + ys[::-1, 1]
    return y.transpose(1, 0, 2, 3)


def _fwd_setup_inputs(seed: int = 0) -> dict:
    key = jax.random.key(seed)
    ks = jax.random.split(key, 32)
    f32 = jnp.float32
    nrm = lambda k, shape, s: jax.random.normal(k, shape, f32) * s
    gain = lambda k, n: 1.0 + nrm(k, (n,), 0.02)
    return {
        "x": nrm(ks[0], (BATCH, SEQ, D_MODEL), 1.0),
        "g_pre": gain(ks[1], D_MODEL),
        "w_in": nrm(ks[2], (D_MODEL, D_IN), D_MODEL ** -0.5),
        "mla_q_norm": gain(ks[3], MLA_Q_RANK),
        "mla_wq_b": nrm(ks[4], (MLA_Q_RANK, MLA_HEADS * (MLA_NOPE_DIM + MLA_ROPE_DIM)), MLA_Q_RANK ** -0.5),
        "mla_kv_norm": gain(ks[5], MLA_KV_RANK),
        "mla_wkv_b": nrm(ks[6], (MLA_KV_RANK, MLA_HEADS * (MLA_NOPE_DIM + MLA_V_DIM)), MLA_KV_RANK ** -0.5),
        "rwkv_mu": jax.random.uniform(ks[7], (RWKV_IN,), f32),
        "rwkv_w0_f": nrm(ks[8], (RWKV_WIDTH,), 0.5),
        "rwkv_w2_f": nrm(ks[9], (DECAY_RANK, RWKV_WIDTH), 0.5 * DECAY_RANK ** -0.5),
        "rwkv_w0_b": nrm(ks[10], (RWKV_WIDTH,), 0.5),
        "rwkv_w2_b": nrm(ks[11], (DECAY_RANK, RWKV_WIDTH), 0.5 * DECAY_RANK ** -0.5),
        "rwkv_a0_f": nrm(ks[12], (RWKV_WIDTH,), 0.1),
        "rwkv_a2_f": nrm(ks[13], (ICLR_RANK, RWKV_WIDTH), 0.5 * ICLR_RANK ** -0.5),
        "rwkv_a0_b": nrm(ks[14], (RWKV_WIDTH,), 0.1),
        "rwkv_a2_b": nrm(ks[15], (ICLR_RANK, RWKV_WIDTH), 0.5 * ICLR_RANK ** -0.5),
        "rwkv_k_k": 0.85 + nrm(ks[16], (RWKV_WIDTH,), 0.02),
        "rwkv_k_a": gain(ks[17], RWKV_WIDTH),
        "rwkv_r_k": nrm(ks[18], (RWKV_HEADS, RWKV_HEAD), 0.1),
        "rwkv_gn_g": gain(ks[19], RWKV_WIDTH),
        "rwkv_gn_b": nrm(ks[20], (RWKV_WIDTH,), 0.01),
        "w_br_mla": nrm(ks[21], (MLA_WIDTH, D_MODEL), MLA_WIDTH ** -0.5),
        "w_br_rwkv": nrm(ks[22], (RWKV_WIDTH, D_MODEL), RWKV_WIDTH ** -0.5),
        "w_out": nrm(ks[23], (D_MODEL, D_MODEL), D_MODEL ** -0.5),
        "g_post": gain(ks[24], D_MODEL),
    }


def _fwd_reference(x, g_pre, w_in, mla_q_norm, mla_wq_b, mla_kv_norm, mla_wkv_b, rwkv_mu,
              rwkv_w0_f, rwkv_w2_f, rwkv_w0_b, rwkv_w2_b, rwkv_a0_f, rwkv_a2_f, rwkv_a0_b,
              rwkv_a2_b, rwkv_k_k, rwkv_k_a, rwkv_r_k, rwkv_gn_g, rwkv_gn_b, w_br_mla,
              w_br_rwkv, w_out, g_post):
    f32 = jnp.float32
    B, S, _ = x.shape
    pos = jnp.arange(S, dtype=f32)
    inv_freq = jnp.power(ROPE_THETA, -jnp.arange(0, MLA_ROPE_DIM, 2, dtype=f32) / MLA_ROPE_DIM)
    ang = pos[:, None] * inv_freq[None, :]
    cos, sin = jnp.cos(ang), jnp.sin(ang)

    for _layer in range(DEPTH):
        h = rms_norm(x, g_pre)
        proj = h @ w_in
        mla_in, rwkv_in, z_mla, z_rwkv, gate_mla, gate_rwkv = jnp.split(proj, _offsets(IN_SIZES), axis=-1)

        q_a, kv_a, k_rope = jnp.split(mla_in, _offsets([MLA_Q_RANK, MLA_KV_RANK, MLA_ROPE_DIM]), axis=-1)
        q = (rms_norm(q_a, mla_q_norm) @ mla_wq_b).reshape(B, S, MLA_HEADS, MLA_NOPE_DIM + MLA_ROPE_DIM)
        kv = (rms_norm(kv_a, mla_kv_norm) @ mla_wkv_b).reshape(B, S, MLA_HEADS, MLA_NOPE_DIM + MLA_V_DIM)
        q_nope, q_rope = q[..., :MLA_NOPE_DIM], q[..., MLA_NOPE_DIM:]
        k_nope, v_mla = kv[..., :MLA_NOPE_DIM], kv[..., MLA_NOPE_DIM:]
        q_rope = apply_rotary(q_rope, cos[:, None, :], sin[:, None, :])
        k_rope = apply_rotary(k_rope, cos, sin)
        y_mla = mla_attention(q_nope, q_rope, k_nope, k_rope, v_mla)

        rin = rwkv_in.astype(f32)
        rin = rin + rwkv_mu * (centred_shift(rin) - rin)
        r, k, v, wd_f, wd_b, ad_f, ad_b = jnp.split(rin, _offsets(RWKV_SIZES), axis=-1)

        def decay(w0, wd, w2):
            z = w0 + jnp.tanh(wd) @ w2
            return jnp.exp(-jnp.exp(-jax.nn.softplus(-z) - 0.5))

        w_f = decay(rwkv_w0_f.astype(f32), wd_f, rwkv_w2_f.astype(f32))
        w_b = decay(rwkv_w0_b.astype(f32), wd_b, rwkv_w2_b.astype(f32))
        a_f = jax.nn.sigmoid(rwkv_a0_f.astype(f32) + ad_f @ rwkv_a2_f.astype(f32))
        a_b = jax.nn.sigmoid(rwkv_a0_b.astype(f32) + ad_b @ rwkv_a2_b.astype(f32))

        hd = lambda t: t.reshape(B, S, RWKV_HEADS, RWKV_HEAD)
        kk = hd(k * rwkv_k_k.astype(f32))
        kk = kk / jnp.maximum(jnp.linalg.norm(kk, axis=-1, keepdims=True), 1e-12)
        k_a = rwkv_k_a.astype(f32)
        k_f = k * (1.0 + (a_f - 1.0) * k_a)
        k_b = k * (1.0 + (a_b - 1.0) * k_a)
        r_h, v_h, k_fh, k_bh = hd(r), hd(v), hd(k_f), hd(k_b)
        y = rwkv7_bidir_scan(r_h, hd(w_f), hd(w_b), k_fh, k_bh, v_h, kk, hd(a_f), hd(a_b))
        mu = jnp.mean(y, axis=-1, keepdims=True)
        var = jnp.mean(jnp.square(y - mu), axis=-1, keepdims=True)
        yn = ((y - mu) * lax.rsqrt(var + GN_EPS)).reshape(B, S, RWKV_WIDTH)
        yn = yn * rwkv_gn_g.astype(f32) + rwkv_gn_b.astype(f32)
        bonus = jnp.sum(r_h * (k_fh + k_bh) * rwkv_r_k.astype(f32), axis=-1, keepdims=True) * v_h
        y_rwkv = (yn + bonus.reshape(B, S, RWKV_WIDTH)).astype(x.dtype)

        u_mla = (y_mla * jax.nn.silu(z_mla)) @ w_br_mla
        u_rwkv = (y_rwkv * jax.nn.silu(z_rwkv)) @ w_br_rwkv
        merged = jax.nn.sigmoid(gate_mla) * u_mla + jax.nn.sigmoid(gate_rwkv) * u_rwkv
        out = merged @ w_out
        x = (x + rms_norm(out, g_post)).astype(x.dtype)
    return x


import jax as _jax
import jax.numpy as _jnp

TWIN_FORMAT = 'train_step'
FWD_PARAMS = ['x', 'g_pre', 'w_in', 'mla_q_norm', 'mla_wq_b', 'mla_kv_norm', 'mla_wkv_b', 'rwkv_mu', 'rwkv_w0_f', 'rwkv_w2_f', 'rwkv_w0_b', 'rwkv_w2_b', 'rwkv_a0_f', 'rwkv_a2_f', 'rwkv_a0_b', 'rwkv_a2_b', 'rwkv_k_k', 'rwkv_k_a', 'rwkv_r_k', 'rwkv_gn_g', 'rwkv_gn_b', 'w_br_mla', 'w_br_rwkv', 'w_out', 'g_post']
TWIN_WEIGHTS = ['g_pre', 'w_in', 'mla_q_norm', 'mla_wq_b', 'mla_kv_norm', 'mla_wkv_b', 'rwkv_mu', 'rwkv_w0_f', 'rwkv_w2_f', 'rwkv_w0_b', 'rwkv_w2_b', 'rwkv_a0_f', 'rwkv_a2_f', 'rwkv_a0_b', 'rwkv_a2_b', 'rwkv_k_k', 'rwkv_k_a', 'rwkv_r_k', 'rwkv_gn_g', 'rwkv_gn_b', 'w_br_mla', 'w_br_rwkv', 'w_out', 'g_post']
TWIN_DIFF_INPUT = 'x'
TWIN_INPUTS = ['x', 'g_pre', 'w_in', 'mla_q_norm', 'mla_wq_b', 'mla_kv_norm', 'mla_wkv_b', 'rwkv_mu', 'rwkv_w0_f', 'rwkv_w2_f', 'rwkv_w0_b', 'rwkv_w2_b', 'rwkv_a0_f', 'rwkv_a2_f', 'rwkv_a0_b', 'rwkv_a2_b', 'rwkv_k_k', 'rwkv_k_a', 'rwkv_r_k', 'rwkv_gn_g', 'rwkv_gn_b', 'w_br_mla', 'w_br_rwkv', 'w_out', 'g_post', 'loss_target', 'm_g_pre', 'm_w_in', 'm_mla_q_norm', 'm_mla_wq_b', 'm_mla_kv_norm', 'm_mla_wkv_b', 'm_rwkv_mu', 'm_rwkv_w0_f', 'm_rwkv_w2_f', 'm_rwkv_w0_b', 'm_rwkv_w2_b', 'm_rwkv_a0_f', 'm_rwkv_a2_f', 'm_rwkv_a0_b', 'm_rwkv_a2_b', 'm_rwkv_k_k', 'm_rwkv_k_a', 'm_rwkv_r_k', 'm_rwkv_gn_g', 'm_rwkv_gn_b', 'm_w_br_mla', 'm_w_br_rwkv', 'm_w_out', 'm_g_post', 'v_g_pre', 'v_w_in', 'v_mla_q_norm', 'v_mla_wq_b', 'v_mla_kv_norm', 'v_mla_wkv_b', 'v_rwkv_mu', 'v_rwkv_w0_f', 'v_rwkv_w2_f', 'v_rwkv_w0_b', 'v_rwkv_w2_b', 'v_rwkv_a0_f', 'v_rwkv_a2_f', 'v_rwkv_a0_b', 'v_rwkv_a2_b', 'v_rwkv_k_k', 'v_rwkv_k_a', 'v_rwkv_r_k', 'v_rwkv_gn_g', 'v_rwkv_gn_b', 'v_w_br_mla', 'v_w_br_rwkv', 'v_w_out', 'v_g_post']
TWIN_OUTPUTS = ['loss', 'grad_x', 'grad_g_pre', 'grad_w_in', 'grad_mla_q_norm', 'grad_mla_wq_b', 'grad_mla_kv_norm', 'grad_mla_wkv_b', 'grad_rwkv_mu', 'grad_rwkv_w0_f', 'grad_rwkv_w2_f', 'grad_rwkv_w0_b', 'grad_rwkv_w2_b', 'grad_rwkv_a0_f', 'grad_rwkv_a2_f', 'grad_rwkv_a0_b', 'grad_rwkv_a2_b', 'grad_rwkv_k_k', 'grad_rwkv_k_a', 'grad_rwkv_r_k', 'grad_rwkv_gn_g', 'grad_rwkv_gn_b', 'grad_w_br_mla', 'grad_w_br_rwkv', 'grad_w_out', 'grad_g_post', 'delta_g_pre', 'delta_w_in', 'delta_mla_q_norm', 'delta_mla_wq_b', 'delta_mla_kv_norm', 'delta_mla_wkv_b', 'delta_rwkv_mu', 'delta_rwkv_w0_f', 'delta_rwkv_w2_f', 'delta_rwkv_w0_b', 'delta_rwkv_w2_b', 'delta_rwkv_a0_f', 'delta_rwkv_a2_f', 'delta_rwkv_a0_b', 'delta_rwkv_a2_b', 'delta_rwkv_k_k', 'delta_rwkv_k_a', 'delta_rwkv_r_k', 'delta_rwkv_gn_g', 'delta_rwkv_gn_b', 'delta_w_br_mla', 'delta_w_br_rwkv', 'delta_w_out', 'delta_g_post', 'new_m_g_pre', 'new_m_w_in', 'new_m_mla_q_norm', 'new_m_mla_wq_b', 'new_m_mla_kv_norm', 'new_m_mla_wkv_b', 'new_m_rwkv_mu', 'new_m_rwkv_w0_f', 'new_m_rwkv_w2_f', 'new_m_rwkv_w0_b', 'new_m_rwkv_w2_b', 'new_m_rwkv_a0_f', 'new_m_rwkv_a2_f', 'new_m_rwkv_a0_b', 'new_m_rwkv_a2_b', 'new_m_rwkv_k_k', 'new_m_rwkv_k_a', 'new_m_rwkv_r_k', 'new_m_rwkv_gn_g', 'new_m_rwkv_gn_b', 'new_m_w_br_mla', 'new_m_w_br_rwkv', 'new_m_w_out', 'new_m_g_post', 'new_v_g_pre', 'new_v_w_in', 'new_v_mla_q_norm', 'new_v_mla_wq_b', 'new_v_mla_kv_norm', 'new_v_mla_wkv_b', 'new_v_rwkv_mu', 'new_v_rwkv_w0_f', 'new_v_rwkv_w2_f', 'new_v_rwkv_w0_b', 'new_v_rwkv_w2_b', 'new_v_rwkv_a0_f', 'new_v_rwkv_a2_f', 'new_v_rwkv_a0_b', 'new_v_rwkv_a2_b', 'new_v_rwkv_k_k', 'new_v_rwkv_k_a', 'new_v_rwkv_r_k', 'new_v_rwkv_gn_g', 'new_v_rwkv_gn_b', 'new_v_w_br_mla', 'new_v_w_br_rwkv', 'new_v_w_out', 'new_v_g_post']
TWIN_LEAF_KINDS = {'loss': 'loss', 'grad_x': 'grad_x', 'grad_g_pre': 'grad_w', 'grad_w_in': 'grad_w', 'grad_mla_q_norm': 'grad_w', 'grad_mla_wq_b': 'grad_w', 'grad_mla_kv_norm': 'grad_w', 'grad_mla_wkv_b': 'grad_w', 'grad_rwkv_mu': 'grad_w', 'grad_rwkv_w0_f': 'grad_w', 'grad_rwkv_w2_f': 'grad_w', 'grad_rwkv_w0_b': 'grad_w', 'grad_rwkv_w2_b': 'grad_w', 'grad_rwkv_a0_f': 'grad_w', 'grad_rwkv_a2_f': 'grad_w', 'grad_rwkv_a0_b': 'grad_w', 'grad_rwkv_a2_b': 'grad_w', 'grad_rwkv_k_k': 'grad_w', 'grad_rwkv_k_a': 'grad_w', 'grad_rwkv_r_k': 'grad_w', 'grad_rwkv_gn_g': 'grad_w', 'grad_rwkv_gn_b': 'grad_w', 'grad_w_br_mla': 'grad_w', 'grad_w_br_rwkv': 'grad_w', 'grad_w_out': 'grad_w', 'grad_g_post': 'grad_w', 'delta_g_pre': 'delta_w', 'delta_w_in': 'delta_w', 'delta_mla_q_norm': 'delta_w', 'delta_mla_wq_b': 'delta_w', 'delta_mla_kv_norm': 'delta_w', 'delta_mla_wkv_b': 'delta_w', 'delta_rwkv_mu': 'delta_w', 'delta_rwkv_w0_f': 'delta_w', 'delta_rwkv_w2_f': 'delta_w', 'delta_rwkv_w0_b': 'delta_w', 'delta_rwkv_w2_b': 'delta_w', 'delta_rwkv_a0_f': 'delta_w', 'delta_rwkv_a2_f': 'delta_w', 'delta_rwkv_a0_b': 'delta_w', 'delta_rwkv_a2_b': 'delta_w', 'delta_rwkv_k_k': 'delta_w', 'delta_rwkv_k_a': 'delta_w', 'delta_rwkv_r_k': 'delta_w', 'delta_rwkv_gn_g': 'delta_w', 'delta_rwkv_gn_b': 'delta_w', 'delta_w_br_mla': 'delta_w', 'delta_w_br_rwkv': 'delta_w', 'delta_w_out': 'delta_w', 'delta_g_post': 'delta_w', 'new_m_g_pre': 'new_m', 'new_m_w_in': 'new_m', 'new_m_mla_q_norm': 'new_m', 'new_m_mla_wq_b': 'new_m', 'new_m_mla_kv_norm': 'new_m', 'new_m_mla_wkv_b': 'new_m', 'new_m_rwkv_mu': 'new_m', 'new_m_rwkv_w0_f': 'new_m', 'new_m_rwkv_w2_f': 'new_m', 'new_m_rwkv_w0_b': 'new_m', 'new_m_rwkv_w2_b': 'new_m', 'new_m_rwkv_a0_f': 'new_m', 'new_m_rwkv_a2_f': 'new_m', 'new_m_rwkv_a0_b': 'new_m', 'new_m_rwkv_a2_b': 'new_m', 'new_m_rwkv_k_k': 'new_m', 'new_m_rwkv_k_a': 'new_m', 'new_m_rwkv_r_k': 'new_m', 'new_m_rwkv_gn_g': 'new_m', 'new_m_rwkv_gn_b': 'new_m', 'new_m_w_br_mla': 'new_m', 'new_m_w_br_rwkv': 'new_m', 'new_m_w_out': 'new_m', 'new_m_g_post': 'new_m', 'new_v_g_pre': 'new_v', 'new_v_w_in': 'new_v', 'new_v_mla_q_norm': 'new_v', 'new_v_mla_wq_b': 'new_v', 'new_v_mla_kv_norm': 'new_v', 'new_v_mla_wkv_b': 'new_v', 'new_v_rwkv_mu': 'new_v', 'new_v_rwkv_w0_f': 'new_v', 'new_v_rwkv_w2_f': 'new_v', 'new_v_rwkv_w0_b': 'new_v', 'new_v_rwkv_w2_b': 'new_v', 'new_v_rwkv_a0_f': 'new_v', 'new_v_rwkv_a2_f': 'new_v', 'new_v_rwkv_a0_b': 'new_v', 'new_v_rwkv_a2_b': 'new_v', 'new_v_rwkv_k_k': 'new_v', 'new_v_rwkv_k_a': 'new_v', 'new_v_rwkv_r_k': 'new_v', 'new_v_rwkv_gn_g': 'new_v', 'new_v_rwkv_gn_b': 'new_v', 'new_v_w_br_mla': 'new_v', 'new_v_w_br_rwkv': 'new_v', 'new_v_w_out': 'new_v', 'new_v_g_post': 'new_v'}


def _forward(args):
    return _fwd_reference(*[args[k] for k in FWD_PARAMS])


def _output_shape():
    out = _jax.eval_shape(lambda: _forward(_fwd_setup_inputs(0)))
    return out.shape, out.dtype

N_MICROBATCH = 1
ADAM_LR = 0.001
ADAM_B1 = 0.9
ADAM_B2 = 0.999
ADAM_EPS = 1e-08
ADAM_WD = 0.01
ADAM_STEP = 10
PER_EXAMPLE_BATCH_AXIS = {'x': 0, 'loss_target': 0}
SHARED_INPUTS = []
_WEIGHT_DTYPES = {'g_pre': _jnp.float32, 'w_in': _jnp.float32, 'mla_q_norm': _jnp.float32, 'mla_wq_b': _jnp.float32, 'mla_kv_norm': _jnp.float32, 'mla_wkv_b': _jnp.float32, 'rwkv_mu': _jnp.float32, 'rwkv_w0_f': _jnp.float32, 'rwkv_w2_f': _jnp.float32, 'rwkv_w0_b': _jnp.float32, 'rwkv_w2_b': _jnp.float32, 'rwkv_a0_f': _jnp.float32, 'rwkv_a2_f': _jnp.float32, 'rwkv_a0_b': _jnp.float32, 'rwkv_a2_b': _jnp.float32, 'rwkv_k_k': _jnp.float32, 'rwkv_k_a': _jnp.float32, 'rwkv_r_k': _jnp.float32, 'rwkv_gn_g': _jnp.float32, 'rwkv_gn_b': _jnp.float32, 'w_br_mla': _jnp.float32, 'w_br_rwkv': _jnp.float32, 'w_out': _jnp.float32, 'g_post': _jnp.float32}
MOMENT_SCALE = {'g_pre': 2.025715e-01, 'w_in': 8.932559e-02, 'mla_q_norm': 1.812998e-02, 'mla_wq_b': 1.060369e-02, 'mla_kv_norm': 2.591658e-02, 'mla_wkv_b': 1.218370e-02, 'rwkv_mu': 2.219715e-01, 'rwkv_w0_f': 3.106761e-02, 'rwkv_w2_f': 9.991259e-03, 'rwkv_w0_b': 3.150625e-02, 'rwkv_w2_b': 1.014070e-02, 'rwkv_a0_f': 4.204360e-02, 'rwkv_a2_f': 2.888292e-02, 'rwkv_a0_b': 4.397472e-02, 'rwkv_a2_b': 3.008179e-02, 'rwkv_k_k': 3.720267e-02, 'rwkv_k_a': 1.617139e-01, 'rwkv_r_k': 3.430447e-01, 'rwkv_gn_g': 1.192430e-01, 'rwkv_gn_b': 2.014888e-01, 'w_br_mla': 9.384544e-03, 'w_br_rwkv': 8.886044e-02, 'w_out': 8.893461e-02, 'g_post': 8.013070e+00}


def _to_microbatches(a, axis):
    t = _jnp.moveaxis(a, axis, 0)
    t = t.reshape((N_MICROBATCH, t.shape[0] // N_MICROBATCH) + t.shape[1:])
    return _jnp.moveaxis(t, 1, axis + 1)


def setup_inputs(seed: int = 0) -> dict:
    inp = _fwd_setup_inputs(seed)
    key = _jax.random.fold_in(_jax.random.key(seed), 7919)
    shape, _ = _output_shape()
    out = dict(inp)
    out["loss_target"] = _jax.random.normal(_jax.random.fold_in(key, 0), shape, _jnp.float32)
    for i, name in enumerate(TWIN_WEIGHTS):
        w = inp[name].astype(_jnp.float32)
        if MOMENT_SCALE is None:
            s = _jnp.sqrt(_jnp.mean(_jnp.square(w)) + 1e-30)
        else:
            s = MOMENT_SCALE[name]
        km, kv = _jax.random.split(_jax.random.fold_in(key, i + 1))
        out[name] = w
        out["m_" + name] = s * _jax.random.normal(km, w.shape, _jnp.float32)
        out["v_" + name] = (s * s) * _jax.random.uniform(kv, w.shape, _jnp.float32, 0.5, 1.5)
    if N_MICROBATCH > 1:
        for name, axis in PER_EXAMPLE_BATCH_AXIS.items():
            out[name] = _to_microbatches(out[name], axis)
    return {'x': out['x'], 'g_pre': out['g_pre'], 'w_in': out['w_in'], 'mla_q_norm': out['mla_q_norm'], 'mla_wq_b': out['mla_wq_b'], 'mla_kv_norm': out['mla_kv_norm'], 'mla_wkv_b': out['mla_wkv_b'], 'rwkv_mu': out['rwkv_mu'], 'rwkv_w0_f': out['rwkv_w0_f'], 'rwkv_w2_f': out['rwkv_w2_f'], 'rwkv_w0_b': out['rwkv_w0_b'], 'rwkv_w2_b': out['rwkv_w2_b'], 'rwkv_a0_f': out['rwkv_a0_f'], 'rwkv_a2_f': out['rwkv_a2_f'], 'rwkv_a0_b': out['rwkv_a0_b'], 'rwkv_a2_b': out['rwkv_a2_b'], 'rwkv_k_k': out['rwkv_k_k'], 'rwkv_k_a': out['rwkv_k_a'], 'rwkv_r_k': out['rwkv_r_k'], 'rwkv_gn_g': out['rwkv_gn_g'], 'rwkv_gn_b': out['rwkv_gn_b'], 'w_br_mla': out['w_br_mla'], 'w_br_rwkv': out['w_br_rwkv'], 'w_out': out['w_out'], 'g_post': out['g_post'], 'loss_target': out['loss_target'], 'm_g_pre': out['m_g_pre'], 'm_w_in': out['m_w_in'], 'm_mla_q_norm': out['m_mla_q_norm'], 'm_mla_wq_b': out['m_mla_wq_b'], 'm_mla_kv_norm': out['m_mla_kv_norm'], 'm_mla_wkv_b': out['m_mla_wkv_b'], 'm_rwkv_mu': out['m_rwkv_mu'], 'm_rwkv_w0_f': out['m_rwkv_w0_f'], 'm_rwkv_w2_f': out['m_rwkv_w2_f'], 'm_rwkv_w0_b': out['m_rwkv_w0_b'], 'm_rwkv_w2_b': out['m_rwkv_w2_b'], 'm_rwkv_a0_f': out['m_rwkv_a0_f'], 'm_rwkv_a2_f': out['m_rwkv_a2_f'], 'm_rwkv_a0_b': out['m_rwkv_a0_b'], 'm_rwkv_a2_b': out['m_rwkv_a2_b'], 'm_rwkv_k_k': out['m_rwkv_k_k'], 'm_rwkv_k_a': out['m_rwkv_k_a'], 'm_rwkv_r_k': out['m_rwkv_r_k'], 'm_rwkv_gn_g': out['m_rwkv_gn_g'], 'm_rwkv_gn_b': out['m_rwkv_gn_b'], 'm_w_br_mla': out['m_w_br_mla'], 'm_w_br_rwkv': out['m_w_br_rwkv'], 'm_w_out': out['m_w_out'], 'm_g_post': out['m_g_post'], 'v_g_pre': out['v_g_pre'], 'v_w_in': out['v_w_in'], 'v_mla_q_norm': out['v_mla_q_norm'], 'v_mla_wq_b': out['v_mla_wq_b'], 'v_mla_kv_norm': out['v_mla_kv_norm'], 'v_mla_wkv_b': out['v_mla_wkv_b'], 'v_rwkv_mu': out['v_rwkv_mu'], 'v_rwkv_w0_f': out['v_rwkv_w0_f'], 'v_rwkv_w2_f': out['v_rwkv_w2_f'], 'v_rwkv_w0_b': out['v_rwkv_w0_b'], 'v_rwkv_w2_b': out['v_rwkv_w2_b'], 'v_rwkv_a0_f': out['v_rwkv_a0_f'], 'v_rwkv_a2_f': out['v_rwkv_a2_f'], 'v_rwkv_a0_b': out['v_rwkv_a0_b'], 'v_rwkv_a2_b': out['v_rwkv_a2_b'], 'v_rwkv_k_k': out['v_rwkv_k_k'], 'v_rwkv_k_a': out['v_rwkv_k_a'], 'v_rwkv_r_k': out['v_rwkv_r_k'], 'v_rwkv_gn_g': out['v_rwkv_gn_g'], 'v_rwkv_gn_b': out['v_rwkv_gn_b'], 'v_w_br_mla': out['v_w_br_mla'], 'v_w_br_rwkv': out['v_w_br_rwkv'], 'v_w_out': out['v_w_out'], 'v_g_post': out['v_g_post']}


def _loss(weights, diff, rest, loss_target):
    with _jax.named_scope("forward"):
        args = {**rest, TWIN_DIFF_INPUT: diff, **{k: w.astype(_WEIGHT_DTYPES[k]) for k, w in weights.items()}}
        y = _forward(args)
    with _jax.named_scope("loss_head"):
        err = _jnp.square(y.astype(_jnp.float32) - loss_target)
        return 0.5 * _jnp.sum(_jnp.mean(err, axis=-1)) if err.ndim else 0.5 * err


def _adamw(w, g, m, v):
    m = ADAM_B1 * m + (1.0 - ADAM_B1) * g
    v = ADAM_B2 * v + (1.0 - ADAM_B2) * _jnp.square(g)
    m_hat = m / (1.0 - ADAM_B1 ** ADAM_STEP)
    v_hat = v / (1.0 - ADAM_B2 ** ADAM_STEP)
    delta = -ADAM_LR * (m_hat / (_jnp.sqrt(v_hat) + ADAM_EPS) + ADAM_WD * w)
    return delta, m, v


def reference(x, g_pre, w_in, mla_q_norm, mla_wq_b, mla_kv_norm, mla_wkv_b, rwkv_mu, rwkv_w0_f, rwkv_w2_f, rwkv_w0_b, rwkv_w2_b, rwkv_a0_f, rwkv_a2_f, rwkv_a0_b, rwkv_a2_b, rwkv_k_k, rwkv_k_a, rwkv_r_k, rwkv_gn_g, rwkv_gn_b, w_br_mla, w_br_rwkv, w_out, g_post, loss_target, m_g_pre, m_w_in, m_mla_q_norm, m_mla_wq_b, m_mla_kv_norm, m_mla_wkv_b, m_rwkv_mu, m_rwkv_w0_f, m_rwkv_w2_f, m_rwkv_w0_b, m_rwkv_w2_b, m_rwkv_a0_f, m_rwkv_a2_f, m_rwkv_a0_b, m_rwkv_a2_b, m_rwkv_k_k, m_rwkv_k_a, m_rwkv_r_k, m_rwkv_gn_g, m_rwkv_gn_b, m_w_br_mla, m_w_br_rwkv, m_w_out, m_g_post, v_g_pre, v_w_in, v_mla_q_norm, v_mla_wq_b, v_mla_kv_norm, v_mla_wkv_b, v_rwkv_mu, v_rwkv_w0_f, v_rwkv_w2_f, v_rwkv_w0_b, v_rwkv_w2_b, v_rwkv_a0_f, v_rwkv_a2_f, v_rwkv_a0_b, v_rwkv_a2_b, v_rwkv_k_k, v_rwkv_k_a, v_rwkv_r_k, v_rwkv_gn_g, v_rwkv_gn_b, v_w_br_mla, v_w_br_rwkv, v_w_out, v_g_post):
    given = dict(x=x, g_pre=g_pre, w_in=w_in, mla_q_norm=mla_q_norm, mla_wq_b=mla_wq_b, mla_kv_norm=mla_kv_norm, mla_wkv_b=mla_wkv_b, rwkv_mu=rwkv_mu, rwkv_w0_f=rwkv_w0_f, rwkv_w2_f=rwkv_w2_f, rwkv_w0_b=rwkv_w0_b, rwkv_w2_b=rwkv_w2_b, rwkv_a0_f=rwkv_a0_f, rwkv_a2_f=rwkv_a2_f, rwkv_a0_b=rwkv_a0_b, rwkv_a2_b=rwkv_a2_b, rwkv_k_k=rwkv_k_k, rwkv_k_a=rwkv_k_a, rwkv_r_k=rwkv_r_k, rwkv_gn_g=rwkv_gn_g, rwkv_gn_b=rwkv_gn_b, w_br_mla=w_br_mla, w_br_rwkv=w_br_rwkv, w_out=w_out, g_post=g_post, loss_target=loss_target, m_g_pre=m_g_pre, m_w_in=m_w_in, m_mla_q_norm=m_mla_q_norm, m_mla_wq_b=m_mla_wq_b, m_mla_kv_norm=m_mla_kv_norm, m_mla_wkv_b=m_mla_wkv_b, m_rwkv_mu=m_rwkv_mu, m_rwkv_w0_f=m_rwkv_w0_f, m_rwkv_w2_f=m_rwkv_w2_f, m_rwkv_w0_b=m_rwkv_w0_b, m_rwkv_w2_b=m_rwkv_w2_b, m_rwkv_a0_f=m_rwkv_a0_f, m_rwkv_a2_f=m_rwkv_a2_f, m_rwkv_a0_b=m_rwkv_a0_b, m_rwkv_a2_b=m_rwkv_a2_b, m_rwkv_k_k=m_rwkv_k_k, m_rwkv_k_a=m_rwkv_k_a, m_rwkv_r_k=m_rwkv_r_k, m_rwkv_gn_g=m_rwkv_gn_g, m_rwkv_gn_b=m_rwkv_gn_b, m_w_br_mla=m_w_br_mla, m_w_br_rwkv=m_w_br_rwkv, m_w_out=m_w_out, m_g_post=m_g_post, v_g_pre=v_g_pre, v_w_in=v_w_in, v_mla_q_norm=v_mla_q_norm, v_mla_wq_b=v_mla_wq_b, v_mla_kv_norm=v_mla_kv_norm, v_mla_wkv_b=v_mla_wkv_b, v_rwkv_mu=v_rwkv_mu, v_rwkv_w0_f=v_rwkv_w0_f, v_rwkv_w2_f=v_rwkv_w2_f, v_rwkv_w0_b=v_rwkv_w0_b, v_rwkv_w2_b=v_rwkv_w2_b, v_rwkv_a0_f=v_rwkv_a0_f, v_rwkv_a2_f=v_rwkv_a2_f, v_rwkv_a0_b=v_rwkv_a0_b, v_rwkv_a2_b=v_rwkv_a2_b, v_rwkv_k_k=v_rwkv_k_k, v_rwkv_k_a=v_rwkv_k_a, v_rwkv_r_k=v_rwkv_r_k, v_rwkv_gn_g=v_rwkv_gn_g, v_rwkv_gn_b=v_rwkv_gn_b, v_w_br_mla=v_w_br_mla, v_w_br_rwkv=v_w_br_rwkv, v_w_out=v_w_out, v_g_post=v_g_post)
    weights = {n: given[n] for n in TWIN_WEIGHTS}
    shared = {n: given[n] for n in SHARED_INPUTS}
    per_example = {n: given[n] for n in ['x']}
    grad_fn = _jax.value_and_grad(_loss, argnums=(0, 1))

    def one_microbatch(ex, loss_target):
        ex = dict(ex)
        diff = ex.pop(TWIN_DIFF_INPUT)
        return grad_fn(weights, diff, {**shared, **ex}, loss_target)

    if N_MICROBATCH == 1:
        loss, (grad_w, grad_x) = one_microbatch(per_example, given["loss_target"])
    else:
        def body(carry, xs):
            loss_sum, grad_sum = carry
            l_k, (gw_k, gx_k) = one_microbatch(xs[0], xs[1])
            with _jax.named_scope("update"):
                return (loss_sum + l_k, _jax.tree.map(_jnp.add, grad_sum, gw_k)), gx_k

        init = (_jnp.zeros((), _jnp.float32), _jax.tree.map(_jnp.zeros_like, weights))
        (loss, grad_w), grad_x = _jax.lax.scan(body, init, (per_example, given["loss_target"]))
    with _jax.named_scope("update"):
        delta_w, new_m, new_v = {}, {}, {}
        for n in TWIN_WEIGHTS:
            delta_w[n], new_m[n], new_v[n] = _adamw(weights[n], grad_w[n], given["m_" + n], given["v_" + n])
    return (loss, grad_x, *[grad_w[n] for n in TWIN_WEIGHTS], *[delta_w[n] for n in TWIN_WEIGHTS],
            *[new_m[n] for n in TWIN_WEIGHTS], *[new_v[n] for n in TWIN_WEIGHTS])
```

```python
import functools
import math

import jax
import jax.numpy as jnp
from jax import lax
from jax.experimental import pallas as pl
from jax.experimental.pallas import tpu as pltpu

F32 = jnp.float32
BF16 = jnp.bfloat16

N_DEV = 8
LANES = 128
BF16_ROWS = 16
NOPE, ROPE, VDIM = 128, 64, 128
QHEAD = 256
ROPE_THETA = 10000.0
NORM_EPS = 1e-6
GN_EPS = 64e-5
CHUNK = 64
SUB = 16
VMEM_LIMIT = 56 * 1024 * 1024

ADAM_LR, ADAM_B1, ADAM_B2, ADAM_EPS, ADAM_WD, ADAM_STEP = 0.001, 0.9, 0.999, 1e-08, 0.01, 10


def _cparams(sem):
    return pltpu.CompilerParams(dimension_semantics=sem, vmem_limit_bytes=VMEM_LIMIT)


def _pick(n, cap):
    if n <= cap:
        return n
    for t in range(cap - cap % LANES, 0, -LANES):
        if n % t == 0:
            return t
    raise ValueError(f"no tile for {n} under {cap}")


def _mm(a, b, *, ta=False, tb=False, out_dtype=F32, name, tm_cap=1024, tn_cap=512, tk_cap=2048):
    K, M = a.shape if ta else a.shape[::-1]
    N = b.shape[0] if tb else b.shape[1]
    assert (b.shape[1] if tb else b.shape[0]) == K, (a.shape, b.shape, ta, tb)
    tm, tn, tk = _pick(M, tm_cap), _pick(N, tn_cap), _pick(K, tk_cap)
    nk = K // tk
    dn = (((0 if ta else 1,), (1 if tb else 0,)), ((), ()))

    def body(a_ref, b_ref, o_ref, acc_ref):
        k = pl.program_id(2)
        p = lax.dot_general(a_ref[...], b_ref[...], dn, preferred_element_type=F32)

        @pl.when(k == 0)
        def _():
            acc_ref[...] = p

        @pl.when(k > 0)
        def _():
            acc_ref[...] += p

        @pl.when(k == nk - 1)
        def _():
            o_ref[...] = acc_ref[...].astype(out_dtype)

    a_spec = pl.BlockSpec((tk, tm), lambda i, j, k: (k, i)) if ta else pl.BlockSpec((tm, tk), lambda i, j, k: (i, k))
    b_spec = pl.BlockSpec((tn, tk), lambda i, j, k: (j, k)) if tb else pl.BlockSpec((tk, tn), lambda i, j, k: (k, j))
    return pl.pallas_call(
        body, name=name, grid=(M // tm, N // tn, nk),
        in_specs=[a_spec, b_spec], out_specs=pl.BlockSpec((tm, tn), lambda i, j, k: (i, j)),
        out_shape=jax.ShapeDtypeStruct((M, N), out_dtype),
        scratch_shapes=[pltpu.VMEM((tm, tn), F32)],
        compiler_params=_cparams(("parallel", "parallel", "arbitrary")),
    )(a, b)


def _view(arr, off, width):
    assert off % width == 0, (off, width)
    return (arr, off // width, width)


def _rowwise(fn, rows, params, out_rows, out_accs=(), *, tile, name):
    rows = [r if isinstance(r, tuple) else (r, 0, r.shape[1]) for r in rows]
    S = rows[0][0].shape[0]
    T = min(tile, S)
    assert S % T == 0
    n_rows, n_par, n_out = len(rows), len(params), len(out_rows)

    def body(*refs):
        ins = [r[...] for r in refs[:n_rows + n_par]]
        outs = fn(*ins)
        out_refs = refs[n_rows + n_par:]
        for o_ref, val in zip(out_refs[:n_out], outs[:n_out]):
            o_ref[...] = val.astype(o_ref.dtype)
        i = pl.program_id(0)
        for o_ref, val in zip(out_refs[n_out:], outs[n_out:]):
            @pl.when(i == 0)
            def _(o_ref=o_ref, val=val):
                o_ref[...] = val

            @pl.when(i > 0)
            def _(o_ref=o_ref, val=val):
                o_ref[...] += val

    in_specs = [pl.BlockSpec((T, w), functools.partial(lambda i, cb: (i, cb), cb=cb)) for _, cb, w in rows]
    in_specs += [pl.BlockSpec(p.shape, lambda i: (0, 0)) for p in params]
    out_specs = [pl.BlockSpec((T, w), lambda i: (i, 0)) for w, _ in out_rows]
    out_specs += [pl.BlockSpec(s, lambda i: (0, 0)) for s in out_accs]
    out_shape = [jax.ShapeDtypeStruct((S, w), dt) for w, dt in out_rows]
    out_shape += [jax.ShapeDtypeStruct(s, F32) for s in out_accs]
    return pl.pallas_call(
        body, name=name, grid=(S // T,), in_specs=in_specs, out_specs=out_specs, out_shape=out_shape,
        compiler_params=_cparams(("arbitrary",)),
    )(*[r[0] for r in rows], *params)


def _split3(x):
    hi = x.astype(BF16)
    r1 = x - hi.astype(F32)
    mid = r1.astype(BF16)
    lo = (r1 - mid.astype(F32)).astype(BF16)
    return hi, mid, lo


def _mm_sel(x, sel):
    hi, mid, lo = _split3(x)
    d = lambda u: jnp.dot(u, sel, preferred_element_type=F32)
    return d(hi) + d(mid) + d(lo)


@jax.custom_vjp
def _sel(x, sel, sel_t):
    return _mm_sel(x, sel)


def _sel_fwd(x, sel, sel_t):
    return _mm_sel(x, sel), (sel, sel_t)


def _sel_bwd(res, ct):
    sel, sel_t = res
    return _mm_sel(ct, sel_t), jnp.zeros_like(sel), jnp.zeros_like(sel_t)


_sel.defvjp(_sel_fwd, _sel_bwd)


def _rms(x, g):
    return x * lax.rsqrt(jnp.mean(x * x, axis=-1, keepdims=True) + NORM_EPS) * g


def _sigmoid(x):
    return 1.0 / (1.0 + jnp.exp(-x))


def _silu(x):
    return x * _sigmoid(x)


def _softplus(x):
    return jnp.maximum(x, 0.0) + jnp.log(1.0 + jnp.exp(-jnp.abs(x)))


def _bdot(x, w):
    return jnp.dot(x.astype(BF16), w.astype(BF16), preferred_element_type=F32)


def _f_mla_norm(q_a, kv_a, qg, kvg):
    return _rms(q_a, qg), _rms(kv_a, kvg)


def _f_rope(hm, qraw, kr_in, cosx, sinx, rot, rot_t):
    def rope(t):
        return t * cosx + _sel(t, rot, rot_t) * sinx
    parts = []
    for h in range(hm):
        parts.append(qraw[:, h * QHEAD:h * QHEAD + NOPE])
        parts.append(rope(qraw[:, h * QHEAD + NOPE:(h + 1) * QHEAD]))
    return jnp.concatenate(parts, axis=1), rope(kr_in)


def _attn_block(qn, qr, kn, kr, v, scale):
    nt = (((1,), (1,)), ((), ()))
    s = lax.dot_general(qn.astype(BF16), kn.astype(BF16), nt, preferred_element_type=F32)
    s = s + lax.dot_general(qr.astype(BF16), kr.astype(BF16), nt, preferred_element_type=F32)
    s = s * scale
    p = jnp.exp(s - jnp.max(s, axis=-1, keepdims=True))
    p = p / jnp.sum(p, axis=-1, keepdims=True)
    return jnp.dot(p.astype(BF16), v.astype(BF16), preferred_element_type=F32)


def _f_rwkv_pre(rw, k, tail, w0f, w0b, a0f, a0b, k_k, k_a, w2cat, a2cat, seg, seg_t):
    zw = _bdot(jnp.tanh(tail), w2cat)
    za = _bdot(tail, a2cat)
    lw_f = -jnp.exp(-_softplus(-(w0f + zw[:, :rw])) - 0.5)
    lw_b = -jnp.exp(-_softplus(-(w0b + zw[:, rw:])) - 0.5)
    a_f = _sigmoid(a0f + za[:, :rw])
    a_b = _sigmoid(a0b + za[:, rw:])
    kk = k * k_k
    nrm = jnp.sqrt(_sel(_sel(kk * kk, seg, seg_t), seg_t, seg))
    kk = kk / jnp.maximum(nrm, 1e-12)
    k_f = k * (1.0 + (a_f - 1.0) * k_a)
    k_b = k * (1.0 + (a_b - 1.0) * k_a)
    return lw_f, lw_b, k_f, k_b, -kk, kk * a_f, kk * a_b


def _f_post(hn, y_f, y_b, r, k_f, k_b, v, z_r, o_mla, z_m, gn_g, gn_b, r_k, seg, seg_t):
    segsum = lambda t: _sel(_sel(t, seg, seg_t), seg_t, seg)
    y = y_f + y_b
    mu = segsum(y) * (1.0 / hn)
    yc = y - mu
    var = segsum(yc * yc) * (1.0 / hn)
    yn = yc * lax.rsqrt(var + GN_EPS) * gn_g + gn_b
    bonus = segsum(r * (k_f + k_b) * r_k) * v
    return o_mla * _silu(z_m), (yn + bonus) * _silu(z_r)


def _f_merge(u_m, u_r, g_m, g_r):
    return _sigmoid(g_m) * u_m + _sigmoid(g_r) * u_r


def _hdot(x, y, dims):
    return lax.dot_general(x, y, (dims, ((0,), (0,))), precision=lax.Precision.HIGHEST,
                           preferred_element_type=F32)


_NN = ((2,), (1,))
_NT = ((2,), (2,))
_TN = ((1,), (1,))


def _tri_solve(n_mat, x, length):
    row = lax.broadcasted_iota(jnp.int32, (length, length), 0)
    col = lax.broadcasted_iota(jnp.int32, (length, length), 1)
    eye = (row == col).astype(F32)[None]
    diag_blk = ((row // SUB) == (col // SUB))[None]
    nd = jnp.where(diag_blk, n_mat, 0.0)
    no = n_mat - nd
    dinv = eye + nd
    p = nd
    for _ in range(int(math.log2(SUB)) - 1):
        p = _hdot(p, p, _NN)
        dinv = dinv + _hdot(dinv, p, _NN)
    q = _hdot(dinv, no, _NN)
    u = _hdot(dinv, x, _NN)
    levels = int(math.log2(length // SUB))
    qs = [q]
    for _ in range(levels - 1):
        qs.append(_hdot(qs[-1], qs[-1], _NN))
    for qk in reversed(qs):
        u = u + _hdot(qk, u, _NN)
    return u


def _rwkv_chunk(rev, s0, r, lw, k, v, a, b):
    h, length, _ = r.shape
    row = lax.broadcasted_iota(jnp.int32, (length, length), 0)
    col = lax.broadcasted_iota(jnp.int32, (length, length), 1)
    incl = (row <= col) if rev else (row >= col)
    strict = (row < col) if rev else (row > col)
    t_incl = jnp.broadcast_to(incl.astype(F32)[None], (h, length, length))
    cum = _hdot(t_incl, lw, _NN)
    g = jnp.exp(cum)
    g_inv = jnp.exp(-cum)
    at = a * jnp.exp(cum - lw)
    rt = r * g
    bt = b * g_inv
    kt = k * g_inv
    a_ab = jnp.where(strict[None], _hdot(at, bt, _NT), 0.0)
    a_ak = jnp.where(strict[None], _hdot(at, kt, _NT), 0.0)
    a_rb = jnp.where(incl[None], _hdot(rt, bt, _NT), 0.0)
    a_rk = jnp.where(incl[None], _hdot(rt, kt, _NT), 0.0)
    x = _hdot(at, s0, _NT) + _hdot(a_ak, v, _NN)
    u = _tri_solve(a_ab, x, length)
    y = _hdot(rt, s0, _NT) + _hdot(a_rb, u, _NN) + _hdot(a_rk, v, _NN)
    g_last = g[:, 0:1, :] if rev else g[:, length - 1:length, :]
    s1 = (s0 + _hdot(u, bt, _TN) + _hdot(v, kt, _TN)) * g_last
    return y, s1


def _scan_specs(hb, n, nc, rev):
    cidx = (lambda c: nc - 1 - c) if rev else (lambda c: c)
    seq = pl.BlockSpec((hb, CHUNK, n), lambda g, c: (g, cidx(c), 0))
    st = pl.BlockSpec((1, hb, n, n), lambda g, c: (cidx(c), g, 0, 0))
    return seq, st


def _rwkv_scan_fwd(rev, r, lw, k, v, a, b, *, hb, name):
    H, S, n = r.shape
    nc = S // CHUNK
    seq, st = _scan_specs(hb, n, nc, rev)

    def body(r_ref, lw_ref, k_ref, v_ref, a_ref, b_ref, y_ref, st_ref, s_ref):
        @pl.when(pl.program_id(1) == 0)
        def _():
            s_ref[...] = jnp.zeros_like(s_ref)

        s0 = s_ref[...]
        st_ref[0] = s0
        y, s1 = _rwkv_chunk(rev, s0, r_ref[...], lw_ref[...], k_ref[...], v_ref[...], a_ref[...], b_ref[...])
        y_ref[...] = y
        s_ref[...] = s1

    return pl.pallas_call(
        body, name=name, grid=(H // hb, nc), in_specs=[seq] * 6, out_specs=[seq, st],
        out_shape=[jax.ShapeDtypeStruct((H, S, n), F32), jax.ShapeDtypeStruct((nc, H, n, n), F32)],
        scratch_shapes=[pltpu.VMEM((hb, n, n), F32)],
        compiler_params=_cparams(("parallel", "arbitrary")),
    )(r, lw, k, v, a, b)


def _rwkv_scan_bwd(rev, r, lw, k, v, a, b, states, dy, *, hb, name):
    H, S, n = r.shape
    nc = S // CHUNK
    seq, st = _scan_specs(hb, n, nc, not rev)

    def body(r_ref, lw_ref, k_ref, v_ref, a_ref, b_ref, st_ref, dy_ref, *rest):
        out_refs, ds_ref = rest[:6], rest[6]

        @pl.when(pl.program_id(1) == 0)
        def _():
            ds_ref[...] = jnp.zeros_like(ds_ref)

        _, vjp = jax.vjp(functools.partial(_rwkv_chunk, rev), st_ref[0], r_ref[...], lw_ref[...], k_ref[...],
                         v_ref[...], a_ref[...], b_ref[...])
        grads = vjp((dy_ref[...], ds_ref[...]))
        ds_ref[...] = grads[0]
        for o_ref, gval in zip(out_refs, grads[1:]):
            o_ref[...] = gval

    return pl.pallas_call(
        body, name=name, grid=(H // hb, nc), in_specs=[seq] * 6 + [st, seq], out_specs=[seq] * 6,
        out_shape=[jax.ShapeDtypeStruct((H, S, n), F32)] * 6,
        scratch_shapes=[pltpu.VMEM((hb, n, n), F32)],
        compiler_params=_cparams(("parallel", "arbitrary")),
    )(r, lw, k, v, a, b, states, dy)


def _shift_lerp(x_view, mu, d=None, *, name):
    arr, off, width = x_view
    S = arr.shape[0]
    cb = _pick(width, 256)
    assert off % cb == 0

    def cshift(t):
        rows = lax.broadcasted_iota(jnp.int32, t.shape, 0)
        prev = jnp.where(rows == 0, 0.0, pltpu.roll(t, 1, 0))
        nxt = jnp.where(rows == S - 1, 0.0, pltpu.roll(t, S - 1, 0))
        return 0.5 * (prev + nxt)

    def fwd_body(x_ref, mu_ref, o_ref):
        x = x_ref[...]
        o_ref[...] = x + mu_ref[...] * (cshift(x) - x)

    def bwd_body(x_ref, mu_ref, d_ref, dx_ref, dmu_ref):
        x, m, dd = x_ref[...], mu_ref[...], d_ref[...]
        gm = m * dd
        dx_ref[...] = dd - gm + cshift(gm)
        dmu_ref[...] = jnp.sum(dd * (cshift(x) - x), axis=0, keepdims=True)

    x_spec = pl.BlockSpec((S, cb), lambda j: (0, off // cb + j))
    blk = pl.BlockSpec((S, cb), lambda j: (0, j))
    vec = pl.BlockSpec((1, cb), lambda j: (0, j))
    if d is None:
        return pl.pallas_call(
            fwd_body, name=name, grid=(width // cb,), in_specs=[x_spec, vec], out_specs=blk,
            out_shape=jax.ShapeDtypeStruct((S, width), F32), compiler_params=_cparams(("parallel",)),
        )(arr, mu)
    return pl.pallas_call(
        bwd_body, name=name, grid=(width // cb,), in_specs=[x_spec, vec, blk], out_specs=[blk, vec],
        out_shape=[jax.ShapeDtypeStruct((S, width), F32), jax.ShapeDtypeStruct((1, width), F32)],
        compiler_params=_cparams(("parallel",)),
    )(arr, mu, d)


def _attention_fwd(qfull, kv, kr, hm, scale, *, tq, name):
    S = qfull.shape[0]

    def body(qn_ref, qr_ref, kn_ref, kr_ref, v_ref, o_ref):
        f = lambda ref: ref[...].astype(F32)
        o_ref[...] = _attn_block(f(qn_ref), f(qr_ref), f(kn_ref), f(kr_ref), f(v_ref), scale)

    return pl.pallas_call(
        body, name=name, grid=(hm, S // tq),
        in_specs=[pl.BlockSpec((tq, NOPE), lambda h, i: (i, 2 * h)),
                  pl.BlockSpec((tq, NOPE), lambda h, i: (i, 2 * h + 1)),
                  pl.BlockSpec((S, NOPE), lambda h, i: (0, h)),
                  pl.BlockSpec((S, LANES), lambda h, i: (0, 0)),
                  pl.BlockSpec((S, VDIM), lambda h, i: (0, hm + h))],
        out_specs=pl.BlockSpec((tq, VDIM), lambda h, i: (i, h)),
        out_shape=jax.ShapeDtypeStruct((S, hm * VDIM), F32),
        compiler_params=_cparams(("parallel", "parallel")),
    )(qfull, qfull, kv, kr, kv)


def _attention_bwd(qfull, kv, kr, d_o, hm, scale, *, tq, name):
    S = qfull.shape[0]

    def body(qn_ref, qr_ref, kn_ref, kr_ref, v_ref, do_ref, dqn_ref, dqr_ref, dkn_ref, dv_ref, dkr_ref):
        f = lambda ref: ref[...].astype(F32)
        _, vjp = jax.vjp(functools.partial(_attn_block, scale=scale), f(qn_ref), f(qr_ref), f(kn_ref), f(kr_ref),
                         f(v_ref))
        dqn, dqr, dkn, dkr, dv = vjp(do_ref[...])
        dqn_ref[...] = dqn
        dqr_ref[...] = dqr
        first = pl.program_id(1) == 0
        for ref, val in ((dkn_ref, dkn), (dv_ref, dv), (dkr_ref, dkr)):
            @pl.when(first)
            def _(ref=ref, val=val):
                ref[...] = val

            @pl.when(jnp.logical_not(first))
            def _(ref=ref, val=val):
                ref[...] += val

    qblk = pl.BlockSpec((tq, NOPE), lambda h, i: (i, h))
    kblk = pl.BlockSpec((S, NOPE), lambda h, i: (0, h))
    shp = jax.ShapeDtypeStruct((S, hm * NOPE), F32)
    return pl.pallas_call(
        body, name=name, grid=(hm, S // tq),
        in_specs=[pl.BlockSpec((tq, NOPE), lambda h, i: (i, 2 * h)),
                  pl.BlockSpec((tq, NOPE), lambda h, i: (i, 2 * h + 1)),
                  kblk,
                  pl.BlockSpec((S, LANES), lambda h, i: (0, 0)),
                  pl.BlockSpec((S, VDIM), lambda h, i: (0, hm + h)),
                  qblk],
        out_specs=[qblk, qblk, kblk, kblk, kblk],
        out_shape=[shp] * 5,
        compiler_params=_cparams(("parallel", "arbitrary")),
    )(qfull, qfull, kv, kr, kv, d_o)


def _layout(D, MW, RW, TAIL, QR, KVR):
    names = ["gate_m", "gate_r", "z_m", "z_r", "r", "k", "v", "tail", "q_a", "kv_a"]
    widths = [D, D, MW, RW, RW, RW, RW, TAIL, QR, KVR]
    offs, o = {}, 0
    for nme, w in zip(names, widths):
        assert o % w == 0, (nme, o, w)
        offs[nme] = (o, w)
        o += w
    return offs, o


def _local_grads(x, target, W, dims):
    S, D = x.shape
    hm, hr, hn, rank = dims["hm"], dims["hr"], dims["hn"], dims["rank"]
    MW, RW = hm * VDIM, hr * hn
    TAIL = W["w2cat"].shape[0]
    QR, KVR = W["mla_q_norm"].shape[1], W["mla_kv_norm"].shape[1]
    lay, d_in = _layout(D, MW, RW, TAIL, QR, KVR)
    T = 256
    scale = (NOPE + ROPE) ** -0.5
    col = lambda arr, nme: _view(arr, *lay[nme])

    pos = jnp.arange(S, dtype=F32)
    inv_freq = jnp.power(ROPE_THETA, -jnp.arange(0, ROPE, 2, dtype=F32) / ROPE)
    ang = pos[:, None] * inv_freq[None, :]
    zpad = jnp.zeros((S, LANES - ROPE), F32)
    cosx = jnp.concatenate([jnp.cos(ang), jnp.cos(ang), zpad], axis=1)
    sinx = jnp.concatenate([jnp.sin(ang), jnp.sin(ang), zpad], axis=1)
    ii = jnp.arange(LANES)
    rot = (jnp.zeros((LANES, LANES), F32).at[ii[:ROPE // 2] + ROPE // 2, ii[:ROPE // 2]].set(-1.0)
           .at[ii[:ROPE // 2], ii[:ROPE // 2] + ROPE // 2].set(1.0)).astype(BF16)
    rot_t = rot.T
    seg = (jnp.arange(RW)[:, None] // hn == jnp.arange(LANES)[None, :]).astype(BF16)
    seg_t = seg.T

    (h,) = _rowwise(lambda xb, g: (_rms(xb, g),), [x], [W["g_pre"]], [(D, BF16)], tile=T, name="pre_norm")
    proj = _mm(h, W["w_in"], name="in_proj")

    qn, kvn = _rowwise(_f_mla_norm, [col(proj, "q_a"), col(proj, "kv_a")], [W["mla_q_norm"], W["mla_kv_norm"]],
                       [(QR, BF16), (KVR, BF16)], tile=T, name="mla_norm")
    qraw = _mm(qn, W["wq_b"], name="q_up")
    kv = _mm(kvn, W["wkv_b"], out_dtype=BF16, name="kv_up")
    kr_view = _view(proj, lay["tail"][0], LANES)
    qfull, kr = _rowwise(functools.partial(_f_rope, hm), [qraw, kr_view, cosx, sinx], [rot, rot_t],
                         [(hm * QHEAD, BF16), (LANES, BF16)], tile=T, name="rope")
    o_mla = _attention_fwd(qfull, kv, kr, hm, scale, tq=T, name="attn_fwd")

    shift_view = (proj, lay["r"][0], 3 * RW + TAIL)
    rl = _shift_lerp(shift_view, W["mu"], name="shift_fwd")
    rl_r, rl_k, rl_v = _view(rl, 0, RW), _view(rl, RW, RW), _view(rl, 2 * RW, RW)
    rl_tail = _view(rl, 3 * RW, TAIL)
    pre_params = [W["w0_f"], W["w0_b"], W["a0_f"], W["a0_b"], W["k_k"], W["k_a"], W["w2cat"], W["a2cat"], seg, seg_t]
    pre_fn = functools.partial(_f_rwkv_pre, RW)
    lw_f, lw_b, k_f, k_b, a_n, b_f, b_b = _rowwise(pre_fn, [rl_k, rl_tail], pre_params, [(RW, F32)] * 7, tile=T,
                                                    name="rwkv_pre")
    to_h = lambda t: t.reshape(S, hr, hn).transpose(1, 0, 2)
    from_h = lambda t: t.transpose(1, 0, 2).reshape(S, RW)
    r_h, v_h, a_h = to_h(rl[:, :RW]), to_h(rl[:, 2 * RW:3 * RW]), to_h(a_n)
    dirs = {}
    for tag, rev, lw, kd, bd in (("f", False, lw_f, k_f, b_f), ("b", True, lw_b, k_b, b_b)):
        ops = (r_h, to_h(lw), to_h(kd), v_h, a_h, to_h(bd))
        y_h, st = _rwkv_scan_fwd(rev, *ops, hb=dims["hb"], name="scan_fwd_" + tag)
        dirs[tag] = (rev, ops, st, from_h(y_h))
    y_f, y_b = dirs["f"][3], dirs["b"][3]

    post_fn = functools.partial(_f_post, hn)
    post_rows = [y_f, y_b, rl_r, k_f, k_b, rl_v, col(proj, "z_r"), o_mla, col(proj, "z_m")]
    post_params = [W["gn_g"], W["gn_b"], W["r_k"], seg, seg_t]
    ymg, yrg = _rowwise(post_fn, post_rows, post_params, [(MW, BF16), (RW, BF16)], tile=T, name="post")
    u_m = _mm(ymg, W["w_br_mla"], name="br_mla")
    u_r = _mm(yrg, W["w_br_rwkv"], name="br_rwkv")
    merge_rows = [u_m, u_r, col(proj, "gate_m"), col(proj, "gate_r")]
    (merged,) = _rowwise(lambda *t: (_f_merge(*t),), merge_rows, [], [(D, BF16)], tile=T, name="merge")
    out = _mm(merged, W["w_out"], name="out_proj")

    def head(ob, xb, tb, g):
        yn, vjp = jax.vjp(_rms, ob, g)
        err = xb + yn - tb
        dy = err * (1.0 / D)
        d_ob, d_g = vjp(dy)
        loss = jnp.broadcast_to(0.5 * jnp.sum(err * err) * (1.0 / D), (1, LANES))
        return dy, d_ob, loss, d_g

    dy, d_out, loss, g_g_post = _rowwise(head, [out, x, target], [W["g_post"]], [(D, F32), (D, BF16)],
                                         [(1, LANES), (1, D)], tile=T, name="head")
    d_merged = _mm(d_out, W["w_out"], tb=True, name="d_merged")
    g_w_out = _mm(merged, d_out, ta=True, out_dtype=BF16, name="g_w_out")

    def merge_bwd(u_m_b, u_r_b, g_m_b, g_r_b, dm):
        _, vjp = jax.vjp(_f_merge, u_m_b, u_r_b, g_m_b, g_r_b)
        return vjp(dm)

    d_u_m, d_u_r, d_gate_m, d_gate_r = _rowwise(merge_bwd, merge_rows + [d_merged], [], [(D, BF16)] * 4, tile=T,
                                                name="merge_bwd")
    d_ymg = _mm(d_u_m, W["w_br_mla"], tb=True, name="d_ymg")
    d_yrg = _mm(d_u_r, W["w_br_rwkv"], tb=True, name="d_yrg")
    g_w_br_mla = _mm(ymg, d_u_m, ta=True, out_dtype=BF16, name="g_w_br_mla")
    g_w_br_rwkv = _mm(yrg, d_u_r, ta=True, out_dtype=BF16, name="g_w_br_rwkv")

    def post_bwd(*args):
        nr = len(post_rows)
        prim, dm, dr = args[:nr] + args[nr + 2:], args[nr], args[nr + 1]
        _, vjp = jax.vjp(post_fn, *prim)
        g = vjp((dm, dr))
        return g[0], g[2], g[3], g[5], g[6], g[7], g[8], g[9], g[10], g[11]

    (d_y, d_r_bonus, d_k_bonus, d_v_bonus, d_z_r, d_o, d_z_m, g_gn_g, g_gn_b, g_r_k) = _rowwise(
        post_bwd, post_rows + [d_ymg, d_yrg], post_params,
        [(RW, F32), (RW, F32), (RW, F32), (RW, F32), (RW, BF16), (MW, F32), (MW, BF16)],
        [(1, RW)] * 3, tile=T // 2, name="post_bwd")

    d_y_h = to_h(d_y)
    dsc = {}
    for tag in ("f", "b"):
        rev, ops, st, _ = dirs[tag]
        g = _rwkv_scan_bwd(rev, *ops, st, d_y_h, hb=dims["hb"], name="scan_bwd_" + tag)
        dsc[tag] = [from_h(t) for t in g]

    d_qn, d_qr, d_kn, d_v_att, d_kr_h = _attention_bwd(qfull, kv, kr, d_o, hm, scale, tq=T, name="attn_bwd")

    def rope_bwd(qraw_b, kr_in, cos_b, sin_b, dqn_b, dqr_b, dkn_b, dv_b, dkrh_b, rot_b, rot_t_b):
        _, vjp = jax.vjp(lambda q_, k_: _f_rope(hm, q_, k_, cos_b, sin_b, rot_b, rot_t_b), qraw_b, kr_in)
        parts = []
        for hh in range(hm):
            parts += [dqn_b[:, hh * NOPE:(hh + 1) * NOPE], dqr_b[:, hh * NOPE:(hh + 1) * NOPE]]
        dkr = dkrh_b[:, :LANES]
        for hh in range(1, hm):
            dkr = dkr + dkrh_b[:, hh * LANES:(hh + 1) * LANES]
        d_qraw, d_kr_in = vjp((jnp.concatenate(parts, axis=1), dkr))
        return d_qraw, jnp.concatenate([dkn_b, dv_b], axis=1), d_kr_in

    d_qraw, d_kv, d_kr_in = _rowwise(rope_bwd, [qraw, kr_view, cosx, sinx, d_qn, d_qr, d_kn, d_v_att, d_kr_h],
                                     [rot, rot_t], [(hm * QHEAD, BF16), (2 * MW, BF16), (LANES, F32)], tile=T,
                                     name="rope_bwd")
    d_qnorm = _mm(d_qraw, W["wq_b"], tb=True, name="d_qn")
    d_kvnorm = _mm(d_kv, W["wkv_b"], tb=True, name="d_kvn")
    g_wq_b = _mm(qn, d_qraw, ta=True, out_dtype=BF16, name="g_wq_b")
    g_wkv_b = _mm(kvn, d_kv, ta=True, out_dtype=BF16, name="g_wkv_b")

    def mla_norm_bwd(q_a, kv_a, qg, kvg, dq, dk):
        _, vjp = jax.vjp(_f_mla_norm, q_a, kv_a, qg, kvg)
        return vjp((dq, dk))

    d_q_a, d_kv_a, g_q_norm, g_kv_norm = _rowwise(
        lambda q_a, kv_a, dq, dk, qg, kvg: mla_norm_bwd(q_a, kv_a, qg, kvg, dq, dk),
        [col(proj, "q_a"), col(proj, "kv_a"), d_qnorm, d_kvnorm], [W["mla_q_norm"], W["mla_kv_norm"]],
        [(QR, BF16), (KVR, BF16)], [(1, QR), (1, KVR)], tile=T, name="mla_norm_bwd")

    def pre_bwd(k_b_, tail_b, dlwf, dlwb, dkf, dkb, dkbon, daf, dab, dbf, dbb, drf, drb, drbon, dvf, dvb, dvbon,
                dkr, *params):
        _, vjp = jax.vjp(pre_fn, k_b_, tail_b, *params[:8], params[8], params[9])
        g = vjp((dlwf, dlwb, dkf + dkbon, dkb + dkbon, daf + dab, dbf, dbb))
        d_tail = g[1] + jnp.concatenate([dkr, jnp.zeros((dkr.shape[0], TAIL - LANES), F32)], axis=1)
        d_rl = jnp.concatenate([drf + drb + drbon, g[0], dvf + dvb + dvbon, d_tail], axis=1)
        return (d_rl,) + tuple(g[2:10])

    f_, b_ = dsc["f"], dsc["b"]
    pre_bwd_rows = [rl_k, rl_tail, f_[1], b_[1], f_[2], b_[2], d_k_bonus, f_[4], b_[4], f_[5], b_[5],
                    f_[0], b_[0], d_r_bonus, f_[3], b_[3], d_v_bonus, d_kr_in]
    (d_rl, g_w0_f, g_w0_b, g_a0_f, g_a0_b, g_k_k, g_k_a, g_w2cat, g_a2cat) = _rowwise(
        pre_bwd, pre_bwd_rows, pre_params, [(3 * RW + TAIL, F32)],
        [(1, RW)] * 6 + [(TAIL, 2 * RW)] * 2, tile=T // 2, name="rwkv_pre_bwd")
    d_shift, g_mu = _shift_lerp(shift_view, W["mu"], d_rl, name="shift_bwd")

    d_proj = jnp.concatenate([d_gate_m, d_gate_r, d_z_m, d_z_r, d_shift.astype(BF16), d_q_a, d_kv_a], axis=1)
    assert d_proj.shape == (S, d_in)
    d_h = _mm(d_proj, W["w_in"], tb=True, name="d_h")
    g_w_in = _mm(h, d_proj, ta=True, out_dtype=BF16, name="g_w_in")

    def pre_norm_bwd(xb, dyb, dhb, g):
        _, vjp = jax.vjp(_rms, xb, g)
        dx, dg = vjp(dhb)
        return dyb + dx, dg

    grad_x, g_g_pre = _rowwise(pre_norm_bwd, [x, dy, d_h], [W["g_pre"]], [(D, F32)], [(1, D)], tile=T,
                               name="pre_norm_bwd")

    grads = dict(g_pre=g_g_pre, w_in=g_w_in, mla_q_norm=g_q_norm, wq_b=g_wq_b, mla_kv_norm=g_kv_norm,
                 wkv_b=g_wkv_b, mu=g_mu, w0_f=g_w0_f, w0_b=g_w0_b, a0_f=g_a0_f, a0_b=g_a0_b, k_k=g_k_k, k_a=g_k_a,
                 w2cat=g_w2cat, a2cat=g_a2cat, r_k=g_r_k, gn_g=g_gn_g, gn_b=g_gn_b, w_br_mla=g_w_br_mla,
                 w_br_rwkv=g_w_br_rwkv, w_out=g_w_out, g_post=g_g_post)
    return loss[0, 0], grad_x, grads


_MATS = ["w_in", "mla_wq_b", "mla_wkv_b", "rwkv_w2_f", "rwkv_w2_b", "rwkv_a2_f", "rwkv_a2_b", "w_br_mla",
         "w_br_rwkv", "w_out"]
_ROW_SHARDED = ("w_out",)
_VECS = ["g_pre", "mla_q_norm", "mla_kv_norm", "rwkv_mu", "rwkv_w0_f", "rwkv_w0_b", "rwkv_a0_f", "rwkv_a0_b",
         "rwkv_k_k", "rwkv_k_a", "rwkv_r_k", "rwkv_gn_g", "rwkv_gn_b", "g_post"]
_WEIGHTS = ["g_pre", "w_in", "mla_q_norm", "mla_wq_b", "mla_kv_norm", "mla_wkv_b", "rwkv_mu", "rwkv_w0_f",
            "rwkv_w2_f", "rwkv_w0_b", "rwkv_w2_b", "rwkv_a0_f", "rwkv_a2_f", "rwkv_a0_b", "rwkv_a2_b", "rwkv_k_k",
            "rwkv_k_a", "rwkv_r_k", "rwkv_gn_g", "rwkv_gn_b", "w_br_mla", "w_br_rwkv", "w_out", "g_post"]
_PACK_QUANTUM = LANES * BF16_ROWS


def _exchange(src, *, name):
    bcast = src.ndim == 2
    R = src.shape[-2]

    def body(src_ref, out_ref, send_sems, recv_sems, local_sem):
        x, y, c = lax.axis_index("x"), lax.axis_index("y"), lax.axis_index("c")
        me = 4 * x + 2 * y + c
        flip = lambda v, bit: (1 - v) if bit else v
        piece = lambda idx: src_ref if bcast else src_ref.at[idx]
        own = pltpu.make_async_copy(piece(me), out_ref.at[me], local_sem)
        own.start()
        sends, peers = [], []
        for d in range(1, N_DEV):
            px, py, pc = flip(x, d & 4), flip(y, d & 2), flip(c, d & 1)
            pidx = 4 * px + 2 * py + pc
            peers.append(((px, py, pc), pidx))
            cp = pltpu.make_async_remote_copy(src_ref=piece(pidx), dst_ref=out_ref.at[me], send_sem=send_sems.at[d - 1],
                                              recv_sem=recv_sems.at[d - 1], device_id=(px, py, pc),
                                              device_id_type=pl.DeviceIdType.MESH)
            cp.start()
            sends.append(cp)
        for d, (peer, pidx) in zip(range(1, N_DEV), peers):
            pltpu.make_async_remote_copy(src_ref=piece(pidx), dst_ref=out_ref.at[pidx], send_sem=send_sems.at[d - 1],
                                         recv_sem=recv_sems.at[d - 1], device_id=peer,
                                         device_id_type=pl.DeviceIdType.MESH).wait_recv()
        for cp in sends:
            cp.wait_send()
        own.wait()

    return pl.pallas_call(
        body, name=name, out_shape=jax.ShapeDtypeStruct((N_DEV, R, LANES), src.dtype),
        in_specs=[pl.BlockSpec(memory_space=pl.ANY)], out_specs=pl.BlockSpec(memory_space=pl.ANY),
        scratch_shapes=[pltpu.SemaphoreType.DMA((N_DEV - 1,)), pltpu.SemaphoreType.DMA((N_DEV - 1,)),
                        pltpu.SemaphoreType.DMA],
    )(src)


def _adamw(recv, w, m, v, *, name):
    r, c = w.shape
    tr = r if r <= 256 else _pick_rows(r, 256)

    def body(g_ref, w_ref, m_ref, v_ref, go_ref, d_ref, mo_ref, vo_ref):
        g = g_ref[0].astype(F32)
        for k in range(1, N_DEV):
            g = g + g_ref[k].astype(F32)
        m_new = ADAM_B1 * m_ref[...] + (1.0 - ADAM_B1) * g
        v_new = ADAM_B2 * v_ref[...] + (1.0 - ADAM_B2) * (g * g)
        m_hat = m_new / (1.0 - ADAM_B1 ** ADAM_STEP)
        v_hat = v_new / (1.0 - ADAM_B2 ** ADAM_STEP)
        go_ref[...] = g
        d_ref[...] = -ADAM_LR * (m_hat / (jnp.sqrt(v_hat) + ADAM_EPS) + ADAM_WD * w_ref[...])
        mo_ref[...] = m_new
        vo_ref[...] = v_new

    blk = pl.BlockSpec((tr, c), lambda i: (i, 0))
    return pl.pallas_call(
        body, name=name, grid=(r // tr,),
        in_specs=[pl.BlockSpec((N_DEV, tr, c), lambda i: (0, i, 0)), blk, blk, blk], out_specs=[blk] * 4,
        out_shape=[jax.ShapeDtypeStruct((r, c), F32)] * 4, compiler_params=_cparams(("parallel",)),
    )(recv, w, m, v)


def _pick_rows(n, cap):
    for t in range(cap, 0, -BF16_ROWS):
        if n % t == 0:
            return t
    raise ValueError(f"no row tile for {n}")


def _pack(pieces, dtype, quantum):
    out = []
    for p in pieces:
        lead, n = p.shape[:-1], p.shape[-1]
        pad = (-n) % quantum
        p = p.astype(dtype)
        if pad:
            p = jnp.concatenate([p, jnp.zeros(lead + (pad,), dtype)], axis=-1)
        out.append(p)
    flat = jnp.concatenate(out, axis=-1)
    return flat.reshape(flat.shape[:-1] + (flat.shape[-1] // LANES, LANES))


def _unpack(flat, sizes, quantum):
    flat = flat.reshape(flat.shape[:-2] + (-1,))
    out, o = [], 0
    for n in sizes:
        out.append(flat[..., o:o + n])
        o += n + (-n) % quantum
    return out


def _prepare_weights(full, vec, dims):
    hm, hr, hn, rank = dims["hm"], dims["hr"], dims["hn"], dims["rank"]
    D, QR, KVR = dims["D"], dims["QR"], dims["KVR"]
    MW, RW, TAIL = hm * VDIM, hr * hn, dims["TAIL"]
    w_in = full["w_in"]
    o = dims["orig"]
    cut = lambda nme: w_in[:, o[nme][0]:o[nme][0] + o[nme][1]]
    tail = jnp.concatenate([cut("k_rope"), cut("lora"), jnp.zeros((D, TAIL - ROPE - 4 * rank), BF16)], axis=1)
    w_in_p = jnp.concatenate([cut("gate_m"), cut("gate_r"), cut("z_m"), cut("z_r"), cut("rkv"), tail, cut("q_a"),
                              cut("kv_a")], axis=1)
    wq = full["mla_wq_b"].reshape(QR, hm, NOPE + ROPE)
    wq = jnp.concatenate([wq, jnp.zeros((QR, hm, QHEAD - NOPE - ROPE), BF16)], axis=2).reshape(QR, hm * QHEAD)
    wkv = full["mla_wkv_b"].reshape(KVR, hm, 2, NOPE).transpose(0, 2, 1, 3).reshape(KVR, 2 * hm * NOPE)
    z = lambda rows: jnp.zeros((rows, RW), F32)
    f = lambda nme: full[nme].astype(F32)
    w2cat = jnp.concatenate([
        jnp.concatenate([z(ROPE), f("rwkv_w2_f"), z(TAIL - ROPE - rank)], axis=0),
        jnp.concatenate([z(ROPE + rank), f("rwkv_w2_b"), z(TAIL - ROPE - 2 * rank)], axis=0)], axis=1)
    a2cat = jnp.concatenate([
        jnp.concatenate([z(ROPE + 2 * rank), f("rwkv_a2_f"), z(TAIL - ROPE - 3 * rank)], axis=0),
        jnp.concatenate([z(ROPE + 3 * rank), f("rwkv_a2_b"), z(TAIL - ROPE - 4 * rank)], axis=0)], axis=1)
    mu = vec["rwkv_mu"]
    mu_p = jnp.concatenate([mu[:3 * RW], jnp.zeros((ROPE,), F32), mu[3 * RW:],
                            jnp.zeros((TAIL - ROPE - 4 * rank,), F32)])
    row = lambda t: t.reshape(1, -1)
    return dict(
        w_in=w_in_p, wq_b=wq, wkv_b=wkv, w2cat=w2cat, a2cat=a2cat, mu=row(mu_p),
        w_br_mla=full["w_br_mla"], w_br_rwkv=full["w_br_rwkv"], w_out=full["w_out"],
        g_pre=row(vec["g_pre"]), g_post=row(vec["g_post"]), mla_q_norm=row(vec["mla_q_norm"]),
        mla_kv_norm=row(vec["mla_kv_norm"]), w0_f=row(vec["rwkv_w0_f"]), w0_b=row(vec["rwkv_w0_b"]),
        a0_f=row(vec["rwkv_a0_f"]), a0_b=row(vec["rwkv_a0_b"]), k_k=row(vec["rwkv_k_k"]), k_a=row(vec["rwkv_k_a"]),
        r_k=row(vec["rwkv_r_k"]), gn_g=row(vec["rwkv_gn_g"]), gn_b=row(vec["rwkv_gn_b"]))


def _restore_grads(g, dims):
    hm, hr, hn, rank = dims["hm"], dims["hr"], dims["hn"], dims["rank"]
    D, QR, KVR = dims["D"], dims["QR"], dims["KVR"]
    MW, RW, TAIL = hm * VDIM, hr * hn, dims["TAIL"]
    lay, _ = _layout(D, MW, RW, TAIL, QR, KVR)
    gw = g["w_in"]
    seg = lambda nme: gw[:, lay[nme][0]:lay[nme][0] + lay[nme][1]]
    t0 = lay["tail"][0]
    w_in = jnp.concatenate([seg("q_a"), seg("kv_a"), gw[:, t0:t0 + ROPE], seg("r"), seg("k"), seg("v"),
                            gw[:, t0 + ROPE:t0 + ROPE + 4 * rank], seg("z_m"), seg("z_r"), seg("gate_m"),
                            seg("gate_r")], axis=1)
    wq = g["wq_b"].reshape(QR, hm, QHEAD)[:, :, :NOPE + ROPE].reshape(QR, hm * (NOPE + ROPE))
    wkv = g["wkv_b"].reshape(KVR, 2, hm, NOPE).transpose(0, 2, 1, 3).reshape(KVR, 2 * hm * NOPE)
    lo = lambda t, i, half: t[ROPE + i * rank:ROPE + (i + 1) * rank, half * RW:(half + 1) * RW]
    mu = g["mu"][0]
    out = dict(
        w_in=w_in, mla_wq_b=wq, mla_wkv_b=wkv, rwkv_w2_f=lo(g["w2cat"], 0, 0), rwkv_w2_b=lo(g["w2cat"], 1, 1),
        rwkv_a2_f=lo(g["a2cat"], 2, 0), rwkv_a2_b=lo(g["a2cat"], 3, 1), w_br_mla=g["w_br_mla"],
        w_br_rwkv=g["w_br_rwkv"], w_out=g["w_out"],
        rwkv_mu=jnp.concatenate([mu[:3 * RW], mu[3 * RW + ROPE:3 * RW + ROPE + 4 * rank]]),
        g_pre=g["g_pre"][0], g_post=g["g_post"][0], mla_q_norm=g["mla_q_norm"][0], mla_kv_norm=g["mla_kv_norm"][0],
        rwkv_w0_f=g["w0_f"][0], rwkv_w0_b=g["w0_b"][0], rwkv_a0_f=g["a0_f"][0], rwkv_a0_b=g["a0_b"][0],
        rwkv_k_k=g["k_k"][0], rwkv_k_a=g["k_a"][0], rwkv_r_k=g["r_k"][0], rwkv_gn_g=g["gn_g"][0],
        rwkv_gn_b=g["gn_b"][0])
    return out


def _dims(inp):
    D = inp["x"].shape[-1]
    QR, KVR = inp["mla_q_norm"].shape[0], inp["mla_kv_norm"].shape[0]
    hm = inp["mla_wq_b"].shape[1] * N_DEV // (NOPE + ROPE)
    hr, hn = inp["rwkv_r_k"].shape
    rank = inp["rwkv_w2_f"].shape[0]
    MW, RW = hm * VDIM, hr * hn
    TAIL = -(-(ROPE + 4 * rank) // LANES) * LANES
    orig, o = {}, 0
    for nme, w in (("q_a", QR), ("kv_a", KVR), ("k_rope", ROPE), ("rkv", 3 * RW), ("lora", 4 * rank), ("z_m", MW),
                   ("z_r", RW), ("gate_m", D), ("gate_r", D)):
        orig[nme] = (o, w)
        o += w
    assert o == inp["w_in"].shape[1] * N_DEV
    return dict(D=D, QR=QR, KVR=KVR, hm=hm, hr=hr, hn=hn, rank=rank, TAIL=TAIL, orig=orig, hb=min(hr, 4))


def _gather_matrices(inp):
    shards = [inp[n] for n in _MATS]
    flat = _pack([s.reshape(-1) for s in shards], BF16, _PACK_QUANTUM)
    got = _exchange(flat, name="gather_weights")
    pieces = _unpack(got, [s.size for s in shards], _PACK_QUANTUM)
    full = {}
    for n, s, p in zip(_MATS, shards, pieces):
        r, c = s.shape
        p = p.reshape(N_DEV, r, c)
        full[n] = p.reshape(N_DEV * r, c) if n in _ROW_SHARDED else p.transpose(1, 0, 2).reshape(r, N_DEV * c)
    return full


def _scatter_matrix_grads(g, inp):
    shards = [inp[n] for n in _MATS]
    pieces = []
    for n, s in zip(_MATS, shards):
        r, c = s.shape
        t = g[n].astype(BF16)
        t = t.reshape(N_DEV, r, c) if n in _ROW_SHARDED else t.reshape(r, N_DEV, c).transpose(1, 0, 2)
        pieces.append(t.reshape(N_DEV, r * c))
    got = _exchange(_pack(pieces, BF16, _PACK_QUANTUM), name="scatter_grads")
    parts = _unpack(got, [s.size for s in shards], _PACK_QUANTUM)
    return {n: p.reshape((N_DEV,) + s.shape) for n, s, p in zip(_MATS, shards, parts)}


def kernel(x, g_pre, w_in, mla_q_norm, mla_wq_b, mla_kv_norm, mla_wkv_b, rwkv_mu, rwkv_w0_f, rwkv_w2_f, rwkv_w0_b, rwkv_w2_b, rwkv_a0_f, rwkv_a2_f, rwkv_a0_b, rwkv_a2_b, rwkv_k_k, rwkv_k_a, rwkv_r_k, rwkv_gn_g, rwkv_gn_b, w_br_mla, w_br_rwkv, w_out, g_post, loss_target, m_g_pre, m_w_in, m_mla_q_norm, m_mla_wq_b, m_mla_kv_norm, m_mla_wkv_b, m_rwkv_mu, m_rwkv_w0_f, m_rwkv_w2_f, m_rwkv_w0_b, m_rwkv_w2_b, m_rwkv_a0_f, m_rwkv_a2_f, m_rwkv_a0_b, m_rwkv_a2_b, m_rwkv_k_k, m_rwkv_k_a, m_rwkv_r_k, m_rwkv_gn_g, m_rwkv_gn_b, m_w_br_mla, m_w_br_rwkv, m_w_out, m_g_post, v_g_pre, v_w_in, v_mla_q_norm, v_mla_wq_b, v_mla_kv_norm, v_mla_wkv_b, v_rwkv_mu, v_rwkv_w0_f, v_rwkv_w2_f, v_rwkv_w0_b, v_rwkv_w2_b, v_rwkv_a0_f, v_rwkv_a2_f, v_rwkv_a0_b, v_rwkv_a2_b, v_rwkv_k_k, v_rwkv_k_a, v_rwkv_r_k, v_rwkv_gn_g, v_rwkv_gn_b, v_w_br_mla, v_w_br_rwkv, v_w_out, v_g_post):
    inp = dict(locals())
    dims = _dims(inp)
    full = _gather_matrices(inp)
    W = _prepare_weights(full, {n: inp[n] for n in _VECS}, dims)
    loss, grad_x, g = _local_grads(x[0], loss_target[0], W, dims)
    loss = lax.psum(loss, ("x", "y", "c"))
    g = _restore_grads(g, dims)

    new = {}
    recv = _scatter_matrix_grads(g, inp)
    for n in _MATS:
        new[n] = _adamw(recv[n], inp[n], inp["m_" + n], inp["v_" + n], name="adamw_" + n)

    vsizes = [inp[n].size for n in _VECS]
    vflat = lambda prefix, src: _pack([src[prefix + n].reshape(-1) for n in _VECS], F32, LANES * 8)
    vrecv = _exchange(vflat("", g), name="gather_vector_grads")
    vout = _adamw(vrecv.reshape(N_DEV, -1, LANES), vflat("", inp), vflat("m_", inp), vflat("v_", inp),
                  name="adamw_vectors")
    vparts = [_unpack(t, vsizes, LANES * 8) for t in vout]
    for i, n in enumerate(_VECS):
        new[n] = [vp[i].reshape(inp[n].shape) for vp in vparts]

    outs = [loss, grad_x[None]]
    for k in range(4):
        outs += [new[n][k] for n in _WEIGHTS]
    return tuple(outs)
```

```python
import functools
import math

import jax
import jax.numpy as jnp
from jax import lax
from jax.experimental import pallas as pl
from jax.experimental.pallas import tpu as pltpu

F32 = jnp.float32
BF16 = jnp.bfloat16

N_DEV = 8
LANES = 128
BF16_ROWS = 16
NOPE, ROPE, VDIM = 128, 64, 128
QHEAD = 256
ROPE_THETA = 10000.0
NORM_EPS = 1e-6
GN_EPS = 64e-5
CHUNK = 64
SUB = 16
VMEM_LIMIT = 56 * 1024 * 1024

ADAM_LR, ADAM_B1, ADAM_B2, ADAM_EPS, ADAM_WD, ADAM_STEP = 0.001, 0.9, 0.999, 1e-08, 0.01, 10


def _cparams(sem):
    return pltpu.CompilerParams(dimension_semantics=sem, vmem_limit_bytes=VMEM_LIMIT)


def _pick(n, cap):
    if n <= cap:
        return n
    for t in range(cap - cap % LANES, 0, -LANES):
        if n % t == 0:
            return t
    raise ValueError(f"no tile for {n} under {cap}")


def _mm(a, b, *, ta=False, tb=False, out_dtype=F32, name, tm_cap=1024, tn_cap=512, tk_cap=2048):
    K, M = a.shape if ta else a.shape[::-1]
    N = b.shape[0] if tb else b.shape[1]
    assert (b.shape[1] if tb else b.shape[0]) == K, (a.shape, b.shape, ta, tb)
    tm, tn, tk = _pick(M, tm_cap), _pick(N, tn_cap), _pick(K, tk_cap)
    nk = K // tk
    dn = (((0 if ta else 1,), (1 if tb else 0,)), ((), ()))

    def body(a_ref, b_ref, o_ref, acc_ref):
        k = pl.program_id(2)
        p = lax.dot_general(a_ref[...], b_ref[...], dn, preferred_element_type=F32)

        @pl.when(k == 0)
        def _():
            acc_ref[...] = p

        @pl.when(k > 0)
        def _():
            acc_ref[...] += p

        @pl.when(k == nk - 1)
        def _():
            o_ref[...] = acc_ref[...].astype(out_dtype)

    a_spec = pl.BlockSpec((tk, tm), lambda i, j, k: (k, i)) if ta else pl.BlockSpec((tm, tk), lambda i, j, k: (i, k))
    b_spec = pl.BlockSpec((tn, tk), lambda i, j, k: (j, k)) if tb else pl.BlockSpec((tk, tn), lambda i, j, k: (k, j))
    return pl.pallas_call(
        body, name=name, grid=(M // tm, N // tn, nk),
        in_specs=[a_spec, b_spec], out_specs=pl.BlockSpec((tm, tn), lambda i, j, k: (i, j)),
        out_shape=jax.ShapeDtypeStruct((M, N), out_dtype),
        scratch_shapes=[pltpu.VMEM((tm, tn), F32)],
        compiler_params=_cparams(("parallel", "parallel", "arbitrary")),
    )(a, b)


def _view(arr, off, width):
    assert off % width == 0, (off, width)
    return (arr, off // width, width)


def _rowwise(fn, rows, params, out_rows, out_accs=(), *, tile, name):
    rows = [r if isinstance(r, tuple) else (r, 0, r.shape[1]) for r in rows]
    S = rows[0][0].shape[0]
    T = min(tile, S)
    assert S % T == 0
    n_rows, n_par, n_out = len(rows), len(params), len(out_rows)

    def body(*refs):
        ins = [r[...] for r in refs[:n_rows + n_par]]
        outs = fn(*ins)
        out_refs = refs[n_rows + n_par:]
        for o_ref, val in zip(out_refs[:n_out], outs[:n_out]):
            o_ref[...] = val.astype(o_ref.dtype)
        i = pl.program_id(0)
        for o_ref, val in zip(out_refs[n_out:], outs[n_out:]):
            @pl.when(i == 0)
            def _(o_ref=o_ref, val=val):
                o_ref[...] = val

            @pl.when(i > 0)
            def _(o_ref=o_ref, val=val):
                o_ref[...] += val

    in_specs = [pl.BlockSpec((T, w), functools.partial(lambda i, cb: (i, cb), cb=cb)) for _, cb, w in rows]
    in_specs += [pl.BlockSpec(p.shape, lambda i: (0, 0)) for p in params]
    out_specs = [pl.BlockSpec((T, w), lambda i: (i, 0)) for w, _ in out_rows]
    out_specs += [pl.BlockSpec(s, lambda i: (0, 0)) for s in out_accs]
    out_shape = [jax.ShapeDtypeStruct((S, w), dt) for w, dt in out_rows]
    out_shape += [jax.ShapeDtypeStruct(s, F32) for s in out_accs]
    return pl.pallas_call(
        body, name=name, grid=(S // T,), in_specs=in_specs, out_specs=out_specs, out_shape=out_shape,
        compiler_params=_cparams(("arbitrary",)),
    )(*[r[0] for r in rows], *params)


def _split3(x):
    hi = x.astype(BF16)
    r1 = x - hi.astype(F32)
    mid = r1.astype(BF16)
    lo = (r1 - mid.astype(F32)).astype(BF16)
    return hi, mid, lo


def _mm_sel(x, sel):
    hi, mid, lo = _split3(x)
    d = lambda u: jnp.dot(u, sel, preferred_element_type=F32)
    return d(hi) + d(mid) + d(lo)


@jax.custom_vjp
def _sel(x, sel, sel_t):
    return _mm_sel(x, sel)


def _sel_fwd(x, sel, sel_t):
    return _mm_sel(x, sel), (sel, sel_t)


def _sel_bwd(res, ct):
    sel, sel_t = res
    return _mm_sel(ct, sel_t), jnp.zeros_like(sel), jnp.zeros_like(sel_t)


_sel.defvjp(_sel_fwd, _sel_bwd)


def _rms(x, g):
    return x * lax.rsqrt(jnp.mean(x * x, axis=-1, keepdims=True) + NORM_EPS) * g


def _sigmoid(x):
    return 1.0 / (1.0 + jnp.exp(-x))


def _silu(x):
    return x * _sigmoid(x)


def _softplus(x):
    return jnp.maximum(x, 0.0) + jnp.log(1.0 + jnp.exp(-jnp.abs(x)))


def _bdot(x, w):
    return jnp.dot(x.astype(BF16), w.astype(BF16), preferred_element_type=F32)


def _f_mla_norm(q_a, kv_a, qg, kvg):
    return _rms(q_a, qg), _rms(kv_a, kvg)


def _f_rope(hm, qraw, kr_in, cosx, sinx, rot, rot_t):
    def rope(t):
        return t * cosx + _sel(t, rot, rot_t) * sinx
    parts = []
    for h in range(hm):
        parts.append(qraw[:, h * QHEAD:h * QHEAD + NOPE])
        parts.append(rope(qraw[:, h * QHEAD + NOPE:(h + 1) * QHEAD]))
    return jnp.concatenate(parts, axis=1), rope(kr_in)


def _attn_block(qn, qr, kn, kr, v, scale):
    nt = (((1,), (1,)), ((), ()))
    s = lax.dot_general(qn.astype(BF16), kn.astype(BF16), nt, preferred_element_type=F32)
    s = s + lax.dot_general(qr.astype(BF16), kr.astype(BF16), nt, preferred_element_type=F32)
    s = s * scale
    p = jnp.exp(s - jnp.max(s, axis=-1, keepdims=True))
    p = p / jnp.sum(p, axis=-1, keepdims=True)
    return jnp.dot(p.astype(BF16), v.astype(BF16), preferred_element_type=F32)


def _f_rwkv_pre(rw, k, tail, w0f, w0b, a0f, a0b, k_k, k_a, w2cat, a2cat, seg, seg_t):
    zw = _bdot(jnp.tanh(tail), w2cat)
    za = _bdot(tail, a2cat)
    lw_f = -jnp.exp(-_softplus(-(w0f + zw[:, :rw])) - 0.5)
    lw_b = -jnp.exp(-_softplus(-(w0b + zw[:, rw:])) - 0.5)
    a_f = _sigmoid(a0f + za[:, :rw])
    a_b = _sigmoid(a0b + za[:, rw:])
    kk = k * k_k
    nrm = jnp.sqrt(_sel(_sel(kk * kk, seg, seg_t), seg_t, seg))
    kk = kk / jnp.maximum(nrm, 1e-12)
    k_f = k * (1.0 + (a_f - 1.0) * k_a)
    k_b = k * (1.0 + (a_b - 1.0) * k_a)
    return lw_f, lw_b, k_f, k_b, -kk, kk * a_f, kk * a_b


def _f_post(hn, y_f, y_b, r, k_f, k_b, v, z_r, o_mla, z_m, gn_g, gn_b, r_k, seg, seg_t):
    segsum = lambda t: _sel(_sel(t, seg, seg_t), seg_t, seg)
    y = y_f + y_b
    mu = segsum(y) * (1.0 / hn)
    yc = y - mu
    var = segsum(yc * yc) * (1.0 / hn)
    yn = yc * lax.rsqrt(var + GN_EPS) * gn_g + gn_b
    bonus = segsum(r * (k_f + k_b) * r_k) * v
    return o_mla * _silu(z_m), (yn + bonus) * _silu(z_r)


def _f_merge(u_m, u_r, g_m, g_r):
    return _sigmoid(g_m) * u_m + _sigmoid(g_r) * u_r


_NN = ((2,), (1,))
_NT = ((2,), (2,))
_TN = ((1,), (1,))

_SCAN_PASSES = {"cum": 3, "gram": 3, "solve": 1, "apply": 1, "state": 1}


def _hdot_raw(passes, x, y, dims):
    dn = (dims, ((0,), (0,)))
    d = lambda p, q: lax.dot_general(p, q, dn, preferred_element_type=F32)
    xh = x.astype(BF16)
    yh = y.astype(BF16)
    if passes == 1:
        return d(xh, yh)
    xl = (x - xh.astype(F32)).astype(BF16)
    yl = (y - yh.astype(F32)).astype(BF16)
    return d(xh, yh) + d(xh, yl) + d(xl, yh)


@functools.partial(jax.custom_vjp, nondiff_argnums=(2, 3))
def _hdot_p(x, y, dims, passes):
    return _hdot_raw(passes, x, y, dims)


def _hdot_fwd(x, y, dims, passes):
    return _hdot_raw(passes, x, y, dims), (x, y)


def _hdot_bwd(dims, passes, res, ct):
    x, y = res
    if dims == _NN:
        return _hdot_raw(passes, ct, y, _NT), _hdot_raw(passes, x, ct, _TN)
    if dims == _NT:
        return _hdot_raw(passes, ct, y, _NN), _hdot_raw(passes, ct, x, _TN)
    return _hdot_raw(passes, y, ct, _NT), _hdot_raw(passes, x, ct, _NN)


_hdot_p.defvjp(_hdot_fwd, _hdot_bwd)


def _hdot(x, y, dims, kind):
    return _hdot_p(x, y, dims, _SCAN_PASSES[kind])


def _tri_solve(n_mat, x, length):
    row = lax.broadcasted_iota(jnp.int32, (length, length), 0)
    col = lax.broadcasted_iota(jnp.int32, (length, length), 1)
    eye = (row == col).astype(F32)[None]
    diag_blk = ((row // SUB) == (col // SUB))[None]
    nd = jnp.where(diag_blk, n_mat, 0.0)
    no = n_mat - nd
    dinv = eye + nd
    p = nd
    for _ in range(int(math.log2(SUB)) - 1):
        p = _hdot(p, p, _NN, "solve")
        dinv = dinv + _hdot(dinv, p, _NN, "solve")
    q = _hdot(dinv, no, _NN, "solve")
    u = _hdot(dinv, x, _NN, "solve")
    levels = int(math.log2(length // SUB))
    qs = [q]
    for _ in range(levels - 1):
        qs.append(_hdot(qs[-1], qs[-1], _NN, "solve"))
    for qk in reversed(qs):
        u = u + _hdot(qk, u, _NN, "solve")
    return u


def _rwkv_chunk(rev, s0, r, lw, k, v, a, b):
    h, length, _ = r.shape
    row = lax.broadcasted_iota(jnp.int32, (length, length), 0)
    col = lax.broadcasted_iota(jnp.int32, (length, length), 1)
    incl = (row <= col) if rev else (row >= col)
    strict = (row < col) if rev else (row > col)
    t_incl = jnp.broadcast_to(incl.astype(F32)[None], (h, length, length))
    cum = _hdot(t_incl, lw, _NN, "cum")
    g = jnp.exp(cum)
    g_inv = jnp.exp(-cum)
    at = a * jnp.exp(cum - lw)
    rt = r * g
    bt = b * g_inv
    kt = k * g_inv
    a_ab = jnp.where(strict[None], _hdot(at, bt, _NT, "gram"), 0.0)
    a_ak = jnp.where(strict[None], _hdot(at, kt, _NT, "gram"), 0.0)
    a_rb = jnp.where(incl[None], _hdot(rt, bt, _NT, "gram"), 0.0)
    a_rk = jnp.where(incl[None], _hdot(rt, kt, _NT, "gram"), 0.0)
    x = _hdot(at, s0, _NT, "apply") + _hdot(a_ak, v, _NN, "apply")
    u = _tri_solve(a_ab, x, length)
    y = _hdot(rt, s0, _NT, "apply") + _hdot(a_rb, u, _NN, "apply") + _hdot(a_rk, v, _NN, "apply")
    g_last = g[:, 0:1, :] if rev else g[:, length - 1:length, :]
    s1 = (s0 + _hdot(u, bt, _TN, "state") + _hdot(v, kt, _TN, "state")) * g_last
    return y, s1


def _scan_specs(hb, n, nc, rev):
    cidx = (lambda c: nc - 1 - c) if rev else (lambda c: c)
    seq = pl.BlockSpec((hb, CHUNK, n), lambda g, c: (g, cidx(c), 0))
    st = pl.BlockSpec((1, hb, n, n), lambda g, c: (cidx(c), g, 0, 0))
    return seq, st


def _rwkv_scan_fwd(rev, r, lw, k, v, a, b, *, hb, name):
    H, S, n = r.shape
    nc = S // CHUNK
    seq, st = _scan_specs(hb, n, nc, rev)

    def body(r_ref, lw_ref, k_ref, v_ref, a_ref, b_ref, y_ref, st_ref, s_ref):
        @pl.when(pl.program_id(1) == 0)
        def _():
            s_ref[...] = jnp.zeros_like(s_ref)

        s0 = s_ref[...]
        st_ref[0] = s0
        y, s1 = _rwkv_chunk(rev, s0, r_ref[...], lw_ref[...], k_ref[...], v_ref[...], a_ref[...], b_ref[...])
        y_ref[...] = y
        s_ref[...] = s1

    return pl.pallas_call(
        body, name=name, grid=(H // hb, nc), in_specs=[seq] * 6, out_specs=[seq, st],
        out_shape=[jax.ShapeDtypeStruct((H, S, n), F32), jax.ShapeDtypeStruct((nc, H, n, n), F32)],
        scratch_shapes=[pltpu.VMEM((hb, n, n), F32)],
        compiler_params=_cparams(("parallel", "arbitrary")),
    )(r, lw, k, v, a, b)


def _rwkv_scan_bwd(rev, r, lw, k, v, a, b, states, dy, *, hb, name):
    H, S, n = r.shape
    nc = S // CHUNK
    seq, st = _scan_specs(hb, n, nc, not rev)

    def body(r_ref, lw_ref, k_ref, v_ref, a_ref, b_ref, st_ref, dy_ref, *rest):
        out_refs, ds_ref = rest[:6], rest[6]

        @pl.when(pl.program_id(1) == 0)
        def _():
            ds_ref[...] = jnp.zeros_like(ds_ref)

        _, vjp = jax.vjp(functools.partial(_rwkv_chunk, rev), st_ref[0], r_ref[...], lw_ref[...], k_ref[...],
                         v_ref[...], a_ref[...], b_ref[...])
        grads = vjp((dy_ref[...], ds_ref[...]))
        ds_ref[...] = grads[0]
        for o_ref, gval in zip(out_refs, grads[1:]):
            o_ref[...] = gval

    return pl.pallas_call(
        body, name=name, grid=(H // hb, nc), in_specs=[seq] * 6 + [st, seq], out_specs=[seq] * 6,
        out_shape=[jax.ShapeDtypeStruct((H, S, n), F32)] * 6,
        scratch_shapes=[pltpu.VMEM((hb, n, n), F32)],
        compiler_params=_cparams(("parallel", "arbitrary")),
    )(r, lw, k, v, a, b, states, dy)


def _shift_lerp(x_view, mu, d=None, *, name):
    arr, off, width = x_view
    S = arr.shape[0]
    cb = _pick(width, 256)
    assert off % cb == 0

    def cshift(t):
        rows = lax.broadcasted_iota(jnp.int32, t.shape, 0)
        prev = jnp.where(rows == 0, 0.0, pltpu.roll(t, 1, 0))
        nxt = jnp.where(rows == S - 1, 0.0, pltpu.roll(t, S - 1, 0))
        return 0.5 * (prev + nxt)

    def fwd_body(x_ref, mu_ref, o_ref):
        x = x_ref[...]
        o_ref[...] = x + mu_ref[...] * (cshift(x) - x)

    def bwd_body(x_ref, mu_ref, d_ref, dx_ref, dmu_ref):
        x, m, dd = x_ref[...], mu_ref[...], d_ref[...]
        gm = m * dd
        dx_ref[...] = dd - gm + cshift(gm)
        dmu_ref[...] = jnp.sum(dd * (cshift(x) - x), axis=0, keepdims=True)

    x_spec = pl.BlockSpec((S, cb), lambda j: (0, off // cb + j))
    blk = pl.BlockSpec((S, cb), lambda j: (0, j))
    vec = pl.BlockSpec((1, cb), lambda j: (0, j))
    if d is None:
        return pl.pallas_call(
            fwd_body, name=name, grid=(width // cb,), in_specs=[x_spec, vec], out_specs=blk,
            out_shape=jax.ShapeDtypeStruct((S, width), F32), compiler_params=_cparams(("parallel",)),
        )(arr, mu)
    return pl.pallas_call(
        bwd_body, name=name, grid=(width // cb,), in_specs=[x_spec, vec, blk], out_specs=[blk, vec],
        out_shape=[jax.ShapeDtypeStruct((S, width), F32), jax.ShapeDtypeStruct((1, width), F32)],
        compiler_params=_cparams(("parallel",)),
    )(arr, mu, d)


def _attention_fwd(qfull, kv, kr, hm, scale, *, tq, name):
    S = qfull.shape[0]

    def body(qn_ref, qr_ref, kn_ref, kr_ref, v_ref, o_ref):
        f = lambda ref: ref[...].astype(F32)
        o_ref[...] = _attn_block(f(qn_ref), f(qr_ref), f(kn_ref), f(kr_ref), f(v_ref), scale)

    return pl.pallas_call(
        body, name=name, grid=(hm, S // tq),
        in_specs=[pl.BlockSpec((tq, NOPE), lambda h, i: (i, 2 * h)),
                  pl.BlockSpec((tq, NOPE), lambda h, i: (i, 2 * h + 1)),
                  pl.BlockSpec((S, NOPE), lambda h, i: (0, h)),
                  pl.BlockSpec((S, LANES), lambda h, i: (0, 0)),
                  pl.BlockSpec((S, VDIM), lambda h, i: (0, hm + h))],
        out_specs=pl.BlockSpec((tq, VDIM), lambda h, i: (i, h)),
        out_shape=jax.ShapeDtypeStruct((S, hm * VDIM), F32),
        compiler_params=_cparams(("parallel", "parallel")),
    )(qfull, qfull, kv, kr, kv)


def _attention_bwd(qfull, kv, kr, d_o, hm, scale, *, tq, name):
    S = qfull.shape[0]

    def body(qn_ref, qr_ref, kn_ref, kr_ref, v_ref, do_ref, dqn_ref, dqr_ref, dkn_ref, dv_ref, dkr_ref):
        f = lambda ref: ref[...].astype(F32)
        _, vjp = jax.vjp(functools.partial(_attn_block, scale=scale), f(qn_ref), f(qr_ref), f(kn_ref), f(kr_ref),
                         f(v_ref))
        dqn, dqr, dkn, dkr, dv = vjp(do_ref[...])
        dqn_ref[...] = dqn
        dqr_ref[...] = dqr
        first = pl.program_id(1) == 0
        for ref, val in ((dkn_ref, dkn), (dv_ref, dv), (dkr_ref, dkr)):
            @pl.when(first)
            def _(ref=ref, val=val):
                ref[...] = val

            @pl.when(jnp.logical_not(first))
            def _(ref=ref, val=val):
                ref[...] += val

    qblk = pl.BlockSpec((tq, NOPE), lambda h, i: (i, h))
    kblk = pl.BlockSpec((S, NOPE), lambda h, i: (0, h))
    shp = jax.ShapeDtypeStruct((S, hm * NOPE), F32)
    return pl.pallas_call(
        body, name=name, grid=(hm, S // tq),
        in_specs=[pl.BlockSpec((tq, NOPE), lambda h, i: (i, 2 * h)),
                  pl.BlockSpec((tq, NOPE), lambda h, i: (i, 2 * h + 1)),
                  kblk,
                  pl.BlockSpec((S, LANES), lambda h, i: (0, 0)),
                  pl.BlockSpec((S, VDIM), lambda h, i: (0, hm + h)),
                  qblk],
        out_specs=[qblk, qblk, kblk, kblk, kblk],
        out_shape=[shp] * 5,
        compiler_params=_cparams(("parallel", "arbitrary")),
    )(qfull, qfull, kv, kr, kv, d_o)


def _layout(D, MW, RW, TAIL, QR, KVR):
    names = ["gate_m", "gate_r", "z_m", "z_r", "r", "k", "v", "tail", "q_a", "kv_a"]
    widths = [D, D, MW, RW, RW, RW, RW, TAIL, QR, KVR]
    offs, o = {}, 0
    for nme, w in zip(names, widths):
        assert o % w == 0, (nme, o, w)
        offs[nme] = (o, w)
        o += w
    return offs, o


def _local_grads(x, target, W, dims):
    S, D = x.shape
    hm, hr, hn, rank = dims["hm"], dims["hr"], dims["hn"], dims["rank"]
    MW, RW = hm * VDIM, hr * hn
    TAIL = W["w2cat"].shape[0]
    QR, KVR = W["mla_q_norm"].shape[1], W["mla_kv_norm"].shape[1]
    lay, d_in = _layout(D, MW, RW, TAIL, QR, KVR)
    T = 256
    scale = (NOPE + ROPE) ** -0.5
    col = lambda arr, nme: _view(arr, *lay[nme])

    pos = jnp.arange(S, dtype=F32)
    inv_freq = jnp.power(ROPE_THETA, -jnp.arange(0, ROPE, 2, dtype=F32) / ROPE)
    ang = pos[:, None] * inv_freq[None, :]
    zpad = jnp.zeros((S, LANES - ROPE), F32)
    cosx = jnp.concatenate([jnp.cos(ang), jnp.cos(ang), zpad], axis=1)
    sinx = jnp.concatenate([jnp.sin(ang), jnp.sin(ang), zpad], axis=1)
    ri, ci = jnp.arange(LANES)[:, None], jnp.arange(LANES)[None, :]
    half = ROPE // 2
    rot = (jnp.where((ri == ci - half) & (ci >= half) & (ci < ROPE), 1.0, 0.0)
           - jnp.where((ri == ci + half) & (ci < half), 1.0, 0.0)).astype(BF16)
    rot_t = rot.T
    seg = (jnp.arange(RW)[:, None] // hn == jnp.arange(LANES)[None, :]).astype(BF16)
    seg_t = seg.T

    (h,) = _rowwise(lambda xb, g: (_rms(xb, g),), [x], [W["g_pre"]], [(D, BF16)], tile=T, name="pre_norm")
    proj = _mm(h, W["w_in"], name="in_proj")

    qn, kvn = _rowwise(_f_mla_norm, [col(proj, "q_a"), col(proj, "kv_a")], [W["mla_q_norm"], W["mla_kv_norm"]],
                       [(QR, BF16), (KVR, BF16)], tile=T, name="mla_norm")
    qraw = _mm(qn, W["wq_b"], name="q_up")
    kv = _mm(kvn, W["wkv_b"], out_dtype=BF16, name="kv_up")
    kr_view = _view(proj, lay["tail"][0], LANES)
    qfull, kr = _rowwise(functools.partial(_f_rope, hm), [qraw, kr_view, cosx, sinx], [rot, rot_t],
                         [(hm * QHEAD, BF16), (LANES, BF16)], tile=T, name="rope")
    o_mla = _attention_fwd(qfull, kv, kr, hm, scale, tq=T, name="attn_fwd")

    shift_view = (proj, lay["r"][0], 3 * RW + TAIL)
    rl = _shift_lerp(shift_view, W["mu"], name="shift_fwd")
    rl_r, rl_k, rl_v = _view(rl, 0, RW), _view(rl, RW, RW), _view(rl, 2 * RW, RW)
    rl_tail = _view(rl, 3 * RW, TAIL)
    pre_params = [W["w0_f"], W["w0_b"], W["a0_f"], W["a0_b"], W["k_k"], W["k_a"], W["w2cat"], W["a2cat"], seg, seg_t]
    pre_fn = functools.partial(_f_rwkv_pre, RW)
    lw_f, lw_b, k_f, k_b, a_n, b_f, b_b = _rowwise(pre_fn, [rl_k, rl_tail], pre_params, [(RW, F32)] * 7, tile=T,
                                                    name="rwkv_pre")
    to_h = lambda t: t.reshape(S, hr, hn).transpose(1, 0, 2)
    from_h = lambda t: t.transpose(1, 0, 2).reshape(S, RW)
    r_h, v_h, a_h = to_h(rl[:, :RW]), to_h(rl[:, 2 * RW:3 * RW]), to_h(a_n)
    dirs = {}
    for tag, rev, lw, kd, bd in (("f", False, lw_f, k_f, b_f), ("b", True, lw_b, k_b, b_b)):
        ops = (r_h, to_h(lw), to_h(kd), v_h, a_h, to_h(bd))
        y_h, st = _rwkv_scan_fwd(rev, *ops, hb=dims["hb"], name="scan_fwd_" + tag)
        dirs[tag] = (rev, ops, st, from_h(y_h))
    y_f, y_b = dirs["f"][3], dirs["b"][3]

    post_fn = functools.partial(_f_post, hn)
    post_rows = [y_f, y_b, rl_r, k_f, k_b, rl_v, col(proj, "z_r"), o_mla, col(proj, "z_m")]
    post_params = [W["gn_g"], W["gn_b"], W["r_k"], seg, seg_t]
    ymg, yrg = _rowwise(post_fn, post_rows, post_params, [(MW, BF16), (RW, BF16)], tile=T, name="post")
    u_m = _mm(ymg, W["w_br_mla"], name="br_mla")
    u_r = _mm(yrg, W["w_br_rwkv"], name="br_rwkv")
    merge_rows = [u_m, u_r, col(proj, "gate_m"), col(proj, "gate_r")]
    (merged,) = _rowwise(lambda *t: (_f_merge(*t),), merge_rows, [], [(D, BF16)], tile=T, name="merge")
    out = _mm(merged, W["w_out"], name="out_proj")

    def head(ob, xb, tb, g):
        yn, vjp = jax.vjp(_rms, ob, g)
        err = xb + yn - tb
        dy = err * (1.0 / D)
        d_ob, d_g = vjp(dy)
        loss = jnp.broadcast_to(0.5 * jnp.sum(err * err) * (1.0 / D), (1, LANES))
        return dy, d_ob, loss, d_g

    dy, d_out, loss, g_g_post = _rowwise(head, [out, x, target], [W["g_post"]], [(D, F32), (D, BF16)],
                                         [(1, LANES), (1, D)], tile=T, name="head")
    d_merged = _mm(d_out, W["w_out"], tb=True, name="d_merged")
    g_w_out = _mm(merged, d_out, ta=True, out_dtype=BF16, name="g_w_out")

    def merge_bwd(u_m_b, u_r_b, g_m_b, g_r_b, dm):
        _, vjp = jax.vjp(_f_merge, u_m_b, u_r_b, g_m_b, g_r_b)
        return vjp(dm)

    d_u_m, d_u_r, d_gate_m, d_gate_r = _rowwise(merge_bwd, merge_rows + [d_merged], [], [(D, BF16)] * 4, tile=T,
                                                name="merge_bwd")
    d_ymg = _mm(d_u_m, W["w_br_mla"], tb=True, name="d_ymg")
    d_yrg = _mm(d_u_r, W["w_br_rwkv"], tb=True, name="d_yrg")
    g_w_br_mla = _mm(ymg, d_u_m, ta=True, out_dtype=BF16, name="g_w_br_mla")
    g_w_br_rwkv = _mm(yrg, d_u_r, ta=True, out_dtype=BF16, name="g_w_br_rwkv")

    def post_bwd(*args):
        nr = len(post_rows)
        prim, dm, dr = args[:nr] + args[nr + 2:], args[nr], args[nr + 1]
        _, vjp = jax.vjp(post_fn, *prim)
        g = vjp((dm, dr))
        return g[0], g[2], g[3], g[5], g[6], g[7], g[8], g[9], g[10], g[11]

    (d_y, d_r_bonus, d_k_bonus, d_v_bonus, d_z_r, d_o, d_z_m, g_gn_g, g_gn_b, g_r_k) = _rowwise(
        post_bwd, post_rows + [d_ymg, d_yrg], post_params,
        [(RW, F32), (RW, F32), (RW, F32), (RW, F32), (RW, BF16), (MW, F32), (MW, BF16)],
        [(1, RW)] * 3, tile=T // 2, name="post_bwd")

    d_y_h = to_h(d_y)
    dsc = {}
    for tag in ("f", "b"):
        rev, ops, st, _ = dirs[tag]
        g = _rwkv_scan_bwd(rev, *ops, st, d_y_h, hb=dims["hb"], name="scan_bwd_" + tag)
        dsc[tag] = [from_h(t) for t in g]

    d_qn, d_qr, d_kn, d_v_att, d_kr_h = _attention_bwd(qfull, kv, kr, d_o, hm, scale, tq=T, name="attn_bwd")

    def rope_bwd(qraw_b, kr_in, cos_b, sin_b, dqn_b, dqr_b, dkn_b, dv_b, dkrh_b, rot_b, rot_t_b):
        _, vjp = jax.vjp(lambda q_, k_: _f_rope(hm, q_, k_, cos_b, sin_b, rot_b, rot_t_b), qraw_b, kr_in)
        parts = []
        for hh in range(hm):
            parts += [dqn_b[:, hh * NOPE:(hh + 1) * NOPE], dqr_b[:, hh * NOPE:(hh + 1) * NOPE]]
        dkr = dkrh_b[:, :LANES]
        for hh in range(1, hm):
            dkr = dkr + dkrh_b[:, hh * LANES:(hh + 1) * LANES]
        d_qraw, d_kr_in = vjp((jnp.concatenate(parts, axis=1), dkr))
        return d_qraw, jnp.concatenate([dkn_b, dv_b], axis=1), d_kr_in

    d_qraw, d_kv, d_kr_in = _rowwise(rope_bwd, [qraw, kr_view, cosx, sinx, d_qn, d_qr, d_kn, d_v_att, d_kr_h],
                                     [rot, rot_t], [(hm * QHEAD, BF16), (2 * MW, BF16), (LANES, F32)], tile=T,
                                     name="rope_bwd")
    d_qnorm = _mm(d_qraw, W["wq_b"], tb=True, name="d_qn")
    d_kvnorm = _mm(d_kv, W["wkv_b"], tb=True, name="d_kvn")
    g_wq_b = _mm(qn, d_qraw, ta=True, out_dtype=BF16, name="g_wq_b")
    g_wkv_b = _mm(kvn, d_kv, ta=True, out_dtype=BF16, name="g_wkv_b")

    def mla_norm_bwd(q_a, kv_a, qg, kvg, dq, dk):
        _, vjp = jax.vjp(_f_mla_norm, q_a, kv_a, qg, kvg)
        return vjp((dq, dk))

    d_q_a, d_kv_a, g_q_norm, g_kv_norm = _rowwise(
        lambda q_a, kv_a, dq, dk, qg, kvg: mla_norm_bwd(q_a, kv_a, qg, kvg, dq, dk),
        [col(proj, "q_a"), col(proj, "kv_a"), d_qnorm, d_kvnorm], [W["mla_q_norm"], W["mla_kv_norm"]],
        [(QR, BF16), (KVR, BF16)], [(1, QR), (1, KVR)], tile=T, name="mla_norm_bwd")

    def pre_bwd(k_b_, tail_b, dlwf, dlwb, dkf, dkb, dkbon, daf, dab, dbf, dbb, drf, drb, drbon, dvf, dvb, dvbon,
                dkr, *params):
        _, vjp = jax.vjp(pre_fn, k_b_, tail_b, *params[:8], params[8], params[9])
        g = vjp((dlwf, dlwb, dkf + dkbon, dkb + dkbon, daf + dab, dbf, dbb))
        d_tail = g[1] + jnp.concatenate([dkr, jnp.zeros((dkr.shape[0], TAIL - LANES), F32)], axis=1)
        d_rl = jnp.concatenate([drf + drb + drbon, g[0], dvf + dvb + dvbon, d_tail], axis=1)
        return (d_rl,) + tuple(g[2:10])

    f_, b_ = dsc["f"], dsc["b"]
    pre_bwd_rows = [rl_k, rl_tail, f_[1], b_[1], f_[2], b_[2], d_k_bonus, f_[4], b_[4], f_[5], b_[5],
                    f_[0], b_[0], d_r_bonus, f_[3], b_[3], d_v_bonus, d_kr_in]
    (d_rl, g_w0_f, g_w0_b, g_a0_f, g_a0_b, g_k_k, g_k_a, g_w2cat, g_a2cat) = _rowwise(
        pre_bwd, pre_bwd_rows, pre_params, [(3 * RW + TAIL, F32)],
        [(1, RW)] * 6 + [(TAIL, 2 * RW)] * 2, tile=T // 2, name="rwkv_pre_bwd")
    d_shift, g_mu = _shift_lerp(shift_view, W["mu"], d_rl, name="shift_bwd")

    d_proj = jnp.concatenate([d_gate_m, d_gate_r, d_z_m, d_z_r, d_shift.astype(BF16), d_q_a, d_kv_a], axis=1)
    assert d_proj.shape == (S, d_in)
    d_h = _mm(d_proj, W["w_in"], tb=True, name="d_h")
    g_w_in = _mm(h, d_proj, ta=True, out_dtype=BF16, name="g_w_in")

    def pre_norm_bwd(xb, dyb, dhb, g):
        _, vjp = jax.vjp(_rms, xb, g)
        dx, dg = vjp(dhb)
        return dyb + dx, dg

    grad_x, g_g_pre = _rowwise(pre_norm_bwd, [x, dy, d_h], [W["g_pre"]], [(D, F32)], [(1, D)], tile=T,
                               name="pre_norm_bwd")

    grads = dict(g_pre=g_g_pre, w_in=g_w_in, mla_q_norm=g_q_norm, wq_b=g_wq_b, mla_kv_norm=g_kv_norm,
                 wkv_b=g_wkv_b, mu=g_mu, w0_f=g_w0_f, w0_b=g_w0_b, a0_f=g_a0_f, a0_b=g_a0_b, k_k=g_k_k, k_a=g_k_a,
                 w2cat=g_w2cat, a2cat=g_a2cat, r_k=g_r_k, gn_g=g_gn_g, gn_b=g_gn_b, w_br_mla=g_w_br_mla,
                 w_br_rwkv=g_w_br_rwkv, w_out=g_w_out, g_post=g_g_post)
    return loss[0, 0], grad_x, grads


_MATS = ["w_in", "mla_wq_b", "mla_wkv_b", "rwkv_w2_f", "rwkv_w2_b", "rwkv_a2_f", "rwkv_a2_b", "w_br_mla",
         "w_br_rwkv", "w_out"]
_ROW_SHARDED = ("w_out",)
_VECS = ["g_pre", "mla_q_norm", "mla_kv_norm", "rwkv_mu", "rwkv_w0_f", "rwkv_w0_b", "rwkv_a0_f", "rwkv_a0_b",
         "rwkv_k_k", "rwkv_k_a", "rwkv_r_k", "rwkv_gn_g", "rwkv_gn_b", "g_post"]
_WEIGHTS = ["g_pre", "w_in", "mla_q_norm", "mla_wq_b", "mla_kv_norm", "mla_wkv_b", "rwkv_mu", "rwkv_w0_f",
            "rwkv_w2_f", "rwkv_w0_b", "rwkv_w2_b", "rwkv_a0_f", "rwkv_a2_f", "rwkv_a0_b", "rwkv_a2_b", "rwkv_k_k",
            "rwkv_k_a", "rwkv_r_k", "rwkv_gn_g", "rwkv_gn_b", "w_br_mla", "w_br_rwkv", "w_out", "g_post"]

def _exchange(srcs, *, name):
    n = len(srcs)

    def body(*refs):
        src_refs, out_refs = refs[:n], refs[n:2 * n]
        send_sems, recv_sems, local_sems = refs[2 * n:]
        x, y, c = lax.axis_index("x"), lax.axis_index("y"), lax.axis_index("c")
        me = 4 * x + 2 * y + c
        flip = lambda v, bit: (1 - v) if bit else v

        def piece(a, idx):
            return src_refs[a] if srcs[a].ndim == 2 else src_refs[a].at[idx]

        owns = [pltpu.make_async_copy(piece(a, me), out_refs[a].at[me], local_sems.at[a]) for a in range(n)]
        for cp in owns:
            cp.start()
        sends, peers = [], []
        for d in range(1, N_DEV):
            px, py, pc = flip(x, d & 4), flip(y, d & 2), flip(c, d & 1)
            pidx = 4 * px + 2 * py + pc
            peers.append(((px, py, pc), pidx))
            for a in range(n):
                cp = pltpu.make_async_remote_copy(
                    src_ref=piece(a, pidx), dst_ref=out_refs[a].at[me], send_sem=send_sems.at[d - 1, a],
                    recv_sem=recv_sems.at[d - 1, a], device_id=(px, py, pc), device_id_type=pl.DeviceIdType.MESH)
                cp.start()
                sends.append(cp)
        for d, (peer, pidx) in zip(range(1, N_DEV), peers):
            for a in range(n):
                pltpu.make_async_remote_copy(
                    src_ref=piece(a, pidx), dst_ref=out_refs[a].at[pidx], send_sem=send_sems.at[d - 1, a],
                    recv_sem=recv_sems.at[d - 1, a], device_id=peer, device_id_type=pl.DeviceIdType.MESH).wait_recv()
        for cp in sends:
            cp.wait_send()
        for cp in owns:
            cp.wait()

    return pl.pallas_call(
        body, name=name,
        out_shape=[jax.ShapeDtypeStruct((N_DEV,) + s.shape[-2:], s.dtype) for s in srcs],
        in_specs=[pl.BlockSpec(memory_space=pl.ANY)] * n, out_specs=[pl.BlockSpec(memory_space=pl.ANY)] * n,
        scratch_shapes=[pltpu.SemaphoreType.DMA((N_DEV - 1, n)), pltpu.SemaphoreType.DMA((N_DEV - 1, n)),
                        pltpu.SemaphoreType.DMA((n,))],
    )(*srcs)


def _adamw(recv, w, m, v, *, name):
    r, c = w.shape
    tr = r if r <= 256 else _pick_rows(r, 256)

    def body(g_ref, w_ref, m_ref, v_ref, go_ref, d_ref, mo_ref, vo_ref):
        g = g_ref[0].astype(F32)
        for k in range(1, N_DEV):
            g = g + g_ref[k].astype(F32)
        m_new = ADAM_B1 * m_ref[...] + (1.0 - ADAM_B1) * g
        v_new = ADAM_B2 * v_ref[...] + (1.0 - ADAM_B2) * (g * g)
        m_hat = m_new / (1.0 - ADAM_B1 ** ADAM_STEP)
        v_hat = v_new / (1.0 - ADAM_B2 ** ADAM_STEP)
        go_ref[...] = g
        d_ref[...] = -ADAM_LR * (m_hat / (jnp.sqrt(v_hat) + ADAM_EPS) + ADAM_WD * w_ref[...])
        mo_ref[...] = m_new
        vo_ref[...] = v_new

    blk = pl.BlockSpec((tr, c), lambda i: (i, 0))
    return pl.pallas_call(
        body, name=name, grid=(r // tr,),
        in_specs=[pl.BlockSpec((N_DEV, tr, c), lambda i: (0, i, 0)), blk, blk, blk], out_specs=[blk] * 4,
        out_shape=[jax.ShapeDtypeStruct((r, c), F32)] * 4, compiler_params=_cparams(("parallel",)),
    )(recv, w, m, v)


def _pick_rows(n, cap):
    for t in range(cap, 0, -BF16_ROWS):
        if n % t == 0:
            return t
    raise ValueError(f"no row tile for {n}")


def _pack(pieces, dtype, quantum):
    out = []
    for p in pieces:
        lead, n = p.shape[:-1], p.shape[-1]
        pad = (-n) % quantum
        p = p.astype(dtype)
        if pad:
            p = jnp.concatenate([p, jnp.zeros(lead + (pad,), dtype)], axis=-1)
        out.append(p)
    flat = jnp.concatenate(out, axis=-1)
    return flat.reshape(flat.shape[:-1] + (flat.shape[-1] // LANES, LANES))


def _unpack(flat, sizes, quantum):
    flat = flat.reshape(flat.shape[:-2] + (-1,))
    out, o = [], 0
    for n in sizes:
        out.append(flat[..., o:o + n])
        o += n + (-n) % quantum
    return out


def _prepare_weights(full, vec, dims):
    hm, hr, hn, rank = dims["hm"], dims["hr"], dims["hn"], dims["rank"]
    D, QR, KVR = dims["D"], dims["QR"], dims["KVR"]
    MW, RW, TAIL = hm * VDIM, hr * hn, dims["TAIL"]
    slabs = full["w_in"]
    c = slabs.shape[2]
    parts, pos = [], 0
    for orig_off, width, perm_off in sorted(dims["segs"], key=lambda t: t[2]):
        if perm_off > pos:
            parts.append(jnp.zeros((D, perm_off - pos), BF16))
        for k in range(N_DEV):
            lo, hi = max(orig_off, k * c), min(orig_off + width, (k + 1) * c)
            if lo < hi:
                parts.append(slabs[k][:, lo - k * c:hi - k * c])
        pos = perm_off + width
    if dims["d_in_perm"] > pos:
        parts.append(jnp.zeros((D, dims["d_in_perm"] - pos), BF16))
    w_in_p = jnp.concatenate(parts, axis=1)
    full = {n: (t if n == "w_in" else t.reshape(-1, t.shape[2]) if n in _ROW_SHARDED
                else t.transpose(1, 0, 2).reshape(t.shape[1], -1)) for n, t in full.items()}
    wq = full["mla_wq_b"].reshape(QR, hm, NOPE + ROPE)
    wq = jnp.concatenate([wq, jnp.zeros((QR, hm, QHEAD - NOPE - ROPE), BF16)], axis=2).reshape(QR, hm * QHEAD)
    wkv = full["mla_wkv_b"].reshape(KVR, hm, 2, NOPE).transpose(0, 2, 1, 3).reshape(KVR, 2 * hm * NOPE)
    z = lambda rows: jnp.zeros((rows, RW), F32)
    f = lambda nme: full[nme].astype(F32)
    w2cat = jnp.concatenate([
        jnp.concatenate([z(ROPE), f("rwkv_w2_f"), z(TAIL - ROPE - rank)], axis=0),
        jnp.concatenate([z(ROPE + rank), f("rwkv_w2_b"), z(TAIL - ROPE - 2 * rank)], axis=0)], axis=1)
    a2cat = jnp.concatenate([
        jnp.concatenate([z(ROPE + 2 * rank), f("rwkv_a2_f"), z(TAIL - ROPE - 3 * rank)], axis=0),
        jnp.concatenate([z(ROPE + 3 * rank), f("rwkv_a2_b"), z(TAIL - ROPE - 4 * rank)], axis=0)], axis=1)
    mu = vec["rwkv_mu"]
    mu_p = jnp.concatenate([mu[:3 * RW], jnp.zeros((ROPE,), F32), mu[3 * RW:],
                            jnp.zeros((TAIL - ROPE - 4 * rank,), F32)])
    row = lambda t: t.reshape(1, -1)
    return dict(
        w_in=w_in_p, wq_b=wq, wkv_b=wkv, w2cat=w2cat, a2cat=a2cat, mu=row(mu_p),
        w_br_mla=full["w_br_mla"], w_br_rwkv=full["w_br_rwkv"], w_out=full["w_out"],
        g_pre=row(vec["g_pre"]), g_post=row(vec["g_post"]), mla_q_norm=row(vec["mla_q_norm"]),
        mla_kv_norm=row(vec["mla_kv_norm"]), w0_f=row(vec["rwkv_w0_f"]), w0_b=row(vec["rwkv_w0_b"]),
        a0_f=row(vec["rwkv_a0_f"]), a0_b=row(vec["rwkv_a0_b"]), k_k=row(vec["rwkv_k_k"]), k_a=row(vec["rwkv_k_a"]),
        r_k=row(vec["rwkv_r_k"]), gn_g=row(vec["rwkv_gn_g"]), gn_b=row(vec["rwkv_gn_b"]))


def _restore_grads(g, dims):
    hm, hr, hn, rank = dims["hm"], dims["hr"], dims["hn"], dims["rank"]
    D, QR, KVR = dims["D"], dims["QR"], dims["KVR"]
    MW, RW, TAIL = hm * VDIM, hr * hn, dims["TAIL"]
    lay, _ = _layout(D, MW, RW, TAIL, QR, KVR)
    gw = g["w_in"]
    c = dims["d_in"] // N_DEV
    slabs = []
    for k in range(N_DEV):
        parts = []
        for orig_off, width, perm_off in sorted(dims["segs"]):
            lo_, hi_ = max(orig_off, k * c), min(orig_off + width, (k + 1) * c)
            if lo_ < hi_:
                parts.append(gw[:, perm_off + lo_ - orig_off:perm_off + hi_ - orig_off])
        slabs.append(jnp.concatenate(parts, axis=1))
    w_in = jnp.stack(slabs)
    wq = g["wq_b"].reshape(QR, hm, QHEAD)[:, :, :NOPE + ROPE].reshape(QR, hm * (NOPE + ROPE))
    wkv = g["wkv_b"].reshape(KVR, 2, hm, NOPE).transpose(0, 2, 1, 3).reshape(KVR, 2 * hm * NOPE)
    lo = lambda t, i, half: t[ROPE + i * rank:ROPE + (i + 1) * rank, half * RW:(half + 1) * RW].astype(BF16)
    cols = lambda t: t.reshape(t.shape[0], N_DEV, -1).transpose(1, 0, 2)
    mu = g["mu"][0]
    out = dict(
        w_in=w_in, mla_wq_b=cols(wq), mla_wkv_b=cols(wkv), rwkv_w2_f=cols(lo(g["w2cat"], 0, 0)),
        rwkv_w2_b=cols(lo(g["w2cat"], 1, 1)), rwkv_a2_f=cols(lo(g["a2cat"], 2, 0)),
        rwkv_a2_b=cols(lo(g["a2cat"], 3, 1)), w_br_mla=cols(g["w_br_mla"]), w_br_rwkv=cols(g["w_br_rwkv"]),
        w_out=g["w_out"].reshape(N_DEV, -1, g["w_out"].shape[1]),
        rwkv_mu=jnp.concatenate([mu[:3 * RW], mu[3 * RW + ROPE:3 * RW + ROPE + 4 * rank]]),
        g_pre=g["g_pre"][0], g_post=g["g_post"][0], mla_q_norm=g["mla_q_norm"][0], mla_kv_norm=g["mla_kv_norm"][0],
        rwkv_w0_f=g["w0_f"][0], rwkv_w0_b=g["w0_b"][0], rwkv_a0_f=g["a0_f"][0], rwkv_a0_b=g["a0_b"][0],
        rwkv_k_k=g["k_k"][0], rwkv_k_a=g["k_a"][0], rwkv_r_k=g["r_k"][0], rwkv_gn_g=g["gn_g"][0],
        rwkv_gn_b=g["gn_b"][0])
    return out


def _dims(inp):
    D = inp["x"].shape[-1]
    QR, KVR = inp["mla_q_norm"].shape[0], inp["mla_kv_norm"].shape[0]
    hm = inp["mla_wq_b"].shape[1] * N_DEV // (NOPE + ROPE)
    hr, hn = inp["rwkv_r_k"].shape
    rank = inp["rwkv_w2_f"].shape[0]
    MW, RW = hm * VDIM, hr * hn
    TAIL = -(-(ROPE + 4 * rank) // LANES) * LANES
    orig, o = {}, 0
    for nme, w in (("q_a", QR), ("kv_a", KVR), ("k_rope", ROPE), ("rkv", 3 * RW), ("lora", 4 * rank), ("z_m", MW),
                   ("z_r", RW), ("gate_m", D), ("gate_r", D)):
        orig[nme] = (o, w)
        o += w
    assert o == inp["w_in"].shape[1] * N_DEV
    lay, d_in_perm = _layout(D, MW, RW, TAIL, QR, KVR)
    perm_off = dict(q_a=lay["q_a"][0], kv_a=lay["kv_a"][0], k_rope=lay["tail"][0], rkv=lay["r"][0],
                    lora=lay["tail"][0] + ROPE, z_m=lay["z_m"][0], z_r=lay["z_r"][0], gate_m=lay["gate_m"][0],
                    gate_r=lay["gate_r"][0])
    segs = [(orig[nme][0], orig[nme][1], perm_off[nme]) for nme in orig]
    return dict(D=D, QR=QR, KVR=KVR, hm=hm, hr=hr, hn=hn, rank=rank, TAIL=TAIL, hb=min(hr, 4), segs=segs, d_in=o,
                d_in_perm=d_in_perm)


def kernel(x, g_pre, w_in, mla_q_norm, mla_wq_b, mla_kv_norm, mla_wkv_b, rwkv_mu, rwkv_w0_f, rwkv_w2_f, rwkv_w0_b, rwkv_w2_b, rwkv_a0_f, rwkv_a2_f, rwkv_a0_b, rwkv_a2_b, rwkv_k_k, rwkv_k_a, rwkv_r_k, rwkv_gn_g, rwkv_gn_b, w_br_mla, w_br_rwkv, w_out, g_post, loss_target, m_g_pre, m_w_in, m_mla_q_norm, m_mla_wq_b, m_mla_kv_norm, m_mla_wkv_b, m_rwkv_mu, m_rwkv_w0_f, m_rwkv_w2_f, m_rwkv_w0_b, m_rwkv_w2_b, m_rwkv_a0_f, m_rwkv_a2_f, m_rwkv_a0_b, m_rwkv_a2_b, m_rwkv_k_k, m_rwkv_k_a, m_rwkv_r_k, m_rwkv_gn_g, m_rwkv_gn_b, m_w_br_mla, m_w_br_rwkv, m_w_out, m_g_post, v_g_pre, v_w_in, v_mla_q_norm, v_mla_wq_b, v_mla_kv_norm, v_mla_wkv_b, v_rwkv_mu, v_rwkv_w0_f, v_rwkv_w2_f, v_rwkv_w0_b, v_rwkv_w2_b, v_rwkv_a0_f, v_rwkv_a2_f, v_rwkv_a0_b, v_rwkv_a2_b, v_rwkv_k_k, v_rwkv_k_a, v_rwkv_r_k, v_rwkv_gn_g, v_rwkv_gn_b, v_w_br_mla, v_w_br_rwkv, v_w_out, v_g_post):
    inp = dict(locals())
    dims = _dims(inp)
    slabs = _exchange([inp[n].astype(BF16) for n in _MATS], name="gather_weights")
    W = _prepare_weights(dict(zip(_MATS, slabs)), {n: inp[n] for n in _VECS}, dims)
    loss, grad_x, g = _local_grads(x[0], loss_target[0], W, dims)
    loss = lax.psum(loss, ("x", "y", "c"))
    g = _restore_grads(g, dims)

    new = {}
    recv = _exchange([g[n] for n in _MATS], name="scatter_grads")
    for n, t in zip(_MATS, recv):
        new[n] = _adamw(t, inp[n], inp["m_" + n], inp["v_" + n], name="adamw_" + n)

    vsizes = [inp[n].size for n in _VECS]
    vflat = lambda prefix, src: _pack([src[prefix + n].reshape(-1) for n in _VECS], F32, LANES * 8)
    (vrecv,) = _exchange([vflat("", g)], name="gather_vector_grads")
    vout = _adamw(vrecv, vflat("", inp), vflat("m_", inp), vflat("v_", inp), name="adamw_vectors")
    vparts = [_unpack(t, vsizes, LANES * 8) for t in vout]
    for i, n in enumerate(_VECS):
        new[n] = [vp[i].reshape(inp[n].shape) for vp in vparts]

    outs = [loss, grad_x[None]]
    for k in range(4):
        outs += [new[n][k] for n in _WEIGHTS]
    return tuple(outs)
```

```python
import functools
import math

import jax
import jax.numpy as jnp
from jax import lax
from jax.experimental import pallas as pl
from jax.experimental.pallas import tpu as pltpu

F32 = jnp.float32
BF16 = jnp.bfloat16

N_DEV = 8
LANES = 128
BF16_ROWS = 16
NOPE, ROPE, VDIM = 128, 64, 128
QHEAD = 256
ROPE_THETA = 10000.0
NORM_EPS = 1e-6
GN_EPS = 64e-5
CHUNK = 64
SUB = 16
VMEM_LIMIT = 56 * 1024 * 1024

ADAM_LR, ADAM_B1, ADAM_B2, ADAM_EPS, ADAM_WD, ADAM_STEP = 0.001, 0.9, 0.999, 1e-08, 0.01, 10


def _cparams(sem):
    return pltpu.CompilerParams(dimension_semantics=sem, vmem_limit_bytes=VMEM_LIMIT)


def _pick(n, cap):
    if n <= cap:
        return n
    for t in range(cap - cap % LANES, 0, -LANES):
        if n % t == 0:
            return t
    raise ValueError(f"no tile for {n} under {cap}")


def _mm(a, b, *, ta=False, tb=False, out_dtype=F32, name, tm_cap=1024, tn_cap=512, tk_cap=2048):
    K, M = a.shape if ta else a.shape[::-1]
    N = b.shape[0] if tb else b.shape[1]
    assert (b.shape[1] if tb else b.shape[0]) == K, (a.shape, b.shape, ta, tb)
    tm, tn, tk = _pick(M, tm_cap), _pick(N, tn_cap), _pick(K, tk_cap)
    nk = K // tk
    dn = (((0 if ta else 1,), (1 if tb else 0,)), ((), ()))

    def body(a_ref, b_ref, o_ref, acc_ref):
        k = pl.program_id(2)
        p = lax.dot_general(a_ref[...], b_ref[...], dn, preferred_element_type=F32)

        @pl.when(k == 0)
        def _():
            acc_ref[...] = p

        @pl.when(k > 0)
        def _():
            acc_ref[...] += p

        @pl.when(k == nk - 1)
        def _():
            o_ref[...] = acc_ref[...].astype(out_dtype)

    a_spec = pl.BlockSpec((tk, tm), lambda i, j, k: (k, i)) if ta else pl.BlockSpec((tm, tk), lambda i, j, k: (i, k))
    b_spec = pl.BlockSpec((tn, tk), lambda i, j, k: (j, k)) if tb else pl.BlockSpec((tk, tn), lambda i, j, k: (k, j))
    return pl.pallas_call(
        body, name=name, grid=(M // tm, N // tn, nk),
        in_specs=[a_spec, b_spec], out_specs=pl.BlockSpec((tm, tn), lambda i, j, k: (i, j)),
        out_shape=jax.ShapeDtypeStruct((M, N), out_dtype),
        scratch_shapes=[pltpu.VMEM((tm, tn), F32)],
        compiler_params=_cparams(("parallel", "parallel", "arbitrary")),
    )(a, b)


def _view(arr, off, width):
    assert off % width == 0, (off, width)
    return (arr, off // width, width)


def _rowwise(fn, rows, params, out_rows, out_accs=(), *, tile, name):
    rows = [r if isinstance(r, tuple) else (r, 0, r.shape[1]) for r in rows]
    S = rows[0][0].shape[0]
    T = min(tile, S)
    assert S % T == 0
    n_rows, n_par, n_out = len(rows), len(params), len(out_rows)

    def body(*refs):
        ins = [r[...] for r in refs[:n_rows + n_par]]
        outs = fn(*ins)
        out_refs = refs[n_rows + n_par:]
        for o_ref, val in zip(out_refs[:n_out], outs[:n_out]):
            o_ref[...] = val.astype(o_ref.dtype)
        i = pl.program_id(0)
        for o_ref, val in zip(out_refs[n_out:], outs[n_out:]):
            @pl.when(i == 0)
            def _(o_ref=o_ref, val=val):
                o_ref[...] = val

            @pl.when(i > 0)
            def _(o_ref=o_ref, val=val):
                o_ref[...] += val

    in_specs = [pl.BlockSpec((T, w), functools.partial(lambda i, cb: (i, cb), cb=cb)) for _, cb, w in rows]
    in_specs += [pl.BlockSpec(p.shape, lambda i: (0, 0)) for p in params]
    out_specs = [pl.BlockSpec((T, w), lambda i: (i, 0)) for w, _ in out_rows]
    out_specs += [pl.BlockSpec(s, lambda i: (0, 0)) for s in out_accs]
    out_shape = [jax.ShapeDtypeStruct((S, w), dt) for w, dt in out_rows]
    out_shape += [jax.ShapeDtypeStruct(s, F32) for s in out_accs]
    return pl.pallas_call(
        body, name=name, grid=(S // T,), in_specs=in_specs, out_specs=out_specs, out_shape=out_shape,
        compiler_params=_cparams(("arbitrary",)),
    )(*[r[0] for r in rows], *params)


def _split3(x):
    hi = x.astype(BF16)
    r1 = x - hi.astype(F32)
    mid = r1.astype(BF16)
    lo = (r1 - mid.astype(F32)).astype(BF16)
    return hi, mid, lo


def _mm_sel(x, sel):
    hi, mid, lo = _split3(x)
    d = lambda u: jnp.dot(u, sel, preferred_element_type=F32)
    return d(hi) + d(mid) + d(lo)


@jax.custom_vjp
def _sel(x, sel, sel_t):
    return _mm_sel(x, sel)


def _sel_fwd(x, sel, sel_t):
    return _mm_sel(x, sel), (sel, sel_t)


def _sel_bwd(res, ct):
    sel, sel_t = res
    return _mm_sel(ct, sel_t), jnp.zeros_like(sel), jnp.zeros_like(sel_t)


_sel.defvjp(_sel_fwd, _sel_bwd)


def _rms(x, g):
    return x * lax.rsqrt(jnp.mean(x * x, axis=-1, keepdims=True) + NORM_EPS) * g


def _sigmoid(x):
    return 1.0 / (1.0 + jnp.exp(-x))


def _silu(x):
    return x * _sigmoid(x)


def _softplus(x):
    return jnp.maximum(x, 0.0) + jnp.log(1.0 + jnp.exp(-jnp.abs(x)))


def _bdot(x, w):
    return jnp.dot(x.astype(BF16), w.astype(BF16), preferred_element_type=F32)


def _f_mla_norm(q_a, kv_a, qg, kvg):
    return _rms(q_a, qg), _rms(kv_a, kvg)


def _f_rope(hm, qraw, kr_in, cosx, sinx, rot, rot_t):
    def rope(t):
        return t * cosx + _sel(t, rot, rot_t) * sinx
    parts = []
    for h in range(hm):
        parts.append(qraw[:, h * QHEAD:h * QHEAD + NOPE])
        parts.append(rope(qraw[:, h * QHEAD + NOPE:(h + 1) * QHEAD]))
    return jnp.concatenate(parts, axis=1), rope(kr_in)


def _attn_block(qn, qr, kn, kr, v, scale):
    nt = (((1,), (1,)), ((), ()))
    s = lax.dot_general(qn.astype(BF16), kn.astype(BF16), nt, preferred_element_type=F32)
    s = s + lax.dot_general(qr.astype(BF16), kr.astype(BF16), nt, preferred_element_type=F32)
    s = s * scale
    p = jnp.exp(s - jnp.max(s, axis=-1, keepdims=True))
    p = p / jnp.sum(p, axis=-1, keepdims=True)
    return jnp.dot(p.astype(BF16), v.astype(BF16), preferred_element_type=F32)


def _f_rwkv_pre(rw, k, tail, w0f, w0b, a0f, a0b, k_k, k_a, w2cat, a2cat, seg, seg_t):
    zw = _bdot(jnp.tanh(tail), w2cat)
    za = _bdot(tail, a2cat)
    lw_f = -jnp.exp(-_softplus(-(w0f + zw[:, :rw])) - 0.5)
    lw_b = -jnp.exp(-_softplus(-(w0b + zw[:, rw:])) - 0.5)
    a_f = _sigmoid(a0f + za[:, :rw])
    a_b = _sigmoid(a0b + za[:, rw:])
    kk = k * k_k
    nrm = jnp.sqrt(_sel(_sel(kk * kk, seg, seg_t), seg_t, seg))
    kk = kk / jnp.maximum(nrm, 1e-12)
    k_f = k * (1.0 + (a_f - 1.0) * k_a)
    k_b = k * (1.0 + (a_b - 1.0) * k_a)
    return lw_f, lw_b, k_f, k_b, -kk, kk * a_f, kk * a_b


def _f_post(hn, y_f, y_b, r, k_f, k_b, v, z_r, o_mla, z_m, gn_g, gn_b, r_k, seg, seg_t):
    segsum = lambda t: _sel(_sel(t, seg, seg_t), seg_t, seg)
    y = y_f + y_b
    mu = segsum(y) * (1.0 / hn)
    yc = y - mu
    var = segsum(yc * yc) * (1.0 / hn)
    yn = yc * lax.rsqrt(var + GN_EPS) * gn_g + gn_b
    bonus = segsum(r * (k_f + k_b) * r_k) * v
    return o_mla * _silu(z_m), (yn + bonus) * _silu(z_r)


def _f_merge(u_m, u_r, g_m, g_r):
    return _sigmoid(g_m) * u_m + _sigmoid(g_r) * u_r


_NN = ((2,), (1,))
_NT = ((2,), (2,))
_TN = ((1,), (1,))

_SCAN_PASSES = {"cum": 3, "gram": 3, "solve": 1, "apply": 1, "state": 1}


def _hdot_raw(passes, x, y, dims):
    dn = (dims, ((0,), (0,)))
    d = lambda p, q: lax.dot_general(p, q, dn, preferred_element_type=F32)
    xh = x.astype(BF16)
    yh = y.astype(BF16)
    if passes == 1:
        return d(xh, yh)
    xl = (x - xh.astype(F32)).astype(BF16)
    yl = (y - yh.astype(F32)).astype(BF16)
    return d(xh, yh) + d(xh, yl) + d(xl, yh)


@functools.partial(jax.custom_vjp, nondiff_argnums=(2, 3))
def _hdot_p(x, y, dims, passes):
    return _hdot_raw(passes, x, y, dims)


def _hdot_fwd(x, y, dims, passes):
    return _hdot_raw(passes, x, y, dims), (x, y)


def _hdot_bwd(dims, passes, res, ct):
    x, y = res
    if dims == _NN:
        return _hdot_raw(passes, ct, y, _NT), _hdot_raw(passes, x, ct, _TN)
    if dims == _NT:
        return _hdot_raw(passes, ct, y, _NN), _hdot_raw(passes, ct, x, _TN)
    return _hdot_raw(passes, y, ct, _NT), _hdot_raw(passes, x, ct, _NN)


_hdot_p.defvjp(_hdot_fwd, _hdot_bwd)


def _hdot(x, y, dims, kind):
    return _hdot_p(x, y, dims, _SCAN_PASSES[kind])


def _tri_solve(n_mat, x, length):
    row = lax.broadcasted_iota(jnp.int32, (length, length), 0)
    col = lax.broadcasted_iota(jnp.int32, (length, length), 1)
    eye = (row == col).astype(F32)[None]
    diag_blk = ((row // SUB) == (col // SUB))[None]
    nd = jnp.where(diag_blk, n_mat, 0.0)
    no = n_mat - nd
    dinv = eye + nd
    p = nd
    for _ in range(int(math.log2(SUB)) - 1):
        p = _hdot(p, p, _NN, "solve")
        dinv = dinv + _hdot(dinv, p, _NN, "solve")
    q = _hdot(dinv, no, _NN, "solve")
    u = _hdot(dinv, x, _NN, "solve")
    levels = int(math.log2(length // SUB))
    qs = [q]
    for _ in range(levels - 1):
        qs.append(_hdot(qs[-1], qs[-1], _NN, "solve"))
    for qk in reversed(qs):
        u = u + _hdot(qk, u, _NN, "solve")
    return u


def _rwkv_chunk(rev, s0, r, lw, k, v, a, b):
    h, length, _ = r.shape
    row = lax.broadcasted_iota(jnp.int32, (length, length), 0)
    col = lax.broadcasted_iota(jnp.int32, (length, length), 1)
    incl = (row <= col) if rev else (row >= col)
    strict = (row < col) if rev else (row > col)
    t_incl = jnp.broadcast_to(incl.astype(F32)[None], (h, length, length))
    cum = _hdot(t_incl, lw, _NN, "cum")
    g = jnp.exp(cum)
    g_inv = jnp.exp(-cum)
    at = a * jnp.exp(cum - lw)
    rt = r * g
    bt = b * g_inv
    kt = k * g_inv
    a_ab = jnp.where(strict[None], _hdot(at, bt, _NT, "gram"), 0.0)
    a_ak = jnp.where(strict[None], _hdot(at, kt, _NT, "gram"), 0.0)
    a_rb = jnp.where(incl[None], _hdot(rt, bt, _NT, "gram"), 0.0)
    a_rk = jnp.where(incl[None], _hdot(rt, kt, _NT, "gram"), 0.0)
    x = _hdot(at, s0, _NT, "apply") + _hdot(a_ak, v, _NN, "apply")
    u = _tri_solve(a_ab, x, length)
    y = _hdot(rt, s0, _NT, "apply") + _hdot(a_rb, u, _NN, "apply") + _hdot(a_rk, v, _NN, "apply")
    g_last = g[:, 0:1, :] if rev else g[:, length - 1:length, :]
    s1 = (s0 + _hdot(u, bt, _TN, "state") + _hdot(v, kt, _TN, "state")) * g_last
    return y, s1


def _scan_specs(hb, n, nc, rev):
    cidx = (lambda c: nc - 1 - c) if rev else (lambda c: c)
    seq = pl.BlockSpec((hb, CHUNK, n), lambda g, c: (g, cidx(c), 0))
    st = pl.BlockSpec((1, hb, n, n), lambda g, c: (cidx(c), g, 0, 0))
    return seq, st


def _rwkv_scan_fwd(rev, r, lw, k, v, a, b, *, hb, name):
    H, S, n = r.shape
    nc = S // CHUNK
    seq, st = _scan_specs(hb, n, nc, rev)

    def body(r_ref, lw_ref, k_ref, v_ref, a_ref, b_ref, y_ref, st_ref, s_ref):
        @pl.when(pl.program_id(1) == 0)
        def _():
            s_ref[...] = jnp.zeros_like(s_ref)

        s0 = s_ref[...]
        st_ref[0] = s0
        y, s1 = _rwkv_chunk(rev, s0, r_ref[...], lw_ref[...], k_ref[...], v_ref[...], a_ref[...], b_ref[...])
        y_ref[...] = y
        s_ref[...] = s1

    return pl.pallas_call(
        body, name=name, grid=(H // hb, nc), in_specs=[seq] * 6, out_specs=[seq, st],
        out_shape=[jax.ShapeDtypeStruct((H, S, n), F32), jax.ShapeDtypeStruct((nc, H, n, n), F32)],
        scratch_shapes=[pltpu.VMEM((hb, n, n), F32)],
        compiler_params=_cparams(("parallel", "arbitrary")),
    )(r, lw, k, v, a, b)


def _rwkv_scan_bwd(rev, r, lw, k, v, a, b, states, dy, *, hb, name):
    H, S, n = r.shape
    nc = S // CHUNK
    seq, st = _scan_specs(hb, n, nc, not rev)

    def body(r_ref, lw_ref, k_ref, v_ref, a_ref, b_ref, st_ref, dy_ref, *rest):
        out_refs, ds_ref = rest[:6], rest[6]

        @pl.when(pl.program_id(1) == 0)
        def _():
            ds_ref[...] = jnp.zeros_like(ds_ref)

        _, vjp = jax.vjp(functools.partial(_rwkv_chunk, rev), st_ref[0], r_ref[...], lw_ref[...], k_ref[...],
                         v_ref[...], a_ref[...], b_ref[...])
        grads = vjp((dy_ref[...], ds_ref[...]))
        ds_ref[...] = grads[0]
        for o_ref, gval in zip(out_refs, grads[1:]):
            o_ref[...] = gval

    return pl.pallas_call(
        body, name=name, grid=(H // hb, nc), in_specs=[seq] * 6 + [st, seq], out_specs=[seq] * 6,
        out_shape=[jax.ShapeDtypeStruct((H, S, n), F32)] * 6,
        scratch_shapes=[pltpu.VMEM((hb, n, n), F32)],
        compiler_params=_cparams(("parallel", "arbitrary")),
    )(r, lw, k, v, a, b, states, dy)


def _shift_lerp(x_view, mu, d=None, *, name):
    arr, off, width = x_view
    S = arr.shape[0]
    cb = _pick(width, 256)
    assert off % cb == 0

    def cshift(t):
        rows = lax.broadcasted_iota(jnp.int32, t.shape, 0)
        prev = jnp.where(rows == 0, 0.0, pltpu.roll(t, 1, 0))
        nxt = jnp.where(rows == S - 1, 0.0, pltpu.roll(t, S - 1, 0))
        return 0.5 * (prev + nxt)

    def fwd_body(x_ref, mu_ref, o_ref):
        x = x_ref[...]
        o_ref[...] = x + mu_ref[...] * (cshift(x) - x)

    def bwd_body(x_ref, mu_ref, d_ref, dx_ref, dmu_ref):
        x, m, dd = x_ref[...], mu_ref[...], d_ref[...]
        gm = m * dd
        dx_ref[...] = dd - gm + cshift(gm)
        dmu_ref[...] = jnp.sum(dd * (cshift(x) - x), axis=0, keepdims=True)

    x_spec = pl.BlockSpec((S, cb), lambda j: (0, off // cb + j))
    blk = pl.BlockSpec((S, cb), lambda j: (0, j))
    vec = pl.BlockSpec((1, cb), lambda j: (0, j))
    if d is None:
        return pl.pallas_call(
            fwd_body, name=name, grid=(width // cb,), in_specs=[x_spec, vec], out_specs=blk,
            out_shape=jax.ShapeDtypeStruct((S, width), F32), compiler_params=_cparams(("parallel",)),
        )(arr, mu)
    return pl.pallas_call(
        bwd_body, name=name, grid=(width // cb,), in_specs=[x_spec, vec, blk], out_specs=[blk, vec],
        out_shape=[jax.ShapeDtypeStruct((S, width), F32), jax.ShapeDtypeStruct((1, width), F32)],
        compiler_params=_cparams(("parallel",)),
    )(arr, mu, d)


def _attention_fwd(qfull, kv, kr, hm, scale, *, tq, name):
    S = qfull.shape[0]

    def body(qn_ref, qr_ref, kn_ref, kr_ref, v_ref, o_ref):
        f = lambda ref: ref[...].astype(F32)
        o_ref[...] = _attn_block(f(qn_ref), f(qr_ref), f(kn_ref), f(kr_ref), f(v_ref), scale)

    return pl.pallas_call(
        body, name=name, grid=(hm, S // tq),
        in_specs=[pl.BlockSpec((tq, NOPE), lambda h, i: (i, 2 * h)),
                  pl.BlockSpec((tq, NOPE), lambda h, i: (i, 2 * h + 1)),
                  pl.BlockSpec((S, NOPE), lambda h, i: (0, h)),
                  pl.BlockSpec((S, LANES), lambda h, i: (0, 0)),
                  pl.BlockSpec((S, VDIM), lambda h, i: (0, hm + h))],
        out_specs=pl.BlockSpec((tq, VDIM), lambda h, i: (i, h)),
        out_shape=jax.ShapeDtypeStruct((S, hm * VDIM), F32),
        compiler_params=_cparams(("parallel", "parallel")),
    )(qfull, qfull, kv, kr, kv)


def _attention_bwd(qfull, kv, kr, d_o, hm, scale, *, tq, name):
    S = qfull.shape[0]

    def body(qn_ref, qr_ref, kn_ref, kr_ref, v_ref, do_ref, dqn_ref, dqr_ref, dkn_ref, dv_ref, dkr_ref):
        f = lambda ref: ref[...].astype(F32)
        _, vjp = jax.vjp(functools.partial(_attn_block, scale=scale), f(qn_ref), f(qr_ref), f(kn_ref), f(kr_ref),
                         f(v_ref))
        dqn, dqr, dkn, dkr, dv = vjp(do_ref[...])
        dqn_ref[...] = dqn
        dqr_ref[...] = dqr
        first = pl.program_id(1) == 0
        for ref, val in ((dkn_ref, dkn), (dv_ref, dv), (dkr_ref, dkr)):
            @pl.when(first)
            def _(ref=ref, val=val):
                ref[...] = val

            @pl.when(jnp.logical_not(first))
            def _(ref=ref, val=val):
                ref[...] += val

    qblk = pl.BlockSpec((tq, NOPE), lambda h, i: (i, h))
    kblk = pl.BlockSpec((S, NOPE), lambda h, i: (0, h))
    shp = jax.ShapeDtypeStruct((S, hm * NOPE), F32)
    return pl.pallas_call(
        body, name=name, grid=(hm, S // tq),
        in_specs=[pl.BlockSpec((tq, NOPE), lambda h, i: (i, 2 * h)),
                  pl.BlockSpec((tq, NOPE), lambda h, i: (i, 2 * h + 1)),
                  kblk,
                  pl.BlockSpec((S, LANES), lambda h, i: (0, 0)),
                  pl.BlockSpec((S, VDIM), lambda h, i: (0, hm + h)),
                  qblk],
        out_specs=[qblk, qblk, kblk, kblk, kblk],
        out_shape=[shp] * 5,
        compiler_params=_cparams(("parallel", "arbitrary")),
    )(qfull, qfull, kv, kr, kv, d_o)


def _layout(D, MW, RW, TAIL, QR, KVR):
    names = ["gate_m", "gate_r", "z_m", "z_r", "r", "k", "v", "tail", "q_a", "kv_a"]
    widths = [D, D, MW, RW, RW, RW, RW, TAIL, QR, KVR]
    offs, o = {}, 0
    for nme, w in zip(names, widths):
        assert o % w == 0, (nme, o, w)
        offs[nme] = (o, w)
        o += w
    return offs, o


def _local_grads(x, target, W, dims):
    S, D = x.shape
    hm, hr, hn, rank = dims["hm"], dims["hr"], dims["hn"], dims["rank"]
    MW, RW = hm * VDIM, hr * hn
    TAIL = W["w2cat"].shape[0]
    QR, KVR = W["mla_q_norm"].shape[1], W["mla_kv_norm"].shape[1]
    lay, d_in = _layout(D, MW, RW, TAIL, QR, KVR)
    T = 256
    scale = (NOPE + ROPE) ** -0.5
    col = lambda arr, nme: _view(arr, *lay[nme])

    pos = jnp.arange(S, dtype=F32)
    inv_freq = jnp.power(ROPE_THETA, -jnp.arange(0, ROPE, 2, dtype=F32) / ROPE)
    ang = pos[:, None] * inv_freq[None, :]
    zpad = jnp.zeros((S, LANES - ROPE), F32)
    cosx = jnp.concatenate([jnp.cos(ang), jnp.cos(ang), zpad], axis=1)
    sinx = jnp.concatenate([jnp.sin(ang), jnp.sin(ang), zpad], axis=1)
    ri, ci = jnp.arange(LANES)[:, None], jnp.arange(LANES)[None, :]
    half = ROPE // 2
    rot = (jnp.where((ri == ci - half) & (ci >= half) & (ci < ROPE), 1.0, 0.0)
           - jnp.where((ri == ci + half) & (ci < half), 1.0, 0.0)).astype(BF16)
    rot_t = rot.T
    seg = (jnp.arange(RW)[:, None] // hn == jnp.arange(LANES)[None, :]).astype(BF16)
    seg_t = seg.T

    (h,) = _rowwise(lambda xb, g: (_rms(xb, g),), [x], [W["g_pre"]], [(D, BF16)], tile=T, name="pre_norm")
    proj = _mm(h, W["w_in"], name="in_proj")

    qn, kvn = _rowwise(_f_mla_norm, [col(proj, "q_a"), col(proj, "kv_a")], [W["mla_q_norm"], W["mla_kv_norm"]],
                       [(QR, BF16), (KVR, BF16)], tile=T, name="mla_norm")
    qraw = _mm(qn, W["wq_b"], name="q_up")
    kv = _mm(kvn, W["wkv_b"], out_dtype=BF16, name="kv_up")
    kr_view = _view(proj, lay["tail"][0], LANES)
    qfull, kr = _rowwise(functools.partial(_f_rope, hm), [qraw, kr_view, cosx, sinx], [rot, rot_t],
                         [(hm * QHEAD, BF16), (LANES, BF16)], tile=T, name="rope")
    o_mla = _attention_fwd(qfull, kv, kr, hm, scale, tq=T, name="attn_fwd")

    shift_view = (proj, lay["r"][0], 3 * RW + TAIL)
    rl = _shift_lerp(shift_view, W["mu"], name="shift_fwd")
    rl_r, rl_k, rl_v = _view(rl, 0, RW), _view(rl, RW, RW), _view(rl, 2 * RW, RW)
    rl_tail = _view(rl, 3 * RW, TAIL)
    pre_params = [W["w0_f"], W["w0_b"], W["a0_f"], W["a0_b"], W["k_k"], W["k_a"], W["w2cat"], W["a2cat"], seg, seg_t]
    pre_fn = functools.partial(_f_rwkv_pre, RW)
    lw_f, lw_b, k_f, k_b, a_n, b_f, b_b = _rowwise(pre_fn, [rl_k, rl_tail], pre_params, [(RW, F32)] * 7, tile=T,
                                                    name="rwkv_pre")
    to_h = lambda t: t.reshape(S, hr, hn).transpose(1, 0, 2)
    from_h = lambda t: t.transpose(1, 0, 2).reshape(S, RW)
    r_h, v_h, a_h = to_h(rl[:, :RW]), to_h(rl[:, 2 * RW:3 * RW]), to_h(a_n)
    dirs = {}
    for tag, rev, lw, kd, bd in (("f", False, lw_f, k_f, b_f), ("b", True, lw_b, k_b, b_b)):
        ops = (r_h, to_h(lw), to_h(kd), v_h, a_h, to_h(bd))
        y_h, st = _rwkv_scan_fwd(rev, *ops, hb=dims["hb"], name="scan_fwd_" + tag)
        dirs[tag] = (rev, ops, st, from_h(y_h))
    y_f, y_b = dirs["f"][3], dirs["b"][3]

    post_fn = functools.partial(_f_post, hn)
    post_rows = [y_f, y_b, rl_r, k_f, k_b, rl_v, col(proj, "z_r"), o_mla, col(proj, "z_m")]
    post_params = [W["gn_g"], W["gn_b"], W["r_k"], seg, seg_t]
    ymg, yrg = _rowwise(post_fn, post_rows, post_params, [(MW, BF16), (RW, BF16)], tile=T, name="post")
    u_m = _mm(ymg, W["w_br_mla"], name="br_mla")
    u_r = _mm(yrg, W["w_br_rwkv"], name="br_rwkv")
    merge_rows = [u_m, u_r, col(proj, "gate_m"), col(proj, "gate_r")]
    (merged,) = _rowwise(lambda *t: (_f_merge(*t),), merge_rows, [], [(D, BF16)], tile=T, name="merge")
    out = _mm(merged, W["w_out"], name="out_proj")

    def head(ob, xb, tb, g):
        yn, vjp = jax.vjp(_rms, ob, g)
        err = xb + yn - tb
        dy = err * (1.0 / D)
        d_ob, d_g = vjp(dy)
        loss = jnp.broadcast_to(0.5 * jnp.sum(err * err) * (1.0 / D), (1, LANES))
        return dy, d_ob, loss, d_g

    dy, d_out, loss, g_g_post = _rowwise(head, [out, x, target], [W["g_post"]], [(D, F32), (D, BF16)],
                                         [(1, LANES), (1, D)], tile=T, name="head")
    d_merged = _mm(d_out, W["w_out"], tb=True, name="d_merged")
    g_w_out = _mm(merged, d_out, ta=True, out_dtype=BF16, name="g_w_out")

    def merge_bwd(u_m_b, u_r_b, g_m_b, g_r_b, dm):
        _, vjp = jax.vjp(_f_merge, u_m_b, u_r_b, g_m_b, g_r_b)
        return vjp(dm)

    d_u_m, d_u_r, d_gate_m, d_gate_r = _rowwise(merge_bwd, merge_rows + [d_merged], [], [(D, BF16)] * 4, tile=T,
                                                name="merge_bwd")
    d_ymg = _mm(d_u_m, W["w_br_mla"], tb=True, name="d_ymg")
    d_yrg = _mm(d_u_r, W["w_br_rwkv"], tb=True, name="d_yrg")
    g_w_br_mla = _mm(ymg, d_u_m, ta=True, out_dtype=BF16, name="g_w_br_mla")
    g_w_br_rwkv = _mm(yrg, d_u_r, ta=True, out_dtype=BF16, name="g_w_br_rwkv")

    def post_bwd(*args):
        nr = len(post_rows)
        prim, dm, dr = args[:nr] + args[nr + 2:], args[nr], args[nr + 1]
        _, vjp = jax.vjp(post_fn, *prim)
        g = vjp((dm, dr))
        return g[0], g[2], g[3], g[5], g[6], g[7], g[8], g[9], g[10], g[11]

    (d_y, d_r_bonus, d_k_bonus, d_v_bonus, d_z_r, d_o, d_z_m, g_gn_g, g_gn_b, g_r_k) = _rowwise(
        post_bwd, post_rows + [d_ymg, d_yrg], post_params,
        [(RW, F32), (RW, F32), (RW, F32), (RW, F32), (RW, BF16), (MW, F32), (MW, BF16)],
        [(1, RW)] * 3, tile=T // 2, name="post_bwd")

    d_y_h = to_h(d_y)
    dsc = {}
    for tag in ("f", "b"):
        rev, ops, st, _ = dirs[tag]
        g = _rwkv_scan_bwd(rev, *ops, st, d_y_h, hb=dims["hb"], name="scan_bwd_" + tag)
        dsc[tag] = [from_h(t) for t in g]

    d_qn, d_qr, d_kn, d_v_att, d_kr_h = _attention_bwd(qfull, kv, kr, d_o, hm, scale, tq=T, name="attn_bwd")

    def rope_bwd(qraw_b, kr_in, cos_b, sin_b, dqn_b, dqr_b, dkn_b, dv_b, dkrh_b, rot_b, rot_t_b):
        _, vjp = jax.vjp(lambda q_, k_: _f_rope(hm, q_, k_, cos_b, sin_b, rot_b, rot_t_b), qraw_b, kr_in)
        parts = []
        for hh in range(hm):
            parts += [dqn_b[:, hh * NOPE:(hh + 1) * NOPE], dqr_b[:, hh * NOPE:(hh + 1) * NOPE]]
        dkr = dkrh_b[:, :LANES]
        for hh in range(1, hm):
            dkr = dkr + dkrh_b[:, hh * LANES:(hh + 1) * LANES]
        d_qraw, d_kr_in = vjp((jnp.concatenate(parts, axis=1), dkr))
        return d_qraw, jnp.concatenate([dkn_b, dv_b], axis=1), d_kr_in

    d_qraw, d_kv, d_kr_in = _rowwise(rope_bwd, [qraw, kr_view, cosx, sinx, d_qn, d_qr, d_kn, d_v_att, d_kr_h],
                                     [rot, rot_t], [(hm * QHEAD, BF16), (2 * MW, BF16), (LANES, F32)], tile=T,
                                     name="rope_bwd")
    d_qnorm = _mm(d_qraw, W["wq_b"], tb=True, name="d_qn")
    d_kvnorm = _mm(d_kv, W["wkv_b"], tb=True, name="d_kvn")
    g_wq_b = _mm(qn, d_qraw, ta=True, out_dtype=BF16, name="g_wq_b")
    g_wkv_b = _mm(kvn, d_kv, ta=True, out_dtype=BF16, name="g_wkv_b")

    def mla_norm_bwd(q_a, kv_a, qg, kvg, dq, dk):
        _, vjp = jax.vjp(_f_mla_norm, q_a, kv_a, qg, kvg)
        return vjp((dq, dk))

    d_q_a, d_kv_a, g_q_norm, g_kv_norm = _rowwise(
        lambda q_a, kv_a, dq, dk, qg, kvg: mla_norm_bwd(q_a, kv_a, qg, kvg, dq, dk),
        [col(proj, "q_a"), col(proj, "kv_a"), d_qnorm, d_kvnorm], [W["mla_q_norm"], W["mla_kv_norm"]],
        [(QR, BF16), (KVR, BF16)], [(1, QR), (1, KVR)], tile=T, name="mla_norm_bwd")

    def pre_bwd(k_b_, tail_b, dlwf, dlwb, dkf, dkb, dkbon, daf, dab, dbf, dbb, drf, drb, drbon, dvf, dvb, dvbon,
                dkr, *params):
        _, vjp = jax.vjp(pre_fn, k_b_, tail_b, *params[:8], params[8], params[9])
        g = vjp((dlwf, dlwb, dkf + dkbon, dkb + dkbon, daf + dab, dbf, dbb))
        d_tail = g[1] + jnp.concatenate([dkr, jnp.zeros((dkr.shape[0], TAIL - LANES), F32)], axis=1)
        d_rl = jnp.concatenate([drf + drb + drbon, g[0], dvf + dvb + dvbon, d_tail], axis=1)
        return (d_rl,) + tuple(g[2:10])

    f_, b_ = dsc["f"], dsc["b"]
    pre_bwd_rows = [rl_k, rl_tail, f_[1], b_[1], f_[2], b_[2], d_k_bonus, f_[4], b_[4], f_[5], b_[5],
                    f_[0], b_[0], d_r_bonus, f_[3], b_[3], d_v_bonus, d_kr_in]
    (d_rl, g_w0_f, g_w0_b, g_a0_f, g_a0_b, g_k_k, g_k_a, g_w2cat, g_a2cat) = _rowwise(
        pre_bwd, pre_bwd_rows, pre_params, [(3 * RW + TAIL, F32)],
        [(1, RW)] * 6 + [(TAIL, 2 * RW)] * 2, tile=T // 2, name="rwkv_pre_bwd")
    d_shift, g_mu = _shift_lerp(shift_view, W["mu"], d_rl, name="shift_bwd")

    d_proj = jnp.concatenate([d_gate_m, d_gate_r, d_z_m, d_z_r, d_shift.astype(BF16), d_q_a, d_kv_a], axis=1)
    assert d_proj.shape == (S, d_in)
    d_h = _mm(d_proj, W["w_in"], tb=True, name="d_h")
    g_w_in = _mm(h, d_proj, ta=True, out_dtype=BF16, name="g_w_in")

    def pre_norm_bwd(xb, dyb, dhb, g):
        _, vjp = jax.vjp(_rms, xb, g)
        dx, dg = vjp(dhb)
        return dyb + dx, dg

    grad_x, g_g_pre = _rowwise(pre_norm_bwd, [x, dy, d_h], [W["g_pre"]], [(D, F32)], [(1, D)], tile=T,
                               name="pre_norm_bwd")

    grads = dict(g_pre=g_g_pre, w_in=g_w_in, mla_q_norm=g_q_norm, wq_b=g_wq_b, mla_kv_norm=g_kv_norm,
                 wkv_b=g_wkv_b, mu=g_mu, w0_f=g_w0_f, w0_b=g_w0_b, a0_f=g_a0_f, a0_b=g_a0_b, k_k=g_k_k, k_a=g_k_a,
                 w2cat=g_w2cat, a2cat=g_a2cat, r_k=g_r_k, gn_g=g_gn_g, gn_b=g_gn_b, w_br_mla=g_w_br_mla,
                 w_br_rwkv=g_w_br_rwkv, w_out=g_w_out, g_post=g_g_post)
    return loss[0, 0], grad_x, grads


_MATS = ["w_in", "mla_wq_b", "mla_wkv_b", "rwkv_w2_f", "rwkv_w2_b", "rwkv_a2_f", "rwkv_a2_b", "w_br_mla",
         "w_br_rwkv", "w_out"]
_ROW_SHARDED = ("w_out",)
_VECS = ["g_pre", "mla_q_norm", "mla_kv_norm", "rwkv_mu", "rwkv_w0_f", "rwkv_w0_b", "rwkv_a0_f", "rwkv_a0_b",
         "rwkv_k_k", "rwkv_k_a", "rwkv_r_k", "rwkv_gn_g", "rwkv_gn_b", "g_post"]
_WEIGHTS = ["g_pre", "w_in", "mla_q_norm", "mla_wq_b", "mla_kv_norm", "mla_wkv_b", "rwkv_mu", "rwkv_w0_f",
            "rwkv_w2_f", "rwkv_w0_b", "rwkv_w2_b", "rwkv_a0_f", "rwkv_a2_f", "rwkv_a0_b", "rwkv_a2_b", "rwkv_k_k",
            "rwkv_k_a", "rwkv_r_k", "rwkv_gn_g", "rwkv_gn_b", "w_br_mla", "w_br_rwkv", "w_out", "g_post"]

def _exchange(srcs, *, name):
    n = len(srcs)

    def body(*refs):
        src_refs, out_refs = refs[:n], refs[n:2 * n]
        send_sems, recv_sems, local_sems = refs[2 * n:]
        x, y, c = lax.axis_index("x"), lax.axis_index("y"), lax.axis_index("c")
        me = 4 * x + 2 * y + c
        flip = lambda v, bit: (1 - v) if bit else v

        def piece(a, idx):
            return src_refs[a] if srcs[a].ndim == 2 else src_refs[a].at[idx]

        owns = [pltpu.make_async_copy(piece(a, me), out_refs[a].at[me], local_sems.at[a]) for a in range(n)]
        for cp in owns:
            cp.start()
        sends, peers = [], []
        for d in range(1, N_DEV):
            px, py, pc = flip(x, d & 4), flip(y, d & 2), flip(c, d & 1)
            pidx = 4 * px + 2 * py + pc
            peers.append(((px, py, pc), pidx))
            for a in range(n):
                cp = pltpu.make_async_remote_copy(
                    src_ref=piece(a, pidx), dst_ref=out_refs[a].at[me], send_sem=send_sems.at[d - 1, a],
                    recv_sem=recv_sems.at[d - 1, a], device_id=(px, py, pc), device_id_type=pl.DeviceIdType.MESH)
                cp.start()
                sends.append(cp)
        for d, (peer, pidx) in zip(range(1, N_DEV), peers):
            for a in range(n):
                pltpu.make_async_remote_copy(
                    src_ref=piece(a, pidx), dst_ref=out_refs[a].at[pidx], send_sem=send_sems.at[d - 1, a],
                    recv_sem=recv_sems.at[d - 1, a], device_id=peer, device_id_type=pl.DeviceIdType.MESH).wait_recv()
        for cp in sends:
            cp.wait_send()
        for cp in owns:
            cp.wait()

    return pl.pallas_call(
        body, name=name,
        out_shape=[jax.ShapeDtypeStruct((N_DEV,) + s.shape[-2:], s.dtype) for s in srcs],
        in_specs=[pl.BlockSpec(memory_space=pl.ANY)] * n, out_specs=[pl.BlockSpec(memory_space=pl.ANY)] * n,
        scratch_shapes=[pltpu.SemaphoreType.DMA((N_DEV - 1, n)), pltpu.SemaphoreType.DMA((N_DEV - 1, n)),
                        pltpu.SemaphoreType.DMA((n,))],
    )(*srcs)


def _remote(src, dst, sems, key, to):
    send_sems, recv_sems = sems
    return pltpu.make_async_remote_copy(src_ref=src, dst_ref=dst, send_sem=send_sems.at[key], recv_sem=recv_sems.at[key],
                                        device_id=to, device_id_type=pl.DeviceIdType.MESH)


def _hbm_call(body, srcs, out_shapes, sem_shapes, *, name):
    n = len(srcs)
    return pl.pallas_call(
        body, name=name, out_shape=out_shapes,
        in_specs=[pl.BlockSpec(memory_space=pl.ANY)] * n,
        out_specs=[pl.BlockSpec(memory_space=pl.ANY)] * len(out_shapes),
        scratch_shapes=[pltpu.SemaphoreType.DMA(s) for s in sem_shapes],
    )(*srcs)


def _gather_two_level(srcs, *, name):
    n = len(srcs)

    def body(*refs):
        src_refs, out_refs = refs[:n], refs[n:2 * n]
        sems, local_sems = refs[2 * n:2 * n + 2], refs[2 * n + 2]
        x, y, c = lax.axis_index("x"), lax.axis_index("y"), lax.axis_index("c")
        idx = lambda px, py, pc: 4 * px + 2 * py + pc
        me, sibling = (x, y, c), (x, y, 1 - c)
        chips = [(1 - x, y), (x, 1 - y), (1 - x, 1 - y)]
        owns = [pltpu.make_async_copy(src_refs[a], out_refs[a].at[idx(*me)], local_sems.at[a]) for a in range(n)]
        for cp in owns:
            cp.start()
        sends = []
        for a in range(n):
            sends.append(_remote(src_refs[a], out_refs[a].at[idx(*me)], sems, (0, a), sibling))
            for j, chip in enumerate(chips):
                sends.append(_remote(src_refs[a], out_refs[a].at[idx(*me)], sems, (1 + j, a), (*chip, c)))
        for cp in sends:
            cp.start()
        for j, chip in enumerate(chips):
            for a in range(n):
                blk = out_refs[a].at[idx(*chip, c)]
                _remote(blk, blk, sems, (1 + j, a), me).wait_recv()
                fwd = _remote(blk, blk, sems, (4 + j, a), sibling)
                fwd.start()
                sends.append(fwd)
        for a in range(n):
            blk = out_refs[a].at[idx(*sibling)]
            _remote(blk, blk, sems, (0, a), me).wait_recv()
            for j, chip in enumerate(chips):
                blk = out_refs[a].at[idx(*chip, 1 - c)]
                _remote(blk, blk, sems, (4 + j, a), me).wait_recv()
        for cp in sends:
            cp.wait_send()
        for cp in owns:
            cp.wait()

    return _hbm_call(body, srcs, [jax.ShapeDtypeStruct((N_DEV,) + s.shape, s.dtype) for s in srcs],
                     [(7, n), (7, n), (n,)], name=name)


def _sibling_swap(srcs, *, name):
    n = len(srcs)

    def body(*refs):
        src_refs, out_refs, sems = refs[:n], refs[n:2 * n], refs[2 * n:]
        x, y, c = lax.axis_index("x"), lax.axis_index("y"), lax.axis_index("c")
        copies = [_remote(src_refs[a], out_refs[a], sems, a, (x, y, 1 - c)) for a in range(n)]
        for cp in copies:
            cp.start()
        for cp in copies:
            cp.wait()

    return _hbm_call(body, srcs, [jax.ShapeDtypeStruct(s.shape, s.dtype) for s in srcs], [(n,), (n,)], name=name)


def _chip_exchange(srcs, *, name):
    n = len(srcs)

    def body(*refs):
        src_refs, out_refs = refs[:n], refs[n:2 * n]
        sems, local_sems = refs[2 * n:2 * n + 2], refs[2 * n + 2]
        x, y, c = lax.axis_index("x"), lax.axis_index("y"), lax.axis_index("c")
        mine = 2 * x + y
        chips = [(1 - x, y), (x, 1 - y), (1 - x, 1 - y)]
        owns = [pltpu.make_async_copy(src_refs[a].at[mine], out_refs[a].at[mine], local_sems.at[a]) for a in range(n)]
        for cp in owns:
            cp.start()
        sends = [_remote(src_refs[a].at[2 * px + py], out_refs[a].at[mine], sems, (j, a), (px, py, c))
                 for j, (px, py) in enumerate(chips) for a in range(n)]
        for cp in sends:
            cp.start()
        for j, (px, py) in enumerate(chips):
            for a in range(n):
                blk = out_refs[a].at[2 * px + py]
                _remote(blk, blk, sems, (j, a), (x, y, c)).wait_recv()
        for cp in sends:
            cp.wait_send()
        for cp in owns:
            cp.wait()

    return _hbm_call(body, srcs, [jax.ShapeDtypeStruct(s.shape, s.dtype) for s in srcs], [(3, n), (3, n), (n,)],
                     name=name)


def _pair_add(a, b, *, name):
    q, r, c = a.shape
    tr = r if r <= 256 else _pick_rows(r, 256)

    def body(a_ref, b_ref, o_ref):
        o_ref[...] = (a_ref[...].astype(F32) + b_ref[...].astype(F32)).astype(BF16)

    blk = pl.BlockSpec((1, tr, c), lambda i, j: (i, j, 0))
    return pl.pallas_call(body, name=name, grid=(q, r // tr), in_specs=[blk, blk], out_specs=blk,
                          out_shape=jax.ShapeDtypeStruct(a.shape, BF16),
                          compiler_params=_cparams(("parallel", "parallel")))(a, b)


def _adamw(recv, w, m, v, *, name):
    r, c = w.shape
    n_terms = recv.shape[0]
    tr = r if r <= 256 else _pick_rows(r, 256)

    def body(g_ref, w_ref, m_ref, v_ref, go_ref, d_ref, mo_ref, vo_ref):
        g = g_ref[0].astype(F32)
        for k in range(1, n_terms):
            g = g + g_ref[k].astype(F32)
        m_new = ADAM_B1 * m_ref[...] + (1.0 - ADAM_B1) * g
        v_new = ADAM_B2 * v_ref[...] + (1.0 - ADAM_B2) * (g * g)
        m_hat = m_new / (1.0 - ADAM_B1 ** ADAM_STEP)
        v_hat = v_new / (1.0 - ADAM_B2 ** ADAM_STEP)
        go_ref[...] = g
        d_ref[...] = -ADAM_LR * (m_hat / (jnp.sqrt(v_hat) + ADAM_EPS) + ADAM_WD * w_ref[...])
        mo_ref[...] = m_new
        vo_ref[...] = v_new

    blk = pl.BlockSpec((tr, c), lambda i: (i, 0))
    return pl.pallas_call(
        body, name=name, grid=(r // tr,),
        in_specs=[pl.BlockSpec((n_terms, tr, c), lambda i: (0, i, 0)), blk, blk, blk], out_specs=[blk] * 4,
        out_shape=[jax.ShapeDtypeStruct((r, c), F32)] * 4, compiler_params=_cparams(("parallel",)),
    )(recv, w, m, v)


def _pick_rows(n, cap):
    for t in range(cap, 0, -BF16_ROWS):
        if n % t == 0:
            return t
    raise ValueError(f"no row tile for {n}")


def _pack(pieces, dtype, quantum):
    out = []
    for p in pieces:
        lead, n = p.shape[:-1], p.shape[-1]
        pad = (-n) % quantum
        p = p.astype(dtype)
        if pad:
            p = jnp.concatenate([p, jnp.zeros(lead + (pad,), dtype)], axis=-1)
        out.append(p)
    flat = jnp.concatenate(out, axis=-1)
    return flat.reshape(flat.shape[:-1] + (flat.shape[-1] // LANES, LANES))


def _unpack(flat, sizes, quantum):
    flat = flat.reshape(flat.shape[:-2] + (-1,))
    out, o = [], 0
    for n in sizes:
        out.append(flat[..., o:o + n])
        o += n + (-n) % quantum
    return out


def _prepare_weights(full, vec, dims):
    hm, hr, hn, rank = dims["hm"], dims["hr"], dims["hn"], dims["rank"]
    D, QR, KVR = dims["D"], dims["QR"], dims["KVR"]
    MW, RW, TAIL = hm * VDIM, hr * hn, dims["TAIL"]
    slabs = full["w_in"]
    c = slabs.shape[2]
    parts, pos = [], 0
    for orig_off, width, perm_off in sorted(dims["segs"], key=lambda t: t[2]):
        if perm_off > pos:
            parts.append(jnp.zeros((D, perm_off - pos), BF16))
        for k in range(N_DEV):
            lo, hi = max(orig_off, k * c), min(orig_off + width, (k + 1) * c)
            if lo < hi:
                parts.append(slabs[k][:, lo - k * c:hi - k * c])
        pos = perm_off + width
    if dims["d_in_perm"] > pos:
        parts.append(jnp.zeros((D, dims["d_in_perm"] - pos), BF16))
    w_in_p = jnp.concatenate(parts, axis=1)
    full = {n: (t if n == "w_in" else t.reshape(-1, t.shape[2]) if n in _ROW_SHARDED
                else t.transpose(1, 0, 2).reshape(t.shape[1], -1)) for n, t in full.items()}
    wq = full["mla_wq_b"].reshape(QR, hm, NOPE + ROPE)
    wq = jnp.concatenate([wq, jnp.zeros((QR, hm, QHEAD - NOPE - ROPE), BF16)], axis=2).reshape(QR, hm * QHEAD)
    wkv = full["mla_wkv_b"].reshape(KVR, hm, 2, NOPE).transpose(0, 2, 1, 3).reshape(KVR, 2 * hm * NOPE)
    z = lambda rows: jnp.zeros((rows, RW), F32)
    f = lambda nme: full[nme].astype(F32)
    w2cat = jnp.concatenate([
        jnp.concatenate([z(ROPE), f("rwkv_w2_f"), z(TAIL - ROPE - rank)], axis=0),
        jnp.concatenate([z(ROPE + rank), f("rwkv_w2_b"), z(TAIL - ROPE - 2 * rank)], axis=0)], axis=1)
    a2cat = jnp.concatenate([
        jnp.concatenate([z(ROPE + 2 * rank), f("rwkv_a2_f"), z(TAIL - ROPE - 3 * rank)], axis=0),
        jnp.concatenate([z(ROPE + 3 * rank), f("rwkv_a2_b"), z(TAIL - ROPE - 4 * rank)], axis=0)], axis=1)
    mu = vec["rwkv_mu"]
    mu_p = jnp.concatenate([mu[:3 * RW], jnp.zeros((ROPE,), F32), mu[3 * RW:],
                            jnp.zeros((TAIL - ROPE - 4 * rank,), F32)])
    row = lambda t: t.reshape(1, -1)
    return dict(
        w_in=w_in_p, wq_b=wq, wkv_b=wkv, w2cat=w2cat, a2cat=a2cat, mu=row(mu_p),
        w_br_mla=full["w_br_mla"], w_br_rwkv=full["w_br_rwkv"], w_out=full["w_out"],
        g_pre=row(vec["g_pre"]), g_post=row(vec["g_post"]), mla_q_norm=row(vec["mla_q_norm"]),
        mla_kv_norm=row(vec["mla_kv_norm"]), w0_f=row(vec["rwkv_w0_f"]), w0_b=row(vec["rwkv_w0_b"]),
        a0_f=row(vec["rwkv_a0_f"]), a0_b=row(vec["rwkv_a0_b"]), k_k=row(vec["rwkv_k_k"]), k_a=row(vec["rwkv_k_a"]),
        r_k=row(vec["rwkv_r_k"]), gn_g=row(vec["rwkv_gn_g"]), gn_b=row(vec["rwkv_gn_b"]))


def _restore_grads(g, dims):
    hm, hr, hn, rank = dims["hm"], dims["hr"], dims["hn"], dims["rank"]
    D, QR, KVR = dims["D"], dims["QR"], dims["KVR"]
    MW, RW, TAIL = hm * VDIM, hr * hn, dims["TAIL"]
    lay, _ = _layout(D, MW, RW, TAIL, QR, KVR)
    gw = g["w_in"]
    c = dims["d_in"] // N_DEV
    slabs = []
    for k in range(N_DEV):
        parts = []
        for orig_off, width, perm_off in sorted(dims["segs"]):
            lo_, hi_ = max(orig_off, k * c), min(orig_off + width, (k + 1) * c)
            if lo_ < hi_:
                parts.append(gw[:, perm_off + lo_ - orig_off:perm_off + hi_ - orig_off])
        slabs.append(jnp.concatenate(parts, axis=1))
    w_in = jnp.stack(slabs)
    wq = g["wq_b"].reshape(QR, hm, QHEAD)[:, :, :NOPE + ROPE].reshape(QR, hm * (NOPE + ROPE))
    wkv = g["wkv_b"].reshape(KVR, 2, hm, NOPE).transpose(0, 2, 1, 3).reshape(KVR, 2 * hm * NOPE)
    lo = lambda t, i, half: t[ROPE + i * rank:ROPE + (i + 1) * rank, half * RW:(half + 1) * RW].astype(BF16)
    cols = lambda t: t.reshape(t.shape[0], N_DEV, -1).transpose(1, 0, 2)
    mu = g["mu"][0]
    out = dict(
        w_in=w_in, mla_wq_b=cols(wq), mla_wkv_b=cols(wkv), rwkv_w2_f=cols(lo(g["w2cat"], 0, 0)),
        rwkv_w2_b=cols(lo(g["w2cat"], 1, 1)), rwkv_a2_f=cols(lo(g["a2cat"], 2, 0)),
        rwkv_a2_b=cols(lo(g["a2cat"], 3, 1)), w_br_mla=cols(g["w_br_mla"]), w_br_rwkv=cols(g["w_br_rwkv"]),
        w_out=g["w_out"].reshape(N_DEV, -1, g["w_out"].shape[1]),
        rwkv_mu=jnp.concatenate([mu[:3 * RW], mu[3 * RW + ROPE:3 * RW + ROPE + 4 * rank]]),
        g_pre=g["g_pre"][0], g_post=g["g_post"][0], mla_q_norm=g["mla_q_norm"][0], mla_kv_norm=g["mla_kv_norm"][0],
        rwkv_w0_f=g["w0_f"][0], rwkv_w0_b=g["w0_b"][0], rwkv_a0_f=g["a0_f"][0], rwkv_a0_b=g["a0_b"][0],
        rwkv_k_k=g["k_k"][0], rwkv_k_a=g["k_a"][0], rwkv_r_k=g["r_k"][0], rwkv_gn_g=g["gn_g"][0],
        rwkv_gn_b=g["gn_b"][0])
    return out


def _dims(inp):
    D = inp["x"].shape[-1]
    QR, KVR = inp["mla_q_norm"].shape[0], inp["mla_kv_norm"].shape[0]
    hm = inp["mla_wq_b"].shape[1] * N_DEV // (NOPE + ROPE)
    hr, hn = inp["rwkv_r_k"].shape
    rank = inp["rwkv_w2_f"].shape[0]
    MW, RW = hm * VDIM, hr * hn
    TAIL = -(-(ROPE + 4 * rank) // LANES) * LANES
    orig, o = {}, 0
    for nme, w in (("q_a", QR), ("kv_a", KVR), ("k_rope", ROPE), ("rkv", 3 * RW), ("lora", 4 * rank), ("z_m", MW),
                   ("z_r", RW), ("gate_m", D), ("gate_r", D)):
        orig[nme] = (o, w)
        o += w
    assert o == inp["w_in"].shape[1] * N_DEV
    lay, d_in_perm = _layout(D, MW, RW, TAIL, QR, KVR)
    perm_off = dict(q_a=lay["q_a"][0], kv_a=lay["kv_a"][0], k_rope=lay["tail"][0], rkv=lay["r"][0],
                    lora=lay["tail"][0] + ROPE, z_m=lay["z_m"][0], z_r=lay["z_r"][0], gate_m=lay["gate_m"][0],
                    gate_r=lay["gate_r"][0])
    segs = [(orig[nme][0], orig[nme][1], perm_off[nme]) for nme in orig]
    return dict(D=D, QR=QR, KVR=KVR, hm=hm, hr=hr, hn=hn, rank=rank, TAIL=TAIL, hb=min(hr, 16), segs=segs, d_in=o,
                d_in_perm=d_in_perm)


def kernel(x, g_pre, w_in, mla_q_norm, mla_wq_b, mla_kv_norm, mla_wkv_b, rwkv_mu, rwkv_w0_f, rwkv_w2_f, rwkv_w0_b, rwkv_w2_b, rwkv_a0_f, rwkv_a2_f, rwkv_a0_b, rwkv_a2_b, rwkv_k_k, rwkv_k_a, rwkv_r_k, rwkv_gn_g, rwkv_gn_b, w_br_mla, w_br_rwkv, w_out, g_post, loss_target, m_g_pre, m_w_in, m_mla_q_norm, m_mla_wq_b, m_mla_kv_norm, m_mla_wkv_b, m_rwkv_mu, m_rwkv_w0_f, m_rwkv_w2_f, m_rwkv_w0_b, m_rwkv_w2_b, m_rwkv_a0_f, m_rwkv_a2_f, m_rwkv_a0_b, m_rwkv_a2_b, m_rwkv_k_k, m_rwkv_k_a, m_rwkv_r_k, m_rwkv_gn_g, m_rwkv_gn_b, m_w_br_mla, m_w_br_rwkv, m_w_out, m_g_post, v_g_pre, v_w_in, v_mla_q_norm, v_mla_wq_b, v_mla_kv_norm, v_mla_wkv_b, v_rwkv_mu, v_rwkv_w0_f, v_rwkv_w2_f, v_rwkv_w0_b, v_rwkv_w2_b, v_rwkv_a0_f, v_rwkv_a2_f, v_rwkv_a0_b, v_rwkv_a2_b, v_rwkv_k_k, v_rwkv_k_a, v_rwkv_r_k, v_rwkv_gn_g, v_rwkv_gn_b, v_w_br_mla, v_w_br_rwkv, v_w_out, v_g_post):
    inp = dict(locals())
    dims = _dims(inp)
    slabs = _gather_two_level([inp[n].astype(BF16) for n in _MATS], name="gather_weights")
    W = _prepare_weights(dict(zip(_MATS, slabs)), {n: inp[n] for n in _VECS}, dims)
    loss, grad_x, g = _local_grads(x[0], loss_target[0], W, dims)
    loss = lax.psum(loss, ("x", "y", "c"))
    g = _restore_grads(g, dims)

    new = {}
    core = lax.axis_index("c")
    by_core = lambda t, cc: lax.dynamic_index_in_dim(t.reshape((4, 2) + t.shape[1:]), cc, axis=1, keepdims=False)
    keep = [by_core(g[n], core) for n in _MATS]
    got = _sibling_swap([by_core(g[n], 1 - core) for n in _MATS], name="pair_swap")
    sums = [_pair_add(a, b, name="pair_add_" + n) for n, a, b in zip(_MATS, keep, got)]
    recv = _chip_exchange(sums, name="scatter_grads")
    for n, t in zip(_MATS, recv):
        new[n] = _adamw(t, inp[n], inp["m_" + n], inp["v_" + n], name="adamw_" + n)

    vsizes = [inp[n].size for n in _VECS]
    vflat = lambda prefix, src: _pack([src[prefix + n].reshape(-1) for n in _VECS], F32, LANES * 8)
    (vrecv,) = _exchange([vflat("", g)], name="gather_vector_grads")
    vout = _adamw(vrecv, vflat("", inp), vflat("m_", inp), vflat("v_", inp), name="adamw_vectors")
    vparts = [_unpack(t, vsizes, LANES * 8) for t in vout]
    for i, n in enumerate(_VECS):
        new[n] = [vp[i].reshape(inp[n].shape) for vp in vparts]

    outs = [loss, grad_x[None]]
    for k in range(4):
        outs += [new[n][k] for n in _WEIGHTS]
    return tuple(outs)
```

```python
import functools
import math

import jax
import jax.numpy as jnp
from jax import lax
from jax.experimental import pallas as pl
from jax.experimental.pallas import tpu as pltpu

F32 = jnp.float32
BF16 = jnp.bfloat16

N_DEV = 8
LANES = 128
BF16_ROWS = 16
NOPE, ROPE, VDIM = 128, 64, 128
QHEAD = 256
ROPE_THETA = 10000.0
NORM_EPS = 1e-6
GN_EPS = 64e-5
CHUNK = 64
SUB = 16
VMEM_LIMIT = 56 * 1024 * 1024

ADAM_LR, ADAM_B1, ADAM_B2, ADAM_EPS, ADAM_WD, ADAM_STEP = 0.001, 0.9, 0.999, 1e-08, 0.01, 10


def _cparams(sem):
    return pltpu.CompilerParams(dimension_semantics=sem, vmem_limit_bytes=VMEM_LIMIT)


def _pick(n, cap):
    if n <= cap:
        return n
    for t in range(cap - cap % LANES, 0, -LANES):
        if n % t == 0:
            return t
    raise ValueError(f"no tile for {n} under {cap}")


def _mm(a, b, *, ta=False, tb=False, out_dtype=F32, name, tm_cap=1024, tn_cap=512, tk_cap=2048):
    K, M = a.shape if ta else a.shape[::-1]
    N = b.shape[0] if tb else b.shape[1]
    assert (b.shape[1] if tb else b.shape[0]) == K, (a.shape, b.shape, ta, tb)
    tm, tn, tk = _pick(M, tm_cap), _pick(N, tn_cap), _pick(K, tk_cap)
    nk = K // tk
    dn = (((0 if ta else 1,), (1 if tb else 0,)), ((), ()))

    def body(a_ref, b_ref, o_ref, acc_ref):
        k = pl.program_id(2)
        p = lax.dot_general(a_ref[...], b_ref[...], dn, preferred_element_type=F32)

        @pl.when(k == 0)
        def _():
            acc_ref[...] = p

        @pl.when(k > 0)
        def _():
            acc_ref[...] += p

        @pl.when(k == nk - 1)
        def _():
            o_ref[...] = acc_ref[...].astype(out_dtype)

    a_spec = pl.BlockSpec((tk, tm), lambda i, j, k: (k, i)) if ta else pl.BlockSpec((tm, tk), lambda i, j, k: (i, k))
    b_spec = pl.BlockSpec((tn, tk), lambda i, j, k: (j, k)) if tb else pl.BlockSpec((tk, tn), lambda i, j, k: (k, j))
    return pl.pallas_call(
        body, name=name, grid=(M // tm, N // tn, nk),
        in_specs=[a_spec, b_spec], out_specs=pl.BlockSpec((tm, tn), lambda i, j, k: (i, j)),
        out_shape=jax.ShapeDtypeStruct((M, N), out_dtype),
        scratch_shapes=[pltpu.VMEM((tm, tn), F32)],
        compiler_params=_cparams(("parallel", "parallel", "arbitrary")),
    )(a, b)


def _view(arr, off, width):
    assert off % width == 0, (off, width)
    return (arr, off // width, width)


def _rowwise(fn, rows, params, out_rows, out_accs=(), *, tile, name):
    rows = [r if isinstance(r, tuple) else (r, 0, r.shape[1]) for r in rows]
    S = rows[0][0].shape[0]
    T = min(tile, S)
    assert S % T == 0
    n_rows, n_par, n_out = len(rows), len(params), len(out_rows)

    def body(*refs):
        ins = [r[...] for r in refs[:n_rows + n_par]]
        outs = fn(*ins)
        out_refs = refs[n_rows + n_par:]
        for o_ref, val in zip(out_refs[:n_out], outs[:n_out]):
            o_ref[...] = val.astype(o_ref.dtype)
        i = pl.program_id(0)
        for o_ref, val in zip(out_refs[n_out:], outs[n_out:]):
            @pl.when(i == 0)
            def _(o_ref=o_ref, val=val):
                o_ref[...] = val

            @pl.when(i > 0)
            def _(o_ref=o_ref, val=val):
                o_ref[...] += val

    in_specs = [pl.BlockSpec((T, w), functools.partial(lambda i, cb: (i, cb), cb=cb)) for _, cb, w in rows]
    in_specs += [pl.BlockSpec(p.shape, lambda i: (0, 0)) for p in params]
    out_specs = [pl.BlockSpec((T, w), lambda i: (i, 0)) for w, _ in out_rows]
    out_specs += [pl.BlockSpec(s, lambda i: (0, 0)) for s in out_accs]
    out_shape = [jax.ShapeDtypeStruct((S, w), dt) for w, dt in out_rows]
    out_shape += [jax.ShapeDtypeStruct(s, F32) for s in out_accs]
    return pl.pallas_call(
        body, name=name, grid=(S // T,), in_specs=in_specs, out_specs=out_specs, out_shape=out_shape,
        compiler_params=_cparams(("arbitrary",)),
    )(*[r[0] for r in rows], *params)


def _split3(x):
    hi = x.astype(BF16)
    r1 = x - hi.astype(F32)
    mid = r1.astype(BF16)
    lo = (r1 - mid.astype(F32)).astype(BF16)
    return hi, mid, lo


def _mm_sel(x, sel):
    hi, mid, lo = _split3(x)
    d = lambda u: jnp.dot(u, sel, preferred_element_type=F32)
    return d(hi) + d(mid) + d(lo)


@jax.custom_vjp
def _sel(x, sel, sel_t):
    return _mm_sel(x, sel)


def _sel_fwd(x, sel, sel_t):
    return _mm_sel(x, sel), (sel, sel_t)


def _sel_bwd(res, ct):
    sel, sel_t = res
    return _mm_sel(ct, sel_t), jnp.zeros_like(sel), jnp.zeros_like(sel_t)


_sel.defvjp(_sel_fwd, _sel_bwd)


def _rms(x, g):
    return x * lax.rsqrt(jnp.mean(x * x, axis=-1, keepdims=True) + NORM_EPS) * g


def _sigmoid(x):
    return 1.0 / (1.0 + jnp.exp(-x))


def _silu(x):
    return x * _sigmoid(x)


def _softplus(x):
    return jnp.maximum(x, 0.0) + jnp.log(1.0 + jnp.exp(-jnp.abs(x)))


def _bdot(x, w):
    return jnp.dot(x.astype(BF16), w.astype(BF16), preferred_element_type=F32)


def _f_mla_norm(q_a, kv_a, qg, kvg):
    return _rms(q_a, qg), _rms(kv_a, kvg)


def _f_rope(hm, qraw, kr_in, cosx, sinx, rot, rot_t):
    def rope(t):
        return t * cosx + _sel(t, rot, rot_t) * sinx
    parts = []
    for h in range(hm):
        parts.append(qraw[:, h * QHEAD:h * QHEAD + NOPE])
        parts.append(rope(qraw[:, h * QHEAD + NOPE:(h + 1) * QHEAD]))
    return jnp.concatenate(parts, axis=1), rope(kr_in)


def _attn_block(qn, qr, kn, kr, v, scale):
    nt = (((1,), (1,)), ((), ()))
    s = lax.dot_general(qn.astype(BF16), kn.astype(BF16), nt, preferred_element_type=F32)
    s = s + lax.dot_general(qr.astype(BF16), kr.astype(BF16), nt, preferred_element_type=F32)
    s = s * scale
    p = jnp.exp(s - jnp.max(s, axis=-1, keepdims=True))
    p = p / jnp.sum(p, axis=-1, keepdims=True)
    return jnp.dot(p.astype(BF16), v.astype(BF16), preferred_element_type=F32)


def _f_rwkv_pre(rw, k, tail, w0f, w0b, a0f, a0b, k_k, k_a, w2cat, a2cat, seg, seg_t):
    zw = _bdot(jnp.tanh(tail), w2cat)
    za = _bdot(tail, a2cat)
    lw_f = -jnp.exp(-_softplus(-(w0f + zw[:, :rw])) - 0.5)
    lw_b = -jnp.exp(-_softplus(-(w0b + zw[:, rw:])) - 0.5)
    a_f = _sigmoid(a0f + za[:, :rw])
    a_b = _sigmoid(a0b + za[:, rw:])
    kk = k * k_k
    nrm = jnp.sqrt(_sel(_sel(kk * kk, seg, seg_t), seg_t, seg))
    kk = kk / jnp.maximum(nrm, 1e-12)
    k_f = k * (1.0 + (a_f - 1.0) * k_a)
    k_b = k * (1.0 + (a_b - 1.0) * k_a)
    return lw_f, lw_b, k_f, k_b, -kk, kk * a_f, kk * a_b


def _f_post(hn, y_f, y_b, r, k_f, k_b, v, z_r, o_mla, z_m, gn_g, gn_b, r_k, seg, seg_t):
    segsum = lambda t: _sel(_sel(t, seg, seg_t), seg_t, seg)
    y = y_f + y_b
    mu = segsum(y) * (1.0 / hn)
    yc = y - mu
    var = segsum(yc * yc) * (1.0 / hn)
    yn = yc * lax.rsqrt(var + GN_EPS) * gn_g + gn_b
    bonus = segsum(r * (k_f + k_b) * r_k) * v
    return o_mla * _silu(z_m), (yn + bonus) * _silu(z_r)


def _f_merge(u_m, u_r, g_m, g_r):
    return _sigmoid(g_m) * u_m + _sigmoid(g_r) * u_r


_NN = ((2,), (1,))
_NT = ((2,), (2,))
_TN = ((1,), (1,))

_SCAN_PASSES = {"cum": 3, "gram": 3, "solve": 1, "apply": 1, "state": 1}


def _hdot_raw(passes, x, y, dims):
    dn = (dims, ((0,), (0,)))
    d = lambda p, q: lax.dot_general(p, q, dn, preferred_element_type=F32)
    xh = x.astype(BF16)
    yh = y.astype(BF16)
    if passes == 1:
        return d(xh, yh)
    xl = (x - xh.astype(F32)).astype(BF16)
    yl = (y - yh.astype(F32)).astype(BF16)
    return d(xh, yh) + d(xh, yl) + d(xl, yh)


@functools.partial(jax.custom_vjp, nondiff_argnums=(2, 3))
def _hdot_p(x, y, dims, passes):
    return _hdot_raw(passes, x, y, dims)


def _hdot_fwd(x, y, dims, passes):
    return _hdot_raw(passes, x, y, dims), (x, y)


def _hdot_bwd(dims, passes, res, ct):
    x, y = res
    if dims == _NN:
        return _hdot_raw(passes, ct, y, _NT), _hdot_raw(passes, x, ct, _TN)
    if dims == _NT:
        return _hdot_raw(passes, ct, y, _NN), _hdot_raw(passes, ct, x, _TN)
    return _hdot_raw(passes, y, ct, _NT), _hdot_raw(passes, x, ct, _NN)


_hdot_p.defvjp(_hdot_fwd, _hdot_bwd)


def _hdot(x, y, dims, kind):
    return _hdot_p(x, y, dims, _SCAN_PASSES[kind])


def _tri_solve(n_mat, x, length):
    row = lax.broadcasted_iota(jnp.int32, (length, length), 0)
    col = lax.broadcasted_iota(jnp.int32, (length, length), 1)
    eye = (row == col).astype(F32)[None]
    diag_blk = ((row // SUB) == (col // SUB))[None]
    nd = jnp.where(diag_blk, n_mat, 0.0)
    no = n_mat - nd
    dinv = eye + nd
    p = nd
    for _ in range(int(math.log2(SUB)) - 1):
        p = _hdot(p, p, _NN, "solve")
        dinv = dinv + _hdot(dinv, p, _NN, "solve")
    q = _hdot(dinv, no, _NN, "solve")
    u = _hdot(dinv, x, _NN, "solve")
    levels = int(math.log2(length // SUB))
    qs = [q]
    for _ in range(levels - 1):
        qs.append(_hdot(qs[-1], qs[-1], _NN, "solve"))
    for qk in reversed(qs):
        u = u + _hdot(qk, u, _NN, "solve")
    return u


def _rwkv_chunk(rev, s0, r, lw, k, v, a, b):
    pairs, length, width = r.shape
    hn = width // 2
    row = lax.broadcasted_iota(jnp.int32, (length, length), 0)
    col = lax.broadcasted_iota(jnp.int32, (length, length), 1)
    incl = ((row <= col) if rev else (row >= col))[None]
    strict = ((row < col) if rev else (row > col))[None]
    lane = lax.broadcasted_iota(jnp.int32, (1, 1, width), 2)
    first = lane < hn
    head_mask = jnp.concatenate([jnp.broadcast_to(first.astype(F32), (pairs, 1, width)),
                                 jnp.broadcast_to(1.0 - first.astype(F32), (pairs, 1, width))], axis=0)
    twice = lambda t: jnp.concatenate([t, t], axis=0)
    pick = lambda t: jnp.where(first, t[:pairs], t[pairs:])

    t_incl = jnp.broadcast_to(incl.astype(F32), (pairs, length, length))
    cum = _hdot(t_incl, lw, _NN, "cum")
    g = jnp.exp(cum)
    g_inv = jnp.exp(-cum)
    at = a * jnp.exp(cum - lw)
    rt = r * g
    bt = b * g_inv
    kt = k * g_inv
    at2, rt2, bt2, kt2, v2 = twice(at) * head_mask, twice(rt) * head_mask, twice(bt), twice(kt), twice(v)
    a_ab = jnp.where(strict, _hdot(at2, bt2, _NT, "gram"), 0.0)
    a_ak = jnp.where(strict, _hdot(at2, kt2, _NT, "gram"), 0.0)
    a_rb = jnp.where(incl, _hdot(rt2, bt2, _NT, "gram"), 0.0)
    a_rk = jnp.where(incl, _hdot(rt2, kt2, _NT, "gram"), 0.0)
    x = _hdot(at, s0, _NT, "apply") + pick(_hdot(a_ak, v2, _NN, "apply"))
    u = pick(_tri_solve(a_ab, twice(x), length))
    y = _hdot(rt, s0, _NT, "apply") + pick(_hdot(a_rb, twice(u), _NN, "apply") + _hdot(a_rk, v2, _NN, "apply"))
    g_last = g[:, 0:1, :] if rev else g[:, length - 1:length, :]
    ri = lax.broadcasted_iota(jnp.int32, (width, width), 0)
    ci = lax.broadcasted_iota(jnp.int32, (width, width), 1)
    same_head = ((ri < hn) == (ci < hn))[None]
    upd = _hdot(u, bt, _TN, "state") + _hdot(v, kt, _TN, "state")
    s1 = (s0 + jnp.where(same_head, upd, 0.0)) * g_last
    return y, s1


def _split_pairs(x):
    return jnp.stack([x[:, p * LANES:(p + 1) * LANES] for p in range(x.shape[1] // LANES)])


def _merge_pairs(x):
    return jnp.concatenate([x[p] for p in range(x.shape[0])], axis=1)


def _scan_specs(views, rw, nc, rev):
    cidx = (lambda c: nc - 1 - c) if rev else (lambda c: c)
    seqs = [pl.BlockSpec((CHUNK, rw), functools.partial(lambda c, cb: (cidx(c), cb), cb=cb)) for _, cb, _ in views]
    plain = pl.BlockSpec((CHUNK, rw), lambda c: (cidx(c), 0))
    st = pl.BlockSpec((1, rw // LANES, LANES, LANES), lambda c: (cidx(c), 0, 0, 0))
    return seqs, plain, st


def _as_views(arrs, rw):
    return [t if isinstance(t, tuple) else (t, 0, rw) for t in arrs]


def _rwkv_scan_fwd(rev, ops, rw, *, name):
    views = _as_views(ops, rw)
    S = views[0][0].shape[0]
    nc, pairs = S // CHUNK, rw // LANES
    seqs, plain, st = _scan_specs(views, rw, nc, rev)

    def body(*refs):
        y_ref, st_ref, s_ref = refs[6:]

        @pl.when(pl.program_id(0) == 0)
        def _():
            s_ref[...] = jnp.zeros_like(s_ref)

        s0 = s_ref[...]
        st_ref[0] = s0
        y, s1 = _rwkv_chunk(rev, s0, *[_split_pairs(t[...]) for t in refs[:6]])
        y_ref[...] = _merge_pairs(y)
        s_ref[...] = s1

    return pl.pallas_call(
        body, name=name, grid=(nc,), in_specs=seqs, out_specs=[plain, st],
        out_shape=[jax.ShapeDtypeStruct((S, rw), F32), jax.ShapeDtypeStruct((nc, pairs, LANES, LANES), F32)],
        scratch_shapes=[pltpu.VMEM((pairs, LANES, LANES), F32)],
        compiler_params=_cparams(("arbitrary",)),
    )(*[t[0] for t in views])


def _rwkv_scan_bwd(rev, ops, states, dy, rw, *, name):
    views = _as_views(list(ops) + [dy], rw)
    S = views[0][0].shape[0]
    nc, pairs = S // CHUNK, rw // LANES
    seqs, plain, st = _scan_specs(views, rw, nc, not rev)

    def body(*refs):
        st_ref, out_refs, ds_ref = refs[7], refs[8:14], refs[14]

        @pl.when(pl.program_id(0) == 0)
        def _():
            ds_ref[...] = jnp.zeros_like(ds_ref)

        _, vjp = jax.vjp(functools.partial(_rwkv_chunk, rev), st_ref[0], *[_split_pairs(t[...]) for t in refs[:6]])
        grads = vjp((_split_pairs(refs[6][...]), ds_ref[...]))
        ds_ref[...] = grads[0]
        for o_ref, gval in zip(out_refs, grads[1:]):
            o_ref[...] = _merge_pairs(gval)

    return pl.pallas_call(
        body, name=name, grid=(nc,), in_specs=seqs + [st], out_specs=[plain] * 6,
        out_shape=[jax.ShapeDtypeStruct((S, rw), F32)] * 6,
        scratch_shapes=[pltpu.VMEM((pairs, LANES, LANES), F32)],
        compiler_params=_cparams(("arbitrary",)),
    )(*[t[0] for t in views], states)


def _shift_lerp(x_view, mu, d=None, *, name):
    arr, off, width = x_view
    S = arr.shape[0]
    cb = _pick(width, 256)
    assert off % cb == 0

    def cshift(t):
        rows = lax.broadcasted_iota(jnp.int32, t.shape, 0)
        prev = jnp.where(rows == 0, 0.0, pltpu.roll(t, 1, 0))
        nxt = jnp.where(rows == S - 1, 0.0, pltpu.roll(t, S - 1, 0))
        return 0.5 * (prev + nxt)

    def fwd_body(x_ref, mu_ref, o_ref):
        x = x_ref[...]
        o_ref[...] = x + mu_ref[...] * (cshift(x) - x)

    def bwd_body(x_ref, mu_ref, d_ref, dx_ref, dmu_ref):
        x, m, dd = x_ref[...], mu_ref[...], d_ref[...]
        gm = m * dd
        dx_ref[...] = dd - gm + cshift(gm)
        dmu_ref[...] = jnp.sum(dd * (cshift(x) - x), axis=0, keepdims=True)

    x_spec = pl.BlockSpec((S, cb), lambda j: (0, off // cb + j))
    blk = pl.BlockSpec((S, cb), lambda j: (0, j))
    vec = pl.BlockSpec((1, cb), lambda j: (0, j))
    if d is None:
        return pl.pallas_call(
            fwd_body, name=name, grid=(width // cb,), in_specs=[x_spec, vec], out_specs=blk,
            out_shape=jax.ShapeDtypeStruct((S, width), F32), compiler_params=_cparams(("parallel",)),
        )(arr, mu)
    return pl.pallas_call(
        bwd_body, name=name, grid=(width // cb,), in_specs=[x_spec, vec, blk], out_specs=[blk, vec],
        out_shape=[jax.ShapeDtypeStruct((S, width), F32), jax.ShapeDtypeStruct((1, width), F32)],
        compiler_params=_cparams(("parallel",)),
    )(arr, mu, d)


def _attention_fwd(qfull, kv, kr, hm, scale, *, tq, name):
    S = qfull.shape[0]

    def body(qn_ref, qr_ref, kn_ref, kr_ref, v_ref, o_ref):
        f = lambda ref: ref[...].astype(F32)
        o_ref[...] = _attn_block(f(qn_ref), f(qr_ref), f(kn_ref), f(kr_ref), f(v_ref), scale)

    return pl.pallas_call(
        body, name=name, grid=(hm, S // tq),
        in_specs=[pl.BlockSpec((tq, NOPE), lambda h, i: (i, 2 * h)),
                  pl.BlockSpec((tq, NOPE), lambda h, i: (i, 2 * h + 1)),
                  pl.BlockSpec((S, NOPE), lambda h, i: (0, h)),
                  pl.BlockSpec((S, LANES), lambda h, i: (0, 0)),
                  pl.BlockSpec((S, VDIM), lambda h, i: (0, hm + h))],
        out_specs=pl.BlockSpec((tq, VDIM), lambda h, i: (i, h)),
        out_shape=jax.ShapeDtypeStruct((S, hm * VDIM), F32),
        compiler_params=_cparams(("parallel", "parallel")),
    )(qfull, qfull, kv, kr, kv)


def _attention_bwd(qfull, kv, kr, d_o, hm, scale, *, tq, name):
    S = qfull.shape[0]

    def body(qn_ref, qr_ref, kn_ref, kr_ref, v_ref, do_ref, dqn_ref, dqr_ref, dkn_ref, dv_ref, dkr_ref):
        f = lambda ref: ref[...].astype(F32)
        _, vjp = jax.vjp(functools.partial(_attn_block, scale=scale), f(qn_ref), f(qr_ref), f(kn_ref), f(kr_ref),
                         f(v_ref))
        dqn, dqr, dkn, dkr, dv = vjp(do_ref[...])
        dqn_ref[...] = dqn
        dqr_ref[...] = dqr
        first = pl.program_id(1) == 0
        for ref, val in ((dkn_ref, dkn), (dv_ref, dv), (dkr_ref, dkr)):
            @pl.when(first)
            def _(ref=ref, val=val):
                ref[...] = val

            @pl.when(jnp.logical_not(first))
            def _(ref=ref, val=val):
                ref[...] += val

    qblk = pl.BlockSpec((tq, NOPE), lambda h, i: (i, h))
    kblk = pl.BlockSpec((S, NOPE), lambda h, i: (0, h))
    shp = jax.ShapeDtypeStruct((S, hm * NOPE), F32)
    return pl.pallas_call(
        body, name=name, grid=(hm, S // tq),
        in_specs=[pl.BlockSpec((tq, NOPE), lambda h, i: (i, 2 * h)),
                  pl.BlockSpec((tq, NOPE), lambda h, i: (i, 2 * h + 1)),
                  kblk,
                  pl.BlockSpec((S, LANES), lambda h, i: (0, 0)),
                  pl.BlockSpec((S, VDIM), lambda h, i: (0, hm + h)),
                  qblk],
        out_specs=[qblk, qblk, kblk, kblk, kblk],
        out_shape=[shp] * 5,
        compiler_params=_cparams(("parallel", "arbitrary")),
    )(qfull, qfull, kv, kr, kv, d_o)


def _layout(D, MW, RW, TAIL, QR, KVR):
    names = ["gate_m", "gate_r", "z_m", "z_r", "r", "k", "v", "tail", "q_a", "kv_a"]
    widths = [D, D, MW, RW, RW, RW, RW, TAIL, QR, KVR]
    offs, o = {}, 0
    for nme, w in zip(names, widths):
        assert o % w == 0, (nme, o, w)
        offs[nme] = (o, w)
        o += w
    return offs, o


def _local_grads(x, target, W, dims):
    S, D = x.shape
    hm, hr, hn, rank = dims["hm"], dims["hr"], dims["hn"], dims["rank"]
    MW, RW = hm * VDIM, hr * hn
    TAIL = W["w2cat"].shape[0]
    QR, KVR = W["mla_q_norm"].shape[1], W["mla_kv_norm"].shape[1]
    lay, d_in = _layout(D, MW, RW, TAIL, QR, KVR)
    T = 256
    scale = (NOPE + ROPE) ** -0.5
    col = lambda arr, nme: _view(arr, *lay[nme])

    pos = jnp.arange(S, dtype=F32)
    inv_freq = jnp.power(ROPE_THETA, -jnp.arange(0, ROPE, 2, dtype=F32) / ROPE)
    ang = pos[:, None] * inv_freq[None, :]
    zpad = jnp.zeros((S, LANES - ROPE), F32)
    cosx = jnp.concatenate([jnp.cos(ang), jnp.cos(ang), zpad], axis=1)
    sinx = jnp.concatenate([jnp.sin(ang), jnp.sin(ang), zpad], axis=1)
    ri, ci = jnp.arange(LANES)[:, None], jnp.arange(LANES)[None, :]
    half = ROPE // 2
    rot = (jnp.where((ri == ci - half) & (ci >= half) & (ci < ROPE), 1.0, 0.0)
           - jnp.where((ri == ci + half) & (ci < half), 1.0, 0.0)).astype(BF16)
    rot_t = rot.T
    seg = (jnp.arange(RW)[:, None] // hn == jnp.arange(LANES)[None, :]).astype(BF16)
    seg_t = seg.T

    (h,) = _rowwise(lambda xb, g: (_rms(xb, g),), [x], [W["g_pre"]], [(D, BF16)], tile=T, name="pre_norm")
    proj = _mm(h, W["w_in"], name="in_proj")

    qn, kvn = _rowwise(_f_mla_norm, [col(proj, "q_a"), col(proj, "kv_a")], [W["mla_q_norm"], W["mla_kv_norm"]],
                       [(QR, BF16), (KVR, BF16)], tile=T, name="mla_norm")
    qraw = _mm(qn, W["wq_b"], name="q_up")
    kv = _mm(kvn, W["wkv_b"], out_dtype=BF16, name="kv_up")
    kr_view = _view(proj, lay["tail"][0], LANES)
    qfull, kr = _rowwise(functools.partial(_f_rope, hm), [qraw, kr_view, cosx, sinx], [rot, rot_t],
                         [(hm * QHEAD, BF16), (LANES, BF16)], tile=T, name="rope")
    o_mla = _attention_fwd(qfull, kv, kr, hm, scale, tq=T, name="attn_fwd")

    shift_view = (proj, lay["r"][0], 3 * RW + TAIL)
    rl = _shift_lerp(shift_view, W["mu"], name="shift_fwd")
    rl_r, rl_k, rl_v = _view(rl, 0, RW), _view(rl, RW, RW), _view(rl, 2 * RW, RW)
    rl_tail = _view(rl, 3 * RW, TAIL)
    pre_params = [W["w0_f"], W["w0_b"], W["a0_f"], W["a0_b"], W["k_k"], W["k_a"], W["w2cat"], W["a2cat"], seg, seg_t]
    pre_fn = functools.partial(_f_rwkv_pre, RW)
    lw_f, lw_b, k_f, k_b, a_n, b_f, b_b = _rowwise(pre_fn, [rl_k, rl_tail], pre_params, [(RW, F32)] * 7, tile=T,
                                                    name="rwkv_pre")
    dirs = {}
    for tag, rev, lw, kd, bd in (("f", False, lw_f, k_f, b_f), ("b", True, lw_b, k_b, b_b)):
        ops = (rl_r, lw, kd, rl_v, a_n, bd)
        y_d, st = _rwkv_scan_fwd(rev, ops, RW, name="scan_fwd_" + tag)
        dirs[tag] = (rev, ops, st, y_d)
    y_f, y_b = dirs["f"][3], dirs["b"][3]

    post_fn = functools.partial(_f_post, hn)
    post_rows = [y_f, y_b, rl_r, k_f, k_b, rl_v, col(proj, "z_r"), o_mla, col(proj, "z_m")]
    post_params = [W["gn_g"], W["gn_b"], W["r_k"], seg, seg_t]
    ymg, yrg = _rowwise(post_fn, post_rows, post_params, [(MW, BF16), (RW, BF16)], tile=T, name="post")
    u_m = _mm(ymg, W["w_br_mla"], name="br_mla")
    u_r = _mm(yrg, W["w_br_rwkv"], name="br_rwkv")
    merge_rows = [u_m, u_r, col(proj, "gate_m"), col(proj, "gate_r")]
    (merged,) = _rowwise(lambda *t: (_f_merge(*t),), merge_rows, [], [(D, BF16)], tile=T, name="merge")
    out = _mm(merged, W["w_out"], name="out_proj")

    def head(ob, xb, tb, g):
        yn, vjp = jax.vjp(_rms, ob, g)
        err = xb + yn - tb
        dy = err * (1.0 / D)
        d_ob, d_g = vjp(dy)
        loss = jnp.broadcast_to(0.5 * jnp.sum(err * err) * (1.0 / D), (1, LANES))
        return dy, d_ob, loss, d_g

    dy, d_out, loss, g_g_post = _rowwise(head, [out, x, target], [W["g_post"]], [(D, F32), (D, BF16)],
                                         [(1, LANES), (1, D)], tile=T, name="head")
    d_merged = _mm(d_out, W["w_out"], tb=True, name="d_merged")
    g_w_out = _mm(merged, d_out, ta=True, out_dtype=BF16, name="g_w_out")

    def merge_bwd(u_m_b, u_r_b, g_m_b, g_r_b, dm):
        _, vjp = jax.vjp(_f_merge, u_m_b, u_r_b, g_m_b, g_r_b)
        return vjp(dm)

    d_u_m, d_u_r, d_gate_m, d_gate_r = _rowwise(merge_bwd, merge_rows + [d_merged], [], [(D, BF16)] * 4, tile=T,
                                                name="merge_bwd")
    d_ymg = _mm(d_u_m, W["w_br_mla"], tb=True, name="d_ymg")
    d_yrg = _mm(d_u_r, W["w_br_rwkv"], tb=True, name="d_yrg")
    g_w_br_mla = _mm(ymg, d_u_m, ta=True, out_dtype=BF16, name="g_w_br_mla")
    g_w_br_rwkv = _mm(yrg, d_u_r, ta=True, out_dtype=BF16, name="g_w_br_rwkv")

    def post_bwd(*args):
        nr = len(post_rows)
        prim, dm, dr = args[:nr] + args[nr + 2:], args[nr], args[nr + 1]
        _, vjp = jax.vjp(post_fn, *prim)
        g = vjp((dm, dr))
        return g[0], g[2], g[3], g[5], g[6], g[7], g[8], g[9], g[10], g[11]

    (d_y, d_r_bonus, d_k_bonus, d_v_bonus, d_z_r, d_o, d_z_m, g_gn_g, g_gn_b, g_r_k) = _rowwise(
        post_bwd, post_rows + [d_ymg, d_yrg], post_params,
        [(RW, F32), (RW, F32), (RW, F32), (RW, F32), (RW, BF16), (MW, F32), (MW, BF16)],
        [(1, RW)] * 3, tile=T // 2, name="post_bwd")

    dsc = {}
    for tag in ("f", "b"):
        rev, ops, st, _ = dirs[tag]
        dsc[tag] = _rwkv_scan_bwd(rev, ops, st, d_y, RW, name="scan_bwd_" + tag)

    d_qn, d_qr, d_kn, d_v_att, d_kr_h = _attention_bwd(qfull, kv, kr, d_o, hm, scale, tq=T, name="attn_bwd")

    def rope_bwd(qraw_b, kr_in, cos_b, sin_b, dqn_b, dqr_b, dkn_b, dv_b, dkrh_b, rot_b, rot_t_b):
        _, vjp = jax.vjp(lambda q_, k_: _f_rope(hm, q_, k_, cos_b, sin_b, rot_b, rot_t_b), qraw_b, kr_in)
        parts = []
        for hh in range(hm):
            parts += [dqn_b[:, hh * NOPE:(hh + 1) * NOPE], dqr_b[:, hh * NOPE:(hh + 1) * NOPE]]
        dkr = dkrh_b[:, :LANES]
        for hh in range(1, hm):
            dkr = dkr + dkrh_b[:, hh * LANES:(hh + 1) * LANES]
        d_qraw, d_kr_in = vjp((jnp.concatenate(parts, axis=1), dkr))
        return d_qraw, jnp.concatenate([dkn_b, dv_b], axis=1), d_kr_in

    d_qraw, d_kv, d_kr_in = _rowwise(rope_bwd, [qraw, kr_view, cosx, sinx, d_qn, d_qr, d_kn, d_v_att, d_kr_h],
                                     [rot, rot_t], [(hm * QHEAD, BF16), (2 * MW, BF16), (LANES, F32)], tile=T,
                                     name="rope_bwd")
    d_qnorm = _mm(d_qraw, W["wq_b"], tb=True, name="d_qn")
    d_kvnorm = _mm(d_kv, W["wkv_b"], tb=True, name="d_kvn")
    g_wq_b = _mm(qn, d_qraw, ta=True, out_dtype=BF16, name="g_wq_b")
    g_wkv_b = _mm(kvn, d_kv, ta=True, out_dtype=BF16, name="g_wkv_b")

    def mla_norm_bwd(q_a, kv_a, qg, kvg, dq, dk):
        _, vjp = jax.vjp(_f_mla_norm, q_a, kv_a, qg, kvg)
        return vjp((dq, dk))

    d_q_a, d_kv_a, g_q_norm, g_kv_norm = _rowwise(
        lambda q_a, kv_a, dq, dk, qg, kvg: mla_norm_bwd(q_a, kv_a, qg, kvg, dq, dk),
        [col(proj, "q_a"), col(proj, "kv_a"), d_qnorm, d_kvnorm], [W["mla_q_norm"], W["mla_kv_norm"]],
        [(QR, BF16), (KVR, BF16)], [(1, QR), (1, KVR)], tile=T, name="mla_norm_bwd")

    def pre_bwd(k_b_, tail_b, dlwf, dlwb, dkf, dkb, dkbon, daf, dab, dbf, dbb, drf, drb, drbon, dvf, dvb, dvbon,
                dkr, *params):
        _, vjp = jax.vjp(pre_fn, k_b_, tail_b, *params[:8], params[8], params[9])
        g = vjp((dlwf, dlwb, dkf + dkbon, dkb + dkbon, daf + dab, dbf, dbb))
        d_tail = g[1] + jnp.concatenate([dkr, jnp.zeros((dkr.shape[0], TAIL - LANES), F32)], axis=1)
        d_rl = jnp.concatenate([drf + drb + drbon, g[0], dvf + dvb + dvbon, d_tail], axis=1)
        return (d_rl,) + tuple(g[2:10])

    f_, b_ = dsc["f"], dsc["b"]
    pre_bwd_rows = [rl_k, rl_tail, f_[1], b_[1], f_[2], b_[2], d_k_bonus, f_[4], b_[4], f_[5], b_[5],
                    f_[0], b_[0], d_r_bonus, f_[3], b_[3], d_v_bonus, d_kr_in]
    (d_rl, g_w0_f, g_w0_b, g_a0_f, g_a0_b, g_k_k, g_k_a, g_w2cat, g_a2cat) = _rowwise(
        pre_bwd, pre_bwd_rows, pre_params, [(3 * RW + TAIL, F32)],
        [(1, RW)] * 6 + [(TAIL, 2 * RW)] * 2, tile=T // 2, name="rwkv_pre_bwd")
    d_shift, g_mu = _shift_lerp(shift_view, W["mu"], d_rl, name="shift_bwd")

    d_proj = jnp.concatenate([d_gate_m, d_gate_r, d_z_m, d_z_r, d_shift.astype(BF16), d_q_a, d_kv_a], axis=1)
    assert d_proj.shape == (S, d_in)
    d_h = _mm(d_proj, W["w_in"], tb=True, name="d_h")
    g_w_in = _mm(h, d_proj, ta=True, out_dtype=BF16, name="g_w_in")

    def pre_norm_bwd(xb, dyb, dhb, g):
        _, vjp = jax.vjp(_rms, xb, g)
        dx, dg = vjp(dhb)
        return dyb + dx, dg

    grad_x, g_g_pre = _rowwise(pre_norm_bwd, [x, dy, d_h], [W["g_pre"]], [(D, F32)], [(1, D)], tile=T,
                               name="pre_norm_bwd")

    grads = dict(g_pre=g_g_pre, w_in=g_w_in, mla_q_norm=g_q_norm, wq_b=g_wq_b, mla_kv_norm=g_kv_norm,
                 wkv_b=g_wkv_b, mu=g_mu, w0_f=g_w0_f, w0_b=g_w0_b, a0_f=g_a0_f, a0_b=g_a0_b, k_k=g_k_k, k_a=g_k_a,
                 w2cat=g_w2cat, a2cat=g_a2cat, r_k=g_r_k, gn_g=g_gn_g, gn_b=g_gn_b, w_br_mla=g_w_br_mla,
                 w_br_rwkv=g_w_br_rwkv, w_out=g_w_out, g_post=g_g_post)
    return loss[0, 0], grad_x, grads


_MATS = ["w_in", "mla_wq_b", "mla_wkv_b", "rwkv_w2_f", "rwkv_w2_b", "rwkv_a2_f", "rwkv_a2_b", "w_br_mla",
         "w_br_rwkv", "w_out"]
_ROW_SHARDED = ("w_out",)
_VECS = ["g_pre", "mla_q_norm", "mla_kv_norm", "rwkv_mu", "rwkv_w0_f", "rwkv_w0_b", "rwkv_a0_f", "rwkv_a0_b",
         "rwkv_k_k", "rwkv_k_a", "rwkv_r_k", "rwkv_gn_g", "rwkv_gn_b", "g_post"]
_WEIGHTS = ["g_pre", "w_in", "mla_q_norm", "mla_wq_b", "mla_kv_norm", "mla_wkv_b", "rwkv_mu", "rwkv_w0_f",
            "rwkv_w2_f", "rwkv_w0_b", "rwkv_w2_b", "rwkv_a0_f", "rwkv_a2_f", "rwkv_a0_b", "rwkv_a2_b", "rwkv_k_k",
            "rwkv_k_a", "rwkv_r_k", "rwkv_gn_g", "rwkv_gn_b", "w_br_mla", "w_br_rwkv", "w_out", "g_post"]

def _exchange(srcs, *, name):
    n = len(srcs)

    def body(*refs):
        src_refs, out_refs = refs[:n], refs[n:2 * n]
        send_sems, recv_sems, local_sems = refs[2 * n:]
        x, y, c = lax.axis_index("x"), lax.axis_index("y"), lax.axis_index("c")
        me = 4 * x + 2 * y + c
        flip = lambda v, bit: (1 - v) if bit else v

        def piece(a, idx):
            return src_refs[a] if srcs[a].ndim == 2 else src_refs[a].at[idx]

        owns = [pltpu.make_async_copy(piece(a, me), out_refs[a].at[me], local_sems.at[a]) for a in range(n)]
        for cp in owns:
            cp.start()
        sends, peers = [], []
        for d in range(1, N_DEV):
            px, py, pc = flip(x, d & 4), flip(y, d & 2), flip(c, d & 1)
            pidx = 4 * px + 2 * py + pc
            peers.append(((px, py, pc), pidx))
            for a in range(n):
                cp = pltpu.make_async_remote_copy(
                    src_ref=piece(a, pidx), dst_ref=out_refs[a].at[me], send_sem=send_sems.at[d - 1, a],
                    recv_sem=recv_sems.at[d - 1, a], device_id=(px, py, pc), device_id_type=pl.DeviceIdType.MESH)
                cp.start()
                sends.append(cp)
        for d, (peer, pidx) in zip(range(1, N_DEV), peers):
            for a in range(n):
                pltpu.make_async_remote_copy(
                    src_ref=piece(a, pidx), dst_ref=out_refs[a].at[pidx], send_sem=send_sems.at[d - 1, a],
                    recv_sem=recv_sems.at[d - 1, a], device_id=peer, device_id_type=pl.DeviceIdType.MESH).wait_recv()
        for cp in sends:
            cp.wait_send()
        for cp in owns:
            cp.wait()

    return pl.pallas_call(
        body, name=name,
        out_shape=[jax.ShapeDtypeStruct((N_DEV,) + s.shape[-2:], s.dtype) for s in srcs],
        in_specs=[pl.BlockSpec(memory_space=pl.ANY)] * n, out_specs=[pl.BlockSpec(memory_space=pl.ANY)] * n,
        scratch_shapes=[pltpu.SemaphoreType.DMA((N_DEV - 1, n)), pltpu.SemaphoreType.DMA((N_DEV - 1, n)),
                        pltpu.SemaphoreType.DMA((n,))],
    )(*srcs)


def _remote(src, dst, sems, key, to):
    send_sems, recv_sems = sems
    return pltpu.make_async_remote_copy(src_ref=src, dst_ref=dst, send_sem=send_sems.at[key], recv_sem=recv_sems.at[key],
                                        device_id=to, device_id_type=pl.DeviceIdType.MESH)


def _hbm_call(body, srcs, out_shapes, sem_shapes, *, name):
    n = len(srcs)
    return pl.pallas_call(
        body, name=name, out_shape=out_shapes,
        in_specs=[pl.BlockSpec(memory_space=pl.ANY)] * n,
        out_specs=[pl.BlockSpec(memory_space=pl.ANY)] * len(out_shapes),
        scratch_shapes=[pltpu.SemaphoreType.DMA(s) for s in sem_shapes],
    )(*srcs)


def _gather_two_level(srcs, *, name):
    n = len(srcs)

    def body(*refs):
        src_refs, out_refs = refs[:n], refs[n:2 * n]
        sems, local_sems = refs[2 * n:2 * n + 2], refs[2 * n + 2]
        x, y, c = lax.axis_index("x"), lax.axis_index("y"), lax.axis_index("c")
        idx = lambda px, py, pc: 4 * px + 2 * py + pc
        me, sibling = (x, y, c), (x, y, 1 - c)
        chips = [(1 - x, y), (x, 1 - y), (1 - x, 1 - y)]
        owns = [pltpu.make_async_copy(src_refs[a], out_refs[a].at[idx(*me)], local_sems.at[a]) for a in range(n)]
        for cp in owns:
            cp.start()
        sends = []
        for a in range(n):
            sends.append(_remote(src_refs[a], out_refs[a].at[idx(*me)], sems, (0, a), sibling))
            for j, chip in enumerate(chips):
                sends.append(_remote(src_refs[a], out_refs[a].at[idx(*me)], sems, (1 + j, a), (*chip, c)))
        for cp in sends:
            cp.start()
        for j, chip in enumerate(chips):
            for a in range(n):
                blk = out_refs[a].at[idx(*chip, c)]
                _remote(blk, blk, sems, (1 + j, a), me).wait_recv()
                fwd = _remote(blk, blk, sems, (4 + j, a), sibling)
                fwd.start()
                sends.append(fwd)
        for a in range(n):
            blk = out_refs[a].at[idx(*sibling)]
            _remote(blk, blk, sems, (0, a), me).wait_recv()
            for j, chip in enumerate(chips):
                blk = out_refs[a].at[idx(*chip, 1 - c)]
                _remote(blk, blk, sems, (4 + j, a), me).wait_recv()
        for cp in sends:
            cp.wait_send()
        for cp in owns:
            cp.wait()

    return _hbm_call(body, srcs, [jax.ShapeDtypeStruct((N_DEV,) + s.shape, s.dtype) for s in srcs],
                     [(7, n), (7, n), (n,)], name=name)


def _sibling_swap(srcs, *, name):
    n = len(srcs)

    def body(*refs):
        src_refs, out_refs, sems = refs[:n], refs[n:2 * n], refs[2 * n:]
        x, y, c = lax.axis_index("x"), lax.axis_index("y"), lax.axis_index("c")
        copies = [_remote(src_refs[a], out_refs[a], sems, a, (x, y, 1 - c)) for a in range(n)]
        for cp in copies:
            cp.start()
        for cp in copies:
            cp.wait()

    return _hbm_call(body, srcs, [jax.ShapeDtypeStruct(s.shape, s.dtype) for s in srcs], [(n,), (n,)], name=name)


def _chip_exchange(srcs, *, name):
    n = len(srcs)

    def body(*refs):
        src_refs, out_refs = refs[:n], refs[n:2 * n]
        sems, local_sems = refs[2 * n:2 * n + 2], refs[2 * n + 2]
        x, y, c = lax.axis_index("x"), lax.axis_index("y"), lax.axis_index("c")
        mine = 2 * x + y
        chips = [(1 - x, y), (x, 1 - y), (1 - x, 1 - y)]
        owns = [pltpu.make_async_copy(src_refs[a].at[mine], out_refs[a].at[mine], local_sems.at[a]) for a in range(n)]
        for cp in owns:
            cp.start()
        sends = [_remote(src_refs[a].at[2 * px + py], out_refs[a].at[mine], sems, (j, a), (px, py, c))
                 for j, (px, py) in enumerate(chips) for a in range(n)]
        for cp in sends:
            cp.start()
        for j, (px, py) in enumerate(chips):
            for a in range(n):
                blk = out_refs[a].at[2 * px + py]
                _remote(blk, blk, sems, (j, a), (x, y, c)).wait_recv()
        for cp in sends:
            cp.wait_send()
        for cp in owns:
            cp.wait()

    return _hbm_call(body, srcs, [jax.ShapeDtypeStruct(s.shape, s.dtype) for s in srcs], [(3, n), (3, n), (n,)],
                     name=name)


def _pair_add(a, b, *, name):
    q, r, c = a.shape
    tr = r if r <= 256 else _pick_rows(r, 256)

    def body(a_ref, b_ref, o_ref):
        o_ref[...] = (a_ref[...].astype(F32) + b_ref[...].astype(F32)).astype(BF16)

    blk = pl.BlockSpec((1, tr, c), lambda i, j: (i, j, 0))
    return pl.pallas_call(body, name=name, grid=(q, r // tr), in_specs=[blk, blk], out_specs=blk,
                          out_shape=jax.ShapeDtypeStruct(a.shape, BF16),
                          compiler_params=_cparams(("parallel", "parallel")))(a, b)


def _adamw(recv, w, m, v, *, name):
    r, c = w.shape
    n_terms = recv.shape[0]
    tr = r if r <= 256 else _pick_rows(r, 256)

    def body(g_ref, w_ref, m_ref, v_ref, go_ref, d_ref, mo_ref, vo_ref):
        g = g_ref[0].astype(F32)
        for k in range(1, n_terms):
            g = g + g_ref[k].astype(F32)
        m_new = ADAM_B1 * m_ref[...] + (1.0 - ADAM_B1) * g
        v_new = ADAM_B2 * v_ref[...] + (1.0 - ADAM_B2) * (g * g)
        m_hat = m_new / (1.0 - ADAM_B1 ** ADAM_STEP)
        v_hat = v_new / (1.0 - ADAM_B2 ** ADAM_STEP)
        go_ref[...] = g
        d_ref[...] = -ADAM_LR * (m_hat / (jnp.sqrt(v_hat) + ADAM_EPS) + ADAM_WD * w_ref[...])
        mo_ref[...] = m_new
        vo_ref[...] = v_new

    blk = pl.BlockSpec((tr, c), lambda i: (i, 0))
    return pl.pallas_call(
        body, name=name, grid=(r // tr,),
        in_specs=[pl.BlockSpec((n_terms, tr, c), lambda i: (0, i, 0)), blk, blk, blk], out_specs=[blk] * 4,
        out_shape=[jax.ShapeDtypeStruct((r, c), F32)] * 4, compiler_params=_cparams(("parallel",)),
    )(recv, w, m, v)


def _pick_rows(n, cap):
    for t in range(cap, 0, -BF16_ROWS):
        if n % t == 0:
            return t
    raise ValueError(f"no row tile for {n}")


def _pack(pieces, dtype, quantum):
    out = []
    for p in pieces:
        lead, n = p.shape[:-1], p.shape[-1]
        pad = (-n) % quantum
        p = p.astype(dtype)
        if pad:
            p = jnp.concatenate([p, jnp.zeros(lead + (pad,), dtype)], axis=-1)
        out.append(p)
    flat = jnp.concatenate(out, axis=-1)
    return flat.reshape(flat.shape[:-1] + (flat.shape[-1] // LANES, LANES))


def _unpack(flat, sizes, quantum):
    flat = flat.reshape(flat.shape[:-2] + (-1,))
    out, o = [], 0
    for n in sizes:
        out.append(flat[..., o:o + n])
        o += n + (-n) % quantum
    return out


def _prepare_weights(full, vec, dims):
    hm, hr, hn, rank = dims["hm"], dims["hr"], dims["hn"], dims["rank"]
    D, QR, KVR = dims["D"], dims["QR"], dims["KVR"]
    MW, RW, TAIL = hm * VDIM, hr * hn, dims["TAIL"]
    slabs = full["w_in"]
    c = slabs.shape[2]
    parts, pos = [], 0
    for orig_off, width, perm_off in sorted(dims["segs"], key=lambda t: t[2]):
        if perm_off > pos:
            parts.append(jnp.zeros((D, perm_off - pos), BF16))
        for k in range(N_DEV):
            lo, hi = max(orig_off, k * c), min(orig_off + width, (k + 1) * c)
            if lo < hi:
                parts.append(slabs[k][:, lo - k * c:hi - k * c])
        pos = perm_off + width
    if dims["d_in_perm"] > pos:
        parts.append(jnp.zeros((D, dims["d_in_perm"] - pos), BF16))
    w_in_p = jnp.concatenate(parts, axis=1)
    full = {n: (t if n == "w_in" else t.reshape(-1, t.shape[2]) if n in _ROW_SHARDED
                else t.transpose(1, 0, 2).reshape(t.shape[1], -1)) for n, t in full.items()}
    wq = full["mla_wq_b"].reshape(QR, hm, NOPE + ROPE)
    wq = jnp.concatenate([wq, jnp.zeros((QR, hm, QHEAD - NOPE - ROPE), BF16)], axis=2).reshape(QR, hm * QHEAD)
    wkv = full["mla_wkv_b"].reshape(KVR, hm, 2, NOPE).transpose(0, 2, 1, 3).reshape(KVR, 2 * hm * NOPE)
    z = lambda rows: jnp.zeros((rows, RW), F32)
    f = lambda nme: full[nme].astype(F32)
    w2cat = jnp.concatenate([
        jnp.concatenate([z(ROPE), f("rwkv_w2_f"), z(TAIL - ROPE - rank)], axis=0),
        jnp.concatenate([z(ROPE + rank), f("rwkv_w2_b"), z(TAIL - ROPE - 2 * rank)], axis=0)], axis=1)
    a2cat = jnp.concatenate([
        jnp.concatenate([z(ROPE + 2 * rank), f("rwkv_a2_f"), z(TAIL - ROPE - 3 * rank)], axis=0),
        jnp.concatenate([z(ROPE + 3 * rank), f("rwkv_a2_b"), z(TAIL - ROPE - 4 * rank)], axis=0)], axis=1)
    mu = vec["rwkv_mu"]
    mu_p = jnp.concatenate([mu[:3 * RW], jnp.zeros((ROPE,), F32), mu[3 * RW:],
                            jnp.zeros((TAIL - ROPE - 4 * rank,), F32)])
    row = lambda t: t.reshape(1, -1)
    return dict(
        w_in=w_in_p, wq_b=wq, wkv_b=wkv, w2cat=w2cat, a2cat=a2cat, mu=row(mu_p),
        w_br_mla=full["w_br_mla"], w_br_rwkv=full["w_br_rwkv"], w_out=full["w_out"],
        g_pre=row(vec["g_pre"]), g_post=row(vec["g_post"]), mla_q_norm=row(vec["mla_q_norm"]),
        mla_kv_norm=row(vec["mla_kv_norm"]), w0_f=row(vec["rwkv_w0_f"]), w0_b=row(vec["rwkv_w0_b"]),
        a0_f=row(vec["rwkv_a0_f"]), a0_b=row(vec["rwkv_a0_b"]), k_k=row(vec["rwkv_k_k"]), k_a=row(vec["rwkv_k_a"]),
        r_k=row(vec["rwkv_r_k"]), gn_g=row(vec["rwkv_gn_g"]), gn_b=row(vec["rwkv_gn_b"]))


def _restore_grads(g, dims):
    hm, hr, hn, rank = dims["hm"], dims["hr"], dims["hn"], dims["rank"]
    D, QR, KVR = dims["D"], dims["QR"], dims["KVR"]
    MW, RW, TAIL = hm * VDIM, hr * hn, dims["TAIL"]
    lay, _ = _layout(D, MW, RW, TAIL, QR, KVR)
    gw = g["w_in"]
    c = dims["d_in"] // N_DEV
    slabs = []
    for k in range(N_DEV):
        parts = []
        for orig_off, width, perm_off in sorted(dims["segs"]):
            lo_, hi_ = max(orig_off, k * c), min(orig_off + width, (k + 1) * c)
            if lo_ < hi_:
                parts.append(gw[:, perm_off + lo_ - orig_off:perm_off + hi_ - orig_off])
        slabs.append(jnp.concatenate(parts, axis=1))
    w_in = jnp.stack(slabs)
    wq = g["wq_b"].reshape(QR, hm, QHEAD)[:, :, :NOPE + ROPE].reshape(QR, hm * (NOPE + ROPE))
    wkv = g["wkv_b"].reshape(KVR, 2, hm, NOPE).transpose(0, 2, 1, 3).reshape(KVR, 2 * hm * NOPE)
    lo = lambda t, i, half: t[ROPE + i * rank:ROPE + (i + 1) * rank, half * RW:(half + 1) * RW].astype(BF16)
    cols = lambda t: t.reshape(t.shape[0], N_DEV, -1).transpose(1, 0, 2)
    mu = g["mu"][0]
    out = dict(
        w_in=w_in, mla_wq_b=cols(wq), mla_wkv_b=cols(wkv), rwkv_w2_f=cols(lo(g["w2cat"], 0, 0)),
        rwkv_w2_b=cols(lo(g["w2cat"], 1, 1)), rwkv_a2_f=cols(lo(g["a2cat"], 2, 0)),
        rwkv_a2_b=cols(lo(g["a2cat"], 3, 1)), w_br_mla=cols(g["w_br_mla"]), w_br_rwkv=cols(g["w_br_rwkv"]),
        w_out=g["w_out"].reshape(N_DEV, -1, g["w_out"].shape[1]),
        rwkv_mu=jnp.concatenate([mu[:3 * RW], mu[3 * RW + ROPE:3 * RW + ROPE + 4 * rank]]),
        g_pre=g["g_pre"][0], g_post=g["g_post"][0], mla_q_norm=g["mla_q_norm"][0], mla_kv_norm=g["mla_kv_norm"][0],
        rwkv_w0_f=g["w0_f"][0], rwkv_w0_b=g["w0_b"][0], rwkv_a0_f=g["a0_f"][0], rwkv_a0_b=g["a0_b"][0],
        rwkv_k_k=g["k_k"][0], rwkv_k_a=g["k_a"][0], rwkv_r_k=g["r_k"][0], rwkv_gn_g=g["gn_g"][0],
        rwkv_gn_b=g["gn_b"][0])
    return out


def _dims(inp):
    D = inp["x"].shape[-1]
    QR, KVR = inp["mla_q_norm"].shape[0], inp["mla_kv_norm"].shape[0]
    hm = inp["mla_wq_b"].shape[1] * N_DEV // (NOPE + ROPE)
    hr, hn = inp["rwkv_r_k"].shape
    rank = inp["rwkv_w2_f"].shape[0]
    MW, RW = hm * VDIM, hr * hn
    TAIL = -(-(ROPE + 4 * rank) // LANES) * LANES
    orig, o = {}, 0
    for nme, w in (("q_a", QR), ("kv_a", KVR), ("k_rope", ROPE), ("rkv", 3 * RW), ("lora", 4 * rank), ("z_m", MW),
                   ("z_r", RW), ("gate_m", D), ("gate_r", D)):
        orig[nme] = (o, w)
        o += w
    assert o == inp["w_in"].shape[1] * N_DEV
    lay, d_in_perm = _layout(D, MW, RW, TAIL, QR, KVR)
    perm_off = dict(q_a=lay["q_a"][0], kv_a=lay["kv_a"][0], k_rope=lay["tail"][0], rkv=lay["r"][0],
                    lora=lay["tail"][0] + ROPE, z_m=lay["z_m"][0], z_r=lay["z_r"][0], gate_m=lay["gate_m"][0],
                    gate_r=lay["gate_r"][0])
    segs = [(orig[nme][0], orig[nme][1], perm_off[nme]) for nme in orig]
    return dict(D=D, QR=QR, KVR=KVR, hm=hm, hr=hr, hn=hn, rank=rank, TAIL=TAIL, hb=min(hr, 16), segs=segs, d_in=o,
                d_in_perm=d_in_perm)


def kernel(x, g_pre, w_in, mla_q_norm, mla_wq_b, mla_kv_norm, mla_wkv_b, rwkv_mu, rwkv_w0_f, rwkv_w2_f, rwkv_w0_b, rwkv_w2_b, rwkv_a0_f, rwkv_a2_f, rwkv_a0_b, rwkv_a2_b, rwkv_k_k, rwkv_k_a, rwkv_r_k, rwkv_gn_g, rwkv_gn_b, w_br_mla, w_br_rwkv, w_out, g_post, loss_target, m_g_pre, m_w_in, m_mla_q_norm, m_mla_wq_b, m_mla_kv_norm, m_mla_wkv_b, m_rwkv_mu, m_rwkv_w0_f, m_rwkv_w2_f, m_rwkv_w0_b, m_rwkv_w2_b, m_rwkv_a0_f, m_rwkv_a2_f, m_rwkv_a0_b, m_rwkv_a2_b, m_rwkv_k_k, m_rwkv_k_a, m_rwkv_r_k, m_rwkv_gn_g, m_rwkv_gn_b, m_w_br_mla, m_w_br_rwkv, m_w_out, m_g_post, v_g_pre, v_w_in, v_mla_q_norm, v_mla_wq_b, v_mla_kv_norm, v_mla_wkv_b, v_rwkv_mu, v_rwkv_w0_f, v_rwkv_w2_f, v_rwkv_w0_b, v_rwkv_w2_b, v_rwkv_a0_f, v_rwkv_a2_f, v_rwkv_a0_b, v_rwkv_a2_b, v_rwkv_k_k, v_rwkv_k_a, v_rwkv_r_k, v_rwkv_gn_g, v_rwkv_gn_b, v_w_br_mla, v_w_br_rwkv, v_w_out, v_g_post):
    inp = dict(locals())
    dims = _dims(inp)
    slabs = _gather_two_level([inp[n].astype(BF16) for n in _MATS], name="gather_weights")
    W = _prepare_weights(dict(zip(_MATS, slabs)), {n: inp[n] for n in _VECS}, dims)
    loss, grad_x, g = _local_grads(x[0], loss_target[0], W, dims)
    loss = lax.psum(loss, ("x", "y", "c"))
    g = _restore_grads(g, dims)

    new = {}
    core = lax.axis_index("c")
    by_core = lambda t, cc: lax.dynamic_index_in_dim(t.reshape((4, 2) + t.shape[1:]), cc, axis=1, keepdims=False)
    keep = [by_core(g[n], core) for n in _MATS]
    got = _sibling_swap([by_core(g[n], 1 - core) for n in _MATS], name="pair_swap")
    sums = [_pair_add(a, b, name="pair_add_" + n) for n, a, b in zip(_MATS, keep, got)]
    recv = _chip_exchange(sums, name="scatter_grads")
    for n, t in zip(_MATS, recv):
        new[n] = _adamw(t, inp[n], inp["m_" + n], inp["v_" + n], name="adamw_" + n)

    vsizes = [inp[n].size for n in _VECS]
    vflat = lambda prefix, src: _pack([src[prefix + n].reshape(-1) for n in _VECS], F32, LANES * 8)
    (vrecv,) = _exchange([vflat("", g)], name="gather_vector_grads")
    vout = _adamw(vrecv, vflat("", inp), vflat("m_", inp), vflat("v_", inp), name="adamw_vectors")
    vparts = [_unpack(t, vsizes, LANES * 8) for t in vout]
    for i, n in enumerate(_VECS):
        new[n] = [vp[i].reshape(inp[n].shape) for vp in vparts]

    outs = [loss, grad_x[None]]
    for k in range(4):
        outs += [new[n][k] for n in _WEIGHTS]
    return tuple(outs)
```

```python
import functools
import math

import jax
import jax.numpy as jnp
from jax import lax
from jax.experimental import pallas as pl
from jax.experimental.pallas import tpu as pltpu

F32 = jnp.float32
BF16 = jnp.bfloat16

N_DEV = 8
LANES = 128
BF16_ROWS = 16
NOPE, ROPE, VDIM = 128, 64, 128
QHEAD = 256
ROPE_THETA = 10000.0
NORM_EPS = 1e-6
GN_EPS = 64e-5
CHUNK = 64
SUB = 16
VMEM_LIMIT = 56 * 1024 * 1024

ADAM_LR, ADAM_B1, ADAM_B2, ADAM_EPS, ADAM_WD, ADAM_STEP = 0.001, 0.9, 0.999, 1e-08, 0.01, 10


def _cparams(sem):
    return pltpu.CompilerParams(dimension_semantics=sem, vmem_limit_bytes=VMEM_LIMIT)


def _pick(n, cap):
    if n <= cap:
        return n
    for t in range(cap - cap % LANES, 0, -LANES):
        if n % t == 0:
            return t
    raise ValueError(f"no tile for {n} under {cap}")


def _mm(a, b, *, ta=False, tb=False, out_dtype=F32, name, tm_cap=1024, tn_cap=512, tk_cap=2048):
    K, M = a.shape if ta else a.shape[::-1]
    N = b.shape[0] if tb else b.shape[1]
    assert (b.shape[1] if tb else b.shape[0]) == K, (a.shape, b.shape, ta, tb)
    tm, tn, tk = _pick(M, tm_cap), _pick(N, tn_cap), _pick(K, tk_cap)
    nk = K // tk
    dn = (((0 if ta else 1,), (1 if tb else 0,)), ((), ()))

    def body(a_ref, b_ref, o_ref, acc_ref):
        k = pl.program_id(2)
        p = lax.dot_general(a_ref[...], b_ref[...], dn, preferred_element_type=F32)

        @pl.when(k == 0)
        def _():
            acc_ref[...] = p

        @pl.when(k > 0)
        def _():
            acc_ref[...] += p

        @pl.when(k == nk - 1)
        def _():
            o_ref[...] = acc_ref[...].astype(out_dtype)

    a_spec = pl.BlockSpec((tk, tm), lambda i, j, k: (k, i)) if ta else pl.BlockSpec((tm, tk), lambda i, j, k: (i, k))
    b_spec = pl.BlockSpec((tn, tk), lambda i, j, k: (j, k)) if tb else pl.BlockSpec((tk, tn), lambda i, j, k: (k, j))
    return pl.pallas_call(
        body, name=name, grid=(M // tm, N // tn, nk),
        in_specs=[a_spec, b_spec], out_specs=pl.BlockSpec((tm, tn), lambda i, j, k: (i, j)),
        out_shape=jax.ShapeDtypeStruct((M, N), out_dtype),
        scratch_shapes=[pltpu.VMEM((tm, tn), F32)],
        compiler_params=_cparams(("parallel", "parallel", "arbitrary")),
    )(a, b)


def _view(arr, off, width):
    assert off % width == 0, (off, width)
    return (arr, off // width, width)


def _rowwise(fn, rows, params, out_rows, out_accs=(), *, tile, name):
    rows = [r if isinstance(r, tuple) else (r, 0, r.shape[1]) for r in rows]
    S = rows[0][0].shape[0]
    T = min(tile, S)
    assert S % T == 0
    n_rows, n_par, n_out = len(rows), len(params), len(out_rows)

    def body(*refs):
        ins = [r[...] for r in refs[:n_rows + n_par]]
        outs = fn(*ins)
        out_refs = refs[n_rows + n_par:]
        for o_ref, val in zip(out_refs[:n_out], outs[:n_out]):
            o_ref[...] = val.astype(o_ref.dtype)
        i = pl.program_id(0)
        for o_ref, val in zip(out_refs[n_out:], outs[n_out:]):
            @pl.when(i == 0)
            def _(o_ref=o_ref, val=val):
                o_ref[...] = val

            @pl.when(i > 0)
            def _(o_ref=o_ref, val=val):
                o_ref[...] += val

    in_specs = [pl.BlockSpec((T, w), functools.partial(lambda i, cb: (i, cb), cb=cb)) for _, cb, w in rows]
    in_specs += [pl.BlockSpec(p.shape, lambda i: (0, 0)) for p in params]
    out_specs = [pl.BlockSpec((T, w), lambda i: (i, 0)) for w, _ in out_rows]
    out_specs += [pl.BlockSpec(s, lambda i: (0, 0)) for s in out_accs]
    out_shape = [jax.ShapeDtypeStruct((S, w), dt) for w, dt in out_rows]
    out_shape += [jax.ShapeDtypeStruct(s, F32) for s in out_accs]
    return pl.pallas_call(
        body, name=name, grid=(S // T,), in_specs=in_specs, out_specs=out_specs, out_shape=out_shape,
        compiler_params=_cparams(("arbitrary",)),
    )(*[r[0] for r in rows], *params)


def _split3(x):
    hi = x.astype(BF16)
    r1 = x - hi.astype(F32)
    mid = r1.astype(BF16)
    lo = (r1 - mid.astype(F32)).astype(BF16)
    return hi, mid, lo


def _mm_sel(x, sel):
    hi, mid, lo = _split3(x)
    d = lambda u: jnp.dot(u, sel, preferred_element_type=F32)
    return d(hi) + d(mid) + d(lo)


@jax.custom_vjp
def _sel(x, sel, sel_t):
    return _mm_sel(x, sel)


def _sel_fwd(x, sel, sel_t):
    return _mm_sel(x, sel), (sel, sel_t)


def _sel_bwd(res, ct):
    sel, sel_t = res
    return _mm_sel(ct, sel_t), jnp.zeros_like(sel), jnp.zeros_like(sel_t)


_sel.defvjp(_sel_fwd, _sel_bwd)


def _rms(x, g):
    return x * lax.rsqrt(jnp.mean(x * x, axis=-1, keepdims=True) + NORM_EPS) * g


def _sigmoid(x):
    return 1.0 / (1.0 + jnp.exp(-x))


def _silu(x):
    return x * _sigmoid(x)


def _softplus(x):
    return jnp.maximum(x, 0.0) + jnp.log(1.0 + jnp.exp(-jnp.abs(x)))


def _bdot(x, w):
    return jnp.dot(x.astype(BF16), w.astype(BF16), preferred_element_type=F32)


def _f_mla_norm(q_a, kv_a, qg, kvg):
    return _rms(q_a, qg), _rms(kv_a, kvg)


def _f_rope(hm, qraw, kr_in, cosx, sinx, rot, rot_t):
    def rope(t):
        return t * cosx + _sel(t, rot, rot_t) * sinx
    parts = []
    for h in range(hm):
        parts.append(qraw[:, h * QHEAD:h * QHEAD + NOPE])
        parts.append(rope(qraw[:, h * QHEAD + NOPE:(h + 1) * QHEAD]))
    return jnp.concatenate(parts, axis=1), rope(kr_in)


def _f_rwkv_pre(rw, k, tail, w0f, w0b, a0f, a0b, k_k, k_a, w2cat, a2cat, seg, seg_t):
    zw = _bdot(jnp.tanh(tail), w2cat)
    za = _bdot(tail, a2cat)
    lw_f = -jnp.exp(-_softplus(-(w0f + zw[:, :rw])) - 0.5)
    lw_b = -jnp.exp(-_softplus(-(w0b + zw[:, rw:])) - 0.5)
    a_f = _sigmoid(a0f + za[:, :rw])
    a_b = _sigmoid(a0b + za[:, rw:])
    kk = k * k_k
    nrm = jnp.sqrt(_sel(_sel(kk * kk, seg, seg_t), seg_t, seg))
    kk = kk / jnp.maximum(nrm, 1e-12)
    k_f = k * (1.0 + (a_f - 1.0) * k_a)
    k_b = k * (1.0 + (a_b - 1.0) * k_a)
    return lw_f, lw_b, k_f, k_b, -kk, kk * a_f, kk * a_b


def _f_post(hn, y_f, y_b, r, k_f, k_b, v, z_r, o_mla, z_m, gn_g, gn_b, r_k, seg, seg_t):
    segsum = lambda t: _sel(_sel(t, seg, seg_t), seg_t, seg)
    y = y_f + y_b
    mu = segsum(y) * (1.0 / hn)
    yc = y - mu
    var = segsum(yc * yc) * (1.0 / hn)
    yn = yc * lax.rsqrt(var + GN_EPS) * gn_g + gn_b
    bonus = segsum(r * (k_f + k_b) * r_k) * v
    return o_mla * _silu(z_m), (yn + bonus) * _silu(z_r)


def _f_merge(u_m, u_r, g_m, g_r):
    return _sigmoid(g_m) * u_m + _sigmoid(g_r) * u_r


_NN = ((2,), (1,))
_NT = ((2,), (2,))
_TN = ((1,), (1,))

_SCAN_PASSES = {"cum": 2, "gram": 3, "solve": 1, "apply": 1, "state": 1}


def _hdot_raw(passes, x, y, dims):
    dn = (dims, ((0,), (0,)))
    d = lambda p, q: lax.dot_general(p, q, dn, preferred_element_type=F32)
    xh = x.astype(BF16)
    yh = y.astype(BF16)
    if passes == 1:
        return d(xh, yh)
    yl = (y - yh.astype(F32)).astype(BF16)
    if passes == 2:
        return d(xh, yh) + d(xh, yl)
    xl = (x - xh.astype(F32)).astype(BF16)
    return d(xh, yh) + d(xh, yl) + d(xl, yh)


@functools.partial(jax.custom_vjp, nondiff_argnums=(2, 3))
def _hdot_p(x, y, dims, passes):
    return _hdot_raw(passes, x, y, dims)


def _hdot_fwd(x, y, dims, passes):
    return _hdot_raw(passes, x, y, dims), (x, y)


def _hdot_bwd(dims, passes, res, ct):
    x, y = res
    if dims == _NN:
        return _hdot_raw(passes, ct, y, _NT), _hdot_raw(passes, x, ct, _TN)
    if dims == _NT:
        return _hdot_raw(passes, ct, y, _NN), _hdot_raw(passes, ct, x, _TN)
    return _hdot_raw(passes, y, ct, _NT), _hdot_raw(passes, x, ct, _NN)


_hdot_p.defvjp(_hdot_fwd, _hdot_bwd)


def _hdot(x, y, dims, kind):
    return _hdot_p(x, y, dims, _SCAN_PASSES[kind])


def _tri_solve(n_mat, x, length):
    row = lax.broadcasted_iota(jnp.int32, (length, length), 0)
    col = lax.broadcasted_iota(jnp.int32, (length, length), 1)
    eye = (row == col).astype(F32)[None]
    diag_blk = ((row // SUB) == (col // SUB))[None]
    nd = jnp.where(diag_blk, n_mat, 0.0)
    no = n_mat - nd
    dinv = eye + nd
    p = nd
    for _ in range(int(math.log2(SUB)) - 1):
        p = _hdot(p, p, _NN, "solve")
        dinv = dinv + _hdot(dinv, p, _NN, "solve")
    q = _hdot(dinv, no, _NN, "solve")
    u = _hdot(dinv, x, _NN, "solve")
    levels = int(math.log2(length // SUB))
    qs = [q]
    for _ in range(levels - 1):
        qs.append(_hdot(qs[-1], qs[-1], _NN, "solve"))
    for qk in reversed(qs):
        u = u + _hdot(qk, u, _NN, "solve")
    return u


def _rwkv_chunk(rev, s0, r, lw, k, v, a, b):
    pairs, length, width = r.shape
    hn = width // 2
    row = lax.broadcasted_iota(jnp.int32, (length, length), 0)
    col = lax.broadcasted_iota(jnp.int32, (length, length), 1)
    incl = ((row <= col) if rev else (row >= col))[None]
    strict = ((row < col) if rev else (row > col))[None]
    lane = lax.broadcasted_iota(jnp.int32, (1, 1, width), 2)
    first = lane < hn
    head_mask = jnp.concatenate([jnp.broadcast_to(first.astype(F32), (pairs, 1, width)),
                                 jnp.broadcast_to(1.0 - first.astype(F32), (pairs, 1, width))], axis=0)
    twice = lambda t: jnp.concatenate([t, t], axis=0)
    pick = lambda t: jnp.where(first, t[:pairs], t[pairs:])

    t_incl = jnp.broadcast_to(incl.astype(F32), (pairs, length, length))
    cum = _hdot(t_incl, lw, _NN, "cum")
    g = jnp.exp(cum)
    g_inv = jnp.exp(-cum)
    at = a * jnp.exp(cum - lw)
    rt = r * g
    bt = b * g_inv
    kt = k * g_inv
    lhs = jnp.concatenate([twice(at) * head_mask, twice(rt) * head_mask], axis=1)
    rhs = jnp.concatenate([twice(bt), twice(kt)], axis=1)
    gram = _hdot(lhs, rhs, _NT, "gram")
    row2 = lax.broadcasted_iota(jnp.int32, (length, 2 * length), 0)
    col2 = lax.broadcasted_iota(jnp.int32, (length, 2 * length), 1)
    col2 = jnp.where(col2 >= length, col2 - length, col2)
    strict2 = ((row2 < col2) if rev else (row2 > col2))[None]
    incl2 = ((row2 <= col2) if rev else (row2 >= col2))[None]
    top = jnp.where(strict2, gram[:, :length], 0.0)
    bot = jnp.where(incl2, gram[:, length:], 0.0)
    v2 = twice(v)
    zeros = jnp.zeros_like(v2)
    x = _hdot(at, s0, _NT, "apply") + pick(_hdot(top, jnp.concatenate([zeros, v2], axis=1), _NN, "apply"))
    u = pick(_tri_solve(top[:, :, :length], twice(x), length))
    y = _hdot(rt, s0, _NT, "apply") + pick(_hdot(bot, jnp.concatenate([twice(u), v2], axis=1), _NN, "apply"))
    g_last = g[:, 0:1, :] if rev else g[:, length - 1:length, :]
    ri = lax.broadcasted_iota(jnp.int32, (width, width), 0)
    ci = lax.broadcasted_iota(jnp.int32, (width, width), 1)
    same_head = ((ri < hn) == (ci < hn))[None]
    upd = _hdot(u, bt, _TN, "state") + _hdot(v, kt, _TN, "state")
    s1 = (s0 + jnp.where(same_head, upd, 0.0)) * g_last
    return y, s1


def _split_pairs(x):
    return jnp.stack([x[:, p * LANES:(p + 1) * LANES] for p in range(x.shape[1] // LANES)])


def _merge_pairs(x):
    return jnp.concatenate([x[p] for p in range(x.shape[0])], axis=1)


def _scan_specs(views, rw, nc, rev):
    cidx = (lambda c: nc - 1 - c) if rev else (lambda c: c)
    seqs = [pl.BlockSpec((CHUNK, rw), functools.partial(lambda c, cb: (cidx(c), cb), cb=cb)) for _, cb, _ in views]
    plain = pl.BlockSpec((CHUNK, rw), lambda c: (cidx(c), 0))
    st = pl.BlockSpec((1, rw // LANES, LANES, LANES), lambda c: (cidx(c), 0, 0, 0))
    return seqs, plain, st


def _as_views(arrs, rw):
    return [t if isinstance(t, tuple) else (t, 0, rw) for t in arrs]


def _rwkv_scan_fwd(rev, ops, rw, *, name):
    views = _as_views(ops, rw)
    S = views[0][0].shape[0]
    nc, pairs = S // CHUNK, rw // LANES
    seqs, plain, st = _scan_specs(views, rw, nc, rev)

    def body(*refs):
        y_ref, st_ref, s_ref = refs[6:]

        @pl.when(pl.program_id(0) == 0)
        def _():
            s_ref[...] = jnp.zeros_like(s_ref)

        s0 = s_ref[...]
        st_ref[0] = s0
        y, s1 = _rwkv_chunk(rev, s0, *[_split_pairs(t[...]) for t in refs[:6]])
        y_ref[...] = _merge_pairs(y)
        s_ref[...] = s1

    return pl.pallas_call(
        body, name=name, grid=(nc,), in_specs=seqs, out_specs=[plain, st],
        out_shape=[jax.ShapeDtypeStruct((S, rw), F32), jax.ShapeDtypeStruct((nc, pairs, LANES, LANES), F32)],
        scratch_shapes=[pltpu.VMEM((pairs, LANES, LANES), F32)],
        compiler_params=_cparams(("arbitrary",)),
    )(*[t[0] for t in views])


def _rwkv_scan_bwd(rev, ops, states, dy, rw, *, name):
    views = _as_views(list(ops) + [dy], rw)
    S = views[0][0].shape[0]
    nc, pairs = S // CHUNK, rw // LANES
    seqs, plain, st = _scan_specs(views, rw, nc, not rev)

    def body(*refs):
        st_ref, out_refs, ds_ref = refs[7], refs[8:14], refs[14]

        @pl.when(pl.program_id(0) == 0)
        def _():
            ds_ref[...] = jnp.zeros_like(ds_ref)

        _, vjp = jax.vjp(functools.partial(_rwkv_chunk, rev), st_ref[0], *[_split_pairs(t[...]) for t in refs[:6]])
        grads = vjp((_split_pairs(refs[6][...]), ds_ref[...]))
        ds_ref[...] = grads[0]
        for o_ref, gval in zip(out_refs, grads[1:]):
            o_ref[...] = _merge_pairs(gval)

    return pl.pallas_call(
        body, name=name, grid=(nc,), in_specs=seqs + [st], out_specs=[plain] * 6,
        out_shape=[jax.ShapeDtypeStruct((S, rw), F32)] * 6,
        scratch_shapes=[pltpu.VMEM((pairs, LANES, LANES), F32)],
        compiler_params=_cparams(("arbitrary",)),
    )(*[t[0] for t in views], states)


def _shift_lerp(x_view, mu, d=None, *, name):
    arr, off, width = x_view
    S = arr.shape[0]
    cb = _pick(width, 256)
    assert off % cb == 0

    def cshift(t):
        rows = lax.broadcasted_iota(jnp.int32, t.shape, 0)
        prev = jnp.where(rows == 0, 0.0, pltpu.roll(t, 1, 0))
        nxt = jnp.where(rows == S - 1, 0.0, pltpu.roll(t, S - 1, 0))
        return 0.5 * (prev + nxt)

    def fwd_body(x_ref, mu_ref, o_ref):
        x = x_ref[...]
        o_ref[...] = x + mu_ref[...] * (cshift(x) - x)

    def bwd_body(x_ref, mu_ref, d_ref, dx_ref, dmu_ref):
        x, m, dd = x_ref[...], mu_ref[...], d_ref[...]
        gm = m * dd
        dx_ref[...] = dd - gm + cshift(gm)
        dmu_ref[...] = jnp.sum(dd * (cshift(x) - x), axis=0, keepdims=True)

    x_spec = pl.BlockSpec((S, cb), lambda j: (0, off // cb + j))
    blk = pl.BlockSpec((S, cb), lambda j: (0, j))
    vec = pl.BlockSpec((1, cb), lambda j: (0, j))
    if d is None:
        return pl.pallas_call(
            fwd_body, name=name, grid=(width // cb,), in_specs=[x_spec, vec], out_specs=blk,
            out_shape=jax.ShapeDtypeStruct((S, width), F32), compiler_params=_cparams(("parallel",)),
        )(arr, mu)
    return pl.pallas_call(
        bwd_body, name=name, grid=(width // cb,), in_specs=[x_spec, vec, blk], out_specs=[blk, vec],
        out_shape=[jax.ShapeDtypeStruct((S, width), F32), jax.ShapeDtypeStruct((1, width), F32)],
        compiler_params=_cparams(("parallel",)),
    )(arr, mu, d)


def _attention_fwd(qfull, kv, kr, hm, scale, *, tq, name):
    S = qfull.shape[0]

    def body(qn_ref, qr_ref, kn_ref, kr_ref, v_ref, o_ref, lse_ref):
        s = _attn_scores(qn_ref, qr_ref, kn_ref, kr_ref)
        m = jnp.max(s, axis=-1, keepdims=True)
        p = jnp.exp((s - m) * scale)
        l = jnp.sum(p, axis=-1, keepdims=True)
        o_ref[...] = jnp.dot(p.astype(BF16), v_ref[...], preferred_element_type=F32) * (1.0 / l)
        lse_ref[...] = jnp.broadcast_to(m * scale + jnp.log(l), lse_ref.shape)

    oblk = pl.BlockSpec((tq, VDIM), lambda h, i: (i, h))
    return pl.pallas_call(
        body, name=name, grid=(hm, S // tq),
        in_specs=[pl.BlockSpec((tq, NOPE), lambda h, i: (i, 2 * h)),
                  pl.BlockSpec((tq, NOPE), lambda h, i: (i, 2 * h + 1)),
                  pl.BlockSpec((S, NOPE), lambda h, i: (0, h)),
                  pl.BlockSpec((S, LANES), lambda h, i: (0, 0)),
                  pl.BlockSpec((S, VDIM), lambda h, i: (0, hm + h))],
        out_specs=[oblk, oblk],
        out_shape=[jax.ShapeDtypeStruct((S, hm * VDIM), F32)] * 2,
        compiler_params=_cparams(("parallel", "parallel")),
    )(qfull, qfull, kv, kr, kv)


def _attn_scores(qn_ref, qr_ref, kn_ref, kr_ref):
    nt = (((1,), (1,)), ((), ()))
    return (lax.dot_general(qn_ref[...], kn_ref[...], nt, preferred_element_type=F32)
            + lax.dot_general(qr_ref[...], kr_ref[...], nt, preferred_element_type=F32))


def _attention_bwd(qfull, kv, kr, o, lse, d_o, hm, scale, *, tq, name):
    S = qfull.shape[0]
    tn = (((0,), (0,)), ((), ()))
    nt = (((1,), (1,)), ((), ()))

    def body(qn_ref, qr_ref, kn_ref, kr_ref, v_ref, o_ref, lse_ref, do_ref,
             dqn_ref, dqr_ref, dkn_ref, dv_ref, dkr_ref):
        s = _attn_scores(qn_ref, qr_ref, kn_ref, kr_ref)
        p = jnp.exp(s * scale - lse_ref[:, 0:1])
        d_out = do_ref[...]
        delta = jnp.sum(d_out * o_ref[...], axis=-1, keepdims=True)
        d_out = d_out.astype(BF16)
        dp = lax.dot_general(d_out, v_ref[...], nt, preferred_element_type=F32)
        ds = (p * ((dp - delta) * scale)).astype(BF16)
        dqn_ref[...] = jnp.dot(ds, kn_ref[...], preferred_element_type=F32)
        dqr_ref[...] = jnp.dot(ds, kr_ref[...], preferred_element_type=F32)
        dv = lax.dot_general(p.astype(BF16), d_out, tn, preferred_element_type=F32)
        dkn = lax.dot_general(ds, qn_ref[...], tn, preferred_element_type=F32)
        dkr = lax.dot_general(ds, qr_ref[...], tn, preferred_element_type=F32)
        first = pl.program_id(1) == 0
        for ref, val in ((dkn_ref, dkn), (dv_ref, dv), (dkr_ref, dkr)):
            @pl.when(first)
            def _(ref=ref, val=val):
                ref[...] = val

            @pl.when(jnp.logical_not(first))
            def _(ref=ref, val=val):
                ref[...] += val

    qblk = pl.BlockSpec((tq, NOPE), lambda h, i: (i, h))
    kblk = pl.BlockSpec((S, NOPE), lambda h, i: (0, h))
    shp = jax.ShapeDtypeStruct((S, hm * NOPE), F32)
    return pl.pallas_call(
        body, name=name, grid=(hm, S // tq),
        in_specs=[pl.BlockSpec((tq, NOPE), lambda h, i: (i, 2 * h)),
                  pl.BlockSpec((tq, NOPE), lambda h, i: (i, 2 * h + 1)),
                  kblk,
                  pl.BlockSpec((S, LANES), lambda h, i: (0, 0)),
                  pl.BlockSpec((S, VDIM), lambda h, i: (0, hm + h)),
                  qblk, qblk, qblk],
        out_specs=[qblk, qblk, kblk, kblk, kblk],
        out_shape=[shp] * 5,
        compiler_params=_cparams(("parallel", "arbitrary")),
    )(qfull, qfull, kv, kr, kv, o, lse, d_o)


def _layout(D, MW, RW, TAIL, QR, KVR):
    names = ["gate_m", "gate_r", "z_m", "z_r", "r", "k", "v", "tail", "q_a", "kv_a"]
    widths = [D, D, MW, RW, RW, RW, RW, TAIL, QR, KVR]
    offs, o = {}, 0
    for nme, w in zip(names, widths):
        assert o % w == 0, (nme, o, w)
        offs[nme] = (o, w)
        o += w
    return offs, o


def _local_grads(x, target, W, dims):
    S, D = x.shape
    hm, hr, hn, rank = dims["hm"], dims["hr"], dims["hn"], dims["rank"]
    MW, RW = hm * VDIM, hr * hn
    TAIL = W["w2cat"].shape[0]
    QR, KVR = W["mla_q_norm"].shape[1], W["mla_kv_norm"].shape[1]
    lay, d_in = _layout(D, MW, RW, TAIL, QR, KVR)
    T = 256
    scale = (NOPE + ROPE) ** -0.5
    col = lambda arr, nme: _view(arr, *lay[nme])

    pos = jnp.arange(S, dtype=F32)
    inv_freq = jnp.power(ROPE_THETA, -jnp.arange(0, ROPE, 2, dtype=F32) / ROPE)
    ang = pos[:, None] * inv_freq[None, :]
    zpad = jnp.zeros((S, LANES - ROPE), F32)
    cosx = jnp.concatenate([jnp.cos(ang), jnp.cos(ang), zpad], axis=1)
    sinx = jnp.concatenate([jnp.sin(ang), jnp.sin(ang), zpad], axis=1)
    ri, ci = jnp.arange(LANES)[:, None], jnp.arange(LANES)[None, :]
    half = ROPE // 2
    rot = (jnp.where((ri == ci - half) & (ci >= half) & (ci < ROPE), 1.0, 0.0)
           - jnp.where((ri == ci + half) & (ci < half), 1.0, 0.0)).astype(BF16)
    rot_t = rot.T
    seg = (jnp.arange(RW)[:, None] // hn == jnp.arange(LANES)[None, :]).astype(BF16)
    seg_t = seg.T

    (h,) = _rowwise(lambda xb, g: (_rms(xb, g),), [x], [W["g_pre"]], [(D, BF16)], tile=T, name="pre_norm")
    proj = _mm(h, W["w_in"], name="in_proj")

    qn, kvn = _rowwise(_f_mla_norm, [col(proj, "q_a"), col(proj, "kv_a")], [W["mla_q_norm"], W["mla_kv_norm"]],
                       [(QR, BF16), (KVR, BF16)], tile=T, name="mla_norm")
    qraw = _mm(qn, W["wq_b"], name="q_up")
    kv = _mm(kvn, W["wkv_b"], out_dtype=BF16, name="kv_up")
    kr_view = _view(proj, lay["tail"][0], LANES)
    qfull, kr = _rowwise(functools.partial(_f_rope, hm), [qraw, kr_view, cosx, sinx], [rot, rot_t],
                         [(hm * QHEAD, BF16), (LANES, BF16)], tile=T, name="rope")
    o_mla, lse = _attention_fwd(qfull, kv, kr, hm, scale, tq=T, name="attn_fwd")

    shift_view = (proj, lay["r"][0], 3 * RW + TAIL)
    rl = _shift_lerp(shift_view, W["mu"], name="shift_fwd")
    rl_r, rl_k, rl_v = _view(rl, 0, RW), _view(rl, RW, RW), _view(rl, 2 * RW, RW)
    rl_tail = _view(rl, 3 * RW, TAIL)
    pre_params = [W["w0_f"], W["w0_b"], W["a0_f"], W["a0_b"], W["k_k"], W["k_a"], W["w2cat"], W["a2cat"], seg, seg_t]
    pre_fn = functools.partial(_f_rwkv_pre, RW)
    lw_f, lw_b, k_f, k_b, a_n, b_f, b_b = _rowwise(pre_fn, [rl_k, rl_tail], pre_params, [(RW, F32)] * 7, tile=T,
                                                    name="rwkv_pre")
    dirs = {}
    for tag, rev, lw, kd, bd in (("f", False, lw_f, k_f, b_f), ("b", True, lw_b, k_b, b_b)):
        ops = (rl_r, lw, kd, rl_v, a_n, bd)
        y_d, st = _rwkv_scan_fwd(rev, ops, RW, name="scan_fwd_" + tag)
        dirs[tag] = (rev, ops, st, y_d)
    y_f, y_b = dirs["f"][3], dirs["b"][3]

    post_fn = functools.partial(_f_post, hn)
    post_rows = [y_f, y_b, rl_r, k_f, k_b, rl_v, col(proj, "z_r"), o_mla, col(proj, "z_m")]
    post_params = [W["gn_g"], W["gn_b"], W["r_k"], seg, seg_t]
    ymg, yrg = _rowwise(post_fn, post_rows, post_params, [(MW, BF16), (RW, BF16)], tile=T, name="post")
    u_m = _mm(ymg, W["w_br_mla"], name="br_mla")
    u_r = _mm(yrg, W["w_br_rwkv"], name="br_rwkv")
    merge_rows = [u_m, u_r, col(proj, "gate_m"), col(proj, "gate_r")]
    (merged,) = _rowwise(lambda *t: (_f_merge(*t),), merge_rows, [], [(D, BF16)], tile=T, name="merge")
    out = _mm(merged, W["w_out"], name="out_proj")

    def head(ob, xb, tb, g):
        yn, vjp = jax.vjp(_rms, ob, g)
        err = xb + yn - tb
        dy = err * (1.0 / D)
        d_ob, d_g = vjp(dy)
        loss = jnp.broadcast_to(0.5 * jnp.sum(err * err) * (1.0 / D), (1, LANES))
        return dy, d_ob, loss, d_g

    dy, d_out, loss, g_g_post = _rowwise(head, [out, x, target], [W["g_post"]], [(D, F32), (D, BF16)],
                                         [(1, LANES), (1, D)], tile=T, name="head")
    d_merged = _mm(d_out, W["w_out"], tb=True, name="d_merged")
    g_w_out = _mm(merged, d_out, ta=True, out_dtype=BF16, name="g_w_out")

    def merge_bwd(u_m_b, u_r_b, g_m_b, g_r_b, dm):
        _, vjp = jax.vjp(_f_merge, u_m_b, u_r_b, g_m_b, g_r_b)
        return vjp(dm)

    d_u_m, d_u_r, d_gate_m, d_gate_r = _rowwise(merge_bwd, merge_rows + [d_merged], [], [(D, BF16)] * 4, tile=T,
                                                name="merge_bwd")
    d_ymg = _mm(d_u_m, W["w_br_mla"], tb=True, name="d_ymg")
    d_yrg = _mm(d_u_r, W["w_br_rwkv"], tb=True, name="d_yrg")
    g_w_br_mla = _mm(ymg, d_u_m, ta=True, out_dtype=BF16, name="g_w_br_mla")
    g_w_br_rwkv = _mm(yrg, d_u_r, ta=True, out_dtype=BF16, name="g_w_br_rwkv")

    def post_bwd(*args):
        nr = len(post_rows)
        prim, dm, dr = args[:nr] + args[nr + 2:], args[nr], args[nr + 1]
        _, vjp = jax.vjp(post_fn, *prim)
        g = vjp((dm, dr))
        return g[0], g[2], g[3], g[5], g[6], g[7], g[8], g[9], g[10], g[11]

    (d_y, d_r_bonus, d_k_bonus, d_v_bonus, d_z_r, d_o, d_z_m, g_gn_g, g_gn_b, g_r_k) = _rowwise(
        post_bwd, post_rows + [d_ymg, d_yrg], post_params,
        [(RW, F32), (RW, F32), (RW, F32), (RW, F32), (RW, BF16), (MW, F32), (MW, BF16)],
        [(1, RW)] * 3, tile=T // 2, name="post_bwd")

    dsc = {}
    for tag in ("f", "b"):
        rev, ops, st, _ = dirs[tag]
        dsc[tag] = _rwkv_scan_bwd(rev, ops, st, d_y, RW, name="scan_bwd_" + tag)

    d_qn, d_qr, d_kn, d_v_att, d_kr_h = _attention_bwd(qfull, kv, kr, o_mla, lse, d_o, hm, scale, tq=T, name="attn_bwd")

    def rope_bwd(qraw_b, kr_in, cos_b, sin_b, dqn_b, dqr_b, dkn_b, dv_b, dkrh_b, rot_b, rot_t_b):
        _, vjp = jax.vjp(lambda q_, k_: _f_rope(hm, q_, k_, cos_b, sin_b, rot_b, rot_t_b), qraw_b, kr_in)
        parts = []
        for hh in range(hm):
            parts += [dqn_b[:, hh * NOPE:(hh + 1) * NOPE], dqr_b[:, hh * NOPE:(hh + 1) * NOPE]]
        dkr = dkrh_b[:, :LANES]
        for hh in range(1, hm):
            dkr = dkr + dkrh_b[:, hh * LANES:(hh + 1) * LANES]
        d_qraw, d_kr_in = vjp((jnp.concatenate(parts, axis=1), dkr))
        return d_qraw, jnp.concatenate([dkn_b, dv_b], axis=1), d_kr_in

    d_qraw, d_kv, d_kr_in = _rowwise(rope_bwd, [qraw, kr_view, cosx, sinx, d_qn, d_qr, d_kn, d_v_att, d_kr_h],
                                     [rot, rot_t], [(hm * QHEAD, BF16), (2 * MW, BF16), (LANES, F32)], tile=T,
                                     name="rope_bwd")
    d_qnorm = _mm(d_qraw, W["wq_b"], tb=True, name="d_qn")
    d_kvnorm = _mm(d_kv, W["wkv_b"], tb=True, name="d_kvn")
    g_wq_b = _mm(qn, d_qraw, ta=True, out_dtype=BF16, name="g_wq_b")
    g_wkv_b = _mm(kvn, d_kv, ta=True, out_dtype=BF16, name="g_wkv_b")

    def mla_norm_bwd(q_a, kv_a, qg, kvg, dq, dk):
        _, vjp = jax.vjp(_f_mla_norm, q_a, kv_a, qg, kvg)
        return vjp((dq, dk))

    d_q_a, d_kv_a, g_q_norm, g_kv_norm = _rowwise(
        lambda q_a, kv_a, dq, dk, qg, kvg: mla_norm_bwd(q_a, kv_a, qg, kvg, dq, dk),
        [col(proj, "q_a"), col(proj, "kv_a"), d_qnorm, d_kvnorm], [W["mla_q_norm"], W["mla_kv_norm"]],
        [(QR, BF16), (KVR, BF16)], [(1, QR), (1, KVR)], tile=T, name="mla_norm_bwd")

    def pre_bwd(k_b_, tail_b, dlwf, dlwb, dkf, dkb, dkbon, daf, dab, dbf, dbb, drf, drb, drbon, dvf, dvb, dvbon,
                dkr, *params):
        _, vjp = jax.vjp(pre_fn, k_b_, tail_b, *params[:8], params[8], params[9])
        g = vjp((dlwf, dlwb, dkf + dkbon, dkb + dkbon, daf + dab, dbf, dbb))
        d_tail = g[1] + jnp.concatenate([dkr, jnp.zeros((dkr.shape[0], TAIL - LANES), F32)], axis=1)
        d_rl = jnp.concatenate([drf + drb + drbon, g[0], dvf + dvb + dvbon, d_tail], axis=1)
        return (d_rl,) + tuple(g[2:10])

    f_, b_ = dsc["f"], dsc["b"]
    pre_bwd_rows = [rl_k, rl_tail, f_[1], b_[1], f_[2], b_[2], d_k_bonus, f_[4], b_[4], f_[5], b_[5],
                    f_[0], b_[0], d_r_bonus, f_[3], b_[3], d_v_bonus, d_kr_in]
    (d_rl, g_w0_f, g_w0_b, g_a0_f, g_a0_b, g_k_k, g_k_a, g_w2cat, g_a2cat) = _rowwise(
        pre_bwd, pre_bwd_rows, pre_params, [(3 * RW + TAIL, F32)],
        [(1, RW)] * 6 + [(TAIL, 2 * RW)] * 2, tile=T // 2, name="rwkv_pre_bwd")
    d_shift, g_mu = _shift_lerp(shift_view, W["mu"], d_rl, name="shift_bwd")

    d_proj = jnp.concatenate([d_gate_m, d_gate_r, d_z_m, d_z_r, d_shift.astype(BF16), d_q_a, d_kv_a], axis=1)
    assert d_proj.shape == (S, d_in)
    d_h = _mm(d_proj, W["w_in"], tb=True, tn_cap=1024, name="d_h")
    g_w_in = _mm(h, d_proj, ta=True, out_dtype=BF16, name="g_w_in")

    def pre_norm_bwd(xb, dyb, dhb, g):
        _, vjp = jax.vjp(_rms, xb, g)
        dx, dg = vjp(dhb)
        return dyb + dx, dg

    grad_x, g_g_pre = _rowwise(pre_norm_bwd, [x, dy, d_h], [W["g_pre"]], [(D, F32)], [(1, D)], tile=T,
                               name="pre_norm_bwd")

    grads = dict(g_pre=g_g_pre, w_in=g_w_in, mla_q_norm=g_q_norm, wq_b=g_wq_b, mla_kv_norm=g_kv_norm,
                 wkv_b=g_wkv_b, mu=g_mu, w0_f=g_w0_f, w0_b=g_w0_b, a0_f=g_a0_f, a0_b=g_a0_b, k_k=g_k_k, k_a=g_k_a,
                 w2cat=g_w2cat, a2cat=g_a2cat, r_k=g_r_k, gn_g=g_gn_g, gn_b=g_gn_b, w_br_mla=g_w_br_mla,
                 w_br_rwkv=g_w_br_rwkv, w_out=g_w_out, g_post=g_g_post)
    return loss[0, 0], grad_x, grads


_MATS = ["w_in", "mla_wq_b", "mla_wkv_b", "rwkv_w2_f", "rwkv_w2_b", "rwkv_a2_f", "rwkv_a2_b", "w_br_mla",
         "w_br_rwkv", "w_out"]
_ROW_SHARDED = ("w_out",)
_VECS = ["g_pre", "mla_q_norm", "mla_kv_norm", "rwkv_mu", "rwkv_w0_f", "rwkv_w0_b", "rwkv_a0_f", "rwkv_a0_b",
         "rwkv_k_k", "rwkv_k_a", "rwkv_r_k", "rwkv_gn_g", "rwkv_gn_b", "g_post"]
_WEIGHTS = ["g_pre", "w_in", "mla_q_norm", "mla_wq_b", "mla_kv_norm", "mla_wkv_b", "rwkv_mu", "rwkv_w0_f",
            "rwkv_w2_f", "rwkv_w0_b", "rwkv_w2_b", "rwkv_a0_f", "rwkv_a2_f", "rwkv_a0_b", "rwkv_a2_b", "rwkv_k_k",
            "rwkv_k_a", "rwkv_r_k", "rwkv_gn_g", "rwkv_gn_b", "w_br_mla", "w_br_rwkv", "w_out", "g_post"]

def _exchange(srcs, *, name):
    n = len(srcs)

    def body(*refs):
        src_refs, out_refs = refs[:n], refs[n:2 * n]
        send_sems, recv_sems, local_sems = refs[2 * n:]
        x, y, c = lax.axis_index("x"), lax.axis_index("y"), lax.axis_index("c")
        me = 4 * x + 2 * y + c
        flip = lambda v, bit: (1 - v) if bit else v

        def piece(a, idx):
            return src_refs[a] if srcs[a].ndim == 2 else src_refs[a].at[idx]

        owns = [pltpu.make_async_copy(piece(a, me), out_refs[a].at[me], local_sems.at[a]) for a in range(n)]
        for cp in owns:
            cp.start()
        sends, peers = [], []
        for d in range(1, N_DEV):
            px, py, pc = flip(x, d & 4), flip(y, d & 2), flip(c, d & 1)
            pidx = 4 * px + 2 * py + pc
            peers.append(((px, py, pc), pidx))
            for a in range(n):
                cp = pltpu.make_async_remote_copy(
                    src_ref=piece(a, pidx), dst_ref=out_refs[a].at[me], send_sem=send_sems.at[d - 1, a],
                    recv_sem=recv_sems.at[d - 1, a], device_id=(px, py, pc), device_id_type=pl.DeviceIdType.MESH)
                cp.start()
                sends.append(cp)
        for d, (peer, pidx) in zip(range(1, N_DEV), peers):
            for a in range(n):
                pltpu.make_async_remote_copy(
                    src_ref=piece(a, pidx), dst_ref=out_refs[a].at[pidx], send_sem=send_sems.at[d - 1, a],
                    recv_sem=recv_sems.at[d - 1, a], device_id=peer, device_id_type=pl.DeviceIdType.MESH).wait_recv()
        for cp in sends:
            cp.wait_send()
        for cp in owns:
            cp.wait()

    return pl.pallas_call(
        body, name=name,
        out_shape=[jax.ShapeDtypeStruct((N_DEV,) + s.shape[-2:], s.dtype) for s in srcs],
        in_specs=[pl.BlockSpec(memory_space=pl.ANY)] * n, out_specs=[pl.BlockSpec(memory_space=pl.ANY)] * n,
        scratch_shapes=[pltpu.SemaphoreType.DMA((N_DEV - 1, n)), pltpu.SemaphoreType.DMA((N_DEV - 1, n)),
                        pltpu.SemaphoreType.DMA((n,))],
    )(*srcs)


def _remote(src, dst, sems, key, to):
    send_sems, recv_sems = sems
    return pltpu.make_async_remote_copy(src_ref=src, dst_ref=dst, send_sem=send_sems.at[key], recv_sem=recv_sems.at[key],
                                        device_id=to, device_id_type=pl.DeviceIdType.MESH)


def _hbm_call(body, srcs, out_shapes, sem_shapes, *, name):
    n = len(srcs)
    return pl.pallas_call(
        body, name=name, out_shape=out_shapes,
        in_specs=[pl.BlockSpec(memory_space=pl.ANY)] * n,
        out_specs=[pl.BlockSpec(memory_space=pl.ANY)] * len(out_shapes),
        scratch_shapes=[pltpu.SemaphoreType.DMA(s) for s in sem_shapes],
    )(*srcs)


def _gather_two_level(srcs, *, name):
    n = len(srcs)

    def body(*refs):
        src_refs, out_refs = refs[:n], refs[n:2 * n]
        sems, local_sems = refs[2 * n:2 * n + 2], refs[2 * n + 2]
        x, y, c = lax.axis_index("x"), lax.axis_index("y"), lax.axis_index("c")
        idx = lambda px, py, pc: 4 * px + 2 * py + pc
        me, sibling = (x, y, c), (x, y, 1 - c)
        chips = [(1 - x, y), (x, 1 - y), (1 - x, 1 - y)]
        owns = [pltpu.make_async_copy(src_refs[a], out_refs[a].at[idx(*me)], local_sems.at[a]) for a in range(n)]
        for cp in owns:
            cp.start()
        sends = []
        for a in range(n):
            sends.append(_remote(src_refs[a], out_refs[a].at[idx(*me)], sems, (0, a), sibling))
            for j, chip in enumerate(chips):
                sends.append(_remote(src_refs[a], out_refs[a].at[idx(*me)], sems, (1 + j, a), (*chip, c)))
        for cp in sends:
            cp.start()
        for j, chip in enumerate(chips):
            for a in range(n):
                blk = out_refs[a].at[idx(*chip, c)]
                _remote(blk, blk, sems, (1 + j, a), me).wait_recv()
                fwd = _remote(blk, blk, sems, (4 + j, a), sibling)
                fwd.start()
                sends.append(fwd)
        for a in range(n):
            blk = out_refs[a].at[idx(*sibling)]
            _remote(blk, blk, sems, (0, a), me).wait_recv()
            for j, chip in enumerate(chips):
                blk = out_refs[a].at[idx(*chip, 1 - c)]
                _remote(blk, blk, sems, (4 + j, a), me).wait_recv()
        for cp in sends:
            cp.wait_send()
        for cp in owns:
            cp.wait()

    return _hbm_call(body, srcs, [jax.ShapeDtypeStruct((N_DEV,) + s.shape, s.dtype) for s in srcs],
                     [(7, n), (7, n), (n,)], name=name)


def _sibling_swap(srcs, *, name):
    n = len(srcs)

    def body(*refs):
        src_refs, out_refs, sems = refs[:n], refs[n:2 * n], refs[2 * n:]
        x, y, c = lax.axis_index("x"), lax.axis_index("y"), lax.axis_index("c")
        copies = [_remote(src_refs[a], out_refs[a], sems, a, (x, y, 1 - c)) for a in range(n)]
        for cp in copies:
            cp.start()
        for cp in copies:
            cp.wait()

    return _hbm_call(body, srcs, [jax.ShapeDtypeStruct(s.shape, s.dtype) for s in srcs], [(n,), (n,)], name=name)


def _chip_exchange(srcs, *, name):
    n = len(srcs)

    def body(*refs):
        src_refs, out_refs = refs[:n], refs[n:2 * n]
        sems, local_sems = refs[2 * n:2 * n + 2], refs[2 * n + 2]
        x, y, c = lax.axis_index("x"), lax.axis_index("y"), lax.axis_index("c")
        mine = 2 * x + y
        chips = [(1 - x, y), (x, 1 - y), (1 - x, 1 - y)]
        owns = [pltpu.make_async_copy(src_refs[a].at[mine], out_refs[a].at[mine], local_sems.at[a]) for a in range(n)]
        for cp in owns:
            cp.start()
        sends = [_remote(src_refs[a].at[2 * px + py], out_refs[a].at[mine], sems, (j, a), (px, py, c))
                 for j, (px, py) in enumerate(chips) for a in range(n)]
        for cp in sends:
            cp.start()
        for j, (px, py) in enumerate(chips):
            for a in range(n):
                blk = out_refs[a].at[2 * px + py]
                _remote(blk, blk, sems, (j, a), (x, y, c)).wait_recv()
        for cp in sends:
            cp.wait_send()
        for cp in owns:
            cp.wait()

    return _hbm_call(body, srcs, [jax.ShapeDtypeStruct(s.shape, s.dtype) for s in srcs], [(3, n), (3, n), (n,)],
                     name=name)


def _pair_add(a, b, *, name):
    q, r, c = a.shape
    tr = r if r <= 256 else _pick_rows(r, 256)

    def body(a_ref, b_ref, o_ref):
        o_ref[...] = (a_ref[...].astype(F32) + b_ref[...].astype(F32)).astype(BF16)

    blk = pl.BlockSpec((1, tr, c), lambda i, j: (i, j, 0))
    return pl.pallas_call(body, name=name, grid=(q, r // tr), in_specs=[blk, blk], out_specs=blk,
                          out_shape=jax.ShapeDtypeStruct(a.shape, BF16),
                          compiler_params=_cparams(("parallel", "parallel")))(a, b)


def _adamw(recv, w, m, v, *, name):
    r, c = w.shape
    n_terms = recv.shape[0]
    tr = r if r <= 256 else _pick_rows(r, 256)

    def body(g_ref, w_ref, m_ref, v_ref, go_ref, d_ref, mo_ref, vo_ref):
        g = g_ref[0].astype(F32)
        for k in range(1, n_terms):
            g = g + g_ref[k].astype(F32)
        m_new = ADAM_B1 * m_ref[...] + (1.0 - ADAM_B1) * g
        v_new = ADAM_B2 * v_ref[...] + (1.0 - ADAM_B2) * (g * g)
        m_hat = m_new / (1.0 - ADAM_B1 ** ADAM_STEP)
        v_hat = v_new / (1.0 - ADAM_B2 ** ADAM_STEP)
        go_ref[...] = g
        d_ref[...] = -ADAM_LR * (m_hat / (jnp.sqrt(v_hat) + ADAM_EPS) + ADAM_WD * w_ref[...])
        mo_ref[...] = m_new
        vo_ref[...] = v_new

    blk = pl.BlockSpec((tr, c), lambda i: (i, 0))
    return pl.pallas_call(
        body, name=name, grid=(r // tr,),
        in_specs=[pl.BlockSpec((n_terms, tr, c), lambda i: (0, i, 0)), blk, blk, blk], out_specs=[blk] * 4,
        out_shape=[jax.ShapeDtypeStruct((r, c), F32)] * 4, compiler_params=_cparams(("parallel",)),
    )(recv, w, m, v)


def _pick_rows(n, cap):
    for t in range(cap, 0, -BF16_ROWS):
        if n % t == 0:
            return t
    raise ValueError(f"no row tile for {n}")


def _pack(pieces, dtype, quantum):
    out = []
    for p in pieces:
        lead, n = p.shape[:-1], p.shape[-1]
        pad = (-n) % quantum
        p = p.astype(dtype)
        if pad:
            p = jnp.concatenate([p, jnp.zeros(lead + (pad,), dtype)], axis=-1)
        out.append(p)
    flat = jnp.concatenate(out, axis=-1)
    return flat.reshape(flat.shape[:-1] + (flat.shape[-1] // LANES, LANES))


def _unpack(flat, sizes, quantum):
    flat = flat.reshape(flat.shape[:-2] + (-1,))
    out, o = [], 0
    for n in sizes:
        out.append(flat[..., o:o + n])
        o += n + (-n) % quantum
    return out


def _prepare_weights(full, vec, dims):
    hm, hr, hn, rank = dims["hm"], dims["hr"], dims["hn"], dims["rank"]
    D, QR, KVR = dims["D"], dims["QR"], dims["KVR"]
    MW, RW, TAIL = hm * VDIM, hr * hn, dims["TAIL"]
    slabs = full["w_in"]
    c = slabs.shape[2]
    parts, pos = [], 0
    for orig_off, width, perm_off in sorted(dims["segs"], key=lambda t: t[2]):
        if perm_off > pos:
            parts.append(jnp.zeros((D, perm_off - pos), BF16))
        for k in range(N_DEV):
            lo, hi = max(orig_off, k * c), min(orig_off + width, (k + 1) * c)
            if lo < hi:
                parts.append(slabs[k][:, lo - k * c:hi - k * c])
        pos = perm_off + width
    if dims["d_in_perm"] > pos:
        parts.append(jnp.zeros((D, dims["d_in_perm"] - pos), BF16))
    w_in_p = jnp.concatenate(parts, axis=1)
    full = {n: (t if n == "w_in" else t.reshape(-1, t.shape[2]) if n in _ROW_SHARDED
                else t.transpose(1, 0, 2).reshape(t.shape[1], -1)) for n, t in full.items()}
    wq = full["mla_wq_b"].reshape(QR, hm, NOPE + ROPE)
    wq = jnp.concatenate([wq, jnp.zeros((QR, hm, QHEAD - NOPE - ROPE), BF16)], axis=2).reshape(QR, hm * QHEAD)
    wkv = full["mla_wkv_b"].reshape(KVR, hm, 2, NOPE).transpose(0, 2, 1, 3).reshape(KVR, 2 * hm * NOPE)
    z = lambda rows: jnp.zeros((rows, RW), F32)
    f = lambda nme: full[nme].astype(F32)
    w2cat = jnp.concatenate([
        jnp.concatenate([z(ROPE), f("rwkv_w2_f"), z(TAIL - ROPE - rank)], axis=0),
        jnp.concatenate([z(ROPE + rank), f("rwkv_w2_b"), z(TAIL - ROPE - 2 * rank)], axis=0)], axis=1)
    a2cat = jnp.concatenate([
        jnp.concatenate([z(ROPE + 2 * rank), f("rwkv_a2_f"), z(TAIL - ROPE - 3 * rank)], axis=0),
        jnp.concatenate([z(ROPE + 3 * rank), f("rwkv_a2_b"), z(TAIL - ROPE - 4 * rank)], axis=0)], axis=1)
    mu = vec["rwkv_mu"]
    mu_p = jnp.concatenate([mu[:3 * RW], jnp.zeros((ROPE,), F32), mu[3 * RW:],
                            jnp.zeros((TAIL - ROPE - 4 * rank,), F32)])
    row = lambda t: t.reshape(1, -1)
    return dict(
        w_in=w_in_p, wq_b=wq, wkv_b=wkv, w2cat=w2cat, a2cat=a2cat, mu=row(mu_p),
        w_br_mla=full["w_br_mla"], w_br_rwkv=full["w_br_rwkv"], w_out=full["w_out"],
        g_pre=row(vec["g_pre"]), g_post=row(vec["g_post"]), mla_q_norm=row(vec["mla_q_norm"]),
        mla_kv_norm=row(vec["mla_kv_norm"]), w0_f=row(vec["rwkv_w0_f"]), w0_b=row(vec["rwkv_w0_b"]),
        a0_f=row(vec["rwkv_a0_f"]), a0_b=row(vec["rwkv_a0_b"]), k_k=row(vec["rwkv_k_k"]), k_a=row(vec["rwkv_k_a"]),
        r_k=row(vec["rwkv_r_k"]), gn_g=row(vec["rwkv_gn_g"]), gn_b=row(vec["rwkv_gn_b"]))


def _restore_grads(g, dims):
    hm, hr, hn, rank = dims["hm"], dims["hr"], dims["hn"], dims["rank"]
    D, QR, KVR = dims["D"], dims["QR"], dims["KVR"]
    MW, RW, TAIL = hm * VDIM, hr * hn, dims["TAIL"]
    lay, _ = _layout(D, MW, RW, TAIL, QR, KVR)
    gw = g["w_in"]
    c = dims["d_in"] // N_DEV
    slabs = []
    for k in range(N_DEV):
        parts = []
        for orig_off, width, perm_off in sorted(dims["segs"]):
            lo_, hi_ = max(orig_off, k * c), min(orig_off + width, (k + 1) * c)
            if lo_ < hi_:
                parts.append(gw[:, perm_off + lo_ - orig_off:perm_off + hi_ - orig_off])
        slabs.append(jnp.concatenate(parts, axis=1))
    w_in = jnp.stack(slabs)
    wq = g["wq_b"].reshape(QR, hm, QHEAD)[:, :, :NOPE + ROPE].reshape(QR, hm * (NOPE + ROPE))
    wkv = g["wkv_b"].reshape(KVR, 2, hm, NOPE).transpose(0, 2, 1, 3).reshape(KVR, 2 * hm * NOPE)
    lo = lambda t, i, half: t[ROPE + i * rank:ROPE + (i + 1) * rank, half * RW:(half + 1) * RW].astype(BF16)
    cols = lambda t: t.reshape(t.shape[0], N_DEV, -1).transpose(1, 0, 2)
    mu = g["mu"][0]
    out = dict(
        w_in=w_in, mla_wq_b=cols(wq), mla_wkv_b=cols(wkv), rwkv_w2_f=cols(lo(g["w2cat"], 0, 0)),
        rwkv_w2_b=cols(lo(g["w2cat"], 1, 1)), rwkv_a2_f=cols(lo(g["a2cat"], 2, 0)),
        rwkv_a2_b=cols(lo(g["a2cat"], 3, 1)), w_br_mla=cols(g["w_br_mla"]), w_br_rwkv=cols(g["w_br_rwkv"]),
        w_out=g["w_out"].reshape(N_DEV, -1, g["w_out"].shape[1]),
        rwkv_mu=jnp.concatenate([mu[:3 * RW], mu[3 * RW + ROPE:3 * RW + ROPE + 4 * rank]]),
        g_pre=g["g_pre"][0], g_post=g["g_post"][0], mla_q_norm=g["mla_q_norm"][0], mla_kv_norm=g["mla_kv_norm"][0],
        rwkv_w0_f=g["w0_f"][0], rwkv_w0_b=g["w0_b"][0], rwkv_a0_f=g["a0_f"][0], rwkv_a0_b=g["a0_b"][0],
        rwkv_k_k=g["k_k"][0], rwkv_k_a=g["k_a"][0], rwkv_r_k=g["r_k"][0], rwkv_gn_g=g["gn_g"][0],
        rwkv_gn_b=g["gn_b"][0])
    return out


def _dims(inp):
    D = inp["x"].shape[-1]
    QR, KVR = inp["mla_q_norm"].shape[0], inp["mla_kv_norm"].shape[0]
    hm = inp["mla_wq_b"].shape[1] * N_DEV // (NOPE + ROPE)
    hr, hn = inp["rwkv_r_k"].shape
    rank = inp["rwkv_w2_f"].shape[0]
    MW, RW = hm * VDIM, hr * hn
    TAIL = -(-(ROPE + 4 * rank) // LANES) * LANES
    orig, o = {}, 0
    for nme, w in (("q_a", QR), ("kv_a", KVR), ("k_rope", ROPE), ("rkv", 3 * RW), ("lora", 4 * rank), ("z_m", MW),
                   ("z_r", RW), ("gate_m", D), ("gate_r", D)):
        orig[nme] = (o, w)
        o += w
    assert o == inp["w_in"].shape[1] * N_DEV
    lay, d_in_perm = _layout(D, MW, RW, TAIL, QR, KVR)
    perm_off = dict(q_a=lay["q_a"][0], kv_a=lay["kv_a"][0], k_rope=lay["tail"][0], rkv=lay["r"][0],
                    lora=lay["tail"][0] + ROPE, z_m=lay["z_m"][0], z_r=lay["z_r"][0], gate_m=lay["gate_m"][0],
                    gate_r=lay["gate_r"][0])
    segs = [(orig[nme][0], orig[nme][1], perm_off[nme]) for nme in orig]
    return dict(D=D, QR=QR, KVR=KVR, hm=hm, hr=hr, hn=hn, rank=rank, TAIL=TAIL, hb=min(hr, 16), segs=segs, d_in=o,
                d_in_perm=d_in_perm)


def kernel(x, g_pre, w_in, mla_q_norm, mla_wq_b, mla_kv_norm, mla_wkv_b, rwkv_mu, rwkv_w0_f, rwkv_w2_f, rwkv_w0_b, rwkv_w2_b, rwkv_a0_f, rwkv_a2_f, rwkv_a0_b, rwkv_a2_b, rwkv_k_k, rwkv_k_a, rwkv_r_k, rwkv_gn_g, rwkv_gn_b, w_br_mla, w_br_rwkv, w_out, g_post, loss_target, m_g_pre, m_w_in, m_mla_q_norm, m_mla_wq_b, m_mla_kv_norm, m_mla_wkv_b, m_rwkv_mu, m_rwkv_w0_f, m_rwkv_w2_f, m_rwkv_w0_b, m_rwkv_w2_b, m_rwkv_a0_f, m_rwkv_a2_f, m_rwkv_a0_b, m_rwkv_a2_b, m_rwkv_k_k, m_rwkv_k_a, m_rwkv_r_k, m_rwkv_gn_g, m_rwkv_gn_b, m_w_br_mla, m_w_br_rwkv, m_w_out, m_g_post, v_g_pre, v_w_in, v_mla_q_norm, v_mla_wq_b, v_mla_kv_norm, v_mla_wkv_b, v_rwkv_mu, v_rwkv_w0_f, v_rwkv_w2_f, v_rwkv_w0_b, v_rwkv_w2_b, v_rwkv_a0_f, v_rwkv_a2_f, v_rwkv_a0_b, v_rwkv_a2_b, v_rwkv_k_k, v_rwkv_k_a, v_rwkv_r_k, v_rwkv_gn_g, v_rwkv_gn_b, v_w_br_mla, v_w_br_rwkv, v_w_out, v_g_post):
    inp = dict(locals())
    dims = _dims(inp)
    slabs = _gather_two_level([inp[n].astype(BF16) for n in _MATS], name="gather_weights")
    W = _prepare_weights(dict(zip(_MATS, slabs)), {n: inp[n] for n in _VECS}, dims)
    loss, grad_x, g = _local_grads(x[0], loss_target[0], W, dims)
    loss = lax.psum(loss, ("x", "y", "c"))
    g = _restore_grads(g, dims)

    new = {}
    core = lax.axis_index("c")
    by_core = lambda t, cc: lax.dynamic_index_in_dim(t.reshape((4, 2) + t.shape[1:]), cc, axis=1, keepdims=False)
    keep = [by_core(g[n], core) for n in _MATS]
    got = _sibling_swap([by_core(g[n], 1 - core) for n in _MATS], name="pair_swap")
    sums = [_pair_add(a, b, name="pair_add_" + n) for n, a, b in zip(_MATS, keep, got)]
    recv = _chip_exchange(sums, name="scatter_grads")
    for n, t in zip(_MATS, recv):
        new[n] = _adamw(t, inp[n], inp["m_" + n], inp["v_" + n], name="adamw_" + n)

    vsizes = [inp[n].size for n in _VECS]
    vflat = lambda prefix, src: _pack([src[prefix + n].reshape(-1) for n in _VECS], F32, LANES * 8)
    (vrecv,) = _exchange([vflat("", g)], name="gather_vector_grads")
    vout = _adamw(vrecv, vflat("", inp), vflat("m_", inp), vflat("v_", inp), name="adamw_vectors")
    vparts = [_unpack(t, vsizes, LANES * 8) for t in vout]
    for i, n in enumerate(_VECS):
        new[n] = [vp[i].reshape(inp[n].shape) for vp in vparts]

    outs = [loss, grad_x[None]]
    for k in range(4):
        outs += [new[n][k] for n in _WEIGHTS]
    return tuple(outs)
```

```python
import functools
import math

import jax
import jax.numpy as jnp
from jax import lax
from jax.experimental import pallas as pl
from jax.experimental.pallas import tpu as pltpu

F32 = jnp.float32
BF16 = jnp.bfloat16

N_DEV = 8
LANES = 128
BF16_ROWS = 16
NOPE, ROPE, VDIM = 128, 64, 128
QHEAD = 256
ROPE_THETA = 10000.0
NORM_EPS = 1e-6
GN_EPS = 64e-5
CHUNK = 64
SUB = 16
VMEM_LIMIT = 56 * 1024 * 1024

ADAM_LR, ADAM_B1, ADAM_B2, ADAM_EPS, ADAM_WD, ADAM_STEP = 0.001, 0.9, 0.999, 1e-08, 0.01, 10


def _cparams(sem):
    return pltpu.CompilerParams(dimension_semantics=sem, vmem_limit_bytes=VMEM_LIMIT)


def _pick(n, cap):
    if n <= cap:
        return n
    for t in range(cap - cap % LANES, 0, -LANES):
        if n % t == 0:
            return t
    raise ValueError(f"no tile for {n} under {cap}")


def _mm(a, b, *, ta=False, tb=False, out_dtype=F32, name, tm_cap=1024, tn_cap=512, tk_cap=2048):
    K, M = a.shape if ta else a.shape[::-1]
    N = b.shape[0] if tb else b.shape[1]
    assert (b.shape[1] if tb else b.shape[0]) == K, (a.shape, b.shape, ta, tb)
    tm, tn, tk = _pick(M, tm_cap), _pick(N, tn_cap), _pick(K, tk_cap)
    nk = K // tk
    dn = (((0 if ta else 1,), (1 if tb else 0,)), ((), ()))

    def body(a_ref, b_ref, o_ref, acc_ref):
        k = pl.program_id(2)
        p = lax.dot_general(a_ref[...], b_ref[...], dn, preferred_element_type=F32)

        @pl.when(k == 0)
        def _():
            acc_ref[...] = p

        @pl.when(k > 0)
        def _():
            acc_ref[...] += p

        @pl.when(k == nk - 1)
        def _():
            o_ref[...] = acc_ref[...].astype(out_dtype)

    a_spec = pl.BlockSpec((tk, tm), lambda i, j, k: (k, i)) if ta else pl.BlockSpec((tm, tk), lambda i, j, k: (i, k))
    b_spec = pl.BlockSpec((tn, tk), lambda i, j, k: (j, k)) if tb else pl.BlockSpec((tk, tn), lambda i, j, k: (k, j))
    return pl.pallas_call(
        body, name=name, grid=(M // tm, N // tn, nk),
        in_specs=[a_spec, b_spec], out_specs=pl.BlockSpec((tm, tn), lambda i, j, k: (i, j)),
        out_shape=jax.ShapeDtypeStruct((M, N), out_dtype),
        scratch_shapes=[pltpu.VMEM((tm, tn), F32)],
        compiler_params=_cparams(("parallel", "parallel", "arbitrary")),
    )(a, b)


def _view(arr, off, width):
    assert off % width == 0, (off, width)
    return (arr, off // width, width)


def _rowwise(fn, rows, params, out_rows, out_accs=(), *, tile, name):
    rows = [r if isinstance(r, tuple) else (r, 0, r.shape[1]) for r in rows]
    S = rows[0][0].shape[0]
    T = min(tile, S)
    assert S % T == 0
    n_rows, n_par, n_out = len(rows), len(params), len(out_rows)
    into = [o[2] if len(o) == 3 else None for o in out_rows]
    carried = [t[0] for t in into if t is not None and t[0] is not None]

    def body(*refs):
        ins = [r[...] for r in refs[:n_rows + n_par]]
        outs = fn(*ins)
        out_refs = refs[n_rows + n_par + len(carried):]
        for o_ref, val in zip(out_refs[:n_out], outs[:n_out]):
            o_ref[...] = val.astype(o_ref.dtype)
        i = pl.program_id(0)
        for o_ref, val in zip(out_refs[n_out:], outs[n_out:]):
            @pl.when(i == 0)
            def _(o_ref=o_ref, val=val):
                o_ref[...] = val

            @pl.when(i > 0)
            def _(o_ref=o_ref, val=val):
                o_ref[...] += val

    in_specs = [pl.BlockSpec((T, w), functools.partial(lambda i, cb: (i, cb), cb=cb)) for _, cb, w in rows]
    in_specs += [pl.BlockSpec(p.shape, lambda i: (0, 0)) for p in params]
    in_specs += [pl.BlockSpec(memory_space=pl.ANY)] * len(carried)
    out_specs, out_shape, aliases = [], [], {}
    for k, (o, t) in enumerate(zip(out_rows, into)):
        w, dt = o[0], o[1]
        if t is None:
            out_specs.append(pl.BlockSpec((T, w), lambda i: (i, 0)))
            out_shape.append(jax.ShapeDtypeStruct((S, w), dt))
            continue
        buf, total, first = t
        assert first % w == 0
        out_specs.append(pl.BlockSpec((T, w), functools.partial(lambda i, cb: (i, cb), cb=first // w)))
        out_shape.append(jax.ShapeDtypeStruct((S, total), dt))
        if buf is not None:
            aliases[n_rows + n_par + len(aliases)] = k
    out_specs += [pl.BlockSpec(s, lambda i: (0, 0)) for s in out_accs]
    out_shape += [jax.ShapeDtypeStruct(s, F32) for s in out_accs]
    return pl.pallas_call(
        body, name=name, grid=(S // T,), in_specs=in_specs, out_specs=out_specs, out_shape=out_shape,
        input_output_aliases=aliases, compiler_params=_cparams(("arbitrary",)),
    )(*[r[0] for r in rows], *params, *carried)


def _split3(x):
    hi = x.astype(BF16)
    r1 = x - hi.astype(F32)
    mid = r1.astype(BF16)
    lo = (r1 - mid.astype(F32)).astype(BF16)
    return hi, mid, lo


def _mm_sel(x, sel):
    hi, mid, lo = _split3(x)
    d = lambda u: jnp.dot(u, sel, preferred_element_type=F32)
    return d(hi) + d(mid) + d(lo)


@jax.custom_vjp
def _sel(x, sel, sel_t):
    return _mm_sel(x, sel)


def _sel_fwd(x, sel, sel_t):
    return _mm_sel(x, sel), (sel, sel_t)


def _sel_bwd(res, ct):
    sel, sel_t = res
    return _mm_sel(ct, sel_t), jnp.zeros_like(sel), jnp.zeros_like(sel_t)


_sel.defvjp(_sel_fwd, _sel_bwd)


def _rms(x, g):
    return x * lax.rsqrt(jnp.mean(x * x, axis=-1, keepdims=True) + NORM_EPS) * g


def _sigmoid(x):
    return 1.0 / (1.0 + jnp.exp(-x))


def _silu(x):
    return x * _sigmoid(x)


def _softplus(x):
    return jnp.maximum(x, 0.0) + jnp.log(1.0 + jnp.exp(-jnp.abs(x)))


def _bdot(x, w):
    return jnp.dot(x.astype(BF16), w.astype(BF16), preferred_element_type=F32)


def _f_mla_norm(q_a, kv_a, qg, kvg):
    return _rms(q_a, qg), _rms(kv_a, kvg)


def _f_rope(hm, qraw, kr_in, cosx, sinx, rot, rot_t):
    def rope(t):
        return t * cosx + _sel(t, rot, rot_t) * sinx
    parts = []
    for h in range(hm):
        parts.append(qraw[:, h * QHEAD:h * QHEAD + NOPE])
        parts.append(rope(qraw[:, h * QHEAD + NOPE:(h + 1) * QHEAD]))
    return jnp.concatenate(parts, axis=1), rope(kr_in)


def _f_rwkv_pre(rw, k, tail, w0f, w0b, a0f, a0b, k_k, k_a, w2cat, a2cat, seg, seg_t):
    zw = _bdot(jnp.tanh(tail), w2cat)
    za = _bdot(tail, a2cat)
    lw_f = -jnp.exp(-_softplus(-(w0f + zw[:, :rw])) - 0.5)
    lw_b = -jnp.exp(-_softplus(-(w0b + zw[:, rw:])) - 0.5)
    a_f = _sigmoid(a0f + za[:, :rw])
    a_b = _sigmoid(a0b + za[:, rw:])
    kk = k * k_k
    nrm = jnp.sqrt(_sel(_sel(kk * kk, seg, seg_t), seg_t, seg))
    kk = kk / jnp.maximum(nrm, 1e-12)
    k_f = k * (1.0 + (a_f - 1.0) * k_a)
    k_b = k * (1.0 + (a_b - 1.0) * k_a)
    return lw_f, lw_b, k_f, k_b, -kk, kk * a_f, kk * a_b


def _f_post(hn, y_f, y_b, r, k_f, k_b, v, z_r, o_mla, z_m, gn_g, gn_b, r_k, seg, seg_t):
    segsum = lambda t: _sel(_sel(t, seg, seg_t), seg_t, seg)
    y = y_f + y_b
    mu = segsum(y) * (1.0 / hn)
    yc = y - mu
    var = segsum(yc * yc) * (1.0 / hn)
    yn = yc * lax.rsqrt(var + GN_EPS) * gn_g + gn_b
    bonus = segsum(r * (k_f + k_b) * r_k) * v
    return o_mla * _silu(z_m), (yn + bonus) * _silu(z_r)


def _f_merge(u_m, u_r, g_m, g_r):
    return _sigmoid(g_m) * u_m + _sigmoid(g_r) * u_r


_NN = ((2,), (1,))
_NT = ((2,), (2,))
_TN = ((1,), (1,))

_SCAN_PASSES = {"cum": 2, "gram": 3, "solve": 1, "apply": 1, "state": 1}


def _hdot_raw(passes, x, y, dims):
    dn = (dims, ((0,), (0,)))
    d = lambda p, q: lax.dot_general(p, q, dn, preferred_element_type=F32)
    xh = x.astype(BF16)
    yh = y.astype(BF16)
    if passes == 1:
        return d(xh, yh)
    yl = (y - yh.astype(F32)).astype(BF16)
    if passes == 2:
        return d(xh, yh) + d(xh, yl)
    xl = (x - xh.astype(F32)).astype(BF16)
    return d(xh, yh) + d(xh, yl) + d(xl, yh)


@functools.partial(jax.custom_vjp, nondiff_argnums=(2, 3))
def _hdot_p(x, y, dims, passes):
    return _hdot_raw(passes, x, y, dims)


def _hdot_fwd(x, y, dims, passes):
    return _hdot_raw(passes, x, y, dims), (x, y)


def _hdot_bwd(dims, passes, res, ct):
    x, y = res
    if dims == _NN:
        return _hdot_raw(passes, ct, y, _NT), _hdot_raw(passes, x, ct, _TN)
    if dims == _NT:
        return _hdot_raw(passes, ct, y, _NN), _hdot_raw(passes, ct, x, _TN)
    return _hdot_raw(passes, y, ct, _NT), _hdot_raw(passes, x, ct, _NN)


_hdot_p.defvjp(_hdot_fwd, _hdot_bwd)


def _hdot(x, y, dims, kind):
    return _hdot_p(x, y, dims, _SCAN_PASSES[kind])


def _tri_solve(n_mat, x, length):
    row = lax.broadcasted_iota(jnp.int32, (length, length), 0)
    col = lax.broadcasted_iota(jnp.int32, (length, length), 1)
    eye = (row == col).astype(F32)[None]
    diag_blk = ((row // SUB) == (col // SUB))[None]
    nd = jnp.where(diag_blk, n_mat, 0.0)
    no = n_mat - nd
    dinv = eye + nd
    p = nd
    for _ in range(int(math.log2(SUB)) - 1):
        p = _hdot(p, p, _NN, "solve")
        dinv = dinv + _hdot(dinv, p, _NN, "solve")
    q = _hdot(dinv, no, _NN, "solve")
    u = _hdot(dinv, x, _NN, "solve")
    levels = int(math.log2(length // SUB))
    qs = [q]
    for _ in range(levels - 1):
        qs.append(_hdot(qs[-1], qs[-1], _NN, "solve"))
    for qk in reversed(qs):
        u = u + _hdot(qk, u, _NN, "solve")
    return u


def _rwkv_chunk(rev, s0, r, lw, k, v, a, b):
    pairs, length, width = r.shape
    hn = width // 2
    row = lax.broadcasted_iota(jnp.int32, (length, length), 0)
    col = lax.broadcasted_iota(jnp.int32, (length, length), 1)
    incl = ((row <= col) if rev else (row >= col))[None]
    strict = ((row < col) if rev else (row > col))[None]
    lane = lax.broadcasted_iota(jnp.int32, (1, 1, width), 2)
    first = lane < hn
    head_mask = jnp.concatenate([jnp.broadcast_to(first.astype(F32), (pairs, 1, width)),
                                 jnp.broadcast_to(1.0 - first.astype(F32), (pairs, 1, width))], axis=0)
    twice = lambda t: jnp.concatenate([t, t], axis=0)
    pick = lambda t: jnp.where(first, t[:pairs], t[pairs:])

    t_incl = jnp.broadcast_to(incl.astype(F32), (pairs, length, length))
    cum = _hdot(t_incl, lw, _NN, "cum")
    g = jnp.exp(cum)
    g_inv = jnp.exp(-cum)
    at = a * jnp.exp(cum - lw)
    rt = r * g
    bt = b * g_inv
    kt = k * g_inv
    lhs = jnp.concatenate([twice(at) * head_mask, twice(rt) * head_mask], axis=1)
    rhs = jnp.concatenate([twice(bt), twice(kt)], axis=1)
    gram = _hdot(lhs, rhs, _NT, "gram")
    row2 = lax.broadcasted_iota(jnp.int32, (length, 2 * length), 0)
    col2 = lax.broadcasted_iota(jnp.int32, (length, 2 * length), 1)
    col2 = jnp.where(col2 >= length, col2 - length, col2)
    strict2 = ((row2 < col2) if rev else (row2 > col2))[None]
    incl2 = ((row2 <= col2) if rev else (row2 >= col2))[None]
    top = jnp.where(strict2, gram[:, :length], 0.0)
    bot = jnp.where(incl2, gram[:, length:], 0.0)
    v2 = twice(v)
    zeros = jnp.zeros_like(v2)
    x = _hdot(at, s0, _NT, "apply") + pick(_hdot(top, jnp.concatenate([zeros, v2], axis=1), _NN, "apply"))
    u = pick(_tri_solve(top[:, :, :length], twice(x), length))
    y = _hdot(rt, s0, _NT, "apply") + pick(_hdot(bot, jnp.concatenate([twice(u), v2], axis=1), _NN, "apply"))
    g_last = g[:, 0:1, :] if rev else g[:, length - 1:length, :]
    ri = lax.broadcasted_iota(jnp.int32, (width, width), 0)
    ci = lax.broadcasted_iota(jnp.int32, (width, width), 1)
    same_head = ((ri < hn) == (ci < hn))[None]
    upd = _hdot(u, bt, _TN, "state") + _hdot(v, kt, _TN, "state")
    s1 = (s0 + jnp.where(same_head, upd, 0.0)) * g_last
    return y, s1


def _split_pairs(x):
    return jnp.stack([x[:, p * LANES:(p + 1) * LANES] for p in range(x.shape[1] // LANES)])


def _merge_pairs(x):
    return jnp.concatenate([x[p] for p in range(x.shape[0])], axis=1)


def _scan_specs(views, rw, nc, rev):
    cidx = (lambda c: nc - 1 - c) if rev else (lambda c: c)
    seqs = [pl.BlockSpec((CHUNK, rw), functools.partial(lambda c, cb: (cidx(c), cb), cb=cb)) for _, cb, _ in views]
    plain = pl.BlockSpec((CHUNK, rw), lambda c: (cidx(c), 0))
    st = pl.BlockSpec((1, rw // LANES, LANES, LANES), lambda c: (cidx(c), 0, 0, 0))
    return seqs, plain, st


def _as_views(arrs, rw):
    return [t if isinstance(t, tuple) else (t, 0, rw) for t in arrs]


def _rwkv_scan_fwd(rev, ops, rw, *, name):
    views = _as_views(ops, rw)
    S = views[0][0].shape[0]
    nc, pairs = S // CHUNK, rw // LANES
    seqs, plain, st = _scan_specs(views, rw, nc, rev)

    def body(*refs):
        y_ref, st_ref, s_ref = refs[6:]

        @pl.when(pl.program_id(0) == 0)
        def _():
            s_ref[...] = jnp.zeros_like(s_ref)

        s0 = s_ref[...]
        st_ref[0] = s0
        y, s1 = _rwkv_chunk(rev, s0, *[_split_pairs(t[...]) for t in refs[:6]])
        y_ref[...] = _merge_pairs(y)
        s_ref[...] = s1

    return pl.pallas_call(
        body, name=name, grid=(nc,), in_specs=seqs, out_specs=[plain, st],
        out_shape=[jax.ShapeDtypeStruct((S, rw), F32), jax.ShapeDtypeStruct((nc, pairs, LANES, LANES), F32)],
        scratch_shapes=[pltpu.VMEM((pairs, LANES, LANES), F32)],
        compiler_params=_cparams(("arbitrary",)),
    )(*[t[0] for t in views])


def _rwkv_scan_bwd(rev, ops, states, dy, rw, *, name):
    views = _as_views(list(ops) + [dy], rw)
    S = views[0][0].shape[0]
    nc, pairs = S // CHUNK, rw // LANES
    seqs, plain, st = _scan_specs(views, rw, nc, not rev)

    def body(*refs):
        st_ref, out_refs, ds_ref = refs[7], refs[8:14], refs[14]

        @pl.when(pl.program_id(0) == 0)
        def _():
            ds_ref[...] = jnp.zeros_like(ds_ref)

        _, vjp = jax.vjp(functools.partial(_rwkv_chunk, rev), st_ref[0], *[_split_pairs(t[...]) for t in refs[:6]])
        grads = vjp((_split_pairs(refs[6][...]), ds_ref[...]))
        ds_ref[...] = grads[0]
        for o_ref, gval in zip(out_refs, grads[1:]):
            o_ref[...] = _merge_pairs(gval)

    return pl.pallas_call(
        body, name=name, grid=(nc,), in_specs=seqs + [st], out_specs=[plain] * 6,
        out_shape=[jax.ShapeDtypeStruct((S, rw), F32)] * 6,
        scratch_shapes=[pltpu.VMEM((pairs, LANES, LANES), F32)],
        compiler_params=_cparams(("arbitrary",)),
    )(*[t[0] for t in views], states)


def _shift_lerp(x_view, mu, d=None, into=None, *, name):
    arr, off, width = x_view
    S = arr.shape[0]
    cb = _pick(width, 256)
    assert off % cb == 0

    def cshift(t):
        rows = lax.broadcasted_iota(jnp.int32, t.shape, 0)
        prev = jnp.where(rows == 0, 0.0, pltpu.roll(t, 1, 0))
        nxt = jnp.where(rows == S - 1, 0.0, pltpu.roll(t, S - 1, 0))
        return 0.5 * (prev + nxt)

    def fwd_body(x_ref, mu_ref, o_ref):
        x = x_ref[...]
        o_ref[...] = x + mu_ref[...] * (cshift(x) - x)

    def bwd_body(x_ref, mu_ref, d_ref, _, dx_ref, dmu_ref):
        x, m, dd = x_ref[...], mu_ref[...], d_ref[...]
        gm = m * dd
        dx_ref[...] = (dd - gm + cshift(gm)).astype(dx_ref.dtype)
        dmu_ref[...] = jnp.sum(dd * (cshift(x) - x), axis=0, keepdims=True)

    x_spec = pl.BlockSpec((S, cb), lambda j: (0, off // cb + j))
    blk = pl.BlockSpec((S, cb), lambda j: (0, j))
    vec = pl.BlockSpec((1, cb), lambda j: (0, j))
    if d is None:
        return pl.pallas_call(
            fwd_body, name=name, grid=(width // cb,), in_specs=[x_spec, vec], out_specs=blk,
            out_shape=jax.ShapeDtypeStruct((S, width), F32), compiler_params=_cparams(("parallel",)),
        )(arr, mu)
    buf, first = into
    assert first % cb == 0
    return pl.pallas_call(
        bwd_body, name=name, grid=(width // cb,),
        in_specs=[x_spec, vec, blk, pl.BlockSpec(memory_space=pl.ANY)],
        out_specs=[pl.BlockSpec((S, cb), lambda j: (0, first // cb + j)), vec],
        out_shape=[jax.ShapeDtypeStruct(buf.shape, buf.dtype), jax.ShapeDtypeStruct((1, width), F32)],
        input_output_aliases={3: 0}, compiler_params=_cparams(("parallel",)),
    )(arr, mu, d, buf)


def _attention_fwd(qfull, kv, kr, hm, scale, *, tq, name):
    S = qfull.shape[0]

    def body(qn_ref, qr_ref, kn_ref, kr_ref, v_ref, o_ref, lse_ref):
        s = _attn_scores(qn_ref, qr_ref, kn_ref, kr_ref)
        m = jnp.max(s, axis=-1, keepdims=True)
        p = jnp.exp((s - m) * scale)
        l = jnp.sum(p, axis=-1, keepdims=True)
        o_ref[...] = jnp.dot(p.astype(BF16), v_ref[...], preferred_element_type=F32) * (1.0 / l)
        lse_ref[...] = jnp.broadcast_to(m * scale + jnp.log(l), lse_ref.shape)

    oblk = pl.BlockSpec((tq, VDIM), lambda h, i: (i, h))
    return pl.pallas_call(
        body, name=name, grid=(hm, S // tq),
        in_specs=[pl.BlockSpec((tq, NOPE), lambda h, i: (i, 2 * h)),
                  pl.BlockSpec((tq, NOPE), lambda h, i: (i, 2 * h + 1)),
                  pl.BlockSpec((S, NOPE), lambda h, i: (0, h)),
                  pl.BlockSpec((S, LANES), lambda h, i: (0, 0)),
                  pl.BlockSpec((S, VDIM), lambda h, i: (0, hm + h))],
        out_specs=[oblk, oblk],
        out_shape=[jax.ShapeDtypeStruct((S, hm * VDIM), F32)] * 2,
        compiler_params=_cparams(("parallel", "parallel")),
    )(qfull, qfull, kv, kr, kv)


def _attn_scores(qn_ref, qr_ref, kn_ref, kr_ref):
    nt = (((1,), (1,)), ((), ()))
    return (lax.dot_general(qn_ref[...], kn_ref[...], nt, preferred_element_type=F32)
            + lax.dot_general(qr_ref[...], kr_ref[...], nt, preferred_element_type=F32))


def _attention_bwd(qfull, kv, kr, o, lse, d_o, hm, scale, *, tq, name):
    S = qfull.shape[0]
    tn = (((0,), (0,)), ((), ()))
    nt = (((1,), (1,)), ((), ()))

    def body(qn_ref, qr_ref, kn_ref, kr_ref, v_ref, o_ref, lse_ref, do_ref,
             dqn_ref, dqr_ref, dkn_ref, dv_ref, dkr_ref):
        s = _attn_scores(qn_ref, qr_ref, kn_ref, kr_ref)
        p = jnp.exp(s * scale - lse_ref[:, 0:1])
        d_out = do_ref[...]
        delta = jnp.sum(d_out * o_ref[...], axis=-1, keepdims=True)
        d_out = d_out.astype(BF16)
        dp = lax.dot_general(d_out, v_ref[...], nt, preferred_element_type=F32)
        ds = (p * ((dp - delta) * scale)).astype(BF16)
        dqn_ref[...] = jnp.dot(ds, kn_ref[...], preferred_element_type=F32)
        dqr_ref[...] = jnp.dot(ds, kr_ref[...], preferred_element_type=F32)
        dv = lax.dot_general(p.astype(BF16), d_out, tn, preferred_element_type=F32)
        dkn = lax.dot_general(ds, qn_ref[...], tn, preferred_element_type=F32)
        dkr = lax.dot_general(ds, qr_ref[...], tn, preferred_element_type=F32)
        first = pl.program_id(1) == 0
        for ref, val in ((dkn_ref, dkn), (dv_ref, dv), (dkr_ref, dkr)):
            @pl.when(first)
            def _(ref=ref, val=val):
                ref[...] = val

            @pl.when(jnp.logical_not(first))
            def _(ref=ref, val=val):
                ref[...] += val

    qblk = pl.BlockSpec((tq, NOPE), lambda h, i: (i, h))
    kblk = pl.BlockSpec((S, NOPE), lambda h, i: (0, h))
    shp = jax.ShapeDtypeStruct((S, hm * NOPE), F32)
    return pl.pallas_call(
        body, name=name, grid=(hm, S // tq),
        in_specs=[pl.BlockSpec((tq, NOPE), lambda h, i: (i, 2 * h)),
                  pl.BlockSpec((tq, NOPE), lambda h, i: (i, 2 * h + 1)),
                  kblk,
                  pl.BlockSpec((S, LANES), lambda h, i: (0, 0)),
                  pl.BlockSpec((S, VDIM), lambda h, i: (0, hm + h)),
                  qblk, qblk, qblk],
        out_specs=[qblk, qblk, kblk, kblk, kblk],
        out_shape=[shp] * 5,
        compiler_params=_cparams(("parallel", "arbitrary")),
    )(qfull, qfull, kv, kr, kv, o, lse, d_o)


def _layout(D, MW, RW, TAIL, QR, KVR):
    names = ["gate_m", "gate_r", "z_m", "z_r", "q_a", "kv_a", "r", "k", "v", "tail"]
    widths = [D, D, MW, RW, QR, KVR, RW, RW, RW, TAIL]
    offs, o = {}, 0
    for nme, w in zip(names, widths):
        assert o % w == 0, (nme, o, w)
        offs[nme] = (o, w)
        o += w
    return offs, o


def _local_grads(x, target, W, dims):
    S, D = x.shape
    hm, hr, hn, rank = dims["hm"], dims["hr"], dims["hn"], dims["rank"]
    MW, RW = hm * VDIM, hr * hn
    TAIL = W["w2cat"].shape[0]
    QR, KVR = W["mla_q_norm"].shape[1], W["mla_kv_norm"].shape[1]
    lay, d_in = _layout(D, MW, RW, TAIL, QR, KVR)
    T = 256
    scale = (NOPE + ROPE) ** -0.5
    col = lambda arr, nme: _view(arr, *lay[nme])

    pos = jnp.arange(S, dtype=F32)
    inv_freq = jnp.power(ROPE_THETA, -jnp.arange(0, ROPE, 2, dtype=F32) / ROPE)
    ang = pos[:, None] * inv_freq[None, :]
    zpad = jnp.zeros((S, LANES - ROPE), F32)
    cosx = jnp.concatenate([jnp.cos(ang), jnp.cos(ang), zpad], axis=1)
    sinx = jnp.concatenate([jnp.sin(ang), jnp.sin(ang), zpad], axis=1)
    ri, ci = jnp.arange(LANES)[:, None], jnp.arange(LANES)[None, :]
    half = ROPE // 2
    rot = (jnp.where((ri == ci - half) & (ci >= half) & (ci < ROPE), 1.0, 0.0)
           - jnp.where((ri == ci + half) & (ci < half), 1.0, 0.0)).astype(BF16)
    rot_t = rot.T
    seg = (jnp.arange(RW)[:, None] // hn == jnp.arange(LANES)[None, :]).astype(BF16)
    seg_t = seg.T

    (h,) = _rowwise(lambda xb, g: (_rms(xb, g),), [x], [W["g_pre"]], [(D, BF16)], tile=T, name="pre_norm")
    proj = _mm(h, W["w_in_t"], tb=True, name="in_proj")

    qn, kvn = _rowwise(_f_mla_norm, [col(proj, "q_a"), col(proj, "kv_a")], [W["mla_q_norm"], W["mla_kv_norm"]],
                       [(QR, BF16), (KVR, BF16)], tile=T, name="mla_norm")
    qraw = _mm(qn, W["wq_b_t"], tb=True, name="q_up")
    kv = _mm(kvn, W["wkv_b"], out_dtype=BF16, name="kv_up")
    kr_view = _view(proj, lay["tail"][0], LANES)
    qfull, kr = _rowwise(functools.partial(_f_rope, hm), [qraw, kr_view, cosx, sinx], [rot, rot_t],
                         [(hm * QHEAD, BF16), (LANES, BF16)], tile=T, name="rope")
    o_mla, lse = _attention_fwd(qfull, kv, kr, hm, scale, tq=T, name="attn_fwd")

    shift_view = (proj, lay["r"][0], 3 * RW + TAIL)
    rl = _shift_lerp(shift_view, W["mu"], name="shift_fwd")
    rl_r, rl_k, rl_v = _view(rl, 0, RW), _view(rl, RW, RW), _view(rl, 2 * RW, RW)
    rl_tail = _view(rl, 3 * RW, TAIL)
    pre_params = [W["w0_f"], W["w0_b"], W["a0_f"], W["a0_b"], W["k_k"], W["k_a"], W["w2cat"], W["a2cat"], seg, seg_t]
    pre_fn = functools.partial(_f_rwkv_pre, RW)
    lw_f, lw_b, k_f, k_b, a_n, b_f, b_b = _rowwise(pre_fn, [rl_k, rl_tail], pre_params, [(RW, F32)] * 7, tile=T,
                                                    name="rwkv_pre")
    dirs = {}
    for tag, rev, lw, kd, bd in (("f", False, lw_f, k_f, b_f), ("b", True, lw_b, k_b, b_b)):
        ops = (rl_r, lw, kd, rl_v, a_n, bd)
        y_d, st = _rwkv_scan_fwd(rev, ops, RW, name="scan_fwd_" + tag)
        dirs[tag] = (rev, ops, st, y_d)
    y_f, y_b = dirs["f"][3], dirs["b"][3]

    post_fn = functools.partial(_f_post, hn)
    post_rows = [y_f, y_b, rl_r, k_f, k_b, rl_v, col(proj, "z_r"), o_mla, col(proj, "z_m")]
    post_params = [W["gn_g"], W["gn_b"], W["r_k"], seg, seg_t]
    ymg, yrg = _rowwise(post_fn, post_rows, post_params, [(MW, BF16), (RW, BF16)], tile=T, name="post")
    u_m = _mm(ymg, W["w_br_mla"], name="br_mla")
    u_r = _mm(yrg, W["w_br_rwkv"], name="br_rwkv")
    merge_rows = [u_m, u_r, col(proj, "gate_m"), col(proj, "gate_r")]
    (merged,) = _rowwise(lambda *t: (_f_merge(*t),), merge_rows, [], [(D, BF16)], tile=T, name="merge")
    out = _mm(merged, W["w_out"], name="out_proj")

    def head(ob, xb, tb, g):
        yn, vjp = jax.vjp(_rms, ob, g)
        err = xb + yn - tb
        dy = err * (1.0 / D)
        d_ob, d_g = vjp(dy)
        loss = jnp.broadcast_to(0.5 * jnp.sum(err * err) * (1.0 / D), (1, LANES))
        return dy, d_ob, loss, d_g

    dy, d_out, loss, g_g_post = _rowwise(head, [out, x, target], [W["g_post"]], [(D, F32), (D, BF16)],
                                         [(1, LANES), (1, D)], tile=T, name="head")
    d_merged = _mm(d_out, W["w_out"], tb=True, name="d_merged")
    g_w_out = _mm(merged, d_out, ta=True, out_dtype=BF16, name="g_w_out")

    def merge_bwd(u_m_b, u_r_b, g_m_b, g_r_b, dm):
        _, vjp = jax.vjp(_f_merge, u_m_b, u_r_b, g_m_b, g_r_b)
        du_m, du_r, dg_m, dg_r = vjp(dm)
        return du_m, du_r, jnp.concatenate([dg_m, dg_r], axis=1)

    d_u_m, d_u_r, d_proj = _rowwise(merge_bwd, merge_rows + [d_merged], [],
                                    [(D, BF16), (D, BF16), (2 * D, BF16, (None, d_in, lay["gate_m"][0]))], tile=T,
                                    name="merge_bwd")
    d_ymg = _mm(d_u_m, W["w_br_mla"], tb=True, name="d_ymg")
    d_yrg = _mm(d_u_r, W["w_br_rwkv"], tb=True, name="d_yrg")
    g_w_br_mla = _mm(ymg, d_u_m, ta=True, out_dtype=BF16, name="g_w_br_mla")
    g_w_br_rwkv = _mm(yrg, d_u_r, ta=True, out_dtype=BF16, name="g_w_br_rwkv")

    def post_bwd(*args):
        nr = len(post_rows)
        prim, dm, dr = args[:nr] + args[nr + 2:], args[nr], args[nr + 1]
        _, vjp = jax.vjp(post_fn, *prim)
        g = vjp((dm, dr))
        return g[0], g[2], g[3], g[5], g[7], jnp.concatenate([g[8], g[6]], axis=1), g[9], g[10], g[11]

    (d_y, d_r_bonus, d_k_bonus, d_v_bonus, d_o, d_proj, g_gn_g, g_gn_b, g_r_k) = _rowwise(
        post_bwd, post_rows + [d_ymg, d_yrg], post_params,
        [(RW, F32), (RW, F32), (RW, F32), (RW, F32), (MW, F32), (MW + RW, BF16, (d_proj, d_in, lay["z_m"][0]))],
        [(1, RW)] * 3, tile=T // 2, name="post_bwd")

    dsc = {}
    for tag in ("f", "b"):
        rev, ops, st, _ = dirs[tag]
        dsc[tag] = _rwkv_scan_bwd(rev, ops, st, d_y, RW, name="scan_bwd_" + tag)

    d_qn, d_qr, d_kn, d_v_att, d_kr_h = _attention_bwd(qfull, kv, kr, o_mla, lse, d_o, hm, scale, tq=T, name="attn_bwd")

    def rope_bwd(qraw_b, kr_in, cos_b, sin_b, dqn_b, dqr_b, dkn_b, dv_b, dkrh_b, rot_b, rot_t_b):
        _, vjp = jax.vjp(lambda q_, k_: _f_rope(hm, q_, k_, cos_b, sin_b, rot_b, rot_t_b), qraw_b, kr_in)
        parts = []
        for hh in range(hm):
            parts += [dqn_b[:, hh * NOPE:(hh + 1) * NOPE], dqr_b[:, hh * NOPE:(hh + 1) * NOPE]]
        dkr = dkrh_b[:, :LANES]
        for hh in range(1, hm):
            dkr = dkr + dkrh_b[:, hh * LANES:(hh + 1) * LANES]
        d_qraw, d_kr_in = vjp((jnp.concatenate(parts, axis=1), dkr))
        return d_qraw, jnp.concatenate([dkn_b, dv_b], axis=1), d_kr_in

    d_qraw, d_kv, d_kr_in = _rowwise(rope_bwd, [qraw, kr_view, cosx, sinx, d_qn, d_qr, d_kn, d_v_att, d_kr_h],
                                     [rot, rot_t], [(hm * QHEAD, BF16), (2 * MW, BF16), (LANES, F32)], tile=T,
                                     name="rope_bwd")
    d_qnorm = _mm(d_qraw, W["wq_b_t"], name="d_qn")
    d_kvnorm = _mm(d_kv, W["wkv_b"], tb=True, name="d_kvn")
    g_wq_b = _mm(d_qraw, qn, ta=True, out_dtype=BF16, name="g_wq_b")
    g_wkv_b = _mm(kvn, d_kv, ta=True, out_dtype=BF16, name="g_wkv_b")

    def mla_norm_bwd(q_a, kv_a, qg, kvg, dq, dk):
        _, vjp = jax.vjp(_f_mla_norm, q_a, kv_a, qg, kvg)
        d_q_a, d_kv_a, d_qg, d_kvg = vjp((dq, dk))
        return jnp.concatenate([d_q_a, d_kv_a], axis=1), d_qg, d_kvg

    d_proj, g_q_norm, g_kv_norm = _rowwise(
        lambda q_a, kv_a, dq, dk, qg, kvg: mla_norm_bwd(q_a, kv_a, qg, kvg, dq, dk),
        [col(proj, "q_a"), col(proj, "kv_a"), d_qnorm, d_kvnorm], [W["mla_q_norm"], W["mla_kv_norm"]],
        [(QR + KVR, BF16, (d_proj, d_in, lay["q_a"][0]))], [(1, QR), (1, KVR)], tile=T, name="mla_norm_bwd")

    def pre_bwd(k_b_, tail_b, dlwf, dlwb, dkf, dkb, dkbon, daf, dab, dbf, dbb, drf, drb, drbon, dvf, dvb, dvbon,
                dkr, *params):
        _, vjp = jax.vjp(pre_fn, k_b_, tail_b, *params[:8], params[8], params[9])
        g = vjp((dlwf, dlwb, dkf + dkbon, dkb + dkbon, daf + dab, dbf, dbb))
        d_tail = g[1] + jnp.concatenate([dkr, jnp.zeros((dkr.shape[0], TAIL - LANES), F32)], axis=1)
        d_rl = jnp.concatenate([drf + drb + drbon, g[0], dvf + dvb + dvbon, d_tail], axis=1)
        return (d_rl,) + tuple(g[2:10])

    f_, b_ = dsc["f"], dsc["b"]
    pre_bwd_rows = [rl_k, rl_tail, f_[1], b_[1], f_[2], b_[2], d_k_bonus, f_[4], b_[4], f_[5], b_[5],
                    f_[0], b_[0], d_r_bonus, f_[3], b_[3], d_v_bonus, d_kr_in]
    (d_rl, g_w0_f, g_w0_b, g_a0_f, g_a0_b, g_k_k, g_k_a, g_w2cat, g_a2cat) = _rowwise(
        pre_bwd, pre_bwd_rows, pre_params, [(3 * RW + TAIL, F32)],
        [(1, RW)] * 6 + [(TAIL, 2 * RW)] * 2, tile=T // 2, name="rwkv_pre_bwd")
    d_proj, g_mu = _shift_lerp(shift_view, W["mu"], d_rl, (d_proj, lay["r"][0]), name="shift_bwd")
    d_h = _mm(d_proj, W["w_in_t"], tn_cap=1024, name="d_h")
    g_w_in = _mm(d_proj, h, ta=True, out_dtype=BF16, name="g_w_in")

    def pre_norm_bwd(xb, dyb, dhb, g):
        _, vjp = jax.vjp(_rms, xb, g)
        dx, dg = vjp(dhb)
        return dyb + dx, dg

    grad_x, g_g_pre = _rowwise(pre_norm_bwd, [x, dy, d_h], [W["g_pre"]], [(D, F32)], [(1, D)], tile=T,
                               name="pre_norm_bwd")

    grads = dict(g_pre=g_g_pre, w_in=g_w_in, mla_q_norm=g_q_norm, wq_b=g_wq_b, mla_kv_norm=g_kv_norm,
                 wkv_b=g_wkv_b, mu=g_mu, w0_f=g_w0_f, w0_b=g_w0_b, a0_f=g_a0_f, a0_b=g_a0_b, k_k=g_k_k, k_a=g_k_a,
                 w2cat=g_w2cat, a2cat=g_a2cat, r_k=g_r_k, gn_g=g_gn_g, gn_b=g_gn_b, w_br_mla=g_w_br_mla,
                 w_br_rwkv=g_w_br_rwkv, w_out=g_w_out, g_post=g_g_post)
    return loss[0, 0], grad_x, grads


_MATS = ["w_in", "mla_wq_b", "mla_wkv_b", "rwkv_w2_f", "rwkv_w2_b", "rwkv_a2_f", "rwkv_a2_b", "w_br_mla",
         "w_br_rwkv", "w_out"]
_ROW_SHARDED = ("w_out",)
_TRANSPOSED = ("w_in", "mla_wq_b")
_VECS = ["g_pre", "mla_q_norm", "mla_kv_norm", "rwkv_mu", "rwkv_w0_f", "rwkv_w0_b", "rwkv_a0_f", "rwkv_a0_b",
         "rwkv_k_k", "rwkv_k_a", "rwkv_r_k", "rwkv_gn_g", "rwkv_gn_b", "g_post"]
_WEIGHTS = ["g_pre", "w_in", "mla_q_norm", "mla_wq_b", "mla_kv_norm", "mla_wkv_b", "rwkv_mu", "rwkv_w0_f",
            "rwkv_w2_f", "rwkv_w0_b", "rwkv_w2_b", "rwkv_a0_f", "rwkv_a2_f", "rwkv_a0_b", "rwkv_a2_b", "rwkv_k_k",
            "rwkv_k_a", "rwkv_r_k", "rwkv_gn_g", "rwkv_gn_b", "w_br_mla", "w_br_rwkv", "w_out", "g_post"]

def _exchange(srcs, *, name):
    n = len(srcs)

    def body(*refs):
        src_refs, out_refs = refs[:n], refs[n:2 * n]
        send_sems, recv_sems, local_sems = refs[2 * n:]
        x, y, c = lax.axis_index("x"), lax.axis_index("y"), lax.axis_index("c")
        me = 4 * x + 2 * y + c
        flip = lambda v, bit: (1 - v) if bit else v

        def piece(a, idx):
            return src_refs[a] if srcs[a].ndim == 2 else src_refs[a].at[idx]

        owns = [pltpu.make_async_copy(piece(a, me), out_refs[a].at[me], local_sems.at[a]) for a in range(n)]
        for cp in owns:
            cp.start()
        sends, peers = [], []
        for d in range(1, N_DEV):
            px, py, pc = flip(x, d & 4), flip(y, d & 2), flip(c, d & 1)
            pidx = 4 * px + 2 * py + pc
            peers.append(((px, py, pc), pidx))
            for a in range(n):
                cp = pltpu.make_async_remote_copy(
                    src_ref=piece(a, pidx), dst_ref=out_refs[a].at[me], send_sem=send_sems.at[d - 1, a],
                    recv_sem=recv_sems.at[d - 1, a], device_id=(px, py, pc), device_id_type=pl.DeviceIdType.MESH)
                cp.start()
                sends.append(cp)
        for d, (peer, pidx) in zip(range(1, N_DEV), peers):
            for a in range(n):
                pltpu.make_async_remote_copy(
                    src_ref=piece(a, pidx), dst_ref=out_refs[a].at[pidx], send_sem=send_sems.at[d - 1, a],
                    recv_sem=recv_sems.at[d - 1, a], device_id=peer, device_id_type=pl.DeviceIdType.MESH).wait_recv()
        for cp in sends:
            cp.wait_send()
        for cp in owns:
            cp.wait()

    return pl.pallas_call(
        body, name=name,
        out_shape=[jax.ShapeDtypeStruct((N_DEV,) + s.shape[-2:], s.dtype) for s in srcs],
        in_specs=[pl.BlockSpec(memory_space=pl.ANY)] * n, out_specs=[pl.BlockSpec(memory_space=pl.ANY)] * n,
        scratch_shapes=[pltpu.SemaphoreType.DMA((N_DEV - 1, n)), pltpu.SemaphoreType.DMA((N_DEV - 1, n)),
                        pltpu.SemaphoreType.DMA((n,))],
    )(*srcs)


def _remote(src, dst, sems, key, to):
    send_sems, recv_sems = sems
    return pltpu.make_async_remote_copy(src_ref=src, dst_ref=dst, send_sem=send_sems.at[key], recv_sem=recv_sems.at[key],
                                        device_id=to, device_id_type=pl.DeviceIdType.MESH)


def _hbm_call(body, srcs, out_shapes, sem_shapes, *, name):
    n = len(srcs)
    return pl.pallas_call(
        body, name=name, out_shape=out_shapes,
        in_specs=[pl.BlockSpec(memory_space=pl.ANY)] * n,
        out_specs=[pl.BlockSpec(memory_space=pl.ANY)] * len(out_shapes),
        scratch_shapes=[pltpu.SemaphoreType.DMA(s) for s in sem_shapes],
    )(*srcs)


def _gather_two_level(srcs, *, name):
    n = len(srcs)

    def body(*refs):
        src_refs, out_refs = refs[:n], refs[n:2 * n]
        sems, local_sems = refs[2 * n:2 * n + 2], refs[2 * n + 2]
        x, y, c = lax.axis_index("x"), lax.axis_index("y"), lax.axis_index("c")
        idx = lambda px, py, pc: 4 * px + 2 * py + pc
        me, sibling = (x, y, c), (x, y, 1 - c)
        chips = [(1 - x, y), (x, 1 - y), (1 - x, 1 - y)]
        owns = [pltpu.make_async_copy(src_refs[a], out_refs[a].at[idx(*me)], local_sems.at[a]) for a in range(n)]
        for cp in owns:
            cp.start()
        sends = []
        for a in range(n):
            sends.append(_remote(src_refs[a], out_refs[a].at[idx(*me)], sems, (0, a), sibling))
            for j, chip in enumerate(chips):
                sends.append(_remote(src_refs[a], out_refs[a].at[idx(*me)], sems, (1 + j, a), (*chip, c)))
        for cp in sends:
            cp.start()
        for j, chip in enumerate(chips):
            for a in range(n):
                blk = out_refs[a].at[idx(*chip, c)]
                _remote(blk, blk, sems, (1 + j, a), me).wait_recv()
                fwd = _remote(blk, blk, sems, (4 + j, a), sibling)
                fwd.start()
                sends.append(fwd)
        for a in range(n):
            blk = out_refs[a].at[idx(*sibling)]
            _remote(blk, blk, sems, (0, a), me).wait_recv()
            for j, chip in enumerate(chips):
                blk = out_refs[a].at[idx(*chip, 1 - c)]
                _remote(blk, blk, sems, (4 + j, a), me).wait_recv()
        for cp in sends:
            cp.wait_send()
        for cp in owns:
            cp.wait()

    return _hbm_call(body, srcs, [jax.ShapeDtypeStruct((N_DEV,) + s.shape, s.dtype) for s in srcs],
                     [(7, n), (7, n), (n,)], name=name)


def _sibling_swap(srcs, *, name):
    n = len(srcs)

    def body(*refs):
        src_refs, out_refs, sems = refs[:n], refs[n:2 * n], refs[2 * n:]
        x, y, c = lax.axis_index("x"), lax.axis_index("y"), lax.axis_index("c")
        copies = [_remote(src_refs[a].at[2 * q + 1 - c], out_refs[a].at[q], sems, (q, a), (x, y, 1 - c))
                  for a in range(n) for q in range(4)]
        for cp in copies:
            cp.start()
        for cp in copies:
            cp.wait()

    return _hbm_call(body, srcs, [jax.ShapeDtypeStruct((4,) + s.shape[1:], s.dtype) for s in srcs],
                     [(4, n), (4, n)], name=name)


def _chip_exchange(srcs, *, name):
    n = len(srcs)

    def body(*refs):
        src_refs, out_refs = refs[:n], refs[n:2 * n]
        sems, local_sems = refs[2 * n:2 * n + 2], refs[2 * n + 2]
        x, y, c = lax.axis_index("x"), lax.axis_index("y"), lax.axis_index("c")
        mine = 2 * x + y
        chips = [(1 - x, y), (x, 1 - y), (1 - x, 1 - y)]
        owns = [pltpu.make_async_copy(src_refs[a].at[mine], out_refs[a].at[mine], local_sems.at[a]) for a in range(n)]
        for cp in owns:
            cp.start()
        sends = [_remote(src_refs[a].at[2 * px + py], out_refs[a].at[mine], sems, (j, a), (px, py, c))
                 for j, (px, py) in enumerate(chips) for a in range(n)]
        for cp in sends:
            cp.start()
        for j, (px, py) in enumerate(chips):
            for a in range(n):
                blk = out_refs[a].at[2 * px + py]
                _remote(blk, blk, sems, (j, a), (x, y, c)).wait_recv()
        for cp in sends:
            cp.wait_send()
        for cp in owns:
            cp.wait()

    return _hbm_call(body, srcs, [jax.ShapeDtypeStruct(s.shape, s.dtype) for s in srcs], [(3, n), (3, n), (n,)],
                     name=name)


def _pair_add(core, g, got, *, name):
    q, r, c = got.shape
    tr, tc = _tile2d(r, c)

    def body(core_ref, a_ref, b_ref, o_ref):
        o_ref[...] = (a_ref[...].astype(F32) + b_ref[...].astype(F32)).astype(BF16)

    blk = pl.BlockSpec((1, tr, tc), lambda i, j, k, core_ref: (i, j, k))
    mine = pl.BlockSpec((1, tr, tc), lambda i, j, k, core_ref: (2 * i + core_ref[0], j, k))
    return pl.pallas_call(
        body, name=name, out_shape=jax.ShapeDtypeStruct(got.shape, BF16),
        grid_spec=pltpu.PrefetchScalarGridSpec(num_scalar_prefetch=1, grid=(q, r // tr, c // tc),
                                               in_specs=[mine, blk], out_specs=blk),
        compiler_params=_cparams(("parallel", "parallel", "parallel")))(core, g, got)


def _adamw(recv, w, m, v, *, name):
    r, c = w.shape
    n_terms = recv.shape[0]
    tr, tc = _tile2d(r, c)

    def body(g_ref, w_ref, m_ref, v_ref, go_ref, d_ref, mo_ref, vo_ref):
        g = g_ref[0].astype(F32)
        for k in range(1, n_terms):
            g = g + g_ref[k].astype(F32)
        m_new = ADAM_B1 * m_ref[...] + (1.0 - ADAM_B1) * g
        v_new = ADAM_B2 * v_ref[...] + (1.0 - ADAM_B2) * (g * g)
        m_hat = m_new / (1.0 - ADAM_B1 ** ADAM_STEP)
        v_hat = v_new / (1.0 - ADAM_B2 ** ADAM_STEP)
        go_ref[...] = g
        d_ref[...] = -ADAM_LR * (m_hat / (jnp.sqrt(v_hat) + ADAM_EPS) + ADAM_WD * w_ref[...])
        mo_ref[...] = m_new
        vo_ref[...] = v_new

    blk = pl.BlockSpec((tr, tc), lambda i, j: (i, j))
    return pl.pallas_call(
        body, name=name, grid=(r // tr, c // tc),
        in_specs=[pl.BlockSpec((n_terms, tr, tc), lambda i, j: (0, i, j)), blk, blk, blk], out_specs=[blk] * 4,
        out_shape=[jax.ShapeDtypeStruct((r, c), F32)] * 4, compiler_params=_cparams(("parallel", "parallel")),
    )(recv, w, m, v)


def _tile2d(r, c, cap=256):
    if r <= cap:
        return r, c
    for t in range(cap, 0, -BF16_ROWS):
        if r % t == 0:
            return t, c
    return r, _pick(c, cap)


def _pack(pieces, dtype, quantum):
    out = []
    for p in pieces:
        lead, n = p.shape[:-1], p.shape[-1]
        pad = (-n) % quantum
        p = p.astype(dtype)
        if pad:
            p = jnp.concatenate([p, jnp.zeros(lead + (pad,), dtype)], axis=-1)
        out.append(p)
    flat = jnp.concatenate(out, axis=-1)
    return flat.reshape(flat.shape[:-1] + (flat.shape[-1] // LANES, LANES))


def _unpack(flat, sizes, quantum):
    flat = flat.reshape(flat.shape[:-2] + (-1,))
    out, o = [], 0
    for n in sizes:
        out.append(flat[..., o:o + n])
        o += n + (-n) % quantum
    return out


def _prepare_weights(full, vec, dims):
    hm, hr, hn, rank = dims["hm"], dims["hr"], dims["hn"], dims["rank"]
    D, QR, KVR = dims["D"], dims["QR"], dims["KVR"]
    MW, RW, TAIL = hm * VDIM, hr * hn, dims["TAIL"]
    slabs = full["w_in"]
    c = slabs.shape[1]
    parts, pos = [], 0
    for orig_off, width, perm_off in sorted(dims["segs"], key=lambda t: t[2]):
        if perm_off > pos:
            parts.append(jnp.zeros((perm_off - pos, D), BF16))
        for k in range(N_DEV):
            lo, hi = max(orig_off, k * c), min(orig_off + width, (k + 1) * c)
            if lo < hi:
                parts.append(slabs[k][lo - k * c:hi - k * c])
        pos = perm_off + width
    if dims["d_in_perm"] > pos:
        parts.append(jnp.zeros((dims["d_in_perm"] - pos, D), BF16))
    w_in_t = jnp.concatenate(parts, axis=0)
    full = {n: (t if n == "w_in" else t.reshape(-1, t.shape[2]) if n in _ROW_SHARDED + _TRANSPOSED
                else t.transpose(1, 0, 2).reshape(t.shape[1], -1)) for n, t in full.items()}
    wq = full["mla_wq_b"].reshape(hm, NOPE + ROPE, QR)
    wq = jnp.concatenate([wq, jnp.zeros((hm, QHEAD - NOPE - ROPE, QR), BF16)], axis=1).reshape(hm * QHEAD, QR)
    wkv = full["mla_wkv_b"].reshape(KVR, hm, 2, NOPE).transpose(0, 2, 1, 3).reshape(KVR, 2 * hm * NOPE)
    z = lambda rows: jnp.zeros((rows, RW), F32)
    f = lambda nme: full[nme].astype(F32)
    w2cat = jnp.concatenate([
        jnp.concatenate([z(ROPE), f("rwkv_w2_f"), z(TAIL - ROPE - rank)], axis=0),
        jnp.concatenate([z(ROPE + rank), f("rwkv_w2_b"), z(TAIL - ROPE - 2 * rank)], axis=0)], axis=1)
    a2cat = jnp.concatenate([
        jnp.concatenate([z(ROPE + 2 * rank), f("rwkv_a2_f"), z(TAIL - ROPE - 3 * rank)], axis=0),
        jnp.concatenate([z(ROPE + 3 * rank), f("rwkv_a2_b"), z(TAIL - ROPE - 4 * rank)], axis=0)], axis=1)
    mu = vec["rwkv_mu"]
    mu_p = jnp.concatenate([mu[:3 * RW], jnp.zeros((ROPE,), F32), mu[3 * RW:],
                            jnp.zeros((TAIL - ROPE - 4 * rank,), F32)])
    row = lambda t: t.reshape(1, -1)
    return dict(
        w_in_t=w_in_t, wq_b_t=wq, wkv_b=wkv, w2cat=w2cat, a2cat=a2cat, mu=row(mu_p),
        w_br_mla=full["w_br_mla"], w_br_rwkv=full["w_br_rwkv"], w_out=full["w_out"],
        g_pre=row(vec["g_pre"]), g_post=row(vec["g_post"]), mla_q_norm=row(vec["mla_q_norm"]),
        mla_kv_norm=row(vec["mla_kv_norm"]), w0_f=row(vec["rwkv_w0_f"]), w0_b=row(vec["rwkv_w0_b"]),
        a0_f=row(vec["rwkv_a0_f"]), a0_b=row(vec["rwkv_a0_b"]), k_k=row(vec["rwkv_k_k"]), k_a=row(vec["rwkv_k_a"]),
        r_k=row(vec["rwkv_r_k"]), gn_g=row(vec["rwkv_gn_g"]), gn_b=row(vec["rwkv_gn_b"]))


def _restore_grads(g, dims):
    hm, hr, hn, rank = dims["hm"], dims["hr"], dims["hn"], dims["rank"]
    D, QR, KVR = dims["D"], dims["QR"], dims["KVR"]
    MW, RW, TAIL = hm * VDIM, hr * hn, dims["TAIL"]
    lay, _ = _layout(D, MW, RW, TAIL, QR, KVR)
    gw = g["w_in"]
    c = dims["d_in"] // N_DEV
    slabs = []
    for k in range(N_DEV):
        parts = []
        for orig_off, width, perm_off in sorted(dims["segs"]):
            lo_, hi_ = max(orig_off, k * c), min(orig_off + width, (k + 1) * c)
            if lo_ < hi_:
                parts.append(gw[perm_off + lo_ - orig_off:perm_off + hi_ - orig_off])
        slabs.append(jnp.concatenate(parts, axis=0))
    w_in = jnp.stack(slabs)
    wq = g["wq_b"].reshape(hm, QHEAD, QR)[:, :NOPE + ROPE].reshape(N_DEV, -1, QR)
    wkv = g["wkv_b"].reshape(KVR, 2, hm, NOPE).transpose(0, 2, 1, 3).reshape(KVR, 2 * hm * NOPE)
    lo = lambda t, i, half: t[ROPE + i * rank:ROPE + (i + 1) * rank, half * RW:(half + 1) * RW].astype(BF16)
    cols = lambda t: t.reshape(t.shape[0], N_DEV, -1).transpose(1, 0, 2)
    mu = g["mu"][0]
    out = dict(
        w_in=w_in, mla_wq_b=wq, mla_wkv_b=cols(wkv), rwkv_w2_f=cols(lo(g["w2cat"], 0, 0)),
        rwkv_w2_b=cols(lo(g["w2cat"], 1, 1)), rwkv_a2_f=cols(lo(g["a2cat"], 2, 0)),
        rwkv_a2_b=cols(lo(g["a2cat"], 3, 1)), w_br_mla=cols(g["w_br_mla"]), w_br_rwkv=cols(g["w_br_rwkv"]),
        w_out=g["w_out"].reshape(N_DEV, -1, g["w_out"].shape[1]),
        rwkv_mu=jnp.concatenate([mu[:3 * RW], mu[3 * RW + ROPE:3 * RW + ROPE + 4 * rank]]),
        g_pre=g["g_pre"][0], g_post=g["g_post"][0], mla_q_norm=g["mla_q_norm"][0], mla_kv_norm=g["mla_kv_norm"][0],
        rwkv_w0_f=g["w0_f"][0], rwkv_w0_b=g["w0_b"][0], rwkv_a0_f=g["a0_f"][0], rwkv_a0_b=g["a0_b"][0],
        rwkv_k_k=g["k_k"][0], rwkv_k_a=g["k_a"][0], rwkv_r_k=g["r_k"][0], rwkv_gn_g=g["gn_g"][0],
        rwkv_gn_b=g["gn_b"][0])
    return out


def _dims(inp):
    D = inp["x"].shape[-1]
    QR, KVR = inp["mla_q_norm"].shape[0], inp["mla_kv_norm"].shape[0]
    hm = inp["mla_wq_b"].shape[1] * N_DEV // (NOPE + ROPE)
    hr, hn = inp["rwkv_r_k"].shape
    rank = inp["rwkv_w2_f"].shape[0]
    MW, RW = hm * VDIM, hr * hn
    TAIL = -(-(ROPE + 4 * rank) // LANES) * LANES
    orig, o = {}, 0
    for nme, w in (("q_a", QR), ("kv_a", KVR), ("k_rope", ROPE), ("rkv", 3 * RW), ("lora", 4 * rank), ("z_m", MW),
                   ("z_r", RW), ("gate_m", D), ("gate_r", D)):
        orig[nme] = (o, w)
        o += w
    assert o == inp["w_in"].shape[1] * N_DEV
    lay, d_in_perm = _layout(D, MW, RW, TAIL, QR, KVR)
    perm_off = dict(q_a=lay["q_a"][0], kv_a=lay["kv_a"][0], k_rope=lay["tail"][0], rkv=lay["r"][0],
                    lora=lay["tail"][0] + ROPE, z_m=lay["z_m"][0], z_r=lay["z_r"][0], gate_m=lay["gate_m"][0],
                    gate_r=lay["gate_r"][0])
    segs = [(orig[nme][0], orig[nme][1], perm_off[nme]) for nme in orig]
    return dict(D=D, QR=QR, KVR=KVR, hm=hm, hr=hr, hn=hn, rank=rank, TAIL=TAIL, hb=min(hr, 16), segs=segs, d_in=o,
                d_in_perm=d_in_perm)


def kernel(x, g_pre, w_in, mla_q_norm, mla_wq_b, mla_kv_norm, mla_wkv_b, rwkv_mu, rwkv_w0_f, rwkv_w2_f, rwkv_w0_b, rwkv_w2_b, rwkv_a0_f, rwkv_a2_f, rwkv_a0_b, rwkv_a2_b, rwkv_k_k, rwkv_k_a, rwkv_r_k, rwkv_gn_g, rwkv_gn_b, w_br_mla, w_br_rwkv, w_out, g_post, loss_target, m_g_pre, m_w_in, m_mla_q_norm, m_mla_wq_b, m_mla_kv_norm, m_mla_wkv_b, m_rwkv_mu, m_rwkv_w0_f, m_rwkv_w2_f, m_rwkv_w0_b, m_rwkv_w2_b, m_rwkv_a0_f, m_rwkv_a2_f, m_rwkv_a0_b, m_rwkv_a2_b, m_rwkv_k_k, m_rwkv_k_a, m_rwkv_r_k, m_rwkv_gn_g, m_rwkv_gn_b, m_w_br_mla, m_w_br_rwkv, m_w_out, m_g_post, v_g_pre, v_w_in, v_mla_q_norm, v_mla_wq_b, v_mla_kv_norm, v_mla_wkv_b, v_rwkv_mu, v_rwkv_w0_f, v_rwkv_w2_f, v_rwkv_w0_b, v_rwkv_w2_b, v_rwkv_a0_f, v_rwkv_a2_f, v_rwkv_a0_b, v_rwkv_a2_b, v_rwkv_k_k, v_rwkv_k_a, v_rwkv_r_k, v_rwkv_gn_g, v_rwkv_gn_b, v_w_br_mla, v_w_br_rwkv, v_w_out, v_g_post):
    inp = dict(locals())
    dims = _dims(inp)
    stored = lambda t, n: t.T if n in _TRANSPOSED else t
    slabs = _gather_two_level([stored(inp[n], n).astype(BF16) for n in _MATS], name="gather_weights")
    W = _prepare_weights(dict(zip(_MATS, slabs)), {n: inp[n] for n in _VECS}, dims)
    loss, grad_x, g = _local_grads(x[0], loss_target[0], W, dims)
    loss = lax.psum(loss, ("x", "y", "c"))
    g = _restore_grads(g, dims)

    new = {}
    core = lax.axis_index("c").astype(jnp.int32).reshape(1)
    got = _sibling_swap([g[n] for n in _MATS], name="pair_swap")
    sums = [_pair_add(core, g[n], t, name="pair_add_" + n) for n, t in zip(_MATS, got)]
    recv = _chip_exchange(sums, name="scatter_grads")
    for n, t in zip(_MATS, recv):
        out = _adamw(t, stored(inp[n], n), stored(inp["m_" + n], n), stored(inp["v_" + n], n), name="adamw_" + n)
        new[n] = [stored(o, n) for o in out]

    vsizes = [inp[n].size for n in _VECS]
    vflat = lambda prefix, src: _pack([src[prefix + n].reshape(-1) for n in _VECS], F32, LANES * 8)
    (vrecv,) = _exchange([vflat("", g)], name="gather_vector_grads")
    vout = _adamw(vrecv, vflat("", inp), vflat("m_", inp), vflat("v_", inp), name="adamw_vectors")
    vparts = [_unpack(t, vsizes, LANES * 8) for t in vout]
    for i, n in enumerate(_VECS):
        new[n] = [vp[i].reshape(inp[n].shape) for vp in vparts]

    outs = [loss, grad_x[None]]
    for k in range(4):
        outs += [new[n][k] for n in _WEIGHTS]
    return tuple(outs)
```

```python
import functools
import math

import jax
import jax.numpy as jnp
from jax import lax
from jax.experimental import pallas as pl
from jax.experimental.pallas import tpu as pltpu

F32 = jnp.float32
BF16 = jnp.bfloat16

N_DEV = 8
LANES = 128
BF16_ROWS = 16
NOPE, ROPE, VDIM = 128, 64, 128
QHEAD = 256
ROPE_THETA = 10000.0
NORM_EPS = 1e-6
GN_EPS = 64e-5
CHUNK = 64
SUB = 16
VMEM_LIMIT = 56 * 1024 * 1024

ADAM_LR, ADAM_B1, ADAM_B2, ADAM_EPS, ADAM_WD, ADAM_STEP = 0.001, 0.9, 0.999, 1e-08, 0.01, 10


def _cparams(sem):
    return pltpu.CompilerParams(dimension_semantics=sem, vmem_limit_bytes=VMEM_LIMIT)


def _pick(n, cap):
    if n <= cap:
        return n
    for t in range(cap - cap % LANES, 0, -LANES):
        if n % t == 0:
            return t
    raise ValueError(f"no tile for {n} under {cap}")


def _mm(a, b, *, ta=False, tb=False, out_dtype=F32, name, tm_cap=1024, tn_cap=512, tk_cap=2048):
    K, M = a.shape if ta else a.shape[::-1]
    N = b.shape[0] if tb else b.shape[1]
    assert (b.shape[1] if tb else b.shape[0]) == K, (a.shape, b.shape, ta, tb)
    tm, tn, tk = _pick(M, tm_cap), _pick(N, tn_cap), _pick(K, tk_cap)
    nk = K // tk
    dn = (((0 if ta else 1,), (1 if tb else 0,)), ((), ()))

    def body(a_ref, b_ref, o_ref, acc_ref):
        k = pl.program_id(2)
        p = lax.dot_general(a_ref[...], b_ref[...], dn, preferred_element_type=F32)

        @pl.when(k == 0)
        def _():
            acc_ref[...] = p

        @pl.when(k > 0)
        def _():
            acc_ref[...] += p

        @pl.when(k == nk - 1)
        def _():
            o_ref[...] = acc_ref[...].astype(out_dtype)

    a_spec = pl.BlockSpec((tk, tm), lambda i, j, k: (k, i)) if ta else pl.BlockSpec((tm, tk), lambda i, j, k: (i, k))
    b_spec = pl.BlockSpec((tn, tk), lambda i, j, k: (j, k)) if tb else pl.BlockSpec((tk, tn), lambda i, j, k: (k, j))
    return pl.pallas_call(
        body, name=name, grid=(M // tm, N // tn, nk),
        in_specs=[a_spec, b_spec], out_specs=pl.BlockSpec((tm, tn), lambda i, j, k: (i, j)),
        out_shape=jax.ShapeDtypeStruct((M, N), out_dtype),
        scratch_shapes=[pltpu.VMEM((tm, tn), F32)],
        compiler_params=_cparams(("parallel", "parallel", "arbitrary")),
    )(a, b)


def _view(arr, off, width):
    assert off % width == 0, (off, width)
    return (arr, off // width, width)


def _rowwise(fn, rows, params, out_rows, out_accs=(), *, tile, name):
    rows = [r if isinstance(r, tuple) else (r, 0, r.shape[1]) for r in rows]
    S = rows[0][0].shape[0]
    T = min(tile, S)
    assert S % T == 0
    n_rows, n_par, n_out = len(rows), len(params), len(out_rows)
    into = [o[2] if len(o) == 3 else None for o in out_rows]
    carried = [t[0] for t in into if t is not None and t[0] is not None]

    def body(*refs):
        ins = [r[...] for r in refs[:n_rows + n_par]]
        outs = fn(*ins)
        out_refs = refs[n_rows + n_par + len(carried):]
        for o_ref, val in zip(out_refs[:n_out], outs[:n_out]):
            o_ref[...] = val.astype(o_ref.dtype)
        i = pl.program_id(0)
        for o_ref, val in zip(out_refs[n_out:], outs[n_out:]):
            @pl.when(i == 0)
            def _(o_ref=o_ref, val=val):
                o_ref[...] = val

            @pl.when(i > 0)
            def _(o_ref=o_ref, val=val):
                o_ref[...] += val

    in_specs = [pl.BlockSpec((T, w), functools.partial(lambda i, cb: (i, cb), cb=cb)) for _, cb, w in rows]
    in_specs += [pl.BlockSpec(p.shape, lambda i: (0, 0)) for p in params]
    in_specs += [pl.BlockSpec(memory_space=pl.ANY)] * len(carried)
    out_specs, out_shape, aliases = [], [], {}
    for k, (o, t) in enumerate(zip(out_rows, into)):
        w, dt = o[0], o[1]
        if t is None:
            out_specs.append(pl.BlockSpec((T, w), lambda i: (i, 0)))
            out_shape.append(jax.ShapeDtypeStruct((S, w), dt))
            continue
        buf, total, first = t
        assert first % w == 0
        out_specs.append(pl.BlockSpec((T, w), functools.partial(lambda i, cb: (i, cb), cb=first // w)))
        out_shape.append(jax.ShapeDtypeStruct((S, total), dt))
        if buf is not None:
            aliases[n_rows + n_par + len(aliases)] = k
    out_specs += [pl.BlockSpec(s, lambda i: (0, 0)) for s in out_accs]
    out_shape += [jax.ShapeDtypeStruct(s, F32) for s in out_accs]
    return pl.pallas_call(
        body, name=name, grid=(S // T,), in_specs=in_specs, out_specs=out_specs, out_shape=out_shape,
        input_output_aliases=aliases, compiler_params=_cparams(("arbitrary",)),
    )(*[r[0] for r in rows], *params, *carried)


def _split3(x):
    hi = x.astype(BF16)
    r1 = x - hi.astype(F32)
    mid = r1.astype(BF16)
    lo = (r1 - mid.astype(F32)).astype(BF16)
    return hi, mid, lo


def _mm_sel(x, sel):
    hi, mid, lo = _split3(x)
    d = lambda u: jnp.dot(u, sel, preferred_element_type=F32)
    return d(hi) + d(mid) + d(lo)


@jax.custom_vjp
def _sel(x, sel, sel_t):
    return _mm_sel(x, sel)


def _sel_fwd(x, sel, sel_t):
    return _mm_sel(x, sel), (sel, sel_t)


def _sel_bwd(res, ct):
    sel, sel_t = res
    return _mm_sel(ct, sel_t), jnp.zeros_like(sel), jnp.zeros_like(sel_t)


_sel.defvjp(_sel_fwd, _sel_bwd)


def _rms(x, g):
    return x * lax.rsqrt(jnp.mean(x * x, axis=-1, keepdims=True) + NORM_EPS) * g


def _sigmoid(x):
    return 1.0 / (1.0 + jnp.exp(-x))


def _silu(x):
    return x * _sigmoid(x)


def _softplus(x):
    return jnp.maximum(x, 0.0) + jnp.log(1.0 + jnp.exp(-jnp.abs(x)))


def _bdot(x, w):
    return jnp.dot(x.astype(BF16), w.astype(BF16), preferred_element_type=F32)


def _f_mla_norm(q_a, kv_a, qg, kvg):
    return _rms(q_a, qg), _rms(kv_a, kvg)


def _f_rope(hm, qraw, kr_in, cosx, sinx, rot, rot_t):
    def rope(t):
        return t * cosx + _sel(t, rot, rot_t) * sinx
    parts = []
    for h in range(hm):
        parts.append(qraw[:, h * QHEAD:h * QHEAD + NOPE])
        parts.append(rope(qraw[:, h * QHEAD + NOPE:(h + 1) * QHEAD]))
    return jnp.concatenate(parts, axis=1), rope(kr_in)


def _f_rwkv_pre(rw, k, tail, w0f, w0b, a0f, a0b, k_k, k_a, w2cat, a2cat, seg, seg_t):
    zw = _bdot(jnp.tanh(tail), w2cat)
    za = _bdot(tail, a2cat)
    lw_f = -jnp.exp(-_softplus(-(w0f + zw[:, :rw])) - 0.5)
    lw_b = -jnp.exp(-_softplus(-(w0b + zw[:, rw:])) - 0.5)
    a_f = _sigmoid(a0f + za[:, :rw])
    a_b = _sigmoid(a0b + za[:, rw:])
    kk = k * k_k
    nrm = jnp.sqrt(_sel(_sel(kk * kk, seg, seg_t), seg_t, seg))
    kk = kk / jnp.maximum(nrm, 1e-12)
    k_f = k * (1.0 + (a_f - 1.0) * k_a)
    k_b = k * (1.0 + (a_b - 1.0) * k_a)
    return lw_f, lw_b, k_f, k_b, -kk, kk * a_f, kk * a_b


def _f_post(hn, y_f, y_b, r, k_f, k_b, v, z_r, o_mla, z_m, gn_g, gn_b, r_k, seg, seg_t):
    segsum = lambda t: _sel(_sel(t, seg, seg_t), seg_t, seg)
    y = y_f + y_b
    mu = segsum(y) * (1.0 / hn)
    yc = y - mu
    var = segsum(yc * yc) * (1.0 / hn)
    yn = yc * lax.rsqrt(var + GN_EPS) * gn_g + gn_b
    bonus = segsum(r * (k_f + k_b) * r_k) * v
    return o_mla * _silu(z_m), (yn + bonus) * _silu(z_r)


def _f_merge(u_m, u_r, g_m, g_r):
    return _sigmoid(g_m) * u_m + _sigmoid(g_r) * u_r


_NN = ((2,), (1,))
_NT = ((2,), (2,))
_TN = ((1,), (1,))

_SCAN_PASSES = {"cum": 2, "gram": 3, "solve": 1, "apply": 1, "state": 1}


def _hdot_raw(passes, x, y, dims):
    dn = (dims, ((0,), (0,)))
    d = lambda p, q: lax.dot_general(p, q, dn, preferred_element_type=F32)
    xh = x.astype(BF16)
    yh = y.astype(BF16)
    if passes == 1:
        return d(xh, yh)
    yl = (y - yh.astype(F32)).astype(BF16)
    if passes == 2:
        return d(xh, yh) + d(xh, yl)
    xl = (x - xh.astype(F32)).astype(BF16)
    return d(xh, yh) + d(xh, yl) + d(xl, yh)


@functools.partial(jax.custom_vjp, nondiff_argnums=(2, 3))
def _hdot_p(x, y, dims, passes):
    return _hdot_raw(passes, x, y, dims)


def _hdot_fwd(x, y, dims, passes):
    return _hdot_raw(passes, x, y, dims), (x, y)


def _hdot_bwd(dims, passes, res, ct):
    x, y = res
    if dims == _NN:
        return _hdot_raw(passes, ct, y, _NT), _hdot_raw(passes, x, ct, _TN)
    if dims == _NT:
        return _hdot_raw(passes, ct, y, _NN), _hdot_raw(passes, ct, x, _TN)
    return _hdot_raw(passes, y, ct, _NT), _hdot_raw(passes, x, ct, _NN)


_hdot_p.defvjp(_hdot_fwd, _hdot_bwd)


def _hdot(x, y, dims, kind):
    return _hdot_p(x, y, dims, _SCAN_PASSES[kind])


def _tri_solve(n_mat, x, length):
    row = lax.broadcasted_iota(jnp.int32, (length, length), 0)
    col = lax.broadcasted_iota(jnp.int32, (length, length), 1)
    eye = (row == col).astype(F32)[None]
    diag_blk = ((row // SUB) == (col // SUB))[None]
    nd = jnp.where(diag_blk, n_mat, 0.0)
    no = n_mat - nd
    dinv = eye + nd
    p = nd
    for _ in range(int(math.log2(SUB)) - 1):
        p = _hdot(p, p, _NN, "solve")
        dinv = dinv + _hdot(dinv, p, _NN, "solve")
    q = _hdot(dinv, no, _NN, "solve")
    u = _hdot(dinv, x, _NN, "solve")
    levels = int(math.log2(length // SUB))
    qs = [q]
    for _ in range(levels - 1):
        qs.append(_hdot(qs[-1], qs[-1], _NN, "solve"))
    for qk in reversed(qs):
        u = u + _hdot(qk, u, _NN, "solve")
    return u


def _rwkv_chunk(rev, s0, r, lw, k, v, a, b):
    pairs, length, width = r.shape
    hn = width // 2
    row = lax.broadcasted_iota(jnp.int32, (length, length), 0)
    col = lax.broadcasted_iota(jnp.int32, (length, length), 1)
    row2 = lax.broadcasted_iota(jnp.int32, (length, 2 * length), 0)
    col2 = lax.broadcasted_iota(jnp.int32, (length, 2 * length), 1)
    col2 = jnp.where(col2 >= length, col2 - length, col2)
    if rev is None:
        half = pairs // 2
        back = lax.broadcasted_iota(jnp.int32, (pairs, length, length), 0) >= half
        idx2 = lax.broadcasted_iota(jnp.int32, (2 * pairs, length, 2 * length), 0)
        back2 = ((idx2 >= half) & (idx2 < pairs)) | (idx2 >= pairs + half)
        ahead = jnp.where(back, (col - row)[None], (row - col)[None])
        ahead2 = jnp.where(back2, (col2 - row2)[None], (row2 - col2)[None])
        incl, strict2, incl2 = ahead >= 0, ahead2 > 0, ahead2 >= 0
    else:
        incl = ((row <= col) if rev else (row >= col))[None]
        strict2 = ((row2 < col2) if rev else (row2 > col2))[None]
        incl2 = ((row2 <= col2) if rev else (row2 >= col2))[None]
    lane = lax.broadcasted_iota(jnp.int32, (1, 1, width), 2)
    first = lane < hn
    head_mask = jnp.concatenate([jnp.broadcast_to(first.astype(F32), (pairs, 1, width)),
                                 jnp.broadcast_to(1.0 - first.astype(F32), (pairs, 1, width))], axis=0)
    twice = lambda t: jnp.concatenate([t, t], axis=0)
    pick = lambda t: jnp.where(first, t[:pairs], t[pairs:])

    t_incl = jnp.broadcast_to(incl.astype(F32), (pairs, length, length))
    cum = _hdot(t_incl, lw, _NN, "cum")
    g = jnp.exp(cum)
    g_inv = jnp.exp(-cum)
    at = a * jnp.exp(cum - lw)
    rt = r * g
    bt = b * g_inv
    kt = k * g_inv
    lhs = jnp.concatenate([twice(at) * head_mask, twice(rt) * head_mask], axis=1)
    rhs = jnp.concatenate([twice(bt), twice(kt)], axis=1)
    gram = _hdot(lhs, rhs, _NT, "gram")
    top = jnp.where(strict2, gram[:, :length], 0.0)
    bot = jnp.where(incl2, gram[:, length:], 0.0)
    v2 = twice(v)
    zeros = jnp.zeros_like(v2)
    x = _hdot(at, s0, _NT, "apply") + pick(_hdot(top, jnp.concatenate([zeros, v2], axis=1), _NN, "apply"))
    u = pick(_tri_solve(top[:, :, :length], twice(x), length))
    y = _hdot(rt, s0, _NT, "apply") + pick(_hdot(bot, jnp.concatenate([twice(u), v2], axis=1), _NN, "apply"))
    g_last = jnp.exp(jnp.sum(lw, axis=1, keepdims=True))
    ri = lax.broadcasted_iota(jnp.int32, (width, width), 0)
    ci = lax.broadcasted_iota(jnp.int32, (width, width), 1)
    same_head = ((ri < hn) == (ci < hn))[None]
    upd = _hdot(u, bt, _TN, "state") + _hdot(v, kt, _TN, "state")
    s1 = (s0 + jnp.where(same_head, upd, 0.0)) * g_last
    return y, s1


def _split_pairs(x):
    return jnp.stack([x[:, p * LANES:(p + 1) * LANES] for p in range(x.shape[1] // LANES)])


def _merge_pairs(x):
    return jnp.concatenate([x[p] for p in range(x.shape[0])], axis=1)


def _scan_specs(views, rw, nc, rev):
    cidx = (lambda c: nc - 1 - c) if rev else (lambda c: c)
    seqs = [pl.BlockSpec((CHUNK, rw), functools.partial(lambda c, cb: (cidx(c), cb), cb=cb)) for _, cb, _ in views]
    plain = pl.BlockSpec((CHUNK, rw), lambda c: (cidx(c), 0))
    st = pl.BlockSpec((1, rw // LANES, LANES, LANES), lambda c: (cidx(c), 0, 0, 0))
    return seqs, plain, st


def _as_views(arrs, rw):
    return [t if isinstance(t, tuple) else (t, 0, rw) for t in arrs]


def _rwkv_scan_fwd(ops_f, ops_b, rw, *, name):
    S = _as_views(ops_f, rw)[0][0].shape[0]
    nc, pairs = S // CHUNK, rw // LANES
    in_specs, out_specs, arrays = [], [], []
    for rev, ops in ((False, ops_f), (True, ops_b)):
        views = _as_views(ops, rw)
        seqs, plain, st = _scan_specs(views, rw, nc, rev)
        in_specs += seqs
        out_specs += [plain, st]
        arrays += [t[0] for t in views]

    def both(refs_f, refs_b):
        return [jnp.concatenate([_split_pairs(f[...]), _split_pairs(b[...])], axis=0) for f, b in zip(refs_f, refs_b)]

    def body(*refs):
        (y_f, st_f, y_b, st_b), s_ref = refs[12:16], refs[16]

        @pl.when(pl.program_id(0) == 0)
        def _():
            s_ref[...] = jnp.zeros_like(s_ref)

        s0 = s_ref[...]
        st_f[0] = s0[:pairs]
        st_b[0] = s0[pairs:]
        y, s1 = _rwkv_chunk(None, s0, *both(refs[:6], refs[6:12]))
        y_f[...] = _merge_pairs(y[:pairs])
        y_b[...] = _merge_pairs(y[pairs:])
        s_ref[...] = s1

    return pl.pallas_call(
        body, name=name, grid=(nc,), in_specs=in_specs, out_specs=out_specs,
        out_shape=[jax.ShapeDtypeStruct((S, rw), F32), jax.ShapeDtypeStruct((nc, pairs, LANES, LANES), F32)] * 2,
        scratch_shapes=[pltpu.VMEM((2 * pairs, LANES, LANES), F32)],
        compiler_params=_cparams(("arbitrary",)),
    )(*arrays)


def _rwkv_scan_bwd(ops_f, ops_b, states_f, states_b, dy, rw, *, name):
    S = dy.shape[0]
    nc, pairs = S // CHUNK, rw // LANES
    in_specs, arrays = [], []
    for rev, ops, states in ((False, ops_f, states_f), (True, ops_b, states_b)):
        views = _as_views(list(ops) + [dy], rw)
        seqs, plain, st = _scan_specs(views, rw, nc, not rev)
        in_specs += seqs + [st]
        arrays += [t[0] for t in views] + [states]
    out_specs = []
    for rev in (False, True):
        out_specs += [_scan_specs([], rw, nc, not rev)[1]] * 6

    def both(refs_f, refs_b):
        return [jnp.concatenate([_split_pairs(f[...]), _split_pairs(b[...])], axis=0) for f, b in zip(refs_f, refs_b)]

    def body(*refs):
        ds_ref = refs[28]

        @pl.when(pl.program_id(0) == 0)
        def _():
            ds_ref[...] = jnp.zeros_like(ds_ref)

        s0 = jnp.concatenate([refs[7][0], refs[15][0]], axis=0)
        _, vjp = jax.vjp(functools.partial(_rwkv_chunk, None), s0, *both(refs[:6], refs[8:14]))
        (dy,) = both(refs[6:7], refs[14:15])
        grads = vjp((dy, ds_ref[...]))
        ds_ref[...] = grads[0]
        for o_f, o_b, gval in zip(refs[16:22], refs[22:28], grads[1:]):
            o_f[...] = _merge_pairs(gval[:pairs])
            o_b[...] = _merge_pairs(gval[pairs:])

    return pl.pallas_call(
        body, name=name, grid=(nc,), in_specs=in_specs, out_specs=out_specs,
        out_shape=[jax.ShapeDtypeStruct((S, rw), F32)] * 12,
        scratch_shapes=[pltpu.VMEM((2 * pairs, LANES, LANES), F32)],
        compiler_params=_cparams(("arbitrary",)),
    )(*arrays)


def _shift_lerp(x_view, mu, d=None, into=None, *, name):
    arr, off, width = x_view
    S = arr.shape[0]
    cb = _pick(width, 256)
    assert off % cb == 0

    def cshift(t):
        rows = lax.broadcasted_iota(jnp.int32, t.shape, 0)
        prev = jnp.where(rows == 0, 0.0, pltpu.roll(t, 1, 0))
        nxt = jnp.where(rows == S - 1, 0.0, pltpu.roll(t, S - 1, 0))
        return 0.5 * (prev + nxt)

    def fwd_body(x_ref, mu_ref, o_ref):
        x = x_ref[...]
        o_ref[...] = x + mu_ref[...] * (cshift(x) - x)

    def bwd_body(x_ref, mu_ref, d_ref, _, dx_ref, dmu_ref):
        x, m, dd = x_ref[...], mu_ref[...], d_ref[...]
        gm = m * dd
        dx_ref[...] = (dd - gm + cshift(gm)).astype(dx_ref.dtype)
        dmu_ref[...] = jnp.sum(dd * (cshift(x) - x), axis=0, keepdims=True)

    x_spec = pl.BlockSpec((S, cb), lambda j: (0, off // cb + j))
    blk = pl.BlockSpec((S, cb), lambda j: (0, j))
    vec = pl.BlockSpec((1, cb), lambda j: (0, j))
    if d is None:
        return pl.pallas_call(
            fwd_body, name=name, grid=(width // cb,), in_specs=[x_spec, vec], out_specs=blk,
            out_shape=jax.ShapeDtypeStruct((S, width), F32), compiler_params=_cparams(("parallel",)),
        )(arr, mu)
    buf, first = into
    assert first % cb == 0
    return pl.pallas_call(
        bwd_body, name=name, grid=(width // cb,),
        in_specs=[x_spec, vec, blk, pl.BlockSpec(memory_space=pl.ANY)],
        out_specs=[pl.BlockSpec((S, cb), lambda j: (0, first // cb + j)), vec],
        out_shape=[jax.ShapeDtypeStruct(buf.shape, buf.dtype), jax.ShapeDtypeStruct((1, width), F32)],
        input_output_aliases={3: 0}, compiler_params=_cparams(("parallel",)),
    )(arr, mu, d, buf)


def _attention_fwd(qfull, kv, kr, hm, scale, *, tq, name):
    S = qfull.shape[0]

    def body(qn_ref, qr_ref, kn_ref, kr_ref, v_ref, o_ref, lse_ref):
        s = _attn_scores(qn_ref, qr_ref, kn_ref, kr_ref)
        m = jnp.max(s, axis=-1, keepdims=True)
        p = jnp.exp((s - m) * scale)
        l = jnp.sum(p, axis=-1, keepdims=True)
        o_ref[...] = jnp.dot(p.astype(BF16), v_ref[...], preferred_element_type=F32) * (1.0 / l)
        lse_ref[...] = jnp.broadcast_to(m * scale + jnp.log(l), lse_ref.shape)

    oblk = pl.BlockSpec((tq, VDIM), lambda h, i: (i, h))
    return pl.pallas_call(
        body, name=name, grid=(hm, S // tq),
        in_specs=[pl.BlockSpec((tq, NOPE), lambda h, i: (i, 2 * h)),
                  pl.BlockSpec((tq, NOPE), lambda h, i: (i, 2 * h + 1)),
                  pl.BlockSpec((S, NOPE), lambda h, i: (0, h)),
                  pl.BlockSpec((S, LANES), lambda h, i: (0, 0)),
                  pl.BlockSpec((S, VDIM), lambda h, i: (0, hm + h))],
        out_specs=[oblk, oblk],
        out_shape=[jax.ShapeDtypeStruct((S, hm * VDIM), F32)] * 2,
        compiler_params=_cparams(("parallel", "parallel")),
    )(qfull, qfull, kv, kr, kv)


def _attn_scores(qn_ref, qr_ref, kn_ref, kr_ref):
    nt = (((1,), (1,)), ((), ()))
    return (lax.dot_general(qn_ref[...], kn_ref[...], nt, preferred_element_type=F32)
            + lax.dot_general(qr_ref[...], kr_ref[...], nt, preferred_element_type=F32))


def _attention_bwd(qfull, kv, kr, o, lse, d_o, hm, scale, *, tq, name):
    S = qfull.shape[0]
    tq = min(tq, S)
    tn = (((0,), (0,)), ((), ()))
    nt = (((1,), (1,)), ((), ()))

    def body(qn_ref, qr_ref, kn_ref, kr_ref, v_ref, o_ref, lse_ref, do_ref,
             dqn_ref, dqr_ref, dkn_ref, dv_ref, dkr_ref):
        s = _attn_scores(qn_ref, qr_ref, kn_ref, kr_ref)
        p = jnp.exp(s * scale - lse_ref[:, 0:1])
        d_out = do_ref[...]
        delta = jnp.sum(d_out * o_ref[...], axis=-1, keepdims=True)
        d_out = d_out.astype(BF16)
        dp = lax.dot_general(d_out, v_ref[...], nt, preferred_element_type=F32)
        ds = (p * ((dp - delta) * scale)).astype(BF16)
        dqn_ref[...] = jnp.dot(ds, kn_ref[...], preferred_element_type=F32)
        dqr_ref[...] = jnp.dot(ds, kr_ref[...], preferred_element_type=F32)
        dv = lax.dot_general(p.astype(BF16), d_out, tn, preferred_element_type=F32)
        dkn = lax.dot_general(ds, qn_ref[...], tn, preferred_element_type=F32)
        dkr = lax.dot_general(ds, qr_ref[...], tn, preferred_element_type=F32)
        first = pl.program_id(1) == 0
        for ref, val in ((dkn_ref, dkn), (dv_ref, dv), (dkr_ref, dkr)):
            @pl.when(first)
            def _(ref=ref, val=val):
                ref[...] = val

            @pl.when(jnp.logical_not(first))
            def _(ref=ref, val=val):
                ref[...] += val

    qblk = pl.BlockSpec((tq, NOPE), lambda h, i: (i, h))
    kblk = pl.BlockSpec((S, NOPE), lambda h, i: (0, h))
    shp = jax.ShapeDtypeStruct((S, hm * NOPE), F32)
    return pl.pallas_call(
        body, name=name, grid=(hm, S // tq),
        in_specs=[pl.BlockSpec((tq, NOPE), lambda h, i: (i, 2 * h)),
                  pl.BlockSpec((tq, NOPE), lambda h, i: (i, 2 * h + 1)),
                  kblk,
                  pl.BlockSpec((S, LANES), lambda h, i: (0, 0)),
                  pl.BlockSpec((S, VDIM), lambda h, i: (0, hm + h)),
                  qblk, qblk, qblk],
        out_specs=[qblk, qblk, kblk, kblk, kblk],
        out_shape=[shp] * 5,
        compiler_params=_cparams(("parallel", "arbitrary")),
    )(qfull, qfull, kv, kr, kv, o, lse, d_o)


def _layout(D, MW, RW, TAIL, QR, KVR):
    names = ["gate_m", "gate_r", "z_m", "z_r", "q_a", "kv_a", "r", "k", "v", "tail"]
    widths = [D, D, MW, RW, QR, KVR, RW, RW, RW, TAIL]
    offs, o = {}, 0
    for nme, w in zip(names, widths):
        assert o % w == 0, (nme, o, w)
        offs[nme] = (o, w)
        o += w
    return offs, o


def _local_grads(x, target, W, dims):
    S, D = x.shape
    hm, hr, hn, rank = dims["hm"], dims["hr"], dims["hn"], dims["rank"]
    MW, RW = hm * VDIM, hr * hn
    TAIL = W["w2cat"].shape[0]
    QR, KVR = W["mla_q_norm"].shape[1], W["mla_kv_norm"].shape[1]
    lay, d_in = _layout(D, MW, RW, TAIL, QR, KVR)
    T = 256
    scale = (NOPE + ROPE) ** -0.5
    col = lambda arr, nme: _view(arr, *lay[nme])

    pos = jnp.arange(S, dtype=F32)
    inv_freq = jnp.power(ROPE_THETA, -jnp.arange(0, ROPE, 2, dtype=F32) / ROPE)
    ang = pos[:, None] * inv_freq[None, :]
    zpad = jnp.zeros((S, LANES - ROPE), F32)
    cosx = jnp.concatenate([jnp.cos(ang), jnp.cos(ang), zpad], axis=1)
    sinx = jnp.concatenate([jnp.sin(ang), jnp.sin(ang), zpad], axis=1)
    ri, ci = jnp.arange(LANES)[:, None], jnp.arange(LANES)[None, :]
    half = ROPE // 2
    rot = (jnp.where((ri == ci - half) & (ci >= half) & (ci < ROPE), 1.0, 0.0)
           - jnp.where((ri == ci + half) & (ci < half), 1.0, 0.0)).astype(BF16)
    rot_t = rot.T
    seg = (jnp.arange(RW)[:, None] // hn == jnp.arange(LANES)[None, :]).astype(BF16)
    seg_t = seg.T

    (h,) = _rowwise(lambda xb, g: (_rms(xb, g),), [x], [W["g_pre"]], [(D, BF16)], tile=T, name="pre_norm")
    proj = _mm(h, W["w_in_t"], tb=True, name="in_proj")

    qn, kvn = _rowwise(_f_mla_norm, [col(proj, "q_a"), col(proj, "kv_a")], [W["mla_q_norm"], W["mla_kv_norm"]],
                       [(QR, BF16), (KVR, BF16)], tile=T, name="mla_norm")
    qraw = _mm(qn, W["wq_b_t"], tb=True, name="q_up")
    kv = _mm(kvn, W["wkv_b"], out_dtype=BF16, name="kv_up")
    kr_view = _view(proj, lay["tail"][0], LANES)
    qfull, kr = _rowwise(functools.partial(_f_rope, hm), [qraw, kr_view, cosx, sinx], [rot, rot_t],
                         [(hm * QHEAD, BF16), (LANES, BF16)], tile=T, name="rope")
    o_mla, lse = _attention_fwd(qfull, kv, kr, hm, scale, tq=T, name="attn_fwd")

    shift_view = (proj, lay["r"][0], 3 * RW + TAIL)
    rl = _shift_lerp(shift_view, W["mu"], name="shift_fwd")
    rl_r, rl_k, rl_v = _view(rl, 0, RW), _view(rl, RW, RW), _view(rl, 2 * RW, RW)
    rl_tail = _view(rl, 3 * RW, TAIL)
    pre_params = [W["w0_f"], W["w0_b"], W["a0_f"], W["a0_b"], W["k_k"], W["k_a"], W["w2cat"], W["a2cat"], seg, seg_t]
    pre_fn = functools.partial(_f_rwkv_pre, RW)
    lw_f, lw_b, k_f, k_b, a_n, b_f, b_b = _rowwise(pre_fn, [rl_k, rl_tail], pre_params, [(RW, F32)] * 7, tile=T,
                                                    name="rwkv_pre")
    ops_f = (rl_r, lw_f, k_f, rl_v, a_n, b_f)
    ops_b = (rl_r, lw_b, k_b, rl_v, a_n, b_b)
    y_f, st_f, y_b, st_b = _rwkv_scan_fwd(ops_f, ops_b, RW, name="scan_fwd")

    post_fn = functools.partial(_f_post, hn)
    post_rows = [y_f, y_b, rl_r, k_f, k_b, rl_v, col(proj, "z_r"), o_mla, col(proj, "z_m")]
    post_params = [W["gn_g"], W["gn_b"], W["r_k"], seg, seg_t]
    ymg, yrg = _rowwise(post_fn, post_rows, post_params, [(MW, BF16), (RW, BF16)], tile=T, name="post")
    u_m = _mm(ymg, W["w_br_mla"], name="br_mla")
    u_r = _mm(yrg, W["w_br_rwkv"], name="br_rwkv")
    merge_rows = [u_m, u_r, col(proj, "gate_m"), col(proj, "gate_r")]
    (merged,) = _rowwise(lambda *t: (_f_merge(*t),), merge_rows, [], [(D, BF16)], tile=T, name="merge")
    out = _mm(merged, W["w_out"], name="out_proj")

    def head(ob, xb, tb, g):
        yn, vjp = jax.vjp(_rms, ob, g)
        err = xb + yn - tb
        dy = err * (1.0 / D)
        d_ob, d_g = vjp(dy)
        loss = jnp.broadcast_to(0.5 * jnp.sum(err * err) * (1.0 / D), (1, LANES))
        return dy, d_ob, loss, d_g

    dy, d_out, loss, g_g_post = _rowwise(head, [out, x, target], [W["g_post"]], [(D, F32), (D, BF16)],
                                         [(1, LANES), (1, D)], tile=T, name="head")
    d_merged = _mm(d_out, W["w_out"], tb=True, name="d_merged")
    g_w_out = _mm(merged, d_out, ta=True, out_dtype=BF16, name="g_w_out")

    def merge_bwd(u_m_b, u_r_b, g_m_b, g_r_b, dm):
        _, vjp = jax.vjp(_f_merge, u_m_b, u_r_b, g_m_b, g_r_b)
        du_m, du_r, dg_m, dg_r = vjp(dm)
        return du_m, du_r, jnp.concatenate([dg_m, dg_r], axis=1)

    d_u_m, d_u_r, d_proj = _rowwise(merge_bwd, merge_rows + [d_merged], [],
                                    [(D, BF16), (D, BF16), (2 * D, BF16, (None, d_in, lay["gate_m"][0]))], tile=T,
                                    name="merge_bwd")
    d_ymg = _mm(d_u_m, W["w_br_mla"], tb=True, name="d_ymg")
    d_yrg = _mm(d_u_r, W["w_br_rwkv"], tb=True, name="d_yrg")
    g_w_br_mla = _mm(ymg, d_u_m, ta=True, out_dtype=BF16, name="g_w_br_mla")
    g_w_br_rwkv = _mm(yrg, d_u_r, ta=True, out_dtype=BF16, name="g_w_br_rwkv")

    def post_bwd(*args):
        nr = len(post_rows)
        prim, dm, dr = args[:nr] + args[nr + 2:], args[nr], args[nr + 1]
        _, vjp = jax.vjp(post_fn, *prim)
        g = vjp((dm, dr))
        return g[0], g[2], g[3], g[5], g[7], jnp.concatenate([g[8], g[6]], axis=1), g[9], g[10], g[11]

    (d_y, d_r_bonus, d_k_bonus, d_v_bonus, d_o, d_proj, g_gn_g, g_gn_b, g_r_k) = _rowwise(
        post_bwd, post_rows + [d_ymg, d_yrg], post_params,
        [(RW, F32), (RW, F32), (RW, F32), (RW, F32), (MW, F32), (MW + RW, BF16, (d_proj, d_in, lay["z_m"][0]))],
        [(1, RW)] * 3, tile=T // 2, name="post_bwd")

    dscan = _rwkv_scan_bwd(ops_f, ops_b, st_f, st_b, d_y, RW, name="scan_bwd")
    dsc = {"f": dscan[:6], "b": dscan[6:]}

    d_qn, d_qr, d_kn, d_v_att, d_kr_h = _attention_bwd(qfull, kv, kr, o_mla, lse, d_o, hm, scale, tq=2 * T, name="attn_bwd")

    def rope_bwd(qraw_b, kr_in, cos_b, sin_b, dqn_b, dqr_b, dkn_b, dv_b, dkrh_b, rot_b, rot_t_b):
        _, vjp = jax.vjp(lambda q_, k_: _f_rope(hm, q_, k_, cos_b, sin_b, rot_b, rot_t_b), qraw_b, kr_in)
        parts = []
        for hh in range(hm):
            parts += [dqn_b[:, hh * NOPE:(hh + 1) * NOPE], dqr_b[:, hh * NOPE:(hh + 1) * NOPE]]
        dkr = dkrh_b[:, :LANES]
        for hh in range(1, hm):
            dkr = dkr + dkrh_b[:, hh * LANES:(hh + 1) * LANES]
        d_qraw, d_kr_in = vjp((jnp.concatenate(parts, axis=1), dkr))
        return d_qraw, jnp.concatenate([dkn_b, dv_b], axis=1), d_kr_in

    d_qraw, d_kv, d_kr_in = _rowwise(rope_bwd, [qraw, kr_view, cosx, sinx, d_qn, d_qr, d_kn, d_v_att, d_kr_h],
                                     [rot, rot_t], [(hm * QHEAD, BF16), (2 * MW, BF16), (LANES, F32)], tile=T,
                                     name="rope_bwd")
    d_qnorm = _mm(d_qraw, W["wq_b_t"], name="d_qn")
    d_kvnorm = _mm(d_kv, W["wkv_b"], tb=True, name="d_kvn")
    g_wq_b = _mm(d_qraw, qn, ta=True, out_dtype=BF16, name="g_wq_b")
    g_wkv_b = _mm(kvn, d_kv, ta=True, out_dtype=BF16, name="g_wkv_b")

    def mla_norm_bwd(q_a, kv_a, qg, kvg, dq, dk):
        _, vjp = jax.vjp(_f_mla_norm, q_a, kv_a, qg, kvg)
        d_q_a, d_kv_a, d_qg, d_kvg = vjp((dq, dk))
        return jnp.concatenate([d_q_a, d_kv_a], axis=1), d_qg, d_kvg

    d_proj, g_q_norm, g_kv_norm = _rowwise(
        lambda q_a, kv_a, dq, dk, qg, kvg: mla_norm_bwd(q_a, kv_a, qg, kvg, dq, dk),
        [col(proj, "q_a"), col(proj, "kv_a"), d_qnorm, d_kvnorm], [W["mla_q_norm"], W["mla_kv_norm"]],
        [(QR + KVR, BF16, (d_proj, d_in, lay["q_a"][0]))], [(1, QR), (1, KVR)], tile=T, name="mla_norm_bwd")

    def pre_bwd(k_b_, tail_b, dlwf, dlwb, dkf, dkb, dkbon, daf, dab, dbf, dbb, drf, drb, drbon, dvf, dvb, dvbon,
                dkr, *params):
        _, vjp = jax.vjp(pre_fn, k_b_, tail_b, *params[:8], params[8], params[9])
        g = vjp((dlwf, dlwb, dkf + dkbon, dkb + dkbon, daf + dab, dbf, dbb))
        d_tail = g[1] + jnp.concatenate([dkr, jnp.zeros((dkr.shape[0], TAIL - LANES), F32)], axis=1)
        d_rl = jnp.concatenate([drf + drb + drbon, g[0], dvf + dvb + dvbon, d_tail], axis=1)
        return (d_rl,) + tuple(g[2:10])

    f_, b_ = dsc["f"], dsc["b"]
    pre_bwd_rows = [rl_k, rl_tail, f_[1], b_[1], f_[2], b_[2], d_k_bonus, f_[4], b_[4], f_[5], b_[5],
                    f_[0], b_[0], d_r_bonus, f_[3], b_[3], d_v_bonus, d_kr_in]
    (d_rl, g_w0_f, g_w0_b, g_a0_f, g_a0_b, g_k_k, g_k_a, g_w2cat, g_a2cat) = _rowwise(
        pre_bwd, pre_bwd_rows, pre_params, [(3 * RW + TAIL, F32)],
        [(1, RW)] * 6 + [(TAIL, 2 * RW)] * 2, tile=T // 2, name="rwkv_pre_bwd")
    d_proj, g_mu = _shift_lerp(shift_view, W["mu"], d_rl, (d_proj, lay["r"][0]), name="shift_bwd")
    d_h = _mm(d_proj, W["w_in_t"], tn_cap=1024, name="d_h")
    g_w_in = _mm(d_proj, h, ta=True, out_dtype=BF16, tn_cap=1024, name="g_w_in")

    def pre_norm_bwd(xb, dyb, dhb, g):
        _, vjp = jax.vjp(_rms, xb, g)
        dx, dg = vjp(dhb)
        return dyb + dx, dg

    grad_x, g_g_pre = _rowwise(pre_norm_bwd, [x, dy, d_h], [W["g_pre"]], [(D, F32)], [(1, D)], tile=T,
                               name="pre_norm_bwd")

    grads = dict(g_pre=g_g_pre, w_in=g_w_in, mla_q_norm=g_q_norm, wq_b=g_wq_b, mla_kv_norm=g_kv_norm,
                 wkv_b=g_wkv_b, mu=g_mu, w0_f=g_w0_f, w0_b=g_w0_b, a0_f=g_a0_f, a0_b=g_a0_b, k_k=g_k_k, k_a=g_k_a,
                 w2cat=g_w2cat, a2cat=g_a2cat, r_k=g_r_k, gn_g=g_gn_g, gn_b=g_gn_b, w_br_mla=g_w_br_mla,
                 w_br_rwkv=g_w_br_rwkv, w_out=g_w_out, g_post=g_g_post)
    return loss[0, 0], grad_x, grads


_MATS = ["w_in", "mla_wq_b", "mla_wkv_b", "rwkv_w2_f", "rwkv_w2_b", "rwkv_a2_f", "rwkv_a2_b", "w_br_mla",
         "w_br_rwkv", "w_out"]
_ROW_SHARDED = ("w_out",)
_TRANSPOSED = ("w_in", "mla_wq_b")
_VECS = ["g_pre", "mla_q_norm", "mla_kv_norm", "rwkv_mu", "rwkv_w0_f", "rwkv_w0_b", "rwkv_a0_f", "rwkv_a0_b",
         "rwkv_k_k", "rwkv_k_a", "rwkv_r_k", "rwkv_gn_g", "rwkv_gn_b", "g_post"]
_WEIGHTS = ["g_pre", "w_in", "mla_q_norm", "mla_wq_b", "mla_kv_norm", "mla_wkv_b", "rwkv_mu", "rwkv_w0_f",
            "rwkv_w2_f", "rwkv_w0_b", "rwkv_w2_b", "rwkv_a0_f", "rwkv_a2_f", "rwkv_a0_b", "rwkv_a2_b", "rwkv_k_k",
            "rwkv_k_a", "rwkv_r_k", "rwkv_gn_g", "rwkv_gn_b", "w_br_mla", "w_br_rwkv", "w_out", "g_post"]

def _exchange(srcs, *, name):
    n = len(srcs)

    def body(*refs):
        src_refs, out_refs = refs[:n], refs[n:2 * n]
        send_sems, recv_sems, local_sems = refs[2 * n:]
        x, y, c = lax.axis_index("x"), lax.axis_index("y"), lax.axis_index("c")
        me = 4 * x + 2 * y + c
        flip = lambda v, bit: (1 - v) if bit else v

        def piece(a, idx):
            return src_refs[a] if srcs[a].ndim == 2 else src_refs[a].at[idx]

        owns = [pltpu.make_async_copy(piece(a, me), out_refs[a].at[me], local_sems.at[a]) for a in range(n)]
        for cp in owns:
            cp.start()
        sends, peers = [], []
        for d in range(1, N_DEV):
            px, py, pc = flip(x, d & 4), flip(y, d & 2), flip(c, d & 1)
            pidx = 4 * px + 2 * py + pc
            peers.append(((px, py, pc), pidx))
            for a in range(n):
                cp = pltpu.make_async_remote_copy(
                    src_ref=piece(a, pidx), dst_ref=out_refs[a].at[me], send_sem=send_sems.at[d - 1, a],
                    recv_sem=recv_sems.at[d - 1, a], device_id=(px, py, pc), device_id_type=pl.DeviceIdType.MESH)
                cp.start()
                sends.append(cp)
        for d, (peer, pidx) in zip(range(1, N_DEV), peers):
            for a in range(n):
                pltpu.make_async_remote_copy(
                    src_ref=piece(a, pidx), dst_ref=out_refs[a].at[pidx], send_sem=send_sems.at[d - 1, a],
                    recv_sem=recv_sems.at[d - 1, a], device_id=peer, device_id_type=pl.DeviceIdType.MESH).wait_recv()
        for cp in sends:
            cp.wait_send()
        for cp in owns:
            cp.wait()

    return pl.pallas_call(
        body, name=name,
        out_shape=[jax.ShapeDtypeStruct((N_DEV,) + s.shape[-2:], s.dtype) for s in srcs],
        in_specs=[pl.BlockSpec(memory_space=pl.ANY)] * n, out_specs=[pl.BlockSpec(memory_space=pl.ANY)] * n,
        scratch_shapes=[pltpu.SemaphoreType.DMA((N_DEV - 1, n)), pltpu.SemaphoreType.DMA((N_DEV - 1, n)),
                        pltpu.SemaphoreType.DMA((n,))],
    )(*srcs)


def _remote(src, dst, sems, key, to):
    send_sems, recv_sems = sems
    return pltpu.make_async_remote_copy(src_ref=src, dst_ref=dst, send_sem=send_sems.at[key], recv_sem=recv_sems.at[key],
                                        device_id=to, device_id_type=pl.DeviceIdType.MESH)


def _hbm_call(body, srcs, out_shapes, sem_shapes, *, name):
    n = len(srcs)
    return pl.pallas_call(
        body, name=name, out_shape=out_shapes,
        in_specs=[pl.BlockSpec(memory_space=pl.ANY)] * n,
        out_specs=[pl.BlockSpec(memory_space=pl.ANY)] * len(out_shapes),
        scratch_shapes=[pltpu.SemaphoreType.DMA(s) for s in sem_shapes],
    )(*srcs)


def _gather_two_level(srcs, *, name):
    n = len(srcs)

    def body(*refs):
        src_refs, out_refs = refs[:n], refs[n:2 * n]
        sems, local_sems = refs[2 * n:2 * n + 2], refs[2 * n + 2]
        x, y, c = lax.axis_index("x"), lax.axis_index("y"), lax.axis_index("c")
        idx = lambda px, py, pc: 4 * px + 2 * py + pc
        me, sibling = (x, y, c), (x, y, 1 - c)
        chips = [(1 - x, y), (x, 1 - y), (1 - x, 1 - y)]
        owns = [pltpu.make_async_copy(src_refs[a], out_refs[a].at[idx(*me)], local_sems.at[a]) for a in range(n)]
        for cp in owns:
            cp.start()
        sends = []
        for a in range(n):
            sends.append(_remote(src_refs[a], out_refs[a].at[idx(*me)], sems, (0, a), sibling))
            for j, chip in enumerate(chips):
                sends.append(_remote(src_refs[a], out_refs[a].at[idx(*me)], sems, (1 + j, a), (*chip, c)))
        for cp in sends:
            cp.start()
        for j, chip in enumerate(chips):
            for a in range(n):
                blk = out_refs[a].at[idx(*chip, c)]
                _remote(blk, blk, sems, (1 + j, a), me).wait_recv()
                fwd = _remote(blk, blk, sems, (4 + j, a), sibling)
                fwd.start()
                sends.append(fwd)
        for a in range(n):
            blk = out_refs[a].at[idx(*sibling)]
            _remote(blk, blk, sems, (0, a), me).wait_recv()
            for j, chip in enumerate(chips):
                blk = out_refs[a].at[idx(*chip, 1 - c)]
                _remote(blk, blk, sems, (4 + j, a), me).wait_recv()
        for cp in sends:
            cp.wait_send()
        for cp in owns:
            cp.wait()

    return _hbm_call(body, srcs, [jax.ShapeDtypeStruct((N_DEV,) + s.shape, s.dtype) for s in srcs],
                     [(7, n), (7, n), (n,)], name=name)


def _sibling_swap(srcs, *, name):
    n = len(srcs)

    def body(*refs):
        src_refs, out_refs, sems = refs[:n], refs[n:2 * n], refs[2 * n:]
        x, y, c = lax.axis_index("x"), lax.axis_index("y"), lax.axis_index("c")
        copies = [_remote(src_refs[a].at[2 * q + 1 - c], out_refs[a].at[q], sems, (q, a), (x, y, 1 - c))
                  for a in range(n) for q in range(4)]
        for cp in copies:
            cp.start()
        for cp in copies:
            cp.wait()

    return _hbm_call(body, srcs, [jax.ShapeDtypeStruct((4,) + s.shape[1:], s.dtype) for s in srcs],
                     [(4, n), (4, n)], name=name)


def _chip_exchange(srcs, *, name):
    n = len(srcs)

    def body(*refs):
        src_refs, out_refs = refs[:n], refs[n:2 * n]
        sems, local_sems = refs[2 * n:2 * n + 2], refs[2 * n + 2]
        x, y, c = lax.axis_index("x"), lax.axis_index("y"), lax.axis_index("c")
        mine = 2 * x + y
        chips = [(1 - x, y), (x, 1 - y), (1 - x, 1 - y)]
        owns = [pltpu.make_async_copy(src_refs[a].at[mine], out_refs[a].at[mine], local_sems.at[a]) for a in range(n)]
        for cp in owns:
            cp.start()
        sends = [_remote(src_refs[a].at[2 * px + py], out_refs[a].at[mine], sems, (j, a), (px, py, c))
                 for j, (px, py) in enumerate(chips) for a in range(n)]
        for cp in sends:
            cp.start()
        for j, (px, py) in enumerate(chips):
            for a in range(n):
                blk = out_refs[a].at[2 * px + py]
                _remote(blk, blk, sems, (j, a), (x, y, c)).wait_recv()
        for cp in sends:
            cp.wait_send()
        for cp in owns:
            cp.wait()

    return _hbm_call(body, srcs, [jax.ShapeDtypeStruct(s.shape, s.dtype) for s in srcs], [(3, n), (3, n), (n,)],
                     name=name)


def _pair_add(core, g, got, *, name):
    q, r, c = got.shape
    tr, tc = _tile2d(r, c)

    def body(core_ref, a_ref, b_ref, o_ref):
        o_ref[...] = (a_ref[...].astype(F32) + b_ref[...].astype(F32)).astype(BF16)

    blk = pl.BlockSpec((1, tr, tc), lambda i, j, k, core_ref: (i, j, k))
    mine = pl.BlockSpec((1, tr, tc), lambda i, j, k, core_ref: (2 * i + core_ref[0], j, k))
    return pl.pallas_call(
        body, name=name, out_shape=jax.ShapeDtypeStruct(got.shape, BF16),
        grid_spec=pltpu.PrefetchScalarGridSpec(num_scalar_prefetch=1, grid=(q, r // tr, c // tc),
                                               in_specs=[mine, blk], out_specs=blk),
        compiler_params=_cparams(("parallel", "parallel", "parallel")))(core, g, got)


def _adamw(recv, w, m, v, *, name):
    r, c = w.shape
    n_terms = recv.shape[0]
    tr, tc = _tile2d(r, c)

    def body(g_ref, w_ref, m_ref, v_ref, go_ref, d_ref, mo_ref, vo_ref):
        g = g_ref[0].astype(F32)
        for k in range(1, n_terms):
            g = g + g_ref[k].astype(F32)
        m_new = ADAM_B1 * m_ref[...] + (1.0 - ADAM_B1) * g
        v_new = ADAM_B2 * v_ref[...] + (1.0 - ADAM_B2) * (g * g)
        m_hat = m_new / (1.0 - ADAM_B1 ** ADAM_STEP)
        v_hat = v_new / (1.0 - ADAM_B2 ** ADAM_STEP)
        go_ref[...] = g
        d_ref[...] = -ADAM_LR * (m_hat / (jnp.sqrt(v_hat) + ADAM_EPS) + ADAM_WD * w_ref[...])
        mo_ref[...] = m_new
        vo_ref[...] = v_new

    blk = pl.BlockSpec((tr, tc), lambda i, j: (i, j))
    return pl.pallas_call(
        body, name=name, grid=(r // tr, c // tc),
        in_specs=[pl.BlockSpec((n_terms, tr, tc), lambda i, j: (0, i, j)), blk, blk, blk], out_specs=[blk] * 4,
        out_shape=[jax.ShapeDtypeStruct((r, c), F32)] * 4, compiler_params=_cparams(("parallel", "parallel")),
    )(recv, w, m, v)


def _tile2d(r, c, cap=256):
    if r <= cap:
        return r, c
    for t in range(cap, 0, -BF16_ROWS):
        if r % t == 0:
            return t, c
    return r, _pick(c, cap)


def _pack(pieces, dtype, quantum):
    out = []
    for p in pieces:
        lead, n = p.shape[:-1], p.shape[-1]
        pad = (-n) % quantum
        p = p.astype(dtype)
        if pad:
            p = jnp.concatenate([p, jnp.zeros(lead + (pad,), dtype)], axis=-1)
        out.append(p)
    flat = jnp.concatenate(out, axis=-1)
    return flat.reshape(flat.shape[:-1] + (flat.shape[-1] // LANES, LANES))


def _unpack(flat, sizes, quantum):
    flat = flat.reshape(flat.shape[:-2] + (-1,))
    out, o = [], 0
    for n in sizes:
        out.append(flat[..., o:o + n])
        o += n + (-n) % quantum
    return out


def _prepare_weights(full, vec, dims):
    hm, hr, hn, rank = dims["hm"], dims["hr"], dims["hn"], dims["rank"]
    D, QR, KVR = dims["D"], dims["QR"], dims["KVR"]
    MW, RW, TAIL = hm * VDIM, hr * hn, dims["TAIL"]
    slabs = full["w_in"]
    c = slabs.shape[1]
    parts, pos = [], 0
    for orig_off, width, perm_off in sorted(dims["segs"], key=lambda t: t[2]):
        if perm_off > pos:
            parts.append(jnp.zeros((perm_off - pos, D), BF16))
        for k in range(N_DEV):
            lo, hi = max(orig_off, k * c), min(orig_off + width, (k + 1) * c)
            if lo < hi:
                parts.append(slabs[k][lo - k * c:hi - k * c])
        pos = perm_off + width
    if dims["d_in_perm"] > pos:
        parts.append(jnp.zeros((dims["d_in_perm"] - pos, D), BF16))
    w_in_t = jnp.concatenate(parts, axis=0)
    full = {n: (t if n == "w_in" else t.reshape(-1, t.shape[2]) if n in _ROW_SHARDED + _TRANSPOSED
                else t.transpose(1, 0, 2).reshape(t.shape[1], -1)) for n, t in full.items()}
    wq = full["mla_wq_b"].reshape(hm, NOPE + ROPE, QR)
    wq = jnp.concatenate([wq, jnp.zeros((hm, QHEAD - NOPE - ROPE, QR), BF16)], axis=1).reshape(hm * QHEAD, QR)
    wkv = full["mla_wkv_b"].reshape(KVR, hm, 2, NOPE).transpose(0, 2, 1, 3).reshape(KVR, 2 * hm * NOPE)
    z = lambda rows: jnp.zeros((rows, RW), F32)
    f = lambda nme: full[nme].astype(F32)
    w2cat = jnp.concatenate([
        jnp.concatenate([z(ROPE), f("rwkv_w2_f"), z(TAIL - ROPE - rank)], axis=0),
        jnp.concatenate([z(ROPE + rank), f("rwkv_w2_b"), z(TAIL - ROPE - 2 * rank)], axis=0)], axis=1)
    a2cat = jnp.concatenate([
        jnp.concatenate([z(ROPE + 2 * rank), f("rwkv_a2_f"), z(TAIL - ROPE - 3 * rank)], axis=0),
        jnp.concatenate([z(ROPE + 3 * rank), f("rwkv_a2_b"), z(TAIL - ROPE - 4 * rank)], axis=0)], axis=1)
    mu = vec["rwkv_mu"]
    mu_p = jnp.concatenate([mu[:3 * RW], jnp.zeros((ROPE,), F32), mu[3 * RW:],
                            jnp.zeros((TAIL - ROPE - 4 * rank,), F32)])
    row = lambda t: t.reshape(1, -1)
    return dict(
        w_in_t=w_in_t, wq_b_t=wq, wkv_b=wkv, w2cat=w2cat, a2cat=a2cat, mu=row(mu_p),
        w_br_mla=full["w_br_mla"], w_br_rwkv=full["w_br_rwkv"], w_out=full["w_out"],
        g_pre=row(vec["g_pre"]), g_post=row(vec["g_post"]), mla_q_norm=row(vec["mla_q_norm"]),
        mla_kv_norm=row(vec["mla_kv_norm"]), w0_f=row(vec["rwkv_w0_f"]), w0_b=row(vec["rwkv_w0_b"]),
        a0_f=row(vec["rwkv_a0_f"]), a0_b=row(vec["rwkv_a0_b"]), k_k=row(vec["rwkv_k_k"]), k_a=row(vec["rwkv_k_a"]),
        r_k=row(vec["rwkv_r_k"]), gn_g=row(vec["rwkv_gn_g"]), gn_b=row(vec["rwkv_gn_b"]))


def _restore_grads(g, dims):
    hm, hr, hn, rank = dims["hm"], dims["hr"], dims["hn"], dims["rank"]
    D, QR, KVR = dims["D"], dims["QR"], dims["KVR"]
    MW, RW, TAIL = hm * VDIM, hr * hn, dims["TAIL"]
    lay, _ = _layout(D, MW, RW, TAIL, QR, KVR)
    gw = g["w_in"]
    c = dims["d_in"] // N_DEV
    slabs = []
    for k in range(N_DEV):
        parts = []
        for orig_off, width, perm_off in sorted(dims["segs"]):
            lo_, hi_ = max(orig_off, k * c), min(orig_off + width, (k + 1) * c)
            if lo_ < hi_:
                parts.append(gw[perm_off + lo_ - orig_off:perm_off + hi_ - orig_off])
        slabs.append(jnp.concatenate(parts, axis=0))
    w_in = jnp.stack(slabs)
    wq = g["wq_b"].reshape(hm, QHEAD, QR)[:, :NOPE + ROPE].reshape(N_DEV, -1, QR)
    wkv = g["wkv_b"].reshape(KVR, 2, hm, NOPE).transpose(0, 2, 1, 3).reshape(KVR, 2 * hm * NOPE)
    lo = lambda t, i, half: t[ROPE + i * rank:ROPE + (i + 1) * rank, half * RW:(half + 1) * RW].astype(BF16)
    cols = lambda t: t.reshape(t.shape[0], N_DEV, -1).transpose(1, 0, 2)
    mu = g["mu"][0]
    out = dict(
        w_in=w_in, mla_wq_b=wq, mla_wkv_b=cols(wkv), rwkv_w2_f=cols(lo(g["w2cat"], 0, 0)),
        rwkv_w2_b=cols(lo(g["w2cat"], 1, 1)), rwkv_a2_f=cols(lo(g["a2cat"], 2, 0)),
        rwkv_a2_b=cols(lo(g["a2cat"], 3, 1)), w_br_mla=cols(g["w_br_mla"]), w_br_rwkv=cols(g["w_br_rwkv"]),
        w_out=g["w_out"].reshape(N_DEV, -1, g["w_out"].shape[1]),
        rwkv_mu=jnp.concatenate([mu[:3 * RW], mu[3 * RW + ROPE:3 * RW + ROPE + 4 * rank]]),
        g_pre=g["g_pre"][0], g_post=g["g_post"][0], mla_q_norm=g["mla_q_norm"][0], mla_kv_norm=g["mla_kv_norm"][0],
        rwkv_w0_f=g["w0_f"][0], rwkv_w0_b=g["w0_b"][0], rwkv_a0_f=g["a0_f"][0], rwkv_a0_b=g["a0_b"][0],
        rwkv_k_k=g["k_k"][0], rwkv_k_a=g["k_a"][0], rwkv_r_k=g["r_k"][0], rwkv_gn_g=g["gn_g"][0],
        rwkv_gn_b=g["gn_b"][0])
    return out


def _dims(inp):
    D = inp["x"].shape[-1]
    QR, KVR = inp["mla_q_norm"].shape[0], inp["mla_kv_norm"].shape[0]
    hm = inp["mla_wq_b"].shape[1] * N_DEV // (NOPE + ROPE)
    hr, hn = inp["rwkv_r_k"].shape
    rank = inp["rwkv_w2_f"].shape[0]
    MW, RW = hm * VDIM, hr * hn
    TAIL = -(-(ROPE + 4 * rank) // LANES) * LANES
    orig, o = {}, 0
    for nme, w in (("q_a", QR), ("kv_a", KVR), ("k_rope", ROPE), ("rkv", 3 * RW), ("lora", 4 * rank), ("z_m", MW),
                   ("z_r", RW), ("gate_m", D), ("gate_r", D)):
        orig[nme] = (o, w)
        o += w
    assert o == inp["w_in"].shape[1] * N_DEV
    lay, d_in_perm = _layout(D, MW, RW, TAIL, QR, KVR)
    perm_off = dict(q_a=lay["q_a"][0], kv_a=lay["kv_a"][0], k_rope=lay["tail"][0], rkv=lay["r"][0],
                    lora=lay["tail"][0] + ROPE, z_m=lay["z_m"][0], z_r=lay["z_r"][0], gate_m=lay["gate_m"][0],
                    gate_r=lay["gate_r"][0])
    segs = [(orig[nme][0], orig[nme][1], perm_off[nme]) for nme in orig]
    return dict(D=D, QR=QR, KVR=KVR, hm=hm, hr=hr, hn=hn, rank=rank, TAIL=TAIL, hb=min(hr, 16), segs=segs, d_in=o,
                d_in_perm=d_in_perm)


def kernel(x, g_pre, w_in, mla_q_norm, mla_wq_b, mla_kv_norm, mla_wkv_b, rwkv_mu, rwkv_w0_f, rwkv_w2_f, rwkv_w0_b, rwkv_w2_b, rwkv_a0_f, rwkv_a2_f, rwkv_a0_b, rwkv_a2_b, rwkv_k_k, rwkv_k_a, rwkv_r_k, rwkv_gn_g, rwkv_gn_b, w_br_mla, w_br_rwkv, w_out, g_post, loss_target, m_g_pre, m_w_in, m_mla_q_norm, m_mla_wq_b, m_mla_kv_norm, m_mla_wkv_b, m_rwkv_mu, m_rwkv_w0_f, m_rwkv_w2_f, m_rwkv_w0_b, m_rwkv_w2_b, m_rwkv_a0_f, m_rwkv_a2_f, m_rwkv_a0_b, m_rwkv_a2_b, m_rwkv_k_k, m_rwkv_k_a, m_rwkv_r_k, m_rwkv_gn_g, m_rwkv_gn_b, m_w_br_mla, m_w_br_rwkv, m_w_out, m_g_post, v_g_pre, v_w_in, v_mla_q_norm, v_mla_wq_b, v_mla_kv_norm, v_mla_wkv_b, v_rwkv_mu, v_rwkv_w0_f, v_rwkv_w2_f, v_rwkv_w0_b, v_rwkv_w2_b, v_rwkv_a0_f, v_rwkv_a2_f, v_rwkv_a0_b, v_rwkv_a2_b, v_rwkv_k_k, v_rwkv_k_a, v_rwkv_r_k, v_rwkv_gn_g, v_rwkv_gn_b, v_w_br_mla, v_w_br_rwkv, v_w_out, v_g_post):
    inp = dict(locals())
    dims = _dims(inp)
    stored = lambda t, n: t.T if n in _TRANSPOSED else t
    slabs = _gather_two_level([stored(inp[n], n).astype(BF16) for n in _MATS], name="gather_weights")
    W = _prepare_weights(dict(zip(_MATS, slabs)), {n: inp[n] for n in _VECS}, dims)
    loss, grad_x, g = _local_grads(x[0], loss_target[0], W, dims)
    loss = lax.psum(loss, ("x", "y", "c"))
    g = _restore_grads(g, dims)

    new = {}
    core = lax.axis_index("c").astype(jnp.int32).reshape(1)
    got = _sibling_swap([g[n] for n in _MATS], name="pair_swap")
    sums = [_pair_add(core, g[n], t, name="pair_add_" + n) for n, t in zip(_MATS, got)]
    recv = _chip_exchange(sums, name="scatter_grads")
    for n, t in zip(_MATS, recv):
        out = _adamw(t, stored(inp[n], n), stored(inp["m_" + n], n), stored(inp["v_" + n], n), name="adamw_" + n)
        new[n] = [stored(o, n) for o in out]

    vsizes = [inp[n].size for n in _VECS]
    vflat = lambda prefix, src: _pack([src[prefix + n].reshape(-1) for n in _VECS], F32, LANES * 8)
    (vrecv,) = _exchange([vflat("", g)], name="gather_vector_grads")
    vout = _adamw(vrecv, vflat("", inp), vflat("m_", inp), vflat("v_", inp), name="adamw_vectors")
    vparts = [_unpack(t, vsizes, LANES * 8) for t in vout]
    for i, n in enumerate(_VECS):
        new[n] = [vp[i].reshape(inp[n].shape) for vp in vparts]

    outs = [loss, grad_x[None]]
    for k in range(4):
        outs += [new[n][k] for n in _WEIGHTS]
    return tuple(outs)
```

```python
import functools
import math

import jax
import jax.numpy as jnp
from jax import lax
from jax.experimental import pallas as pl
from jax.experimental.pallas import tpu as pltpu

F32 = jnp.float32
BF16 = jnp.bfloat16

N_DEV = 8
LANES = 128
BF16_ROWS = 16
NOPE, ROPE, VDIM = 128, 64, 128
QHEAD = 256
ROPE_THETA = 10000.0
NORM_EPS = 1e-6
GN_EPS = 64e-5
CHUNK = 64
SUB = 16
VMEM_LIMIT = 56 * 1024 * 1024

ADAM_LR, ADAM_B1, ADAM_B2, ADAM_EPS, ADAM_WD, ADAM_STEP = 0.001, 0.9, 0.999, 1e-08, 0.01, 10


def _cparams(sem):
    return pltpu.CompilerParams(dimension_semantics=sem, vmem_limit_bytes=VMEM_LIMIT)


def _pick(n, cap):
    if n <= cap:
        return n
    for t in range(cap - cap % LANES, 0, -LANES):
        if n % t == 0:
            return t
    raise ValueError(f"no tile for {n} under {cap}")


def _mm(a, b, *, ta=False, tb=False, out_dtype=F32, name, tm_cap=1024, tn_cap=512, tk_cap=2048, ride=None):
    K, M = a.shape if ta else a.shape[::-1]
    N = b.shape[0] if tb else b.shape[1]
    assert (b.shape[1] if tb else b.shape[0]) == K, (a.shape, b.shape, ta, tb)
    tm, tn, tk = _pick(M, tm_cap), _pick(N, tn_cap), _pick(K, tk_cap)
    nj, nk = N // tn, K // tk
    steps = (M // tm) * nj * nk
    dn = (((0 if ta else 1,), (1 if tb else 0,)), ((), ()))
    srcs, extra_shapes, sem_shapes, phases = ride if ride else ((), (), (), None)
    n_src, n_extra = len(srcs), len(extra_shapes)

    def body(*refs):
        a_ref, b_ref, o_ref = refs[0], refs[1], refs[2 + n_src]
        acc_ref = refs[3 + n_src + n_extra]
        k = pl.program_id(2)
        if ride:
            step = (pl.program_id(0) * nj + pl.program_id(1)) * nk + k
            first, middle, last = phases(refs[2:2 + n_src], refs[3 + n_src:3 + n_src + n_extra],
                                         refs[4 + n_src + n_extra:])
            pl.when(step == 0)(first)
            pl.when(step == steps // 2)(middle)
        p = lax.dot_general(a_ref[...], b_ref[...], dn, preferred_element_type=F32)

        @pl.when(k == 0)
        def _():
            acc_ref[...] = p

        @pl.when(k > 0)
        def _():
            acc_ref[...] += p

        @pl.when(k == nk - 1)
        def _():
            o_ref[...] = acc_ref[...].astype(out_dtype)

        if ride:
            pl.when(step == steps - 1)(last)

    a_spec = pl.BlockSpec((tk, tm), lambda i, j, k: (k, i)) if ta else pl.BlockSpec((tm, tk), lambda i, j, k: (i, k))
    b_spec = pl.BlockSpec((tn, tk), lambda i, j, k: (j, k)) if tb else pl.BlockSpec((tk, tn), lambda i, j, k: (k, j))
    hbm = pl.BlockSpec(memory_space=pl.ANY)
    out = pl.pallas_call(
        body, name=name, grid=(M // tm, nj, nk),
        in_specs=[a_spec, b_spec] + [hbm] * n_src,
        out_specs=[pl.BlockSpec((tm, tn), lambda i, j, k: (i, j))] + [hbm] * n_extra,
        out_shape=[jax.ShapeDtypeStruct((M, N), out_dtype)] + list(extra_shapes),
        scratch_shapes=[pltpu.VMEM((tm, tn), F32)] + [pltpu.SemaphoreType.DMA(s) for s in sem_shapes],
        compiler_params=_cparams(("arbitrary",) * 3 if ride else ("parallel", "parallel", "arbitrary")),
    )(a, b, *srcs)
    return out if ride else out[0]


def _view(arr, off, width):
    assert off % width == 0, (off, width)
    return (arr, off // width, width)


def _rowwise(fn, rows, params, out_rows, out_accs=(), *, tile, name):
    rows = [r if isinstance(r, tuple) else (r, 0, r.shape[1]) for r in rows]
    S = rows[0][0].shape[0]
    T = min(tile, S)
    assert S % T == 0
    n_rows, n_par, n_out = len(rows), len(params), len(out_rows)
    into = [o[2] if len(o) == 3 else None for o in out_rows]
    carried = [t[0] for t in into if t is not None and t[0] is not None]

    def body(*refs):
        ins = [r[...] for r in refs[:n_rows + n_par]]
        outs = fn(*ins)
        out_refs = refs[n_rows + n_par + len(carried):]
        for o_ref, val in zip(out_refs[:n_out], outs[:n_out]):
            o_ref[...] = val.astype(o_ref.dtype)
        i = pl.program_id(0)
        for o_ref, val in zip(out_refs[n_out:], outs[n_out:]):
            @pl.when(i == 0)
            def _(o_ref=o_ref, val=val):
                o_ref[...] = val

            @pl.when(i > 0)
            def _(o_ref=o_ref, val=val):
                o_ref[...] += val

    in_specs = [pl.BlockSpec((T, w), functools.partial(lambda i, cb: (i, cb), cb=cb)) for _, cb, w in rows]
    in_specs += [pl.BlockSpec(p.shape, lambda i: (0, 0)) for p in params]
    in_specs += [pl.BlockSpec(memory_space=pl.ANY)] * len(carried)
    out_specs, out_shape, aliases = [], [], {}
    for k, (o, t) in enumerate(zip(out_rows, into)):
        w, dt = o[0], o[1]
        if t is None:
            out_specs.append(pl.BlockSpec((T, w), lambda i: (i, 0)))
            out_shape.append(jax.ShapeDtypeStruct((S, w), dt))
            continue
        buf, total, first = t
        assert first % w == 0
        out_specs.append(pl.BlockSpec((T, w), functools.partial(lambda i, cb: (i, cb), cb=first // w)))
        out_shape.append(jax.ShapeDtypeStruct((S, total), dt))
        if buf is not None:
            aliases[n_rows + n_par + len(aliases)] = k
    out_specs += [pl.BlockSpec(s, lambda i: (0, 0)) for s in out_accs]
    out_shape += [jax.ShapeDtypeStruct(s, F32) for s in out_accs]
    return pl.pallas_call(
        body, name=name, grid=(S // T,), in_specs=in_specs, out_specs=out_specs, out_shape=out_shape,
        input_output_aliases=aliases, compiler_params=_cparams(("arbitrary",)),
    )(*[r[0] for r in rows], *params, *carried)


def _split3(x):
    hi = x.astype(BF16)
    r1 = x - hi.astype(F32)
    mid = r1.astype(BF16)
    lo = (r1 - mid.astype(F32)).astype(BF16)
    return hi, mid, lo


def _mm_sel(x, sel):
    hi, mid, lo = _split3(x)
    d = lambda u: jnp.dot(u, sel, preferred_element_type=F32)
    return d(hi) + d(mid) + d(lo)


@jax.custom_vjp
def _sel(x, sel, sel_t):
    return _mm_sel(x, sel)


def _sel_fwd(x, sel, sel_t):
    return _mm_sel(x, sel), (sel, sel_t)


def _sel_bwd(res, ct):
    sel, sel_t = res
    return _mm_sel(ct, sel_t), jnp.zeros_like(sel), jnp.zeros_like(sel_t)


_sel.defvjp(_sel_fwd, _sel_bwd)


def _rms(x, g):
    return x * lax.rsqrt(jnp.mean(x * x, axis=-1, keepdims=True) + NORM_EPS) * g


def _sigmoid(x):
    return 1.0 / (1.0 + jnp.exp(-x))


def _silu(x):
    return x * _sigmoid(x)


def _softplus(x):
    return jnp.maximum(x, 0.0) + jnp.log(1.0 + jnp.exp(-jnp.abs(x)))


def _bdot(x, w):
    return jnp.dot(x.astype(BF16), w.astype(BF16), preferred_element_type=F32)


def _f_mla_norm(q_a, kv_a, qg, kvg):
    return _rms(q_a, qg), _rms(kv_a, kvg)


def _f_rope(hm, qraw, kr_in, cosx, sinx, rot, rot_t):
    def rope(t):
        return t * cosx + _sel(t, rot, rot_t) * sinx
    parts = []
    for h in range(hm):
        parts.append(qraw[:, h * QHEAD:h * QHEAD + NOPE])
        parts.append(rope(qraw[:, h * QHEAD + NOPE:(h + 1) * QHEAD]))
    return jnp.concatenate(parts, axis=1), rope(kr_in)


def _f_rwkv_pre(rw, k, tail, w0f, w0b, a0f, a0b, k_k, k_a, w2cat, a2cat, seg, seg_t):
    zw = _bdot(jnp.tanh(tail), w2cat)
    za = _bdot(tail, a2cat)
    lw_f = -jnp.exp(-_softplus(-(w0f + zw[:, :rw])) - 0.5)
    lw_b = -jnp.exp(-_softplus(-(w0b + zw[:, rw:])) - 0.5)
    a_f = _sigmoid(a0f + za[:, :rw])
    a_b = _sigmoid(a0b + za[:, rw:])
    kk = k * k_k
    nrm = jnp.sqrt(_sel(_sel(kk * kk, seg, seg_t), seg_t, seg))
    kk = kk / jnp.maximum(nrm, 1e-12)
    k_f = k * (1.0 + (a_f - 1.0) * k_a)
    k_b = k * (1.0 + (a_b - 1.0) * k_a)
    return lw_f, lw_b, k_f, k_b, -kk, kk * a_f, kk * a_b


def _f_post(hn, y_f, y_b, r, k_f, k_b, v, z_r, o_mla, z_m, gn_g, gn_b, r_k, seg, seg_t):
    segsum = lambda t: _sel(_sel(t, seg, seg_t), seg_t, seg)
    y = y_f + y_b
    mu = segsum(y) * (1.0 / hn)
    yc = y - mu
    var = segsum(yc * yc) * (1.0 / hn)
    yn = yc * lax.rsqrt(var + GN_EPS) * gn_g + gn_b
    bonus = segsum(r * (k_f + k_b) * r_k) * v
    return o_mla * _silu(z_m), (yn + bonus) * _silu(z_r)


def _f_merge(u_m, u_r, g_m, g_r):
    return _sigmoid(g_m) * u_m + _sigmoid(g_r) * u_r


_NN = ((2,), (1,))
_NT = ((2,), (2,))
_TN = ((1,), (1,))

_SCAN_PASSES = {"cum": 2, "gram": 3, "solve": 1, "apply": 1, "state": 1}


def _hdot_raw(passes, x, y, dims):
    dn = (dims, ((0,), (0,)))
    d = lambda p, q: lax.dot_general(p, q, dn, preferred_element_type=F32)
    xh = x.astype(BF16)
    yh = y.astype(BF16)
    if passes == 1:
        return d(xh, yh)
    yl = (y - yh.astype(F32)).astype(BF16)
    if passes == 2:
        return d(xh, yh) + d(xh, yl)
    xl = (x - xh.astype(F32)).astype(BF16)
    return d(xh, yh) + d(xh, yl) + d(xl, yh)


@functools.partial(jax.custom_vjp, nondiff_argnums=(2, 3))
def _hdot_p(x, y, dims, passes):
    return _hdot_raw(passes, x, y, dims)


def _hdot_fwd(x, y, dims, passes):
    return _hdot_raw(passes, x, y, dims), (x, y)


def _hdot_bwd(dims, passes, res, ct):
    x, y = res
    if dims == _NN:
        return _hdot_raw(passes, ct, y, _NT), _hdot_raw(passes, x, ct, _TN)
    if dims == _NT:
        return _hdot_raw(passes, ct, y, _NN), _hdot_raw(passes, ct, x, _TN)
    return _hdot_raw(passes, y, ct, _NT), _hdot_raw(passes, x, ct, _NN)


_hdot_p.defvjp(_hdot_fwd, _hdot_bwd)


def _hdot(x, y, dims, kind):
    return _hdot_p(x, y, dims, _SCAN_PASSES[kind])


def _tri_solve(n_mat, x, length):
    row = lax.broadcasted_iota(jnp.int32, (length, length), 0)
    col = lax.broadcasted_iota(jnp.int32, (length, length), 1)
    eye = (row == col).astype(F32)[None]
    diag_blk = ((row // SUB) == (col // SUB))[None]
    nd = jnp.where(diag_blk, n_mat, 0.0)
    no = n_mat - nd
    dinv = eye + nd
    p = nd
    for _ in range(int(math.log2(SUB)) - 1):
        p = _hdot(p, p, _NN, "solve")
        dinv = dinv + _hdot(dinv, p, _NN, "solve")
    q = _hdot(dinv, no, _NN, "solve")
    u = _hdot(dinv, x, _NN, "solve")
    levels = int(math.log2(length // SUB))
    qs = [q]
    for _ in range(levels - 1):
        qs.append(_hdot(qs[-1], qs[-1], _NN, "solve"))
    for qk in reversed(qs):
        u = u + _hdot(qk, u, _NN, "solve")
    return u


def _rwkv_chunk(rev, s0, r, lw, k, v, a, b):
    pairs, length, width = r.shape
    hn = width // 2
    row = lax.broadcasted_iota(jnp.int32, (length, length), 0)
    col = lax.broadcasted_iota(jnp.int32, (length, length), 1)
    row2 = lax.broadcasted_iota(jnp.int32, (length, 2 * length), 0)
    col2 = lax.broadcasted_iota(jnp.int32, (length, 2 * length), 1)
    col2 = jnp.where(col2 >= length, col2 - length, col2)
    if rev is None:
        half = pairs // 2
        back = lax.broadcasted_iota(jnp.int32, (pairs, length, length), 0) >= half
        idx2 = lax.broadcasted_iota(jnp.int32, (2 * pairs, length, 2 * length), 0)
        back2 = ((idx2 >= half) & (idx2 < pairs)) | (idx2 >= pairs + half)
        ahead = jnp.where(back, (col - row)[None], (row - col)[None])
        ahead2 = jnp.where(back2, (col2 - row2)[None], (row2 - col2)[None])
        incl, strict2, incl2 = ahead >= 0, ahead2 > 0, ahead2 >= 0
    else:
        incl = ((row <= col) if rev else (row >= col))[None]
        strict2 = ((row2 < col2) if rev else (row2 > col2))[None]
        incl2 = ((row2 <= col2) if rev else (row2 >= col2))[None]
    lane = lax.broadcasted_iota(jnp.int32, (1, 1, width), 2)
    first = lane < hn
    head_mask = jnp.concatenate([jnp.broadcast_to(first.astype(F32), (pairs, 1, width)),
                                 jnp.broadcast_to(1.0 - first.astype(F32), (pairs, 1, width))], axis=0)
    twice = lambda t: jnp.concatenate([t, t], axis=0)
    pick = lambda t: jnp.where(first, t[:pairs], t[pairs:])

    t_incl = jnp.broadcast_to(incl.astype(F32), (pairs, length, length))
    cum = _hdot(t_incl, lw, _NN, "cum")
    g = jnp.exp(cum)
    g_inv = jnp.exp(-cum)
    at = a * jnp.exp(cum - lw)
    rt = r * g
    bt = b * g_inv
    kt = k * g_inv
    lhs = jnp.concatenate([twice(at) * head_mask, twice(rt) * head_mask], axis=1)
    rhs = jnp.concatenate([twice(bt), twice(kt)], axis=1)
    gram = _hdot(lhs, rhs, _NT, "gram")
    top = jnp.where(strict2, gram[:, :length], 0.0)
    bot = jnp.where(incl2, gram[:, length:], 0.0)
    v2 = twice(v)
    zeros = jnp.zeros_like(v2)
    x = _hdot(at, s0, _NT, "apply") + pick(_hdot(top, jnp.concatenate([zeros, v2], axis=1), _NN, "apply"))
    u = pick(_tri_solve(top[:, :, :length], twice(x), length))
    y = _hdot(rt, s0, _NT, "apply") + pick(_hdot(bot, jnp.concatenate([twice(u), v2], axis=1), _NN, "apply"))
    g_last = jnp.exp(jnp.sum(lw, axis=1, keepdims=True))
    ri = lax.broadcasted_iota(jnp.int32, (width, width), 0)
    ci = lax.broadcasted_iota(jnp.int32, (width, width), 1)
    same_head = ((ri < hn) == (ci < hn))[None]
    upd = _hdot(u, bt, _TN, "state") + _hdot(v, kt, _TN, "state")
    s1 = (s0 + jnp.where(same_head, upd, 0.0)) * g_last
    return y, s1


def _split_pairs(x):
    return jnp.stack([x[:, p * LANES:(p + 1) * LANES] for p in range(x.shape[1] // LANES)])


def _merge_pairs(x):
    return jnp.concatenate([x[p] for p in range(x.shape[0])], axis=1)


def _scan_specs(views, rw, nc, rev):
    cidx = (lambda c: nc - 1 - c) if rev else (lambda c: c)
    seqs = [pl.BlockSpec((CHUNK, rw), functools.partial(lambda c, cb: (cidx(c), cb), cb=cb)) for _, cb, _ in views]
    plain = pl.BlockSpec((CHUNK, rw), lambda c: (cidx(c), 0))
    st = pl.BlockSpec((1, rw // LANES, LANES, LANES), lambda c: (cidx(c), 0, 0, 0))
    return seqs, plain, st


def _as_views(arrs, rw):
    return [t if isinstance(t, tuple) else (t, 0, rw) for t in arrs]


def _rwkv_scan_fwd(ops_f, ops_b, rw, *, name):
    S = _as_views(ops_f, rw)[0][0].shape[0]
    nc, pairs = S // CHUNK, rw // LANES
    in_specs, out_specs, arrays = [], [], []
    for rev, ops in ((False, ops_f), (True, ops_b)):
        views = _as_views(ops, rw)
        seqs, plain, st = _scan_specs(views, rw, nc, rev)
        in_specs += seqs
        out_specs += [plain, st]
        arrays += [t[0] for t in views]

    def both(refs_f, refs_b):
        return [jnp.concatenate([_split_pairs(f[...]), _split_pairs(b[...])], axis=0) for f, b in zip(refs_f, refs_b)]

    def body(*refs):
        (y_f, st_f, y_b, st_b), s_ref = refs[12:16], refs[16]

        @pl.when(pl.program_id(0) == 0)
        def _():
            s_ref[...] = jnp.zeros_like(s_ref)

        s0 = s_ref[...]
        st_f[0] = s0[:pairs]
        st_b[0] = s0[pairs:]
        y, s1 = _rwkv_chunk(None, s0, *both(refs[:6], refs[6:12]))
        y_f[...] = _merge_pairs(y[:pairs])
        y_b[...] = _merge_pairs(y[pairs:])
        s_ref[...] = s1

    return pl.pallas_call(
        body, name=name, grid=(nc,), in_specs=in_specs, out_specs=out_specs,
        out_shape=[jax.ShapeDtypeStruct((S, rw), F32), jax.ShapeDtypeStruct((nc, pairs, LANES, LANES), F32)] * 2,
        scratch_shapes=[pltpu.VMEM((2 * pairs, LANES, LANES), F32)],
        compiler_params=_cparams(("arbitrary",)),
    )(*arrays)


def _rwkv_scan_bwd(ops_f, ops_b, states_f, states_b, dy, rw, *, name):
    S = dy.shape[0]
    nc, pairs = S // CHUNK, rw // LANES
    in_specs, arrays = [], []
    for rev, ops, states in ((False, ops_f, states_f), (True, ops_b, states_b)):
        views = _as_views(list(ops) + [dy], rw)
        seqs, plain, st = _scan_specs(views, rw, nc, not rev)
        in_specs += seqs + [st]
        arrays += [t[0] for t in views] + [states]
    out_specs = []
    for rev in (False, True):
        out_specs += [_scan_specs([], rw, nc, not rev)[1]] * 6

    def both(refs_f, refs_b):
        return [jnp.concatenate([_split_pairs(f[...]), _split_pairs(b[...])], axis=0) for f, b in zip(refs_f, refs_b)]

    def body(*refs):
        ds_ref = refs[28]

        @pl.when(pl.program_id(0) == 0)
        def _():
            ds_ref[...] = jnp.zeros_like(ds_ref)

        s0 = jnp.concatenate([refs[7][0], refs[15][0]], axis=0)
        _, vjp = jax.vjp(functools.partial(_rwkv_chunk, None), s0, *both(refs[:6], refs[8:14]))
        (dy,) = both(refs[6:7], refs[14:15])
        grads = vjp((dy, ds_ref[...]))
        ds_ref[...] = grads[0]
        for o_f, o_b, gval in zip(refs[16:22], refs[22:28], grads[1:]):
            o_f[...] = _merge_pairs(gval[:pairs])
            o_b[...] = _merge_pairs(gval[pairs:])

    return pl.pallas_call(
        body, name=name, grid=(nc,), in_specs=in_specs, out_specs=out_specs,
        out_shape=[jax.ShapeDtypeStruct((S, rw), F32)] * 12,
        scratch_shapes=[pltpu.VMEM((2 * pairs, LANES, LANES), F32)],
        compiler_params=_cparams(("arbitrary",)),
    )(*arrays)


def _shift_lerp(x_view, mu, d=None, into=None, *, name):
    arr, off, width = x_view
    S = arr.shape[0]
    cb = _pick(width, 256)
    assert off % cb == 0

    def cshift(t):
        rows = lax.broadcasted_iota(jnp.int32, t.shape, 0)
        prev = jnp.where(rows == 0, 0.0, pltpu.roll(t, 1, 0))
        nxt = jnp.where(rows == S - 1, 0.0, pltpu.roll(t, S - 1, 0))
        return 0.5 * (prev + nxt)

    def fwd_body(x_ref, mu_ref, o_ref):
        x = x_ref[...]
        o_ref[...] = x + mu_ref[...] * (cshift(x) - x)

    def bwd_body(x_ref, mu_ref, d_ref, _, dx_ref, dmu_ref):
        x, m, dd = x_ref[...], mu_ref[...], d_ref[...]
        gm = m * dd
        dx_ref[...] = (dd - gm + cshift(gm)).astype(dx_ref.dtype)
        dmu_ref[...] = jnp.sum(dd * (cshift(x) - x), axis=0, keepdims=True)

    x_spec = pl.BlockSpec((S, cb), lambda j: (0, off // cb + j))
    blk = pl.BlockSpec((S, cb), lambda j: (0, j))
    vec = pl.BlockSpec((1, cb), lambda j: (0, j))
    if d is None:
        return pl.pallas_call(
            fwd_body, name=name, grid=(width // cb,), in_specs=[x_spec, vec], out_specs=blk,
            out_shape=jax.ShapeDtypeStruct((S, width), F32), compiler_params=_cparams(("parallel",)),
        )(arr, mu)
    buf, first = into
    assert first % cb == 0
    return pl.pallas_call(
        bwd_body, name=name, grid=(width // cb,),
        in_specs=[x_spec, vec, blk, pl.BlockSpec(memory_space=pl.ANY)],
        out_specs=[pl.BlockSpec((S, cb), lambda j: (0, first // cb + j)), vec],
        out_shape=[jax.ShapeDtypeStruct(buf.shape, buf.dtype), jax.ShapeDtypeStruct((1, width), F32)],
        input_output_aliases={3: 0}, compiler_params=_cparams(("parallel",)),
    )(arr, mu, d, buf)


def _attention_fwd(qfull, kv, kr, hm, scale, *, tq, name):
    S = qfull.shape[0]

    def body(qn_ref, qr_ref, kn_ref, kr_ref, v_ref, o_ref, lse_ref):
        s = _attn_scores(qn_ref, qr_ref, kn_ref, kr_ref)
        m = jnp.max(s, axis=-1, keepdims=True)
        p = jnp.exp((s - m) * scale)
        l = jnp.sum(p, axis=-1, keepdims=True)
        o_ref[...] = jnp.dot(p.astype(BF16), v_ref[...], preferred_element_type=F32) * (1.0 / l)
        lse_ref[...] = jnp.broadcast_to(m * scale + jnp.log(l), lse_ref.shape)

    oblk = pl.BlockSpec((tq, VDIM), lambda h, i: (i, h))
    return pl.pallas_call(
        body, name=name, grid=(hm, S // tq),
        in_specs=[pl.BlockSpec((tq, NOPE), lambda h, i: (i, 2 * h)),
                  pl.BlockSpec((tq, NOPE), lambda h, i: (i, 2 * h + 1)),
                  pl.BlockSpec((S, NOPE), lambda h, i: (0, h)),
                  pl.BlockSpec((S, LANES), lambda h, i: (0, 0)),
                  pl.BlockSpec((S, VDIM), lambda h, i: (0, hm + h))],
        out_specs=[oblk, oblk],
        out_shape=[jax.ShapeDtypeStruct((S, hm * VDIM), F32)] * 2,
        compiler_params=_cparams(("parallel", "parallel")),
    )(qfull, qfull, kv, kr, kv)


def _attn_scores(qn_ref, qr_ref, kn_ref, kr_ref):
    nt = (((1,), (1,)), ((), ()))
    return (lax.dot_general(qn_ref[...], kn_ref[...], nt, preferred_element_type=F32)
            + lax.dot_general(qr_ref[...], kr_ref[...], nt, preferred_element_type=F32))


def _attention_bwd(qfull, kv, kr, o, lse, d_o, hm, scale, *, tq, name):
    S = qfull.shape[0]
    tq = min(tq, S)
    tn = (((0,), (0,)), ((), ()))
    nt = (((1,), (1,)), ((), ()))

    def body(qn_ref, qr_ref, kn_ref, kr_ref, v_ref, o_ref, lse_ref, do_ref,
             dqn_ref, dqr_ref, dkn_ref, dv_ref, dkr_ref):
        s = _attn_scores(qn_ref, qr_ref, kn_ref, kr_ref)
        p = jnp.exp(s * scale - lse_ref[:, 0:1])
        d_out = do_ref[...]
        delta = jnp.sum(d_out * o_ref[...], axis=-1, keepdims=True)
        d_out = d_out.astype(BF16)
        dp = lax.dot_general(d_out, v_ref[...], nt, preferred_element_type=F32)
        ds = (p * ((dp - delta) * scale)).astype(BF16)
        dqn_ref[...] = jnp.dot(ds, kn_ref[...], preferred_element_type=F32)
        dqr_ref[...] = jnp.dot(ds, kr_ref[...], preferred_element_type=F32)
        dv = lax.dot_general(p.astype(BF16), d_out, tn, preferred_element_type=F32)
        dkn = lax.dot_general(ds, qn_ref[...], tn, preferred_element_type=F32)
        dkr = lax.dot_general(ds, qr_ref[...], tn, preferred_element_type=F32)
        first = pl.program_id(1) == 0
        for ref, val in ((dkn_ref, dkn), (dv_ref, dv), (dkr_ref, dkr)):
            @pl.when(first)
            def _(ref=ref, val=val):
                ref[...] = val

            @pl.when(jnp.logical_not(first))
            def _(ref=ref, val=val):
                ref[...] += val

    qblk = pl.BlockSpec((tq, NOPE), lambda h, i: (i, h))
    kblk = pl.BlockSpec((S, NOPE), lambda h, i: (0, h))
    shp = jax.ShapeDtypeStruct((S, hm * NOPE), F32)
    return pl.pallas_call(
        body, name=name, grid=(hm, S // tq),
        in_specs=[pl.BlockSpec((tq, NOPE), lambda h, i: (i, 2 * h)),
                  pl.BlockSpec((tq, NOPE), lambda h, i: (i, 2 * h + 1)),
                  kblk,
                  pl.BlockSpec((S, LANES), lambda h, i: (0, 0)),
                  pl.BlockSpec((S, VDIM), lambda h, i: (0, hm + h)),
                  qblk, qblk, qblk],
        out_specs=[qblk, qblk, kblk, kblk, kblk],
        out_shape=[shp] * 5,
        compiler_params=_cparams(("parallel", "arbitrary")),
    )(qfull, qfull, kv, kr, kv, o, lse, d_o)


def _layout(D, MW, RW, TAIL, QR, KVR):
    names = ["gate_m", "gate_r", "z_m", "z_r", "q_a", "kv_a", "r", "k", "v", "tail"]
    widths = [D, D, MW, RW, QR, KVR, RW, RW, RW, TAIL]
    offs, o = {}, 0
    for nme, w in zip(names, widths):
        assert o % w == 0, (nme, o, w)
        offs[nme] = (o, w)
        o += w
    return offs, o


def _local_grads(x, target, W, dims, exchange=None):
    S, D = x.shape
    hm, hr, hn, rank = dims["hm"], dims["hr"], dims["hn"], dims["rank"]
    MW, RW = hm * VDIM, hr * hn
    TAIL = dims["TAIL"]
    QR, KVR = W["mla_q_norm"].shape[1], W["mla_kv_norm"].shape[1]
    lay, d_in = _layout(D, MW, RW, TAIL, QR, KVR)
    T = 256
    scale = (NOPE + ROPE) ** -0.5
    col = lambda arr, nme: _view(arr, *lay[nme])

    pos = jnp.arange(S, dtype=F32)
    inv_freq = jnp.power(ROPE_THETA, -jnp.arange(0, ROPE, 2, dtype=F32) / ROPE)
    ang = pos[:, None] * inv_freq[None, :]
    zpad = jnp.zeros((S, LANES - ROPE), F32)
    cosx = jnp.concatenate([jnp.cos(ang), jnp.cos(ang), zpad], axis=1)
    sinx = jnp.concatenate([jnp.sin(ang), jnp.sin(ang), zpad], axis=1)
    ri, ci = jnp.arange(LANES)[:, None], jnp.arange(LANES)[None, :]
    half = ROPE // 2
    rot = (jnp.where((ri == ci - half) & (ci >= half) & (ci < ROPE), 1.0, 0.0)
           - jnp.where((ri == ci + half) & (ci < half), 1.0, 0.0)).astype(BF16)
    rot_t = rot.T
    seg = (jnp.arange(RW)[:, None] // hn == jnp.arange(LANES)[None, :]).astype(BF16)
    seg_t = seg.T

    (h,) = _rowwise(lambda xb, g: (_rms(xb, g),), [x], [W["g_pre"]], [(D, BF16)], tile=T, name="pre_norm")
    if exchange is None:
        proj = _mm(h, W["w_in_t"], tb=True, name="in_proj")
    else:
        proj, *slabs = _mm(h, W["w_in_t"], tb=True, ride=_gather_plan(exchange[0]), name="in_proj")
        W = {**W, **_prepare_rest(dict(zip(_MATS[1:], slabs)), dims)}

    qn, kvn = _rowwise(_f_mla_norm, [col(proj, "q_a"), col(proj, "kv_a")], [W["mla_q_norm"], W["mla_kv_norm"]],
                       [(QR, BF16), (KVR, BF16)], tile=T, name="mla_norm")
    qraw = _mm(qn, W["wq_b_t"], tb=True, name="q_up")
    kv = _mm(kvn, W["wkv_b"], out_dtype=BF16, name="kv_up")
    kr_view = _view(proj, lay["tail"][0], LANES)
    qfull, kr = _rowwise(functools.partial(_f_rope, hm), [qraw, kr_view, cosx, sinx], [rot, rot_t],
                         [(hm * QHEAD, BF16), (LANES, BF16)], tile=T, name="rope")
    o_mla, lse = _attention_fwd(qfull, kv, kr, hm, scale, tq=T, name="attn_fwd")

    shift_view = (proj, lay["r"][0], 3 * RW + TAIL)
    rl = _shift_lerp(shift_view, W["mu"], name="shift_fwd")
    rl_r, rl_k, rl_v = _view(rl, 0, RW), _view(rl, RW, RW), _view(rl, 2 * RW, RW)
    rl_tail = _view(rl, 3 * RW, TAIL)
    pre_params = [W["w0_f"], W["w0_b"], W["a0_f"], W["a0_b"], W["k_k"], W["k_a"], W["w2cat"], W["a2cat"], seg, seg_t]
    pre_fn = functools.partial(_f_rwkv_pre, RW)
    lw_f, lw_b, k_f, k_b, a_n, b_f, b_b = _rowwise(pre_fn, [rl_k, rl_tail], pre_params, [(RW, F32)] * 7, tile=T,
                                                    name="rwkv_pre")
    ops_f = (rl_r, lw_f, k_f, rl_v, a_n, b_f)
    ops_b = (rl_r, lw_b, k_b, rl_v, a_n, b_b)
    y_f, st_f, y_b, st_b = _rwkv_scan_fwd(ops_f, ops_b, RW, name="scan_fwd")

    post_fn = functools.partial(_f_post, hn)
    post_rows = [y_f, y_b, rl_r, k_f, k_b, rl_v, col(proj, "z_r"), o_mla, col(proj, "z_m")]
    post_params = [W["gn_g"], W["gn_b"], W["r_k"], seg, seg_t]
    ymg, yrg = _rowwise(post_fn, post_rows, post_params, [(MW, BF16), (RW, BF16)], tile=T, name="post")
    u_m = _mm(ymg, W["w_br_mla"], name="br_mla")
    u_r = _mm(yrg, W["w_br_rwkv"], name="br_rwkv")
    merge_rows = [u_m, u_r, col(proj, "gate_m"), col(proj, "gate_r")]
    (merged,) = _rowwise(lambda *t: (_f_merge(*t),), merge_rows, [], [(D, BF16)], tile=T, name="merge")
    out = _mm(merged, W["w_out"], name="out_proj")

    def head(ob, xb, tb, g):
        yn, vjp = jax.vjp(_rms, ob, g)
        err = xb + yn - tb
        dy = err * (1.0 / D)
        d_ob, d_g = vjp(dy)
        loss = jnp.broadcast_to(0.5 * jnp.sum(err * err) * (1.0 / D), (1, LANES))
        return dy, d_ob, loss, d_g

    dy, d_out, loss, g_g_post = _rowwise(head, [out, x, target], [W["g_post"]], [(D, F32), (D, BF16)],
                                         [(1, LANES), (1, D)], tile=T, name="head")
    d_merged = _mm(d_out, W["w_out"], tb=True, name="d_merged")
    g_w_out = _mm(merged, d_out, ta=True, out_dtype=BF16, name="g_w_out")

    def merge_bwd(u_m_b, u_r_b, g_m_b, g_r_b, dm):
        _, vjp = jax.vjp(_f_merge, u_m_b, u_r_b, g_m_b, g_r_b)
        du_m, du_r, dg_m, dg_r = vjp(dm)
        return du_m, du_r, jnp.concatenate([dg_m, dg_r], axis=1)

    d_u_m, d_u_r, d_proj = _rowwise(merge_bwd, merge_rows + [d_merged], [],
                                    [(D, BF16), (D, BF16), (2 * D, BF16, (None, d_in, lay["gate_m"][0]))], tile=T,
                                    name="merge_bwd")
    d_ymg = _mm(d_u_m, W["w_br_mla"], tb=True, name="d_ymg")
    d_yrg = _mm(d_u_r, W["w_br_rwkv"], tb=True, name="d_yrg")
    g_w_br_mla = _mm(ymg, d_u_m, ta=True, out_dtype=BF16, name="g_w_br_mla")
    g_w_br_rwkv = _mm(yrg, d_u_r, ta=True, out_dtype=BF16, name="g_w_br_rwkv")

    def post_bwd(*args):
        nr = len(post_rows)
        prim, dm, dr = args[:nr] + args[nr + 2:], args[nr], args[nr + 1]
        _, vjp = jax.vjp(post_fn, *prim)
        g = vjp((dm, dr))
        return g[0], g[2], g[3], g[5], g[7], jnp.concatenate([g[8], g[6]], axis=1), g[9], g[10], g[11]

    (d_y, d_r_bonus, d_k_bonus, d_v_bonus, d_o, d_proj, g_gn_g, g_gn_b, g_r_k) = _rowwise(
        post_bwd, post_rows + [d_ymg, d_yrg], post_params,
        [(RW, F32), (RW, F32), (RW, F32), (RW, F32), (MW, F32), (MW + RW, BF16, (d_proj, d_in, lay["z_m"][0]))],
        [(1, RW)] * 3, tile=T // 2, name="post_bwd")

    dscan = _rwkv_scan_bwd(ops_f, ops_b, st_f, st_b, d_y, RW, name="scan_bwd")
    dsc = {"f": dscan[:6], "b": dscan[6:]}

    d_qn, d_qr, d_kn, d_v_att, d_kr_h = _attention_bwd(qfull, kv, kr, o_mla, lse, d_o, hm, scale, tq=2 * T, name="attn_bwd")

    def rope_bwd(qraw_b, kr_in, cos_b, sin_b, dqn_b, dqr_b, dkn_b, dv_b, dkrh_b, rot_b, rot_t_b):
        _, vjp = jax.vjp(lambda q_, k_: _f_rope(hm, q_, k_, cos_b, sin_b, rot_b, rot_t_b), qraw_b, kr_in)
        parts = []
        for hh in range(hm):
            parts += [dqn_b[:, hh * NOPE:(hh + 1) * NOPE], dqr_b[:, hh * NOPE:(hh + 1) * NOPE]]
        dkr = dkrh_b[:, :LANES]
        for hh in range(1, hm):
            dkr = dkr + dkrh_b[:, hh * LANES:(hh + 1) * LANES]
        d_qraw, d_kr_in = vjp((jnp.concatenate(parts, axis=1), dkr))
        return d_qraw, jnp.concatenate([dkn_b, dv_b], axis=1), d_kr_in

    d_qraw, d_kv, d_kr_in = _rowwise(rope_bwd, [qraw, kr_view, cosx, sinx, d_qn, d_qr, d_kn, d_v_att, d_kr_h],
                                     [rot, rot_t], [(hm * QHEAD, BF16), (2 * MW, BF16), (LANES, F32)], tile=T,
                                     name="rope_bwd")
    d_qnorm = _mm(d_qraw, W["wq_b_t"], name="d_qn")
    d_kvnorm = _mm(d_kv, W["wkv_b"], tb=True, name="d_kvn")
    g_wq_b = _mm(d_qraw, qn, ta=True, out_dtype=BF16, name="g_wq_b")
    g_wkv_b = _mm(kvn, d_kv, ta=True, out_dtype=BF16, name="g_wkv_b")

    def mla_norm_bwd(q_a, kv_a, qg, kvg, dq, dk):
        _, vjp = jax.vjp(_f_mla_norm, q_a, kv_a, qg, kvg)
        d_q_a, d_kv_a, d_qg, d_kvg = vjp((dq, dk))
        return jnp.concatenate([d_q_a, d_kv_a], axis=1), d_qg, d_kvg

    d_proj, g_q_norm, g_kv_norm = _rowwise(
        lambda q_a, kv_a, dq, dk, qg, kvg: mla_norm_bwd(q_a, kv_a, qg, kvg, dq, dk),
        [col(proj, "q_a"), col(proj, "kv_a"), d_qnorm, d_kvnorm], [W["mla_q_norm"], W["mla_kv_norm"]],
        [(QR + KVR, BF16, (d_proj, d_in, lay["q_a"][0]))], [(1, QR), (1, KVR)], tile=T, name="mla_norm_bwd")

    def pre_bwd(k_b_, tail_b, dlwf, dlwb, dkf, dkb, dkbon, daf, dab, dbf, dbb, drf, drb, drbon, dvf, dvb, dvbon,
                dkr, *params):
        _, vjp = jax.vjp(pre_fn, k_b_, tail_b, *params[:8], params[8], params[9])
        g = vjp((dlwf, dlwb, dkf + dkbon, dkb + dkbon, daf + dab, dbf, dbb))
        d_tail = g[1] + jnp.concatenate([dkr, jnp.zeros((dkr.shape[0], TAIL - LANES), F32)], axis=1)
        d_rl = jnp.concatenate([drf + drb + drbon, g[0], dvf + dvb + dvbon, d_tail], axis=1)
        return (d_rl,) + tuple(g[2:10])

    f_, b_ = dsc["f"], dsc["b"]
    pre_bwd_rows = [rl_k, rl_tail, f_[1], b_[1], f_[2], b_[2], d_k_bonus, f_[4], b_[4], f_[5], b_[5],
                    f_[0], b_[0], d_r_bonus, f_[3], b_[3], d_v_bonus, d_kr_in]
    (d_rl, g_w0_f, g_w0_b, g_a0_f, g_a0_b, g_k_k, g_k_a, g_w2cat, g_a2cat) = _rowwise(
        pre_bwd, pre_bwd_rows, pre_params, [(3 * RW + TAIL, F32)],
        [(1, RW)] * 6 + [(TAIL, 2 * RW)] * 2, tile=T // 2, name="rwkv_pre_bwd")
    d_proj, g_mu = _shift_lerp(shift_view, W["mu"], d_rl, (d_proj, lay["r"][0]), name="shift_bwd")
    small = dict(wq_b=g_wq_b, wkv_b=g_wkv_b, w2cat=g_w2cat, a2cat=g_a2cat, w_br_mla=g_w_br_mla,
                 w_br_rwkv=g_w_br_rwkv, w_out=g_w_out)
    if exchange is None:
        received = None
        g_w_in = _mm(d_proj, h, ta=True, out_dtype=BF16, tn_cap=1024, name="g_w_in")
        d_h = _mm(d_proj, W["w_in_t"], tn_cap=1024, name="d_h")
    else:
        slabs = _restore_rest(small, dims)
        slabs = [slabs[n] for n in _MATS[1:]]
        g_w_in, *got = _mm(d_proj, h, ta=True, out_dtype=BF16, tn_cap=1024, ride=_sibling_swap_plan(slabs),
                           name="g_w_in")
        sums = [_pair_add(exchange[1], s, t, name="pair_add_" + n) for n, s, t in zip(_MATS[1:], slabs, got)]
        d_h, *received = _mm(d_proj, W["w_in_t"], tn_cap=1024, ride=_chip_exchange_plan(sums), name="d_h")
        small = {}

    def pre_norm_bwd(xb, dyb, dhb, g):
        _, vjp = jax.vjp(_rms, xb, g)
        dx, dg = vjp(dhb)
        return dyb + dx, dg

    grad_x, g_g_pre = _rowwise(pre_norm_bwd, [x, dy, d_h], [W["g_pre"]], [(D, F32)], [(1, D)], tile=T,
                               name="pre_norm_bwd")

    grads = dict(g_pre=g_g_pre, w_in=g_w_in, mla_q_norm=g_q_norm, mla_kv_norm=g_kv_norm, mu=g_mu, w0_f=g_w0_f,
                 w0_b=g_w0_b, a0_f=g_a0_f, a0_b=g_a0_b, k_k=g_k_k, k_a=g_k_a, r_k=g_r_k, gn_g=g_gn_g, gn_b=g_gn_b,
                 g_post=g_g_post, **small)
    return loss[0, 0], grad_x, grads, received


_MATS = ["w_in", "mla_wq_b", "mla_wkv_b", "rwkv_w2_f", "rwkv_w2_b", "rwkv_a2_f", "rwkv_a2_b", "w_br_mla",
         "w_br_rwkv", "w_out"]
_ROW_SHARDED = ("w_out",)
_TRANSPOSED = ("w_in", "mla_wq_b")
_VECS = ["g_pre", "mla_q_norm", "mla_kv_norm", "rwkv_mu", "rwkv_w0_f", "rwkv_w0_b", "rwkv_a0_f", "rwkv_a0_b",
         "rwkv_k_k", "rwkv_k_a", "rwkv_r_k", "rwkv_gn_g", "rwkv_gn_b", "g_post"]
_WEIGHTS = ["g_pre", "w_in", "mla_q_norm", "mla_wq_b", "mla_kv_norm", "mla_wkv_b", "rwkv_mu", "rwkv_w0_f",
            "rwkv_w2_f", "rwkv_w0_b", "rwkv_w2_b", "rwkv_a0_f", "rwkv_a2_f", "rwkv_a0_b", "rwkv_a2_b", "rwkv_k_k",
            "rwkv_k_a", "rwkv_r_k", "rwkv_gn_g", "rwkv_gn_b", "w_br_mla", "w_br_rwkv", "w_out", "g_post"]

def _exchange(srcs, *, name):
    n = len(srcs)

    def body(*refs):
        src_refs, out_refs = refs[:n], refs[n:2 * n]
        send_sems, recv_sems, local_sems = refs[2 * n:]
        x, y, c = lax.axis_index("x"), lax.axis_index("y"), lax.axis_index("c")
        me = 4 * x + 2 * y + c
        flip = lambda v, bit: (1 - v) if bit else v

        def piece(a, idx):
            return src_refs[a] if srcs[a].ndim == 2 else src_refs[a].at[idx]

        owns = [pltpu.make_async_copy(piece(a, me), out_refs[a].at[me], local_sems.at[a]) for a in range(n)]
        for cp in owns:
            cp.start()
        sends, peers = [], []
        for d in range(1, N_DEV):
            px, py, pc = flip(x, d & 4), flip(y, d & 2), flip(c, d & 1)
            pidx = 4 * px + 2 * py + pc
            peers.append(((px, py, pc), pidx))
            for a in range(n):
                cp = pltpu.make_async_remote_copy(
                    src_ref=piece(a, pidx), dst_ref=out_refs[a].at[me], send_sem=send_sems.at[d - 1, a],
                    recv_sem=recv_sems.at[d - 1, a], device_id=(px, py, pc), device_id_type=pl.DeviceIdType.MESH)
                cp.start()
                sends.append(cp)
        for d, (peer, pidx) in zip(range(1, N_DEV), peers):
            for a in range(n):
                pltpu.make_async_remote_copy(
                    src_ref=piece(a, pidx), dst_ref=out_refs[a].at[pidx], send_sem=send_sems.at[d - 1, a],
                    recv_sem=recv_sems.at[d - 1, a], device_id=peer, device_id_type=pl.DeviceIdType.MESH).wait_recv()
        for cp in sends:
            cp.wait_send()
        for cp in owns:
            cp.wait()

    return pl.pallas_call(
        body, name=name,
        out_shape=[jax.ShapeDtypeStruct((N_DEV,) + s.shape[-2:], s.dtype) for s in srcs],
        in_specs=[pl.BlockSpec(memory_space=pl.ANY)] * n, out_specs=[pl.BlockSpec(memory_space=pl.ANY)] * n,
        scratch_shapes=[pltpu.SemaphoreType.DMA((N_DEV - 1, n)), pltpu.SemaphoreType.DMA((N_DEV - 1, n)),
                        pltpu.SemaphoreType.DMA((n,))],
    )(*srcs)


def _remote(src, dst, sems, key, to):
    send_sems, recv_sems = sems
    return pltpu.make_async_remote_copy(src_ref=src, dst_ref=dst, send_sem=send_sems.at[key], recv_sem=recv_sems.at[key],
                                        device_id=to, device_id_type=pl.DeviceIdType.MESH)


def _run_exchange(plan, *, name):
    srcs, out_shapes, sem_shapes, phases = plan
    n, m = len(srcs), len(out_shapes)

    def body(*refs):
        for phase in phases(refs[:n], refs[n:n + m], refs[n + m:]):
            phase()

    return pl.pallas_call(
        body, name=name, out_shape=out_shapes,
        in_specs=[pl.BlockSpec(memory_space=pl.ANY)] * n, out_specs=[pl.BlockSpec(memory_space=pl.ANY)] * m,
        scratch_shapes=[pltpu.SemaphoreType.DMA(s) for s in sem_shapes],
    )(*srcs)


def _gather_plan(srcs):
    n = len(srcs)

    def phases(src_refs, out_refs, sem_refs):
        sems, local_sems = sem_refs[:2], sem_refs[2]
        x, y, c = lax.axis_index("x"), lax.axis_index("y"), lax.axis_index("c")
        idx = lambda px, py, pc: 4 * px + 2 * py + pc
        me, sibling = (x, y, c), (x, y, 1 - c)
        chips = [(1 - x, y), (x, 1 - y), (1 - x, 1 - y)]
        own = lambda a: pltpu.make_async_copy(src_refs[a], out_refs[a].at[idx(*me)], local_sems.at[a])
        to_sibling = lambda a: _remote(src_refs[a], out_refs[a].at[idx(*me)], sems, (0, a), sibling)
        to_chip = lambda a, j: _remote(src_refs[a], out_refs[a].at[idx(*me)], sems, (1 + j, a), (*chips[j], c))
        landed = lambda a, j: out_refs[a].at[idx(*chips[j], c)]
        passed_on = lambda a, j: _remote(landed(a, j), landed(a, j), sems, (4 + j, a), sibling)

        def first():
            for a in range(n):
                own(a).start()
                to_sibling(a).start()
                for j in range(3):
                    to_chip(a, j).start()

        def middle():
            for j in range(3):
                for a in range(n):
                    _remote(landed(a, j), landed(a, j), sems, (1 + j, a), me).wait_recv()
                    passed_on(a, j).start()

        def last():
            for a in range(n):
                blk = out_refs[a].at[idx(*sibling)]
                _remote(blk, blk, sems, (0, a), me).wait_recv()
                for j in range(3):
                    blk = out_refs[a].at[idx(*chips[j], 1 - c)]
                    _remote(blk, blk, sems, (4 + j, a), me).wait_recv()
            for a in range(n):
                to_sibling(a).wait_send()
                for j in range(3):
                    to_chip(a, j).wait_send()
                    passed_on(a, j).wait_send()
                own(a).wait()

        return first, middle, last

    return srcs, [jax.ShapeDtypeStruct((N_DEV,) + s.shape, s.dtype) for s in srcs], [(7, n), (7, n), (n,)], phases


def _sibling_swap_plan(srcs):
    n = len(srcs)

    def phases(src_refs, out_refs, sems):
        x, y, c = lax.axis_index("x"), lax.axis_index("y"), lax.axis_index("c")
        copies = lambda: [_remote(src_refs[a].at[2 * q + 1 - c], out_refs[a].at[q], sems, (q, a), (x, y, 1 - c))
                          for a in range(n) for q in range(4)]

        def first():
            for cp in copies():
                cp.start()

        def last():
            for cp in copies():
                cp.wait()

        return first, (lambda: None), last

    return srcs, [jax.ShapeDtypeStruct((4,) + s.shape[1:], s.dtype) for s in srcs], [(4, n), (4, n)], phases


def _chip_exchange_plan(srcs):
    n = len(srcs)

    def phases(src_refs, out_refs, sem_refs):
        sems, local_sems = sem_refs[:2], sem_refs[2]
        x, y, c = lax.axis_index("x"), lax.axis_index("y"), lax.axis_index("c")
        mine = 2 * x + y
        chips = [(1 - x, y), (x, 1 - y), (1 - x, 1 - y)]
        own = lambda a: pltpu.make_async_copy(src_refs[a].at[mine], out_refs[a].at[mine], local_sems.at[a])
        send = lambda a, j: _remote(src_refs[a].at[2 * chips[j][0] + chips[j][1]], out_refs[a].at[mine], sems, (j, a),
                                    (*chips[j], c))

        def first():
            for a in range(n):
                own(a).start()
                for j in range(3):
                    send(a, j).start()

        def last():
            for j in range(3):
                for a in range(n):
                    blk = out_refs[a].at[2 * chips[j][0] + chips[j][1]]
                    _remote(blk, blk, sems, (j, a), (x, y, c)).wait_recv()
            for a in range(n):
                for j in range(3):
                    send(a, j).wait_send()
                own(a).wait()

        return first, (lambda: None), last

    return srcs, [jax.ShapeDtypeStruct(s.shape, s.dtype) for s in srcs], [(3, n), (3, n), (n,)], phases


def _pair_add(core, g, got, *, name):
    q, r, c = got.shape
    tr, tc = _tile2d(r, c)

    def body(core_ref, a_ref, b_ref, o_ref):
        o_ref[...] = (a_ref[...].astype(F32) + b_ref[...].astype(F32)).astype(BF16)

    blk = pl.BlockSpec((1, tr, tc), lambda i, j, k, core_ref: (i, j, k))
    mine = pl.BlockSpec((1, tr, tc), lambda i, j, k, core_ref: (2 * i + core_ref[0], j, k))
    return pl.pallas_call(
        body, name=name, out_shape=jax.ShapeDtypeStruct(got.shape, BF16),
        grid_spec=pltpu.PrefetchScalarGridSpec(num_scalar_prefetch=1, grid=(q, r // tr, c // tc),
                                               in_specs=[mine, blk], out_specs=blk),
        compiler_params=_cparams(("parallel", "parallel", "parallel")))(core, g, got)


def _adamw(recv, w, m, v, *, name):
    r, c = w.shape
    n_terms = recv.shape[0]
    tr, tc = _tile2d(r, c)

    def body(g_ref, w_ref, m_ref, v_ref, go_ref, d_ref, mo_ref, vo_ref):
        g = g_ref[0].astype(F32)
        for k in range(1, n_terms):
            g = g + g_ref[k].astype(F32)
        m_new = ADAM_B1 * m_ref[...] + (1.0 - ADAM_B1) * g
        v_new = ADAM_B2 * v_ref[...] + (1.0 - ADAM_B2) * (g * g)
        m_hat = m_new / (1.0 - ADAM_B1 ** ADAM_STEP)
        v_hat = v_new / (1.0 - ADAM_B2 ** ADAM_STEP)
        go_ref[...] = g
        d_ref[...] = -ADAM_LR * (m_hat / (jnp.sqrt(v_hat) + ADAM_EPS) + ADAM_WD * w_ref[...])
        mo_ref[...] = m_new
        vo_ref[...] = v_new

    blk = pl.BlockSpec((tr, tc), lambda i, j: (i, j))
    return pl.pallas_call(
        body, name=name, grid=(r // tr, c // tc),
        in_specs=[pl.BlockSpec((n_terms, tr, tc), lambda i, j: (0, i, j)), blk, blk, blk], out_specs=[blk] * 4,
        out_shape=[jax.ShapeDtypeStruct((r, c), F32)] * 4, compiler_params=_cparams(("parallel", "parallel")),
    )(recv, w, m, v)


def _tile2d(r, c, cap=256):
    if r <= cap:
        return r, c
    for t in range(cap, 0, -BF16_ROWS):
        if r % t == 0:
            return t, c
    return r, _pick(c, cap)


def _pack(pieces, dtype, quantum):
    out = []
    for p in pieces:
        lead, n = p.shape[:-1], p.shape[-1]
        pad = (-n) % quantum
        p = p.astype(dtype)
        if pad:
            p = jnp.concatenate([p, jnp.zeros(lead + (pad,), dtype)], axis=-1)
        out.append(p)
    flat = jnp.concatenate(out, axis=-1)
    return flat.reshape(flat.shape[:-1] + (flat.shape[-1] // LANES, LANES))


def _unpack(flat, sizes, quantum):
    flat = flat.reshape(flat.shape[:-2] + (-1,))
    out, o = [], 0
    for n in sizes:
        out.append(flat[..., o:o + n])
        o += n + (-n) % quantum
    return out


def _prepare_weights(full, vec, dims):
    rest = {n: t for n, t in full.items() if n != "w_in"}
    return {"w_in_t": _prepare_w_in(full["w_in"], dims), **_prepare_rest(rest, dims), **_prepare_vectors(vec, dims)}


def _prepare_w_in(slabs, dims):
    D = dims["D"]
    c = slabs.shape[1]
    parts, pos = [], 0
    for orig_off, width, perm_off in sorted(dims["segs"], key=lambda t: t[2]):
        if perm_off > pos:
            parts.append(jnp.zeros((perm_off - pos, D), BF16))
        for k in range(N_DEV):
            lo, hi = max(orig_off, k * c), min(orig_off + width, (k + 1) * c)
            if lo < hi:
                parts.append(slabs[k][lo - k * c:hi - k * c])
        pos = perm_off + width
    if dims["d_in_perm"] > pos:
        parts.append(jnp.zeros((dims["d_in_perm"] - pos, D), BF16))
    return jnp.concatenate(parts, axis=0)


def _prepare_rest(full, dims):
    hm, hr, hn, rank = dims["hm"], dims["hr"], dims["hn"], dims["rank"]
    QR, KVR = dims["QR"], dims["KVR"]
    RW, TAIL = hr * hn, dims["TAIL"]
    full = {n: (t.reshape(-1, t.shape[2]) if n in _ROW_SHARDED + _TRANSPOSED
                else t.transpose(1, 0, 2).reshape(t.shape[1], -1)) for n, t in full.items()}
    wq = full["mla_wq_b"].reshape(hm, NOPE + ROPE, QR)
    wq = jnp.concatenate([wq, jnp.zeros((hm, QHEAD - NOPE - ROPE, QR), BF16)], axis=1).reshape(hm * QHEAD, QR)
    wkv = full["mla_wkv_b"].reshape(KVR, hm, 2, NOPE).transpose(0, 2, 1, 3).reshape(KVR, 2 * hm * NOPE)
    z = lambda rows: jnp.zeros((rows, RW), F32)
    f = lambda nme: full[nme].astype(F32)
    w2cat = jnp.concatenate([
        jnp.concatenate([z(ROPE), f("rwkv_w2_f"), z(TAIL - ROPE - rank)], axis=0),
        jnp.concatenate([z(ROPE + rank), f("rwkv_w2_b"), z(TAIL - ROPE - 2 * rank)], axis=0)], axis=1)
    a2cat = jnp.concatenate([
        jnp.concatenate([z(ROPE + 2 * rank), f("rwkv_a2_f"), z(TAIL - ROPE - 3 * rank)], axis=0),
        jnp.concatenate([z(ROPE + 3 * rank), f("rwkv_a2_b"), z(TAIL - ROPE - 4 * rank)], axis=0)], axis=1)
    return dict(wq_b_t=wq, wkv_b=wkv, w2cat=w2cat, a2cat=a2cat, w_br_mla=full["w_br_mla"],
                w_br_rwkv=full["w_br_rwkv"], w_out=full["w_out"])


def _prepare_vectors(vec, dims):
    rank, RW, TAIL = dims["rank"], dims["hr"] * dims["hn"], dims["TAIL"]
    mu = vec["rwkv_mu"]
    mu_p = jnp.concatenate([mu[:3 * RW], jnp.zeros((ROPE,), F32), mu[3 * RW:],
                            jnp.zeros((TAIL - ROPE - 4 * rank,), F32)])
    row = lambda t: t.reshape(1, -1)
    return dict(
        mu=row(mu_p), g_pre=row(vec["g_pre"]), g_post=row(vec["g_post"]), mla_q_norm=row(vec["mla_q_norm"]),
        mla_kv_norm=row(vec["mla_kv_norm"]), w0_f=row(vec["rwkv_w0_f"]), w0_b=row(vec["rwkv_w0_b"]),
        a0_f=row(vec["rwkv_a0_f"]), a0_b=row(vec["rwkv_a0_b"]), k_k=row(vec["rwkv_k_k"]), k_a=row(vec["rwkv_k_a"]),
        r_k=row(vec["rwkv_r_k"]), gn_g=row(vec["rwkv_gn_g"]), gn_b=row(vec["rwkv_gn_b"]))


def _restore_grads(g, dims):
    return {"w_in": _restore_w_in(g["w_in"], dims), **_restore_rest(g, dims), **_restore_vectors(g, dims)}


def _restore_w_in(gw, dims):
    c = dims["d_in"] // N_DEV
    slabs = []
    for k in range(N_DEV):
        parts = []
        for orig_off, width, perm_off in sorted(dims["segs"]):
            lo_, hi_ = max(orig_off, k * c), min(orig_off + width, (k + 1) * c)
            if lo_ < hi_:
                parts.append(gw[perm_off + lo_ - orig_off:perm_off + hi_ - orig_off])
        slabs.append(jnp.concatenate(parts, axis=0))
    return jnp.stack(slabs)


def _restore_rest(g, dims):
    hm, hr, hn, rank = dims["hm"], dims["hr"], dims["hn"], dims["rank"]
    QR, KVR, RW = dims["QR"], dims["KVR"], hr * hn
    wq = g["wq_b"].reshape(hm, QHEAD, QR)[:, :NOPE + ROPE].reshape(N_DEV, -1, QR)
    wkv = g["wkv_b"].reshape(KVR, 2, hm, NOPE).transpose(0, 2, 1, 3).reshape(KVR, 2 * hm * NOPE)
    lo = lambda t, i, half: t[ROPE + i * rank:ROPE + (i + 1) * rank, half * RW:(half + 1) * RW].astype(BF16)
    cols = lambda t: t.reshape(t.shape[0], N_DEV, -1).transpose(1, 0, 2)
    return dict(
        mla_wq_b=wq, mla_wkv_b=cols(wkv), rwkv_w2_f=cols(lo(g["w2cat"], 0, 0)),
        rwkv_w2_b=cols(lo(g["w2cat"], 1, 1)), rwkv_a2_f=cols(lo(g["a2cat"], 2, 0)),
        rwkv_a2_b=cols(lo(g["a2cat"], 3, 1)), w_br_mla=cols(g["w_br_mla"]), w_br_rwkv=cols(g["w_br_rwkv"]),
        w_out=g["w_out"].reshape(N_DEV, -1, g["w_out"].shape[1]))


def _restore_vectors(g, dims):
    rank, RW = dims["rank"], dims["hr"] * dims["hn"]
    mu = g["mu"][0]
    out = dict(
        rwkv_mu=jnp.concatenate([mu[:3 * RW], mu[3 * RW + ROPE:3 * RW + ROPE + 4 * rank]]),
        g_pre=g["g_pre"][0], g_post=g["g_post"][0], mla_q_norm=g["mla_q_norm"][0], mla_kv_norm=g["mla_kv_norm"][0],
        rwkv_w0_f=g["w0_f"][0], rwkv_w0_b=g["w0_b"][0], rwkv_a0_f=g["a0_f"][0], rwkv_a0_b=g["a0_b"][0],
        rwkv_k_k=g["k_k"][0], rwkv_k_a=g["k_a"][0], rwkv_r_k=g["r_k"][0], rwkv_gn_g=g["gn_g"][0],
        rwkv_gn_b=g["gn_b"][0])
    return out


def _dims(inp):
    D = inp["x"].shape[-1]
    QR, KVR = inp["mla_q_norm"].shape[0], inp["mla_kv_norm"].shape[0]
    hm = inp["mla_wq_b"].shape[1] * N_DEV // (NOPE + ROPE)
    hr, hn = inp["rwkv_r_k"].shape
    rank = inp["rwkv_w2_f"].shape[0]
    MW, RW = hm * VDIM, hr * hn
    TAIL = -(-(ROPE + 4 * rank) // LANES) * LANES
    orig, o = {}, 0
    for nme, w in (("q_a", QR), ("kv_a", KVR), ("k_rope", ROPE), ("rkv", 3 * RW), ("lora", 4 * rank), ("z_m", MW),
                   ("z_r", RW), ("gate_m", D), ("gate_r", D)):
        orig[nme] = (o, w)
        o += w
    assert o == inp["w_in"].shape[1] * N_DEV
    lay, d_in_perm = _layout(D, MW, RW, TAIL, QR, KVR)
    perm_off = dict(q_a=lay["q_a"][0], kv_a=lay["kv_a"][0], k_rope=lay["tail"][0], rkv=lay["r"][0],
                    lora=lay["tail"][0] + ROPE, z_m=lay["z_m"][0], z_r=lay["z_r"][0], gate_m=lay["gate_m"][0],
                    gate_r=lay["gate_r"][0])
    segs = [(orig[nme][0], orig[nme][1], perm_off[nme]) for nme in orig]
    return dict(D=D, QR=QR, KVR=KVR, hm=hm, hr=hr, hn=hn, rank=rank, TAIL=TAIL, segs=segs, d_in=o,
                d_in_perm=d_in_perm)


def kernel(x, g_pre, w_in, mla_q_norm, mla_wq_b, mla_kv_norm, mla_wkv_b, rwkv_mu, rwkv_w0_f, rwkv_w2_f, rwkv_w0_b, rwkv_w2_b, rwkv_a0_f, rwkv_a2_f, rwkv_a0_b, rwkv_a2_b, rwkv_k_k, rwkv_k_a, rwkv_r_k, rwkv_gn_g, rwkv_gn_b, w_br_mla, w_br_rwkv, w_out, g_post, loss_target, m_g_pre, m_w_in, m_mla_q_norm, m_mla_wq_b, m_mla_kv_norm, m_mla_wkv_b, m_rwkv_mu, m_rwkv_w0_f, m_rwkv_w2_f, m_rwkv_w0_b, m_rwkv_w2_b, m_rwkv_a0_f, m_rwkv_a2_f, m_rwkv_a0_b, m_rwkv_a2_b, m_rwkv_k_k, m_rwkv_k_a, m_rwkv_r_k, m_rwkv_gn_g, m_rwkv_gn_b, m_w_br_mla, m_w_br_rwkv, m_w_out, m_g_post, v_g_pre, v_w_in, v_mla_q_norm, v_mla_wq_b, v_mla_kv_norm, v_mla_wkv_b, v_rwkv_mu, v_rwkv_w0_f, v_rwkv_w2_f, v_rwkv_w0_b, v_rwkv_w2_b, v_rwkv_a0_f, v_rwkv_a2_f, v_rwkv_a0_b, v_rwkv_a2_b, v_rwkv_k_k, v_rwkv_k_a, v_rwkv_r_k, v_rwkv_gn_g, v_rwkv_gn_b, v_w_br_mla, v_w_br_rwkv, v_w_out, v_g_post):
    inp = dict(locals())
    dims = _dims(inp)
    stored = lambda t, n: t.T if n in _TRANSPOSED else t
    assert _MATS[0] == "w_in"
    shards = [stored(inp[n], n).astype(BF16) for n in _MATS]
    core = lax.axis_index("c").astype(jnp.int32).reshape(1)
    (w_in_slabs,) = _run_exchange(_gather_plan(shards[:1]), name="gather_w_in")
    W = {"w_in_t": _prepare_w_in(w_in_slabs, dims), **_prepare_vectors({n: inp[n] for n in _VECS}, dims)}
    loss, grad_x, g, recv_rest = _local_grads(x[0], loss_target[0], W, dims, exchange=(shards[1:], core))
    loss = lax.psum(loss, ("x", "y", "c"))

    new = {}
    g_w_in = _restore_w_in(g["w_in"], dims)
    (got,) = _run_exchange(_sibling_swap_plan([g_w_in]), name="pair_swap_w_in")
    (recv_w_in,) = _run_exchange(_chip_exchange_plan([_pair_add(core, g_w_in, got, name="pair_add_w_in")]),
                                 name="scatter_w_in")
    g = _restore_vectors(g, dims)
    for n, t in zip(_MATS, [recv_w_in] + recv_rest):
        out = _adamw(t, stored(inp[n], n), stored(inp["m_" + n], n), stored(inp["v_" + n], n), name="adamw_" + n)
        new[n] = [stored(o, n) for o in out]

    vsizes = [inp[n].size for n in _VECS]
    vflat = lambda prefix, src: _pack([src[prefix + n].reshape(-1) for n in _VECS], F32, LANES * 8)
    (vrecv,) = _exchange([vflat("", g)], name="gather_vector_grads")
    vout = _adamw(vrecv, vflat("", inp), vflat("m_", inp), vflat("v_", inp), name="adamw_vectors")
    vparts = [_unpack(t, vsizes, LANES * 8) for t in vout]
    for i, n in enumerate(_VECS):
        new[n] = [vp[i].reshape(inp[n].shape) for vp in vparts]

    outs = [loss, grad_x[None]]
    for k in range(4):
        outs += [new[n][k] for n in _WEIGHTS]
    return tuple(outs)
```

```python
import functools
import math

import jax
import jax.numpy as jnp
from jax import lax
from jax.experimental import pallas as pl
from jax.experimental.pallas import tpu as pltpu

F32 = jnp.float32
BF16 = jnp.bfloat16

N_DEV = 8
LANES = 128
BF16_ROWS = 16
NOPE, ROPE, VDIM = 128, 64, 128
QHEAD = 256
ROPE_THETA = 10000.0
NORM_EPS = 1e-6
GN_EPS = 64e-5
CHUNK = 64
SUB = 16
VMEM_LIMIT = 56 * 1024 * 1024

ADAM_LR, ADAM_B1, ADAM_B2, ADAM_EPS, ADAM_WD, ADAM_STEP = 0.001, 0.9, 0.999, 1e-08, 0.01, 10


def _cparams(sem):
    return pltpu.CompilerParams(dimension_semantics=sem, vmem_limit_bytes=VMEM_LIMIT)


def _pick(n, cap):
    if n <= cap:
        return n
    for t in range(cap - cap % LANES, 0, -LANES):
        if n % t == 0:
            return t
    raise ValueError(f"no tile for {n} under {cap}")


def _mm(a, b, *, ta=False, tb=False, out_dtype=F32, name, tm_cap=1024, tn_cap=512, tk_cap=2048, ride=None):
    K, M = a.shape if ta else a.shape[::-1]
    N = b.shape[0] if tb else b.shape[1]
    assert (b.shape[1] if tb else b.shape[0]) == K, (a.shape, b.shape, ta, tb)
    tm, tn, tk = _pick(M, tm_cap), _pick(N, tn_cap), _pick(K, tk_cap)
    nj, nk = N // tn, K // tk
    steps = (M // tm) * nj * nk
    dn = (((0 if ta else 1,), (1 if tb else 0,)), ((), ()))
    srcs, extra_shapes, sem_shapes, phases = ride if ride else ((), (), (), None)
    n_src, n_extra = len(srcs), len(extra_shapes)

    def body(*refs):
        a_ref, b_ref, o_ref = refs[0], refs[1], refs[2 + n_src]
        acc_ref = refs[3 + n_src + n_extra]
        k = pl.program_id(2)
        if ride:
            step = (pl.program_id(0) * nj + pl.program_id(1)) * nk + k
            first, middle, last = phases(refs[2:2 + n_src], refs[3 + n_src:3 + n_src + n_extra],
                                         refs[4 + n_src + n_extra:])
            pl.when(step == 0)(first)
            pl.when(step == (steps * 7) // 8)(middle)
        p = lax.dot_general(a_ref[...], b_ref[...], dn, preferred_element_type=F32)

        @pl.when(k == 0)
        def _():
            acc_ref[...] = p

        @pl.when(k > 0)
        def _():
            acc_ref[...] += p

        @pl.when(k == nk - 1)
        def _():
            o_ref[...] = acc_ref[...].astype(out_dtype)

        if ride:
            pl.when(step == steps - 1)(last)

    a_spec = pl.BlockSpec((tk, tm), lambda i, j, k: (k, i)) if ta else pl.BlockSpec((tm, tk), lambda i, j, k: (i, k))
    b_spec = pl.BlockSpec((tn, tk), lambda i, j, k: (j, k)) if tb else pl.BlockSpec((tk, tn), lambda i, j, k: (k, j))
    hbm = pl.BlockSpec(memory_space=pl.ANY)
    out = pl.pallas_call(
        body, name=name, grid=(M // tm, nj, nk),
        in_specs=[a_spec, b_spec] + [hbm] * n_src,
        out_specs=[pl.BlockSpec((tm, tn), lambda i, j, k: (i, j))] + [hbm] * n_extra,
        out_shape=[jax.ShapeDtypeStruct((M, N), out_dtype)] + list(extra_shapes),
        scratch_shapes=[pltpu.VMEM((tm, tn), F32)] + [pltpu.SemaphoreType.DMA(s) for s in sem_shapes],
        compiler_params=_cparams(("arbitrary",) * 3 if ride else ("parallel", "parallel", "arbitrary")),
    )(a, b, *srcs)
    return out if ride else out[0]


def _view(arr, off, width):
    assert off % width == 0, (off, width)
    return (arr, off // width, width)


def _rowwise(fn, rows, params, out_rows, out_accs=(), *, tile, name):
    rows = [r if isinstance(r, tuple) else (r, 0, r.shape[1]) for r in rows]
    S = rows[0][0].shape[0]
    T = min(tile, S)
    assert S % T == 0
    n_rows, n_par, n_out = len(rows), len(params), len(out_rows)
    into = [o[2] if len(o) == 3 else None for o in out_rows]
    carried = [t[0] for t in into if t is not None and t[0] is not None]

    def body(*refs):
        ins = [r[...] for r in refs[:n_rows + n_par]]
        outs = fn(*ins)
        out_refs = refs[n_rows + n_par + len(carried):]
        for o_ref, val in zip(out_refs[:n_out], outs[:n_out]):
            o_ref[...] = val.astype(o_ref.dtype)
        i = pl.program_id(0)
        for o_ref, val in zip(out_refs[n_out:], outs[n_out:]):
            @pl.when(i == 0)
            def _(o_ref=o_ref, val=val):
                o_ref[...] = val

            @pl.when(i > 0)
            def _(o_ref=o_ref, val=val):
                o_ref[...] += val

    in_specs = [pl.BlockSpec((T, w), functools.partial(lambda i, cb: (i, cb), cb=cb)) for _, cb, w in rows]
    in_specs += [pl.BlockSpec(p.shape, lambda i: (0, 0)) for p in params]
    in_specs += [pl.BlockSpec(memory_space=pl.ANY)] * len(carried)
    out_specs, out_shape, aliases = [], [], {}
    for k, (o, t) in enumerate(zip(out_rows, into)):
        w, dt = o[0], o[1]
        if t is None:
            out_specs.append(pl.BlockSpec((T, w), lambda i: (i, 0)))
            out_shape.append(jax.ShapeDtypeStruct((S, w), dt))
            continue
        buf, total, first = t
        assert first % w == 0
        out_specs.append(pl.BlockSpec((T, w), functools.partial(lambda i, cb: (i, cb), cb=first // w)))
        out_shape.append(jax.ShapeDtypeStruct((S, total), dt))
        if buf is not None:
            aliases[n_rows + n_par + len(aliases)] = k
    out_specs += [pl.BlockSpec(s, lambda i: (0, 0)) for s in out_accs]
    out_shape += [jax.ShapeDtypeStruct(s, F32) for s in out_accs]
    return pl.pallas_call(
        body, name=name, grid=(S // T,), in_specs=in_specs, out_specs=out_specs, out_shape=out_shape,
        input_output_aliases=aliases, compiler_params=_cparams(("arbitrary",)),
    )(*[r[0] for r in rows], *params, *carried)


def _mm_sel(x, sel):
    hi = x.astype(BF16)
    lo = (x - hi.astype(F32)).astype(BF16)
    d = lambda u: jnp.dot(u, sel, preferred_element_type=F32)
    return d(hi) + d(lo)


@jax.custom_vjp
def _sel(x, sel, sel_t):
    return _mm_sel(x, sel)


def _sel_fwd(x, sel, sel_t):
    return _mm_sel(x, sel), (sel, sel_t)


def _sel_bwd(res, ct):
    sel, sel_t = res
    return _mm_sel(ct, sel_t), jnp.zeros_like(sel), jnp.zeros_like(sel_t)


_sel.defvjp(_sel_fwd, _sel_bwd)


def _rms(x, g):
    return x * lax.rsqrt(jnp.mean(x * x, axis=-1, keepdims=True) + NORM_EPS) * g


def _sigmoid(x):
    return 1.0 / (1.0 + jnp.exp(-x))


def _silu(x):
    return x * _sigmoid(x)


def _softplus(x):
    return jnp.maximum(x, 0.0) + jnp.log(1.0 + jnp.exp(-jnp.abs(x)))


def _bdot(x, w):
    return jnp.dot(x.astype(BF16), w.astype(BF16), preferred_element_type=F32)


def _f_mla_norm(q_a, kv_a, qg, kvg):
    return _rms(q_a, qg), _rms(kv_a, kvg)


def _f_rope(hm, qraw, kr_in, cosx, sinx, rot, rot_t):
    def rope(t):
        return t * cosx + _sel(t, rot, rot_t) * sinx
    parts = []
    for h in range(hm):
        parts.append(qraw[:, h * QHEAD:h * QHEAD + NOPE])
        parts.append(rope(qraw[:, h * QHEAD + NOPE:(h + 1) * QHEAD]))
    return jnp.concatenate(parts, axis=1), rope(kr_in)


def _f_rwkv_pre(rw, k, tail, w0f, w0b, a0f, a0b, k_k, k_a, w2cat, a2cat, seg, seg_t):
    split = w2cat.shape[0]
    zw = jnp.dot(jnp.tanh(tail[:, :split]).astype(BF16), w2cat, preferred_element_type=F32)
    za = jnp.dot(tail[:, split:].astype(BF16), a2cat, preferred_element_type=F32)
    return _f_rwkv_core(rw, k, zw, za, w0f, w0b, a0f, a0b, k_k, k_a, seg, seg_t)


def _f_rwkv_core(rw, k, zw, za, w0f, w0b, a0f, a0b, k_k, k_a, seg, seg_t):
    lw_f = -jnp.exp(-_softplus(-(w0f + zw[:, :rw])) - 0.5)
    lw_b = -jnp.exp(-_softplus(-(w0b + zw[:, rw:])) - 0.5)
    a_f = _sigmoid(a0f + za[:, :rw])
    a_b = _sigmoid(a0b + za[:, rw:])
    kk = k * k_k
    nrm = jnp.sqrt(_sel(_sel(kk * kk, seg, seg_t), seg_t, seg))
    kk = kk / jnp.maximum(nrm, 1e-12)
    k_f = k * (1.0 + (a_f - 1.0) * k_a)
    k_b = k * (1.0 + (a_b - 1.0) * k_a)
    return lw_f, lw_b, k_f, k_b, -kk, kk * a_f, kk * a_b


def _f_post(hn, y_f, y_b, r, k_f, k_b, v, z_r, o_mla, z_m, gn_g, gn_b, r_k, seg, seg_t):
    segsum = lambda t: _sel(_sel(t, seg, seg_t), seg_t, seg)
    y = y_f + y_b
    mu = segsum(y) * (1.0 / hn)
    yc = y - mu
    var = segsum(yc * yc) * (1.0 / hn)
    yn = yc * lax.rsqrt(var + GN_EPS) * gn_g + gn_b
    bonus = segsum(r * (k_f + k_b) * r_k) * v
    return o_mla * _silu(z_m), (yn + bonus) * _silu(z_r)


def _f_merge(u_m, u_r, g_m, g_r):
    return _sigmoid(g_m) * u_m + _sigmoid(g_r) * u_r


_NN = ((2,), (1,))
_NT = ((2,), (2,))
_TN = ((1,), (1,))

_SCAN_PASSES = {"cum": 2, "gram": 3, "solve": 1, "apply": 1, "state": 1}


def _hdot_raw(passes, x, y, dims):
    dn = (dims, ((0,), (0,)))
    d = lambda p, q: lax.dot_general(p, q, dn, preferred_element_type=F32)
    xh = x.astype(BF16)
    yh = y.astype(BF16)
    if passes == 1:
        return d(xh, yh)
    yl = (y - yh.astype(F32)).astype(BF16)
    if passes == 2:
        return d(xh, yh) + d(xh, yl)
    xl = (x - xh.astype(F32)).astype(BF16)
    return d(xh, yh) + d(xh, yl) + d(xl, yh)


@functools.partial(jax.custom_vjp, nondiff_argnums=(2, 3))
def _hdot_p(x, y, dims, passes):
    return _hdot_raw(passes, x, y, dims)


def _hdot_fwd(x, y, dims, passes):
    return _hdot_raw(passes, x, y, dims), (x, y)


def _hdot_bwd(dims, passes, res, ct):
    x, y = res
    if dims == _NN:
        return _hdot_raw(passes, ct, y, _NT), _hdot_raw(passes, x, ct, _TN)
    if dims == _NT:
        return _hdot_raw(passes, ct, y, _NN), _hdot_raw(passes, ct, x, _TN)
    return _hdot_raw(passes, y, ct, _NT), _hdot_raw(passes, x, ct, _NN)


_hdot_p.defvjp(_hdot_fwd, _hdot_bwd)


def _hdot(x, y, dims, kind):
    return _hdot_p(x, y, dims, _SCAN_PASSES[kind])


def _tri_solve(n_mat, x, length):
    row = lax.broadcasted_iota(jnp.int32, (length, length), 0)
    col = lax.broadcasted_iota(jnp.int32, (length, length), 1)
    eye = (row == col).astype(F32)[None]
    diag_blk = ((row // SUB) == (col // SUB))[None]
    nd = jnp.where(diag_blk, n_mat, 0.0)
    no = n_mat - nd
    dinv = eye + nd
    p = nd
    for _ in range(int(math.log2(SUB)) - 1):
        p = _hdot(p, p, _NN, "solve")
        dinv = dinv + _hdot(dinv, p, _NN, "solve")
    q = _hdot(dinv, no, _NN, "solve")
    u = _hdot(dinv, x, _NN, "solve")
    levels = int(math.log2(length // SUB))
    qs = [q]
    for _ in range(levels - 1):
        qs.append(_hdot(qs[-1], qs[-1], _NN, "solve"))
    for qk in reversed(qs):
        u = u + _hdot(qk, u, _NN, "solve")
    return u


def _rwkv_chunk(rev, s0, r, lw, k, v, a, b):
    pairs, length, width = r.shape
    hn = width // 2
    row = lax.broadcasted_iota(jnp.int32, (length, length), 0)
    col = lax.broadcasted_iota(jnp.int32, (length, length), 1)
    row2 = lax.broadcasted_iota(jnp.int32, (length, 2 * length), 0)
    col2 = lax.broadcasted_iota(jnp.int32, (length, 2 * length), 1)
    col2 = jnp.where(col2 >= length, col2 - length, col2)
    if rev is None:
        half = pairs // 2
        back = lax.broadcasted_iota(jnp.int32, (pairs, length, length), 0) >= half
        idx2 = lax.broadcasted_iota(jnp.int32, (2 * pairs, length, 2 * length), 0)
        back2 = ((idx2 >= half) & (idx2 < pairs)) | (idx2 >= pairs + half)
        ahead = jnp.where(back, (col - row)[None], (row - col)[None])
        ahead2 = jnp.where(back2, (col2 - row2)[None], (row2 - col2)[None])
        incl, strict2, incl2 = ahead >= 0, ahead2 > 0, ahead2 >= 0
    else:
        incl = ((row <= col) if rev else (row >= col))[None]
        strict2 = ((row2 < col2) if rev else (row2 > col2))[None]
        incl2 = ((row2 <= col2) if rev else (row2 >= col2))[None]
    lane = lax.broadcasted_iota(jnp.int32, (1, 1, width), 2)
    first = lane < hn
    head_mask = jnp.concatenate([jnp.broadcast_to(first.astype(F32), (pairs, 1, width)),
                                 jnp.broadcast_to(1.0 - first.astype(F32), (pairs, 1, width))], axis=0)
    twice = lambda t: jnp.concatenate([t, t], axis=0)
    pick = lambda t: jnp.where(first, t[:pairs], t[pairs:])

    t_incl = jnp.broadcast_to(incl.astype(F32), (pairs, length, length))
    cum = _hdot(t_incl, lw, _NN, "cum")
    g = jnp.exp(cum)
    g_inv = jnp.exp(-cum)
    at = a * jnp.exp(cum - lw)
    rt = r * g
    bt = b * g_inv
    kt = k * g_inv
    lhs = jnp.concatenate([twice(at) * head_mask, twice(rt) * head_mask], axis=1)
    rhs = jnp.concatenate([twice(bt), twice(kt)], axis=1)
    gram = _hdot(lhs, rhs, _NT, "gram")
    top = jnp.where(strict2, gram[:, :length], 0.0)
    bot = jnp.where(incl2, gram[:, length:], 0.0)
    v2 = twice(v)
    zeros = jnp.zeros_like(v2)
    x = _hdot(at, s0, _NT, "apply") + pick(_hdot(top, jnp.concatenate([zeros, v2], axis=1), _NN, "apply"))
    u = pick(_tri_solve(top[:, :, :length], twice(x), length))
    y = _hdot(rt, s0, _NT, "apply") + pick(_hdot(bot, jnp.concatenate([twice(u), v2], axis=1), _NN, "apply"))
    g_last = jnp.exp(jnp.sum(lw, axis=1, keepdims=True))
    ri = lax.broadcasted_iota(jnp.int32, (width, width), 0)
    ci = lax.broadcasted_iota(jnp.int32, (width, width), 1)
    same_head = ((ri < hn) == (ci < hn))[None]
    upd = _hdot(u, bt, _TN, "state") + _hdot(v, kt, _TN, "state")
    s1 = (s0 + jnp.where(same_head, upd, 0.0)) * g_last
    return y, s1


def _split_pairs(x):
    return jnp.stack([x[:, p * LANES:(p + 1) * LANES] for p in range(x.shape[1] // LANES)])


def _merge_pairs(x):
    return jnp.concatenate([x[p] for p in range(x.shape[0])], axis=1)


def _scan_specs(views, rw, nc, rev):
    cidx = (lambda c: nc - 1 - c) if rev else (lambda c: c)
    seqs = [pl.BlockSpec((CHUNK, rw), functools.partial(lambda c, cb: (cidx(c), cb), cb=cb)) for _, cb, _ in views]
    plain = pl.BlockSpec((CHUNK, rw), lambda c: (cidx(c), 0))
    st = pl.BlockSpec((1, rw // LANES, LANES, LANES), lambda c: (cidx(c), 0, 0, 0))
    return seqs, plain, st


def _as_views(arrs, rw):
    return [t if isinstance(t, tuple) else (t, 0, rw) for t in arrs]


def _rwkv_scan_fwd(ops_f, ops_b, rw, *, name):
    S = _as_views(ops_f, rw)[0][0].shape[0]
    nc, pairs = S // CHUNK, rw // LANES
    in_specs, out_specs, arrays = [], [], []
    for rev, ops in ((False, ops_f), (True, ops_b)):
        views = _as_views(ops, rw)
        seqs, plain, st = _scan_specs(views, rw, nc, rev)
        in_specs += seqs
        out_specs += [plain, st]
        arrays += [t[0] for t in views]

    def both(refs_f, refs_b):
        return [jnp.concatenate([_split_pairs(f[...]), _split_pairs(b[...])], axis=0) for f, b in zip(refs_f, refs_b)]

    def body(*refs):
        (y_f, st_f, y_b, st_b), s_ref = refs[12:16], refs[16]

        @pl.when(pl.program_id(0) == 0)
        def _():
            s_ref[...] = jnp.zeros_like(s_ref)

        s0 = s_ref[...]
        st_f[0] = s0[:pairs]
        st_b[0] = s0[pairs:]
        y, s1 = _rwkv_chunk(None, s0, *both(refs[:6], refs[6:12]))
        y_f[...] = _merge_pairs(y[:pairs])
        y_b[...] = _merge_pairs(y[pairs:])
        s_ref[...] = s1

    return pl.pallas_call(
        body, name=name, grid=(nc,), in_specs=in_specs, out_specs=out_specs,
        out_shape=[jax.ShapeDtypeStruct((S, rw), F32), jax.ShapeDtypeStruct((nc, pairs, LANES, LANES), F32)] * 2,
        scratch_shapes=[pltpu.VMEM((2 * pairs, LANES, LANES), F32)],
        compiler_params=_cparams(("arbitrary",)),
    )(*arrays)


def _rwkv_scan_bwd(ops_f, ops_b, states_f, states_b, dy, rw, *, name):
    S = dy.shape[0]
    nc, pairs = S // CHUNK, rw // LANES
    in_specs, arrays = [], []
    for rev, ops, states in ((False, ops_f, states_f), (True, ops_b, states_b)):
        views = _as_views(list(ops) + [dy], rw)
        seqs, plain, st = _scan_specs(views, rw, nc, not rev)
        in_specs += seqs + [st]
        arrays += [t[0] for t in views] + [states]
    out_specs = []
    for rev in (False, True):
        out_specs += [_scan_specs([], rw, nc, not rev)[1]] * 6

    def both(refs_f, refs_b):
        return [jnp.concatenate([_split_pairs(f[...]), _split_pairs(b[...])], axis=0) for f, b in zip(refs_f, refs_b)]

    def body(*refs):
        ds_ref = refs[28]

        @pl.when(pl.program_id(0) == 0)
        def _():
            ds_ref[...] = jnp.zeros_like(ds_ref)

        s0 = jnp.concatenate([refs[7][0], refs[15][0]], axis=0)
        _, vjp = jax.vjp(functools.partial(_rwkv_chunk, None), s0, *both(refs[:6], refs[8:14]))
        (dy,) = both(refs[6:7], refs[14:15])
        grads = vjp((dy, ds_ref[...]))
        ds_ref[...] = grads[0]
        for o_f, o_b, gval in zip(refs[16:22], refs[22:28], grads[1:]):
            o_f[...] = _merge_pairs(gval[:pairs])
            o_b[...] = _merge_pairs(gval[pairs:])

    return pl.pallas_call(
        body, name=name, grid=(nc,), in_specs=in_specs, out_specs=out_specs,
        out_shape=[jax.ShapeDtypeStruct((S, rw), F32)] * 12,
        scratch_shapes=[pltpu.VMEM((2 * pairs, LANES, LANES), F32)],
        compiler_params=_cparams(("arbitrary",)),
    )(*arrays)


def _shift_lerp(x_view, mu, d=None, into=None, *, name):
    arr, off, width = x_view
    S = arr.shape[0]
    cb = _pick(width, 256)
    assert off % cb == 0

    def cshift(t):
        rows = lax.broadcasted_iota(jnp.int32, t.shape, 0)
        prev = jnp.where(rows == 0, 0.0, pltpu.roll(t, 1, 0))
        nxt = jnp.where(rows == S - 1, 0.0, pltpu.roll(t, S - 1, 0))
        return 0.5 * (prev + nxt)

    def fwd_body(x_ref, mu_ref, o_ref):
        x = x_ref[...]
        o_ref[...] = x + mu_ref[...] * (cshift(x) - x)

    def bwd_body(x_ref, mu_ref, d_ref, _, dx_ref, dmu_ref):
        x, m, dd = x_ref[...], mu_ref[...], d_ref[...]
        gm = m * dd
        dx_ref[...] = (dd - gm + cshift(gm)).astype(dx_ref.dtype)
        dmu_ref[...] = jnp.sum(dd * (cshift(x) - x), axis=0, keepdims=True)

    x_spec = pl.BlockSpec((S, cb), lambda j: (0, off // cb + j))
    blk = pl.BlockSpec((S, cb), lambda j: (0, j))
    vec = pl.BlockSpec((1, cb), lambda j: (0, j))
    if d is None:
        return pl.pallas_call(
            fwd_body, name=name, grid=(width // cb,), in_specs=[x_spec, vec], out_specs=blk,
            out_shape=jax.ShapeDtypeStruct((S, width), F32), compiler_params=_cparams(("parallel",)),
        )(arr, mu)
    buf, first = into
    assert first % cb == 0
    return pl.pallas_call(
        bwd_body, name=name, grid=(width // cb,),
        in_specs=[x_spec, vec, blk, pl.BlockSpec(memory_space=pl.ANY)],
        out_specs=[pl.BlockSpec((S, cb), lambda j: (0, first // cb + j)), vec],
        out_shape=[jax.ShapeDtypeStruct(buf.shape, buf.dtype), jax.ShapeDtypeStruct((1, width), F32)],
        input_output_aliases={3: 0}, compiler_params=_cparams(("parallel",)),
    )(arr, mu, d, buf)


def _attention_fwd(qfull, kv, kr, hm, scale, *, tq, name):
    S = qfull.shape[0]

    def body(qn_ref, qr_ref, kn_ref, kr_ref, v_ref, o_ref, lse_ref):
        s = _attn_scores(qn_ref, qr_ref, kn_ref, kr_ref)
        m = jnp.max(s, axis=-1, keepdims=True)
        p = jnp.exp((s - m) * scale)
        l = jnp.sum(p, axis=-1, keepdims=True)
        o_ref[...] = jnp.dot(p.astype(BF16), v_ref[...], preferred_element_type=F32) * (1.0 / l)
        lse_ref[...] = jnp.broadcast_to(m * scale + jnp.log(l), lse_ref.shape)

    oblk = pl.BlockSpec((tq, VDIM), lambda h, i: (i, h))
    return pl.pallas_call(
        body, name=name, grid=(hm, S // tq),
        in_specs=[pl.BlockSpec((tq, NOPE), lambda h, i: (i, 2 * h)),
                  pl.BlockSpec((tq, NOPE), lambda h, i: (i, 2 * h + 1)),
                  pl.BlockSpec((S, NOPE), lambda h, i: (0, h)),
                  pl.BlockSpec((S, LANES), lambda h, i: (0, 0)),
                  pl.BlockSpec((S, VDIM), lambda h, i: (0, hm + h))],
        out_specs=[oblk, oblk],
        out_shape=[jax.ShapeDtypeStruct((S, hm * VDIM), F32)] * 2,
        compiler_params=_cparams(("parallel", "parallel")),
    )(qfull, qfull, kv, kr, kv)


def _attn_scores(qn_ref, qr_ref, kn_ref, kr_ref):
    nt = (((1,), (1,)), ((), ()))
    return (lax.dot_general(qn_ref[...], kn_ref[...], nt, preferred_element_type=F32)
            + lax.dot_general(qr_ref[...], kr_ref[...], nt, preferred_element_type=F32))


def _attention_bwd(qfull, kv, kr, o, lse, d_o, hm, scale, *, tq, name):
    S = qfull.shape[0]
    tq = min(tq, S)
    tn = (((0,), (0,)), ((), ()))
    nt = (((1,), (1,)), ((), ()))

    def body(qn_ref, qr_ref, kn_ref, kr_ref, v_ref, o_ref, lse_ref, do_ref,
             dqn_ref, dqr_ref, dkn_ref, dv_ref, dkr_ref):
        s = _attn_scores(qn_ref, qr_ref, kn_ref, kr_ref)
        p = jnp.exp(s * scale - lse_ref[:, 0:1])
        d_out = do_ref[...]
        delta = jnp.sum(d_out * o_ref[...], axis=-1, keepdims=True)
        d_out = d_out.astype(BF16)
        dp = lax.dot_general(d_out, v_ref[...], nt, preferred_element_type=F32)
        ds = (p * ((dp - delta) * scale)).astype(BF16)
        dqn_ref[...] = jnp.dot(ds, kn_ref[...], preferred_element_type=F32)
        dqr_ref[...] = jnp.dot(ds, kr_ref[...], preferred_element_type=F32)
        dv = lax.dot_general(p.astype(BF16), d_out, tn, preferred_element_type=F32)
        dkn = lax.dot_general(ds, qn_ref[...], tn, preferred_element_type=F32)
        dkr = lax.dot_general(ds, qr_ref[...], tn, preferred_element_type=F32)
        first = pl.program_id(1) == 0
        for ref, val in ((dkn_ref, dkn), (dv_ref, dv), (dkr_ref, dkr)):
            @pl.when(first)
            def _(ref=ref, val=val):
                ref[...] = val

            @pl.when(jnp.logical_not(first))
            def _(ref=ref, val=val):
                ref[...] += val

    qblk = pl.BlockSpec((tq, NOPE), lambda h, i: (i, h))
    kblk = pl.BlockSpec((S, NOPE), lambda h, i: (0, h))
    shp = jax.ShapeDtypeStruct((S, hm * NOPE), F32)
    return pl.pallas_call(
        body, name=name, grid=(hm, S // tq),
        in_specs=[pl.BlockSpec((tq, NOPE), lambda h, i: (i, 2 * h)),
                  pl.BlockSpec((tq, NOPE), lambda h, i: (i, 2 * h + 1)),
                  kblk,
                  pl.BlockSpec((S, LANES), lambda h, i: (0, 0)),
                  pl.BlockSpec((S, VDIM), lambda h, i: (0, hm + h)),
                  qblk, qblk, qblk],
        out_specs=[qblk, qblk, kblk, kblk, kblk],
        out_shape=[shp] * 5,
        compiler_params=_cparams(("parallel", "arbitrary")),
    )(qfull, qfull, kv, kr, kv, o, lse, d_o)


def _layout(D, MW, RW, TAIL, QR, KVR):
    names = ["gate_m", "gate_r", "z_m", "z_r", "q_a", "kv_a", "r", "k", "v", "tail"]
    widths = [D, D, MW, RW, QR, KVR, RW, RW, RW, TAIL]
    offs, o = {}, 0
    for nme, w in zip(names, widths):
        assert o % w == 0, (nme, o, w)
        offs[nme] = (o, w)
        o += w
    return offs, o


def _local_grads(x, target, W, dims, exchange=None):
    S, D = x.shape
    hm, hr, hn, rank = dims["hm"], dims["hr"], dims["hn"], dims["rank"]
    MW, RW = hm * VDIM, hr * hn
    TAIL = dims["TAIL"]
    QR, KVR = W["mla_q_norm"].shape[1], W["mla_kv_norm"].shape[1]
    lay, d_in = _layout(D, MW, RW, TAIL, QR, KVR)
    T = 256
    scale = (NOPE + ROPE) ** -0.5
    col = lambda arr, nme: _view(arr, *lay[nme])

    pos = jnp.arange(S, dtype=F32)
    inv_freq = jnp.power(ROPE_THETA, -jnp.arange(0, ROPE, 2, dtype=F32) / ROPE)
    ang = pos[:, None] * inv_freq[None, :]
    zpad = jnp.zeros((S, LANES - ROPE), F32)
    cosx = jnp.concatenate([jnp.cos(ang), jnp.cos(ang), zpad], axis=1)
    sinx = jnp.concatenate([jnp.sin(ang), jnp.sin(ang), zpad], axis=1)
    ri, ci = jnp.arange(LANES)[:, None], jnp.arange(LANES)[None, :]
    half = ROPE // 2
    rot = (jnp.where((ri == ci - half) & (ci >= half) & (ci < ROPE), 1.0, 0.0)
           - jnp.where((ri == ci + half) & (ci < half), 1.0, 0.0)).astype(BF16)
    rot_t = rot.T
    seg = (jnp.arange(RW)[:, None] // hn == jnp.arange(LANES)[None, :]).astype(BF16)
    seg_t = seg.T

    (h,) = _rowwise(lambda xb, g: (_rms(xb, g),), [x], [W["g_pre"]], [(D, BF16)], tile=T, name="pre_norm")
    if exchange is None:
        proj = _mm(h, W["w_in_t"], tb=True, name="in_proj")
    else:
        proj, *slabs = _mm(h, W["w_in_t"], tb=True, ride=_gather_plan(exchange[0]), name="in_proj")
        W = {**W, **_prepare_rest(dict(zip(_MATS[1:], slabs)), dims)}

    qn, kvn = _rowwise(_f_mla_norm, [col(proj, "q_a"), col(proj, "kv_a")], [W["mla_q_norm"], W["mla_kv_norm"]],
                       [(QR, BF16), (KVR, BF16)], tile=T, name="mla_norm")
    qraw = _mm(qn, W["wq_b_t"], tb=True, name="q_up")
    kv = _mm(kvn, W["wkv_b"], out_dtype=BF16, name="kv_up")
    kr_view = _view(proj, lay["tail"][0], LANES)
    qfull, kr = _rowwise(functools.partial(_f_rope, hm), [qraw, kr_view, cosx, sinx], [rot, rot_t],
                         [(hm * QHEAD, BF16), (LANES, BF16)], tile=T, name="rope")
    o_mla, lse = _attention_fwd(qfull, kv, kr, hm, scale, tq=T, name="attn_fwd")

    shift_view = (proj, lay["r"][0], 3 * RW + TAIL)
    rl = _shift_lerp(shift_view, W["mu"], name="shift_fwd")
    rl_r, rl_k, rl_v = _view(rl, 0, RW), _view(rl, RW, RW), _view(rl, 2 * RW, RW)
    rl_tail = _view(rl, 3 * RW, TAIL)
    pre_params = [W["w0_f"], W["w0_b"], W["a0_f"], W["a0_b"], W["k_k"], W["k_a"], W["w2cat"], W["a2cat"], seg, seg_t]
    pre_fn = functools.partial(_f_rwkv_pre, RW)
    lw_f, lw_b, k_f, k_b, a_n, b_f, b_b = _rowwise(pre_fn, [rl_k, rl_tail], pre_params, [(RW, F32)] * 7, tile=T,
                                                    name="rwkv_pre")
    ops_f = (rl_r, lw_f, k_f, rl_v, a_n, b_f)
    ops_b = (rl_r, lw_b, k_b, rl_v, a_n, b_b)
    y_f, st_f, y_b, st_b = _rwkv_scan_fwd(ops_f, ops_b, RW, name="scan_fwd")

    post_fn = functools.partial(_f_post, hn)
    post_rows = [y_f, y_b, rl_r, k_f, k_b, rl_v, col(proj, "z_r"), o_mla, col(proj, "z_m")]
    post_params = [W["gn_g"], W["gn_b"], W["r_k"], seg, seg_t]
    ymg, yrg = _rowwise(post_fn, post_rows, post_params, [(MW, BF16), (RW, BF16)], tile=T, name="post")
    u_m = _mm(ymg, W["w_br_mla"], name="br_mla")
    u_r = _mm(yrg, W["w_br_rwkv"], name="br_rwkv")
    merge_rows = [u_m, u_r, col(proj, "gate_m"), col(proj, "gate_r")]
    (merged,) = _rowwise(lambda *t: (_f_merge(*t),), merge_rows, [], [(D, BF16)], tile=T, name="merge")
    out = _mm(merged, W["w_out"], name="out_proj")

    def head(ob, xb, tb, g):
        yn, vjp = jax.vjp(_rms, ob, g)
        err = xb + yn - tb
        dy = err * (1.0 / D)
        d_ob, d_g = vjp(dy)
        loss = jnp.broadcast_to(0.5 * jnp.sum(err * err) * (1.0 / D), (1, LANES))
        return dy, d_ob, loss, d_g

    dy, d_out, loss, g_g_post = _rowwise(head, [out, x, target], [W["g_post"]], [(D, F32), (D, BF16)],
                                         [(1, LANES), (1, D)], tile=T, name="head")
    d_merged = _mm(d_out, W["w_out"], tb=True, name="d_merged")
    g_w_out = _mm(merged, d_out, ta=True, out_dtype=BF16, name="g_w_out")

    def merge_bwd(u_m_b, u_r_b, g_m_b, g_r_b, dm):
        _, vjp = jax.vjp(_f_merge, u_m_b, u_r_b, g_m_b, g_r_b)
        du_m, du_r, dg_m, dg_r = vjp(dm)
        return du_m, du_r, jnp.concatenate([dg_m, dg_r], axis=1)

    d_u_m, d_u_r, d_proj = _rowwise(merge_bwd, merge_rows + [d_merged], [],
                                    [(D, BF16), (D, BF16), (2 * D, BF16, (None, d_in, lay["gate_m"][0]))], tile=T,
                                    name="merge_bwd")
    d_ymg = _mm(d_u_m, W["w_br_mla"], tb=True, name="d_ymg")
    d_yrg = _mm(d_u_r, W["w_br_rwkv"], tb=True, name="d_yrg")
    g_w_br_mla = _mm(ymg, d_u_m, ta=True, out_dtype=BF16, name="g_w_br_mla")
    g_w_br_rwkv = _mm(yrg, d_u_r, ta=True, out_dtype=BF16, name="g_w_br_rwkv")

    def post_bwd(*args):
        nr = len(post_rows)
        prim, dm, dr = args[:nr] + args[nr + 2:], args[nr], args[nr + 1]
        _, vjp = jax.vjp(post_fn, *prim)
        g = vjp((dm, dr))
        return g[0], g[2], g[3], g[5], g[7], jnp.concatenate([g[8], g[6]], axis=1), g[9], g[10], g[11]

    (d_y, d_r_bonus, d_k_bonus, d_v_bonus, d_o, d_proj, g_gn_g, g_gn_b, g_r_k) = _rowwise(
        post_bwd, post_rows + [d_ymg, d_yrg], post_params,
        [(RW, F32), (RW, F32), (RW, F32), (RW, F32), (MW, F32), (MW + RW, BF16, (d_proj, d_in, lay["z_m"][0]))],
        [(1, RW)] * 3, tile=T // 2, name="post_bwd")

    dscan = _rwkv_scan_bwd(ops_f, ops_b, st_f, st_b, d_y, RW, name="scan_bwd")
    dsc = {"f": dscan[:6], "b": dscan[6:]}

    d_qn, d_qr, d_kn, d_v_att, d_kr_h = _attention_bwd(qfull, kv, kr, o_mla, lse, d_o, hm, scale, tq=2 * T, name="attn_bwd")

    def rope_bwd(qraw_b, kr_in, cos_b, sin_b, dqn_b, dqr_b, dkn_b, dv_b, dkrh_b, rot_b, rot_t_b):
        _, vjp = jax.vjp(lambda q_, k_: _f_rope(hm, q_, k_, cos_b, sin_b, rot_b, rot_t_b), qraw_b, kr_in)
        parts = []
        for hh in range(hm):
            parts += [dqn_b[:, hh * NOPE:(hh + 1) * NOPE], dqr_b[:, hh * NOPE:(hh + 1) * NOPE]]
        dkr = dkrh_b[:, :LANES]
        for hh in range(1, hm):
            dkr = dkr + dkrh_b[:, hh * LANES:(hh + 1) * LANES]
        d_qraw, d_kr_in = vjp((jnp.concatenate(parts, axis=1), dkr))
        return d_qraw, jnp.concatenate([dkn_b, dv_b], axis=1), d_kr_in

    d_qraw, d_kv, d_kr_in = _rowwise(rope_bwd, [qraw, kr_view, cosx, sinx, d_qn, d_qr, d_kn, d_v_att, d_kr_h],
                                     [rot, rot_t], [(hm * QHEAD, BF16), (2 * MW, BF16), (LANES, F32)], tile=T,
                                     name="rope_bwd")
    d_qnorm = _mm(d_qraw, W["wq_b_t"], name="d_qn")
    d_kvnorm = _mm(d_kv, W["wkv_b"], tb=True, name="d_kvn")
    g_wq_b = _mm(d_qraw, qn, ta=True, out_dtype=BF16, name="g_wq_b")
    g_wkv_b = _mm(kvn, d_kv, ta=True, out_dtype=BF16, name="g_wkv_b")

    def mla_norm_bwd(q_a, kv_a, qg, kvg, dq, dk):
        _, vjp = jax.vjp(_f_mla_norm, q_a, kv_a, qg, kvg)
        d_q_a, d_kv_a, d_qg, d_kvg = vjp((dq, dk))
        return jnp.concatenate([d_q_a, d_kv_a], axis=1), d_qg, d_kvg

    d_proj, g_q_norm, g_kv_norm = _rowwise(
        lambda q_a, kv_a, dq, dk, qg, kvg: mla_norm_bwd(q_a, kv_a, qg, kvg, dq, dk),
        [col(proj, "q_a"), col(proj, "kv_a"), d_qnorm, d_kvnorm], [W["mla_q_norm"], W["mla_kv_norm"]],
        [(QR + KVR, BF16, (d_proj, d_in, lay["q_a"][0]))], [(1, QR), (1, KVR)], tile=T, name="mla_norm_bwd")

    def pre_bwd(k_b_, tail_b, dlwf, dlwb, dkf, dkb, dkbon, daf, dab, dbf, dbb, drf, drb, drbon, dvf, dvb, dvbon,
                dkr, *params):
        w2, a2 = params[6], params[7]
        nt, tn = (((1,), (1,)), ((), ())), (((0,), (0,)), ((), ()))
        split = w2.shape[0]
        th = jnp.tanh(tail_b[:, :split])
        th_b, tail_h = th.astype(BF16), tail_b[:, split:].astype(BF16)
        zw = jnp.dot(th_b, w2, preferred_element_type=F32)
        za = jnp.dot(tail_h, a2, preferred_element_type=F32)
        _, vjp = jax.vjp(functools.partial(_f_rwkv_core, RW), k_b_, zw, za, *params[:6], params[8], params[9])
        g = vjp((dlwf, dlwb, dkf + dkbon, dkb + dkbon, daf + dab, dbf, dbb))
        d_zw, d_za = g[1].astype(BF16), g[2].astype(BF16)
        d_tail = (jnp.concatenate([lax.dot_general(d_zw, w2, nt, preferred_element_type=F32) * (1.0 - th * th),
                                   lax.dot_general(d_za, a2, nt, preferred_element_type=F32)], axis=1)
                  + jnp.concatenate([dkr, jnp.zeros((dkr.shape[0], TAIL - LANES), F32)], axis=1))
        g_w2 = lax.dot_general(th_b, d_zw, tn, preferred_element_type=F32)
        g_a2 = lax.dot_general(tail_h, d_za, tn, preferred_element_type=F32)
        d_rl = jnp.concatenate([drf + drb + drbon, g[0], dvf + dvb + dvbon, d_tail], axis=1)
        return (d_rl,) + tuple(g[3:9]) + (g_w2, g_a2)

    f_, b_ = dsc["f"], dsc["b"]
    pre_bwd_rows = [rl_k, rl_tail, f_[1], b_[1], f_[2], b_[2], d_k_bonus, f_[4], b_[4], f_[5], b_[5],
                    f_[0], b_[0], d_r_bonus, f_[3], b_[3], d_v_bonus, d_kr_in]
    (d_rl, g_w0_f, g_w0_b, g_a0_f, g_a0_b, g_k_k, g_k_a, g_w2cat, g_a2cat) = _rowwise(
        pre_bwd, pre_bwd_rows, pre_params, [(3 * RW + TAIL, F32)],
        [(1, RW)] * 6 + [W["w2cat"].shape, W["a2cat"].shape], tile=T // 2, name="rwkv_pre_bwd")
    d_proj, g_mu = _shift_lerp(shift_view, W["mu"], d_rl, (d_proj, lay["r"][0]), name="shift_bwd")
    small = dict(wq_b=g_wq_b, wkv_b=g_wkv_b, w2cat=g_w2cat, a2cat=g_a2cat, w_br_mla=g_w_br_mla,
                 w_br_rwkv=g_w_br_rwkv, w_out=g_w_out)
    if exchange is None:
        received = None
        g_w_in = _mm(d_proj, h, ta=True, out_dtype=BF16, tn_cap=1024, name="g_w_in")
        d_h = _mm(d_proj, W["w_in_t"], tn_cap=1024, name="d_h")
    else:
        slabs = _restore_rest(small, dims)
        slabs = [slabs[n] for n in _MATS[1:]]
        g_w_in, *got = _mm(d_proj, h, ta=True, out_dtype=BF16, tn_cap=1024, ride=_sibling_swap_plan(slabs),
                           name="g_w_in")
        sums = [_pair_add(exchange[1], s, t, name="pair_add_" + n) for n, s, t in zip(_MATS[1:], slabs, got)]
        d_h, *received = _mm(d_proj, W["w_in_t"], tn_cap=1024, ride=_chip_exchange_plan(sums), name="d_h")
        small = {}

    def pre_norm_bwd(xb, dyb, dhb, g):
        _, vjp = jax.vjp(_rms, xb, g)
        dx, dg = vjp(dhb)
        return dyb + dx, dg

    grad_x, g_g_pre = _rowwise(pre_norm_bwd, [x, dy, d_h], [W["g_pre"]], [(D, F32)], [(1, D)], tile=T,
                               name="pre_norm_bwd")

    grads = dict(g_pre=g_g_pre, w_in=g_w_in, mla_q_norm=g_q_norm, mla_kv_norm=g_kv_norm, mu=g_mu, w0_f=g_w0_f,
                 w0_b=g_w0_b, a0_f=g_a0_f, a0_b=g_a0_b, k_k=g_k_k, k_a=g_k_a, r_k=g_r_k, gn_g=g_gn_g, gn_b=g_gn_b,
                 g_post=g_g_post, **small)
    return loss[0, 0], grad_x, grads, received


_MATS = ["w_in", "mla_wq_b", "mla_wkv_b", "rwkv_w2_f", "rwkv_w2_b", "rwkv_a2_f", "rwkv_a2_b", "w_br_mla",
         "w_br_rwkv", "w_out"]
_ROW_SHARDED = ("w_out",)
_TRANSPOSED = ("w_in", "mla_wq_b")
_VECS = ["g_pre", "mla_q_norm", "mla_kv_norm", "rwkv_mu", "rwkv_w0_f", "rwkv_w0_b", "rwkv_a0_f", "rwkv_a0_b",
         "rwkv_k_k", "rwkv_k_a", "rwkv_r_k", "rwkv_gn_g", "rwkv_gn_b", "g_post"]
_WEIGHTS = ["g_pre", "w_in", "mla_q_norm", "mla_wq_b", "mla_kv_norm", "mla_wkv_b", "rwkv_mu", "rwkv_w0_f",
            "rwkv_w2_f", "rwkv_w0_b", "rwkv_w2_b", "rwkv_a0_f", "rwkv_a2_f", "rwkv_a0_b", "rwkv_a2_b", "rwkv_k_k",
            "rwkv_k_a", "rwkv_r_k", "rwkv_gn_g", "rwkv_gn_b", "w_br_mla", "w_br_rwkv", "w_out", "g_post"]

def _exchange(srcs, *, name):
    n = len(srcs)

    def body(*refs):
        src_refs, out_refs = refs[:n], refs[n:2 * n]
        send_sems, recv_sems, local_sems = refs[2 * n:]
        x, y, c = lax.axis_index("x"), lax.axis_index("y"), lax.axis_index("c")
        me = 4 * x + 2 * y + c
        flip = lambda v, bit: (1 - v) if bit else v

        def piece(a, idx):
            return src_refs[a] if srcs[a].ndim == 2 else src_refs[a].at[idx]

        owns = [pltpu.make_async_copy(piece(a, me), out_refs[a].at[me], local_sems.at[a]) for a in range(n)]
        for cp in owns:
            cp.start()
        sends, peers = [], []
        for d in range(1, N_DEV):
            px, py, pc = flip(x, d & 4), flip(y, d & 2), flip(c, d & 1)
            pidx = 4 * px + 2 * py + pc
            peers.append(((px, py, pc), pidx))
            for a in range(n):
                cp = pltpu.make_async_remote_copy(
                    src_ref=piece(a, pidx), dst_ref=out_refs[a].at[me], send_sem=send_sems.at[d - 1, a],
                    recv_sem=recv_sems.at[d - 1, a], device_id=(px, py, pc), device_id_type=pl.DeviceIdType.MESH)
                cp.start()
                sends.append(cp)
        for d, (peer, pidx) in zip(range(1, N_DEV), peers):
            for a in range(n):
                pltpu.make_async_remote_copy(
                    src_ref=piece(a, pidx), dst_ref=out_refs[a].at[pidx], send_sem=send_sems.at[d - 1, a],
                    recv_sem=recv_sems.at[d - 1, a], device_id=peer, device_id_type=pl.DeviceIdType.MESH).wait_recv()
        for cp in sends:
            cp.wait_send()
        for cp in owns:
            cp.wait()

    return pl.pallas_call(
        body, name=name,
        out_shape=[jax.ShapeDtypeStruct((N_DEV,) + s.shape[-2:], s.dtype) for s in srcs],
        in_specs=[pl.BlockSpec(memory_space=pl.ANY)] * n, out_specs=[pl.BlockSpec(memory_space=pl.ANY)] * n,
        scratch_shapes=[pltpu.SemaphoreType.DMA((N_DEV - 1, n)), pltpu.SemaphoreType.DMA((N_DEV - 1, n)),
                        pltpu.SemaphoreType.DMA((n,))],
    )(*srcs)


def _remote(src, dst, sems, key, to):
    send_sems, recv_sems = sems
    return pltpu.make_async_remote_copy(src_ref=src, dst_ref=dst, send_sem=send_sems.at[key], recv_sem=recv_sems.at[key],
                                        device_id=to, device_id_type=pl.DeviceIdType.MESH)


def _run_exchange(plan, *, name):
    srcs, out_shapes, sem_shapes, phases = plan
    n, m = len(srcs), len(out_shapes)

    def body(*refs):
        for phase in phases(refs[:n], refs[n:n + m], refs[n + m:]):
            phase()

    return pl.pallas_call(
        body, name=name, out_shape=out_shapes,
        in_specs=[pl.BlockSpec(memory_space=pl.ANY)] * n, out_specs=[pl.BlockSpec(memory_space=pl.ANY)] * m,
        scratch_shapes=[pltpu.SemaphoreType.DMA(s) for s in sem_shapes],
    )(*srcs)


def _gather_plan(srcs):
    n = len(srcs)

    def phases(src_refs, out_refs, sem_refs):
        sems, local_sems = sem_refs[:2], sem_refs[2]
        x, y, c = lax.axis_index("x"), lax.axis_index("y"), lax.axis_index("c")
        idx = lambda px, py, pc: 4 * px + 2 * py + pc
        me, sibling = (x, y, c), (x, y, 1 - c)
        chips = [(1 - x, y), (x, 1 - y), (1 - x, 1 - y)]
        own = lambda a: pltpu.make_async_copy(src_refs[a], out_refs[a].at[idx(*me)], local_sems.at[a])
        to_sibling = lambda a: _remote(src_refs[a], out_refs[a].at[idx(*me)], sems, (0, a), sibling)
        to_chip = lambda a, j: _remote(src_refs[a], out_refs[a].at[idx(*me)], sems, (1 + j, a), (*chips[j], c))
        landed = lambda a, j: out_refs[a].at[idx(*chips[j], c)]
        passed_on = lambda a, j: _remote(landed(a, j), landed(a, j), sems, (4 + j, a), sibling)

        def first():
            for a in range(n):
                own(a).start()
                to_sibling(a).start()
                for j in range(3):
                    to_chip(a, j).start()

        def middle():
            for j in range(3):
                for a in range(n):
                    _remote(landed(a, j), landed(a, j), sems, (1 + j, a), me).wait_recv()
                    passed_on(a, j).start()

        def last():
            for a in range(n):
                blk = out_refs[a].at[idx(*sibling)]
                _remote(blk, blk, sems, (0, a), me).wait_recv()
                for j in range(3):
                    blk = out_refs[a].at[idx(*chips[j], 1 - c)]
                    _remote(blk, blk, sems, (4 + j, a), me).wait_recv()
            for a in range(n):
                to_sibling(a).wait_send()
                for j in range(3):
                    to_chip(a, j).wait_send()
                    passed_on(a, j).wait_send()
                own(a).wait()

        return first, middle, last

    return srcs, [jax.ShapeDtypeStruct((N_DEV,) + s.shape, s.dtype) for s in srcs], [(7, n), (7, n), (n,)], phases


def _sibling_swap_plan(srcs):
    n = len(srcs)

    def phases(src_refs, out_refs, sems):
        x, y, c = lax.axis_index("x"), lax.axis_index("y"), lax.axis_index("c")
        copies = lambda: [_remote(src_refs[a].at[2 * q + 1 - c], out_refs[a].at[q], sems, (q, a), (x, y, 1 - c))
                          for a in range(n) for q in range(4)]

        def first():
            for cp in copies():
                cp.start()

        def last():
            for cp in copies():
                cp.wait()

        return first, (lambda: None), last

    return srcs, [jax.ShapeDtypeStruct((4,) + s.shape[1:], s.dtype) for s in srcs], [(4, n), (4, n)], phases


def _chip_exchange_plan(srcs):
    n = len(srcs)

    def phases(src_refs, out_refs, sem_refs):
        sems, local_sems = sem_refs[:2], sem_refs[2]
        x, y, c = lax.axis_index("x"), lax.axis_index("y"), lax.axis_index("c")
        mine = 2 * x + y
        chips = [(1 - x, y), (x, 1 - y), (1 - x, 1 - y)]
        own = lambda a: pltpu.make_async_copy(src_refs[a].at[mine], out_refs[a].at[mine], local_sems.at[a])
        send = lambda a, j: _remote(src_refs[a].at[2 * chips[j][0] + chips[j][1]], out_refs[a].at[mine], sems, (j, a),
                                    (*chips[j], c))

        def first():
            for a in range(n):
                own(a).start()
                for j in range(3):
                    send(a, j).start()

        def last():
            for j in range(3):
                for a in range(n):
                    blk = out_refs[a].at[2 * chips[j][0] + chips[j][1]]
                    _remote(blk, blk, sems, (j, a), (x, y, c)).wait_recv()
            for a in range(n):
                for j in range(3):
                    send(a, j).wait_send()
                own(a).wait()

        return first, (lambda: None), last

    return srcs, [jax.ShapeDtypeStruct(s.shape, s.dtype) for s in srcs], [(3, n), (3, n), (n,)], phases


def _pair_add(core, g, got, *, name):
    q, r, c = got.shape
    tr, tc = _tile2d(r, c)

    def body(core_ref, a_ref, b_ref, o_ref):
        o_ref[...] = (a_ref[...].astype(F32) + b_ref[...].astype(F32)).astype(BF16)

    blk = pl.BlockSpec((1, tr, tc), lambda i, j, k, core_ref: (i, j, k))
    mine = pl.BlockSpec((1, tr, tc), lambda i, j, k, core_ref: (2 * i + core_ref[0], j, k))
    return pl.pallas_call(
        body, name=name, out_shape=jax.ShapeDtypeStruct(got.shape, BF16),
        grid_spec=pltpu.PrefetchScalarGridSpec(num_scalar_prefetch=1, grid=(q, r // tr, c // tc),
                                               in_specs=[mine, blk], out_specs=blk),
        compiler_params=_cparams(("parallel", "parallel", "parallel")))(core, g, got)


def _adamw(recv, w, m, v, *, name):
    r, c = w.shape
    n_terms = recv.shape[0]
    tr, tc = _tile2d(r, c)

    def body(g_ref, w_ref, m_ref, v_ref, go_ref, d_ref, mo_ref, vo_ref):
        g = g_ref[0].astype(F32)
        for k in range(1, n_terms):
            g = g + g_ref[k].astype(F32)
        m_new = ADAM_B1 * m_ref[...] + (1.0 - ADAM_B1) * g
        v_new = ADAM_B2 * v_ref[...] + (1.0 - ADAM_B2) * (g * g)
        m_hat = m_new / (1.0 - ADAM_B1 ** ADAM_STEP)
        v_hat = v_new / (1.0 - ADAM_B2 ** ADAM_STEP)
        go_ref[...] = g
        d_ref[...] = -ADAM_LR * (m_hat / (jnp.sqrt(v_hat) + ADAM_EPS) + ADAM_WD * w_ref[...])
        mo_ref[...] = m_new
        vo_ref[...] = v_new

    blk = pl.BlockSpec((tr, tc), lambda i, j: (i, j))
    return pl.pallas_call(
        body, name=name, grid=(r // tr, c // tc),
        in_specs=[pl.BlockSpec((n_terms, tr, tc), lambda i, j: (0, i, j)), blk, blk, blk], out_specs=[blk] * 4,
        out_shape=[jax.ShapeDtypeStruct((r, c), F32)] * 4, compiler_params=_cparams(("parallel", "parallel")),
    )(recv, w, m, v)


def _tile2d(r, c, cap=256):
    if r <= cap:
        return r, c
    for t in range(cap, 0, -BF16_ROWS):
        if r % t == 0:
            return t, c
    return r, _pick(c, cap)


def _pack(pieces, dtype, quantum):
    out = []
    for p in pieces:
        lead, n = p.shape[:-1], p.shape[-1]
        pad = (-n) % quantum
        p = p.astype(dtype)
        if pad:
            p = jnp.concatenate([p, jnp.zeros(lead + (pad,), dtype)], axis=-1)
        out.append(p)
    flat = jnp.concatenate(out, axis=-1)
    return flat.reshape(flat.shape[:-1] + (flat.shape[-1] // LANES, LANES))


def _unpack(flat, sizes, quantum):
    flat = flat.reshape(flat.shape[:-2] + (-1,))
    out, o = [], 0
    for n in sizes:
        out.append(flat[..., o:o + n])
        o += n + (-n) % quantum
    return out


def _prepare_weights(full, vec, dims):
    rest = {n: t for n, t in full.items() if n != "w_in"}
    return {"w_in_t": _prepare_w_in(full["w_in"], dims), **_prepare_rest(rest, dims), **_prepare_vectors(vec, dims)}


def _prepare_w_in(slabs, dims):
    D = dims["D"]
    c = slabs.shape[1]
    parts, pos = [], 0
    for orig_off, width, perm_off in sorted(dims["segs"], key=lambda t: t[2]):
        if perm_off > pos:
            parts.append(jnp.zeros((perm_off - pos, D), BF16))
        for k in range(N_DEV):
            lo, hi = max(orig_off, k * c), min(orig_off + width, (k + 1) * c)
            if lo < hi:
                parts.append(slabs[k][lo - k * c:hi - k * c])
        pos = perm_off + width
    if dims["d_in_perm"] > pos:
        parts.append(jnp.zeros((dims["d_in_perm"] - pos, D), BF16))
    return jnp.concatenate(parts, axis=0)


def _prepare_rest(full, dims):
    hm, hr, hn, rank = dims["hm"], dims["hr"], dims["hn"], dims["rank"]
    QR, KVR = dims["QR"], dims["KVR"]
    RW, TAIL = hr * hn, dims["TAIL"]
    full = {n: (t.reshape(-1, t.shape[2]) if n in _ROW_SHARDED + _TRANSPOSED
                else t.transpose(1, 0, 2).reshape(t.shape[1], -1)) for n, t in full.items()}
    wq = full["mla_wq_b"].reshape(hm, NOPE + ROPE, QR)
    wq = jnp.concatenate([wq, jnp.zeros((hm, QHEAD - NOPE - ROPE, QR), BF16)], axis=1).reshape(hm * QHEAD, QR)
    wkv = full["mla_wkv_b"].reshape(KVR, hm, 2, NOPE).transpose(0, 2, 1, 3).reshape(KVR, 2 * hm * NOPE)
    z = lambda rows: jnp.zeros((rows, RW), BF16)
    f = lambda nme: full[nme]
    split = ROPE + 2 * rank
    assert split % LANES == 0, split
    w2cat = jnp.concatenate([
        jnp.concatenate([z(ROPE), f("rwkv_w2_f"), z(rank)], axis=0),
        jnp.concatenate([z(ROPE + rank), f("rwkv_w2_b")], axis=0)], axis=1)
    a2cat = jnp.concatenate([
        jnp.concatenate([f("rwkv_a2_f"), z(TAIL - split - rank)], axis=0),
        jnp.concatenate([z(rank), f("rwkv_a2_b"), z(TAIL - split - 2 * rank)], axis=0)], axis=1)
    return dict(wq_b_t=wq, wkv_b=wkv, w2cat=w2cat, a2cat=a2cat, w_br_mla=full["w_br_mla"],
                w_br_rwkv=full["w_br_rwkv"], w_out=full["w_out"])


def _prepare_vectors(vec, dims):
    rank, RW, TAIL = dims["rank"], dims["hr"] * dims["hn"], dims["TAIL"]
    mu = vec["rwkv_mu"]
    mu_p = jnp.concatenate([mu[:3 * RW], jnp.zeros((ROPE,), F32), mu[3 * RW:],
                            jnp.zeros((TAIL - ROPE - 4 * rank,), F32)])
    row = lambda t: t.reshape(1, -1)
    return dict(
        mu=row(mu_p), g_pre=row(vec["g_pre"]), g_post=row(vec["g_post"]), mla_q_norm=row(vec["mla_q_norm"]),
        mla_kv_norm=row(vec["mla_kv_norm"]), w0_f=row(vec["rwkv_w0_f"]), w0_b=row(vec["rwkv_w0_b"]),
        a0_f=row(vec["rwkv_a0_f"]), a0_b=row(vec["rwkv_a0_b"]), k_k=row(vec["rwkv_k_k"]), k_a=row(vec["rwkv_k_a"]),
        r_k=row(vec["rwkv_r_k"]), gn_g=row(vec["rwkv_gn_g"]), gn_b=row(vec["rwkv_gn_b"]))


def _restore_grads(g, dims):
    return {"w_in": _restore_w_in(g["w_in"], dims), **_restore_rest(g, dims), **_restore_vectors(g, dims)}


def _restore_w_in(gw, dims):
    c = dims["d_in"] // N_DEV
    slabs = []
    for k in range(N_DEV):
        parts = []
        for orig_off, width, perm_off in sorted(dims["segs"]):
            lo_, hi_ = max(orig_off, k * c), min(orig_off + width, (k + 1) * c)
            if lo_ < hi_:
                parts.append(gw[perm_off + lo_ - orig_off:perm_off + hi_ - orig_off])
        slabs.append(jnp.concatenate(parts, axis=0))
    return jnp.stack(slabs)


def _restore_rest(g, dims):
    hm, hr, hn, rank = dims["hm"], dims["hr"], dims["hn"], dims["rank"]
    QR, KVR, RW = dims["QR"], dims["KVR"], hr * hn
    wq = g["wq_b"].reshape(hm, QHEAD, QR)[:, :NOPE + ROPE].reshape(N_DEV, -1, QR)
    wkv = g["wkv_b"].reshape(KVR, 2, hm, NOPE).transpose(0, 2, 1, 3).reshape(KVR, 2 * hm * NOPE)
    lo = lambda t, first, half: t[first:first + rank, half * RW:(half + 1) * RW].astype(BF16)
    cols = lambda t: t.reshape(t.shape[0], N_DEV, -1).transpose(1, 0, 2)
    return dict(
        mla_wq_b=wq, mla_wkv_b=cols(wkv), rwkv_w2_f=cols(lo(g["w2cat"], ROPE, 0)),
        rwkv_w2_b=cols(lo(g["w2cat"], ROPE + rank, 1)), rwkv_a2_f=cols(lo(g["a2cat"], 0, 0)),
        rwkv_a2_b=cols(lo(g["a2cat"], rank, 1)), w_br_mla=cols(g["w_br_mla"]), w_br_rwkv=cols(g["w_br_rwkv"]),
        w_out=g["w_out"].reshape(N_DEV, -1, g["w_out"].shape[1]))


def _restore_vectors(g, dims):
    rank, RW = dims["rank"], dims["hr"] * dims["hn"]
    mu = g["mu"][0]
    out = dict(
        rwkv_mu=jnp.concatenate([mu[:3 * RW], mu[3 * RW + ROPE:3 * RW + ROPE + 4 * rank]]),
        g_pre=g["g_pre"][0], g_post=g["g_post"][0], mla_q_norm=g["mla_q_norm"][0], mla_kv_norm=g["mla_kv_norm"][0],
        rwkv_w0_f=g["w0_f"][0], rwkv_w0_b=g["w0_b"][0], rwkv_a0_f=g["a0_f"][0], rwkv_a0_b=g["a0_b"][0],
        rwkv_k_k=g["k_k"][0], rwkv_k_a=g["k_a"][0], rwkv_r_k=g["r_k"][0], rwkv_gn_g=g["gn_g"][0],
        rwkv_gn_b=g["gn_b"][0])
    return out


def _dims(inp):
    D = inp["x"].shape[-1]
    QR, KVR = inp["mla_q_norm"].shape[0], inp["mla_kv_norm"].shape[0]
    hm = inp["mla_wq_b"].shape[1] * N_DEV // (NOPE + ROPE)
    hr, hn = inp["rwkv_r_k"].shape
    rank = inp["rwkv_w2_f"].shape[0]
    MW, RW = hm * VDIM, hr * hn
    TAIL = -(-(ROPE + 4 * rank) // LANES) * LANES
    orig, o = {}, 0
    for nme, w in (("q_a", QR), ("kv_a", KVR), ("k_rope", ROPE), ("rkv", 3 * RW), ("lora", 4 * rank), ("z_m", MW),
                   ("z_r", RW), ("gate_m", D), ("gate_r", D)):
        orig[nme] = (o, w)
        o += w
    assert o == inp["w_in"].shape[1] * N_DEV
    lay, d_in_perm = _layout(D, MW, RW, TAIL, QR, KVR)
    perm_off = dict(q_a=lay["q_a"][0], kv_a=lay["kv_a"][0], k_rope=lay["tail"][0], rkv=lay["r"][0],
                    lora=lay["tail"][0] + ROPE, z_m=lay["z_m"][0], z_r=lay["z_r"][0], gate_m=lay["gate_m"][0],
                    gate_r=lay["gate_r"][0])
    segs = [(orig[nme][0], orig[nme][1], perm_off[nme]) for nme in orig]
    return dict(D=D, QR=QR, KVR=KVR, hm=hm, hr=hr, hn=hn, rank=rank, TAIL=TAIL, segs=segs, d_in=o,
                d_in_perm=d_in_perm)


def kernel(x, g_pre, w_in, mla_q_norm, mla_wq_b, mla_kv_norm, mla_wkv_b, rwkv_mu, rwkv_w0_f, rwkv_w2_f, rwkv_w0_b, rwkv_w2_b, rwkv_a0_f, rwkv_a2_f, rwkv_a0_b, rwkv_a2_b, rwkv_k_k, rwkv_k_a, rwkv_r_k, rwkv_gn_g, rwkv_gn_b, w_br_mla, w_br_rwkv, w_out, g_post, loss_target, m_g_pre, m_w_in, m_mla_q_norm, m_mla_wq_b, m_mla_kv_norm, m_mla_wkv_b, m_rwkv_mu, m_rwkv_w0_f, m_rwkv_w2_f, m_rwkv_w0_b, m_rwkv_w2_b, m_rwkv_a0_f, m_rwkv_a2_f, m_rwkv_a0_b, m_rwkv_a2_b, m_rwkv_k_k, m_rwkv_k_a, m_rwkv_r_k, m_rwkv_gn_g, m_rwkv_gn_b, m_w_br_mla, m_w_br_rwkv, m_w_out, m_g_post, v_g_pre, v_w_in, v_mla_q_norm, v_mla_wq_b, v_mla_kv_norm, v_mla_wkv_b, v_rwkv_mu, v_rwkv_w0_f, v_rwkv_w2_f, v_rwkv_w0_b, v_rwkv_w2_b, v_rwkv_a0_f, v_rwkv_a2_f, v_rwkv_a0_b, v_rwkv_a2_b, v_rwkv_k_k, v_rwkv_k_a, v_rwkv_r_k, v_rwkv_gn_g, v_rwkv_gn_b, v_w_br_mla, v_w_br_rwkv, v_w_out, v_g_post):
    inp = dict(locals())
    dims = _dims(inp)
    stored = lambda t, n: t.T if n in _TRANSPOSED else t
    assert _MATS[0] == "w_in"
    shards = [stored(inp[n], n).astype(BF16) for n in _MATS]
    core = lax.axis_index("c").astype(jnp.int32).reshape(1)
    (w_in_slabs,) = _run_exchange(_gather_plan(shards[:1]), name="gather_w_in")
    W = {"w_in_t": _prepare_w_in(w_in_slabs, dims), **_prepare_vectors({n: inp[n] for n in _VECS}, dims)}
    loss, grad_x, g, recv_rest = _local_grads(x[0], loss_target[0], W, dims, exchange=(shards[1:], core))
    loss = lax.psum(loss, ("x", "y", "c"))

    new = {}
    g_w_in = _restore_w_in(g["w_in"], dims)
    (got,) = _run_exchange(_sibling_swap_plan([g_w_in]), name="pair_swap_w_in")
    (recv_w_in,) = _run_exchange(_chip_exchange_plan([_pair_add(core, g_w_in, got, name="pair_add_w_in")]),
                                 name="scatter_w_in")
    g = _restore_vectors(g, dims)
    for n, t in zip(_MATS, [recv_w_in] + recv_rest):
        out = _adamw(t, stored(inp[n], n), stored(inp["m_" + n], n), stored(inp["v_" + n], n), name="adamw_" + n)
        new[n] = [stored(o, n) for o in out]

    vsizes = [inp[n].size for n in _VECS]
    vflat = lambda prefix, src: _pack([src[prefix + n].reshape(-1) for n in _VECS], F32, LANES * 8)
    (vrecv,) = _exchange([vflat("", g)], name="gather_vector_grads")
    vout = _adamw(vrecv, vflat("", inp), vflat("m_", inp), vflat("v_", inp), name="adamw_vectors")
    vparts = [_unpack(t, vsizes, LANES * 8) for t in vout]
    for i, n in enumerate(_VECS):
        new[n] = [vp[i].reshape(inp[n].shape) for vp in vparts]

    outs = [loss, grad_x[None]]
    for k in range(4):
        outs += [new[n][k] for n in _WEIGHTS]
    return tuple(outs)
```

```python
import functools
import math

import jax
import jax.numpy as jnp
from jax import lax
from jax.experimental import pallas as pl
from jax.experimental.pallas import tpu as pltpu

F32 = jnp.float32
BF16 = jnp.bfloat16

N_DEV = 8
LANES = 128
BF16_ROWS = 16
NOPE, ROPE, VDIM = 128, 64, 128
QHEAD = 256
ROPE_THETA = 10000.0
NORM_EPS = 1e-6
GN_EPS = 64e-5
CHUNK = 64
SUB = 16
VMEM_LIMIT = 56 * 1024 * 1024

ADAM_LR, ADAM_B1, ADAM_B2, ADAM_EPS, ADAM_WD, ADAM_STEP = 0.001, 0.9, 0.999, 1e-08, 0.01, 10


def _cparams(sem):
    return pltpu.CompilerParams(dimension_semantics=sem, vmem_limit_bytes=VMEM_LIMIT)


def _pick(n, cap):
    if n <= cap:
        return n
    for t in range(cap - cap % LANES, 0, -LANES):
        if n % t == 0:
            return t
    raise ValueError(f"no tile for {n} under {cap}")


def _mm(a, b, *, ta=False, tb=False, out_dtype=F32, name, tm_cap=1024, tn_cap=512, tk_cap=2048, ride=None):
    K, M = a.shape if ta else a.shape[::-1]
    N = b.shape[0] if tb else b.shape[1]
    assert (b.shape[1] if tb else b.shape[0]) == K, (a.shape, b.shape, ta, tb)
    tm, tn, tk = _pick(M, tm_cap), _pick(N, tn_cap), _pick(K, tk_cap)
    nj, nk = N // tn, K // tk
    steps = (M // tm) * nj * nk
    dn = (((0 if ta else 1,), (1 if tb else 0,)), ((), ()))
    srcs, extra_shapes, sem_shapes, phases = ride if ride else ((), (), (), None)
    n_src, n_extra = len(srcs), len(extra_shapes)

    def body(*refs):
        a_ref, b_ref, o_ref = refs[0], refs[1], refs[2 + n_src]
        acc_ref = refs[3 + n_src + n_extra]
        k = pl.program_id(2)
        if ride:
            step = (pl.program_id(0) * nj + pl.program_id(1)) * nk + k
            first, middle, last = phases(refs[2:2 + n_src], refs[3 + n_src:3 + n_src + n_extra],
                                         refs[4 + n_src + n_extra:])
            pl.when(step == 0)(first)
            pl.when(step == (steps * 7) // 8)(middle)
        p = lax.dot_general(a_ref[...], b_ref[...], dn, preferred_element_type=F32)

        @pl.when(k == 0)
        def _():
            acc_ref[...] = p

        @pl.when(k > 0)
        def _():
            acc_ref[...] += p

        @pl.when(k == nk - 1)
        def _():
            o_ref[...] = acc_ref[...].astype(out_dtype)

        if ride:
            pl.when(step == steps - 1)(last)

    a_spec = pl.BlockSpec((tk, tm), lambda i, j, k: (k, i)) if ta else pl.BlockSpec((tm, tk), lambda i, j, k: (i, k))
    b_spec = pl.BlockSpec((tn, tk), lambda i, j, k: (j, k)) if tb else pl.BlockSpec((tk, tn), lambda i, j, k: (k, j))
    hbm = pl.BlockSpec(memory_space=pl.ANY)
    out = pl.pallas_call(
        body, name=name, grid=(M // tm, nj, nk),
        in_specs=[a_spec, b_spec] + [hbm] * n_src,
        out_specs=[pl.BlockSpec((tm, tn), lambda i, j, k: (i, j))] + [hbm] * n_extra,
        out_shape=[jax.ShapeDtypeStruct((M, N), out_dtype)] + list(extra_shapes),
        scratch_shapes=[pltpu.VMEM((tm, tn), F32)] + [pltpu.SemaphoreType.DMA(s) for s in sem_shapes],
        compiler_params=_cparams(("arbitrary",) * 3 if ride else ("parallel", "parallel", "arbitrary")),
    )(a, b, *srcs)
    return out if ride else out[0]


def _view(arr, off, width):
    assert off % width == 0, (off, width)
    return (arr, off // width, width)


def _rowwise(fn, rows, params, out_rows, out_accs=(), *, tile, name):
    rows = [r if isinstance(r, tuple) else (r, 0, r.shape[1]) for r in rows]
    S = rows[0][0].shape[0]
    T = min(tile, S)
    assert S % T == 0
    n_rows, n_par, n_out = len(rows), len(params), len(out_rows)
    into = [o[2] if len(o) == 3 else None for o in out_rows]
    carried = [t[0] for t in into if t is not None and t[0] is not None]

    def body(*refs):
        ins = [r[...] for r in refs[:n_rows + n_par]]
        outs = fn(*ins)
        out_refs = refs[n_rows + n_par + len(carried):]
        for o_ref, val in zip(out_refs[:n_out], outs[:n_out]):
            o_ref[...] = val.astype(o_ref.dtype)
        i = pl.program_id(0)
        for o_ref, val in zip(out_refs[n_out:], outs[n_out:]):
            @pl.when(i == 0)
            def _(o_ref=o_ref, val=val):
                o_ref[...] = val

            @pl.when(i > 0)
            def _(o_ref=o_ref, val=val):
                o_ref[...] += val

    in_specs = [pl.BlockSpec((T, w), functools.partial(lambda i, cb: (i, cb), cb=cb)) for _, cb, w in rows]
    in_specs += [pl.BlockSpec(p.shape, lambda i: (0, 0)) for p in params]
    in_specs += [pl.BlockSpec(memory_space=pl.ANY)] * len(carried)
    out_specs, out_shape, aliases = [], [], {}
    for k, (o, t) in enumerate(zip(out_rows, into)):
        w, dt = o[0], o[1]
        if t is None:
            out_specs.append(pl.BlockSpec((T, w), lambda i: (i, 0)))
            out_shape.append(jax.ShapeDtypeStruct((S, w), dt))
            continue
        buf, total, first = t
        assert first % w == 0
        out_specs.append(pl.BlockSpec((T, w), functools.partial(lambda i, cb: (i, cb), cb=first // w)))
        out_shape.append(jax.ShapeDtypeStruct((S, total), dt))
        if buf is not None:
            aliases[n_rows + n_par + len(aliases)] = k
    out_specs += [pl.BlockSpec(s, lambda i: (0, 0)) for s in out_accs]
    out_shape += [jax.ShapeDtypeStruct(s, F32) for s in out_accs]
    return pl.pallas_call(
        body, name=name, grid=(S // T,), in_specs=in_specs, out_specs=out_specs, out_shape=out_shape,
        input_output_aliases=aliases, compiler_params=_cparams(("arbitrary",)),
    )(*[r[0] for r in rows], *params, *carried)


def _mm_sel(x, sel):
    hi = x.astype(BF16)
    lo = (x - hi.astype(F32)).astype(BF16)
    d = lambda u: jnp.dot(u, sel, preferred_element_type=F32)
    return d(hi) + d(lo)


@jax.custom_vjp
def _sel(x, sel, sel_t):
    return _mm_sel(x, sel)


def _sel_fwd(x, sel, sel_t):
    return _mm_sel(x, sel), (sel, sel_t)


def _sel_bwd(res, ct):
    sel, sel_t = res
    return _mm_sel(ct, sel_t), jnp.zeros_like(sel), jnp.zeros_like(sel_t)


_sel.defvjp(_sel_fwd, _sel_bwd)


def _rms(x, g):
    return x * lax.rsqrt(jnp.mean(x * x, axis=-1, keepdims=True) + NORM_EPS) * g


def _sigmoid(x):
    return 1.0 / (1.0 + jnp.exp(-x))


def _silu(x):
    return x * _sigmoid(x)


def _softplus(x):
    return jnp.maximum(x, 0.0) + jnp.log(1.0 + jnp.exp(-jnp.abs(x)))


def _bdot(x, w):
    return jnp.dot(x.astype(BF16), w.astype(BF16), preferred_element_type=F32)


def _f_mla_norm(q_a, kv_a, qg, kvg):
    return _rms(q_a, qg), _rms(kv_a, kvg)


def _f_rope(hm, qraw, kr_in, cosx, sinx, rot, rot_t):
    def rope(t):
        return t * cosx + _sel(t, rot, rot_t) * sinx
    parts = []
    for h in range(hm):
        parts.append(qraw[:, h * QHEAD:h * QHEAD + NOPE])
        parts.append(rope(qraw[:, h * QHEAD + NOPE:(h + 1) * QHEAD]))
    return jnp.concatenate(parts, axis=1), rope(kr_in)


def _f_rwkv_pre(rw, k, tail, w0f, w0b, a0f, a0b, k_k, k_a, w2cat, a2cat, seg, seg_t):
    split = w2cat.shape[0]
    zw = jnp.dot(jnp.tanh(tail[:, :split]).astype(BF16), w2cat, preferred_element_type=F32)
    za = jnp.dot(tail[:, split:].astype(BF16), a2cat, preferred_element_type=F32)
    return _f_rwkv_core(rw, k, zw, za, w0f, w0b, a0f, a0b, k_k, k_a, seg, seg_t)


def _f_rwkv_core(rw, k, zw, za, w0f, w0b, a0f, a0b, k_k, k_a, seg, seg_t):
    lw_f = -jnp.exp(-_softplus(-(w0f + zw[:, :rw])) - 0.5)
    lw_b = -jnp.exp(-_softplus(-(w0b + zw[:, rw:])) - 0.5)
    a_f = _sigmoid(a0f + za[:, :rw])
    a_b = _sigmoid(a0b + za[:, rw:])
    kk = k * k_k
    nrm = jnp.sqrt(_sel(_sel(kk * kk, seg, seg_t), seg_t, seg))
    kk = kk / jnp.maximum(nrm, 1e-12)
    k_f = k * (1.0 + (a_f - 1.0) * k_a)
    k_b = k * (1.0 + (a_b - 1.0) * k_a)
    return lw_f, lw_b, k_f, k_b, -kk, kk * a_f, kk * a_b


def _f_post(hn, y_f, y_b, r, k_f, k_b, v, z_r, o_mla, z_m, gn_g, gn_b, r_k, seg, seg_t):
    segsum = lambda t: _sel(_sel(t, seg, seg_t), seg_t, seg)
    y = y_f + y_b
    mu = segsum(y) * (1.0 / hn)
    yc = y - mu
    var = segsum(yc * yc) * (1.0 / hn)
    yn = yc * lax.rsqrt(var + GN_EPS) * gn_g + gn_b
    bonus = segsum(r * (k_f + k_b) * r_k) * v
    return o_mla * _silu(z_m), (yn + bonus) * _silu(z_r)


def _f_merge(u_m, u_r, g_m, g_r):
    return _sigmoid(g_m) * u_m + _sigmoid(g_r) * u_r


_NN = ((2,), (1,))
_NT = ((2,), (2,))
_TN = ((1,), (1,))

_SCAN_PASSES = {"cum": 2, "gram": 3, "solve": 1, "apply": 1, "state": 1}


def _hdot_raw(passes, x, y, dims):
    dn = (dims, ((0,), (0,)))
    d = lambda p, q: lax.dot_general(p, q, dn, preferred_element_type=F32)
    xh = x.astype(BF16)
    yh = y.astype(BF16)
    if passes == 1:
        return d(xh, yh)
    yl = (y - yh.astype(F32)).astype(BF16)
    if passes == 2:
        return d(xh, yh) + d(xh, yl)
    xl = (x - xh.astype(F32)).astype(BF16)
    return d(xh, yh) + d(xh, yl) + d(xl, yh)


@functools.partial(jax.custom_vjp, nondiff_argnums=(2, 3))
def _hdot_p(x, y, dims, passes):
    return _hdot_raw(passes, x, y, dims)


def _hdot_fwd(x, y, dims, passes):
    return _hdot_raw(passes, x, y, dims), (x, y)


def _hdot_bwd(dims, passes, res, ct):
    x, y = res
    if dims == _NN:
        return _hdot_raw(passes, ct, y, _NT), _hdot_raw(passes, x, ct, _TN)
    if dims == _NT:
        return _hdot_raw(passes, ct, y, _NN), _hdot_raw(passes, ct, x, _TN)
    return _hdot_raw(passes, y, ct, _NT), _hdot_raw(passes, x, ct, _NN)


_hdot_p.defvjp(_hdot_fwd, _hdot_bwd)


def _hdot(x, y, dims, kind):
    return _hdot_p(x, y, dims, _SCAN_PASSES[kind])


def _tri_solve(n_mat, x, length):
    row = lax.broadcasted_iota(jnp.int32, (length, length), 0)
    col = lax.broadcasted_iota(jnp.int32, (length, length), 1)
    eye = (row == col).astype(F32)[None]
    diag_blk = ((row // SUB) == (col // SUB))[None]
    nd = jnp.where(diag_blk, n_mat, 0.0)
    no = n_mat - nd
    dinv = eye + nd
    p = nd
    for _ in range(int(math.log2(SUB)) - 1):
        p = _hdot(p, p, _NN, "solve")
        dinv = dinv + _hdot(dinv, p, _NN, "solve")
    q = _hdot(dinv, no, _NN, "solve")
    u = _hdot(dinv, x, _NN, "solve")
    levels = int(math.log2(length // SUB))
    qs = [q]
    for _ in range(levels - 1):
        qs.append(_hdot(qs[-1], qs[-1], _NN, "solve"))
    for qk in reversed(qs):
        u = u + _hdot(qk, u, _NN, "solve")
    return u


def _rwkv_chunk(rev, s0, r, lw, k, v, a, b):
    pairs, length, width = r.shape
    hn = width // 2
    row = lax.broadcasted_iota(jnp.int32, (length, length), 0)
    col = lax.broadcasted_iota(jnp.int32, (length, length), 1)
    row2 = lax.broadcasted_iota(jnp.int32, (length, 2 * length), 0)
    col2 = lax.broadcasted_iota(jnp.int32, (length, 2 * length), 1)
    col2 = jnp.where(col2 >= length, col2 - length, col2)
    if rev is None:
        half = pairs // 2
        back = lax.broadcasted_iota(jnp.int32, (pairs, length, length), 0) >= half
        idx2 = lax.broadcasted_iota(jnp.int32, (2 * pairs, length, 2 * length), 0)
        back2 = ((idx2 >= half) & (idx2 < pairs)) | (idx2 >= pairs + half)
        ahead = jnp.where(back, (col - row)[None], (row - col)[None])
        ahead2 = jnp.where(back2, (col2 - row2)[None], (row2 - col2)[None])
        incl, strict2, incl2 = ahead >= 0, ahead2 > 0, ahead2 >= 0
    else:
        incl = ((row <= col) if rev else (row >= col))[None]
        strict2 = ((row2 < col2) if rev else (row2 > col2))[None]
        incl2 = ((row2 <= col2) if rev else (row2 >= col2))[None]
    lane = lax.broadcasted_iota(jnp.int32, (1, 1, width), 2)
    first = lane < hn
    head_mask = jnp.concatenate([jnp.broadcast_to(first.astype(F32), (pairs, 1, width)),
                                 jnp.broadcast_to(1.0 - first.astype(F32), (pairs, 1, width))], axis=0)
    twice = lambda t: jnp.concatenate([t, t], axis=0)
    pick = lambda t: jnp.where(first, t[:pairs], t[pairs:])

    t_incl = jnp.broadcast_to(incl.astype(F32), (pairs, length, length))
    cum = _hdot(t_incl, lw, _NN, "cum")
    g = jnp.exp(cum)
    g_inv = jnp.exp(-cum)
    at = a * jnp.exp(cum - lw)
    rt = r * g
    bt = b * g_inv
    kt = k * g_inv
    lhs = jnp.concatenate([twice(at) * head_mask, twice(rt) * head_mask], axis=1)
    rhs = jnp.concatenate([twice(bt), twice(kt)], axis=1)
    gram = _hdot(lhs, rhs, _NT, "gram")
    top = jnp.where(strict2, gram[:, :length], 0.0)
    bot = jnp.where(incl2, gram[:, length:], 0.0)
    v2 = twice(v)
    zeros = jnp.zeros_like(v2)
    x = _hdot(at, s0, _NT, "apply") + pick(_hdot(top, jnp.concatenate([zeros, v2], axis=1), _NN, "apply"))
    u = pick(_tri_solve(top[:, :, :length], twice(x), length))
    y = _hdot(rt, s0, _NT, "apply") + pick(_hdot(bot, jnp.concatenate([twice(u), v2], axis=1), _NN, "apply"))
    g_last = jnp.exp(jnp.sum(lw, axis=1, keepdims=True))
    ri = lax.broadcasted_iota(jnp.int32, (width, width), 0)
    ci = lax.broadcasted_iota(jnp.int32, (width, width), 1)
    same_head = ((ri < hn) == (ci < hn))[None]
    upd = _hdot(u, bt, _TN, "state") + _hdot(v, kt, _TN, "state")
    s1 = (s0 + jnp.where(same_head, upd, 0.0)) * g_last
    return y, s1


def _split_pairs(x):
    return jnp.stack([x[:, p * LANES:(p + 1) * LANES] for p in range(x.shape[1] // LANES)])


def _merge_pairs(x):
    return jnp.concatenate([x[p] for p in range(x.shape[0])], axis=1)


def _scan_specs(views, rw, nc, rev):
    cidx = (lambda c: nc - 1 - c) if rev else (lambda c: c)
    seqs = [pl.BlockSpec((CHUNK, rw), functools.partial(lambda c, cb: (cidx(c), cb), cb=cb)) for _, cb, _ in views]
    plain = pl.BlockSpec((CHUNK, rw), lambda c: (cidx(c), 0))
    st = pl.BlockSpec((1, rw // LANES, LANES, LANES), lambda c: (cidx(c), 0, 0, 0))
    return seqs, plain, st


def _as_views(arrs, rw):
    return [t if isinstance(t, tuple) else (t, 0, rw) for t in arrs]


def _rwkv_scan_fwd(ops_f, ops_b, rw, *, name):
    S = _as_views(ops_f, rw)[0][0].shape[0]
    nc, pairs = S // CHUNK, rw // LANES
    in_specs, out_specs, arrays = [], [], []
    for rev, ops in ((False, ops_f), (True, ops_b)):
        views = _as_views(ops, rw)
        seqs, plain, st = _scan_specs(views, rw, nc, rev)
        in_specs += seqs
        out_specs += [plain, st]
        arrays += [t[0] for t in views]

    def both(refs_f, refs_b):
        return [jnp.concatenate([_split_pairs(f[...]), _split_pairs(b[...])], axis=0) for f, b in zip(refs_f, refs_b)]

    def body(*refs):
        (y_f, st_f, y_b, st_b), s_ref = refs[12:16], refs[16]

        @pl.when(pl.program_id(0) == 0)
        def _():
            s_ref[...] = jnp.zeros_like(s_ref)

        s0 = s_ref[...]
        st_f[0] = s0[:pairs]
        st_b[0] = s0[pairs:]
        y, s1 = _rwkv_chunk(None, s0, *both(refs[:6], refs[6:12]))
        y_f[...] = _merge_pairs(y[:pairs])
        y_b[...] = _merge_pairs(y[pairs:])
        s_ref[...] = s1

    return pl.pallas_call(
        body, name=name, grid=(nc,), in_specs=in_specs, out_specs=out_specs,
        out_shape=[jax.ShapeDtypeStruct((S, rw), F32), jax.ShapeDtypeStruct((nc, pairs, LANES, LANES), F32)] * 2,
        scratch_shapes=[pltpu.VMEM((2 * pairs, LANES, LANES), F32)],
        compiler_params=_cparams(("arbitrary",)),
    )(*arrays)


def _rwkv_scan_bwd(ops_f, ops_b, states_f, states_b, dy, rw, *, name):
    S = dy.shape[0]
    nc, pairs = S // CHUNK, rw // LANES
    in_specs, arrays = [], []
    for rev, ops, states in ((False, ops_f, states_f), (True, ops_b, states_b)):
        views = _as_views(list(ops) + [dy], rw)
        seqs, plain, st = _scan_specs(views, rw, nc, not rev)
        in_specs += seqs + [st]
        arrays += [t[0] for t in views] + [states]
    out_specs = []
    for rev in (False, True):
        out_specs += [_scan_specs([], rw, nc, not rev)[1]] * 6

    def both(refs_f, refs_b):
        return [jnp.concatenate([_split_pairs(f[...]), _split_pairs(b[...])], axis=0) for f, b in zip(refs_f, refs_b)]

    def body(*refs):
        ds_ref = refs[28]

        @pl.when(pl.program_id(0) == 0)
        def _():
            ds_ref[...] = jnp.zeros_like(ds_ref)

        s0 = jnp.concatenate([refs[7][0], refs[15][0]], axis=0)
        _, vjp = jax.vjp(functools.partial(_rwkv_chunk, None), s0, *both(refs[:6], refs[8:14]))
        (dy,) = both(refs[6:7], refs[14:15])
        grads = vjp((dy, ds_ref[...]))
        ds_ref[...] = grads[0]
        for o_f, o_b, gval in zip(refs[16:22], refs[22:28], grads[1:]):
            o_f[...] = _merge_pairs(gval[:pairs])
            o_b[...] = _merge_pairs(gval[pairs:])

    return pl.pallas_call(
        body, name=name, grid=(nc,), in_specs=in_specs, out_specs=out_specs,
        out_shape=[jax.ShapeDtypeStruct((S, rw), F32)] * 12,
        scratch_shapes=[pltpu.VMEM((2 * pairs, LANES, LANES), F32)],
        compiler_params=_cparams(("arbitrary",)),
    )(*arrays)


def _shift_lerp(x_view, mu, d=None, into=None, *, name):
    arr, off, width = x_view
    S = arr.shape[0]
    cb = _pick(width, 256)
    assert off % cb == 0

    def cshift(t):
        rows = lax.broadcasted_iota(jnp.int32, t.shape, 0)
        prev = jnp.where(rows == 0, 0.0, pltpu.roll(t, 1, 0))
        nxt = jnp.where(rows == S - 1, 0.0, pltpu.roll(t, S - 1, 0))
        return 0.5 * (prev + nxt)

    def fwd_body(x_ref, mu_ref, o_ref):
        x = x_ref[...]
        o_ref[...] = x + mu_ref[...] * (cshift(x) - x)

    def bwd_body(x_ref, mu_ref, d_ref, _, dx_ref, dmu_ref):
        x, m, dd = x_ref[...], mu_ref[...], d_ref[...]
        gm = m * dd
        dx_ref[...] = (dd - gm + cshift(gm)).astype(dx_ref.dtype)
        dmu_ref[...] = jnp.sum(dd * (cshift(x) - x), axis=0, keepdims=True)

    x_spec = pl.BlockSpec((S, cb), lambda j: (0, off // cb + j))
    blk = pl.BlockSpec((S, cb), lambda j: (0, j))
    vec = pl.BlockSpec((1, cb), lambda j: (0, j))
    if d is None:
        return pl.pallas_call(
            fwd_body, name=name, grid=(width // cb,), in_specs=[x_spec, vec], out_specs=blk,
            out_shape=jax.ShapeDtypeStruct((S, width), F32), compiler_params=_cparams(("parallel",)),
        )(arr, mu)
    buf, first = into
    assert first % cb == 0
    return pl.pallas_call(
        bwd_body, name=name, grid=(width // cb,),
        in_specs=[x_spec, vec, blk, pl.BlockSpec(memory_space=pl.ANY)],
        out_specs=[pl.BlockSpec((S, cb), lambda j: (0, first // cb + j)), vec],
        out_shape=[jax.ShapeDtypeStruct(buf.shape, buf.dtype), jax.ShapeDtypeStruct((1, width), F32)],
        input_output_aliases={3: 0}, compiler_params=_cparams(("parallel",)),
    )(arr, mu, d, buf)


def _attention_fwd(qfull, kv, kr, hm, scale, *, tq, name):
    S = qfull.shape[0]
    nt = (((1,), (1,)), ((), ()))

    def body(q_ref, kn_ref, kr_ref, v_ref, o_ref, lse_ref, k_scr):
        _head_keys(kn_ref, kr_ref, k_scr)
        s = lax.dot_general(q_ref[...], k_scr[...], nt, preferred_element_type=F32)
        m = jnp.max(s, axis=-1, keepdims=True)
        p = jnp.exp((s - m) * scale)
        l = jnp.sum(p, axis=-1, keepdims=True)
        o_ref[...] = jnp.dot(p.astype(BF16), v_ref[...], preferred_element_type=F32) * (1.0 / l)
        lse_ref[...] = jnp.broadcast_to(m * scale + jnp.log(l), lse_ref.shape)

    oblk = pl.BlockSpec((tq, VDIM), lambda h, i: (i, h))
    return pl.pallas_call(
        body, name=name, grid=(hm, S // tq),
        in_specs=[pl.BlockSpec((tq, QHEAD), lambda h, i: (i, h)),
                  pl.BlockSpec((S, NOPE), lambda h, i: (0, h)),
                  pl.BlockSpec((S, LANES), lambda h, i: (0, 0)),
                  pl.BlockSpec((S, VDIM), lambda h, i: (0, hm + h))],
        out_specs=[oblk, oblk],
        out_shape=[jax.ShapeDtypeStruct((S, hm * VDIM), F32)] * 2,
        scratch_shapes=[pltpu.VMEM((S, QHEAD), BF16)],
        compiler_params=_cparams(("parallel", "arbitrary")),
    )(qfull, kv, kr, kv)


def _head_keys(kn_ref, kr_ref, k_scr):
    @pl.when(pl.program_id(1) == 0)
    def _():
        k_scr[:, :NOPE] = kn_ref[...]
        k_scr[:, NOPE:] = kr_ref[...]


def _attention_bwd(qfull, kv, kr, o, lse, d_o, hm, scale, *, tq, name):
    S = qfull.shape[0]
    tq = min(tq, S)
    nq = S // tq
    tn = (((0,), (0,)), ((), ()))
    nt = (((1,), (1,)), ((), ()))

    def body(q_ref, kn_ref, kr_ref, v_ref, o_ref, lse_ref, do_ref, dq_ref, dk_ref, dv_ref, k_scr):
        _head_keys(kn_ref, kr_ref, k_scr)
        s = lax.dot_general(q_ref[...], k_scr[...], nt, preferred_element_type=F32)
        p = jnp.exp(s * scale - lse_ref[:, 0:1])
        d_out = do_ref[...]
        delta = jnp.sum(d_out * o_ref[...], axis=-1, keepdims=True)
        d_out = d_out.astype(BF16)
        dp = lax.dot_general(d_out, v_ref[...], nt, preferred_element_type=F32)
        ds = (p * (dp - delta)).astype(BF16)
        dq_ref[...] = jnp.dot(ds, k_scr[...], preferred_element_type=F32) * scale
        dv = lax.dot_general(p.astype(BF16), d_out, tn, preferred_element_type=F32)
        dk = lax.dot_general(ds, q_ref[...], tn, preferred_element_type=F32)
        i = pl.program_id(1)
        for ref, val in ((dk_ref, dk), (dv_ref, dv)):
            @pl.when(i == 0)
            def _(ref=ref, val=val):
                ref[...] = val

            @pl.when(i > 0)
            def _(ref=ref, val=val):
                ref[...] += val

        @pl.when(i == nq - 1)
        def _():
            dk_ref[...] = dk_ref[...] * scale

    qblk = pl.BlockSpec((tq, QHEAD), lambda h, i: (i, h))
    oblk = pl.BlockSpec((tq, VDIM), lambda h, i: (i, h))
    return pl.pallas_call(
        body, name=name, grid=(hm, nq),
        in_specs=[qblk,
                  pl.BlockSpec((S, NOPE), lambda h, i: (0, h)),
                  pl.BlockSpec((S, LANES), lambda h, i: (0, 0)),
                  pl.BlockSpec((S, VDIM), lambda h, i: (0, hm + h)),
                  oblk, oblk, oblk],
        out_specs=[qblk, pl.BlockSpec((S, QHEAD), lambda h, i: (0, h)), pl.BlockSpec((S, VDIM), lambda h, i: (0, h))],
        out_shape=[jax.ShapeDtypeStruct((S, hm * QHEAD), F32), jax.ShapeDtypeStruct((S, hm * QHEAD), F32),
                   jax.ShapeDtypeStruct((S, hm * VDIM), F32)],
        scratch_shapes=[pltpu.VMEM((S, QHEAD), BF16)],
        compiler_params=_cparams(("parallel", "arbitrary")),
    )(qfull, kv, kr, kv, o, lse, d_o)


def _layout(D, MW, RW, TAIL, QR, KVR):
    names = ["gate_m", "gate_r", "z_m", "z_r", "q_a", "kv_a", "r", "k", "v", "tail"]
    widths = [D, D, MW, RW, QR, KVR, RW, RW, RW, TAIL]
    offs, o = {}, 0
    for nme, w in zip(names, widths):
        assert o % w == 0, (nme, o, w)
        offs[nme] = (o, w)
        o += w
    return offs, o


def _local_grads(x, target, W, dims, exchange=None):
    S, D = x.shape
    hm, hr, hn, rank = dims["hm"], dims["hr"], dims["hn"], dims["rank"]
    MW, RW = hm * VDIM, hr * hn
    TAIL = dims["TAIL"]
    QR, KVR = W["mla_q_norm"].shape[1], W["mla_kv_norm"].shape[1]
    lay, d_in = _layout(D, MW, RW, TAIL, QR, KVR)
    T = 256
    scale = (NOPE + ROPE) ** -0.5
    col = lambda arr, nme: _view(arr, *lay[nme])

    pos = jnp.arange(S, dtype=F32)
    inv_freq = jnp.power(ROPE_THETA, -jnp.arange(0, ROPE, 2, dtype=F32) / ROPE)
    ang = pos[:, None] * inv_freq[None, :]
    zpad = jnp.zeros((S, LANES - ROPE), F32)
    cosx = jnp.concatenate([jnp.cos(ang), jnp.cos(ang), zpad], axis=1)
    sinx = jnp.concatenate([jnp.sin(ang), jnp.sin(ang), zpad], axis=1)
    ri, ci = jnp.arange(LANES)[:, None], jnp.arange(LANES)[None, :]
    half = ROPE // 2
    rot = (jnp.where((ri == ci - half) & (ci >= half) & (ci < ROPE), 1.0, 0.0)
           - jnp.where((ri == ci + half) & (ci < half), 1.0, 0.0)).astype(BF16)
    rot_t = rot.T
    seg = (jnp.arange(RW)[:, None] // hn == jnp.arange(LANES)[None, :]).astype(BF16)
    seg_t = seg.T

    (h,) = _rowwise(lambda xb, g: (_rms(xb, g),), [x], [W["g_pre"]], [(D, BF16)], tile=T, name="pre_norm")
    if exchange is None:
        proj = _mm(h, W["w_in_t"], tb=True, name="in_proj")
    else:
        proj, *slabs = _mm(h, W["w_in_t"], tb=True, ride=_gather_plan(exchange[0]), name="in_proj")
        W = {**W, **_prepare_rest(dict(zip(_MATS[1:], slabs)), dims)}

    qn, kvn = _rowwise(_f_mla_norm, [col(proj, "q_a"), col(proj, "kv_a")], [W["mla_q_norm"], W["mla_kv_norm"]],
                       [(QR, BF16), (KVR, BF16)], tile=T, name="mla_norm")
    qraw = _mm(qn, W["wq_b_t"], tb=True, name="q_up")
    kv = _mm(kvn, W["wkv_b"], out_dtype=BF16, name="kv_up")
    kr_view = _view(proj, lay["tail"][0], LANES)
    qfull, kr = _rowwise(functools.partial(_f_rope, hm), [qraw, kr_view, cosx, sinx], [rot, rot_t],
                         [(hm * QHEAD, BF16), (LANES, BF16)], tile=T, name="rope")
    o_mla, lse = _attention_fwd(qfull, kv, kr, hm, scale, tq=T, name="attn_fwd")

    shift_view = (proj, lay["r"][0], 3 * RW + TAIL)
    rl = _shift_lerp(shift_view, W["mu"], name="shift_fwd")
    rl_r, rl_k, rl_v = _view(rl, 0, RW), _view(rl, RW, RW), _view(rl, 2 * RW, RW)
    rl_tail = _view(rl, 3 * RW, TAIL)
    pre_params = [W["w0_f"], W["w0_b"], W["a0_f"], W["a0_b"], W["k_k"], W["k_a"], W["w2cat"], W["a2cat"], seg, seg_t]
    pre_fn = functools.partial(_f_rwkv_pre, RW)
    lw_f, lw_b, k_f, k_b, a_n, b_f, b_b = _rowwise(pre_fn, [rl_k, rl_tail], pre_params, [(RW, F32)] * 7, tile=T,
                                                    name="rwkv_pre")
    ops_f = (rl_r, lw_f, k_f, rl_v, a_n, b_f)
    ops_b = (rl_r, lw_b, k_b, rl_v, a_n, b_b)
    y_f, st_f, y_b, st_b = _rwkv_scan_fwd(ops_f, ops_b, RW, name="scan_fwd")

    post_fn = functools.partial(_f_post, hn)
    post_rows = [y_f, y_b, rl_r, k_f, k_b, rl_v, col(proj, "z_r"), o_mla, col(proj, "z_m")]
    post_params = [W["gn_g"], W["gn_b"], W["r_k"], seg, seg_t]
    ymg, yrg = _rowwise(post_fn, post_rows, post_params, [(MW, BF16), (RW, BF16)], tile=T, name="post")
    u_m = _mm(ymg, W["w_br_mla"], name="br_mla")
    u_r = _mm(yrg, W["w_br_rwkv"], name="br_rwkv")
    merge_rows = [u_m, u_r, col(proj, "gate_m"), col(proj, "gate_r")]
    (merged,) = _rowwise(lambda *t: (_f_merge(*t),), merge_rows, [], [(D, BF16)], tile=T, name="merge")
    out = _mm(merged, W["w_out"], name="out_proj")

    def head(ob, xb, tb, g):
        yn, vjp = jax.vjp(_rms, ob, g)
        err = xb + yn - tb
        dy = err * (1.0 / D)
        d_ob, d_g = vjp(dy)
        loss = jnp.broadcast_to(0.5 * jnp.sum(err * err) * (1.0 / D), (1, LANES))
        return dy, d_ob, loss, d_g

    dy, d_out, loss, g_g_post = _rowwise(head, [out, x, target], [W["g_post"]], [(D, F32), (D, BF16)],
                                         [(1, LANES), (1, D)], tile=T, name="head")
    d_merged = _mm(d_out, W["w_out"], tb=True, name="d_merged")
    g_w_out = _mm(merged, d_out, ta=True, out_dtype=BF16, name="g_w_out")

    def merge_bwd(u_m_b, u_r_b, g_m_b, g_r_b, dm):
        _, vjp = jax.vjp(_f_merge, u_m_b, u_r_b, g_m_b, g_r_b)
        du_m, du_r, dg_m, dg_r = vjp(dm)
        return du_m, du_r, jnp.concatenate([dg_m, dg_r], axis=1)

    d_u_m, d_u_r, d_proj = _rowwise(merge_bwd, merge_rows + [d_merged], [],
                                    [(D, BF16), (D, BF16), (2 * D, BF16, (None, d_in, lay["gate_m"][0]))], tile=T,
                                    name="merge_bwd")
    d_ymg = _mm(d_u_m, W["w_br_mla"], tb=True, name="d_ymg")
    d_yrg = _mm(d_u_r, W["w_br_rwkv"], tb=True, name="d_yrg")
    g_w_br_mla = _mm(ymg, d_u_m, ta=True, out_dtype=BF16, name="g_w_br_mla")
    g_w_br_rwkv = _mm(yrg, d_u_r, ta=True, out_dtype=BF16, name="g_w_br_rwkv")

    def post_bwd(*args):
        nr = len(post_rows)
        prim, dm, dr = args[:nr] + args[nr + 2:], args[nr], args[nr + 1]
        _, vjp = jax.vjp(post_fn, *prim)
        g = vjp((dm, dr))
        return g[0], g[2], g[3], g[5], g[7], jnp.concatenate([g[8], g[6]], axis=1), g[9], g[10], g[11]

    (d_y, d_r_bonus, d_k_bonus, d_v_bonus, d_o, d_proj, g_gn_g, g_gn_b, g_r_k) = _rowwise(
        post_bwd, post_rows + [d_ymg, d_yrg], post_params,
        [(RW, F32), (RW, F32), (RW, F32), (RW, F32), (MW, F32), (MW + RW, BF16, (d_proj, d_in, lay["z_m"][0]))],
        [(1, RW)] * 3, tile=T // 2, name="post_bwd")

    dscan = _rwkv_scan_bwd(ops_f, ops_b, st_f, st_b, d_y, RW, name="scan_bwd")
    dsc = {"f": dscan[:6], "b": dscan[6:]}

    d_q_att, d_k_att, d_v_att = _attention_bwd(qfull, kv, kr, o_mla, lse, d_o, hm, scale, tq=2 * T, name="attn_bwd")

    def rope_bwd(qraw_b, kr_in, cos_b, sin_b, dq_b, dk_b, dv_b, rot_b, rot_t_b):
        _, vjp = jax.vjp(lambda q_, k_: _f_rope(hm, q_, k_, cos_b, sin_b, rot_b, rot_t_b), qraw_b, kr_in)
        dkn = jnp.concatenate([dk_b[:, hh * QHEAD:hh * QHEAD + NOPE] for hh in range(hm)], axis=1)
        dkr = dk_b[:, NOPE:QHEAD]
        for hh in range(1, hm):
            dkr = dkr + dk_b[:, hh * QHEAD + NOPE:(hh + 1) * QHEAD]
        d_qraw, d_kr_in = vjp((dq_b, dkr))
        return d_qraw, jnp.concatenate([dkn, dv_b], axis=1), d_kr_in

    d_qraw, d_kv, d_kr_in = _rowwise(rope_bwd, [qraw, kr_view, cosx, sinx, d_q_att, d_k_att, d_v_att],
                                     [rot, rot_t], [(hm * QHEAD, BF16), (2 * MW, BF16), (LANES, F32)], tile=T,
                                     name="rope_bwd")
    d_qnorm = _mm(d_qraw, W["wq_b_t"], name="d_qn")
    d_kvnorm = _mm(d_kv, W["wkv_b"], tb=True, name="d_kvn")
    g_wq_b = _mm(d_qraw, qn, ta=True, out_dtype=BF16, name="g_wq_b")
    g_wkv_b = _mm(kvn, d_kv, ta=True, out_dtype=BF16, name="g_wkv_b")

    def mla_norm_bwd(q_a, kv_a, qg, kvg, dq, dk):
        _, vjp = jax.vjp(_f_mla_norm, q_a, kv_a, qg, kvg)
        d_q_a, d_kv_a, d_qg, d_kvg = vjp((dq, dk))
        return jnp.concatenate([d_q_a, d_kv_a], axis=1), d_qg, d_kvg

    d_proj, g_q_norm, g_kv_norm = _rowwise(
        lambda q_a, kv_a, dq, dk, qg, kvg: mla_norm_bwd(q_a, kv_a, qg, kvg, dq, dk),
        [col(proj, "q_a"), col(proj, "kv_a"), d_qnorm, d_kvnorm], [W["mla_q_norm"], W["mla_kv_norm"]],
        [(QR + KVR, BF16, (d_proj, d_in, lay["q_a"][0]))], [(1, QR), (1, KVR)], tile=T, name="mla_norm_bwd")

    def pre_bwd(k_b_, tail_b, dlwf, dlwb, dkf, dkb, dkbon, daf, dab, dbf, dbb, drf, drb, drbon, dvf, dvb, dvbon,
                dkr, *params):
        w2, a2 = params[6], params[7]
        nt, tn = (((1,), (1,)), ((), ())), (((0,), (0,)), ((), ()))
        split = w2.shape[0]
        th = jnp.tanh(tail_b[:, :split])
        th_b, tail_h = th.astype(BF16), tail_b[:, split:].astype(BF16)
        zw = jnp.dot(th_b, w2, preferred_element_type=F32)
        za = jnp.dot(tail_h, a2, preferred_element_type=F32)
        _, vjp = jax.vjp(functools.partial(_f_rwkv_core, RW), k_b_, zw, za, *params[:6], params[8], params[9])
        g = vjp((dlwf, dlwb, dkf + dkbon, dkb + dkbon, daf + dab, dbf, dbb))
        d_zw, d_za = g[1].astype(BF16), g[2].astype(BF16)
        d_tail = (jnp.concatenate([lax.dot_general(d_zw, w2, nt, preferred_element_type=F32) * (1.0 - th * th),
                                   lax.dot_general(d_za, a2, nt, preferred_element_type=F32)], axis=1)
                  + jnp.concatenate([dkr, jnp.zeros((dkr.shape[0], TAIL - LANES), F32)], axis=1))
        g_w2 = lax.dot_general(th_b, d_zw, tn, preferred_element_type=F32)
        g_a2 = lax.dot_general(tail_h, d_za, tn, preferred_element_type=F32)
        d_rl = jnp.concatenate([drf + drb + drbon, g[0], dvf + dvb + dvbon, d_tail], axis=1)
        return (d_rl,) + tuple(g[3:9]) + (g_w2, g_a2)

    f_, b_ = dsc["f"], dsc["b"]
    pre_bwd_rows = [rl_k, rl_tail, f_[1], b_[1], f_[2], b_[2], d_k_bonus, f_[4], b_[4], f_[5], b_[5],
                    f_[0], b_[0], d_r_bonus, f_[3], b_[3], d_v_bonus, d_kr_in]
    (d_rl, g_w0_f, g_w0_b, g_a0_f, g_a0_b, g_k_k, g_k_a, g_w2cat, g_a2cat) = _rowwise(
        pre_bwd, pre_bwd_rows, pre_params, [(3 * RW + TAIL, F32)],
        [(1, RW)] * 6 + [W["w2cat"].shape, W["a2cat"].shape], tile=T // 2, name="rwkv_pre_bwd")
    d_proj, g_mu = _shift_lerp(shift_view, W["mu"], d_rl, (d_proj, lay["r"][0]), name="shift_bwd")
    small = dict(wq_b=g_wq_b, wkv_b=g_wkv_b, w2cat=g_w2cat, a2cat=g_a2cat, w_br_mla=g_w_br_mla,
                 w_br_rwkv=g_w_br_rwkv, w_out=g_w_out)
    if exchange is None:
        received = None
        g_w_in = _mm(d_proj, h, ta=True, out_dtype=BF16, tn_cap=1024, name="g_w_in")
        d_h = _mm(d_proj, W["w_in_t"], tn_cap=1024, name="d_h")
    else:
        slabs = _restore_rest(small, dims)
        slabs = [slabs[n] for n in _MATS[1:]]
        g_w_in, *got = _mm(d_proj, h, ta=True, out_dtype=BF16, tn_cap=1024, ride=_sibling_swap_plan(slabs),
                           name="g_w_in")
        sums = [_pair_add(exchange[1], s, t, name="pair_add_" + n) for n, s, t in zip(_MATS[1:], slabs, got)]
        d_h, *received = _mm(d_proj, W["w_in_t"], tn_cap=1024, ride=_chip_exchange_plan(sums), name="d_h")
        small = {}

    def pre_norm_bwd(xb, dyb, dhb, g):
        _, vjp = jax.vjp(_rms, xb, g)
        dx, dg = vjp(dhb)
        return dyb + dx, dg

    grad_x, g_g_pre = _rowwise(pre_norm_bwd, [x, dy, d_h], [W["g_pre"]], [(D, F32)], [(1, D)], tile=T,
                               name="pre_norm_bwd")

    grads = dict(g_pre=g_g_pre, w_in=g_w_in, mla_q_norm=g_q_norm, mla_kv_norm=g_kv_norm, mu=g_mu, w0_f=g_w0_f,
                 w0_b=g_w0_b, a0_f=g_a0_f, a0_b=g_a0_b, k_k=g_k_k, k_a=g_k_a, r_k=g_r_k, gn_g=g_gn_g, gn_b=g_gn_b,
                 g_post=g_g_post, **small)
    return loss[0, 0], grad_x, grads, received


_MATS = ["w_in", "mla_wq_b", "mla_wkv_b", "rwkv_w2_f", "rwkv_w2_b", "rwkv_a2_f", "rwkv_a2_b", "w_br_mla",
         "w_br_rwkv", "w_out"]
_ROW_SHARDED = ("w_out",)
_TRANSPOSED = ("w_in", "mla_wq_b")
_VECS = ["g_pre", "mla_q_norm", "mla_kv_norm", "rwkv_mu", "rwkv_w0_f", "rwkv_w0_b", "rwkv_a0_f", "rwkv_a0_b",
         "rwkv_k_k", "rwkv_k_a", "rwkv_r_k", "rwkv_gn_g", "rwkv_gn_b", "g_post"]
_WEIGHTS = ["g_pre", "w_in", "mla_q_norm", "mla_wq_b", "mla_kv_norm", "mla_wkv_b", "rwkv_mu", "rwkv_w0_f",
            "rwkv_w2_f", "rwkv_w0_b", "rwkv_w2_b", "rwkv_a0_f", "rwkv_a2_f", "rwkv_a0_b", "rwkv_a2_b", "rwkv_k_k",
            "rwkv_k_a", "rwkv_r_k", "rwkv_gn_g", "rwkv_gn_b", "w_br_mla", "w_br_rwkv", "w_out", "g_post"]

def _exchange(srcs, *, name):
    n = len(srcs)

    def body(*refs):
        src_refs, out_refs = refs[:n], refs[n:2 * n]
        send_sems, recv_sems, local_sems = refs[2 * n:]
        x, y, c = lax.axis_index("x"), lax.axis_index("y"), lax.axis_index("c")
        me = 4 * x + 2 * y + c
        flip = lambda v, bit: (1 - v) if bit else v

        def piece(a, idx):
            return src_refs[a] if srcs[a].ndim == 2 else src_refs[a].at[idx]

        owns = [pltpu.make_async_copy(piece(a, me), out_refs[a].at[me], local_sems.at[a]) for a in range(n)]
        for cp in owns:
            cp.start()
        sends, peers = [], []
        for d in range(1, N_DEV):
            px, py, pc = flip(x, d & 4), flip(y, d & 2), flip(c, d & 1)
            pidx = 4 * px + 2 * py + pc
            peers.append(((px, py, pc), pidx))
            for a in range(n):
                cp = pltpu.make_async_remote_copy(
                    src_ref=piece(a, pidx), dst_ref=out_refs[a].at[me], send_sem=send_sems.at[d - 1, a],
                    recv_sem=recv_sems.at[d - 1, a], device_id=(px, py, pc), device_id_type=pl.DeviceIdType.MESH)
                cp.start()
                sends.append(cp)
        for d, (peer, pidx) in zip(range(1, N_DEV), peers):
            for a in range(n):
                pltpu.make_async_remote_copy(
                    src_ref=piece(a, pidx), dst_ref=out_refs[a].at[pidx], send_sem=send_sems.at[d - 1, a],
                    recv_sem=recv_sems.at[d - 1, a], device_id=peer, device_id_type=pl.DeviceIdType.MESH).wait_recv()
        for cp in sends:
            cp.wait_send()
        for cp in owns:
            cp.wait()

    return pl.pallas_call(
        body, name=name,
        out_shape=[jax.ShapeDtypeStruct((N_DEV,) + s.shape[-2:], s.dtype) for s in srcs],
        in_specs=[pl.BlockSpec(memory_space=pl.ANY)] * n, out_specs=[pl.BlockSpec(memory_space=pl.ANY)] * n,
        scratch_shapes=[pltpu.SemaphoreType.DMA((N_DEV - 1, n)), pltpu.SemaphoreType.DMA((N_DEV - 1, n)),
                        pltpu.SemaphoreType.DMA((n,))],
    )(*srcs)


def _remote(src, dst, sems, key, to):
    send_sems, recv_sems = sems
    return pltpu.make_async_remote_copy(src_ref=src, dst_ref=dst, send_sem=send_sems.at[key], recv_sem=recv_sems.at[key],
                                        device_id=to, device_id_type=pl.DeviceIdType.MESH)


def _run_exchange(plan, *, name):
    srcs, out_shapes, sem_shapes, phases = plan
    n, m = len(srcs), len(out_shapes)

    def body(*refs):
        for phase in phases(refs[:n], refs[n:n + m], refs[n + m:]):
            phase()

    return pl.pallas_call(
        body, name=name, out_shape=out_shapes,
        in_specs=[pl.BlockSpec(memory_space=pl.ANY)] * n, out_specs=[pl.BlockSpec(memory_space=pl.ANY)] * m,
        scratch_shapes=[pltpu.SemaphoreType.DMA(s) for s in sem_shapes],
    )(*srcs)


def _gather_plan(srcs):
    n = len(srcs)

    def phases(src_refs, out_refs, sem_refs):
        sems, local_sems = sem_refs[:2], sem_refs[2]
        x, y, c = lax.axis_index("x"), lax.axis_index("y"), lax.axis_index("c")
        idx = lambda px, py, pc: 4 * px + 2 * py + pc
        me, sibling = (x, y, c), (x, y, 1 - c)
        chips = [(1 - x, y), (x, 1 - y), (1 - x, 1 - y)]
        own = lambda a: pltpu.make_async_copy(src_refs[a], out_refs[a].at[idx(*me)], local_sems.at[a])
        to_sibling = lambda a: _remote(src_refs[a], out_refs[a].at[idx(*me)], sems, (0, a), sibling)
        to_chip = lambda a, j: _remote(src_refs[a], out_refs[a].at[idx(*me)], sems, (1 + j, a), (*chips[j], c))
        landed = lambda a, j: out_refs[a].at[idx(*chips[j], c)]
        passed_on = lambda a, j: _remote(landed(a, j), landed(a, j), sems, (4 + j, a), sibling)

        def first():
            for a in range(n):
                own(a).start()
                to_sibling(a).start()
                for j in range(3):
                    to_chip(a, j).start()

        def middle():
            for j in range(3):
                for a in range(n):
                    _remote(landed(a, j), landed(a, j), sems, (1 + j, a), me).wait_recv()
                    passed_on(a, j).start()

        def last():
            for a in range(n):
                blk = out_refs[a].at[idx(*sibling)]
                _remote(blk, blk, sems, (0, a), me).wait_recv()
                for j in range(3):
                    blk = out_refs[a].at[idx(*chips[j], 1 - c)]
                    _remote(blk, blk, sems, (4 + j, a), me).wait_recv()
            for a in range(n):
                to_sibling(a).wait_send()
                for j in range(3):
                    to_chip(a, j).wait_send()
                    passed_on(a, j).wait_send()
                own(a).wait()

        return first, middle, last

    return srcs, [jax.ShapeDtypeStruct((N_DEV,) + s.shape, s.dtype) for s in srcs], [(7, n), (7, n), (n,)], phases


def _sibling_swap_plan(srcs):
    n = len(srcs)

    def phases(src_refs, out_refs, sems):
        x, y, c = lax.axis_index("x"), lax.axis_index("y"), lax.axis_index("c")
        copies = lambda: [_remote(src_refs[a].at[2 * q + 1 - c], out_refs[a].at[q], sems, (q, a), (x, y, 1 - c))
                          for a in range(n) for q in range(4)]

        def first():
            for cp in copies():
                cp.start()

        def last():
            for cp in copies():
                cp.wait()

        return first, (lambda: None), last

    return srcs, [jax.ShapeDtypeStruct((4,) + s.shape[1:], s.dtype) for s in srcs], [(4, n), (4, n)], phases


def _chip_exchange_plan(srcs):
    n = len(srcs)

    def phases(src_refs, out_refs, sem_refs):
        sems, local_sems = sem_refs[:2], sem_refs[2]
        x, y, c = lax.axis_index("x"), lax.axis_index("y"), lax.axis_index("c")
        mine = 2 * x + y
        chips = [(1 - x, y), (x, 1 - y), (1 - x, 1 - y)]
        own = lambda a: pltpu.make_async_copy(src_refs[a].at[mine], out_refs[a].at[mine], local_sems.at[a])
        send = lambda a, j: _remote(src_refs[a].at[2 * chips[j][0] + chips[j][1]], out_refs[a].at[mine], sems, (j, a),
                                    (*chips[j], c))

        def first():
            for a in range(n):
                own(a).start()
                for j in range(3):
                    send(a, j).start()

        def last():
            for j in range(3):
                for a in range(n):
                    blk = out_refs[a].at[2 * chips[j][0] + chips[j][1]]
                    _remote(blk, blk, sems, (j, a), (x, y, c)).wait_recv()
            for a in range(n):
                for j in range(3):
                    send(a, j).wait_send()
                own(a).wait()

        return first, (lambda: None), last

    return srcs, [jax.ShapeDtypeStruct(s.shape, s.dtype) for s in srcs], [(3, n), (3, n), (n,)], phases


def _pair_add(core, g, got, *, name):
    q, r, c = got.shape
    tr, tc = _tile2d(r, c)

    def body(core_ref, a_ref, b_ref, o_ref):
        o_ref[...] = (a_ref[...].astype(F32) + b_ref[...].astype(F32)).astype(BF16)

    blk = pl.BlockSpec((1, tr, tc), lambda i, j, k, core_ref: (i, j, k))
    mine = pl.BlockSpec((1, tr, tc), lambda i, j, k, core_ref: (2 * i + core_ref[0], j, k))
    return pl.pallas_call(
        body, name=name, out_shape=jax.ShapeDtypeStruct(got.shape, BF16),
        grid_spec=pltpu.PrefetchScalarGridSpec(num_scalar_prefetch=1, grid=(q, r // tr, c // tc),
                                               in_specs=[mine, blk], out_specs=blk),
        compiler_params=_cparams(("parallel", "parallel", "parallel")))(core, g, got)


def _adamw(recv, w, m, v, *, name):
    r, c = w.shape
    n_terms = recv.shape[0]
    tr, tc = _tile2d(r, c)

    def body(g_ref, w_ref, m_ref, v_ref, go_ref, d_ref, mo_ref, vo_ref):
        g = g_ref[0].astype(F32)
        for k in range(1, n_terms):
            g = g + g_ref[k].astype(F32)
        m_new = ADAM_B1 * m_ref[...] + (1.0 - ADAM_B1) * g
        v_new = ADAM_B2 * v_ref[...] + (1.0 - ADAM_B2) * (g * g)
        m_hat = m_new / (1.0 - ADAM_B1 ** ADAM_STEP)
        v_hat = v_new / (1.0 - ADAM_B2 ** ADAM_STEP)
        go_ref[...] = g
        d_ref[...] = -ADAM_LR * (m_hat / (jnp.sqrt(v_hat) + ADAM_EPS) + ADAM_WD * w_ref[...])
        mo_ref[...] = m_new
        vo_ref[...] = v_new

    blk = pl.BlockSpec((tr, tc), lambda i, j: (i, j))
    return pl.pallas_call(
        body, name=name, grid=(r // tr, c // tc),
        in_specs=[pl.BlockSpec((n_terms, tr, tc), lambda i, j: (0, i, j)), blk, blk, blk], out_specs=[blk] * 4,
        out_shape=[jax.ShapeDtypeStruct((r, c), F32)] * 4, compiler_params=_cparams(("parallel", "parallel")),
    )(recv, w, m, v)


def _tile2d(r, c, cap=256):
    if r <= cap:
        return r, c
    for t in range(cap, 0, -BF16_ROWS):
        if r % t == 0:
            return t, c
    return r, _pick(c, cap)


def _pack(pieces, dtype, quantum):
    out = []
    for p in pieces:
        lead, n = p.shape[:-1], p.shape[-1]
        pad = (-n) % quantum
        p = p.astype(dtype)
        if pad:
            p = jnp.concatenate([p, jnp.zeros(lead + (pad,), dtype)], axis=-1)
        out.append(p)
    flat = jnp.concatenate(out, axis=-1)
    return flat.reshape(flat.shape[:-1] + (flat.shape[-1] // LANES, LANES))


def _unpack(flat, sizes, quantum):
    flat = flat.reshape(flat.shape[:-2] + (-1,))
    out, o = [], 0
    for n in sizes:
        out.append(flat[..., o:o + n])
        o += n + (-n) % quantum
    return out


def _prepare_weights(full, vec, dims):
    rest = {n: t for n, t in full.items() if n != "w_in"}
    return {"w_in_t": _prepare_w_in(full["w_in"], dims), **_prepare_rest(rest, dims), **_prepare_vectors(vec, dims)}


def _prepare_w_in(slabs, dims):
    D = dims["D"]
    c = slabs.shape[1]
    parts, pos = [], 0
    for orig_off, width, perm_off in sorted(dims["segs"], key=lambda t: t[2]):
        if perm_off > pos:
            parts.append(jnp.zeros((perm_off - pos, D), BF16))
        for k in range(N_DEV):
            lo, hi = max(orig_off, k * c), min(orig_off + width, (k + 1) * c)
            if lo < hi:
                parts.append(slabs[k][lo - k * c:hi - k * c])
        pos = perm_off + width
    if dims["d_in_perm"] > pos:
        parts.append(jnp.zeros((dims["d_in_perm"] - pos, D), BF16))
    return jnp.concatenate(parts, axis=0)


def _prepare_rest(full, dims):
    hm, hr, hn, rank = dims["hm"], dims["hr"], dims["hn"], dims["rank"]
    QR, KVR = dims["QR"], dims["KVR"]
    RW, TAIL = hr * hn, dims["TAIL"]
    full = {n: (t.reshape(-1, t.shape[2]) if n in _ROW_SHARDED + _TRANSPOSED
                else t.transpose(1, 0, 2).reshape(t.shape[1], -1)) for n, t in full.items()}
    wq = full["mla_wq_b"].reshape(hm, NOPE + ROPE, QR)
    wq = jnp.concatenate([wq, jnp.zeros((hm, QHEAD - NOPE - ROPE, QR), BF16)], axis=1).reshape(hm * QHEAD, QR)
    wkv = full["mla_wkv_b"].reshape(KVR, hm, 2, NOPE).transpose(0, 2, 1, 3).reshape(KVR, 2 * hm * NOPE)
    z = lambda rows: jnp.zeros((rows, RW), BF16)
    f = lambda nme: full[nme]
    split = ROPE + 2 * rank
    assert split % LANES == 0, split
    w2cat = jnp.concatenate([
        jnp.concatenate([z(ROPE), f("rwkv_w2_f"), z(rank)], axis=0),
        jnp.concatenate([z(ROPE + rank), f("rwkv_w2_b")], axis=0)], axis=1)
    a2cat = jnp.concatenate([
        jnp.concatenate([f("rwkv_a2_f"), z(TAIL - split - rank)], axis=0),
        jnp.concatenate([z(rank), f("rwkv_a2_b"), z(TAIL - split - 2 * rank)], axis=0)], axis=1)
    return dict(wq_b_t=wq, wkv_b=wkv, w2cat=w2cat, a2cat=a2cat, w_br_mla=full["w_br_mla"],
                w_br_rwkv=full["w_br_rwkv"], w_out=full["w_out"])


def _prepare_vectors(vec, dims):
    rank, RW, TAIL = dims["rank"], dims["hr"] * dims["hn"], dims["TAIL"]
    mu = vec["rwkv_mu"]
    mu_p = jnp.concatenate([mu[:3 * RW], jnp.zeros((ROPE,), F32), mu[3 * RW:],
                            jnp.zeros((TAIL - ROPE - 4 * rank,), F32)])
    row = lambda t: t.reshape(1, -1)
    return dict(
        mu=row(mu_p), g_pre=row(vec["g_pre"]), g_post=row(vec["g_post"]), mla_q_norm=row(vec["mla_q_norm"]),
        mla_kv_norm=row(vec["mla_kv_norm"]), w0_f=row(vec["rwkv_w0_f"]), w0_b=row(vec["rwkv_w0_b"]),
        a0_f=row(vec["rwkv_a0_f"]), a0_b=row(vec["rwkv_a0_b"]), k_k=row(vec["rwkv_k_k"]), k_a=row(vec["rwkv_k_a"]),
        r_k=row(vec["rwkv_r_k"]), gn_g=row(vec["rwkv_gn_g"]), gn_b=row(vec["rwkv_gn_b"]))


def _restore_grads(g, dims):
    return {"w_in": _restore_w_in(g["w_in"], dims), **_restore_rest(g, dims), **_restore_vectors(g, dims)}


def _restore_w_in(gw, dims):
    c = dims["d_in"] // N_DEV
    slabs = []
    for k in range(N_DEV):
        parts = []
        for orig_off, width, perm_off in sorted(dims["segs"]):
            lo_, hi_ = max(orig_off, k * c), min(orig_off + width, (k + 1) * c)
            if lo_ < hi_:
                parts.append(gw[perm_off + lo_ - orig_off:perm_off + hi_ - orig_off])
        slabs.append(jnp.concatenate(parts, axis=0))
    return jnp.stack(slabs)


def _restore_rest(g, dims):
    hm, hr, hn, rank = dims["hm"], dims["hr"], dims["hn"], dims["rank"]
    QR, KVR, RW = dims["QR"], dims["KVR"], hr * hn
    wq = g["wq_b"].reshape(hm, QHEAD, QR)[:, :NOPE + ROPE].reshape(N_DEV, -1, QR)
    wkv = g["wkv_b"].reshape(KVR, 2, hm, NOPE).transpose(0, 2, 1, 3).reshape(KVR, 2 * hm * NOPE)
    lo = lambda t, first, half: t[first:first + rank, half * RW:(half + 1) * RW].astype(BF16)
    cols = lambda t: t.reshape(t.shape[0], N_DEV, -1).transpose(1, 0, 2)
    return dict(
        mla_wq_b=wq, mla_wkv_b=cols(wkv), rwkv_w2_f=cols(lo(g["w2cat"], ROPE, 0)),
        rwkv_w2_b=cols(lo(g["w2cat"], ROPE + rank, 1)), rwkv_a2_f=cols(lo(g["a2cat"], 0, 0)),
        rwkv_a2_b=cols(lo(g["a2cat"], rank, 1)), w_br_mla=cols(g["w_br_mla"]), w_br_rwkv=cols(g["w_br_rwkv"]),
        w_out=g["w_out"].reshape(N_DEV, -1, g["w_out"].shape[1]))


def _restore_vectors(g, dims):
    rank, RW = dims["rank"], dims["hr"] * dims["hn"]
    mu = g["mu"][0]
    out = dict(
        rwkv_mu=jnp.concatenate([mu[:3 * RW], mu[3 * RW + ROPE:3 * RW + ROPE + 4 * rank]]),
        g_pre=g["g_pre"][0], g_post=g["g_post"][0], mla_q_norm=g["mla_q_norm"][0], mla_kv_norm=g["mla_kv_norm"][0],
        rwkv_w0_f=g["w0_f"][0], rwkv_w0_b=g["w0_b"][0], rwkv_a0_f=g["a0_f"][0], rwkv_a0_b=g["a0_b"][0],
        rwkv_k_k=g["k_k"][0], rwkv_k_a=g["k_a"][0], rwkv_r_k=g["r_k"][0], rwkv_gn_g=g["gn_g"][0],
        rwkv_gn_b=g["gn_b"][0])
    return out


def _dims(inp):
    D = inp["x"].shape[-1]
    QR, KVR = inp["mla_q_norm"].shape[0], inp["mla_kv_norm"].shape[0]
    hm = inp["mla_wq_b"].shape[1] * N_DEV // (NOPE + ROPE)
    hr, hn = inp["rwkv_r_k"].shape
    rank = inp["rwkv_w2_f"].shape[0]
    MW, RW = hm * VDIM, hr * hn
    TAIL = -(-(ROPE + 4 * rank) // LANES) * LANES
    orig, o = {}, 0
    for nme, w in (("q_a", QR), ("kv_a", KVR), ("k_rope", ROPE), ("rkv", 3 * RW), ("lora", 4 * rank), ("z_m", MW),
                   ("z_r", RW), ("gate_m", D), ("gate_r", D)):
        orig[nme] = (o, w)
        o += w
    assert o == inp["w_in"].shape[1] * N_DEV
    lay, d_in_perm = _layout(D, MW, RW, TAIL, QR, KVR)
    perm_off = dict(q_a=lay["q_a"][0], kv_a=lay["kv_a"][0], k_rope=lay["tail"][0], rkv=lay["r"][0],
                    lora=lay["tail"][0] + ROPE, z_m=lay["z_m"][0], z_r=lay["z_r"][0], gate_m=lay["gate_m"][0],
                    gate_r=lay["gate_r"][0])
    segs = [(orig[nme][0], orig[nme][1], perm_off[nme]) for nme in orig]
    return dict(D=D, QR=QR, KVR=KVR, hm=hm, hr=hr, hn=hn, rank=rank, TAIL=TAIL, segs=segs, d_in=o,
                d_in_perm=d_in_perm)


def kernel(x, g_pre, w_in, mla_q_norm, mla_wq_b, mla_kv_norm, mla_wkv_b, rwkv_mu, rwkv_w0_f, rwkv_w2_f, rwkv_w0_b, rwkv_w2_b, rwkv_a0_f, rwkv_a2_f, rwkv_a0_b, rwkv_a2_b, rwkv_k_k, rwkv_k_a, rwkv_r_k, rwkv_gn_g, rwkv_gn_b, w_br_mla, w_br_rwkv, w_out, g_post, loss_target, m_g_pre, m_w_in, m_mla_q_norm, m_mla_wq_b, m_mla_kv_norm, m_mla_wkv_b, m_rwkv_mu, m_rwkv_w0_f, m_rwkv_w2_f, m_rwkv_w0_b, m_rwkv_w2_b, m_rwkv_a0_f, m_rwkv_a2_f, m_rwkv_a0_b, m_rwkv_a2_b, m_rwkv_k_k, m_rwkv_k_a, m_rwkv_r_k, m_rwkv_gn_g, m_rwkv_gn_b, m_w_br_mla, m_w_br_rwkv, m_w_out, m_g_post, v_g_pre, v_w_in, v_mla_q_norm, v_mla_wq_b, v_mla_kv_norm, v_mla_wkv_b, v_rwkv_mu, v_rwkv_w0_f, v_rwkv_w2_f, v_rwkv_w0_b, v_rwkv_w2_b, v_rwkv_a0_f, v_rwkv_a2_f, v_rwkv_a0_b, v_rwkv_a2_b, v_rwkv_k_k, v_rwkv_k_a, v_rwkv_r_k, v_rwkv_gn_g, v_rwkv_gn_b, v_w_br_mla, v_w_br_rwkv, v_w_out, v_g_post):
    inp = dict(locals())
    dims = _dims(inp)
    stored = lambda t, n: t.T if n in _TRANSPOSED else t
    assert _MATS[0] == "w_in"
    shards = [stored(inp[n], n).astype(BF16) for n in _MATS]
    core = lax.axis_index("c").astype(jnp.int32).reshape(1)
    (w_in_slabs,) = _run_exchange(_gather_plan(shards[:1]), name="gather_w_in")
    W = {"w_in_t": _prepare_w_in(w_in_slabs, dims), **_prepare_vectors({n: inp[n] for n in _VECS}, dims)}
    loss, grad_x, g, recv_rest = _local_grads(x[0], loss_target[0], W, dims, exchange=(shards[1:], core))
    loss = lax.psum(loss, ("x", "y", "c"))

    new = {}
    g_w_in = _restore_w_in(g["w_in"], dims)
    (got,) = _run_exchange(_sibling_swap_plan([g_w_in]), name="pair_swap_w_in")
    (recv_w_in,) = _run_exchange(_chip_exchange_plan([_pair_add(core, g_w_in, got, name="pair_add_w_in")]),
                                 name="scatter_w_in")
    g = _restore_vectors(g, dims)
    for n, t in zip(_MATS, [recv_w_in] + recv_rest):
        out = _adamw(t, stored(inp[n], n), stored(inp["m_" + n], n), stored(inp["v_" + n], n), name="adamw_" + n)
        new[n] = [stored(o, n) for o in out]

    vsizes = [inp[n].size for n in _VECS]
    vflat = lambda prefix, src: _pack([src[prefix + n].reshape(-1) for n in _VECS], F32, LANES * 8)
    (vrecv,) = _exchange([vflat("", g)], name="gather_vector_grads")
    vout = _adamw(vrecv, vflat("", inp), vflat("m_", inp), vflat("v_", inp), name="adamw_vectors")
    vparts = [_unpack(t, vsizes, LANES * 8) for t in vout]
    for i, n in enumerate(_VECS):
        new[n] = [vp[i].reshape(inp[n].shape) for vp in vparts]

    outs = [loss, grad_x[None]]
    for k in range(4):
        outs += [new[n][k] for n in _WEIGHTS]
    return tuple(outs)
```

```python
import functools
import math

import jax
import jax.numpy as jnp
from jax import lax
from jax.experimental import pallas as pl
from jax.experimental.pallas import tpu as pltpu

F32 = jnp.float32
BF16 = jnp.bfloat16

N_DEV = 8
LANES = 128
BF16_ROWS = 16
NOPE, ROPE, VDIM = 128, 64, 128
QHEAD = 256
ROPE_THETA = 10000.0
NORM_EPS = 1e-6
GN_EPS = 64e-5
CHUNK = 64
SUB = 16
VMEM_LIMIT = 56 * 1024 * 1024

ADAM_LR, ADAM_B1, ADAM_B2, ADAM_EPS, ADAM_WD, ADAM_STEP = 0.001, 0.9, 0.999, 1e-08, 0.01, 10


def _cparams(sem):
    return pltpu.CompilerParams(dimension_semantics=sem, vmem_limit_bytes=VMEM_LIMIT)


def _pick(n, cap):
    if n <= cap:
        return n
    for t in range(cap - cap % LANES, 0, -LANES):
        if n % t == 0:
            return t
    raise ValueError(f"no tile for {n} under {cap}")


def _mm(a, b, *, ta=False, tb=False, out_dtype=F32, name, tm_cap=1024, tn_cap=512, tk_cap=2048, ride=None):
    K, M = a.shape if ta else a.shape[::-1]
    N = b.shape[0] if tb else b.shape[1]
    assert (b.shape[1] if tb else b.shape[0]) == K, (a.shape, b.shape, ta, tb)
    tm, tn, tk = _pick(M, tm_cap), _pick(N, tn_cap), _pick(K, tk_cap)
    nj, nk = N // tn, K // tk
    steps = (M // tm) * nj * nk
    dn = (((0 if ta else 1,), (1 if tb else 0,)), ((), ()))
    srcs, extra_shapes, sem_shapes, phases = ride if ride else ((), (), (), None)
    n_src, n_extra = len(srcs), len(extra_shapes)

    def body(*refs):
        a_ref, b_ref, o_ref = refs[0], refs[1], refs[2 + n_src]
        acc_ref = refs[3 + n_src + n_extra]
        k = pl.program_id(2)
        if ride:
            step = (pl.program_id(0) * nj + pl.program_id(1)) * nk + k
            first, middle, last = phases(refs[2:2 + n_src], refs[3 + n_src:3 + n_src + n_extra],
                                         refs[4 + n_src + n_extra:])
            pl.when(step == 0)(first)
            pl.when(step == (steps * 7) // 8)(middle)
        p = lax.dot_general(a_ref[...], b_ref[...], dn, preferred_element_type=F32)

        @pl.when(k == 0)
        def _():
            acc_ref[...] = p

        @pl.when(k > 0)
        def _():
            acc_ref[...] += p

        @pl.when(k == nk - 1)
        def _():
            o_ref[...] = acc_ref[...].astype(out_dtype)

        if ride:
            pl.when(step == steps - 1)(last)

    a_spec = pl.BlockSpec((tk, tm), lambda i, j, k: (k, i)) if ta else pl.BlockSpec((tm, tk), lambda i, j, k: (i, k))
    b_spec = pl.BlockSpec((tn, tk), lambda i, j, k: (j, k)) if tb else pl.BlockSpec((tk, tn), lambda i, j, k: (k, j))
    hbm = pl.BlockSpec(memory_space=pl.ANY)
    out = pl.pallas_call(
        body, name=name, grid=(M // tm, nj, nk),
        in_specs=[a_spec, b_spec] + [hbm] * n_src,
        out_specs=[pl.BlockSpec((tm, tn), lambda i, j, k: (i, j))] + [hbm] * n_extra,
        out_shape=[jax.ShapeDtypeStruct((M, N), out_dtype)] + list(extra_shapes),
        scratch_shapes=[pltpu.VMEM((tm, tn), F32)] + [pltpu.SemaphoreType.DMA(s) for s in sem_shapes],
        compiler_params=_cparams(("arbitrary",) * 3 if ride else ("parallel", "parallel", "arbitrary")),
    )(a, b, *srcs)
    return out if ride else out[0]


def _view(arr, off, width):
    assert off % width == 0, (off, width)
    return (arr, off // width, width)


def _rowwise(fn, rows, params, out_rows, out_accs=(), *, tile, name):
    rows = [r if isinstance(r, tuple) else (r, 0, r.shape[1]) for r in rows]
    S = rows[0][0].shape[0]
    T = min(tile, S)
    assert S % T == 0
    n_rows, n_par, n_out = len(rows), len(params), len(out_rows)
    into = [o[2] if len(o) == 3 else None for o in out_rows]
    carried = [t[0] for t in into if t is not None and t[0] is not None]

    def body(*refs):
        ins = [r[...] for r in refs[:n_rows + n_par]]
        outs = fn(*ins)
        out_refs = refs[n_rows + n_par + len(carried):]
        for o_ref, val in zip(out_refs[:n_out], outs[:n_out]):
            o_ref[...] = val.astype(o_ref.dtype)
        i = pl.program_id(0)
        for o_ref, val in zip(out_refs[n_out:], outs[n_out:]):
            @pl.when(i == 0)
            def _(o_ref=o_ref, val=val):
                o_ref[...] = val

            @pl.when(i > 0)
            def _(o_ref=o_ref, val=val):
                o_ref[...] += val

    in_specs = [pl.BlockSpec((T, w), functools.partial(lambda i, cb: (i, cb), cb=cb)) for _, cb, w in rows]
    in_specs += [pl.BlockSpec(p.shape, lambda i: (0, 0)) for p in params]
    in_specs += [pl.BlockSpec(memory_space=pl.ANY)] * len(carried)
    out_specs, out_shape, aliases = [], [], {}
    for k, (o, t) in enumerate(zip(out_rows, into)):
        w, dt = o[0], o[1]
        if t is None:
            out_specs.append(pl.BlockSpec((T, w), lambda i: (i, 0)))
            out_shape.append(jax.ShapeDtypeStruct((S, w), dt))
            continue
        buf, total, first = t
        assert first % w == 0
        out_specs.append(pl.BlockSpec((T, w), functools.partial(lambda i, cb: (i, cb), cb=first // w)))
        out_shape.append(jax.ShapeDtypeStruct((S, total), dt))
        if buf is not None:
            aliases[n_rows + n_par + len(aliases)] = k
    out_specs += [pl.BlockSpec(s, lambda i: (0, 0)) for s in out_accs]
    out_shape += [jax.ShapeDtypeStruct(s, F32) for s in out_accs]
    return pl.pallas_call(
        body, name=name, grid=(S // T,), in_specs=in_specs, out_specs=out_specs, out_shape=out_shape,
        input_output_aliases=aliases, compiler_params=_cparams(("arbitrary",)),
    )(*[r[0] for r in rows], *params, *carried)


def _mm_sel(x, sel):
    hi = x.astype(BF16)
    lo = (x - hi.astype(F32)).astype(BF16)
    d = lambda u: jnp.dot(u, sel, preferred_element_type=F32)
    return d(hi) + d(lo)


@jax.custom_vjp
def _sel(x, sel, sel_t):
    return _mm_sel(x, sel)


def _sel_fwd(x, sel, sel_t):
    return _mm_sel(x, sel), (sel, sel_t)


def _sel_bwd(res, ct):
    sel, sel_t = res
    return _mm_sel(ct, sel_t), jnp.zeros_like(sel), jnp.zeros_like(sel_t)


_sel.defvjp(_sel_fwd, _sel_bwd)


def _rms(x, g):
    return x * lax.rsqrt(jnp.mean(x * x, axis=-1, keepdims=True) + NORM_EPS) * g


def _sigmoid(x):
    return 1.0 / (1.0 + jnp.exp(-x))


def _silu(x):
    return x * _sigmoid(x)


def _softplus(x):
    return jnp.maximum(x, 0.0) + jnp.log(1.0 + jnp.exp(-jnp.abs(x)))


def _f_mla_norm(q_a, kv_a, qg, kvg):
    return _rms(q_a, qg), _rms(kv_a, kvg)


def _f_rope(hm, qraw, kr_in, cosx, sinx, rot, rot_t):
    def rope(t):
        return t * cosx + _sel(t, rot, rot_t) * sinx
    parts = []
    for h in range(hm):
        parts.append(qraw[:, h * QHEAD:h * QHEAD + NOPE])
        parts.append(rope(qraw[:, h * QHEAD + NOPE:(h + 1) * QHEAD]))
    return jnp.concatenate(parts, axis=1), rope(kr_in)


def _f_rwkv_pre(rw, k, tail, w0f, w0b, a0f, a0b, k_k, k_a, w2cat, a2cat, seg, seg_t):
    split = w2cat.shape[0]
    zw = jnp.dot(jnp.tanh(tail[:, :split]).astype(BF16), w2cat, preferred_element_type=F32)
    za = jnp.dot(tail[:, split:].astype(BF16), a2cat, preferred_element_type=F32)
    return _f_rwkv_core(rw, k, zw, za, w0f, w0b, a0f, a0b, k_k, k_a, seg, seg_t)


def _f_rwkv_core(rw, k, zw, za, w0f, w0b, a0f, a0b, k_k, k_a, seg, seg_t):
    lw_f = -jnp.exp(-_softplus(-(w0f + zw[:, :rw])) - 0.5)
    lw_b = -jnp.exp(-_softplus(-(w0b + zw[:, rw:])) - 0.5)
    a_f = _sigmoid(a0f + za[:, :rw])
    a_b = _sigmoid(a0b + za[:, rw:])
    kk = k * k_k
    nrm = jnp.sqrt(_sel(_sel(kk * kk, seg, seg_t), seg_t, seg))
    kk = kk / jnp.maximum(nrm, 1e-12)
    k_f = k * (1.0 + (a_f - 1.0) * k_a)
    k_b = k * (1.0 + (a_b - 1.0) * k_a)
    return lw_f, lw_b, k_f, k_b, -kk, kk * a_f, kk * a_b


def _f_post(hn, y_f, y_b, r, k_f, k_b, v, z_r, o_mla, z_m, gn_g, gn_b, r_k, seg, seg_t):
    segsum = lambda t: _sel(_sel(t, seg, seg_t), seg_t, seg)
    y = y_f + y_b
    mu = segsum(y) * (1.0 / hn)
    yc = y - mu
    var = segsum(yc * yc) * (1.0 / hn)
    yn = yc * lax.rsqrt(var + GN_EPS) * gn_g + gn_b
    bonus = segsum(r * (k_f + k_b) * r_k) * v
    return o_mla * _silu(z_m), (yn + bonus) * _silu(z_r)


def _f_merge(u_m, u_r, g_m, g_r):
    return _sigmoid(g_m) * u_m + _sigmoid(g_r) * u_r


_NN = ((2,), (1,))
_NT = ((2,), (2,))
_TN = ((1,), (1,))

_SCAN_PASSES = {"cum": 2, "gram": 3, "solve": 1, "apply": 1, "state": 1}


def _hdot_raw(passes, x, y, dims):
    dn = (dims, ((0,), (0,)))
    d = lambda p, q: lax.dot_general(p, q, dn, preferred_element_type=F32)
    xh = x.astype(BF16)
    yh = y.astype(BF16)
    if passes == 1:
        return d(xh, yh)
    yl = (y - yh.astype(F32)).astype(BF16)
    if passes == 2:
        return d(xh, yh) + d(xh, yl)
    xl = (x - xh.astype(F32)).astype(BF16)
    return d(xh, yh) + d(xh, yl) + d(xl, yh)


@functools.partial(jax.custom_vjp, nondiff_argnums=(2, 3))
def _hdot_p(x, y, dims, passes):
    return _hdot_raw(passes, x, y, dims)


def _hdot_fwd(x, y, dims, passes):
    return _hdot_raw(passes, x, y, dims), (x, y)


def _hdot_bwd(dims, passes, res, ct):
    x, y = res
    if dims == _NN:
        return _hdot_raw(passes, ct, y, _NT), _hdot_raw(passes, x, ct, _TN)
    if dims == _NT:
        return _hdot_raw(passes, ct, y, _NN), _hdot_raw(passes, ct, x, _TN)
    return _hdot_raw(passes, y, ct, _NT), _hdot_raw(passes, x, ct, _NN)


_hdot_p.defvjp(_hdot_fwd, _hdot_bwd)


def _hdot(x, y, dims, kind):
    return _hdot_p(x, y, dims, _SCAN_PASSES[kind])


def _tri_solve(n_mat, x, length):
    row = lax.broadcasted_iota(jnp.int32, (length, length), 0)
    col = lax.broadcasted_iota(jnp.int32, (length, length), 1)
    eye = (row == col).astype(F32)[None]
    diag_blk = ((row // SUB) == (col // SUB))[None]
    nd = jnp.where(diag_blk, n_mat, 0.0)
    no = n_mat - nd
    dinv = eye + nd
    p = nd
    for _ in range(int(math.log2(SUB)) - 1):
        p = _hdot(p, p, _NN, "solve")
        dinv = dinv + _hdot(dinv, p, _NN, "solve")
    q = _hdot(dinv, no, _NN, "solve")
    u = _hdot(dinv, x, _NN, "solve")
    levels = int(math.log2(length // SUB))
    qs = [q]
    for _ in range(levels - 1):
        qs.append(_hdot(qs[-1], qs[-1], _NN, "solve"))
    for qk in reversed(qs):
        u = u + _hdot(qk, u, _NN, "solve")
    return u


def _rwkv_chunk(rev, s0, r, lw, k, v, a, b):
    pairs, length, width = r.shape
    hn = width // 2
    row = lax.broadcasted_iota(jnp.int32, (length, length), 0)
    col = lax.broadcasted_iota(jnp.int32, (length, length), 1)
    row2 = lax.broadcasted_iota(jnp.int32, (length, 2 * length), 0)
    col2 = lax.broadcasted_iota(jnp.int32, (length, 2 * length), 1)
    col2 = jnp.where(col2 >= length, col2 - length, col2)
    if rev is None:
        half = pairs // 2
        back = lax.broadcasted_iota(jnp.int32, (pairs, length, length), 0) >= half
        idx2 = lax.broadcasted_iota(jnp.int32, (2 * pairs, length, 2 * length), 0)
        back2 = ((idx2 >= half) & (idx2 < pairs)) | (idx2 >= pairs + half)
        ahead = jnp.where(back, (col - row)[None], (row - col)[None])
        ahead2 = jnp.where(back2, (col2 - row2)[None], (row2 - col2)[None])
        incl, strict2, incl2 = ahead >= 0, ahead2 > 0, ahead2 >= 0
    else:
        incl = ((row <= col) if rev else (row >= col))[None]
        strict2 = ((row2 < col2) if rev else (row2 > col2))[None]
        incl2 = ((row2 <= col2) if rev else (row2 >= col2))[None]
    lane = lax.broadcasted_iota(jnp.int32, (1, 1, width), 2)
    first = lane < hn
    head_mask = jnp.concatenate([jnp.broadcast_to(first.astype(F32), (pairs, 1, width)),
                                 jnp.broadcast_to(1.0 - first.astype(F32), (pairs, 1, width))], axis=0)
    twice = lambda t: jnp.concatenate([t, t], axis=0)
    pick = lambda t: jnp.where(first, t[:pairs], t[pairs:])

    t_incl = jnp.broadcast_to(incl.astype(F32), (pairs, length, length))
    cum = _hdot(t_incl, lw, _NN, "cum")
    g = jnp.exp(cum)
    g_inv = jnp.exp(-cum)
    at = a * jnp.exp(cum - lw)
    rt = r * g
    bt = b * g_inv
    kt = k * g_inv
    lhs = jnp.concatenate([twice(at) * head_mask, twice(rt) * head_mask], axis=1)
    rhs = jnp.concatenate([twice(bt), twice(kt)], axis=1)
    gram = _hdot(lhs, rhs, _NT, "gram")
    top = jnp.where(strict2, gram[:, :length], 0.0)
    bot = jnp.where(incl2, gram[:, length:], 0.0)
    v2 = twice(v)
    zeros = jnp.zeros_like(v2)
    x = _hdot(at, s0, _NT, "apply") + pick(_hdot(top, jnp.concatenate([zeros, v2], axis=1), _NN, "apply"))
    u = pick(_tri_solve(top[:, :, :length], twice(x), length))
    y = _hdot(rt, s0, _NT, "apply") + pick(_hdot(bot, jnp.concatenate([twice(u), v2], axis=1), _NN, "apply"))
    g_last = jnp.exp(jnp.sum(lw, axis=1, keepdims=True))
    ri = lax.broadcasted_iota(jnp.int32, (width, width), 0)
    ci = lax.broadcasted_iota(jnp.int32, (width, width), 1)
    same_head = ((ri < hn) == (ci < hn))[None]
    upd = _hdot(u, bt, _TN, "state") + _hdot(v, kt, _TN, "state")
    s1 = (s0 + jnp.where(same_head, upd, 0.0)) * g_last
    return y, s1


def _split_pairs(x):
    return jnp.stack([x[:, p * LANES:(p + 1) * LANES] for p in range(x.shape[1] // LANES)])


def _merge_pairs(x):
    return jnp.concatenate([x[p] for p in range(x.shape[0])], axis=1)


def _scan_specs(views, rw, nc, rev):
    cidx = (lambda c: nc - 1 - c) if rev else (lambda c: c)
    seqs = [pl.BlockSpec((CHUNK, rw), functools.partial(lambda c, cb: (cidx(c), cb), cb=cb)) for _, cb, _ in views]
    plain = pl.BlockSpec((CHUNK, rw), lambda c: (cidx(c), 0))
    st = pl.BlockSpec((1, rw // LANES, LANES, LANES), lambda c: (cidx(c), 0, 0, 0))
    return seqs, plain, st


def _as_views(arrs, rw):
    return [t if isinstance(t, tuple) else (t, 0, rw) for t in arrs]


def _rwkv_scan_fwd(ops_f, ops_b, rw, *, name):
    S = _as_views(ops_f, rw)[0][0].shape[0]
    nc, pairs = S // CHUNK, rw // LANES
    in_specs, out_specs, arrays = [], [], []
    for rev, ops in ((False, ops_f), (True, ops_b)):
        views = _as_views(ops, rw)
        seqs, plain, st = _scan_specs(views, rw, nc, rev)
        in_specs += seqs
        out_specs += [plain, st]
        arrays += [t[0] for t in views]

    def both(refs_f, refs_b):
        return [jnp.concatenate([_split_pairs(f[...]), _split_pairs(b[...])], axis=0) for f, b in zip(refs_f, refs_b)]

    def body(*refs):
        (y_f, st_f, y_b, st_b), s_ref = refs[12:16], refs[16]

        @pl.when(pl.program_id(0) == 0)
        def _():
            s_ref[...] = jnp.zeros_like(s_ref)

        s0 = s_ref[...]
        st_f[0] = s0[:pairs]
        st_b[0] = s0[pairs:]
        y, s1 = _rwkv_chunk(None, s0, *both(refs[:6], refs[6:12]))
        y_f[...] = _merge_pairs(y[:pairs])
        y_b[...] = _merge_pairs(y[pairs:])
        s_ref[...] = s1

    return pl.pallas_call(
        body, name=name, grid=(nc,), in_specs=in_specs, out_specs=out_specs,
        out_shape=[jax.ShapeDtypeStruct((S, rw), F32), jax.ShapeDtypeStruct((nc, pairs, LANES, LANES), F32)] * 2,
        scratch_shapes=[pltpu.VMEM((2 * pairs, LANES, LANES), F32)],
        compiler_params=_cparams(("arbitrary",)),
    )(*arrays)


def _rwkv_scan_bwd(ops_f, ops_b, states_f, states_b, dy, rw, *, name):
    S = dy.shape[0]
    nc, pairs = S // CHUNK, rw // LANES
    in_specs, arrays = [], []
    for rev, ops, states in ((False, ops_f, states_f), (True, ops_b, states_b)):
        views = _as_views(list(ops) + [dy], rw)
        seqs, plain, st = _scan_specs(views, rw, nc, not rev)
        in_specs += seqs + [st]
        arrays += [t[0] for t in views] + [states]
    out_specs = []
    for rev in (False, True):
        out_specs += [_scan_specs([], rw, nc, not rev)[1]] * 6

    def both(refs_f, refs_b):
        return [jnp.concatenate([_split_pairs(f[...]), _split_pairs(b[...])], axis=0) for f, b in zip(refs_f, refs_b)]

    def body(*refs):
        ds_ref = refs[28]

        @pl.when(pl.program_id(0) == 0)
        def _():
            ds_ref[...] = jnp.zeros_like(ds_ref)

        s0 = jnp.concatenate([refs[7][0], refs[15][0]], axis=0)
        _, vjp = jax.vjp(functools.partial(_rwkv_chunk, None), s0, *both(refs[:6], refs[8:14]))
        (dy,) = both(refs[6:7], refs[14:15])
        grads = vjp((dy, ds_ref[...]))
        ds_ref[...] = grads[0]
        for o_f, o_b, gval in zip(refs[16:22], refs[22:28], grads[1:]):
            o_f[...] = _merge_pairs(gval[:pairs])
            o_b[...] = _merge_pairs(gval[pairs:])

    return pl.pallas_call(
        body, name=name, grid=(nc,), in_specs=in_specs, out_specs=out_specs,
        out_shape=[jax.ShapeDtypeStruct((S, rw), F32)] * 12,
        scratch_shapes=[pltpu.VMEM((2 * pairs, LANES, LANES), F32)],
        compiler_params=_cparams(("arbitrary",)),
    )(*arrays)


def _shift_lerp(x_view, mu, d=None, into=None, *, name):
    arr, off, width = x_view
    S = arr.shape[0]
    cb = _pick(width, 256)
    assert off % cb == 0

    def cshift(t):
        rows = lax.broadcasted_iota(jnp.int32, t.shape, 0)
        prev = jnp.where(rows == 0, 0.0, pltpu.roll(t, 1, 0))
        nxt = jnp.where(rows == S - 1, 0.0, pltpu.roll(t, S - 1, 0))
        return 0.5 * (prev + nxt)

    def fwd_body(x_ref, mu_ref, o_ref):
        x = x_ref[...]
        o_ref[...] = x + mu_ref[...] * (cshift(x) - x)

    def bwd_body(x_ref, mu_ref, d_ref, _, dx_ref, dmu_ref):
        x, m, dd = x_ref[...], mu_ref[...], d_ref[...]
        gm = m * dd
        dx_ref[...] = (dd - gm + cshift(gm)).astype(dx_ref.dtype)
        dmu_ref[...] = jnp.sum(dd * (cshift(x) - x), axis=0, keepdims=True)

    x_spec = pl.BlockSpec((S, cb), lambda j: (0, off // cb + j))
    blk = pl.BlockSpec((S, cb), lambda j: (0, j))
    vec = pl.BlockSpec((1, cb), lambda j: (0, j))
    if d is None:
        return pl.pallas_call(
            fwd_body, name=name, grid=(width // cb,), in_specs=[x_spec, vec], out_specs=blk,
            out_shape=jax.ShapeDtypeStruct((S, width), F32), compiler_params=_cparams(("parallel",)),
        )(arr, mu)
    buf, first = into
    assert first % cb == 0
    return pl.pallas_call(
        bwd_body, name=name, grid=(width // cb,),
        in_specs=[x_spec, vec, blk, pl.BlockSpec(memory_space=pl.ANY)],
        out_specs=[pl.BlockSpec((S, cb), lambda j: (0, first // cb + j)), vec],
        out_shape=[jax.ShapeDtypeStruct(buf.shape, buf.dtype), jax.ShapeDtypeStruct((1, width), F32)],
        input_output_aliases={3: 0}, compiler_params=_cparams(("parallel",)),
    )(arr, mu, d, buf)


def _attention_fwd(qfull, kv, kr, hm, scale, *, tq, name):
    S = qfull.shape[0]
    nt = (((1,), (1,)), ((), ()))

    def body(q_ref, kn_ref, kr_ref, v_ref, o_ref, lse_ref, k_scr):
        _head_keys(kn_ref, kr_ref, k_scr)
        s = lax.dot_general(q_ref[...], k_scr[...], nt, preferred_element_type=F32)
        m = jnp.max(s, axis=-1, keepdims=True)
        p = jnp.exp((s - m) * scale)
        l = jnp.sum(p, axis=-1, keepdims=True)
        o_ref[...] = jnp.dot(p.astype(BF16), v_ref[...], preferred_element_type=F32) * (1.0 / l)
        lse_ref[...] = jnp.broadcast_to(m * scale + jnp.log(l), lse_ref.shape)

    oblk = pl.BlockSpec((tq, VDIM), lambda h, i: (i, h))
    return pl.pallas_call(
        body, name=name, grid=(hm, S // tq),
        in_specs=[pl.BlockSpec((tq, QHEAD), lambda h, i: (i, h)),
                  pl.BlockSpec((S, NOPE), lambda h, i: (0, h)),
                  pl.BlockSpec((S, LANES), lambda h, i: (0, 0)),
                  pl.BlockSpec((S, VDIM), lambda h, i: (0, hm + h))],
        out_specs=[oblk, oblk],
        out_shape=[jax.ShapeDtypeStruct((S, hm * VDIM), F32)] * 2,
        scratch_shapes=[pltpu.VMEM((S, QHEAD), BF16)],
        compiler_params=_cparams(("parallel", "arbitrary")),
    )(qfull, kv, kr, kv)


def _head_keys(kn_ref, kr_ref, k_scr):
    @pl.when(pl.program_id(1) == 0)
    def _():
        k_scr[:, :NOPE] = kn_ref[...]
        k_scr[:, NOPE:] = kr_ref[...]


def _attention_bwd(qfull, kv, kr, o, lse, d_o, hm, scale, *, tq, name):
    S = qfull.shape[0]
    tq = min(tq, S)
    nq = S // tq
    tn = (((0,), (0,)), ((), ()))
    nt = (((1,), (1,)), ((), ()))

    def body(q_ref, kn_ref, kr_ref, v_ref, o_ref, lse_ref, do_ref, dq_ref, dk_ref, dv_ref, k_scr):
        _head_keys(kn_ref, kr_ref, k_scr)
        s = lax.dot_general(q_ref[...], k_scr[...], nt, preferred_element_type=F32)
        p = jnp.exp(s * scale - lse_ref[:, 0:1])
        d_out = do_ref[...]
        delta = jnp.sum(d_out * o_ref[...], axis=-1, keepdims=True)
        d_out = d_out.astype(BF16)
        dp = lax.dot_general(d_out, v_ref[...], nt, preferred_element_type=F32)
        ds = (p * (dp - delta)).astype(BF16)
        dq_ref[...] = jnp.dot(ds, k_scr[...], preferred_element_type=F32) * scale
        dv = lax.dot_general(p.astype(BF16), d_out, tn, preferred_element_type=F32)
        dk = lax.dot_general(ds, q_ref[...], tn, preferred_element_type=F32)
        i = pl.program_id(1)
        for ref, val in ((dk_ref, dk), (dv_ref, dv)):
            @pl.when(i == 0)
            def _(ref=ref, val=val):
                ref[...] = val

            @pl.when(i > 0)
            def _(ref=ref, val=val):
                ref[...] += val

        @pl.when(i == nq - 1)
        def _():
            dk_ref[...] = dk_ref[...] * scale

    qblk = pl.BlockSpec((tq, QHEAD), lambda h, i: (i, h))
    oblk = pl.BlockSpec((tq, VDIM), lambda h, i: (i, h))
    return pl.pallas_call(
        body, name=name, grid=(hm, nq),
        in_specs=[qblk,
                  pl.BlockSpec((S, NOPE), lambda h, i: (0, h)),
                  pl.BlockSpec((S, LANES), lambda h, i: (0, 0)),
                  pl.BlockSpec((S, VDIM), lambda h, i: (0, hm + h)),
                  oblk, oblk, oblk],
        out_specs=[qblk, pl.BlockSpec((S, QHEAD), lambda h, i: (0, h)), pl.BlockSpec((S, VDIM), lambda h, i: (0, h))],
        out_shape=[jax.ShapeDtypeStruct((S, hm * QHEAD), F32), jax.ShapeDtypeStruct((S, hm * QHEAD), F32),
                   jax.ShapeDtypeStruct((S, hm * VDIM), F32)],
        scratch_shapes=[pltpu.VMEM((S, QHEAD), BF16)],
        compiler_params=_cparams(("parallel", "arbitrary")),
    )(qfull, kv, kr, kv, o, lse, d_o)


def _layout(D, MW, RW, TAIL, QR, KVR):
    names = ["gate_m", "gate_r", "z_m", "z_r", "q_a", "kv_a", "r", "k", "v", "tail"]
    widths = [D, D, MW, RW, QR, KVR, RW, RW, RW, TAIL]
    offs, o = {}, 0
    for nme, w in zip(names, widths):
        assert o % w == 0, (nme, o, w)
        offs[nme] = (o, w)
        o += w
    return offs, o


def _local_grads(x, target, W, dims, exchange=None):
    S, D = x.shape
    hm, hr, hn, rank = dims["hm"], dims["hr"], dims["hn"], dims["rank"]
    MW, RW = hm * VDIM, hr * hn
    TAIL = dims["TAIL"]
    QR, KVR = W["mla_q_norm"].shape[1], W["mla_kv_norm"].shape[1]
    lay, d_in = _layout(D, MW, RW, TAIL, QR, KVR)
    T = 256
    scale = (NOPE + ROPE) ** -0.5
    col = lambda arr, nme: _view(arr, *lay[nme])

    pos = jnp.arange(S, dtype=F32)
    inv_freq = jnp.power(ROPE_THETA, -jnp.arange(0, ROPE, 2, dtype=F32) / ROPE)
    ang = pos[:, None] * inv_freq[None, :]
    zpad = jnp.zeros((S, LANES - ROPE), F32)
    cosx = jnp.concatenate([jnp.cos(ang), jnp.cos(ang), zpad], axis=1)
    sinx = jnp.concatenate([jnp.sin(ang), jnp.sin(ang), zpad], axis=1)
    ri, ci = jnp.arange(LANES)[:, None], jnp.arange(LANES)[None, :]
    half = ROPE // 2
    rot = (jnp.where((ri == ci - half) & (ci >= half) & (ci < ROPE), 1.0, 0.0)
           - jnp.where((ri == ci + half) & (ci < half), 1.0, 0.0)).astype(BF16)
    rot_t = rot.T
    seg = (jnp.arange(RW)[:, None] // hn == jnp.arange(LANES)[None, :]).astype(BF16)
    seg_t = seg.T

    (h,) = _rowwise(lambda xb, g: (_rms(xb, g),), [x], [W["g_pre"]], [(D, BF16)], tile=T, name="pre_norm")
    if exchange is None:
        proj = _mm(h, W["w_in_t"], tb=True, name="in_proj")
    else:
        proj, *slabs = _mm(h, W["w_in_t"], tb=True, ride=_gather_plan(exchange[0]), name="in_proj")
        W = {**W, **_prepare_rest(dict(zip(_MATS[1:], slabs)), dims)}

    qn, kvn = _rowwise(_f_mla_norm, [col(proj, "q_a"), col(proj, "kv_a")], [W["mla_q_norm"], W["mla_kv_norm"]],
                       [(QR, BF16), (KVR, BF16)], tile=T, name="mla_norm")
    qraw = _mm(qn, W["wq_b_t"], tb=True, name="q_up")
    kv = _mm(kvn, W["wkv_b"], out_dtype=BF16, name="kv_up")
    kr_view = _view(proj, lay["tail"][0], LANES)
    qfull, kr = _rowwise(functools.partial(_f_rope, hm), [qraw, kr_view, cosx, sinx], [rot, rot_t],
                         [(hm * QHEAD, BF16), (LANES, BF16)], tile=T, name="rope")
    o_mla, lse = _attention_fwd(qfull, kv, kr, hm, scale, tq=T, name="attn_fwd")

    shift_view = (proj, lay["r"][0], 3 * RW + TAIL)
    rl = _shift_lerp(shift_view, W["mu"], name="shift_fwd")
    rl_r, rl_k, rl_v = _view(rl, 0, RW), _view(rl, RW, RW), _view(rl, 2 * RW, RW)
    rl_tail = _view(rl, 3 * RW, TAIL)
    pre_params = [W["w0_f"], W["w0_b"], W["a0_f"], W["a0_b"], W["k_k"], W["k_a"], W["w2cat"], W["a2cat"], seg, seg_t]
    pre_fn = functools.partial(_f_rwkv_pre, RW)
    lw_f, lw_b, k_f, k_b, a_n, b_f, b_b = _rowwise(pre_fn, [rl_k, rl_tail], pre_params, [(RW, F32)] * 7, tile=T,
                                                    name="rwkv_pre")
    ops_f = (rl_r, lw_f, k_f, rl_v, a_n, b_f)
    ops_b = (rl_r, lw_b, k_b, rl_v, a_n, b_b)
    y_f, st_f, y_b, st_b = _rwkv_scan_fwd(ops_f, ops_b, RW, name="scan_fwd")

    post_fn = functools.partial(_f_post, hn)
    post_rows = [y_f, y_b, rl_r, k_f, k_b, rl_v, col(proj, "z_r"), o_mla, col(proj, "z_m")]
    post_params = [W["gn_g"], W["gn_b"], W["r_k"], seg, seg_t]
    ymg, yrg = _rowwise(post_fn, post_rows, post_params, [(MW, BF16), (RW, BF16)], tile=T, name="post")
    u_m = _mm(ymg, W["w_br_mla"], name="br_mla")
    u_r = _mm(yrg, W["w_br_rwkv"], name="br_rwkv")
    merge_rows = [u_m, u_r, col(proj, "gate_m"), col(proj, "gate_r")]
    (merged,) = _rowwise(lambda *t: (_f_merge(*t),), merge_rows, [], [(D, BF16)], tile=T, name="merge")
    out = _mm(merged, W["w_out"], name="out_proj")

    def head(ob, xb, tb, g):
        yn, vjp = jax.vjp(_rms, ob, g)
        err = xb + yn - tb
        dy = err * (1.0 / D)
        d_ob, d_g = vjp(dy)
        loss = jnp.broadcast_to(0.5 * jnp.sum(err * err) * (1.0 / D), (1, LANES))
        return dy, d_ob, loss, d_g

    dy, d_out, loss, g_g_post = _rowwise(head, [out, x, target], [W["g_post"]], [(D, F32), (D, BF16)],
                                         [(1, LANES), (1, D)], tile=T, name="head")
    d_merged = _mm(d_out, W["w_out"], tb=True, name="d_merged")
    g_w_out = _mm(merged, d_out, ta=True, out_dtype=BF16, name="g_w_out")

    def merge_bwd(u_m_b, u_r_b, g_m_b, g_r_b, dm):
        _, vjp = jax.vjp(_f_merge, u_m_b, u_r_b, g_m_b, g_r_b)
        du_m, du_r, dg_m, dg_r = vjp(dm)
        return du_m, du_r, jnp.concatenate([dg_m, dg_r], axis=1)

    d_u_m, d_u_r, d_proj = _rowwise(merge_bwd, merge_rows + [d_merged], [],
                                    [(D, BF16), (D, BF16), (2 * D, BF16, (None, d_in, lay["gate_m"][0]))], tile=T,
                                    name="merge_bwd")
    d_ymg = _mm(d_u_m, W["w_br_mla"], tb=True, name="d_ymg")
    d_yrg = _mm(d_u_r, W["w_br_rwkv"], tb=True, name="d_yrg")
    g_w_br_mla = _mm(ymg, d_u_m, ta=True, out_dtype=BF16, name="g_w_br_mla")
    g_w_br_rwkv = _mm(yrg, d_u_r, ta=True, out_dtype=BF16, name="g_w_br_rwkv")

    def post_bwd(*args):
        nr = len(post_rows)
        prim, dm, dr = args[:nr] + args[nr + 2:], args[nr], args[nr + 1]
        _, vjp = jax.vjp(post_fn, *prim)
        g = vjp((dm, dr))
        return g[0], g[2], g[3], g[5], g[7], jnp.concatenate([g[8], g[6]], axis=1), g[9], g[10], g[11]

    (d_y, d_r_bonus, d_k_bonus, d_v_bonus, d_o, d_proj, g_gn_g, g_gn_b, g_r_k) = _rowwise(
        post_bwd, post_rows + [d_ymg, d_yrg], post_params,
        [(RW, F32), (RW, F32), (RW, F32), (RW, F32), (MW, F32), (MW + RW, BF16, (d_proj, d_in, lay["z_m"][0]))],
        [(1, RW)] * 3, tile=T // 2, name="post_bwd")

    dscan = _rwkv_scan_bwd(ops_f, ops_b, st_f, st_b, d_y, RW, name="scan_bwd")
    dsc = {"f": dscan[:6], "b": dscan[6:]}

    d_q_att, d_k_att, d_v_att = _attention_bwd(qfull, kv, kr, o_mla, lse, d_o, hm, scale, tq=2 * T, name="attn_bwd")

    def rope_bwd(qraw_b, kr_in, cos_b, sin_b, dq_b, dk_b, dv_b, rot_b, rot_t_b):
        _, vjp = jax.vjp(lambda q_, k_: _f_rope(hm, q_, k_, cos_b, sin_b, rot_b, rot_t_b), qraw_b, kr_in)
        dkn = jnp.concatenate([dk_b[:, hh * QHEAD:hh * QHEAD + NOPE] for hh in range(hm)], axis=1)
        dkr = dk_b[:, NOPE:QHEAD]
        for hh in range(1, hm):
            dkr = dkr + dk_b[:, hh * QHEAD + NOPE:(hh + 1) * QHEAD]
        d_qraw, d_kr_in = vjp((dq_b, dkr))
        return d_qraw, jnp.concatenate([dkn, dv_b], axis=1), d_kr_in

    d_qraw, d_kv, d_kr_in = _rowwise(rope_bwd, [qraw, kr_view, cosx, sinx, d_q_att, d_k_att, d_v_att],
                                     [rot, rot_t], [(hm * QHEAD, BF16), (2 * MW, BF16), (LANES, F32)], tile=T,
                                     name="rope_bwd")
    d_qnorm = _mm(d_qraw, W["wq_b_t"], name="d_qn")
    d_kvnorm = _mm(d_kv, W["wkv_b"], tb=True, name="d_kvn")
    g_wq_b = _mm(d_qraw, qn, ta=True, out_dtype=BF16, name="g_wq_b")
    g_wkv_b = _mm(kvn, d_kv, ta=True, out_dtype=BF16, name="g_wkv_b")

    def mla_norm_bwd(q_a, kv_a, qg, kvg, dq, dk):
        _, vjp = jax.vjp(_f_mla_norm, q_a, kv_a, qg, kvg)
        d_q_a, d_kv_a, d_qg, d_kvg = vjp((dq, dk))
        return jnp.concatenate([d_q_a, d_kv_a], axis=1), d_qg, d_kvg

    d_proj, g_q_norm, g_kv_norm = _rowwise(
        lambda q_a, kv_a, dq, dk, qg, kvg: mla_norm_bwd(q_a, kv_a, qg, kvg, dq, dk),
        [col(proj, "q_a"), col(proj, "kv_a"), d_qnorm, d_kvnorm], [W["mla_q_norm"], W["mla_kv_norm"]],
        [(QR + KVR, BF16, (d_proj, d_in, lay["q_a"][0]))], [(1, QR), (1, KVR)], tile=T, name="mla_norm_bwd")

    def pre_bwd(k_b_, tail_b, dlwf, dlwb, dkf, dkb, dkbon, daf, dab, dbf, dbb, drf, drb, drbon, dvf, dvb, dvbon,
                dkr, *params):
        w2, a2 = params[6], params[7]
        nt, tn = (((1,), (1,)), ((), ())), (((0,), (0,)), ((), ()))
        split = w2.shape[0]
        th = jnp.tanh(tail_b[:, :split])
        th_b, tail_h = th.astype(BF16), tail_b[:, split:].astype(BF16)
        zw = jnp.dot(th_b, w2, preferred_element_type=F32)
        za = jnp.dot(tail_h, a2, preferred_element_type=F32)
        _, vjp = jax.vjp(functools.partial(_f_rwkv_core, RW), k_b_, zw, za, *params[:6], params[8], params[9])
        g = vjp((dlwf, dlwb, dkf + dkbon, dkb + dkbon, daf + dab, dbf, dbb))
        d_zw, d_za = g[1].astype(BF16), g[2].astype(BF16)
        d_tail = (jnp.concatenate([lax.dot_general(d_zw, w2, nt, preferred_element_type=F32) * (1.0 - th * th),
                                   lax.dot_general(d_za, a2, nt, preferred_element_type=F32)], axis=1)
                  + jnp.concatenate([dkr, jnp.zeros((dkr.shape[0], TAIL - LANES), F32)], axis=1))
        g_w2 = lax.dot_general(th_b, d_zw, tn, preferred_element_type=F32)
        g_a2 = lax.dot_general(tail_h, d_za, tn, preferred_element_type=F32)
        d_rl = jnp.concatenate([drf + drb + drbon, g[0], dvf + dvb + dvbon, d_tail], axis=1)
        return (d_rl,) + tuple(g[3:9]) + (g_w2, g_a2)

    f_, b_ = dsc["f"], dsc["b"]
    pre_bwd_rows = [rl_k, rl_tail, f_[1], b_[1], f_[2], b_[2], d_k_bonus, f_[4], b_[4], f_[5], b_[5],
                    f_[0], b_[0], d_r_bonus, f_[3], b_[3], d_v_bonus, d_kr_in]
    (d_rl, g_w0_f, g_w0_b, g_a0_f, g_a0_b, g_k_k, g_k_a, g_w2cat, g_a2cat) = _rowwise(
        pre_bwd, pre_bwd_rows, pre_params, [(3 * RW + TAIL, F32)],
        [(1, RW)] * 6 + [W["w2cat"].shape, W["a2cat"].shape], tile=T // 2, name="rwkv_pre_bwd")
    d_proj, g_mu = _shift_lerp(shift_view, W["mu"], d_rl, (d_proj, lay["r"][0]), name="shift_bwd")
    small = dict(wq_b=g_wq_b, wkv_b=g_wkv_b, w2cat=g_w2cat, a2cat=g_a2cat, w_br_mla=g_w_br_mla,
                 w_br_rwkv=g_w_br_rwkv, w_out=g_w_out)
    if exchange is None:
        received = None
        g_w_in = _mm(d_proj, h, ta=True, out_dtype=BF16, tn_cap=1024, name="g_w_in")
        d_h = _mm(d_proj, W["w_in_t"], tn_cap=1024, name="d_h")
    else:
        slabs = _restore_rest(small, dims)
        slabs = [slabs[n] for n in _MATS[1:]]
        g_w_in, *got = _mm(d_proj, h, ta=True, out_dtype=BF16, tn_cap=1024, ride=_sibling_swap_plan(slabs),
                           name="g_w_in")
        sums = [_pair_add(exchange[1], s, t, name="pair_add_" + n) for n, s, t in zip(_MATS[1:], slabs, got)]
        g_w_in = _restore_w_in(g_w_in, dims)
        d_h, *received = _mm(d_proj, W["w_in_t"], tn_cap=1024, name="d_h",
                             ride=_join_plans(_chip_exchange_plan(sums), _sibling_swap_plan([g_w_in])))
        small = {}

    def pre_norm_bwd(xb, dyb, dhb, g):
        _, vjp = jax.vjp(_rms, xb, g)
        dx, dg = vjp(dhb)
        return dyb + dx, dg

    grad_x, g_g_pre = _rowwise(pre_norm_bwd, [x, dy, d_h], [W["g_pre"]], [(D, F32)], [(1, D)], tile=T,
                               name="pre_norm_bwd")

    grads = dict(g_pre=g_g_pre, w_in=g_w_in, mla_q_norm=g_q_norm, mla_kv_norm=g_kv_norm, mu=g_mu, w0_f=g_w0_f,
                 w0_b=g_w0_b, a0_f=g_a0_f, a0_b=g_a0_b, k_k=g_k_k, k_a=g_k_a, r_k=g_r_k, gn_g=g_gn_g, gn_b=g_gn_b,
                 g_post=g_g_post, **small)
    return loss[0, 0], grad_x, grads, received


_MATS = ["w_in", "mla_wq_b", "mla_wkv_b", "rwkv_w2_f", "rwkv_w2_b", "rwkv_a2_f", "rwkv_a2_b", "w_br_mla",
         "w_br_rwkv", "w_out"]
_ROW_SHARDED = ("w_out",)
_TRANSPOSED = ("w_in", "mla_wq_b")
_VECS = ["g_pre", "mla_q_norm", "mla_kv_norm", "rwkv_mu", "rwkv_w0_f", "rwkv_w0_b", "rwkv_a0_f", "rwkv_a0_b",
         "rwkv_k_k", "rwkv_k_a", "rwkv_r_k", "rwkv_gn_g", "rwkv_gn_b", "g_post"]
_WEIGHTS = ["g_pre", "w_in", "mla_q_norm", "mla_wq_b", "mla_kv_norm", "mla_wkv_b", "rwkv_mu", "rwkv_w0_f",
            "rwkv_w2_f", "rwkv_w0_b", "rwkv_w2_b", "rwkv_a0_f", "rwkv_a2_f", "rwkv_a0_b", "rwkv_a2_b", "rwkv_k_k",
            "rwkv_k_a", "rwkv_r_k", "rwkv_gn_g", "rwkv_gn_b", "w_br_mla", "w_br_rwkv", "w_out", "g_post"]

def _exchange(srcs, *, name):
    n = len(srcs)

    def body(*refs):
        src_refs, out_refs = refs[:n], refs[n:2 * n]
        send_sems, recv_sems, local_sems = refs[2 * n:]
        x, y, c = lax.axis_index("x"), lax.axis_index("y"), lax.axis_index("c")
        me = 4 * x + 2 * y + c
        flip = lambda v, bit: (1 - v) if bit else v

        def piece(a, idx):
            return src_refs[a] if srcs[a].ndim == 2 else src_refs[a].at[idx]

        owns = [pltpu.make_async_copy(piece(a, me), out_refs[a].at[me], local_sems.at[a]) for a in range(n)]
        for cp in owns:
            cp.start()
        sends, peers = [], []
        for d in range(1, N_DEV):
            px, py, pc = flip(x, d & 4), flip(y, d & 2), flip(c, d & 1)
            pidx = 4 * px + 2 * py + pc
            peers.append(((px, py, pc), pidx))
            for a in range(n):
                cp = pltpu.make_async_remote_copy(
                    src_ref=piece(a, pidx), dst_ref=out_refs[a].at[me], send_sem=send_sems.at[d - 1, a],
                    recv_sem=recv_sems.at[d - 1, a], device_id=(px, py, pc), device_id_type=pl.DeviceIdType.MESH)
                cp.start()
                sends.append(cp)
        for d, (peer, pidx) in zip(range(1, N_DEV), peers):
            for a in range(n):
                pltpu.make_async_remote_copy(
                    src_ref=piece(a, pidx), dst_ref=out_refs[a].at[pidx], send_sem=send_sems.at[d - 1, a],
                    recv_sem=recv_sems.at[d - 1, a], device_id=peer, device_id_type=pl.DeviceIdType.MESH).wait_recv()
        for cp in sends:
            cp.wait_send()
        for cp in owns:
            cp.wait()

    return pl.pallas_call(
        body, name=name,
        out_shape=[jax.ShapeDtypeStruct((N_DEV,) + s.shape[-2:], s.dtype) for s in srcs],
        in_specs=[pl.BlockSpec(memory_space=pl.ANY)] * n, out_specs=[pl.BlockSpec(memory_space=pl.ANY)] * n,
        scratch_shapes=[pltpu.SemaphoreType.DMA((N_DEV - 1, n)), pltpu.SemaphoreType.DMA((N_DEV - 1, n)),
                        pltpu.SemaphoreType.DMA((n,))],
    )(*srcs)


def _remote(src, dst, sems, key, to):
    send_sems, recv_sems = sems
    return pltpu.make_async_remote_copy(src_ref=src, dst_ref=dst, send_sem=send_sems.at[key], recv_sem=recv_sems.at[key],
                                        device_id=to, device_id_type=pl.DeviceIdType.MESH)


def _run_exchange(plan, *, name):
    srcs, out_shapes, sem_shapes, phases = plan
    n, m = len(srcs), len(out_shapes)

    def body(*refs):
        for phase in phases(refs[:n], refs[n:n + m], refs[n + m:]):
            phase()

    return pl.pallas_call(
        body, name=name, out_shape=out_shapes,
        in_specs=[pl.BlockSpec(memory_space=pl.ANY)] * n, out_specs=[pl.BlockSpec(memory_space=pl.ANY)] * m,
        scratch_shapes=[pltpu.SemaphoreType.DMA(s) for s in sem_shapes],
    )(*srcs)


def _join_plans(p, q):
    (srcs_p, outs_p, sems_p, phases_p), (srcs_q, outs_q, sems_q, phases_q) = p, q

    def phases(src_refs, out_refs, sem_refs):
        a = phases_p(src_refs[:len(srcs_p)], out_refs[:len(outs_p)], sem_refs[:len(sems_p)])
        b = phases_q(src_refs[len(srcs_p):], out_refs[len(outs_p):], sem_refs[len(sems_p):])

        def both(fa, fb):
            def run():
                fa()
                fb()
            return run

        return tuple(both(fa, fb) for fa, fb in zip(a, b))

    return list(srcs_p) + list(srcs_q), list(outs_p) + list(outs_q), list(sems_p) + list(sems_q), phases


def _gather_plan(srcs):
    n = len(srcs)

    def phases(src_refs, out_refs, sem_refs):
        sems, local_sems = sem_refs[:2], sem_refs[2]
        x, y, c = lax.axis_index("x"), lax.axis_index("y"), lax.axis_index("c")
        idx = lambda px, py, pc: 4 * px + 2 * py + pc
        me, sibling = (x, y, c), (x, y, 1 - c)
        chips = [(1 - x, y), (x, 1 - y), (1 - x, 1 - y)]
        own = lambda a: pltpu.make_async_copy(src_refs[a], out_refs[a].at[idx(*me)], local_sems.at[a])
        to_sibling = lambda a: _remote(src_refs[a], out_refs[a].at[idx(*me)], sems, (0, a), sibling)
        to_chip = lambda a, j: _remote(src_refs[a], out_refs[a].at[idx(*me)], sems, (1 + j, a), (*chips[j], c))
        landed = lambda a, j: out_refs[a].at[idx(*chips[j], c)]
        passed_on = lambda a, j: _remote(landed(a, j), landed(a, j), sems, (4 + j, a), sibling)

        def first():
            for a in range(n):
                own(a).start()
                to_sibling(a).start()
                for j in range(3):
                    to_chip(a, j).start()

        def middle():
            for j in range(3):
                for a in range(n):
                    _remote(landed(a, j), landed(a, j), sems, (1 + j, a), me).wait_recv()
                    passed_on(a, j).start()

        def last():
            for a in range(n):
                blk = out_refs[a].at[idx(*sibling)]
                _remote(blk, blk, sems, (0, a), me).wait_recv()
                for j in range(3):
                    blk = out_refs[a].at[idx(*chips[j], 1 - c)]
                    _remote(blk, blk, sems, (4 + j, a), me).wait_recv()
            for a in range(n):
                to_sibling(a).wait_send()
                for j in range(3):
                    to_chip(a, j).wait_send()
                    passed_on(a, j).wait_send()
                own(a).wait()

        return first, middle, last

    return srcs, [jax.ShapeDtypeStruct((N_DEV,) + s.shape, s.dtype) for s in srcs], [(7, n), (7, n), (n,)], phases


def _sibling_swap_plan(srcs):
    n = len(srcs)

    def phases(src_refs, out_refs, sems):
        x, y, c = lax.axis_index("x"), lax.axis_index("y"), lax.axis_index("c")
        copies = lambda: [_remote(src_refs[a].at[2 * q + 1 - c], out_refs[a].at[q], sems, (q, a), (x, y, 1 - c))
                          for a in range(n) for q in range(4)]

        def first():
            for cp in copies():
                cp.start()

        def last():
            for cp in copies():
                cp.wait()

        return first, (lambda: None), last

    return srcs, [jax.ShapeDtypeStruct((4,) + s.shape[1:], s.dtype) for s in srcs], [(4, n), (4, n)], phases


def _chip_exchange_plan(srcs):
    n = len(srcs)

    def phases(src_refs, out_refs, sem_refs):
        sems, local_sems = sem_refs[:2], sem_refs[2]
        x, y, c = lax.axis_index("x"), lax.axis_index("y"), lax.axis_index("c")
        mine = 2 * x + y
        chips = [(1 - x, y), (x, 1 - y), (1 - x, 1 - y)]
        own = lambda a: pltpu.make_async_copy(src_refs[a].at[mine], out_refs[a].at[mine], local_sems.at[a])
        send = lambda a, j: _remote(src_refs[a].at[2 * chips[j][0] + chips[j][1]], out_refs[a].at[mine], sems, (j, a),
                                    (*chips[j], c))

        def first():
            for a in range(n):
                own(a).start()
                for j in range(3):
                    send(a, j).start()

        def last():
            for j in range(3):
                for a in range(n):
                    blk = out_refs[a].at[2 * chips[j][0] + chips[j][1]]
                    _remote(blk, blk, sems, (j, a), (x, y, c)).wait_recv()
            for a in range(n):
                for j in range(3):
                    send(a, j).wait_send()
                own(a).wait()

        return first, (lambda: None), last

    return srcs, [jax.ShapeDtypeStruct(s.shape, s.dtype) for s in srcs], [(3, n), (3, n), (n,)], phases


def _pair_add(core, g, got, *, name):
    q, r, c = got.shape
    tr, tc = _tile2d(r, c)

    def body(core_ref, a_ref, b_ref, o_ref):
        o_ref[...] = (a_ref[...].astype(F32) + b_ref[...].astype(F32)).astype(BF16)

    blk = pl.BlockSpec((1, tr, tc), lambda i, j, k, core_ref: (i, j, k))
    mine = pl.BlockSpec((1, tr, tc), lambda i, j, k, core_ref: (2 * i + core_ref[0], j, k))
    return pl.pallas_call(
        body, name=name, out_shape=jax.ShapeDtypeStruct(got.shape, BF16),
        grid_spec=pltpu.PrefetchScalarGridSpec(num_scalar_prefetch=1, grid=(q, r // tr, c // tc),
                                               in_specs=[mine, blk], out_specs=blk),
        compiler_params=_cparams(("parallel", "parallel", "parallel")))(core, g, got)


def _adamw(recv, w, m, v, *, name):
    r, c = w.shape
    n_terms = recv.shape[0]
    tr, tc = _tile2d(r, c)

    def body(g_ref, w_ref, m_ref, v_ref, go_ref, d_ref, mo_ref, vo_ref):
        g = g_ref[0].astype(F32)
        for k in range(1, n_terms):
            g = g + g_ref[k].astype(F32)
        m_new = ADAM_B1 * m_ref[...] + (1.0 - ADAM_B1) * g
        v_new = ADAM_B2 * v_ref[...] + (1.0 - ADAM_B2) * (g * g)
        m_hat = m_new / (1.0 - ADAM_B1 ** ADAM_STEP)
        v_hat = v_new / (1.0 - ADAM_B2 ** ADAM_STEP)
        go_ref[...] = g
        d_ref[...] = -ADAM_LR * (m_hat / (jnp.sqrt(v_hat) + ADAM_EPS) + ADAM_WD * w_ref[...])
        mo_ref[...] = m_new
        vo_ref[...] = v_new

    blk = pl.BlockSpec((tr, tc), lambda i, j: (i, j))
    return pl.pallas_call(
        body, name=name, grid=(r // tr, c // tc),
        in_specs=[pl.BlockSpec((n_terms, tr, tc), lambda i, j: (0, i, j)), blk, blk, blk], out_specs=[blk] * 4,
        out_shape=[jax.ShapeDtypeStruct((r, c), F32)] * 4, compiler_params=_cparams(("parallel", "parallel")),
    )(recv, w, m, v)


def _tile2d(r, c, cap=256):
    if r <= cap:
        return r, c
    for t in range(cap, 0, -BF16_ROWS):
        if r % t == 0:
            return t, c
    return r, _pick(c, cap)


def _pack(pieces, dtype, quantum):
    out = []
    for p in pieces:
        lead, n = p.shape[:-1], p.shape[-1]
        pad = (-n) % quantum
        p = p.astype(dtype)
        if pad:
            p = jnp.concatenate([p, jnp.zeros(lead + (pad,), dtype)], axis=-1)
        out.append(p)
    flat = jnp.concatenate(out, axis=-1)
    return flat.reshape(flat.shape[:-1] + (flat.shape[-1] // LANES, LANES))


def _unpack(flat, sizes, quantum):
    flat = flat.reshape(flat.shape[:-2] + (-1,))
    out, o = [], 0
    for n in sizes:
        out.append(flat[..., o:o + n])
        o += n + (-n) % quantum
    return out


def _prepare_weights(full, vec, dims):
    rest = {n: t for n, t in full.items() if n != "w_in"}
    return {"w_in_t": _prepare_w_in(full["w_in"], dims), **_prepare_rest(rest, dims), **_prepare_vectors(vec, dims)}


def _prepare_w_in(slabs, dims):
    D = dims["D"]
    c = slabs.shape[1]
    parts, pos = [], 0
    for orig_off, width, perm_off in sorted(dims["segs"], key=lambda t: t[2]):
        if perm_off > pos:
            parts.append(jnp.zeros((perm_off - pos, D), BF16))
        for k in range(N_DEV):
            lo, hi = max(orig_off, k * c), min(orig_off + width, (k + 1) * c)
            if lo < hi:
                parts.append(slabs[k][lo - k * c:hi - k * c])
        pos = perm_off + width
    if dims["d_in_perm"] > pos:
        parts.append(jnp.zeros((dims["d_in_perm"] - pos, D), BF16))
    return jnp.concatenate(parts, axis=0)


def _prepare_rest(full, dims):
    hm, hr, hn, rank = dims["hm"], dims["hr"], dims["hn"], dims["rank"]
    QR, KVR = dims["QR"], dims["KVR"]
    RW, TAIL = hr * hn, dims["TAIL"]
    full = {n: (t.reshape(-1, t.shape[2]) if n in _ROW_SHARDED + _TRANSPOSED
                else t.transpose(1, 0, 2).reshape(t.shape[1], -1)) for n, t in full.items()}
    wq = full["mla_wq_b"].reshape(hm, NOPE + ROPE, QR)
    wq = jnp.concatenate([wq, jnp.zeros((hm, QHEAD - NOPE - ROPE, QR), BF16)], axis=1).reshape(hm * QHEAD, QR)
    wkv = full["mla_wkv_b"].reshape(KVR, hm, 2, NOPE).transpose(0, 2, 1, 3).reshape(KVR, 2 * hm * NOPE)
    z = lambda rows: jnp.zeros((rows, RW), BF16)
    f = lambda nme: full[nme]
    split = ROPE + 2 * rank
    assert split % LANES == 0, split
    w2cat = jnp.concatenate([
        jnp.concatenate([z(ROPE), f("rwkv_w2_f"), z(rank)], axis=0),
        jnp.concatenate([z(ROPE + rank), f("rwkv_w2_b")], axis=0)], axis=1)
    a2cat = jnp.concatenate([
        jnp.concatenate([f("rwkv_a2_f"), z(TAIL - split - rank)], axis=0),
        jnp.concatenate([z(rank), f("rwkv_a2_b"), z(TAIL - split - 2 * rank)], axis=0)], axis=1)
    return dict(wq_b_t=wq, wkv_b=wkv, w2cat=w2cat, a2cat=a2cat, w_br_mla=full["w_br_mla"],
                w_br_rwkv=full["w_br_rwkv"], w_out=full["w_out"])


def _prepare_vectors(vec, dims):
    rank, RW, TAIL = dims["rank"], dims["hr"] * dims["hn"], dims["TAIL"]
    mu = vec["rwkv_mu"]
    mu_p = jnp.concatenate([mu[:3 * RW], jnp.zeros((ROPE,), F32), mu[3 * RW:],
                            jnp.zeros((TAIL - ROPE - 4 * rank,), F32)])
    row = lambda t: t.reshape(1, -1)
    return dict(
        mu=row(mu_p), g_pre=row(vec["g_pre"]), g_post=row(vec["g_post"]), mla_q_norm=row(vec["mla_q_norm"]),
        mla_kv_norm=row(vec["mla_kv_norm"]), w0_f=row(vec["rwkv_w0_f"]), w0_b=row(vec["rwkv_w0_b"]),
        a0_f=row(vec["rwkv_a0_f"]), a0_b=row(vec["rwkv_a0_b"]), k_k=row(vec["rwkv_k_k"]), k_a=row(vec["rwkv_k_a"]),
        r_k=row(vec["rwkv_r_k"]), gn_g=row(vec["rwkv_gn_g"]), gn_b=row(vec["rwkv_gn_b"]))


def _restore_grads(g, dims):
    return {"w_in": _restore_w_in(g["w_in"], dims), **_restore_rest(g, dims), **_restore_vectors(g, dims)}


def _restore_w_in(gw, dims):
    c = dims["d_in"] // N_DEV
    slabs = []
    for k in range(N_DEV):
        parts = []
        for orig_off, width, perm_off in sorted(dims["segs"]):
            lo_, hi_ = max(orig_off, k * c), min(orig_off + width, (k + 1) * c)
            if lo_ < hi_:
                parts.append(gw[perm_off + lo_ - orig_off:perm_off + hi_ - orig_off])
        slabs.append(jnp.concatenate(parts, axis=0))
    return jnp.stack(slabs)


def _restore_rest(g, dims):
    hm, hr, hn, rank = dims["hm"], dims["hr"], dims["hn"], dims["rank"]
    QR, KVR, RW = dims["QR"], dims["KVR"], hr * hn
    wq = g["wq_b"].reshape(hm, QHEAD, QR)[:, :NOPE + ROPE].reshape(N_DEV, -1, QR)
    wkv = g["wkv_b"].reshape(KVR, 2, hm, NOPE).transpose(0, 2, 1, 3).reshape(KVR, 2 * hm * NOPE)
    lo = lambda t, first, half: t[first:first + rank, half * RW:(half + 1) * RW].astype(BF16)
    cols = lambda t: t.reshape(t.shape[0], N_DEV, -1).transpose(1, 0, 2)
    return dict(
        mla_wq_b=wq, mla_wkv_b=cols(wkv), rwkv_w2_f=cols(lo(g["w2cat"], ROPE, 0)),
        rwkv_w2_b=cols(lo(g["w2cat"], ROPE + rank, 1)), rwkv_a2_f=cols(lo(g["a2cat"], 0, 0)),
        rwkv_a2_b=cols(lo(g["a2cat"], rank, 1)), w_br_mla=cols(g["w_br_mla"]), w_br_rwkv=cols(g["w_br_rwkv"]),
        w_out=g["w_out"].reshape(N_DEV, -1, g["w_out"].shape[1]))


def _restore_vectors(g, dims):
    rank, RW = dims["rank"], dims["hr"] * dims["hn"]
    mu = g["mu"][0]
    out = dict(
        rwkv_mu=jnp.concatenate([mu[:3 * RW], mu[3 * RW + ROPE:3 * RW + ROPE + 4 * rank]]),
        g_pre=g["g_pre"][0], g_post=g["g_post"][0], mla_q_norm=g["mla_q_norm"][0], mla_kv_norm=g["mla_kv_norm"][0],
        rwkv_w0_f=g["w0_f"][0], rwkv_w0_b=g["w0_b"][0], rwkv_a0_f=g["a0_f"][0], rwkv_a0_b=g["a0_b"][0],
        rwkv_k_k=g["k_k"][0], rwkv_k_a=g["k_a"][0], rwkv_r_k=g["r_k"][0], rwkv_gn_g=g["gn_g"][0],
        rwkv_gn_b=g["gn_b"][0])
    return out


def _dims(inp):
    D = inp["x"].shape[-1]
    QR, KVR = inp["mla_q_norm"].shape[0], inp["mla_kv_norm"].shape[0]
    hm = inp["mla_wq_b"].shape[1] * N_DEV // (NOPE + ROPE)
    hr, hn = inp["rwkv_r_k"].shape
    rank = inp["rwkv_w2_f"].shape[0]
    MW, RW = hm * VDIM, hr * hn
    TAIL = -(-(ROPE + 4 * rank) // LANES) * LANES
    orig, o = {}, 0
    for nme, w in (("q_a", QR), ("kv_a", KVR), ("k_rope", ROPE), ("rkv", 3 * RW), ("lora", 4 * rank), ("z_m", MW),
                   ("z_r", RW), ("gate_m", D), ("gate_r", D)):
        orig[nme] = (o, w)
        o += w
    assert o == inp["w_in"].shape[1] * N_DEV
    lay, d_in_perm = _layout(D, MW, RW, TAIL, QR, KVR)
    perm_off = dict(q_a=lay["q_a"][0], kv_a=lay["kv_a"][0], k_rope=lay["tail"][0], rkv=lay["r"][0],
                    lora=lay["tail"][0] + ROPE, z_m=lay["z_m"][0], z_r=lay["z_r"][0], gate_m=lay["gate_m"][0],
                    gate_r=lay["gate_r"][0])
    segs = [(orig[nme][0], orig[nme][1], perm_off[nme]) for nme in orig]
    return dict(D=D, QR=QR, KVR=KVR, hm=hm, hr=hr, hn=hn, rank=rank, TAIL=TAIL, segs=segs, d_in=o,
                d_in_perm=d_in_perm)


def kernel(x, g_pre, w_in, mla_q_norm, mla_wq_b, mla_kv_norm, mla_wkv_b, rwkv_mu, rwkv_w0_f, rwkv_w2_f, rwkv_w0_b, rwkv_w2_b, rwkv_a0_f, rwkv_a2_f, rwkv_a0_b, rwkv_a2_b, rwkv_k_k, rwkv_k_a, rwkv_r_k, rwkv_gn_g, rwkv_gn_b, w_br_mla, w_br_rwkv, w_out, g_post, loss_target, m_g_pre, m_w_in, m_mla_q_norm, m_mla_wq_b, m_mla_kv_norm, m_mla_wkv_b, m_rwkv_mu, m_rwkv_w0_f, m_rwkv_w2_f, m_rwkv_w0_b, m_rwkv_w2_b, m_rwkv_a0_f, m_rwkv_a2_f, m_rwkv_a0_b, m_rwkv_a2_b, m_rwkv_k_k, m_rwkv_k_a, m_rwkv_r_k, m_rwkv_gn_g, m_rwkv_gn_b, m_w_br_mla, m_w_br_rwkv, m_w_out, m_g_post, v_g_pre, v_w_in, v_mla_q_norm, v_mla_wq_b, v_mla_kv_norm, v_mla_wkv_b, v_rwkv_mu, v_rwkv_w0_f, v_rwkv_w2_f, v_rwkv_w0_b, v_rwkv_w2_b, v_rwkv_a0_f, v_rwkv_a2_f, v_rwkv_a0_b, v_rwkv_a2_b, v_rwkv_k_k, v_rwkv_k_a, v_rwkv_r_k, v_rwkv_gn_g, v_rwkv_gn_b, v_w_br_mla, v_w_br_rwkv, v_w_out, v_g_post):
    inp = dict(locals())
    dims = _dims(inp)
    stored = lambda t, n: t.T if n in _TRANSPOSED else t
    assert _MATS[0] == "w_in"
    shards = [stored(inp[n], n).astype(BF16) for n in _MATS]
    core = lax.axis_index("c").astype(jnp.int32).reshape(1)
    (w_in_slabs,) = _run_exchange(_gather_plan(shards[:1]), name="gather_w_in")
    W = {"w_in_t": _prepare_w_in(w_in_slabs, dims), **_prepare_vectors({n: inp[n] for n in _VECS}, dims)}
    loss, grad_x, g, recv_rest = _local_grads(x[0], loss_target[0], W, dims, exchange=(shards[1:], core))
    loss = lax.psum(loss, ("x", "y", "c"))

    new = {}
    *recv_rest, got = recv_rest
    (recv_w_in,) = _run_exchange(_chip_exchange_plan([_pair_add(core, g["w_in"], got, name="pair_add_w_in")]),
                                 name="scatter_w_in")
    g = _restore_vectors(g, dims)
    for n, t in zip(_MATS, [recv_w_in] + recv_rest):
        out = _adamw(t, stored(inp[n], n), stored(inp["m_" + n], n), stored(inp["v_" + n], n), name="adamw_" + n)
        new[n] = [stored(o, n) for o in out]

    vsizes = [inp[n].size for n in _VECS]
    vflat = lambda prefix, src: _pack([src[prefix + n].reshape(-1) for n in _VECS], F32, LANES * 8)
    (vrecv,) = _exchange([vflat("", g)], name="gather_vector_grads")
    vout = _adamw(vrecv, vflat("", inp), vflat("m_", inp), vflat("v_", inp), name="adamw_vectors")
    vparts = [_unpack(t, vsizes, LANES * 8) for t in vout]
    for i, n in enumerate(_VECS):
        new[n] = [vp[i].reshape(inp[n].shape) for vp in vparts]

    outs = [loss, grad_x[None]]
    for k in range(4):
        outs += [new[n][k] for n in _WEIGHTS]
    return tuple(outs)
```

```python
import functools
import math

import jax
import jax.numpy as jnp
from jax import lax
from jax.experimental import pallas as pl
from jax.experimental.pallas import tpu as pltpu

F32 = jnp.float32
BF16 = jnp.bfloat16

N_DEV = 8
LANES = 128
BF16_ROWS = 16
NOPE, ROPE, VDIM = 128, 64, 128
QHEAD = 256
ROPE_THETA = 10000.0
NORM_EPS = 1e-6
GN_EPS = 64e-5
CHUNK = 64
SUB = 16
VMEM_LIMIT = 56 * 1024 * 1024

ADAM_LR, ADAM_B1, ADAM_B2, ADAM_EPS, ADAM_WD, ADAM_STEP = 0.001, 0.9, 0.999, 1e-08, 0.01, 10


def _cparams(sem):
    return pltpu.CompilerParams(dimension_semantics=sem, vmem_limit_bytes=VMEM_LIMIT)


def _pick(n, cap):
    if n <= cap:
        return n
    for t in range(cap - cap % LANES, 0, -LANES):
        if n % t == 0:
            return t
    raise ValueError(f"no tile for {n} under {cap}")


def _mm(a, b, *, ta=False, tb=False, out_dtype=F32, name, tm_cap=1024, tn_cap=512, tk_cap=2048, ride=None):
    K, M = a.shape if ta else a.shape[::-1]
    N = b.shape[0] if tb else b.shape[1]
    assert (b.shape[1] if tb else b.shape[0]) == K, (a.shape, b.shape, ta, tb)
    tm, tn, tk = _pick(M, tm_cap), _pick(N, tn_cap), _pick(K, tk_cap)
    nj, nk = N // tn, K // tk
    steps = (M // tm) * nj * nk
    dn = (((0 if ta else 1,), (1 if tb else 0,)), ((), ()))
    srcs, extra_shapes, sem_shapes, phases = ride if ride else ((), (), (), None)
    n_src, n_extra = len(srcs), len(extra_shapes)

    def body(*refs):
        a_ref, b_ref, o_ref = refs[0], refs[1], refs[2 + n_src]
        acc_ref = refs[3 + n_src + n_extra]
        k = pl.program_id(2)
        if ride:
            step = (pl.program_id(0) * nj + pl.program_id(1)) * nk + k
            first, middle, last = phases(refs[2:2 + n_src], refs[3 + n_src:3 + n_src + n_extra],
                                         refs[4 + n_src + n_extra:])
            pl.when(step == 0)(first)
            pl.when(step == (steps * 7) // 8)(middle)
        p = lax.dot_general(a_ref[...], b_ref[...], dn, preferred_element_type=F32)

        @pl.when(k == 0)
        def _():
            acc_ref[...] = p

        @pl.when(k > 0)
        def _():
            acc_ref[...] += p

        @pl.when(k == nk - 1)
        def _():
            o_ref[...] = acc_ref[...].astype(out_dtype)

        if ride:
            pl.when(step == steps - 1)(last)

    a_spec = pl.BlockSpec((tk, tm), lambda i, j, k: (k, i)) if ta else pl.BlockSpec((tm, tk), lambda i, j, k: (i, k))
    b_spec = pl.BlockSpec((tn, tk), lambda i, j, k: (j, k)) if tb else pl.BlockSpec((tk, tn), lambda i, j, k: (k, j))
    hbm = pl.BlockSpec(memory_space=pl.ANY)
    out = pl.pallas_call(
        body, name=name, grid=(M // tm, nj, nk),
        in_specs=[a_spec, b_spec] + [hbm] * n_src,
        out_specs=[pl.BlockSpec((tm, tn), lambda i, j, k: (i, j))] + [hbm] * n_extra,
        out_shape=[jax.ShapeDtypeStruct((M, N), out_dtype)] + list(extra_shapes),
        scratch_shapes=[pltpu.VMEM((tm, tn), F32)] + [pltpu.SemaphoreType.DMA(s) for s in sem_shapes],
        compiler_params=_cparams(("arbitrary",) * 3 if ride else ("parallel", "parallel", "arbitrary")),
    )(a, b, *srcs)
    return out if ride else out[0]


def _view(arr, off, width):
    assert off % width == 0, (off, width)
    return (arr, off // width, width)


def _rowwise(fn, rows, params, out_rows, out_accs=(), *, tile, name):
    rows = [r if isinstance(r, tuple) else (r, 0, r.shape[1]) for r in rows]
    S = rows[0][0].shape[0]
    T = min(tile, S)
    assert S % T == 0
    n_rows, n_par, n_out = len(rows), len(params), len(out_rows)
    into = [o[2] if len(o) == 3 else None for o in out_rows]
    carried = [t[0] for t in into if t is not None and t[0] is not None]

    def body(*refs):
        ins = [r[...] for r in refs[:n_rows + n_par]]
        outs = fn(*ins)
        out_refs = refs[n_rows + n_par + len(carried):]
        for o_ref, val in zip(out_refs[:n_out], outs[:n_out]):
            o_ref[...] = val.astype(o_ref.dtype)
        i = pl.program_id(0)
        for o_ref, val in zip(out_refs[n_out:], outs[n_out:]):
            @pl.when(i == 0)
            def _(o_ref=o_ref, val=val):
                o_ref[...] = val

            @pl.when(i > 0)
            def _(o_ref=o_ref, val=val):
                o_ref[...] += val

    in_specs = [pl.BlockSpec((T, w), functools.partial(lambda i, cb: (i, cb), cb=cb)) for _, cb, w in rows]
    in_specs += [pl.BlockSpec(p.shape, lambda i: (0, 0)) for p in params]
    in_specs += [pl.BlockSpec(memory_space=pl.ANY)] * len(carried)
    out_specs, out_shape, aliases = [], [], {}
    for k, (o, t) in enumerate(zip(out_rows, into)):
        w, dt = o[0], o[1]
        if t is None:
            out_specs.append(pl.BlockSpec((T, w), lambda i: (i, 0)))
            out_shape.append(jax.ShapeDtypeStruct((S, w), dt))
            continue
        buf, total, first = t
        assert first % w == 0
        out_specs.append(pl.BlockSpec((T, w), functools.partial(lambda i, cb: (i, cb), cb=first // w)))
        out_shape.append(jax.ShapeDtypeStruct((S, total), dt))
        if buf is not None:
            aliases[n_rows + n_par + len(aliases)] = k
    out_specs += [pl.BlockSpec(s, lambda i: (0, 0)) for s in out_accs]
    out_shape += [jax.ShapeDtypeStruct(s, F32) for s in out_accs]
    return pl.pallas_call(
        body, name=name, grid=(S // T,), in_specs=in_specs, out_specs=out_specs, out_shape=out_shape,
        input_output_aliases=aliases, compiler_params=_cparams(("arbitrary",)),
    )(*[r[0] for r in rows], *params, *carried)


def _mm_sel(x, sel):
    hi = x.astype(BF16)
    lo = (x - hi.astype(F32)).astype(BF16)
    d = lambda u: jnp.dot(u, sel, preferred_element_type=F32)
    return d(hi) + d(lo)


@jax.custom_vjp
def _sel(x, sel, sel_t):
    return _mm_sel(x, sel)


def _sel_fwd(x, sel, sel_t):
    return _mm_sel(x, sel), (sel, sel_t)


def _sel_bwd(res, ct):
    sel, sel_t = res
    return _mm_sel(ct, sel_t), jnp.zeros_like(sel), jnp.zeros_like(sel_t)


_sel.defvjp(_sel_fwd, _sel_bwd)


def _rms(x, g):
    return x * lax.rsqrt(jnp.mean(x * x, axis=-1, keepdims=True) + NORM_EPS) * g


def _sigmoid(x):
    return 1.0 / (1.0 + jnp.exp(-x))


def _silu(x):
    return x * _sigmoid(x)


def _softplus(x):
    return jnp.maximum(x, 0.0) + jnp.log(1.0 + jnp.exp(-jnp.abs(x)))


def _f_mla_norm(q_a, kv_a, qg, kvg):
    return _rms(q_a, qg), _rms(kv_a, kvg)


def _f_rope(hm, qraw, kr_in, cosx, sinx, rot, rot_t):
    def rope(t):
        return t * cosx + _sel(t, rot, rot_t) * sinx
    parts = []
    for h in range(hm):
        parts.append(qraw[:, h * QHEAD:h * QHEAD + NOPE])
        parts.append(rope(qraw[:, h * QHEAD + NOPE:(h + 1) * QHEAD]))
    return jnp.concatenate(parts, axis=1), rope(kr_in)


def _f_rwkv_pre(rw, k, tail, w0f, w0b, a0f, a0b, k_k, k_a, w2cat, a2cat, seg, seg_t):
    split = w2cat.shape[0]
    zw = jnp.dot(jnp.tanh(tail[:, :split]).astype(BF16), w2cat, preferred_element_type=F32)
    za = jnp.dot(tail[:, split:].astype(BF16), a2cat, preferred_element_type=F32)
    return _f_rwkv_core(rw, k, zw, za, w0f, w0b, a0f, a0b, k_k, k_a, seg, seg_t)


def _f_rwkv_core(rw, k, zw, za, w0f, w0b, a0f, a0b, k_k, k_a, seg, seg_t):
    lw_f = -jnp.exp(-_softplus(-(w0f + zw[:, :rw])) - 0.5)
    lw_b = -jnp.exp(-_softplus(-(w0b + zw[:, rw:])) - 0.5)
    a_f = _sigmoid(a0f + za[:, :rw])
    a_b = _sigmoid(a0b + za[:, rw:])
    kk = k * k_k
    nrm = jnp.sqrt(_sel(_sel(kk * kk, seg, seg_t), seg_t, seg))
    kk = kk / jnp.maximum(nrm, 1e-12)
    k_f = k * (1.0 + (a_f - 1.0) * k_a)
    k_b = k * (1.0 + (a_b - 1.0) * k_a)
    return lw_f, lw_b, k_f, k_b, -kk, kk * a_f, kk * a_b


def _f_post(hn, y_f, y_b, r, k_f, k_b, v, z_r, o_mla, z_m, gn_g, gn_b, r_k, seg, seg_t):
    segsum = lambda t: _sel(_sel(t, seg, seg_t), seg_t, seg)
    y = y_f + y_b
    mu = segsum(y) * (1.0 / hn)
    yc = y - mu
    var = segsum(yc * yc) * (1.0 / hn)
    yn = yc * lax.rsqrt(var + GN_EPS) * gn_g + gn_b
    bonus = segsum(r * (k_f + k_b) * r_k) * v
    return o_mla * _silu(z_m), (yn + bonus) * _silu(z_r)


def _f_merge(u_m, u_r, g_m, g_r):
    return _sigmoid(g_m) * u_m + _sigmoid(g_r) * u_r


_NN = ((2,), (1,))
_NT = ((2,), (2,))
_TN = ((1,), (1,))

_SCAN_PASSES = {"cum": 2, "gram": 3, "solve": 1, "apply": 1, "state": 1}


def _hdot_raw(passes, x, y, dims):
    dn = (dims, ((0,), (0,)))
    d = lambda p, q: lax.dot_general(p, q, dn, preferred_element_type=F32)
    xh = x.astype(BF16)
    yh = y.astype(BF16)
    if passes == 1:
        return d(xh, yh)
    yl = (y - yh.astype(F32)).astype(BF16)
    if passes == 2:
        return d(xh, yh) + d(xh, yl)
    xl = (x - xh.astype(F32)).astype(BF16)
    return d(xh, yh) + d(xh, yl) + d(xl, yh)


@functools.partial(jax.custom_vjp, nondiff_argnums=(2, 3))
def _hdot_p(x, y, dims, passes):
    return _hdot_raw(passes, x, y, dims)


def _hdot_fwd(x, y, dims, passes):
    return _hdot_raw(passes, x, y, dims), (x, y)


def _hdot_bwd(dims, passes, res, ct):
    x, y = res
    if dims == _NN:
        return _hdot_raw(passes, ct, y, _NT), _hdot_raw(passes, x, ct, _TN)
    if dims == _NT:
        return _hdot_raw(passes, ct, y, _NN), _hdot_raw(passes, ct, x, _TN)
    return _hdot_raw(passes, y, ct, _NT), _hdot_raw(passes, x, ct, _NN)


_hdot_p.defvjp(_hdot_fwd, _hdot_bwd)


def _hdot(x, y, dims, kind):
    return _hdot_p(x, y, dims, _SCAN_PASSES[kind])


def _tri_solve(n_mat, x, length):
    row = lax.broadcasted_iota(jnp.int32, (length, length), 0)
    col = lax.broadcasted_iota(jnp.int32, (length, length), 1)
    eye = (row == col).astype(F32)[None]
    diag_blk = ((row // SUB) == (col // SUB))[None]
    nd = jnp.where(diag_blk, n_mat, 0.0)
    no = n_mat - nd
    dinv = eye + nd
    p = nd
    for _ in range(int(math.log2(SUB)) - 1):
        p = _hdot(p, p, _NN, "solve")
        dinv = dinv + _hdot(dinv, p, _NN, "solve")
    q = _hdot(dinv, no, _NN, "solve")
    u = _hdot(dinv, x, _NN, "solve")
    levels = int(math.log2(length // SUB))
    qs = [q]
    for _ in range(levels - 1):
        qs.append(_hdot(qs[-1], qs[-1], _NN, "solve"))
    for qk in reversed(qs):
        u = u + _hdot(qk, u, _NN, "solve")
    return u


def _rwkv_chunk(rev, s0, r, lw, k, v, a, b):
    pairs, length, width = r.shape
    hn = width // 2
    row = lax.broadcasted_iota(jnp.int32, (length, length), 0)
    col = lax.broadcasted_iota(jnp.int32, (length, length), 1)
    row2 = lax.broadcasted_iota(jnp.int32, (length, 2 * length), 0)
    col2 = lax.broadcasted_iota(jnp.int32, (length, 2 * length), 1)
    col2 = jnp.where(col2 >= length, col2 - length, col2)
    if rev is None:
        half = pairs // 2
        back = lax.broadcasted_iota(jnp.int32, (pairs, length, length), 0) >= half
        idx2 = lax.broadcasted_iota(jnp.int32, (2 * pairs, length, 2 * length), 0)
        back2 = ((idx2 >= half) & (idx2 < pairs)) | (idx2 >= pairs + half)
        ahead = jnp.where(back, (col - row)[None], (row - col)[None])
        ahead2 = jnp.where(back2, (col2 - row2)[None], (row2 - col2)[None])
        incl, strict2, incl2 = ahead >= 0, ahead2 > 0, ahead2 >= 0
    else:
        incl = ((row <= col) if rev else (row >= col))[None]
        strict2 = ((row2 < col2) if rev else (row2 > col2))[None]
        incl2 = ((row2 <= col2) if rev else (row2 >= col2))[None]
    lane = lax.broadcasted_iota(jnp.int32, (1, 1, width), 2)
    first = lane < hn
    head_mask = jnp.concatenate([jnp.broadcast_to(first.astype(F32), (pairs, 1, width)),
                                 jnp.broadcast_to(1.0 - first.astype(F32), (pairs, 1, width))], axis=0)
    twice = lambda t: jnp.concatenate([t, t], axis=0)
    pick = lambda t: jnp.where(first, t[:pairs], t[pairs:])

    t_incl = jnp.broadcast_to(incl.astype(F32), (pairs, length, length))
    cum = _hdot(t_incl, lw, _NN, "cum")
    g = jnp.exp(cum)
    g_inv = jnp.exp(-cum)
    at = a * jnp.exp(cum - lw)
    rt = r * g
    bt = b * g_inv
    kt = k * g_inv
    lhs = jnp.concatenate([twice(at) * head_mask, twice(rt) * head_mask], axis=1)
    rhs = jnp.concatenate([twice(bt), twice(kt)], axis=1)
    gram = _hdot(lhs, rhs, _NT, "gram")
    top = jnp.where(strict2, gram[:, :length], 0.0)
    bot = jnp.where(incl2, gram[:, length:], 0.0)
    v2 = twice(v)
    zeros = jnp.zeros_like(v2)
    x = _hdot(at, s0, _NT, "apply") + pick(_hdot(top, jnp.concatenate([zeros, v2], axis=1), _NN, "apply"))
    u = pick(_tri_solve(top[:, :, :length], twice(x), length))
    y = _hdot(rt, s0, _NT, "apply") + pick(_hdot(bot, jnp.concatenate([twice(u), v2], axis=1), _NN, "apply"))
    g_last = jnp.exp(jnp.sum(lw, axis=1, keepdims=True))
    ri = lax.broadcasted_iota(jnp.int32, (width, width), 0)
    ci = lax.broadcasted_iota(jnp.int32, (width, width), 1)
    same_head = ((ri < hn) == (ci < hn))[None]
    upd = _hdot(u, bt, _TN, "state") + _hdot(v, kt, _TN, "state")
    s1 = (s0 + jnp.where(same_head, upd, 0.0)) * g_last
    return y, s1


def _split_pairs(x):
    return jnp.stack([x[:, p * LANES:(p + 1) * LANES] for p in range(x.shape[1] // LANES)])


def _merge_pairs(x):
    return jnp.concatenate([x[p] for p in range(x.shape[0])], axis=1)


def _scan_specs(views, rw, nc, rev):
    cidx = (lambda c: nc - 1 - c) if rev else (lambda c: c)
    seqs = [pl.BlockSpec((CHUNK, rw), functools.partial(lambda c, cb: (cidx(c), cb), cb=cb)) for _, cb, _ in views]
    plain = pl.BlockSpec((CHUNK, rw), lambda c: (cidx(c), 0))
    st = pl.BlockSpec((1, rw // LANES, LANES, LANES), lambda c: (cidx(c), 0, 0, 0))
    return seqs, plain, st


def _as_views(arrs, rw):
    return [t if isinstance(t, tuple) else (t, 0, rw) for t in arrs]


def _rwkv_scan_fwd(ops_f, ops_b, rw, *, name):
    S = _as_views(ops_f, rw)[0][0].shape[0]
    nc, pairs = S // CHUNK, rw // LANES
    in_specs, out_specs, arrays = [], [], []
    for rev, ops in ((False, ops_f), (True, ops_b)):
        views = _as_views(ops, rw)
        seqs, plain, st = _scan_specs(views, rw, nc, rev)
        in_specs += seqs
        out_specs += [plain, st]
        arrays += [t[0] for t in views]

    def both(refs_f, refs_b):
        return [jnp.concatenate([_split_pairs(f[...]), _split_pairs(b[...])], axis=0) for f, b in zip(refs_f, refs_b)]

    def body(*refs):
        (y_f, st_f, y_b, st_b), s_ref = refs[12:16], refs[16]

        @pl.when(pl.program_id(0) == 0)
        def _():
            s_ref[...] = jnp.zeros_like(s_ref)

        s0 = s_ref[...]
        st_f[0] = s0[:pairs]
        st_b[0] = s0[pairs:]
        y, s1 = _rwkv_chunk(None, s0, *both(refs[:6], refs[6:12]))
        y_f[...] = _merge_pairs(y[:pairs])
        y_b[...] = _merge_pairs(y[pairs:])
        s_ref[...] = s1

    return pl.pallas_call(
        body, name=name, grid=(nc,), in_specs=in_specs, out_specs=out_specs,
        out_shape=[jax.ShapeDtypeStruct((S, rw), F32), jax.ShapeDtypeStruct((nc, pairs, LANES, LANES), F32)] * 2,
        scratch_shapes=[pltpu.VMEM((2 * pairs, LANES, LANES), F32)],
        compiler_params=_cparams(("arbitrary",)),
    )(*arrays)


def _rwkv_scan_bwd(ops_f, ops_b, states_f, states_b, dy, rw, *, name):
    S = dy.shape[0]
    nc, pairs = S // CHUNK, rw // LANES
    in_specs, arrays = [], []
    for rev, ops, states in ((False, ops_f, states_f), (True, ops_b, states_b)):
        views = _as_views(list(ops) + [dy], rw)
        seqs, plain, st = _scan_specs(views, rw, nc, not rev)
        in_specs += seqs + [st]
        arrays += [t[0] for t in views] + [states]
    out_specs = []
    for rev in (False, True):
        out_specs += [_scan_specs([], rw, nc, not rev)[1]] * 6

    def both(refs_f, refs_b):
        return [jnp.concatenate([_split_pairs(f[...]), _split_pairs(b[...])], axis=0) for f, b in zip(refs_f, refs_b)]

    def body(*refs):
        ds_ref = refs[28]

        @pl.when(pl.program_id(0) == 0)
        def _():
            ds_ref[...] = jnp.zeros_like(ds_ref)

        s0 = jnp.concatenate([refs[7][0], refs[15][0]], axis=0)
        _, vjp = jax.vjp(functools.partial(_rwkv_chunk, None), s0, *both(refs[:6], refs[8:14]))
        (dy,) = both(refs[6:7], refs[14:15])
        grads = vjp((dy, ds_ref[...]))
        ds_ref[...] = grads[0]
        for o_f, o_b, gval in zip(refs[16:22], refs[22:28], grads[1:]):
            o_f[...] = _merge_pairs(gval[:pairs])
            o_b[...] = _merge_pairs(gval[pairs:])

    return pl.pallas_call(
        body, name=name, grid=(nc,), in_specs=in_specs, out_specs=out_specs,
        out_shape=[jax.ShapeDtypeStruct((S, rw), F32)] * 12,
        scratch_shapes=[pltpu.VMEM((2 * pairs, LANES, LANES), F32)],
        compiler_params=_cparams(("arbitrary",)),
    )(*arrays)


def _shift_lerp(x_view, mu, d=None, into=None, *, name):
    arr, off, width = x_view
    S = arr.shape[0]
    cb = _pick(width, 256)
    assert off % cb == 0

    def cshift(t):
        rows = lax.broadcasted_iota(jnp.int32, t.shape, 0)
        prev = jnp.where(rows == 0, 0.0, pltpu.roll(t, 1, 0))
        nxt = jnp.where(rows == S - 1, 0.0, pltpu.roll(t, S - 1, 0))
        return 0.5 * (prev + nxt)

    def fwd_body(x_ref, mu_ref, o_ref):
        x = x_ref[...]
        o_ref[...] = x + mu_ref[...] * (cshift(x) - x)

    def bwd_body(x_ref, mu_ref, d_ref, _, dx_ref, dmu_ref):
        x, m, dd = x_ref[...], mu_ref[...], d_ref[...]
        gm = m * dd
        dx_ref[...] = (dd - gm + cshift(gm)).astype(dx_ref.dtype)
        dmu_ref[...] = jnp.sum(dd * (cshift(x) - x), axis=0, keepdims=True)

    x_spec = pl.BlockSpec((S, cb), lambda j: (0, off // cb + j))
    blk = pl.BlockSpec((S, cb), lambda j: (0, j))
    vec = pl.BlockSpec((1, cb), lambda j: (0, j))
    if d is None:
        return pl.pallas_call(
            fwd_body, name=name, grid=(width // cb,), in_specs=[x_spec, vec], out_specs=blk,
            out_shape=jax.ShapeDtypeStruct((S, width), F32), compiler_params=_cparams(("parallel",)),
        )(arr, mu)
    buf, first = into
    assert first % cb == 0
    return pl.pallas_call(
        bwd_body, name=name, grid=(width // cb,),
        in_specs=[x_spec, vec, blk, pl.BlockSpec(memory_space=pl.ANY)],
        out_specs=[pl.BlockSpec((S, cb), lambda j: (0, first // cb + j)), vec],
        out_shape=[jax.ShapeDtypeStruct(buf.shape, buf.dtype), jax.ShapeDtypeStruct((1, width), F32)],
        input_output_aliases={3: 0}, compiler_params=_cparams(("parallel",)),
    )(arr, mu, d, buf)


def _attention_fwd(qfull, kv, kr, hm, scale, *, tq, name):
    S = qfull.shape[0]
    nt = (((1,), (1,)), ((), ()))

    def body(q_ref, kn_ref, kr_ref, v_ref, o_ref, lse_ref, k_scr):
        _head_keys(kn_ref, kr_ref, k_scr)
        s = lax.dot_general(q_ref[...], k_scr[...], nt, preferred_element_type=F32)
        m = jnp.max(s, axis=-1, keepdims=True)
        p = jnp.exp((s - m) * scale)
        l = jnp.sum(p, axis=-1, keepdims=True)
        o_ref[...] = jnp.dot(p.astype(BF16), v_ref[...], preferred_element_type=F32) * (1.0 / l)
        lse_ref[...] = jnp.broadcast_to(m * scale + jnp.log(l), lse_ref.shape)

    oblk = pl.BlockSpec((tq, VDIM), lambda h, i: (i, h))
    return pl.pallas_call(
        body, name=name, grid=(hm, S // tq),
        in_specs=[pl.BlockSpec((tq, QHEAD), lambda h, i: (i, h)),
                  pl.BlockSpec((S, NOPE), lambda h, i: (0, h)),
                  pl.BlockSpec((S, LANES), lambda h, i: (0, 0)),
                  pl.BlockSpec((S, VDIM), lambda h, i: (0, hm + h))],
        out_specs=[oblk, oblk],
        out_shape=[jax.ShapeDtypeStruct((S, hm * VDIM), F32)] * 2,
        scratch_shapes=[pltpu.VMEM((S, QHEAD), BF16)],
        compiler_params=_cparams(("parallel", "arbitrary")),
    )(qfull, kv, kr, kv)


def _head_keys(kn_ref, kr_ref, k_scr):
    @pl.when(pl.program_id(1) == 0)
    def _():
        k_scr[:, :NOPE] = kn_ref[...]
        k_scr[:, NOPE:] = kr_ref[...]


def _attention_bwd(qfull, kv, kr, o, lse, d_o, hm, scale, *, tq, name):
    S = qfull.shape[0]
    tq = min(tq, S)
    nq = S // tq
    tn = (((0,), (0,)), ((), ()))
    nt = (((1,), (1,)), ((), ()))

    def body(q_ref, kn_ref, kr_ref, v_ref, o_ref, lse_ref, do_ref, dq_ref, dk_ref, dv_ref, k_scr):
        _head_keys(kn_ref, kr_ref, k_scr)
        s = lax.dot_general(q_ref[...], k_scr[...], nt, preferred_element_type=F32)
        p = jnp.exp(s * scale - lse_ref[:, 0:1])
        d_out = do_ref[...]
        delta = jnp.sum(d_out * o_ref[...], axis=-1, keepdims=True)
        d_out = d_out.astype(BF16)
        dp = lax.dot_general(d_out, v_ref[...], nt, preferred_element_type=F32)
        ds = (p * (dp - delta)).astype(BF16)
        dq_ref[...] = jnp.dot(ds, k_scr[...], preferred_element_type=F32) * scale
        dv = lax.dot_general(p.astype(BF16), d_out, tn, preferred_element_type=F32)
        dk = lax.dot_general(ds, q_ref[...], tn, preferred_element_type=F32)
        i = pl.program_id(1)
        for ref, val in ((dk_ref, dk), (dv_ref, dv)):
            @pl.when(i == 0)
            def _(ref=ref, val=val):
                ref[...] = val

            @pl.when(i > 0)
            def _(ref=ref, val=val):
                ref[...] += val

        @pl.when(i == nq - 1)
        def _():
            dk_ref[...] = dk_ref[...] * scale

    qblk = pl.BlockSpec((tq, QHEAD), lambda h, i: (i, h))
    oblk = pl.BlockSpec((tq, VDIM), lambda h, i: (i, h))
    return pl.pallas_call(
        body, name=name, grid=(hm, nq),
        in_specs=[qblk,
                  pl.BlockSpec((S, NOPE), lambda h, i: (0, h)),
                  pl.BlockSpec((S, LANES), lambda h, i: (0, 0)),
                  pl.BlockSpec((S, VDIM), lambda h, i: (0, hm + h)),
                  oblk, oblk, oblk],
        out_specs=[qblk, pl.BlockSpec((S, QHEAD), lambda h, i: (0, h)), pl.BlockSpec((S, VDIM), lambda h, i: (0, h))],
        out_shape=[jax.ShapeDtypeStruct((S, hm * QHEAD), F32), jax.ShapeDtypeStruct((S, hm * QHEAD), F32),
                   jax.ShapeDtypeStruct((S, hm * VDIM), F32)],
        scratch_shapes=[pltpu.VMEM((S, QHEAD), BF16)],
        compiler_params=_cparams(("parallel", "arbitrary")),
    )(qfull, kv, kr, kv, o, lse, d_o)


def _layout(D, MW, RW, TAIL, QR, KVR):
    names = ["gate_m", "gate_r", "z_m", "z_r", "q_a", "kv_a", "r", "k", "v", "tail"]
    widths = [D, D, MW, RW, QR, KVR, RW, RW, RW, TAIL]
    offs, o = {}, 0
    for nme, w in zip(names, widths):
        assert o % w == 0, (nme, o, w)
        offs[nme] = (o, w)
        o += w
    return offs, o


def _local_grads(x, target, W, dims, exchange=None):
    S, D = x.shape
    hm, hr, hn, rank = dims["hm"], dims["hr"], dims["hn"], dims["rank"]
    MW, RW = hm * VDIM, hr * hn
    TAIL = dims["TAIL"]
    QR, KVR = W["mla_q_norm"].shape[1], W["mla_kv_norm"].shape[1]
    lay, d_in = _layout(D, MW, RW, TAIL, QR, KVR)
    T = 256
    scale = (NOPE + ROPE) ** -0.5
    col = lambda arr, nme: _view(arr, *lay[nme])

    pos = jnp.arange(S, dtype=F32)
    inv_freq = jnp.power(ROPE_THETA, -jnp.arange(0, ROPE, 2, dtype=F32) / ROPE)
    ang = pos[:, None] * inv_freq[None, :]
    zpad = jnp.zeros((S, LANES - ROPE), F32)
    cosx = jnp.concatenate([jnp.cos(ang), jnp.cos(ang), zpad], axis=1)
    sinx = jnp.concatenate([jnp.sin(ang), jnp.sin(ang), zpad], axis=1)
    ri, ci = jnp.arange(LANES)[:, None], jnp.arange(LANES)[None, :]
    half = ROPE // 2
    rot = (jnp.where((ri == ci - half) & (ci >= half) & (ci < ROPE), 1.0, 0.0)
           - jnp.where((ri == ci + half) & (ci < half), 1.0, 0.0)).astype(BF16)
    rot_t = rot.T
    seg = (jnp.arange(RW)[:, None] // hn == jnp.arange(LANES)[None, :]).astype(BF16)
    seg_t = seg.T

    (h,) = _rowwise(lambda xb, g: (_rms(xb, g),), [x], [W["g_pre"]], [(D, BF16)], tile=T, name="pre_norm")
    if exchange is None:
        proj = _mm(h, W["w_in_t"], tb=True, name="in_proj")
    else:
        proj, *slabs = _mm(h, W["w_in_t"], tb=True, ride=_gather_plan(exchange[0]), name="in_proj")
        W = {**W, **_prepare_rest(dict(zip(_MATS[1:], slabs)), dims)}

    qn, kvn = _rowwise(_f_mla_norm, [col(proj, "q_a"), col(proj, "kv_a")], [W["mla_q_norm"], W["mla_kv_norm"]],
                       [(QR, BF16), (KVR, BF16)], tile=T, name="mla_norm")
    qraw = _mm(qn, W["wq_b_t"], tb=True, name="q_up")
    kv = _mm(kvn, W["wkv_b"], out_dtype=BF16, name="kv_up")
    kr_view = _view(proj, lay["tail"][0], LANES)
    qfull, kr = _rowwise(functools.partial(_f_rope, hm), [qraw, kr_view, cosx, sinx], [rot, rot_t],
                         [(hm * QHEAD, BF16), (LANES, BF16)], tile=T, name="rope")
    o_mla, lse = _attention_fwd(qfull, kv, kr, hm, scale, tq=T, name="attn_fwd")

    shift_view = (proj, lay["r"][0], 3 * RW + TAIL)
    rl = _shift_lerp(shift_view, W["mu"], name="shift_fwd")
    rl_r, rl_k, rl_v = _view(rl, 0, RW), _view(rl, RW, RW), _view(rl, 2 * RW, RW)
    rl_tail = _view(rl, 3 * RW, TAIL)
    pre_params = [W["w0_f"], W["w0_b"], W["a0_f"], W["a0_b"], W["k_k"], W["k_a"], W["w2cat"], W["a2cat"], seg, seg_t]
    pre_fn = functools.partial(_f_rwkv_pre, RW)
    lw_f, lw_b, k_f, k_b, a_n, b_f, b_b = _rowwise(pre_fn, [rl_k, rl_tail], pre_params, [(RW, F32)] * 7, tile=T,
                                                    name="rwkv_pre")
    ops_f = (rl_r, lw_f, k_f, rl_v, a_n, b_f)
    ops_b = (rl_r, lw_b, k_b, rl_v, a_n, b_b)
    y_f, st_f, y_b, st_b = _rwkv_scan_fwd(ops_f, ops_b, RW, name="scan_fwd")

    post_fn = functools.partial(_f_post, hn)
    post_rows = [y_f, y_b, rl_r, k_f, k_b, rl_v, col(proj, "z_r"), o_mla, col(proj, "z_m")]
    post_params = [W["gn_g"], W["gn_b"], W["r_k"], seg, seg_t]
    ymg, yrg = _rowwise(post_fn, post_rows, post_params, [(MW, BF16), (RW, BF16)], tile=T, name="post")
    u_m = _mm(ymg, W["w_br_mla"], name="br_mla")
    u_r = _mm(yrg, W["w_br_rwkv"], name="br_rwkv")
    merge_rows = [u_m, u_r, col(proj, "gate_m"), col(proj, "gate_r")]
    (merged,) = _rowwise(lambda *t: (_f_merge(*t),), merge_rows, [], [(D, BF16)], tile=T, name="merge")
    out = _mm(merged, W["w_out"], name="out_proj")

    def head(ob, xb, tb, g):
        yn, vjp = jax.vjp(_rms, ob, g)
        err = xb + yn - tb
        dy = err * (1.0 / D)
        d_ob, d_g = vjp(dy)
        loss = jnp.broadcast_to(0.5 * jnp.sum(err * err) * (1.0 / D), (1, LANES))
        return dy, d_ob, loss, d_g

    dy, d_out, loss, g_g_post = _rowwise(head, [out, x, target], [W["g_post"]], [(D, F32), (D, BF16)],
                                         [(1, LANES), (1, D)], tile=T, name="head")
    d_merged = _mm(d_out, W["w_out"], tb=True, name="d_merged")
    g_w_out = _mm(merged, d_out, ta=True, out_dtype=BF16, name="g_w_out")

    def merge_bwd(u_m_b, u_r_b, g_m_b, g_r_b, dm):
        _, vjp = jax.vjp(_f_merge, u_m_b, u_r_b, g_m_b, g_r_b)
        du_m, du_r, dg_m, dg_r = vjp(dm)
        return du_m, du_r, jnp.concatenate([dg_m, dg_r], axis=1)

    d_u_m, d_u_r, d_proj = _rowwise(merge_bwd, merge_rows + [d_merged], [],
                                    [(D, BF16), (D, BF16), (2 * D, BF16, (None, d_in, lay["gate_m"][0]))], tile=T,
                                    name="merge_bwd")
    d_ymg = _mm(d_u_m, W["w_br_mla"], tb=True, name="d_ymg")
    d_yrg = _mm(d_u_r, W["w_br_rwkv"], tb=True, name="d_yrg")
    g_w_br_mla = _mm(ymg, d_u_m, ta=True, out_dtype=BF16, name="g_w_br_mla")
    g_w_br_rwkv = _mm(yrg, d_u_r, ta=True, out_dtype=BF16, name="g_w_br_rwkv")

    def post_bwd(*args):
        nr = len(post_rows)
        prim, dm, dr = args[:nr] + args[nr + 2:], args[nr], args[nr + 1]
        _, vjp = jax.vjp(post_fn, *prim)
        g = vjp((dm, dr))
        return g[0], g[2], g[3], g[5], g[7], jnp.concatenate([g[8], g[6]], axis=1), g[9], g[10], g[11]

    (d_y, d_r_bonus, d_k_bonus, d_v_bonus, d_o, d_proj, g_gn_g, g_gn_b, g_r_k) = _rowwise(
        post_bwd, post_rows + [d_ymg, d_yrg], post_params,
        [(RW, F32), (RW, F32), (RW, F32), (RW, F32), (MW, F32), (MW + RW, BF16, (d_proj, d_in, lay["z_m"][0]))],
        [(1, RW)] * 3, tile=T // 2, name="post_bwd")

    dscan = _rwkv_scan_bwd(ops_f, ops_b, st_f, st_b, d_y, RW, name="scan_bwd")
    dsc = {"f": dscan[:6], "b": dscan[6:]}

    d_q_att, d_k_att, d_v_att = _attention_bwd(qfull, kv, kr, o_mla, lse, d_o, hm, scale, tq=2 * T, name="attn_bwd")

    def rope_bwd(qraw_b, kr_in, cos_b, sin_b, dq_b, dk_b, dv_b, rot_b, rot_t_b):
        _, vjp = jax.vjp(lambda q_, k_: _f_rope(hm, q_, k_, cos_b, sin_b, rot_b, rot_t_b), qraw_b, kr_in)
        dkn = jnp.concatenate([dk_b[:, hh * QHEAD:hh * QHEAD + NOPE] for hh in range(hm)], axis=1)
        dkr = dk_b[:, NOPE:QHEAD]
        for hh in range(1, hm):
            dkr = dkr + dk_b[:, hh * QHEAD + NOPE:(hh + 1) * QHEAD]
        d_qraw, d_kr_in = vjp((dq_b, dkr))
        return d_qraw, jnp.concatenate([dkn, dv_b], axis=1), d_kr_in

    d_qraw, d_kv, d_kr_in = _rowwise(rope_bwd, [qraw, kr_view, cosx, sinx, d_q_att, d_k_att, d_v_att],
                                     [rot, rot_t], [(hm * QHEAD, BF16), (2 * MW, BF16), (LANES, F32)], tile=T,
                                     name="rope_bwd")
    d_qnorm = _mm(d_qraw, W["wq_b_t"], name="d_qn")
    d_kvnorm = _mm(d_kv, W["wkv_b"], tb=True, name="d_kvn")
    g_wq_b = _mm(d_qraw, qn, ta=True, out_dtype=BF16, name="g_wq_b")
    g_wkv_b = _mm(kvn, d_kv, ta=True, out_dtype=BF16, name="g_wkv_b")

    def mla_norm_bwd(q_a, kv_a, qg, kvg, dq, dk):
        _, vjp = jax.vjp(_f_mla_norm, q_a, kv_a, qg, kvg)
        d_q_a, d_kv_a, d_qg, d_kvg = vjp((dq, dk))
        return jnp.concatenate([d_q_a, d_kv_a], axis=1), d_qg, d_kvg

    d_proj, g_q_norm, g_kv_norm = _rowwise(
        lambda q_a, kv_a, dq, dk, qg, kvg: mla_norm_bwd(q_a, kv_a, qg, kvg, dq, dk),
        [col(proj, "q_a"), col(proj, "kv_a"), d_qnorm, d_kvnorm], [W["mla_q_norm"], W["mla_kv_norm"]],
        [(QR + KVR, BF16, (d_proj, d_in, lay["q_a"][0]))], [(1, QR), (1, KVR)], tile=T, name="mla_norm_bwd")

    def pre_bwd(k_b_, tail_b, dlwf, dlwb, dkf, dkb, dkbon, daf, dab, dbf, dbb, drf, drb, drbon, dvf, dvb, dvbon,
                dkr, *params):
        w2, a2 = params[6], params[7]
        nt, tn = (((1,), (1,)), ((), ())), (((0,), (0,)), ((), ()))
        split = w2.shape[0]
        th = jnp.tanh(tail_b[:, :split])
        th_b, tail_h = th.astype(BF16), tail_b[:, split:].astype(BF16)
        zw = jnp.dot(th_b, w2, preferred_element_type=F32)
        za = jnp.dot(tail_h, a2, preferred_element_type=F32)
        _, vjp = jax.vjp(functools.partial(_f_rwkv_core, RW), k_b_, zw, za, *params[:6], params[8], params[9])
        g = vjp((dlwf, dlwb, dkf + dkbon, dkb + dkbon, daf + dab, dbf, dbb))
        d_zw, d_za = g[1].astype(BF16), g[2].astype(BF16)
        d_tail = (jnp.concatenate([lax.dot_general(d_zw, w2, nt, preferred_element_type=F32) * (1.0 - th * th),
                                   lax.dot_general(d_za, a2, nt, preferred_element_type=F32)], axis=1)
                  + jnp.concatenate([dkr, jnp.zeros((dkr.shape[0], TAIL - LANES), F32)], axis=1))
        g_w2 = lax.dot_general(th_b, d_zw, tn, preferred_element_type=F32)
        g_a2 = lax.dot_general(tail_h, d_za, tn, preferred_element_type=F32)
        d_rl = jnp.concatenate([drf + drb + drbon, g[0], dvf + dvb + dvbon, d_tail], axis=1)
        return (d_rl,) + tuple(g[3:9]) + (g_w2, g_a2)

    f_, b_ = dsc["f"], dsc["b"]
    pre_bwd_rows = [rl_k, rl_tail, f_[1], b_[1], f_[2], b_[2], d_k_bonus, f_[4], b_[4], f_[5], b_[5],
                    f_[0], b_[0], d_r_bonus, f_[3], b_[3], d_v_bonus, d_kr_in]
    (d_rl, g_w0_f, g_w0_b, g_a0_f, g_a0_b, g_k_k, g_k_a, g_w2cat, g_a2cat) = _rowwise(
        pre_bwd, pre_bwd_rows, pre_params, [(3 * RW + TAIL, F32)],
        [(1, RW)] * 6 + [W["w2cat"].shape, W["a2cat"].shape], tile=T // 2, name="rwkv_pre_bwd")
    d_proj, g_mu = _shift_lerp(shift_view, W["mu"], d_rl, (d_proj, lay["r"][0]), name="shift_bwd")
    small = dict(wq_b=g_wq_b, wkv_b=g_wkv_b, w2cat=g_w2cat, a2cat=g_a2cat, w_br_mla=g_w_br_mla,
                 w_br_rwkv=g_w_br_rwkv, w_out=g_w_out)
    if exchange is None:
        received = None
        g_w_in = _mm(d_proj, h, ta=True, out_dtype=BF16, tn_cap=1024, name="g_w_in")
        d_h = _mm(d_proj, W["w_in_t"], tn_cap=1024, name="d_h")
    else:
        slabs = _restore_rest(small, dims)
        slabs = [slabs[n] for n in _MATS[1:]]
        g_w_in, *got = _mm(d_proj, h, ta=True, out_dtype=BF16, tn_cap=1024, ride=_sibling_swap_plan(slabs),
                           name="g_w_in")
        sums = [_pair_add(exchange[1], s, t, name="pair_add_" + n) for n, s, t in zip(_MATS[1:], slabs, got)]
        g_w_in = _restore_w_in(g_w_in, dims)
        d_h, *received = _mm(d_proj, W["w_in_t"], tn_cap=1024, name="d_h",
                             ride=_join_plans(_chip_exchange_plan(sums), _sibling_swap_plan([g_w_in])))
        small = {}

    def pre_norm_bwd(xb, dyb, dhb, g):
        _, vjp = jax.vjp(_rms, xb, g)
        dx, dg = vjp(dhb)
        return dyb + dx, dg

    grad_x, g_g_pre = _rowwise(pre_norm_bwd, [x, dy, d_h], [W["g_pre"]], [(D, F32)], [(1, D)], tile=T,
                               name="pre_norm_bwd")

    grads = dict(g_pre=g_g_pre, w_in=g_w_in, mla_q_norm=g_q_norm, mla_kv_norm=g_kv_norm, mu=g_mu, w0_f=g_w0_f,
                 w0_b=g_w0_b, a0_f=g_a0_f, a0_b=g_a0_b, k_k=g_k_k, k_a=g_k_a, r_k=g_r_k, gn_g=g_gn_g, gn_b=g_gn_b,
                 g_post=g_g_post, **small)
    return loss[0, 0], grad_x, grads, received


_MATS = ["w_in", "mla_wq_b", "mla_wkv_b", "rwkv_w2_f", "rwkv_w2_b", "rwkv_a2_f", "rwkv_a2_b", "w_br_mla",
         "w_br_rwkv", "w_out"]
_ROW_SHARDED = ("w_out",)
_TRANSPOSED = ("w_in", "mla_wq_b")
_VECS = ["g_pre", "mla_q_norm", "mla_kv_norm", "rwkv_mu", "rwkv_w0_f", "rwkv_w0_b", "rwkv_a0_f", "rwkv_a0_b",
         "rwkv_k_k", "rwkv_k_a", "rwkv_r_k", "rwkv_gn_g", "rwkv_gn_b", "g_post"]
_WEIGHTS = ["g_pre", "w_in", "mla_q_norm", "mla_wq_b", "mla_kv_norm", "mla_wkv_b", "rwkv_mu", "rwkv_w0_f",
            "rwkv_w2_f", "rwkv_w0_b", "rwkv_w2_b", "rwkv_a0_f", "rwkv_a2_f", "rwkv_a0_b", "rwkv_a2_b", "rwkv_k_k",
            "rwkv_k_a", "rwkv_r_k", "rwkv_gn_g", "rwkv_gn_b", "w_br_mla", "w_br_rwkv", "w_out", "g_post"]

def _direct_gather_plan(src):
    def phases(src_refs, out_refs, sem_refs):
        (src_ref,), (out_ref,), sems, local_sem = src_refs, out_refs, sem_refs[:2], sem_refs[2]
        x, y, c = lax.axis_index("x"), lax.axis_index("y"), lax.axis_index("c")
        me = 4 * x + 2 * y + c
        flip = lambda v, bit: (1 - v) if bit else v
        peers = [(flip(x, d & 4), flip(y, d & 2), flip(c, d & 1)) for d in range(1, N_DEV)]
        own = lambda: pltpu.make_async_copy(src_ref, out_ref.at[me], local_sem)
        send = lambda d: _remote(src_ref, out_ref.at[me], sems, d, peers[d])

        def first():
            own().start()
            for d in range(N_DEV - 1):
                send(d).start()

        def last():
            for d, (px, py, pc) in enumerate(peers):
                blk = out_ref.at[4 * px + 2 * py + pc]
                _remote(blk, blk, sems, d, (x, y, c)).wait_recv()
            for d in range(N_DEV - 1):
                send(d).wait_send()
            own().wait()

        return first, (lambda: None), last

    return [src], [jax.ShapeDtypeStruct((N_DEV,) + src.shape, src.dtype)], [(N_DEV - 1,), (N_DEV - 1,), ()], phases


def _remote(src, dst, sems, key, to):
    send_sems, recv_sems = sems
    return pltpu.make_async_remote_copy(src_ref=src, dst_ref=dst, send_sem=send_sems.at[key], recv_sem=recv_sems.at[key],
                                        device_id=to, device_id_type=pl.DeviceIdType.MESH)


def _run_exchange(plan, *, name):
    srcs, out_shapes, sem_shapes, phases = plan
    n, m = len(srcs), len(out_shapes)

    def body(*refs):
        for phase in phases(refs[:n], refs[n:n + m], refs[n + m:]):
            phase()

    return pl.pallas_call(
        body, name=name, out_shape=out_shapes,
        in_specs=[pl.BlockSpec(memory_space=pl.ANY)] * n, out_specs=[pl.BlockSpec(memory_space=pl.ANY)] * m,
        scratch_shapes=[pltpu.SemaphoreType.DMA(s) for s in sem_shapes],
    )(*srcs)


def _join_plans(p, q):
    (srcs_p, outs_p, sems_p, phases_p), (srcs_q, outs_q, sems_q, phases_q) = p, q

    def phases(src_refs, out_refs, sem_refs):
        a = phases_p(src_refs[:len(srcs_p)], out_refs[:len(outs_p)], sem_refs[:len(sems_p)])
        b = phases_q(src_refs[len(srcs_p):], out_refs[len(outs_p):], sem_refs[len(sems_p):])

        def both(fa, fb):
            def run():
                fa()
                fb()
            return run

        return tuple(both(fa, fb) for fa, fb in zip(a, b))

    return list(srcs_p) + list(srcs_q), list(outs_p) + list(outs_q), list(sems_p) + list(sems_q), phases


def _gather_plan(srcs):
    n = len(srcs)

    def phases(src_refs, out_refs, sem_refs):
        sems, local_sems = sem_refs[:2], sem_refs[2]
        x, y, c = lax.axis_index("x"), lax.axis_index("y"), lax.axis_index("c")
        idx = lambda px, py, pc: 4 * px + 2 * py + pc
        me, sibling = (x, y, c), (x, y, 1 - c)
        chips = [(1 - x, y), (x, 1 - y), (1 - x, 1 - y)]
        own = lambda a: pltpu.make_async_copy(src_refs[a], out_refs[a].at[idx(*me)], local_sems.at[a])
        to_sibling = lambda a: _remote(src_refs[a], out_refs[a].at[idx(*me)], sems, (0, a), sibling)
        to_chip = lambda a, j: _remote(src_refs[a], out_refs[a].at[idx(*me)], sems, (1 + j, a), (*chips[j], c))
        landed = lambda a, j: out_refs[a].at[idx(*chips[j], c)]
        passed_on = lambda a, j: _remote(landed(a, j), landed(a, j), sems, (4 + j, a), sibling)

        def first():
            for a in range(n):
                own(a).start()
                to_sibling(a).start()
                for j in range(3):
                    to_chip(a, j).start()

        def middle():
            for j in range(3):
                for a in range(n):
                    _remote(landed(a, j), landed(a, j), sems, (1 + j, a), me).wait_recv()
                    passed_on(a, j).start()

        def last():
            for a in range(n):
                blk = out_refs[a].at[idx(*sibling)]
                _remote(blk, blk, sems, (0, a), me).wait_recv()
                for j in range(3):
                    blk = out_refs[a].at[idx(*chips[j], 1 - c)]
                    _remote(blk, blk, sems, (4 + j, a), me).wait_recv()
            for a in range(n):
                to_sibling(a).wait_send()
                for j in range(3):
                    to_chip(a, j).wait_send()
                    passed_on(a, j).wait_send()
                own(a).wait()

        return first, middle, last

    return srcs, [jax.ShapeDtypeStruct((N_DEV,) + s.shape, s.dtype) for s in srcs], [(7, n), (7, n), (n,)], phases


def _sibling_swap_plan(srcs):
    n = len(srcs)

    def phases(src_refs, out_refs, sems):
        x, y, c = lax.axis_index("x"), lax.axis_index("y"), lax.axis_index("c")
        copies = lambda: [_remote(src_refs[a].at[2 * q + 1 - c], out_refs[a].at[q], sems, (q, a), (x, y, 1 - c))
                          for a in range(n) for q in range(4)]

        def first():
            for cp in copies():
                cp.start()

        def last():
            for cp in copies():
                cp.wait()

        return first, (lambda: None), last

    return srcs, [jax.ShapeDtypeStruct((4,) + s.shape[1:], s.dtype) for s in srcs], [(4, n), (4, n)], phases


def _chip_exchange_plan(srcs):
    n = len(srcs)

    def phases(src_refs, out_refs, sem_refs):
        sems, local_sems = sem_refs[:2], sem_refs[2]
        x, y, c = lax.axis_index("x"), lax.axis_index("y"), lax.axis_index("c")
        mine = 2 * x + y
        chips = [(1 - x, y), (x, 1 - y), (1 - x, 1 - y)]
        own = lambda a: pltpu.make_async_copy(src_refs[a].at[mine], out_refs[a].at[mine], local_sems.at[a])
        send = lambda a, j: _remote(src_refs[a].at[2 * chips[j][0] + chips[j][1]], out_refs[a].at[mine], sems, (j, a),
                                    (*chips[j], c))

        def first():
            for a in range(n):
                own(a).start()
                for j in range(3):
                    send(a, j).start()

        def last():
            for j in range(3):
                for a in range(n):
                    blk = out_refs[a].at[2 * chips[j][0] + chips[j][1]]
                    _remote(blk, blk, sems, (j, a), (x, y, c)).wait_recv()
            for a in range(n):
                for j in range(3):
                    send(a, j).wait_send()
                own(a).wait()

        return first, (lambda: None), last

    return srcs, [jax.ShapeDtypeStruct(s.shape, s.dtype) for s in srcs], [(3, n), (3, n), (n,)], phases


def _pair_add(core, g, got, *, name):
    q, r, c = got.shape
    tr, tc = _tile2d(r, c)

    def body(core_ref, a_ref, b_ref, o_ref):
        o_ref[...] = (a_ref[...].astype(F32) + b_ref[...].astype(F32)).astype(BF16)

    blk = pl.BlockSpec((1, tr, tc), lambda i, j, k, core_ref: (i, j, k))
    mine = pl.BlockSpec((1, tr, tc), lambda i, j, k, core_ref: (2 * i + core_ref[0], j, k))
    return pl.pallas_call(
        body, name=name, out_shape=jax.ShapeDtypeStruct(got.shape, BF16),
        grid_spec=pltpu.PrefetchScalarGridSpec(num_scalar_prefetch=1, grid=(q, r // tr, c // tc),
                                               in_specs=[mine, blk], out_specs=blk),
        compiler_params=_cparams(("parallel", "parallel", "parallel")))(core, g, got)


def _adamw(recv, w, m, v, *, name):
    r, c = w.shape
    n_terms = recv.shape[0]
    tr, tc = _tile2d(r, c)

    def body(g_ref, w_ref, m_ref, v_ref, go_ref, d_ref, mo_ref, vo_ref):
        g = g_ref[0].astype(F32)
        for k in range(1, n_terms):
            g = g + g_ref[k].astype(F32)
        m_new = ADAM_B1 * m_ref[...] + (1.0 - ADAM_B1) * g
        v_new = ADAM_B2 * v_ref[...] + (1.0 - ADAM_B2) * (g * g)
        m_hat = m_new / (1.0 - ADAM_B1 ** ADAM_STEP)
        v_hat = v_new / (1.0 - ADAM_B2 ** ADAM_STEP)
        go_ref[...] = g
        d_ref[...] = -ADAM_LR * (m_hat / (jnp.sqrt(v_hat) + ADAM_EPS) + ADAM_WD * w_ref[...])
        mo_ref[...] = m_new
        vo_ref[...] = v_new

    blk = pl.BlockSpec((tr, tc), lambda i, j: (i, j))
    return pl.pallas_call(
        body, name=name, grid=(r // tr, c // tc),
        in_specs=[pl.BlockSpec((n_terms, tr, tc), lambda i, j: (0, i, j)), blk, blk, blk], out_specs=[blk] * 4,
        out_shape=[jax.ShapeDtypeStruct((r, c), F32)] * 4, compiler_params=_cparams(("parallel", "parallel")),
    )(recv, w, m, v)


def _tile2d(r, c, cap=256):
    if r <= cap:
        return r, c
    for t in range(cap, 0, -BF16_ROWS):
        if r % t == 0:
            return t, c
    return r, _pick(c, cap)


def _pack(pieces):
    total = sum(p.shape[0] for p in pieces)
    pad = (-total) % (8 * LANES)
    flat = jnp.concatenate(list(pieces) + [jnp.zeros((pad,), F32)])
    return flat.reshape(-1, LANES)


def _unpack(flat, sizes):
    flat = flat.reshape(-1)
    out, o = [], 0
    for n in sizes:
        out.append(flat[o:o + n])
        o += n
    return out


def _prepare_weights(full, vec, dims):
    rest = {n: t for n, t in full.items() if n != "w_in"}
    return {"w_in_t": _prepare_w_in(full["w_in"], dims), **_prepare_rest(rest, dims), **_prepare_vectors(vec, dims)}


def _prepare_w_in(slabs, dims):
    D = dims["D"]
    c = slabs.shape[1]
    parts, pos = [], 0
    for orig_off, width, perm_off in sorted(dims["segs"], key=lambda t: t[2]):
        if perm_off > pos:
            parts.append(jnp.zeros((perm_off - pos, D), BF16))
        for k in range(N_DEV):
            lo, hi = max(orig_off, k * c), min(orig_off + width, (k + 1) * c)
            if lo < hi:
                parts.append(slabs[k][lo - k * c:hi - k * c])
        pos = perm_off + width
    if dims["d_in_perm"] > pos:
        parts.append(jnp.zeros((dims["d_in_perm"] - pos, D), BF16))
    return jnp.concatenate(parts, axis=0)


def _prepare_rest(full, dims):
    hm, hr, hn, rank = dims["hm"], dims["hr"], dims["hn"], dims["rank"]
    QR, KVR = dims["QR"], dims["KVR"]
    RW, TAIL = hr * hn, dims["TAIL"]
    full = {n: (t.reshape(-1, t.shape[2]) if n in _ROW_SHARDED + _TRANSPOSED
                else t.transpose(1, 0, 2).reshape(t.shape[1], -1)) for n, t in full.items()}
    wq = full["mla_wq_b"].reshape(hm, NOPE + ROPE, QR)
    wq = jnp.concatenate([wq, jnp.zeros((hm, QHEAD - NOPE - ROPE, QR), BF16)], axis=1).reshape(hm * QHEAD, QR)
    wkv = full["mla_wkv_b"].reshape(KVR, hm, 2, NOPE).transpose(0, 2, 1, 3).reshape(KVR, 2 * hm * NOPE)
    z = lambda rows: jnp.zeros((rows, RW), BF16)
    f = lambda nme: full[nme]
    split = ROPE + 2 * rank
    assert split % LANES == 0, split
    w2cat = jnp.concatenate([
        jnp.concatenate([z(ROPE), f("rwkv_w2_f"), z(rank)], axis=0),
        jnp.concatenate([z(ROPE + rank), f("rwkv_w2_b")], axis=0)], axis=1)
    a2cat = jnp.concatenate([
        jnp.concatenate([f("rwkv_a2_f"), z(TAIL - split - rank)], axis=0),
        jnp.concatenate([z(rank), f("rwkv_a2_b"), z(TAIL - split - 2 * rank)], axis=0)], axis=1)
    return dict(wq_b_t=wq, wkv_b=wkv, w2cat=w2cat, a2cat=a2cat, w_br_mla=full["w_br_mla"],
                w_br_rwkv=full["w_br_rwkv"], w_out=full["w_out"])


def _prepare_vectors(vec, dims):
    rank, RW, TAIL = dims["rank"], dims["hr"] * dims["hn"], dims["TAIL"]
    mu = vec["rwkv_mu"]
    mu_p = jnp.concatenate([mu[:3 * RW], jnp.zeros((ROPE,), F32), mu[3 * RW:],
                            jnp.zeros((TAIL - ROPE - 4 * rank,), F32)])
    row = lambda t: t.reshape(1, -1)
    return dict(
        mu=row(mu_p), g_pre=row(vec["g_pre"]), g_post=row(vec["g_post"]), mla_q_norm=row(vec["mla_q_norm"]),
        mla_kv_norm=row(vec["mla_kv_norm"]), w0_f=row(vec["rwkv_w0_f"]), w0_b=row(vec["rwkv_w0_b"]),
        a0_f=row(vec["rwkv_a0_f"]), a0_b=row(vec["rwkv_a0_b"]), k_k=row(vec["rwkv_k_k"]), k_a=row(vec["rwkv_k_a"]),
        r_k=row(vec["rwkv_r_k"]), gn_g=row(vec["rwkv_gn_g"]), gn_b=row(vec["rwkv_gn_b"]))


def _restore_grads(g, dims):
    return {"w_in": _restore_w_in(g["w_in"], dims), **_restore_rest(g, dims), **_restore_vectors(g, dims)}


def _restore_w_in(gw, dims):
    c = dims["d_in"] // N_DEV
    slabs = []
    for k in range(N_DEV):
        parts = []
        for orig_off, width, perm_off in sorted(dims["segs"]):
            lo_, hi_ = max(orig_off, k * c), min(orig_off + width, (k + 1) * c)
            if lo_ < hi_:
                parts.append(gw[perm_off + lo_ - orig_off:perm_off + hi_ - orig_off])
        slabs.append(jnp.concatenate(parts, axis=0))
    return jnp.stack(slabs)


def _restore_rest(g, dims):
    hm, hr, hn, rank = dims["hm"], dims["hr"], dims["hn"], dims["rank"]
    QR, KVR, RW = dims["QR"], dims["KVR"], hr * hn
    wq = g["wq_b"].reshape(hm, QHEAD, QR)[:, :NOPE + ROPE].reshape(N_DEV, -1, QR)
    wkv = g["wkv_b"].reshape(KVR, 2, hm, NOPE).transpose(0, 2, 1, 3).reshape(KVR, 2 * hm * NOPE)
    lo = lambda t, first, half: t[first:first + rank, half * RW:(half + 1) * RW].astype(BF16)
    cols = lambda t: t.reshape(t.shape[0], N_DEV, -1).transpose(1, 0, 2)
    return dict(
        mla_wq_b=wq, mla_wkv_b=cols(wkv), rwkv_w2_f=cols(lo(g["w2cat"], ROPE, 0)),
        rwkv_w2_b=cols(lo(g["w2cat"], ROPE + rank, 1)), rwkv_a2_f=cols(lo(g["a2cat"], 0, 0)),
        rwkv_a2_b=cols(lo(g["a2cat"], rank, 1)), w_br_mla=cols(g["w_br_mla"]), w_br_rwkv=cols(g["w_br_rwkv"]),
        w_out=g["w_out"].reshape(N_DEV, -1, g["w_out"].shape[1]))


def _restore_vectors(g, dims):
    rank, RW = dims["rank"], dims["hr"] * dims["hn"]
    mu = g["mu"][0]
    out = dict(
        rwkv_mu=jnp.concatenate([mu[:3 * RW], mu[3 * RW + ROPE:3 * RW + ROPE + 4 * rank]]),
        g_pre=g["g_pre"][0], g_post=g["g_post"][0], mla_q_norm=g["mla_q_norm"][0], mla_kv_norm=g["mla_kv_norm"][0],
        rwkv_w0_f=g["w0_f"][0], rwkv_w0_b=g["w0_b"][0], rwkv_a0_f=g["a0_f"][0], rwkv_a0_b=g["a0_b"][0],
        rwkv_k_k=g["k_k"][0], rwkv_k_a=g["k_a"][0], rwkv_r_k=g["r_k"][0], rwkv_gn_g=g["gn_g"][0],
        rwkv_gn_b=g["gn_b"][0])
    return out


def _dims(inp):
    D = inp["x"].shape[-1]
    QR, KVR = inp["mla_q_norm"].shape[0], inp["mla_kv_norm"].shape[0]
    hm = inp["mla_wq_b"].shape[1] * N_DEV // (NOPE + ROPE)
    hr, hn = inp["rwkv_r_k"].shape
    rank = inp["rwkv_w2_f"].shape[0]
    MW, RW = hm * VDIM, hr * hn
    TAIL = -(-(ROPE + 4 * rank) // LANES) * LANES
    orig, o = {}, 0
    for nme, w in (("q_a", QR), ("kv_a", KVR), ("k_rope", ROPE), ("rkv", 3 * RW), ("lora", 4 * rank), ("z_m", MW),
                   ("z_r", RW), ("gate_m", D), ("gate_r", D)):
        orig[nme] = (o, w)
        o += w
    assert o == inp["w_in"].shape[1] * N_DEV
    lay, d_in_perm = _layout(D, MW, RW, TAIL, QR, KVR)
    perm_off = dict(q_a=lay["q_a"][0], kv_a=lay["kv_a"][0], k_rope=lay["tail"][0], rkv=lay["r"][0],
                    lora=lay["tail"][0] + ROPE, z_m=lay["z_m"][0], z_r=lay["z_r"][0], gate_m=lay["gate_m"][0],
                    gate_r=lay["gate_r"][0])
    segs = [(orig[nme][0], orig[nme][1], perm_off[nme]) for nme in orig]
    return dict(D=D, QR=QR, KVR=KVR, hm=hm, hr=hr, hn=hn, rank=rank, TAIL=TAIL, segs=segs, d_in=o,
                d_in_perm=d_in_perm)


def kernel(x, g_pre, w_in, mla_q_norm, mla_wq_b, mla_kv_norm, mla_wkv_b, rwkv_mu, rwkv_w0_f, rwkv_w2_f, rwkv_w0_b, rwkv_w2_b, rwkv_a0_f, rwkv_a2_f, rwkv_a0_b, rwkv_a2_b, rwkv_k_k, rwkv_k_a, rwkv_r_k, rwkv_gn_g, rwkv_gn_b, w_br_mla, w_br_rwkv, w_out, g_post, loss_target, m_g_pre, m_w_in, m_mla_q_norm, m_mla_wq_b, m_mla_kv_norm, m_mla_wkv_b, m_rwkv_mu, m_rwkv_w0_f, m_rwkv_w2_f, m_rwkv_w0_b, m_rwkv_w2_b, m_rwkv_a0_f, m_rwkv_a2_f, m_rwkv_a0_b, m_rwkv_a2_b, m_rwkv_k_k, m_rwkv_k_a, m_rwkv_r_k, m_rwkv_gn_g, m_rwkv_gn_b, m_w_br_mla, m_w_br_rwkv, m_w_out, m_g_post, v_g_pre, v_w_in, v_mla_q_norm, v_mla_wq_b, v_mla_kv_norm, v_mla_wkv_b, v_rwkv_mu, v_rwkv_w0_f, v_rwkv_w2_f, v_rwkv_w0_b, v_rwkv_w2_b, v_rwkv_a0_f, v_rwkv_a2_f, v_rwkv_a0_b, v_rwkv_a2_b, v_rwkv_k_k, v_rwkv_k_a, v_rwkv_r_k, v_rwkv_gn_g, v_rwkv_gn_b, v_w_br_mla, v_w_br_rwkv, v_w_out, v_g_post):
    inp = dict(locals())
    dims = _dims(inp)
    stored = lambda t, n: t.T if n in _TRANSPOSED else t
    assert _MATS[0] == "w_in"
    shards = [stored(inp[n], n).astype(BF16) for n in _MATS]
    core = lax.axis_index("c").astype(jnp.int32).reshape(1)
    (w_in_slabs,) = _run_exchange(_gather_plan(shards[:1]), name="gather_w_in")
    W = {"w_in_t": _prepare_w_in(w_in_slabs, dims), **_prepare_vectors({n: inp[n] for n in _VECS}, dims)}
    loss, grad_x, g, recv_rest = _local_grads(x[0], loss_target[0], W, dims, exchange=(shards[1:], core))

    new = {}
    *recv_rest, got = recv_rest
    g_w_in, g = g["w_in"], _restore_vectors(g, dims)
    vsizes = [inp[n].size for n in _VECS] + [1]
    vflat = lambda prefix, src, last: _pack([src[prefix + n].reshape(-1) for n in _VECS] + [last])
    one = jnp.zeros((1,), F32)
    recv_w_in, vrecv = _run_exchange(
        _join_plans(_chip_exchange_plan([_pair_add(core, g_w_in, got, name="pair_add_w_in")]),
                    _direct_gather_plan(vflat("", g, loss.reshape(1)))), name="scatter_w_in")
    for n, t in zip(_MATS, [recv_w_in] + recv_rest):
        out = _adamw(t, stored(inp[n], n), stored(inp["m_" + n], n), stored(inp["v_" + n], n), name="adamw_" + n)
        new[n] = [stored(o, n) for o in out]

    vout = _adamw(vrecv, vflat("", inp, one), vflat("m_", inp, one), vflat("v_", inp, one), name="adamw_vectors")
    vparts = [_unpack(t, vsizes) for t in vout]
    for i, n in enumerate(_VECS):
        new[n] = [vp[i].reshape(inp[n].shape) for vp in vparts]
    loss = vparts[0][-1].reshape(())

    outs = [loss, grad_x[None]]
    for k in range(4):
        outs += [new[n][k] for n in _WEIGHTS]
    return tuple(outs)
```

```python
import functools
import math

import jax
import jax.numpy as jnp
from jax import lax
from jax.experimental import pallas as pl
from jax.experimental.pallas import tpu as pltpu

F32 = jnp.float32
BF16 = jnp.bfloat16

N_DEV = 8
LANES = 128
BF16_ROWS = 16
NOPE, ROPE, VDIM = 128, 64, 128
QHEAD = 256
ROPE_THETA = 10000.0
NORM_EPS = 1e-6
GN_EPS = 64e-5
CHUNK = 64
SUB = 16
VMEM_LIMIT = 56 * 1024 * 1024

ADAM_LR, ADAM_B1, ADAM_B2, ADAM_EPS, ADAM_WD, ADAM_STEP = 0.001, 0.9, 0.999, 1e-08, 0.01, 10


def _cparams(sem):
    return pltpu.CompilerParams(dimension_semantics=sem, vmem_limit_bytes=VMEM_LIMIT)


def _pick(n, cap):
    if n <= cap:
        return n
    for t in range(cap - cap % LANES, 0, -LANES):
        if n % t == 0:
            return t
    raise ValueError(f"no tile for {n} under {cap}")


def _mm(a, b, *, ta=False, tb=False, out_dtype=F32, name, tm_cap=1024, tn_cap=512, tk_cap=2048, ride=None):
    K, M = a.shape if ta else a.shape[::-1]
    N = b.shape[0] if tb else b.shape[1]
    assert (b.shape[1] if tb else b.shape[0]) == K, (a.shape, b.shape, ta, tb)
    tm, tn, tk = _pick(M, tm_cap), _pick(N, tn_cap), _pick(K, tk_cap)
    nj, nk = N // tn, K // tk
    steps = (M // tm) * nj * nk
    dn = (((0 if ta else 1,), (1 if tb else 0,)), ((), ()))
    srcs, extra_shapes, sem_shapes, phases = ride if ride else ((), (), (), None)
    n_src, n_extra = len(srcs), len(extra_shapes)

    def body(*refs):
        a_ref, b_ref, o_ref = refs[0], refs[1], refs[2 + n_src]
        acc_ref = refs[3 + n_src + n_extra]
        k = pl.program_id(2)
        if ride:
            step = (pl.program_id(0) * nj + pl.program_id(1)) * nk + k
            first, middle, last = phases(refs[2:2 + n_src], refs[3 + n_src:3 + n_src + n_extra],
                                         refs[4 + n_src + n_extra:])
            pl.when(step == 0)(first)
            pl.when(step == (steps * 7) // 8)(middle)
        p = lax.dot_general(a_ref[...], b_ref[...], dn, preferred_element_type=F32)

        @pl.when(k == 0)
        def _():
            acc_ref[...] = p

        @pl.when(k > 0)
        def _():
            acc_ref[...] += p

        @pl.when(k == nk - 1)
        def _():
            o_ref[...] = acc_ref[...].astype(out_dtype)

        if ride:
            pl.when(step == steps - 1)(last)

    a_spec = pl.BlockSpec((tk, tm), lambda i, j, k: (k, i)) if ta else pl.BlockSpec((tm, tk), lambda i, j, k: (i, k))
    b_spec = pl.BlockSpec((tn, tk), lambda i, j, k: (j, k)) if tb else pl.BlockSpec((tk, tn), lambda i, j, k: (k, j))
    hbm = pl.BlockSpec(memory_space=pl.ANY)
    out = pl.pallas_call(
        body, name=name, grid=(M // tm, nj, nk),
        in_specs=[a_spec, b_spec] + [hbm] * n_src,
        out_specs=[pl.BlockSpec((tm, tn), lambda i, j, k: (i, j))] + [hbm] * n_extra,
        out_shape=[jax.ShapeDtypeStruct((M, N), out_dtype)] + list(extra_shapes),
        scratch_shapes=[pltpu.VMEM((tm, tn), F32)] + [pltpu.SemaphoreType.DMA(s) for s in sem_shapes],
        compiler_params=_cparams(("arbitrary",) * 3 if ride else ("parallel", "parallel", "arbitrary")),
    )(a, b, *srcs)
    return out if ride else out[0]


def _view(arr, off, width):
    assert off % width == 0, (off, width)
    return (arr, off // width, width)


def _rowwise(fn, rows, params, out_rows, out_accs=(), *, tile, name):
    rows = [r if isinstance(r, tuple) else (r, 0, r.shape[1]) for r in rows]
    S = rows[0][0].shape[0]
    T = min(tile, S)
    assert S % T == 0
    n_rows, n_par, n_out = len(rows), len(params), len(out_rows)
    into = [o[2] if len(o) == 3 else None for o in out_rows]
    carried = [t[0] for t in into if t is not None and t[0] is not None]

    def body(*refs):
        ins = [r[...] for r in refs[:n_rows + n_par]]
        outs = fn(*ins)
        out_refs = refs[n_rows + n_par + len(carried):]
        for o_ref, val in zip(out_refs[:n_out], outs[:n_out]):
            o_ref[...] = val.astype(o_ref.dtype)
        i = pl.program_id(0)
        for o_ref, val in zip(out_refs[n_out:], outs[n_out:]):
            @pl.when(i == 0)
            def _(o_ref=o_ref, val=val):
                o_ref[...] = val

            @pl.when(i > 0)
            def _(o_ref=o_ref, val=val):
                o_ref[...] += val

    in_specs = [pl.BlockSpec((T, w), functools.partial(lambda i, cb: (i, cb), cb=cb)) for _, cb, w in rows]
    in_specs += [pl.BlockSpec(p.shape, lambda i: (0, 0)) for p in params]
    in_specs += [pl.BlockSpec(memory_space=pl.ANY)] * len(carried)
    out_specs, out_shape, aliases = [], [], {}
    for k, (o, t) in enumerate(zip(out_rows, into)):
        w, dt = o[0], o[1]
        if t is None:
            out_specs.append(pl.BlockSpec((T, w), lambda i: (i, 0)))
            out_shape.append(jax.ShapeDtypeStruct((S, w), dt))
            continue
        buf, total, first = t
        assert first % w == 0
        out_specs.append(pl.BlockSpec((T, w), functools.partial(lambda i, cb: (i, cb), cb=first // w)))
        out_shape.append(jax.ShapeDtypeStruct((S, total), dt))
        if buf is not None:
            aliases[n_rows + n_par + len(aliases)] = k
    out_specs += [pl.BlockSpec(s, lambda i: (0, 0)) for s in out_accs]
    out_shape += [jax.ShapeDtypeStruct(s, F32) for s in out_accs]
    return pl.pallas_call(
        body, name=name, grid=(S // T,), in_specs=in_specs, out_specs=out_specs, out_shape=out_shape,
        input_output_aliases=aliases, compiler_params=_cparams(("arbitrary",)),
    )(*[r[0] for r in rows], *params, *carried)


def _mm_sel(x, sel):
    hi = x.astype(BF16)
    lo = (x - hi.astype(F32)).astype(BF16)
    d = lambda u: jnp.dot(u, sel, preferred_element_type=F32)
    return d(hi) + d(lo)


@jax.custom_vjp
def _sel(x, sel, sel_t):
    return _mm_sel(x, sel)


def _sel_fwd(x, sel, sel_t):
    return _mm_sel(x, sel), (sel, sel_t)


def _sel_bwd(res, ct):
    sel, sel_t = res
    return _mm_sel(ct, sel_t), jnp.zeros_like(sel), jnp.zeros_like(sel_t)


_sel.defvjp(_sel_fwd, _sel_bwd)


def _rms(x, g):
    return x * lax.rsqrt(jnp.mean(x * x, axis=-1, keepdims=True) + NORM_EPS) * g


def _sigmoid(x):
    return 1.0 / (1.0 + jnp.exp(-x))


def _silu(x):
    return x * _sigmoid(x)


def _softplus(x):
    return jnp.maximum(x, 0.0) + jnp.log(1.0 + jnp.exp(-jnp.abs(x)))


def _f_mla_norm(q_a, kv_a, qg, kvg):
    return _rms(q_a, qg), _rms(kv_a, kvg)


def _f_rope(hm, qraw, kr_in, cosx, sinx, rot, rot_t):
    def rope(t):
        return t * cosx + _sel(t, rot, rot_t) * sinx
    parts = []
    for h in range(hm):
        parts.append(qraw[:, h * QHEAD:h * QHEAD + NOPE])
        parts.append(rope(qraw[:, h * QHEAD + NOPE:(h + 1) * QHEAD]))
    return jnp.concatenate(parts, axis=1), rope(kr_in)


def _f_rwkv_pre(rw, k, tail, w0f, w0b, a0f, a0b, k_k, k_a, w2cat, a2cat, seg, seg_t):
    split = w2cat.shape[0]
    zw = jnp.dot(jnp.tanh(tail[:, :split]).astype(BF16), w2cat, preferred_element_type=F32)
    za = jnp.dot(tail[:, split:].astype(BF16), a2cat, preferred_element_type=F32)
    return _f_rwkv_core(rw, k, zw, za, w0f, w0b, a0f, a0b, k_k, k_a, seg, seg_t)


def _f_rwkv_core(rw, k, zw, za, w0f, w0b, a0f, a0b, k_k, k_a, seg, seg_t):
    lw_f = -jnp.exp(-_softplus(-(w0f + zw[:, :rw])) - 0.5)
    lw_b = -jnp.exp(-_softplus(-(w0b + zw[:, rw:])) - 0.5)
    a_f = _sigmoid(a0f + za[:, :rw])
    a_b = _sigmoid(a0b + za[:, rw:])
    kk = k * k_k
    nrm = jnp.sqrt(_sel(_sel(kk * kk, seg, seg_t), seg_t, seg))
    kk = kk / jnp.maximum(nrm, 1e-12)
    k_f = k * (1.0 + (a_f - 1.0) * k_a)
    k_b = k * (1.0 + (a_b - 1.0) * k_a)
    return lw_f, lw_b, k_f, k_b, -kk, kk * a_f, kk * a_b


def _f_post(hn, y_f, y_b, r, k_f, k_b, v, z_r, o_mla, z_m, gn_g, gn_b, r_k, seg, seg_t):
    segsum = lambda t: _sel(_sel(t, seg, seg_t), seg_t, seg)
    y = y_f + y_b
    mu = segsum(y) * (1.0 / hn)
    yc = y - mu
    var = segsum(yc * yc) * (1.0 / hn)
    yn = yc * lax.rsqrt(var + GN_EPS) * gn_g + gn_b
    bonus = segsum(r * (k_f + k_b) * r_k) * v
    return o_mla * _silu(z_m), (yn + bonus) * _silu(z_r)


def _f_merge(u_m, u_r, g_m, g_r):
    return _sigmoid(g_m) * u_m + _sigmoid(g_r) * u_r


_NN = ((2,), (1,))
_NT = ((2,), (2,))
_TN = ((1,), (1,))

_SCAN_PASSES = {"cum": 2, "gram": 3, "solve": 1, "apply": 1, "state": 1}


def _hdot_raw(passes, x, y, dims):
    dn = (dims, ((0,), (0,)))
    d = lambda p, q: lax.dot_general(p, q, dn, preferred_element_type=F32)
    xh = x.astype(BF16)
    yh = y.astype(BF16)
    if passes == 1:
        return d(xh, yh)
    yl = (y - yh.astype(F32)).astype(BF16)
    if passes == 2:
        return d(xh, yh) + d(xh, yl)
    xl = (x - xh.astype(F32)).astype(BF16)
    return d(xh, yh) + d(xh, yl) + d(xl, yh)


@functools.partial(jax.custom_vjp, nondiff_argnums=(2, 3))
def _hdot_p(x, y, dims, passes):
    return _hdot_raw(passes, x, y, dims)


def _hdot_fwd(x, y, dims, passes):
    return _hdot_raw(passes, x, y, dims), (x, y)


def _hdot_bwd(dims, passes, res, ct):
    x, y = res
    if dims == _NN:
        return _hdot_raw(passes, ct, y, _NT), _hdot_raw(passes, x, ct, _TN)
    if dims == _NT:
        return _hdot_raw(passes, ct, y, _NN), _hdot_raw(passes, ct, x, _TN)
    return _hdot_raw(passes, y, ct, _NT), _hdot_raw(passes, x, ct, _NN)


_hdot_p.defvjp(_hdot_fwd, _hdot_bwd)


def _hdot(x, y, dims, kind):
    return _hdot_p(x, y, dims, _SCAN_PASSES[kind])


def _tri_solve(n_mat, x, length):
    row = lax.broadcasted_iota(jnp.int32, (length, length), 0)
    col = lax.broadcasted_iota(jnp.int32, (length, length), 1)
    eye = (row == col).astype(F32)[None]
    diag_blk = ((row // SUB) == (col // SUB))[None]
    nd = jnp.where(diag_blk, n_mat, 0.0)
    no = n_mat - nd
    dinv = eye + nd
    p = nd
    for _ in range(int(math.log2(SUB)) - 1):
        p = _hdot(p, p, _NN, "solve")
        dinv = dinv + _hdot(dinv, p, _NN, "solve")
    q = _hdot(dinv, no, _NN, "solve")
    u = _hdot(dinv, x, _NN, "solve")
    levels = int(math.log2(length // SUB))
    qs = [q]
    for _ in range(levels - 1):
        qs.append(_hdot(qs[-1], qs[-1], _NN, "solve"))
    for qk in reversed(qs):
        u = u + _hdot(qk, u, _NN, "solve")
    return u


def _rwkv_chunk(rev, s0, r, lw, k, v, a, b):
    pairs, length, width = r.shape
    hn = width // 2
    row = lax.broadcasted_iota(jnp.int32, (length, length), 0)
    col = lax.broadcasted_iota(jnp.int32, (length, length), 1)
    row2 = lax.broadcasted_iota(jnp.int32, (length, 2 * length), 0)
    col2 = lax.broadcasted_iota(jnp.int32, (length, 2 * length), 1)
    col2 = jnp.where(col2 >= length, col2 - length, col2)
    if rev is None:
        half = pairs // 2
        back = lax.broadcasted_iota(jnp.int32, (pairs, length, length), 0) >= half
        idx2 = lax.broadcasted_iota(jnp.int32, (2 * pairs, length, 2 * length), 0)
        back2 = ((idx2 >= half) & (idx2 < pairs)) | (idx2 >= pairs + half)
        ahead = jnp.where(back, (col - row)[None], (row - col)[None])
        ahead2 = jnp.where(back2, (col2 - row2)[None], (row2 - col2)[None])
        incl, strict2, incl2 = ahead >= 0, ahead2 > 0, ahead2 >= 0
    else:
        incl = ((row <= col) if rev else (row >= col))[None]
        strict2 = ((row2 < col2) if rev else (row2 > col2))[None]
        incl2 = ((row2 <= col2) if rev else (row2 >= col2))[None]
    lane = lax.broadcasted_iota(jnp.int32, (1, 1, width), 2)
    first = lane < hn
    head_mask = jnp.concatenate([jnp.broadcast_to(first.astype(F32), (pairs, 1, width)),
                                 jnp.broadcast_to(1.0 - first.astype(F32), (pairs, 1, width))], axis=0)
    twice = lambda t: jnp.concatenate([t, t], axis=0)
    pick = lambda t: jnp.where(first, t[:pairs], t[pairs:])

    t_incl = jnp.broadcast_to(incl.astype(F32), (pairs, length, length))
    cum = _hdot(t_incl, lw, _NN, "cum")
    g = jnp.exp(cum)
    g_inv = jnp.exp(-cum)
    at = a * jnp.exp(cum - lw)
    rt = r * g
    bt = b * g_inv
    kt = k * g_inv
    lhs = jnp.concatenate([twice(at) * head_mask, twice(rt) * head_mask], axis=1)
    rhs = jnp.concatenate([twice(bt), twice(kt)], axis=1)
    gram = _hdot(lhs, rhs, _NT, "gram")
    top = jnp.where(strict2, gram[:, :length], 0.0)
    bot = jnp.where(incl2, gram[:, length:], 0.0)
    v2 = twice(v)
    zeros = jnp.zeros_like(v2)
    x = _hdot(at, s0, _NT, "apply") + pick(_hdot(top, jnp.concatenate([zeros, v2], axis=1), _NN, "apply"))
    u = pick(_tri_solve(top[:, :, :length], twice(x), length))
    y = _hdot(rt, s0, _NT, "apply") + pick(_hdot(bot, jnp.concatenate([twice(u), v2], axis=1), _NN, "apply"))
    g_last = jnp.exp(jnp.sum(lw, axis=1, keepdims=True))
    ri = lax.broadcasted_iota(jnp.int32, (width, width), 0)
    ci = lax.broadcasted_iota(jnp.int32, (width, width), 1)
    same_head = ((ri < hn) == (ci < hn))[None]
    upd = _hdot(u, bt, _TN, "state") + _hdot(v, kt, _TN, "state")
    s1 = (s0 + jnp.where(same_head, upd, 0.0)) * g_last
    return y, s1


def _split_pairs(x):
    return jnp.stack([x[:, p * LANES:(p + 1) * LANES] for p in range(x.shape[1] // LANES)])


def _merge_pairs(x):
    return jnp.concatenate([x[p] for p in range(x.shape[0])], axis=1)


def _scan_specs(views, rw, nc, rev):
    cidx = (lambda c: nc - 1 - c) if rev else (lambda c: c)
    seqs = [pl.BlockSpec((CHUNK, rw), functools.partial(lambda c, cb: (cidx(c), cb), cb=cb)) for _, cb, _ in views]
    plain = pl.BlockSpec((CHUNK, rw), lambda c: (cidx(c), 0))
    st = pl.BlockSpec((1, rw // LANES, LANES, LANES), lambda c: (cidx(c), 0, 0, 0))
    return seqs, plain, st


def _as_views(arrs, rw):
    return [t if isinstance(t, tuple) else (t, 0, rw) for t in arrs]


def _rwkv_scan_fwd(ops_f, ops_b, rw, *, name):
    S = _as_views(ops_f, rw)[0][0].shape[0]
    nc, pairs = S // CHUNK, rw // LANES
    in_specs, out_specs, arrays = [], [], []
    for rev, ops in ((False, ops_f), (True, ops_b)):
        views = _as_views(ops, rw)
        seqs, plain, st = _scan_specs(views, rw, nc, rev)
        in_specs += seqs
        out_specs += [plain, st]
        arrays += [t[0] for t in views]

    def both(refs_f, refs_b):
        return [jnp.concatenate([_split_pairs(f[...]), _split_pairs(b[...])], axis=0) for f, b in zip(refs_f, refs_b)]

    def body(*refs):
        (y_f, st_f, y_b, st_b), s_ref = refs[12:16], refs[16]

        @pl.when(pl.program_id(0) == 0)
        def _():
            s_ref[...] = jnp.zeros_like(s_ref)

        s0 = s_ref[...]
        st_f[0] = s0[:pairs]
        st_b[0] = s0[pairs:]
        y, s1 = _rwkv_chunk(None, s0, *both(refs[:6], refs[6:12]))
        y_f[...] = _merge_pairs(y[:pairs])
        y_b[...] = _merge_pairs(y[pairs:])
        s_ref[...] = s1

    return pl.pallas_call(
        body, name=name, grid=(nc,), in_specs=in_specs, out_specs=out_specs,
        out_shape=[jax.ShapeDtypeStruct((S, rw), F32), jax.ShapeDtypeStruct((nc, pairs, LANES, LANES), F32)] * 2,
        scratch_shapes=[pltpu.VMEM((2 * pairs, LANES, LANES), F32)],
        compiler_params=_cparams(("arbitrary",)),
    )(*arrays)


def _rwkv_scan_bwd(ops_f, ops_b, states_f, states_b, dy, rw, *, name):
    S = dy.shape[0]
    nc, pairs = S // CHUNK, rw // LANES
    in_specs, arrays = [], []
    for rev, ops, states in ((False, ops_f, states_f), (True, ops_b, states_b)):
        views = _as_views(list(ops) + [dy], rw)
        seqs, plain, st = _scan_specs(views, rw, nc, not rev)
        in_specs += seqs + [st]
        arrays += [t[0] for t in views] + [states]
    out_specs = []
    for rev in (False, True):
        out_specs += [_scan_specs([], rw, nc, not rev)[1]] * 6

    def both(refs_f, refs_b):
        return [jnp.concatenate([_split_pairs(f[...]), _split_pairs(b[...])], axis=0) for f, b in zip(refs_f, refs_b)]

    def body(*refs):
        ds_ref = refs[28]

        @pl.when(pl.program_id(0) == 0)
        def _():
            ds_ref[...] = jnp.zeros_like(ds_ref)

        s0 = jnp.concatenate([refs[7][0], refs[15][0]], axis=0)
        _, vjp = jax.vjp(functools.partial(_rwkv_chunk, None), s0, *both(refs[:6], refs[8:14]))
        (dy,) = both(refs[6:7], refs[14:15])
        grads = vjp((dy, ds_ref[...]))
        ds_ref[...] = grads[0]
        for o_f, o_b, gval in zip(refs[16:22], refs[22:28], grads[1:]):
            o_f[...] = _merge_pairs(gval[:pairs])
            o_b[...] = _merge_pairs(gval[pairs:])

    return pl.pallas_call(
        body, name=name, grid=(nc,), in_specs=in_specs, out_specs=out_specs,
        out_shape=[jax.ShapeDtypeStruct((S, rw), F32)] * 12,
        scratch_shapes=[pltpu.VMEM((2 * pairs, LANES, LANES), F32)],
        compiler_params=_cparams(("arbitrary",)),
    )(*arrays)


def _shift_lerp(x_view, mu, d=None, into=None, *, name):
    arr, off, width = x_view
    S = arr.shape[0]
    cb = _pick(width, 256)
    assert off % cb == 0

    def cshift(t):
        rows = lax.broadcasted_iota(jnp.int32, t.shape, 0)
        prev = jnp.where(rows == 0, 0.0, pltpu.roll(t, 1, 0))
        nxt = jnp.where(rows == S - 1, 0.0, pltpu.roll(t, S - 1, 0))
        return 0.5 * (prev + nxt)

    def fwd_body(x_ref, mu_ref, o_ref):
        x = x_ref[...]
        o_ref[...] = x + mu_ref[...] * (cshift(x) - x)

    def bwd_body(x_ref, mu_ref, d_ref, _, dx_ref, dmu_ref):
        x, m, dd = x_ref[...], mu_ref[...], d_ref[...]
        gm = m * dd
        dx_ref[...] = (dd - gm + cshift(gm)).astype(dx_ref.dtype)
        dmu_ref[...] = jnp.sum(dd * (cshift(x) - x), axis=0, keepdims=True)

    x_spec = pl.BlockSpec((S, cb), lambda j: (0, off // cb + j))
    blk = pl.BlockSpec((S, cb), lambda j: (0, j))
    vec = pl.BlockSpec((1, cb), lambda j: (0, j))
    if d is None:
        return pl.pallas_call(
            fwd_body, name=name, grid=(width // cb,), in_specs=[x_spec, vec], out_specs=blk,
            out_shape=jax.ShapeDtypeStruct((S, width), F32), compiler_params=_cparams(("parallel",)),
        )(arr, mu)
    buf, first = into
    assert first % cb == 0
    return pl.pallas_call(
        bwd_body, name=name, grid=(width // cb,),
        in_specs=[x_spec, vec, blk, pl.BlockSpec(memory_space=pl.ANY)],
        out_specs=[pl.BlockSpec((S, cb), lambda j: (0, first // cb + j)), vec],
        out_shape=[jax.ShapeDtypeStruct(buf.shape, buf.dtype), jax.ShapeDtypeStruct((1, width), F32)],
        input_output_aliases={3: 0}, compiler_params=_cparams(("parallel",)),
    )(arr, mu, d, buf)


def _attention_fwd(qfull, kv, kr, hm, scale, *, tq, name):
    S = qfull.shape[0]
    nt = (((1,), (1,)), ((), ()))

    def body(q_ref, kn_ref, kr_ref, v_ref, o_ref, lse_ref, k_scr):
        _head_keys(kn_ref, kr_ref, k_scr)
        s = lax.dot_general(q_ref[...], k_scr[...], nt, preferred_element_type=F32)
        m = jnp.max(s, axis=-1, keepdims=True)
        p = jnp.exp((s - m) * scale)
        l = jnp.sum(p, axis=-1, keepdims=True)
        o_ref[...] = jnp.dot(p.astype(BF16), v_ref[...], preferred_element_type=F32) * (1.0 / l)
        lse_ref[...] = jnp.broadcast_to(m * scale + jnp.log(l), lse_ref.shape)

    oblk = pl.BlockSpec((tq, VDIM), lambda h, i: (i, h))
    return pl.pallas_call(
        body, name=name, grid=(hm, S // tq),
        in_specs=[pl.BlockSpec((tq, QHEAD), lambda h, i: (i, h)),
                  pl.BlockSpec((S, NOPE), lambda h, i: (0, h)),
                  pl.BlockSpec((S, LANES), lambda h, i: (0, 0)),
                  pl.BlockSpec((S, VDIM), lambda h, i: (0, hm + h))],
        out_specs=[oblk, oblk],
        out_shape=[jax.ShapeDtypeStruct((S, hm * VDIM), F32)] * 2,
        scratch_shapes=[pltpu.VMEM((S, QHEAD), BF16)],
        compiler_params=_cparams(("parallel", "arbitrary")),
    )(qfull, kv, kr, kv)


def _head_keys(kn_ref, kr_ref, k_scr):
    @pl.when(pl.program_id(1) == 0)
    def _():
        k_scr[:, :NOPE] = kn_ref[...]
        k_scr[:, NOPE:] = kr_ref[...]


def _attention_bwd(qfull, kv, kr, o, lse, d_o, hm, scale, *, tq, name):
    S = qfull.shape[0]
    tq = min(tq, S)
    nq = S // tq
    tn = (((0,), (0,)), ((), ()))
    nt = (((1,), (1,)), ((), ()))

    def body(q_ref, kn_ref, kr_ref, v_ref, o_ref, lse_ref, do_ref, dq_ref, dk_ref, dv_ref, k_scr):
        _head_keys(kn_ref, kr_ref, k_scr)
        s = lax.dot_general(q_ref[...], k_scr[...], nt, preferred_element_type=F32)
        p = jnp.exp(s * scale - lse_ref[:, 0:1])
        d_out = do_ref[...]
        delta = jnp.sum(d_out * o_ref[...], axis=-1, keepdims=True)
        d_out = d_out.astype(BF16)
        dp = lax.dot_general(d_out, v_ref[...], nt, preferred_element_type=F32)
        ds = (p * (dp - delta)).astype(BF16)
        dq_ref[...] = jnp.dot(ds, k_scr[...], preferred_element_type=F32) * scale
        dv = lax.dot_general(p.astype(BF16), d_out, tn, preferred_element_type=F32)
        dk = lax.dot_general(ds, q_ref[...], tn, preferred_element_type=F32)
        i = pl.program_id(1)
        for ref, val in ((dk_ref, dk), (dv_ref, dv)):
            @pl.when(i == 0)
            def _(ref=ref, val=val):
                ref[...] = val

            @pl.when(i > 0)
            def _(ref=ref, val=val):
                ref[...] += val

        @pl.when(i == nq - 1)
        def _():
            dk_ref[...] = dk_ref[...] * scale

    qblk = pl.BlockSpec((tq, QHEAD), lambda h, i: (i, h))
    oblk = pl.BlockSpec((tq, VDIM), lambda h, i: (i, h))
    return pl.pallas_call(
        body, name=name, grid=(hm, nq),
        in_specs=[qblk,
                  pl.BlockSpec((S, NOPE), lambda h, i: (0, h)),
                  pl.BlockSpec((S, LANES), lambda h, i: (0, 0)),
                  pl.BlockSpec((S, VDIM), lambda h, i: (0, hm + h)),
                  oblk, oblk, oblk],
        out_specs=[qblk, pl.BlockSpec((S, QHEAD), lambda h, i: (0, h)), pl.BlockSpec((S, VDIM), lambda h, i: (0, h))],
        out_shape=[jax.ShapeDtypeStruct((S, hm * QHEAD), F32), jax.ShapeDtypeStruct((S, hm * QHEAD), F32),
                   jax.ShapeDtypeStruct((S, hm * VDIM), F32)],
        scratch_shapes=[pltpu.VMEM((S, QHEAD), BF16)],
        compiler_params=_cparams(("parallel", "arbitrary")),
    )(qfull, kv, kr, kv, o, lse, d_o)


def _layout(D, MW, RW, TAIL, QR, KVR):
    names = ["gate_m", "gate_r", "z_m", "z_r", "q_a", "kv_a", "r", "k", "v", "tail"]
    widths = [D, D, MW, RW, QR, KVR, RW, RW, RW, TAIL]
    offs, o = {}, 0
    for nme, w in zip(names, widths):
        assert o % w == 0, (nme, o, w)
        offs[nme] = (o, w)
        o += w
    return offs, o


def _local_grads(x, target, W, dims, exchange=None):
    S, D = x.shape
    hm, hr, hn, rank = dims["hm"], dims["hr"], dims["hn"], dims["rank"]
    MW, RW = hm * VDIM, hr * hn
    TAIL = dims["TAIL"]
    QR, KVR = W["mla_q_norm"].shape[1], W["mla_kv_norm"].shape[1]
    lay, d_in = _layout(D, MW, RW, TAIL, QR, KVR)
    T = 256
    scale = (NOPE + ROPE) ** -0.5
    col = lambda arr, nme: _view(arr, *lay[nme])

    pos = jnp.arange(S, dtype=F32)
    inv_freq = jnp.power(ROPE_THETA, -jnp.arange(0, ROPE, 2, dtype=F32) / ROPE)
    ang = pos[:, None] * inv_freq[None, :]
    zpad = jnp.zeros((S, LANES - ROPE), F32)
    cosx = jnp.concatenate([jnp.cos(ang), jnp.cos(ang), zpad], axis=1)
    sinx = jnp.concatenate([jnp.sin(ang), jnp.sin(ang), zpad], axis=1)
    ri, ci = jnp.arange(LANES)[:, None], jnp.arange(LANES)[None, :]
    half = ROPE // 2
    rot = (jnp.where((ri == ci - half) & (ci >= half) & (ci < ROPE), 1.0, 0.0)
           - jnp.where((ri == ci + half) & (ci < half), 1.0, 0.0)).astype(BF16)
    rot_t = rot.T
    seg = (jnp.arange(RW)[:, None] // hn == jnp.arange(LANES)[None, :]).astype(BF16)
    seg_t = seg.T

    (h,) = _rowwise(lambda xb, g: (_rms(xb, g),), [x], [W["g_pre"]], [(D, BF16)], tile=T, name="pre_norm")
    if exchange is None:
        proj = _mm(h, W["w_in_t"], tb=True, name="in_proj")
    else:
        proj, *slabs = _mm(h, W["w_in_t"], tb=True, ride=_gather_plan(exchange[0]), name="in_proj")
        W = {**W, **_prepare_rest(dict(zip(_MATS[1:], slabs)), dims)}

    qn, kvn = _rowwise(_f_mla_norm, [col(proj, "q_a"), col(proj, "kv_a")], [W["mla_q_norm"], W["mla_kv_norm"]],
                       [(QR, BF16), (KVR, BF16)], tile=T, name="mla_norm")
    qraw = _mm(qn, W["wq_b_t"], tb=True, name="q_up")
    kv = _mm(kvn, W["wkv_b"], out_dtype=BF16, name="kv_up")
    kr_view = _view(proj, lay["tail"][0], LANES)
    qfull, kr = _rowwise(functools.partial(_f_rope, hm), [qraw, kr_view, cosx, sinx], [rot, rot_t],
                         [(hm * QHEAD, BF16), (LANES, BF16)], tile=T, name="rope")
    o_mla, lse = _attention_fwd(qfull, kv, kr, hm, scale, tq=T, name="attn_fwd")

    shift_view = (proj, lay["r"][0], 3 * RW + TAIL)
    rl = _shift_lerp(shift_view, W["mu"], name="shift_fwd")
    rl_r, rl_k, rl_v = _view(rl, 0, RW), _view(rl, RW, RW), _view(rl, 2 * RW, RW)
    rl_tail = _view(rl, 3 * RW, TAIL)
    pre_params = [W["w0_f"], W["w0_b"], W["a0_f"], W["a0_b"], W["k_k"], W["k_a"], W["w2cat"], W["a2cat"], seg, seg_t]
    pre_fn = functools.partial(_f_rwkv_pre, RW)
    lw_f, lw_b, k_f, k_b, a_n, b_f, b_b = _rowwise(pre_fn, [rl_k, rl_tail], pre_params, [(RW, F32)] * 7, tile=T,
                                                    name="rwkv_pre")
    ops_f = (rl_r, lw_f, k_f, rl_v, a_n, b_f)
    ops_b = (rl_r, lw_b, k_b, rl_v, a_n, b_b)
    y_f, st_f, y_b, st_b = _rwkv_scan_fwd(ops_f, ops_b, RW, name="scan_fwd")

    post_fn = functools.partial(_f_post, hn)
    post_rows = [y_f, y_b, rl_r, k_f, k_b, rl_v, col(proj, "z_r"), o_mla, col(proj, "z_m")]
    post_params = [W["gn_g"], W["gn_b"], W["r_k"], seg, seg_t]
    ymg, yrg = _rowwise(post_fn, post_rows, post_params, [(MW, BF16), (RW, BF16)], tile=T, name="post")
    u_m = _mm(ymg, W["w_br_mla"], name="br_mla")
    u_r = _mm(yrg, W["w_br_rwkv"], name="br_rwkv")
    merge_rows = [u_m, u_r, col(proj, "gate_m"), col(proj, "gate_r")]
    (merged,) = _rowwise(lambda *t: (_f_merge(*t),), merge_rows, [], [(D, BF16)], tile=T, name="merge")
    out = _mm(merged, W["w_out"], name="out_proj")

    def head(ob, xb, tb, g):
        yn, vjp = jax.vjp(_rms, ob, g)
        err = xb + yn - tb
        dy = err * (1.0 / D)
        d_ob, d_g = vjp(dy)
        loss = jnp.broadcast_to(0.5 * jnp.sum(err * err) * (1.0 / D), (1, LANES))
        return dy, d_ob, loss, d_g

    dy, d_out, loss, g_g_post = _rowwise(head, [out, x, target], [W["g_post"]], [(D, F32), (D, BF16)],
                                         [(1, LANES), (1, D)], tile=T, name="head")
    d_merged = _mm(d_out, W["w_out"], tb=True, name="d_merged")
    g_w_out = _mm(merged, d_out, ta=True, out_dtype=BF16, name="g_w_out")

    def merge_bwd(u_m_b, u_r_b, g_m_b, g_r_b, dm):
        _, vjp = jax.vjp(_f_merge, u_m_b, u_r_b, g_m_b, g_r_b)
        du_m, du_r, dg_m, dg_r = vjp(dm)
        return du_m, du_r, jnp.concatenate([dg_m, dg_r], axis=1)

    d_u_m, d_u_r, d_proj = _rowwise(merge_bwd, merge_rows + [d_merged], [],
                                    [(D, BF16), (D, BF16), (2 * D, BF16, (None, d_in, lay["gate_m"][0]))], tile=T,
                                    name="merge_bwd")
    d_ymg = _mm(d_u_m, W["w_br_mla"], tb=True, name="d_ymg")
    d_yrg = _mm(d_u_r, W["w_br_rwkv"], tb=True, name="d_yrg")
    g_w_br_mla = _mm(ymg, d_u_m, ta=True, out_dtype=BF16, name="g_w_br_mla")
    g_w_br_rwkv = _mm(yrg, d_u_r, ta=True, out_dtype=BF16, name="g_w_br_rwkv")

    def post_bwd(*args):
        nr = len(post_rows)
        prim, dm, dr = args[:nr] + args[nr + 2:], args[nr], args[nr + 1]
        _, vjp = jax.vjp(post_fn, *prim)
        g = vjp((dm, dr))
        return g[0], g[2], g[3], g[5], g[7], jnp.concatenate([g[8], g[6]], axis=1), g[9], g[10], g[11]

    (d_y, d_r_bonus, d_k_bonus, d_v_bonus, d_o, d_proj, g_gn_g, g_gn_b, g_r_k) = _rowwise(
        post_bwd, post_rows + [d_ymg, d_yrg], post_params,
        [(RW, F32), (RW, F32), (RW, F32), (RW, F32), (MW, F32), (MW + RW, BF16, (d_proj, d_in, lay["z_m"][0]))],
        [(1, RW)] * 3, tile=T // 2, name="post_bwd")

    dscan = _rwkv_scan_bwd(ops_f, ops_b, st_f, st_b, d_y, RW, name="scan_bwd")
    dsc = {"f": dscan[:6], "b": dscan[6:]}

    d_q_att, d_k_att, d_v_att = _attention_bwd(qfull, kv, kr, o_mla, lse, d_o, hm, scale, tq=2 * T, name="attn_bwd")

    def rope_bwd(qraw_b, kr_in, cos_b, sin_b, dq_b, dk_b, dv_b, rot_b, rot_t_b):
        _, vjp = jax.vjp(lambda q_, k_: _f_rope(hm, q_, k_, cos_b, sin_b, rot_b, rot_t_b), qraw_b, kr_in)
        dkn = jnp.concatenate([dk_b[:, hh * QHEAD:hh * QHEAD + NOPE] for hh in range(hm)], axis=1)
        dkr = dk_b[:, NOPE:QHEAD]
        for hh in range(1, hm):
            dkr = dkr + dk_b[:, hh * QHEAD + NOPE:(hh + 1) * QHEAD]
        d_qraw, d_kr_in = vjp((dq_b, dkr))
        return d_qraw, jnp.concatenate([dkn, dv_b], axis=1), d_kr_in

    d_qraw, d_kv, d_kr_in = _rowwise(rope_bwd, [qraw, kr_view, cosx, sinx, d_q_att, d_k_att, d_v_att],
                                     [rot, rot_t], [(hm * QHEAD, BF16), (2 * MW, BF16), (LANES, F32)], tile=T,
                                     name="rope_bwd")
    d_qnorm = _mm(d_qraw, W["wq_b_t"], name="d_qn")
    d_kvnorm = _mm(d_kv, W["wkv_b"], tb=True, name="d_kvn")
    g_wq_b = _mm(d_qraw, qn, ta=True, out_dtype=BF16, name="g_wq_b")
    g_wkv_b = _mm(kvn, d_kv, ta=True, out_dtype=BF16, name="g_wkv_b")

    def mla_norm_bwd(q_a, kv_a, qg, kvg, dq, dk):
        _, vjp = jax.vjp(_f_mla_norm, q_a, kv_a, qg, kvg)
        d_q_a, d_kv_a, d_qg, d_kvg = vjp((dq, dk))
        return jnp.concatenate([d_q_a, d_kv_a], axis=1), d_qg, d_kvg

    d_proj, g_q_norm, g_kv_norm = _rowwise(
        lambda q_a, kv_a, dq, dk, qg, kvg: mla_norm_bwd(q_a, kv_a, qg, kvg, dq, dk),
        [col(proj, "q_a"), col(proj, "kv_a"), d_qnorm, d_kvnorm], [W["mla_q_norm"], W["mla_kv_norm"]],
        [(QR + KVR, BF16, (d_proj, d_in, lay["q_a"][0]))], [(1, QR), (1, KVR)], tile=T, name="mla_norm_bwd")

    def pre_bwd(k_b_, tail_b, dlwf, dlwb, dkf, dkb, dkbon, daf, dab, dbf, dbb, drf, drb, drbon, dvf, dvb, dvbon,
                dkr, *params):
        w2, a2 = params[6], params[7]
        nt, tn = (((1,), (1,)), ((), ())), (((0,), (0,)), ((), ()))
        split = w2.shape[0]
        th = jnp.tanh(tail_b[:, :split])
        th_b, tail_h = th.astype(BF16), tail_b[:, split:].astype(BF16)
        zw = jnp.dot(th_b, w2, preferred_element_type=F32)
        za = jnp.dot(tail_h, a2, preferred_element_type=F32)
        _, vjp = jax.vjp(functools.partial(_f_rwkv_core, RW), k_b_, zw, za, *params[:6], params[8], params[9])
        g = vjp((dlwf, dlwb, dkf + dkbon, dkb + dkbon, daf + dab, dbf, dbb))
        d_zw, d_za = g[1].astype(BF16), g[2].astype(BF16)
        d_tail = (jnp.concatenate([lax.dot_general(d_zw, w2, nt, preferred_element_type=F32) * (1.0 - th * th),
                                   lax.dot_general(d_za, a2, nt, preferred_element_type=F32)], axis=1)
                  + jnp.concatenate([dkr, jnp.zeros((dkr.shape[0], TAIL - LANES), F32)], axis=1))
        g_w2 = lax.dot_general(th_b, d_zw, tn, preferred_element_type=F32)
        g_a2 = lax.dot_general(tail_h, d_za, tn, preferred_element_type=F32)
        d_rl = jnp.concatenate([drf + drb + drbon, g[0], dvf + dvb + dvbon, d_tail], axis=1)
        return (d_rl,) + tuple(g[3:9]) + (g_w2, g_a2)

    f_, b_ = dsc["f"], dsc["b"]
    pre_bwd_rows = [rl_k, rl_tail, f_[1], b_[1], f_[2], b_[2], d_k_bonus, f_[4], b_[4], f_[5], b_[5],
                    f_[0], b_[0], d_r_bonus, f_[3], b_[3], d_v_bonus, d_kr_in]
    (d_rl, g_w0_f, g_w0_b, g_a0_f, g_a0_b, g_k_k, g_k_a, g_w2cat, g_a2cat) = _rowwise(
        pre_bwd, pre_bwd_rows, pre_params, [(3 * RW + TAIL, F32)],
        [(1, RW)] * 6 + [W["w2cat"].shape, W["a2cat"].shape], tile=T // 2, name="rwkv_pre_bwd")
    d_proj, g_mu = _shift_lerp(shift_view, W["mu"], d_rl, (d_proj, lay["r"][0]), name="shift_bwd")
    small = dict(wq_b=g_wq_b, wkv_b=g_wkv_b, w2cat=g_w2cat, a2cat=g_a2cat, w_br_mla=g_w_br_mla,
                 w_br_rwkv=g_w_br_rwkv, w_out=g_w_out)
    if exchange is None:
        received = None
        g_w_in = _mm(d_proj, h, ta=True, out_dtype=BF16, tn_cap=1024, name="g_w_in")
        d_h = _mm(d_proj, W["w_in_t"], tn_cap=1024, name="d_h")
    else:
        slabs = _restore_rest(small, dims)
        slabs = [slabs[n] for n in _MATS[1:]]
        g_w_in, *got = _mm(d_proj, h, ta=True, out_dtype=BF16, tn_cap=1024, ride=_sibling_swap_plan(slabs),
                           name="g_w_in")
        sums = [_pair_add(exchange[1], s, t, name="pair_add_" + n) for n, s, t in zip(_MATS[1:], slabs, got)]
        g_w_in = _restore_w_in(g_w_in, dims)
        d_h, *received = _mm(d_proj, W["w_in_t"], tn_cap=1024, name="d_h",
                             ride=_join_plans(_chip_exchange_plan(sums), _sibling_swap_plan([g_w_in])))
        small = {}

    def pre_norm_bwd(xb, dyb, dhb, g):
        _, vjp = jax.vjp(_rms, xb, g)
        dx, dg = vjp(dhb)
        return dyb + dx, dg

    grad_x, g_g_pre = _rowwise(pre_norm_bwd, [x, dy, d_h], [W["g_pre"]], [(D, F32)], [(1, D)], tile=T,
                               name="pre_norm_bwd")

    grads = dict(g_pre=g_g_pre, w_in=g_w_in, mla_q_norm=g_q_norm, mla_kv_norm=g_kv_norm, mu=g_mu, w0_f=g_w0_f,
                 w0_b=g_w0_b, a0_f=g_a0_f, a0_b=g_a0_b, k_k=g_k_k, k_a=g_k_a, r_k=g_r_k, gn_g=g_gn_g, gn_b=g_gn_b,
                 g_post=g_g_post, **small)
    return loss[0, 0], grad_x, grads, received


_MATS = ["w_in", "mla_wq_b", "mla_wkv_b", "rwkv_w2_f", "rwkv_w2_b", "rwkv_a2_f", "rwkv_a2_b", "w_br_mla",
         "w_br_rwkv", "w_out"]
_ROW_SHARDED = ("w_out",)
_TRANSPOSED = ("w_in", "mla_wq_b")
_VECS = ["g_pre", "mla_q_norm", "mla_kv_norm", "rwkv_mu", "rwkv_w0_f", "rwkv_w0_b", "rwkv_a0_f", "rwkv_a0_b",
         "rwkv_k_k", "rwkv_k_a", "rwkv_r_k", "rwkv_gn_g", "rwkv_gn_b", "g_post"]
_WEIGHTS = ["g_pre", "w_in", "mla_q_norm", "mla_wq_b", "mla_kv_norm", "mla_wkv_b", "rwkv_mu", "rwkv_w0_f",
            "rwkv_w2_f", "rwkv_w0_b", "rwkv_w2_b", "rwkv_a0_f", "rwkv_a2_f", "rwkv_a0_b", "rwkv_a2_b", "rwkv_k_k",
            "rwkv_k_a", "rwkv_r_k", "rwkv_gn_g", "rwkv_gn_b", "w_br_mla", "w_br_rwkv", "w_out", "g_post"]

def _direct_gather_plan(src):
    def phases(src_refs, out_refs, sem_refs):
        (src_ref,), (out_ref,), sems, local_sem = src_refs, out_refs, sem_refs[:2], sem_refs[2]
        x, y, c = lax.axis_index("x"), lax.axis_index("y"), lax.axis_index("c")
        me = 4 * x + 2 * y + c
        flip = lambda v, bit: (1 - v) if bit else v
        peers = [(flip(x, d & 4), flip(y, d & 2), flip(c, d & 1)) for d in range(1, N_DEV)]
        own = lambda: pltpu.make_async_copy(src_ref, out_ref.at[me], local_sem)
        send = lambda d: _remote(src_ref, out_ref.at[me], sems, d, peers[d])

        def first():
            own().start()
            for d in range(N_DEV - 1):
                send(d).start()

        def last():
            for d, (px, py, pc) in enumerate(peers):
                blk = out_ref.at[4 * px + 2 * py + pc]
                _remote(blk, blk, sems, d, (x, y, c)).wait_recv()
            for d in range(N_DEV - 1):
                send(d).wait_send()
            own().wait()

        return first, (lambda: None), last

    return [src], [jax.ShapeDtypeStruct((N_DEV,) + src.shape, src.dtype)], [(N_DEV - 1,), (N_DEV - 1,), ()], phases


def _remote(src, dst, sems, key, to):
    send_sems, recv_sems = sems
    return pltpu.make_async_remote_copy(src_ref=src, dst_ref=dst, send_sem=send_sems.at[key], recv_sem=recv_sems.at[key],
                                        device_id=to, device_id_type=pl.DeviceIdType.MESH)


def _run_exchange(plan, *, name):
    srcs, out_shapes, sem_shapes, phases = plan
    n, m = len(srcs), len(out_shapes)

    def body(*refs):
        for phase in phases(refs[:n], refs[n:n + m], refs[n + m:]):
            phase()

    return pl.pallas_call(
        body, name=name, out_shape=out_shapes,
        in_specs=[pl.BlockSpec(memory_space=pl.ANY)] * n, out_specs=[pl.BlockSpec(memory_space=pl.ANY)] * m,
        scratch_shapes=[pltpu.SemaphoreType.DMA(s) for s in sem_shapes],
    )(*srcs)


def _join_plans(p, q):
    (srcs_p, outs_p, sems_p, phases_p), (srcs_q, outs_q, sems_q, phases_q) = p, q

    def phases(src_refs, out_refs, sem_refs):
        a = phases_p(src_refs[:len(srcs_p)], out_refs[:len(outs_p)], sem_refs[:len(sems_p)])
        b = phases_q(src_refs[len(srcs_p):], out_refs[len(outs_p):], sem_refs[len(sems_p):])

        def both(fa, fb):
            def run():
                fa()
                fb()
            return run

        return tuple(both(fa, fb) for fa, fb in zip(a, b))

    return list(srcs_p) + list(srcs_q), list(outs_p) + list(outs_q), list(sems_p) + list(sems_q), phases


def _gather_plan(srcs):
    n = len(srcs)

    def phases(src_refs, out_refs, sem_refs):
        sems, local_sems = sem_refs[:2], sem_refs[2]
        x, y, c = lax.axis_index("x"), lax.axis_index("y"), lax.axis_index("c")
        idx = lambda px, py, pc: 4 * px + 2 * py + pc
        me, sibling = (x, y, c), (x, y, 1 - c)
        chips = [(1 - x, y), (x, 1 - y), (1 - x, 1 - y)]
        own = lambda a: pltpu.make_async_copy(src_refs[a], out_refs[a].at[idx(*me)], local_sems.at[a])
        to_sibling = lambda a: _remote(src_refs[a], out_refs[a].at[idx(*me)], sems, (0, a), sibling)
        to_chip = lambda a, j: _remote(src_refs[a], out_refs[a].at[idx(*me)], sems, (1 + j, a), (*chips[j], c))
        landed = lambda a, j: out_refs[a].at[idx(*chips[j], c)]
        passed_on = lambda a, j: _remote(landed(a, j), landed(a, j), sems, (4 + j, a), sibling)

        def first():
            for a in range(n):
                own(a).start()
                to_sibling(a).start()
                for j in range(3):
                    to_chip(a, j).start()

        def middle():
            for j in range(3):
                for a in range(n):
                    _remote(landed(a, j), landed(a, j), sems, (1 + j, a), me).wait_recv()
                    passed_on(a, j).start()

        def last():
            for a in range(n):
                blk = out_refs[a].at[idx(*sibling)]
                _remote(blk, blk, sems, (0, a), me).wait_recv()
                for j in range(3):
                    blk = out_refs[a].at[idx(*chips[j], 1 - c)]
                    _remote(blk, blk, sems, (4 + j, a), me).wait_recv()
            for a in range(n):
                to_sibling(a).wait_send()
                for j in range(3):
                    to_chip(a, j).wait_send()
                    passed_on(a, j).wait_send()
                own(a).wait()

        return first, middle, last

    return srcs, [jax.ShapeDtypeStruct((N_DEV,) + s.shape, s.dtype) for s in srcs], [(7, n), (7, n), (n,)], phases


def _sibling_swap_plan(srcs):
    n = len(srcs)

    def phases(src_refs, out_refs, sems):
        x, y, c = lax.axis_index("x"), lax.axis_index("y"), lax.axis_index("c")
        copies = lambda: [_remote(src_refs[a].at[2 * q + 1 - c], out_refs[a].at[q], sems, (q, a), (x, y, 1 - c))
                          for a in range(n) for q in range(4)]

        def first():
            for cp in copies():
                cp.start()

        def last():
            for cp in copies():
                cp.wait()

        return first, (lambda: None), last

    return srcs, [jax.ShapeDtypeStruct((4,) + s.shape[1:], s.dtype) for s in srcs], [(4, n), (4, n)], phases


def _chip_exchange_plan(srcs):
    n = len(srcs)

    def phases(src_refs, out_refs, sem_refs):
        sems, local_sems = sem_refs[:2], sem_refs[2]
        x, y, c = lax.axis_index("x"), lax.axis_index("y"), lax.axis_index("c")
        mine = 2 * x + y
        chips = [(1 - x, y), (x, 1 - y), (1 - x, 1 - y)]
        own = lambda a: pltpu.make_async_copy(src_refs[a].at[mine], out_refs[a].at[mine], local_sems.at[a])
        send = lambda a, j: _remote(src_refs[a].at[2 * chips[j][0] + chips[j][1]], out_refs[a].at[mine], sems, (j, a),
                                    (*chips[j], c))

        def first():
            for a in range(n):
                own(a).start()
                for j in range(3):
                    send(a, j).start()

        def last():
            for j in range(3):
                for a in range(n):
                    blk = out_refs[a].at[2 * chips[j][0] + chips[j][1]]
                    _remote(blk, blk, sems, (j, a), (x, y, c)).wait_recv()
            for a in range(n):
                for j in range(3):
                    send(a, j).wait_send()
                own(a).wait()

        return first, (lambda: None), last

    return srcs, [jax.ShapeDtypeStruct(s.shape, s.dtype) for s in srcs], [(3, n), (3, n), (n,)], phases


def _pair_add(core, g, got, *, name):
    q, r, c = got.shape
    tr, tc = _tile2d(r, c)

    def body(core_ref, a_ref, b_ref, o_ref):
        o_ref[...] = (a_ref[...].astype(F32) + b_ref[...].astype(F32)).astype(BF16)

    blk = pl.BlockSpec((1, tr, tc), lambda i, j, k, core_ref: (i, j, k))
    mine = pl.BlockSpec((1, tr, tc), lambda i, j, k, core_ref: (2 * i + core_ref[0], j, k))
    return pl.pallas_call(
        body, name=name, out_shape=jax.ShapeDtypeStruct(got.shape, BF16),
        grid_spec=pltpu.PrefetchScalarGridSpec(num_scalar_prefetch=1, grid=(q, r // tr, c // tc),
                                               in_specs=[mine, blk], out_specs=blk),
        compiler_params=_cparams(("parallel", "parallel", "parallel")))(core, g, got)


def _adamw(recv, w, m, v, *, name):
    r, c = w.shape
    n_terms = recv.shape[0]
    tr, tc = _tile2d(r, c)

    def body(g_ref, w_ref, m_ref, v_ref, go_ref, d_ref, mo_ref, vo_ref):
        g = g_ref[0].astype(F32)
        for k in range(1, n_terms):
            g = g + g_ref[k].astype(F32)
        m_new = ADAM_B1 * m_ref[...] + (1.0 - ADAM_B1) * g
        v_new = ADAM_B2 * v_ref[...] + (1.0 - ADAM_B2) * (g * g)
        m_hat = m_new / (1.0 - ADAM_B1 ** ADAM_STEP)
        v_hat = v_new / (1.0 - ADAM_B2 ** ADAM_STEP)
        go_ref[...] = g
        d_ref[...] = -ADAM_LR * (m_hat / (jnp.sqrt(v_hat) + ADAM_EPS) + ADAM_WD * w_ref[...])
        mo_ref[...] = m_new
        vo_ref[...] = v_new

    blk = pl.BlockSpec((tr, tc), lambda i, j: (i, j))
    return pl.pallas_call(
        body, name=name, grid=(r // tr, c // tc),
        in_specs=[pl.BlockSpec((n_terms, tr, tc), lambda i, j: (0, i, j)), blk, blk, blk], out_specs=[blk] * 4,
        out_shape=[jax.ShapeDtypeStruct((r, c), F32)] * 4, compiler_params=_cparams(("parallel", "parallel")),
    )(recv, w, m, v)


def _tile2d(r, c, cap=256):
    if r <= cap:
        return r, c
    for t in range(cap, 0, -BF16_ROWS):
        if r % t == 0:
            return t, c
    return r, _pick(c, cap)


def _pack(pieces):
    total = sum(p.shape[0] for p in pieces)
    pad = (-total) % (8 * LANES)
    flat = jnp.concatenate(list(pieces) + [jnp.zeros((pad,), F32)])
    return flat.reshape(-1, LANES)


def _unpack(flat, sizes):
    flat = flat.reshape(-1)
    out, o = [], 0
    for n in sizes:
        out.append(flat[o:o + n])
        o += n
    return out


def _prepare_weights(full, vec, dims):
    rest = {n: t for n, t in full.items() if n != "w_in"}
    return {"w_in_t": _prepare_w_in(full["w_in"], dims), **_prepare_rest(rest, dims), **_prepare_vectors(vec, dims)}


def _prepare_w_in(slabs, dims):
    D = dims["D"]
    flat = slabs.reshape(-1, D)
    parts, pos = [], 0
    for orig_off, width, perm_off in sorted(dims["segs"], key=lambda t: t[2]):
        if perm_off > pos:
            parts.append(jnp.zeros((perm_off - pos, D), BF16))
        parts.append(flat[orig_off:orig_off + width])
        pos = perm_off + width
    if dims["d_in_perm"] > pos:
        parts.append(jnp.zeros((dims["d_in_perm"] - pos, D), BF16))
    return jnp.concatenate(parts, axis=0)


def _prepare_rest(full, dims):
    hm, hr, hn, rank = dims["hm"], dims["hr"], dims["hn"], dims["rank"]
    QR, KVR = dims["QR"], dims["KVR"]
    RW, TAIL = hr * hn, dims["TAIL"]
    full = {n: (t.reshape(-1, t.shape[2]) if n in _ROW_SHARDED + _TRANSPOSED
                else t.transpose(1, 0, 2).reshape(t.shape[1], -1)) for n, t in full.items()}
    wq = full["mla_wq_b"].reshape(hm, NOPE + ROPE, QR)
    wq = jnp.concatenate([wq, jnp.zeros((hm, QHEAD - NOPE - ROPE, QR), BF16)], axis=1).reshape(hm * QHEAD, QR)
    wkv = full["mla_wkv_b"].reshape(KVR, hm, 2, NOPE).transpose(0, 2, 1, 3).reshape(KVR, 2 * hm * NOPE)
    z = lambda rows: jnp.zeros((rows, RW), BF16)
    f = lambda nme: full[nme]
    split = ROPE + 2 * rank
    assert split % LANES == 0, split
    w2cat = jnp.concatenate([
        jnp.concatenate([z(ROPE), f("rwkv_w2_f"), z(rank)], axis=0),
        jnp.concatenate([z(ROPE + rank), f("rwkv_w2_b")], axis=0)], axis=1)
    a2cat = jnp.concatenate([
        jnp.concatenate([f("rwkv_a2_f"), z(TAIL - split - rank)], axis=0),
        jnp.concatenate([z(rank), f("rwkv_a2_b"), z(TAIL - split - 2 * rank)], axis=0)], axis=1)
    return dict(wq_b_t=wq, wkv_b=wkv, w2cat=w2cat, a2cat=a2cat, w_br_mla=full["w_br_mla"],
                w_br_rwkv=full["w_br_rwkv"], w_out=full["w_out"])


def _prepare_vectors(vec, dims):
    rank, RW, TAIL = dims["rank"], dims["hr"] * dims["hn"], dims["TAIL"]
    mu = vec["rwkv_mu"]
    mu_p = jnp.concatenate([mu[:3 * RW], jnp.zeros((ROPE,), F32), mu[3 * RW:],
                            jnp.zeros((TAIL - ROPE - 4 * rank,), F32)])
    row = lambda t: t.reshape(1, -1)
    return dict(
        mu=row(mu_p), g_pre=row(vec["g_pre"]), g_post=row(vec["g_post"]), mla_q_norm=row(vec["mla_q_norm"]),
        mla_kv_norm=row(vec["mla_kv_norm"]), w0_f=row(vec["rwkv_w0_f"]), w0_b=row(vec["rwkv_w0_b"]),
        a0_f=row(vec["rwkv_a0_f"]), a0_b=row(vec["rwkv_a0_b"]), k_k=row(vec["rwkv_k_k"]), k_a=row(vec["rwkv_k_a"]),
        r_k=row(vec["rwkv_r_k"]), gn_g=row(vec["rwkv_gn_g"]), gn_b=row(vec["rwkv_gn_b"]))


def _restore_grads(g, dims):
    return {"w_in": _restore_w_in(g["w_in"], dims), **_restore_rest(g, dims), **_restore_vectors(g, dims)}


def _restore_w_in(gw, dims):
    parts = [gw[perm_off:perm_off + width] for _, width, perm_off in sorted(dims["segs"])]
    return jnp.concatenate(parts, axis=0).reshape(N_DEV, dims["d_in"] // N_DEV, gw.shape[1])


def _restore_rest(g, dims):
    hm, hr, hn, rank = dims["hm"], dims["hr"], dims["hn"], dims["rank"]
    QR, KVR, RW = dims["QR"], dims["KVR"], hr * hn
    wq = g["wq_b"].reshape(hm, QHEAD, QR)[:, :NOPE + ROPE].reshape(N_DEV, -1, QR)
    wkv = g["wkv_b"].reshape(KVR, 2, hm, NOPE).transpose(0, 2, 1, 3).reshape(KVR, 2 * hm * NOPE)
    lo = lambda t, first, half: t[first:first + rank, half * RW:(half + 1) * RW].astype(BF16)
    cols = lambda t: t.reshape(t.shape[0], N_DEV, -1).transpose(1, 0, 2)
    return dict(
        mla_wq_b=wq, mla_wkv_b=cols(wkv), rwkv_w2_f=cols(lo(g["w2cat"], ROPE, 0)),
        rwkv_w2_b=cols(lo(g["w2cat"], ROPE + rank, 1)), rwkv_a2_f=cols(lo(g["a2cat"], 0, 0)),
        rwkv_a2_b=cols(lo(g["a2cat"], rank, 1)), w_br_mla=cols(g["w_br_mla"]), w_br_rwkv=cols(g["w_br_rwkv"]),
        w_out=g["w_out"].reshape(N_DEV, -1, g["w_out"].shape[1]))


def _restore_vectors(g, dims):
    rank, RW = dims["rank"], dims["hr"] * dims["hn"]
    mu = g["mu"][0]
    out = dict(
        rwkv_mu=jnp.concatenate([mu[:3 * RW], mu[3 * RW + ROPE:3 * RW + ROPE + 4 * rank]]),
        g_pre=g["g_pre"][0], g_post=g["g_post"][0], mla_q_norm=g["mla_q_norm"][0], mla_kv_norm=g["mla_kv_norm"][0],
        rwkv_w0_f=g["w0_f"][0], rwkv_w0_b=g["w0_b"][0], rwkv_a0_f=g["a0_f"][0], rwkv_a0_b=g["a0_b"][0],
        rwkv_k_k=g["k_k"][0], rwkv_k_a=g["k_a"][0], rwkv_r_k=g["r_k"][0], rwkv_gn_g=g["gn_g"][0],
        rwkv_gn_b=g["gn_b"][0])
    return out


def _dims(inp):
    D = inp["x"].shape[-1]
    QR, KVR = inp["mla_q_norm"].shape[0], inp["mla_kv_norm"].shape[0]
    hm = inp["mla_wq_b"].shape[1] * N_DEV // (NOPE + ROPE)
    hr, hn = inp["rwkv_r_k"].shape
    rank = inp["rwkv_w2_f"].shape[0]
    MW, RW = hm * VDIM, hr * hn
    TAIL = -(-(ROPE + 4 * rank) // LANES) * LANES
    orig, o = {}, 0
    for nme, w in (("q_a", QR), ("kv_a", KVR), ("k_rope", ROPE), ("rkv", 3 * RW), ("lora", 4 * rank), ("z_m", MW),
                   ("z_r", RW), ("gate_m", D), ("gate_r", D)):
        orig[nme] = (o, w)
        o += w
    assert o == inp["w_in"].shape[1] * N_DEV
    lay, d_in_perm = _layout(D, MW, RW, TAIL, QR, KVR)
    perm_off = dict(q_a=lay["q_a"][0], kv_a=lay["kv_a"][0], k_rope=lay["tail"][0], rkv=lay["r"][0],
                    lora=lay["tail"][0] + ROPE, z_m=lay["z_m"][0], z_r=lay["z_r"][0], gate_m=lay["gate_m"][0],
                    gate_r=lay["gate_r"][0])
    segs = [(orig[nme][0], orig[nme][1], perm_off[nme]) for nme in orig]
    return dict(D=D, QR=QR, KVR=KVR, hm=hm, hr=hr, hn=hn, rank=rank, TAIL=TAIL, segs=segs, d_in=o,
                d_in_perm=d_in_perm)


def kernel(x, g_pre, w_in, mla_q_norm, mla_wq_b, mla_kv_norm, mla_wkv_b, rwkv_mu, rwkv_w0_f, rwkv_w2_f, rwkv_w0_b, rwkv_w2_b, rwkv_a0_f, rwkv_a2_f, rwkv_a0_b, rwkv_a2_b, rwkv_k_k, rwkv_k_a, rwkv_r_k, rwkv_gn_g, rwkv_gn_b, w_br_mla, w_br_rwkv, w_out, g_post, loss_target, m_g_pre, m_w_in, m_mla_q_norm, m_mla_wq_b, m_mla_kv_norm, m_mla_wkv_b, m_rwkv_mu, m_rwkv_w0_f, m_rwkv_w2_f, m_rwkv_w0_b, m_rwkv_w2_b, m_rwkv_a0_f, m_rwkv_a2_f, m_rwkv_a0_b, m_rwkv_a2_b, m_rwkv_k_k, m_rwkv_k_a, m_rwkv_r_k, m_rwkv_gn_g, m_rwkv_gn_b, m_w_br_mla, m_w_br_rwkv, m_w_out, m_g_post, v_g_pre, v_w_in, v_mla_q_norm, v_mla_wq_b, v_mla_kv_norm, v_mla_wkv_b, v_rwkv_mu, v_rwkv_w0_f, v_rwkv_w2_f, v_rwkv_w0_b, v_rwkv_w2_b, v_rwkv_a0_f, v_rwkv_a2_f, v_rwkv_a0_b, v_rwkv_a2_b, v_rwkv_k_k, v_rwkv_k_a, v_rwkv_r_k, v_rwkv_gn_g, v_rwkv_gn_b, v_w_br_mla, v_w_br_rwkv, v_w_out, v_g_post):
    inp = dict(locals())
    dims = _dims(inp)
    stored = lambda t, n: t.T if n in _TRANSPOSED else t
    assert _MATS[0] == "w_in"
    shards = [stored(inp[n], n).astype(BF16) for n in _MATS]
    core = lax.axis_index("c").astype(jnp.int32).reshape(1)
    (w_in_slabs,) = _run_exchange(_gather_plan(shards[:1]), name="gather_w_in")
    W = {"w_in_t": _prepare_w_in(w_in_slabs, dims), **_prepare_vectors({n: inp[n] for n in _VECS}, dims)}
    loss, grad_x, g, recv_rest = _local_grads(x[0], loss_target[0], W, dims, exchange=(shards[1:], core))

    new = {}
    *recv_rest, got = recv_rest
    g_w_in, g = g["w_in"], _restore_vectors(g, dims)
    vsizes = [inp[n].size for n in _VECS] + [1]
    vflat = lambda prefix, src, last: _pack([src[prefix + n].reshape(-1) for n in _VECS] + [last])
    one = jnp.zeros((1,), F32)
    recv_w_in, vrecv = _run_exchange(
        _join_plans(_chip_exchange_plan([_pair_add(core, g_w_in, got, name="pair_add_w_in")]),
                    _direct_gather_plan(vflat("", g, loss.reshape(1)))), name="scatter_w_in")
    for n, t in zip(_MATS, [recv_w_in] + recv_rest):
        out = _adamw(t, stored(inp[n], n), stored(inp["m_" + n], n), stored(inp["v_" + n], n), name="adamw_" + n)
        new[n] = [stored(o, n) for o in out]

    vout = _adamw(vrecv, vflat("", inp, one), vflat("m_", inp, one), vflat("v_", inp, one), name="adamw_vectors")
    vparts = [_unpack(t, vsizes) for t in vout]
    for i, n in enumerate(_VECS):
        new[n] = [vp[i].reshape(inp[n].shape) for vp in vparts]
    loss = vparts[0][-1].reshape(())

    outs = [loss, grad_x[None]]
    for k in range(4):
        outs += [new[n][k] for n in _WEIGHTS]
    return tuple(outs)
```

```python
import functools
import math

import jax
import jax.numpy as jnp
from jax import lax
from jax.experimental import pallas as pl
from jax.experimental.pallas import tpu as pltpu

F32 = jnp.float32
BF16 = jnp.bfloat16

N_DEV = 8
LANES = 128
BF16_ROWS = 16
NOPE, ROPE, VDIM = 128, 64, 128
QHEAD = 256
ROPE_THETA = 10000.0
NORM_EPS = 1e-6
GN_EPS = 64e-5
CHUNK = 64
SUB = 16
VMEM_LIMIT = 56 * 1024 * 1024

ADAM_LR, ADAM_B1, ADAM_B2, ADAM_EPS, ADAM_WD, ADAM_STEP = 0.001, 0.9, 0.999, 1e-08, 0.01, 10


def _cparams(sem):
    return pltpu.CompilerParams(dimension_semantics=sem, vmem_limit_bytes=VMEM_LIMIT)


def _pick(n, cap):
    if n <= cap:
        return n
    for t in range(cap - cap % LANES, 0, -LANES):
        if n % t == 0:
            return t
    raise ValueError(f"no tile for {n} under {cap}")


def _mm(a, b, *, ta=False, tb=False, out_dtype=F32, name, tm_cap=1024, tn_cap=512, tk_cap=2048, ride=None):
    K, M = a.shape if ta else a.shape[::-1]
    N = b.shape[0] if tb else b.shape[1]
    assert (b.shape[1] if tb else b.shape[0]) == K, (a.shape, b.shape, ta, tb)
    tm, tn, tk = _pick(M, tm_cap), _pick(N, tn_cap), _pick(K, tk_cap)
    nj, nk = N // tn, K // tk
    steps = (M // tm) * nj * nk
    dn = (((0 if ta else 1,), (1 if tb else 0,)), ((), ()))
    srcs, extra_shapes, sem_shapes, phases = ride if ride else ((), (), (), None)
    n_src, n_extra = len(srcs), len(extra_shapes)

    def body(*refs):
        a_ref, b_ref, o_ref = refs[0], refs[1], refs[2 + n_src]
        acc_ref = refs[3 + n_src + n_extra]
        k = pl.program_id(2)
        if ride:
            step = (pl.program_id(0) * nj + pl.program_id(1)) * nk + k
            first, middle, last = phases(refs[2:2 + n_src], refs[3 + n_src:3 + n_src + n_extra],
                                         refs[4 + n_src + n_extra:])
            pl.when(step == 0)(first)
            pl.when(step == (steps * 15) // 16)(middle)
        p = lax.dot_general(a_ref[...], b_ref[...], dn, preferred_element_type=F32)

        @pl.when(k == 0)
        def _():
            acc_ref[...] = p

        @pl.when(k > 0)
        def _():
            acc_ref[...] += p

        @pl.when(k == nk - 1)
        def _():
            o_ref[...] = acc_ref[...].astype(out_dtype)

        if ride:
            pl.when(step == steps - 1)(last)

    a_spec = pl.BlockSpec((tk, tm), lambda i, j, k: (k, i)) if ta else pl.BlockSpec((tm, tk), lambda i, j, k: (i, k))
    b_spec = pl.BlockSpec((tn, tk), lambda i, j, k: (j, k)) if tb else pl.BlockSpec((tk, tn), lambda i, j, k: (k, j))
    hbm = pl.BlockSpec(memory_space=pl.ANY)
    out = pl.pallas_call(
        body, name=name, grid=(M // tm, nj, nk),
        in_specs=[a_spec, b_spec] + [hbm] * n_src,
        out_specs=[pl.BlockSpec((tm, tn), lambda i, j, k: (i, j))] + [hbm] * n_extra,
        out_shape=[jax.ShapeDtypeStruct((M, N), out_dtype)] + list(extra_shapes),
        scratch_shapes=[pltpu.VMEM((tm, tn), F32)] + [pltpu.SemaphoreType.DMA(s) for s in sem_shapes],
        compiler_params=_cparams(("arbitrary",) * 3 if ride else ("parallel", "parallel", "arbitrary")),
    )(a, b, *srcs)
    return out if ride else out[0]


def _view(arr, off, width):
    assert off % width == 0, (off, width)
    return (arr, off // width, width)


def _rowwise(fn, rows, params, out_rows, out_accs=(), *, tile, name):
    rows = [r if isinstance(r, tuple) else (r, 0, r.shape[1]) for r in rows]
    S = rows[0][0].shape[0]
    T = min(tile, S)
    assert S % T == 0
    n_rows, n_par, n_out = len(rows), len(params), len(out_rows)
    into = [o[2] if len(o) == 3 else None for o in out_rows]
    carried = [t[0] for t in into if t is not None and t[0] is not None]

    def body(*refs):
        ins = [r[...] for r in refs[:n_rows + n_par]]
        outs = fn(*ins)
        out_refs = refs[n_rows + n_par + len(carried):]
        for o_ref, val in zip(out_refs[:n_out], outs[:n_out]):
            o_ref[...] = val.astype(o_ref.dtype)
        i = pl.program_id(0)
        for o_ref, val in zip(out_refs[n_out:], outs[n_out:]):
            @pl.when(i == 0)
            def _(o_ref=o_ref, val=val):
                o_ref[...] = val

            @pl.when(i > 0)
            def _(o_ref=o_ref, val=val):
                o_ref[...] += val

    in_specs = [pl.BlockSpec((T, w), functools.partial(lambda i, cb: (i, cb), cb=cb)) for _, cb, w in rows]
    in_specs += [pl.BlockSpec(p.shape, lambda i: (0, 0)) for p in params]
    in_specs += [pl.BlockSpec(memory_space=pl.ANY)] * len(carried)
    out_specs, out_shape, aliases = [], [], {}
    for k, (o, t) in enumerate(zip(out_rows, into)):
        w, dt = o[0], o[1]
        if t is None:
            out_specs.append(pl.BlockSpec((T, w), lambda i: (i, 0)))
            out_shape.append(jax.ShapeDtypeStruct((S, w), dt))
            continue
        buf, total, first = t
        assert first % w == 0
        out_specs.append(pl.BlockSpec((T, w), functools.partial(lambda i, cb: (i, cb), cb=first // w)))
        out_shape.append(jax.ShapeDtypeStruct((S, total), dt))
        if buf is not None:
            aliases[n_rows + n_par + len(aliases)] = k
    out_specs += [pl.BlockSpec(s, lambda i: (0, 0)) for s in out_accs]
    out_shape += [jax.ShapeDtypeStruct(s, F32) for s in out_accs]
    return pl.pallas_call(
        body, name=name, grid=(S // T,), in_specs=in_specs, out_specs=out_specs, out_shape=out_shape,
        input_output_aliases=aliases, compiler_params=_cparams(("arbitrary",)),
    )(*[r[0] for r in rows], *params, *carried)


def _mm_sel(x, sel):
    hi = x.astype(BF16)
    lo = (x - hi.astype(F32)).astype(BF16)
    d = lambda u: jnp.dot(u, sel, preferred_element_type=F32)
    return d(hi) + d(lo)


@jax.custom_vjp
def _sel(x, sel, sel_t):
    return _mm_sel(x, sel)


def _sel_fwd(x, sel, sel_t):
    return _mm_sel(x, sel), (sel, sel_t)


def _sel_bwd(res, ct):
    sel, sel_t = res
    return _mm_sel(ct, sel_t), jnp.zeros_like(sel), jnp.zeros_like(sel_t)


_sel.defvjp(_sel_fwd, _sel_bwd)


def _rms(x, g):
    return x * lax.rsqrt(jnp.mean(x * x, axis=-1, keepdims=True) + NORM_EPS) * g


def _sigmoid(x):
    return 1.0 / (1.0 + jnp.exp(-x))


def _silu(x):
    return x * _sigmoid(x)


def _softplus(x):
    return jnp.maximum(x, 0.0) + jnp.log(1.0 + jnp.exp(-jnp.abs(x)))


def _f_mla_norm(q_a, kv_a, qg, kvg):
    return _rms(q_a, qg), _rms(kv_a, kvg)


def _f_rope(hm, qraw, kr_in, cosx, sinx, rot, rot_t):
    def rope(t):
        return t * cosx + _sel(t, rot, rot_t) * sinx
    parts = []
    for h in range(hm):
        parts.append(qraw[:, h * QHEAD:h * QHEAD + NOPE])
        parts.append(rope(qraw[:, h * QHEAD + NOPE:(h + 1) * QHEAD]))
    return jnp.concatenate(parts, axis=1), rope(kr_in)


def _f_rwkv_pre(rw, k, tail, w0f, w0b, a0f, a0b, k_k, k_a, w2cat, a2cat, seg, seg_t):
    split = w2cat.shape[0]
    zw = jnp.dot(jnp.tanh(tail[:, :split]).astype(BF16), w2cat, preferred_element_type=F32)
    za = jnp.dot(tail[:, split:].astype(BF16), a2cat, preferred_element_type=F32)
    return _f_rwkv_core(rw, k, zw, za, w0f, w0b, a0f, a0b, k_k, k_a, seg, seg_t)


def _f_rwkv_core(rw, k, zw, za, w0f, w0b, a0f, a0b, k_k, k_a, seg, seg_t):
    lw_f = -jnp.exp(-_softplus(-(w0f + zw[:, :rw])) - 0.5)
    lw_b = -jnp.exp(-_softplus(-(w0b + zw[:, rw:])) - 0.5)
    a_f = _sigmoid(a0f + za[:, :rw])
    a_b = _sigmoid(a0b + za[:, rw:])
    kk = k * k_k
    nrm = jnp.sqrt(_sel(_sel(kk * kk, seg, seg_t), seg_t, seg))
    kk = kk / jnp.maximum(nrm, 1e-12)
    k_f = k * (1.0 + (a_f - 1.0) * k_a)
    k_b = k * (1.0 + (a_b - 1.0) * k_a)
    return lw_f, lw_b, k_f, k_b, -kk, kk * a_f, kk * a_b


def _f_post(hn, y_f, y_b, r, k_f, k_b, v, z_r, o_mla, z_m, gn_g, gn_b, r_k, seg, seg_t):
    segsum = lambda t: _sel(_sel(t, seg, seg_t), seg_t, seg)
    y = y_f + y_b
    mu = segsum(y) * (1.0 / hn)
    yc = y - mu
    var = segsum(yc * yc) * (1.0 / hn)
    yn = yc * lax.rsqrt(var + GN_EPS) * gn_g + gn_b
    bonus = segsum(r * (k_f + k_b) * r_k) * v
    return o_mla * _silu(z_m), (yn + bonus) * _silu(z_r)


def _f_merge(u_m, u_r, g_m, g_r):
    return _sigmoid(g_m) * u_m + _sigmoid(g_r) * u_r


_NN = ((2,), (1,))
_NT = ((2,), (2,))
_TN = ((1,), (1,))

_SCAN_PASSES = {"cum": 2, "gram": 3, "solve": 1, "apply": 1, "state": 1}


def _hdot_raw(passes, x, y, dims):
    dn = (dims, ((0,), (0,)))
    d = lambda p, q: lax.dot_general(p, q, dn, preferred_element_type=F32)
    xh = x.astype(BF16)
    yh = y.astype(BF16)
    if passes == 1:
        return d(xh, yh)
    yl = (y - yh.astype(F32)).astype(BF16)
    if passes == 2:
        return d(xh, yh) + d(xh, yl)
    xl = (x - xh.astype(F32)).astype(BF16)
    return d(xh, yh) + d(xh, yl) + d(xl, yh)


@functools.partial(jax.custom_vjp, nondiff_argnums=(2, 3))
def _hdot_p(x, y, dims, passes):
    return _hdot_raw(passes, x, y, dims)


def _hdot_fwd(x, y, dims, passes):
    return _hdot_raw(passes, x, y, dims), (x, y)


def _hdot_bwd(dims, passes, res, ct):
    x, y = res
    if dims == _NN:
        return _hdot_raw(passes, ct, y, _NT), _hdot_raw(passes, x, ct, _TN)
    if dims == _NT:
        return _hdot_raw(passes, ct, y, _NN), _hdot_raw(passes, ct, x, _TN)
    return _hdot_raw(passes, y, ct, _NT), _hdot_raw(passes, x, ct, _NN)


_hdot_p.defvjp(_hdot_fwd, _hdot_bwd)


def _hdot(x, y, dims, kind):
    return _hdot_p(x, y, dims, _SCAN_PASSES[kind])


def _tri_solve(n_mat, x, length):
    row = lax.broadcasted_iota(jnp.int32, (length, length), 0)
    col = lax.broadcasted_iota(jnp.int32, (length, length), 1)
    eye = (row == col).astype(F32)[None]
    diag_blk = ((row // SUB) == (col // SUB))[None]
    nd = jnp.where(diag_blk, n_mat, 0.0)
    no = n_mat - nd
    dinv = eye + nd
    p = nd
    for _ in range(int(math.log2(SUB)) - 1):
        p = _hdot(p, p, _NN, "solve")
        dinv = dinv + _hdot(dinv, p, _NN, "solve")
    q = _hdot(dinv, no, _NN, "solve")
    u = _hdot(dinv, x, _NN, "solve")
    levels = int(math.log2(length // SUB))
    qs = [q]
    for _ in range(levels - 1):
        qs.append(_hdot(qs[-1], qs[-1], _NN, "solve"))
    for qk in reversed(qs):
        u = u + _hdot(qk, u, _NN, "solve")
    return u


def _rwkv_chunk(rev, s0, r, lw, k, v, a, b):
    pairs, length, width = r.shape
    hn = width // 2
    row = lax.broadcasted_iota(jnp.int32, (length, length), 0)
    col = lax.broadcasted_iota(jnp.int32, (length, length), 1)
    row2 = lax.broadcasted_iota(jnp.int32, (length, 2 * length), 0)
    col2 = lax.broadcasted_iota(jnp.int32, (length, 2 * length), 1)
    col2 = jnp.where(col2 >= length, col2 - length, col2)
    if rev is None:
        half = pairs // 2
        back = lax.broadcasted_iota(jnp.int32, (pairs, length, length), 0) >= half
        idx2 = lax.broadcasted_iota(jnp.int32, (2 * pairs, length, 2 * length), 0)
        back2 = ((idx2 >= half) & (idx2 < pairs)) | (idx2 >= pairs + half)
        ahead = jnp.where(back, (col - row)[None], (row - col)[None])
        ahead2 = jnp.where(back2, (col2 - row2)[None], (row2 - col2)[None])
        incl, strict2, incl2 = ahead >= 0, ahead2 > 0, ahead2 >= 0
    else:
        incl = ((row <= col) if rev else (row >= col))[None]
        strict2 = ((row2 < col2) if rev else (row2 > col2))[None]
        incl2 = ((row2 <= col2) if rev else (row2 >= col2))[None]
    lane = lax.broadcasted_iota(jnp.int32, (1, 1, width), 2)
    first = lane < hn
    head_mask = jnp.concatenate([jnp.broadcast_to(first.astype(F32), (pairs, 1, width)),
                                 jnp.broadcast_to(1.0 - first.astype(F32), (pairs, 1, width))], axis=0)
    twice = lambda t: jnp.concatenate([t, t], axis=0)
    pick = lambda t: jnp.where(first, t[:pairs], t[pairs:])

    t_incl = jnp.broadcast_to(incl.astype(F32), (pairs, length, length))
    cum = _hdot(t_incl, lw, _NN, "cum")
    g = jnp.exp(cum)
    g_inv = jnp.exp(-cum)
    at = a * jnp.exp(cum - lw)
    rt = r * g
    bt = b * g_inv
    kt = k * g_inv
    lhs = jnp.concatenate([twice(at) * head_mask, twice(rt) * head_mask], axis=1)
    rhs = jnp.concatenate([twice(bt), twice(kt)], axis=1)
    gram = _hdot(lhs, rhs, _NT, "gram")
    top = jnp.where(strict2, gram[:, :length], 0.0)
    bot = jnp.where(incl2, gram[:, length:], 0.0)
    v2 = twice(v)
    zeros = jnp.zeros_like(v2)
    x = _hdot(at, s0, _NT, "apply") + pick(_hdot(top, jnp.concatenate([zeros, v2], axis=1), _NN, "apply"))
    u = pick(_tri_solve(top[:, :, :length], twice(x), length))
    y = _hdot(rt, s0, _NT, "apply") + pick(_hdot(bot, jnp.concatenate([twice(u), v2], axis=1), _NN, "apply"))
    g_last = jnp.exp(jnp.sum(lw, axis=1, keepdims=True))
    ri = lax.broadcasted_iota(jnp.int32, (width, width), 0)
    ci = lax.broadcasted_iota(jnp.int32, (width, width), 1)
    same_head = ((ri < hn) == (ci < hn))[None]
    upd = _hdot(u, bt, _TN, "state") + _hdot(v, kt, _TN, "state")
    s1 = (s0 + jnp.where(same_head, upd, 0.0)) * g_last
    return y, s1


def _split_pairs(x):
    return jnp.stack([x[:, p * LANES:(p + 1) * LANES] for p in range(x.shape[1] // LANES)])


def _merge_pairs(x):
    return jnp.concatenate([x[p] for p in range(x.shape[0])], axis=1)


def _scan_specs(views, rw, nc, rev):
    cidx = (lambda c: nc - 1 - c) if rev else (lambda c: c)
    seqs = [pl.BlockSpec((CHUNK, rw), functools.partial(lambda c, cb: (cidx(c), cb), cb=cb)) for _, cb, _ in views]
    plain = pl.BlockSpec((CHUNK, rw), lambda c: (cidx(c), 0))
    st = pl.BlockSpec((1, rw // LANES, LANES, LANES), lambda c: (cidx(c), 0, 0, 0))
    return seqs, plain, st


def _as_views(arrs, rw):
    return [t if isinstance(t, tuple) else (t, 0, rw) for t in arrs]


def _rwkv_scan_fwd(ops_f, ops_b, rw, *, name):
    S = _as_views(ops_f, rw)[0][0].shape[0]
    nc, pairs = S // CHUNK, rw // LANES
    in_specs, out_specs, arrays = [], [], []
    for rev, ops in ((False, ops_f), (True, ops_b)):
        views = _as_views(ops, rw)
        seqs, plain, st = _scan_specs(views, rw, nc, rev)
        in_specs += seqs
        out_specs += [plain, st]
        arrays += [t[0] for t in views]

    def both(refs_f, refs_b):
        return [jnp.concatenate([_split_pairs(f[...]), _split_pairs(b[...])], axis=0) for f, b in zip(refs_f, refs_b)]

    def body(*refs):
        (y_f, st_f, y_b, st_b), s_ref = refs[12:16], refs[16]

        @pl.when(pl.program_id(0) == 0)
        def _():
            s_ref[...] = jnp.zeros_like(s_ref)

        s0 = s_ref[...]
        st_f[0] = s0[:pairs]
        st_b[0] = s0[pairs:]
        y, s1 = _rwkv_chunk(None, s0, *both(refs[:6], refs[6:12]))
        y_f[...] = _merge_pairs(y[:pairs])
        y_b[...] = _merge_pairs(y[pairs:])
        s_ref[...] = s1

    return pl.pallas_call(
        body, name=name, grid=(nc,), in_specs=in_specs, out_specs=out_specs,
        out_shape=[jax.ShapeDtypeStruct((S, rw), F32), jax.ShapeDtypeStruct((nc, pairs, LANES, LANES), F32)] * 2,
        scratch_shapes=[pltpu.VMEM((2 * pairs, LANES, LANES), F32)],
        compiler_params=_cparams(("arbitrary",)),
    )(*arrays)


def _rwkv_scan_bwd(ops_f, ops_b, states_f, states_b, dy, rw, *, name):
    S = dy.shape[0]
    nc, pairs = S // CHUNK, rw // LANES
    in_specs, arrays = [], []
    for rev, ops, states in ((False, ops_f, states_f), (True, ops_b, states_b)):
        views = _as_views(list(ops) + [dy], rw)
        seqs, plain, st = _scan_specs(views, rw, nc, not rev)
        in_specs += seqs + [st]
        arrays += [t[0] for t in views] + [states]
    out_specs = []
    for rev in (False, True):
        out_specs += [_scan_specs([], rw, nc, not rev)[1]] * 6

    def both(refs_f, refs_b):
        return [jnp.concatenate([_split_pairs(f[...]), _split_pairs(b[...])], axis=0) for f, b in zip(refs_f, refs_b)]

    def body(*refs):
        ds_ref = refs[28]

        @pl.when(pl.program_id(0) == 0)
        def _():
            ds_ref[...] = jnp.zeros_like(ds_ref)

        s0 = jnp.concatenate([refs[7][0], refs[15][0]], axis=0)
        _, vjp = jax.vjp(functools.partial(_rwkv_chunk, None), s0, *both(refs[:6], refs[8:14]))
        (dy,) = both(refs[6:7], refs[14:15])
        grads = vjp((dy, ds_ref[...]))
        ds_ref[...] = grads[0]
        for o_f, o_b, gval in zip(refs[16:22], refs[22:28], grads[1:]):
            o_f[...] = _merge_pairs(gval[:pairs])
            o_b[...] = _merge_pairs(gval[pairs:])

    return pl.pallas_call(
        body, name=name, grid=(nc,), in_specs=in_specs, out_specs=out_specs,
        out_shape=[jax.ShapeDtypeStruct((S, rw), F32)] * 12,
        scratch_shapes=[pltpu.VMEM((2 * pairs, LANES, LANES), F32)],
        compiler_params=_cparams(("arbitrary",)),
    )(*arrays)


def _shift_lerp(x_view, mu, d=None, into=None, *, name):
    arr, off, width = x_view
    S = arr.shape[0]
    cb = _pick(width, 256)
    assert off % cb == 0

    def cshift(t):
        rows = lax.broadcasted_iota(jnp.int32, t.shape, 0)
        prev = jnp.where(rows == 0, 0.0, pltpu.roll(t, 1, 0))
        nxt = jnp.where(rows == S - 1, 0.0, pltpu.roll(t, S - 1, 0))
        return 0.5 * (prev + nxt)

    def fwd_body(x_ref, mu_ref, o_ref):
        x = x_ref[...]
        o_ref[...] = x + mu_ref[...] * (cshift(x) - x)

    def bwd_body(x_ref, mu_ref, d_ref, _, dx_ref, dmu_ref):
        x, m, dd = x_ref[...], mu_ref[...], d_ref[...]
        gm = m * dd
        dx_ref[...] = (dd - gm + cshift(gm)).astype(dx_ref.dtype)
        dmu_ref[...] = jnp.sum(dd * (cshift(x) - x), axis=0, keepdims=True)

    x_spec = pl.BlockSpec((S, cb), lambda j: (0, off // cb + j))
    blk = pl.BlockSpec((S, cb), lambda j: (0, j))
    vec = pl.BlockSpec((1, cb), lambda j: (0, j))
    if d is None:
        return pl.pallas_call(
            fwd_body, name=name, grid=(width // cb,), in_specs=[x_spec, vec], out_specs=blk,
            out_shape=jax.ShapeDtypeStruct((S, width), F32), compiler_params=_cparams(("parallel",)),
        )(arr, mu)
    buf, first = into
    assert first % cb == 0
    return pl.pallas_call(
        bwd_body, name=name, grid=(width // cb,),
        in_specs=[x_spec, vec, blk, pl.BlockSpec(memory_space=pl.ANY)],
        out_specs=[pl.BlockSpec((S, cb), lambda j: (0, first // cb + j)), vec],
        out_shape=[jax.ShapeDtypeStruct(buf.shape, buf.dtype), jax.ShapeDtypeStruct((1, width), F32)],
        input_output_aliases={3: 0}, compiler_params=_cparams(("parallel",)),
    )(arr, mu, d, buf)


def _attention_fwd(qfull, kv, kr, hm, scale, *, tq, name):
    S = qfull.shape[0]
    nt = (((1,), (1,)), ((), ()))

    def body(q_ref, kn_ref, kr_ref, v_ref, o_ref, lse_ref, k_scr):
        _head_keys(kn_ref, kr_ref, k_scr)
        s = lax.dot_general(q_ref[...], k_scr[...], nt, preferred_element_type=F32)
        m = jnp.max(s, axis=-1, keepdims=True)
        p = jnp.exp((s - m) * scale)
        l = jnp.sum(p, axis=-1, keepdims=True)
        o_ref[...] = jnp.dot(p.astype(BF16), v_ref[...], preferred_element_type=F32) * (1.0 / l)
        lse_ref[...] = jnp.broadcast_to(m * scale + jnp.log(l), lse_ref.shape)

    oblk = pl.BlockSpec((tq, VDIM), lambda h, i: (i, h))
    return pl.pallas_call(
        body, name=name, grid=(hm, S // tq),
        in_specs=[pl.BlockSpec((tq, QHEAD), lambda h, i: (i, h)),
                  pl.BlockSpec((S, NOPE), lambda h, i: (0, h)),
                  pl.BlockSpec((S, LANES), lambda h, i: (0, 0)),
                  pl.BlockSpec((S, VDIM), lambda h, i: (0, hm + h))],
        out_specs=[oblk, oblk],
        out_shape=[jax.ShapeDtypeStruct((S, hm * VDIM), F32)] * 2,
        scratch_shapes=[pltpu.VMEM((S, QHEAD), BF16)],
        compiler_params=_cparams(("parallel", "arbitrary")),
    )(qfull, kv, kr, kv)


def _head_keys(kn_ref, kr_ref, k_scr):
    @pl.when(pl.program_id(1) == 0)
    def _():
        k_scr[:, :NOPE] = kn_ref[...]
        k_scr[:, NOPE:] = kr_ref[...]


def _attention_bwd(qfull, kv, kr, o, lse, d_o, hm, scale, *, tq, name):
    S = qfull.shape[0]
    tq = min(tq, S)
    nq = S // tq
    tn = (((0,), (0,)), ((), ()))
    nt = (((1,), (1,)), ((), ()))

    def body(q_ref, kn_ref, kr_ref, v_ref, o_ref, lse_ref, do_ref, dq_ref, dk_ref, dv_ref, k_scr):
        _head_keys(kn_ref, kr_ref, k_scr)
        s = lax.dot_general(q_ref[...], k_scr[...], nt, preferred_element_type=F32)
        p = jnp.exp(s * scale - lse_ref[:, 0:1])
        d_out = do_ref[...]
        delta = jnp.sum(d_out * o_ref[...], axis=-1, keepdims=True)
        d_out = d_out.astype(BF16)
        dp = lax.dot_general(d_out, v_ref[...], nt, preferred_element_type=F32)
        ds = (p * (dp - delta)).astype(BF16)
        dq_ref[...] = jnp.dot(ds, k_scr[...], preferred_element_type=F32) * scale
        dv = lax.dot_general(p.astype(BF16), d_out, tn, preferred_element_type=F32)
        dk = lax.dot_general(ds, q_ref[...], tn, preferred_element_type=F32)
        i = pl.program_id(1)
        for ref, val in ((dk_ref, dk), (dv_ref, dv)):
            @pl.when(i == 0)
            def _(ref=ref, val=val):
                ref[...] = val

            @pl.when(i > 0)
            def _(ref=ref, val=val):
                ref[...] += val

        @pl.when(i == nq - 1)
        def _():
            dk_ref[...] = dk_ref[...] * scale

    qblk = pl.BlockSpec((tq, QHEAD), lambda h, i: (i, h))
    oblk = pl.BlockSpec((tq, VDIM), lambda h, i: (i, h))
    return pl.pallas_call(
        body, name=name, grid=(hm, nq),
        in_specs=[qblk,
                  pl.BlockSpec((S, NOPE), lambda h, i: (0, h)),
                  pl.BlockSpec((S, LANES), lambda h, i: (0, 0)),
                  pl.BlockSpec((S, VDIM), lambda h, i: (0, hm + h)),
                  oblk, oblk, oblk],
        out_specs=[qblk, pl.BlockSpec((S, QHEAD), lambda h, i: (0, h)), pl.BlockSpec((S, VDIM), lambda h, i: (0, h))],
        out_shape=[jax.ShapeDtypeStruct((S, hm * QHEAD), F32), jax.ShapeDtypeStruct((S, hm * QHEAD), F32),
                   jax.ShapeDtypeStruct((S, hm * VDIM), F32)],
        scratch_shapes=[pltpu.VMEM((S, QHEAD), BF16)],
        compiler_params=_cparams(("parallel", "arbitrary")),
    )(qfull, kv, kr, kv, o, lse, d_o)


def _layout(D, MW, RW, TAIL, QR, KVR):
    names = ["gate_m", "gate_r", "z_m", "z_r", "q_a", "kv_a", "r", "k", "v", "tail"]
    widths = [D, D, MW, RW, QR, KVR, RW, RW, RW, TAIL]
    offs, o = {}, 0
    for nme, w in zip(names, widths):
        assert o % w == 0, (nme, o, w)
        offs[nme] = (o, w)
        o += w
    return offs, o


def _local_grads(x, target, W, dims, exchange=None):
    S, D = x.shape
    hm, hr, hn, rank = dims["hm"], dims["hr"], dims["hn"], dims["rank"]
    MW, RW = hm * VDIM, hr * hn
    TAIL = dims["TAIL"]
    QR, KVR = W["mla_q_norm"].shape[1], W["mla_kv_norm"].shape[1]
    lay, d_in = _layout(D, MW, RW, TAIL, QR, KVR)
    T = 256
    scale = (NOPE + ROPE) ** -0.5
    col = lambda arr, nme: _view(arr, *lay[nme])

    pos = jnp.arange(S, dtype=F32)
    inv_freq = jnp.power(ROPE_THETA, -jnp.arange(0, ROPE, 2, dtype=F32) / ROPE)
    ang = pos[:, None] * inv_freq[None, :]
    zpad = jnp.zeros((S, LANES - ROPE), F32)
    cosx = jnp.concatenate([jnp.cos(ang), jnp.cos(ang), zpad], axis=1)
    sinx = jnp.concatenate([jnp.sin(ang), jnp.sin(ang), zpad], axis=1)
    ri, ci = jnp.arange(LANES)[:, None], jnp.arange(LANES)[None, :]
    half = ROPE // 2
    rot = (jnp.where((ri == ci - half) & (ci >= half) & (ci < ROPE), 1.0, 0.0)
           - jnp.where((ri == ci + half) & (ci < half), 1.0, 0.0)).astype(BF16)
    rot_t = rot.T
    seg = (jnp.arange(RW)[:, None] // hn == jnp.arange(LANES)[None, :]).astype(BF16)
    seg_t = seg.T

    (h,) = _rowwise(lambda xb, g: (_rms(xb, g),), [x], [W["g_pre"]], [(D, BF16)], tile=T, name="pre_norm")
    if exchange is None:
        proj = _mm(h, W["w_in_t"], tb=True, name="in_proj")
    else:
        proj, *slabs = _mm(h, W["w_in_t"], tb=True, ride=_gather_plan(exchange[0]), name="in_proj")
        W = {**W, **_prepare_rest(dict(zip(_MATS[1:], slabs)), dims)}

    qn, kvn = _rowwise(_f_mla_norm, [col(proj, "q_a"), col(proj, "kv_a")], [W["mla_q_norm"], W["mla_kv_norm"]],
                       [(QR, BF16), (KVR, BF16)], tile=T, name="mla_norm")
    qraw = _mm(qn, W["wq_b_t"], tb=True, name="q_up")
    kv = _mm(kvn, W["wkv_b"], out_dtype=BF16, name="kv_up")
    kr_view = _view(proj, lay["tail"][0], LANES)
    qfull, kr = _rowwise(functools.partial(_f_rope, hm), [qraw, kr_view, cosx, sinx], [rot, rot_t],
                         [(hm * QHEAD, BF16), (LANES, BF16)], tile=T, name="rope")
    o_mla, lse = _attention_fwd(qfull, kv, kr, hm, scale, tq=T, name="attn_fwd")

    shift_view = (proj, lay["r"][0], 3 * RW + TAIL)
    rl = _shift_lerp(shift_view, W["mu"], name="shift_fwd")
    rl_r, rl_k, rl_v = _view(rl, 0, RW), _view(rl, RW, RW), _view(rl, 2 * RW, RW)
    rl_tail = _view(rl, 3 * RW, TAIL)
    pre_params = [W["w0_f"], W["w0_b"], W["a0_f"], W["a0_b"], W["k_k"], W["k_a"], W["w2cat"], W["a2cat"], seg, seg_t]
    pre_fn = functools.partial(_f_rwkv_pre, RW)
    lw_f, lw_b, k_f, k_b, a_n, b_f, b_b = _rowwise(pre_fn, [rl_k, rl_tail], pre_params, [(RW, F32)] * 7, tile=T,
                                                    name="rwkv_pre")
    ops_f = (rl_r, lw_f, k_f, rl_v, a_n, b_f)
    ops_b = (rl_r, lw_b, k_b, rl_v, a_n, b_b)
    y_f, st_f, y_b, st_b = _rwkv_scan_fwd(ops_f, ops_b, RW, name="scan_fwd")

    post_fn = functools.partial(_f_post, hn)
    post_rows = [y_f, y_b, rl_r, k_f, k_b, rl_v, col(proj, "z_r"), o_mla, col(proj, "z_m")]
    post_params = [W["gn_g"], W["gn_b"], W["r_k"], seg, seg_t]
    ymg, yrg = _rowwise(post_fn, post_rows, post_params, [(MW, BF16), (RW, BF16)], tile=T, name="post")
    u_m = _mm(ymg, W["w_br_mla"], name="br_mla")
    u_r = _mm(yrg, W["w_br_rwkv"], name="br_rwkv")
    merge_rows = [u_m, u_r, col(proj, "gate_m"), col(proj, "gate_r")]
    (merged,) = _rowwise(lambda *t: (_f_merge(*t),), merge_rows, [], [(D, BF16)], tile=T, name="merge")
    out = _mm(merged, W["w_out"], name="out_proj")

    def head(ob, xb, tb, g):
        yn, vjp = jax.vjp(_rms, ob, g)
        err = xb + yn - tb
        dy = err * (1.0 / D)
        d_ob, d_g = vjp(dy)
        loss = jnp.broadcast_to(0.5 * jnp.sum(err * err) * (1.0 / D), (1, LANES))
        return dy, d_ob, loss, d_g

    dy, d_out, loss, g_g_post = _rowwise(head, [out, x, target], [W["g_post"]], [(D, F32), (D, BF16)],
                                         [(1, LANES), (1, D)], tile=T, name="head")
    d_merged = _mm(d_out, W["w_out"], tb=True, name="d_merged")
    g_w_out = _mm(merged, d_out, ta=True, out_dtype=BF16, name="g_w_out")

    def merge_bwd(u_m_b, u_r_b, g_m_b, g_r_b, dm):
        _, vjp = jax.vjp(_f_merge, u_m_b, u_r_b, g_m_b, g_r_b)
        du_m, du_r, dg_m, dg_r = vjp(dm)
        return du_m, du_r, jnp.concatenate([dg_m, dg_r], axis=1)

    d_u_m, d_u_r, d_proj = _rowwise(merge_bwd, merge_rows + [d_merged], [],
                                    [(D, BF16), (D, BF16), (2 * D, BF16, (None, d_in, lay["gate_m"][0]))], tile=T,
                                    name="merge_bwd")
    d_ymg = _mm(d_u_m, W["w_br_mla"], tb=True, name="d_ymg")
    d_yrg = _mm(d_u_r, W["w_br_rwkv"], tb=True, name="d_yrg")
    g_w_br_mla = _mm(ymg, d_u_m, ta=True, out_dtype=BF16, name="g_w_br_mla")
    g_w_br_rwkv = _mm(yrg, d_u_r, ta=True, out_dtype=BF16, name="g_w_br_rwkv")

    def post_bwd(*args):
        nr = len(post_rows)
        prim, dm, dr = args[:nr] + args[nr + 2:], args[nr], args[nr + 1]
        _, vjp = jax.vjp(post_fn, *prim)
        g = vjp((dm, dr))
        return g[0], g[2], g[3], g[5], g[7], jnp.concatenate([g[8], g[6]], axis=1), g[9], g[10], g[11]

    (d_y, d_r_bonus, d_k_bonus, d_v_bonus, d_o, d_proj, g_gn_g, g_gn_b, g_r_k) = _rowwise(
        post_bwd, post_rows + [d_ymg, d_yrg], post_params,
        [(RW, F32), (RW, F32), (RW, F32), (RW, F32), (MW, F32), (MW + RW, BF16, (d_proj, d_in, lay["z_m"][0]))],
        [(1, RW)] * 3, tile=T // 2, name="post_bwd")

    dscan = _rwkv_scan_bwd(ops_f, ops_b, st_f, st_b, d_y, RW, name="scan_bwd")
    dsc = {"f": dscan[:6], "b": dscan[6:]}

    d_q_att, d_k_att, d_v_att = _attention_bwd(qfull, kv, kr, o_mla, lse, d_o, hm, scale, tq=2 * T, name="attn_bwd")

    def rope_bwd(qraw_b, kr_in, cos_b, sin_b, dq_b, dk_b, dv_b, rot_b, rot_t_b):
        _, vjp = jax.vjp(lambda q_, k_: _f_rope(hm, q_, k_, cos_b, sin_b, rot_b, rot_t_b), qraw_b, kr_in)
        dkn = jnp.concatenate([dk_b[:, hh * QHEAD:hh * QHEAD + NOPE] for hh in range(hm)], axis=1)
        dkr = dk_b[:, NOPE:QHEAD]
        for hh in range(1, hm):
            dkr = dkr + dk_b[:, hh * QHEAD + NOPE:(hh + 1) * QHEAD]
        d_qraw, d_kr_in = vjp((dq_b, dkr))
        return d_qraw, jnp.concatenate([dkn, dv_b], axis=1), d_kr_in

    d_qraw, d_kv, d_kr_in = _rowwise(rope_bwd, [qraw, kr_view, cosx, sinx, d_q_att, d_k_att, d_v_att],
                                     [rot, rot_t], [(hm * QHEAD, BF16), (2 * MW, BF16), (LANES, F32)], tile=T,
                                     name="rope_bwd")
    d_qnorm = _mm(d_qraw, W["wq_b_t"], name="d_qn")
    d_kvnorm = _mm(d_kv, W["wkv_b"], tb=True, name="d_kvn")
    g_wq_b = _mm(d_qraw, qn, ta=True, out_dtype=BF16, name="g_wq_b")
    g_wkv_b = _mm(kvn, d_kv, ta=True, out_dtype=BF16, name="g_wkv_b")

    def mla_norm_bwd(q_a, kv_a, qg, kvg, dq, dk):
        _, vjp = jax.vjp(_f_mla_norm, q_a, kv_a, qg, kvg)
        d_q_a, d_kv_a, d_qg, d_kvg = vjp((dq, dk))
        return jnp.concatenate([d_q_a, d_kv_a], axis=1), d_qg, d_kvg

    d_proj, g_q_norm, g_kv_norm = _rowwise(
        lambda q_a, kv_a, dq, dk, qg, kvg: mla_norm_bwd(q_a, kv_a, qg, kvg, dq, dk),
        [col(proj, "q_a"), col(proj, "kv_a"), d_qnorm, d_kvnorm], [W["mla_q_norm"], W["mla_kv_norm"]],
        [(QR + KVR, BF16, (d_proj, d_in, lay["q_a"][0]))], [(1, QR), (1, KVR)], tile=T, name="mla_norm_bwd")

    def pre_bwd(k_b_, tail_b, dlwf, dlwb, dkf, dkb, dkbon, daf, dab, dbf, dbb, drf, drb, drbon, dvf, dvb, dvbon,
                dkr, *params):
        w2, a2 = params[6], params[7]
        nt, tn = (((1,), (1,)), ((), ())), (((0,), (0,)), ((), ()))
        split = w2.shape[0]
        th = jnp.tanh(tail_b[:, :split])
        th_b, tail_h = th.astype(BF16), tail_b[:, split:].astype(BF16)
        zw = jnp.dot(th_b, w2, preferred_element_type=F32)
        za = jnp.dot(tail_h, a2, preferred_element_type=F32)
        _, vjp = jax.vjp(functools.partial(_f_rwkv_core, RW), k_b_, zw, za, *params[:6], params[8], params[9])
        g = vjp((dlwf, dlwb, dkf + dkbon, dkb + dkbon, daf + dab, dbf, dbb))
        d_zw, d_za = g[1].astype(BF16), g[2].astype(BF16)
        d_tail = (jnp.concatenate([lax.dot_general(d_zw, w2, nt, preferred_element_type=F32) * (1.0 - th * th),
                                   lax.dot_general(d_za, a2, nt, preferred_element_type=F32)], axis=1)
                  + jnp.concatenate([dkr, jnp.zeros((dkr.shape[0], TAIL - LANES), F32)], axis=1))
        g_w2 = lax.dot_general(th_b, d_zw, tn, preferred_element_type=F32)
        g_a2 = lax.dot_general(tail_h, d_za, tn, preferred_element_type=F32)
        d_rl = jnp.concatenate([drf + drb + drbon, g[0], dvf + dvb + dvbon, d_tail], axis=1)
        return (d_rl,) + tuple(g[3:9]) + (g_w2, g_a2)

    f_, b_ = dsc["f"], dsc["b"]
    pre_bwd_rows = [rl_k, rl_tail, f_[1], b_[1], f_[2], b_[2], d_k_bonus, f_[4], b_[4], f_[5], b_[5],
                    f_[0], b_[0], d_r_bonus, f_[3], b_[3], d_v_bonus, d_kr_in]
    (d_rl, g_w0_f, g_w0_b, g_a0_f, g_a0_b, g_k_k, g_k_a, g_w2cat, g_a2cat) = _rowwise(
        pre_bwd, pre_bwd_rows, pre_params, [(3 * RW + TAIL, F32)],
        [(1, RW)] * 6 + [W["w2cat"].shape, W["a2cat"].shape], tile=T // 2, name="rwkv_pre_bwd")
    d_proj, g_mu = _shift_lerp(shift_view, W["mu"], d_rl, (d_proj, lay["r"][0]), name="shift_bwd")
    small = dict(wq_b=g_wq_b, wkv_b=g_wkv_b, w2cat=g_w2cat, a2cat=g_a2cat, w_br_mla=g_w_br_mla,
                 w_br_rwkv=g_w_br_rwkv, w_out=g_w_out)
    if exchange is None:
        received = None
        g_w_in = _mm(d_proj, h, ta=True, out_dtype=BF16, tn_cap=1024, name="g_w_in")
        d_h = _mm(d_proj, W["w_in_t"], tn_cap=1024, name="d_h")
    else:
        slabs = _restore_rest(small, dims)
        slabs = [slabs[n] for n in _MATS[1:]]
        g_w_in, *got = _mm(d_proj, h, ta=True, out_dtype=BF16, tn_cap=1024, ride=_sibling_swap_plan(slabs),
                           name="g_w_in")
        sums = [_pair_add(exchange[1], s, t, name="pair_add_" + n) for n, s, t in zip(_MATS[1:], slabs, got)]
        g_w_in = _restore_w_in(g_w_in, dims)
        d_h, *received = _mm(d_proj, W["w_in_t"], tn_cap=1024, name="d_h",
                             ride=_join_plans(_chip_exchange_plan(sums), _sibling_swap_plan([g_w_in])))
        small = {}

    def pre_norm_bwd(xb, dyb, dhb, g):
        _, vjp = jax.vjp(_rms, xb, g)
        dx, dg = vjp(dhb)
        return dyb + dx, dg

    grad_x, g_g_pre = _rowwise(pre_norm_bwd, [x, dy, d_h], [W["g_pre"]], [(D, F32)], [(1, D)], tile=T,
                               name="pre_norm_bwd")

    grads = dict(g_pre=g_g_pre, w_in=g_w_in, mla_q_norm=g_q_norm, mla_kv_norm=g_kv_norm, mu=g_mu, w0_f=g_w0_f,
                 w0_b=g_w0_b, a0_f=g_a0_f, a0_b=g_a0_b, k_k=g_k_k, k_a=g_k_a, r_k=g_r_k, gn_g=g_gn_g, gn_b=g_gn_b,
                 g_post=g_g_post, **small)
    return loss[0, 0], grad_x, grads, received


_MATS = ["w_in", "mla_wq_b", "mla_wkv_b", "rwkv_w2_f", "rwkv_w2_b", "rwkv_a2_f", "rwkv_a2_b", "w_br_mla",
         "w_br_rwkv", "w_out"]
_ROW_SHARDED = ("w_out",)
_TRANSPOSED = ("w_in", "mla_wq_b")
_VECS = ["g_pre", "mla_q_norm", "mla_kv_norm", "rwkv_mu", "rwkv_w0_f", "rwkv_w0_b", "rwkv_a0_f", "rwkv_a0_b",
         "rwkv_k_k", "rwkv_k_a", "rwkv_r_k", "rwkv_gn_g", "rwkv_gn_b", "g_post"]
_WEIGHTS = ["g_pre", "w_in", "mla_q_norm", "mla_wq_b", "mla_kv_norm", "mla_wkv_b", "rwkv_mu", "rwkv_w0_f",
            "rwkv_w2_f", "rwkv_w0_b", "rwkv_w2_b", "rwkv_a0_f", "rwkv_a2_f", "rwkv_a0_b", "rwkv_a2_b", "rwkv_k_k",
            "rwkv_k_a", "rwkv_r_k", "rwkv_gn_g", "rwkv_gn_b", "w_br_mla", "w_br_rwkv", "w_out", "g_post"]

def _direct_gather_plan(src):
    def phases(src_refs, out_refs, sem_refs):
        (src_ref,), (out_ref,), sems, local_sem = src_refs, out_refs, sem_refs[:2], sem_refs[2]
        x, y, c = lax.axis_index("x"), lax.axis_index("y"), lax.axis_index("c")
        me = 4 * x + 2 * y + c
        flip = lambda v, bit: (1 - v) if bit else v
        peers = [(flip(x, d & 4), flip(y, d & 2), flip(c, d & 1)) for d in range(1, N_DEV)]
        own = lambda: pltpu.make_async_copy(src_ref, out_ref.at[me], local_sem)
        send = lambda d: _remote(src_ref, out_ref.at[me], sems, d, peers[d])

        def first():
            own().start()
            for d in range(N_DEV - 1):
                send(d).start()

        def last():
            for d, (px, py, pc) in enumerate(peers):
                blk = out_ref.at[4 * px + 2 * py + pc]
                _remote(blk, blk, sems, d, (x, y, c)).wait_recv()
            for d in range(N_DEV - 1):
                send(d).wait_send()
            own().wait()

        return first, (lambda: None), last

    return [src], [jax.ShapeDtypeStruct((N_DEV,) + src.shape, src.dtype)], [(N_DEV - 1,), (N_DEV - 1,), ()], phases


def _remote(src, dst, sems, key, to):
    send_sems, recv_sems = sems
    return pltpu.make_async_remote_copy(src_ref=src, dst_ref=dst, send_sem=send_sems.at[key], recv_sem=recv_sems.at[key],
                                        device_id=to, device_id_type=pl.DeviceIdType.MESH)


def _run_exchange(plan, *, name):
    srcs, out_shapes, sem_shapes, phases = plan
    n, m = len(srcs), len(out_shapes)

    def body(*refs):
        for phase in phases(refs[:n], refs[n:n + m], refs[n + m:]):
            phase()

    return pl.pallas_call(
        body, name=name, out_shape=out_shapes,
        in_specs=[pl.BlockSpec(memory_space=pl.ANY)] * n, out_specs=[pl.BlockSpec(memory_space=pl.ANY)] * m,
        scratch_shapes=[pltpu.SemaphoreType.DMA(s) for s in sem_shapes],
    )(*srcs)


def _join_plans(p, q):
    (srcs_p, outs_p, sems_p, phases_p), (srcs_q, outs_q, sems_q, phases_q) = p, q

    def phases(src_refs, out_refs, sem_refs):
        a = phases_p(src_refs[:len(srcs_p)], out_refs[:len(outs_p)], sem_refs[:len(sems_p)])
        b = phases_q(src_refs[len(srcs_p):], out_refs[len(outs_p):], sem_refs[len(sems_p):])

        def both(fa, fb):
            def run():
                fa()
                fb()
            return run

        return tuple(both(fa, fb) for fa, fb in zip(a, b))

    return list(srcs_p) + list(srcs_q), list(outs_p) + list(outs_q), list(sems_p) + list(sems_q), phases


def _gather_plan(srcs):
    n = len(srcs)

    def phases(src_refs, out_refs, sem_refs):
        sems, local_sems = sem_refs[:2], sem_refs[2]
        x, y, c = lax.axis_index("x"), lax.axis_index("y"), lax.axis_index("c")
        idx = lambda px, py, pc: 4 * px + 2 * py + pc
        me, sibling = (x, y, c), (x, y, 1 - c)
        chips = [(1 - x, y), (x, 1 - y), (1 - x, 1 - y)]
        own = lambda a: pltpu.make_async_copy(src_refs[a], out_refs[a].at[idx(*me)], local_sems.at[a])
        to_sibling = lambda a: _remote(src_refs[a], out_refs[a].at[idx(*me)], sems, (0, a), sibling)
        to_chip = lambda a, j: _remote(src_refs[a], out_refs[a].at[idx(*me)], sems, (1 + j, a), (*chips[j], c))
        landed = lambda a, j: out_refs[a].at[idx(*chips[j], c)]
        passed_on = lambda a, j: _remote(landed(a, j), landed(a, j), sems, (4 + j, a), sibling)

        def first():
            for a in range(n):
                own(a).start()
                to_sibling(a).start()
                for j in range(3):
                    to_chip(a, j).start()

        def middle():
            for j in range(3):
                for a in range(n):
                    _remote(landed(a, j), landed(a, j), sems, (1 + j, a), me).wait_recv()
                    passed_on(a, j).start()

        def last():
            for a in range(n):
                blk = out_refs[a].at[idx(*sibling)]
                _remote(blk, blk, sems, (0, a), me).wait_recv()
                for j in range(3):
                    blk = out_refs[a].at[idx(*chips[j], 1 - c)]
                    _remote(blk, blk, sems, (4 + j, a), me).wait_recv()
            for a in range(n):
                to_sibling(a).wait_send()
                for j in range(3):
                    to_chip(a, j).wait_send()
                    passed_on(a, j).wait_send()
                own(a).wait()

        return first, middle, last

    return srcs, [jax.ShapeDtypeStruct((N_DEV,) + s.shape, s.dtype) for s in srcs], [(7, n), (7, n), (n,)], phases


def _sibling_swap_plan(srcs):
    n = len(srcs)

    def phases(src_refs, out_refs, sems):
        x, y, c = lax.axis_index("x"), lax.axis_index("y"), lax.axis_index("c")
        copies = lambda: [_remote(src_refs[a].at[2 * q + 1 - c], out_refs[a].at[q], sems, (q, a), (x, y, 1 - c))
                          for a in range(n) for q in range(4)]

        def first():
            for cp in copies():
                cp.start()

        def last():
            for cp in copies():
                cp.wait()

        return first, (lambda: None), last

    return srcs, [jax.ShapeDtypeStruct((4,) + s.shape[1:], s.dtype) for s in srcs], [(4, n), (4, n)], phases


def _chip_exchange_plan(srcs):
    n = len(srcs)

    def phases(src_refs, out_refs, sem_refs):
        sems, local_sems = sem_refs[:2], sem_refs[2]
        x, y, c = lax.axis_index("x"), lax.axis_index("y"), lax.axis_index("c")
        mine = 2 * x + y
        chips = [(1 - x, y), (x, 1 - y), (1 - x, 1 - y)]
        own = lambda a: pltpu.make_async_copy(src_refs[a].at[mine], out_refs[a].at[mine], local_sems.at[a])
        send = lambda a, j: _remote(src_refs[a].at[2 * chips[j][0] + chips[j][1]], out_refs[a].at[mine], sems, (j, a),
                                    (*chips[j], c))

        def first():
            for a in range(n):
                own(a).start()
                for j in range(3):
                    send(a, j).start()

        def last():
            for j in range(3):
                for a in range(n):
                    blk = out_refs[a].at[2 * chips[j][0] + chips[j][1]]
                    _remote(blk, blk, sems, (j, a), (x, y, c)).wait_recv()
            for a in range(n):
                for j in range(3):
                    send(a, j).wait_send()
                own(a).wait()

        return first, (lambda: None), last

    return srcs, [jax.ShapeDtypeStruct(s.shape, s.dtype) for s in srcs], [(3, n), (3, n), (n,)], phases


def _pair_add(core, g, got, *, name):
    q, r, c = got.shape
    tr, tc = _tile2d(r, c, cap=1024)

    def body(core_ref, a_ref, b_ref, o_ref):
        o_ref[...] = (a_ref[...].astype(F32) + b_ref[...].astype(F32)).astype(BF16)

    blk = pl.BlockSpec((1, tr, tc), lambda i, j, k, core_ref: (i, j, k))
    mine = pl.BlockSpec((1, tr, tc), lambda i, j, k, core_ref: (2 * i + core_ref[0], j, k))
    return pl.pallas_call(
        body, name=name, out_shape=jax.ShapeDtypeStruct(got.shape, BF16),
        grid_spec=pltpu.PrefetchScalarGridSpec(num_scalar_prefetch=1, grid=(q, r // tr, c // tc),
                                               in_specs=[mine, blk], out_specs=blk),
        compiler_params=_cparams(("parallel", "parallel", "parallel")))(core, g, got)


def _adamw(recv, w, m, v, *, name):
    r, c = w.shape
    n_terms = recv.shape[0]
    tr, tc = _tile2d(r, c)

    def body(g_ref, w_ref, m_ref, v_ref, go_ref, d_ref, mo_ref, vo_ref):
        g = g_ref[0].astype(F32)
        for k in range(1, n_terms):
            g = g + g_ref[k].astype(F32)
        m_new = ADAM_B1 * m_ref[...] + (1.0 - ADAM_B1) * g
        v_new = ADAM_B2 * v_ref[...] + (1.0 - ADAM_B2) * (g * g)
        m_hat = m_new / (1.0 - ADAM_B1 ** ADAM_STEP)
        v_hat = v_new / (1.0 - ADAM_B2 ** ADAM_STEP)
        go_ref[...] = g
        d_ref[...] = -ADAM_LR * (m_hat / (jnp.sqrt(v_hat) + ADAM_EPS) + ADAM_WD * w_ref[...])
        mo_ref[...] = m_new
        vo_ref[...] = v_new

    blk = pl.BlockSpec((tr, tc), lambda i, j: (i, j))
    return pl.pallas_call(
        body, name=name, grid=(r // tr, c // tc),
        in_specs=[pl.BlockSpec((n_terms, tr, tc), lambda i, j: (0, i, j)), blk, blk, blk], out_specs=[blk] * 4,
        out_shape=[jax.ShapeDtypeStruct((r, c), F32)] * 4, compiler_params=_cparams(("parallel", "parallel")),
    )(recv, w, m, v)


def _tile2d(r, c, cap=256):
    if r <= cap:
        return r, c
    for t in range(cap - cap % BF16_ROWS, 0, -BF16_ROWS):
        if r % t == 0:
            return t, c
    return r, _pick(c, cap)


def _pack(pieces):
    total = sum(p.shape[0] for p in pieces)
    pad = (-total) % (8 * LANES)
    flat = jnp.concatenate(list(pieces) + [jnp.zeros((pad,), F32)])
    return flat.reshape(-1, LANES)


def _unpack(flat, sizes):
    flat = flat.reshape(-1)
    out, o = [], 0
    for n in sizes:
        out.append(flat[o:o + n])
        o += n
    return out


def _prepare_weights(full, vec, dims):
    rest = {n: t for n, t in full.items() if n != "w_in"}
    return {"w_in_t": _prepare_w_in(full["w_in"], dims), **_prepare_rest(rest, dims), **_prepare_vectors(vec, dims)}


def _prepare_w_in(slabs, dims):
    D = dims["D"]
    flat = slabs.reshape(-1, D)
    parts, pos = [], 0
    for orig_off, width, perm_off in sorted(dims["segs"], key=lambda t: t[2]):
        if perm_off > pos:
            parts.append(jnp.zeros((perm_off - pos, D), BF16))
        parts.append(flat[orig_off:orig_off + width])
        pos = perm_off + width
    if dims["d_in_perm"] > pos:
        parts.append(jnp.zeros((dims["d_in_perm"] - pos, D), BF16))
    return jnp.concatenate(parts, axis=0)


def _prepare_rest(full, dims):
    hm, hr, hn, rank = dims["hm"], dims["hr"], dims["hn"], dims["rank"]
    QR, KVR = dims["QR"], dims["KVR"]
    RW, TAIL = hr * hn, dims["TAIL"]
    full = {n: (t.reshape(-1, t.shape[2]) if n in _ROW_SHARDED + _TRANSPOSED
                else t.transpose(1, 0, 2).reshape(t.shape[1], -1)) for n, t in full.items()}
    wq = full["mla_wq_b"].reshape(hm, NOPE + ROPE, QR)
    wq = jnp.concatenate([wq, jnp.zeros((hm, QHEAD - NOPE - ROPE, QR), BF16)], axis=1).reshape(hm * QHEAD, QR)
    wkv = full["mla_wkv_b"].reshape(KVR, hm, 2, NOPE).transpose(0, 2, 1, 3).reshape(KVR, 2 * hm * NOPE)
    z = lambda rows: jnp.zeros((rows, RW), BF16)
    f = lambda nme: full[nme]
    split = ROPE + 2 * rank
    assert split % LANES == 0, split
    w2cat = jnp.concatenate([
        jnp.concatenate([z(ROPE), f("rwkv_w2_f"), z(rank)], axis=0),
        jnp.concatenate([z(ROPE + rank), f("rwkv_w2_b")], axis=0)], axis=1)
    a2cat = jnp.concatenate([
        jnp.concatenate([f("rwkv_a2_f"), z(TAIL - split - rank)], axis=0),
        jnp.concatenate([z(rank), f("rwkv_a2_b"), z(TAIL - split - 2 * rank)], axis=0)], axis=1)
    return dict(wq_b_t=wq, wkv_b=wkv, w2cat=w2cat, a2cat=a2cat, w_br_mla=full["w_br_mla"],
                w_br_rwkv=full["w_br_rwkv"], w_out=full["w_out"])


def _prepare_vectors(vec, dims):
    rank, RW, TAIL = dims["rank"], dims["hr"] * dims["hn"], dims["TAIL"]
    mu = vec["rwkv_mu"]
    mu_p = jnp.concatenate([mu[:3 * RW], jnp.zeros((ROPE,), F32), mu[3 * RW:],
                            jnp.zeros((TAIL - ROPE - 4 * rank,), F32)])
    row = lambda t: t.reshape(1, -1)
    return dict(
        mu=row(mu_p), g_pre=row(vec["g_pre"]), g_post=row(vec["g_post"]), mla_q_norm=row(vec["mla_q_norm"]),
        mla_kv_norm=row(vec["mla_kv_norm"]), w0_f=row(vec["rwkv_w0_f"]), w0_b=row(vec["rwkv_w0_b"]),
        a0_f=row(vec["rwkv_a0_f"]), a0_b=row(vec["rwkv_a0_b"]), k_k=row(vec["rwkv_k_k"]), k_a=row(vec["rwkv_k_a"]),
        r_k=row(vec["rwkv_r_k"]), gn_g=row(vec["rwkv_gn_g"]), gn_b=row(vec["rwkv_gn_b"]))


def _restore_grads(g, dims):
    return {"w_in": _restore_w_in(g["w_in"], dims), **_restore_rest(g, dims), **_restore_vectors(g, dims)}


def _restore_w_in(gw, dims):
    parts = [gw[perm_off:perm_off + width] for _, width, perm_off in sorted(dims["segs"])]
    return jnp.concatenate(parts, axis=0).reshape(N_DEV, dims["d_in"] // N_DEV, gw.shape[1])


def _restore_rest(g, dims):
    hm, hr, hn, rank = dims["hm"], dims["hr"], dims["hn"], dims["rank"]
    QR, KVR, RW = dims["QR"], dims["KVR"], hr * hn
    wq = g["wq_b"].reshape(hm, QHEAD, QR)[:, :NOPE + ROPE].reshape(N_DEV, -1, QR)
    wkv = g["wkv_b"].reshape(KVR, 2, hm, NOPE).transpose(0, 2, 1, 3).reshape(KVR, 2 * hm * NOPE)
    lo = lambda t, first, half: t[first:first + rank, half * RW:(half + 1) * RW].astype(BF16)
    cols = lambda t: t.reshape(t.shape[0], N_DEV, -1).transpose(1, 0, 2)
    return dict(
        mla_wq_b=wq, mla_wkv_b=cols(wkv), rwkv_w2_f=cols(lo(g["w2cat"], ROPE, 0)),
        rwkv_w2_b=cols(lo(g["w2cat"], ROPE + rank, 1)), rwkv_a2_f=cols(lo(g["a2cat"], 0, 0)),
        rwkv_a2_b=cols(lo(g["a2cat"], rank, 1)), w_br_mla=cols(g["w_br_mla"]), w_br_rwkv=cols(g["w_br_rwkv"]),
        w_out=g["w_out"].reshape(N_DEV, -1, g["w_out"].shape[1]))


def _restore_vectors(g, dims):
    rank, RW = dims["rank"], dims["hr"] * dims["hn"]
    mu = g["mu"][0]
    out = dict(
        rwkv_mu=jnp.concatenate([mu[:3 * RW], mu[3 * RW + ROPE:3 * RW + ROPE + 4 * rank]]),
        g_pre=g["g_pre"][0], g_post=g["g_post"][0], mla_q_norm=g["mla_q_norm"][0], mla_kv_norm=g["mla_kv_norm"][0],
        rwkv_w0_f=g["w0_f"][0], rwkv_w0_b=g["w0_b"][0], rwkv_a0_f=g["a0_f"][0], rwkv_a0_b=g["a0_b"][0],
        rwkv_k_k=g["k_k"][0], rwkv_k_a=g["k_a"][0], rwkv_r_k=g["r_k"][0], rwkv_gn_g=g["gn_g"][0],
        rwkv_gn_b=g["gn_b"][0])
    return out


def _dims(inp):
    D = inp["x"].shape[-1]
    QR, KVR = inp["mla_q_norm"].shape[0], inp["mla_kv_norm"].shape[0]
    hm = inp["mla_wq_b"].shape[1] * N_DEV // (NOPE + ROPE)
    hr, hn = inp["rwkv_r_k"].shape
    rank = inp["rwkv_w2_f"].shape[0]
    MW, RW = hm * VDIM, hr * hn
    TAIL = -(-(ROPE + 4 * rank) // LANES) * LANES
    orig, o = {}, 0
    for nme, w in (("q_a", QR), ("kv_a", KVR), ("k_rope", ROPE), ("rkv", 3 * RW), ("lora", 4 * rank), ("z_m", MW),
                   ("z_r", RW), ("gate_m", D), ("gate_r", D)):
        orig[nme] = (o, w)
        o += w
    assert o == inp["w_in"].shape[1] * N_DEV
    lay, d_in_perm = _layout(D, MW, RW, TAIL, QR, KVR)
    perm_off = dict(q_a=lay["q_a"][0], kv_a=lay["kv_a"][0], k_rope=lay["tail"][0], rkv=lay["r"][0],
                    lora=lay["tail"][0] + ROPE, z_m=lay["z_m"][0], z_r=lay["z_r"][0], gate_m=lay["gate_m"][0],
                    gate_r=lay["gate_r"][0])
    segs = [(orig[nme][0], orig[nme][1], perm_off[nme]) for nme in orig]
    return dict(D=D, QR=QR, KVR=KVR, hm=hm, hr=hr, hn=hn, rank=rank, TAIL=TAIL, segs=segs, d_in=o,
                d_in_perm=d_in_perm)


def kernel(x, g_pre, w_in, mla_q_norm, mla_wq_b, mla_kv_norm, mla_wkv_b, rwkv_mu, rwkv_w0_f, rwkv_w2_f, rwkv_w0_b, rwkv_w2_b, rwkv_a0_f, rwkv_a2_f, rwkv_a0_b, rwkv_a2_b, rwkv_k_k, rwkv_k_a, rwkv_r_k, rwkv_gn_g, rwkv_gn_b, w_br_mla, w_br_rwkv, w_out, g_post, loss_target, m_g_pre, m_w_in, m_mla_q_norm, m_mla_wq_b, m_mla_kv_norm, m_mla_wkv_b, m_rwkv_mu, m_rwkv_w0_f, m_rwkv_w2_f, m_rwkv_w0_b, m_rwkv_w2_b, m_rwkv_a0_f, m_rwkv_a2_f, m_rwkv_a0_b, m_rwkv_a2_b, m_rwkv_k_k, m_rwkv_k_a, m_rwkv_r_k, m_rwkv_gn_g, m_rwkv_gn_b, m_w_br_mla, m_w_br_rwkv, m_w_out, m_g_post, v_g_pre, v_w_in, v_mla_q_norm, v_mla_wq_b, v_mla_kv_norm, v_mla_wkv_b, v_rwkv_mu, v_rwkv_w0_f, v_rwkv_w2_f, v_rwkv_w0_b, v_rwkv_w2_b, v_rwkv_a0_f, v_rwkv_a2_f, v_rwkv_a0_b, v_rwkv_a2_b, v_rwkv_k_k, v_rwkv_k_a, v_rwkv_r_k, v_rwkv_gn_g, v_rwkv_gn_b, v_w_br_mla, v_w_br_rwkv, v_w_out, v_g_post):
    inp = dict(locals())
    dims = _dims(inp)
    stored = lambda t, n: t.T if n in _TRANSPOSED else t
    assert _MATS[0] == "w_in"
    shards = [stored(inp[n], n).astype(BF16) for n in _MATS]
    core = lax.axis_index("c").astype(jnp.int32).reshape(1)
    (w_in_slabs,) = _run_exchange(_gather_plan(shards[:1]), name="gather_w_in")
    W = {"w_in_t": _prepare_w_in(w_in_slabs, dims), **_prepare_vectors({n: inp[n] for n in _VECS}, dims)}
    loss, grad_x, g, recv_rest = _local_grads(x[0], loss_target[0], W, dims, exchange=(shards[1:], core))

    new = {}
    *recv_rest, got = recv_rest
    g_w_in, g = g["w_in"], _restore_vectors(g, dims)
    vsizes = [inp[n].size for n in _VECS] + [1]
    vflat = lambda prefix, src, last: _pack([src[prefix + n].reshape(-1) for n in _VECS] + [last])
    one = jnp.zeros((1,), F32)
    recv_w_in, vrecv = _run_exchange(
        _join_plans(_chip_exchange_plan([_pair_add(core, g_w_in, got, name="pair_add_w_in")]),
                    _direct_gather_plan(vflat("", g, loss.reshape(1)))), name="scatter_w_in")
    for n, t in zip(_MATS, [recv_w_in] + recv_rest):
        out = _adamw(t, stored(inp[n], n), stored(inp["m_" + n], n), stored(inp["v_" + n], n), name="adamw_" + n)
        new[n] = [stored(o, n) for o in out]

    vout = _adamw(vrecv, vflat("", inp, one), vflat("m_", inp, one), vflat("v_", inp, one), name="adamw_vectors")
    vparts = [_unpack(t, vsizes) for t in vout]
    for i, n in enumerate(_VECS):
        new[n] = [vp[i].reshape(inp[n].shape) for vp in vparts]
    loss = vparts[0][-1].reshape(())

    outs = [loss, grad_x[None]]
    for k in range(4):
        outs += [new[n][k] for n in _WEIGHTS]
    return tuple(outs)
```

```python
import functools
import math

import jax
import jax.numpy as jnp
from jax import lax
from jax.experimental import pallas as pl
from jax.experimental.pallas import tpu as pltpu

F32 = jnp.float32
BF16 = jnp.bfloat16

N_DEV = 8
LANES = 128
BF16_ROWS = 16
NOPE, ROPE, VDIM = 128, 64, 128
QHEAD = 256
ROPE_THETA = 10000.0
NORM_EPS = 1e-6
GN_EPS = 64e-5
CHUNK = 64
SUB = 16
VMEM_LIMIT = 56 * 1024 * 1024

ADAM_LR, ADAM_B1, ADAM_B2, ADAM_EPS, ADAM_WD, ADAM_STEP = 0.001, 0.9, 0.999, 1e-08, 0.01, 10


def _cparams(sem):
    return pltpu.CompilerParams(dimension_semantics=sem, vmem_limit_bytes=VMEM_LIMIT)


def _pick(n, cap):
    if n <= cap:
        return n
    for t in range(cap - cap % LANES, 0, -LANES):
        if n % t == 0:
            return t
    raise ValueError(f"no tile for {n} under {cap}")


def _mm(a, b, *, ta=False, tb=False, out_dtype=F32, name, tm_cap=1024, tn_cap=512, tk_cap=2048, ride=None):
    K, M = a.shape if ta else a.shape[::-1]
    N = b.shape[0] if tb else b.shape[1]
    assert (b.shape[1] if tb else b.shape[0]) == K, (a.shape, b.shape, ta, tb)
    tm, tn, tk = _pick(M, tm_cap), _pick(N, tn_cap), _pick(K, tk_cap)
    nj, nk = N // tn, K // tk
    steps = (M // tm) * nj * nk
    dn = (((0 if ta else 1,), (1 if tb else 0,)), ((), ()))
    srcs, extra_shapes, sem_shapes, phases = ride if ride else ((), (), (), None)
    n_src, n_extra = len(srcs), len(extra_shapes)

    def body(*refs):
        a_ref, b_ref, o_ref = refs[0], refs[1], refs[2 + n_src]
        acc_ref = refs[3 + n_src + n_extra]
        k = pl.program_id(2)
        if ride:
            step = (pl.program_id(0) * nj + pl.program_id(1)) * nk + k
            first, middle, last = phases(refs[2:2 + n_src], refs[3 + n_src:3 + n_src + n_extra],
                                         refs[4 + n_src + n_extra:])
            pl.when(step == 0)(first)
            pl.when(step == (steps * 15) // 16)(middle)
        p = lax.dot_general(a_ref[...], b_ref[...], dn, preferred_element_type=F32)

        @pl.when(k == 0)
        def _():
            acc_ref[...] = p

        @pl.when(k > 0)
        def _():
            acc_ref[...] += p

        @pl.when(k == nk - 1)
        def _():
            o_ref[...] = acc_ref[...].astype(out_dtype)

        if ride:
            pl.when(step == steps - 1)(last)

    a_spec = pl.BlockSpec((tk, tm), lambda i, j, k: (k, i)) if ta else pl.BlockSpec((tm, tk), lambda i, j, k: (i, k))
    b_spec = pl.BlockSpec((tn, tk), lambda i, j, k: (j, k)) if tb else pl.BlockSpec((tk, tn), lambda i, j, k: (k, j))
    hbm = pl.BlockSpec(memory_space=pl.ANY)
    out = pl.pallas_call(
        body, name=name, grid=(M // tm, nj, nk),
        in_specs=[a_spec, b_spec] + [hbm] * n_src,
        out_specs=[pl.BlockSpec((tm, tn), lambda i, j, k: (i, j))] + [hbm] * n_extra,
        out_shape=[jax.ShapeDtypeStruct((M, N), out_dtype)] + list(extra_shapes),
        scratch_shapes=[pltpu.VMEM((tm, tn), F32)] + [pltpu.SemaphoreType.DMA(s) for s in sem_shapes],
        compiler_params=_cparams(("arbitrary",) * 3 if ride else ("parallel", "parallel", "arbitrary")),
    )(a, b, *srcs)
    return out if ride else out[0]


def _view(arr, off, width):
    assert off % width == 0, (off, width)
    return (arr, off // width, width)


def _rowwise(fn, rows, params, out_rows, out_accs=(), *, tile, name):
    rows = [r if isinstance(r, tuple) else (r, 0, r.shape[1]) for r in rows]
    S = rows[0][0].shape[0]
    T = min(tile, S)
    assert S % T == 0
    n_rows, n_par, n_out = len(rows), len(params), len(out_rows)
    into = [o[2] if len(o) == 3 else None for o in out_rows]
    carried = [t[0] for t in into if t is not None and t[0] is not None]

    def body(*refs):
        ins = [r[...] for r in refs[:n_rows + n_par]]
        outs = fn(*ins)
        out_refs = refs[n_rows + n_par + len(carried):]
        for o_ref, val in zip(out_refs[:n_out], outs[:n_out]):
            o_ref[...] = val.astype(o_ref.dtype)
        i = pl.program_id(0)
        for o_ref, val in zip(out_refs[n_out:], outs[n_out:]):
            @pl.when(i == 0)
            def _(o_ref=o_ref, val=val):
                o_ref[...] = val

            @pl.when(i > 0)
            def _(o_ref=o_ref, val=val):
                o_ref[...] += val

    in_specs = [pl.BlockSpec((T, w), functools.partial(lambda i, cb: (i, cb), cb=cb)) for _, cb, w in rows]
    in_specs += [pl.BlockSpec(p.shape, lambda i: (0, 0)) for p in params]
    in_specs += [pl.BlockSpec(memory_space=pl.ANY)] * len(carried)
    out_specs, out_shape, aliases = [], [], {}
    for k, (o, t) in enumerate(zip(out_rows, into)):
        w, dt = o[0], o[1]
        if t is None:
            out_specs.append(pl.BlockSpec((T, w), lambda i: (i, 0)))
            out_shape.append(jax.ShapeDtypeStruct((S, w), dt))
            continue
        buf, total, first = t
        assert first % w == 0
        out_specs.append(pl.BlockSpec((T, w), functools.partial(lambda i, cb: (i, cb), cb=first // w)))
        out_shape.append(jax.ShapeDtypeStruct((S, total), dt))
        if buf is not None:
            aliases[n_rows + n_par + len(aliases)] = k
    out_specs += [pl.BlockSpec(s, lambda i: (0, 0)) for s in out_accs]
    out_shape += [jax.ShapeDtypeStruct(s, F32) for s in out_accs]
    return pl.pallas_call(
        body, name=name, grid=(S // T,), in_specs=in_specs, out_specs=out_specs, out_shape=out_shape,
        input_output_aliases=aliases, compiler_params=_cparams(("arbitrary",)),
    )(*[r[0] for r in rows], *params, *carried)


def _mm_sel(x, sel):
    hi = x.astype(BF16)
    lo = (x - hi.astype(F32)).astype(BF16)
    d = lambda u: jnp.dot(u, sel, preferred_element_type=F32)
    return d(hi) + d(lo)


@jax.custom_vjp
def _sel(x, sel, sel_t):
    return _mm_sel(x, sel)


def _sel_fwd(x, sel, sel_t):
    return _mm_sel(x, sel), (sel, sel_t)


def _sel_bwd(res, ct):
    sel, sel_t = res
    return _mm_sel(ct, sel_t), jnp.zeros_like(sel), jnp.zeros_like(sel_t)


_sel.defvjp(_sel_fwd, _sel_bwd)


def _rms(x, g):
    return x * lax.rsqrt(jnp.mean(x * x, axis=-1, keepdims=True) + NORM_EPS) * g


def _sigmoid(x):
    return 1.0 / (1.0 + jnp.exp(-x))


def _silu(x):
    return x * _sigmoid(x)


def _softplus(x):
    return jnp.maximum(x, 0.0) + jnp.log(1.0 + jnp.exp(-jnp.abs(x)))


def _f_mla_norm(q_a, kv_a, qg, kvg):
    return _rms(q_a, qg), _rms(kv_a, kvg)


def _f_rope(hm, qraw, kr_in, cosx, sinx, rot, rot_t):
    def rope(t):
        return t * cosx + _sel(t, rot, rot_t) * sinx
    parts = []
    for h in range(hm):
        parts.append(qraw[:, h * QHEAD:h * QHEAD + NOPE])
        parts.append(rope(qraw[:, h * QHEAD + NOPE:(h + 1) * QHEAD]))
    return jnp.concatenate(parts, axis=1), rope(kr_in)


def _f_rwkv_pre(rw, k, tail, w0f, w0b, a0f, a0b, k_k, k_a, w2cat, a2cat, seg, seg_t):
    split = w2cat.shape[0]
    zw = jnp.dot(jnp.tanh(tail[:, :split]).astype(BF16), w2cat, preferred_element_type=F32)
    za = jnp.dot(tail[:, split:].astype(BF16), a2cat, preferred_element_type=F32)
    return _f_rwkv_core(rw, k, zw, za, w0f, w0b, a0f, a0b, k_k, k_a, seg, seg_t)


def _f_rwkv_core(rw, k, zw, za, w0f, w0b, a0f, a0b, k_k, k_a, seg, seg_t):
    lw_f = -jnp.exp(-_softplus(-(w0f + zw[:, :rw])) - 0.5)
    lw_b = -jnp.exp(-_softplus(-(w0b + zw[:, rw:])) - 0.5)
    a_f = _sigmoid(a0f + za[:, :rw])
    a_b = _sigmoid(a0b + za[:, rw:])
    kk = k * k_k
    nrm = jnp.sqrt(_sel(_sel(kk * kk, seg, seg_t), seg_t, seg))
    kk = kk / jnp.maximum(nrm, 1e-12)
    k_f = k * (1.0 + (a_f - 1.0) * k_a)
    k_b = k * (1.0 + (a_b - 1.0) * k_a)
    return lw_f, lw_b, k_f, k_b, -kk, kk * a_f, kk * a_b


def _f_post(hn, y_f, y_b, r, k_f, k_b, v, z_r, o_mla, z_m, gn_g, gn_b, r_k, seg, seg_t):
    segsum = lambda t: _sel(_sel(t, seg, seg_t), seg_t, seg)
    y = y_f + y_b
    mu = segsum(y) * (1.0 / hn)
    yc = y - mu
    var = segsum(yc * yc) * (1.0 / hn)
    yn = yc * lax.rsqrt(var + GN_EPS) * gn_g + gn_b
    bonus = segsum(r * (k_f + k_b) * r_k) * v
    return o_mla * _silu(z_m), (yn + bonus) * _silu(z_r)


def _f_merge(u_m, u_r, g_m, g_r):
    return _sigmoid(g_m) * u_m + _sigmoid(g_r) * u_r


_NN = ((2,), (1,))
_NT = ((2,), (2,))
_TN = ((1,), (1,))

_SCAN_PASSES = {"cum": 2, "gram": 3, "solve": 1, "apply": 1, "state": 1}


def _hdot_raw(passes, x, y, dims):
    dn = (dims, ((0,), (0,)))
    d = lambda p, q: lax.dot_general(p, q, dn, preferred_element_type=F32)
    xh = x.astype(BF16)
    yh = y.astype(BF16)
    if passes == 1:
        return d(xh, yh)
    yl = (y - yh.astype(F32)).astype(BF16)
    if passes == 2:
        return d(xh, yh) + d(xh, yl)
    xl = (x - xh.astype(F32)).astype(BF16)
    return d(xh, yh) + d(xh, yl) + d(xl, yh)


@functools.partial(jax.custom_vjp, nondiff_argnums=(2, 3))
def _hdot_p(x, y, dims, passes):
    return _hdot_raw(passes, x, y, dims)


def _hdot_fwd(x, y, dims, passes):
    return _hdot_raw(passes, x, y, dims), (x, y)


def _hdot_bwd(dims, passes, res, ct):
    x, y = res
    if dims == _NN:
        return _hdot_raw(passes, ct, y, _NT), _hdot_raw(passes, x, ct, _TN)
    if dims == _NT:
        return _hdot_raw(passes, ct, y, _NN), _hdot_raw(passes, ct, x, _TN)
    return _hdot_raw(passes, y, ct, _NT), _hdot_raw(passes, x, ct, _NN)


_hdot_p.defvjp(_hdot_fwd, _hdot_bwd)


def _hdot(x, y, dims, kind):
    return _hdot_p(x, y, dims, _SCAN_PASSES[kind])


def _tri_solve(n_mat, x, length):
    row = lax.broadcasted_iota(jnp.int32, (length, length), 0)
    col = lax.broadcasted_iota(jnp.int32, (length, length), 1)
    eye = (row == col).astype(F32)[None]
    diag_blk = ((row // SUB) == (col // SUB))[None]
    nd = jnp.where(diag_blk, n_mat, 0.0)
    no = n_mat - nd
    dinv = eye + nd
    p = nd
    for _ in range(int(math.log2(SUB)) - 1):
        p = _hdot(p, p, _NN, "solve")
        dinv = dinv + _hdot(dinv, p, _NN, "solve")
    q = _hdot(dinv, no, _NN, "solve")
    u = _hdot(dinv, x, _NN, "solve")
    levels = int(math.log2(length // SUB))
    qs = [q]
    for _ in range(levels - 1):
        qs.append(_hdot(qs[-1], qs[-1], _NN, "solve"))
    for qk in reversed(qs):
        u = u + _hdot(qk, u, _NN, "solve")
    return u


def _rwkv_chunk(rev, s0, r, lw, k, v, a, b):
    pairs, length, width = r.shape
    hn = width // 2
    row = lax.broadcasted_iota(jnp.int32, (length, length), 0)
    col = lax.broadcasted_iota(jnp.int32, (length, length), 1)
    row2 = lax.broadcasted_iota(jnp.int32, (length, 2 * length), 0)
    col2 = lax.broadcasted_iota(jnp.int32, (length, 2 * length), 1)
    col2 = jnp.where(col2 >= length, col2 - length, col2)
    if rev is None:
        half = pairs // 2
        back = lax.broadcasted_iota(jnp.int32, (pairs, length, length), 0) >= half
        idx2 = lax.broadcasted_iota(jnp.int32, (2 * pairs, length, 2 * length), 0)
        back2 = ((idx2 >= half) & (idx2 < pairs)) | (idx2 >= pairs + half)
        ahead = jnp.where(back, (col - row)[None], (row - col)[None])
        ahead2 = jnp.where(back2, (col2 - row2)[None], (row2 - col2)[None])
        incl, strict2, incl2 = ahead >= 0, ahead2 > 0, ahead2 >= 0
    else:
        incl = ((row <= col) if rev else (row >= col))[None]
        strict2 = ((row2 < col2) if rev else (row2 > col2))[None]
        incl2 = ((row2 <= col2) if rev else (row2 >= col2))[None]
    lane = lax.broadcasted_iota(jnp.int32, (1, 1, width), 2)
    first = lane < hn
    head_mask = jnp.concatenate([jnp.broadcast_to(first.astype(F32), (pairs, 1, width)),
                                 jnp.broadcast_to(1.0 - first.astype(F32), (pairs, 1, width))], axis=0)
    twice = lambda t: jnp.concatenate([t, t], axis=0)
    pick = lambda t: jnp.where(first, t[:pairs], t[pairs:])

    t_incl = jnp.broadcast_to(incl.astype(F32), (pairs, length, length))
    cum = _hdot(t_incl, lw, _NN, "cum")
    g = jnp.exp(cum)
    g_inv = jnp.exp(-cum)
    at = a * jnp.exp(cum - lw)
    rt = r * g
    bt = b * g_inv
    kt = k * g_inv
    lhs = jnp.concatenate([twice(at) * head_mask, twice(rt) * head_mask], axis=1)
    rhs = jnp.concatenate([twice(bt), twice(kt)], axis=1)
    gram = _hdot(lhs, rhs, _NT, "gram")
    top = jnp.where(strict2, gram[:, :length], 0.0)
    bot = jnp.where(incl2, gram[:, length:], 0.0)
    v2 = twice(v)
    zeros = jnp.zeros_like(v2)
    x = _hdot(at, s0, _NT, "apply") + pick(_hdot(top, jnp.concatenate([zeros, v2], axis=1), _NN, "apply"))
    u = pick(_tri_solve(top[:, :, :length], twice(x), length))
    y = _hdot(rt, s0, _NT, "apply") + pick(_hdot(bot, jnp.concatenate([twice(u), v2], axis=1), _NN, "apply"))
    g_last = jnp.exp(jnp.sum(lw, axis=1, keepdims=True))
    ri = lax.broadcasted_iota(jnp.int32, (width, width), 0)
    ci = lax.broadcasted_iota(jnp.int32, (width, width), 1)
    same_head = ((ri < hn) == (ci < hn))[None]
    upd = _hdot(u, bt, _TN, "state") + _hdot(v, kt, _TN, "state")
    s1 = (s0 + jnp.where(same_head, upd, 0.0)) * g_last
    return y, s1


def _split_pairs(x):
    return jnp.stack([x[:, p * LANES:(p + 1) * LANES] for p in range(x.shape[1] // LANES)])


def _merge_pairs(x):
    return jnp.concatenate([x[p] for p in range(x.shape[0])], axis=1)


def _scan_specs(views, rw, nc, rev):
    cidx = (lambda c: nc - 1 - c) if rev else (lambda c: c)
    seqs = [pl.BlockSpec((CHUNK, rw), functools.partial(lambda c, cb: (cidx(c), cb), cb=cb)) for _, cb, _ in views]
    plain = pl.BlockSpec((CHUNK, rw), lambda c: (cidx(c), 0))
    st = pl.BlockSpec((1, rw // LANES, LANES, LANES), lambda c: (cidx(c), 0, 0, 0))
    return seqs, plain, st


def _as_views(arrs, rw):
    return [t if isinstance(t, tuple) else (t, 0, rw) for t in arrs]


def _rwkv_scan_fwd(ops_f, ops_b, rw, *, name):
    S = _as_views(ops_f, rw)[0][0].shape[0]
    nc, pairs = S // CHUNK, rw // LANES
    in_specs, out_specs, arrays = [], [], []
    for rev, ops in ((False, ops_f), (True, ops_b)):
        views = _as_views(ops, rw)
        seqs, plain, st = _scan_specs(views, rw, nc, rev)
        in_specs += seqs
        out_specs += [plain, st]
        arrays += [t[0] for t in views]

    def both(refs_f, refs_b):
        return [jnp.concatenate([_split_pairs(f[...]), _split_pairs(b[...])], axis=0) for f, b in zip(refs_f, refs_b)]

    def body(*refs):
        (y_f, st_f, y_b, st_b), s_ref = refs[12:16], refs[16]

        @pl.when(pl.program_id(0) == 0)
        def _():
            s_ref[...] = jnp.zeros_like(s_ref)

        s0 = s_ref[...]
        st_f[0] = s0[:pairs]
        st_b[0] = s0[pairs:]
        y, s1 = _rwkv_chunk(None, s0, *both(refs[:6], refs[6:12]))
        y_f[...] = _merge_pairs(y[:pairs])
        y_b[...] = _merge_pairs(y[pairs:])
        s_ref[...] = s1

    return pl.pallas_call(
        body, name=name, grid=(nc,), in_specs=in_specs, out_specs=out_specs,
        out_shape=[jax.ShapeDtypeStruct((S, rw), F32), jax.ShapeDtypeStruct((nc, pairs, LANES, LANES), F32)] * 2,
        scratch_shapes=[pltpu.VMEM((2 * pairs, LANES, LANES), F32)],
        compiler_params=_cparams(("arbitrary",)),
    )(*arrays)


def _rwkv_scan_bwd(ops_f, ops_b, states_f, states_b, dy, rw, *, name):
    S = dy.shape[0]
    nc, pairs = S // CHUNK, rw // LANES
    in_specs, arrays = [], []
    for rev, ops, states in ((False, ops_f, states_f), (True, ops_b, states_b)):
        views = _as_views(list(ops) + [dy], rw)
        seqs, plain, st = _scan_specs(views, rw, nc, not rev)
        in_specs += seqs + [st]
        arrays += [t[0] for t in views] + [states]
    out_specs = []
    for rev in (False, True):
        out_specs += [_scan_specs([], rw, nc, not rev)[1]] * 6

    def both(refs_f, refs_b):
        return [jnp.concatenate([_split_pairs(f[...]), _split_pairs(b[...])], axis=0) for f, b in zip(refs_f, refs_b)]

    def body(*refs):
        ds_ref = refs[28]

        @pl.when(pl.program_id(0) == 0)
        def _():
            ds_ref[...] = jnp.zeros_like(ds_ref)

        s0 = jnp.concatenate([refs[7][0], refs[15][0]], axis=0)
        _, vjp = jax.vjp(functools.partial(_rwkv_chunk, None), s0, *both(refs[:6], refs[8:14]))
        (dy,) = both(refs[6:7], refs[14:15])
        grads = vjp((dy, ds_ref[...]))
        ds_ref[...] = grads[0]
        for o_f, o_b, gval in zip(refs[16:22], refs[22:28], grads[1:]):
            o_f[...] = _merge_pairs(gval[:pairs])
            o_b[...] = _merge_pairs(gval[pairs:])

    return pl.pallas_call(
        body, name=name, grid=(nc,), in_specs=in_specs, out_specs=out_specs,
        out_shape=[jax.ShapeDtypeStruct((S, rw), F32)] * 12,
        scratch_shapes=[pltpu.VMEM((2 * pairs, LANES, LANES), F32)],
        compiler_params=_cparams(("arbitrary",)),
    )(*arrays)


def _shift_lerp(x_view, mu, d=None, into=None, *, name):
    arr, off, width = x_view
    S = arr.shape[0]
    cb = _pick(width, 512)
    assert off % cb == 0

    def cshift(t):
        rows = lax.broadcasted_iota(jnp.int32, t.shape, 0)
        prev = jnp.where(rows == 0, 0.0, pltpu.roll(t, 1, 0))
        nxt = jnp.where(rows == S - 1, 0.0, pltpu.roll(t, S - 1, 0))
        return 0.5 * (prev + nxt)

    def fwd_body(x_ref, mu_ref, o_ref):
        x = x_ref[...]
        o_ref[...] = x + mu_ref[...] * (cshift(x) - x)

    def bwd_body(x_ref, mu_ref, d_ref, _, dx_ref, dmu_ref):
        x, m, dd = x_ref[...], mu_ref[...], d_ref[...]
        gm = m * dd
        dx_ref[...] = (dd - gm + cshift(gm)).astype(dx_ref.dtype)
        dmu_ref[...] = jnp.sum(dd * (cshift(x) - x), axis=0, keepdims=True)

    x_spec = pl.BlockSpec((S, cb), lambda j: (0, off // cb + j))
    blk = pl.BlockSpec((S, cb), lambda j: (0, j))
    vec = pl.BlockSpec((1, cb), lambda j: (0, j))
    if d is None:
        return pl.pallas_call(
            fwd_body, name=name, grid=(width // cb,), in_specs=[x_spec, vec], out_specs=blk,
            out_shape=jax.ShapeDtypeStruct((S, width), F32), compiler_params=_cparams(("parallel",)),
        )(arr, mu)
    buf, first = into
    assert first % cb == 0
    return pl.pallas_call(
        bwd_body, name=name, grid=(width // cb,),
        in_specs=[x_spec, vec, blk, pl.BlockSpec(memory_space=pl.ANY)],
        out_specs=[pl.BlockSpec((S, cb), lambda j: (0, first // cb + j)), vec],
        out_shape=[jax.ShapeDtypeStruct(buf.shape, buf.dtype), jax.ShapeDtypeStruct((1, width), F32)],
        input_output_aliases={3: 0}, compiler_params=_cparams(("parallel",)),
    )(arr, mu, d, buf)


def _attention_fwd(qfull, kv, kr, hm, scale, *, tq, name):
    S = qfull.shape[0]
    nt = (((1,), (1,)), ((), ()))

    def body(q_ref, kn_ref, kr_ref, v_ref, o_ref, lse_ref, k_scr):
        _head_keys(kn_ref, kr_ref, k_scr)
        s = lax.dot_general(q_ref[...], k_scr[...], nt, preferred_element_type=F32)
        m = jnp.max(s, axis=-1, keepdims=True)
        p = jnp.exp((s - m) * scale)
        l = jnp.sum(p, axis=-1, keepdims=True)
        o_ref[...] = jnp.dot(p.astype(BF16), v_ref[...], preferred_element_type=F32) * (1.0 / l)
        lse_ref[...] = jnp.broadcast_to(m * scale + jnp.log(l), lse_ref.shape)

    oblk = pl.BlockSpec((tq, VDIM), lambda h, i: (i, h))
    return pl.pallas_call(
        body, name=name, grid=(hm, S // tq),
        in_specs=[pl.BlockSpec((tq, QHEAD), lambda h, i: (i, h)),
                  pl.BlockSpec((S, NOPE), lambda h, i: (0, h)),
                  pl.BlockSpec((S, LANES), lambda h, i: (0, 0)),
                  pl.BlockSpec((S, VDIM), lambda h, i: (0, hm + h))],
        out_specs=[oblk, oblk],
        out_shape=[jax.ShapeDtypeStruct((S, hm * VDIM), F32)] * 2,
        scratch_shapes=[pltpu.VMEM((S, QHEAD), BF16)],
        compiler_params=_cparams(("parallel", "arbitrary")),
    )(qfull, kv, kr, kv)


def _head_keys(kn_ref, kr_ref, k_scr):
    @pl.when(pl.program_id(1) == 0)
    def _():
        k_scr[:, :NOPE] = kn_ref[...]
        k_scr[:, NOPE:] = kr_ref[...]


def _attention_bwd(qfull, kv, kr, o, lse, d_o, hm, scale, *, tq, name):
    S = qfull.shape[0]
    tq = min(tq, S)
    nq = S // tq
    tn = (((0,), (0,)), ((), ()))
    nt = (((1,), (1,)), ((), ()))

    def body(q_ref, kn_ref, kr_ref, v_ref, o_ref, lse_ref, do_ref, dq_ref, dk_ref, dv_ref, k_scr):
        _head_keys(kn_ref, kr_ref, k_scr)
        s = lax.dot_general(q_ref[...], k_scr[...], nt, preferred_element_type=F32)
        p = jnp.exp(s * scale - lse_ref[:, 0:1])
        d_out = do_ref[...]
        delta = jnp.sum(d_out * o_ref[...], axis=-1, keepdims=True)
        d_out = d_out.astype(BF16)
        dp = lax.dot_general(d_out, v_ref[...], nt, preferred_element_type=F32)
        ds = (p * (dp - delta)).astype(BF16)
        dq_ref[...] = jnp.dot(ds, k_scr[...], preferred_element_type=F32) * scale
        dv = lax.dot_general(p.astype(BF16), d_out, tn, preferred_element_type=F32)
        dk = lax.dot_general(ds, q_ref[...], tn, preferred_element_type=F32)
        i = pl.program_id(1)
        for ref, val in ((dk_ref, dk), (dv_ref, dv)):
            @pl.when(i == 0)
            def _(ref=ref, val=val):
                ref[...] = val

            @pl.when(i > 0)
            def _(ref=ref, val=val):
                ref[...] += val

        @pl.when(i == nq - 1)
        def _():
            dk_ref[...] = dk_ref[...] * scale

    qblk = pl.BlockSpec((tq, QHEAD), lambda h, i: (i, h))
    oblk = pl.BlockSpec((tq, VDIM), lambda h, i: (i, h))
    return pl.pallas_call(
        body, name=name, grid=(hm, nq),
        in_specs=[qblk,
                  pl.BlockSpec((S, NOPE), lambda h, i: (0, h)),
                  pl.BlockSpec((S, LANES), lambda h, i: (0, 0)),
                  pl.BlockSpec((S, VDIM), lambda h, i: (0, hm + h)),
                  oblk, oblk, oblk],
        out_specs=[qblk, pl.BlockSpec((S, QHEAD), lambda h, i: (0, h)), pl.BlockSpec((S, VDIM), lambda h, i: (0, h))],
        out_shape=[jax.ShapeDtypeStruct((S, hm * QHEAD), F32), jax.ShapeDtypeStruct((S, hm * QHEAD), F32),
                   jax.ShapeDtypeStruct((S, hm * VDIM), F32)],
        scratch_shapes=[pltpu.VMEM((S, QHEAD), BF16)],
        compiler_params=_cparams(("parallel", "arbitrary")),
    )(qfull, kv, kr, kv, o, lse, d_o)


def _layout(D, MW, RW, TAIL, QR, KVR):
    names = ["gate_m", "gate_r", "z_m", "z_r", "q_a", "kv_a", "r", "k", "v", "tail"]
    widths = [D, D, MW, RW, QR, KVR, RW, RW, RW, TAIL]
    offs, o = {}, 0
    for nme, w in zip(names, widths):
        assert o % w == 0, (nme, o, w)
        offs[nme] = (o, w)
        o += w
    return offs, o


def _local_grads(x, target, W, dims, exchange=None):
    S, D = x.shape
    hm, hr, hn, rank = dims["hm"], dims["hr"], dims["hn"], dims["rank"]
    MW, RW = hm * VDIM, hr * hn
    TAIL = dims["TAIL"]
    QR, KVR = W["mla_q_norm"].shape[1], W["mla_kv_norm"].shape[1]
    lay, d_in = _layout(D, MW, RW, TAIL, QR, KVR)
    T = 256
    scale = (NOPE + ROPE) ** -0.5
    col = lambda arr, nme: _view(arr, *lay[nme])

    pos = jnp.arange(S, dtype=F32)
    inv_freq = jnp.power(ROPE_THETA, -jnp.arange(0, ROPE, 2, dtype=F32) / ROPE)
    ang = pos[:, None] * inv_freq[None, :]
    zpad = jnp.zeros((S, LANES - ROPE), F32)
    cosx = jnp.concatenate([jnp.cos(ang), jnp.cos(ang), zpad], axis=1)
    sinx = jnp.concatenate([jnp.sin(ang), jnp.sin(ang), zpad], axis=1)
    ri, ci = jnp.arange(LANES)[:, None], jnp.arange(LANES)[None, :]
    half = ROPE // 2
    rot = (jnp.where((ri == ci - half) & (ci >= half) & (ci < ROPE), 1.0, 0.0)
           - jnp.where((ri == ci + half) & (ci < half), 1.0, 0.0)).astype(BF16)
    rot_t = rot.T
    seg = (jnp.arange(RW)[:, None] // hn == jnp.arange(LANES)[None, :]).astype(BF16)
    seg_t = seg.T

    (h,) = _rowwise(lambda xb, g: (_rms(xb, g),), [x], [W["g_pre"]], [(D, BF16)], tile=2 * T, name="pre_norm")
    if exchange is None:
        proj = _mm(h, W["w_in_t"], tb=True, name="in_proj")
    else:
        proj, *slabs = _mm(h, W["w_in_t"], tb=True, ride=_gather_plan(exchange[0]), name="in_proj")
        W = {**W, **_prepare_rest(dict(zip(_MATS[1:], slabs)), dims)}

    qn, kvn = _rowwise(_f_mla_norm, [col(proj, "q_a"), col(proj, "kv_a")], [W["mla_q_norm"], W["mla_kv_norm"]],
                       [(QR, BF16), (KVR, BF16)], tile=2 * T, name="mla_norm")
    qraw = _mm(qn, W["wq_b_t"], tb=True, name="q_up")
    kv = _mm(kvn, W["wkv_b"], out_dtype=BF16, name="kv_up")
    kr_view = _view(proj, lay["tail"][0], LANES)
    qfull, kr = _rowwise(functools.partial(_f_rope, hm), [qraw, kr_view, cosx, sinx], [rot, rot_t],
                         [(hm * QHEAD, BF16), (LANES, BF16)], tile=2 * T, name="rope")
    o_mla, lse = _attention_fwd(qfull, kv, kr, hm, scale, tq=T, name="attn_fwd")

    shift_view = (proj, lay["r"][0], 3 * RW + TAIL)
    rl = _shift_lerp(shift_view, W["mu"], name="shift_fwd")
    rl_r, rl_k, rl_v = _view(rl, 0, RW), _view(rl, RW, RW), _view(rl, 2 * RW, RW)
    rl_tail = _view(rl, 3 * RW, TAIL)
    pre_params = [W["w0_f"], W["w0_b"], W["a0_f"], W["a0_b"], W["k_k"], W["k_a"], W["w2cat"], W["a2cat"], seg, seg_t]
    pre_fn = functools.partial(_f_rwkv_pre, RW)
    lw_f, lw_b, k_f, k_b, a_n, b_f, b_b = _rowwise(pre_fn, [rl_k, rl_tail], pre_params, [(RW, F32)] * 7, tile=T,
                                                    name="rwkv_pre")
    ops_f = (rl_r, lw_f, k_f, rl_v, a_n, b_f)
    ops_b = (rl_r, lw_b, k_b, rl_v, a_n, b_b)
    y_f, st_f, y_b, st_b = _rwkv_scan_fwd(ops_f, ops_b, RW, name="scan_fwd")

    post_fn = functools.partial(_f_post, hn)
    post_rows = [y_f, y_b, rl_r, k_f, k_b, rl_v, col(proj, "z_r"), o_mla, col(proj, "z_m")]
    post_params = [W["gn_g"], W["gn_b"], W["r_k"], seg, seg_t]
    ymg, yrg = _rowwise(post_fn, post_rows, post_params, [(MW, BF16), (RW, BF16)], tile=T, name="post")
    u_m = _mm(ymg, W["w_br_mla"], name="br_mla")
    u_r = _mm(yrg, W["w_br_rwkv"], name="br_rwkv")
    merge_rows = [u_m, u_r, col(proj, "gate_m"), col(proj, "gate_r")]
    (merged,) = _rowwise(lambda *t: (_f_merge(*t),), merge_rows, [], [(D, BF16)], tile=2 * T, name="merge")
    out = _mm(merged, W["w_out"], name="out_proj")

    def head(ob, xb, tb, g):
        yn, vjp = jax.vjp(_rms, ob, g)
        err = xb + yn - tb
        dy = err * (1.0 / D)
        d_ob, d_g = vjp(dy)
        loss = jnp.broadcast_to(0.5 * jnp.sum(err * err) * (1.0 / D), (1, LANES))
        return dy, d_ob, loss, d_g

    dy, d_out, loss, g_g_post = _rowwise(head, [out, x, target], [W["g_post"]], [(D, F32), (D, BF16)],
                                         [(1, LANES), (1, D)], tile=2 * T, name="head")
    d_merged = _mm(d_out, W["w_out"], tb=True, name="d_merged")
    g_w_out = _mm(merged, d_out, ta=True, out_dtype=BF16, name="g_w_out")

    def merge_bwd(u_m_b, u_r_b, g_m_b, g_r_b, dm):
        _, vjp = jax.vjp(_f_merge, u_m_b, u_r_b, g_m_b, g_r_b)
        du_m, du_r, dg_m, dg_r = vjp(dm)
        return du_m, du_r, jnp.concatenate([dg_m, dg_r], axis=1)

    d_u_m, d_u_r, d_proj = _rowwise(merge_bwd, merge_rows + [d_merged], [],
                                    [(D, BF16), (D, BF16), (2 * D, BF16, (None, d_in, lay["gate_m"][0]))], tile=T,
                                    name="merge_bwd")
    d_ymg = _mm(d_u_m, W["w_br_mla"], tb=True, name="d_ymg")
    d_yrg = _mm(d_u_r, W["w_br_rwkv"], tb=True, name="d_yrg")
    g_w_br_mla = _mm(ymg, d_u_m, ta=True, out_dtype=BF16, name="g_w_br_mla")
    g_w_br_rwkv = _mm(yrg, d_u_r, ta=True, out_dtype=BF16, name="g_w_br_rwkv")

    def post_bwd(*args):
        nr = len(post_rows)
        prim, dm, dr = args[:nr] + args[nr + 2:], args[nr], args[nr + 1]
        _, vjp = jax.vjp(post_fn, *prim)
        g = vjp((dm, dr))
        return g[0], g[2], g[3], g[5], g[7], jnp.concatenate([g[8], g[6]], axis=1), g[9], g[10], g[11]

    (d_y, d_r_bonus, d_k_bonus, d_v_bonus, d_o, d_proj, g_gn_g, g_gn_b, g_r_k) = _rowwise(
        post_bwd, post_rows + [d_ymg, d_yrg], post_params,
        [(RW, F32), (RW, F32), (RW, F32), (RW, F32), (MW, F32), (MW + RW, BF16, (d_proj, d_in, lay["z_m"][0]))],
        [(1, RW)] * 3, tile=T // 2, name="post_bwd")

    dscan = _rwkv_scan_bwd(ops_f, ops_b, st_f, st_b, d_y, RW, name="scan_bwd")
    dsc = {"f": dscan[:6], "b": dscan[6:]}

    d_q_att, d_k_att, d_v_att = _attention_bwd(qfull, kv, kr, o_mla, lse, d_o, hm, scale, tq=2 * T, name="attn_bwd")

    def rope_bwd(qraw_b, kr_in, cos_b, sin_b, dq_b, dk_b, dv_b, rot_b, rot_t_b):
        _, vjp = jax.vjp(lambda q_, k_: _f_rope(hm, q_, k_, cos_b, sin_b, rot_b, rot_t_b), qraw_b, kr_in)
        dkn = jnp.concatenate([dk_b[:, hh * QHEAD:hh * QHEAD + NOPE] for hh in range(hm)], axis=1)
        dkr = dk_b[:, NOPE:QHEAD]
        for hh in range(1, hm):
            dkr = dkr + dk_b[:, hh * QHEAD + NOPE:(hh + 1) * QHEAD]
        d_qraw, d_kr_in = vjp((dq_b, dkr))
        return d_qraw, jnp.concatenate([dkn, dv_b], axis=1), d_kr_in

    d_qraw, d_kv, d_kr_in = _rowwise(rope_bwd, [qraw, kr_view, cosx, sinx, d_q_att, d_k_att, d_v_att],
                                     [rot, rot_t], [(hm * QHEAD, BF16), (2 * MW, BF16), (LANES, F32)], tile=T,
                                     name="rope_bwd")
    d_qnorm = _mm(d_qraw, W["wq_b_t"], name="d_qn")
    d_kvnorm = _mm(d_kv, W["wkv_b"], tb=True, name="d_kvn")
    g_wq_b = _mm(d_qraw, qn, ta=True, out_dtype=BF16, name="g_wq_b")
    g_wkv_b = _mm(kvn, d_kv, ta=True, out_dtype=BF16, name="g_wkv_b")

    def mla_norm_bwd(q_a, kv_a, qg, kvg, dq, dk):
        _, vjp = jax.vjp(_f_mla_norm, q_a, kv_a, qg, kvg)
        d_q_a, d_kv_a, d_qg, d_kvg = vjp((dq, dk))
        return jnp.concatenate([d_q_a, d_kv_a], axis=1), d_qg, d_kvg

    d_proj, g_q_norm, g_kv_norm = _rowwise(
        lambda q_a, kv_a, dq, dk, qg, kvg: mla_norm_bwd(q_a, kv_a, qg, kvg, dq, dk),
        [col(proj, "q_a"), col(proj, "kv_a"), d_qnorm, d_kvnorm], [W["mla_q_norm"], W["mla_kv_norm"]],
        [(QR + KVR, BF16, (d_proj, d_in, lay["q_a"][0]))], [(1, QR), (1, KVR)], tile=2 * T, name="mla_norm_bwd")

    def pre_bwd(k_b_, tail_b, dlwf, dlwb, dkf, dkb, dkbon, daf, dab, dbf, dbb, drf, drb, drbon, dvf, dvb, dvbon,
                dkr, *params):
        w2, a2 = params[6], params[7]
        nt, tn = (((1,), (1,)), ((), ())), (((0,), (0,)), ((), ()))
        split = w2.shape[0]
        th = jnp.tanh(tail_b[:, :split])
        th_b, tail_h = th.astype(BF16), tail_b[:, split:].astype(BF16)
        zw = jnp.dot(th_b, w2, preferred_element_type=F32)
        za = jnp.dot(tail_h, a2, preferred_element_type=F32)
        _, vjp = jax.vjp(functools.partial(_f_rwkv_core, RW), k_b_, zw, za, *params[:6], params[8], params[9])
        g = vjp((dlwf, dlwb, dkf + dkbon, dkb + dkbon, daf + dab, dbf, dbb))
        d_zw, d_za = g[1].astype(BF16), g[2].astype(BF16)
        d_tail = (jnp.concatenate([lax.dot_general(d_zw, w2, nt, preferred_element_type=F32) * (1.0 - th * th),
                                   lax.dot_general(d_za, a2, nt, preferred_element_type=F32)], axis=1)
                  + jnp.concatenate([dkr, jnp.zeros((dkr.shape[0], TAIL - LANES), F32)], axis=1))
        g_w2 = lax.dot_general(th_b, d_zw, tn, preferred_element_type=F32)
        g_a2 = lax.dot_general(tail_h, d_za, tn, preferred_element_type=F32)
        d_rl = jnp.concatenate([drf + drb + drbon, g[0], dvf + dvb + dvbon, d_tail], axis=1)
        return (d_rl,) + tuple(g[3:9]) + (g_w2, g_a2)

    f_, b_ = dsc["f"], dsc["b"]
    pre_bwd_rows = [rl_k, rl_tail, f_[1], b_[1], f_[2], b_[2], d_k_bonus, f_[4], b_[4], f_[5], b_[5],
                    f_[0], b_[0], d_r_bonus, f_[3], b_[3], d_v_bonus, d_kr_in]
    (d_rl, g_w0_f, g_w0_b, g_a0_f, g_a0_b, g_k_k, g_k_a, g_w2cat, g_a2cat) = _rowwise(
        pre_bwd, pre_bwd_rows, pre_params, [(3 * RW + TAIL, F32)],
        [(1, RW)] * 6 + [W["w2cat"].shape, W["a2cat"].shape], tile=T // 2, name="rwkv_pre_bwd")
    d_proj, g_mu = _shift_lerp(shift_view, W["mu"], d_rl, (d_proj, lay["r"][0]), name="shift_bwd")
    small = dict(wq_b=g_wq_b, wkv_b=g_wkv_b, w2cat=g_w2cat, a2cat=g_a2cat, w_br_mla=g_w_br_mla,
                 w_br_rwkv=g_w_br_rwkv, w_out=g_w_out)
    if exchange is None:
        received = None
        g_w_in = _mm(d_proj, h, ta=True, out_dtype=BF16, tn_cap=1024, name="g_w_in")
        d_h = _mm(d_proj, W["w_in_t"], tn_cap=1024, name="d_h")
    else:
        slabs = _restore_rest(small, dims)
        slabs = [slabs[n] for n in _MATS[1:]]
        g_w_in, *got = _mm(d_proj, h, ta=True, out_dtype=BF16, tn_cap=1024, ride=_sibling_swap_plan(slabs),
                           name="g_w_in")
        sums = [_pair_add(exchange[1], s, t, name="pair_add_" + n) for n, s, t in zip(_MATS[1:], slabs, got)]
        g_w_in = _restore_w_in(g_w_in, dims)
        d_h, *received = _mm(d_proj, W["w_in_t"], tn_cap=1024, name="d_h",
                             ride=_join_plans(_chip_exchange_plan(sums), _sibling_swap_plan([g_w_in])))
        small = {}

    def pre_norm_bwd(xb, dyb, dhb, g):
        _, vjp = jax.vjp(_rms, xb, g)
        dx, dg = vjp(dhb)
        return dyb + dx, dg

    grad_x, g_g_pre = _rowwise(pre_norm_bwd, [x, dy, d_h], [W["g_pre"]], [(D, F32)], [(1, D)], tile=2 * T,
                               name="pre_norm_bwd")

    grads = dict(g_pre=g_g_pre, w_in=g_w_in, mla_q_norm=g_q_norm, mla_kv_norm=g_kv_norm, mu=g_mu, w0_f=g_w0_f,
                 w0_b=g_w0_b, a0_f=g_a0_f, a0_b=g_a0_b, k_k=g_k_k, k_a=g_k_a, r_k=g_r_k, gn_g=g_gn_g, gn_b=g_gn_b,
                 g_post=g_g_post, **small)
    return loss[0, 0], grad_x, grads, received


_MATS = ["w_in", "mla_wq_b", "mla_wkv_b", "rwkv_w2_f", "rwkv_w2_b", "rwkv_a2_f", "rwkv_a2_b", "w_br_mla",
         "w_br_rwkv", "w_out"]
_ROW_SHARDED = ("w_out",)
_TRANSPOSED = ("w_in", "mla_wq_b")
_VECS = ["g_pre", "mla_q_norm", "mla_kv_norm", "rwkv_mu", "rwkv_w0_f", "rwkv_w0_b", "rwkv_a0_f", "rwkv_a0_b",
         "rwkv_k_k", "rwkv_k_a", "rwkv_r_k", "rwkv_gn_g", "rwkv_gn_b", "g_post"]
_WEIGHTS = ["g_pre", "w_in", "mla_q_norm", "mla_wq_b", "mla_kv_norm", "mla_wkv_b", "rwkv_mu", "rwkv_w0_f",
            "rwkv_w2_f", "rwkv_w0_b", "rwkv_w2_b", "rwkv_a0_f", "rwkv_a2_f", "rwkv_a0_b", "rwkv_a2_b", "rwkv_k_k",
            "rwkv_k_a", "rwkv_r_k", "rwkv_gn_g", "rwkv_gn_b", "w_br_mla", "w_br_rwkv", "w_out", "g_post"]

def _direct_gather_plan(src):
    def phases(src_refs, out_refs, sem_refs):
        (src_ref,), (out_ref,), sems, local_sem = src_refs, out_refs, sem_refs[:2], sem_refs[2]
        x, y, c = lax.axis_index("x"), lax.axis_index("y"), lax.axis_index("c")
        me = 4 * x + 2 * y + c
        flip = lambda v, bit: (1 - v) if bit else v
        peers = [(flip(x, d & 4), flip(y, d & 2), flip(c, d & 1)) for d in range(1, N_DEV)]
        own = lambda: pltpu.make_async_copy(src_ref, out_ref.at[me], local_sem)
        send = lambda d: _remote(src_ref, out_ref.at[me], sems, d, peers[d])

        def first():
            own().start()
            for d in range(N_DEV - 1):
                send(d).start()

        def last():
            for d, (px, py, pc) in enumerate(peers):
                blk = out_ref.at[4 * px + 2 * py + pc]
                _remote(blk, blk, sems, d, (x, y, c)).wait_recv()
            for d in range(N_DEV - 1):
                send(d).wait_send()
            own().wait()

        return first, (lambda: None), last

    return [src], [jax.ShapeDtypeStruct((N_DEV,) + src.shape, src.dtype)], [(N_DEV - 1,), (N_DEV - 1,), ()], phases


def _remote(src, dst, sems, key, to):
    send_sems, recv_sems = sems
    return pltpu.make_async_remote_copy(src_ref=src, dst_ref=dst, send_sem=send_sems.at[key], recv_sem=recv_sems.at[key],
                                        device_id=to, device_id_type=pl.DeviceIdType.MESH)


def _run_exchange(plan, *, name):
    srcs, out_shapes, sem_shapes, phases = plan
    n, m = len(srcs), len(out_shapes)

    def body(*refs):
        for phase in phases(refs[:n], refs[n:n + m], refs[n + m:]):
            phase()

    return pl.pallas_call(
        body, name=name, out_shape=out_shapes,
        in_specs=[pl.BlockSpec(memory_space=pl.ANY)] * n, out_specs=[pl.BlockSpec(memory_space=pl.ANY)] * m,
        scratch_shapes=[pltpu.SemaphoreType.DMA(s) for s in sem_shapes],
    )(*srcs)


def _join_plans(p, q):
    (srcs_p, outs_p, sems_p, phases_p), (srcs_q, outs_q, sems_q, phases_q) = p, q

    def phases(src_refs, out_refs, sem_refs):
        a = phases_p(src_refs[:len(srcs_p)], out_refs[:len(outs_p)], sem_refs[:len(sems_p)])
        b = phases_q(src_refs[len(srcs_p):], out_refs[len(outs_p):], sem_refs[len(sems_p):])

        def both(fa, fb):
            def run():
                fa()
                fb()
            return run

        return tuple(both(fa, fb) for fa, fb in zip(a, b))

    return list(srcs_p) + list(srcs_q), list(outs_p) + list(outs_q), list(sems_p) + list(sems_q), phases


def _gather_plan(srcs):
    n = len(srcs)

    def phases(src_refs, out_refs, sem_refs):
        sems, local_sems = sem_refs[:2], sem_refs[2]
        x, y, c = lax.axis_index("x"), lax.axis_index("y"), lax.axis_index("c")
        idx = lambda px, py, pc: 4 * px + 2 * py + pc
        me, sibling = (x, y, c), (x, y, 1 - c)
        chips = [(1 - x, y), (x, 1 - y), (1 - x, 1 - y)]
        own = lambda a: pltpu.make_async_copy(src_refs[a], out_refs[a].at[idx(*me)], local_sems.at[a])
        to_sibling = lambda a: _remote(src_refs[a], out_refs[a].at[idx(*me)], sems, (0, a), sibling)
        to_chip = lambda a, j: _remote(src_refs[a], out_refs[a].at[idx(*me)], sems, (1 + j, a), (*chips[j], c))
        landed = lambda a, j: out_refs[a].at[idx(*chips[j], c)]
        passed_on = lambda a, j: _remote(landed(a, j), landed(a, j), sems, (4 + j, a), sibling)

        def first():
            for a in range(n):
                own(a).start()
                to_sibling(a).start()
                for j in range(3):
                    to_chip(a, j).start()

        def middle():
            for j in range(3):
                for a in range(n):
                    _remote(landed(a, j), landed(a, j), sems, (1 + j, a), me).wait_recv()
                    passed_on(a, j).start()

        def last():
            for a in range(n):
                blk = out_refs[a].at[idx(*sibling)]
                _remote(blk, blk, sems, (0, a), me).wait_recv()
                for j in range(3):
                    blk = out_refs[a].at[idx(*chips[j], 1 - c)]
                    _remote(blk, blk, sems, (4 + j, a), me).wait_recv()
            for a in range(n):
                to_sibling(a).wait_send()
                for j in range(3):
                    to_chip(a, j).wait_send()
                    passed_on(a, j).wait_send()
                own(a).wait()

        return first, middle, last

    return srcs, [jax.ShapeDtypeStruct((N_DEV,) + s.shape, s.dtype) for s in srcs], [(7, n), (7, n), (n,)], phases


def _sibling_swap_plan(srcs):
    n = len(srcs)

    def phases(src_refs, out_refs, sems):
        x, y, c = lax.axis_index("x"), lax.axis_index("y"), lax.axis_index("c")
        copies = lambda: [_remote(src_refs[a].at[2 * q + 1 - c], out_refs[a].at[q], sems, (q, a), (x, y, 1 - c))
                          for a in range(n) for q in range(4)]

        def first():
            for cp in copies():
                cp.start()

        def last():
            for cp in copies():
                cp.wait()

        return first, (lambda: None), last

    return srcs, [jax.ShapeDtypeStruct((4,) + s.shape[1:], s.dtype) for s in srcs], [(4, n), (4, n)], phases


def _chip_exchange_plan(srcs):
    n = len(srcs)

    def phases(src_refs, out_refs, sem_refs):
        sems, local_sems = sem_refs[:2], sem_refs[2]
        x, y, c = lax.axis_index("x"), lax.axis_index("y"), lax.axis_index("c")
        mine = 2 * x + y
        chips = [(1 - x, y), (x, 1 - y), (1 - x, 1 - y)]
        own = lambda a: pltpu.make_async_copy(src_refs[a].at[mine], out_refs[a].at[mine], local_sems.at[a])
        send = lambda a, j: _remote(src_refs[a].at[2 * chips[j][0] + chips[j][1]], out_refs[a].at[mine], sems, (j, a),
                                    (*chips[j], c))

        def first():
            for a in range(n):
                own(a).start()
                for j in range(3):
                    send(a, j).start()

        def last():
            for j in range(3):
                for a in range(n):
                    blk = out_refs[a].at[2 * chips[j][0] + chips[j][1]]
                    _remote(blk, blk, sems, (j, a), (x, y, c)).wait_recv()
            for a in range(n):
                for j in range(3):
                    send(a, j).wait_send()
                own(a).wait()

        return first, (lambda: None), last

    return srcs, [jax.ShapeDtypeStruct(s.shape, s.dtype) for s in srcs], [(3, n), (3, n), (n,)], phases


def _pair_add(core, g, got, *, name):
    q, r, c = got.shape
    tr, tc = _tile2d(r, c, cap=1024)

    def body(core_ref, a_ref, b_ref, o_ref):
        o_ref[...] = (a_ref[...].astype(F32) + b_ref[...].astype(F32)).astype(BF16)

    blk = pl.BlockSpec((1, tr, tc), lambda i, j, k, core_ref: (i, j, k))
    mine = pl.BlockSpec((1, tr, tc), lambda i, j, k, core_ref: (2 * i + core_ref[0], j, k))
    return pl.pallas_call(
        body, name=name, out_shape=jax.ShapeDtypeStruct(got.shape, BF16),
        grid_spec=pltpu.PrefetchScalarGridSpec(num_scalar_prefetch=1, grid=(q, r // tr, c // tc),
                                               in_specs=[mine, blk], out_specs=blk),
        compiler_params=_cparams(("parallel", "parallel", "parallel")))(core, g, got)


def _adamw(recv, w, m, v, *, name):
    r, c = w.shape
    n_terms = recv.shape[0]
    tr, tc = _tile2d(r, c, cap=512)

    def body(g_ref, w_ref, m_ref, v_ref, go_ref, d_ref, mo_ref, vo_ref):
        g = g_ref[0].astype(F32)
        for k in range(1, n_terms):
            g = g + g_ref[k].astype(F32)
        m_new = ADAM_B1 * m_ref[...] + (1.0 - ADAM_B1) * g
        v_new = ADAM_B2 * v_ref[...] + (1.0 - ADAM_B2) * (g * g)
        m_hat = m_new / (1.0 - ADAM_B1 ** ADAM_STEP)
        v_hat = v_new / (1.0 - ADAM_B2 ** ADAM_STEP)
        go_ref[...] = g
        d_ref[...] = -ADAM_LR * (m_hat / (jnp.sqrt(v_hat) + ADAM_EPS) + ADAM_WD * w_ref[...])
        mo_ref[...] = m_new
        vo_ref[...] = v_new

    blk = pl.BlockSpec((tr, tc), lambda i, j: (i, j))
    return pl.pallas_call(
        body, name=name, grid=(r // tr, c // tc),
        in_specs=[pl.BlockSpec((n_terms, tr, tc), lambda i, j: (0, i, j)), blk, blk, blk], out_specs=[blk] * 4,
        out_shape=[jax.ShapeDtypeStruct((r, c), F32)] * 4, compiler_params=_cparams(("parallel", "parallel")),
    )(recv, w, m, v)


def _tile2d(r, c, cap=256):
    if r <= cap:
        return r, c
    for t in range(cap - cap % BF16_ROWS, 0, -BF16_ROWS):
        if r % t == 0:
            return t, c
    return r, _pick(c, cap)


def _pack(pieces):
    total = sum(p.shape[0] for p in pieces)
    pad = (-total) % (8 * LANES)
    flat = jnp.concatenate(list(pieces) + [jnp.zeros((pad,), F32)])
    return flat.reshape(-1, LANES)


def _unpack(flat, sizes):
    flat = flat.reshape(-1)
    out, o = [], 0
    for n in sizes:
        out.append(flat[o:o + n])
        o += n
    return out


def _prepare_weights(full, vec, dims):
    rest = {n: t for n, t in full.items() if n != "w_in"}
    return {"w_in_t": _prepare_w_in(full["w_in"], dims), **_prepare_rest(rest, dims), **_prepare_vectors(vec, dims)}


def _prepare_w_in(slabs, dims):
    D = dims["D"]
    flat = slabs.reshape(-1, D)
    parts, pos = [], 0
    for orig_off, width, perm_off in sorted(dims["segs"], key=lambda t: t[2]):
        if perm_off > pos:
            parts.append(jnp.zeros((perm_off - pos, D), BF16))
        parts.append(flat[orig_off:orig_off + width])
        pos = perm_off + width
    if dims["d_in_perm"] > pos:
        parts.append(jnp.zeros((dims["d_in_perm"] - pos, D), BF16))
    return jnp.concatenate(parts, axis=0)


def _prepare_rest(full, dims):
    hm, hr, hn, rank = dims["hm"], dims["hr"], dims["hn"], dims["rank"]
    QR, KVR = dims["QR"], dims["KVR"]
    RW, TAIL = hr * hn, dims["TAIL"]
    full = {n: (t.reshape(-1, t.shape[2]) if n in _ROW_SHARDED + _TRANSPOSED
                else t.transpose(1, 0, 2).reshape(t.shape[1], -1)) for n, t in full.items()}
    wq = full["mla_wq_b"].reshape(hm, NOPE + ROPE, QR)
    wq = jnp.concatenate([wq, jnp.zeros((hm, QHEAD - NOPE - ROPE, QR), BF16)], axis=1).reshape(hm * QHEAD, QR)
    wkv = full["mla_wkv_b"].reshape(KVR, hm, 2, NOPE).transpose(0, 2, 1, 3).reshape(KVR, 2 * hm * NOPE)
    z = lambda rows: jnp.zeros((rows, RW), BF16)
    f = lambda nme: full[nme]
    split = ROPE + 2 * rank
    assert split % LANES == 0, split
    w2cat = jnp.concatenate([
        jnp.concatenate([z(ROPE), f("rwkv_w2_f"), z(rank)], axis=0),
        jnp.concatenate([z(ROPE + rank), f("rwkv_w2_b")], axis=0)], axis=1)
    a2cat = jnp.concatenate([
        jnp.concatenate([f("rwkv_a2_f"), z(TAIL - split - rank)], axis=0),
        jnp.concatenate([z(rank), f("rwkv_a2_b"), z(TAIL - split - 2 * rank)], axis=0)], axis=1)
    return dict(wq_b_t=wq, wkv_b=wkv, w2cat=w2cat, a2cat=a2cat, w_br_mla=full["w_br_mla"],
                w_br_rwkv=full["w_br_rwkv"], w_out=full["w_out"])


def _prepare_vectors(vec, dims):
    rank, RW, TAIL = dims["rank"], dims["hr"] * dims["hn"], dims["TAIL"]
    mu = vec["rwkv_mu"]
    mu_p = jnp.concatenate([mu[:3 * RW], jnp.zeros((ROPE,), F32), mu[3 * RW:],
                            jnp.zeros((TAIL - ROPE - 4 * rank,), F32)])
    row = lambda t: t.reshape(1, -1)
    return dict(
        mu=row(mu_p), g_pre=row(vec["g_pre"]), g_post=row(vec["g_post"]), mla_q_norm=row(vec["mla_q_norm"]),
        mla_kv_norm=row(vec["mla_kv_norm"]), w0_f=row(vec["rwkv_w0_f"]), w0_b=row(vec["rwkv_w0_b"]),
        a0_f=row(vec["rwkv_a0_f"]), a0_b=row(vec["rwkv_a0_b"]), k_k=row(vec["rwkv_k_k"]), k_a=row(vec["rwkv_k_a"]),
        r_k=row(vec["rwkv_r_k"]), gn_g=row(vec["rwkv_gn_g"]), gn_b=row(vec["rwkv_gn_b"]))


def _restore_grads(g, dims):
    return {"w_in": _restore_w_in(g["w_in"], dims), **_restore_rest(g, dims), **_restore_vectors(g, dims)}


def _restore_w_in(gw, dims):
    parts = [gw[perm_off:perm_off + width] for _, width, perm_off in sorted(dims["segs"])]
    return jnp.concatenate(parts, axis=0).reshape(N_DEV, dims["d_in"] // N_DEV, gw.shape[1])


def _restore_rest(g, dims):
    hm, hr, hn, rank = dims["hm"], dims["hr"], dims["hn"], dims["rank"]
    QR, KVR, RW = dims["QR"], dims["KVR"], hr * hn
    wq = g["wq_b"].reshape(hm, QHEAD, QR)[:, :NOPE + ROPE].reshape(N_DEV, -1, QR)
    wkv = g["wkv_b"].reshape(KVR, 2, hm, NOPE).transpose(0, 2, 1, 3).reshape(KVR, 2 * hm * NOPE)
    lo = lambda t, first, half: t[first:first + rank, half * RW:(half + 1) * RW].astype(BF16)
    cols = lambda t: t.reshape(t.shape[0], N_DEV, -1).transpose(1, 0, 2)
    return dict(
        mla_wq_b=wq, mla_wkv_b=cols(wkv), rwkv_w2_f=cols(lo(g["w2cat"], ROPE, 0)),
        rwkv_w2_b=cols(lo(g["w2cat"], ROPE + rank, 1)), rwkv_a2_f=cols(lo(g["a2cat"], 0, 0)),
        rwkv_a2_b=cols(lo(g["a2cat"], rank, 1)), w_br_mla=cols(g["w_br_mla"]), w_br_rwkv=cols(g["w_br_rwkv"]),
        w_out=g["w_out"].reshape(N_DEV, -1, g["w_out"].shape[1]))


def _restore_vectors(g, dims):
    rank, RW = dims["rank"], dims["hr"] * dims["hn"]
    mu = g["mu"][0]
    out = dict(
        rwkv_mu=jnp.concatenate([mu[:3 * RW], mu[3 * RW + ROPE:3 * RW + ROPE + 4 * rank]]),
        g_pre=g["g_pre"][0], g_post=g["g_post"][0], mla_q_norm=g["mla_q_norm"][0], mla_kv_norm=g["mla_kv_norm"][0],
        rwkv_w0_f=g["w0_f"][0], rwkv_w0_b=g["w0_b"][0], rwkv_a0_f=g["a0_f"][0], rwkv_a0_b=g["a0_b"][0],
        rwkv_k_k=g["k_k"][0], rwkv_k_a=g["k_a"][0], rwkv_r_k=g["r_k"][0], rwkv_gn_g=g["gn_g"][0],
        rwkv_gn_b=g["gn_b"][0])
    return out


def _dims(inp):
    D = inp["x"].shape[-1]
    QR, KVR = inp["mla_q_norm"].shape[0], inp["mla_kv_norm"].shape[0]
    hm = inp["mla_wq_b"].shape[1] * N_DEV // (NOPE + ROPE)
    hr, hn = inp["rwkv_r_k"].shape
    rank = inp["rwkv_w2_f"].shape[0]
    MW, RW = hm * VDIM, hr * hn
    TAIL = -(-(ROPE + 4 * rank) // LANES) * LANES
    orig, o = {}, 0
    for nme, w in (("q_a", QR), ("kv_a", KVR), ("k_rope", ROPE), ("rkv", 3 * RW), ("lora", 4 * rank), ("z_m", MW),
                   ("z_r", RW), ("gate_m", D), ("gate_r", D)):
        orig[nme] = (o, w)
        o += w
    assert o == inp["w_in"].shape[1] * N_DEV
    lay, d_in_perm = _layout(D, MW, RW, TAIL, QR, KVR)
    perm_off = dict(q_a=lay["q_a"][0], kv_a=lay["kv_a"][0], k_rope=lay["tail"][0], rkv=lay["r"][0],
                    lora=lay["tail"][0] + ROPE, z_m=lay["z_m"][0], z_r=lay["z_r"][0], gate_m=lay["gate_m"][0],
                    gate_r=lay["gate_r"][0])
    segs = [(orig[nme][0], orig[nme][1], perm_off[nme]) for nme in orig]
    return dict(D=D, QR=QR, KVR=KVR, hm=hm, hr=hr, hn=hn, rank=rank, TAIL=TAIL, segs=segs, d_in=o,
                d_in_perm=d_in_perm)


def kernel(x, g_pre, w_in, mla_q_norm, mla_wq_b, mla_kv_norm, mla_wkv_b, rwkv_mu, rwkv_w0_f, rwkv_w2_f, rwkv_w0_b, rwkv_w2_b, rwkv_a0_f, rwkv_a2_f, rwkv_a0_b, rwkv_a2_b, rwkv_k_k, rwkv_k_a, rwkv_r_k, rwkv_gn_g, rwkv_gn_b, w_br_mla, w_br_rwkv, w_out, g_post, loss_target, m_g_pre, m_w_in, m_mla_q_norm, m_mla_wq_b, m_mla_kv_norm, m_mla_wkv_b, m_rwkv_mu, m_rwkv_w0_f, m_rwkv_w2_f, m_rwkv_w0_b, m_rwkv_w2_b, m_rwkv_a0_f, m_rwkv_a2_f, m_rwkv_a0_b, m_rwkv_a2_b, m_rwkv_k_k, m_rwkv_k_a, m_rwkv_r_k, m_rwkv_gn_g, m_rwkv_gn_b, m_w_br_mla, m_w_br_rwkv, m_w_out, m_g_post, v_g_pre, v_w_in, v_mla_q_norm, v_mla_wq_b, v_mla_kv_norm, v_mla_wkv_b, v_rwkv_mu, v_rwkv_w0_f, v_rwkv_w2_f, v_rwkv_w0_b, v_rwkv_w2_b, v_rwkv_a0_f, v_rwkv_a2_f, v_rwkv_a0_b, v_rwkv_a2_b, v_rwkv_k_k, v_rwkv_k_a, v_rwkv_r_k, v_rwkv_gn_g, v_rwkv_gn_b, v_w_br_mla, v_w_br_rwkv, v_w_out, v_g_post):
    inp = dict(locals())
    dims = _dims(inp)
    stored = lambda t, n: t.T if n in _TRANSPOSED else t
    assert _MATS[0] == "w_in"
    shards = [stored(inp[n], n).astype(BF16) for n in _MATS]
    core = lax.axis_index("c").astype(jnp.int32).reshape(1)
    (w_in_slabs,) = _run_exchange(_gather_plan(shards[:1]), name="gather_w_in")
    W = {"w_in_t": _prepare_w_in(w_in_slabs, dims), **_prepare_vectors({n: inp[n] for n in _VECS}, dims)}
    loss, grad_x, g, recv_rest = _local_grads(x[0], loss_target[0], W, dims, exchange=(shards[1:], core))

    new = {}
    *recv_rest, got = recv_rest
    g_w_in, g = g["w_in"], _restore_vectors(g, dims)
    vsizes = [inp[n].size for n in _VECS] + [1]
    vflat = lambda prefix, src, last: _pack([src[prefix + n].reshape(-1) for n in _VECS] + [last])
    one = jnp.zeros((1,), F32)
    recv_w_in, vrecv = _run_exchange(
        _join_plans(_chip_exchange_plan([_pair_add(core, g_w_in, got, name="pair_add_w_in")]),
                    _direct_gather_plan(vflat("", g, loss.reshape(1)))), name="scatter_w_in")
    for n, t in zip(_MATS, [recv_w_in] + recv_rest):
        out = _adamw(t, stored(inp[n], n), stored(inp["m_" + n], n), stored(inp["v_" + n], n), name="adamw_" + n)
        new[n] = [stored(o, n) for o in out]

    vout = _adamw(vrecv, vflat("", inp, one), vflat("m_", inp, one), vflat("v_", inp, one), name="adamw_vectors")
    vparts = [_unpack(t, vsizes) for t in vout]
    for i, n in enumerate(_VECS):
        new[n] = [vp[i].reshape(inp[n].shape) for vp in vparts]
    loss = vparts[0][-1].reshape(())

    outs = [loss, grad_x[None]]
    for k in range(4):
        outs += [new[n][k] for n in _WEIGHTS]
    return tuple(outs)
```

```python
import functools
import math

import jax
import jax.numpy as jnp
from jax import lax
from jax.experimental import pallas as pl
from jax.experimental.pallas import tpu as pltpu

F32 = jnp.float32
BF16 = jnp.bfloat16

N_DEV = 8
LANES = 128
BF16_ROWS = 16
NOPE, ROPE, VDIM = 128, 64, 128
QHEAD = 256
ROPE_THETA = 10000.0
NORM_EPS = 1e-6
GN_EPS = 64e-5
CHUNK = 64
SUB = 16
VMEM_LIMIT = 56 * 1024 * 1024

ADAM_LR, ADAM_B1, ADAM_B2, ADAM_EPS, ADAM_WD, ADAM_STEP = 0.001, 0.9, 0.999, 1e-08, 0.01, 10


def _cparams(sem):
    return pltpu.CompilerParams(dimension_semantics=sem, vmem_limit_bytes=VMEM_LIMIT)


def _pick(n, cap):
    if n <= cap:
        return n
    for t in range(cap - cap % LANES, 0, -LANES):
        if n % t == 0:
            return t
    raise ValueError(f"no tile for {n} under {cap}")


def _mm(a, b, *, ta=False, tb=False, out_dtype=F32, name, tm_cap=1024, tn_cap=512, tk_cap=2048, ride=None):
    K, M = a.shape if ta else a.shape[::-1]
    N = b.shape[0] if tb else b.shape[1]
    assert (b.shape[1] if tb else b.shape[0]) == K, (a.shape, b.shape, ta, tb)
    tm, tn, tk = _pick(M, tm_cap), _pick(N, tn_cap), _pick(K, tk_cap)
    nj, nk = N // tn, K // tk
    steps = (M // tm) * nj * nk
    dn = (((0 if ta else 1,), (1 if tb else 0,)), ((), ()))
    srcs, extra_shapes, sem_shapes, phases = ride if ride else ((), (), (), None)
    n_src, n_extra = len(srcs), len(extra_shapes)

    def body(*refs):
        a_ref, b_ref, o_ref = refs[0], refs[1], refs[2 + n_src]
        acc_ref = refs[3 + n_src + n_extra]
        k = pl.program_id(2)
        if ride:
            step = (pl.program_id(0) * nj + pl.program_id(1)) * nk + k
            first, middle, last = phases(refs[2:2 + n_src], refs[3 + n_src:3 + n_src + n_extra],
                                         refs[4 + n_src + n_extra:])
            pl.when(step == 0)(first)
            pl.when(step == (steps * 15) // 16)(middle)
        p = lax.dot_general(a_ref[...], b_ref[...], dn, preferred_element_type=F32)

        @pl.when(k == 0)
        def _():
            acc_ref[...] = p

        @pl.when(k > 0)
        def _():
            acc_ref[...] += p

        @pl.when(k == nk - 1)
        def _():
            o_ref[...] = acc_ref[...].astype(out_dtype)

        if ride:
            pl.when(step == steps - 1)(last)

    a_spec = pl.BlockSpec((tk, tm), lambda i, j, k: (k, i)) if ta else pl.BlockSpec((tm, tk), lambda i, j, k: (i, k))
    b_spec = pl.BlockSpec((tn, tk), lambda i, j, k: (j, k)) if tb else pl.BlockSpec((tk, tn), lambda i, j, k: (k, j))
    hbm = pl.BlockSpec(memory_space=pl.ANY)
    out = pl.pallas_call(
        body, name=name, grid=(M // tm, nj, nk),
        in_specs=[a_spec, b_spec] + [hbm] * n_src,
        out_specs=[pl.BlockSpec((tm, tn), lambda i, j, k: (i, j))] + [hbm] * n_extra,
        out_shape=[jax.ShapeDtypeStruct((M, N), out_dtype)] + list(extra_shapes),
        scratch_shapes=[pltpu.VMEM((tm, tn), F32)] + [pltpu.SemaphoreType.DMA(s) for s in sem_shapes],
        compiler_params=_cparams(("arbitrary",) * 3 if ride else ("parallel", "parallel", "arbitrary")),
    )(a, b, *srcs)
    return out if ride else out[0]


def _view(arr, off, width):
    assert off % width == 0, (off, width)
    return (arr, off // width, width)


def _rowwise(fn, rows, params, out_rows, out_accs=(), *, tile, name):
    rows = [r if isinstance(r, tuple) else (r, 0, r.shape[1]) for r in rows]
    S = rows[0][0].shape[0]
    T = min(tile, S)
    assert S % T == 0
    n_rows, n_par, n_out = len(rows), len(params), len(out_rows)
    into = [o[2] if len(o) == 3 else None for o in out_rows]
    carried = [t[0] for t in into if t is not None and t[0] is not None]

    def body(*refs):
        ins = [r[...] for r in refs[:n_rows + n_par]]
        outs = fn(*ins)
        out_refs = refs[n_rows + n_par + len(carried):]
        for o_ref, val in zip(out_refs[:n_out], outs[:n_out]):
            o_ref[...] = val.astype(o_ref.dtype)
        i = pl.program_id(0)
        for o_ref, val in zip(out_refs[n_out:], outs[n_out:]):
            @pl.when(i == 0)
            def _(o_ref=o_ref, val=val):
                o_ref[...] = val

            @pl.when(i > 0)
            def _(o_ref=o_ref, val=val):
                o_ref[...] += val

    in_specs = [pl.BlockSpec((T, w), functools.partial(lambda i, cb: (i, cb), cb=cb)) for _, cb, w in rows]
    in_specs += [pl.BlockSpec(p.shape, lambda i: (0, 0)) for p in params]
    in_specs += [pl.BlockSpec(memory_space=pl.ANY)] * len(carried)
    out_specs, out_shape, aliases = [], [], {}
    for k, (o, t) in enumerate(zip(out_rows, into)):
        w, dt = o[0], o[1]
        if t is None:
            out_specs.append(pl.BlockSpec((T, w), lambda i: (i, 0)))
            out_shape.append(jax.ShapeDtypeStruct((S, w), dt))
            continue
        buf, total, first = t
        assert first % w == 0
        out_specs.append(pl.BlockSpec((T, w), functools.partial(lambda i, cb: (i, cb), cb=first // w)))
        out_shape.append(jax.ShapeDtypeStruct((S, total), dt))
        if buf is not None:
            aliases[n_rows + n_par + len(aliases)] = k
    out_specs += [pl.BlockSpec(s, lambda i: (0, 0)) for s in out_accs]
    out_shape += [jax.ShapeDtypeStruct(s, F32) for s in out_accs]
    return pl.pallas_call(
        body, name=name, grid=(S // T,), in_specs=in_specs, out_specs=out_specs, out_shape=out_shape,
        input_output_aliases=aliases, compiler_params=_cparams(("arbitrary",)),
    )(*[r[0] for r in rows], *params, *carried)


def _mm_sel(x, sel):
    hi = x.astype(BF16)
    lo = (x - hi.astype(F32)).astype(BF16)
    d = lambda u: jnp.dot(u, sel, preferred_element_type=F32)
    return d(hi) + d(lo)


@jax.custom_vjp
def _sel(x, sel, sel_t):
    return _mm_sel(x, sel)


def _sel_fwd(x, sel, sel_t):
    return _mm_sel(x, sel), (sel, sel_t)


def _sel_bwd(res, ct):
    sel, sel_t = res
    return _mm_sel(ct, sel_t), jnp.zeros_like(sel), jnp.zeros_like(sel_t)


_sel.defvjp(_sel_fwd, _sel_bwd)


def _rms(x, g):
    return x * lax.rsqrt(jnp.mean(x * x, axis=-1, keepdims=True) + NORM_EPS) * g


def _sigmoid(x):
    return 1.0 / (1.0 + jnp.exp(-x))


def _silu(x):
    return x * _sigmoid(x)


def _softplus(x):
    return jnp.maximum(x, 0.0) + jnp.log(1.0 + jnp.exp(-jnp.abs(x)))


def _f_mla_norm(q_a, kv_a, qg, kvg):
    return _rms(q_a, qg), _rms(kv_a, kvg)


def _f_rope(hm, qraw, kr_in, cosx, sinx, rot, rot_t):
    def rope(t):
        return t * cosx + _sel(t, rot, rot_t) * sinx
    parts = []
    for h in range(hm):
        parts.append(qraw[:, h * QHEAD:h * QHEAD + NOPE])
        parts.append(rope(qraw[:, h * QHEAD + NOPE:(h + 1) * QHEAD]))
    return jnp.concatenate(parts, axis=1), rope(kr_in)


def _f_rwkv_pre(rw, k, tail, w0f, w0b, a0f, a0b, k_k, k_a, w2cat, a2cat, seg, seg_t):
    split = w2cat.shape[0]
    zw = jnp.dot(jnp.tanh(tail[:, :split]).astype(BF16), w2cat, preferred_element_type=F32)
    za = jnp.dot(tail[:, split:].astype(BF16), a2cat, preferred_element_type=F32)
    return _f_rwkv_core(rw, k, zw, za, w0f, w0b, a0f, a0b, k_k, k_a, seg, seg_t)


def _f_rwkv_core(rw, k, zw, za, w0f, w0b, a0f, a0b, k_k, k_a, seg, seg_t):
    lw_f = -jnp.exp(-_softplus(-(w0f + zw[:, :rw])) - 0.5)
    lw_b = -jnp.exp(-_softplus(-(w0b + zw[:, rw:])) - 0.5)
    a_f = _sigmoid(a0f + za[:, :rw])
    a_b = _sigmoid(a0b + za[:, rw:])
    kk = k * k_k
    nrm = jnp.sqrt(_sel(_sel(kk * kk, seg, seg_t), seg_t, seg))
    kk = kk / jnp.maximum(nrm, 1e-12)
    k_f = k * (1.0 + (a_f - 1.0) * k_a)
    k_b = k * (1.0 + (a_b - 1.0) * k_a)
    return lw_f, lw_b, k_f, k_b, -kk, kk * a_f, kk * a_b


def _f_post(hn, y_f, y_b, r, k_f, k_b, v, z_r, o_mla, z_m, gn_g, gn_b, r_k, seg, seg_t):
    segsum = lambda t: _sel(_sel(t, seg, seg_t), seg_t, seg)
    y = y_f + y_b
    mu = segsum(y) * (1.0 / hn)
    yc = y - mu
    var = segsum(yc * yc) * (1.0 / hn)
    yn = yc * lax.rsqrt(var + GN_EPS) * gn_g + gn_b
    bonus = segsum(r * (k_f + k_b) * r_k) * v
    return o_mla * _silu(z_m), (yn + bonus) * _silu(z_r)


def _f_merge(u_m, u_r, g_m, g_r):
    return _sigmoid(g_m) * u_m + _sigmoid(g_r) * u_r


_NN = ((2,), (1,))
_NT = ((2,), (2,))
_TN = ((1,), (1,))

_SCAN_PASSES = {"cum": 2, "gram": 3, "solve": 1, "apply": 1, "state": 1}


def _hdot_raw(passes, x, y, dims):
    dn = (dims, ((0,), (0,)))
    d = lambda p, q: lax.dot_general(p, q, dn, preferred_element_type=F32)
    xh = x.astype(BF16)
    yh = y.astype(BF16)
    if passes == 1:
        return d(xh, yh)
    yl = (y - yh.astype(F32)).astype(BF16)
    if passes == 2:
        return d(xh, yh) + d(xh, yl)
    xl = (x - xh.astype(F32)).astype(BF16)
    return d(xh, yh) + d(xh, yl) + d(xl, yh)


@functools.partial(jax.custom_vjp, nondiff_argnums=(2, 3))
def _hdot_p(x, y, dims, passes):
    return _hdot_raw(passes, x, y, dims)


def _hdot_fwd(x, y, dims, passes):
    return _hdot_raw(passes, x, y, dims), (x, y)


def _hdot_bwd(dims, passes, res, ct):
    x, y = res
    if dims == _NN:
        return _hdot_raw(passes, ct, y, _NT), _hdot_raw(passes, x, ct, _TN)
    if dims == _NT:
        return _hdot_raw(passes, ct, y, _NN), _hdot_raw(passes, ct, x, _TN)
    return _hdot_raw(passes, y, ct, _NT), _hdot_raw(passes, x, ct, _NN)


_hdot_p.defvjp(_hdot_fwd, _hdot_bwd)


def _hdot(x, y, dims, kind):
    return _hdot_p(x, y, dims, _SCAN_PASSES[kind])


def _tri_solve(n_mat, x, length):
    row = lax.broadcasted_iota(jnp.int32, (length, length), 0)
    col = lax.broadcasted_iota(jnp.int32, (length, length), 1)
    eye = (row == col).astype(F32)[None]
    diag_blk = ((row // SUB) == (col // SUB))[None]
    nd = jnp.where(diag_blk, n_mat, 0.0)
    no = n_mat - nd
    dinv = eye + nd
    p = _hdot(nd, nd, _NN, "solve")
    for k in range(int(math.log2(SUB)) - 1):
        if k == int(math.log2(SUB)) - 2:
            dinv = dinv + _hdot(dinv, p, _NN, "solve")
        else:
            both = _hdot(jnp.concatenate([dinv, p], axis=1), p, _NN, "solve")
            dinv, p = dinv + both[:, :length], both[:, length:]
    both = _hdot(dinv, jnp.concatenate([x, no], axis=2), _NN, "solve")
    u, q = both[:, :, :x.shape[2]], both[:, :, x.shape[2]:]
    width = x.shape[2]
    for level in range(int(math.log2(length // SUB))):
        if level == int(math.log2(length // SUB)) - 1:
            u = u + _hdot(q, u, _NN, "solve")
        else:
            both = _hdot(q, jnp.concatenate([u, q], axis=2), _NN, "solve")
            u, q = u + both[:, :, :width], both[:, :, width:]
    return u


def _rwkv_chunk(rev, s0, r, lw, k, v, a, b):
    pairs, length, width = r.shape
    hn = width // 2
    row = lax.broadcasted_iota(jnp.int32, (length, length), 0)
    col = lax.broadcasted_iota(jnp.int32, (length, length), 1)
    row2 = lax.broadcasted_iota(jnp.int32, (length, 2 * length), 0)
    col2 = lax.broadcasted_iota(jnp.int32, (length, 2 * length), 1)
    col2 = jnp.where(col2 >= length, col2 - length, col2)
    if rev is None:
        half = pairs // 2
        back = lax.broadcasted_iota(jnp.int32, (pairs, length, length), 0) >= half
        idx2 = lax.broadcasted_iota(jnp.int32, (2 * pairs, length, 2 * length), 0)
        back2 = ((idx2 >= half) & (idx2 < pairs)) | (idx2 >= pairs + half)
        ahead = jnp.where(back, (col - row)[None], (row - col)[None])
        ahead2 = jnp.where(back2, (col2 - row2)[None], (row2 - col2)[None])
        incl, strict2, incl2 = ahead >= 0, ahead2 > 0, ahead2 >= 0
    else:
        incl = ((row <= col) if rev else (row >= col))[None]
        strict2 = ((row2 < col2) if rev else (row2 > col2))[None]
        incl2 = ((row2 <= col2) if rev else (row2 >= col2))[None]
    lane = lax.broadcasted_iota(jnp.int32, (1, 1, width), 2)
    first = lane < hn
    head_mask = jnp.concatenate([jnp.broadcast_to(first.astype(F32), (pairs, 1, width)),
                                 jnp.broadcast_to(1.0 - first.astype(F32), (pairs, 1, width))], axis=0)
    twice = lambda t: jnp.concatenate([t, t], axis=0)
    pick = lambda t: jnp.where(first, t[:pairs], t[pairs:])

    t_incl = jnp.broadcast_to(incl.astype(F32), (pairs, length, length))
    cum = _hdot(t_incl, lw, _NN, "cum")
    g = jnp.exp(cum)
    g_inv = jnp.exp(-cum)
    at = a * jnp.exp(cum - lw)
    rt = r * g
    bt = b * g_inv
    kt = k * g_inv
    lhs = jnp.concatenate([twice(at) * head_mask, twice(rt) * head_mask], axis=1)
    rhs = jnp.concatenate([twice(bt), twice(kt)], axis=1)
    gram = _hdot(lhs, rhs, _NT, "gram")
    top = jnp.where(strict2, gram[:, :length], 0.0)
    bot = jnp.where(incl2, gram[:, length:], 0.0)
    v2 = twice(v)
    zeros = jnp.zeros_like(v2)
    from_state = _hdot(jnp.concatenate([at, rt], axis=1), s0, _NT, "apply")
    x = from_state[:, :length] + pick(_hdot(top, jnp.concatenate([zeros, v2], axis=1), _NN, "apply"))
    u = pick(_tri_solve(top[:, :, :length], twice(x), length))
    y = from_state[:, length:] + pick(_hdot(bot, jnp.concatenate([twice(u), v2], axis=1), _NN, "apply"))
    g_last = jnp.exp(jnp.sum(lw, axis=1, keepdims=True))
    ri = lax.broadcasted_iota(jnp.int32, (width, width), 0)
    ci = lax.broadcasted_iota(jnp.int32, (width, width), 1)
    same_head = ((ri < hn) == (ci < hn))[None]
    upd = _hdot(jnp.concatenate([u, v], axis=1), jnp.concatenate([bt, kt], axis=1), _TN, "state")
    s1 = (s0 + jnp.where(same_head, upd, 0.0)) * g_last
    return y, s1


def _split_pairs(x):
    return jnp.stack([x[:, p * LANES:(p + 1) * LANES] for p in range(x.shape[1] // LANES)])


def _merge_pairs(x):
    return jnp.concatenate([x[p] for p in range(x.shape[0])], axis=1)


def _scan_specs(views, rw, nc, rev):
    cidx = (lambda c: nc - 1 - c) if rev else (lambda c: c)
    seqs = [pl.BlockSpec((CHUNK, rw), functools.partial(lambda c, cb: (cidx(c), cb), cb=cb)) for _, cb, _ in views]
    plain = pl.BlockSpec((CHUNK, rw), lambda c: (cidx(c), 0))
    st = pl.BlockSpec((1, rw // LANES, LANES, LANES), lambda c: (cidx(c), 0, 0, 0))
    return seqs, plain, st


def _as_views(arrs, rw):
    return [t if isinstance(t, tuple) else (t, 0, rw) for t in arrs]


def _rwkv_scan_fwd(ops_f, ops_b, rw, *, name):
    S = _as_views(ops_f, rw)[0][0].shape[0]
    nc, pairs = S // CHUNK, rw // LANES
    in_specs, out_specs, arrays = [], [], []
    for rev, ops in ((False, ops_f), (True, ops_b)):
        views = _as_views(ops, rw)
        seqs, plain, st = _scan_specs(views, rw, nc, rev)
        in_specs += seqs
        out_specs += [plain, st]
        arrays += [t[0] for t in views]

    def both(refs_f, refs_b):
        return [jnp.concatenate([_split_pairs(f[...]), _split_pairs(b[...])], axis=0) for f, b in zip(refs_f, refs_b)]

    def body(*refs):
        (y_f, st_f, y_b, st_b), s_ref = refs[12:16], refs[16]

        @pl.when(pl.program_id(0) == 0)
        def _():
            s_ref[...] = jnp.zeros_like(s_ref)

        s0 = s_ref[...]
        st_f[0] = s0[:pairs]
        st_b[0] = s0[pairs:]
        y, s1 = _rwkv_chunk(None, s0, *both(refs[:6], refs[6:12]))
        y_f[...] = _merge_pairs(y[:pairs])
        y_b[...] = _merge_pairs(y[pairs:])
        s_ref[...] = s1

    return pl.pallas_call(
        body, name=name, grid=(nc,), in_specs=in_specs, out_specs=out_specs,
        out_shape=[jax.ShapeDtypeStruct((S, rw), F32), jax.ShapeDtypeStruct((nc, pairs, LANES, LANES), F32)] * 2,
        scratch_shapes=[pltpu.VMEM((2 * pairs, LANES, LANES), F32)],
        compiler_params=_cparams(("arbitrary",)),
    )(*arrays)


def _rwkv_scan_bwd(ops_f, ops_b, states_f, states_b, dy, rw, *, name):
    S = dy.shape[0]
    nc, pairs = S // CHUNK, rw // LANES
    in_specs, arrays = [], []
    for rev, ops, states in ((False, ops_f, states_f), (True, ops_b, states_b)):
        views = _as_views(list(ops) + [dy], rw)
        seqs, plain, st = _scan_specs(views, rw, nc, not rev)
        in_specs += seqs + [st]
        arrays += [t[0] for t in views] + [states]
    out_specs = []
    for rev in (False, True):
        out_specs += [_scan_specs([], rw, nc, not rev)[1]] * 6

    def both(refs_f, refs_b):
        return [jnp.concatenate([_split_pairs(f[...]), _split_pairs(b[...])], axis=0) for f, b in zip(refs_f, refs_b)]

    def body(*refs):
        ds_ref = refs[28]

        @pl.when(pl.program_id(0) == 0)
        def _():
            ds_ref[...] = jnp.zeros_like(ds_ref)

        s0 = jnp.concatenate([refs[7][0], refs[15][0]], axis=0)
        _, vjp = jax.vjp(functools.partial(_rwkv_chunk, None), s0, *both(refs[:6], refs[8:14]))
        (dy,) = both(refs[6:7], refs[14:15])
        grads = vjp((dy, ds_ref[...]))
        ds_ref[...] = grads[0]
        for o_f, o_b, gval in zip(refs[16:22], refs[22:28], grads[1:]):
            o_f[...] = _merge_pairs(gval[:pairs])
            o_b[...] = _merge_pairs(gval[pairs:])

    return pl.pallas_call(
        body, name=name, grid=(nc,), in_specs=in_specs, out_specs=out_specs,
        out_shape=[jax.ShapeDtypeStruct((S, rw), F32)] * 12,
        scratch_shapes=[pltpu.VMEM((2 * pairs, LANES, LANES), F32)],
        compiler_params=_cparams(("arbitrary",)),
    )(*arrays)


def _shift_lerp(x_view, mu, d=None, into=None, *, name):
    arr, off, width = x_view
    S = arr.shape[0]
    cb = _pick(width, 512)
    assert off % cb == 0

    def cshift(t):
        rows = lax.broadcasted_iota(jnp.int32, t.shape, 0)
        prev = jnp.where(rows == 0, 0.0, pltpu.roll(t, 1, 0))
        nxt = jnp.where(rows == S - 1, 0.0, pltpu.roll(t, S - 1, 0))
        return 0.5 * (prev + nxt)

    def fwd_body(x_ref, mu_ref, o_ref):
        x = x_ref[...]
        o_ref[...] = x + mu_ref[...] * (cshift(x) - x)

    def bwd_body(x_ref, mu_ref, d_ref, _, dx_ref, dmu_ref):
        x, m, dd = x_ref[...], mu_ref[...], d_ref[...]
        gm = m * dd
        dx_ref[...] = (dd - gm + cshift(gm)).astype(dx_ref.dtype)
        dmu_ref[...] = jnp.sum(dd * (cshift(x) - x), axis=0, keepdims=True)

    x_spec = pl.BlockSpec((S, cb), lambda j: (0, off // cb + j))
    blk = pl.BlockSpec((S, cb), lambda j: (0, j))
    vec = pl.BlockSpec((1, cb), lambda j: (0, j))
    if d is None:
        return pl.pallas_call(
            fwd_body, name=name, grid=(width // cb,), in_specs=[x_spec, vec], out_specs=blk,
            out_shape=jax.ShapeDtypeStruct((S, width), F32), compiler_params=_cparams(("parallel",)),
        )(arr, mu)
    buf, first = into
    assert first % cb == 0
    return pl.pallas_call(
        bwd_body, name=name, grid=(width // cb,),
        in_specs=[x_spec, vec, blk, pl.BlockSpec(memory_space=pl.ANY)],
        out_specs=[pl.BlockSpec((S, cb), lambda j: (0, first // cb + j)), vec],
        out_shape=[jax.ShapeDtypeStruct(buf.shape, buf.dtype), jax.ShapeDtypeStruct((1, width), F32)],
        input_output_aliases={3: 0}, compiler_params=_cparams(("parallel",)),
    )(arr, mu, d, buf)


def _attention_fwd(qfull, kv, kr, hm, scale, *, tq, name):
    S = qfull.shape[0]
    nt = (((1,), (1,)), ((), ()))

    def body(q_ref, kn_ref, kr_ref, v_ref, o_ref, lse_ref, k_scr):
        _head_keys(kn_ref, kr_ref, k_scr)
        s = lax.dot_general(q_ref[...], k_scr[...], nt, preferred_element_type=F32)
        m = jnp.max(s, axis=-1, keepdims=True)
        p = jnp.exp((s - m) * scale)
        l = jnp.sum(p, axis=-1, keepdims=True)
        o_ref[...] = jnp.dot(p.astype(BF16), v_ref[...], preferred_element_type=F32) * (1.0 / l)
        lse_ref[...] = jnp.broadcast_to(m * scale + jnp.log(l), lse_ref.shape)

    oblk = pl.BlockSpec((tq, VDIM), lambda h, i: (i, h))
    return pl.pallas_call(
        body, name=name, grid=(hm, S // tq),
        in_specs=[pl.BlockSpec((tq, QHEAD), lambda h, i: (i, h)),
                  pl.BlockSpec((S, NOPE), lambda h, i: (0, h)),
                  pl.BlockSpec((S, LANES), lambda h, i: (0, 0)),
                  pl.BlockSpec((S, VDIM), lambda h, i: (0, hm + h))],
        out_specs=[oblk, oblk],
        out_shape=[jax.ShapeDtypeStruct((S, hm * VDIM), F32)] * 2,
        scratch_shapes=[pltpu.VMEM((S, QHEAD), BF16)],
        compiler_params=_cparams(("parallel", "arbitrary")),
    )(qfull, kv, kr, kv)


def _head_keys(kn_ref, kr_ref, k_scr):
    @pl.when(pl.program_id(1) == 0)
    def _():
        k_scr[:, :NOPE] = kn_ref[...]
        k_scr[:, NOPE:] = kr_ref[...]


def _attention_bwd(qfull, kv, kr, o, lse, d_o, hm, scale, *, tq, name):
    S = qfull.shape[0]
    tq = min(tq, S)
    nq = S // tq
    tn = (((0,), (0,)), ((), ()))
    nt = (((1,), (1,)), ((), ()))

    def body(q_ref, kn_ref, kr_ref, v_ref, o_ref, lse_ref, do_ref, dq_ref, dk_ref, dv_ref, k_scr):
        _head_keys(kn_ref, kr_ref, k_scr)
        s = lax.dot_general(q_ref[...], k_scr[...], nt, preferred_element_type=F32)
        p = jnp.exp(s * scale - lse_ref[:, 0:1])
        d_out = do_ref[...]
        delta = jnp.sum(d_out * o_ref[...], axis=-1, keepdims=True)
        d_out = d_out.astype(BF16)
        dp = lax.dot_general(d_out, v_ref[...], nt, preferred_element_type=F32)
        ds = (p * (dp - delta)).astype(BF16)
        dq_ref[...] = jnp.dot(ds, k_scr[...], preferred_element_type=F32) * scale
        dv = lax.dot_general(p.astype(BF16), d_out, tn, preferred_element_type=F32)
        dk = lax.dot_general(ds, q_ref[...], tn, preferred_element_type=F32)
        i = pl.program_id(1)
        for ref, val in ((dk_ref, dk), (dv_ref, dv)):
            @pl.when(i == 0)
            def _(ref=ref, val=val):
                ref[...] = val

            @pl.when(i > 0)
            def _(ref=ref, val=val):
                ref[...] += val

        @pl.when(i == nq - 1)
        def _():
            dk_ref[...] = dk_ref[...] * scale

    qblk = pl.BlockSpec((tq, QHEAD), lambda h, i: (i, h))
    oblk = pl.BlockSpec((tq, VDIM), lambda h, i: (i, h))
    return pl.pallas_call(
        body, name=name, grid=(hm, nq),
        in_specs=[qblk,
                  pl.BlockSpec((S, NOPE), lambda h, i: (0, h)),
                  pl.BlockSpec((S, LANES), lambda h, i: (0, 0)),
                  pl.BlockSpec((S, VDIM), lambda h, i: (0, hm + h)),
                  oblk, oblk, oblk],
        out_specs=[qblk, pl.BlockSpec((S, QHEAD), lambda h, i: (0, h)), pl.BlockSpec((S, VDIM), lambda h, i: (0, h))],
        out_shape=[jax.ShapeDtypeStruct((S, hm * QHEAD), F32), jax.ShapeDtypeStruct((S, hm * QHEAD), F32),
                   jax.ShapeDtypeStruct((S, hm * VDIM), F32)],
        scratch_shapes=[pltpu.VMEM((S, QHEAD), BF16)],
        compiler_params=_cparams(("parallel", "arbitrary")),
    )(qfull, kv, kr, kv, o, lse, d_o)


def _layout(D, MW, RW, TAIL, QR, KVR):
    names = ["gate_m", "gate_r", "z_m", "z_r", "q_a", "kv_a", "r", "k", "v", "tail"]
    widths = [D, D, MW, RW, QR, KVR, RW, RW, RW, TAIL]
    offs, o = {}, 0
    for nme, w in zip(names, widths):
        assert o % w == 0, (nme, o, w)
        offs[nme] = (o, w)
        o += w
    return offs, o


def _local_grads(x, target, W, dims, exchange=None):
    S, D = x.shape
    hm, hr, hn, rank = dims["hm"], dims["hr"], dims["hn"], dims["rank"]
    MW, RW = hm * VDIM, hr * hn
    TAIL = dims["TAIL"]
    QR, KVR = W["mla_q_norm"].shape[1], W["mla_kv_norm"].shape[1]
    lay, d_in = _layout(D, MW, RW, TAIL, QR, KVR)
    T = 256
    scale = (NOPE + ROPE) ** -0.5
    col = lambda arr, nme: _view(arr, *lay[nme])

    pos = jnp.arange(S, dtype=F32)
    inv_freq = jnp.power(ROPE_THETA, -jnp.arange(0, ROPE, 2, dtype=F32) / ROPE)
    ang = pos[:, None] * inv_freq[None, :]
    zpad = jnp.zeros((S, LANES - ROPE), F32)
    cosx = jnp.concatenate([jnp.cos(ang), jnp.cos(ang), zpad], axis=1)
    sinx = jnp.concatenate([jnp.sin(ang), jnp.sin(ang), zpad], axis=1)
    ri, ci = jnp.arange(LANES)[:, None], jnp.arange(LANES)[None, :]
    half = ROPE // 2
    rot = (jnp.where((ri == ci - half) & (ci >= half) & (ci < ROPE), 1.0, 0.0)
           - jnp.where((ri == ci + half) & (ci < half), 1.0, 0.0)).astype(BF16)
    rot_t = rot.T
    seg = (jnp.arange(RW)[:, None] // hn == jnp.arange(LANES)[None, :]).astype(BF16)
    seg_t = seg.T

    (h,) = _rowwise(lambda xb, g: (_rms(xb, g),), [x], [W["g_pre"]], [(D, BF16)], tile=2 * T, name="pre_norm")
    if exchange is None:
        proj = _mm(h, W["w_in_t"], tb=True, name="in_proj")
    else:
        proj, *slabs = _mm(h, W["w_in_t"], tb=True, ride=_gather_plan(exchange[0]), name="in_proj")
        W = {**W, **_prepare_rest(dict(zip(_MATS[1:], slabs)), dims)}

    qn, kvn = _rowwise(_f_mla_norm, [col(proj, "q_a"), col(proj, "kv_a")], [W["mla_q_norm"], W["mla_kv_norm"]],
                       [(QR, BF16), (KVR, BF16)], tile=2 * T, name="mla_norm")
    qraw = _mm(qn, W["wq_b_t"], tb=True, name="q_up")
    kv = _mm(kvn, W["wkv_b"], out_dtype=BF16, name="kv_up")
    kr_view = _view(proj, lay["tail"][0], LANES)
    qfull, kr = _rowwise(functools.partial(_f_rope, hm), [qraw, kr_view, cosx, sinx], [rot, rot_t],
                         [(hm * QHEAD, BF16), (LANES, BF16)], tile=2 * T, name="rope")
    o_mla, lse = _attention_fwd(qfull, kv, kr, hm, scale, tq=T, name="attn_fwd")

    shift_view = (proj, lay["r"][0], 3 * RW + TAIL)
    rl = _shift_lerp(shift_view, W["mu"], name="shift_fwd")
    rl_r, rl_k, rl_v = _view(rl, 0, RW), _view(rl, RW, RW), _view(rl, 2 * RW, RW)
    rl_tail = _view(rl, 3 * RW, TAIL)
    pre_params = [W["w0_f"], W["w0_b"], W["a0_f"], W["a0_b"], W["k_k"], W["k_a"], W["w2cat"], W["a2cat"], seg, seg_t]
    pre_fn = functools.partial(_f_rwkv_pre, RW)
    lw_f, lw_b, k_f, k_b, a_n, b_f, b_b = _rowwise(pre_fn, [rl_k, rl_tail], pre_params, [(RW, F32)] * 7, tile=T,
                                                    name="rwkv_pre")
    ops_f = (rl_r, lw_f, k_f, rl_v, a_n, b_f)
    ops_b = (rl_r, lw_b, k_b, rl_v, a_n, b_b)
    y_f, st_f, y_b, st_b = _rwkv_scan_fwd(ops_f, ops_b, RW, name="scan_fwd")

    post_fn = functools.partial(_f_post, hn)
    post_rows = [y_f, y_b, rl_r, k_f, k_b, rl_v, col(proj, "z_r"), o_mla, col(proj, "z_m")]
    post_params = [W["gn_g"], W["gn_b"], W["r_k"], seg, seg_t]
    ymg, yrg = _rowwise(post_fn, post_rows, post_params, [(MW, BF16), (RW, BF16)], tile=T, name="post")
    u_m = _mm(ymg, W["w_br_mla"], name="br_mla")
    u_r = _mm(yrg, W["w_br_rwkv"], name="br_rwkv")
    merge_rows = [u_m, u_r, col(proj, "gate_m"), col(proj, "gate_r")]
    (merged,) = _rowwise(lambda *t: (_f_merge(*t),), merge_rows, [], [(D, BF16)], tile=2 * T, name="merge")
    out = _mm(merged, W["w_out"], name="out_proj")

    def head(ob, xb, tb, g):
        yn, vjp = jax.vjp(_rms, ob, g)
        err = xb + yn - tb
        dy = err * (1.0 / D)
        d_ob, d_g = vjp(dy)
        loss = jnp.broadcast_to(0.5 * jnp.sum(err * err) * (1.0 / D), (1, LANES))
        return dy, d_ob, loss, d_g

    dy, d_out, loss, g_g_post = _rowwise(head, [out, x, target], [W["g_post"]], [(D, F32), (D, BF16)],
                                         [(1, LANES), (1, D)], tile=2 * T, name="head")
    d_merged = _mm(d_out, W["w_out"], tb=True, name="d_merged")
    g_w_out = _mm(merged, d_out, ta=True, out_dtype=BF16, name="g_w_out")

    def merge_bwd(u_m_b, u_r_b, g_m_b, g_r_b, dm):
        _, vjp = jax.vjp(_f_merge, u_m_b, u_r_b, g_m_b, g_r_b)
        du_m, du_r, dg_m, dg_r = vjp(dm)
        return du_m, du_r, jnp.concatenate([dg_m, dg_r], axis=1)

    d_u_m, d_u_r, d_proj = _rowwise(merge_bwd, merge_rows + [d_merged], [],
                                    [(D, BF16), (D, BF16), (2 * D, BF16, (None, d_in, lay["gate_m"][0]))], tile=T,
                                    name="merge_bwd")
    d_ymg = _mm(d_u_m, W["w_br_mla"], tb=True, name="d_ymg")
    d_yrg = _mm(d_u_r, W["w_br_rwkv"], tb=True, name="d_yrg")
    g_w_br_mla = _mm(ymg, d_u_m, ta=True, out_dtype=BF16, name="g_w_br_mla")
    g_w_br_rwkv = _mm(yrg, d_u_r, ta=True, out_dtype=BF16, name="g_w_br_rwkv")

    def post_bwd(*args):
        nr = len(post_rows)
        prim, dm, dr = args[:nr] + args[nr + 2:], args[nr], args[nr + 1]
        _, vjp = jax.vjp(post_fn, *prim)
        g = vjp((dm, dr))
        return g[0], g[2], g[3], g[5], g[7], jnp.concatenate([g[8], g[6]], axis=1), g[9], g[10], g[11]

    (d_y, d_r_bonus, d_k_bonus, d_v_bonus, d_o, d_proj, g_gn_g, g_gn_b, g_r_k) = _rowwise(
        post_bwd, post_rows + [d_ymg, d_yrg], post_params,
        [(RW, F32), (RW, F32), (RW, F32), (RW, F32), (MW, F32), (MW + RW, BF16, (d_proj, d_in, lay["z_m"][0]))],
        [(1, RW)] * 3, tile=T // 2, name="post_bwd")

    dscan = _rwkv_scan_bwd(ops_f, ops_b, st_f, st_b, d_y, RW, name="scan_bwd")
    dsc = {"f": dscan[:6], "b": dscan[6:]}

    d_q_att, d_k_att, d_v_att = _attention_bwd(qfull, kv, kr, o_mla, lse, d_o, hm, scale, tq=2 * T, name="attn_bwd")

    def rope_bwd(qraw_b, kr_in, cos_b, sin_b, dq_b, dk_b, dv_b, rot_b, rot_t_b):
        _, vjp = jax.vjp(lambda q_, k_: _f_rope(hm, q_, k_, cos_b, sin_b, rot_b, rot_t_b), qraw_b, kr_in)
        dkn = jnp.concatenate([dk_b[:, hh * QHEAD:hh * QHEAD + NOPE] for hh in range(hm)], axis=1)
        dkr = dk_b[:, NOPE:QHEAD]
        for hh in range(1, hm):
            dkr = dkr + dk_b[:, hh * QHEAD + NOPE:(hh + 1) * QHEAD]
        d_qraw, d_kr_in = vjp((dq_b, dkr))
        return d_qraw, jnp.concatenate([dkn, dv_b], axis=1), d_kr_in

    d_qraw, d_kv, d_kr_in = _rowwise(rope_bwd, [qraw, kr_view, cosx, sinx, d_q_att, d_k_att, d_v_att],
                                     [rot, rot_t], [(hm * QHEAD, BF16), (2 * MW, BF16), (LANES, F32)], tile=T,
                                     name="rope_bwd")
    d_qnorm = _mm(d_qraw, W["wq_b_t"], name="d_qn")
    d_kvnorm = _mm(d_kv, W["wkv_b"], tb=True, name="d_kvn")
    g_wq_b = _mm(d_qraw, qn, ta=True, out_dtype=BF16, name="g_wq_b")
    g_wkv_b = _mm(kvn, d_kv, ta=True, out_dtype=BF16, name="g_wkv_b")

    def mla_norm_bwd(q_a, kv_a, qg, kvg, dq, dk):
        _, vjp = jax.vjp(_f_mla_norm, q_a, kv_a, qg, kvg)
        d_q_a, d_kv_a, d_qg, d_kvg = vjp((dq, dk))
        return jnp.concatenate([d_q_a, d_kv_a], axis=1), d_qg, d_kvg

    d_proj, g_q_norm, g_kv_norm = _rowwise(
        lambda q_a, kv_a, dq, dk, qg, kvg: mla_norm_bwd(q_a, kv_a, qg, kvg, dq, dk),
        [col(proj, "q_a"), col(proj, "kv_a"), d_qnorm, d_kvnorm], [W["mla_q_norm"], W["mla_kv_norm"]],
        [(QR + KVR, BF16, (d_proj, d_in, lay["q_a"][0]))], [(1, QR), (1, KVR)], tile=2 * T, name="mla_norm_bwd")

    def pre_bwd(k_b_, tail_b, dlwf, dlwb, dkf, dkb, dkbon, daf, dab, dbf, dbb, drf, drb, drbon, dvf, dvb, dvbon,
                dkr, *params):
        w2, a2 = params[6], params[7]
        nt, tn = (((1,), (1,)), ((), ())), (((0,), (0,)), ((), ()))
        split = w2.shape[0]
        th = jnp.tanh(tail_b[:, :split])
        th_b, tail_h = th.astype(BF16), tail_b[:, split:].astype(BF16)
        zw = jnp.dot(th_b, w2, preferred_element_type=F32)
        za = jnp.dot(tail_h, a2, preferred_element_type=F32)
        _, vjp = jax.vjp(functools.partial(_f_rwkv_core, RW), k_b_, zw, za, *params[:6], params[8], params[9])
        g = vjp((dlwf, dlwb, dkf + dkbon, dkb + dkbon, daf + dab, dbf, dbb))
        d_zw, d_za = g[1].astype(BF16), g[2].astype(BF16)
        d_tail = (jnp.concatenate([lax.dot_general(d_zw, w2, nt, preferred_element_type=F32) * (1.0 - th * th),
                                   lax.dot_general(d_za, a2, nt, preferred_element_type=F32)], axis=1)
                  + jnp.concatenate([dkr, jnp.zeros((dkr.shape[0], TAIL - LANES), F32)], axis=1))
        g_w2 = lax.dot_general(th_b, d_zw, tn, preferred_element_type=F32)
        g_a2 = lax.dot_general(tail_h, d_za, tn, preferred_element_type=F32)
        d_rl = jnp.concatenate([drf + drb + drbon, g[0], dvf + dvb + dvbon, d_tail], axis=1)
        return (d_rl,) + tuple(g[3:9]) + (g_w2, g_a2)

    f_, b_ = dsc["f"], dsc["b"]
    pre_bwd_rows = [rl_k, rl_tail, f_[1], b_[1], f_[2], b_[2], d_k_bonus, f_[4], b_[4], f_[5], b_[5],
                    f_[0], b_[0], d_r_bonus, f_[3], b_[3], d_v_bonus, d_kr_in]
    (d_rl, g_w0_f, g_w0_b, g_a0_f, g_a0_b, g_k_k, g_k_a, g_w2cat, g_a2cat) = _rowwise(
        pre_bwd, pre_bwd_rows, pre_params, [(3 * RW + TAIL, F32)],
        [(1, RW)] * 6 + [W["w2cat"].shape, W["a2cat"].shape], tile=T // 2, name="rwkv_pre_bwd")
    d_proj, g_mu = _shift_lerp(shift_view, W["mu"], d_rl, (d_proj, lay["r"][0]), name="shift_bwd")
    small = dict(wq_b=g_wq_b, wkv_b=g_wkv_b, w2cat=g_w2cat, a2cat=g_a2cat, w_br_mla=g_w_br_mla,
                 w_br_rwkv=g_w_br_rwkv, w_out=g_w_out)
    if exchange is None:
        received = None
        g_w_in = _mm(d_proj, h, ta=True, out_dtype=BF16, tn_cap=1024, name="g_w_in")
        d_h = _mm(d_proj, W["w_in_t"], tn_cap=1024, name="d_h")
    else:
        slabs = _restore_rest(small, dims)
        slabs = [slabs[n] for n in _MATS[1:]]
        g_w_in, *got = _mm(d_proj, h, ta=True, out_dtype=BF16, tn_cap=1024, ride=_sibling_swap_plan(slabs),
                           name="g_w_in")
        sums = [_pair_add(exchange[1], s, t, name="pair_add_" + n) for n, s, t in zip(_MATS[1:], slabs, got)]
        g_w_in = _restore_w_in(g_w_in, dims)
        d_h, *received = _mm(d_proj, W["w_in_t"], tn_cap=1024, name="d_h",
                             ride=_join_plans(_chip_exchange_plan(sums), _sibling_swap_plan([g_w_in])))
        small = {}

    def pre_norm_bwd(xb, dyb, dhb, g):
        _, vjp = jax.vjp(_rms, xb, g)
        dx, dg = vjp(dhb)
        return dyb + dx, dg

    grad_x, g_g_pre = _rowwise(pre_norm_bwd, [x, dy, d_h], [W["g_pre"]], [(D, F32)], [(1, D)], tile=2 * T,
                               name="pre_norm_bwd")

    grads = dict(g_pre=g_g_pre, w_in=g_w_in, mla_q_norm=g_q_norm, mla_kv_norm=g_kv_norm, mu=g_mu, w0_f=g_w0_f,
                 w0_b=g_w0_b, a0_f=g_a0_f, a0_b=g_a0_b, k_k=g_k_k, k_a=g_k_a, r_k=g_r_k, gn_g=g_gn_g, gn_b=g_gn_b,
                 g_post=g_g_post, **small)
    return loss[0, 0], grad_x, grads, received


_MATS = ["w_in", "mla_wq_b", "mla_wkv_b", "rwkv_w2_f", "rwkv_w2_b", "rwkv_a2_f", "rwkv_a2_b", "w_br_mla",
         "w_br_rwkv", "w_out"]
_ROW_SHARDED = ("w_out",)
_TRANSPOSED = ("w_in", "mla_wq_b")
_VECS = ["g_pre", "mla_q_norm", "mla_kv_norm", "rwkv_mu", "rwkv_w0_f", "rwkv_w0_b", "rwkv_a0_f", "rwkv_a0_b",
         "rwkv_k_k", "rwkv_k_a", "rwkv_r_k", "rwkv_gn_g", "rwkv_gn_b", "g_post"]
_WEIGHTS = ["g_pre", "w_in", "mla_q_norm", "mla_wq_b", "mla_kv_norm", "mla_wkv_b", "rwkv_mu", "rwkv_w0_f",
            "rwkv_w2_f", "rwkv_w0_b", "rwkv_w2_b", "rwkv_a0_f", "rwkv_a2_f", "rwkv_a0_b", "rwkv_a2_b", "rwkv_k_k",
            "rwkv_k_a", "rwkv_r_k", "rwkv_gn_g", "rwkv_gn_b", "w_br_mla", "w_br_rwkv", "w_out", "g_post"]

def _direct_gather_plan(src):
    def phases(src_refs, out_refs, sem_refs):
        (src_ref,), (out_ref,), sems, local_sem = src_refs, out_refs, sem_refs[:2], sem_refs[2]
        x, y, c = lax.axis_index("x"), lax.axis_index("y"), lax.axis_index("c")
        me = 4 * x + 2 * y + c
        flip = lambda v, bit: (1 - v) if bit else v
        peers = [(flip(x, d & 4), flip(y, d & 2), flip(c, d & 1)) for d in range(1, N_DEV)]
        own = lambda: pltpu.make_async_copy(src_ref, out_ref.at[me], local_sem)
        send = lambda d: _remote(src_ref, out_ref.at[me], sems, d, peers[d])

        def first():
            own().start()
            for d in range(N_DEV - 1):
                send(d).start()

        def last():
            for d, (px, py, pc) in enumerate(peers):
                blk = out_ref.at[4 * px + 2 * py + pc]
                _remote(blk, blk, sems, d, (x, y, c)).wait_recv()
            for d in range(N_DEV - 1):
                send(d).wait_send()
            own().wait()

        return first, (lambda: None), last

    return [src], [jax.ShapeDtypeStruct((N_DEV,) + src.shape, src.dtype)], [(N_DEV - 1,), (N_DEV - 1,), ()], phases


def _remote(src, dst, sems, key, to):
    send_sems, recv_sems = sems
    return pltpu.make_async_remote_copy(src_ref=src, dst_ref=dst, send_sem=send_sems.at[key], recv_sem=recv_sems.at[key],
                                        device_id=to, device_id_type=pl.DeviceIdType.MESH)


def _run_exchange(plan, *, name):
    srcs, out_shapes, sem_shapes, phases = plan
    n, m = len(srcs), len(out_shapes)

    def body(*refs):
        for phase in phases(refs[:n], refs[n:n + m], refs[n + m:]):
            phase()

    return pl.pallas_call(
        body, name=name, out_shape=out_shapes,
        in_specs=[pl.BlockSpec(memory_space=pl.ANY)] * n, out_specs=[pl.BlockSpec(memory_space=pl.ANY)] * m,
        scratch_shapes=[pltpu.SemaphoreType.DMA(s) for s in sem_shapes],
    )(*srcs)


def _join_plans(p, q):
    (srcs_p, outs_p, sems_p, phases_p), (srcs_q, outs_q, sems_q, phases_q) = p, q

    def phases(src_refs, out_refs, sem_refs):
        a = phases_p(src_refs[:len(srcs_p)], out_refs[:len(outs_p)], sem_refs[:len(sems_p)])
        b = phases_q(src_refs[len(srcs_p):], out_refs[len(outs_p):], sem_refs[len(sems_p):])

        def both(fa, fb):
            def run():
                fa()
                fb()
            return run

        return tuple(both(fa, fb) for fa, fb in zip(a, b))

    return list(srcs_p) + list(srcs_q), list(outs_p) + list(outs_q), list(sems_p) + list(sems_q), phases


def _gather_plan(srcs):
    n = len(srcs)

    def phases(src_refs, out_refs, sem_refs):
        sems, local_sems = sem_refs[:2], sem_refs[2]
        x, y, c = lax.axis_index("x"), lax.axis_index("y"), lax.axis_index("c")
        idx = lambda px, py, pc: 4 * px + 2 * py + pc
        me, sibling = (x, y, c), (x, y, 1 - c)
        chips = [(1 - x, y), (x, 1 - y), (1 - x, 1 - y)]
        own = lambda a: pltpu.make_async_copy(src_refs[a], out_refs[a].at[idx(*me)], local_sems.at[a])
        to_sibling = lambda a: _remote(src_refs[a], out_refs[a].at[idx(*me)], sems, (0, a), sibling)
        to_chip = lambda a, j: _remote(src_refs[a], out_refs[a].at[idx(*me)], sems, (1 + j, a), (*chips[j], c))
        landed = lambda a, j: out_refs[a].at[idx(*chips[j], c)]
        passed_on = lambda a, j: _remote(landed(a, j), landed(a, j), sems, (4 + j, a), sibling)

        def first():
            for a in range(n):
                own(a).start()
                to_sibling(a).start()
                for j in range(3):
                    to_chip(a, j).start()

        def middle():
            for j in range(3):
                for a in range(n):
                    _remote(landed(a, j), landed(a, j), sems, (1 + j, a), me).wait_recv()
                    passed_on(a, j).start()

        def last():
            for a in range(n):
                blk = out_refs[a].at[idx(*sibling)]
                _remote(blk, blk, sems, (0, a), me).wait_recv()
                for j in range(3):
                    blk = out_refs[a].at[idx(*chips[j], 1 - c)]
                    _remote(blk, blk, sems, (4 + j, a), me).wait_recv()
            for a in range(n):
                to_sibling(a).wait_send()
                for j in range(3):
                    to_chip(a, j).wait_send()
                    passed_on(a, j).wait_send()
                own(a).wait()

        return first, middle, last

    return srcs, [jax.ShapeDtypeStruct((N_DEV,) + s.shape, s.dtype) for s in srcs], [(7, n), (7, n), (n,)], phases


def _sibling_swap_plan(srcs):
    n = len(srcs)

    def phases(src_refs, out_refs, sems):
        x, y, c = lax.axis_index("x"), lax.axis_index("y"), lax.axis_index("c")
        copies = lambda: [_remote(src_refs[a].at[2 * q + 1 - c], out_refs[a].at[q], sems, (q, a), (x, y, 1 - c))
                          for a in range(n) for q in range(4)]

        def first():
            for cp in copies():
                cp.start()

        def last():
            for cp in copies():
                cp.wait()

        return first, (lambda: None), last

    return srcs, [jax.ShapeDtypeStruct((4,) + s.shape[1:], s.dtype) for s in srcs], [(4, n), (4, n)], phases


def _chip_exchange_plan(srcs):
    n = len(srcs)

    def phases(src_refs, out_refs, sem_refs):
        sems, local_sems = sem_refs[:2], sem_refs[2]
        x, y, c = lax.axis_index("x"), lax.axis_index("y"), lax.axis_index("c")
        mine = 2 * x + y
        chips = [(1 - x, y), (x, 1 - y), (1 - x, 1 - y)]
        own = lambda a: pltpu.make_async_copy(src_refs[a].at[mine], out_refs[a].at[mine], local_sems.at[a])
        send = lambda a, j: _remote(src_refs[a].at[2 * chips[j][0] + chips[j][1]], out_refs[a].at[mine], sems, (j, a),
                                    (*chips[j], c))

        def first():
            for a in range(n):
                own(a).start()
                for j in range(3):
                    send(a, j).start()

        def last():
            for j in range(3):
                for a in range(n):
                    blk = out_refs[a].at[2 * chips[j][0] + chips[j][1]]
                    _remote(blk, blk, sems, (j, a), (x, y, c)).wait_recv()
            for a in range(n):
                for j in range(3):
                    send(a, j).wait_send()
                own(a).wait()

        return first, (lambda: None), last

    return srcs, [jax.ShapeDtypeStruct(s.shape, s.dtype) for s in srcs], [(3, n), (3, n), (n,)], phases


def _pair_add(core, g, got, *, name):
    q, r, c = got.shape
    tr, tc = _tile2d(r, c, cap=1024)

    def body(core_ref, a_ref, b_ref, o_ref):
        o_ref[...] = (a_ref[...].astype(F32) + b_ref[...].astype(F32)).astype(BF16)

    blk = pl.BlockSpec((1, tr, tc), lambda i, j, k, core_ref: (i, j, k))
    mine = pl.BlockSpec((1, tr, tc), lambda i, j, k, core_ref: (2 * i + core_ref[0], j, k))
    return pl.pallas_call(
        body, name=name, out_shape=jax.ShapeDtypeStruct(got.shape, BF16),
        grid_spec=pltpu.PrefetchScalarGridSpec(num_scalar_prefetch=1, grid=(q, r // tr, c // tc),
                                               in_specs=[mine, blk], out_specs=blk),
        compiler_params=_cparams(("parallel", "parallel", "parallel")))(core, g, got)


def _adamw(recv, w, m, v, *, name):
    r, c = w.shape
    n_terms = recv.shape[0]
    tr, tc = _tile2d(r, c, cap=512)

    def body(g_ref, w_ref, m_ref, v_ref, go_ref, d_ref, mo_ref, vo_ref):
        g = g_ref[0].astype(F32)
        for k in range(1, n_terms):
            g = g + g_ref[k].astype(F32)
        m_new = ADAM_B1 * m_ref[...] + (1.0 - ADAM_B1) * g
        v_new = ADAM_B2 * v_ref[...] + (1.0 - ADAM_B2) * (g * g)
        m_hat = m_new / (1.0 - ADAM_B1 ** ADAM_STEP)
        v_hat = v_new / (1.0 - ADAM_B2 ** ADAM_STEP)
        go_ref[...] = g
        d_ref[...] = -ADAM_LR * (m_hat / (jnp.sqrt(v_hat) + ADAM_EPS) + ADAM_WD * w_ref[...])
        mo_ref[...] = m_new
        vo_ref[...] = v_new

    blk = pl.BlockSpec((tr, tc), lambda i, j: (i, j))
    return pl.pallas_call(
        body, name=name, grid=(r // tr, c // tc),
        in_specs=[pl.BlockSpec((n_terms, tr, tc), lambda i, j: (0, i, j)), blk, blk, blk], out_specs=[blk] * 4,
        out_shape=[jax.ShapeDtypeStruct((r, c), F32)] * 4, compiler_params=_cparams(("parallel", "parallel")),
    )(recv, w, m, v)


def _tile2d(r, c, cap=256):
    if r <= cap:
        return r, c
    for t in range(cap - cap % BF16_ROWS, 0, -BF16_ROWS):
        if r % t == 0:
            return t, c
    return r, _pick(c, cap)


def _pack(pieces):
    total = sum(p.shape[0] for p in pieces)
    pad = (-total) % (8 * LANES)
    flat = jnp.concatenate(list(pieces) + [jnp.zeros((pad,), F32)])
    return flat.reshape(-1, LANES)


def _unpack(flat, sizes):
    flat = flat.reshape(-1)
    out, o = [], 0
    for n in sizes:
        out.append(flat[o:o + n])
        o += n
    return out


def _prepare_weights(full, vec, dims):
    rest = {n: t for n, t in full.items() if n != "w_in"}
    return {"w_in_t": _prepare_w_in(full["w_in"], dims), **_prepare_rest(rest, dims), **_prepare_vectors(vec, dims)}


def _prepare_w_in(slabs, dims):
    D = dims["D"]
    flat = slabs.reshape(-1, D)
    parts, pos = [], 0
    for orig_off, width, perm_off in sorted(dims["segs"], key=lambda t: t[2]):
        if perm_off > pos:
            parts.append(jnp.zeros((perm_off - pos, D), BF16))
        parts.append(flat[orig_off:orig_off + width])
        pos = perm_off + width
    if dims["d_in_perm"] > pos:
        parts.append(jnp.zeros((dims["d_in_perm"] - pos, D), BF16))
    return jnp.concatenate(parts, axis=0)


def _prepare_rest(full, dims):
    hm, hr, hn, rank = dims["hm"], dims["hr"], dims["hn"], dims["rank"]
    QR, KVR = dims["QR"], dims["KVR"]
    RW, TAIL = hr * hn, dims["TAIL"]
    full = {n: (t.reshape(-1, t.shape[2]) if n in _ROW_SHARDED + _TRANSPOSED
                else t.transpose(1, 0, 2).reshape(t.shape[1], -1)) for n, t in full.items()}
    wq = full["mla_wq_b"].reshape(hm, NOPE + ROPE, QR)
    wq = jnp.concatenate([wq, jnp.zeros((hm, QHEAD - NOPE - ROPE, QR), BF16)], axis=1).reshape(hm * QHEAD, QR)
    wkv = full["mla_wkv_b"].reshape(KVR, hm, 2, NOPE).transpose(0, 2, 1, 3).reshape(KVR, 2 * hm * NOPE)
    z = lambda rows: jnp.zeros((rows, RW), BF16)
    f = lambda nme: full[nme]
    split = ROPE + 2 * rank
    assert split % LANES == 0, split
    w2cat = jnp.concatenate([
        jnp.concatenate([z(ROPE), f("rwkv_w2_f"), z(rank)], axis=0),
        jnp.concatenate([z(ROPE + rank), f("rwkv_w2_b")], axis=0)], axis=1)
    a2cat = jnp.concatenate([
        jnp.concatenate([f("rwkv_a2_f"), z(TAIL - split - rank)], axis=0),
        jnp.concatenate([z(rank), f("rwkv_a2_b"), z(TAIL - split - 2 * rank)], axis=0)], axis=1)
    return dict(wq_b_t=wq, wkv_b=wkv, w2cat=w2cat, a2cat=a2cat, w_br_mla=full["w_br_mla"],
                w_br_rwkv=full["w_br_rwkv"], w_out=full["w_out"])


def _prepare_vectors(vec, dims):
    rank, RW, TAIL = dims["rank"], dims["hr"] * dims["hn"], dims["TAIL"]
    mu = vec["rwkv_mu"]
    mu_p = jnp.concatenate([mu[:3 * RW], jnp.zeros((ROPE,), F32), mu[3 * RW:],
                            jnp.zeros((TAIL - ROPE - 4 * rank,), F32)])
    row = lambda t: t.reshape(1, -1)
    return dict(
        mu=row(mu_p), g_pre=row(vec["g_pre"]), g_post=row(vec["g_post"]), mla_q_norm=row(vec["mla_q_norm"]),
        mla_kv_norm=row(vec["mla_kv_norm"]), w0_f=row(vec["rwkv_w0_f"]), w0_b=row(vec["rwkv_w0_b"]),
        a0_f=row(vec["rwkv_a0_f"]), a0_b=row(vec["rwkv_a0_b"]), k_k=row(vec["rwkv_k_k"]), k_a=row(vec["rwkv_k_a"]),
        r_k=row(vec["rwkv_r_k"]), gn_g=row(vec["rwkv_gn_g"]), gn_b=row(vec["rwkv_gn_b"]))


def _restore_grads(g, dims):
    return {"w_in": _restore_w_in(g["w_in"], dims), **_restore_rest(g, dims), **_restore_vectors(g, dims)}


def _restore_w_in(gw, dims):
    parts = [gw[perm_off:perm_off + width] for _, width, perm_off in sorted(dims["segs"])]
    return jnp.concatenate(parts, axis=0).reshape(N_DEV, dims["d_in"] // N_DEV, gw.shape[1])


def _restore_rest(g, dims):
    hm, hr, hn, rank = dims["hm"], dims["hr"], dims["hn"], dims["rank"]
    QR, KVR, RW = dims["QR"], dims["KVR"], hr * hn
    wq = g["wq_b"].reshape(hm, QHEAD, QR)[:, :NOPE + ROPE].reshape(N_DEV, -1, QR)
    wkv = g["wkv_b"].reshape(KVR, 2, hm, NOPE).transpose(0, 2, 1, 3).reshape(KVR, 2 * hm * NOPE)
    lo = lambda t, first, half: t[first:first + rank, half * RW:(half + 1) * RW].astype(BF16)
    cols = lambda t: t.reshape(t.shape[0], N_DEV, -1).transpose(1, 0, 2)
    return dict(
        mla_wq_b=wq, mla_wkv_b=cols(wkv), rwkv_w2_f=cols(lo(g["w2cat"], ROPE, 0)),
        rwkv_w2_b=cols(lo(g["w2cat"], ROPE + rank, 1)), rwkv_a2_f=cols(lo(g["a2cat"], 0, 0)),
        rwkv_a2_b=cols(lo(g["a2cat"], rank, 1)), w_br_mla=cols(g["w_br_mla"]), w_br_rwkv=cols(g["w_br_rwkv"]),
        w_out=g["w_out"].reshape(N_DEV, -1, g["w_out"].shape[1]))


def _restore_vectors(g, dims):
    rank, RW = dims["rank"], dims["hr"] * dims["hn"]
    mu = g["mu"][0]
    out = dict(
        rwkv_mu=jnp.concatenate([mu[:3 * RW], mu[3 * RW + ROPE:3 * RW + ROPE + 4 * rank]]),
        g_pre=g["g_pre"][0], g_post=g["g_post"][0], mla_q_norm=g["mla_q_norm"][0], mla_kv_norm=g["mla_kv_norm"][0],
        rwkv_w0_f=g["w0_f"][0], rwkv_w0_b=g["w0_b"][0], rwkv_a0_f=g["a0_f"][0], rwkv_a0_b=g["a0_b"][0],
        rwkv_k_k=g["k_k"][0], rwkv_k_a=g["k_a"][0], rwkv_r_k=g["r_k"][0], rwkv_gn_g=g["gn_g"][0],
        rwkv_gn_b=g["gn_b"][0])
    return out


def _dims(inp):
    D = inp["x"].shape[-1]
    QR, KVR = inp["mla_q_norm"].shape[0], inp["mla_kv_norm"].shape[0]
    hm = inp["mla_wq_b"].shape[1] * N_DEV // (NOPE + ROPE)
    hr, hn = inp["rwkv_r_k"].shape
    rank = inp["rwkv_w2_f"].shape[0]
    MW, RW = hm * VDIM, hr * hn
    TAIL = -(-(ROPE + 4 * rank) // LANES) * LANES
    orig, o = {}, 0
    for nme, w in (("q_a", QR), ("kv_a", KVR), ("k_rope", ROPE), ("rkv", 3 * RW), ("lora", 4 * rank), ("z_m", MW),
                   ("z_r", RW), ("gate_m", D), ("gate_r", D)):
        orig[nme] = (o, w)
        o += w
    assert o == inp["w_in"].shape[1] * N_DEV
    lay, d_in_perm = _layout(D, MW, RW, TAIL, QR, KVR)
    perm_off = dict(q_a=lay["q_a"][0], kv_a=lay["kv_a"][0], k_rope=lay["tail"][0], rkv=lay["r"][0],
                    lora=lay["tail"][0] + ROPE, z_m=lay["z_m"][0], z_r=lay["z_r"][0], gate_m=lay["gate_m"][0],
                    gate_r=lay["gate_r"][0])
    segs = [(orig[nme][0], orig[nme][1], perm_off[nme]) for nme in orig]
    return dict(D=D, QR=QR, KVR=KVR, hm=hm, hr=hr, hn=hn, rank=rank, TAIL=TAIL, segs=segs, d_in=o,
                d_in_perm=d_in_perm)


def kernel(x, g_pre, w_in, mla_q_norm, mla_wq_b, mla_kv_norm, mla_wkv_b, rwkv_mu, rwkv_w0_f, rwkv_w2_f, rwkv_w0_b, rwkv_w2_b, rwkv_a0_f, rwkv_a2_f, rwkv_a0_b, rwkv_a2_b, rwkv_k_k, rwkv_k_a, rwkv_r_k, rwkv_gn_g, rwkv_gn_b, w_br_mla, w_br_rwkv, w_out, g_post, loss_target, m_g_pre, m_w_in, m_mla_q_norm, m_mla_wq_b, m_mla_kv_norm, m_mla_wkv_b, m_rwkv_mu, m_rwkv_w0_f, m_rwkv_w2_f, m_rwkv_w0_b, m_rwkv_w2_b, m_rwkv_a0_f, m_rwkv_a2_f, m_rwkv_a0_b, m_rwkv_a2_b, m_rwkv_k_k, m_rwkv_k_a, m_rwkv_r_k, m_rwkv_gn_g, m_rwkv_gn_b, m_w_br_mla, m_w_br_rwkv, m_w_out, m_g_post, v_g_pre, v_w_in, v_mla_q_norm, v_mla_wq_b, v_mla_kv_norm, v_mla_wkv_b, v_rwkv_mu, v_rwkv_w0_f, v_rwkv_w2_f, v_rwkv_w0_b, v_rwkv_w2_b, v_rwkv_a0_f, v_rwkv_a2_f, v_rwkv_a0_b, v_rwkv_a2_b, v_rwkv_k_k, v_rwkv_k_a, v_rwkv_r_k, v_rwkv_gn_g, v_rwkv_gn_b, v_w_br_mla, v_w_br_rwkv, v_w_out, v_g_post):
    inp = dict(locals())
    dims = _dims(inp)
    stored = lambda t, n: t.T if n in _TRANSPOSED else t
    assert _MATS[0] == "w_in"
    shards = [stored(inp[n], n).astype(BF16) for n in _MATS]
    core = lax.axis_index("c").astype(jnp.int32).reshape(1)
    (w_in_slabs,) = _run_exchange(_gather_plan(shards[:1]), name="gather_w_in")
    W = {"w_in_t": _prepare_w_in(w_in_slabs, dims), **_prepare_vectors({n: inp[n] for n in _VECS}, dims)}
    loss, grad_x, g, recv_rest = _local_grads(x[0], loss_target[0], W, dims, exchange=(shards[1:], core))

    new = {}
    *recv_rest, got = recv_rest
    g_w_in, g = g["w_in"], _restore_vectors(g, dims)
    vsizes = [inp[n].size for n in _VECS] + [1]
    vflat = lambda prefix, src, last: _pack([src[prefix + n].reshape(-1) for n in _VECS] + [last])
    one = jnp.zeros((1,), F32)
    recv_w_in, vrecv = _run_exchange(
        _join_plans(_chip_exchange_plan([_pair_add(core, g_w_in, got, name="pair_add_w_in")]),
                    _direct_gather_plan(vflat("", g, loss.reshape(1)))), name="scatter_w_in")
    for n, t in zip(_MATS, [recv_w_in] + recv_rest):
        out = _adamw(t, stored(inp[n], n), stored(inp["m_" + n], n), stored(inp["v_" + n], n), name="adamw_" + n)
        new[n] = [stored(o, n) for o in out]

    vout = _adamw(vrecv, vflat("", inp, one), vflat("m_", inp, one), vflat("v_", inp, one), name="adamw_vectors")
    vparts = [_unpack(t, vsizes) for t in vout]
    for i, n in enumerate(_VECS):
        new[n] = [vp[i].reshape(inp[n].shape) for vp in vparts]
    loss = vparts[0][-1].reshape(())

    outs = [loss, grad_x[None]]
    for k in range(4):
        outs += [new[n][k] for n in _WEIGHTS]
    return tuple(outs)
```

```python
import functools
import math

import jax
import jax.numpy as jnp
from jax import lax
from jax.experimental import pallas as pl
from jax.experimental.pallas import tpu as pltpu

F32 = jnp.float32
BF16 = jnp.bfloat16

N_DEV = 8
LANES = 128
BF16_ROWS = 16
NOPE, ROPE, VDIM = 128, 64, 128
QHEAD = 256
ROPE_THETA = 10000.0
NORM_EPS = 1e-6
GN_EPS = 64e-5
CHUNK = 64
SUB = 16
VMEM_LIMIT = 56 * 1024 * 1024

ADAM_LR, ADAM_B1, ADAM_B2, ADAM_EPS, ADAM_WD, ADAM_STEP = 0.001, 0.9, 0.999, 1e-08, 0.01, 10


def _cparams(sem):
    return pltpu.CompilerParams(dimension_semantics=sem, vmem_limit_bytes=VMEM_LIMIT)


def _pick(n, cap):
    if n <= cap:
        return n
    for t in range(cap - cap % LANES, 0, -LANES):
        if n % t == 0:
            return t
    raise ValueError(f"no tile for {n} under {cap}")


def _mm(a, b, *, ta=False, tb=False, out_dtype=F32, name, tm_cap=1024, tn_cap=512, tk_cap=2048, ride=None):
    K, M = a.shape if ta else a.shape[::-1]
    N = b.shape[0] if tb else b.shape[1]
    assert (b.shape[1] if tb else b.shape[0]) == K, (a.shape, b.shape, ta, tb)
    tm, tn, tk = _pick(M, tm_cap), _pick(N, tn_cap), _pick(K, tk_cap)
    nj, nk = N // tn, K // tk
    steps = (M // tm) * nj * nk
    dn = (((0 if ta else 1,), (1 if tb else 0,)), ((), ()))
    srcs, extra_shapes, sem_shapes, phases = ride if ride else ((), (), (), None)
    n_src, n_extra = len(srcs), len(extra_shapes)

    def body(*refs):
        a_ref, b_ref, o_ref = refs[0], refs[1], refs[2 + n_src]
        acc_ref = refs[3 + n_src + n_extra]
        k = pl.program_id(2)
        if ride:
            step = (pl.program_id(0) * nj + pl.program_id(1)) * nk + k
            first, middle, last = phases(refs[2:2 + n_src], refs[3 + n_src:3 + n_src + n_extra],
                                         refs[4 + n_src + n_extra:])
            pl.when(step == 0)(first)
            pl.when(step == (steps * 15) // 16)(middle)
        p = lax.dot_general(a_ref[...], b_ref[...], dn, preferred_element_type=F32)

        @pl.when(k == 0)
        def _():
            acc_ref[...] = p

        @pl.when(k > 0)
        def _():
            acc_ref[...] += p

        @pl.when(k == nk - 1)
        def _():
            o_ref[...] = acc_ref[...].astype(out_dtype)

        if ride:
            pl.when(step == steps - 1)(last)

    a_spec = pl.BlockSpec((tk, tm), lambda i, j, k: (k, i)) if ta else pl.BlockSpec((tm, tk), lambda i, j, k: (i, k))
    b_spec = pl.BlockSpec((tn, tk), lambda i, j, k: (j, k)) if tb else pl.BlockSpec((tk, tn), lambda i, j, k: (k, j))
    hbm = pl.BlockSpec(memory_space=pl.ANY)
    out = pl.pallas_call(
        body, name=name, grid=(M // tm, nj, nk),
        in_specs=[a_spec, b_spec] + [hbm] * n_src,
        out_specs=[pl.BlockSpec((tm, tn), lambda i, j, k: (i, j))] + [hbm] * n_extra,
        out_shape=[jax.ShapeDtypeStruct((M, N), out_dtype)] + list(extra_shapes),
        scratch_shapes=[pltpu.VMEM((tm, tn), F32)] + [pltpu.SemaphoreType.DMA(s) for s in sem_shapes],
        compiler_params=_cparams(("arbitrary",) * 3 if ride else ("parallel", "parallel", "arbitrary")),
    )(a, b, *srcs)
    return out if ride else out[0]


def _view(arr, off, width):
    assert off % width == 0, (off, width)
    return (arr, off // width, width)


def _rowwise(fn, rows, params, out_rows, out_accs=(), *, tile, name):
    rows = [r if isinstance(r, tuple) else (r, 0, r.shape[1]) for r in rows]
    S = rows[0][0].shape[0]
    T = min(tile, S)
    assert S % T == 0
    n_rows, n_par, n_out = len(rows), len(params), len(out_rows)
    into = [o[2] if len(o) == 3 else None for o in out_rows]
    carried = [t[0] for t in into if t is not None and t[0] is not None]

    def body(*refs):
        ins = [r[...] for r in refs[:n_rows + n_par]]
        outs = fn(*ins)
        out_refs = refs[n_rows + n_par + len(carried):]
        for o_ref, val in zip(out_refs[:n_out], outs[:n_out]):
            o_ref[...] = val.astype(o_ref.dtype)
        i = pl.program_id(0)
        for o_ref, val in zip(out_refs[n_out:], outs[n_out:]):
            @pl.when(i == 0)
            def _(o_ref=o_ref, val=val):
                o_ref[...] = val

            @pl.when(i > 0)
            def _(o_ref=o_ref, val=val):
                o_ref[...] += val

    in_specs = [pl.BlockSpec((T, w), functools.partial(lambda i, cb: (i, cb), cb=cb)) for _, cb, w in rows]
    in_specs += [pl.BlockSpec(p.shape, lambda i: (0, 0)) for p in params]
    in_specs += [pl.BlockSpec(memory_space=pl.ANY)] * len(carried)
    out_specs, out_shape, aliases = [], [], {}
    for k, (o, t) in enumerate(zip(out_rows, into)):
        w, dt = o[0], o[1]
        if t is None:
            out_specs.append(pl.BlockSpec((T, w), lambda i: (i, 0)))
            out_shape.append(jax.ShapeDtypeStruct((S, w), dt))
            continue
        buf, total, first = t
        assert first % w == 0
        out_specs.append(pl.BlockSpec((T, w), functools.partial(lambda i, cb: (i, cb), cb=first // w)))
        out_shape.append(jax.ShapeDtypeStruct((S, total), dt))
        if buf is not None:
            aliases[n_rows + n_par + len(aliases)] = k
    out_specs += [pl.BlockSpec(s, lambda i: (0, 0)) for s in out_accs]
    out_shape += [jax.ShapeDtypeStruct(s, F32) for s in out_accs]
    return pl.pallas_call(
        body, name=name, grid=(S // T,), in_specs=in_specs, out_specs=out_specs, out_shape=out_shape,
        input_output_aliases=aliases, compiler_params=_cparams(("arbitrary",)),
    )(*[r[0] for r in rows], *params, *carried)


def _mm_sel(x, sel):
    hi = x.astype(BF16)
    lo = (x - hi.astype(F32)).astype(BF16)
    d = lambda u: jnp.dot(u, sel, preferred_element_type=F32)
    return d(hi) + d(lo)


@jax.custom_vjp
def _sel(x, sel, sel_t):
    return _mm_sel(x, sel)


def _sel_fwd(x, sel, sel_t):
    return _mm_sel(x, sel), (sel, sel_t)


def _sel_bwd(res, ct):
    sel, sel_t = res
    return _mm_sel(ct, sel_t), jnp.zeros_like(sel), jnp.zeros_like(sel_t)


_sel.defvjp(_sel_fwd, _sel_bwd)


def _rms(x, g):
    return x * lax.rsqrt(jnp.mean(x * x, axis=-1, keepdims=True) + NORM_EPS) * g


def _sigmoid(x):
    return 1.0 / (1.0 + jnp.exp(-x))


def _silu(x):
    return x * _sigmoid(x)


def _softplus(x):
    return jnp.maximum(x, 0.0) + jnp.log(1.0 + jnp.exp(-jnp.abs(x)))


def _f_mla_norm(q_a, kv_a, qg, kvg):
    return _rms(q_a, qg), _rms(kv_a, kvg)


def _f_rope(hm, qraw, kr_in, cosx, sinx, rot, rot_t):
    def rope(t):
        return t * cosx + _sel(t, rot, rot_t) * sinx
    parts = []
    for h in range(hm):
        parts.append(qraw[:, h * QHEAD:h * QHEAD + NOPE])
        parts.append(rope(qraw[:, h * QHEAD + NOPE:(h + 1) * QHEAD]))
    return jnp.concatenate(parts, axis=1), rope(kr_in)


def _f_rwkv_pre(rw, k, tail, w0f, w0b, a0f, a0b, k_k, k_a, w2cat, a2cat, seg, seg_t):
    split = w2cat.shape[0]
    zw = jnp.dot(jnp.tanh(tail[:, :split]).astype(BF16), w2cat, preferred_element_type=F32)
    za = jnp.dot(tail[:, split:].astype(BF16), a2cat, preferred_element_type=F32)
    return _f_rwkv_core(rw, k, zw, za, w0f, w0b, a0f, a0b, k_k, k_a, seg, seg_t)


def _f_rwkv_core(rw, k, zw, za, w0f, w0b, a0f, a0b, k_k, k_a, seg, seg_t):
    lw_f = -jnp.exp(-_softplus(-(w0f + zw[:, :rw])) - 0.5)
    lw_b = -jnp.exp(-_softplus(-(w0b + zw[:, rw:])) - 0.5)
    a_f = _sigmoid(a0f + za[:, :rw])
    a_b = _sigmoid(a0b + za[:, rw:])
    kk = k * k_k
    nrm = jnp.sqrt(_sel(_sel(kk * kk, seg, seg_t), seg_t, seg))
    kk = kk / jnp.maximum(nrm, 1e-12)
    k_f = k * (1.0 + (a_f - 1.0) * k_a)
    k_b = k * (1.0 + (a_b - 1.0) * k_a)
    return lw_f, lw_b, k_f, k_b, -kk, kk * a_f, kk * a_b


def _f_post(hn, y_f, y_b, r, k_f, k_b, v, z_r, o_mla, z_m, gn_g, gn_b, r_k, seg, seg_t):
    segsum = lambda t: _sel(_sel(t, seg, seg_t), seg_t, seg)
    y = y_f + y_b
    mu = segsum(y) * (1.0 / hn)
    yc = y - mu
    var = segsum(yc * yc) * (1.0 / hn)
    yn = yc * lax.rsqrt(var + GN_EPS) * gn_g + gn_b
    bonus = segsum(r * (k_f + k_b) * r_k) * v
    return o_mla * _silu(z_m), (yn + bonus) * _silu(z_r)


def _f_merge(u_m, u_r, g_m, g_r):
    return _sigmoid(g_m) * u_m + _sigmoid(g_r) * u_r


_NN = ((2,), (1,))
_NT = ((2,), (2,))
_TN = ((1,), (1,))

_SCAN_PASSES = {"cum": 2, "gram": 3, "solve": 1, "apply": 1, "state": 1}


def _hdot_raw(passes, x, y, dims):
    dn = (dims, ((0,), (0,)))
    d = lambda p, q: lax.dot_general(p, q, dn, preferred_element_type=F32)
    xh = x.astype(BF16)
    yh = y.astype(BF16)
    if passes == 1:
        return d(xh, yh)
    yl = (y - yh.astype(F32)).astype(BF16)
    if passes == 2:
        axis = 1 if dims == _NT else 2
        width = y.shape[axis]
        both = d(xh, jnp.concatenate([yh, yl], axis=axis))
        return both[:, :, :width] + both[:, :, width:]
    xl = (x - xh.astype(F32)).astype(BF16)
    if dims == _TN:
        return d(xh, yh) + d(xh, yl) + d(xl, yh)
    rows = x.shape[1]
    both = d(jnp.concatenate([xh, xl], axis=1), yh)
    return both[:, :rows] + both[:, rows:] + d(xh, yl)


@functools.partial(jax.custom_vjp, nondiff_argnums=(2, 3))
def _hdot_p(x, y, dims, passes):
    return _hdot_raw(passes, x, y, dims)


def _hdot_fwd(x, y, dims, passes):
    return _hdot_raw(passes, x, y, dims), (x, y)


def _hdot_bwd(dims, passes, res, ct):
    x, y = res
    if dims == _NN:
        return _hdot_raw(passes, ct, y, _NT), _hdot_raw(passes, x, ct, _TN)
    if dims == _NT:
        return _hdot_raw(passes, ct, y, _NN), _hdot_raw(passes, ct, x, _TN)
    return _hdot_raw(passes, y, ct, _NT), _hdot_raw(passes, x, ct, _NN)


_hdot_p.defvjp(_hdot_fwd, _hdot_bwd)


def _hdot(x, y, dims, kind):
    return _hdot_p(x, y, dims, _SCAN_PASSES[kind])


def _tri_solve(n_mat, x, length):
    row = lax.broadcasted_iota(jnp.int32, (length, length), 0)
    col = lax.broadcasted_iota(jnp.int32, (length, length), 1)
    eye = (row == col).astype(F32)[None]
    diag_blk = ((row // SUB) == (col // SUB))[None]
    nd = jnp.where(diag_blk, n_mat, 0.0)
    no = n_mat - nd
    dinv = eye + nd
    p = _hdot(nd, nd, _NN, "solve")
    for k in range(int(math.log2(SUB)) - 1):
        if k == int(math.log2(SUB)) - 2:
            dinv = dinv + _hdot(dinv, p, _NN, "solve")
        else:
            both = _hdot(jnp.concatenate([dinv, p], axis=1), p, _NN, "solve")
            dinv, p = dinv + both[:, :length], both[:, length:]
    both = _hdot(dinv, jnp.concatenate([x, no], axis=2), _NN, "solve")
    u, q = both[:, :, :x.shape[2]], both[:, :, x.shape[2]:]
    width = x.shape[2]
    for level in range(int(math.log2(length // SUB))):
        if level == int(math.log2(length // SUB)) - 1:
            u = u + _hdot(q, u, _NN, "solve")
        else:
            both = _hdot(q, jnp.concatenate([u, q], axis=2), _NN, "solve")
            u, q = u + both[:, :, :width], both[:, :, width:]
    return u


def _rwkv_chunk(rev, s0, r, lw, k, v, a, b):
    pairs, length, width = r.shape
    hn = width // 2
    row = lax.broadcasted_iota(jnp.int32, (length, length), 0)
    col = lax.broadcasted_iota(jnp.int32, (length, length), 1)
    row2 = lax.broadcasted_iota(jnp.int32, (length, 2 * length), 0)
    col2 = lax.broadcasted_iota(jnp.int32, (length, 2 * length), 1)
    col2 = jnp.where(col2 >= length, col2 - length, col2)
    if rev is None:
        half = pairs // 2
        back = lax.broadcasted_iota(jnp.int32, (pairs, length, length), 0) >= half
        idx2 = lax.broadcasted_iota(jnp.int32, (2 * pairs, length, 2 * length), 0)
        back2 = ((idx2 >= half) & (idx2 < pairs)) | (idx2 >= pairs + half)
        ahead = jnp.where(back, (col - row)[None], (row - col)[None])
        ahead2 = jnp.where(back2, (col2 - row2)[None], (row2 - col2)[None])
        incl, strict2, incl2 = ahead >= 0, ahead2 > 0, ahead2 >= 0
    else:
        incl = ((row <= col) if rev else (row >= col))[None]
        strict2 = ((row2 < col2) if rev else (row2 > col2))[None]
        incl2 = ((row2 <= col2) if rev else (row2 >= col2))[None]
    lane = lax.broadcasted_iota(jnp.int32, (1, 1, width), 2)
    first = lane < hn
    head_mask = jnp.concatenate([jnp.broadcast_to(first.astype(F32), (pairs, 1, width)),
                                 jnp.broadcast_to(1.0 - first.astype(F32), (pairs, 1, width))], axis=0)
    twice = lambda t: jnp.concatenate([t, t], axis=0)
    pick = lambda t: jnp.where(first, t[:pairs], t[pairs:])

    t_incl = jnp.broadcast_to(incl.astype(F32), (pairs, length, length))
    cum = _hdot(t_incl, lw, _NN, "cum")
    g = jnp.exp(cum)
    g_inv = jnp.exp(-cum)
    at = a * jnp.exp(cum - lw)
    rt = r * g
    bt = b * g_inv
    kt = k * g_inv
    by_pair = lambda t: jnp.concatenate([t[:pairs], t[pairs:]], axis=1)
    lhs = jnp.concatenate([twice(at) * head_mask, twice(rt) * head_mask], axis=1)
    gram = _hdot(by_pair(lhs), jnp.concatenate([bt, kt], axis=1), _NT, "gram")
    gram = jnp.concatenate([gram[:, :2 * length], gram[:, 2 * length:]], axis=0)
    top = jnp.where(strict2, gram[:, :length], 0.0)
    bot = jnp.where(incl2, gram[:, length:], 0.0)
    pick_rows = lambda t: jnp.where(first, t[:, :length], t[:, length:])
    from_state = _hdot(jnp.concatenate([at, rt], axis=1), s0, _NT, "apply")
    x = from_state[:, :length] + pick_rows(
        _hdot(by_pair(top), jnp.concatenate([jnp.zeros_like(v), v], axis=1), _NN, "apply"))
    u = pick(_tri_solve(top[:, :, :length], twice(x), length))
    y = from_state[:, length:] + pick_rows(_hdot(by_pair(bot), jnp.concatenate([u, v], axis=1), _NN, "apply"))
    g_last = jnp.exp(jnp.sum(lw, axis=1, keepdims=True))
    ri = lax.broadcasted_iota(jnp.int32, (width, width), 0)
    ci = lax.broadcasted_iota(jnp.int32, (width, width), 1)
    same_head = ((ri < hn) == (ci < hn))[None]
    upd = _hdot(jnp.concatenate([u, v], axis=1), jnp.concatenate([bt, kt], axis=1), _TN, "state")
    s1 = (s0 + jnp.where(same_head, upd, 0.0)) * g_last
    return y, s1


def _split_pairs(x):
    return jnp.stack([x[:, p * LANES:(p + 1) * LANES] for p in range(x.shape[1] // LANES)])


def _merge_pairs(x):
    return jnp.concatenate([x[p] for p in range(x.shape[0])], axis=1)


def _scan_specs(views, rw, nc, rev):
    cidx = (lambda c: nc - 1 - c) if rev else (lambda c: c)
    seqs = [pl.BlockSpec((CHUNK, rw), functools.partial(lambda c, cb: (cidx(c), cb), cb=cb)) for _, cb, _ in views]
    plain = pl.BlockSpec((CHUNK, rw), lambda c: (cidx(c), 0))
    st = pl.BlockSpec((1, rw // LANES, LANES, LANES), lambda c: (cidx(c), 0, 0, 0))
    return seqs, plain, st


def _as_views(arrs, rw):
    return [t if isinstance(t, tuple) else (t, 0, rw) for t in arrs]


def _rwkv_scan_fwd(ops_f, ops_b, rw, *, name):
    S = _as_views(ops_f, rw)[0][0].shape[0]
    nc, pairs = S // CHUNK, rw // LANES
    in_specs, out_specs, arrays = [], [], []
    for rev, ops in ((False, ops_f), (True, ops_b)):
        views = _as_views(ops, rw)
        seqs, plain, st = _scan_specs(views, rw, nc, rev)
        in_specs += seqs
        out_specs += [plain, st]
        arrays += [t[0] for t in views]

    def both(refs_f, refs_b):
        return [jnp.concatenate([_split_pairs(f[...]), _split_pairs(b[...])], axis=0) for f, b in zip(refs_f, refs_b)]

    def body(*refs):
        (y_f, st_f, y_b, st_b), s_ref = refs[12:16], refs[16]

        @pl.when(pl.program_id(0) == 0)
        def _():
            s_ref[...] = jnp.zeros_like(s_ref)

        s0 = s_ref[...]
        st_f[0] = s0[:pairs]
        st_b[0] = s0[pairs:]
        y, s1 = _rwkv_chunk(None, s0, *both(refs[:6], refs[6:12]))
        y_f[...] = _merge_pairs(y[:pairs])
        y_b[...] = _merge_pairs(y[pairs:])
        s_ref[...] = s1

    return pl.pallas_call(
        body, name=name, grid=(nc,), in_specs=in_specs, out_specs=out_specs,
        out_shape=[jax.ShapeDtypeStruct((S, rw), F32), jax.ShapeDtypeStruct((nc, pairs, LANES, LANES), F32)] * 2,
        scratch_shapes=[pltpu.VMEM((2 * pairs, LANES, LANES), F32)],
        compiler_params=_cparams(("arbitrary",)),
    )(*arrays)


def _rwkv_scan_bwd(ops_f, ops_b, states_f, states_b, dy, rw, *, name):
    S = dy.shape[0]
    nc, pairs = S // CHUNK, rw // LANES
    in_specs, arrays = [], []
    for rev, ops, states in ((False, ops_f, states_f), (True, ops_b, states_b)):
        views = _as_views(list(ops) + [dy], rw)
        seqs, plain, st = _scan_specs(views, rw, nc, not rev)
        in_specs += seqs + [st]
        arrays += [t[0] for t in views] + [states]
    out_specs = []
    for rev in (False, True):
        out_specs += [_scan_specs([], rw, nc, not rev)[1]] * 6

    def both(refs_f, refs_b):
        return [jnp.concatenate([_split_pairs(f[...]), _split_pairs(b[...])], axis=0) for f, b in zip(refs_f, refs_b)]

    def body(*refs):
        ds_ref = refs[28]

        @pl.when(pl.program_id(0) == 0)
        def _():
            ds_ref[...] = jnp.zeros_like(ds_ref)

        s0 = jnp.concatenate([refs[7][0], refs[15][0]], axis=0)
        _, vjp = jax.vjp(functools.partial(_rwkv_chunk, None), s0, *both(refs[:6], refs[8:14]))
        (dy,) = both(refs[6:7], refs[14:15])
        grads = vjp((dy, ds_ref[...]))
        ds_ref[...] = grads[0]
        for o_f, o_b, gval in zip(refs[16:22], refs[22:28], grads[1:]):
            o_f[...] = _merge_pairs(gval[:pairs])
            o_b[...] = _merge_pairs(gval[pairs:])

    return pl.pallas_call(
        body, name=name, grid=(nc,), in_specs=in_specs, out_specs=out_specs,
        out_shape=[jax.ShapeDtypeStruct((S, rw), F32)] * 12,
        scratch_shapes=[pltpu.VMEM((2 * pairs, LANES, LANES), F32)],
        compiler_params=_cparams(("arbitrary",)),
    )(*arrays)


def _shift_lerp(x_view, mu, d=None, into=None, *, name):
    arr, off, width = x_view
    S = arr.shape[0]
    cb = _pick(width, 512)
    assert off % cb == 0

    def cshift(t):
        rows = lax.broadcasted_iota(jnp.int32, t.shape, 0)
        prev = jnp.where(rows == 0, 0.0, pltpu.roll(t, 1, 0))
        nxt = jnp.where(rows == S - 1, 0.0, pltpu.roll(t, S - 1, 0))
        return 0.5 * (prev + nxt)

    def fwd_body(x_ref, mu_ref, o_ref):
        x = x_ref[...]
        o_ref[...] = x + mu_ref[...] * (cshift(x) - x)

    def bwd_body(x_ref, mu_ref, d_ref, _, dx_ref, dmu_ref):
        x, m, dd = x_ref[...], mu_ref[...], d_ref[...]
        gm = m * dd
        dx_ref[...] = (dd - gm + cshift(gm)).astype(dx_ref.dtype)
        dmu_ref[...] = jnp.sum(dd * (cshift(x) - x), axis=0, keepdims=True)

    x_spec = pl.BlockSpec((S, cb), lambda j: (0, off // cb + j))
    blk = pl.BlockSpec((S, cb), lambda j: (0, j))
    vec = pl.BlockSpec((1, cb), lambda j: (0, j))
    if d is None:
        return pl.pallas_call(
            fwd_body, name=name, grid=(width // cb,), in_specs=[x_spec, vec], out_specs=blk,
            out_shape=jax.ShapeDtypeStruct((S, width), F32), compiler_params=_cparams(("parallel",)),
        )(arr, mu)
    buf, first = into
    assert first % cb == 0
    return pl.pallas_call(
        bwd_body, name=name, grid=(width // cb,),
        in_specs=[x_spec, vec, blk, pl.BlockSpec(memory_space=pl.ANY)],
        out_specs=[pl.BlockSpec((S, cb), lambda j: (0, first // cb + j)), vec],
        out_shape=[jax.ShapeDtypeStruct(buf.shape, buf.dtype), jax.ShapeDtypeStruct((1, width), F32)],
        input_output_aliases={3: 0}, compiler_params=_cparams(("parallel",)),
    )(arr, mu, d, buf)


def _attention_fwd(qfull, kv, kr, hm, scale, *, tq, name):
    S = qfull.shape[0]
    nt = (((1,), (1,)), ((), ()))

    def body(q_ref, kn_ref, kr_ref, v_ref, o_ref, lse_ref, k_scr):
        _head_keys(kn_ref, kr_ref, k_scr)
        s = lax.dot_general(q_ref[...], k_scr[...], nt, preferred_element_type=F32)
        m = jnp.max(s, axis=-1, keepdims=True)
        p = jnp.exp((s - m) * scale)
        l = jnp.sum(p, axis=-1, keepdims=True)
        o_ref[...] = jnp.dot(p.astype(BF16), v_ref[...], preferred_element_type=F32) * (1.0 / l)
        lse_ref[...] = jnp.broadcast_to(m * scale + jnp.log(l), lse_ref.shape)

    oblk = pl.BlockSpec((tq, VDIM), lambda h, i: (i, h))
    return pl.pallas_call(
        body, name=name, grid=(hm, S // tq),
        in_specs=[pl.BlockSpec((tq, QHEAD), lambda h, i: (i, h)),
                  pl.BlockSpec((S, NOPE), lambda h, i: (0, h)),
                  pl.BlockSpec((S, LANES), lambda h, i: (0, 0)),
                  pl.BlockSpec((S, VDIM), lambda h, i: (0, hm + h))],
        out_specs=[oblk, oblk],
        out_shape=[jax.ShapeDtypeStruct((S, hm * VDIM), F32)] * 2,
        scratch_shapes=[pltpu.VMEM((S, QHEAD), BF16)],
        compiler_params=_cparams(("parallel", "arbitrary")),
    )(qfull, kv, kr, kv)


def _head_keys(kn_ref, kr_ref, k_scr):
    @pl.when(pl.program_id(1) == 0)
    def _():
        k_scr[:, :NOPE] = kn_ref[...]
        k_scr[:, NOPE:] = kr_ref[...]


def _attention_bwd(qfull, kv, kr, o, lse, d_o, hm, scale, *, tq, name):
    S = qfull.shape[0]
    tq = min(tq, S)
    nq = S // tq
    tn = (((0,), (0,)), ((), ()))
    nt = (((1,), (1,)), ((), ()))

    def body(q_ref, kn_ref, kr_ref, v_ref, o_ref, lse_ref, do_ref, dq_ref, dk_ref, dv_ref, k_scr):
        _head_keys(kn_ref, kr_ref, k_scr)
        s = lax.dot_general(q_ref[...], k_scr[...], nt, preferred_element_type=F32)
        p = jnp.exp(s * scale - lse_ref[:, 0:1])
        d_out = do_ref[...]
        delta = jnp.sum(d_out * o_ref[...], axis=-1, keepdims=True)
        d_out = d_out.astype(BF16)
        dp = lax.dot_general(d_out, v_ref[...], nt, preferred_element_type=F32)
        ds = (p * (dp - delta)).astype(BF16)
        dq_ref[...] = jnp.dot(ds, k_scr[...], preferred_element_type=F32) * scale
        dv = lax.dot_general(p.astype(BF16), d_out, tn, preferred_element_type=F32)
        dk = lax.dot_general(ds, q_ref[...], tn, preferred_element_type=F32)
        i = pl.program_id(1)
        for ref, val in ((dk_ref, dk), (dv_ref, dv)):
            @pl.when(i == 0)
            def _(ref=ref, val=val):
                ref[...] = val

            @pl.when(i > 0)
            def _(ref=ref, val=val):
                ref[...] += val

        @pl.when(i == nq - 1)
        def _():
            dk_ref[...] = dk_ref[...] * scale

    qblk = pl.BlockSpec((tq, QHEAD), lambda h, i: (i, h))
    oblk = pl.BlockSpec((tq, VDIM), lambda h, i: (i, h))
    return pl.pallas_call(
        body, name=name, grid=(hm, nq),
        in_specs=[qblk,
                  pl.BlockSpec((S, NOPE), lambda h, i: (0, h)),
                  pl.BlockSpec((S, LANES), lambda h, i: (0, 0)),
                  pl.BlockSpec((S, VDIM), lambda h, i: (0, hm + h)),
                  oblk, oblk, oblk],
        out_specs=[qblk, pl.BlockSpec((S, QHEAD), lambda h, i: (0, h)), pl.BlockSpec((S, VDIM), lambda h, i: (0, h))],
        out_shape=[jax.ShapeDtypeStruct((S, hm * QHEAD), F32), jax.ShapeDtypeStruct((S, hm * QHEAD), F32),
                   jax.ShapeDtypeStruct((S, hm * VDIM), F32)],
        scratch_shapes=[pltpu.VMEM((S, QHEAD), BF16)],
        compiler_params=_cparams(("parallel", "arbitrary")),
    )(qfull, kv, kr, kv, o, lse, d_o)


def _layout(D, MW, RW, TAIL, QR, KVR):
    names = ["gate_m", "gate_r", "z_m", "z_r", "q_a", "kv_a", "r", "k", "v", "tail"]
    widths = [D, D, MW, RW, QR, KVR, RW, RW, RW, TAIL]
    offs, o = {}, 0
    for nme, w in zip(names, widths):
        assert o % w == 0, (nme, o, w)
        offs[nme] = (o, w)
        o += w
    return offs, o


def _local_grads(x, target, W, dims, exchange=None):
    S, D = x.shape
    hm, hr, hn, rank = dims["hm"], dims["hr"], dims["hn"], dims["rank"]
    MW, RW = hm * VDIM, hr * hn
    TAIL = dims["TAIL"]
    QR, KVR = W["mla_q_norm"].shape[1], W["mla_kv_norm"].shape[1]
    lay, d_in = _layout(D, MW, RW, TAIL, QR, KVR)
    T = 256
    scale = (NOPE + ROPE) ** -0.5
    col = lambda arr, nme: _view(arr, *lay[nme])

    pos = jnp.arange(S, dtype=F32)
    inv_freq = jnp.power(ROPE_THETA, -jnp.arange(0, ROPE, 2, dtype=F32) / ROPE)
    ang = pos[:, None] * inv_freq[None, :]
    zpad = jnp.zeros((S, LANES - ROPE), F32)
    cosx = jnp.concatenate([jnp.cos(ang), jnp.cos(ang), zpad], axis=1)
    sinx = jnp.concatenate([jnp.sin(ang), jnp.sin(ang), zpad], axis=1)
    ri, ci = jnp.arange(LANES)[:, None], jnp.arange(LANES)[None, :]
    half = ROPE // 2
    rot = (jnp.where((ri == ci - half) & (ci >= half) & (ci < ROPE), 1.0, 0.0)
           - jnp.where((ri == ci + half) & (ci < half), 1.0, 0.0)).astype(BF16)
    rot_t = rot.T
    seg = (jnp.arange(RW)[:, None] // hn == jnp.arange(LANES)[None, :]).astype(BF16)
    seg_t = seg.T

    (h,) = _rowwise(lambda xb, g: (_rms(xb, g),), [x], [W["g_pre"]], [(D, BF16)], tile=2 * T, name="pre_norm")
    if exchange is None:
        proj = _mm(h, W["w_in_t"], tb=True, name="in_proj")
    else:
        proj, *slabs = _mm(h, W["w_in_t"], tb=True, ride=_gather_plan(exchange[0]), name="in_proj")
        W = {**W, **_prepare_rest(dict(zip(_MATS[1:], slabs)), dims)}

    qn, kvn = _rowwise(_f_mla_norm, [col(proj, "q_a"), col(proj, "kv_a")], [W["mla_q_norm"], W["mla_kv_norm"]],
                       [(QR, BF16), (KVR, BF16)], tile=2 * T, name="mla_norm")
    qraw = _mm(qn, W["wq_b_t"], tb=True, name="q_up")
    kv = _mm(kvn, W["wkv_b"], out_dtype=BF16, name="kv_up")
    kr_view = _view(proj, lay["tail"][0], LANES)
    qfull, kr = _rowwise(functools.partial(_f_rope, hm), [qraw, kr_view, cosx, sinx], [rot, rot_t],
                         [(hm * QHEAD, BF16), (LANES, BF16)], tile=2 * T, name="rope")
    o_mla, lse = _attention_fwd(qfull, kv, kr, hm, scale, tq=T, name="attn_fwd")

    shift_view = (proj, lay["r"][0], 3 * RW + TAIL)
    rl = _shift_lerp(shift_view, W["mu"], name="shift_fwd")
    rl_r, rl_k, rl_v = _view(rl, 0, RW), _view(rl, RW, RW), _view(rl, 2 * RW, RW)
    rl_tail = _view(rl, 3 * RW, TAIL)
    pre_params = [W["w0_f"], W["w0_b"], W["a0_f"], W["a0_b"], W["k_k"], W["k_a"], W["w2cat"], W["a2cat"], seg, seg_t]
    pre_fn = functools.partial(_f_rwkv_pre, RW)
    lw_f, lw_b, k_f, k_b, a_n, b_f, b_b = _rowwise(pre_fn, [rl_k, rl_tail], pre_params, [(RW, F32)] * 7, tile=T,
                                                    name="rwkv_pre")
    ops_f = (rl_r, lw_f, k_f, rl_v, a_n, b_f)
    ops_b = (rl_r, lw_b, k_b, rl_v, a_n, b_b)
    y_f, st_f, y_b, st_b = _rwkv_scan_fwd(ops_f, ops_b, RW, name="scan_fwd")

    post_fn = functools.partial(_f_post, hn)
    post_rows = [y_f, y_b, rl_r, k_f, k_b, rl_v, col(proj, "z_r"), o_mla, col(proj, "z_m")]
    post_params = [W["gn_g"], W["gn_b"], W["r_k"], seg, seg_t]
    ymg, yrg = _rowwise(post_fn, post_rows, post_params, [(MW, BF16), (RW, BF16)], tile=T, name="post")
    u_m = _mm(ymg, W["w_br_mla"], name="br_mla")
    u_r = _mm(yrg, W["w_br_rwkv"], name="br_rwkv")
    merge_rows = [u_m, u_r, col(proj, "gate_m"), col(proj, "gate_r")]
    (merged,) = _rowwise(lambda *t: (_f_merge(*t),), merge_rows, [], [(D, BF16)], tile=2 * T, name="merge")
    out = _mm(merged, W["w_out"], name="out_proj")

    def head(ob, xb, tb, g):
        yn, vjp = jax.vjp(_rms, ob, g)
        err = xb + yn - tb
        dy = err * (1.0 / D)
        d_ob, d_g = vjp(dy)
        loss = jnp.broadcast_to(0.5 * jnp.sum(err * err) * (1.0 / D), (1, LANES))
        return dy, d_ob, loss, d_g

    dy, d_out, loss, g_g_post = _rowwise(head, [out, x, target], [W["g_post"]], [(D, F32), (D, BF16)],
                                         [(1, LANES), (1, D)], tile=2 * T, name="head")
    d_merged = _mm(d_out, W["w_out"], tb=True, name="d_merged")
    g_w_out = _mm(merged, d_out, ta=True, out_dtype=BF16, name="g_w_out")

    def merge_bwd(u_m_b, u_r_b, g_m_b, g_r_b, dm):
        _, vjp = jax.vjp(_f_merge, u_m_b, u_r_b, g_m_b, g_r_b)
        du_m, du_r, dg_m, dg_r = vjp(dm)
        return du_m, du_r, jnp.concatenate([dg_m, dg_r], axis=1)

    d_u_m, d_u_r, d_proj = _rowwise(merge_bwd, merge_rows + [d_merged], [],
                                    [(D, BF16), (D, BF16), (2 * D, BF16, (None, d_in, lay["gate_m"][0]))], tile=T,
                                    name="merge_bwd")
    d_ymg = _mm(d_u_m, W["w_br_mla"], tb=True, name="d_ymg")
    d_yrg = _mm(d_u_r, W["w_br_rwkv"], tb=True, name="d_yrg")
    g_w_br_mla = _mm(ymg, d_u_m, ta=True, out_dtype=BF16, name="g_w_br_mla")
    g_w_br_rwkv = _mm(yrg, d_u_r, ta=True, out_dtype=BF16, name="g_w_br_rwkv")

    def post_bwd(*args):
        nr = len(post_rows)
        prim, dm, dr = args[:nr] + args[nr + 2:], args[nr], args[nr + 1]
        _, vjp = jax.vjp(post_fn, *prim)
        g = vjp((dm, dr))
        return g[0], g[2], g[3], g[5], g[7], jnp.concatenate([g[8], g[6]], axis=1), g[9], g[10], g[11]

    (d_y, d_r_bonus, d_k_bonus, d_v_bonus, d_o, d_proj, g_gn_g, g_gn_b, g_r_k) = _rowwise(
        post_bwd, post_rows + [d_ymg, d_yrg], post_params,
        [(RW, F32), (RW, F32), (RW, F32), (RW, F32), (MW, F32), (MW + RW, BF16, (d_proj, d_in, lay["z_m"][0]))],
        [(1, RW)] * 3, tile=T // 2, name="post_bwd")

    dscan = _rwkv_scan_bwd(ops_f, ops_b, st_f, st_b, d_y, RW, name="scan_bwd")
    dsc = {"f": dscan[:6], "b": dscan[6:]}

    d_q_att, d_k_att, d_v_att = _attention_bwd(qfull, kv, kr, o_mla, lse, d_o, hm, scale, tq=2 * T, name="attn_bwd")

    def rope_bwd(qraw_b, kr_in, cos_b, sin_b, dq_b, dk_b, dv_b, rot_b, rot_t_b):
        _, vjp = jax.vjp(lambda q_, k_: _f_rope(hm, q_, k_, cos_b, sin_b, rot_b, rot_t_b), qraw_b, kr_in)
        dkn = jnp.concatenate([dk_b[:, hh * QHEAD:hh * QHEAD + NOPE] for hh in range(hm)], axis=1)
        dkr = dk_b[:, NOPE:QHEAD]
        for hh in range(1, hm):
            dkr = dkr + dk_b[:, hh * QHEAD + NOPE:(hh + 1) * QHEAD]
        d_qraw, d_kr_in = vjp((dq_b, dkr))
        return d_qraw, jnp.concatenate([dkn, dv_b], axis=1), d_kr_in

    d_qraw, d_kv, d_kr_in = _rowwise(rope_bwd, [qraw, kr_view, cosx, sinx, d_q_att, d_k_att, d_v_att],
                                     [rot, rot_t], [(hm * QHEAD, BF16), (2 * MW, BF16), (LANES, F32)], tile=T,
                                     name="rope_bwd")
    d_qnorm = _mm(d_qraw, W["wq_b_t"], name="d_qn")
    d_kvnorm = _mm(d_kv, W["wkv_b"], tb=True, name="d_kvn")
    g_wq_b = _mm(d_qraw, qn, ta=True, out_dtype=BF16, name="g_wq_b")
    g_wkv_b = _mm(kvn, d_kv, ta=True, out_dtype=BF16, name="g_wkv_b")

    def mla_norm_bwd(q_a, kv_a, qg, kvg, dq, dk):
        _, vjp = jax.vjp(_f_mla_norm, q_a, kv_a, qg, kvg)
        d_q_a, d_kv_a, d_qg, d_kvg = vjp((dq, dk))
        return jnp.concatenate([d_q_a, d_kv_a], axis=1), d_qg, d_kvg

    d_proj, g_q_norm, g_kv_norm = _rowwise(
        lambda q_a, kv_a, dq, dk, qg, kvg: mla_norm_bwd(q_a, kv_a, qg, kvg, dq, dk),
        [col(proj, "q_a"), col(proj, "kv_a"), d_qnorm, d_kvnorm], [W["mla_q_norm"], W["mla_kv_norm"]],
        [(QR + KVR, BF16, (d_proj, d_in, lay["q_a"][0]))], [(1, QR), (1, KVR)], tile=2 * T, name="mla_norm_bwd")

    def pre_bwd(k_b_, tail_b, dlwf, dlwb, dkf, dkb, dkbon, daf, dab, dbf, dbb, drf, drb, drbon, dvf, dvb, dvbon,
                dkr, *params):
        w2, a2 = params[6], params[7]
        nt, tn = (((1,), (1,)), ((), ())), (((0,), (0,)), ((), ()))
        split = w2.shape[0]
        th = jnp.tanh(tail_b[:, :split])
        th_b, tail_h = th.astype(BF16), tail_b[:, split:].astype(BF16)
        zw = jnp.dot(th_b, w2, preferred_element_type=F32)
        za = jnp.dot(tail_h, a2, preferred_element_type=F32)
        _, vjp = jax.vjp(functools.partial(_f_rwkv_core, RW), k_b_, zw, za, *params[:6], params[8], params[9])
        g = vjp((dlwf, dlwb, dkf + dkbon, dkb + dkbon, daf + dab, dbf, dbb))
        d_zw, d_za = g[1].astype(BF16), g[2].astype(BF16)
        d_tail = (jnp.concatenate([lax.dot_general(d_zw, w2, nt, preferred_element_type=F32) * (1.0 - th * th),
                                   lax.dot_general(d_za, a2, nt, preferred_element_type=F32)], axis=1)
                  + jnp.concatenate([dkr, jnp.zeros((dkr.shape[0], TAIL - LANES), F32)], axis=1))
        g_w2 = lax.dot_general(th_b, d_zw, tn, preferred_element_type=F32)
        g_a2 = lax.dot_general(tail_h, d_za, tn, preferred_element_type=F32)
        d_rl = jnp.concatenate([drf + drb + drbon, g[0], dvf + dvb + dvbon, d_tail], axis=1)
        return (d_rl,) + tuple(g[3:9]) + (g_w2, g_a2)

    f_, b_ = dsc["f"], dsc["b"]
    pre_bwd_rows = [rl_k, rl_tail, f_[1], b_[1], f_[2], b_[2], d_k_bonus, f_[4], b_[4], f_[5], b_[5],
                    f_[0], b_[0], d_r_bonus, f_[3], b_[3], d_v_bonus, d_kr_in]
    (d_rl, g_w0_f, g_w0_b, g_a0_f, g_a0_b, g_k_k, g_k_a, g_w2cat, g_a2cat) = _rowwise(
        pre_bwd, pre_bwd_rows, pre_params, [(3 * RW + TAIL, F32)],
        [(1, RW)] * 6 + [W["w2cat"].shape, W["a2cat"].shape], tile=T // 2, name="rwkv_pre_bwd")
    d_proj, g_mu = _shift_lerp(shift_view, W["mu"], d_rl, (d_proj, lay["r"][0]), name="shift_bwd")
    small = dict(wq_b=g_wq_b, wkv_b=g_wkv_b, w2cat=g_w2cat, a2cat=g_a2cat, w_br_mla=g_w_br_mla,
                 w_br_rwkv=g_w_br_rwkv, w_out=g_w_out)
    if exchange is None:
        received = None
        g_w_in = _mm(d_proj, h, ta=True, out_dtype=BF16, tn_cap=1024, name="g_w_in")
        d_h = _mm(d_proj, W["w_in_t"], tn_cap=1024, name="d_h")
    else:
        slabs = _restore_rest(small, dims)
        slabs = [slabs[n] for n in _MATS[1:]]
        g_w_in, *got = _mm(d_proj, h, ta=True, out_dtype=BF16, tn_cap=1024, ride=_sibling_swap_plan(slabs),
                           name="g_w_in")
        sums = [_pair_add(exchange[1], s, t, name="pair_add_" + n) for n, s, t in zip(_MATS[1:], slabs, got)]
        g_w_in = _restore_w_in(g_w_in, dims)
        d_h, *received = _mm(d_proj, W["w_in_t"], tn_cap=1024, name="d_h",
                             ride=_join_plans(_chip_exchange_plan(sums), _sibling_swap_plan([g_w_in])))
        small = {}

    def pre_norm_bwd(xb, dyb, dhb, g):
        _, vjp = jax.vjp(_rms, xb, g)
        dx, dg = vjp(dhb)
        return dyb + dx, dg

    grad_x, g_g_pre = _rowwise(pre_norm_bwd, [x, dy, d_h], [W["g_pre"]], [(D, F32)], [(1, D)], tile=2 * T,
                               name="pre_norm_bwd")

    grads = dict(g_pre=g_g_pre, w_in=g_w_in, mla_q_norm=g_q_norm, mla_kv_norm=g_kv_norm, mu=g_mu, w0_f=g_w0_f,
                 w0_b=g_w0_b, a0_f=g_a0_f, a0_b=g_a0_b, k_k=g_k_k, k_a=g_k_a, r_k=g_r_k, gn_g=g_gn_g, gn_b=g_gn_b,
                 g_post=g_g_post, **small)
    return loss[0, 0], grad_x, grads, received


_MATS = ["w_in", "mla_wq_b", "mla_wkv_b", "rwkv_w2_f", "rwkv_w2_b", "rwkv_a2_f", "rwkv_a2_b", "w_br_mla",
         "w_br_rwkv", "w_out"]
_ROW_SHARDED = ("w_out",)
_TRANSPOSED = ("w_in", "mla_wq_b")
_VECS = ["g_pre", "mla_q_norm", "mla_kv_norm", "rwkv_mu", "rwkv_w0_f", "rwkv_w0_b", "rwkv_a0_f", "rwkv_a0_b",
         "rwkv_k_k", "rwkv_k_a", "rwkv_r_k", "rwkv_gn_g", "rwkv_gn_b", "g_post"]
_WEIGHTS = ["g_pre", "w_in", "mla_q_norm", "mla_wq_b", "mla_kv_norm", "mla_wkv_b", "rwkv_mu", "rwkv_w0_f",
            "rwkv_w2_f", "rwkv_w0_b", "rwkv_w2_b", "rwkv_a0_f", "rwkv_a2_f", "rwkv_a0_b", "rwkv_a2_b", "rwkv_k_k",
            "rwkv_k_a", "rwkv_r_k", "rwkv_gn_g", "rwkv_gn_b", "w_br_mla", "w_br_rwkv", "w_out", "g_post"]

def _direct_gather_plan(src):
    def phases(src_refs, out_refs, sem_refs):
        (src_ref,), (out_ref,), sems, local_sem = src_refs, out_refs, sem_refs[:2], sem_refs[2]
        x, y, c = lax.axis_index("x"), lax.axis_index("y"), lax.axis_index("c")
        me = 4 * x + 2 * y + c
        flip = lambda v, bit: (1 - v) if bit else v
        peers = [(flip(x, d & 4), flip(y, d & 2), flip(c, d & 1)) for d in range(1, N_DEV)]
        own = lambda: pltpu.make_async_copy(src_ref, out_ref.at[me], local_sem)
        send = lambda d: _remote(src_ref, out_ref.at[me], sems, d, peers[d])

        def first():
            own().start()
            for d in range(N_DEV - 1):
                send(d).start()

        def last():
            for d, (px, py, pc) in enumerate(peers):
                blk = out_ref.at[4 * px + 2 * py + pc]
                _remote(blk, blk, sems, d, (x, y, c)).wait_recv()
            for d in range(N_DEV - 1):
                send(d).wait_send()
            own().wait()

        return first, (lambda: None), last

    return [src], [jax.ShapeDtypeStruct((N_DEV,) + src.shape, src.dtype)], [(N_DEV - 1,), (N_DEV - 1,), ()], phases


def _remote(src, dst, sems, key, to):
    send_sems, recv_sems = sems
    return pltpu.make_async_remote_copy(src_ref=src, dst_ref=dst, send_sem=send_sems.at[key], recv_sem=recv_sems.at[key],
                                        device_id=to, device_id_type=pl.DeviceIdType.MESH)


def _run_exchange(plan, *, name):
    srcs, out_shapes, sem_shapes, phases = plan
    n, m = len(srcs), len(out_shapes)

    def body(*refs):
        for phase in phases(refs[:n], refs[n:n + m], refs[n + m:]):
            phase()

    return pl.pallas_call(
        body, name=name, out_shape=out_shapes,
        in_specs=[pl.BlockSpec(memory_space=pl.ANY)] * n, out_specs=[pl.BlockSpec(memory_space=pl.ANY)] * m,
        scratch_shapes=[pltpu.SemaphoreType.DMA(s) for s in sem_shapes],
    )(*srcs)


def _join_plans(p, q):
    (srcs_p, outs_p, sems_p, phases_p), (srcs_q, outs_q, sems_q, phases_q) = p, q

    def phases(src_refs, out_refs, sem_refs):
        a = phases_p(src_refs[:len(srcs_p)], out_refs[:len(outs_p)], sem_refs[:len(sems_p)])
        b = phases_q(src_refs[len(srcs_p):], out_refs[len(outs_p):], sem_refs[len(sems_p):])

        def both(fa, fb):
            def run():
                fa()
                fb()
            return run

        return tuple(both(fa, fb) for fa, fb in zip(a, b))

    return list(srcs_p) + list(srcs_q), list(outs_p) + list(outs_q), list(sems_p) + list(sems_q), phases


def _gather_plan(srcs):
    n = len(srcs)

    def phases(src_refs, out_refs, sem_refs):
        sems, local_sems = sem_refs[:2], sem_refs[2]
        x, y, c = lax.axis_index("x"), lax.axis_index("y"), lax.axis_index("c")
        idx = lambda px, py, pc: 4 * px + 2 * py + pc
        me, sibling = (x, y, c), (x, y, 1 - c)
        chips = [(1 - x, y), (x, 1 - y), (1 - x, 1 - y)]
        own = lambda a: pltpu.make_async_copy(src_refs[a], out_refs[a].at[idx(*me)], local_sems.at[a])
        to_sibling = lambda a: _remote(src_refs[a], out_refs[a].at[idx(*me)], sems, (0, a), sibling)
        to_chip = lambda a, j: _remote(src_refs[a], out_refs[a].at[idx(*me)], sems, (1 + j, a), (*chips[j], c))
        landed = lambda a, j: out_refs[a].at[idx(*chips[j], c)]
        passed_on = lambda a, j: _remote(landed(a, j), landed(a, j), sems, (4 + j, a), sibling)

        def first():
            for a in range(n):
                own(a).start()
                to_sibling(a).start()
                for j in range(3):
                    to_chip(a, j).start()

        def middle():
            for j in range(3):
                for a in range(n):
                    _remote(landed(a, j), landed(a, j), sems, (1 + j, a), me).wait_recv()
                    passed_on(a, j).start()

        def last():
            for a in range(n):
                blk = out_refs[a].at[idx(*sibling)]
                _remote(blk, blk, sems, (0, a), me).wait_recv()
                for j in range(3):
                    blk = out_refs[a].at[idx(*chips[j], 1 - c)]
                    _remote(blk, blk, sems, (4 + j, a), me).wait_recv()
            for a in range(n):
                to_sibling(a).wait_send()
                for j in range(3):
                    to_chip(a, j).wait_send()
                    passed_on(a, j).wait_send()
                own(a).wait()

        return first, middle, last

    return srcs, [jax.ShapeDtypeStruct((N_DEV,) + s.shape, s.dtype) for s in srcs], [(7, n), (7, n), (n,)], phases


def _sibling_swap_plan(srcs):
    n = len(srcs)

    def phases(src_refs, out_refs, sems):
        x, y, c = lax.axis_index("x"), lax.axis_index("y"), lax.axis_index("c")
        copies = lambda: [_remote(src_refs[a].at[2 * q + 1 - c], out_refs[a].at[q], sems, (q, a), (x, y, 1 - c))
                          for a in range(n) for q in range(4)]

        def first():
            for cp in copies():
                cp.start()

        def last():
            for cp in copies():
                cp.wait()

        return first, (lambda: None), last

    return srcs, [jax.ShapeDtypeStruct((4,) + s.shape[1:], s.dtype) for s in srcs], [(4, n), (4, n)], phases


def _chip_exchange_plan(srcs):
    n = len(srcs)

    def phases(src_refs, out_refs, sem_refs):
        sems, local_sems = sem_refs[:2], sem_refs[2]
        x, y, c = lax.axis_index("x"), lax.axis_index("y"), lax.axis_index("c")
        mine = 2 * x + y
        chips = [(1 - x, y), (x, 1 - y), (1 - x, 1 - y)]
        own = lambda a: pltpu.make_async_copy(src_refs[a].at[mine], out_refs[a].at[mine], local_sems.at[a])
        send = lambda a, j: _remote(src_refs[a].at[2 * chips[j][0] + chips[j][1]], out_refs[a].at[mine], sems, (j, a),
                                    (*chips[j], c))

        def first():
            for a in range(n):
                own(a).start()
                for j in range(3):
                    send(a, j).start()

        def last():
            for j in range(3):
                for a in range(n):
                    blk = out_refs[a].at[2 * chips[j][0] + chips[j][1]]
                    _remote(blk, blk, sems, (j, a), (x, y, c)).wait_recv()
            for a in range(n):
                for j in range(3):
                    send(a, j).wait_send()
                own(a).wait()

        return first, (lambda: None), last

    return srcs, [jax.ShapeDtypeStruct(s.shape, s.dtype) for s in srcs], [(3, n), (3, n), (n,)], phases


def _pair_add(core, g, got, *, name):
    q, r, c = got.shape
    tr, tc = _tile2d(r, c, cap=1024)

    def body(core_ref, a_ref, b_ref, o_ref):
        o_ref[...] = (a_ref[...].astype(F32) + b_ref[...].astype(F32)).astype(BF16)

    blk = pl.BlockSpec((1, tr, tc), lambda i, j, k, core_ref: (i, j, k))
    mine = pl.BlockSpec((1, tr, tc), lambda i, j, k, core_ref: (2 * i + core_ref[0], j, k))
    return pl.pallas_call(
        body, name=name, out_shape=jax.ShapeDtypeStruct(got.shape, BF16),
        grid_spec=pltpu.PrefetchScalarGridSpec(num_scalar_prefetch=1, grid=(q, r // tr, c // tc),
                                               in_specs=[mine, blk], out_specs=blk),
        compiler_params=_cparams(("parallel", "parallel", "parallel")))(core, g, got)


def _adamw(recv, w, m, v, *, name):
    r, c = w.shape
    n_terms = recv.shape[0]
    tr, tc = _tile2d(r, c, cap=512)

    def body(g_ref, w_ref, m_ref, v_ref, go_ref, d_ref, mo_ref, vo_ref):
        g = g_ref[0].astype(F32)
        for k in range(1, n_terms):
            g = g + g_ref[k].astype(F32)
        m_new = ADAM_B1 * m_ref[...] + (1.0 - ADAM_B1) * g
        v_new = ADAM_B2 * v_ref[...] + (1.0 - ADAM_B2) * (g * g)
        m_hat = m_new / (1.0 - ADAM_B1 ** ADAM_STEP)
        v_hat = v_new / (1.0 - ADAM_B2 ** ADAM_STEP)
        go_ref[...] = g
        d_ref[...] = -ADAM_LR * (m_hat / (jnp.sqrt(v_hat) + ADAM_EPS) + ADAM_WD * w_ref[...])
        mo_ref[...] = m_new
        vo_ref[...] = v_new

    blk = pl.BlockSpec((tr, tc), lambda i, j: (i, j))
    return pl.pallas_call(
        body, name=name, grid=(r // tr, c // tc),
        in_specs=[pl.BlockSpec((n_terms, tr, tc), lambda i, j: (0, i, j)), blk, blk, blk], out_specs=[blk] * 4,
        out_shape=[jax.ShapeDtypeStruct((r, c), F32)] * 4, compiler_params=_cparams(("parallel", "parallel")),
    )(recv, w, m, v)


def _tile2d(r, c, cap=256):
    if r <= cap:
        return r, c
    for t in range(cap - cap % BF16_ROWS, 0, -BF16_ROWS):
        if r % t == 0:
            return t, c
    return r, _pick(c, cap)


def _pack(pieces):
    total = sum(p.shape[0] for p in pieces)
    pad = (-total) % (8 * LANES)
    flat = jnp.concatenate(list(pieces) + [jnp.zeros((pad,), F32)])
    return flat.reshape(-1, LANES)


def _unpack(flat, sizes):
    flat = flat.reshape(-1)
    out, o = [], 0
    for n in sizes:
        out.append(flat[o:o + n])
        o += n
    return out


def _prepare_weights(full, vec, dims):
    rest = {n: t for n, t in full.items() if n != "w_in"}
    return {"w_in_t": _prepare_w_in(full["w_in"], dims), **_prepare_rest(rest, dims), **_prepare_vectors(vec, dims)}


def _prepare_w_in(slabs, dims):
    D = dims["D"]
    flat = slabs.reshape(-1, D)
    parts, pos = [], 0
    for orig_off, width, perm_off in sorted(dims["segs"], key=lambda t: t[2]):
        if perm_off > pos:
            parts.append(jnp.zeros((perm_off - pos, D), BF16))
        parts.append(flat[orig_off:orig_off + width])
        pos = perm_off + width
    if dims["d_in_perm"] > pos:
        parts.append(jnp.zeros((dims["d_in_perm"] - pos, D), BF16))
    return jnp.concatenate(parts, axis=0)


def _prepare_rest(full, dims):
    hm, hr, hn, rank = dims["hm"], dims["hr"], dims["hn"], dims["rank"]
    QR, KVR = dims["QR"], dims["KVR"]
    RW, TAIL = hr * hn, dims["TAIL"]
    full = {n: (t.reshape(-1, t.shape[2]) if n in _ROW_SHARDED + _TRANSPOSED
                else t.transpose(1, 0, 2).reshape(t.shape[1], -1)) for n, t in full.items()}
    wq = full["mla_wq_b"].reshape(hm, NOPE + ROPE, QR)
    wq = jnp.concatenate([wq, jnp.zeros((hm, QHEAD - NOPE - ROPE, QR), BF16)], axis=1).reshape(hm * QHEAD, QR)
    wkv = full["mla_wkv_b"].reshape(KVR, hm, 2, NOPE).transpose(0, 2, 1, 3).reshape(KVR, 2 * hm * NOPE)
    z = lambda rows: jnp.zeros((rows, RW), BF16)
    f = lambda nme: full[nme]
    split = ROPE + 2 * rank
    assert split % LANES == 0, split
    w2cat = jnp.concatenate([
        jnp.concatenate([z(ROPE), f("rwkv_w2_f"), z(rank)], axis=0),
        jnp.concatenate([z(ROPE + rank), f("rwkv_w2_b")], axis=0)], axis=1)
    a2cat = jnp.concatenate([
        jnp.concatenate([f("rwkv_a2_f"), z(TAIL - split - rank)], axis=0),
        jnp.concatenate([z(rank), f("rwkv_a2_b"), z(TAIL - split - 2 * rank)], axis=0)], axis=1)
    return dict(wq_b_t=wq, wkv_b=wkv, w2cat=w2cat, a2cat=a2cat, w_br_mla=full["w_br_mla"],
                w_br_rwkv=full["w_br_rwkv"], w_out=full["w_out"])


def _prepare_vectors(vec, dims):
    rank, RW, TAIL = dims["rank"], dims["hr"] * dims["hn"], dims["TAIL"]
    mu = vec["rwkv_mu"]
    mu_p = jnp.concatenate([mu[:3 * RW], jnp.zeros((ROPE,), F32), mu[3 * RW:],
                            jnp.zeros((TAIL - ROPE - 4 * rank,), F32)])
    row = lambda t: t.reshape(1, -1)
    return dict(
        mu=row(mu_p), g_pre=row(vec["g_pre"]), g_post=row(vec["g_post"]), mla_q_norm=row(vec["mla_q_norm"]),
        mla_kv_norm=row(vec["mla_kv_norm"]), w0_f=row(vec["rwkv_w0_f"]), w0_b=row(vec["rwkv_w0_b"]),
        a0_f=row(vec["rwkv_a0_f"]), a0_b=row(vec["rwkv_a0_b"]), k_k=row(vec["rwkv_k_k"]), k_a=row(vec["rwkv_k_a"]),
        r_k=row(vec["rwkv_r_k"]), gn_g=row(vec["rwkv_gn_g"]), gn_b=row(vec["rwkv_gn_b"]))


def _restore_grads(g, dims):
    return {"w_in": _restore_w_in(g["w_in"], dims), **_restore_rest(g, dims), **_restore_vectors(g, dims)}


def _restore_w_in(gw, dims):
    parts = [gw[perm_off:perm_off + width] for _, width, perm_off in sorted(dims["segs"])]
    return jnp.concatenate(parts, axis=0).reshape(N_DEV, dims["d_in"] // N_DEV, gw.shape[1])


def _restore_rest(g, dims):
    hm, hr, hn, rank = dims["hm"], dims["hr"], dims["hn"], dims["rank"]
    QR, KVR, RW = dims["QR"], dims["KVR"], hr * hn
    wq = g["wq_b"].reshape(hm, QHEAD, QR)[:, :NOPE + ROPE].reshape(N_DEV, -1, QR)
    wkv = g["wkv_b"].reshape(KVR, 2, hm, NOPE).transpose(0, 2, 1, 3).reshape(KVR, 2 * hm * NOPE)
    lo = lambda t, first, half: t[first:first + rank, half * RW:(half + 1) * RW].astype(BF16)
    cols = lambda t: t.reshape(t.shape[0], N_DEV, -1).transpose(1, 0, 2)
    return dict(
        mla_wq_b=wq, mla_wkv_b=cols(wkv), rwkv_w2_f=cols(lo(g["w2cat"], ROPE, 0)),
        rwkv_w2_b=cols(lo(g["w2cat"], ROPE + rank, 1)), rwkv_a2_f=cols(lo(g["a2cat"], 0, 0)),
        rwkv_a2_b=cols(lo(g["a2cat"], rank, 1)), w_br_mla=cols(g["w_br_mla"]), w_br_rwkv=cols(g["w_br_rwkv"]),
        w_out=g["w_out"].reshape(N_DEV, -1, g["w_out"].shape[1]))


def _restore_vectors(g, dims):
    rank, RW = dims["rank"], dims["hr"] * dims["hn"]
    mu = g["mu"][0]
    out = dict(
        rwkv_mu=jnp.concatenate([mu[:3 * RW], mu[3 * RW + ROPE:3 * RW + ROPE + 4 * rank]]),
        g_pre=g["g_pre"][0], g_post=g["g_post"][0], mla_q_norm=g["mla_q_norm"][0], mla_kv_norm=g["mla_kv_norm"][0],
        rwkv_w0_f=g["w0_f"][0], rwkv_w0_b=g["w0_b"][0], rwkv_a0_f=g["a0_f"][0], rwkv_a0_b=g["a0_b"][0],
        rwkv_k_k=g["k_k"][0], rwkv_k_a=g["k_a"][0], rwkv_r_k=g["r_k"][0], rwkv_gn_g=g["gn_g"][0],
        rwkv_gn_b=g["gn_b"][0])
    return out


def _dims(inp):
    D = inp["x"].shape[-1]
    QR, KVR = inp["mla_q_norm"].shape[0], inp["mla_kv_norm"].shape[0]
    hm = inp["mla_wq_b"].shape[1] * N_DEV // (NOPE + ROPE)
    hr, hn = inp["rwkv_r_k"].shape
    rank = inp["rwkv_w2_f"].shape[0]
    MW, RW = hm * VDIM, hr * hn
    TAIL = -(-(ROPE + 4 * rank) // LANES) * LANES
    orig, o = {}, 0
    for nme, w in (("q_a", QR), ("kv_a", KVR), ("k_rope", ROPE), ("rkv", 3 * RW), ("lora", 4 * rank), ("z_m", MW),
                   ("z_r", RW), ("gate_m", D), ("gate_r", D)):
        orig[nme] = (o, w)
        o += w
    assert o == inp["w_in"].shape[1] * N_DEV
    lay, d_in_perm = _layout(D, MW, RW, TAIL, QR, KVR)
    perm_off = dict(q_a=lay["q_a"][0], kv_a=lay["kv_a"][0], k_rope=lay["tail"][0], rkv=lay["r"][0],
                    lora=lay["tail"][0] + ROPE, z_m=lay["z_m"][0], z_r=lay["z_r"][0], gate_m=lay["gate_m"][0],
                    gate_r=lay["gate_r"][0])
    segs = [(orig[nme][0], orig[nme][1], perm_off[nme]) for nme in orig]
    return dict(D=D, QR=QR, KVR=KVR, hm=hm, hr=hr, hn=hn, rank=rank, TAIL=TAIL, segs=segs, d_in=o,
                d_in_perm=d_in_perm)


def kernel(x, g_pre, w_in, mla_q_norm, mla_wq_b, mla_kv_norm, mla_wkv_b, rwkv_mu, rwkv_w0_f, rwkv_w2_f, rwkv_w0_b, rwkv_w2_b, rwkv_a0_f, rwkv_a2_f, rwkv_a0_b, rwkv_a2_b, rwkv_k_k, rwkv_k_a, rwkv_r_k, rwkv_gn_g, rwkv_gn_b, w_br_mla, w_br_rwkv, w_out, g_post, loss_target, m_g_pre, m_w_in, m_mla_q_norm, m_mla_wq_b, m_mla_kv_norm, m_mla_wkv_b, m_rwkv_mu, m_rwkv_w0_f, m_rwkv_w2_f, m_rwkv_w0_b, m_rwkv_w2_b, m_rwkv_a0_f, m_rwkv_a2_f, m_rwkv_a0_b, m_rwkv_a2_b, m_rwkv_k_k, m_rwkv_k_a, m_rwkv_r_k, m_rwkv_gn_g, m_rwkv_gn_b, m_w_br_mla, m_w_br_rwkv, m_w_out, m_g_post, v_g_pre, v_w_in, v_mla_q_norm, v_mla_wq_b, v_mla_kv_norm, v_mla_wkv_b, v_rwkv_mu, v_rwkv_w0_f, v_rwkv_w2_f, v_rwkv_w0_b, v_rwkv_w2_b, v_rwkv_a0_f, v_rwkv_a2_f, v_rwkv_a0_b, v_rwkv_a2_b, v_rwkv_k_k, v_rwkv_k_a, v_rwkv_r_k, v_rwkv_gn_g, v_rwkv_gn_b, v_w_br_mla, v_w_br_rwkv, v_w_out, v_g_post):
    inp = dict(locals())
    dims = _dims(inp)
    stored = lambda t, n: t.T if n in _TRANSPOSED else t
    assert _MATS[0] == "w_in"
    shards = [stored(inp[n], n).astype(BF16) for n in _MATS]
    core = lax.axis_index("c").astype(jnp.int32).reshape(1)
    (w_in_slabs,) = _run_exchange(_gather_plan(shards[:1]), name="gather_w_in")
    W = {"w_in_t": _prepare_w_in(w_in_slabs, dims), **_prepare_vectors({n: inp[n] for n in _VECS}, dims)}
    loss, grad_x, g, recv_rest = _local_grads(x[0], loss_target[0], W, dims, exchange=(shards[1:], core))

    new = {}
    *recv_rest, got = recv_rest
    g_w_in, g = g["w_in"], _restore_vectors(g, dims)
    vsizes = [inp[n].size for n in _VECS] + [1]
    vflat = lambda prefix, src, last: _pack([src[prefix + n].reshape(-1) for n in _VECS] + [last])
    one = jnp.zeros((1,), F32)
    recv_w_in, vrecv = _run_exchange(
        _join_plans(_chip_exchange_plan([_pair_add(core, g_w_in, got, name="pair_add_w_in")]),
                    _direct_gather_plan(vflat("", g, loss.reshape(1)))), name="scatter_w_in")
    for n, t in zip(_MATS, [recv_w_in] + recv_rest):
        out = _adamw(t, stored(inp[n], n), stored(inp["m_" + n], n), stored(inp["v_" + n], n), name="adamw_" + n)
        new[n] = [stored(o, n) for o in out]

    vout = _adamw(vrecv, vflat("", inp, one), vflat("m_", inp, one), vflat("v_", inp, one), name="adamw_vectors")
    vparts = [_unpack(t, vsizes) for t in vout]
    for i, n in enumerate(_VECS):
        new[n] = [vp[i].reshape(inp[n].shape) for vp in vparts]
    loss = vparts[0][-1].reshape(())

    outs = [loss, grad_x[None]]
    for k in range(4):
        outs += [new[n][k] for n in _WEIGHTS]
    return tuple(outs)
```

```python
import functools
import math

import jax
import jax.numpy as jnp
from jax import lax
from jax.experimental import pallas as pl
from jax.experimental.pallas import tpu as pltpu

F32 = jnp.float32
BF16 = jnp.bfloat16

N_DEV = 8
LANES = 128
BF16_ROWS = 16
NOPE, ROPE, VDIM = 128, 64, 128
QHEAD = 256
ROPE_THETA = 10000.0
NORM_EPS = 1e-6
GN_EPS = 64e-5
CHUNK = 64
SUB = 16
VMEM_LIMIT = 56 * 1024 * 1024

ADAM_LR, ADAM_B1, ADAM_B2, ADAM_EPS, ADAM_WD, ADAM_STEP = 0.001, 0.9, 0.999, 1e-08, 0.01, 10


def _cparams(sem):
    return pltpu.CompilerParams(dimension_semantics=sem, vmem_limit_bytes=VMEM_LIMIT)


def _pick(n, cap):
    if n <= cap:
        return n
    for t in range(cap - cap % LANES, 0, -LANES):
        if n % t == 0:
            return t
    raise ValueError(f"no tile for {n} under {cap}")


def _mm(a, b, *, ta=False, tb=False, out_dtype=F32, name, tm_cap=1024, tn_cap=512, tk_cap=2048, ride=None):
    K, M = a.shape if ta else a.shape[::-1]
    N = b.shape[0] if tb else b.shape[1]
    assert (b.shape[1] if tb else b.shape[0]) == K, (a.shape, b.shape, ta, tb)
    tm, tn, tk = _pick(M, tm_cap), _pick(N, tn_cap), _pick(K, tk_cap)
    nj, nk = N // tn, K // tk
    steps = (M // tm) * nj * nk
    dn = (((0 if ta else 1,), (1 if tb else 0,)), ((), ()))
    srcs, extra_shapes, sem_shapes, phases = ride if ride else ((), (), (), None)
    n_src, n_extra = len(srcs), len(extra_shapes)

    def body(*refs):
        a_ref, b_ref, o_ref = refs[0], refs[1], refs[2 + n_src]
        acc_ref = refs[3 + n_src + n_extra]
        k = pl.program_id(2)
        if ride:
            step = (pl.program_id(0) * nj + pl.program_id(1)) * nk + k
            first, middle, last = phases(refs[2:2 + n_src], refs[3 + n_src:3 + n_src + n_extra],
                                         refs[4 + n_src + n_extra:])
            pl.when(step == 0)(first)
            pl.when(step == (steps * 15) // 16)(middle)
        p = lax.dot_general(a_ref[...], b_ref[...], dn, preferred_element_type=F32)

        @pl.when(k == 0)
        def _():
            acc_ref[...] = p

        @pl.when(k > 0)
        def _():
            acc_ref[...] += p

        @pl.when(k == nk - 1)
        def _():
            o_ref[...] = acc_ref[...].astype(out_dtype)

        if ride:
            pl.when(step == steps - 1)(last)

    a_spec = pl.BlockSpec((tk, tm), lambda i, j, k: (k, i)) if ta else pl.BlockSpec((tm, tk), lambda i, j, k: (i, k))
    b_spec = pl.BlockSpec((tn, tk), lambda i, j, k: (j, k)) if tb else pl.BlockSpec((tk, tn), lambda i, j, k: (k, j))
    hbm = pl.BlockSpec(memory_space=pl.ANY)
    out = pl.pallas_call(
        body, name=name, grid=(M // tm, nj, nk),
        in_specs=[a_spec, b_spec] + [hbm] * n_src,
        out_specs=[pl.BlockSpec((tm, tn), lambda i, j, k: (i, j))] + [hbm] * n_extra,
        out_shape=[jax.ShapeDtypeStruct((M, N), out_dtype)] + list(extra_shapes),
        scratch_shapes=[pltpu.VMEM((tm, tn), F32)] + [pltpu.SemaphoreType.DMA(s) for s in sem_shapes],
        compiler_params=_cparams(("arbitrary",) * 3 if ride else ("parallel", "parallel", "arbitrary")),
    )(a, b, *srcs)
    return out if ride else out[0]


def _view(arr, off, width):
    assert off % width == 0, (off, width)
    return (arr, off // width, width)


def _rowwise(fn, rows, params, out_rows, out_accs=(), *, tile, name):
    rows = [r if isinstance(r, tuple) else (r, 0, r.shape[1]) for r in rows]
    S = rows[0][0].shape[0]
    T = min(tile, S)
    assert S % T == 0
    n_rows, n_par, n_out = len(rows), len(params), len(out_rows)
    into = [o[2] if len(o) == 3 else None for o in out_rows]
    carried = [t[0] for t in into if t is not None and t[0] is not None]

    def body(*refs):
        ins = [r[...] for r in refs[:n_rows + n_par]]
        outs = fn(*ins)
        out_refs = refs[n_rows + n_par + len(carried):]
        for o_ref, val in zip(out_refs[:n_out], outs[:n_out]):
            o_ref[...] = val.astype(o_ref.dtype)
        i = pl.program_id(0)
        for o_ref, val in zip(out_refs[n_out:], outs[n_out:]):
            @pl.when(i == 0)
            def _(o_ref=o_ref, val=val):
                o_ref[...] = val

            @pl.when(i > 0)
            def _(o_ref=o_ref, val=val):
                o_ref[...] += val

    in_specs = [pl.BlockSpec((T, w), functools.partial(lambda i, cb: (i, cb), cb=cb)) for _, cb, w in rows]
    in_specs += [pl.BlockSpec(p.shape, lambda i: (0, 0)) for p in params]
    in_specs += [pl.BlockSpec(memory_space=pl.ANY)] * len(carried)
    out_specs, out_shape, aliases = [], [], {}
    for k, (o, t) in enumerate(zip(out_rows, into)):
        w, dt = o[0], o[1]
        if t is None:
            out_specs.append(pl.BlockSpec((T, w), lambda i: (i, 0)))
            out_shape.append(jax.ShapeDtypeStruct((S, w), dt))
            continue
        buf, total, first = t
        assert first % w == 0
        out_specs.append(pl.BlockSpec((T, w), functools.partial(lambda i, cb: (i, cb), cb=first // w)))
        out_shape.append(jax.ShapeDtypeStruct((S, total), dt))
        if buf is not None:
            aliases[n_rows + n_par + len(aliases)] = k
    out_specs += [pl.BlockSpec(s, lambda i: (0, 0)) for s in out_accs]
    out_shape += [jax.ShapeDtypeStruct(s, F32) for s in out_accs]
    return pl.pallas_call(
        body, name=name, grid=(S // T,), in_specs=in_specs, out_specs=out_specs, out_shape=out_shape,
        input_output_aliases=aliases, compiler_params=_cparams(("arbitrary",)),
    )(*[r[0] for r in rows], *params, *carried)


def _mm_sel(x, sel):
    hi = x.astype(BF16)
    lo = (x - hi.astype(F32)).astype(BF16)
    d = lambda u: jnp.dot(u, sel, preferred_element_type=F32)
    return d(hi) + d(lo)


@jax.custom_vjp
def _sel(x, sel, sel_t):
    return _mm_sel(x, sel)


def _sel_fwd(x, sel, sel_t):
    return _mm_sel(x, sel), (sel, sel_t)


def _sel_bwd(res, ct):
    sel, sel_t = res
    return _mm_sel(ct, sel_t), jnp.zeros_like(sel), jnp.zeros_like(sel_t)


_sel.defvjp(_sel_fwd, _sel_bwd)


def _rms(x, g):
    return x * lax.rsqrt(jnp.mean(x * x, axis=-1, keepdims=True) + NORM_EPS) * g


def _sigmoid(x):
    return 1.0 / (1.0 + jnp.exp(-x))


def _silu(x):
    return x * _sigmoid(x)


def _softplus(x):
    return jnp.maximum(x, 0.0) + jnp.log(1.0 + jnp.exp(-jnp.abs(x)))


def _f_mla_norm(q_a, kv_a, qg, kvg):
    return _rms(q_a, qg), _rms(kv_a, kvg)


def _f_rope(hm, qraw, kr_in, cosx, sinx, rot, rot_t):
    def rope(t):
        return t * cosx + _sel(t, rot, rot_t) * sinx
    parts = []
    for h in range(hm):
        parts.append(qraw[:, h * QHEAD:h * QHEAD + NOPE])
        parts.append(rope(qraw[:, h * QHEAD + NOPE:(h + 1) * QHEAD]))
    return jnp.concatenate(parts, axis=1), rope(kr_in)


def _f_rwkv_pre(rw, k, tail, w0f, w0b, a0f, a0b, k_k, k_a, w2cat, a2cat, seg, seg_t):
    split = w2cat.shape[0]
    zw = jnp.dot(jnp.tanh(tail[:, :split]).astype(BF16), w2cat, preferred_element_type=F32)
    za = jnp.dot(tail[:, split:].astype(BF16), a2cat, preferred_element_type=F32)
    return _f_rwkv_core(rw, k, zw, za, w0f, w0b, a0f, a0b, k_k, k_a, seg, seg_t)


def _f_rwkv_core(rw, k, zw, za, w0f, w0b, a0f, a0b, k_k, k_a, seg, seg_t):
    lw_f = -jnp.exp(-_softplus(-(w0f + zw[:, :rw])) - 0.5)
    lw_b = -jnp.exp(-_softplus(-(w0b + zw[:, rw:])) - 0.5)
    a_f = _sigmoid(a0f + za[:, :rw])
    a_b = _sigmoid(a0b + za[:, rw:])
    kk = k * k_k
    nrm = jnp.sqrt(_sel(_sel(kk * kk, seg, seg_t), seg_t, seg))
    kk = kk / jnp.maximum(nrm, 1e-12)
    k_f = k * (1.0 + (a_f - 1.0) * k_a)
    k_b = k * (1.0 + (a_b - 1.0) * k_a)
    return lw_f, lw_b, k_f, k_b, -kk, kk * a_f, kk * a_b


def _f_post(hn, y_f, y_b, r, k_f, k_b, v, z_r, o_mla, z_m, gn_g, gn_b, r_k, seg, seg_t):
    segsum = lambda t: _sel(_sel(t, seg, seg_t), seg_t, seg)
    y = y_f + y_b
    mu = segsum(y) * (1.0 / hn)
    yc = y - mu
    var = segsum(yc * yc) * (1.0 / hn)
    yn = yc * lax.rsqrt(var + GN_EPS) * gn_g + gn_b
    bonus = segsum(r * (k_f + k_b) * r_k) * v
    return o_mla * _silu(z_m), (yn + bonus) * _silu(z_r)


def _f_merge(u_m, u_r, g_m, g_r):
    return _sigmoid(g_m) * u_m + _sigmoid(g_r) * u_r


_NN = ((2,), (1,))
_NT = ((2,), (2,))
_TN = ((1,), (1,))

_SCAN_PASSES = {"cum": 2, "gram": 3, "solve": 1, "apply": 1, "state": 1}


def _hdot_raw(passes, x, y, dims):
    dn = (dims, ((0,), (0,)))
    d = lambda p, q: lax.dot_general(p, q, dn, preferred_element_type=F32)
    xh = x.astype(BF16)
    yh = y.astype(BF16)
    if passes == 1:
        return d(xh, yh)
    yl = (y - yh.astype(F32)).astype(BF16)
    kx, ky = (1 if dims == _TN else 2), (2 if dims == _NT else 1)
    depth = x.shape[kx]
    if all(axis == 1 or depth % LANES == 0 for axis in (kx, ky)):
        if passes == 2:
            return d(jnp.concatenate([xh, xh], axis=kx), jnp.concatenate([yh, yl], axis=ky))
        xl = (x - xh.astype(F32)).astype(BF16)
        return d(jnp.concatenate([xh, xl, xh], axis=kx), jnp.concatenate([yh, yh, yl], axis=ky))
    if passes == 2:
        axis = 1 if dims == _NT else 2
        width = y.shape[axis]
        both = d(xh, jnp.concatenate([yh, yl], axis=axis))
        return both[:, :, :width] + both[:, :, width:]
    xl = (x - xh.astype(F32)).astype(BF16)
    if dims == _TN:
        return d(xh, yh) + d(xh, yl) + d(xl, yh)
    rows = x.shape[1]
    both = d(jnp.concatenate([xh, xl], axis=1), yh)
    return both[:, :rows] + both[:, rows:] + d(xh, yl)


@functools.partial(jax.custom_vjp, nondiff_argnums=(2, 3))
def _hdot_p(x, y, dims, passes):
    return _hdot_raw(passes, x, y, dims)


def _hdot_fwd(x, y, dims, passes):
    return _hdot_raw(passes, x, y, dims), (x, y)


def _hdot_bwd(dims, passes, res, ct):
    x, y = res
    if dims == _NN:
        return _hdot_raw(passes, ct, y, _NT), _hdot_raw(passes, x, ct, _TN)
    if dims == _NT:
        return _hdot_raw(passes, ct, y, _NN), _hdot_raw(passes, ct, x, _TN)
    return _hdot_raw(passes, y, ct, _NT), _hdot_raw(passes, x, ct, _NN)


_hdot_p.defvjp(_hdot_fwd, _hdot_bwd)


def _hdot(x, y, dims, kind):
    return _hdot_p(x, y, dims, _SCAN_PASSES[kind])


def _tri_solve(n_mat, x, length):
    row = lax.broadcasted_iota(jnp.int32, (length, length), 0)
    col = lax.broadcasted_iota(jnp.int32, (length, length), 1)
    eye = (row == col).astype(F32)[None]
    diag_blk = ((row // SUB) == (col // SUB))[None]
    nd = jnp.where(diag_blk, n_mat, 0.0)
    no = n_mat - nd
    dinv = eye + nd
    p = _hdot(nd, nd, _NN, "solve")
    for k in range(int(math.log2(SUB)) - 1):
        if k == int(math.log2(SUB)) - 2:
            dinv = dinv + _hdot(dinv, p, _NN, "solve")
        else:
            both = _hdot(jnp.concatenate([dinv, p], axis=1), p, _NN, "solve")
            dinv, p = dinv + both[:, :length], both[:, length:]
    both = _hdot(dinv, jnp.concatenate([x, no], axis=2), _NN, "solve")
    u, q = both[:, :, :x.shape[2]], both[:, :, x.shape[2]:]
    width = x.shape[2]
    for level in range(int(math.log2(length // SUB))):
        if level == int(math.log2(length // SUB)) - 1:
            u = u + _hdot(q, u, _NN, "solve")
        else:
            both = _hdot(q, jnp.concatenate([u, q], axis=2), _NN, "solve")
            u, q = u + both[:, :, :width], both[:, :, width:]
    return u


def _rwkv_chunk(rev, s0, r, lw, k, v, a, b):
    pairs, length, width = r.shape
    hn = width // 2
    row = lax.broadcasted_iota(jnp.int32, (length, length), 0)
    col = lax.broadcasted_iota(jnp.int32, (length, length), 1)
    row2 = lax.broadcasted_iota(jnp.int32, (length, 2 * length), 0)
    col2 = lax.broadcasted_iota(jnp.int32, (length, 2 * length), 1)
    col2 = jnp.where(col2 >= length, col2 - length, col2)
    if rev is None:
        half = pairs // 2
        back = lax.broadcasted_iota(jnp.int32, (pairs, length, length), 0) >= half
        idx2 = lax.broadcasted_iota(jnp.int32, (2 * pairs, length, 2 * length), 0)
        back2 = ((idx2 >= half) & (idx2 < pairs)) | (idx2 >= pairs + half)
        ahead = jnp.where(back, (col - row)[None], (row - col)[None])
        ahead2 = jnp.where(back2, (col2 - row2)[None], (row2 - col2)[None])
        incl, strict2, incl2 = ahead >= 0, ahead2 > 0, ahead2 >= 0
    else:
        incl = ((row <= col) if rev else (row >= col))[None]
        strict2 = ((row2 < col2) if rev else (row2 > col2))[None]
        incl2 = ((row2 <= col2) if rev else (row2 >= col2))[None]
    lane = lax.broadcasted_iota(jnp.int32, (1, 1, width), 2)
    first = lane < hn
    head_mask = jnp.concatenate([jnp.broadcast_to(first.astype(F32), (pairs, 1, width)),
                                 jnp.broadcast_to(1.0 - first.astype(F32), (pairs, 1, width))], axis=0)
    twice = lambda t: jnp.concatenate([t, t], axis=0)
    pick = lambda t: jnp.where(first, t[:pairs], t[pairs:])

    t_incl = jnp.broadcast_to(incl.astype(F32), (pairs, length, length))
    cum = _hdot(t_incl, lw, _NN, "cum")
    g = jnp.exp(cum)
    g_inv = jnp.exp(-cum)
    at = a * jnp.exp(cum - lw)
    rt = r * g
    bt = b * g_inv
    kt = k * g_inv
    by_pair = lambda t: jnp.concatenate([t[:pairs], t[pairs:]], axis=1)
    lhs = jnp.concatenate([twice(at) * head_mask, twice(rt) * head_mask], axis=1)
    gram = _hdot(by_pair(lhs), jnp.concatenate([bt, kt], axis=1), _NT, "gram")
    gram = jnp.concatenate([gram[:, :2 * length], gram[:, 2 * length:]], axis=0)
    top = jnp.where(strict2, gram[:, :length], 0.0)
    bot = jnp.where(incl2, gram[:, length:], 0.0)
    pick_rows = lambda t: jnp.where(first, t[:, :length], t[:, length:])
    from_state = _hdot(jnp.concatenate([at, rt], axis=1), s0, _NT, "apply")
    x = from_state[:, :length] + pick_rows(
        _hdot(by_pair(top), jnp.concatenate([jnp.zeros_like(v), v], axis=1), _NN, "apply"))
    u = pick(_tri_solve(top[:, :, :length], twice(x), length))
    y = from_state[:, length:] + pick_rows(_hdot(by_pair(bot), jnp.concatenate([u, v], axis=1), _NN, "apply"))
    g_last = jnp.exp(jnp.sum(lw, axis=1, keepdims=True))
    ri = lax.broadcasted_iota(jnp.int32, (width, width), 0)
    ci = lax.broadcasted_iota(jnp.int32, (width, width), 1)
    same_head = ((ri < hn) == (ci < hn))[None]
    upd = _hdot(jnp.concatenate([u, v], axis=1), jnp.concatenate([bt, kt], axis=1), _TN, "state")
    s1 = (s0 + jnp.where(same_head, upd, 0.0)) * g_last
    return y, s1


def _split_pairs(x):
    return jnp.stack([x[:, p * LANES:(p + 1) * LANES] for p in range(x.shape[1] // LANES)])


def _merge_pairs(x):
    return jnp.concatenate([x[p] for p in range(x.shape[0])], axis=1)


def _scan_specs(views, rw, nc, rev):
    cidx = (lambda c: nc - 1 - c) if rev else (lambda c: c)
    seqs = [pl.BlockSpec((CHUNK, rw), functools.partial(lambda c, cb: (cidx(c), cb), cb=cb)) for _, cb, _ in views]
    plain = pl.BlockSpec((CHUNK, rw), lambda c: (cidx(c), 0))
    st = pl.BlockSpec((1, rw // LANES, LANES, LANES), lambda c: (cidx(c), 0, 0, 0))
    return seqs, plain, st


def _as_views(arrs, rw):
    return [t if isinstance(t, tuple) else (t, 0, rw) for t in arrs]


def _rwkv_scan_fwd(ops_f, ops_b, rw, *, name):
    S = _as_views(ops_f, rw)[0][0].shape[0]
    nc, pairs = S // CHUNK, rw // LANES
    in_specs, out_specs, arrays = [], [], []
    for rev, ops in ((False, ops_f), (True, ops_b)):
        views = _as_views(ops, rw)
        seqs, plain, st = _scan_specs(views, rw, nc, rev)
        in_specs += seqs
        out_specs += [plain, st]
        arrays += [t[0] for t in views]

    def both(refs_f, refs_b):
        return [jnp.concatenate([_split_pairs(f[...]), _split_pairs(b[...])], axis=0) for f, b in zip(refs_f, refs_b)]

    def body(*refs):
        (y_f, st_f, y_b, st_b), s_ref = refs[12:16], refs[16]

        @pl.when(pl.program_id(0) == 0)
        def _():
            s_ref[...] = jnp.zeros_like(s_ref)

        s0 = s_ref[...]
        st_f[0] = s0[:pairs]
        st_b[0] = s0[pairs:]
        y, s1 = _rwkv_chunk(None, s0, *both(refs[:6], refs[6:12]))
        y_f[...] = _merge_pairs(y[:pairs])
        y_b[...] = _merge_pairs(y[pairs:])
        s_ref[...] = s1

    return pl.pallas_call(
        body, name=name, grid=(nc,), in_specs=in_specs, out_specs=out_specs,
        out_shape=[jax.ShapeDtypeStruct((S, rw), F32), jax.ShapeDtypeStruct((nc, pairs, LANES, LANES), F32)] * 2,
        scratch_shapes=[pltpu.VMEM((2 * pairs, LANES, LANES), F32)],
        compiler_params=_cparams(("arbitrary",)),
    )(*arrays)


def _rwkv_scan_bwd(ops_f, ops_b, states_f, states_b, dy, rw, *, name):
    S = dy.shape[0]
    nc, pairs = S // CHUNK, rw // LANES
    in_specs, arrays = [], []
    for rev, ops, states in ((False, ops_f, states_f), (True, ops_b, states_b)):
        views = _as_views(list(ops) + [dy], rw)
        seqs, plain, st = _scan_specs(views, rw, nc, not rev)
        in_specs += seqs + [st]
        arrays += [t[0] for t in views] + [states]
    out_specs = []
    for rev in (False, True):
        out_specs += [_scan_specs([], rw, nc, not rev)[1]] * 6

    def both(refs_f, refs_b):
        return [jnp.concatenate([_split_pairs(f[...]), _split_pairs(b[...])], axis=0) for f, b in zip(refs_f, refs_b)]

    def body(*refs):
        ds_ref = refs[28]

        @pl.when(pl.program_id(0) == 0)
        def _():
            ds_ref[...] = jnp.zeros_like(ds_ref)

        s0 = jnp.concatenate([refs[7][0], refs[15][0]], axis=0)
        _, vjp = jax.vjp(functools.partial(_rwkv_chunk, None), s0, *both(refs[:6], refs[8:14]))
        (dy,) = both(refs[6:7], refs[14:15])
        grads = vjp((dy, ds_ref[...]))
        ds_ref[...] = grads[0]
        for o_f, o_b, gval in zip(refs[16:22], refs[22:28], grads[1:]):
            o_f[...] = _merge_pairs(gval[:pairs])
            o_b[...] = _merge_pairs(gval[pairs:])

    return pl.pallas_call(
        body, name=name, grid=(nc,), in_specs=in_specs, out_specs=out_specs,
        out_shape=[jax.ShapeDtypeStruct((S, rw), F32)] * 12,
        scratch_shapes=[pltpu.VMEM((2 * pairs, LANES, LANES), F32)],
        compiler_params=_cparams(("arbitrary",)),
    )(*arrays)


def _shift_lerp(x_view, mu, d=None, into=None, *, name):
    arr, off, width = x_view
    S = arr.shape[0]
    cb = _pick(width, 512)
    assert off % cb == 0

    def cshift(t):
        rows = lax.broadcasted_iota(jnp.int32, t.shape, 0)
        prev = jnp.where(rows == 0, 0.0, pltpu.roll(t, 1, 0))
        nxt = jnp.where(rows == S - 1, 0.0, pltpu.roll(t, S - 1, 0))
        return 0.5 * (prev + nxt)

    def fwd_body(x_ref, mu_ref, o_ref):
        x = x_ref[...]
        o_ref[...] = x + mu_ref[...] * (cshift(x) - x)

    def bwd_body(x_ref, mu_ref, d_ref, _, dx_ref, dmu_ref):
        x, m, dd = x_ref[...], mu_ref[...], d_ref[...]
        gm = m * dd
        dx_ref[...] = (dd - gm + cshift(gm)).astype(dx_ref.dtype)
        dmu_ref[...] = jnp.sum(dd * (cshift(x) - x), axis=0, keepdims=True)

    x_spec = pl.BlockSpec((S, cb), lambda j: (0, off // cb + j))
    blk = pl.BlockSpec((S, cb), lambda j: (0, j))
    vec = pl.BlockSpec((1, cb), lambda j: (0, j))
    if d is None:
        return pl.pallas_call(
            fwd_body, name=name, grid=(width // cb,), in_specs=[x_spec, vec], out_specs=blk,
            out_shape=jax.ShapeDtypeStruct((S, width), F32), compiler_params=_cparams(("parallel",)),
        )(arr, mu)
    buf, first = into
    assert first % cb == 0
    return pl.pallas_call(
        bwd_body, name=name, grid=(width // cb,),
        in_specs=[x_spec, vec, blk, pl.BlockSpec(memory_space=pl.ANY)],
        out_specs=[pl.BlockSpec((S, cb), lambda j: (0, first // cb + j)), vec],
        out_shape=[jax.ShapeDtypeStruct(buf.shape, buf.dtype), jax.ShapeDtypeStruct((1, width), F32)],
        input_output_aliases={3: 0}, compiler_params=_cparams(("parallel",)),
    )(arr, mu, d, buf)


def _attention_fwd(qfull, kv, kr, hm, scale, *, tq, name):
    S = qfull.shape[0]
    nt = (((1,), (1,)), ((), ()))

    def body(q_ref, kn_ref, kr_ref, v_ref, o_ref, lse_ref, k_scr):
        _head_keys(kn_ref, kr_ref, k_scr)
        s = lax.dot_general(q_ref[...], k_scr[...], nt, preferred_element_type=F32)
        m = jnp.max(s, axis=-1, keepdims=True)
        p = jnp.exp((s - m) * scale)
        l = jnp.sum(p, axis=-1, keepdims=True)
        o_ref[...] = jnp.dot(p.astype(BF16), v_ref[...], preferred_element_type=F32) * (1.0 / l)
        lse_ref[...] = jnp.broadcast_to(m * scale + jnp.log(l), lse_ref.shape)

    oblk = pl.BlockSpec((tq, VDIM), lambda h, i: (i, h))
    return pl.pallas_call(
        body, name=name, grid=(hm, S // tq),
        in_specs=[pl.BlockSpec((tq, QHEAD), lambda h, i: (i, h)),
                  pl.BlockSpec((S, NOPE), lambda h, i: (0, h)),
                  pl.BlockSpec((S, LANES), lambda h, i: (0, 0)),
                  pl.BlockSpec((S, VDIM), lambda h, i: (0, hm + h))],
        out_specs=[oblk, oblk],
        out_shape=[jax.ShapeDtypeStruct((S, hm * VDIM), F32)] * 2,
        scratch_shapes=[pltpu.VMEM((S, QHEAD), BF16)],
        compiler_params=_cparams(("parallel", "arbitrary")),
    )(qfull, kv, kr, kv)


def _head_keys(kn_ref, kr_ref, k_scr):
    @pl.when(pl.program_id(1) == 0)
    def _():
        k_scr[:, :NOPE] = kn_ref[...]
        k_scr[:, NOPE:] = kr_ref[...]


def _attention_bwd(qfull, kv, kr, o, lse, d_o, hm, scale, *, tq, name):
    S = qfull.shape[0]
    tq = min(tq, S)
    nq = S // tq
    tn = (((0,), (0,)), ((), ()))
    nt = (((1,), (1,)), ((), ()))

    def body(q_ref, kn_ref, kr_ref, v_ref, o_ref, lse_ref, do_ref, dq_ref, dk_ref, dv_ref, k_scr):
        _head_keys(kn_ref, kr_ref, k_scr)
        s = lax.dot_general(q_ref[...], k_scr[...], nt, preferred_element_type=F32)
        p = jnp.exp(s * scale - lse_ref[:, 0:1])
        d_out = do_ref[...]
        delta = jnp.sum(d_out * o_ref[...], axis=-1, keepdims=True)
        d_out = d_out.astype(BF16)
        dp = lax.dot_general(d_out, v_ref[...], nt, preferred_element_type=F32)
        ds = (p * (dp - delta)).astype(BF16)
        dq_ref[...] = jnp.dot(ds, k_scr[...], preferred_element_type=F32) * scale
        dv = lax.dot_general(p.astype(BF16), d_out, tn, preferred_element_type=F32)
        dk = lax.dot_general(ds, q_ref[...], tn, preferred_element_type=F32)
        i = pl.program_id(1)
        for ref, val in ((dk_ref, dk), (dv_ref, dv)):
            @pl.when(i == 0)
            def _(ref=ref, val=val):
                ref[...] = val

            @pl.when(i > 0)
            def _(ref=ref, val=val):
                ref[...] += val

        @pl.when(i == nq - 1)
        def _():
            dk_ref[...] = dk_ref[...] * scale

    qblk = pl.BlockSpec((tq, QHEAD), lambda h, i: (i, h))
    oblk = pl.BlockSpec((tq, VDIM), lambda h, i: (i, h))
    return pl.pallas_call(
        body, name=name, grid=(hm, nq),
        in_specs=[qblk,
                  pl.BlockSpec((S, NOPE), lambda h, i: (0, h)),
                  pl.BlockSpec((S, LANES), lambda h, i: (0, 0)),
                  pl.BlockSpec((S, VDIM), lambda h, i: (0, hm + h)),
                  oblk, oblk, oblk],
        out_specs=[qblk, pl.BlockSpec((S, QHEAD), lambda h, i: (0, h)), pl.BlockSpec((S, VDIM), lambda h, i: (0, h))],
        out_shape=[jax.ShapeDtypeStruct((S, hm * QHEAD), F32), jax.ShapeDtypeStruct((S, hm * QHEAD), F32),
                   jax.ShapeDtypeStruct((S, hm * VDIM), F32)],
        scratch_shapes=[pltpu.VMEM((S, QHEAD), BF16)],
        compiler_params=_cparams(("parallel", "arbitrary")),
    )(qfull, kv, kr, kv, o, lse, d_o)


def _layout(D, MW, RW, TAIL, QR, KVR):
    names = ["gate_m", "gate_r", "z_m", "z_r", "q_a", "kv_a", "r", "k", "v", "tail"]
    widths = [D, D, MW, RW, QR, KVR, RW, RW, RW, TAIL]
    offs, o = {}, 0
    for nme, w in zip(names, widths):
        assert o % w == 0, (nme, o, w)
        offs[nme] = (o, w)
        o += w
    return offs, o


def _local_grads(x, target, W, dims, exchange=None):
    S, D = x.shape
    hm, hr, hn, rank = dims["hm"], dims["hr"], dims["hn"], dims["rank"]
    MW, RW = hm * VDIM, hr * hn
    TAIL = dims["TAIL"]
    QR, KVR = W["mla_q_norm"].shape[1], W["mla_kv_norm"].shape[1]
    lay, d_in = _layout(D, MW, RW, TAIL, QR, KVR)
    T = 256
    scale = (NOPE + ROPE) ** -0.5
    col = lambda arr, nme: _view(arr, *lay[nme])

    pos = jnp.arange(S, dtype=F32)
    inv_freq = jnp.power(ROPE_THETA, -jnp.arange(0, ROPE, 2, dtype=F32) / ROPE)
    ang = pos[:, None] * inv_freq[None, :]
    zpad = jnp.zeros((S, LANES - ROPE), F32)
    cosx = jnp.concatenate([jnp.cos(ang), jnp.cos(ang), zpad], axis=1)
    sinx = jnp.concatenate([jnp.sin(ang), jnp.sin(ang), zpad], axis=1)
    ri, ci = jnp.arange(LANES)[:, None], jnp.arange(LANES)[None, :]
    half = ROPE // 2
    rot = (jnp.where((ri == ci - half) & (ci >= half) & (ci < ROPE), 1.0, 0.0)
           - jnp.where((ri == ci + half) & (ci < half), 1.0, 0.0)).astype(BF16)
    rot_t = rot.T
    seg = (jnp.arange(RW)[:, None] // hn == jnp.arange(LANES)[None, :]).astype(BF16)
    seg_t = seg.T

    (h,) = _rowwise(lambda xb, g: (_rms(xb, g),), [x], [W["g_pre"]], [(D, BF16)], tile=2 * T, name="pre_norm")
    if exchange is None:
        proj = _mm(h, W["w_in_t"], tb=True, name="in_proj")
    else:
        proj, *slabs = _mm(h, W["w_in_t"], tb=True, ride=_gather_plan(exchange[0]), name="in_proj")
        W = {**W, **_prepare_rest(dict(zip(_MATS[1:], slabs)), dims)}

    qn, kvn = _rowwise(_f_mla_norm, [col(proj, "q_a"), col(proj, "kv_a")], [W["mla_q_norm"], W["mla_kv_norm"]],
                       [(QR, BF16), (KVR, BF16)], tile=2 * T, name="mla_norm")
    qraw = _mm(qn, W["wq_b_t"], tb=True, name="q_up")
    kv = _mm(kvn, W["wkv_b"], out_dtype=BF16, name="kv_up")
    kr_view = _view(proj, lay["tail"][0], LANES)
    qfull, kr = _rowwise(functools.partial(_f_rope, hm), [qraw, kr_view, cosx, sinx], [rot, rot_t],
                         [(hm * QHEAD, BF16), (LANES, BF16)], tile=2 * T, name="rope")
    o_mla, lse = _attention_fwd(qfull, kv, kr, hm, scale, tq=T, name="attn_fwd")

    shift_view = (proj, lay["r"][0], 3 * RW + TAIL)
    rl = _shift_lerp(shift_view, W["mu"], name="shift_fwd")
    rl_r, rl_k, rl_v = _view(rl, 0, RW), _view(rl, RW, RW), _view(rl, 2 * RW, RW)
    rl_tail = _view(rl, 3 * RW, TAIL)
    pre_params = [W["w0_f"], W["w0_b"], W["a0_f"], W["a0_b"], W["k_k"], W["k_a"], W["w2cat"], W["a2cat"], seg, seg_t]
    pre_fn = functools.partial(_f_rwkv_pre, RW)
    lw_f, lw_b, k_f, k_b, a_n, b_f, b_b = _rowwise(pre_fn, [rl_k, rl_tail], pre_params, [(RW, F32)] * 7, tile=T,
                                                    name="rwkv_pre")
    ops_f = (rl_r, lw_f, k_f, rl_v, a_n, b_f)
    ops_b = (rl_r, lw_b, k_b, rl_v, a_n, b_b)
    y_f, st_f, y_b, st_b = _rwkv_scan_fwd(ops_f, ops_b, RW, name="scan_fwd")

    post_fn = functools.partial(_f_post, hn)
    post_rows = [y_f, y_b, rl_r, k_f, k_b, rl_v, col(proj, "z_r"), o_mla, col(proj, "z_m")]
    post_params = [W["gn_g"], W["gn_b"], W["r_k"], seg, seg_t]
    ymg, yrg = _rowwise(post_fn, post_rows, post_params, [(MW, BF16), (RW, BF16)], tile=T, name="post")
    u_m = _mm(ymg, W["w_br_mla"], name="br_mla")
    u_r = _mm(yrg, W["w_br_rwkv"], name="br_rwkv")
    merge_rows = [u_m, u_r, col(proj, "gate_m"), col(proj, "gate_r")]
    (merged,) = _rowwise(lambda *t: (_f_merge(*t),), merge_rows, [], [(D, BF16)], tile=2 * T, name="merge")
    out = _mm(merged, W["w_out"], name="out_proj")

    def head(ob, xb, tb, g):
        yn, vjp = jax.vjp(_rms, ob, g)
        err = xb + yn - tb
        dy = err * (1.0 / D)
        d_ob, d_g = vjp(dy)
        loss = jnp.broadcast_to(0.5 * jnp.sum(err * err) * (1.0 / D), (1, LANES))
        return dy, d_ob, loss, d_g

    dy, d_out, loss, g_g_post = _rowwise(head, [out, x, target], [W["g_post"]], [(D, F32), (D, BF16)],
                                         [(1, LANES), (1, D)], tile=2 * T, name="head")
    d_merged = _mm(d_out, W["w_out"], tb=True, name="d_merged")
    g_w_out = _mm(merged, d_out, ta=True, out_dtype=BF16, name="g_w_out")

    def merge_bwd(u_m_b, u_r_b, g_m_b, g_r_b, dm):
        _, vjp = jax.vjp(_f_merge, u_m_b, u_r_b, g_m_b, g_r_b)
        du_m, du_r, dg_m, dg_r = vjp(dm)
        return du_m, du_r, jnp.concatenate([dg_m, dg_r], axis=1)

    d_u_m, d_u_r, d_proj = _rowwise(merge_bwd, merge_rows + [d_merged], [],
                                    [(D, BF16), (D, BF16), (2 * D, BF16, (None, d_in, lay["gate_m"][0]))], tile=T,
                                    name="merge_bwd")
    d_ymg = _mm(d_u_m, W["w_br_mla"], tb=True, name="d_ymg")
    d_yrg = _mm(d_u_r, W["w_br_rwkv"], tb=True, name="d_yrg")
    g_w_br_mla = _mm(ymg, d_u_m, ta=True, out_dtype=BF16, name="g_w_br_mla")
    g_w_br_rwkv = _mm(yrg, d_u_r, ta=True, out_dtype=BF16, name="g_w_br_rwkv")

    def post_bwd(*args):
        nr = len(post_rows)
        prim, dm, dr = args[:nr] + args[nr + 2:], args[nr], args[nr + 1]
        _, vjp = jax.vjp(post_fn, *prim)
        g = vjp((dm, dr))
        return g[0], g[2], g[3], g[5], g[7], jnp.concatenate([g[8], g[6]], axis=1), g[9], g[10], g[11]

    (d_y, d_r_bonus, d_k_bonus, d_v_bonus, d_o, d_proj, g_gn_g, g_gn_b, g_r_k) = _rowwise(
        post_bwd, post_rows + [d_ymg, d_yrg], post_params,
        [(RW, F32), (RW, F32), (RW, F32), (RW, F32), (MW, F32), (MW + RW, BF16, (d_proj, d_in, lay["z_m"][0]))],
        [(1, RW)] * 3, tile=T // 2, name="post_bwd")

    dscan = _rwkv_scan_bwd(ops_f, ops_b, st_f, st_b, d_y, RW, name="scan_bwd")
    dsc = {"f": dscan[:6], "b": dscan[6:]}

    d_q_att, d_k_att, d_v_att = _attention_bwd(qfull, kv, kr, o_mla, lse, d_o, hm, scale, tq=2 * T, name="attn_bwd")

    def rope_bwd(qraw_b, kr_in, cos_b, sin_b, dq_b, dk_b, dv_b, rot_b, rot_t_b):
        _, vjp = jax.vjp(lambda q_, k_: _f_rope(hm, q_, k_, cos_b, sin_b, rot_b, rot_t_b), qraw_b, kr_in)
        dkn = jnp.concatenate([dk_b[:, hh * QHEAD:hh * QHEAD + NOPE] for hh in range(hm)], axis=1)
        dkr = dk_b[:, NOPE:QHEAD]
        for hh in range(1, hm):
            dkr = dkr + dk_b[:, hh * QHEAD + NOPE:(hh + 1) * QHEAD]
        d_qraw, d_kr_in = vjp((dq_b, dkr))
        return d_qraw, jnp.concatenate([dkn, dv_b], axis=1), d_kr_in

    d_qraw, d_kv, d_kr_in = _rowwise(rope_bwd, [qraw, kr_view, cosx, sinx, d_q_att, d_k_att, d_v_att],
                                     [rot, rot_t], [(hm * QHEAD, BF16), (2 * MW, BF16), (LANES, F32)], tile=T,
                                     name="rope_bwd")
    d_qnorm = _mm(d_qraw, W["wq_b_t"], name="d_qn")
    d_kvnorm = _mm(d_kv, W["wkv_b"], tb=True, name="d_kvn")
    g_wq_b = _mm(d_qraw, qn, ta=True, out_dtype=BF16, name="g_wq_b")
    g_wkv_b = _mm(kvn, d_kv, ta=True, out_dtype=BF16, name="g_wkv_b")

    def mla_norm_bwd(q_a, kv_a, qg, kvg, dq, dk):
        _, vjp = jax.vjp(_f_mla_norm, q_a, kv_a, qg, kvg)
        d_q_a, d_kv_a, d_qg, d_kvg = vjp((dq, dk))
        return jnp.concatenate([d_q_a, d_kv_a], axis=1), d_qg, d_kvg

    d_proj, g_q_norm, g_kv_norm = _rowwise(
        lambda q_a, kv_a, dq, dk, qg, kvg: mla_norm_bwd(q_a, kv_a, qg, kvg, dq, dk),
        [col(proj, "q_a"), col(proj, "kv_a"), d_qnorm, d_kvnorm], [W["mla_q_norm"], W["mla_kv_norm"]],
        [(QR + KVR, BF16, (d_proj, d_in, lay["q_a"][0]))], [(1, QR), (1, KVR)], tile=2 * T, name="mla_norm_bwd")

    def pre_bwd(k_b_, tail_b, dlwf, dlwb, dkf, dkb, dkbon, daf, dab, dbf, dbb, drf, drb, drbon, dvf, dvb, dvbon,
                dkr, *params):
        w2, a2 = params[6], params[7]
        nt, tn = (((1,), (1,)), ((), ())), (((0,), (0,)), ((), ()))
        split = w2.shape[0]
        th = jnp.tanh(tail_b[:, :split])
        th_b, tail_h = th.astype(BF16), tail_b[:, split:].astype(BF16)
        zw = jnp.dot(th_b, w2, preferred_element_type=F32)
        za = jnp.dot(tail_h, a2, preferred_element_type=F32)
        _, vjp = jax.vjp(functools.partial(_f_rwkv_core, RW), k_b_, zw, za, *params[:6], params[8], params[9])
        g = vjp((dlwf, dlwb, dkf + dkbon, dkb + dkbon, daf + dab, dbf, dbb))
        d_zw, d_za = g[1].astype(BF16), g[2].astype(BF16)
        d_tail = (jnp.concatenate([lax.dot_general(d_zw, w2, nt, preferred_element_type=F32) * (1.0 - th * th),
                                   lax.dot_general(d_za, a2, nt, preferred_element_type=F32)], axis=1)
                  + jnp.concatenate([dkr, jnp.zeros((dkr.shape[0], TAIL - LANES), F32)], axis=1))
        g_w2 = lax.dot_general(th_b, d_zw, tn, preferred_element_type=F32)
        g_a2 = lax.dot_general(tail_h, d_za, tn, preferred_element_type=F32)
        d_rl = jnp.concatenate([drf + drb + drbon, g[0], dvf + dvb + dvbon, d_tail], axis=1)
        return (d_rl,) + tuple(g[3:9]) + (g_w2, g_a2)

    f_, b_ = dsc["f"], dsc["b"]
    pre_bwd_rows = [rl_k, rl_tail, f_[1], b_[1], f_[2], b_[2], d_k_bonus, f_[4], b_[4], f_[5], b_[5],
                    f_[0], b_[0], d_r_bonus, f_[3], b_[3], d_v_bonus, d_kr_in]
    (d_rl, g_w0_f, g_w0_b, g_a0_f, g_a0_b, g_k_k, g_k_a, g_w2cat, g_a2cat) = _rowwise(
        pre_bwd, pre_bwd_rows, pre_params, [(3 * RW + TAIL, F32)],
        [(1, RW)] * 6 + [W["w2cat"].shape, W["a2cat"].shape], tile=T // 2, name="rwkv_pre_bwd")
    d_proj, g_mu = _shift_lerp(shift_view, W["mu"], d_rl, (d_proj, lay["r"][0]), name="shift_bwd")
    small = dict(wq_b=g_wq_b, wkv_b=g_wkv_b, w2cat=g_w2cat, a2cat=g_a2cat, w_br_mla=g_w_br_mla,
                 w_br_rwkv=g_w_br_rwkv, w_out=g_w_out)
    if exchange is None:
        received = None
        g_w_in = _mm(d_proj, h, ta=True, out_dtype=BF16, tn_cap=1024, name="g_w_in")
        d_h = _mm(d_proj, W["w_in_t"], tn_cap=1024, name="d_h")
    else:
        slabs = _restore_rest(small, dims)
        slabs = [slabs[n] for n in _MATS[1:]]
        g_w_in, *got = _mm(d_proj, h, ta=True, out_dtype=BF16, tn_cap=1024, ride=_sibling_swap_plan(slabs),
                           name="g_w_in")
        sums = [_pair_add(exchange[1], s, t, name="pair_add_" + n) for n, s, t in zip(_MATS[1:], slabs, got)]
        g_w_in = _restore_w_in(g_w_in, dims)
        d_h, *received = _mm(d_proj, W["w_in_t"], tn_cap=1024, name="d_h",
                             ride=_join_plans(_chip_exchange_plan(sums), _sibling_swap_plan([g_w_in])))
        small = {}

    def pre_norm_bwd(xb, dyb, dhb, g):
        _, vjp = jax.vjp(_rms, xb, g)
        dx, dg = vjp(dhb)
        return dyb + dx, dg

    grad_x, g_g_pre = _rowwise(pre_norm_bwd, [x, dy, d_h], [W["g_pre"]], [(D, F32)], [(1, D)], tile=2 * T,
                               name="pre_norm_bwd")

    grads = dict(g_pre=g_g_pre, w_in=g_w_in, mla_q_norm=g_q_norm, mla_kv_norm=g_kv_norm, mu=g_mu, w0_f=g_w0_f,
                 w0_b=g_w0_b, a0_f=g_a0_f, a0_b=g_a0_b, k_k=g_k_k, k_a=g_k_a, r_k=g_r_k, gn_g=g_gn_g, gn_b=g_gn_b,
                 g_post=g_g_post, **small)
    return loss[0, 0], grad_x, grads, received


_MATS = ["w_in", "mla_wq_b", "mla_wkv_b", "rwkv_w2_f", "rwkv_w2_b", "rwkv_a2_f", "rwkv_a2_b", "w_br_mla",
         "w_br_rwkv", "w_out"]
_ROW_SHARDED = ("w_out",)
_TRANSPOSED = ("w_in", "mla_wq_b")
_VECS = ["g_pre", "mla_q_norm", "mla_kv_norm", "rwkv_mu", "rwkv_w0_f", "rwkv_w0_b", "rwkv_a0_f", "rwkv_a0_b",
         "rwkv_k_k", "rwkv_k_a", "rwkv_r_k", "rwkv_gn_g", "rwkv_gn_b", "g_post"]
_WEIGHTS = ["g_pre", "w_in", "mla_q_norm", "mla_wq_b", "mla_kv_norm", "mla_wkv_b", "rwkv_mu", "rwkv_w0_f",
            "rwkv_w2_f", "rwkv_w0_b", "rwkv_w2_b", "rwkv_a0_f", "rwkv_a2_f", "rwkv_a0_b", "rwkv_a2_b", "rwkv_k_k",
            "rwkv_k_a", "rwkv_r_k", "rwkv_gn_g", "rwkv_gn_b", "w_br_mla", "w_br_rwkv", "w_out", "g_post"]

def _direct_gather_plan(src):
    def phases(src_refs, out_refs, sem_refs):
        (src_ref,), (out_ref,), sems, local_sem = src_refs, out_refs, sem_refs[:2], sem_refs[2]
        x, y, c = lax.axis_index("x"), lax.axis_index("y"), lax.axis_index("c")
        me = 4 * x + 2 * y + c
        flip = lambda v, bit: (1 - v) if bit else v
        peers = [(flip(x, d & 4), flip(y, d & 2), flip(c, d & 1)) for d in range(1, N_DEV)]
        own = lambda: pltpu.make_async_copy(src_ref, out_ref.at[me], local_sem)
        send = lambda d: _remote(src_ref, out_ref.at[me], sems, d, peers[d])

        def first():
            own().start()
            for d in range(N_DEV - 1):
                send(d).start()

        def last():
            for d, (px, py, pc) in enumerate(peers):
                blk = out_ref.at[4 * px + 2 * py + pc]
                _remote(blk, blk, sems, d, (x, y, c)).wait_recv()
            for d in range(N_DEV - 1):
                send(d).wait_send()
            own().wait()

        return first, (lambda: None), last

    return [src], [jax.ShapeDtypeStruct((N_DEV,) + src.shape, src.dtype)], [(N_DEV - 1,), (N_DEV - 1,), ()], phases


def _remote(src, dst, sems, key, to):
    send_sems, recv_sems = sems
    return pltpu.make_async_remote_copy(src_ref=src, dst_ref=dst, send_sem=send_sems.at[key], recv_sem=recv_sems.at[key],
                                        device_id=to, device_id_type=pl.DeviceIdType.MESH)


def _run_exchange(plan, *, name):
    srcs, out_shapes, sem_shapes, phases = plan
    n, m = len(srcs), len(out_shapes)

    def body(*refs):
        for phase in phases(refs[:n], refs[n:n + m], refs[n + m:]):
            phase()

    return pl.pallas_call(
        body, name=name, out_shape=out_shapes,
        in_specs=[pl.BlockSpec(memory_space=pl.ANY)] * n, out_specs=[pl.BlockSpec(memory_space=pl.ANY)] * m,
        scratch_shapes=[pltpu.SemaphoreType.DMA(s) for s in sem_shapes],
    )(*srcs)


def _join_plans(p, q):
    (srcs_p, outs_p, sems_p, phases_p), (srcs_q, outs_q, sems_q, phases_q) = p, q

    def phases(src_refs, out_refs, sem_refs):
        a = phases_p(src_refs[:len(srcs_p)], out_refs[:len(outs_p)], sem_refs[:len(sems_p)])
        b = phases_q(src_refs[len(srcs_p):], out_refs[len(outs_p):], sem_refs[len(sems_p):])

        def both(fa, fb):
            def run():
                fa()
                fb()
            return run

        return tuple(both(fa, fb) for fa, fb in zip(a, b))

    return list(srcs_p) + list(srcs_q), list(outs_p) + list(outs_q), list(sems_p) + list(sems_q), phases


def _gather_plan(srcs):
    n = len(srcs)

    def phases(src_refs, out_refs, sem_refs):
        sems, local_sems = sem_refs[:2], sem_refs[2]
        x, y, c = lax.axis_index("x"), lax.axis_index("y"), lax.axis_index("c")
        idx = lambda px, py, pc: 4 * px + 2 * py + pc
        me, sibling = (x, y, c), (x, y, 1 - c)
        chips = [(1 - x, y), (x, 1 - y), (1 - x, 1 - y)]
        own = lambda a: pltpu.make_async_copy(src_refs[a], out_refs[a].at[idx(*me)], local_sems.at[a])
        to_sibling = lambda a: _remote(src_refs[a], out_refs[a].at[idx(*me)], sems, (0, a), sibling)
        to_chip = lambda a, j: _remote(src_refs[a], out_refs[a].at[idx(*me)], sems, (1 + j, a), (*chips[j], c))
        landed = lambda a, j: out_refs[a].at[idx(*chips[j], c)]
        passed_on = lambda a, j: _remote(landed(a, j), landed(a, j), sems, (4 + j, a), sibling)

        def first():
            for a in range(n):
                own(a).start()
                to_sibling(a).start()
                for j in range(3):
                    to_chip(a, j).start()

        def middle():
            for j in range(3):
                for a in range(n):
                    _remote(landed(a, j), landed(a, j), sems, (1 + j, a), me).wait_recv()
                    passed_on(a, j).start()

        def last():
            for a in range(n):
                blk = out_refs[a].at[idx(*sibling)]
                _remote(blk, blk, sems, (0, a), me).wait_recv()
                for j in range(3):
                    blk = out_refs[a].at[idx(*chips[j], 1 - c)]
                    _remote(blk, blk, sems, (4 + j, a), me).wait_recv()
            for a in range(n):
                to_sibling(a).wait_send()
                for j in range(3):
                    to_chip(a, j).wait_send()
                    passed_on(a, j).wait_send()
                own(a).wait()

        return first, middle, last

    return srcs, [jax.ShapeDtypeStruct((N_DEV,) + s.shape, s.dtype) for s in srcs], [(7, n), (7, n), (n,)], phases


def _sibling_swap_plan(srcs):
    n = len(srcs)

    def phases(src_refs, out_refs, sems):
        x, y, c = lax.axis_index("x"), lax.axis_index("y"), lax.axis_index("c")
        copies = lambda: [_remote(src_refs[a].at[2 * q + 1 - c], out_refs[a].at[q], sems, (q, a), (x, y, 1 - c))
                          for a in range(n) for q in range(4)]

        def first():
            for cp in copies():
                cp.start()

        def last():
            for cp in copies():
                cp.wait()

        return first, (lambda: None), last

    return srcs, [jax.ShapeDtypeStruct((4,) + s.shape[1:], s.dtype) for s in srcs], [(4, n), (4, n)], phases


def _chip_exchange_plan(srcs):
    n = len(srcs)

    def phases(src_refs, out_refs, sem_refs):
        sems, local_sems = sem_refs[:2], sem_refs[2]
        x, y, c = lax.axis_index("x"), lax.axis_index("y"), lax.axis_index("c")
        mine = 2 * x + y
        chips = [(1 - x, y), (x, 1 - y), (1 - x, 1 - y)]
        own = lambda a: pltpu.make_async_copy(src_refs[a].at[mine], out_refs[a].at[mine], local_sems.at[a])
        send = lambda a, j: _remote(src_refs[a].at[2 * chips[j][0] + chips[j][1]], out_refs[a].at[mine], sems, (j, a),
                                    (*chips[j], c))

        def first():
            for a in range(n):
                own(a).start()
                for j in range(3):
                    send(a, j).start()

        def last():
            for j in range(3):
                for a in range(n):
                    blk = out_refs[a].at[2 * chips[j][0] + chips[j][1]]
                    _remote(blk, blk, sems, (j, a), (x, y, c)).wait_recv()
            for a in range(n):
                for j in range(3):
                    send(a, j).wait_send()
                own(a).wait()

        return first, (lambda: None), last

    return srcs, [jax.ShapeDtypeStruct(s.shape, s.dtype) for s in srcs], [(3, n), (3, n), (n,)], phases


def _pair_add(core, g, got, *, name):
    q, r, c = got.shape
    tr, tc = _tile2d(r, c, cap=1024)

    def body(core_ref, a_ref, b_ref, o_ref):
        o_ref[...] = (a_ref[...].astype(F32) + b_ref[...].astype(F32)).astype(BF16)

    blk = pl.BlockSpec((1, tr, tc), lambda i, j, k, core_ref: (i, j, k))
    mine = pl.BlockSpec((1, tr, tc), lambda i, j, k, core_ref: (2 * i + core_ref[0], j, k))
    return pl.pallas_call(
        body, name=name, out_shape=jax.ShapeDtypeStruct(got.shape, BF16),
        grid_spec=pltpu.PrefetchScalarGridSpec(num_scalar_prefetch=1, grid=(q, r // tr, c // tc),
                                               in_specs=[mine, blk], out_specs=blk),
        compiler_params=_cparams(("parallel", "parallel", "parallel")))(core, g, got)


def _adamw(recv, w, m, v, *, name):
    r, c = w.shape
    n_terms = recv.shape[0]
    tr, tc = _tile2d(r, c, cap=512)

    def body(g_ref, w_ref, m_ref, v_ref, go_ref, d_ref, mo_ref, vo_ref):
        g = g_ref[0].astype(F32)
        for k in range(1, n_terms):
            g = g + g_ref[k].astype(F32)
        m_new = ADAM_B1 * m_ref[...] + (1.0 - ADAM_B1) * g
        v_new = ADAM_B2 * v_ref[...] + (1.0 - ADAM_B2) * (g * g)
        m_hat = m_new / (1.0 - ADAM_B1 ** ADAM_STEP)
        v_hat = v_new / (1.0 - ADAM_B2 ** ADAM_STEP)
        go_ref[...] = g
        d_ref[...] = -ADAM_LR * (m_hat / (jnp.sqrt(v_hat) + ADAM_EPS) + ADAM_WD * w_ref[...])
        mo_ref[...] = m_new
        vo_ref[...] = v_new

    blk = pl.BlockSpec((tr, tc), lambda i, j: (i, j))
    return pl.pallas_call(
        body, name=name, grid=(r // tr, c // tc),
        in_specs=[pl.BlockSpec((n_terms, tr, tc), lambda i, j: (0, i, j)), blk, blk, blk], out_specs=[blk] * 4,
        out_shape=[jax.ShapeDtypeStruct((r, c), F32)] * 4, compiler_params=_cparams(("parallel", "parallel")),
    )(recv, w, m, v)


def _tile2d(r, c, cap=256):
    if r <= cap:
        return r, c
    for t in range(cap - cap % BF16_ROWS, 0, -BF16_ROWS):
        if r % t == 0:
            return t, c
    return r, _pick(c, cap)


def _pack(pieces):
    total = sum(p.shape[0] for p in pieces)
    pad = (-total) % (8 * LANES)
    flat = jnp.concatenate(list(pieces) + [jnp.zeros((pad,), F32)])
    return flat.reshape(-1, LANES)


def _unpack(flat, sizes):
    flat = flat.reshape(-1)
    out, o = [], 0
    for n in sizes:
        out.append(flat[o:o + n])
        o += n
    return out


def _prepare_weights(full, vec, dims):
    rest = {n: t for n, t in full.items() if n != "w_in"}
    return {"w_in_t": _prepare_w_in(full["w_in"], dims), **_prepare_rest(rest, dims), **_prepare_vectors(vec, dims)}


def _prepare_w_in(slabs, dims):
    D = dims["D"]
    flat = slabs.reshape(-1, D)
    parts, pos = [], 0
    for orig_off, width, perm_off in sorted(dims["segs"], key=lambda t: t[2]):
        if perm_off > pos:
            parts.append(jnp.zeros((perm_off - pos, D), BF16))
        parts.append(flat[orig_off:orig_off + width])
        pos = perm_off + width
    if dims["d_in_perm"] > pos:
        parts.append(jnp.zeros((dims["d_in_perm"] - pos, D), BF16))
    return jnp.concatenate(parts, axis=0)


def _prepare_rest(full, dims):
    hm, hr, hn, rank = dims["hm"], dims["hr"], dims["hn"], dims["rank"]
    QR, KVR = dims["QR"], dims["KVR"]
    RW, TAIL = hr * hn, dims["TAIL"]
    full = {n: (t.reshape(-1, t.shape[2]) if n in _ROW_SHARDED + _TRANSPOSED
                else t.transpose(1, 0, 2).reshape(t.shape[1], -1)) for n, t in full.items()}
    wq = full["mla_wq_b"].reshape(hm, NOPE + ROPE, QR)
    wq = jnp.concatenate([wq, jnp.zeros((hm, QHEAD - NOPE - ROPE, QR), BF16)], axis=1).reshape(hm * QHEAD, QR)
    wkv = full["mla_wkv_b"].reshape(KVR, hm, 2, NOPE).transpose(0, 2, 1, 3).reshape(KVR, 2 * hm * NOPE)
    z = lambda rows: jnp.zeros((rows, RW), BF16)
    f = lambda nme: full[nme]
    split = ROPE + 2 * rank
    assert split % LANES == 0, split
    w2cat = jnp.concatenate([
        jnp.concatenate([z(ROPE), f("rwkv_w2_f"), z(rank)], axis=0),
        jnp.concatenate([z(ROPE + rank), f("rwkv_w2_b")], axis=0)], axis=1)
    a2cat = jnp.concatenate([
        jnp.concatenate([f("rwkv_a2_f"), z(TAIL - split - rank)], axis=0),
        jnp.concatenate([z(rank), f("rwkv_a2_b"), z(TAIL - split - 2 * rank)], axis=0)], axis=1)
    return dict(wq_b_t=wq, wkv_b=wkv, w2cat=w2cat, a2cat=a2cat, w_br_mla=full["w_br_mla"],
                w_br_rwkv=full["w_br_rwkv"], w_out=full["w_out"])


def _prepare_vectors(vec, dims):
    rank, RW, TAIL = dims["rank"], dims["hr"] * dims["hn"], dims["TAIL"]
    mu = vec["rwkv_mu"]
    mu_p = jnp.concatenate([mu[:3 * RW], jnp.zeros((ROPE,), F32), mu[3 * RW:],
                            jnp.zeros((TAIL - ROPE - 4 * rank,), F32)])
    row = lambda t: t.reshape(1, -1)
    return dict(
        mu=row(mu_p), g_pre=row(vec["g_pre"]), g_post=row(vec["g_post"]), mla_q_norm=row(vec["mla_q_norm"]),
        mla_kv_norm=row(vec["mla_kv_norm"]), w0_f=row(vec["rwkv_w0_f"]), w0_b=row(vec["rwkv_w0_b"]),
        a0_f=row(vec["rwkv_a0_f"]), a0_b=row(vec["rwkv_a0_b"]), k_k=row(vec["rwkv_k_k"]), k_a=row(vec["rwkv_k_a"]),
        r_k=row(vec["rwkv_r_k"]), gn_g=row(vec["rwkv_gn_g"]), gn_b=row(vec["rwkv_gn_b"]))


def _restore_grads(g, dims):
    return {"w_in": _restore_w_in(g["w_in"], dims), **_restore_rest(g, dims), **_restore_vectors(g, dims)}


def _restore_w_in(gw, dims):
    parts = [gw[perm_off:perm_off + width] for _, width, perm_off in sorted(dims["segs"])]
    return jnp.concatenate(parts, axis=0).reshape(N_DEV, dims["d_in"] // N_DEV, gw.shape[1])


def _restore_rest(g, dims):
    hm, hr, hn, rank = dims["hm"], dims["hr"], dims["hn"], dims["rank"]
    QR, KVR, RW = dims["QR"], dims["KVR"], hr * hn
    wq = g["wq_b"].reshape(hm, QHEAD, QR)[:, :NOPE + ROPE].reshape(N_DEV, -1, QR)
    wkv = g["wkv_b"].reshape(KVR, 2, hm, NOPE).transpose(0, 2, 1, 3).reshape(KVR, 2 * hm * NOPE)
    lo = lambda t, first, half: t[first:first + rank, half * RW:(half + 1) * RW].astype(BF16)
    cols = lambda t: t.reshape(t.shape[0], N_DEV, -1).transpose(1, 0, 2)
    return dict(
        mla_wq_b=wq, mla_wkv_b=cols(wkv), rwkv_w2_f=cols(lo(g["w2cat"], ROPE, 0)),
        rwkv_w2_b=cols(lo(g["w2cat"], ROPE + rank, 1)), rwkv_a2_f=cols(lo(g["a2cat"], 0, 0)),
        rwkv_a2_b=cols(lo(g["a2cat"], rank, 1)), w_br_mla=cols(g["w_br_mla"]), w_br_rwkv=cols(g["w_br_rwkv"]),
        w_out=g["w_out"].reshape(N_DEV, -1, g["w_out"].shape[1]))


def _restore_vectors(g, dims):
    rank, RW = dims["rank"], dims["hr"] * dims["hn"]
    mu = g["mu"][0]
    out = dict(
        rwkv_mu=jnp.concatenate([mu[:3 * RW], mu[3 * RW + ROPE:3 * RW + ROPE + 4 * rank]]),
        g_pre=g["g_pre"][0], g_post=g["g_post"][0], mla_q_norm=g["mla_q_norm"][0], mla_kv_norm=g["mla_kv_norm"][0],
        rwkv_w0_f=g["w0_f"][0], rwkv_w0_b=g["w0_b"][0], rwkv_a0_f=g["a0_f"][0], rwkv_a0_b=g["a0_b"][0],
        rwkv_k_k=g["k_k"][0], rwkv_k_a=g["k_a"][0], rwkv_r_k=g["r_k"][0], rwkv_gn_g=g["gn_g"][0],
        rwkv_gn_b=g["gn_b"][0])
    return out


def _dims(inp):
    D = inp["x"].shape[-1]
    QR, KVR = inp["mla_q_norm"].shape[0], inp["mla_kv_norm"].shape[0]
    hm = inp["mla_wq_b"].shape[1] * N_DEV // (NOPE + ROPE)
    hr, hn = inp["rwkv_r_k"].shape
    rank = inp["rwkv_w2_f"].shape[0]
    MW, RW = hm * VDIM, hr * hn
    TAIL = -(-(ROPE + 4 * rank) // LANES) * LANES
    orig, o = {}, 0
    for nme, w in (("q_a", QR), ("kv_a", KVR), ("k_rope", ROPE), ("rkv", 3 * RW), ("lora", 4 * rank), ("z_m", MW),
                   ("z_r", RW), ("gate_m", D), ("gate_r", D)):
        orig[nme] = (o, w)
        o += w
    assert o == inp["w_in"].shape[1] * N_DEV
    lay, d_in_perm = _layout(D, MW, RW, TAIL, QR, KVR)
    perm_off = dict(q_a=lay["q_a"][0], kv_a=lay["kv_a"][0], k_rope=lay["tail"][0], rkv=lay["r"][0],
                    lora=lay["tail"][0] + ROPE, z_m=lay["z_m"][0], z_r=lay["z_r"][0], gate_m=lay["gate_m"][0],
                    gate_r=lay["gate_r"][0])
    segs = [(orig[nme][0], orig[nme][1], perm_off[nme]) for nme in orig]
    return dict(D=D, QR=QR, KVR=KVR, hm=hm, hr=hr, hn=hn, rank=rank, TAIL=TAIL, segs=segs, d_in=o,
                d_in_perm=d_in_perm)


def kernel(x, g_pre, w_in, mla_q_norm, mla_wq_b, mla_kv_norm, mla_wkv_b, rwkv_mu, rwkv_w0_f, rwkv_w2_f, rwkv_w0_b, rwkv_w2_b, rwkv_a0_f, rwkv_a2_f, rwkv_a0_b, rwkv_a2_b, rwkv_k_k, rwkv_k_a, rwkv_r_k, rwkv_gn_g, rwkv_gn_b, w_br_mla, w_br_rwkv, w_out, g_post, loss_target, m_g_pre, m_w_in, m_mla_q_norm, m_mla_wq_b, m_mla_kv_norm, m_mla_wkv_b, m_rwkv_mu, m_rwkv_w0_f, m_rwkv_w2_f, m_rwkv_w0_b, m_rwkv_w2_b, m_rwkv_a0_f, m_rwkv_a2_f, m_rwkv_a0_b, m_rwkv_a2_b, m_rwkv_k_k, m_rwkv_k_a, m_rwkv_r_k, m_rwkv_gn_g, m_rwkv_gn_b, m_w_br_mla, m_w_br_rwkv, m_w_out, m_g_post, v_g_pre, v_w_in, v_mla_q_norm, v_mla_wq_b, v_mla_kv_norm, v_mla_wkv_b, v_rwkv_mu, v_rwkv_w0_f, v_rwkv_w2_f, v_rwkv_w0_b, v_rwkv_w2_b, v_rwkv_a0_f, v_rwkv_a2_f, v_rwkv_a0_b, v_rwkv_a2_b, v_rwkv_k_k, v_rwkv_k_a, v_rwkv_r_k, v_rwkv_gn_g, v_rwkv_gn_b, v_w_br_mla, v_w_br_rwkv, v_w_out, v_g_post):
    inp = dict(locals())
    dims = _dims(inp)
    stored = lambda t, n: t.T if n in _TRANSPOSED else t
    assert _MATS[0] == "w_in"
    shards = [stored(inp[n], n).astype(BF16) for n in _MATS]
    core = lax.axis_index("c").astype(jnp.int32).reshape(1)
    (w_in_slabs,) = _run_exchange(_gather_plan(shards[:1]), name="gather_w_in")
    W = {"w_in_t": _prepare_w_in(w_in_slabs, dims), **_prepare_vectors({n: inp[n] for n in _VECS}, dims)}
    loss, grad_x, g, recv_rest = _local_grads(x[0], loss_target[0], W, dims, exchange=(shards[1:], core))

    new = {}
    *recv_rest, got = recv_rest
    g_w_in, g = g["w_in"], _restore_vectors(g, dims)
    vsizes = [inp[n].size for n in _VECS] + [1]
    vflat = lambda prefix, src, last: _pack([src[prefix + n].reshape(-1) for n in _VECS] + [last])
    one = jnp.zeros((1,), F32)
    recv_w_in, vrecv = _run_exchange(
        _join_plans(_chip_exchange_plan([_pair_add(core, g_w_in, got, name="pair_add_w_in")]),
                    _direct_gather_plan(vflat("", g, loss.reshape(1)))), name="scatter_w_in")
    for n, t in zip(_MATS, [recv_w_in] + recv_rest):
        out = _adamw(t, stored(inp[n], n), stored(inp["m_" + n], n), stored(inp["v_" + n], n), name="adamw_" + n)
        new[n] = [stored(o, n) for o in out]

    vout = _adamw(vrecv, vflat("", inp, one), vflat("m_", inp, one), vflat("v_", inp, one), name="adamw_vectors")
    vparts = [_unpack(t, vsizes) for t in vout]
    for i, n in enumerate(_VECS):
        new[n] = [vp[i].reshape(inp[n].shape) for vp in vparts]
    loss = vparts[0][-1].reshape(())

    outs = [loss, grad_x[None]]
    for k in range(4):
        outs += [new[n][k] for n in _WEIGHTS]
    return tuple(outs)
```

```python
import functools
import math

import jax
import jax.numpy as jnp
from jax import lax
from jax.experimental import pallas as pl
from jax.experimental.pallas import tpu as pltpu

F32 = jnp.float32
BF16 = jnp.bfloat16

N_DEV = 8
LANES = 128
BF16_ROWS = 16
NOPE, ROPE, VDIM = 128, 64, 128
QHEAD = 256
ROPE_THETA = 10000.0
NORM_EPS = 1e-6
GN_EPS = 64e-5
CHUNK = 64
SUB = 16
VMEM_LIMIT = 56 * 1024 * 1024

ADAM_LR, ADAM_B1, ADAM_B2, ADAM_EPS, ADAM_WD, ADAM_STEP = 0.001, 0.9, 0.999, 1e-08, 0.01, 10


def _cparams(sem):
    return pltpu.CompilerParams(dimension_semantics=sem, vmem_limit_bytes=VMEM_LIMIT)


def _pick(n, cap):
    if n <= cap:
        return n
    for t in range(cap - cap % LANES, 0, -LANES):
        if n % t == 0:
            return t
    raise ValueError(f"no tile for {n} under {cap}")


def _mm(a, b, *, ta=False, tb=False, out_dtype=F32, name, tm_cap=1024, tn_cap=512, tk_cap=2048, ride=None):
    K, M = a.shape if ta else a.shape[::-1]
    N = b.shape[0] if tb else b.shape[1]
    assert (b.shape[1] if tb else b.shape[0]) == K, (a.shape, b.shape, ta, tb)
    tm, tn, tk = _pick(M, tm_cap), _pick(N, tn_cap), _pick(K, tk_cap)
    nj, nk = N // tn, K // tk
    steps = (M // tm) * nj * nk
    dn = (((0 if ta else 1,), (1 if tb else 0,)), ((), ()))
    srcs, extra_shapes, sem_shapes, phases = ride if ride else ((), (), (), None)
    n_src, n_extra = len(srcs), len(extra_shapes)

    def body(*refs):
        a_ref, b_ref, o_ref = refs[0], refs[1], refs[2 + n_src]
        acc_ref = refs[3 + n_src + n_extra]
        k = pl.program_id(2)
        if ride:
            step = (pl.program_id(0) * nj + pl.program_id(1)) * nk + k
            first, middle, last = phases(refs[2:2 + n_src], refs[3 + n_src:3 + n_src + n_extra],
                                         refs[4 + n_src + n_extra:])
            pl.when(step == 0)(first)
            pl.when(step == (steps * 15) // 16)(middle)
        p = lax.dot_general(a_ref[...], b_ref[...], dn, preferred_element_type=F32)

        @pl.when(k == 0)
        def _():
            acc_ref[...] = p

        @pl.when(k > 0)
        def _():
            acc_ref[...] += p

        @pl.when(k == nk - 1)
        def _():
            o_ref[...] = acc_ref[...].astype(out_dtype)

        if ride:
            pl.when(step == steps - 1)(last)

    a_spec = pl.BlockSpec((tk, tm), lambda i, j, k: (k, i)) if ta else pl.BlockSpec((tm, tk), lambda i, j, k: (i, k))
    b_spec = pl.BlockSpec((tn, tk), lambda i, j, k: (j, k)) if tb else pl.BlockSpec((tk, tn), lambda i, j, k: (k, j))
    hbm = pl.BlockSpec(memory_space=pl.ANY)
    out = pl.pallas_call(
        body, name=name, grid=(M // tm, nj, nk),
        in_specs=[a_spec, b_spec] + [hbm] * n_src,
        out_specs=[pl.BlockSpec((tm, tn), lambda i, j, k: (i, j))] + [hbm] * n_extra,
        out_shape=[jax.ShapeDtypeStruct((M, N), out_dtype)] + list(extra_shapes),
        scratch_shapes=[pltpu.VMEM((tm, tn), F32)] + [pltpu.SemaphoreType.DMA(s) for s in sem_shapes],
        compiler_params=_cparams(("arbitrary",) * 3 if ride else ("parallel", "parallel", "arbitrary")),
    )(a, b, *srcs)
    return out if ride else out[0]


def _view(arr, off, width):
    assert off % width == 0, (off, width)
    return (arr, off // width, width)


def _rowwise(fn, rows, params, out_rows, out_accs=(), *, tile, name):
    rows = [r if isinstance(r, tuple) else (r, 0, r.shape[1]) for r in rows]
    S = rows[0][0].shape[0]
    T = min(tile, S)
    assert S % T == 0
    n_rows, n_par, n_out = len(rows), len(params), len(out_rows)
    into = [o[2] if len(o) == 3 else None for o in out_rows]
    carried = [t[0] for t in into if t is not None and t[0] is not None]

    def body(*refs):
        ins = [r[...] for r in refs[:n_rows + n_par]]
        outs = fn(*ins)
        out_refs = refs[n_rows + n_par + len(carried):]
        for o_ref, val in zip(out_refs[:n_out], outs[:n_out]):
            o_ref[...] = val.astype(o_ref.dtype)
        i = pl.program_id(0)
        for o_ref, val in zip(out_refs[n_out:], outs[n_out:]):
            @pl.when(i == 0)
            def _(o_ref=o_ref, val=val):
                o_ref[...] = val

            @pl.when(i > 0)
            def _(o_ref=o_ref, val=val):
                o_ref[...] += val

    in_specs = [pl.BlockSpec((T, w), functools.partial(lambda i, cb: (i, cb), cb=cb)) for _, cb, w in rows]
    in_specs += [pl.BlockSpec(p.shape, lambda i: (0, 0)) for p in params]
    in_specs += [pl.BlockSpec(memory_space=pl.ANY)] * len(carried)
    out_specs, out_shape, aliases = [], [], {}
    for k, (o, t) in enumerate(zip(out_rows, into)):
        w, dt = o[0], o[1]
        if t is None:
            out_specs.append(pl.BlockSpec((T, w), lambda i: (i, 0)))
            out_shape.append(jax.ShapeDtypeStruct((S, w), dt))
            continue
        buf, total, first = t
        assert first % w == 0
        out_specs.append(pl.BlockSpec((T, w), functools.partial(lambda i, cb: (i, cb), cb=first // w)))
        out_shape.append(jax.ShapeDtypeStruct((S, total), dt))
        if buf is not None:
            aliases[n_rows + n_par + len(aliases)] = k
    out_specs += [pl.BlockSpec(s, lambda i: (0, 0)) for s in out_accs]
    out_shape += [jax.ShapeDtypeStruct(s, F32) for s in out_accs]
    return pl.pallas_call(
        body, name=name, grid=(S // T,), in_specs=in_specs, out_specs=out_specs, out_shape=out_shape,
        input_output_aliases=aliases, compiler_params=_cparams(("arbitrary",)),
    )(*[r[0] for r in rows], *params, *carried)


def _mm_sel(x, sel2):
    hi = x.astype(BF16)
    lo = (x - hi.astype(F32)).astype(BF16)
    return jnp.dot(jnp.concatenate([hi, lo], axis=1), sel2, preferred_element_type=F32)


@jax.custom_vjp
def _sel(x, sel, sel_t):
    return _mm_sel(x, sel)


def _sel_fwd(x, sel, sel_t):
    return _mm_sel(x, sel), (sel, sel_t)


def _sel_bwd(res, ct):
    sel, sel_t = res
    return _mm_sel(ct, sel_t), jnp.zeros_like(sel), jnp.zeros_like(sel_t)


_sel.defvjp(_sel_fwd, _sel_bwd)


def _rms(x, g):
    return x * lax.rsqrt(jnp.mean(x * x, axis=-1, keepdims=True) + NORM_EPS) * g


def _sigmoid(x):
    return 1.0 / (1.0 + jnp.exp(-x))


def _silu(x):
    return x * _sigmoid(x)


def _softplus(x):
    return jnp.maximum(x, 0.0) + jnp.log(1.0 + jnp.exp(-jnp.abs(x)))


def _f_mla_norm(q_a, kv_a, qg, kvg):
    return _rms(q_a, qg), _rms(kv_a, kvg)


def _f_rope(hm, qraw, kr_in, cosx, sinx, rot, rot_t):
    def rope(t):
        return t * cosx + _sel(t, rot, rot_t) * sinx
    parts = []
    for h in range(hm):
        parts.append(qraw[:, h * QHEAD:h * QHEAD + NOPE])
        parts.append(rope(qraw[:, h * QHEAD + NOPE:(h + 1) * QHEAD]))
    return jnp.concatenate(parts, axis=1), rope(kr_in)


def _f_rwkv_pre(rw, k, tail, w0f, w0b, a0f, a0b, k_k, k_a, w2cat, a2cat, seg, seg_t):
    split = w2cat.shape[0]
    zw = jnp.dot(jnp.tanh(tail[:, :split]).astype(BF16), w2cat, preferred_element_type=F32)
    za = jnp.dot(tail[:, split:].astype(BF16), a2cat, preferred_element_type=F32)
    return _f_rwkv_core(rw, k, zw, za, w0f, w0b, a0f, a0b, k_k, k_a, seg, seg_t)


def _f_rwkv_core(rw, k, zw, za, w0f, w0b, a0f, a0b, k_k, k_a, seg, seg_t):
    lw_f = -jnp.exp(-_softplus(-(w0f + zw[:, :rw])) - 0.5)
    lw_b = -jnp.exp(-_softplus(-(w0b + zw[:, rw:])) - 0.5)
    a_f = _sigmoid(a0f + za[:, :rw])
    a_b = _sigmoid(a0b + za[:, rw:])
    kk = k * k_k
    nrm = jnp.sqrt(_sel(_sel(kk * kk, seg, seg_t), seg_t, seg))
    kk = kk / jnp.maximum(nrm, 1e-12)
    k_f = k * (1.0 + (a_f - 1.0) * k_a)
    k_b = k * (1.0 + (a_b - 1.0) * k_a)
    return lw_f, lw_b, k_f, k_b, -kk, kk * a_f, kk * a_b


def _f_post(hn, y_f, y_b, r, k_f, k_b, v, z_r, o_mla, z_m, gn_g, gn_b, r_k, seg, seg_t):
    segsum = lambda t: _sel(_sel(t, seg, seg_t), seg_t, seg)
    y = y_f + y_b
    mu = segsum(y) * (1.0 / hn)
    yc = y - mu
    var = segsum(yc * yc) * (1.0 / hn)
    yn = yc * lax.rsqrt(var + GN_EPS) * gn_g + gn_b
    bonus = segsum(r * (k_f + k_b) * r_k) * v
    return o_mla * _silu(z_m), (yn + bonus) * _silu(z_r)


def _f_merge(u_m, u_r, g_m, g_r):
    return _sigmoid(g_m) * u_m + _sigmoid(g_r) * u_r


_NN = ((2,), (1,))
_NT = ((2,), (2,))
_TN = ((1,), (1,))

_SCAN_PASSES = {"cum": 2, "gram": 3, "solve": 1, "apply": 1, "state": 1}


def _hdot_raw(passes, x, y, dims):
    dn = (dims, ((0,), (0,)))
    d = lambda p, q: lax.dot_general(p, q, dn, preferred_element_type=F32)
    xh = x.astype(BF16)
    yh = y.astype(BF16)
    if passes == 1:
        return d(xh, yh)
    yl = (y - yh.astype(F32)).astype(BF16)
    kx, ky = (1 if dims == _TN else 2), (2 if dims == _NT else 1)
    depth = x.shape[kx]
    if all(axis == 1 or depth % LANES == 0 for axis in (kx, ky)):
        if passes == 2:
            return d(jnp.concatenate([xh, xh], axis=kx), jnp.concatenate([yh, yl], axis=ky))
        xl = (x - xh.astype(F32)).astype(BF16)
        return d(jnp.concatenate([xh, xl, xh], axis=kx), jnp.concatenate([yh, yh, yl], axis=ky))
    if passes == 2:
        axis = 1 if dims == _NT else 2
        width = y.shape[axis]
        both = d(xh, jnp.concatenate([yh, yl], axis=axis))
        return both[:, :, :width] + both[:, :, width:]
    xl = (x - xh.astype(F32)).astype(BF16)
    if dims == _TN:
        return d(xh, yh) + d(xh, yl) + d(xl, yh)
    rows = x.shape[1]
    both = d(jnp.concatenate([xh, xl], axis=1), yh)
    return both[:, :rows] + both[:, rows:] + d(xh, yl)


@functools.partial(jax.custom_vjp, nondiff_argnums=(2, 3))
def _hdot_p(x, y, dims, passes):
    return _hdot_raw(passes, x, y, dims)


def _hdot_fwd(x, y, dims, passes):
    return _hdot_raw(passes, x, y, dims), (x, y)


def _hdot_bwd(dims, passes, res, ct):
    x, y = res
    if dims == _NN:
        return _hdot_raw(passes, ct, y, _NT), _hdot_raw(passes, x, ct, _TN)
    if dims == _NT:
        return _hdot_raw(passes, ct, y, _NN), _hdot_raw(passes, ct, x, _TN)
    return _hdot_raw(passes, y, ct, _NT), _hdot_raw(passes, x, ct, _NN)


_hdot_p.defvjp(_hdot_fwd, _hdot_bwd)


def _hdot(x, y, dims, kind):
    return _hdot_p(x, y, dims, _SCAN_PASSES[kind])


def _tri_solve(n_mat, x, length):
    row = lax.broadcasted_iota(jnp.int32, (length, length), 0)
    col = lax.broadcasted_iota(jnp.int32, (length, length), 1)
    eye = (row == col).astype(F32)[None]
    diag_blk = ((row // SUB) == (col // SUB))[None]
    nd = jnp.where(diag_blk, n_mat, 0.0)
    no = n_mat - nd
    dinv = eye + nd
    p = _hdot(nd, nd, _NN, "solve")
    for k in range(int(math.log2(SUB)) - 1):
        if k == int(math.log2(SUB)) - 2:
            dinv = dinv + _hdot(dinv, p, _NN, "solve")
        else:
            both = _hdot(jnp.concatenate([dinv, p], axis=1), p, _NN, "solve")
            dinv, p = dinv + both[:, :length], both[:, length:]
    both = _hdot(dinv, jnp.concatenate([x, no], axis=2), _NN, "solve")
    u, q = both[:, :, :x.shape[2]], both[:, :, x.shape[2]:]
    width = x.shape[2]
    for level in range(int(math.log2(length // SUB))):
        if level == int(math.log2(length // SUB)) - 1:
            u = u + _hdot(q, u, _NN, "solve")
        else:
            both = _hdot(q, jnp.concatenate([u, q], axis=2), _NN, "solve")
            u, q = u + both[:, :, :width], both[:, :, width:]
    return u


def _rwkv_chunk(rev, s0, r, lw, k, v, a, b):
    pairs, length, width = r.shape
    hn = width // 2
    row = lax.broadcasted_iota(jnp.int32, (length, length), 0)
    col = lax.broadcasted_iota(jnp.int32, (length, length), 1)
    row2 = lax.broadcasted_iota(jnp.int32, (length, 2 * length), 0)
    col2 = lax.broadcasted_iota(jnp.int32, (length, 2 * length), 1)
    col2 = jnp.where(col2 >= length, col2 - length, col2)
    if rev is None:
        half = pairs // 2
        back = lax.broadcasted_iota(jnp.int32, (pairs, length, length), 0) >= half
        idx2 = lax.broadcasted_iota(jnp.int32, (2 * pairs, length, 2 * length), 0)
        back2 = ((idx2 >= half) & (idx2 < pairs)) | (idx2 >= pairs + half)
        ahead = jnp.where(back, (col - row)[None], (row - col)[None])
        ahead2 = jnp.where(back2, (col2 - row2)[None], (row2 - col2)[None])
        incl, strict2, incl2 = ahead >= 0, ahead2 > 0, ahead2 >= 0
    else:
        incl = ((row <= col) if rev else (row >= col))[None]
        strict2 = ((row2 < col2) if rev else (row2 > col2))[None]
        incl2 = ((row2 <= col2) if rev else (row2 >= col2))[None]
    lane = lax.broadcasted_iota(jnp.int32, (1, 1, width), 2)
    first = lane < hn
    head_mask = jnp.concatenate([jnp.broadcast_to(first.astype(F32), (pairs, 1, width)),
                                 jnp.broadcast_to(1.0 - first.astype(F32), (pairs, 1, width))], axis=0)
    twice = lambda t: jnp.concatenate([t, t], axis=0)
    pick = lambda t: jnp.where(first, t[:pairs], t[pairs:])

    t_incl = jnp.broadcast_to(incl.astype(F32), (pairs, length, length))
    cum = _hdot(t_incl, lw, _NN, "cum")
    g = jnp.exp(cum)
    g_inv = jnp.exp(-cum)
    at = a * jnp.exp(cum - lw)
    rt = r * g
    bt = b * g_inv
    kt = k * g_inv
    by_pair = lambda t: jnp.concatenate([t[:pairs], t[pairs:]], axis=1)
    lhs = jnp.concatenate([twice(at) * head_mask, twice(rt) * head_mask], axis=1)
    gram = _hdot(by_pair(lhs), jnp.concatenate([bt, kt], axis=1), _NT, "gram")
    gram = jnp.concatenate([gram[:, :2 * length], gram[:, 2 * length:]], axis=0)
    top = jnp.where(strict2, gram[:, :length], 0.0)
    bot = jnp.where(incl2, gram[:, length:], 0.0)
    pick_rows = lambda t: jnp.where(first, t[:, :length], t[:, length:])
    from_state = _hdot(jnp.concatenate([at, rt], axis=1), s0, _NT, "apply")
    x = from_state[:, :length] + pick_rows(
        _hdot(by_pair(top), jnp.concatenate([jnp.zeros_like(v), v], axis=1), _NN, "apply"))
    u = pick(_tri_solve(top[:, :, :length], twice(x), length))
    y = from_state[:, length:] + pick_rows(_hdot(by_pair(bot), jnp.concatenate([u, v], axis=1), _NN, "apply"))
    g_last = jnp.exp(jnp.sum(lw, axis=1, keepdims=True))
    ri = lax.broadcasted_iota(jnp.int32, (width, width), 0)
    ci = lax.broadcasted_iota(jnp.int32, (width, width), 1)
    same_head = ((ri < hn) == (ci < hn))[None]
    upd = _hdot(jnp.concatenate([u, v], axis=1), jnp.concatenate([bt, kt], axis=1), _TN, "state")
    s1 = (s0 + jnp.where(same_head, upd, 0.0)) * g_last
    return y, s1


def _split_pairs(x):
    return jnp.stack([x[:, p * LANES:(p + 1) * LANES] for p in range(x.shape[1] // LANES)])


def _merge_pairs(x):
    return jnp.concatenate([x[p] for p in range(x.shape[0])], axis=1)


def _scan_specs(views, rw, nc, rev):
    cidx = (lambda c: nc - 1 - c) if rev else (lambda c: c)
    seqs = [pl.BlockSpec((CHUNK, rw), functools.partial(lambda c, cb: (cidx(c), cb), cb=cb)) for _, cb, _ in views]
    plain = pl.BlockSpec((CHUNK, rw), lambda c: (cidx(c), 0))
    st = pl.BlockSpec((1, rw // LANES, LANES, LANES), lambda c: (cidx(c), 0, 0, 0))
    return seqs, plain, st


def _as_views(arrs, rw):
    return [t if isinstance(t, tuple) else (t, 0, rw) for t in arrs]


def _rwkv_scan_fwd(ops_f, ops_b, rw, *, name):
    S = _as_views(ops_f, rw)[0][0].shape[0]
    nc, pairs = S // CHUNK, rw // LANES
    in_specs, out_specs, arrays = [], [], []
    for rev, ops in ((False, ops_f), (True, ops_b)):
        views = _as_views(ops, rw)
        seqs, plain, st = _scan_specs(views, rw, nc, rev)
        in_specs += seqs
        out_specs += [plain, st]
        arrays += [t[0] for t in views]

    def both(refs_f, refs_b):
        return [jnp.concatenate([_split_pairs(f[...]), _split_pairs(b[...])], axis=0) for f, b in zip(refs_f, refs_b)]

    def body(*refs):
        (y_f, st_f, y_b, st_b), s_ref = refs[12:16], refs[16]

        @pl.when(pl.program_id(0) == 0)
        def _():
            s_ref[...] = jnp.zeros_like(s_ref)

        s0 = s_ref[...]
        st_f[0] = s0[:pairs]
        st_b[0] = s0[pairs:]
        y, s1 = _rwkv_chunk(None, s0, *both(refs[:6], refs[6:12]))
        y_f[...] = _merge_pairs(y[:pairs])
        y_b[...] = _merge_pairs(y[pairs:])
        s_ref[...] = s1

    return pl.pallas_call(
        body, name=name, grid=(nc,), in_specs=in_specs, out_specs=out_specs,
        out_shape=[jax.ShapeDtypeStruct((S, rw), F32), jax.ShapeDtypeStruct((nc, pairs, LANES, LANES), F32)] * 2,
        scratch_shapes=[pltpu.VMEM((2 * pairs, LANES, LANES), F32)],
        compiler_params=_cparams(("arbitrary",)),
    )(*arrays)


def _rwkv_scan_bwd(ops_f, ops_b, states_f, states_b, dy, rw, *, name):
    S = dy.shape[0]
    nc, pairs = S // CHUNK, rw // LANES
    in_specs, arrays = [], []
    for rev, ops, states in ((False, ops_f, states_f), (True, ops_b, states_b)):
        views = _as_views(list(ops) + [dy], rw)
        seqs, plain, st = _scan_specs(views, rw, nc, not rev)
        in_specs += seqs + [st]
        arrays += [t[0] for t in views] + [states]
    out_specs = []
    for rev in (False, True):
        out_specs += [_scan_specs([], rw, nc, not rev)[1]] * 6

    def both(refs_f, refs_b):
        return [jnp.concatenate([_split_pairs(f[...]), _split_pairs(b[...])], axis=0) for f, b in zip(refs_f, refs_b)]

    def body(*refs):
        ds_ref = refs[28]

        @pl.when(pl.program_id(0) == 0)
        def _():
            ds_ref[...] = jnp.zeros_like(ds_ref)

        s0 = jnp.concatenate([refs[7][0], refs[15][0]], axis=0)
        _, vjp = jax.vjp(functools.partial(_rwkv_chunk, None), s0, *both(refs[:6], refs[8:14]))
        (dy,) = both(refs[6:7], refs[14:15])
        grads = vjp((dy, ds_ref[...]))
        ds_ref[...] = grads[0]
        for o_f, o_b, gval in zip(refs[16:22], refs[22:28], grads[1:]):
            o_f[...] = _merge_pairs(gval[:pairs])
            o_b[...] = _merge_pairs(gval[pairs:])

    return pl.pallas_call(
        body, name=name, grid=(nc,), in_specs=in_specs, out_specs=out_specs,
        out_shape=[jax.ShapeDtypeStruct((S, rw), F32)] * 12,
        scratch_shapes=[pltpu.VMEM((2 * pairs, LANES, LANES), F32)],
        compiler_params=_cparams(("arbitrary",)),
    )(*arrays)


def _shift_lerp(x_view, mu, d=None, into=None, *, name):
    arr, off, width = x_view
    S = arr.shape[0]
    cb = _pick(width, 512)
    assert off % cb == 0

    def cshift(t):
        rows = lax.broadcasted_iota(jnp.int32, t.shape, 0)
        prev = jnp.where(rows == 0, 0.0, pltpu.roll(t, 1, 0))
        nxt = jnp.where(rows == S - 1, 0.0, pltpu.roll(t, S - 1, 0))
        return 0.5 * (prev + nxt)

    def fwd_body(x_ref, mu_ref, o_ref):
        x = x_ref[...]
        o_ref[...] = x + mu_ref[...] * (cshift(x) - x)

    def bwd_body(x_ref, mu_ref, d_ref, _, dx_ref, dmu_ref):
        x, m, dd = x_ref[...], mu_ref[...], d_ref[...]
        gm = m * dd
        dx_ref[...] = (dd - gm + cshift(gm)).astype(dx_ref.dtype)
        dmu_ref[...] = jnp.sum(dd * (cshift(x) - x), axis=0, keepdims=True)

    x_spec = pl.BlockSpec((S, cb), lambda j: (0, off // cb + j))
    blk = pl.BlockSpec((S, cb), lambda j: (0, j))
    vec = pl.BlockSpec((1, cb), lambda j: (0, j))
    if d is None:
        return pl.pallas_call(
            fwd_body, name=name, grid=(width // cb,), in_specs=[x_spec, vec], out_specs=blk,
            out_shape=jax.ShapeDtypeStruct((S, width), F32), compiler_params=_cparams(("parallel",)),
        )(arr, mu)
    buf, first = into
    assert first % cb == 0
    return pl.pallas_call(
        bwd_body, name=name, grid=(width // cb,),
        in_specs=[x_spec, vec, blk, pl.BlockSpec(memory_space=pl.ANY)],
        out_specs=[pl.BlockSpec((S, cb), lambda j: (0, first // cb + j)), vec],
        out_shape=[jax.ShapeDtypeStruct(buf.shape, buf.dtype), jax.ShapeDtypeStruct((1, width), F32)],
        input_output_aliases={3: 0}, compiler_params=_cparams(("parallel",)),
    )(arr, mu, d, buf)


def _attention_fwd(qfull, kv, kr, hm, scale, *, tq, name):
    S = qfull.shape[0]
    nt = (((1,), (1,)), ((), ()))

    def body(q_ref, kn_ref, kr_ref, v_ref, o_ref, lse_ref, k_scr):
        _head_keys(kn_ref, kr_ref, k_scr)
        s = lax.dot_general(q_ref[...], k_scr[...], nt, preferred_element_type=F32)
        m = jnp.max(s, axis=-1, keepdims=True)
        p = jnp.exp((s - m) * scale)
        l = jnp.sum(p, axis=-1, keepdims=True)
        o_ref[...] = jnp.dot(p.astype(BF16), v_ref[...], preferred_element_type=F32) * (1.0 / l)
        lse_ref[...] = jnp.broadcast_to(m * scale + jnp.log(l), lse_ref.shape)

    oblk = pl.BlockSpec((tq, VDIM), lambda h, i: (i, h))
    return pl.pallas_call(
        body, name=name, grid=(hm, S // tq),
        in_specs=[pl.BlockSpec((tq, QHEAD), lambda h, i: (i, h)),
                  pl.BlockSpec((S, NOPE), lambda h, i: (0, h)),
                  pl.BlockSpec((S, LANES), lambda h, i: (0, 0)),
                  pl.BlockSpec((S, VDIM), lambda h, i: (0, hm + h))],
        out_specs=[oblk, oblk],
        out_shape=[jax.ShapeDtypeStruct((S, hm * VDIM), F32)] * 2,
        scratch_shapes=[pltpu.VMEM((S, QHEAD), BF16)],
        compiler_params=_cparams(("parallel", "arbitrary")),
    )(qfull, kv, kr, kv)


def _head_keys(kn_ref, kr_ref, k_scr):
    @pl.when(pl.program_id(1) == 0)
    def _():
        k_scr[:, :NOPE] = kn_ref[...]
        k_scr[:, NOPE:] = kr_ref[...]


def _attention_bwd(qfull, kv, kr, o, lse, d_o, hm, scale, *, tq, name):
    S = qfull.shape[0]
    tq = min(tq, S)
    nq = S // tq
    tn = (((0,), (0,)), ((), ()))
    nt = (((1,), (1,)), ((), ()))

    def body(q_ref, kn_ref, kr_ref, v_ref, o_ref, lse_ref, do_ref, dq_ref, dk_ref, dv_ref, k_scr):
        _head_keys(kn_ref, kr_ref, k_scr)
        s = lax.dot_general(q_ref[...], k_scr[...], nt, preferred_element_type=F32)
        p = jnp.exp(s * scale - lse_ref[:, 0:1])
        d_out = do_ref[...]
        delta = jnp.sum(d_out * o_ref[...], axis=-1, keepdims=True)
        d_out = d_out.astype(BF16)
        dp = lax.dot_general(d_out, v_ref[...], nt, preferred_element_type=F32)
        ds = (p * (dp - delta)).astype(BF16)
        dq_ref[...] = jnp.dot(ds, k_scr[...], preferred_element_type=F32) * scale
        dv = lax.dot_general(p.astype(BF16), d_out, tn, preferred_element_type=F32)
        dk = lax.dot_general(ds, q_ref[...], tn, preferred_element_type=F32)
        i = pl.program_id(1)
        for ref, val in ((dk_ref, dk), (dv_ref, dv)):
            @pl.when(i == 0)
            def _(ref=ref, val=val):
                ref[...] = val

            @pl.when(i > 0)
            def _(ref=ref, val=val):
                ref[...] += val

        @pl.when(i == nq - 1)
        def _():
            dk_ref[...] = dk_ref[...] * scale

    qblk = pl.BlockSpec((tq, QHEAD), lambda h, i: (i, h))
    oblk = pl.BlockSpec((tq, VDIM), lambda h, i: (i, h))
    return pl.pallas_call(
        body, name=name, grid=(hm, nq),
        in_specs=[qblk,
                  pl.BlockSpec((S, NOPE), lambda h, i: (0, h)),
                  pl.BlockSpec((S, LANES), lambda h, i: (0, 0)),
                  pl.BlockSpec((S, VDIM), lambda h, i: (0, hm + h)),
                  oblk, oblk, oblk],
        out_specs=[qblk, pl.BlockSpec((S, QHEAD), lambda h, i: (0, h)), pl.BlockSpec((S, VDIM), lambda h, i: (0, h))],
        out_shape=[jax.ShapeDtypeStruct((S, hm * QHEAD), F32), jax.ShapeDtypeStruct((S, hm * QHEAD), F32),
                   jax.ShapeDtypeStruct((S, hm * VDIM), F32)],
        scratch_shapes=[pltpu.VMEM((S, QHEAD), BF16)],
        compiler_params=_cparams(("parallel", "arbitrary")),
    )(qfull, kv, kr, kv, o, lse, d_o)


def _layout(D, MW, RW, TAIL, QR, KVR):
    names = ["gate_m", "gate_r", "z_m", "z_r", "q_a", "kv_a", "r", "k", "v", "tail"]
    widths = [D, D, MW, RW, QR, KVR, RW, RW, RW, TAIL]
    offs, o = {}, 0
    for nme, w in zip(names, widths):
        assert o % w == 0, (nme, o, w)
        offs[nme] = (o, w)
        o += w
    return offs, o


def _local_grads(x, target, W, dims, exchange=None):
    S, D = x.shape
    hm, hr, hn, rank = dims["hm"], dims["hr"], dims["hn"], dims["rank"]
    MW, RW = hm * VDIM, hr * hn
    TAIL = dims["TAIL"]
    QR, KVR = W["mla_q_norm"].shape[1], W["mla_kv_norm"].shape[1]
    lay, d_in = _layout(D, MW, RW, TAIL, QR, KVR)
    T = 256
    scale = (NOPE + ROPE) ** -0.5
    col = lambda arr, nme: _view(arr, *lay[nme])

    pos = jnp.arange(S, dtype=F32)
    inv_freq = jnp.power(ROPE_THETA, -jnp.arange(0, ROPE, 2, dtype=F32) / ROPE)
    ang = pos[:, None] * inv_freq[None, :]
    zpad = jnp.zeros((S, LANES - ROPE), F32)
    cosx = jnp.concatenate([jnp.cos(ang), jnp.cos(ang), zpad], axis=1)
    sinx = jnp.concatenate([jnp.sin(ang), jnp.sin(ang), zpad], axis=1)
    ri, ci = jnp.arange(LANES)[:, None], jnp.arange(LANES)[None, :]
    half = ROPE // 2
    rot = (jnp.where((ri == ci - half) & (ci >= half) & (ci < ROPE), 1.0, 0.0)
           - jnp.where((ri == ci + half) & (ci < half), 1.0, 0.0)).astype(BF16)
    seg = (jnp.arange(RW)[:, None] // hn == jnp.arange(LANES)[None, :]).astype(BF16)
    stacked = lambda t: jnp.concatenate([t, t], axis=0)
    rot, rot_t, seg, seg_t = stacked(rot), stacked(rot.T), stacked(seg), stacked(seg.T)

    (h,) = _rowwise(lambda xb, g: (_rms(xb, g),), [x], [W["g_pre"]], [(D, BF16)], tile=2 * T, name="pre_norm")
    if exchange is None:
        proj = _mm(h, W["w_in_t"], tb=True, name="in_proj")
    else:
        proj, *slabs = _mm(h, W["w_in_t"], tb=True, ride=_gather_plan(exchange[0]), name="in_proj")
        W = {**W, **_prepare_rest(dict(zip(_MATS[1:], slabs)), dims)}

    qn, kvn = _rowwise(_f_mla_norm, [col(proj, "q_a"), col(proj, "kv_a")], [W["mla_q_norm"], W["mla_kv_norm"]],
                       [(QR, BF16), (KVR, BF16)], tile=2 * T, name="mla_norm")
    qraw = _mm(qn, W["wq_b_t"], tb=True, name="q_up")
    kv = _mm(kvn, W["wkv_b"], out_dtype=BF16, name="kv_up")
    kr_view = _view(proj, lay["tail"][0], LANES)
    qfull, kr = _rowwise(functools.partial(_f_rope, hm), [qraw, kr_view, cosx, sinx], [rot, rot_t],
                         [(hm * QHEAD, BF16), (LANES, BF16)], tile=2 * T, name="rope")
    o_mla, lse = _attention_fwd(qfull, kv, kr, hm, scale, tq=T, name="attn_fwd")

    shift_view = (proj, lay["r"][0], 3 * RW + TAIL)
    rl = _shift_lerp(shift_view, W["mu"], name="shift_fwd")
    rl_r, rl_k, rl_v = _view(rl, 0, RW), _view(rl, RW, RW), _view(rl, 2 * RW, RW)
    rl_tail = _view(rl, 3 * RW, TAIL)
    pre_params = [W["w0_f"], W["w0_b"], W["a0_f"], W["a0_b"], W["k_k"], W["k_a"], W["w2cat"], W["a2cat"], seg, seg_t]
    pre_fn = functools.partial(_f_rwkv_pre, RW)
    lw_f, lw_b, k_f, k_b, a_n, b_f, b_b = _rowwise(pre_fn, [rl_k, rl_tail], pre_params, [(RW, F32)] * 7, tile=T,
                                                    name="rwkv_pre")
    ops_f = (rl_r, lw_f, k_f, rl_v, a_n, b_f)
    ops_b = (rl_r, lw_b, k_b, rl_v, a_n, b_b)
    y_f, st_f, y_b, st_b = _rwkv_scan_fwd(ops_f, ops_b, RW, name="scan_fwd")

    post_fn = functools.partial(_f_post, hn)
    post_rows = [y_f, y_b, rl_r, k_f, k_b, rl_v, col(proj, "z_r"), o_mla, col(proj, "z_m")]
    post_params = [W["gn_g"], W["gn_b"], W["r_k"], seg, seg_t]
    ymg, yrg = _rowwise(post_fn, post_rows, post_params, [(MW, BF16), (RW, BF16)], tile=T, name="post")
    u_m = _mm(ymg, W["w_br_mla"], name="br_mla")
    u_r = _mm(yrg, W["w_br_rwkv"], name="br_rwkv")
    merge_rows = [u_m, u_r, col(proj, "gate_m"), col(proj, "gate_r")]
    (merged,) = _rowwise(lambda *t: (_f_merge(*t),), merge_rows, [], [(D, BF16)], tile=T, name="merge")
    out = _mm(merged, W["w_out"], name="out_proj")

    def head(ob, xb, tb, g):
        yn, vjp = jax.vjp(_rms, ob, g)
        err = xb + yn - tb
        dy = err * (1.0 / D)
        d_ob, d_g = vjp(dy)
        loss = jnp.broadcast_to(0.5 * jnp.sum(err * err) * (1.0 / D), (1, LANES))
        return dy, d_ob, loss, d_g

    dy, d_out, loss, g_g_post = _rowwise(head, [out, x, target], [W["g_post"]], [(D, F32), (D, BF16)],
                                         [(1, LANES), (1, D)], tile=2 * T, name="head")
    d_merged = _mm(d_out, W["w_out"], tb=True, name="d_merged")
    g_w_out = _mm(merged, d_out, ta=True, out_dtype=BF16, name="g_w_out")

    def merge_bwd(u_m_b, u_r_b, g_m_b, g_r_b, dm):
        _, vjp = jax.vjp(_f_merge, u_m_b, u_r_b, g_m_b, g_r_b)
        du_m, du_r, dg_m, dg_r = vjp(dm)
        return du_m, du_r, jnp.concatenate([dg_m, dg_r], axis=1)

    d_u_m, d_u_r, d_proj = _rowwise(merge_bwd, merge_rows + [d_merged], [],
                                    [(D, BF16), (D, BF16), (2 * D, BF16, (None, d_in, lay["gate_m"][0]))], tile=T,
                                    name="merge_bwd")
    d_ymg = _mm(d_u_m, W["w_br_mla"], tb=True, name="d_ymg")
    d_yrg = _mm(d_u_r, W["w_br_rwkv"], tb=True, name="d_yrg")
    g_w_br_mla = _mm(ymg, d_u_m, ta=True, out_dtype=BF16, name="g_w_br_mla")
    g_w_br_rwkv = _mm(yrg, d_u_r, ta=True, out_dtype=BF16, name="g_w_br_rwkv")

    def post_bwd(*args):
        nr = len(post_rows)
        prim, dm, dr = args[:nr] + args[nr + 2:], args[nr], args[nr + 1]
        _, vjp = jax.vjp(post_fn, *prim)
        g = vjp((dm, dr))
        return g[0], g[2], g[3], g[5], g[7], jnp.concatenate([g[8], g[6]], axis=1), g[9], g[10], g[11]

    (d_y, d_r_bonus, d_k_bonus, d_v_bonus, d_o, d_proj, g_gn_g, g_gn_b, g_r_k) = _rowwise(
        post_bwd, post_rows + [d_ymg, d_yrg], post_params,
        [(RW, F32), (RW, F32), (RW, F32), (RW, F32), (MW, F32), (MW + RW, BF16, (d_proj, d_in, lay["z_m"][0]))],
        [(1, RW)] * 3, tile=T // 2, name="post_bwd")

    dscan = _rwkv_scan_bwd(ops_f, ops_b, st_f, st_b, d_y, RW, name="scan_bwd")
    dsc = {"f": dscan[:6], "b": dscan[6:]}

    d_q_att, d_k_att, d_v_att = _attention_bwd(qfull, kv, kr, o_mla, lse, d_o, hm, scale, tq=4 * T, name="attn_bwd")

    def rope_bwd(qraw_b, kr_in, cos_b, sin_b, dq_b, dk_b, dv_b, rot_b, rot_t_b):
        _, vjp = jax.vjp(lambda q_, k_: _f_rope(hm, q_, k_, cos_b, sin_b, rot_b, rot_t_b), qraw_b, kr_in)
        dkn = jnp.concatenate([dk_b[:, hh * QHEAD:hh * QHEAD + NOPE] for hh in range(hm)], axis=1)
        dkr = dk_b[:, NOPE:QHEAD]
        for hh in range(1, hm):
            dkr = dkr + dk_b[:, hh * QHEAD + NOPE:(hh + 1) * QHEAD]
        d_qraw, d_kr_in = vjp((dq_b, dkr))
        return d_qraw, jnp.concatenate([dkn, dv_b], axis=1), d_kr_in

    d_qraw, d_kv, d_kr_in = _rowwise(rope_bwd, [qraw, kr_view, cosx, sinx, d_q_att, d_k_att, d_v_att],
                                     [rot, rot_t], [(hm * QHEAD, BF16), (2 * MW, BF16), (LANES, F32)], tile=T,
                                     name="rope_bwd")
    d_qnorm = _mm(d_qraw, W["wq_b_t"], name="d_qn")
    d_kvnorm = _mm(d_kv, W["wkv_b"], tb=True, name="d_kvn")
    g_wq_b = _mm(d_qraw, qn, ta=True, out_dtype=BF16, name="g_wq_b")
    g_wkv_b = _mm(kvn, d_kv, ta=True, out_dtype=BF16, name="g_wkv_b")

    def mla_norm_bwd(q_a, kv_a, qg, kvg, dq, dk):
        _, vjp = jax.vjp(_f_mla_norm, q_a, kv_a, qg, kvg)
        d_q_a, d_kv_a, d_qg, d_kvg = vjp((dq, dk))
        return jnp.concatenate([d_q_a, d_kv_a], axis=1), d_qg, d_kvg

    d_proj, g_q_norm, g_kv_norm = _rowwise(
        lambda q_a, kv_a, dq, dk, qg, kvg: mla_norm_bwd(q_a, kv_a, qg, kvg, dq, dk),
        [col(proj, "q_a"), col(proj, "kv_a"), d_qnorm, d_kvnorm], [W["mla_q_norm"], W["mla_kv_norm"]],
        [(QR + KVR, BF16, (d_proj, d_in, lay["q_a"][0]))], [(1, QR), (1, KVR)], tile=2 * T, name="mla_norm_bwd")

    def pre_bwd(k_b_, tail_b, dlwf, dlwb, dkf, dkb, dkbon, daf, dab, dbf, dbb, drf, drb, drbon, dvf, dvb, dvbon,
                dkr, *params):
        w2, a2 = params[6], params[7]
        nt, tn = (((1,), (1,)), ((), ())), (((0,), (0,)), ((), ()))
        split = w2.shape[0]
        th = jnp.tanh(tail_b[:, :split])
        th_b, tail_h = th.astype(BF16), tail_b[:, split:].astype(BF16)
        zw = jnp.dot(th_b, w2, preferred_element_type=F32)
        za = jnp.dot(tail_h, a2, preferred_element_type=F32)
        _, vjp = jax.vjp(functools.partial(_f_rwkv_core, RW), k_b_, zw, za, *params[:6], params[8], params[9])
        g = vjp((dlwf, dlwb, dkf + dkbon, dkb + dkbon, daf + dab, dbf, dbb))
        d_zw, d_za = g[1].astype(BF16), g[2].astype(BF16)
        d_tail = (jnp.concatenate([lax.dot_general(d_zw, w2, nt, preferred_element_type=F32) * (1.0 - th * th),
                                   lax.dot_general(d_za, a2, nt, preferred_element_type=F32)], axis=1)
                  + jnp.concatenate([dkr, jnp.zeros((dkr.shape[0], TAIL - LANES), F32)], axis=1))
        g_w2 = lax.dot_general(th_b, d_zw, tn, preferred_element_type=F32)
        g_a2 = lax.dot_general(tail_h, d_za, tn, preferred_element_type=F32)
        d_rl = jnp.concatenate([drf + drb + drbon, g[0], dvf + dvb + dvbon, d_tail], axis=1)
        return (d_rl,) + tuple(g[3:9]) + (g_w2, g_a2)

    f_, b_ = dsc["f"], dsc["b"]
    pre_bwd_rows = [rl_k, rl_tail, f_[1], b_[1], f_[2], b_[2], d_k_bonus, f_[4], b_[4], f_[5], b_[5],
                    f_[0], b_[0], d_r_bonus, f_[3], b_[3], d_v_bonus, d_kr_in]
    (d_rl, g_w0_f, g_w0_b, g_a0_f, g_a0_b, g_k_k, g_k_a, g_w2cat, g_a2cat) = _rowwise(
        pre_bwd, pre_bwd_rows, pre_params, [(3 * RW + TAIL, F32)],
        [(1, RW)] * 6 + [W["w2cat"].shape, W["a2cat"].shape], tile=T // 2, name="rwkv_pre_bwd")
    d_proj, g_mu = _shift_lerp(shift_view, W["mu"], d_rl, (d_proj, lay["r"][0]), name="shift_bwd")
    small = dict(wq_b=g_wq_b, wkv_b=g_wkv_b, w2cat=g_w2cat, a2cat=g_a2cat, w_br_mla=g_w_br_mla,
                 w_br_rwkv=g_w_br_rwkv, w_out=g_w_out)
    if exchange is None:
        received = None
        g_w_in = _mm(d_proj, h, ta=True, out_dtype=BF16, tn_cap=1024, name="g_w_in")
        d_h = _mm(d_proj, W["w_in_t"], tn_cap=1024, name="d_h")
    else:
        slabs = _restore_rest(small, dims)
        slabs = [slabs[n] for n in _MATS[1:]]
        g_w_in, *got = _mm(d_proj, h, ta=True, out_dtype=BF16, tn_cap=1024, ride=_sibling_swap_plan(slabs),
                           name="g_w_in")
        sums = [_pair_add(exchange[1], s, t, name="pair_add_" + n) for n, s, t in zip(_MATS[1:], slabs, got)]
        g_w_in = _restore_w_in(g_w_in, dims)
        d_h, *received = _mm(d_proj, W["w_in_t"], tn_cap=1024, name="d_h",
                             ride=_join_plans(_chip_exchange_plan(sums), _sibling_swap_plan([g_w_in])))
        small = {}

    def pre_norm_bwd(xb, dyb, dhb, g):
        _, vjp = jax.vjp(_rms, xb, g)
        dx, dg = vjp(dhb)
        return dyb + dx, dg

    grad_x, g_g_pre = _rowwise(pre_norm_bwd, [x, dy, d_h], [W["g_pre"]], [(D, F32)], [(1, D)], tile=2 * T,
                               name="pre_norm_bwd")

    grads = dict(g_pre=g_g_pre, w_in=g_w_in, mla_q_norm=g_q_norm, mla_kv_norm=g_kv_norm, mu=g_mu, w0_f=g_w0_f,
                 w0_b=g_w0_b, a0_f=g_a0_f, a0_b=g_a0_b, k_k=g_k_k, k_a=g_k_a, r_k=g_r_k, gn_g=g_gn_g, gn_b=g_gn_b,
                 g_post=g_g_post, **small)
    return loss[0, 0], grad_x, grads, received


_MATS = ["w_in", "mla_wq_b", "mla_wkv_b", "rwkv_w2_f", "rwkv_w2_b", "rwkv_a2_f", "rwkv_a2_b", "w_br_mla",
         "w_br_rwkv", "w_out"]
_ROW_SHARDED = ("w_out",)
_TRANSPOSED = ("w_in", "mla_wq_b")
_VECS = ["g_pre", "mla_q_norm", "mla_kv_norm", "rwkv_mu", "rwkv_w0_f", "rwkv_w0_b", "rwkv_a0_f", "rwkv_a0_b",
         "rwkv_k_k", "rwkv_k_a", "rwkv_r_k", "rwkv_gn_g", "rwkv_gn_b", "g_post"]
_WEIGHTS = ["g_pre", "w_in", "mla_q_norm", "mla_wq_b", "mla_kv_norm", "mla_wkv_b", "rwkv_mu", "rwkv_w0_f",
            "rwkv_w2_f", "rwkv_w0_b", "rwkv_w2_b", "rwkv_a0_f", "rwkv_a2_f", "rwkv_a0_b", "rwkv_a2_b", "rwkv_k_k",
            "rwkv_k_a", "rwkv_r_k", "rwkv_gn_g", "rwkv_gn_b", "w_br_mla", "w_br_rwkv", "w_out", "g_post"]

def _direct_gather_plan(src):
    def phases(src_refs, out_refs, sem_refs):
        (src_ref,), (out_ref,), sems, local_sem = src_refs, out_refs, sem_refs[:2], sem_refs[2]
        x, y, c = lax.axis_index("x"), lax.axis_index("y"), lax.axis_index("c")
        me = 4 * x + 2 * y + c
        flip = lambda v, bit: (1 - v) if bit else v
        peers = [(flip(x, d & 4), flip(y, d & 2), flip(c, d & 1)) for d in range(1, N_DEV)]
        own = lambda: pltpu.make_async_copy(src_ref, out_ref.at[me], local_sem)
        send = lambda d: _remote(src_ref, out_ref.at[me], sems, d, peers[d])

        def first():
            own().start()
            for d in range(N_DEV - 1):
                send(d).start()

        def last():
            for d, (px, py, pc) in enumerate(peers):
                blk = out_ref.at[4 * px + 2 * py + pc]
                _remote(blk, blk, sems, d, (x, y, c)).wait_recv()
            for d in range(N_DEV - 1):
                send(d).wait_send()
            own().wait()

        return first, (lambda: None), last

    return [src], [jax.ShapeDtypeStruct((N_DEV,) + src.shape, src.dtype)], [(N_DEV - 1,), (N_DEV - 1,), ()], phases


def _remote(src, dst, sems, key, to):
    send_sems, recv_sems = sems
    return pltpu.make_async_remote_copy(src_ref=src, dst_ref=dst, send_sem=send_sems.at[key], recv_sem=recv_sems.at[key],
                                        device_id=to, device_id_type=pl.DeviceIdType.MESH)


def _run_exchange(plan, *, name):
    srcs, out_shapes, sem_shapes, phases = plan
    n, m = len(srcs), len(out_shapes)

    def body(*refs):
        for phase in phases(refs[:n], refs[n:n + m], refs[n + m:]):
            phase()

    return pl.pallas_call(
        body, name=name, out_shape=out_shapes,
        in_specs=[pl.BlockSpec(memory_space=pl.ANY)] * n, out_specs=[pl.BlockSpec(memory_space=pl.ANY)] * m,
        scratch_shapes=[pltpu.SemaphoreType.DMA(s) for s in sem_shapes],
    )(*srcs)


def _join_plans(p, q):
    (srcs_p, outs_p, sems_p, phases_p), (srcs_q, outs_q, sems_q, phases_q) = p, q

    def phases(src_refs, out_refs, sem_refs):
        a = phases_p(src_refs[:len(srcs_p)], out_refs[:len(outs_p)], sem_refs[:len(sems_p)])
        b = phases_q(src_refs[len(srcs_p):], out_refs[len(outs_p):], sem_refs[len(sems_p):])

        def both(fa, fb):
            def run():
                fa()
                fb()
            return run

        return tuple(both(fa, fb) for fa, fb in zip(a, b))

    return list(srcs_p) + list(srcs_q), list(outs_p) + list(outs_q), list(sems_p) + list(sems_q), phases


def _gather_plan(srcs):
    n = len(srcs)

    def phases(src_refs, out_refs, sem_refs):
        sems, local_sems = sem_refs[:2], sem_refs[2]
        x, y, c = lax.axis_index("x"), lax.axis_index("y"), lax.axis_index("c")
        idx = lambda px, py, pc: 4 * px + 2 * py + pc
        me, sibling = (x, y, c), (x, y, 1 - c)
        chips = [(1 - x, y), (x, 1 - y), (1 - x, 1 - y)]
        own = lambda a: pltpu.make_async_copy(src_refs[a], out_refs[a].at[idx(*me)], local_sems.at[a])
        to_sibling = lambda a: _remote(src_refs[a], out_refs[a].at[idx(*me)], sems, (0, a), sibling)
        to_chip = lambda a, j: _remote(src_refs[a], out_refs[a].at[idx(*me)], sems, (1 + j, a), (*chips[j], c))
        landed = lambda a, j: out_refs[a].at[idx(*chips[j], c)]
        passed_on = lambda a, j: _remote(landed(a, j), landed(a, j), sems, (4 + j, a), sibling)

        def first():
            for a in range(n):
                own(a).start()
                to_sibling(a).start()
                for j in range(3):
                    to_chip(a, j).start()

        def middle():
            for j in range(3):
                for a in range(n):
                    _remote(landed(a, j), landed(a, j), sems, (1 + j, a), me).wait_recv()
                    passed_on(a, j).start()

        def last():
            for a in range(n):
                blk = out_refs[a].at[idx(*sibling)]
                _remote(blk, blk, sems, (0, a), me).wait_recv()
                for j in range(3):
                    blk = out_refs[a].at[idx(*chips[j], 1 - c)]
                    _remote(blk, blk, sems, (4 + j, a), me).wait_recv()
            for a in range(n):
                to_sibling(a).wait_send()
                for j in range(3):
                    to_chip(a, j).wait_send()
                    passed_on(a, j).wait_send()
                own(a).wait()

        return first, middle, last

    return srcs, [jax.ShapeDtypeStruct((N_DEV,) + s.shape, s.dtype) for s in srcs], [(7, n), (7, n), (n,)], phases


def _sibling_swap_plan(srcs):
    n = len(srcs)

    def phases(src_refs, out_refs, sems):
        x, y, c = lax.axis_index("x"), lax.axis_index("y"), lax.axis_index("c")
        copies = lambda: [_remote(src_refs[a].at[2 * q + 1 - c], out_refs[a].at[q], sems, (q, a), (x, y, 1 - c))
                          for a in range(n) for q in range(4)]

        def first():
            for cp in copies():
                cp.start()

        def last():
            for cp in copies():
                cp.wait()

        return first, (lambda: None), last

    return srcs, [jax.ShapeDtypeStruct((4,) + s.shape[1:], s.dtype) for s in srcs], [(4, n), (4, n)], phases


def _chip_exchange_plan(srcs):
    n = len(srcs)

    def phases(src_refs, out_refs, sem_refs):
        sems, local_sems = sem_refs[:2], sem_refs[2]
        x, y, c = lax.axis_index("x"), lax.axis_index("y"), lax.axis_index("c")
        mine = 2 * x + y
        chips = [(1 - x, y), (x, 1 - y), (1 - x, 1 - y)]
        own = lambda a: pltpu.make_async_copy(src_refs[a].at[mine], out_refs[a].at[mine], local_sems.at[a])
        send = lambda a, j: _remote(src_refs[a].at[2 * chips[j][0] + chips[j][1]], out_refs[a].at[mine], sems, (j, a),
                                    (*chips[j], c))

        def first():
            for a in range(n):
                own(a).start()
                for j in range(3):
                    send(a, j).start()

        def last():
            for j in range(3):
                for a in range(n):
                    blk = out_refs[a].at[2 * chips[j][0] + chips[j][1]]
                    _remote(blk, blk, sems, (j, a), (x, y, c)).wait_recv()
            for a in range(n):
                for j in range(3):
                    send(a, j).wait_send()
                own(a).wait()

        return first, (lambda: None), last

    return srcs, [jax.ShapeDtypeStruct(s.shape, s.dtype) for s in srcs], [(3, n), (3, n), (n,)], phases


def _pair_add(core, g, got, *, name):
    q, r, c = got.shape
    tr, tc = _tile2d(r, c, cap=1024)

    def body(core_ref, a_ref, b_ref, o_ref):
        o_ref[...] = (a_ref[...].astype(F32) + b_ref[...].astype(F32)).astype(BF16)

    blk = pl.BlockSpec((1, tr, tc), lambda i, j, k, core_ref: (i, j, k))
    mine = pl.BlockSpec((1, tr, tc), lambda i, j, k, core_ref: (2 * i + core_ref[0], j, k))
    return pl.pallas_call(
        body, name=name, out_shape=jax.ShapeDtypeStruct(got.shape, BF16),
        grid_spec=pltpu.PrefetchScalarGridSpec(num_scalar_prefetch=1, grid=(q, r // tr, c // tc),
                                               in_specs=[mine, blk], out_specs=blk),
        compiler_params=_cparams(("parallel", "parallel", "parallel")))(core, g, got)


def _adamw(recv, w, m, v, *, name):
    r, c = w.shape
    n_terms = recv.shape[0]
    tr, tc = _tile2d(r, c)

    def body(g_ref, w_ref, m_ref, v_ref, go_ref, d_ref, mo_ref, vo_ref):
        g = g_ref[0].astype(F32)
        for k in range(1, n_terms):
            g = g + g_ref[k].astype(F32)
        m_new = ADAM_B1 * m_ref[...] + (1.0 - ADAM_B1) * g
        v_new = ADAM_B2 * v_ref[...] + (1.0 - ADAM_B2) * (g * g)
        m_hat = m_new / (1.0 - ADAM_B1 ** ADAM_STEP)
        v_hat = v_new / (1.0 - ADAM_B2 ** ADAM_STEP)
        go_ref[...] = g
        d_ref[...] = -ADAM_LR * (m_hat / (jnp.sqrt(v_hat) + ADAM_EPS) + ADAM_WD * w_ref[...])
        mo_ref[...] = m_new
        vo_ref[...] = v_new

    blk = pl.BlockSpec((tr, tc), lambda i, j: (i, j))
    return pl.pallas_call(
        body, name=name, grid=(r // tr, c // tc),
        in_specs=[pl.BlockSpec((n_terms, tr, tc), lambda i, j: (0, i, j)), blk, blk, blk], out_specs=[blk] * 4,
        out_shape=[jax.ShapeDtypeStruct((r, c), F32)] * 4, compiler_params=_cparams(("parallel", "parallel")),
    )(recv, w, m, v)


def _tile2d(r, c, cap=256):
    if r <= cap:
        return r, c
    for t in range(cap - cap % BF16_ROWS, 0, -BF16_ROWS):
        if r % t == 0:
            return t, c
    return r, _pick(c, cap)


def _pack(pieces):
    total = sum(p.shape[0] for p in pieces)
    pad = (-total) % (8 * LANES)
    flat = jnp.concatenate(list(pieces) + [jnp.zeros((pad,), F32)])
    return flat.reshape(-1, LANES)


def _unpack(flat, sizes):
    flat = flat.reshape(-1)
    out, o = [], 0
    for n in sizes:
        out.append(flat[o:o + n])
        o += n
    return out


def _prepare_weights(full, vec, dims):
    rest = {n: t for n, t in full.items() if n != "w_in"}
    return {"w_in_t": _prepare_w_in(full["w_in"], dims), **_prepare_rest(rest, dims), **_prepare_vectors(vec, dims)}


def _prepare_w_in(slabs, dims):
    D = dims["D"]
    flat = slabs.reshape(-1, D)
    parts, pos = [], 0
    for orig_off, width, perm_off in sorted(dims["segs"], key=lambda t: t[2]):
        if perm_off > pos:
            parts.append(jnp.zeros((perm_off - pos, D), BF16))
        parts.append(flat[orig_off:orig_off + width])
        pos = perm_off + width
    if dims["d_in_perm"] > pos:
        parts.append(jnp.zeros((dims["d_in_perm"] - pos, D), BF16))
    return jnp.concatenate(parts, axis=0)


def _prepare_rest(full, dims):
    hm, hr, hn, rank = dims["hm"], dims["hr"], dims["hn"], dims["rank"]
    QR, KVR = dims["QR"], dims["KVR"]
    RW, TAIL = hr * hn, dims["TAIL"]
    full = {n: (t.reshape(-1, t.shape[2]) if n in _ROW_SHARDED + _TRANSPOSED
                else t.transpose(1, 0, 2).reshape(t.shape[1], -1)) for n, t in full.items()}
    wq = full["mla_wq_b"].reshape(hm, NOPE + ROPE, QR)
    wq = jnp.concatenate([wq, jnp.zeros((hm, QHEAD - NOPE - ROPE, QR), BF16)], axis=1).reshape(hm * QHEAD, QR)
    wkv = full["mla_wkv_b"].reshape(KVR, hm, 2, NOPE).transpose(0, 2, 1, 3).reshape(KVR, 2 * hm * NOPE)
    z = lambda rows: jnp.zeros((rows, RW), BF16)
    f = lambda nme: full[nme]
    split = ROPE + 2 * rank
    assert split % LANES == 0, split
    w2cat = jnp.concatenate([
        jnp.concatenate([z(ROPE), f("rwkv_w2_f"), z(rank)], axis=0),
        jnp.concatenate([z(ROPE + rank), f("rwkv_w2_b")], axis=0)], axis=1)
    a2cat = jnp.concatenate([
        jnp.concatenate([f("rwkv_a2_f"), z(TAIL - split - rank)], axis=0),
        jnp.concatenate([z(rank), f("rwkv_a2_b"), z(TAIL - split - 2 * rank)], axis=0)], axis=1)
    return dict(wq_b_t=wq, wkv_b=wkv, w2cat=w2cat, a2cat=a2cat, w_br_mla=full["w_br_mla"],
                w_br_rwkv=full["w_br_rwkv"], w_out=full["w_out"])


def _prepare_vectors(vec, dims):
    rank, RW, TAIL = dims["rank"], dims["hr"] * dims["hn"], dims["TAIL"]
    mu = vec["rwkv_mu"]
    mu_p = jnp.concatenate([mu[:3 * RW], jnp.zeros((ROPE,), F32), mu[3 * RW:],
                            jnp.zeros((TAIL - ROPE - 4 * rank,), F32)])
    row = lambda t: t.reshape(1, -1)
    return dict(
        mu=row(mu_p), g_pre=row(vec["g_pre"]), g_post=row(vec["g_post"]), mla_q_norm=row(vec["mla_q_norm"]),
        mla_kv_norm=row(vec["mla_kv_norm"]), w0_f=row(vec["rwkv_w0_f"]), w0_b=row(vec["rwkv_w0_b"]),
        a0_f=row(vec["rwkv_a0_f"]), a0_b=row(vec["rwkv_a0_b"]), k_k=row(vec["rwkv_k_k"]), k_a=row(vec["rwkv_k_a"]),
        r_k=row(vec["rwkv_r_k"]), gn_g=row(vec["rwkv_gn_g"]), gn_b=row(vec["rwkv_gn_b"]))


def _restore_grads(g, dims):
    return {"w_in": _restore_w_in(g["w_in"], dims), **_restore_rest(g, dims), **_restore_vectors(g, dims)}


def _restore_w_in(gw, dims):
    parts = [gw[perm_off:perm_off + width] for _, width, perm_off in sorted(dims["segs"])]
    return jnp.concatenate(parts, axis=0).reshape(N_DEV, dims["d_in"] // N_DEV, gw.shape[1])


def _restore_rest(g, dims):
    hm, hr, hn, rank = dims["hm"], dims["hr"], dims["hn"], dims["rank"]
    QR, KVR, RW = dims["QR"], dims["KVR"], hr * hn
    wq = g["wq_b"].reshape(hm, QHEAD, QR)[:, :NOPE + ROPE].reshape(N_DEV, -1, QR)
    wkv = g["wkv_b"].reshape(KVR, 2, hm, NOPE).transpose(0, 2, 1, 3).reshape(KVR, 2 * hm * NOPE)
    lo = lambda t, first, half: t[first:first + rank, half * RW:(half + 1) * RW].astype(BF16)
    cols = lambda t: t.reshape(t.shape[0], N_DEV, -1).transpose(1, 0, 2)
    return dict(
        mla_wq_b=wq, mla_wkv_b=cols(wkv), rwkv_w2_f=cols(lo(g["w2cat"], ROPE, 0)),
        rwkv_w2_b=cols(lo(g["w2cat"], ROPE + rank, 1)), rwkv_a2_f=cols(lo(g["a2cat"], 0, 0)),
        rwkv_a2_b=cols(lo(g["a2cat"], rank, 1)), w_br_mla=cols(g["w_br_mla"]), w_br_rwkv=cols(g["w_br_rwkv"]),
        w_out=g["w_out"].reshape(N_DEV, -1, g["w_out"].shape[1]))


def _restore_vectors(g, dims):
    rank, RW = dims["rank"], dims["hr"] * dims["hn"]
    mu = g["mu"][0]
    out = dict(
        rwkv_mu=jnp.concatenate([mu[:3 * RW], mu[3 * RW + ROPE:3 * RW + ROPE + 4 * rank]]),
        g_pre=g["g_pre"][0], g_post=g["g_post"][0], mla_q_norm=g["mla_q_norm"][0], mla_kv_norm=g["mla_kv_norm"][0],
        rwkv_w0_f=g["w0_f"][0], rwkv_w0_b=g["w0_b"][0], rwkv_a0_f=g["a0_f"][0], rwkv_a0_b=g["a0_b"][0],
        rwkv_k_k=g["k_k"][0], rwkv_k_a=g["k_a"][0], rwkv_r_k=g["r_k"][0], rwkv_gn_g=g["gn_g"][0],
        rwkv_gn_b=g["gn_b"][0])
    return out


def _dims(inp):
    D = inp["x"].shape[-1]
    QR, KVR = inp["mla_q_norm"].shape[0], inp["mla_kv_norm"].shape[0]
    hm = inp["mla_wq_b"].shape[1] * N_DEV // (NOPE + ROPE)
    hr, hn = inp["rwkv_r_k"].shape
    rank = inp["rwkv_w2_f"].shape[0]
    MW, RW = hm * VDIM, hr * hn
    TAIL = -(-(ROPE + 4 * rank) // LANES) * LANES
    orig, o = {}, 0
    for nme, w in (("q_a", QR), ("kv_a", KVR), ("k_rope", ROPE), ("rkv", 3 * RW), ("lora", 4 * rank), ("z_m", MW),
                   ("z_r", RW), ("gate_m", D), ("gate_r", D)):
        orig[nme] = (o, w)
        o += w
    assert o == inp["w_in"].shape[1] * N_DEV
    lay, d_in_perm = _layout(D, MW, RW, TAIL, QR, KVR)
    perm_off = dict(q_a=lay["q_a"][0], kv_a=lay["kv_a"][0], k_rope=lay["tail"][0], rkv=lay["r"][0],
                    lora=lay["tail"][0] + ROPE, z_m=lay["z_m"][0], z_r=lay["z_r"][0], gate_m=lay["gate_m"][0],
                    gate_r=lay["gate_r"][0])
    segs = [(orig[nme][0], orig[nme][1], perm_off[nme]) for nme in orig]
    return dict(D=D, QR=QR, KVR=KVR, hm=hm, hr=hr, hn=hn, rank=rank, TAIL=TAIL, segs=segs, d_in=o,
                d_in_perm=d_in_perm)


def kernel(x, g_pre, w_in, mla_q_norm, mla_wq_b, mla_kv_norm, mla_wkv_b, rwkv_mu, rwkv_w0_f, rwkv_w2_f, rwkv_w0_b, rwkv_w2_b, rwkv_a0_f, rwkv_a2_f, rwkv_a0_b, rwkv_a2_b, rwkv_k_k, rwkv_k_a, rwkv_r_k, rwkv_gn_g, rwkv_gn_b, w_br_mla, w_br_rwkv, w_out, g_post, loss_target, m_g_pre, m_w_in, m_mla_q_norm, m_mla_wq_b, m_mla_kv_norm, m_mla_wkv_b, m_rwkv_mu, m_rwkv_w0_f, m_rwkv_w2_f, m_rwkv_w0_b, m_rwkv_w2_b, m_rwkv_a0_f, m_rwkv_a2_f, m_rwkv_a0_b, m_rwkv_a2_b, m_rwkv_k_k, m_rwkv_k_a, m_rwkv_r_k, m_rwkv_gn_g, m_rwkv_gn_b, m_w_br_mla, m_w_br_rwkv, m_w_out, m_g_post, v_g_pre, v_w_in, v_mla_q_norm, v_mla_wq_b, v_mla_kv_norm, v_mla_wkv_b, v_rwkv_mu, v_rwkv_w0_f, v_rwkv_w2_f, v_rwkv_w0_b, v_rwkv_w2_b, v_rwkv_a0_f, v_rwkv_a2_f, v_rwkv_a0_b, v_rwkv_a2_b, v_rwkv_k_k, v_rwkv_k_a, v_rwkv_r_k, v_rwkv_gn_g, v_rwkv_gn_b, v_w_br_mla, v_w_br_rwkv, v_w_out, v_g_post):
    inp = dict(locals())
    dims = _dims(inp)
    stored = lambda t, n: t.T if n in _TRANSPOSED else t
    assert _MATS[0] == "w_in"
    shards = [stored(inp[n], n).astype(BF16) for n in _MATS]
    core = lax.axis_index("c").astype(jnp.int32).reshape(1)
    (w_in_slabs,) = _run_exchange(_gather_plan(shards[:1]), name="gather_w_in")
    W = {"w_in_t": _prepare_w_in(w_in_slabs, dims), **_prepare_vectors({n: inp[n] for n in _VECS}, dims)}
    loss, grad_x, g, recv_rest = _local_grads(x[0], loss_target[0], W, dims, exchange=(shards[1:], core))

    new = {}
    *recv_rest, got = recv_rest
    g_w_in, g = g["w_in"], _restore_vectors(g, dims)
    vsizes = [inp[n].size for n in _VECS] + [1]
    vflat = lambda prefix, src, last: _pack([src[prefix + n].reshape(-1) for n in _VECS] + [last])
    one = jnp.zeros((1,), F32)
    recv_w_in, vrecv = _run_exchange(
        _join_plans(_chip_exchange_plan([_pair_add(core, g_w_in, got, name="pair_add_w_in")]),
                    _direct_gather_plan(vflat("", g, loss.reshape(1)))), name="scatter_w_in")
    for n, t in zip(_MATS, [recv_w_in] + recv_rest):
        out = _adamw(t, stored(inp[n], n), stored(inp["m_" + n], n), stored(inp["v_" + n], n), name="adamw_" + n)
        new[n] = [stored(o, n) for o in out]

    vout = _adamw(vrecv, vflat("", inp, one), vflat("m_", inp, one), vflat("v_", inp, one), name="adamw_vectors")
    vparts = [_unpack(t, vsizes) for t in vout]
    for i, n in enumerate(_VECS):
        new[n] = [vp[i].reshape(inp[n].shape) for vp in vparts]
    loss = vparts[0][-1].reshape(())

    outs = [loss, grad_x[None]]
    for k in range(4):
        outs += [new[n][k] for n in _WEIGHTS]
    return tuple(outs)
```

```python
import functools
import math

import jax
import jax.numpy as jnp
from jax import lax
from jax.experimental import pallas as pl
from jax.experimental.pallas import tpu as pltpu

F32 = jnp.float32
BF16 = jnp.bfloat16

N_DEV = 8
LANES = 128
BF16_ROWS = 16
NOPE, ROPE, VDIM = 128, 64, 128
QHEAD = 256
ROPE_THETA = 10000.0
NORM_EPS = 1e-6
GN_EPS = 64e-5
CHUNK = 64
SUB = 16
VMEM_LIMIT = 56 * 1024 * 1024

ADAM_LR, ADAM_B1, ADAM_B2, ADAM_EPS, ADAM_WD, ADAM_STEP = 0.001, 0.9, 0.999, 1e-08, 0.01, 10


def _cparams(sem):
    return pltpu.CompilerParams(dimension_semantics=sem, vmem_limit_bytes=VMEM_LIMIT)


def _pick(n, cap):
    if n <= cap:
        return n
    for t in range(cap - cap % LANES, 0, -LANES):
        if n % t == 0:
            return t
    raise ValueError(f"no tile for {n} under {cap}")


def _mm(a, b, *, ta=False, tb=False, out_dtype=F32, name, tm_cap=1024, tn_cap=512, tk_cap=2048, ride=None):
    K, M = a.shape if ta else a.shape[::-1]
    N = b.shape[0] if tb else b.shape[1]
    assert (b.shape[1] if tb else b.shape[0]) == K, (a.shape, b.shape, ta, tb)
    tm, tn, tk = _pick(M, tm_cap), _pick(N, tn_cap), _pick(K, tk_cap)
    nj, nk = N // tn, K // tk
    steps = (M // tm) * nj * nk
    dn = (((0 if ta else 1,), (1 if tb else 0,)), ((), ()))
    srcs, extra_shapes, sem_shapes, phases = ride if ride else ((), (), (), None)
    n_src, n_extra = len(srcs), len(extra_shapes)

    def body(*refs):
        a_ref, b_ref, o_ref = refs[0], refs[1], refs[2 + n_src]
        acc_ref = refs[3 + n_src + n_extra]
        k = pl.program_id(2)
        if ride:
            step = (pl.program_id(0) * nj + pl.program_id(1)) * nk + k
            first, middle, last = phases(refs[2:2 + n_src], refs[3 + n_src:3 + n_src + n_extra],
                                         refs[4 + n_src + n_extra:])
            pl.when(step == 0)(first)
            pl.when(step == (steps * 15) // 16)(middle)
        p = lax.dot_general(a_ref[...], b_ref[...], dn, preferred_element_type=F32)

        @pl.when(k == 0)
        def _():
            acc_ref[...] = p

        @pl.when(k > 0)
        def _():
            acc_ref[...] += p

        @pl.when(k == nk - 1)
        def _():
            o_ref[...] = acc_ref[...].astype(out_dtype)

        if ride:
            pl.when(step == steps - 1)(last)

    a_spec = pl.BlockSpec((tk, tm), lambda i, j, k: (k, i)) if ta else pl.BlockSpec((tm, tk), lambda i, j, k: (i, k))
    b_spec = pl.BlockSpec((tn, tk), lambda i, j, k: (j, k)) if tb else pl.BlockSpec((tk, tn), lambda i, j, k: (k, j))
    hbm = pl.BlockSpec(memory_space=pl.ANY)
    out = pl.pallas_call(
        body, name=name, grid=(M // tm, nj, nk),
        in_specs=[a_spec, b_spec] + [hbm] * n_src,
        out_specs=[pl.BlockSpec((tm, tn), lambda i, j, k: (i, j))] + [hbm] * n_extra,
        out_shape=[jax.ShapeDtypeStruct((M, N), out_dtype)] + list(extra_shapes),
        scratch_shapes=[pltpu.VMEM((tm, tn), F32)] + [pltpu.SemaphoreType.DMA(s) for s in sem_shapes],
        compiler_params=_cparams(("arbitrary",) * 3 if ride else ("parallel", "parallel", "arbitrary")),
    )(a, b, *srcs)
    return out if ride else out[0]


def _view(arr, off, width):
    assert off % width == 0, (off, width)
    return (arr, off // width, width)


def _rowwise(fn, rows, params, out_rows, out_accs=(), *, tile, name):
    rows = [r if isinstance(r, tuple) else (r, 0, r.shape[1]) for r in rows]
    S = rows[0][0].shape[0]
    T = min(tile, S)
    assert S % T == 0
    n_rows, n_par, n_out = len(rows), len(params), len(out_rows)
    into = [o[2] if len(o) == 3 else None for o in out_rows]
    carried = [t[0] for t in into if t is not None and t[0] is not None]

    def body(*refs):
        ins = [r[...] for r in refs[:n_rows + n_par]]
        outs = fn(*ins)
        out_refs = refs[n_rows + n_par + len(carried):]
        for o_ref, val in zip(out_refs[:n_out], outs[:n_out]):
            o_ref[...] = val.astype(o_ref.dtype)
        i = pl.program_id(0)
        for o_ref, val in zip(out_refs[n_out:], outs[n_out:]):
            @pl.when(i == 0)
            def _(o_ref=o_ref, val=val):
                o_ref[...] = val

            @pl.when(i > 0)
            def _(o_ref=o_ref, val=val):
                o_ref[...] += val

    in_specs = [pl.BlockSpec((T, w), functools.partial(lambda i, cb: (i, cb), cb=cb)) for _, cb, w in rows]
    in_specs += [pl.BlockSpec(p.shape, lambda i: (0, 0)) for p in params]
    in_specs += [pl.BlockSpec(memory_space=pl.ANY)] * len(carried)
    out_specs, out_shape, aliases = [], [], {}
    for k, (o, t) in enumerate(zip(out_rows, into)):
        w, dt = o[0], o[1]
        if t is None:
            out_specs.append(pl.BlockSpec((T, w), lambda i: (i, 0)))
            out_shape.append(jax.ShapeDtypeStruct((S, w), dt))
            continue
        buf, total, first = t
        assert first % w == 0
        out_specs.append(pl.BlockSpec((T, w), functools.partial(lambda i, cb: (i, cb), cb=first // w)))
        out_shape.append(jax.ShapeDtypeStruct((S, total), dt))
        if buf is not None:
            aliases[n_rows + n_par + len(aliases)] = k
    out_specs += [pl.BlockSpec(s, lambda i: (0, 0)) for s in out_accs]
    out_shape += [jax.ShapeDtypeStruct(s, F32) for s in out_accs]
    return pl.pallas_call(
        body, name=name, grid=(S // T,), in_specs=in_specs, out_specs=out_specs, out_shape=out_shape,
        input_output_aliases=aliases, compiler_params=_cparams(("arbitrary",)),
    )(*[r[0] for r in rows], *params, *carried)


def _mm_sel(x, sel2):
    hi = x.astype(BF16)
    lo = (x - hi.astype(F32)).astype(BF16)
    return jnp.dot(jnp.concatenate([hi, lo], axis=1), sel2, preferred_element_type=F32)


@jax.custom_vjp
def _sel(x, sel, sel_t):
    return _mm_sel(x, sel)


def _sel_fwd(x, sel, sel_t):
    return _mm_sel(x, sel), (sel, sel_t)


def _sel_bwd(res, ct):
    sel, sel_t = res
    return _mm_sel(ct, sel_t), jnp.zeros_like(sel), jnp.zeros_like(sel_t)


_sel.defvjp(_sel_fwd, _sel_bwd)


def _rms(x, g):
    return x * lax.rsqrt(jnp.mean(x * x, axis=-1, keepdims=True) + NORM_EPS) * g


def _sigmoid(x):
    return 1.0 / (1.0 + jnp.exp(-x))


def _silu(x):
    return x * _sigmoid(x)


def _softplus(x):
    return jnp.maximum(x, 0.0) + jnp.log(1.0 + jnp.exp(-jnp.abs(x)))


def _f_mla_norm(q_a, kv_a, qg, kvg):
    return _rms(q_a, qg), _rms(kv_a, kvg)


def _f_rope(hm, qraw, kr_in, cosx, sinx, rot, rot_t):
    def rope(t):
        return t * cosx + _sel(t, rot, rot_t) * sinx
    parts = []
    for h in range(hm):
        parts.append(qraw[:, h * QHEAD:h * QHEAD + NOPE])
        parts.append(rope(qraw[:, h * QHEAD + NOPE:(h + 1) * QHEAD]))
    return jnp.concatenate(parts, axis=1), rope(kr_in)


def _f_rwkv_pre(rw, k, tail, w0f, w0b, a0f, a0b, k_k, k_a, w2cat, a2cat, seg, seg_t):
    split = w2cat.shape[0]
    zw = jnp.dot(jnp.tanh(tail[:, :split]).astype(BF16), w2cat, preferred_element_type=F32)
    za = jnp.dot(tail[:, split:].astype(BF16), a2cat, preferred_element_type=F32)
    return _f_rwkv_core(rw, k, zw, za, w0f, w0b, a0f, a0b, k_k, k_a, seg, seg_t)


def _f_rwkv_core(rw, k, zw, za, w0f, w0b, a0f, a0b, k_k, k_a, seg, seg_t):
    lw_f = -jnp.exp(-_softplus(-(w0f + zw[:, :rw])) - 0.5)
    lw_b = -jnp.exp(-_softplus(-(w0b + zw[:, rw:])) - 0.5)
    a_f = _sigmoid(a0f + za[:, :rw])
    a_b = _sigmoid(a0b + za[:, rw:])
    kk = k * k_k
    nrm = jnp.sqrt(_sel(_sel(kk * kk, seg, seg_t), seg_t, seg))
    kk = kk / jnp.maximum(nrm, 1e-12)
    k_f = k * (1.0 + (a_f - 1.0) * k_a)
    k_b = k * (1.0 + (a_b - 1.0) * k_a)
    return lw_f, lw_b, k_f, k_b, -kk, kk * a_f, kk * a_b


def _f_post(hn, y_f, y_b, r, k_f, k_b, v, z_r, o_mla, z_m, gn_g, gn_b, r_k, seg, seg_t):
    segsum = lambda t: _sel(_sel(t, seg, seg_t), seg_t, seg)
    y = y_f + y_b
    mu = segsum(y) * (1.0 / hn)
    yc = y - mu
    var = segsum(yc * yc) * (1.0 / hn)
    yn = yc * lax.rsqrt(var + GN_EPS) * gn_g + gn_b
    bonus = segsum(r * (k_f + k_b) * r_k) * v
    return o_mla * _silu(z_m), (yn + bonus) * _silu(z_r)


def _f_merge(u_m, u_r, g_m, g_r):
    return _sigmoid(g_m) * u_m + _sigmoid(g_r) * u_r


_NN = ((2,), (1,))
_NT = ((2,), (2,))
_TN = ((1,), (1,))

_SCAN_PASSES = {"cum": 2, "gram": 3, "solve": 1, "apply": 1, "state": 1}


def _hdot_raw(passes, x, y, dims):
    dn = (dims, ((0,), (0,)))
    d = lambda p, q: lax.dot_general(p, q, dn, preferred_element_type=F32)
    xh = x.astype(BF16)
    yh = y.astype(BF16)
    if passes == 1:
        return d(xh, yh)
    yl = (y - yh.astype(F32)).astype(BF16)
    kx, ky = (1 if dims == _TN else 2), (2 if dims == _NT else 1)
    depth = x.shape[kx]
    if all(axis == 1 or depth % LANES == 0 for axis in (kx, ky)):
        if passes == 2:
            return d(jnp.concatenate([xh, xh], axis=kx), jnp.concatenate([yh, yl], axis=ky))
        xl = (x - xh.astype(F32)).astype(BF16)
        return d(jnp.concatenate([xh, xl, xh], axis=kx), jnp.concatenate([yh, yh, yl], axis=ky))
    if passes == 2:
        axis = 1 if dims == _NT else 2
        width = y.shape[axis]
        both = d(xh, jnp.concatenate([yh, yl], axis=axis))
        return both[:, :, :width] + both[:, :, width:]
    xl = (x - xh.astype(F32)).astype(BF16)
    if dims == _TN:
        return d(xh, yh) + d(xh, yl) + d(xl, yh)
    rows = x.shape[1]
    both = d(jnp.concatenate([xh, xl], axis=1), yh)
    return both[:, :rows] + both[:, rows:] + d(xh, yl)


@functools.partial(jax.custom_vjp, nondiff_argnums=(2, 3))
def _hdot_p(x, y, dims, passes):
    return _hdot_raw(passes, x, y, dims)


def _hdot_fwd(x, y, dims, passes):
    return _hdot_raw(passes, x, y, dims), (x, y)


def _hdot_bwd(dims, passes, res, ct):
    x, y = res
    if dims == _NN:
        return _hdot_raw(passes, ct, y, _NT), _hdot_raw(passes, x, ct, _TN)
    if dims == _NT:
        return _hdot_raw(passes, ct, y, _NN), _hdot_raw(passes, ct, x, _TN)
    return _hdot_raw(passes, y, ct, _NT), _hdot_raw(passes, x, ct, _NN)


_hdot_p.defvjp(_hdot_fwd, _hdot_bwd)


def _hdot(x, y, dims, kind):
    return _hdot_p(x, y, dims, _SCAN_PASSES[kind])


def _tri_solve(n_mat, x, length, blocks):
    row = lax.broadcasted_iota(jnp.int32, (length, 2 * length), 0)
    col = lax.broadcasted_iota(jnp.int32, (length, 2 * length), 1)
    col = jnp.where(col >= length, col - length, col)
    eye = (row == col).astype(F32)[None]
    diag_blk = ((row // SUB) == (col // SUB))[None]
    nd = jnp.where(diag_blk, n_mat, 0.0)
    no = n_mat - nd
    dinv = eye + nd
    p = _hdot(nd, blocks(nd), _NN, "solve")
    for k in range(int(math.log2(SUB)) - 1):
        if k == int(math.log2(SUB)) - 2:
            dinv = dinv + _hdot(dinv, blocks(p), _NN, "solve")
        else:
            both = _hdot(jnp.concatenate([dinv, p], axis=1), blocks(p), _NN, "solve")
            dinv, p = dinv + both[:, :length], both[:, length:]
    width = x.shape[2]
    both = _hdot(dinv, jnp.concatenate([blocks(x), blocks(no)], axis=2), _NN, "solve")
    u, q = both[:, :, :width], both[:, :, width:]
    for level in range(int(math.log2(length // SUB))):
        if level == int(math.log2(length // SUB)) - 1:
            u = u + _hdot(q, blocks(u), _NN, "solve")
        else:
            both = _hdot(q, jnp.concatenate([blocks(u), blocks(q)], axis=2), _NN, "solve")
            u, q = u + both[:, :, :width], both[:, :, width:]
    return u


def _rwkv_chunk(rev, s0, r, lw, k, v, a, b):
    pairs, length, width = r.shape
    hn = width // 2
    assert 2 * length == width
    row = lax.broadcasted_iota(jnp.int32, (length, length), 0)
    col = lax.broadcasted_iota(jnp.int32, (length, length), 1)
    row2 = lax.broadcasted_iota(jnp.int32, (length, 2 * length), 0)
    col2 = lax.broadcasted_iota(jnp.int32, (length, 2 * length), 1)
    col2 = jnp.where(col2 >= length, col2 - length, col2)
    if rev is None:
        half = pairs // 2
        back = lax.broadcasted_iota(jnp.int32, (pairs, length, length), 0) >= half
        back2 = lax.broadcasted_iota(jnp.int32, (pairs, length, 2 * length), 0) >= half
        ahead = jnp.where(back, (col - row)[None], (row - col)[None])
        ahead2 = jnp.where(back2, (col2 - row2)[None], (row2 - col2)[None])
        incl, strict2, incl2 = ahead >= 0, ahead2 > 0, ahead2 >= 0
    else:
        incl = ((row <= col) if rev else (row >= col))[None]
        strict2 = ((row2 < col2) if rev else (row2 > col2))[None]
        incl2 = ((row2 <= col2) if rev else (row2 >= col2))[None]
    first = (lax.broadcasted_iota(jnp.int32, (1, 1, width), 2) < hn).astype(F32)
    blocks = lambda t: jnp.concatenate([t * first, t * (1.0 - first)], axis=1)

    t_incl = jnp.broadcast_to(incl.astype(F32), (pairs, length, length))
    cum = _hdot(t_incl, lw, _NN, "cum")
    g = jnp.exp(cum)
    g_inv = jnp.exp(-cum)
    at = a * jnp.exp(cum - lw)
    rt = r * g
    bt = b * g_inv
    kt = k * g_inv
    both_rows = jnp.concatenate([at, rt], axis=1)
    gram = _hdot(both_rows, jnp.concatenate([blocks(bt), blocks(kt)], axis=1), _NT, "gram")
    a_ab = jnp.where(strict2, gram[:, :length, :width], 0.0)
    a_ak = jnp.where(strict2, gram[:, :length, width:], 0.0)
    a_rb = jnp.where(incl2, gram[:, length:, :width], 0.0)
    a_rk = jnp.where(incl2, gram[:, length:, width:], 0.0)
    from_state = _hdot(both_rows, s0, _NT, "apply")
    x = from_state[:, :length] + _hdot(a_ak, blocks(v), _NN, "apply")
    u = _tri_solve(a_ab, x, length, blocks)
    y = from_state[:, length:] + _hdot(jnp.concatenate([a_rb, a_rk], axis=2),
                                       jnp.concatenate([blocks(u), blocks(v)], axis=1), _NN, "apply")
    g_last = jnp.exp(jnp.sum(lw, axis=1, keepdims=True))
    ri = lax.broadcasted_iota(jnp.int32, (width, width), 0)
    ci = lax.broadcasted_iota(jnp.int32, (width, width), 1)
    same_head = ((ri < hn) == (ci < hn))[None]
    upd = _hdot(jnp.concatenate([u, v], axis=1), jnp.concatenate([bt, kt], axis=1), _TN, "state")
    s1 = (s0 + jnp.where(same_head, upd, 0.0)) * g_last
    return y, s1


def _split_pairs(x):
    return jnp.stack([x[:, p * LANES:(p + 1) * LANES] for p in range(x.shape[1] // LANES)])


def _merge_pairs(x):
    return jnp.concatenate([x[p] for p in range(x.shape[0])], axis=1)


def _scan_specs(views, rw, nc, rev):
    cidx = (lambda c: nc - 1 - c) if rev else (lambda c: c)
    seqs = [pl.BlockSpec((CHUNK, rw), functools.partial(lambda c, cb: (cidx(c), cb), cb=cb)) for _, cb, _ in views]
    plain = pl.BlockSpec((CHUNK, rw), lambda c: (cidx(c), 0))
    st = pl.BlockSpec((1, rw // LANES, LANES, LANES), lambda c: (cidx(c), 0, 0, 0))
    return seqs, plain, st


def _as_views(arrs, rw):
    return [t if isinstance(t, tuple) else (t, 0, rw) for t in arrs]


def _rwkv_scan_fwd(ops_f, ops_b, rw, *, name):
    S = _as_views(ops_f, rw)[0][0].shape[0]
    nc, pairs = S // CHUNK, rw // LANES
    in_specs, out_specs, arrays = [], [], []
    for rev, ops in ((False, ops_f), (True, ops_b)):
        views = _as_views(ops, rw)
        seqs, plain, st = _scan_specs(views, rw, nc, rev)
        in_specs += seqs
        out_specs += [plain, st]
        arrays += [t[0] for t in views]

    def both(refs_f, refs_b):
        return [jnp.concatenate([_split_pairs(f[...]), _split_pairs(b[...])], axis=0) for f, b in zip(refs_f, refs_b)]

    def body(*refs):
        (y_f, st_f, y_b, st_b), s_ref = refs[12:16], refs[16]

        @pl.when(pl.program_id(0) == 0)
        def _():
            s_ref[...] = jnp.zeros_like(s_ref)

        s0 = s_ref[...]
        st_f[0] = s0[:pairs]
        st_b[0] = s0[pairs:]
        y, s1 = _rwkv_chunk(None, s0, *both(refs[:6], refs[6:12]))
        y_f[...] = _merge_pairs(y[:pairs])
        y_b[...] = _merge_pairs(y[pairs:])
        s_ref[...] = s1

    return pl.pallas_call(
        body, name=name, grid=(nc,), in_specs=in_specs, out_specs=out_specs,
        out_shape=[jax.ShapeDtypeStruct((S, rw), F32), jax.ShapeDtypeStruct((nc, pairs, LANES, LANES), F32)] * 2,
        scratch_shapes=[pltpu.VMEM((2 * pairs, LANES, LANES), F32)],
        compiler_params=_cparams(("arbitrary",)),
    )(*arrays)


def _rwkv_scan_bwd(ops_f, ops_b, states_f, states_b, dy, rw, *, name):
    S = dy.shape[0]
    nc, pairs = S // CHUNK, rw // LANES
    in_specs, arrays = [], []
    for rev, ops, states in ((False, ops_f, states_f), (True, ops_b, states_b)):
        views = _as_views(list(ops) + [dy], rw)
        seqs, plain, st = _scan_specs(views, rw, nc, not rev)
        in_specs += seqs + [st]
        arrays += [t[0] for t in views] + [states]
    out_specs = []
    for rev in (False, True):
        out_specs += [_scan_specs([], rw, nc, not rev)[1]] * 6

    def both(refs_f, refs_b):
        return [jnp.concatenate([_split_pairs(f[...]), _split_pairs(b[...])], axis=0) for f, b in zip(refs_f, refs_b)]

    def body(*refs):
        ds_ref = refs[28]

        @pl.when(pl.program_id(0) == 0)
        def _():
            ds_ref[...] = jnp.zeros_like(ds_ref)

        s0 = jnp.concatenate([refs[7][0], refs[15][0]], axis=0)
        _, vjp = jax.vjp(functools.partial(_rwkv_chunk, None), s0, *both(refs[:6], refs[8:14]))
        (dy,) = both(refs[6:7], refs[14:15])
        grads = vjp((dy, ds_ref[...]))
        ds_ref[...] = grads[0]
        for o_f, o_b, gval in zip(refs[16:22], refs[22:28], grads[1:]):
            o_f[...] = _merge_pairs(gval[:pairs])
            o_b[...] = _merge_pairs(gval[pairs:])

    return pl.pallas_call(
        body, name=name, grid=(nc,), in_specs=in_specs, out_specs=out_specs,
        out_shape=[jax.ShapeDtypeStruct((S, rw), F32)] * 12,
        scratch_shapes=[pltpu.VMEM((2 * pairs, LANES, LANES), F32)],
        compiler_params=_cparams(("arbitrary",)),
    )(*arrays)


def _shift_lerp(x_view, mu, d=None, into=None, *, name):
    arr, off, width = x_view
    S = arr.shape[0]
    cb = _pick(width, 512)
    assert off % cb == 0

    def cshift(t):
        rows = lax.broadcasted_iota(jnp.int32, t.shape, 0)
        prev = jnp.where(rows == 0, 0.0, pltpu.roll(t, 1, 0))
        nxt = jnp.where(rows == S - 1, 0.0, pltpu.roll(t, S - 1, 0))
        return 0.5 * (prev + nxt)

    def fwd_body(x_ref, mu_ref, o_ref):
        x = x_ref[...]
        o_ref[...] = x + mu_ref[...] * (cshift(x) - x)

    def bwd_body(x_ref, mu_ref, d_ref, _, dx_ref, dmu_ref):
        x, m, dd = x_ref[...], mu_ref[...], d_ref[...]
        gm = m * dd
        dx_ref[...] = (dd - gm + cshift(gm)).astype(dx_ref.dtype)
        dmu_ref[...] = jnp.sum(dd * (cshift(x) - x), axis=0, keepdims=True)

    x_spec = pl.BlockSpec((S, cb), lambda j: (0, off // cb + j))
    blk = pl.BlockSpec((S, cb), lambda j: (0, j))
    vec = pl.BlockSpec((1, cb), lambda j: (0, j))
    if d is None:
        return pl.pallas_call(
            fwd_body, name=name, grid=(width // cb,), in_specs=[x_spec, vec], out_specs=blk,
            out_shape=jax.ShapeDtypeStruct((S, width), F32), compiler_params=_cparams(("parallel",)),
        )(arr, mu)
    buf, first = into
    assert first % cb == 0
    return pl.pallas_call(
        bwd_body, name=name, grid=(width // cb,),
        in_specs=[x_spec, vec, blk, pl.BlockSpec(memory_space=pl.ANY)],
        out_specs=[pl.BlockSpec((S, cb), lambda j: (0, first // cb + j)), vec],
        out_shape=[jax.ShapeDtypeStruct(buf.shape, buf.dtype), jax.ShapeDtypeStruct((1, width), F32)],
        input_output_aliases={3: 0}, compiler_params=_cparams(("parallel",)),
    )(arr, mu, d, buf)


def _attention_fwd(qfull, kv, kr, hm, scale, *, tq, name):
    S = qfull.shape[0]
    nt = (((1,), (1,)), ((), ()))

    def body(q_ref, kn_ref, kr_ref, v_ref, o_ref, lse_ref, k_scr):
        _head_keys(kn_ref, kr_ref, k_scr)
        s = lax.dot_general(q_ref[...], k_scr[...], nt, preferred_element_type=F32)
        m = jnp.max(s, axis=-1, keepdims=True)
        p = jnp.exp((s - m) * scale)
        l = jnp.sum(p, axis=-1, keepdims=True)
        o_ref[...] = jnp.dot(p.astype(BF16), v_ref[...], preferred_element_type=F32) * (1.0 / l)
        lse_ref[...] = jnp.broadcast_to(m * scale + jnp.log(l), lse_ref.shape)

    oblk = pl.BlockSpec((tq, VDIM), lambda h, i: (i, h))
    return pl.pallas_call(
        body, name=name, grid=(hm, S // tq),
        in_specs=[pl.BlockSpec((tq, QHEAD), lambda h, i: (i, h)),
                  pl.BlockSpec((S, NOPE), lambda h, i: (0, h)),
                  pl.BlockSpec((S, LANES), lambda h, i: (0, 0)),
                  pl.BlockSpec((S, VDIM), lambda h, i: (0, hm + h))],
        out_specs=[oblk, oblk],
        out_shape=[jax.ShapeDtypeStruct((S, hm * VDIM), F32)] * 2,
        scratch_shapes=[pltpu.VMEM((S, QHEAD), BF16)],
        compiler_params=_cparams(("parallel", "arbitrary")),
    )(qfull, kv, kr, kv)


def _head_keys(kn_ref, kr_ref, k_scr):
    @pl.when(pl.program_id(1) == 0)
    def _():
        k_scr[:, :NOPE] = kn_ref[...]
        k_scr[:, NOPE:] = kr_ref[...]


def _attention_bwd(qfull, kv, kr, o, lse, d_o, hm, scale, *, tq, name):
    S = qfull.shape[0]
    tq = min(tq, S)
    nq = S // tq
    tn = (((0,), (0,)), ((), ()))
    nt = (((1,), (1,)), ((), ()))

    def body(q_ref, kn_ref, kr_ref, v_ref, o_ref, lse_ref, do_ref, dq_ref, dk_ref, dv_ref, k_scr):
        _head_keys(kn_ref, kr_ref, k_scr)
        s = lax.dot_general(q_ref[...], k_scr[...], nt, preferred_element_type=F32)
        p = jnp.exp(s * scale - lse_ref[:, 0:1])
        d_out = do_ref[...]
        delta = jnp.sum(d_out * o_ref[...], axis=-1, keepdims=True)
        d_out = d_out.astype(BF16)
        dp = lax.dot_general(d_out, v_ref[...], nt, preferred_element_type=F32)
        ds = (p * (dp - delta)).astype(BF16)
        dq_ref[...] = jnp.dot(ds, k_scr[...], preferred_element_type=F32) * scale
        dv = lax.dot_general(p.astype(BF16), d_out, tn, preferred_element_type=F32)
        dk = lax.dot_general(ds, q_ref[...], tn, preferred_element_type=F32)
        i = pl.program_id(1)
        for ref, val in ((dk_ref, dk), (dv_ref, dv)):
            @pl.when(i == 0)
            def _(ref=ref, val=val):
                ref[...] = val

            @pl.when(i > 0)
            def _(ref=ref, val=val):
                ref[...] += val

        @pl.when(i == nq - 1)
        def _():
            dk_ref[...] = dk_ref[...] * scale

    qblk = pl.BlockSpec((tq, QHEAD), lambda h, i: (i, h))
    oblk = pl.BlockSpec((tq, VDIM), lambda h, i: (i, h))
    return pl.pallas_call(
        body, name=name, grid=(hm, nq),
        in_specs=[qblk,
                  pl.BlockSpec((S, NOPE), lambda h, i: (0, h)),
                  pl.BlockSpec((S, LANES), lambda h, i: (0, 0)),
                  pl.BlockSpec((S, VDIM), lambda h, i: (0, hm + h)),
                  oblk, oblk, oblk],
        out_specs=[qblk, pl.BlockSpec((S, QHEAD), lambda h, i: (0, h)), pl.BlockSpec((S, VDIM), lambda h, i: (0, h))],
        out_shape=[jax.ShapeDtypeStruct((S, hm * QHEAD), F32), jax.ShapeDtypeStruct((S, hm * QHEAD), F32),
                   jax.ShapeDtypeStruct((S, hm * VDIM), F32)],
        scratch_shapes=[pltpu.VMEM((S, QHEAD), BF16)],
        compiler_params=_cparams(("parallel", "arbitrary")),
    )(qfull, kv, kr, kv, o, lse, d_o)


def _layout(D, MW, RW, TAIL, QR, KVR):
    names = ["gate_m", "gate_r", "z_m", "z_r", "q_a", "kv_a", "r", "k", "v", "tail"]
    widths = [D, D, MW, RW, QR, KVR, RW, RW, RW, TAIL]
    offs, o = {}, 0
    for nme, w in zip(names, widths):
        assert o % w == 0, (nme, o, w)
        offs[nme] = (o, w)
        o += w
    return offs, o


def _local_grads(x, target, W, dims, exchange=None):
    S, D = x.shape
    hm, hr, hn, rank = dims["hm"], dims["hr"], dims["hn"], dims["rank"]
    MW, RW = hm * VDIM, hr * hn
    TAIL = dims["TAIL"]
    QR, KVR = W["mla_q_norm"].shape[1], W["mla_kv_norm"].shape[1]
    lay, d_in = _layout(D, MW, RW, TAIL, QR, KVR)
    T = 256
    scale = (NOPE + ROPE) ** -0.5
    col = lambda arr, nme: _view(arr, *lay[nme])

    pos = jnp.arange(S, dtype=F32)
    inv_freq = jnp.power(ROPE_THETA, -jnp.arange(0, ROPE, 2, dtype=F32) / ROPE)
    ang = pos[:, None] * inv_freq[None, :]
    zpad = jnp.zeros((S, LANES - ROPE), F32)
    cosx = jnp.concatenate([jnp.cos(ang), jnp.cos(ang), zpad], axis=1)
    sinx = jnp.concatenate([jnp.sin(ang), jnp.sin(ang), zpad], axis=1)
    ri, ci = jnp.arange(LANES)[:, None], jnp.arange(LANES)[None, :]
    half = ROPE // 2
    rot = (jnp.where((ri == ci - half) & (ci >= half) & (ci < ROPE), 1.0, 0.0)
           - jnp.where((ri == ci + half) & (ci < half), 1.0, 0.0)).astype(BF16)
    seg = (jnp.arange(RW)[:, None] // hn == jnp.arange(LANES)[None, :]).astype(BF16)
    stacked = lambda t: jnp.concatenate([t, t], axis=0)
    rot, rot_t, seg, seg_t = stacked(rot), stacked(rot.T), stacked(seg), stacked(seg.T)

    (h,) = _rowwise(lambda xb, g: (_rms(xb, g),), [x], [W["g_pre"]], [(D, BF16)], tile=2 * T, name="pre_norm")
    if exchange is None:
        proj = _mm(h, W["w_in_t"], tb=True, name="in_proj")
    else:
        proj, *slabs = _mm(h, W["w_in_t"], tb=True, ride=_gather_plan(exchange[0]), name="in_proj")
        W = {**W, **_prepare_rest(dict(zip(_MATS[1:], slabs)), dims)}

    qn, kvn = _rowwise(_f_mla_norm, [col(proj, "q_a"), col(proj, "kv_a")], [W["mla_q_norm"], W["mla_kv_norm"]],
                       [(QR, BF16), (KVR, BF16)], tile=2 * T, name="mla_norm")
    qraw = _mm(qn, W["wq_b_t"], tb=True, name="q_up")
    kv = _mm(kvn, W["wkv_b"], out_dtype=BF16, name="kv_up")
    kr_view = _view(proj, lay["tail"][0], LANES)
    qfull, kr = _rowwise(functools.partial(_f_rope, hm), [qraw, kr_view, cosx, sinx], [rot, rot_t],
                         [(hm * QHEAD, BF16), (LANES, BF16)], tile=2 * T, name="rope")
    o_mla, lse = _attention_fwd(qfull, kv, kr, hm, scale, tq=T, name="attn_fwd")

    shift_view = (proj, lay["r"][0], 3 * RW + TAIL)
    rl = _shift_lerp(shift_view, W["mu"], name="shift_fwd")
    rl_r, rl_k, rl_v = _view(rl, 0, RW), _view(rl, RW, RW), _view(rl, 2 * RW, RW)
    rl_tail = _view(rl, 3 * RW, TAIL)
    pre_params = [W["w0_f"], W["w0_b"], W["a0_f"], W["a0_b"], W["k_k"], W["k_a"], W["w2cat"], W["a2cat"], seg, seg_t]
    pre_fn = functools.partial(_f_rwkv_pre, RW)
    lw_f, lw_b, k_f, k_b, a_n, b_f, b_b = _rowwise(pre_fn, [rl_k, rl_tail], pre_params, [(RW, F32)] * 7, tile=T,
                                                    name="rwkv_pre")
    ops_f = (rl_r, lw_f, k_f, rl_v, a_n, b_f)
    ops_b = (rl_r, lw_b, k_b, rl_v, a_n, b_b)
    y_f, st_f, y_b, st_b = _rwkv_scan_fwd(ops_f, ops_b, RW, name="scan_fwd")

    post_fn = functools.partial(_f_post, hn)
    post_rows = [y_f, y_b, rl_r, k_f, k_b, rl_v, col(proj, "z_r"), o_mla, col(proj, "z_m")]
    post_params = [W["gn_g"], W["gn_b"], W["r_k"], seg, seg_t]
    ymg, yrg = _rowwise(post_fn, post_rows, post_params, [(MW, BF16), (RW, BF16)], tile=T, name="post")
    u_m = _mm(ymg, W["w_br_mla"], name="br_mla")
    u_r = _mm(yrg, W["w_br_rwkv"], name="br_rwkv")
    merge_rows = [u_m, u_r, col(proj, "gate_m"), col(proj, "gate_r")]
    (merged,) = _rowwise(lambda *t: (_f_merge(*t),), merge_rows, [], [(D, BF16)], tile=T, name="merge")
    out = _mm(merged, W["w_out"], name="out_proj")

    def head(ob, xb, tb, g):
        yn, vjp = jax.vjp(_rms, ob, g)
        err = xb + yn - tb
        dy = err * (1.0 / D)
        d_ob, d_g = vjp(dy)
        loss = jnp.broadcast_to(0.5 * jnp.sum(err * err) * (1.0 / D), (1, LANES))
        return dy, d_ob, loss, d_g

    dy, d_out, loss, g_g_post = _rowwise(head, [out, x, target], [W["g_post"]], [(D, F32), (D, BF16)],
                                         [(1, LANES), (1, D)], tile=2 * T, name="head")
    d_merged = _mm(d_out, W["w_out"], tb=True, name="d_merged")
    g_w_out = _mm(merged, d_out, ta=True, out_dtype=BF16, name="g_w_out")

    def merge_bwd(u_m_b, u_r_b, g_m_b, g_r_b, dm):
        _, vjp = jax.vjp(_f_merge, u_m_b, u_r_b, g_m_b, g_r_b)
        du_m, du_r, dg_m, dg_r = vjp(dm)
        return du_m, du_r, jnp.concatenate([dg_m, dg_r], axis=1)

    d_u_m, d_u_r, d_proj = _rowwise(merge_bwd, merge_rows + [d_merged], [],
                                    [(D, BF16), (D, BF16), (2 * D, BF16, (None, d_in, lay["gate_m"][0]))], tile=T,
                                    name="merge_bwd")
    d_ymg = _mm(d_u_m, W["w_br_mla"], tb=True, name="d_ymg")
    d_yrg = _mm(d_u_r, W["w_br_rwkv"], tb=True, name="d_yrg")
    g_w_br_mla = _mm(ymg, d_u_m, ta=True, out_dtype=BF16, name="g_w_br_mla")
    g_w_br_rwkv = _mm(yrg, d_u_r, ta=True, out_dtype=BF16, name="g_w_br_rwkv")

    def post_bwd(*args):
        nr = len(post_rows)
        prim, dm, dr = args[:nr] + args[nr + 2:], args[nr], args[nr + 1]
        _, vjp = jax.vjp(post_fn, *prim)
        g = vjp((dm, dr))
        return g[0], g[2], g[3], g[5], g[7], jnp.concatenate([g[8], g[6]], axis=1), g[9], g[10], g[11]

    (d_y, d_r_bonus, d_k_bonus, d_v_bonus, d_o, d_proj, g_gn_g, g_gn_b, g_r_k) = _rowwise(
        post_bwd, post_rows + [d_ymg, d_yrg], post_params,
        [(RW, F32), (RW, F32), (RW, F32), (RW, F32), (MW, F32), (MW + RW, BF16, (d_proj, d_in, lay["z_m"][0]))],
        [(1, RW)] * 3, tile=T // 2, name="post_bwd")

    dscan = _rwkv_scan_bwd(ops_f, ops_b, st_f, st_b, d_y, RW, name="scan_bwd")
    dsc = {"f": dscan[:6], "b": dscan[6:]}

    d_q_att, d_k_att, d_v_att = _attention_bwd(qfull, kv, kr, o_mla, lse, d_o, hm, scale, tq=4 * T, name="attn_bwd")

    def rope_bwd(qraw_b, kr_in, cos_b, sin_b, dq_b, dk_b, dv_b, rot_b, rot_t_b):
        _, vjp = jax.vjp(lambda q_, k_: _f_rope(hm, q_, k_, cos_b, sin_b, rot_b, rot_t_b), qraw_b, kr_in)
        dkn = jnp.concatenate([dk_b[:, hh * QHEAD:hh * QHEAD + NOPE] for hh in range(hm)], axis=1)
        dkr = dk_b[:, NOPE:QHEAD]
        for hh in range(1, hm):
            dkr = dkr + dk_b[:, hh * QHEAD + NOPE:(hh + 1) * QHEAD]
        d_qraw, d_kr_in = vjp((dq_b, dkr))
        return d_qraw, jnp.concatenate([dkn, dv_b], axis=1), d_kr_in

    d_qraw, d_kv, d_kr_in = _rowwise(rope_bwd, [qraw, kr_view, cosx, sinx, d_q_att, d_k_att, d_v_att],
                                     [rot, rot_t], [(hm * QHEAD, BF16), (2 * MW, BF16), (LANES, F32)], tile=T,
                                     name="rope_bwd")
    d_qnorm = _mm(d_qraw, W["wq_b_t"], name="d_qn")
    d_kvnorm = _mm(d_kv, W["wkv_b"], tb=True, name="d_kvn")
    g_wq_b = _mm(d_qraw, qn, ta=True, out_dtype=BF16, name="g_wq_b")
    g_wkv_b = _mm(kvn, d_kv, ta=True, out_dtype=BF16, name="g_wkv_b")

    def mla_norm_bwd(q_a, kv_a, qg, kvg, dq, dk):
        _, vjp = jax.vjp(_f_mla_norm, q_a, kv_a, qg, kvg)
        d_q_a, d_kv_a, d_qg, d_kvg = vjp((dq, dk))
        return jnp.concatenate([d_q_a, d_kv_a], axis=1), d_qg, d_kvg

    d_proj, g_q_norm, g_kv_norm = _rowwise(
        lambda q_a, kv_a, dq, dk, qg, kvg: mla_norm_bwd(q_a, kv_a, qg, kvg, dq, dk),
        [col(proj, "q_a"), col(proj, "kv_a"), d_qnorm, d_kvnorm], [W["mla_q_norm"], W["mla_kv_norm"]],
        [(QR + KVR, BF16, (d_proj, d_in, lay["q_a"][0]))], [(1, QR), (1, KVR)], tile=2 * T, name="mla_norm_bwd")

    def pre_bwd(k_b_, tail_b, dlwf, dlwb, dkf, dkb, dkbon, daf, dab, dbf, dbb, drf, drb, drbon, dvf, dvb, dvbon,
                dkr, *params):
        w2, a2 = params[6], params[7]
        nt, tn = (((1,), (1,)), ((), ())), (((0,), (0,)), ((), ()))
        split = w2.shape[0]
        th = jnp.tanh(tail_b[:, :split])
        th_b, tail_h = th.astype(BF16), tail_b[:, split:].astype(BF16)
        zw = jnp.dot(th_b, w2, preferred_element_type=F32)
        za = jnp.dot(tail_h, a2, preferred_element_type=F32)
        _, vjp = jax.vjp(functools.partial(_f_rwkv_core, RW), k_b_, zw, za, *params[:6], params[8], params[9])
        g = vjp((dlwf, dlwb, dkf + dkbon, dkb + dkbon, daf + dab, dbf, dbb))
        d_zw, d_za = g[1].astype(BF16), g[2].astype(BF16)
        d_tail = (jnp.concatenate([lax.dot_general(d_zw, w2, nt, preferred_element_type=F32) * (1.0 - th * th),
                                   lax.dot_general(d_za, a2, nt, preferred_element_type=F32)], axis=1)
                  + jnp.concatenate([dkr, jnp.zeros((dkr.shape[0], TAIL - LANES), F32)], axis=1))
        g_w2 = lax.dot_general(th_b, d_zw, tn, preferred_element_type=F32)
        g_a2 = lax.dot_general(tail_h, d_za, tn, preferred_element_type=F32)
        d_rl = jnp.concatenate([drf + drb + drbon, g[0], dvf + dvb + dvbon, d_tail], axis=1)
        return (d_rl,) + tuple(g[3:9]) + (g_w2, g_a2)

    f_, b_ = dsc["f"], dsc["b"]
    pre_bwd_rows = [rl_k, rl_tail, f_[1], b_[1], f_[2], b_[2], d_k_bonus, f_[4], b_[4], f_[5], b_[5],
                    f_[0], b_[0], d_r_bonus, f_[3], b_[3], d_v_bonus, d_kr_in]
    (d_rl, g_w0_f, g_w0_b, g_a0_f, g_a0_b, g_k_k, g_k_a, g_w2cat, g_a2cat) = _rowwise(
        pre_bwd, pre_bwd_rows, pre_params, [(3 * RW + TAIL, F32)],
        [(1, RW)] * 6 + [W["w2cat"].shape, W["a2cat"].shape], tile=T // 2, name="rwkv_pre_bwd")
    d_proj, g_mu = _shift_lerp(shift_view, W["mu"], d_rl, (d_proj, lay["r"][0]), name="shift_bwd")
    small = dict(wq_b=g_wq_b, wkv_b=g_wkv_b, w2cat=g_w2cat, a2cat=g_a2cat, w_br_mla=g_w_br_mla,
                 w_br_rwkv=g_w_br_rwkv, w_out=g_w_out)
    if exchange is None:
        received = None
        g_w_in = _mm(d_proj, h, ta=True, out_dtype=BF16, tn_cap=1024, name="g_w_in")
        d_h = _mm(d_proj, W["w_in_t"], tn_cap=1024, name="d_h")
    else:
        slabs = _restore_rest(small, dims)
        slabs = [slabs[n] for n in _MATS[1:]]
        g_w_in, *got = _mm(d_proj, h, ta=True, out_dtype=BF16, tn_cap=1024, ride=_sibling_swap_plan(slabs),
                           name="g_w_in")
        sums = [_pair_add(exchange[1], s, t, name="pair_add_" + n) for n, s, t in zip(_MATS[1:], slabs, got)]
        g_w_in = _restore_w_in(g_w_in, dims)
        d_h, *received = _mm(d_proj, W["w_in_t"], tn_cap=1024, name="d_h",
                             ride=_join_plans(_chip_exchange_plan(sums), _sibling_swap_plan([g_w_in])))
        small = {}

    def pre_norm_bwd(xb, dyb, dhb, g):
        _, vjp = jax.vjp(_rms, xb, g)
        dx, dg = vjp(dhb)
        return dyb + dx, dg

    grad_x, g_g_pre = _rowwise(pre_norm_bwd, [x, dy, d_h], [W["g_pre"]], [(D, F32)], [(1, D)], tile=2 * T,
                               name="pre_norm_bwd")

    grads = dict(g_pre=g_g_pre, w_in=g_w_in, mla_q_norm=g_q_norm, mla_kv_norm=g_kv_norm, mu=g_mu, w0_f=g_w0_f,
                 w0_b=g_w0_b, a0_f=g_a0_f, a0_b=g_a0_b, k_k=g_k_k, k_a=g_k_a, r_k=g_r_k, gn_g=g_gn_g, gn_b=g_gn_b,
                 g_post=g_g_post, **small)
    return loss[0, 0], grad_x, grads, received


_MATS = ["w_in", "mla_wq_b", "mla_wkv_b", "rwkv_w2_f", "rwkv_w2_b", "rwkv_a2_f", "rwkv_a2_b", "w_br_mla",
         "w_br_rwkv", "w_out"]
_ROW_SHARDED = ("w_out",)
_TRANSPOSED = ("w_in", "mla_wq_b")
_VECS = ["g_pre", "mla_q_norm", "mla_kv_norm", "rwkv_mu", "rwkv_w0_f", "rwkv_w0_b", "rwkv_a0_f", "rwkv_a0_b",
         "rwkv_k_k", "rwkv_k_a", "rwkv_r_k", "rwkv_gn_g", "rwkv_gn_b", "g_post"]
_WEIGHTS = ["g_pre", "w_in", "mla_q_norm", "mla_wq_b", "mla_kv_norm", "mla_wkv_b", "rwkv_mu", "rwkv_w0_f",
            "rwkv_w2_f", "rwkv_w0_b", "rwkv_w2_b", "rwkv_a0_f", "rwkv_a2_f", "rwkv_a0_b", "rwkv_a2_b", "rwkv_k_k",
            "rwkv_k_a", "rwkv_r_k", "rwkv_gn_g", "rwkv_gn_b", "w_br_mla", "w_br_rwkv", "w_out", "g_post"]

def _direct_gather_plan(src):
    def phases(src_refs, out_refs, sem_refs):
        (src_ref,), (out_ref,), sems, local_sem = src_refs, out_refs, sem_refs[:2], sem_refs[2]
        x, y, c = lax.axis_index("x"), lax.axis_index("y"), lax.axis_index("c")
        me = 4 * x + 2 * y + c
        flip = lambda v, bit: (1 - v) if bit else v
        peers = [(flip(x, d & 4), flip(y, d & 2), flip(c, d & 1)) for d in range(1, N_DEV)]
        own = lambda: pltpu.make_async_copy(src_ref, out_ref.at[me], local_sem)
        send = lambda d: _remote(src_ref, out_ref.at[me], sems, d, peers[d])

        def first():
            own().start()
            for d in range(N_DEV - 1):
                send(d).start()

        def last():
            for d, (px, py, pc) in enumerate(peers):
                blk = out_ref.at[4 * px + 2 * py + pc]
                _remote(blk, blk, sems, d, (x, y, c)).wait_recv()
            for d in range(N_DEV - 1):
                send(d).wait_send()
            own().wait()

        return first, (lambda: None), last

    return [src], [jax.ShapeDtypeStruct((N_DEV,) + src.shape, src.dtype)], [(N_DEV - 1,), (N_DEV - 1,), ()], phases


def _remote(src, dst, sems, key, to):
    send_sems, recv_sems = sems
    return pltpu.make_async_remote_copy(src_ref=src, dst_ref=dst, send_sem=send_sems.at[key], recv_sem=recv_sems.at[key],
                                        device_id=to, device_id_type=pl.DeviceIdType.MESH)


def _run_exchange(plan, *, name):
    srcs, out_shapes, sem_shapes, phases = plan
    n, m = len(srcs), len(out_shapes)

    def body(*refs):
        for phase in phases(refs[:n], refs[n:n + m], refs[n + m:]):
            phase()

    return pl.pallas_call(
        body, name=name, out_shape=out_shapes,
        in_specs=[pl.BlockSpec(memory_space=pl.ANY)] * n, out_specs=[pl.BlockSpec(memory_space=pl.ANY)] * m,
        scratch_shapes=[pltpu.SemaphoreType.DMA(s) for s in sem_shapes],
    )(*srcs)


def _join_plans(p, q):
    (srcs_p, outs_p, sems_p, phases_p), (srcs_q, outs_q, sems_q, phases_q) = p, q

    def phases(src_refs, out_refs, sem_refs):
        a = phases_p(src_refs[:len(srcs_p)], out_refs[:len(outs_p)], sem_refs[:len(sems_p)])
        b = phases_q(src_refs[len(srcs_p):], out_refs[len(outs_p):], sem_refs[len(sems_p):])

        def both(fa, fb):
            def run():
                fa()
                fb()
            return run

        return tuple(both(fa, fb) for fa, fb in zip(a, b))

    return list(srcs_p) + list(srcs_q), list(outs_p) + list(outs_q), list(sems_p) + list(sems_q), phases


def _gather_plan(srcs):
    n = len(srcs)

    def phases(src_refs, out_refs, sem_refs):
        sems, local_sems = sem_refs[:2], sem_refs[2]
        x, y, c = lax.axis_index("x"), lax.axis_index("y"), lax.axis_index("c")
        idx = lambda px, py, pc: 4 * px + 2 * py + pc
        me, sibling = (x, y, c), (x, y, 1 - c)
        chips = [(1 - x, y), (x, 1 - y), (1 - x, 1 - y)]
        own = lambda a: pltpu.make_async_copy(src_refs[a], out_refs[a].at[idx(*me)], local_sems.at[a])
        to_sibling = lambda a: _remote(src_refs[a], out_refs[a].at[idx(*me)], sems, (0, a), sibling)
        to_chip = lambda a, j: _remote(src_refs[a], out_refs[a].at[idx(*me)], sems, (1 + j, a), (*chips[j], c))
        landed = lambda a, j: out_refs[a].at[idx(*chips[j], c)]
        passed_on = lambda a, j: _remote(landed(a, j), landed(a, j), sems, (4 + j, a), sibling)

        def first():
            for a in range(n):
                own(a).start()
                to_sibling(a).start()
                for j in range(3):
                    to_chip(a, j).start()

        def middle():
            for j in range(3):
                for a in range(n):
                    _remote(landed(a, j), landed(a, j), sems, (1 + j, a), me).wait_recv()
                    passed_on(a, j).start()

        def last():
            for a in range(n):
                blk = out_refs[a].at[idx(*sibling)]
                _remote(blk, blk, sems, (0, a), me).wait_recv()
                for j in range(3):
                    blk = out_refs[a].at[idx(*chips[j], 1 - c)]
                    _remote(blk, blk, sems, (4 + j, a), me).wait_recv()
            for a in range(n):
                to_sibling(a).wait_send()
                for j in range(3):
                    to_chip(a, j).wait_send()
                    passed_on(a, j).wait_send()
                own(a).wait()

        return first, middle, last

    return srcs, [jax.ShapeDtypeStruct((N_DEV,) + s.shape, s.dtype) for s in srcs], [(7, n), (7, n), (n,)], phases


def _sibling_swap_plan(srcs):
    n = len(srcs)

    def phases(src_refs, out_refs, sems):
        x, y, c = lax.axis_index("x"), lax.axis_index("y"), lax.axis_index("c")
        copies = lambda: [_remote(src_refs[a].at[2 * q + 1 - c], out_refs[a].at[q], sems, (q, a), (x, y, 1 - c))
                          for a in range(n) for q in range(4)]

        def first():
            for cp in copies():
                cp.start()

        def last():
            for cp in copies():
                cp.wait()

        return first, (lambda: None), last

    return srcs, [jax.ShapeDtypeStruct((4,) + s.shape[1:], s.dtype) for s in srcs], [(4, n), (4, n)], phases


def _chip_exchange_plan(srcs):
    n = len(srcs)

    def phases(src_refs, out_refs, sem_refs):
        sems, local_sems = sem_refs[:2], sem_refs[2]
        x, y, c = lax.axis_index("x"), lax.axis_index("y"), lax.axis_index("c")
        mine = 2 * x + y
        chips = [(1 - x, y), (x, 1 - y), (1 - x, 1 - y)]
        own = lambda a: pltpu.make_async_copy(src_refs[a].at[mine], out_refs[a].at[mine], local_sems.at[a])
        send = lambda a, j: _remote(src_refs[a].at[2 * chips[j][0] + chips[j][1]], out_refs[a].at[mine], sems, (j, a),
                                    (*chips[j], c))

        def first():
            for a in range(n):
                own(a).start()
                for j in range(3):
                    send(a, j).start()

        def last():
            for j in range(3):
                for a in range(n):
                    blk = out_refs[a].at[2 * chips[j][0] + chips[j][1]]
                    _remote(blk, blk, sems, (j, a), (x, y, c)).wait_recv()
            for a in range(n):
                for j in range(3):
                    send(a, j).wait_send()
                own(a).wait()

        return first, (lambda: None), last

    return srcs, [jax.ShapeDtypeStruct(s.shape, s.dtype) for s in srcs], [(3, n), (3, n), (n,)], phases


def _pair_add(core, g, got, *, name):
    q, r, c = got.shape
    tr, tc = _tile2d(r, c, cap=1024)

    def body(core_ref, a_ref, b_ref, o_ref):
        o_ref[...] = (a_ref[...].astype(F32) + b_ref[...].astype(F32)).astype(BF16)

    blk = pl.BlockSpec((1, tr, tc), lambda i, j, k, core_ref: (i, j, k))
    mine = pl.BlockSpec((1, tr, tc), lambda i, j, k, core_ref: (2 * i + core_ref[0], j, k))
    return pl.pallas_call(
        body, name=name, out_shape=jax.ShapeDtypeStruct(got.shape, BF16),
        grid_spec=pltpu.PrefetchScalarGridSpec(num_scalar_prefetch=1, grid=(q, r // tr, c // tc),
                                               in_specs=[mine, blk], out_specs=blk),
        compiler_params=_cparams(("parallel", "parallel", "parallel")))(core, g, got)


def _adamw(recv, w, m, v, *, name):
    r, c = w.shape
    n_terms = recv.shape[0]
    tr, tc = _tile2d(r, c)

    def body(g_ref, w_ref, m_ref, v_ref, go_ref, d_ref, mo_ref, vo_ref):
        g = g_ref[0].astype(F32)
        for k in range(1, n_terms):
            g = g + g_ref[k].astype(F32)
        m_new = ADAM_B1 * m_ref[...] + (1.0 - ADAM_B1) * g
        v_new = ADAM_B2 * v_ref[...] + (1.0 - ADAM_B2) * (g * g)
        m_hat = m_new / (1.0 - ADAM_B1 ** ADAM_STEP)
        v_hat = v_new / (1.0 - ADAM_B2 ** ADAM_STEP)
        go_ref[...] = g
        d_ref[...] = -ADAM_LR * (m_hat / (jnp.sqrt(v_hat) + ADAM_EPS) + ADAM_WD * w_ref[...])
        mo_ref[...] = m_new
        vo_ref[...] = v_new

    blk = pl.BlockSpec((tr, tc), lambda i, j: (i, j))
    return pl.pallas_call(
        body, name=name, grid=(r // tr, c // tc),
        in_specs=[pl.BlockSpec((n_terms, tr, tc), lambda i, j: (0, i, j)), blk, blk, blk], out_specs=[blk] * 4,
        out_shape=[jax.ShapeDtypeStruct((r, c), F32)] * 4, compiler_params=_cparams(("parallel", "parallel")),
    )(recv, w, m, v)


def _tile2d(r, c, cap=256):
    if r <= cap:
        return r, c
    for t in range(cap - cap % BF16_ROWS, 0, -BF16_ROWS):
        if r % t == 0:
            return t, c
    return r, _pick(c, cap)


def _pack(pieces):
    total = sum(p.shape[0] for p in pieces)
    pad = (-total) % (8 * LANES)
    flat = jnp.concatenate(list(pieces) + [jnp.zeros((pad,), F32)])
    return flat.reshape(-1, LANES)


def _unpack(flat, sizes):
    flat = flat.reshape(-1)
    out, o = [], 0
    for n in sizes:
        out.append(flat[o:o + n])
        o += n
    return out


def _prepare_weights(full, vec, dims):
    rest = {n: t for n, t in full.items() if n != "w_in"}
    return {"w_in_t": _prepare_w_in(full["w_in"], dims), **_prepare_rest(rest, dims), **_prepare_vectors(vec, dims)}


def _prepare_w_in(slabs, dims):
    D = dims["D"]
    flat = slabs.reshape(-1, D)
    parts, pos = [], 0
    for orig_off, width, perm_off in sorted(dims["segs"], key=lambda t: t[2]):
        if perm_off > pos:
            parts.append(jnp.zeros((perm_off - pos, D), BF16))
        parts.append(flat[orig_off:orig_off + width])
        pos = perm_off + width
    if dims["d_in_perm"] > pos:
        parts.append(jnp.zeros((dims["d_in_perm"] - pos, D), BF16))
    return jnp.concatenate(parts, axis=0)


def _prepare_rest(full, dims):
    hm, hr, hn, rank = dims["hm"], dims["hr"], dims["hn"], dims["rank"]
    QR, KVR = dims["QR"], dims["KVR"]
    RW, TAIL = hr * hn, dims["TAIL"]
    full = {n: (t.reshape(-1, t.shape[2]) if n in _ROW_SHARDED + _TRANSPOSED
                else t.transpose(1, 0, 2).reshape(t.shape[1], -1)) for n, t in full.items()}
    wq = full["mla_wq_b"].reshape(hm, NOPE + ROPE, QR)
    wq = jnp.concatenate([wq, jnp.zeros((hm, QHEAD - NOPE - ROPE, QR), BF16)], axis=1).reshape(hm * QHEAD, QR)
    wkv = full["mla_wkv_b"].reshape(KVR, hm, 2, NOPE).transpose(0, 2, 1, 3).reshape(KVR, 2 * hm * NOPE)
    z = lambda rows: jnp.zeros((rows, RW), BF16)
    f = lambda nme: full[nme]
    split = ROPE + 2 * rank
    assert split % LANES == 0, split
    w2cat = jnp.concatenate([
        jnp.concatenate([z(ROPE), f("rwkv_w2_f"), z(rank)], axis=0),
        jnp.concatenate([z(ROPE + rank), f("rwkv_w2_b")], axis=0)], axis=1)
    a2cat = jnp.concatenate([
        jnp.concatenate([f("rwkv_a2_f"), z(TAIL - split - rank)], axis=0),
        jnp.concatenate([z(rank), f("rwkv_a2_b"), z(TAIL - split - 2 * rank)], axis=0)], axis=1)
    return dict(wq_b_t=wq, wkv_b=wkv, w2cat=w2cat, a2cat=a2cat, w_br_mla=full["w_br_mla"],
                w_br_rwkv=full["w_br_rwkv"], w_out=full["w_out"])


def _prepare_vectors(vec, dims):
    rank, RW, TAIL = dims["rank"], dims["hr"] * dims["hn"], dims["TAIL"]
    mu = vec["rwkv_mu"]
    mu_p = jnp.concatenate([mu[:3 * RW], jnp.zeros((ROPE,), F32), mu[3 * RW:],
                            jnp.zeros((TAIL - ROPE - 4 * rank,), F32)])
    row = lambda t: t.reshape(1, -1)
    return dict(
        mu=row(mu_p), g_pre=row(vec["g_pre"]), g_post=row(vec["g_post"]), mla_q_norm=row(vec["mla_q_norm"]),
        mla_kv_norm=row(vec["mla_kv_norm"]), w0_f=row(vec["rwkv_w0_f"]), w0_b=row(vec["rwkv_w0_b"]),
        a0_f=row(vec["rwkv_a0_f"]), a0_b=row(vec["rwkv_a0_b"]), k_k=row(vec["rwkv_k_k"]), k_a=row(vec["rwkv_k_a"]),
        r_k=row(vec["rwkv_r_k"]), gn_g=row(vec["rwkv_gn_g"]), gn_b=row(vec["rwkv_gn_b"]))


def _restore_grads(g, dims):
    return {"w_in": _restore_w_in(g["w_in"], dims), **_restore_rest(g, dims), **_restore_vectors(g, dims)}


def _restore_w_in(gw, dims):
    parts = [gw[perm_off:perm_off + width] for _, width, perm_off in sorted(dims["segs"])]
    return jnp.concatenate(parts, axis=0).reshape(N_DEV, dims["d_in"] // N_DEV, gw.shape[1])


def _restore_rest(g, dims):
    hm, hr, hn, rank = dims["hm"], dims["hr"], dims["hn"], dims["rank"]
    QR, KVR, RW = dims["QR"], dims["KVR"], hr * hn
    wq = g["wq_b"].reshape(hm, QHEAD, QR)[:, :NOPE + ROPE].reshape(N_DEV, -1, QR)
    wkv = g["wkv_b"].reshape(KVR, 2, hm, NOPE).transpose(0, 2, 1, 3).reshape(KVR, 2 * hm * NOPE)
    lo = lambda t, first, half: t[first:first + rank, half * RW:(half + 1) * RW].astype(BF16)
    cols = lambda t: t.reshape(t.shape[0], N_DEV, -1).transpose(1, 0, 2)
    return dict(
        mla_wq_b=wq, mla_wkv_b=cols(wkv), rwkv_w2_f=cols(lo(g["w2cat"], ROPE, 0)),
        rwkv_w2_b=cols(lo(g["w2cat"], ROPE + rank, 1)), rwkv_a2_f=cols(lo(g["a2cat"], 0, 0)),
        rwkv_a2_b=cols(lo(g["a2cat"], rank, 1)), w_br_mla=cols(g["w_br_mla"]), w_br_rwkv=cols(g["w_br_rwkv"]),
        w_out=g["w_out"].reshape(N_DEV, -1, g["w_out"].shape[1]))


def _restore_vectors(g, dims):
    rank, RW = dims["rank"], dims["hr"] * dims["hn"]
    mu = g["mu"][0]
    out = dict(
        rwkv_mu=jnp.concatenate([mu[:3 * RW], mu[3 * RW + ROPE:3 * RW + ROPE + 4 * rank]]),
        g_pre=g["g_pre"][0], g_post=g["g_post"][0], mla_q_norm=g["mla_q_norm"][0], mla_kv_norm=g["mla_kv_norm"][0],
        rwkv_w0_f=g["w0_f"][0], rwkv_w0_b=g["w0_b"][0], rwkv_a0_f=g["a0_f"][0], rwkv_a0_b=g["a0_b"][0],
        rwkv_k_k=g["k_k"][0], rwkv_k_a=g["k_a"][0], rwkv_r_k=g["r_k"][0], rwkv_gn_g=g["gn_g"][0],
        rwkv_gn_b=g["gn_b"][0])
    return out


def _dims(inp):
    D = inp["x"].shape[-1]
    QR, KVR = inp["mla_q_norm"].shape[0], inp["mla_kv_norm"].shape[0]
    hm = inp["mla_wq_b"].shape[1] * N_DEV // (NOPE + ROPE)
    hr, hn = inp["rwkv_r_k"].shape
    rank = inp["rwkv_w2_f"].shape[0]
    MW, RW = hm * VDIM, hr * hn
    TAIL = -(-(ROPE + 4 * rank) // LANES) * LANES
    orig, o = {}, 0
    for nme, w in (("q_a", QR), ("kv_a", KVR), ("k_rope", ROPE), ("rkv", 3 * RW), ("lora", 4 * rank), ("z_m", MW),
                   ("z_r", RW), ("gate_m", D), ("gate_r", D)):
        orig[nme] = (o, w)
        o += w
    assert o == inp["w_in"].shape[1] * N_DEV
    lay, d_in_perm = _layout(D, MW, RW, TAIL, QR, KVR)
    perm_off = dict(q_a=lay["q_a"][0], kv_a=lay["kv_a"][0], k_rope=lay["tail"][0], rkv=lay["r"][0],
                    lora=lay["tail"][0] + ROPE, z_m=lay["z_m"][0], z_r=lay["z_r"][0], gate_m=lay["gate_m"][0],
                    gate_r=lay["gate_r"][0])
    segs = [(orig[nme][0], orig[nme][1], perm_off[nme]) for nme in orig]
    return dict(D=D, QR=QR, KVR=KVR, hm=hm, hr=hr, hn=hn, rank=rank, TAIL=TAIL, segs=segs, d_in=o,
                d_in_perm=d_in_perm)


def kernel(x, g_pre, w_in, mla_q_norm, mla_wq_b, mla_kv_norm, mla_wkv_b, rwkv_mu, rwkv_w0_f, rwkv_w2_f, rwkv_w0_b, rwkv_w2_b, rwkv_a0_f, rwkv_a2_f, rwkv_a0_b, rwkv_a2_b, rwkv_k_k, rwkv_k_a, rwkv_r_k, rwkv_gn_g, rwkv_gn_b, w_br_mla, w_br_rwkv, w_out, g_post, loss_target, m_g_pre, m_w_in, m_mla_q_norm, m_mla_wq_b, m_mla_kv_norm, m_mla_wkv_b, m_rwkv_mu, m_rwkv_w0_f, m_rwkv_w2_f, m_rwkv_w0_b, m_rwkv_w2_b, m_rwkv_a0_f, m_rwkv_a2_f, m_rwkv_a0_b, m_rwkv_a2_b, m_rwkv_k_k, m_rwkv_k_a, m_rwkv_r_k, m_rwkv_gn_g, m_rwkv_gn_b, m_w_br_mla, m_w_br_rwkv, m_w_out, m_g_post, v_g_pre, v_w_in, v_mla_q_norm, v_mla_wq_b, v_mla_kv_norm, v_mla_wkv_b, v_rwkv_mu, v_rwkv_w0_f, v_rwkv_w2_f, v_rwkv_w0_b, v_rwkv_w2_b, v_rwkv_a0_f, v_rwkv_a2_f, v_rwkv_a0_b, v_rwkv_a2_b, v_rwkv_k_k, v_rwkv_k_a, v_rwkv_r_k, v_rwkv_gn_g, v_rwkv_gn_b, v_w_br_mla, v_w_br_rwkv, v_w_out, v_g_post):
    inp = dict(locals())
    dims = _dims(inp)
    stored = lambda t, n: t.T if n in _TRANSPOSED else t
    assert _MATS[0] == "w_in"
    shards = [stored(inp[n], n).astype(BF16) for n in _MATS]
    core = lax.axis_index("c").astype(jnp.int32).reshape(1)
    (w_in_slabs,) = _run_exchange(_gather_plan(shards[:1]), name="gather_w_in")
    W = {"w_in_t": _prepare_w_in(w_in_slabs, dims), **_prepare_vectors({n: inp[n] for n in _VECS}, dims)}
    loss, grad_x, g, recv_rest = _local_grads(x[0], loss_target[0], W, dims, exchange=(shards[1:], core))

    new = {}
    *recv_rest, got = recv_rest
    g_w_in, g = g["w_in"], _restore_vectors(g, dims)
    vsizes = [inp[n].size for n in _VECS] + [1]
    vflat = lambda prefix, src, last: _pack([src[prefix + n].reshape(-1) for n in _VECS] + [last])
    one = jnp.zeros((1,), F32)
    recv_w_in, vrecv = _run_exchange(
        _join_plans(_chip_exchange_plan([_pair_add(core, g_w_in, got, name="pair_add_w_in")]),
                    _direct_gather_plan(vflat("", g, loss.reshape(1)))), name="scatter_w_in")
    for n, t in zip(_MATS, [recv_w_in] + recv_rest):
        out = _adamw(t, stored(inp[n], n), stored(inp["m_" + n], n), stored(inp["v_" + n], n), name="adamw_" + n)
        new[n] = [stored(o, n) for o in out]

    vout = _adamw(vrecv, vflat("", inp, one), vflat("m_", inp, one), vflat("v_", inp, one), name="adamw_vectors")
    vparts = [_unpack(t, vsizes) for t in vout]
    for i, n in enumerate(_VECS):
        new[n] = [vp[i].reshape(inp[n].shape) for vp in vparts]
    loss = vparts[0][-1].reshape(())

    outs = [loss, grad_x[None]]
    for k in range(4):
        outs += [new[n][k] for n in _WEIGHTS]
    return tuple(outs)
```

```python
import functools
import math

import jax
import jax.numpy as jnp
from jax import lax
from jax.experimental import pallas as pl
from jax.experimental.pallas import tpu as pltpu

F32 = jnp.float32
BF16 = jnp.bfloat16

N_DEV = 8
LANES = 128
BF16_ROWS = 16
NOPE, ROPE, VDIM = 128, 64, 128
QHEAD = 256
ROPE_THETA = 10000.0
NORM_EPS = 1e-6
GN_EPS = 64e-5
CHUNK = 64
SUB = 16
VMEM_LIMIT = 56 * 1024 * 1024

ADAM_LR, ADAM_B1, ADAM_B2, ADAM_EPS, ADAM_WD, ADAM_STEP = 0.001, 0.9, 0.999, 1e-08, 0.01, 10


def _cparams(sem):
    return pltpu.CompilerParams(dimension_semantics=sem, vmem_limit_bytes=VMEM_LIMIT)


def _pick(n, cap):
    if n <= cap:
        return n
    for t in range(cap - cap % LANES, 0, -LANES):
        if n % t == 0:
            return t
    raise ValueError(f"no tile for {n} under {cap}")


def _mm(a, b, *, ta=False, tb=False, out_dtype=F32, name, tm_cap=1024, tn_cap=512, tk_cap=2048, ride=None):
    K, M = a.shape if ta else a.shape[::-1]
    N = b.shape[0] if tb else b.shape[1]
    assert (b.shape[1] if tb else b.shape[0]) == K, (a.shape, b.shape, ta, tb)
    tm, tn, tk = _pick(M, tm_cap), _pick(N, tn_cap), _pick(K, tk_cap)
    nj, nk = N // tn, K // tk
    steps = (M // tm) * nj * nk
    dn = (((0 if ta else 1,), (1 if tb else 0,)), ((), ()))
    srcs, extra_shapes, sem_shapes, phases = ride if ride else ((), (), (), None)
    n_src, n_extra = len(srcs), len(extra_shapes)

    def body(*refs):
        a_ref, b_ref, o_ref = refs[0], refs[1], refs[2 + n_src]
        acc_ref = refs[3 + n_src + n_extra]
        k = pl.program_id(2)
        if ride:
            step = (pl.program_id(0) * nj + pl.program_id(1)) * nk + k
            first, middle, last = phases(refs[2:2 + n_src], refs[3 + n_src:3 + n_src + n_extra],
                                         refs[4 + n_src + n_extra:])
            pl.when(step == 0)(first)
            pl.when(step == (steps * 15) // 16)(middle)
        p = lax.dot_general(a_ref[...], b_ref[...], dn, preferred_element_type=F32)

        @pl.when(k == 0)
        def _():
            acc_ref[...] = p

        @pl.when(k > 0)
        def _():
            acc_ref[...] += p

        @pl.when(k == nk - 1)
        def _():
            o_ref[...] = acc_ref[...].astype(out_dtype)

        if ride:
            pl.when(step == steps - 1)(last)

    a_spec = pl.BlockSpec((tk, tm), lambda i, j, k: (k, i)) if ta else pl.BlockSpec((tm, tk), lambda i, j, k: (i, k))
    b_spec = pl.BlockSpec((tn, tk), lambda i, j, k: (j, k)) if tb else pl.BlockSpec((tk, tn), lambda i, j, k: (k, j))
    hbm = pl.BlockSpec(memory_space=pl.ANY)
    out = pl.pallas_call(
        body, name=name, grid=(M // tm, nj, nk),
        in_specs=[a_spec, b_spec] + [hbm] * n_src,
        out_specs=[pl.BlockSpec((tm, tn), lambda i, j, k: (i, j))] + [hbm] * n_extra,
        out_shape=[jax.ShapeDtypeStruct((M, N), out_dtype)] + list(extra_shapes),
        scratch_shapes=[pltpu.VMEM((tm, tn), F32)] + [pltpu.SemaphoreType.DMA(s) for s in sem_shapes],
        compiler_params=_cparams(("arbitrary",) * 3 if ride else ("parallel", "parallel", "arbitrary")),
    )(a, b, *srcs)
    return out if ride else out[0]


def _view(arr, off, width):
    assert off % width == 0, (off, width)
    return (arr, off // width, width)


def _rowwise(fn, rows, params, out_rows, out_accs=(), *, tile, name):
    rows = [r if isinstance(r, tuple) else (r, 0, r.shape[1]) for r in rows]
    S = rows[0][0].shape[0]
    T = min(tile, S)
    assert S % T == 0
    n_rows, n_par, n_out = len(rows), len(params), len(out_rows)
    into = [o[2] if len(o) == 3 else None for o in out_rows]
    carried = [t[0] for t in into if t is not None and t[0] is not None]

    def body(*refs):
        ins = [r[...] for r in refs[:n_rows + n_par]]
        outs = fn(*ins)
        out_refs = refs[n_rows + n_par + len(carried):]
        for o_ref, val in zip(out_refs[:n_out], outs[:n_out]):
            o_ref[...] = val.astype(o_ref.dtype)
        i = pl.program_id(0)
        for o_ref, val in zip(out_refs[n_out:], outs[n_out:]):
            @pl.when(i == 0)
            def _(o_ref=o_ref, val=val):
                o_ref[...] = val

            @pl.when(i > 0)
            def _(o_ref=o_ref, val=val):
                o_ref[...] += val

    in_specs = [pl.BlockSpec((T, w), functools.partial(lambda i, cb: (i, cb), cb=cb)) for _, cb, w in rows]
    in_specs += [pl.BlockSpec(p.shape, lambda i: (0, 0)) for p in params]
    in_specs += [pl.BlockSpec(memory_space=pl.ANY)] * len(carried)
    out_specs, out_shape, aliases = [], [], {}
    for k, (o, t) in enumerate(zip(out_rows, into)):
        w, dt = o[0], o[1]
        if t is None:
            out_specs.append(pl.BlockSpec((T, w), lambda i: (i, 0)))
            out_shape.append(jax.ShapeDtypeStruct((S, w), dt))
            continue
        buf, total, first = t
        assert first % w == 0
        out_specs.append(pl.BlockSpec((T, w), functools.partial(lambda i, cb: (i, cb), cb=first // w)))
        out_shape.append(jax.ShapeDtypeStruct((S, total), dt))
        if buf is not None:
            aliases[n_rows + n_par + len(aliases)] = k
    out_specs += [pl.BlockSpec(s, lambda i: (0, 0)) for s in out_accs]
    out_shape += [jax.ShapeDtypeStruct(s, F32) for s in out_accs]
    return pl.pallas_call(
        body, name=name, grid=(S // T,), in_specs=in_specs, out_specs=out_specs, out_shape=out_shape,
        input_output_aliases=aliases, compiler_params=_cparams(("arbitrary",)),
    )(*[r[0] for r in rows], *params, *carried)


def _mm_sel(x, sel2):
    hi = x.astype(BF16)
    lo = (x - hi.astype(F32)).astype(BF16)
    return jnp.dot(jnp.concatenate([hi, lo], axis=1), sel2, preferred_element_type=F32)


@jax.custom_vjp
def _sel(x, sel, sel_t):
    return _mm_sel(x, sel)


def _sel_fwd(x, sel, sel_t):
    return _mm_sel(x, sel), (sel, sel_t)


def _sel_bwd(res, ct):
    sel, sel_t = res
    return _mm_sel(ct, sel_t), jnp.zeros_like(sel), jnp.zeros_like(sel_t)


_sel.defvjp(_sel_fwd, _sel_bwd)


def _rms(x, g):
    return x * lax.rsqrt(jnp.mean(x * x, axis=-1, keepdims=True) + NORM_EPS) * g


def _sigmoid(x):
    return 0.5 * jnp.tanh(0.5 * x) + 0.5


def _silu(x):
    return x * _sigmoid(x)


def _softplus(x):
    return jnp.maximum(x, 0.0) + jnp.log(1.0 + jnp.exp(-jnp.abs(x)))


def _f_mla_norm(q_a, kv_a, qg, kvg):
    return _rms(q_a, qg), _rms(kv_a, kvg)


def _f_rope(hm, qraw, kr_in, cosx, sinx, rot, rot_t):
    def rope(t):
        return t * cosx + _sel(t, rot, rot_t) * sinx
    parts = []
    for h in range(hm):
        parts.append(qraw[:, h * QHEAD:h * QHEAD + NOPE])
        parts.append(rope(qraw[:, h * QHEAD + NOPE:(h + 1) * QHEAD]))
    return jnp.concatenate(parts, axis=1), rope(kr_in)


def _f_rwkv_pre(rw, k, tail, w0f, w0b, a0f, a0b, k_k, k_a, w2cat, a2cat, seg, seg_t):
    split = w2cat.shape[0]
    zw = jnp.dot(jnp.tanh(tail[:, :split]).astype(BF16), w2cat, preferred_element_type=F32)
    za = jnp.dot(tail[:, split:].astype(BF16), a2cat, preferred_element_type=F32)
    return _f_rwkv_core(rw, k, zw, za, w0f, w0b, a0f, a0b, k_k, k_a, seg, seg_t)


def _f_rwkv_core(rw, k, zw, za, w0f, w0b, a0f, a0b, k_k, k_a, seg, seg_t):
    lw_f = -jnp.exp(-_softplus(-(w0f + zw[:, :rw])) - 0.5)
    lw_b = -jnp.exp(-_softplus(-(w0b + zw[:, rw:])) - 0.5)
    a_f = _sigmoid(a0f + za[:, :rw])
    a_b = _sigmoid(a0b + za[:, rw:])
    kk = k * k_k
    nrm = jnp.sqrt(_sel(_sel(kk * kk, seg, seg_t), seg_t, seg))
    kk = kk / jnp.maximum(nrm, 1e-12)
    k_f = k * (1.0 + (a_f - 1.0) * k_a)
    k_b = k * (1.0 + (a_b - 1.0) * k_a)
    return lw_f, lw_b, k_f, k_b, -kk, kk * a_f, kk * a_b


def _f_post(hn, y_f, y_b, r, k_f, k_b, v, z_r, o_mla, z_m, gn_g, gn_b, r_k, seg, seg_t):
    segsum = lambda t: _sel(_sel(t, seg, seg_t), seg_t, seg)
    y = y_f + y_b
    mu = segsum(y) * (1.0 / hn)
    yc = y - mu
    var = segsum(yc * yc) * (1.0 / hn)
    yn = yc * lax.rsqrt(var + GN_EPS) * gn_g + gn_b
    bonus = segsum(r * (k_f + k_b) * r_k) * v
    return o_mla * _silu(z_m), (yn + bonus) * _silu(z_r)


def _f_merge(u_m, u_r, g_m, g_r):
    return _sigmoid(g_m) * u_m + _sigmoid(g_r) * u_r


_NN = ((2,), (1,))
_NT = ((2,), (2,))
_TN = ((1,), (1,))

_SCAN_PASSES = {"cum": 2, "gram": 3, "solve": 1, "apply": 1, "state": 1}


def _hdot_raw(passes, x, y, dims):
    dn = (dims, ((0,), (0,)))
    d = lambda p, q: lax.dot_general(p, q, dn, preferred_element_type=F32)
    xh = x.astype(BF16)
    yh = y.astype(BF16)
    if passes == 1:
        return d(xh, yh)
    yl = (y - yh.astype(F32)).astype(BF16)
    kx, ky = (1 if dims == _TN else 2), (2 if dims == _NT else 1)
    depth = x.shape[kx]
    if all(axis == 1 or depth % LANES == 0 for axis in (kx, ky)):
        if passes == 2:
            return d(jnp.concatenate([xh, xh], axis=kx), jnp.concatenate([yh, yl], axis=ky))
        xl = (x - xh.astype(F32)).astype(BF16)
        return d(jnp.concatenate([xh, xl, xh], axis=kx), jnp.concatenate([yh, yh, yl], axis=ky))
    if passes == 2:
        axis = 1 if dims == _NT else 2
        width = y.shape[axis]
        both = d(xh, jnp.concatenate([yh, yl], axis=axis))
        return both[:, :, :width] + both[:, :, width:]
    xl = (x - xh.astype(F32)).astype(BF16)
    if dims == _TN:
        return d(xh, yh) + d(xh, yl) + d(xl, yh)
    rows = x.shape[1]
    both = d(jnp.concatenate([xh, xl], axis=1), yh)
    return both[:, :rows] + both[:, rows:] + d(xh, yl)


@functools.partial(jax.custom_vjp, nondiff_argnums=(2, 3))
def _hdot_p(x, y, dims, passes):
    return _hdot_raw(passes, x, y, dims)


def _hdot_fwd(x, y, dims, passes):
    return _hdot_raw(passes, x, y, dims), (x, y)


def _hdot_bwd(dims, passes, res, ct):
    x, y = res
    if dims == _NN:
        return _hdot_raw(passes, ct, y, _NT), _hdot_raw(passes, x, ct, _TN)
    if dims == _NT:
        return _hdot_raw(passes, ct, y, _NN), _hdot_raw(passes, ct, x, _TN)
    return _hdot_raw(passes, y, ct, _NT), _hdot_raw(passes, x, ct, _NN)


_hdot_p.defvjp(_hdot_fwd, _hdot_bwd)


def _hdot(x, y, dims, kind):
    return _hdot_p(x, y, dims, _SCAN_PASSES[kind])


def _tri_solve(n_mat, x, length, blocks):
    row = lax.broadcasted_iota(jnp.int32, (length, 2 * length), 0)
    col = lax.broadcasted_iota(jnp.int32, (length, 2 * length), 1)
    col = jnp.where(col >= length, col - length, col)
    eye = (row == col).astype(F32)[None]
    diag_blk = ((row // SUB) == (col // SUB))[None]
    nd = jnp.where(diag_blk, n_mat, 0.0)
    no = n_mat - nd
    dinv = eye + nd
    p = _hdot(nd, blocks(nd), _NN, "solve")
    for k in range(int(math.log2(SUB)) - 1):
        if k == int(math.log2(SUB)) - 2:
            dinv = dinv + _hdot(dinv, blocks(p), _NN, "solve")
        else:
            both = _hdot(jnp.concatenate([dinv, p], axis=1), blocks(p), _NN, "solve")
            dinv, p = dinv + both[:, :length], both[:, length:]
    width = x.shape[2]
    both = _hdot(dinv, jnp.concatenate([blocks(x), blocks(no)], axis=2), _NN, "solve")
    u, q = both[:, :, :width], both[:, :, width:]
    for level in range(int(math.log2(length // SUB))):
        if level == int(math.log2(length // SUB)) - 1:
            u = u + _hdot(q, blocks(u), _NN, "solve")
        else:
            both = _hdot(q, jnp.concatenate([blocks(u), blocks(q)], axis=2), _NN, "solve")
            u, q = u + both[:, :, :width], both[:, :, width:]
    return u


def _rwkv_chunk(rev, s0, r, lw, k, v, a, b):
    pairs, length, width = r.shape
    hn = width // 2
    assert 2 * length == width
    row = lax.broadcasted_iota(jnp.int32, (length, length), 0)
    col = lax.broadcasted_iota(jnp.int32, (length, length), 1)
    row2 = lax.broadcasted_iota(jnp.int32, (length, 2 * length), 0)
    col2 = lax.broadcasted_iota(jnp.int32, (length, 2 * length), 1)
    col2 = jnp.where(col2 >= length, col2 - length, col2)
    if rev is None:
        half = pairs // 2
        back = lax.broadcasted_iota(jnp.int32, (pairs, length, length), 0) >= half
        back2 = lax.broadcasted_iota(jnp.int32, (pairs, length, 2 * length), 0) >= half
        ahead = jnp.where(back, (col - row)[None], (row - col)[None])
        ahead2 = jnp.where(back2, (col2 - row2)[None], (row2 - col2)[None])
        incl, strict2, incl2 = ahead >= 0, ahead2 > 0, ahead2 >= 0
    else:
        incl = ((row <= col) if rev else (row >= col))[None]
        strict2 = ((row2 < col2) if rev else (row2 > col2))[None]
        incl2 = ((row2 <= col2) if rev else (row2 >= col2))[None]
    first = (lax.broadcasted_iota(jnp.int32, (1, 1, width), 2) < hn).astype(F32)
    blocks = lambda t: jnp.concatenate([t * first, t * (1.0 - first)], axis=1)

    t_incl = jnp.broadcast_to(incl.astype(F32), (pairs, length, length))
    cum = _hdot(t_incl, lw, _NN, "cum")
    g = jnp.exp(cum)
    g_inv = jnp.exp(-cum)
    at = a * jnp.exp(cum - lw)
    rt = r * g
    bt = b * g_inv
    kt = k * g_inv
    both_rows = jnp.concatenate([at, rt], axis=1)
    gram = _hdot(both_rows, jnp.concatenate([blocks(bt), blocks(kt)], axis=1), _NT, "gram")
    a_ab = jnp.where(strict2, gram[:, :length, :width], 0.0)
    a_ak = jnp.where(strict2, gram[:, :length, width:], 0.0)
    a_rb = jnp.where(incl2, gram[:, length:, :width], 0.0)
    a_rk = jnp.where(incl2, gram[:, length:, width:], 0.0)
    from_state = _hdot(both_rows, s0, _NT, "apply")
    x = from_state[:, :length] + _hdot(a_ak, blocks(v), _NN, "apply")
    u = _tri_solve(a_ab, x, length, blocks)
    y = from_state[:, length:] + _hdot(jnp.concatenate([a_rb, a_rk], axis=2),
                                       jnp.concatenate([blocks(u), blocks(v)], axis=1), _NN, "apply")
    g_last = jnp.exp(jnp.sum(lw, axis=1, keepdims=True))
    ri = lax.broadcasted_iota(jnp.int32, (width, width), 0)
    ci = lax.broadcasted_iota(jnp.int32, (width, width), 1)
    same_head = ((ri < hn) == (ci < hn))[None]
    upd = _hdot(jnp.concatenate([u, v], axis=1), jnp.concatenate([bt, kt], axis=1), _TN, "state")
    s1 = (s0 + jnp.where(same_head, upd, 0.0)) * g_last
    return y, s1


def _split_pairs(x):
    return jnp.stack([x[:, p * LANES:(p + 1) * LANES] for p in range(x.shape[1] // LANES)])


def _merge_pairs(x):
    return jnp.concatenate([x[p] for p in range(x.shape[0])], axis=1)


def _scan_specs(views, rw, nc, rev):
    cidx = (lambda c: nc - 1 - c) if rev else (lambda c: c)
    seqs = [pl.BlockSpec((CHUNK, rw), functools.partial(lambda c, cb: (cidx(c), cb), cb=cb)) for _, cb, _ in views]
    plain = pl.BlockSpec((CHUNK, rw), lambda c: (cidx(c), 0))
    st = pl.BlockSpec((1, rw // LANES, LANES, LANES), lambda c: (cidx(c), 0, 0, 0))
    return seqs, plain, st


def _as_views(arrs, rw):
    return [t if isinstance(t, tuple) else (t, 0, rw) for t in arrs]


def _rwkv_scan_fwd(ops_f, ops_b, rw, *, name):
    S = _as_views(ops_f, rw)[0][0].shape[0]
    nc, pairs = S // CHUNK, rw // LANES
    in_specs, out_specs, arrays = [], [], []
    for rev, ops in ((False, ops_f), (True, ops_b)):
        views = _as_views(ops, rw)
        seqs, plain, st = _scan_specs(views, rw, nc, rev)
        in_specs += seqs
        out_specs += [plain, st]
        arrays += [t[0] for t in views]

    def both(refs_f, refs_b):
        return [jnp.concatenate([_split_pairs(f[...]), _split_pairs(b[...])], axis=0) for f, b in zip(refs_f, refs_b)]

    def body(*refs):
        (y_f, st_f, y_b, st_b), s_ref = refs[12:16], refs[16]

        @pl.when(pl.program_id(0) == 0)
        def _():
            s_ref[...] = jnp.zeros_like(s_ref)

        s0 = s_ref[...]
        st_f[0] = s0[:pairs]
        st_b[0] = s0[pairs:]
        y, s1 = _rwkv_chunk(None, s0, *both(refs[:6], refs[6:12]))
        y_f[...] = _merge_pairs(y[:pairs])
        y_b[...] = _merge_pairs(y[pairs:])
        s_ref[...] = s1

    return pl.pallas_call(
        body, name=name, grid=(nc,), in_specs=in_specs, out_specs=out_specs,
        out_shape=[jax.ShapeDtypeStruct((S, rw), F32), jax.ShapeDtypeStruct((nc, pairs, LANES, LANES), F32)] * 2,
        scratch_shapes=[pltpu.VMEM((2 * pairs, LANES, LANES), F32)],
        compiler_params=_cparams(("arbitrary",)),
    )(*arrays)


def _rwkv_scan_bwd(ops_f, ops_b, states_f, states_b, dy, rw, *, name):
    S = dy.shape[0]
    nc, pairs = S // CHUNK, rw // LANES
    in_specs, arrays = [], []
    for rev, ops, states in ((False, ops_f, states_f), (True, ops_b, states_b)):
        views = _as_views(list(ops) + [dy], rw)
        seqs, plain, st = _scan_specs(views, rw, nc, not rev)
        in_specs += seqs + [st]
        arrays += [t[0] for t in views] + [states]
    out_specs = []
    for rev in (False, True):
        out_specs += [_scan_specs([], rw, nc, not rev)[1]] * 6

    def both(refs_f, refs_b):
        return [jnp.concatenate([_split_pairs(f[...]), _split_pairs(b[...])], axis=0) for f, b in zip(refs_f, refs_b)]

    def body(*refs):
        ds_ref = refs[28]

        @pl.when(pl.program_id(0) == 0)
        def _():
            ds_ref[...] = jnp.zeros_like(ds_ref)

        s0 = jnp.concatenate([refs[7][0], refs[15][0]], axis=0)
        _, vjp = jax.vjp(functools.partial(_rwkv_chunk, None), s0, *both(refs[:6], refs[8:14]))
        (dy,) = both(refs[6:7], refs[14:15])
        grads = vjp((dy, ds_ref[...]))
        ds_ref[...] = grads[0]
        for o_f, o_b, gval in zip(refs[16:22], refs[22:28], grads[1:]):
            o_f[...] = _merge_pairs(gval[:pairs])
            o_b[...] = _merge_pairs(gval[pairs:])

    return pl.pallas_call(
        body, name=name, grid=(nc,), in_specs=in_specs, out_specs=out_specs,
        out_shape=[jax.ShapeDtypeStruct((S, rw), F32)] * 12,
        scratch_shapes=[pltpu.VMEM((2 * pairs, LANES, LANES), F32)],
        compiler_params=_cparams(("arbitrary",)),
    )(*arrays)


def _shift_lerp(x_view, mu, d=None, into=None, *, name):
    arr, off, width = x_view
    S = arr.shape[0]
    cb = _pick(width, 512)
    assert off % cb == 0

    def cshift(t):
        rows = lax.broadcasted_iota(jnp.int32, t.shape, 0)
        prev = jnp.where(rows == 0, 0.0, pltpu.roll(t, 1, 0))
        nxt = jnp.where(rows == S - 1, 0.0, pltpu.roll(t, S - 1, 0))
        return 0.5 * (prev + nxt)

    def fwd_body(x_ref, mu_ref, o_ref):
        x = x_ref[...]
        o_ref[...] = x + mu_ref[...] * (cshift(x) - x)

    def bwd_body(x_ref, mu_ref, d_ref, _, dx_ref, dmu_ref):
        x, m, dd = x_ref[...], mu_ref[...], d_ref[...]
        gm = m * dd
        dx_ref[...] = (dd - gm + cshift(gm)).astype(dx_ref.dtype)
        dmu_ref[...] = jnp.sum(dd * (cshift(x) - x), axis=0, keepdims=True)

    x_spec = pl.BlockSpec((S, cb), lambda j: (0, off // cb + j))
    blk = pl.BlockSpec((S, cb), lambda j: (0, j))
    vec = pl.BlockSpec((1, cb), lambda j: (0, j))
    if d is None:
        return pl.pallas_call(
            fwd_body, name=name, grid=(width // cb,), in_specs=[x_spec, vec], out_specs=blk,
            out_shape=jax.ShapeDtypeStruct((S, width), F32), compiler_params=_cparams(("parallel",)),
        )(arr, mu)
    buf, first = into
    assert first % cb == 0
    return pl.pallas_call(
        bwd_body, name=name, grid=(width // cb,),
        in_specs=[x_spec, vec, blk, pl.BlockSpec(memory_space=pl.ANY)],
        out_specs=[pl.BlockSpec((S, cb), lambda j: (0, first // cb + j)), vec],
        out_shape=[jax.ShapeDtypeStruct(buf.shape, buf.dtype), jax.ShapeDtypeStruct((1, width), F32)],
        input_output_aliases={3: 0}, compiler_params=_cparams(("parallel",)),
    )(arr, mu, d, buf)


def _attention_fwd(qfull, kv, kr, hm, scale, *, tq, name):
    S = qfull.shape[0]
    nt = (((1,), (1,)), ((), ()))

    def body(q_ref, kn_ref, kr_ref, v_ref, o_ref, lse_ref, k_scr):
        _head_keys(kn_ref, kr_ref, k_scr)
        s = lax.dot_general(q_ref[...], k_scr[...], nt, preferred_element_type=F32)
        m = jnp.max(s, axis=-1, keepdims=True)
        p = jnp.exp((s - m) * scale)
        l = jnp.sum(p, axis=-1, keepdims=True)
        o_ref[...] = jnp.dot(p.astype(BF16), v_ref[...], preferred_element_type=F32) * (1.0 / l)
        lse_ref[...] = jnp.broadcast_to(m * scale + jnp.log(l), lse_ref.shape)

    oblk = pl.BlockSpec((tq, VDIM), lambda h, i: (i, h))
    return pl.pallas_call(
        body, name=name, grid=(hm, S // tq),
        in_specs=[pl.BlockSpec((tq, QHEAD), lambda h, i: (i, h)),
                  pl.BlockSpec((S, NOPE), lambda h, i: (0, h)),
                  pl.BlockSpec((S, LANES), lambda h, i: (0, 0)),
                  pl.BlockSpec((S, VDIM), lambda h, i: (0, hm + h))],
        out_specs=[oblk, oblk],
        out_shape=[jax.ShapeDtypeStruct((S, hm * VDIM), F32)] * 2,
        scratch_shapes=[pltpu.VMEM((S, QHEAD), BF16)],
        compiler_params=_cparams(("parallel", "arbitrary")),
    )(qfull, kv, kr, kv)


def _head_keys(kn_ref, kr_ref, k_scr):
    @pl.when(pl.program_id(1) == 0)
    def _():
        k_scr[:, :NOPE] = kn_ref[...]
        k_scr[:, NOPE:] = kr_ref[...]


def _attention_bwd(qfull, kv, kr, o, lse, d_o, hm, scale, *, tq, name):
    S = qfull.shape[0]
    tq = min(tq, S)
    nq = S // tq
    tn = (((0,), (0,)), ((), ()))
    nt = (((1,), (1,)), ((), ()))

    def body(q_ref, kn_ref, kr_ref, v_ref, o_ref, lse_ref, do_ref, dq_ref, dk_ref, dv_ref, k_scr):
        _head_keys(kn_ref, kr_ref, k_scr)
        s = lax.dot_general(q_ref[...], k_scr[...], nt, preferred_element_type=F32)
        p = jnp.exp(s * scale - lse_ref[:, 0:1])
        d_out = do_ref[...]
        delta = jnp.sum(d_out * o_ref[...], axis=-1, keepdims=True)
        d_out = d_out.astype(BF16)
        dp = lax.dot_general(d_out, v_ref[...], nt, preferred_element_type=F32)
        ds = (p * (dp - delta)).astype(BF16)
        dq_ref[...] = jnp.dot(ds, k_scr[...], preferred_element_type=F32) * scale
        dv = lax.dot_general(p.astype(BF16), d_out, tn, preferred_element_type=F32)
        dk = lax.dot_general(ds, q_ref[...], tn, preferred_element_type=F32)
        i = pl.program_id(1)
        for ref, val in ((dk_ref, dk), (dv_ref, dv)):
            @pl.when(i == 0)
            def _(ref=ref, val=val):
                ref[...] = val

            @pl.when(i > 0)
            def _(ref=ref, val=val):
                ref[...] += val

        @pl.when(i == nq - 1)
        def _():
            dk_ref[...] = dk_ref[...] * scale

    qblk = pl.BlockSpec((tq, QHEAD), lambda h, i: (i, h))
    oblk = pl.BlockSpec((tq, VDIM), lambda h, i: (i, h))
    return pl.pallas_call(
        body, name=name, grid=(hm, nq),
        in_specs=[qblk,
                  pl.BlockSpec((S, NOPE), lambda h, i: (0, h)),
                  pl.BlockSpec((S, LANES), lambda h, i: (0, 0)),
                  pl.BlockSpec((S, VDIM), lambda h, i: (0, hm + h)),
                  oblk, oblk, oblk],
        out_specs=[qblk, pl.BlockSpec((S, QHEAD), lambda h, i: (0, h)), pl.BlockSpec((S, VDIM), lambda h, i: (0, h))],
        out_shape=[jax.ShapeDtypeStruct((S, hm * QHEAD), F32), jax.ShapeDtypeStruct((S, hm * QHEAD), F32),
                   jax.ShapeDtypeStruct((S, hm * VDIM), F32)],
        scratch_shapes=[pltpu.VMEM((S, QHEAD), BF16)],
        compiler_params=_cparams(("parallel", "arbitrary")),
    )(qfull, kv, kr, kv, o, lse, d_o)


def _layout(D, MW, RW, TAIL, QR, KVR):
    names = ["gate_m", "gate_r", "z_m", "z_r", "q_a", "kv_a", "r", "k", "v", "tail"]
    widths = [D, D, MW, RW, QR, KVR, RW, RW, RW, TAIL]
    offs, o = {}, 0
    for nme, w in zip(names, widths):
        assert o % w == 0, (nme, o, w)
        offs[nme] = (o, w)
        o += w
    return offs, o


def _local_grads(x, target, W, dims, exchange=None):
    S, D = x.shape
    hm, hr, hn, rank = dims["hm"], dims["hr"], dims["hn"], dims["rank"]
    MW, RW = hm * VDIM, hr * hn
    TAIL = dims["TAIL"]
    QR, KVR = W["mla_q_norm"].shape[1], W["mla_kv_norm"].shape[1]
    lay, d_in = _layout(D, MW, RW, TAIL, QR, KVR)
    T = 256
    scale = (NOPE + ROPE) ** -0.5
    col = lambda arr, nme: _view(arr, *lay[nme])

    pos = jnp.arange(S, dtype=F32)
    inv_freq = jnp.power(ROPE_THETA, -jnp.arange(0, ROPE, 2, dtype=F32) / ROPE)
    ang = pos[:, None] * inv_freq[None, :]
    zpad = jnp.zeros((S, LANES - ROPE), F32)
    cosx = jnp.concatenate([jnp.cos(ang), jnp.cos(ang), zpad], axis=1)
    sinx = jnp.concatenate([jnp.sin(ang), jnp.sin(ang), zpad], axis=1)
    ri, ci = jnp.arange(LANES)[:, None], jnp.arange(LANES)[None, :]
    half = ROPE // 2
    rot = (jnp.where((ri == ci - half) & (ci >= half) & (ci < ROPE), 1.0, 0.0)
           - jnp.where((ri == ci + half) & (ci < half), 1.0, 0.0)).astype(BF16)
    seg = (jnp.arange(RW)[:, None] // hn == jnp.arange(LANES)[None, :]).astype(BF16)
    stacked = lambda t: jnp.concatenate([t, t], axis=0)
    rot, rot_t, seg, seg_t = stacked(rot), stacked(rot.T), stacked(seg), stacked(seg.T)

    (h,) = _rowwise(lambda xb, g: (_rms(xb, g),), [x], [W["g_pre"]], [(D, BF16)], tile=2 * T, name="pre_norm")
    if exchange is None:
        proj = _mm(h, W["w_in_t"], tb=True, name="in_proj")
    else:
        proj, *slabs = _mm(h, W["w_in_t"], tb=True, ride=_gather_plan(exchange[0]), name="in_proj")
        W = {**W, **_prepare_rest(dict(zip(_MATS[1:], slabs)), dims)}

    qn, kvn = _rowwise(_f_mla_norm, [col(proj, "q_a"), col(proj, "kv_a")], [W["mla_q_norm"], W["mla_kv_norm"]],
                       [(QR, BF16), (KVR, BF16)], tile=2 * T, name="mla_norm")
    qraw = _mm(qn, W["wq_b_t"], tb=True, name="q_up")
    kv = _mm(kvn, W["wkv_b"], out_dtype=BF16, name="kv_up")
    kr_view = _view(proj, lay["tail"][0], LANES)
    qfull, kr = _rowwise(functools.partial(_f_rope, hm), [qraw, kr_view, cosx, sinx], [rot, rot_t],
                         [(hm * QHEAD, BF16), (LANES, BF16)], tile=2 * T, name="rope")
    o_mla, lse = _attention_fwd(qfull, kv, kr, hm, scale, tq=T, name="attn_fwd")

    shift_view = (proj, lay["r"][0], 3 * RW + TAIL)
    rl = _shift_lerp(shift_view, W["mu"], name="shift_fwd")
    rl_r, rl_k, rl_v = _view(rl, 0, RW), _view(rl, RW, RW), _view(rl, 2 * RW, RW)
    rl_tail = _view(rl, 3 * RW, TAIL)
    pre_params = [W["w0_f"], W["w0_b"], W["a0_f"], W["a0_b"], W["k_k"], W["k_a"], W["w2cat"], W["a2cat"], seg, seg_t]
    pre_fn = functools.partial(_f_rwkv_pre, RW)
    lw_f, lw_b, k_f, k_b, a_n, b_f, b_b = _rowwise(pre_fn, [rl_k, rl_tail], pre_params, [(RW, F32)] * 7, tile=T,
                                                    name="rwkv_pre")
    ops_f = (rl_r, lw_f, k_f, rl_v, a_n, b_f)
    ops_b = (rl_r, lw_b, k_b, rl_v, a_n, b_b)
    y_f, st_f, y_b, st_b = _rwkv_scan_fwd(ops_f, ops_b, RW, name="scan_fwd")

    post_fn = functools.partial(_f_post, hn)
    post_rows = [y_f, y_b, rl_r, k_f, k_b, rl_v, col(proj, "z_r"), o_mla, col(proj, "z_m")]
    post_params = [W["gn_g"], W["gn_b"], W["r_k"], seg, seg_t]
    ymg, yrg = _rowwise(post_fn, post_rows, post_params, [(MW, BF16), (RW, BF16)], tile=T, name="post")
    u_m = _mm(ymg, W["w_br_mla"], name="br_mla")
    u_r = _mm(yrg, W["w_br_rwkv"], name="br_rwkv")
    merge_rows = [u_m, u_r, col(proj, "gate_m"), col(proj, "gate_r")]
    (merged,) = _rowwise(lambda *t: (_f_merge(*t),), merge_rows, [], [(D, BF16)], tile=T, name="merge")
    out = _mm(merged, W["w_out"], name="out_proj")

    def head(ob, xb, tb, g):
        yn, vjp = jax.vjp(_rms, ob, g)
        err = xb + yn - tb
        dy = err * (1.0 / D)
        d_ob, d_g = vjp(dy)
        loss = jnp.broadcast_to(0.5 * jnp.sum(err * err) * (1.0 / D), (1, LANES))
        return dy, d_ob, loss, d_g

    dy, d_out, loss, g_g_post = _rowwise(head, [out, x, target], [W["g_post"]], [(D, F32), (D, BF16)],
                                         [(1, LANES), (1, D)], tile=2 * T, name="head")
    d_merged = _mm(d_out, W["w_out"], tb=True, name="d_merged")
    g_w_out = _mm(merged, d_out, ta=True, out_dtype=BF16, name="g_w_out")

    def merge_bwd(u_m_b, u_r_b, g_m_b, g_r_b, dm):
        _, vjp = jax.vjp(_f_merge, u_m_b, u_r_b, g_m_b, g_r_b)
        du_m, du_r, dg_m, dg_r = vjp(dm)
        return du_m, du_r, jnp.concatenate([dg_m, dg_r], axis=1)

    d_u_m, d_u_r, d_proj = _rowwise(merge_bwd, merge_rows + [d_merged], [],
                                    [(D, BF16), (D, BF16), (2 * D, BF16, (None, d_in, lay["gate_m"][0]))], tile=T,
                                    name="merge_bwd")
    d_ymg = _mm(d_u_m, W["w_br_mla"], tb=True, name="d_ymg")
    d_yrg = _mm(d_u_r, W["w_br_rwkv"], tb=True, name="d_yrg")
    g_w_br_mla = _mm(ymg, d_u_m, ta=True, out_dtype=BF16, name="g_w_br_mla")
    g_w_br_rwkv = _mm(yrg, d_u_r, ta=True, out_dtype=BF16, name="g_w_br_rwkv")

    def post_bwd(*args):
        nr = len(post_rows)
        prim, dm, dr = args[:nr] + args[nr + 2:], args[nr], args[nr + 1]
        _, vjp = jax.vjp(post_fn, *prim)
        g = vjp((dm, dr))
        return g[0], g[2], g[3], g[5], g[7], jnp.concatenate([g[8], g[6]], axis=1), g[9], g[10], g[11]

    (d_y, d_r_bonus, d_k_bonus, d_v_bonus, d_o, d_proj, g_gn_g, g_gn_b, g_r_k) = _rowwise(
        post_bwd, post_rows + [d_ymg, d_yrg], post_params,
        [(RW, F32), (RW, F32), (RW, F32), (RW, F32), (MW, F32), (MW + RW, BF16, (d_proj, d_in, lay["z_m"][0]))],
        [(1, RW)] * 3, tile=T, name="post_bwd")

    dscan = _rwkv_scan_bwd(ops_f, ops_b, st_f, st_b, d_y, RW, name="scan_bwd")
    dsc = {"f": dscan[:6], "b": dscan[6:]}

    d_q_att, d_k_att, d_v_att = _attention_bwd(qfull, kv, kr, o_mla, lse, d_o, hm, scale, tq=4 * T, name="attn_bwd")

    def rope_bwd(qraw_b, kr_in, cos_b, sin_b, dq_b, dk_b, dv_b, rot_b, rot_t_b):
        _, vjp = jax.vjp(lambda q_, k_: _f_rope(hm, q_, k_, cos_b, sin_b, rot_b, rot_t_b), qraw_b, kr_in)
        dkn = jnp.concatenate([dk_b[:, hh * QHEAD:hh * QHEAD + NOPE] for hh in range(hm)], axis=1)
        dkr = dk_b[:, NOPE:QHEAD]
        for hh in range(1, hm):
            dkr = dkr + dk_b[:, hh * QHEAD + NOPE:(hh + 1) * QHEAD]
        d_qraw, d_kr_in = vjp((dq_b, dkr))
        return d_qraw, jnp.concatenate([dkn, dv_b], axis=1), d_kr_in

    d_qraw, d_kv, d_kr_in = _rowwise(rope_bwd, [qraw, kr_view, cosx, sinx, d_q_att, d_k_att, d_v_att],
                                     [rot, rot_t], [(hm * QHEAD, BF16), (2 * MW, BF16), (LANES, F32)], tile=T,
                                     name="rope_bwd")
    d_qnorm = _mm(d_qraw, W["wq_b_t"], name="d_qn")
    d_kvnorm = _mm(d_kv, W["wkv_b"], tb=True, name="d_kvn")
    g_wq_b = _mm(d_qraw, qn, ta=True, out_dtype=BF16, name="g_wq_b")
    g_wkv_b = _mm(kvn, d_kv, ta=True, out_dtype=BF16, name="g_wkv_b")

    def mla_norm_bwd(q_a, kv_a, qg, kvg, dq, dk):
        _, vjp = jax.vjp(_f_mla_norm, q_a, kv_a, qg, kvg)
        d_q_a, d_kv_a, d_qg, d_kvg = vjp((dq, dk))
        return jnp.concatenate([d_q_a, d_kv_a], axis=1), d_qg, d_kvg

    d_proj, g_q_norm, g_kv_norm = _rowwise(
        lambda q_a, kv_a, dq, dk, qg, kvg: mla_norm_bwd(q_a, kv_a, qg, kvg, dq, dk),
        [col(proj, "q_a"), col(proj, "kv_a"), d_qnorm, d_kvnorm], [W["mla_q_norm"], W["mla_kv_norm"]],
        [(QR + KVR, BF16, (d_proj, d_in, lay["q_a"][0]))], [(1, QR), (1, KVR)], tile=2 * T, name="mla_norm_bwd")

    def pre_bwd(k_b_, tail_b, dlwf, dlwb, dkf, dkb, dkbon, daf, dab, dbf, dbb, drf, drb, drbon, dvf, dvb, dvbon,
                dkr, *params):
        w2, a2 = params[6], params[7]
        nt, tn = (((1,), (1,)), ((), ())), (((0,), (0,)), ((), ()))
        split = w2.shape[0]
        th = jnp.tanh(tail_b[:, :split])
        th_b, tail_h = th.astype(BF16), tail_b[:, split:].astype(BF16)
        zw = jnp.dot(th_b, w2, preferred_element_type=F32)
        za = jnp.dot(tail_h, a2, preferred_element_type=F32)
        _, vjp = jax.vjp(functools.partial(_f_rwkv_core, RW), k_b_, zw, za, *params[:6], params[8], params[9])
        g = vjp((dlwf, dlwb, dkf + dkbon, dkb + dkbon, daf + dab, dbf, dbb))
        d_zw, d_za = g[1].astype(BF16), g[2].astype(BF16)
        d_tail = (jnp.concatenate([lax.dot_general(d_zw, w2, nt, preferred_element_type=F32) * (1.0 - th * th),
                                   lax.dot_general(d_za, a2, nt, preferred_element_type=F32)], axis=1)
                  + jnp.concatenate([dkr, jnp.zeros((dkr.shape[0], TAIL - LANES), F32)], axis=1))
        g_w2 = lax.dot_general(th_b, d_zw, tn, preferred_element_type=F32)
        g_a2 = lax.dot_general(tail_h, d_za, tn, preferred_element_type=F32)
        d_rl = jnp.concatenate([drf + drb + drbon, g[0], dvf + dvb + dvbon, d_tail], axis=1)
        return (d_rl,) + tuple(g[3:9]) + (g_w2, g_a2)

    f_, b_ = dsc["f"], dsc["b"]
    pre_bwd_rows = [rl_k, rl_tail, f_[1], b_[1], f_[2], b_[2], d_k_bonus, f_[4], b_[4], f_[5], b_[5],
                    f_[0], b_[0], d_r_bonus, f_[3], b_[3], d_v_bonus, d_kr_in]
    (d_rl, g_w0_f, g_w0_b, g_a0_f, g_a0_b, g_k_k, g_k_a, g_w2cat, g_a2cat) = _rowwise(
        pre_bwd, pre_bwd_rows, pre_params, [(3 * RW + TAIL, F32)],
        [(1, RW)] * 6 + [W["w2cat"].shape, W["a2cat"].shape], tile=T // 2, name="rwkv_pre_bwd")
    d_proj, g_mu = _shift_lerp(shift_view, W["mu"], d_rl, (d_proj, lay["r"][0]), name="shift_bwd")
    small = dict(wq_b=g_wq_b, wkv_b=g_wkv_b, w2cat=g_w2cat, a2cat=g_a2cat, w_br_mla=g_w_br_mla,
                 w_br_rwkv=g_w_br_rwkv, w_out=g_w_out)
    if exchange is None:
        received = None
        g_w_in = _mm(d_proj, h, ta=True, out_dtype=BF16, tn_cap=1024, name="g_w_in")
        d_h = _mm(d_proj, W["w_in_t"], tn_cap=1024, name="d_h")
    else:
        slabs = _restore_rest(small, dims)
        slabs = [slabs[n] for n in _MATS[1:]]
        g_w_in, *got = _mm(d_proj, h, ta=True, out_dtype=BF16, tn_cap=1024, ride=_sibling_swap_plan(slabs),
                           name="g_w_in")
        sums = [_pair_add(exchange[1], s, t, name="pair_add_" + n) for n, s, t in zip(_MATS[1:], slabs, got)]
        g_w_in = _restore_w_in(g_w_in, dims)
        d_h, *received = _mm(d_proj, W["w_in_t"], tn_cap=1024, name="d_h",
                             ride=_join_plans(_chip_exchange_plan(sums), _sibling_swap_plan([g_w_in])))
        small = {}

    def pre_norm_bwd(xb, dyb, dhb, g):
        _, vjp = jax.vjp(_rms, xb, g)
        dx, dg = vjp(dhb)
        return dyb + dx, dg

    grad_x, g_g_pre = _rowwise(pre_norm_bwd, [x, dy, d_h], [W["g_pre"]], [(D, F32)], [(1, D)], tile=2 * T,
                               name="pre_norm_bwd")

    grads = dict(g_pre=g_g_pre, w_in=g_w_in, mla_q_norm=g_q_norm, mla_kv_norm=g_kv_norm, mu=g_mu, w0_f=g_w0_f,
                 w0_b=g_w0_b, a0_f=g_a0_f, a0_b=g_a0_b, k_k=g_k_k, k_a=g_k_a, r_k=g_r_k, gn_g=g_gn_g, gn_b=g_gn_b,
                 g_post=g_g_post, **small)
    return loss[0, 0], grad_x, grads, received


_MATS = ["w_in", "mla_wq_b", "mla_wkv_b", "rwkv_w2_f", "rwkv_w2_b", "rwkv_a2_f", "rwkv_a2_b", "w_br_mla",
         "w_br_rwkv", "w_out"]
_ROW_SHARDED = ("w_out",)
_TRANSPOSED = ("w_in", "mla_wq_b")
_VECS = ["g_pre", "mla_q_norm", "mla_kv_norm", "rwkv_mu", "rwkv_w0_f", "rwkv_w0_b", "rwkv_a0_f", "rwkv_a0_b",
         "rwkv_k_k", "rwkv_k_a", "rwkv_r_k", "rwkv_gn_g", "rwkv_gn_b", "g_post"]
_WEIGHTS = ["g_pre", "w_in", "mla_q_norm", "mla_wq_b", "mla_kv_norm", "mla_wkv_b", "rwkv_mu", "rwkv_w0_f",
            "rwkv_w2_f", "rwkv_w0_b", "rwkv_w2_b", "rwkv_a0_f", "rwkv_a2_f", "rwkv_a0_b", "rwkv_a2_b", "rwkv_k_k",
            "rwkv_k_a", "rwkv_r_k", "rwkv_gn_g", "rwkv_gn_b", "w_br_mla", "w_br_rwkv", "w_out", "g_post"]

def _direct_gather_plan(src):
    def phases(src_refs, out_refs, sem_refs):
        (src_ref,), (out_ref,), sems, local_sem = src_refs, out_refs, sem_refs[:2], sem_refs[2]
        x, y, c = lax.axis_index("x"), lax.axis_index("y"), lax.axis_index("c")
        me = 4 * x + 2 * y + c
        flip = lambda v, bit: (1 - v) if bit else v
        peers = [(flip(x, d & 4), flip(y, d & 2), flip(c, d & 1)) for d in range(1, N_DEV)]
        own = lambda: pltpu.make_async_copy(src_ref, out_ref.at[me], local_sem)
        send = lambda d: _remote(src_ref, out_ref.at[me], sems, d, peers[d])

        def first():
            own().start()
            for d in range(N_DEV - 1):
                send(d).start()

        def last():
            for d, (px, py, pc) in enumerate(peers):
                blk = out_ref.at[4 * px + 2 * py + pc]
                _remote(blk, blk, sems, d, (x, y, c)).wait_recv()
            for d in range(N_DEV - 1):
                send(d).wait_send()
            own().wait()

        return first, (lambda: None), last

    return [src], [jax.ShapeDtypeStruct((N_DEV,) + src.shape, src.dtype)], [(N_DEV - 1,), (N_DEV - 1,), ()], phases


def _remote(src, dst, sems, key, to):
    send_sems, recv_sems = sems
    return pltpu.make_async_remote_copy(src_ref=src, dst_ref=dst, send_sem=send_sems.at[key], recv_sem=recv_sems.at[key],
                                        device_id=to, device_id_type=pl.DeviceIdType.MESH)


def _run_exchange(plan, *, name):
    srcs, out_shapes, sem_shapes, phases = plan
    n, m = len(srcs), len(out_shapes)

    def body(*refs):
        for phase in phases(refs[:n], refs[n:n + m], refs[n + m:]):
            phase()

    return pl.pallas_call(
        body, name=name, out_shape=out_shapes,
        in_specs=[pl.BlockSpec(memory_space=pl.ANY)] * n, out_specs=[pl.BlockSpec(memory_space=pl.ANY)] * m,
        scratch_shapes=[pltpu.SemaphoreType.DMA(s) for s in sem_shapes],
    )(*srcs)


def _join_plans(p, q):
    (srcs_p, outs_p, sems_p, phases_p), (srcs_q, outs_q, sems_q, phases_q) = p, q

    def phases(src_refs, out_refs, sem_refs):
        a = phases_p(src_refs[:len(srcs_p)], out_refs[:len(outs_p)], sem_refs[:len(sems_p)])
        b = phases_q(src_refs[len(srcs_p):], out_refs[len(outs_p):], sem_refs[len(sems_p):])

        def both(fa, fb):
            def run():
                fa()
                fb()
            return run

        return tuple(both(fa, fb) for fa, fb in zip(a, b))

    return list(srcs_p) + list(srcs_q), list(outs_p) + list(outs_q), list(sems_p) + list(sems_q), phases


def _gather_plan(srcs):
    n = len(srcs)

    def phases(src_refs, out_refs, sem_refs):
        sems, local_sems = sem_refs[:2], sem_refs[2]
        x, y, c = lax.axis_index("x"), lax.axis_index("y"), lax.axis_index("c")
        idx = lambda px, py, pc: 4 * px + 2 * py + pc
        me, sibling = (x, y, c), (x, y, 1 - c)
        chips = [(1 - x, y), (x, 1 - y), (1 - x, 1 - y)]
        own = lambda a: pltpu.make_async_copy(src_refs[a], out_refs[a].at[idx(*me)], local_sems.at[a])
        to_sibling = lambda a: _remote(src_refs[a], out_refs[a].at[idx(*me)], sems, (0, a), sibling)
        to_chip = lambda a, j: _remote(src_refs[a], out_refs[a].at[idx(*me)], sems, (1 + j, a), (*chips[j], c))
        landed = lambda a, j: out_refs[a].at[idx(*chips[j], c)]
        passed_on = lambda a, j: _remote(landed(a, j), landed(a, j), sems, (4 + j, a), sibling)

        def first():
            for a in range(n):
                own(a).start()
                to_sibling(a).start()
                for j in range(3):
                    to_chip(a, j).start()

        def middle():
            for j in range(3):
                for a in range(n):
                    _remote(landed(a, j), landed(a, j), sems, (1 + j, a), me).wait_recv()
                    passed_on(a, j).start()

        def last():
            for a in range(n):
                blk = out_refs[a].at[idx(*sibling)]
                _remote(blk, blk, sems, (0, a), me).wait_recv()
                for j in range(3):
                    blk = out_refs[a].at[idx(*chips[j], 1 - c)]
                    _remote(blk, blk, sems, (4 + j, a), me).wait_recv()
            for a in range(n):
                to_sibling(a).wait_send()
                for j in range(3):
                    to_chip(a, j).wait_send()
                    passed_on(a, j).wait_send()
                own(a).wait()

        return first, middle, last

    return srcs, [jax.ShapeDtypeStruct((N_DEV,) + s.shape, s.dtype) for s in srcs], [(7, n), (7, n), (n,)], phases


def _sibling_swap_plan(srcs):
    n = len(srcs)

    def phases(src_refs, out_refs, sems):
        x, y, c = lax.axis_index("x"), lax.axis_index("y"), lax.axis_index("c")
        copies = lambda: [_remote(src_refs[a].at[2 * q + 1 - c], out_refs[a].at[q], sems, (q, a), (x, y, 1 - c))
                          for a in range(n) for q in range(4)]

        def first():
            for cp in copies():
                cp.start()

        def last():
            for cp in copies():
                cp.wait()

        return first, (lambda: None), last

    return srcs, [jax.ShapeDtypeStruct((4,) + s.shape[1:], s.dtype) for s in srcs], [(4, n), (4, n)], phases


def _chip_exchange_plan(srcs):
    n = len(srcs)

    def phases(src_refs, out_refs, sem_refs):
        sems, local_sems = sem_refs[:2], sem_refs[2]
        x, y, c = lax.axis_index("x"), lax.axis_index("y"), lax.axis_index("c")
        mine = 2 * x + y
        chips = [(1 - x, y), (x, 1 - y), (1 - x, 1 - y)]
        own = lambda a: pltpu.make_async_copy(src_refs[a].at[mine], out_refs[a].at[mine], local_sems.at[a])
        send = lambda a, j: _remote(src_refs[a].at[2 * chips[j][0] + chips[j][1]], out_refs[a].at[mine], sems, (j, a),
                                    (*chips[j], c))

        def first():
            for a in range(n):
                own(a).start()
                for j in range(3):
                    send(a, j).start()

        def last():
            for j in range(3):
                for a in range(n):
                    blk = out_refs[a].at[2 * chips[j][0] + chips[j][1]]
                    _remote(blk, blk, sems, (j, a), (x, y, c)).wait_recv()
            for a in range(n):
                for j in range(3):
                    send(a, j).wait_send()
                own(a).wait()

        return first, (lambda: None), last

    return srcs, [jax.ShapeDtypeStruct(s.shape, s.dtype) for s in srcs], [(3, n), (3, n), (n,)], phases


def _pair_add(core, g, got, *, name):
    q, r, c = got.shape
    tr, tc = _tile2d(r, c, cap=1024)

    def body(core_ref, a_ref, b_ref, o_ref):
        o_ref[...] = (a_ref[...].astype(F32) + b_ref[...].astype(F32)).astype(BF16)

    blk = pl.BlockSpec((1, tr, tc), lambda i, j, k, core_ref: (i, j, k))
    mine = pl.BlockSpec((1, tr, tc), lambda i, j, k, core_ref: (2 * i + core_ref[0], j, k))
    return pl.pallas_call(
        body, name=name, out_shape=jax.ShapeDtypeStruct(got.shape, BF16),
        grid_spec=pltpu.PrefetchScalarGridSpec(num_scalar_prefetch=1, grid=(q, r // tr, c // tc),
                                               in_specs=[mine, blk], out_specs=blk),
        compiler_params=_cparams(("parallel", "parallel", "parallel")))(core, g, got)


def _adamw(recv, w, m, v, *, name):
    r, c = w.shape
    n_terms = recv.shape[0]
    tr, tc = _tile2d(r, c)

    def body(g_ref, w_ref, m_ref, v_ref, go_ref, d_ref, mo_ref, vo_ref):
        g = g_ref[0].astype(F32)
        for k in range(1, n_terms):
            g = g + g_ref[k].astype(F32)
        m_new = ADAM_B1 * m_ref[...] + (1.0 - ADAM_B1) * g
        v_new = ADAM_B2 * v_ref[...] + (1.0 - ADAM_B2) * (g * g)
        m_hat = m_new / (1.0 - ADAM_B1 ** ADAM_STEP)
        v_hat = v_new / (1.0 - ADAM_B2 ** ADAM_STEP)
        go_ref[...] = g
        d_ref[...] = -ADAM_LR * (m_hat / (jnp.sqrt(v_hat) + ADAM_EPS) + ADAM_WD * w_ref[...])
        mo_ref[...] = m_new
        vo_ref[...] = v_new

    blk = pl.BlockSpec((tr, tc), lambda i, j: (i, j))
    return pl.pallas_call(
        body, name=name, grid=(r // tr, c // tc),
        in_specs=[pl.BlockSpec((n_terms, tr, tc), lambda i, j: (0, i, j)), blk, blk, blk], out_specs=[blk] * 4,
        out_shape=[jax.ShapeDtypeStruct((r, c), F32)] * 4, compiler_params=_cparams(("parallel", "parallel")),
    )(recv, w, m, v)


def _tile2d(r, c, cap=256):
    if r <= cap:
        return r, c
    for t in range(cap - cap % BF16_ROWS, 0, -BF16_ROWS):
        if r % t == 0:
            return t, c
    return r, _pick(c, cap)


def _pack(pieces):
    total = sum(p.shape[0] for p in pieces)
    pad = (-total) % (8 * LANES)
    flat = jnp.concatenate(list(pieces) + [jnp.zeros((pad,), F32)])
    return flat.reshape(-1, LANES)


def _unpack(flat, sizes):
    flat = flat.reshape(-1)
    out, o = [], 0
    for n in sizes:
        out.append(flat[o:o + n])
        o += n
    return out


def _prepare_weights(full, vec, dims):
    rest = {n: t for n, t in full.items() if n != "w_in"}
    return {"w_in_t": _prepare_w_in(full["w_in"], dims), **_prepare_rest(rest, dims), **_prepare_vectors(vec, dims)}


def _prepare_w_in(slabs, dims):
    D = dims["D"]
    flat = slabs.reshape(-1, D)
    parts, pos = [], 0
    for orig_off, width, perm_off in sorted(dims["segs"], key=lambda t: t[2]):
        if perm_off > pos:
            parts.append(jnp.zeros((perm_off - pos, D), BF16))
        parts.append(flat[orig_off:orig_off + width])
        pos = perm_off + width
    if dims["d_in_perm"] > pos:
        parts.append(jnp.zeros((dims["d_in_perm"] - pos, D), BF16))
    return jnp.concatenate(parts, axis=0)


def _prepare_rest(full, dims):
    hm, hr, hn, rank = dims["hm"], dims["hr"], dims["hn"], dims["rank"]
    QR, KVR = dims["QR"], dims["KVR"]
    RW, TAIL = hr * hn, dims["TAIL"]
    full = {n: (t.reshape(-1, t.shape[2]) if n in _ROW_SHARDED + _TRANSPOSED
                else t.transpose(1, 0, 2).reshape(t.shape[1], -1)) for n, t in full.items()}
    wq = full["mla_wq_b"].reshape(hm, NOPE + ROPE, QR)
    wq = jnp.concatenate([wq, jnp.zeros((hm, QHEAD - NOPE - ROPE, QR), BF16)], axis=1).reshape(hm * QHEAD, QR)
    wkv = full["mla_wkv_b"].reshape(KVR, hm, 2, NOPE).transpose(0, 2, 1, 3).reshape(KVR, 2 * hm * NOPE)
    z = lambda rows: jnp.zeros((rows, RW), BF16)
    f = lambda nme: full[nme]
    split = ROPE + 2 * rank
    assert split % LANES == 0, split
    w2cat = jnp.concatenate([
        jnp.concatenate([z(ROPE), f("rwkv_w2_f"), z(rank)], axis=0),
        jnp.concatenate([z(ROPE + rank), f("rwkv_w2_b")], axis=0)], axis=1)
    a2cat = jnp.concatenate([
        jnp.concatenate([f("rwkv_a2_f"), z(TAIL - split - rank)], axis=0),
        jnp.concatenate([z(rank), f("rwkv_a2_b"), z(TAIL - split - 2 * rank)], axis=0)], axis=1)
    return dict(wq_b_t=wq, wkv_b=wkv, w2cat=w2cat, a2cat=a2cat, w_br_mla=full["w_br_mla"],
                w_br_rwkv=full["w_br_rwkv"], w_out=full["w_out"])


def _prepare_vectors(vec, dims):
    rank, RW, TAIL = dims["rank"], dims["hr"] * dims["hn"], dims["TAIL"]
    mu = vec["rwkv_mu"]
    mu_p = jnp.concatenate([mu[:3 * RW], jnp.zeros((ROPE,), F32), mu[3 * RW:],
                            jnp.zeros((TAIL - ROPE - 4 * rank,), F32)])
    row = lambda t: t.reshape(1, -1)
    return dict(
        mu=row(mu_p), g_pre=row(vec["g_pre"]), g_post=row(vec["g_post"]), mla_q_norm=row(vec["mla_q_norm"]),
        mla_kv_norm=row(vec["mla_kv_norm"]), w0_f=row(vec["rwkv_w0_f"]), w0_b=row(vec["rwkv_w0_b"]),
        a0_f=row(vec["rwkv_a0_f"]), a0_b=row(vec["rwkv_a0_b"]), k_k=row(vec["rwkv_k_k"]), k_a=row(vec["rwkv_k_a"]),
        r_k=row(vec["rwkv_r_k"]), gn_g=row(vec["rwkv_gn_g"]), gn_b=row(vec["rwkv_gn_b"]))


def _restore_grads(g, dims):
    return {"w_in": _restore_w_in(g["w_in"], dims), **_restore_rest(g, dims), **_restore_vectors(g, dims)}


def _restore_w_in(gw, dims):
    parts = [gw[perm_off:perm_off + width] for _, width, perm_off in sorted(dims["segs"])]
    return jnp.concatenate(parts, axis=0).reshape(N_DEV, dims["d_in"] // N_DEV, gw.shape[1])


def _restore_rest(g, dims):
    hm, hr, hn, rank = dims["hm"], dims["hr"], dims["hn"], dims["rank"]
    QR, KVR, RW = dims["QR"], dims["KVR"], hr * hn
    wq = g["wq_b"].reshape(hm, QHEAD, QR)[:, :NOPE + ROPE].reshape(N_DEV, -1, QR)
    wkv = g["wkv_b"].reshape(KVR, 2, hm, NOPE).transpose(0, 2, 1, 3).reshape(KVR, 2 * hm * NOPE)
    lo = lambda t, first, half: t[first:first + rank, half * RW:(half + 1) * RW].astype(BF16)
    cols = lambda t: t.reshape(t.shape[0], N_DEV, -1).transpose(1, 0, 2)
    return dict(
        mla_wq_b=wq, mla_wkv_b=cols(wkv), rwkv_w2_f=cols(lo(g["w2cat"], ROPE, 0)),
        rwkv_w2_b=cols(lo(g["w2cat"], ROPE + rank, 1)), rwkv_a2_f=cols(lo(g["a2cat"], 0, 0)),
        rwkv_a2_b=cols(lo(g["a2cat"], rank, 1)), w_br_mla=cols(g["w_br_mla"]), w_br_rwkv=cols(g["w_br_rwkv"]),
        w_out=g["w_out"].reshape(N_DEV, -1, g["w_out"].shape[1]))


def _restore_vectors(g, dims):
    rank, RW = dims["rank"], dims["hr"] * dims["hn"]
    mu = g["mu"][0]
    out = dict(
        rwkv_mu=jnp.concatenate([mu[:3 * RW], mu[3 * RW + ROPE:3 * RW + ROPE + 4 * rank]]),
        g_pre=g["g_pre"][0], g_post=g["g_post"][0], mla_q_norm=g["mla_q_norm"][0], mla_kv_norm=g["mla_kv_norm"][0],
        rwkv_w0_f=g["w0_f"][0], rwkv_w0_b=g["w0_b"][0], rwkv_a0_f=g["a0_f"][0], rwkv_a0_b=g["a0_b"][0],
        rwkv_k_k=g["k_k"][0], rwkv_k_a=g["k_a"][0], rwkv_r_k=g["r_k"][0], rwkv_gn_g=g["gn_g"][0],
        rwkv_gn_b=g["gn_b"][0])
    return out


def _dims(inp):
    D = inp["x"].shape[-1]
    QR, KVR = inp["mla_q_norm"].shape[0], inp["mla_kv_norm"].shape[0]
    hm = inp["mla_wq_b"].shape[1] * N_DEV // (NOPE + ROPE)
    hr, hn = inp["rwkv_r_k"].shape
    rank = inp["rwkv_w2_f"].shape[0]
    MW, RW = hm * VDIM, hr * hn
    TAIL = -(-(ROPE + 4 * rank) // LANES) * LANES
    orig, o = {}, 0
    for nme, w in (("q_a", QR), ("kv_a", KVR), ("k_rope", ROPE), ("rkv", 3 * RW), ("lora", 4 * rank), ("z_m", MW),
                   ("z_r", RW), ("gate_m", D), ("gate_r", D)):
        orig[nme] = (o, w)
        o += w
    assert o == inp["w_in"].shape[1] * N_DEV
    lay, d_in_perm = _layout(D, MW, RW, TAIL, QR, KVR)
    perm_off = dict(q_a=lay["q_a"][0], kv_a=lay["kv_a"][0], k_rope=lay["tail"][0], rkv=lay["r"][0],
                    lora=lay["tail"][0] + ROPE, z_m=lay["z_m"][0], z_r=lay["z_r"][0], gate_m=lay["gate_m"][0],
                    gate_r=lay["gate_r"][0])
    segs = [(orig[nme][0], orig[nme][1], perm_off[nme]) for nme in orig]
    return dict(D=D, QR=QR, KVR=KVR, hm=hm, hr=hr, hn=hn, rank=rank, TAIL=TAIL, segs=segs, d_in=o,
                d_in_perm=d_in_perm)


def kernel(x, g_pre, w_in, mla_q_norm, mla_wq_b, mla_kv_norm, mla_wkv_b, rwkv_mu, rwkv_w0_f, rwkv_w2_f, rwkv_w0_b, rwkv_w2_b, rwkv_a0_f, rwkv_a2_f, rwkv_a0_b, rwkv_a2_b, rwkv_k_k, rwkv_k_a, rwkv_r_k, rwkv_gn_g, rwkv_gn_b, w_br_mla, w_br_rwkv, w_out, g_post, loss_target, m_g_pre, m_w_in, m_mla_q_norm, m_mla_wq_b, m_mla_kv_norm, m_mla_wkv_b, m_rwkv_mu, m_rwkv_w0_f, m_rwkv_w2_f, m_rwkv_w0_b, m_rwkv_w2_b, m_rwkv_a0_f, m_rwkv_a2_f, m_rwkv_a0_b, m_rwkv_a2_b, m_rwkv_k_k, m_rwkv_k_a, m_rwkv_r_k, m_rwkv_gn_g, m_rwkv_gn_b, m_w_br_mla, m_w_br_rwkv, m_w_out, m_g_post, v_g_pre, v_w_in, v_mla_q_norm, v_mla_wq_b, v_mla_kv_norm, v_mla_wkv_b, v_rwkv_mu, v_rwkv_w0_f, v_rwkv_w2_f, v_rwkv_w0_b, v_rwkv_w2_b, v_rwkv_a0_f, v_rwkv_a2_f, v_rwkv_a0_b, v_rwkv_a2_b, v_rwkv_k_k, v_rwkv_k_a, v_rwkv_r_k, v_rwkv_gn_g, v_rwkv_gn_b, v_w_br_mla, v_w_br_rwkv, v_w_out, v_g_post):
    inp = dict(locals())
    dims = _dims(inp)
    stored = lambda t, n: t.T if n in _TRANSPOSED else t
    assert _MATS[0] == "w_in"
    shards = [stored(inp[n], n).astype(BF16) for n in _MATS]
    core = lax.axis_index("c").astype(jnp.int32).reshape(1)
    (w_in_slabs,) = _run_exchange(_gather_plan(shards[:1]), name="gather_w_in")
    W = {"w_in_t": _prepare_w_in(w_in_slabs, dims), **_prepare_vectors({n: inp[n] for n in _VECS}, dims)}
    loss, grad_x, g, recv_rest = _local_grads(x[0], loss_target[0], W, dims, exchange=(shards[1:], core))

    new = {}
    *recv_rest, got = recv_rest
    g_w_in, g = g["w_in"], _restore_vectors(g, dims)
    vsizes = [inp[n].size for n in _VECS] + [1]
    vflat = lambda prefix, src, last: _pack([src[prefix + n].reshape(-1) for n in _VECS] + [last])
    one = jnp.zeros((1,), F32)
    recv_w_in, vrecv = _run_exchange(
        _join_plans(_chip_exchange_plan([_pair_add(core, g_w_in, got, name="pair_add_w_in")]),
                    _direct_gather_plan(vflat("", g, loss.reshape(1)))), name="scatter_w_in")
    for n, t in zip(_MATS, [recv_w_in] + recv_rest):
        out = _adamw(t, stored(inp[n], n), stored(inp["m_" + n], n), stored(inp["v_" + n], n), name="adamw_" + n)
        new[n] = [stored(o, n) for o in out]

    vout = _adamw(vrecv, vflat("", inp, one), vflat("m_", inp, one), vflat("v_", inp, one), name="adamw_vectors")
    vparts = [_unpack(t, vsizes) for t in vout]
    for i, n in enumerate(_VECS):
        new[n] = [vp[i].reshape(inp[n].shape) for vp in vparts]
    loss = vparts[0][-1].reshape(())

    outs = [loss, grad_x[None]]
    for k in range(4):
        outs += [new[n][k] for n in _WEIGHTS]
    return tuple(outs)
```

```python
import functools
import math

import jax
import jax.numpy as jnp
from jax import lax
from jax.experimental import pallas as pl
from jax.experimental.pallas import tpu as pltpu

F32 = jnp.float32
BF16 = jnp.bfloat16

N_DEV = 8
LANES = 128
BF16_ROWS = 16
NOPE, ROPE, VDIM = 128, 64, 128
QHEAD = 256
ROPE_THETA = 10000.0
NORM_EPS = 1e-6
GN_EPS = 64e-5
CHUNK = 64
SUB = 16
VMEM_LIMIT = 56 * 1024 * 1024

ADAM_LR, ADAM_B1, ADAM_B2, ADAM_EPS, ADAM_WD, ADAM_STEP = 0.001, 0.9, 0.999, 1e-08, 0.01, 10


def _cparams(sem):
    return pltpu.CompilerParams(dimension_semantics=sem, vmem_limit_bytes=VMEM_LIMIT)


def _pick(n, cap):
    if n <= cap:
        return n
    for t in range(cap - cap % LANES, 0, -LANES):
        if n % t == 0:
            return t
    raise ValueError(f"no tile for {n} under {cap}")


def _mm(a, b, *, ta=False, tb=False, out_dtype=F32, name, tm_cap=1024, tn_cap=512, tk_cap=2048, ride=None):
    K, M = a.shape if ta else a.shape[::-1]
    N = b.shape[0] if tb else b.shape[1]
    assert (b.shape[1] if tb else b.shape[0]) == K, (a.shape, b.shape, ta, tb)
    tm, tn, tk = _pick(M, tm_cap), _pick(N, tn_cap), _pick(K, tk_cap)
    nj, nk = N // tn, K // tk
    steps = (M // tm) * nj * nk
    dn = (((0 if ta else 1,), (1 if tb else 0,)), ((), ()))
    srcs, extra_shapes, sem_shapes, phases = ride if ride else ((), (), (), None)
    n_src, n_extra = len(srcs), len(extra_shapes)

    def body(*refs):
        a_ref, b_ref, o_ref = refs[0], refs[1], refs[2 + n_src]
        acc_ref = refs[3 + n_src + n_extra]
        k = pl.program_id(2)
        if ride:
            step = (pl.program_id(0) * nj + pl.program_id(1)) * nk + k
            first, middle, last = phases(refs[2:2 + n_src], refs[3 + n_src:3 + n_src + n_extra],
                                         refs[4 + n_src + n_extra:])
            pl.when(step == 0)(first)
            pl.when(step == (steps * 15) // 16)(middle)
        p = lax.dot_general(a_ref[...], b_ref[...], dn, preferred_element_type=F32)

        @pl.when(k == 0)
        def _():
            acc_ref[...] = p

        @pl.when(k > 0)
        def _():
            acc_ref[...] += p

        @pl.when(k == nk - 1)
        def _():
            o_ref[...] = acc_ref[...].astype(out_dtype)

        if ride:
            pl.when(step == steps - 1)(last)

    a_spec = pl.BlockSpec((tk, tm), lambda i, j, k: (k, i)) if ta else pl.BlockSpec((tm, tk), lambda i, j, k: (i, k))
    b_spec = pl.BlockSpec((tn, tk), lambda i, j, k: (j, k)) if tb else pl.BlockSpec((tk, tn), lambda i, j, k: (k, j))
    hbm = pl.BlockSpec(memory_space=pl.ANY)
    out = pl.pallas_call(
        body, name=name, grid=(M // tm, nj, nk),
        in_specs=[a_spec, b_spec] + [hbm] * n_src,
        out_specs=[pl.BlockSpec((tm, tn), lambda i, j, k: (i, j))] + [hbm] * n_extra,
        out_shape=[jax.ShapeDtypeStruct((M, N), out_dtype)] + list(extra_shapes),
        scratch_shapes=[pltpu.VMEM((tm, tn), F32)] + [pltpu.SemaphoreType.DMA(s) for s in sem_shapes],
        compiler_params=_cparams(("arbitrary",) * 3 if ride else ("parallel", "parallel", "arbitrary")),
    )(a, b, *srcs)
    return out if ride else out[0]


def _view(arr, off, width):
    assert off % width == 0, (off, width)
    return (arr, off // width, width)


def _rowwise(fn, rows, params, out_rows, out_accs=(), *, tile, name):
    rows = [r if isinstance(r, tuple) else (r, 0, r.shape[1]) for r in rows]
    S = rows[0][0].shape[0]
    T = min(tile, S)
    assert S % T == 0
    n_rows, n_par, n_out = len(rows), len(params), len(out_rows)
    into = [o[2] if len(o) == 3 else None for o in out_rows]
    carried = [t[0] for t in into if t is not None and t[0] is not None]

    def body(*refs):
        ins = [r[...] for r in refs[:n_rows + n_par]]
        outs = fn(*ins)
        out_refs = refs[n_rows + n_par + len(carried):]
        for o_ref, val in zip(out_refs[:n_out], outs[:n_out]):
            o_ref[...] = val.astype(o_ref.dtype)
        i = pl.program_id(0)
        for o_ref, val in zip(out_refs[n_out:], outs[n_out:]):
            @pl.when(i == 0)
            def _(o_ref=o_ref, val=val):
                o_ref[...] = val

            @pl.when(i > 0)
            def _(o_ref=o_ref, val=val):
                o_ref[...] += val

    in_specs = [pl.BlockSpec((T, w), functools.partial(lambda i, cb: (i, cb), cb=cb)) for _, cb, w in rows]
    in_specs += [pl.BlockSpec(p.shape, lambda i: (0, 0)) for p in params]
    in_specs += [pl.BlockSpec(memory_space=pl.ANY)] * len(carried)
    out_specs, out_shape, aliases = [], [], {}
    for k, (o, t) in enumerate(zip(out_rows, into)):
        w, dt = o[0], o[1]
        if t is None:
            out_specs.append(pl.BlockSpec((T, w), lambda i: (i, 0)))
            out_shape.append(jax.ShapeDtypeStruct((S, w), dt))
            continue
        buf, total, first = t
        assert first % w == 0
        out_specs.append(pl.BlockSpec((T, w), functools.partial(lambda i, cb: (i, cb), cb=first // w)))
        out_shape.append(jax.ShapeDtypeStruct((S, total), dt))
        if buf is not None:
            aliases[n_rows + n_par + len(aliases)] = k
    out_specs += [pl.BlockSpec(s, lambda i: (0, 0)) for s in out_accs]
    out_shape += [jax.ShapeDtypeStruct(s, F32) for s in out_accs]
    return pl.pallas_call(
        body, name=name, grid=(S // T,), in_specs=in_specs, out_specs=out_specs, out_shape=out_shape,
        input_output_aliases=aliases, compiler_params=_cparams(("arbitrary",)),
    )(*[r[0] for r in rows], *params, *carried)


def _mm_sel(x, sel2):
    hi = x.astype(BF16)
    lo = (x - hi.astype(F32)).astype(BF16)
    return jnp.dot(jnp.concatenate([hi, lo], axis=1), sel2, preferred_element_type=F32)


@jax.custom_vjp
def _sel(x, sel, sel_t):
    return _mm_sel(x, sel)


def _sel_fwd(x, sel, sel_t):
    return _mm_sel(x, sel), (sel, sel_t)


def _sel_bwd(res, ct):
    sel, sel_t = res
    return _mm_sel(ct, sel_t), jnp.zeros_like(sel), jnp.zeros_like(sel_t)


_sel.defvjp(_sel_fwd, _sel_bwd)


def _rms(x, g):
    return x * lax.rsqrt(jnp.mean(x * x, axis=-1, keepdims=True) + NORM_EPS) * g


def _sigmoid(x):
    return 0.5 * jnp.tanh(0.5 * x) + 0.5


def _silu(x):
    return x * _sigmoid(x)


def _softplus(x):
    return jnp.maximum(x, 0.0) + jnp.log(1.0 + jnp.exp(-jnp.abs(x)))


def _f_mla_norm(q_a, kv_a, qg, kvg):
    return _rms(q_a, qg), _rms(kv_a, kvg)


def _f_rope(hm, qraw, kr_in, cosx, sinx, rot, rot_t):
    def rope(t):
        return t * cosx + _sel(t, rot, rot_t) * sinx
    parts = []
    for h in range(hm):
        parts.append(qraw[:, h * QHEAD:h * QHEAD + NOPE])
        parts.append(rope(qraw[:, h * QHEAD + NOPE:(h + 1) * QHEAD]))
    return jnp.concatenate(parts, axis=1), rope(kr_in)


def _f_rwkv_pre(rw, k, tail, w0f, w0b, a0f, a0b, k_k, k_a, w2cat, a2cat, seg, seg_t):
    split = w2cat.shape[0]
    zw = jnp.dot(jnp.tanh(tail[:, :split]).astype(BF16), w2cat, preferred_element_type=F32)
    za = jnp.dot(tail[:, split:].astype(BF16), a2cat, preferred_element_type=F32)
    return _f_rwkv_core(rw, k, zw, za, w0f, w0b, a0f, a0b, k_k, k_a, seg, seg_t)


def _f_rwkv_core(rw, k, zw, za, w0f, w0b, a0f, a0b, k_k, k_a, seg, seg_t):
    lw_f = -jnp.exp(-_softplus(-(w0f + zw[:, :rw])) - 0.5)
    lw_b = -jnp.exp(-_softplus(-(w0b + zw[:, rw:])) - 0.5)
    a_f = _sigmoid(a0f + za[:, :rw])
    a_b = _sigmoid(a0b + za[:, rw:])
    kk = k * k_k
    nrm = jnp.sqrt(_sel(_sel(kk * kk, seg, seg_t), seg_t, seg))
    kk = kk / jnp.maximum(nrm, 1e-12)
    k_f = k * (1.0 + (a_f - 1.0) * k_a)
    k_b = k * (1.0 + (a_b - 1.0) * k_a)
    return lw_f, lw_b, k_f, k_b, -kk, kk * a_f, kk * a_b


def _f_post(hn, y_f, y_b, r, k_f, k_b, v, z_r, o_mla, z_m, gn_g, gn_b, r_k, seg, seg_t):
    segsum = lambda t: _sel(_sel(t, seg, seg_t), seg_t, seg)
    y = y_f + y_b
    mu = segsum(y) * (1.0 / hn)
    yc = y - mu
    var = segsum(yc * yc) * (1.0 / hn)
    yn = yc * lax.rsqrt(var + GN_EPS) * gn_g + gn_b
    bonus = segsum(r * (k_f + k_b) * r_k) * v
    return o_mla * _silu(z_m), (yn + bonus) * _silu(z_r)


def _f_merge(u_m, u_r, g_m, g_r):
    return _sigmoid(g_m) * u_m + _sigmoid(g_r) * u_r


_NN = ((2,), (1,))
_NT = ((2,), (2,))
_TN = ((1,), (1,))

_SCAN_PASSES = {"cum": 2, "gram": 3, "solve": 1, "apply": 1, "state": 1}


def _hdot_raw(passes, x, y, dims):
    dn = (dims, ((0,), (0,)))
    d = lambda p, q: lax.dot_general(p, q, dn, preferred_element_type=F32)
    xh = x.astype(BF16)
    yh = y.astype(BF16)
    if passes == 1:
        return d(xh, yh)
    yl = (y - yh.astype(F32)).astype(BF16)
    kx, ky = (1 if dims == _TN else 2), (2 if dims == _NT else 1)
    depth = x.shape[kx]
    if all(axis == 1 or depth % LANES == 0 for axis in (kx, ky)):
        if passes == 2:
            return d(jnp.concatenate([xh, xh], axis=kx), jnp.concatenate([yh, yl], axis=ky))
        xl = (x - xh.astype(F32)).astype(BF16)
        return d(jnp.concatenate([xh, xl, xh], axis=kx), jnp.concatenate([yh, yh, yl], axis=ky))
    if passes == 2:
        axis = 1 if dims == _NT else 2
        width = y.shape[axis]
        both = d(xh, jnp.concatenate([yh, yl], axis=axis))
        return both[:, :, :width] + both[:, :, width:]
    xl = (x - xh.astype(F32)).astype(BF16)
    if dims == _TN:
        return d(xh, yh) + d(xh, yl) + d(xl, yh)
    rows = x.shape[1]
    both = d(jnp.concatenate([xh, xl], axis=1), yh)
    return both[:, :rows] + both[:, rows:] + d(xh, yl)


@functools.partial(jax.custom_vjp, nondiff_argnums=(2, 3))
def _hdot_p(x, y, dims, passes):
    return _hdot_raw(passes, x, y, dims)


def _hdot_fwd(x, y, dims, passes):
    return _hdot_raw(passes, x, y, dims), (x, y)


def _hdot_bwd(dims, passes, res, ct):
    x, y = res
    if dims == _NN:
        return _hdot_raw(passes, ct, y, _NT), _hdot_raw(passes, x, ct, _TN)
    if dims == _NT:
        return _hdot_raw(passes, ct, y, _NN), _hdot_raw(passes, ct, x, _TN)
    return _hdot_raw(passes, y, ct, _NT), _hdot_raw(passes, x, ct, _NN)


_hdot_p.defvjp(_hdot_fwd, _hdot_bwd)


def _hdot(x, y, dims, kind):
    return _hdot_p(x, y, dims, _SCAN_PASSES[kind])


def _tri_solve(n_mat, x, length, blocks):
    row = lax.broadcasted_iota(jnp.int32, (length, 2 * length), 0)
    col = lax.broadcasted_iota(jnp.int32, (length, 2 * length), 1)
    col = jnp.where(col >= length, col - length, col)
    eye = (row == col).astype(F32)[None]
    diag_blk = ((row // SUB) == (col // SUB))[None]
    nd = jnp.where(diag_blk, n_mat, 0.0)
    no = n_mat - nd
    dinv = eye + nd
    p = _hdot(nd, blocks(nd), _NN, "solve")
    for k in range(int(math.log2(SUB)) - 1):
        if k == int(math.log2(SUB)) - 2:
            dinv = dinv + _hdot(dinv, blocks(p), _NN, "solve")
        else:
            both = _hdot(jnp.concatenate([dinv, p], axis=1), blocks(p), _NN, "solve")
            dinv, p = dinv + both[:, :length], both[:, length:]
    width = x.shape[2]
    both = _hdot(dinv, jnp.concatenate([blocks(x), blocks(no)], axis=2), _NN, "solve")
    u, q = both[:, :, :width], both[:, :, width:]
    for level in range(int(math.log2(length // SUB))):
        if level == int(math.log2(length // SUB)) - 1:
            u = u + _hdot(q, blocks(u), _NN, "solve")
        else:
            both = _hdot(q, jnp.concatenate([blocks(u), blocks(q)], axis=2), _NN, "solve")
            u, q = u + both[:, :, :width], both[:, :, width:]
    return u


def _rwkv_chunk(rev, s0, r, lw, k, v, a, b):
    pairs, length, width = r.shape
    hn = width // 2
    assert 2 * length == width
    row = lax.broadcasted_iota(jnp.int32, (length, length), 0)
    col = lax.broadcasted_iota(jnp.int32, (length, length), 1)
    row2 = lax.broadcasted_iota(jnp.int32, (length, 2 * length), 0)
    col2 = lax.broadcasted_iota(jnp.int32, (length, 2 * length), 1)
    col2 = jnp.where(col2 >= length, col2 - length, col2)
    if rev is None:
        half = pairs // 2
        back = lax.broadcasted_iota(jnp.int32, (pairs, length, length), 0) >= half
        back2 = lax.broadcasted_iota(jnp.int32, (pairs, length, 2 * length), 0) >= half
        ahead = jnp.where(back, (col - row)[None], (row - col)[None])
        ahead2 = jnp.where(back2, (col2 - row2)[None], (row2 - col2)[None])
        incl, strict2, incl2 = ahead >= 0, ahead2 > 0, ahead2 >= 0
    else:
        incl = ((row <= col) if rev else (row >= col))[None]
        strict2 = ((row2 < col2) if rev else (row2 > col2))[None]
        incl2 = ((row2 <= col2) if rev else (row2 >= col2))[None]
    first = (lax.broadcasted_iota(jnp.int32, (1, 1, width), 2) < hn).astype(F32)
    blocks = lambda t: jnp.concatenate([t * first, t * (1.0 - first)], axis=1)

    t_incl = jnp.broadcast_to(incl.astype(F32), (pairs, length, length))
    cum = _hdot(t_incl, lw, _NN, "cum")
    g = jnp.exp(cum)
    g_inv = jnp.exp(-cum)
    at = a * jnp.exp(cum - lw)
    rt = r * g
    bt = b * g_inv
    kt = k * g_inv
    both_rows = jnp.concatenate([at, rt], axis=1)
    gram = _hdot(both_rows, jnp.concatenate([blocks(bt), blocks(kt)], axis=1), _NT, "gram")
    a_ab = jnp.where(strict2, gram[:, :length, :width], 0.0)
    a_ak = jnp.where(strict2, gram[:, :length, width:], 0.0)
    a_rb = jnp.where(incl2, gram[:, length:, :width], 0.0)
    a_rk = jnp.where(incl2, gram[:, length:, width:], 0.0)
    from_state = _hdot(both_rows, s0, _NT, "apply")
    x = from_state[:, :length] + _hdot(a_ak, blocks(v), _NN, "apply")
    u = _tri_solve(a_ab, x, length, blocks)
    y = from_state[:, length:] + _hdot(jnp.concatenate([a_rb, a_rk], axis=2),
                                       jnp.concatenate([blocks(u), blocks(v)], axis=1), _NN, "apply")
    g_last = jnp.exp(jnp.sum(lw, axis=1, keepdims=True))
    ri = lax.broadcasted_iota(jnp.int32, (width, width), 0)
    ci = lax.broadcasted_iota(jnp.int32, (width, width), 1)
    same_head = ((ri < hn) == (ci < hn))[None]
    upd = _hdot(jnp.concatenate([u, v], axis=1), jnp.concatenate([bt, kt], axis=1), _TN, "state")
    s1 = (s0 + jnp.where(same_head, upd, 0.0)) * g_last
    return y, s1


def _split_pairs(x):
    return jnp.stack([x[:, p * LANES:(p + 1) * LANES] for p in range(x.shape[1] // LANES)])


def _merge_pairs(x):
    return jnp.concatenate([x[p] for p in range(x.shape[0])], axis=1)


def _scan_specs(views, rw, nc, rev):
    cidx = (lambda c: nc - 1 - c) if rev else (lambda c: c)
    seqs = [pl.BlockSpec((CHUNK, rw), functools.partial(lambda c, cb: (cidx(c), cb), cb=cb)) for _, cb, _ in views]
    plain = pl.BlockSpec((CHUNK, rw), lambda c: (cidx(c), 0))
    st = pl.BlockSpec((1, rw // LANES, LANES, LANES), lambda c: (cidx(c), 0, 0, 0))
    return seqs, plain, st


def _as_views(arrs, rw):
    return [t if isinstance(t, tuple) else (t, 0, rw) for t in arrs]


def _rwkv_scan_fwd(ops_f, ops_b, rw, *, name):
    S = _as_views(ops_f, rw)[0][0].shape[0]
    nc, pairs = S // CHUNK, rw // LANES
    in_specs, out_specs, arrays = [], [], []
    for rev, ops in ((False, ops_f), (True, ops_b)):
        views = _as_views(ops, rw)
        seqs, plain, st = _scan_specs(views, rw, nc, rev)
        in_specs += seqs
        out_specs += [plain, st]
        arrays += [t[0] for t in views]

    def both(refs_f, refs_b):
        return [jnp.concatenate([_split_pairs(f[...]), _split_pairs(b[...])], axis=0) for f, b in zip(refs_f, refs_b)]

    def body(*refs):
        (y_f, st_f, y_b, st_b), s_ref = refs[12:16], refs[16]

        @pl.when(pl.program_id(0) == 0)
        def _():
            s_ref[...] = jnp.zeros_like(s_ref)

        s0 = s_ref[...]
        st_f[0] = s0[:pairs]
        st_b[0] = s0[pairs:]
        y, s1 = _rwkv_chunk(None, s0, *both(refs[:6], refs[6:12]))
        y_f[...] = _merge_pairs(y[:pairs])
        y_b[...] = _merge_pairs(y[pairs:])
        s_ref[...] = s1

    return pl.pallas_call(
        body, name=name, grid=(nc,), in_specs=in_specs, out_specs=out_specs,
        out_shape=[jax.ShapeDtypeStruct((S, rw), F32), jax.ShapeDtypeStruct((nc, pairs, LANES, LANES), F32)] * 2,
        scratch_shapes=[pltpu.VMEM((2 * pairs, LANES, LANES), F32)],
        compiler_params=_cparams(("arbitrary",)),
    )(*arrays)


def _rwkv_scan_bwd(ops_f, ops_b, states_f, states_b, dy, rw, *, name):
    S = dy.shape[0]
    nc, pairs = S // CHUNK, rw // LANES
    in_specs, arrays = [], []
    for rev, ops, states in ((False, ops_f, states_f), (True, ops_b, states_b)):
        views = _as_views(list(ops) + [dy], rw)
        seqs, plain, st = _scan_specs(views, rw, nc, not rev)
        in_specs += seqs + [st]
        arrays += [t[0] for t in views] + [states]
    out_specs = []
    for rev in (False, True):
        out_specs += [_scan_specs([], rw, nc, not rev)[1]] * 6

    def both(refs_f, refs_b):
        return [jnp.concatenate([_split_pairs(f[...]), _split_pairs(b[...])], axis=0) for f, b in zip(refs_f, refs_b)]

    def body(*refs):
        ds_ref = refs[28]

        @pl.when(pl.program_id(0) == 0)
        def _():
            ds_ref[...] = jnp.zeros_like(ds_ref)

        s0 = jnp.concatenate([refs[7][0], refs[15][0]], axis=0)
        _, vjp = jax.vjp(functools.partial(_rwkv_chunk, None), s0, *both(refs[:6], refs[8:14]))
        (dy,) = both(refs[6:7], refs[14:15])
        grads = vjp((dy, ds_ref[...]))
        ds_ref[...] = grads[0]
        for o_f, o_b, gval in zip(refs[16:22], refs[22:28], grads[1:]):
            o_f[...] = _merge_pairs(gval[:pairs])
            o_b[...] = _merge_pairs(gval[pairs:])

    return pl.pallas_call(
        body, name=name, grid=(nc,), in_specs=in_specs, out_specs=out_specs,
        out_shape=[jax.ShapeDtypeStruct((S, rw), F32)] * 12,
        scratch_shapes=[pltpu.VMEM((2 * pairs, LANES, LANES), F32)],
        compiler_params=_cparams(("arbitrary",)),
    )(*arrays)


def _shift_lerp(x_view, mu, d=None, into=None, *, name):
    arr, off, width = x_view
    S = arr.shape[0]
    cb = _pick(width, 512)
    assert off % cb == 0

    def cshift(t):
        rows = lax.broadcasted_iota(jnp.int32, t.shape, 0)
        prev = jnp.where(rows == 0, 0.0, pltpu.roll(t, 1, 0))
        nxt = jnp.where(rows == S - 1, 0.0, pltpu.roll(t, S - 1, 0))
        return 0.5 * (prev + nxt)

    def fwd_body(x_ref, mu_ref, o_ref):
        x = x_ref[...]
        o_ref[...] = x + mu_ref[...] * (cshift(x) - x)

    def bwd_body(x_ref, mu_ref, d_ref, _, dx_ref, dmu_ref):
        x, m, dd = x_ref[...], mu_ref[...], d_ref[...]
        gm = m * dd
        dx_ref[...] = (dd - gm + cshift(gm)).astype(dx_ref.dtype)
        dmu_ref[...] = jnp.sum(dd * (cshift(x) - x), axis=0, keepdims=True)

    x_spec = pl.BlockSpec((S, cb), lambda j: (0, off // cb + j))
    blk = pl.BlockSpec((S, cb), lambda j: (0, j))
    vec = pl.BlockSpec((1, cb), lambda j: (0, j))
    if d is None:
        return pl.pallas_call(
            fwd_body, name=name, grid=(width // cb,), in_specs=[x_spec, vec], out_specs=blk,
            out_shape=jax.ShapeDtypeStruct((S, width), F32), compiler_params=_cparams(("parallel",)),
        )(arr, mu)
    buf, first = into
    assert first % cb == 0
    return pl.pallas_call(
        bwd_body, name=name, grid=(width // cb,),
        in_specs=[x_spec, vec, blk, pl.BlockSpec(memory_space=pl.ANY)],
        out_specs=[pl.BlockSpec((S, cb), lambda j: (0, first // cb + j)), vec],
        out_shape=[jax.ShapeDtypeStruct(buf.shape, buf.dtype), jax.ShapeDtypeStruct((1, width), F32)],
        input_output_aliases={3: 0}, compiler_params=_cparams(("parallel",)),
    )(arr, mu, d, buf)


def _attention_fwd(qfull, kv, kr, hm, scale, *, tq, name):
    S = qfull.shape[0]
    nt = (((1,), (1,)), ((), ()))

    def body(q_ref, kn_ref, kr_ref, v_ref, o_ref, lse_ref, k_scr):
        _head_keys(kn_ref, kr_ref, k_scr)
        s = lax.dot_general(q_ref[...], k_scr[...], nt, preferred_element_type=F32)
        m = jnp.max(s, axis=-1, keepdims=True)
        p = jnp.exp((s - m) * scale)
        l = jnp.sum(p, axis=-1, keepdims=True)
        o_ref[...] = jnp.dot(p.astype(BF16), v_ref[...], preferred_element_type=F32) * (1.0 / l)
        lse_ref[...] = jnp.broadcast_to(m * scale + jnp.log(l), lse_ref.shape)

    oblk = pl.BlockSpec((tq, VDIM), lambda h, i: (i, h))
    return pl.pallas_call(
        body, name=name, grid=(hm, S // tq),
        in_specs=[pl.BlockSpec((tq, QHEAD), lambda h, i: (i, h)),
                  pl.BlockSpec((S, NOPE), lambda h, i: (0, h)),
                  pl.BlockSpec((S, LANES), lambda h, i: (0, 0)),
                  pl.BlockSpec((S, VDIM), lambda h, i: (0, hm + h))],
        out_specs=[oblk, oblk],
        out_shape=[jax.ShapeDtypeStruct((S, hm * VDIM), F32)] * 2,
        scratch_shapes=[pltpu.VMEM((S, QHEAD), BF16)],
        compiler_params=_cparams(("parallel", "arbitrary")),
    )(qfull, kv, kr, kv)


def _head_keys(kn_ref, kr_ref, k_scr):
    @pl.when(pl.program_id(1) == 0)
    def _():
        k_scr[:, :NOPE] = kn_ref[...]
        k_scr[:, NOPE:] = kr_ref[...]


def _attention_bwd(qfull, kv, kr, o, lse, d_o, hm, scale, *, tq, name):
    S = qfull.shape[0]
    tq = min(tq, S)
    nq = S // tq
    tn = (((0,), (0,)), ((), ()))
    nt = (((1,), (1,)), ((), ()))

    def body(q_ref, kn_ref, kr_ref, v_ref, o_ref, lse_ref, do_ref, dq_ref, dk_ref, dv_ref, k_scr):
        _head_keys(kn_ref, kr_ref, k_scr)
        s = lax.dot_general(q_ref[...], k_scr[...], nt, preferred_element_type=F32)
        p = jnp.exp(s * scale - lse_ref[:, 0:1])
        d_out = do_ref[...]
        delta = jnp.sum(d_out * o_ref[...], axis=-1, keepdims=True)
        d_out = d_out.astype(BF16)
        dp = lax.dot_general(d_out, v_ref[...], nt, preferred_element_type=F32)
        ds = (p * (dp - delta)).astype(BF16)
        dq_ref[...] = jnp.dot(ds, k_scr[...], preferred_element_type=F32) * scale
        dv = lax.dot_general(p.astype(BF16), d_out, tn, preferred_element_type=F32)
        dk = lax.dot_general(ds, q_ref[...], tn, preferred_element_type=F32)
        i = pl.program_id(1)
        for ref, val in ((dk_ref, dk), (dv_ref, dv)):
            @pl.when(i == 0)
            def _(ref=ref, val=val):
                ref[...] = val

            @pl.when(i > 0)
            def _(ref=ref, val=val):
                ref[...] += val

        @pl.when(i == nq - 1)
        def _():
            dk_ref[...] = dk_ref[...] * scale

    qblk = pl.BlockSpec((tq, QHEAD), lambda h, i: (i, h))
    oblk = pl.BlockSpec((tq, VDIM), lambda h, i: (i, h))
    return pl.pallas_call(
        body, name=name, grid=(hm, nq),
        in_specs=[qblk,
                  pl.BlockSpec((S, NOPE), lambda h, i: (0, h)),
                  pl.BlockSpec((S, LANES), lambda h, i: (0, 0)),
                  pl.BlockSpec((S, VDIM), lambda h, i: (0, hm + h)),
                  oblk, oblk, oblk],
        out_specs=[qblk, pl.BlockSpec((S, QHEAD), lambda h, i: (0, h)), pl.BlockSpec((S, VDIM), lambda h, i: (0, h))],
        out_shape=[jax.ShapeDtypeStruct((S, hm * QHEAD), F32), jax.ShapeDtypeStruct((S, hm * QHEAD), F32),
                   jax.ShapeDtypeStruct((S, hm * VDIM), F32)],
        scratch_shapes=[pltpu.VMEM((S, QHEAD), BF16)],
        compiler_params=_cparams(("parallel", "arbitrary")),
    )(qfull, kv, kr, kv, o, lse, d_o)


def _layout(D, MW, RW, TAIL, QR, KVR):
    names = ["gate_m", "gate_r", "z_m", "z_r", "q_a", "kv_a", "r", "k", "v", "tail"]
    widths = [D, D, MW, RW, QR, KVR, RW, RW, RW, TAIL]
    offs, o = {}, 0
    for nme, w in zip(names, widths):
        assert o % w == 0, (nme, o, w)
        offs[nme] = (o, w)
        o += w
    return offs, o


def _local_grads(x, target, W, dims, exchange=None):
    S, D = x.shape
    hm, hr, hn, rank = dims["hm"], dims["hr"], dims["hn"], dims["rank"]
    MW, RW = hm * VDIM, hr * hn
    TAIL = dims["TAIL"]
    QR, KVR = W["mla_q_norm"].shape[1], W["mla_kv_norm"].shape[1]
    lay, d_in = _layout(D, MW, RW, TAIL, QR, KVR)
    T = 256
    scale = (NOPE + ROPE) ** -0.5
    col = lambda arr, nme: _view(arr, *lay[nme])

    pos = jnp.arange(S, dtype=F32)
    inv_freq = jnp.power(ROPE_THETA, -jnp.arange(0, ROPE, 2, dtype=F32) / ROPE)
    ang = pos[:, None] * inv_freq[None, :]
    zpad = jnp.zeros((S, LANES - ROPE), F32)
    cosx = jnp.concatenate([jnp.cos(ang), jnp.cos(ang), zpad], axis=1)
    sinx = jnp.concatenate([jnp.sin(ang), jnp.sin(ang), zpad], axis=1)
    ri, ci = jnp.arange(LANES)[:, None], jnp.arange(LANES)[None, :]
    half = ROPE // 2
    rot = (jnp.where((ri == ci - half) & (ci >= half) & (ci < ROPE), 1.0, 0.0)
           - jnp.where((ri == ci + half) & (ci < half), 1.0, 0.0)).astype(BF16)
    seg = (jnp.arange(RW)[:, None] // hn == jnp.arange(LANES)[None, :]).astype(BF16)
    stacked = lambda t: jnp.concatenate([t, t], axis=0)
    rot, rot_t, seg, seg_t = stacked(rot), stacked(rot.T), stacked(seg), stacked(seg.T)

    (h,) = _rowwise(lambda xb, g: (_rms(xb, g),), [x], [W["g_pre"]], [(D, BF16)], tile=2 * T, name="pre_norm")
    if exchange is None:
        proj = _mm(h, W["w_in_t"], tb=True, name="in_proj")
    else:
        proj, *slabs = _mm(h, W["w_in_t"], tb=True, ride=_gather_plan(exchange[0]), name="in_proj")
        W = {**W, **_prepare_rest(dict(zip(_MATS[1:], slabs)), dims)}

    qn, kvn = _rowwise(_f_mla_norm, [col(proj, "q_a"), col(proj, "kv_a")], [W["mla_q_norm"], W["mla_kv_norm"]],
                       [(QR, BF16), (KVR, BF16)], tile=2 * T, name="mla_norm")
    qraw = _mm(qn, W["wq_b_t"], tb=True, name="q_up")
    kv = _mm(kvn, W["wkv_b"], out_dtype=BF16, name="kv_up")
    kr_view = _view(proj, lay["tail"][0], LANES)
    qfull, kr = _rowwise(functools.partial(_f_rope, hm), [qraw, kr_view, cosx, sinx], [rot, rot_t],
                         [(hm * QHEAD, BF16), (LANES, BF16)], tile=2 * T, name="rope")
    o_mla, lse = _attention_fwd(qfull, kv, kr, hm, scale, tq=T, name="attn_fwd")

    shift_view = (proj, lay["r"][0], 3 * RW + TAIL)
    rl = _shift_lerp(shift_view, W["mu"], name="shift_fwd")
    rl_r, rl_k, rl_v = _view(rl, 0, RW), _view(rl, RW, RW), _view(rl, 2 * RW, RW)
    rl_tail = _view(rl, 3 * RW, TAIL)
    pre_params = [W["w0_f"], W["w0_b"], W["a0_f"], W["a0_b"], W["k_k"], W["k_a"], W["w2cat"], W["a2cat"], seg, seg_t]
    pre_fn = functools.partial(_f_rwkv_pre, RW)
    lw_f, lw_b, k_f, k_b, a_n, b_f, b_b = _rowwise(pre_fn, [rl_k, rl_tail], pre_params, [(RW, F32)] * 7, tile=2 * T,
                                                    name="rwkv_pre")
    ops_f = (rl_r, lw_f, k_f, rl_v, a_n, b_f)
    ops_b = (rl_r, lw_b, k_b, rl_v, a_n, b_b)
    y_f, st_f, y_b, st_b = _rwkv_scan_fwd(ops_f, ops_b, RW, name="scan_fwd")

    post_fn = functools.partial(_f_post, hn)
    post_rows = [y_f, y_b, rl_r, k_f, k_b, rl_v, col(proj, "z_r"), o_mla, col(proj, "z_m")]
    post_params = [W["gn_g"], W["gn_b"], W["r_k"], seg, seg_t]
    ymg, yrg = _rowwise(post_fn, post_rows, post_params, [(MW, BF16), (RW, BF16)], tile=T, name="post")
    u_m = _mm(ymg, W["w_br_mla"], name="br_mla")
    u_r = _mm(yrg, W["w_br_rwkv"], name="br_rwkv")
    merge_rows = [u_m, u_r, col(proj, "gate_m"), col(proj, "gate_r")]
    (merged,) = _rowwise(lambda *t: (_f_merge(*t),), merge_rows, [], [(D, BF16)], tile=T, name="merge")
    out = _mm(merged, W["w_out"], name="out_proj")

    def head(ob, xb, tb, g):
        yn, vjp = jax.vjp(_rms, ob, g)
        err = xb + yn - tb
        dy = err * (1.0 / D)
        d_ob, d_g = vjp(dy)
        loss = jnp.broadcast_to(0.5 * jnp.sum(err * err) * (1.0 / D), (1, LANES))
        return dy, d_ob, loss, d_g

    dy, d_out, loss, g_g_post = _rowwise(head, [out, x, target], [W["g_post"]], [(D, F32), (D, BF16)],
                                         [(1, LANES), (1, D)], tile=2 * T, name="head")
    d_merged = _mm(d_out, W["w_out"], tb=True, name="d_merged")
    g_w_out = _mm(merged, d_out, ta=True, out_dtype=BF16, name="g_w_out")

    def merge_bwd(u_m_b, u_r_b, g_m_b, g_r_b, dm):
        _, vjp = jax.vjp(_f_merge, u_m_b, u_r_b, g_m_b, g_r_b)
        du_m, du_r, dg_m, dg_r = vjp(dm)
        return du_m, du_r, jnp.concatenate([dg_m, dg_r], axis=1)

    d_u_m, d_u_r, d_proj = _rowwise(merge_bwd, merge_rows + [d_merged], [],
                                    [(D, BF16), (D, BF16), (2 * D, BF16, (None, d_in, lay["gate_m"][0]))], tile=T,
                                    name="merge_bwd")
    d_ymg = _mm(d_u_m, W["w_br_mla"], tb=True, name="d_ymg")
    d_yrg = _mm(d_u_r, W["w_br_rwkv"], tb=True, name="d_yrg")
    g_w_br_mla = _mm(ymg, d_u_m, ta=True, out_dtype=BF16, name="g_w_br_mla")
    g_w_br_rwkv = _mm(yrg, d_u_r, ta=True, out_dtype=BF16, name="g_w_br_rwkv")

    def post_bwd(*args):
        nr = len(post_rows)
        prim, dm, dr = args[:nr] + args[nr + 2:], args[nr], args[nr + 1]
        _, vjp = jax.vjp(post_fn, *prim)
        g = vjp((dm, dr))
        return g[0], g[2], g[3], g[5], g[7], jnp.concatenate([g[8], g[6]], axis=1), g[9], g[10], g[11]

    (d_y, d_r_bonus, d_k_bonus, d_v_bonus, d_o, d_proj, g_gn_g, g_gn_b, g_r_k) = _rowwise(
        post_bwd, post_rows + [d_ymg, d_yrg], post_params,
        [(RW, F32), (RW, F32), (RW, F32), (RW, F32), (MW, F32), (MW + RW, BF16, (d_proj, d_in, lay["z_m"][0]))],
        [(1, RW)] * 3, tile=T, name="post_bwd")

    dscan = _rwkv_scan_bwd(ops_f, ops_b, st_f, st_b, d_y, RW, name="scan_bwd")
    dsc = {"f": dscan[:6], "b": dscan[6:]}

    d_q_att, d_k_att, d_v_att = _attention_bwd(qfull, kv, kr, o_mla, lse, d_o, hm, scale, tq=4 * T, name="attn_bwd")

    def rope_bwd(qraw_b, kr_in, cos_b, sin_b, dq_b, dk_b, dv_b, rot_b, rot_t_b):
        _, vjp = jax.vjp(lambda q_, k_: _f_rope(hm, q_, k_, cos_b, sin_b, rot_b, rot_t_b), qraw_b, kr_in)
        dkn = jnp.concatenate([dk_b[:, hh * QHEAD:hh * QHEAD + NOPE] for hh in range(hm)], axis=1)
        dkr = dk_b[:, NOPE:QHEAD]
        for hh in range(1, hm):
            dkr = dkr + dk_b[:, hh * QHEAD + NOPE:(hh + 1) * QHEAD]
        d_qraw, d_kr_in = vjp((dq_b, dkr))
        return d_qraw, jnp.concatenate([dkn, dv_b], axis=1), d_kr_in

    d_qraw, d_kv, d_kr_in = _rowwise(rope_bwd, [qraw, kr_view, cosx, sinx, d_q_att, d_k_att, d_v_att],
                                     [rot, rot_t], [(hm * QHEAD, BF16), (2 * MW, BF16), (LANES, F32)], tile=T,
                                     name="rope_bwd")
    d_qnorm = _mm(d_qraw, W["wq_b_t"], name="d_qn")
    d_kvnorm = _mm(d_kv, W["wkv_b"], tb=True, name="d_kvn")
    g_wq_b = _mm(d_qraw, qn, ta=True, out_dtype=BF16, name="g_wq_b")
    g_wkv_b = _mm(kvn, d_kv, ta=True, out_dtype=BF16, name="g_wkv_b")

    def mla_norm_bwd(q_a, kv_a, qg, kvg, dq, dk):
        _, vjp = jax.vjp(_f_mla_norm, q_a, kv_a, qg, kvg)
        d_q_a, d_kv_a, d_qg, d_kvg = vjp((dq, dk))
        return jnp.concatenate([d_q_a, d_kv_a], axis=1), d_qg, d_kvg

    d_proj, g_q_norm, g_kv_norm = _rowwise(
        lambda q_a, kv_a, dq, dk, qg, kvg: mla_norm_bwd(q_a, kv_a, qg, kvg, dq, dk),
        [col(proj, "q_a"), col(proj, "kv_a"), d_qnorm, d_kvnorm], [W["mla_q_norm"], W["mla_kv_norm"]],
        [(QR + KVR, BF16, (d_proj, d_in, lay["q_a"][0]))], [(1, QR), (1, KVR)], tile=2 * T, name="mla_norm_bwd")

    def pre_bwd(k_b_, tail_b, dlwf, dlwb, dkf, dkb, dkbon, daf, dab, dbf, dbb, drf, drb, drbon, dvf, dvb, dvbon,
                dkr, *params):
        w2, a2 = params[6], params[7]
        nt, tn = (((1,), (1,)), ((), ())), (((0,), (0,)), ((), ()))
        split = w2.shape[0]
        th = jnp.tanh(tail_b[:, :split])
        th_b, tail_h = th.astype(BF16), tail_b[:, split:].astype(BF16)
        zw = jnp.dot(th_b, w2, preferred_element_type=F32)
        za = jnp.dot(tail_h, a2, preferred_element_type=F32)
        _, vjp = jax.vjp(functools.partial(_f_rwkv_core, RW), k_b_, zw, za, *params[:6], params[8], params[9])
        g = vjp((dlwf, dlwb, dkf + dkbon, dkb + dkbon, daf + dab, dbf, dbb))
        d_zw, d_za = g[1].astype(BF16), g[2].astype(BF16)
        d_tail = (jnp.concatenate([lax.dot_general(d_zw, w2, nt, preferred_element_type=F32) * (1.0 - th * th),
                                   lax.dot_general(d_za, a2, nt, preferred_element_type=F32)], axis=1)
                  + jnp.concatenate([dkr, jnp.zeros((dkr.shape[0], TAIL - LANES), F32)], axis=1))
        g_w2 = lax.dot_general(th_b, d_zw, tn, preferred_element_type=F32)
        g_a2 = lax.dot_general(tail_h, d_za, tn, preferred_element_type=F32)
        d_rl = jnp.concatenate([drf + drb + drbon, g[0], dvf + dvb + dvbon, d_tail], axis=1)
        return (d_rl,) + tuple(g[3:9]) + (g_w2, g_a2)

    f_, b_ = dsc["f"], dsc["b"]
    pre_bwd_rows = [rl_k, rl_tail, f_[1], b_[1], f_[2], b_[2], d_k_bonus, f_[4], b_[4], f_[5], b_[5],
                    f_[0], b_[0], d_r_bonus, f_[3], b_[3], d_v_bonus, d_kr_in]
    (d_rl, g_w0_f, g_w0_b, g_a0_f, g_a0_b, g_k_k, g_k_a, g_w2cat, g_a2cat) = _rowwise(
        pre_bwd, pre_bwd_rows, pre_params, [(3 * RW + TAIL, F32)],
        [(1, RW)] * 6 + [W["w2cat"].shape, W["a2cat"].shape], tile=T // 2, name="rwkv_pre_bwd")
    d_proj, g_mu = _shift_lerp(shift_view, W["mu"], d_rl, (d_proj, lay["r"][0]), name="shift_bwd")
    small = dict(wq_b=g_wq_b, wkv_b=g_wkv_b, w2cat=g_w2cat, a2cat=g_a2cat, w_br_mla=g_w_br_mla,
                 w_br_rwkv=g_w_br_rwkv, w_out=g_w_out)
    if exchange is None:
        received = None
        g_w_in = _mm(d_proj, h, ta=True, out_dtype=BF16, tn_cap=1024, name="g_w_in")
        d_h = _mm(d_proj, W["w_in_t"], tn_cap=1024, name="d_h")
    else:
        slabs = _restore_rest(small, dims)
        slabs = [slabs[n] for n in _MATS[1:]]
        g_w_in, *got = _mm(d_proj, h, ta=True, out_dtype=BF16, tn_cap=1024, ride=_sibling_swap_plan(slabs),
                           name="g_w_in")
        sums = [_pair_add(exchange[1], s, t, name="pair_add_" + n) for n, s, t in zip(_MATS[1:], slabs, got)]
        g_w_in = _restore_w_in(g_w_in, dims)
        d_h, *received = _mm(d_proj, W["w_in_t"], tn_cap=1024, name="d_h",
                             ride=_join_plans(_chip_exchange_plan(sums), _sibling_swap_plan([g_w_in])))
        small = {}

    def pre_norm_bwd(xb, dyb, dhb, g):
        _, vjp = jax.vjp(_rms, xb, g)
        dx, dg = vjp(dhb)
        return dyb + dx, dg

    grad_x, g_g_pre = _rowwise(pre_norm_bwd, [x, dy, d_h], [W["g_pre"]], [(D, F32)], [(1, D)], tile=2 * T,
                               name="pre_norm_bwd")

    grads = dict(g_pre=g_g_pre, w_in=g_w_in, mla_q_norm=g_q_norm, mla_kv_norm=g_kv_norm, mu=g_mu, w0_f=g_w0_f,
                 w0_b=g_w0_b, a0_f=g_a0_f, a0_b=g_a0_b, k_k=g_k_k, k_a=g_k_a, r_k=g_r_k, gn_g=g_gn_g, gn_b=g_gn_b,
                 g_post=g_g_post, **small)
    return loss[0, 0], grad_x, grads, received


_MATS = ["w_in", "mla_wq_b", "mla_wkv_b", "rwkv_w2_f", "rwkv_w2_b", "rwkv_a2_f", "rwkv_a2_b", "w_br_mla",
         "w_br_rwkv", "w_out"]
_ROW_SHARDED = ("w_out",)
_TRANSPOSED = ("w_in", "mla_wq_b")
_VECS = ["g_pre", "mla_q_norm", "mla_kv_norm", "rwkv_mu", "rwkv_w0_f", "rwkv_w0_b", "rwkv_a0_f", "rwkv_a0_b",
         "rwkv_k_k", "rwkv_k_a", "rwkv_r_k", "rwkv_gn_g", "rwkv_gn_b", "g_post"]
_WEIGHTS = ["g_pre", "w_in", "mla_q_norm", "mla_wq_b", "mla_kv_norm", "mla_wkv_b", "rwkv_mu", "rwkv_w0_f",
            "rwkv_w2_f", "rwkv_w0_b", "rwkv_w2_b", "rwkv_a0_f", "rwkv_a2_f", "rwkv_a0_b", "rwkv_a2_b", "rwkv_k_k",
            "rwkv_k_a", "rwkv_r_k", "rwkv_gn_g", "rwkv_gn_b", "w_br_mla", "w_br_rwkv", "w_out", "g_post"]

def _direct_gather_plan(src):
    def phases(src_refs, out_refs, sem_refs):
        (src_ref,), (out_ref,), sems, local_sem = src_refs, out_refs, sem_refs[:2], sem_refs[2]
        x, y, c = lax.axis_index("x"), lax.axis_index("y"), lax.axis_index("c")
        me = 4 * x + 2 * y + c
        flip = lambda v, bit: (1 - v) if bit else v
        peers = [(flip(x, d & 4), flip(y, d & 2), flip(c, d & 1)) for d in range(1, N_DEV)]
        own = lambda: pltpu.make_async_copy(src_ref, out_ref.at[me], local_sem)
        send = lambda d: _remote(src_ref, out_ref.at[me], sems, d, peers[d])

        def first():
            own().start()
            for d in range(N_DEV - 1):
                send(d).start()

        def last():
            for d, (px, py, pc) in enumerate(peers):
                blk = out_ref.at[4 * px + 2 * py + pc]
                _remote(blk, blk, sems, d, (x, y, c)).wait_recv()
            for d in range(N_DEV - 1):
                send(d).wait_send()
            own().wait()

        return first, (lambda: None), last

    return [src], [jax.ShapeDtypeStruct((N_DEV,) + src.shape, src.dtype)], [(N_DEV - 1,), (N_DEV - 1,), ()], phases


def _remote(src, dst, sems, key, to):
    send_sems, recv_sems = sems
    return pltpu.make_async_remote_copy(src_ref=src, dst_ref=dst, send_sem=send_sems.at[key], recv_sem=recv_sems.at[key],
                                        device_id=to, device_id_type=pl.DeviceIdType.MESH)


def _run_exchange(plan, *, name):
    srcs, out_shapes, sem_shapes, phases = plan
    n, m = len(srcs), len(out_shapes)

    def body(*refs):
        for phase in phases(refs[:n], refs[n:n + m], refs[n + m:]):
            phase()

    return pl.pallas_call(
        body, name=name, out_shape=out_shapes,
        in_specs=[pl.BlockSpec(memory_space=pl.ANY)] * n, out_specs=[pl.BlockSpec(memory_space=pl.ANY)] * m,
        scratch_shapes=[pltpu.SemaphoreType.DMA(s) for s in sem_shapes],
    )(*srcs)


def _join_plans(p, q):
    (srcs_p, outs_p, sems_p, phases_p), (srcs_q, outs_q, sems_q, phases_q) = p, q

    def phases(src_refs, out_refs, sem_refs):
        a = phases_p(src_refs[:len(srcs_p)], out_refs[:len(outs_p)], sem_refs[:len(sems_p)])
        b = phases_q(src_refs[len(srcs_p):], out_refs[len(outs_p):], sem_refs[len(sems_p):])

        def both(fa, fb):
            def run():
                fa()
                fb()
            return run

        return tuple(both(fa, fb) for fa, fb in zip(a, b))

    return list(srcs_p) + list(srcs_q), list(outs_p) + list(outs_q), list(sems_p) + list(sems_q), phases


def _gather_plan(srcs):
    n = len(srcs)

    def phases(src_refs, out_refs, sem_refs):
        sems, local_sems = sem_refs[:2], sem_refs[2]
        x, y, c = lax.axis_index("x"), lax.axis_index("y"), lax.axis_index("c")
        idx = lambda px, py, pc: 4 * px + 2 * py + pc
        me, sibling = (x, y, c), (x, y, 1 - c)
        chips = [(1 - x, y), (x, 1 - y), (1 - x, 1 - y)]
        own = lambda a: pltpu.make_async_copy(src_refs[a], out_refs[a].at[idx(*me)], local_sems.at[a])
        to_sibling = lambda a: _remote(src_refs[a], out_refs[a].at[idx(*me)], sems, (0, a), sibling)
        to_chip = lambda a, j: _remote(src_refs[a], out_refs[a].at[idx(*me)], sems, (1 + j, a), (*chips[j], c))
        landed = lambda a, j: out_refs[a].at[idx(*chips[j], c)]
        passed_on = lambda a, j: _remote(landed(a, j), landed(a, j), sems, (4 + j, a), sibling)

        def first():
            for a in range(n):
                own(a).start()
                to_sibling(a).start()
                for j in range(3):
                    to_chip(a, j).start()

        def middle():
            for j in range(3):
                for a in range(n):
                    _remote(landed(a, j), landed(a, j), sems, (1 + j, a), me).wait_recv()
                    passed_on(a, j).start()

        def last():
            for a in range(n):
                blk = out_refs[a].at[idx(*sibling)]
                _remote(blk, blk, sems, (0, a), me).wait_recv()
                for j in range(3):
                    blk = out_refs[a].at[idx(*chips[j], 1 - c)]
                    _remote(blk, blk, sems, (4 + j, a), me).wait_recv()
            for a in range(n):
                to_sibling(a).wait_send()
                for j in range(3):
                    to_chip(a, j).wait_send()
                    passed_on(a, j).wait_send()
                own(a).wait()

        return first, middle, last

    return srcs, [jax.ShapeDtypeStruct((N_DEV,) + s.shape, s.dtype) for s in srcs], [(7, n), (7, n), (n,)], phases


def _sibling_swap_plan(srcs):
    n = len(srcs)

    def phases(src_refs, out_refs, sems):
        x, y, c = lax.axis_index("x"), lax.axis_index("y"), lax.axis_index("c")
        copies = lambda: [_remote(src_refs[a].at[2 * q + 1 - c], out_refs[a].at[q], sems, (q, a), (x, y, 1 - c))
                          for a in range(n) for q in range(4)]

        def first():
            for cp in copies():
                cp.start()

        def last():
            for cp in copies():
                cp.wait()

        return first, (lambda: None), last

    return srcs, [jax.ShapeDtypeStruct((4,) + s.shape[1:], s.dtype) for s in srcs], [(4, n), (4, n)], phases


def _chip_exchange_plan(srcs):
    n = len(srcs)

    def phases(src_refs, out_refs, sem_refs):
        sems, local_sems = sem_refs[:2], sem_refs[2]
        x, y, c = lax.axis_index("x"), lax.axis_index("y"), lax.axis_index("c")
        mine = 2 * x + y
        chips = [(1 - x, y), (x, 1 - y), (1 - x, 1 - y)]
        own = lambda a: pltpu.make_async_copy(src_refs[a].at[mine], out_refs[a].at[mine], local_sems.at[a])
        send = lambda a, j: _remote(src_refs[a].at[2 * chips[j][0] + chips[j][1]], out_refs[a].at[mine], sems, (j, a),
                                    (*chips[j], c))

        def first():
            for a in range(n):
                own(a).start()
                for j in range(3):
                    send(a, j).start()

        def last():
            for j in range(3):
                for a in range(n):
                    blk = out_refs[a].at[2 * chips[j][0] + chips[j][1]]
                    _remote(blk, blk, sems, (j, a), (x, y, c)).wait_recv()
            for a in range(n):
                for j in range(3):
                    send(a, j).wait_send()
                own(a).wait()

        return first, (lambda: None), last

    return srcs, [jax.ShapeDtypeStruct(s.shape, s.dtype) for s in srcs], [(3, n), (3, n), (n,)], phases


def _pair_add(core, g, got, *, name):
    q, r, c = got.shape
    tr, tc = _tile2d(r, c, cap=1024)

    def body(core_ref, a_ref, b_ref, o_ref):
        o_ref[...] = (a_ref[...].astype(F32) + b_ref[...].astype(F32)).astype(BF16)

    blk = pl.BlockSpec((1, tr, tc), lambda i, j, k, core_ref: (i, j, k))
    mine = pl.BlockSpec((1, tr, tc), lambda i, j, k, core_ref: (2 * i + core_ref[0], j, k))
    return pl.pallas_call(
        body, name=name, out_shape=jax.ShapeDtypeStruct(got.shape, BF16),
        grid_spec=pltpu.PrefetchScalarGridSpec(num_scalar_prefetch=1, grid=(q, r // tr, c // tc),
                                               in_specs=[mine, blk], out_specs=blk),
        compiler_params=_cparams(("parallel", "parallel", "parallel")))(core, g, got)


def _adamw(recv, w, m, v, *, name):
    r, c = w.shape
    n_terms = recv.shape[0]
    tr, tc = _tile2d(r, c)

    def body(g_ref, w_ref, m_ref, v_ref, go_ref, d_ref, mo_ref, vo_ref):
        g = g_ref[0].astype(F32)
        for k in range(1, n_terms):
            g = g + g_ref[k].astype(F32)
        m_new = ADAM_B1 * m_ref[...] + (1.0 - ADAM_B1) * g
        v_new = ADAM_B2 * v_ref[...] + (1.0 - ADAM_B2) * (g * g)
        m_hat = m_new / (1.0 - ADAM_B1 ** ADAM_STEP)
        v_hat = v_new / (1.0 - ADAM_B2 ** ADAM_STEP)
        go_ref[...] = g
        d_ref[...] = -ADAM_LR * (m_hat / (jnp.sqrt(v_hat) + ADAM_EPS) + ADAM_WD * w_ref[...])
        mo_ref[...] = m_new
        vo_ref[...] = v_new

    blk = pl.BlockSpec((tr, tc), lambda i, j: (i, j))
    return pl.pallas_call(
        body, name=name, grid=(r // tr, c // tc),
        in_specs=[pl.BlockSpec((n_terms, tr, tc), lambda i, j: (0, i, j)), blk, blk, blk], out_specs=[blk] * 4,
        out_shape=[jax.ShapeDtypeStruct((r, c), F32)] * 4, compiler_params=_cparams(("parallel", "parallel")),
    )(recv, w, m, v)


def _tile2d(r, c, cap=256):
    if r <= cap:
        return r, c
    for t in range(cap - cap % BF16_ROWS, 0, -BF16_ROWS):
        if r % t == 0:
            return t, c
    return r, _pick(c, cap)


def _pack(pieces):
    total = sum(p.shape[0] for p in pieces)
    pad = (-total) % (8 * LANES)
    flat = jnp.concatenate(list(pieces) + [jnp.zeros((pad,), F32)])
    return flat.reshape(-1, LANES)


def _unpack(flat, sizes):
    flat = flat.reshape(-1)
    out, o = [], 0
    for n in sizes:
        out.append(flat[o:o + n])
        o += n
    return out


def _prepare_weights(full, vec, dims):
    rest = {n: t for n, t in full.items() if n != "w_in"}
    return {"w_in_t": _prepare_w_in(full["w_in"], dims), **_prepare_rest(rest, dims), **_prepare_vectors(vec, dims)}


def _prepare_w_in(slabs, dims):
    D = dims["D"]
    flat = slabs.reshape(-1, D)
    parts, pos = [], 0
    for orig_off, width, perm_off in sorted(dims["segs"], key=lambda t: t[2]):
        if perm_off > pos:
            parts.append(jnp.zeros((perm_off - pos, D), BF16))
        parts.append(flat[orig_off:orig_off + width])
        pos = perm_off + width
    if dims["d_in_perm"] > pos:
        parts.append(jnp.zeros((dims["d_in_perm"] - pos, D), BF16))
    return jnp.concatenate(parts, axis=0)


def _prepare_rest(full, dims):
    hm, hr, hn, rank = dims["hm"], dims["hr"], dims["hn"], dims["rank"]
    QR, KVR = dims["QR"], dims["KVR"]
    RW, TAIL = hr * hn, dims["TAIL"]
    full = {n: (t.reshape(-1, t.shape[2]) if n in _ROW_SHARDED + _TRANSPOSED
                else t.transpose(1, 0, 2).reshape(t.shape[1], -1)) for n, t in full.items()}
    wq = full["mla_wq_b"].reshape(hm, NOPE + ROPE, QR)
    wq = jnp.concatenate([wq, jnp.zeros((hm, QHEAD - NOPE - ROPE, QR), BF16)], axis=1).reshape(hm * QHEAD, QR)
    wkv = full["mla_wkv_b"].reshape(KVR, hm, 2, NOPE).transpose(0, 2, 1, 3).reshape(KVR, 2 * hm * NOPE)
    z = lambda rows: jnp.zeros((rows, RW), BF16)
    f = lambda nme: full[nme]
    split = ROPE + 2 * rank
    assert split % LANES == 0, split
    w2cat = jnp.concatenate([
        jnp.concatenate([z(ROPE), f("rwkv_w2_f"), z(rank)], axis=0),
        jnp.concatenate([z(ROPE + rank), f("rwkv_w2_b")], axis=0)], axis=1)
    a2cat = jnp.concatenate([
        jnp.concatenate([f("rwkv_a2_f"), z(TAIL - split - rank)], axis=0),
        jnp.concatenate([z(rank), f("rwkv_a2_b"), z(TAIL - split - 2 * rank)], axis=0)], axis=1)
    return dict(wq_b_t=wq, wkv_b=wkv, w2cat=w2cat, a2cat=a2cat, w_br_mla=full["w_br_mla"],
                w_br_rwkv=full["w_br_rwkv"], w_out=full["w_out"])


def _prepare_vectors(vec, dims):
    rank, RW, TAIL = dims["rank"], dims["hr"] * dims["hn"], dims["TAIL"]
    mu = vec["rwkv_mu"]
    mu_p = jnp.concatenate([mu[:3 * RW], jnp.zeros((ROPE,), F32), mu[3 * RW:],
                            jnp.zeros((TAIL - ROPE - 4 * rank,), F32)])
    row = lambda t: t.reshape(1, -1)
    return dict(
        mu=row(mu_p), g_pre=row(vec["g_pre"]), g_post=row(vec["g_post"]), mla_q_norm=row(vec["mla_q_norm"]),
        mla_kv_norm=row(vec["mla_kv_norm"]), w0_f=row(vec["rwkv_w0_f"]), w0_b=row(vec["rwkv_w0_b"]),
        a0_f=row(vec["rwkv_a0_f"]), a0_b=row(vec["rwkv_a0_b"]), k_k=row(vec["rwkv_k_k"]), k_a=row(vec["rwkv_k_a"]),
        r_k=row(vec["rwkv_r_k"]), gn_g=row(vec["rwkv_gn_g"]), gn_b=row(vec["rwkv_gn_b"]))


def _restore_grads(g, dims):
    return {"w_in": _restore_w_in(g["w_in"], dims), **_restore_rest(g, dims), **_restore_vectors(g, dims)}


def _restore_w_in(gw, dims):
    parts = [gw[perm_off:perm_off + width] for _, width, perm_off in sorted(dims["segs"])]
    return jnp.concatenate(parts, axis=0).reshape(N_DEV, dims["d_in"] // N_DEV, gw.shape[1])


def _restore_rest(g, dims):
    hm, hr, hn, rank = dims["hm"], dims["hr"], dims["hn"], dims["rank"]
    QR, KVR, RW = dims["QR"], dims["KVR"], hr * hn
    wq = g["wq_b"].reshape(hm, QHEAD, QR)[:, :NOPE + ROPE].reshape(N_DEV, -1, QR)
    wkv = g["wkv_b"].reshape(KVR, 2, hm, NOPE).transpose(0, 2, 1, 3).reshape(KVR, 2 * hm * NOPE)
    lo = lambda t, first, half: t[first:first + rank, half * RW:(half + 1) * RW].astype(BF16)
    cols = lambda t: t.reshape(t.shape[0], N_DEV, -1).transpose(1, 0, 2)
    return dict(
        mla_wq_b=wq, mla_wkv_b=cols(wkv), rwkv_w2_f=cols(lo(g["w2cat"], ROPE, 0)),
        rwkv_w2_b=cols(lo(g["w2cat"], ROPE + rank, 1)), rwkv_a2_f=cols(lo(g["a2cat"], 0, 0)),
        rwkv_a2_b=cols(lo(g["a2cat"], rank, 1)), w_br_mla=cols(g["w_br_mla"]), w_br_rwkv=cols(g["w_br_rwkv"]),
        w_out=g["w_out"].reshape(N_DEV, -1, g["w_out"].shape[1]))


def _restore_vectors(g, dims):
    rank, RW = dims["rank"], dims["hr"] * dims["hn"]
    mu = g["mu"][0]
    out = dict(
        rwkv_mu=jnp.concatenate([mu[:3 * RW], mu[3 * RW + ROPE:3 * RW + ROPE + 4 * rank]]),
        g_pre=g["g_pre"][0], g_post=g["g_post"][0], mla_q_norm=g["mla_q_norm"][0], mla_kv_norm=g["mla_kv_norm"][0],
        rwkv_w0_f=g["w0_f"][0], rwkv_w0_b=g["w0_b"][0], rwkv_a0_f=g["a0_f"][0], rwkv_a0_b=g["a0_b"][0],
        rwkv_k_k=g["k_k"][0], rwkv_k_a=g["k_a"][0], rwkv_r_k=g["r_k"][0], rwkv_gn_g=g["gn_g"][0],
        rwkv_gn_b=g["gn_b"][0])
    return out


def _dims(inp):
    D = inp["x"].shape[-1]
    QR, KVR = inp["mla_q_norm"].shape[0], inp["mla_kv_norm"].shape[0]
    hm = inp["mla_wq_b"].shape[1] * N_DEV // (NOPE + ROPE)
    hr, hn = inp["rwkv_r_k"].shape
    rank = inp["rwkv_w2_f"].shape[0]
    MW, RW = hm * VDIM, hr * hn
    TAIL = -(-(ROPE + 4 * rank) // LANES) * LANES
    orig, o = {}, 0
    for nme, w in (("q_a", QR), ("kv_a", KVR), ("k_rope", ROPE), ("rkv", 3 * RW), ("lora", 4 * rank), ("z_m", MW),
                   ("z_r", RW), ("gate_m", D), ("gate_r", D)):
        orig[nme] = (o, w)
        o += w
    assert o == inp["w_in"].shape[1] * N_DEV
    lay, d_in_perm = _layout(D, MW, RW, TAIL, QR, KVR)
    perm_off = dict(q_a=lay["q_a"][0], kv_a=lay["kv_a"][0], k_rope=lay["tail"][0], rkv=lay["r"][0],
                    lora=lay["tail"][0] + ROPE, z_m=lay["z_m"][0], z_r=lay["z_r"][0], gate_m=lay["gate_m"][0],
                    gate_r=lay["gate_r"][0])
    segs = [(orig[nme][0], orig[nme][1], perm_off[nme]) for nme in orig]
    return dict(D=D, QR=QR, KVR=KVR, hm=hm, hr=hr, hn=hn, rank=rank, TAIL=TAIL, segs=segs, d_in=o,
                d_in_perm=d_in_perm)


def kernel(x, g_pre, w_in, mla_q_norm, mla_wq_b, mla_kv_norm, mla_wkv_b, rwkv_mu, rwkv_w0_f, rwkv_w2_f, rwkv_w0_b, rwkv_w2_b, rwkv_a0_f, rwkv_a2_f, rwkv_a0_b, rwkv_a2_b, rwkv_k_k, rwkv_k_a, rwkv_r_k, rwkv_gn_g, rwkv_gn_b, w_br_mla, w_br_rwkv, w_out, g_post, loss_target, m_g_pre, m_w_in, m_mla_q_norm, m_mla_wq_b, m_mla_kv_norm, m_mla_wkv_b, m_rwkv_mu, m_rwkv_w0_f, m_rwkv_w2_f, m_rwkv_w0_b, m_rwkv_w2_b, m_rwkv_a0_f, m_rwkv_a2_f, m_rwkv_a0_b, m_rwkv_a2_b, m_rwkv_k_k, m_rwkv_k_a, m_rwkv_r_k, m_rwkv_gn_g, m_rwkv_gn_b, m_w_br_mla, m_w_br_rwkv, m_w_out, m_g_post, v_g_pre, v_w_in, v_mla_q_norm, v_mla_wq_b, v_mla_kv_norm, v_mla_wkv_b, v_rwkv_mu, v_rwkv_w0_f, v_rwkv_w2_f, v_rwkv_w0_b, v_rwkv_w2_b, v_rwkv_a0_f, v_rwkv_a2_f, v_rwkv_a0_b, v_rwkv_a2_b, v_rwkv_k_k, v_rwkv_k_a, v_rwkv_r_k, v_rwkv_gn_g, v_rwkv_gn_b, v_w_br_mla, v_w_br_rwkv, v_w_out, v_g_post):
    inp = dict(locals())
    dims = _dims(inp)
    stored = lambda t, n: t.T if n in _TRANSPOSED else t
    assert _MATS[0] == "w_in"
    shards = [stored(inp[n], n).astype(BF16) for n in _MATS]
    core = lax.axis_index("c").astype(jnp.int32).reshape(1)
    (w_in_slabs,) = _run_exchange(_gather_plan(shards[:1]), name="gather_w_in")
    W = {"w_in_t": _prepare_w_in(w_in_slabs, dims), **_prepare_vectors({n: inp[n] for n in _VECS}, dims)}
    loss, grad_x, g, recv_rest = _local_grads(x[0], loss_target[0], W, dims, exchange=(shards[1:], core))

    new = {}
    *recv_rest, got = recv_rest
    g_w_in, g = g["w_in"], _restore_vectors(g, dims)
    vsizes = [inp[n].size for n in _VECS] + [1]
    vflat = lambda prefix, src, last: _pack([src[prefix + n].reshape(-1) for n in _VECS] + [last])
    one = jnp.zeros((1,), F32)
    recv_w_in, vrecv = _run_exchange(
        _join_plans(_chip_exchange_plan([_pair_add(core, g_w_in, got, name="pair_add_w_in")]),
                    _direct_gather_plan(vflat("", g, loss.reshape(1)))), name="scatter_w_in")
    for n, t in zip(_MATS, [recv_w_in] + recv_rest):
        out = _adamw(t, stored(inp[n], n), stored(inp["m_" + n], n), stored(inp["v_" + n], n), name="adamw_" + n)
        new[n] = [stored(o, n) for o in out]

    vout = _adamw(vrecv, vflat("", inp, one), vflat("m_", inp, one), vflat("v_", inp, one), name="adamw_vectors")
    vparts = [_unpack(t, vsizes) for t in vout]
    for i, n in enumerate(_VECS):
        new[n] = [vp[i].reshape(inp[n].shape) for vp in vparts]
    loss = vparts[0][-1].reshape(())

    outs = [loss, grad_x[None]]
    for k in range(4):
        outs += [new[n][k] for n in _WEIGHTS]
    return tuple(outs)
```

```python
import functools
import math

import jax
import jax.numpy as jnp
from jax import lax
from jax.experimental import pallas as pl
from jax.experimental.pallas import tpu as pltpu

F32 = jnp.float32
BF16 = jnp.bfloat16

N_DEV = 8
LANES = 128
BF16_ROWS = 16
NOPE, ROPE, VDIM = 128, 64, 128
QHEAD = 256
ROPE_THETA = 10000.0
NORM_EPS = 1e-6
GN_EPS = 64e-5
CHUNK = 64
SUB = 16
VMEM_LIMIT = 56 * 1024 * 1024

ADAM_LR, ADAM_B1, ADAM_B2, ADAM_EPS, ADAM_WD, ADAM_STEP = 0.001, 0.9, 0.999, 1e-08, 0.01, 10


def _cparams(sem):
    return pltpu.CompilerParams(dimension_semantics=sem, vmem_limit_bytes=VMEM_LIMIT)


def _pick(n, cap):
    if n <= cap:
        return n
    for t in range(cap - cap % LANES, 0, -LANES):
        if n % t == 0:
            return t
    raise ValueError(f"no tile for {n} under {cap}")


def _mm(a, b, *, ta=False, tb=False, out_dtype=F32, name, tm_cap=1024, tn_cap=512, tk_cap=2048, ride=None):
    K, M = a.shape if ta else a.shape[::-1]
    N = b.shape[0] if tb else b.shape[1]
    assert (b.shape[1] if tb else b.shape[0]) == K, (a.shape, b.shape, ta, tb)
    tm, tn, tk = _pick(M, tm_cap), _pick(N, tn_cap), _pick(K, tk_cap)
    nj, nk = N // tn, K // tk
    steps = (M // tm) * nj * nk
    dn = (((0 if ta else 1,), (1 if tb else 0,)), ((), ()))
    srcs, extra_shapes, sem_shapes, phases = ride if ride else ((), (), (), None)
    n_src, n_extra = len(srcs), len(extra_shapes)

    def body(*refs):
        a_ref, b_ref, o_ref = refs[0], refs[1], refs[2 + n_src]
        acc_ref = refs[3 + n_src + n_extra]
        k = pl.program_id(2)
        if ride:
            step = (pl.program_id(0) * nj + pl.program_id(1)) * nk + k
            first, middle, last = phases(refs[2:2 + n_src], refs[3 + n_src:3 + n_src + n_extra],
                                         refs[4 + n_src + n_extra:])
            pl.when(step == 0)(first)
            pl.when(step == (steps * 15) // 16)(middle)
        p = lax.dot_general(a_ref[...], b_ref[...], dn, preferred_element_type=F32)

        @pl.when(k == 0)
        def _():
            acc_ref[...] = p

        @pl.when(k > 0)
        def _():
            acc_ref[...] += p

        @pl.when(k == nk - 1)
        def _():
            o_ref[...] = acc_ref[...].astype(out_dtype)

        if ride:
            pl.when(step == steps - 1)(last)

    a_spec = pl.BlockSpec((tk, tm), lambda i, j, k: (k, i)) if ta else pl.BlockSpec((tm, tk), lambda i, j, k: (i, k))
    b_spec = pl.BlockSpec((tn, tk), lambda i, j, k: (j, k)) if tb else pl.BlockSpec((tk, tn), lambda i, j, k: (k, j))
    hbm = pl.BlockSpec(memory_space=pl.ANY)
    out = pl.pallas_call(
        body, name=name, grid=(M // tm, nj, nk),
        in_specs=[a_spec, b_spec] + [hbm] * n_src,
        out_specs=[pl.BlockSpec((tm, tn), lambda i, j, k: (i, j))] + [hbm] * n_extra,
        out_shape=[jax.ShapeDtypeStruct((M, N), out_dtype)] + list(extra_shapes),
        scratch_shapes=[pltpu.VMEM((tm, tn), F32)] + [pltpu.SemaphoreType.DMA(s) for s in sem_shapes],
        compiler_params=_cparams(("arbitrary",) * 3 if ride else ("parallel", "parallel", "arbitrary")),
    )(a, b, *srcs)
    return out if ride else out[0]


def _view(arr, off, width):
    assert off % width == 0, (off, width)
    return (arr, off // width, width)


def _rowwise(fn, rows, params, out_rows, out_accs=(), *, tile, name):
    rows = [r if isinstance(r, tuple) else (r, 0, r.shape[1]) for r in rows]
    S = rows[0][0].shape[0]
    T = min(tile, S)
    assert S % T == 0
    n_rows, n_par, n_out = len(rows), len(params), len(out_rows)
    into = [o[2] if len(o) == 3 else None for o in out_rows]
    carried = [t[0] for t in into if t is not None and t[0] is not None]

    def body(*refs):
        ins = [r[...] for r in refs[:n_rows + n_par]]
        outs = fn(*ins)
        out_refs = refs[n_rows + n_par + len(carried):]
        for o_ref, val in zip(out_refs[:n_out], outs[:n_out]):
            o_ref[...] = val.astype(o_ref.dtype)
        i = pl.program_id(0)
        for o_ref, val in zip(out_refs[n_out:], outs[n_out:]):
            @pl.when(i == 0)
            def _(o_ref=o_ref, val=val):
                o_ref[...] = val

            @pl.when(i > 0)
            def _(o_ref=o_ref, val=val):
                o_ref[...] += val

    in_specs = [pl.BlockSpec((T, w), functools.partial(lambda i, cb: (i, cb), cb=cb)) for _, cb, w in rows]
    in_specs += [pl.BlockSpec(p.shape, lambda i: (0, 0)) for p in params]
    in_specs += [pl.BlockSpec(memory_space=pl.ANY)] * len(carried)
    out_specs, out_shape, aliases = [], [], {}
    for k, (o, t) in enumerate(zip(out_rows, into)):
        w, dt = o[0], o[1]
        if t is None:
            out_specs.append(pl.BlockSpec((T, w), lambda i: (i, 0)))
            out_shape.append(jax.ShapeDtypeStruct((S, w), dt))
            continue
        buf, total, first = t
        assert first % w == 0
        out_specs.append(pl.BlockSpec((T, w), functools.partial(lambda i, cb: (i, cb), cb=first // w)))
        out_shape.append(jax.ShapeDtypeStruct((S, total), dt))
        if buf is not None:
            aliases[n_rows + n_par + len(aliases)] = k
    out_specs += [pl.BlockSpec(s, lambda i: (0, 0)) for s in out_accs]
    out_shape += [jax.ShapeDtypeStruct(s, F32) for s in out_accs]
    return pl.pallas_call(
        body, name=name, grid=(S // T,), in_specs=in_specs, out_specs=out_specs, out_shape=out_shape,
        input_output_aliases=aliases, compiler_params=_cparams(("arbitrary",)),
    )(*[r[0] for r in rows], *params, *carried)


def _mm_sel(x, sel2):
    hi = x.astype(BF16)
    lo = (x - hi.astype(F32)).astype(BF16)
    return jnp.dot(jnp.concatenate([hi, lo], axis=1), sel2, preferred_element_type=F32)


@jax.custom_vjp
def _sel(x, sel, sel_t):
    return _mm_sel(x, sel)


def _sel_fwd(x, sel, sel_t):
    return _mm_sel(x, sel), (sel, sel_t)


def _sel_bwd(res, ct):
    sel, sel_t = res
    return _mm_sel(ct, sel_t), jnp.zeros_like(sel), jnp.zeros_like(sel_t)


_sel.defvjp(_sel_fwd, _sel_bwd)


def _rms(x, g):
    return x * lax.rsqrt(jnp.mean(x * x, axis=-1, keepdims=True) + NORM_EPS) * g


def _sigmoid(x):
    return 0.5 * jnp.tanh(0.5 * x) + 0.5


def _silu(x):
    return x * _sigmoid(x)


def _softplus(x):
    return jnp.maximum(x, 0.0) + jnp.log(1.0 + jnp.exp(-jnp.abs(x)))


def _f_mla_norm(q_a, kv_a, qg, kvg):
    return _rms(q_a, qg), _rms(kv_a, kvg)


def _f_rope(hm, qraw, kr_in, cosx, sinx, rot, rot_t):
    def rope(t):
        return t * cosx + _sel(t, rot, rot_t) * sinx
    parts = []
    for h in range(hm):
        parts.append(qraw[:, h * QHEAD:h * QHEAD + NOPE])
        parts.append(rope(qraw[:, h * QHEAD + NOPE:(h + 1) * QHEAD]))
    return jnp.concatenate(parts, axis=1), rope(kr_in)


def _f_rwkv_pre(rw, k, tail, w0f, w0b, a0f, a0b, k_k, k_a, w2cat, a2cat, seg, seg_t):
    split = w2cat.shape[0]
    zw = jnp.dot(jnp.tanh(tail[:, :split]).astype(BF16), w2cat, preferred_element_type=F32)
    za = jnp.dot(tail[:, split:].astype(BF16), a2cat, preferred_element_type=F32)
    return _f_rwkv_core(rw, k, zw, za, w0f, w0b, a0f, a0b, k_k, k_a, seg, seg_t)


def _f_rwkv_core(rw, k, zw, za, w0f, w0b, a0f, a0b, k_k, k_a, seg, seg_t):
    lw_f = -jnp.exp(-_softplus(-(w0f + zw[:, :rw])) - 0.5)
    lw_b = -jnp.exp(-_softplus(-(w0b + zw[:, rw:])) - 0.5)
    a_f = _sigmoid(a0f + za[:, :rw])
    a_b = _sigmoid(a0b + za[:, rw:])
    kk = k * k_k
    nrm = jnp.sqrt(_sel(_sel(kk * kk, seg, seg_t), seg_t, seg))
    kk = kk / jnp.maximum(nrm, 1e-12)
    k_f = k * (1.0 + (a_f - 1.0) * k_a)
    k_b = k * (1.0 + (a_b - 1.0) * k_a)
    return lw_f, lw_b, k_f, k_b, -kk, kk * a_f, kk * a_b


def _f_post(hn, y_f, y_b, r, k_f, k_b, v, z_r, o_mla, z_m, gn_g, gn_b, r_k, seg, seg_t):
    segsum = lambda t: _sel(_sel(t, seg, seg_t), seg_t, seg)
    y = y_f + y_b
    mu = segsum(y) * (1.0 / hn)
    yc = y - mu
    var = segsum(yc * yc) * (1.0 / hn)
    yn = yc * lax.rsqrt(var + GN_EPS) * gn_g + gn_b
    bonus = segsum(r * (k_f + k_b) * r_k) * v
    return o_mla * _silu(z_m), (yn + bonus) * _silu(z_r)


def _f_merge(u_m, u_r, g_m, g_r):
    return _sigmoid(g_m) * u_m + _sigmoid(g_r) * u_r


_NN = ((2,), (1,))
_NT = ((2,), (2,))
_TN = ((1,), (1,))

_SCAN_PASSES = {"cum": 2, "gram": 3, "solve": 1, "apply": 1, "state": 1}


def _hdot_raw(passes, x, y, dims):
    dn = (dims, ((0,), (0,)))
    d = lambda p, q: lax.dot_general(p, q, dn, preferred_element_type=F32)
    xh = x.astype(BF16)
    yh = y.astype(BF16)
    if passes == 1:
        return d(xh, yh)
    yl = (y - yh.astype(F32)).astype(BF16)
    kx, ky = (1 if dims == _TN else 2), (2 if dims == _NT else 1)
    depth = x.shape[kx]
    if all(axis == 1 or depth % LANES == 0 for axis in (kx, ky)):
        if passes == 2:
            return d(jnp.concatenate([xh, xh], axis=kx), jnp.concatenate([yh, yl], axis=ky))
        xl = (x - xh.astype(F32)).astype(BF16)
        return d(jnp.concatenate([xh, xl, xh], axis=kx), jnp.concatenate([yh, yh, yl], axis=ky))
    if passes == 2:
        axis = 1 if dims == _NT else 2
        width = y.shape[axis]
        both = d(xh, jnp.concatenate([yh, yl], axis=axis))
        return both[:, :, :width] + both[:, :, width:]
    xl = (x - xh.astype(F32)).astype(BF16)
    if dims == _TN:
        return d(xh, yh) + d(xh, yl) + d(xl, yh)
    rows = x.shape[1]
    both = d(jnp.concatenate([xh, xl], axis=1), yh)
    return both[:, :rows] + both[:, rows:] + d(xh, yl)


@functools.partial(jax.custom_vjp, nondiff_argnums=(2, 3))
def _hdot_p(x, y, dims, passes):
    return _hdot_raw(passes, x, y, dims)


def _hdot_fwd(x, y, dims, passes):
    return _hdot_raw(passes, x, y, dims), (x, y)


def _hdot_bwd(dims, passes, res, ct):
    x, y = res
    if dims == _NN:
        return _hdot_raw(passes, ct, y, _NT), _hdot_raw(passes, x, ct, _TN)
    if dims == _NT:
        return _hdot_raw(passes, ct, y, _NN), _hdot_raw(passes, ct, x, _TN)
    return _hdot_raw(passes, y, ct, _NT), _hdot_raw(passes, x, ct, _NN)


_hdot_p.defvjp(_hdot_fwd, _hdot_bwd)


def _hdot(x, y, dims, kind):
    return _hdot_p(x, y, dims, _SCAN_PASSES[kind])


def _tri_solve(n_mat, x, length, blocks):
    row = lax.broadcasted_iota(jnp.int32, (length, 2 * length), 0)
    col = lax.broadcasted_iota(jnp.int32, (length, 2 * length), 1)
    col = jnp.where(col >= length, col - length, col)
    eye = (row == col).astype(F32)[None]
    diag_blk = ((row // SUB) == (col // SUB))[None]
    nd = jnp.where(diag_blk, n_mat, 0.0)
    no = n_mat - nd
    dinv = eye + nd
    p = _hdot(nd, blocks(nd), _NN, "solve")
    for k in range(int(math.log2(SUB)) - 1):
        if k == int(math.log2(SUB)) - 2:
            dinv = dinv + _hdot(dinv, blocks(p), _NN, "solve")
        else:
            both = _hdot(jnp.concatenate([dinv, p], axis=1), blocks(p), _NN, "solve")
            dinv, p = dinv + both[:, :length], both[:, length:]
    width = x.shape[2]
    both = _hdot(dinv, jnp.concatenate([blocks(x), blocks(no)], axis=2), _NN, "solve")
    u, q = both[:, :, :width], both[:, :, width:]
    for level in range(int(math.log2(length // SUB))):
        if level == int(math.log2(length // SUB)) - 1:
            u = u + _hdot(q, blocks(u), _NN, "solve")
        else:
            both = _hdot(q, jnp.concatenate([blocks(u), blocks(q)], axis=2), _NN, "solve")
            u, q = u + both[:, :, :width], both[:, :, width:]
    return u


def _rwkv_chunk(rev, s0, r, lw, k, v, a, b):
    pairs, length, width = r.shape
    hn = width // 2
    assert 2 * length == width
    row = lax.broadcasted_iota(jnp.int32, (length, length), 0)
    col = lax.broadcasted_iota(jnp.int32, (length, length), 1)
    row2 = lax.broadcasted_iota(jnp.int32, (length, 2 * length), 0)
    col2 = lax.broadcasted_iota(jnp.int32, (length, 2 * length), 1)
    col2 = jnp.where(col2 >= length, col2 - length, col2)
    if rev is None:
        half = pairs // 2
        back = lax.broadcasted_iota(jnp.int32, (pairs, length, length), 0) >= half
        back2 = lax.broadcasted_iota(jnp.int32, (pairs, length, 2 * length), 0) >= half
        ahead = jnp.where(back, (col - row)[None], (row - col)[None])
        ahead2 = jnp.where(back2, (col2 - row2)[None], (row2 - col2)[None])
        incl, strict2, incl2 = ahead >= 0, ahead2 > 0, ahead2 >= 0
    else:
        incl = ((row <= col) if rev else (row >= col))[None]
        strict2 = ((row2 < col2) if rev else (row2 > col2))[None]
        incl2 = ((row2 <= col2) if rev else (row2 >= col2))[None]
    first = (lax.broadcasted_iota(jnp.int32, (1, 1, width), 2) < hn).astype(F32)
    blocks = lambda t: jnp.concatenate([t * first, t * (1.0 - first)], axis=1)

    t_incl = jnp.broadcast_to(incl.astype(F32), (pairs, length, length))
    cum = _hdot(t_incl, lw, _NN, "cum")
    g = jnp.exp(cum)
    g_inv = jnp.exp(-cum)
    at = a * jnp.exp(cum - lw)
    rt = r * g
    bt = b * g_inv
    kt = k * g_inv
    both_rows = jnp.concatenate([at, rt], axis=1)
    gram = _hdot(both_rows, jnp.concatenate([blocks(bt), blocks(kt)], axis=1), _NT, "gram")
    a_ab = jnp.where(strict2, gram[:, :length, :width], 0.0)
    a_ak = jnp.where(strict2, gram[:, :length, width:], 0.0)
    a_rb = jnp.where(incl2, gram[:, length:, :width], 0.0)
    a_rk = jnp.where(incl2, gram[:, length:, width:], 0.0)
    from_state = _hdot(both_rows, s0, _NT, "apply")
    x = from_state[:, :length] + _hdot(a_ak, blocks(v), _NN, "apply")
    u = _tri_solve(a_ab, x, length, blocks)
    y = from_state[:, length:] + _hdot(jnp.concatenate([a_rb, a_rk], axis=2),
                                       jnp.concatenate([blocks(u), blocks(v)], axis=1), _NN, "apply")
    g_last = jnp.exp(jnp.sum(lw, axis=1, keepdims=True))
    ri = lax.broadcasted_iota(jnp.int32, (width, width), 0)
    ci = lax.broadcasted_iota(jnp.int32, (width, width), 1)
    same_head = ((ri < hn) == (ci < hn))[None]
    upd = _hdot(jnp.concatenate([u, v], axis=1), jnp.concatenate([bt, kt], axis=1), _TN, "state")
    s1 = (s0 + jnp.where(same_head, upd, 0.0)) * g_last
    return y, s1


def _split_pairs(x):
    return jnp.stack([x[:, p * LANES:(p + 1) * LANES] for p in range(x.shape[1] // LANES)])


def _merge_pairs(x):
    return jnp.concatenate([x[p] for p in range(x.shape[0])], axis=1)


def _scan_specs(views, rw, nc, rev):
    cidx = (lambda c: nc - 1 - c) if rev else (lambda c: c)
    seqs = [pl.BlockSpec((CHUNK, rw), functools.partial(lambda c, cb: (cidx(c), cb), cb=cb)) for _, cb, _ in views]
    plain = pl.BlockSpec((CHUNK, rw), lambda c: (cidx(c), 0))
    st = pl.BlockSpec((1, rw // LANES, LANES, LANES), lambda c: (cidx(c), 0, 0, 0))
    return seqs, plain, st


def _as_views(arrs, rw):
    return [t if isinstance(t, tuple) else (t, 0, rw) for t in arrs]


def _rwkv_scan_fwd(ops_f, ops_b, rw, *, name):
    S = _as_views(ops_f, rw)[0][0].shape[0]
    nc, pairs = S // CHUNK, rw // LANES
    in_specs, out_specs, arrays = [], [], []
    for rev, ops in ((False, ops_f), (True, ops_b)):
        views = _as_views(ops, rw)
        seqs, plain, st = _scan_specs(views, rw, nc, rev)
        in_specs += seqs
        out_specs += [plain, st]
        arrays += [t[0] for t in views]

    def both(refs_f, refs_b):
        return [jnp.concatenate([_split_pairs(f[...]), _split_pairs(b[...])], axis=0) for f, b in zip(refs_f, refs_b)]

    def body(*refs):
        (y_f, st_f, y_b, st_b), s_ref = refs[12:16], refs[16]

        @pl.when(pl.program_id(0) == 0)
        def _():
            s_ref[...] = jnp.zeros_like(s_ref)

        s0 = s_ref[...]
        st_f[0] = s0[:pairs]
        st_b[0] = s0[pairs:]
        y, s1 = _rwkv_chunk(None, s0, *both(refs[:6], refs[6:12]))
        y_f[...] = _merge_pairs(y[:pairs])
        y_b[...] = _merge_pairs(y[pairs:])
        s_ref[...] = s1

    return pl.pallas_call(
        body, name=name, grid=(nc,), in_specs=in_specs, out_specs=out_specs,
        out_shape=[jax.ShapeDtypeStruct((S, rw), F32), jax.ShapeDtypeStruct((nc, pairs, LANES, LANES), F32)] * 2,
        scratch_shapes=[pltpu.VMEM((2 * pairs, LANES, LANES), F32)],
        compiler_params=_cparams(("arbitrary",)),
    )(*arrays)


def _rwkv_scan_bwd(ops_f, ops_b, states_f, states_b, dy, rw, *, name):
    S = dy.shape[0]
    nc, pairs = S // CHUNK, rw // LANES
    in_specs, arrays = [], []
    for rev, ops, states in ((False, ops_f, states_f), (True, ops_b, states_b)):
        views = _as_views(list(ops) + [dy], rw)
        seqs, plain, st = _scan_specs(views, rw, nc, not rev)
        in_specs += seqs + [st]
        arrays += [t[0] for t in views] + [states]
    out_specs = []
    for rev in (False, True):
        out_specs += [_scan_specs([], rw, nc, not rev)[1]] * 6

    def both(refs_f, refs_b):
        return [jnp.concatenate([_split_pairs(f[...]), _split_pairs(b[...])], axis=0) for f, b in zip(refs_f, refs_b)]

    def body(*refs):
        ds_ref = refs[28]

        @pl.when(pl.program_id(0) == 0)
        def _():
            ds_ref[...] = jnp.zeros_like(ds_ref)

        s0 = jnp.concatenate([refs[7][0], refs[15][0]], axis=0)
        _, vjp = jax.vjp(functools.partial(_rwkv_chunk, None), s0, *both(refs[:6], refs[8:14]))
        (dy,) = both(refs[6:7], refs[14:15])
        grads = vjp((dy, ds_ref[...]))
        ds_ref[...] = grads[0]
        for o_f, o_b, gval in zip(refs[16:22], refs[22:28], grads[1:]):
            o_f[...] = _merge_pairs(gval[:pairs])
            o_b[...] = _merge_pairs(gval[pairs:])

    return pl.pallas_call(
        body, name=name, grid=(nc,), in_specs=in_specs, out_specs=out_specs,
        out_shape=[jax.ShapeDtypeStruct((S, rw), F32)] * 12,
        scratch_shapes=[pltpu.VMEM((2 * pairs, LANES, LANES), F32)],
        compiler_params=_cparams(("arbitrary",)),
    )(*arrays)


def _shift_lerp(x_view, mu, d=None, into=None, *, name):
    arr, off, width = x_view
    S = arr.shape[0]
    cb = _pick(width, 512)
    assert off % cb == 0

    def cshift(t):
        rows = lax.broadcasted_iota(jnp.int32, t.shape, 0)
        prev = jnp.where(rows == 0, 0.0, pltpu.roll(t, 1, 0))
        nxt = jnp.where(rows == S - 1, 0.0, pltpu.roll(t, S - 1, 0))
        return 0.5 * (prev + nxt)

    def fwd_body(x_ref, mu_ref, o_ref):
        x = x_ref[...]
        o_ref[...] = x + mu_ref[...] * (cshift(x) - x)

    def bwd_body(x_ref, mu_ref, d_ref, _, dx_ref, dmu_ref):
        x, m, dd = x_ref[...], mu_ref[...], d_ref[...]
        gm = m * dd
        dx_ref[...] = (dd - gm + cshift(gm)).astype(dx_ref.dtype)
        dmu_ref[...] = jnp.sum(dd * (cshift(x) - x), axis=0, keepdims=True)

    x_spec = pl.BlockSpec((S, cb), lambda j: (0, off // cb + j))
    blk = pl.BlockSpec((S, cb), lambda j: (0, j))
    vec = pl.BlockSpec((1, cb), lambda j: (0, j))
    if d is None:
        return pl.pallas_call(
            fwd_body, name=name, grid=(width // cb,), in_specs=[x_spec, vec], out_specs=blk,
            out_shape=jax.ShapeDtypeStruct((S, width), F32), compiler_params=_cparams(("parallel",)),
        )(arr, mu)
    buf, first = into
    assert first % cb == 0
    return pl.pallas_call(
        bwd_body, name=name, grid=(width // cb,),
        in_specs=[x_spec, vec, blk, pl.BlockSpec(memory_space=pl.ANY)],
        out_specs=[pl.BlockSpec((S, cb), lambda j: (0, first // cb + j)), vec],
        out_shape=[jax.ShapeDtypeStruct(buf.shape, buf.dtype), jax.ShapeDtypeStruct((1, width), F32)],
        input_output_aliases={3: 0}, compiler_params=_cparams(("parallel",)),
    )(arr, mu, d, buf)


def _attention_fwd(qfull, kv, kr, hm, scale, *, tq, name):
    S = qfull.shape[0]
    nt = (((1,), (1,)), ((), ()))

    def body(q_ref, kn_ref, kr_ref, v_ref, o_ref, lse_ref, k_scr):
        _head_keys(kn_ref, kr_ref, k_scr)
        s = lax.dot_general(q_ref[...], k_scr[...], nt, preferred_element_type=F32)
        m = jnp.max(s, axis=-1, keepdims=True)
        p = jnp.exp((s - m) * scale)
        l = jnp.sum(p, axis=-1, keepdims=True)
        o_ref[...] = jnp.dot(p.astype(BF16), v_ref[...], preferred_element_type=F32) * (1.0 / l)
        lse_ref[...] = jnp.broadcast_to(m * scale + jnp.log(l), lse_ref.shape)

    oblk = pl.BlockSpec((tq, VDIM), lambda h, i: (i, h))
    return pl.pallas_call(
        body, name=name, grid=(hm, S // tq),
        in_specs=[pl.BlockSpec((tq, QHEAD), lambda h, i: (i, h)),
                  pl.BlockSpec((S, NOPE), lambda h, i: (0, h)),
                  pl.BlockSpec((S, LANES), lambda h, i: (0, 0)),
                  pl.BlockSpec((S, VDIM), lambda h, i: (0, hm + h))],
        out_specs=[oblk, oblk],
        out_shape=[jax.ShapeDtypeStruct((S, hm * VDIM), F32)] * 2,
        scratch_shapes=[pltpu.VMEM((S, QHEAD), BF16)],
        compiler_params=_cparams(("parallel", "arbitrary")),
    )(qfull, kv, kr, kv)


def _head_keys(kn_ref, kr_ref, k_scr):
    @pl.when(pl.program_id(1) == 0)
    def _():
        k_scr[:, :NOPE] = kn_ref[...]
        k_scr[:, NOPE:] = kr_ref[...]


def _attention_bwd(qfull, kv, kr, o, lse, d_o, hm, scale, *, tq, name):
    S = qfull.shape[0]
    tq = min(tq, S)
    nq = S // tq
    tn = (((0,), (0,)), ((), ()))
    nt = (((1,), (1,)), ((), ()))

    def body(q_ref, kn_ref, kr_ref, v_ref, o_ref, lse_ref, do_ref, dq_ref, dk_ref, dv_ref, k_scr):
        _head_keys(kn_ref, kr_ref, k_scr)
        s = lax.dot_general(q_ref[...], k_scr[...], nt, preferred_element_type=F32)
        p = jnp.exp(s * scale - lse_ref[:, 0:1])
        d_out = do_ref[...]
        delta = jnp.sum(d_out * o_ref[...], axis=-1, keepdims=True)
        d_out = d_out.astype(BF16)
        dp = lax.dot_general(d_out, v_ref[...], nt, preferred_element_type=F32)
        ds = (p * (dp - delta)).astype(BF16)
        dq_ref[...] = jnp.dot(ds, k_scr[...], preferred_element_type=F32) * scale
        dv = lax.dot_general(p.astype(BF16), d_out, tn, preferred_element_type=F32)
        dk = lax.dot_general(ds, q_ref[...], tn, preferred_element_type=F32)
        i = pl.program_id(1)
        for ref, val in ((dk_ref, dk), (dv_ref, dv)):
            @pl.when(i == 0)
            def _(ref=ref, val=val):
                ref[...] = val

            @pl.when(i > 0)
            def _(ref=ref, val=val):
                ref[...] += val

        @pl.when(i == nq - 1)
        def _():
            dk_ref[...] = dk_ref[...] * scale

    qblk = pl.BlockSpec((tq, QHEAD), lambda h, i: (i, h))
    oblk = pl.BlockSpec((tq, VDIM), lambda h, i: (i, h))
    return pl.pallas_call(
        body, name=name, grid=(hm, nq),
        in_specs=[qblk,
                  pl.BlockSpec((S, NOPE), lambda h, i: (0, h)),
                  pl.BlockSpec((S, LANES), lambda h, i: (0, 0)),
                  pl.BlockSpec((S, VDIM), lambda h, i: (0, hm + h)),
                  oblk, oblk, oblk],
        out_specs=[qblk, pl.BlockSpec((S, QHEAD), lambda h, i: (0, h)), pl.BlockSpec((S, VDIM), lambda h, i: (0, h))],
        out_shape=[jax.ShapeDtypeStruct((S, hm * QHEAD), F32), jax.ShapeDtypeStruct((S, hm * QHEAD), F32),
                   jax.ShapeDtypeStruct((S, hm * VDIM), F32)],
        scratch_shapes=[pltpu.VMEM((S, QHEAD), BF16)],
        compiler_params=_cparams(("parallel", "arbitrary")),
    )(qfull, kv, kr, kv, o, lse, d_o)


def _layout(D, MW, RW, TAIL, QR, KVR):
    names = ["gate_m", "gate_r", "z_m", "z_r", "q_a", "kv_a", "r", "k", "v", "tail"]
    widths = [D, D, MW, RW, QR, KVR, RW, RW, RW, TAIL]
    offs, o = {}, 0
    for nme, w in zip(names, widths):
        assert o % w == 0, (nme, o, w)
        offs[nme] = (o, w)
        o += w
    return offs, o


def _local_grads(x, target, W, dims, exchange=None):
    S, D = x.shape
    hm, hr, hn, rank = dims["hm"], dims["hr"], dims["hn"], dims["rank"]
    MW, RW = hm * VDIM, hr * hn
    TAIL = dims["TAIL"]
    QR, KVR = W["mla_q_norm"].shape[1], W["mla_kv_norm"].shape[1]
    lay, d_in = _layout(D, MW, RW, TAIL, QR, KVR)
    T = 256
    scale = (NOPE + ROPE) ** -0.5
    col = lambda arr, nme: _view(arr, *lay[nme])

    pos = jnp.arange(S, dtype=F32)
    inv_freq = jnp.power(ROPE_THETA, -jnp.arange(0, ROPE, 2, dtype=F32) / ROPE)
    ang = pos[:, None] * inv_freq[None, :]
    zpad = jnp.zeros((S, LANES - ROPE), F32)
    cosx = jnp.concatenate([jnp.cos(ang), jnp.cos(ang), zpad], axis=1)
    sinx = jnp.concatenate([jnp.sin(ang), jnp.sin(ang), zpad], axis=1)
    ri, ci = jnp.arange(LANES)[:, None], jnp.arange(LANES)[None, :]
    half = ROPE // 2
    rot = (jnp.where((ri == ci - half) & (ci >= half) & (ci < ROPE), 1.0, 0.0)
           - jnp.where((ri == ci + half) & (ci < half), 1.0, 0.0)).astype(BF16)
    seg = (jnp.arange(RW)[:, None] // hn == jnp.arange(LANES)[None, :]).astype(BF16)
    stacked = lambda t: jnp.concatenate([t, t], axis=0)
    rot, rot_t, seg, seg_t = stacked(rot), stacked(rot.T), stacked(seg), stacked(seg.T)

    (h,) = _rowwise(lambda xb, g: (_rms(xb, g),), [x], [W["g_pre"]], [(D, BF16)], tile=2 * T, name="pre_norm")
    if exchange is None:
        proj = _mm(h, W["w_in_t"], tb=True, name="in_proj")
    else:
        proj, *slabs = _mm(h, W["w_in_t"], tb=True, ride=_gather_plan(exchange[0]), name="in_proj")
        W = {**W, **_prepare_rest(dict(zip(_MATS[1:], slabs)), dims)}

    qn, kvn = _rowwise(_f_mla_norm, [col(proj, "q_a"), col(proj, "kv_a")], [W["mla_q_norm"], W["mla_kv_norm"]],
                       [(QR, BF16), (KVR, BF16)], tile=2 * T, name="mla_norm")
    qraw = _mm(qn, W["wq_b_t"], tb=True, name="q_up")
    kv = _mm(kvn, W["wkv_b"], out_dtype=BF16, name="kv_up")
    kr_view = _view(proj, lay["tail"][0], LANES)
    qfull, kr = _rowwise(functools.partial(_f_rope, hm), [qraw, kr_view, cosx, sinx], [rot, rot_t],
                         [(hm * QHEAD, BF16), (LANES, BF16)], tile=2 * T, name="rope")
    o_mla, lse = _attention_fwd(qfull, kv, kr, hm, scale, tq=T, name="attn_fwd")

    shift_view = (proj, lay["r"][0], 3 * RW + TAIL)
    rl = _shift_lerp(shift_view, W["mu"], name="shift_fwd")
    rl_r, rl_k, rl_v = _view(rl, 0, RW), _view(rl, RW, RW), _view(rl, 2 * RW, RW)
    rl_tail = _view(rl, 3 * RW, TAIL)
    pre_params = [W["w0_f"], W["w0_b"], W["a0_f"], W["a0_b"], W["k_k"], W["k_a"], W["w2cat"], W["a2cat"], seg, seg_t]
    pre_fn = functools.partial(_f_rwkv_pre, RW)
    lw_f, lw_b, k_f, k_b, a_n, b_f, b_b = _rowwise(pre_fn, [rl_k, rl_tail], pre_params, [(RW, F32)] * 7, tile=2 * T,
                                                    name="rwkv_pre")
    ops_f = (rl_r, lw_f, k_f, rl_v, a_n, b_f)
    ops_b = (rl_r, lw_b, k_b, rl_v, a_n, b_b)
    y_f, st_f, y_b, st_b = _rwkv_scan_fwd(ops_f, ops_b, RW, name="scan_fwd")

    post_fn = functools.partial(_f_post, hn)
    post_rows = [y_f, y_b, rl_r, k_f, k_b, rl_v, col(proj, "z_r"), o_mla, col(proj, "z_m")]
    post_params = [W["gn_g"], W["gn_b"], W["r_k"], seg, seg_t]
    ymg, yrg = _rowwise(post_fn, post_rows, post_params, [(MW, BF16), (RW, BF16)], tile=T, name="post")
    u_m = _mm(ymg, W["w_br_mla"], out_dtype=BF16, name="br_mla")
    u_r = _mm(yrg, W["w_br_rwkv"], out_dtype=BF16, name="br_rwkv")
    merge_rows = [u_m, u_r, col(proj, "gate_m"), col(proj, "gate_r")]
    (merged,) = _rowwise(lambda *t: (_f_merge(*t),), merge_rows, [], [(D, BF16)], tile=T, name="merge")
    out = _mm(merged, W["w_out"], name="out_proj")

    def head(ob, xb, tb, g):
        yn, vjp = jax.vjp(_rms, ob, g)
        err = xb + yn - tb
        dy = err * (1.0 / D)
        d_ob, d_g = vjp(dy)
        loss = jnp.broadcast_to(0.5 * jnp.sum(err * err) * (1.0 / D), (1, LANES))
        return dy, d_ob, loss, d_g

    dy, d_out, loss, g_g_post = _rowwise(head, [out, x, target], [W["g_post"]], [(D, F32), (D, BF16)],
                                         [(1, LANES), (1, D)], tile=2 * T, name="head")
    d_merged = _mm(d_out, W["w_out"], tb=True, name="d_merged")
    g_w_out = _mm(merged, d_out, ta=True, out_dtype=BF16, name="g_w_out")

    def merge_bwd(u_m_b, u_r_b, g_m_b, g_r_b, dm):
        _, vjp = jax.vjp(_f_merge, u_m_b, u_r_b, g_m_b, g_r_b)
        du_m, du_r, dg_m, dg_r = vjp(dm)
        return du_m, du_r, jnp.concatenate([dg_m, dg_r], axis=1)

    d_u_m, d_u_r, d_proj = _rowwise(merge_bwd, merge_rows + [d_merged], [],
                                    [(D, BF16), (D, BF16), (2 * D, BF16, (None, d_in, lay["gate_m"][0]))], tile=T,
                                    name="merge_bwd")
    d_ymg = _mm(d_u_m, W["w_br_mla"], tb=True, name="d_ymg")
    d_yrg = _mm(d_u_r, W["w_br_rwkv"], tb=True, name="d_yrg")
    g_w_br_mla = _mm(ymg, d_u_m, ta=True, out_dtype=BF16, name="g_w_br_mla")
    g_w_br_rwkv = _mm(yrg, d_u_r, ta=True, out_dtype=BF16, name="g_w_br_rwkv")

    def post_bwd(*args):
        nr = len(post_rows)
        prim, dm, dr = args[:nr] + args[nr + 2:], args[nr], args[nr + 1]
        _, vjp = jax.vjp(post_fn, *prim)
        g = vjp((dm, dr))
        return g[0], g[2], g[3], g[5], g[7], jnp.concatenate([g[8], g[6]], axis=1), g[9], g[10], g[11]

    (d_y, d_r_bonus, d_k_bonus, d_v_bonus, d_o, d_proj, g_gn_g, g_gn_b, g_r_k) = _rowwise(
        post_bwd, post_rows + [d_ymg, d_yrg], post_params,
        [(RW, F32), (RW, F32), (RW, F32), (RW, F32), (MW, F32), (MW + RW, BF16, (d_proj, d_in, lay["z_m"][0]))],
        [(1, RW)] * 3, tile=T, name="post_bwd")

    dscan = _rwkv_scan_bwd(ops_f, ops_b, st_f, st_b, d_y, RW, name="scan_bwd")
    dsc = {"f": dscan[:6], "b": dscan[6:]}

    d_q_att, d_k_att, d_v_att = _attention_bwd(qfull, kv, kr, o_mla, lse, d_o, hm, scale, tq=4 * T, name="attn_bwd")

    def rope_bwd(qraw_b, kr_in, cos_b, sin_b, dq_b, dk_b, dv_b, rot_b, rot_t_b):
        _, vjp = jax.vjp(lambda q_, k_: _f_rope(hm, q_, k_, cos_b, sin_b, rot_b, rot_t_b), qraw_b, kr_in)
        dkn = jnp.concatenate([dk_b[:, hh * QHEAD:hh * QHEAD + NOPE] for hh in range(hm)], axis=1)
        dkr = dk_b[:, NOPE:QHEAD]
        for hh in range(1, hm):
            dkr = dkr + dk_b[:, hh * QHEAD + NOPE:(hh + 1) * QHEAD]
        d_qraw, d_kr_in = vjp((dq_b, dkr))
        return d_qraw, jnp.concatenate([dkn, dv_b], axis=1), d_kr_in

    d_qraw, d_kv, d_kr_in = _rowwise(rope_bwd, [qraw, kr_view, cosx, sinx, d_q_att, d_k_att, d_v_att],
                                     [rot, rot_t], [(hm * QHEAD, BF16), (2 * MW, BF16), (LANES, F32)], tile=T,
                                     name="rope_bwd")
    d_qnorm = _mm(d_qraw, W["wq_b_t"], name="d_qn")
    d_kvnorm = _mm(d_kv, W["wkv_b"], tb=True, name="d_kvn")
    g_wq_b = _mm(d_qraw, qn, ta=True, out_dtype=BF16, name="g_wq_b")
    g_wkv_b = _mm(kvn, d_kv, ta=True, out_dtype=BF16, name="g_wkv_b")

    def mla_norm_bwd(q_a, kv_a, qg, kvg, dq, dk):
        _, vjp = jax.vjp(_f_mla_norm, q_a, kv_a, qg, kvg)
        d_q_a, d_kv_a, d_qg, d_kvg = vjp((dq, dk))
        return jnp.concatenate([d_q_a, d_kv_a], axis=1), d_qg, d_kvg

    d_proj, g_q_norm, g_kv_norm = _rowwise(
        lambda q_a, kv_a, dq, dk, qg, kvg: mla_norm_bwd(q_a, kv_a, qg, kvg, dq, dk),
        [col(proj, "q_a"), col(proj, "kv_a"), d_qnorm, d_kvnorm], [W["mla_q_norm"], W["mla_kv_norm"]],
        [(QR + KVR, BF16, (d_proj, d_in, lay["q_a"][0]))], [(1, QR), (1, KVR)], tile=2 * T, name="mla_norm_bwd")

    def pre_bwd(k_b_, tail_b, dlwf, dlwb, dkf, dkb, dkbon, daf, dab, dbf, dbb, drf, drb, drbon, dvf, dvb, dvbon,
                dkr, *params):
        w2, a2 = params[6], params[7]
        nt, tn = (((1,), (1,)), ((), ())), (((0,), (0,)), ((), ()))
        split = w2.shape[0]
        th = jnp.tanh(tail_b[:, :split])
        th_b, tail_h = th.astype(BF16), tail_b[:, split:].astype(BF16)
        zw = jnp.dot(th_b, w2, preferred_element_type=F32)
        za = jnp.dot(tail_h, a2, preferred_element_type=F32)
        _, vjp = jax.vjp(functools.partial(_f_rwkv_core, RW), k_b_, zw, za, *params[:6], params[8], params[9])
        g = vjp((dlwf, dlwb, dkf + dkbon, dkb + dkbon, daf + dab, dbf, dbb))
        d_zw, d_za = g[1].astype(BF16), g[2].astype(BF16)
        d_tail = (jnp.concatenate([lax.dot_general(d_zw, w2, nt, preferred_element_type=F32) * (1.0 - th * th),
                                   lax.dot_general(d_za, a2, nt, preferred_element_type=F32)], axis=1)
                  + jnp.concatenate([dkr, jnp.zeros((dkr.shape[0], TAIL - LANES), F32)], axis=1))
        g_w2 = lax.dot_general(th_b, d_zw, tn, preferred_element_type=F32)
        g_a2 = lax.dot_general(tail_h, d_za, tn, preferred_element_type=F32)
        d_rl = jnp.concatenate([drf + drb + drbon, g[0], dvf + dvb + dvbon, d_tail], axis=1)
        return (d_rl,) + tuple(g[3:9]) + (g_w2, g_a2)

    f_, b_ = dsc["f"], dsc["b"]
    pre_bwd_rows = [rl_k, rl_tail, f_[1], b_[1], f_[2], b_[2], d_k_bonus, f_[4], b_[4], f_[5], b_[5],
                    f_[0], b_[0], d_r_bonus, f_[3], b_[3], d_v_bonus, d_kr_in]
    (d_rl, g_w0_f, g_w0_b, g_a0_f, g_a0_b, g_k_k, g_k_a, g_w2cat, g_a2cat) = _rowwise(
        pre_bwd, pre_bwd_rows, pre_params, [(3 * RW + TAIL, F32)],
        [(1, RW)] * 6 + [W["w2cat"].shape, W["a2cat"].shape], tile=T // 2, name="rwkv_pre_bwd")
    d_proj, g_mu = _shift_lerp(shift_view, W["mu"], d_rl, (d_proj, lay["r"][0]), name="shift_bwd")
    small = dict(wq_b=g_wq_b, wkv_b=g_wkv_b, w2cat=g_w2cat, a2cat=g_a2cat, w_br_mla=g_w_br_mla,
                 w_br_rwkv=g_w_br_rwkv, w_out=g_w_out)
    if exchange is None:
        received = None
        g_w_in = _mm(d_proj, h, ta=True, out_dtype=BF16, tn_cap=1024, name="g_w_in")
        d_h = _mm(d_proj, W["w_in_t"], tn_cap=1024, name="d_h")
    else:
        slabs = _restore_rest(small, dims)
        slabs = [slabs[n] for n in _MATS[1:]]
        g_w_in, *got = _mm(d_proj, h, ta=True, out_dtype=BF16, tn_cap=1024, ride=_sibling_swap_plan(slabs),
                           name="g_w_in")
        sums = [_pair_add(exchange[1], s, t, name="pair_add_" + n) for n, s, t in zip(_MATS[1:], slabs, got)]
        g_w_in = _restore_w_in(g_w_in, dims)
        d_h, *received = _mm(d_proj, W["w_in_t"], tn_cap=1024, name="d_h",
                             ride=_join_plans(_chip_exchange_plan(sums), _sibling_swap_plan([g_w_in])))
        small = {}

    def pre_norm_bwd(xb, dyb, dhb, g):
        _, vjp = jax.vjp(_rms, xb, g)
        dx, dg = vjp(dhb)
        return dyb + dx, dg

    grad_x, g_g_pre = _rowwise(pre_norm_bwd, [x, dy, d_h], [W["g_pre"]], [(D, F32)], [(1, D)], tile=2 * T,
                               name="pre_norm_bwd")

    grads = dict(g_pre=g_g_pre, w_in=g_w_in, mla_q_norm=g_q_norm, mla_kv_norm=g_kv_norm, mu=g_mu, w0_f=g_w0_f,
                 w0_b=g_w0_b, a0_f=g_a0_f, a0_b=g_a0_b, k_k=g_k_k, k_a=g_k_a, r_k=g_r_k, gn_g=g_gn_g, gn_b=g_gn_b,
                 g_post=g_g_post, **small)
    return loss[0, 0], grad_x, grads, received


_MATS = ["w_in", "mla_wq_b", "mla_wkv_b", "rwkv_w2_f", "rwkv_w2_b", "rwkv_a2_f", "rwkv_a2_b", "w_br_mla",
         "w_br_rwkv", "w_out"]
_ROW_SHARDED = ("w_out",)
_TRANSPOSED = ("w_in", "mla_wq_b")
_VECS = ["g_pre", "mla_q_norm", "mla_kv_norm", "rwkv_mu", "rwkv_w0_f", "rwkv_w0_b", "rwkv_a0_f", "rwkv_a0_b",
         "rwkv_k_k", "rwkv_k_a", "rwkv_r_k", "rwkv_gn_g", "rwkv_gn_b", "g_post"]
_WEIGHTS = ["g_pre", "w_in", "mla_q_norm", "mla_wq_b", "mla_kv_norm", "mla_wkv_b", "rwkv_mu", "rwkv_w0_f",
            "rwkv_w2_f", "rwkv_w0_b", "rwkv_w2_b", "rwkv_a0_f", "rwkv_a2_f", "rwkv_a0_b", "rwkv_a2_b", "rwkv_k_k",
            "rwkv_k_a", "rwkv_r_k", "rwkv_gn_g", "rwkv_gn_b", "w_br_mla", "w_br_rwkv", "w_out", "g_post"]

def _direct_gather_plan(src):
    def phases(src_refs, out_refs, sem_refs):
        (src_ref,), (out_ref,), sems, local_sem = src_refs, out_refs, sem_refs[:2], sem_refs[2]
        x, y, c = lax.axis_index("x"), lax.axis_index("y"), lax.axis_index("c")
        me = 4 * x + 2 * y + c
        flip = lambda v, bit: (1 - v) if bit else v
        peers = [(flip(x, d & 4), flip(y, d & 2), flip(c, d & 1)) for d in range(1, N_DEV)]
        own = lambda: pltpu.make_async_copy(src_ref, out_ref.at[me], local_sem)
        send = lambda d: _remote(src_ref, out_ref.at[me], sems, d, peers[d])

        def first():
            own().start()
            for d in range(N_DEV - 1):
                send(d).start()

        def last():
            for d, (px, py, pc) in enumerate(peers):
                blk = out_ref.at[4 * px + 2 * py + pc]
                _remote(blk, blk, sems, d, (x, y, c)).wait_recv()
            for d in range(N_DEV - 1):
                send(d).wait_send()
            own().wait()

        return first, (lambda: None), last

    return [src], [jax.ShapeDtypeStruct((N_DEV,) + src.shape, src.dtype)], [(N_DEV - 1,), (N_DEV - 1,), ()], phases


def _remote(src, dst, sems, key, to):
    send_sems, recv_sems = sems
    return pltpu.make_async_remote_copy(src_ref=src, dst_ref=dst, send_sem=send_sems.at[key], recv_sem=recv_sems.at[key],
                                        device_id=to, device_id_type=pl.DeviceIdType.MESH)


def _run_exchange(plan, *, name):
    srcs, out_shapes, sem_shapes, phases = plan
    n, m = len(srcs), len(out_shapes)

    def body(*refs):
        for phase in phases(refs[:n], refs[n:n + m], refs[n + m:]):
            phase()

    return pl.pallas_call(
        body, name=name, out_shape=out_shapes,
        in_specs=[pl.BlockSpec(memory_space=pl.ANY)] * n, out_specs=[pl.BlockSpec(memory_space=pl.ANY)] * m,
        scratch_shapes=[pltpu.SemaphoreType.DMA(s) for s in sem_shapes],
    )(*srcs)


def _join_plans(p, q):
    (srcs_p, outs_p, sems_p, phases_p), (srcs_q, outs_q, sems_q, phases_q) = p, q

    def phases(src_refs, out_refs, sem_refs):
        a = phases_p(src_refs[:len(srcs_p)], out_refs[:len(outs_p)], sem_refs[:len(sems_p)])
        b = phases_q(src_refs[len(srcs_p):], out_refs[len(outs_p):], sem_refs[len(sems_p):])

        def both(fa, fb):
            def run():
                fa()
                fb()
            return run

        return tuple(both(fa, fb) for fa, fb in zip(a, b))

    return list(srcs_p) + list(srcs_q), list(outs_p) + list(outs_q), list(sems_p) + list(sems_q), phases


def _gather_plan(srcs):
    n = len(srcs)

    def phases(src_refs, out_refs, sem_refs):
        sems, local_sems = sem_refs[:2], sem_refs[2]
        x, y, c = lax.axis_index("x"), lax.axis_index("y"), lax.axis_index("c")
        idx = lambda px, py, pc: 4 * px + 2 * py + pc
        me, sibling = (x, y, c), (x, y, 1 - c)
        chips = [(1 - x, y), (x, 1 - y), (1 - x, 1 - y)]
        own = lambda a: pltpu.make_async_copy(src_refs[a], out_refs[a].at[idx(*me)], local_sems.at[a])
        to_sibling = lambda a: _remote(src_refs[a], out_refs[a].at[idx(*me)], sems, (0, a), sibling)
        to_chip = lambda a, j: _remote(src_refs[a], out_refs[a].at[idx(*me)], sems, (1 + j, a), (*chips[j], c))
        landed = lambda a, j: out_refs[a].at[idx(*chips[j], c)]
        passed_on = lambda a, j: _remote(landed(a, j), landed(a, j), sems, (4 + j, a), sibling)

        def first():
            for a in range(n):
                own(a).start()
                to_sibling(a).start()
                for j in range(3):
                    to_chip(a, j).start()

        def middle():
            for j in range(3):
                for a in range(n):
                    _remote(landed(a, j), landed(a, j), sems, (1 + j, a), me).wait_recv()
                    passed_on(a, j).start()

        def last():
            for a in range(n):
                blk = out_refs[a].at[idx(*sibling)]
                _remote(blk, blk, sems, (0, a), me).wait_recv()
                for j in range(3):
                    blk = out_refs[a].at[idx(*chips[j], 1 - c)]
                    _remote(blk, blk, sems, (4 + j, a), me).wait_recv()
            for a in range(n):
                to_sibling(a).wait_send()
                for j in range(3):
                    to_chip(a, j).wait_send()
                    passed_on(a, j).wait_send()
                own(a).wait()

        return first, middle, last

    return srcs, [jax.ShapeDtypeStruct((N_DEV,) + s.shape, s.dtype) for s in srcs], [(7, n), (7, n), (n,)], phases


def _sibling_swap_plan(srcs):
    n = len(srcs)

    def phases(src_refs, out_refs, sems):
        x, y, c = lax.axis_index("x"), lax.axis_index("y"), lax.axis_index("c")
        copies = lambda: [_remote(src_refs[a].at[2 * q + 1 - c], out_refs[a].at[q], sems, (q, a), (x, y, 1 - c))
                          for a in range(n) for q in range(4)]

        def first():
            for cp in copies():
                cp.start()

        def last():
            for cp in copies():
                cp.wait()

        return first, (lambda: None), last

    return srcs, [jax.ShapeDtypeStruct((4,) + s.shape[1:], s.dtype) for s in srcs], [(4, n), (4, n)], phases


def _chip_exchange_plan(srcs):
    n = len(srcs)

    def phases(src_refs, out_refs, sem_refs):
        sems, local_sems = sem_refs[:2], sem_refs[2]
        x, y, c = lax.axis_index("x"), lax.axis_index("y"), lax.axis_index("c")
        mine = 2 * x + y
        chips = [(1 - x, y), (x, 1 - y), (1 - x, 1 - y)]
        own = lambda a: pltpu.make_async_copy(src_refs[a].at[mine], out_refs[a].at[mine], local_sems.at[a])
        send = lambda a, j: _remote(src_refs[a].at[2 * chips[j][0] + chips[j][1]], out_refs[a].at[mine], sems, (j, a),
                                    (*chips[j], c))

        def first():
            for a in range(n):
                own(a).start()
                for j in range(3):
                    send(a, j).start()

        def last():
            for j in range(3):
                for a in range(n):
                    blk = out_refs[a].at[2 * chips[j][0] + chips[j][1]]
                    _remote(blk, blk, sems, (j, a), (x, y, c)).wait_recv()
            for a in range(n):
                for j in range(3):
                    send(a, j).wait_send()
                own(a).wait()

        return first, (lambda: None), last

    return srcs, [jax.ShapeDtypeStruct(s.shape, s.dtype) for s in srcs], [(3, n), (3, n), (n,)], phases


def _pair_add(core, g, got, *, name):
    q, r, c = got.shape
    tr, tc = _tile2d(r, c, cap=1024)

    def body(core_ref, a_ref, b_ref, o_ref):
        o_ref[...] = (a_ref[...].astype(F32) + b_ref[...].astype(F32)).astype(BF16)

    blk = pl.BlockSpec((1, tr, tc), lambda i, j, k, core_ref: (i, j, k))
    mine = pl.BlockSpec((1, tr, tc), lambda i, j, k, core_ref: (2 * i + core_ref[0], j, k))
    return pl.pallas_call(
        body, name=name, out_shape=jax.ShapeDtypeStruct(got.shape, BF16),
        grid_spec=pltpu.PrefetchScalarGridSpec(num_scalar_prefetch=1, grid=(q, r // tr, c // tc),
                                               in_specs=[mine, blk], out_specs=blk),
        compiler_params=_cparams(("parallel", "parallel", "parallel")))(core, g, got)


def _adamw(recv, w, m, v, *, name):
    r, c = w.shape
    n_terms = recv.shape[0]
    tr, tc = _tile2d(r, c)

    def body(g_ref, w_ref, m_ref, v_ref, go_ref, d_ref, mo_ref, vo_ref):
        g = g_ref[0].astype(F32)
        for k in range(1, n_terms):
            g = g + g_ref[k].astype(F32)
        m_new = ADAM_B1 * m_ref[...] + (1.0 - ADAM_B1) * g
        v_new = ADAM_B2 * v_ref[...] + (1.0 - ADAM_B2) * (g * g)
        m_hat = m_new / (1.0 - ADAM_B1 ** ADAM_STEP)
        v_hat = v_new / (1.0 - ADAM_B2 ** ADAM_STEP)
        go_ref[...] = g
        d_ref[...] = -ADAM_LR * (m_hat / (jnp.sqrt(v_hat) + ADAM_EPS) + ADAM_WD * w_ref[...])
        mo_ref[...] = m_new
        vo_ref[...] = v_new

    blk = pl.BlockSpec((tr, tc), lambda i, j: (i, j))
    return pl.pallas_call(
        body, name=name, grid=(r // tr, c // tc),
        in_specs=[pl.BlockSpec((n_terms, tr, tc), lambda i, j: (0, i, j)), blk, blk, blk], out_specs=[blk] * 4,
        out_shape=[jax.ShapeDtypeStruct((r, c), F32)] * 4, compiler_params=_cparams(("parallel", "parallel")),
    )(recv, w, m, v)


def _tile2d(r, c, cap=256):
    if r <= cap:
        return r, c
    for t in range(cap - cap % BF16_ROWS, 0, -BF16_ROWS):
        if r % t == 0:
            return t, c
    return r, _pick(c, cap)


def _pack(pieces):
    total = sum(p.shape[0] for p in pieces)
    pad = (-total) % (8 * LANES)
    flat = jnp.concatenate(list(pieces) + [jnp.zeros((pad,), F32)])
    return flat.reshape(-1, LANES)


def _unpack(flat, sizes):
    flat = flat.reshape(-1)
    out, o = [], 0
    for n in sizes:
        out.append(flat[o:o + n])
        o += n
    return out


def _prepare_weights(full, vec, dims):
    rest = {n: t for n, t in full.items() if n != "w_in"}
    return {"w_in_t": _prepare_w_in(full["w_in"], dims), **_prepare_rest(rest, dims), **_prepare_vectors(vec, dims)}


def _prepare_w_in(slabs, dims):
    D = dims["D"]
    flat = slabs.reshape(-1, D)
    parts, pos = [], 0
    for orig_off, width, perm_off in sorted(dims["segs"], key=lambda t: t[2]):
        if perm_off > pos:
            parts.append(jnp.zeros((perm_off - pos, D), BF16))
        parts.append(flat[orig_off:orig_off + width])
        pos = perm_off + width
    if dims["d_in_perm"] > pos:
        parts.append(jnp.zeros((dims["d_in_perm"] - pos, D), BF16))
    return jnp.concatenate(parts, axis=0)


def _prepare_rest(full, dims):
    hm, hr, hn, rank = dims["hm"], dims["hr"], dims["hn"], dims["rank"]
    QR, KVR = dims["QR"], dims["KVR"]
    RW, TAIL = hr * hn, dims["TAIL"]
    full = {n: (t.reshape(-1, t.shape[2]) if n in _ROW_SHARDED + _TRANSPOSED
                else t.transpose(1, 0, 2).reshape(t.shape[1], -1)) for n, t in full.items()}
    wq = full["mla_wq_b"].reshape(hm, NOPE + ROPE, QR)
    wq = jnp.concatenate([wq, jnp.zeros((hm, QHEAD - NOPE - ROPE, QR), BF16)], axis=1).reshape(hm * QHEAD, QR)
    wkv = full["mla_wkv_b"].reshape(KVR, hm, 2, NOPE).transpose(0, 2, 1, 3).reshape(KVR, 2 * hm * NOPE)
    z = lambda rows: jnp.zeros((rows, RW), BF16)
    f = lambda nme: full[nme]
    split = ROPE + 2 * rank
    assert split % LANES == 0, split
    w2cat = jnp.concatenate([
        jnp.concatenate([z(ROPE), f("rwkv_w2_f"), z(rank)], axis=0),
        jnp.concatenate([z(ROPE + rank), f("rwkv_w2_b")], axis=0)], axis=1)
    a2cat = jnp.concatenate([
        jnp.concatenate([f("rwkv_a2_f"), z(TAIL - split - rank)], axis=0),
        jnp.concatenate([z(rank), f("rwkv_a2_b"), z(TAIL - split - 2 * rank)], axis=0)], axis=1)
    return dict(wq_b_t=wq, wkv_b=wkv, w2cat=w2cat, a2cat=a2cat, w_br_mla=full["w_br_mla"],
                w_br_rwkv=full["w_br_rwkv"], w_out=full["w_out"])


def _prepare_vectors(vec, dims):
    rank, RW, TAIL = dims["rank"], dims["hr"] * dims["hn"], dims["TAIL"]
    mu = vec["rwkv_mu"]
    mu_p = jnp.concatenate([mu[:3 * RW], jnp.zeros((ROPE,), F32), mu[3 * RW:],
                            jnp.zeros((TAIL - ROPE - 4 * rank,), F32)])
    row = lambda t: t.reshape(1, -1)
    return dict(
        mu=row(mu_p), g_pre=row(vec["g_pre"]), g_post=row(vec["g_post"]), mla_q_norm=row(vec["mla_q_norm"]),
        mla_kv_norm=row(vec["mla_kv_norm"]), w0_f=row(vec["rwkv_w0_f"]), w0_b=row(vec["rwkv_w0_b"]),
        a0_f=row(vec["rwkv_a0_f"]), a0_b=row(vec["rwkv_a0_b"]), k_k=row(vec["rwkv_k_k"]), k_a=row(vec["rwkv_k_a"]),
        r_k=row(vec["rwkv_r_k"]), gn_g=row(vec["rwkv_gn_g"]), gn_b=row(vec["rwkv_gn_b"]))


def _restore_grads(g, dims):
    return {"w_in": _restore_w_in(g["w_in"], dims), **_restore_rest(g, dims), **_restore_vectors(g, dims)}


def _restore_w_in(gw, dims):
    parts = [gw[perm_off:perm_off + width] for _, width, perm_off in sorted(dims["segs"])]
    return jnp.concatenate(parts, axis=0).reshape(N_DEV, dims["d_in"] // N_DEV, gw.shape[1])


def _restore_rest(g, dims):
    hm, hr, hn, rank = dims["hm"], dims["hr"], dims["hn"], dims["rank"]
    QR, KVR, RW = dims["QR"], dims["KVR"], hr * hn
    wq = g["wq_b"].reshape(hm, QHEAD, QR)[:, :NOPE + ROPE].reshape(N_DEV, -1, QR)
    wkv = g["wkv_b"].reshape(KVR, 2, hm, NOPE).transpose(0, 2, 1, 3).reshape(KVR, 2 * hm * NOPE)
    lo = lambda t, first, half: t[first:first + rank, half * RW:(half + 1) * RW].astype(BF16)
    cols = lambda t: t.reshape(t.shape[0], N_DEV, -1).transpose(1, 0, 2)
    return dict(
        mla_wq_b=wq, mla_wkv_b=cols(wkv), rwkv_w2_f=cols(lo(g["w2cat"], ROPE, 0)),
        rwkv_w2_b=cols(lo(g["w2cat"], ROPE + rank, 1)), rwkv_a2_f=cols(lo(g["a2cat"], 0, 0)),
        rwkv_a2_b=cols(lo(g["a2cat"], rank, 1)), w_br_mla=cols(g["w_br_mla"]), w_br_rwkv=cols(g["w_br_rwkv"]),
        w_out=g["w_out"].reshape(N_DEV, -1, g["w_out"].shape[1]))


def _restore_vectors(g, dims):
    rank, RW = dims["rank"], dims["hr"] * dims["hn"]
    mu = g["mu"][0]
    out = dict(
        rwkv_mu=jnp.concatenate([mu[:3 * RW], mu[3 * RW + ROPE:3 * RW + ROPE + 4 * rank]]),
        g_pre=g["g_pre"][0], g_post=g["g_post"][0], mla_q_norm=g["mla_q_norm"][0], mla_kv_norm=g["mla_kv_norm"][0],
        rwkv_w0_f=g["w0_f"][0], rwkv_w0_b=g["w0_b"][0], rwkv_a0_f=g["a0_f"][0], rwkv_a0_b=g["a0_b"][0],
        rwkv_k_k=g["k_k"][0], rwkv_k_a=g["k_a"][0], rwkv_r_k=g["r_k"][0], rwkv_gn_g=g["gn_g"][0],
        rwkv_gn_b=g["gn_b"][0])
    return out


def _dims(inp):
    D = inp["x"].shape[-1]
    QR, KVR = inp["mla_q_norm"].shape[0], inp["mla_kv_norm"].shape[0]
    hm = inp["mla_wq_b"].shape[1] * N_DEV // (NOPE + ROPE)
    hr, hn = inp["rwkv_r_k"].shape
    rank = inp["rwkv_w2_f"].shape[0]
    MW, RW = hm * VDIM, hr * hn
    TAIL = -(-(ROPE + 4 * rank) // LANES) * LANES
    orig, o = {}, 0
    for nme, w in (("q_a", QR), ("kv_a", KVR), ("k_rope", ROPE), ("rkv", 3 * RW), ("lora", 4 * rank), ("z_m", MW),
                   ("z_r", RW), ("gate_m", D), ("gate_r", D)):
        orig[nme] = (o, w)
        o += w
    assert o == inp["w_in"].shape[1] * N_DEV
    lay, d_in_perm = _layout(D, MW, RW, TAIL, QR, KVR)
    perm_off = dict(q_a=lay["q_a"][0], kv_a=lay["kv_a"][0], k_rope=lay["tail"][0], rkv=lay["r"][0],
                    lora=lay["tail"][0] + ROPE, z_m=lay["z_m"][0], z_r=lay["z_r"][0], gate_m=lay["gate_m"][0],
                    gate_r=lay["gate_r"][0])
    segs = [(orig[nme][0], orig[nme][1], perm_off[nme]) for nme in orig]
    return dict(D=D, QR=QR, KVR=KVR, hm=hm, hr=hr, hn=hn, rank=rank, TAIL=TAIL, segs=segs, d_in=o,
                d_in_perm=d_in_perm)


def kernel(x, g_pre, w_in, mla_q_norm, mla_wq_b, mla_kv_norm, mla_wkv_b, rwkv_mu, rwkv_w0_f, rwkv_w2_f, rwkv_w0_b, rwkv_w2_b, rwkv_a0_f, rwkv_a2_f, rwkv_a0_b, rwkv_a2_b, rwkv_k_k, rwkv_k_a, rwkv_r_k, rwkv_gn_g, rwkv_gn_b, w_br_mla, w_br_rwkv, w_out, g_post, loss_target, m_g_pre, m_w_in, m_mla_q_norm, m_mla_wq_b, m_mla_kv_norm, m_mla_wkv_b, m_rwkv_mu, m_rwkv_w0_f, m_rwkv_w2_f, m_rwkv_w0_b, m_rwkv_w2_b, m_rwkv_a0_f, m_rwkv_a2_f, m_rwkv_a0_b, m_rwkv_a2_b, m_rwkv_k_k, m_rwkv_k_a, m_rwkv_r_k, m_rwkv_gn_g, m_rwkv_gn_b, m_w_br_mla, m_w_br_rwkv, m_w_out, m_g_post, v_g_pre, v_w_in, v_mla_q_norm, v_mla_wq_b, v_mla_kv_norm, v_mla_wkv_b, v_rwkv_mu, v_rwkv_w0_f, v_rwkv_w2_f, v_rwkv_w0_b, v_rwkv_w2_b, v_rwkv_a0_f, v_rwkv_a2_f, v_rwkv_a0_b, v_rwkv_a2_b, v_rwkv_k_k, v_rwkv_k_a, v_rwkv_r_k, v_rwkv_gn_g, v_rwkv_gn_b, v_w_br_mla, v_w_br_rwkv, v_w_out, v_g_post):
    inp = dict(locals())
    dims = _dims(inp)
    stored = lambda t, n: t.T if n in _TRANSPOSED else t
    assert _MATS[0] == "w_in"
    shards = [stored(inp[n], n).astype(BF16) for n in _MATS]
    core = lax.axis_index("c").astype(jnp.int32).reshape(1)
    (w_in_slabs,) = _run_exchange(_gather_plan(shards[:1]), name="gather_w_in")
    W = {"w_in_t": _prepare_w_in(w_in_slabs, dims), **_prepare_vectors({n: inp[n] for n in _VECS}, dims)}
    loss, grad_x, g, recv_rest = _local_grads(x[0], loss_target[0], W, dims, exchange=(shards[1:], core))

    new = {}
    *recv_rest, got = recv_rest
    g_w_in, g = g["w_in"], _restore_vectors(g, dims)
    vsizes = [inp[n].size for n in _VECS] + [1]
    vflat = lambda prefix, src, last: _pack([src[prefix + n].reshape(-1) for n in _VECS] + [last])
    one = jnp.zeros((1,), F32)
    recv_w_in, vrecv = _run_exchange(
        _join_plans(_chip_exchange_plan([_pair_add(core, g_w_in, got, name="pair_add_w_in")]),
                    _direct_gather_plan(vflat("", g, loss.reshape(1)))), name="scatter_w_in")
    for n, t in zip(_MATS, [recv_w_in] + recv_rest):
        out = _adamw(t, stored(inp[n], n), stored(inp["m_" + n], n), stored(inp["v_" + n], n), name="adamw_" + n)
        new[n] = [stored(o, n) for o in out]

    vout = _adamw(vrecv, vflat("", inp, one), vflat("m_", inp, one), vflat("v_", inp, one), name="adamw_vectors")
    vparts = [_unpack(t, vsizes) for t in vout]
    for i, n in enumerate(_VECS):
        new[n] = [vp[i].reshape(inp[n].shape) for vp in vparts]
    loss = vparts[0][-1].reshape(())

    outs = [loss, grad_x[None]]
    for k in range(4):
        outs += [new[n][k] for n in _WEIGHTS]
    return tuple(outs)
```

```python
import functools
import math

import jax
import jax.numpy as jnp
from jax import lax
from jax.experimental import pallas as pl
from jax.experimental.pallas import tpu as pltpu

F32 = jnp.float32
BF16 = jnp.bfloat16

N_DEV = 8
LANES = 128
BF16_ROWS = 16
NOPE, ROPE, VDIM = 128, 64, 128
QHEAD = 256
ROPE_THETA = 10000.0
NORM_EPS = 1e-6
GN_EPS = 64e-5
CHUNK = 64
SUB = 16
VMEM_LIMIT = 56 * 1024 * 1024

ADAM_LR, ADAM_B1, ADAM_B2, ADAM_EPS, ADAM_WD, ADAM_STEP = 0.001, 0.9, 0.999, 1e-08, 0.01, 10


def _cparams(sem):
    return pltpu.CompilerParams(dimension_semantics=sem, vmem_limit_bytes=VMEM_LIMIT)


def _pick(n, cap):
    if n <= cap:
        return n
    for t in range(cap - cap % LANES, 0, -LANES):
        if n % t == 0:
            return t
    raise ValueError(f"no tile for {n} under {cap}")


def _mm(a, b, *, ta=False, tb=False, out_dtype=F32, name, tm_cap=1024, tn_cap=512, tk_cap=2048, ride=None):
    K, M = a.shape if ta else a.shape[::-1]
    N = b.shape[0] if tb else b.shape[1]
    assert (b.shape[1] if tb else b.shape[0]) == K, (a.shape, b.shape, ta, tb)
    tm, tn, tk = _pick(M, tm_cap), _pick(N, tn_cap), _pick(K, tk_cap)
    nj, nk = N // tn, K // tk
    steps = (M // tm) * nj * nk
    dn = (((0 if ta else 1,), (1 if tb else 0,)), ((), ()))
    srcs, extra_shapes, sem_shapes, phases = ride if ride else ((), (), (), None)
    n_src, n_extra = len(srcs), len(extra_shapes)

    def body(*refs):
        a_ref, b_ref, o_ref = refs[0], refs[1], refs[2 + n_src]
        acc_ref = refs[3 + n_src + n_extra]
        k = pl.program_id(2)
        if ride:
            step = (pl.program_id(0) * nj + pl.program_id(1)) * nk + k
            first, middle, last = phases(refs[2:2 + n_src], refs[3 + n_src:3 + n_src + n_extra],
                                         refs[4 + n_src + n_extra:])
            pl.when(step == 0)(first)
            pl.when(step == (steps * 15) // 16)(middle)
        p = lax.dot_general(a_ref[...], b_ref[...], dn, preferred_element_type=F32)

        @pl.when(k == 0)
        def _():
            acc_ref[...] = p

        @pl.when(k > 0)
        def _():
            acc_ref[...] += p

        @pl.when(k == nk - 1)
        def _():
            o_ref[...] = acc_ref[...].astype(out_dtype)

        if ride:
            pl.when(step == steps - 1)(last)

    a_spec = pl.BlockSpec((tk, tm), lambda i, j, k: (k, i)) if ta else pl.BlockSpec((tm, tk), lambda i, j, k: (i, k))
    b_spec = pl.BlockSpec((tn, tk), lambda i, j, k: (j, k)) if tb else pl.BlockSpec((tk, tn), lambda i, j, k: (k, j))
    hbm = pl.BlockSpec(memory_space=pl.ANY)
    out = pl.pallas_call(
        body, name=name, grid=(M // tm, nj, nk),
        in_specs=[a_spec, b_spec] + [hbm] * n_src,
        out_specs=[pl.BlockSpec((tm, tn), lambda i, j, k: (i, j))] + [hbm] * n_extra,
        out_shape=[jax.ShapeDtypeStruct((M, N), out_dtype)] + list(extra_shapes),
        scratch_shapes=[pltpu.VMEM((tm, tn), F32)] + [pltpu.SemaphoreType.DMA(s) for s in sem_shapes],
        compiler_params=_cparams(("arbitrary",) * 3 if ride else ("parallel", "parallel", "arbitrary")),
    )(a, b, *srcs)
    return out if ride else out[0]


def _view(arr, off, width):
    assert off % width == 0, (off, width)
    return (arr, off // width, width)


def _rowwise(fn, rows, params, out_rows, out_accs=(), *, tile, name, ride=None):
    rows = [r if isinstance(r, tuple) else (r, 0, r.shape[1]) for r in rows]
    S = rows[0][0].shape[0]
    T = min(tile, S)
    assert S % T == 0
    steps = S // T
    n_rows, n_par, n_out, n_acc = len(rows), len(params), len(out_rows), len(out_accs)
    into = [o[2] if len(o) == 3 else None for o in out_rows]
    carried = [t[0] for t in into if t is not None and t[0] is not None]
    srcs, extra_shapes, sem_shapes, phases = ride if ride else ((), (), (), None)
    n_in = n_rows + n_par + len(carried) + len(srcs)

    def body(*refs):
        i = pl.program_id(0)
        if ride:
            first, middle, last = phases(refs[n_in - len(srcs):n_in],
                                         refs[n_in + n_out + n_acc:n_in + n_out + n_acc + len(extra_shapes)],
                                         refs[n_in + n_out + n_acc + len(extra_shapes):])
            pl.when(i == 0)(first)
            pl.when(i == (steps * 15) // 16)(middle)
        ins = [r[...] for r in refs[:n_rows + n_par]]
        outs = fn(*ins)
        out_refs = refs[n_in:n_in + n_out + n_acc]
        for o_ref, val in zip(out_refs[:n_out], outs[:n_out]):
            o_ref[...] = val.astype(o_ref.dtype)
        for o_ref, val in zip(out_refs[n_out:], outs[n_out:]):
            @pl.when(i == 0)
            def _(o_ref=o_ref, val=val):
                o_ref[...] = val

            @pl.when(i > 0)
            def _(o_ref=o_ref, val=val):
                o_ref[...] += val
        if ride:
            pl.when(i == steps - 1)(last)

    in_specs = [pl.BlockSpec((T, w), functools.partial(lambda i, cb: (i, cb), cb=cb)) for _, cb, w in rows]
    in_specs += [pl.BlockSpec(p.shape, lambda i: (0, 0)) for p in params]
    in_specs += [pl.BlockSpec(memory_space=pl.ANY)] * len(carried)
    out_specs, out_shape, aliases = [], [], {}
    for k, (o, t) in enumerate(zip(out_rows, into)):
        w, dt = o[0], o[1]
        if t is None:
            out_specs.append(pl.BlockSpec((T, w), lambda i: (i, 0)))
            out_shape.append(jax.ShapeDtypeStruct((S, w), dt))
            continue
        buf, total, first = t
        assert first % w == 0
        out_specs.append(pl.BlockSpec((T, w), functools.partial(lambda i, cb: (i, cb), cb=first // w)))
        out_shape.append(jax.ShapeDtypeStruct((S, total), dt))
        if buf is not None:
            aliases[n_rows + n_par + len(aliases)] = k
    out_specs += [pl.BlockSpec(s, lambda i: (0, 0)) for s in out_accs]
    out_shape += [jax.ShapeDtypeStruct(s, F32) for s in out_accs]
    hbm = pl.BlockSpec(memory_space=pl.ANY)
    return pl.pallas_call(
        body, name=name, grid=(steps,), in_specs=in_specs + [hbm] * len(srcs),
        out_specs=out_specs + [hbm] * len(extra_shapes), out_shape=out_shape + list(extra_shapes),
        scratch_shapes=[pltpu.SemaphoreType.DMA(s) for s in sem_shapes],
        input_output_aliases=aliases, compiler_params=_cparams(("arbitrary",)),
    )(*[r[0] for r in rows], *params, *carried, *srcs)


def _mm_sel(x, sel2):
    hi = x.astype(BF16)
    lo = (x - hi.astype(F32)).astype(BF16)
    return jnp.dot(jnp.concatenate([hi, lo], axis=1), sel2, preferred_element_type=F32)


@jax.custom_vjp
def _sel(x, sel, sel_t):
    return _mm_sel(x, sel)


def _sel_fwd(x, sel, sel_t):
    return _mm_sel(x, sel), (sel, sel_t)


def _sel_bwd(res, ct):
    sel, sel_t = res
    return _mm_sel(ct, sel_t), jnp.zeros_like(sel), jnp.zeros_like(sel_t)


_sel.defvjp(_sel_fwd, _sel_bwd)


def _rms(x, g):
    return x * lax.rsqrt(jnp.mean(x * x, axis=-1, keepdims=True) + NORM_EPS) * g


def _sigmoid(x):
    return 0.5 * jnp.tanh(0.5 * x) + 0.5


def _silu(x):
    return x * _sigmoid(x)


def _softplus(x):
    return jnp.maximum(x, 0.0) + jnp.log(1.0 + jnp.exp(-jnp.abs(x)))


def _f_mla_norm(q_a, kv_a, qg, kvg):
    return _rms(q_a, qg), _rms(kv_a, kvg)


def _f_rope(hm, qraw, kr_in, cosx, sinx, rot, rot_t):
    def rope(t):
        return t * cosx + _sel(t, rot, rot_t) * sinx
    parts = []
    for h in range(hm):
        parts.append(qraw[:, h * QHEAD:h * QHEAD + NOPE])
        parts.append(rope(qraw[:, h * QHEAD + NOPE:(h + 1) * QHEAD]))
    return jnp.concatenate(parts, axis=1), rope(kr_in)


def _f_rwkv_pre(rw, k, tail, w0f, w0b, a0f, a0b, k_k, k_a, w2cat, a2cat, seg, seg_t):
    split = w2cat.shape[0]
    zw = jnp.dot(jnp.tanh(tail[:, :split]).astype(BF16), w2cat, preferred_element_type=F32)
    za = jnp.dot(tail[:, split:].astype(BF16), a2cat, preferred_element_type=F32)
    return _f_rwkv_core(rw, k, zw, za, w0f, w0b, a0f, a0b, k_k, k_a, seg, seg_t)


def _f_rwkv_core(rw, k, zw, za, w0f, w0b, a0f, a0b, k_k, k_a, seg, seg_t):
    lw_f = -jnp.exp(-_softplus(-(w0f + zw[:, :rw])) - 0.5)
    lw_b = -jnp.exp(-_softplus(-(w0b + zw[:, rw:])) - 0.5)
    a_f = _sigmoid(a0f + za[:, :rw])
    a_b = _sigmoid(a0b + za[:, rw:])
    kk = k * k_k
    nrm = jnp.sqrt(_sel(_sel(kk * kk, seg, seg_t), seg_t, seg))
    kk = kk / jnp.maximum(nrm, 1e-12)
    k_f = k * (1.0 + (a_f - 1.0) * k_a)
    k_b = k * (1.0 + (a_b - 1.0) * k_a)
    return lw_f, lw_b, k_f, k_b, -kk, kk * a_f, kk * a_b


def _f_post(hn, y_f, y_b, r, k_f, k_b, v, z_r, o_mla, z_m, gn_g, gn_b, r_k, seg, seg_t):
    segsum = lambda t: _sel(_sel(t, seg, seg_t), seg_t, seg)
    y = y_f + y_b
    mu = segsum(y) * (1.0 / hn)
    yc = y - mu
    var = segsum(yc * yc) * (1.0 / hn)
    yn = yc * lax.rsqrt(var + GN_EPS) * gn_g + gn_b
    bonus = segsum(r * (k_f + k_b) * r_k) * v
    return o_mla * _silu(z_m), (yn + bonus) * _silu(z_r)


def _f_merge(u_m, u_r, g_m, g_r):
    return _sigmoid(g_m) * u_m + _sigmoid(g_r) * u_r


_NN = ((2,), (1,))
_NT = ((2,), (2,))
_TN = ((1,), (1,))

_SCAN_PASSES = {"cum": 2, "gram": 3, "solve": 1, "apply": 1, "state": 1}


def _hdot_raw(passes, x, y, dims):
    dn = (dims, ((0,), (0,)))
    d = lambda p, q: lax.dot_general(p, q, dn, preferred_element_type=F32)
    xh = x.astype(BF16)
    yh = y.astype(BF16)
    if passes == 1:
        return d(xh, yh)
    yl = (y - yh.astype(F32)).astype(BF16)
    kx, ky = (1 if dims == _TN else 2), (2 if dims == _NT else 1)
    depth = x.shape[kx]
    if all(axis == 1 or depth % LANES == 0 for axis in (kx, ky)):
        if passes == 2:
            return d(jnp.concatenate([xh, xh], axis=kx), jnp.concatenate([yh, yl], axis=ky))
        xl = (x - xh.astype(F32)).astype(BF16)
        return d(jnp.concatenate([xh, xl, xh], axis=kx), jnp.concatenate([yh, yh, yl], axis=ky))
    if passes == 2:
        axis = 1 if dims == _NT else 2
        width = y.shape[axis]
        both = d(xh, jnp.concatenate([yh, yl], axis=axis))
        return both[:, :, :width] + both[:, :, width:]
    xl = (x - xh.astype(F32)).astype(BF16)
    if dims == _TN:
        return d(xh, yh) + d(xh, yl) + d(xl, yh)
    rows = x.shape[1]
    both = d(jnp.concatenate([xh, xl], axis=1), yh)
    return both[:, :rows] + both[:, rows:] + d(xh, yl)


@functools.partial(jax.custom_vjp, nondiff_argnums=(2, 3))
def _hdot_p(x, y, dims, passes):
    return _hdot_raw(passes, x, y, dims)


def _hdot_fwd(x, y, dims, passes):
    return _hdot_raw(passes, x, y, dims), (x, y)


def _hdot_bwd(dims, passes, res, ct):
    x, y = res
    if dims == _NN:
        return _hdot_raw(passes, ct, y, _NT), _hdot_raw(passes, x, ct, _TN)
    if dims == _NT:
        return _hdot_raw(passes, ct, y, _NN), _hdot_raw(passes, ct, x, _TN)
    return _hdot_raw(passes, y, ct, _NT), _hdot_raw(passes, x, ct, _NN)


_hdot_p.defvjp(_hdot_fwd, _hdot_bwd)


def _hdot(x, y, dims, kind):
    return _hdot_p(x, y, dims, _SCAN_PASSES[kind])


def _tri_solve(n_mat, x, length, blocks):
    row = lax.broadcasted_iota(jnp.int32, (length, 2 * length), 0)
    col = lax.broadcasted_iota(jnp.int32, (length, 2 * length), 1)
    col = jnp.where(col >= length, col - length, col)
    eye = (row == col).astype(F32)[None]
    diag_blk = ((row // SUB) == (col // SUB))[None]
    nd = jnp.where(diag_blk, n_mat, 0.0)
    no = n_mat - nd
    dinv = eye + nd
    p = _hdot(nd, blocks(nd), _NN, "solve")
    for k in range(int(math.log2(SUB)) - 1):
        if k == int(math.log2(SUB)) - 2:
            dinv = dinv + _hdot(dinv, blocks(p), _NN, "solve")
        else:
            both = _hdot(jnp.concatenate([dinv, p], axis=1), blocks(p), _NN, "solve")
            dinv, p = dinv + both[:, :length], both[:, length:]
    width = x.shape[2]
    both = _hdot(dinv, jnp.concatenate([blocks(x), blocks(no)], axis=2), _NN, "solve")
    u, q = both[:, :, :width], both[:, :, width:]
    for level in range(int(math.log2(length // SUB))):
        if level == int(math.log2(length // SUB)) - 1:
            u = u + _hdot(q, blocks(u), _NN, "solve")
        else:
            both = _hdot(q, jnp.concatenate([blocks(u), blocks(q)], axis=2), _NN, "solve")
            u, q = u + both[:, :, :width], both[:, :, width:]
    return u


def _rwkv_chunk(rev, s0, r, lw, k, v, a, b):
    pairs, length, width = r.shape
    hn = width // 2
    assert 2 * length == width
    row = lax.broadcasted_iota(jnp.int32, (length, length), 0)
    col = lax.broadcasted_iota(jnp.int32, (length, length), 1)
    row2 = lax.broadcasted_iota(jnp.int32, (length, 2 * length), 0)
    col2 = lax.broadcasted_iota(jnp.int32, (length, 2 * length), 1)
    col2 = jnp.where(col2 >= length, col2 - length, col2)
    if rev is None:
        half = pairs // 2
        back = lax.broadcasted_iota(jnp.int32, (pairs, length, length), 0) >= half
        back2 = lax.broadcasted_iota(jnp.int32, (pairs, length, 2 * length), 0) >= half
        ahead = jnp.where(back, (col - row)[None], (row - col)[None])
        ahead2 = jnp.where(back2, (col2 - row2)[None], (row2 - col2)[None])
        incl, strict2, incl2 = ahead >= 0, ahead2 > 0, ahead2 >= 0
    else:
        incl = ((row <= col) if rev else (row >= col))[None]
        strict2 = ((row2 < col2) if rev else (row2 > col2))[None]
        incl2 = ((row2 <= col2) if rev else (row2 >= col2))[None]
    first = (lax.broadcasted_iota(jnp.int32, (1, 1, width), 2) < hn).astype(F32)
    blocks = lambda t: jnp.concatenate([t * first, t * (1.0 - first)], axis=1)

    t_incl = jnp.broadcast_to(incl.astype(F32), (pairs, length, length))
    cum = _hdot(t_incl, lw, _NN, "cum")
    g = jnp.exp(cum)
    g_inv = jnp.exp(-cum)
    at = a * jnp.exp(cum - lw)
    rt = r * g
    bt = b * g_inv
    kt = k * g_inv
    both_rows = jnp.concatenate([at, rt], axis=1)
    gram = _hdot(both_rows, jnp.concatenate([blocks(bt), blocks(kt)], axis=1), _NT, "gram")
    a_ab = jnp.where(strict2, gram[:, :length, :width], 0.0)
    a_ak = jnp.where(strict2, gram[:, :length, width:], 0.0)
    a_rb = jnp.where(incl2, gram[:, length:, :width], 0.0)
    a_rk = jnp.where(incl2, gram[:, length:, width:], 0.0)
    from_state = _hdot(both_rows, s0, _NT, "apply")
    x = from_state[:, :length] + _hdot(a_ak, blocks(v), _NN, "apply")
    u = _tri_solve(a_ab, x, length, blocks)
    y = from_state[:, length:] + _hdot(jnp.concatenate([a_rb, a_rk], axis=2),
                                       jnp.concatenate([blocks(u), blocks(v)], axis=1), _NN, "apply")
    g_last = jnp.exp(jnp.sum(lw, axis=1, keepdims=True))
    ri = lax.broadcasted_iota(jnp.int32, (width, width), 0)
    ci = lax.broadcasted_iota(jnp.int32, (width, width), 1)
    same_head = ((ri < hn) == (ci < hn))[None]
    upd = _hdot(jnp.concatenate([u, v], axis=1), jnp.concatenate([bt, kt], axis=1), _TN, "state")
    s1 = (s0 + jnp.where(same_head, upd, 0.0)) * g_last
    return y, s1


def _split_pairs(x):
    return jnp.stack([x[:, p * LANES:(p + 1) * LANES] for p in range(x.shape[1] // LANES)])


def _merge_pairs(x):
    return jnp.concatenate([x[p] for p in range(x.shape[0])], axis=1)


def _scan_specs(views, rw, nc, rev):
    cidx = (lambda c: nc - 1 - c) if rev else (lambda c: c)
    seqs = [pl.BlockSpec((CHUNK, rw), functools.partial(lambda c, cb: (cidx(c), cb), cb=cb)) for _, cb, _ in views]
    plain = pl.BlockSpec((CHUNK, rw), lambda c: (cidx(c), 0))
    st = pl.BlockSpec((1, rw // LANES, LANES, LANES), lambda c: (cidx(c), 0, 0, 0))
    return seqs, plain, st


def _as_views(arrs, rw):
    return [t if isinstance(t, tuple) else (t, 0, rw) for t in arrs]


def _rwkv_scan_fwd(ops_f, ops_b, rw, *, name):
    S = _as_views(ops_f, rw)[0][0].shape[0]
    nc, pairs = S // CHUNK, rw // LANES
    in_specs, out_specs, arrays = [], [], []
    for rev, ops in ((False, ops_f), (True, ops_b)):
        views = _as_views(ops, rw)
        seqs, plain, st = _scan_specs(views, rw, nc, rev)
        in_specs += seqs
        out_specs += [plain, st]
        arrays += [t[0] for t in views]

    def both(refs_f, refs_b):
        return [jnp.concatenate([_split_pairs(f[...]), _split_pairs(b[...])], axis=0) for f, b in zip(refs_f, refs_b)]

    def body(*refs):
        (y_f, st_f, y_b, st_b), s_ref = refs[12:16], refs[16]

        @pl.when(pl.program_id(0) == 0)
        def _():
            s_ref[...] = jnp.zeros_like(s_ref)

        s0 = s_ref[...]
        st_f[0] = s0[:pairs]
        st_b[0] = s0[pairs:]
        y, s1 = _rwkv_chunk(None, s0, *both(refs[:6], refs[6:12]))
        y_f[...] = _merge_pairs(y[:pairs])
        y_b[...] = _merge_pairs(y[pairs:])
        s_ref[...] = s1

    return pl.pallas_call(
        body, name=name, grid=(nc,), in_specs=in_specs, out_specs=out_specs,
        out_shape=[jax.ShapeDtypeStruct((S, rw), F32), jax.ShapeDtypeStruct((nc, pairs, LANES, LANES), F32)] * 2,
        scratch_shapes=[pltpu.VMEM((2 * pairs, LANES, LANES), F32)],
        compiler_params=_cparams(("arbitrary",)),
    )(*arrays)


def _rwkv_scan_bwd(ops_f, ops_b, states_f, states_b, dy, rw, *, name):
    S = dy.shape[0]
    nc, pairs = S // CHUNK, rw // LANES
    in_specs, arrays = [], []
    for rev, ops, states in ((False, ops_f, states_f), (True, ops_b, states_b)):
        views = _as_views(list(ops) + [dy], rw)
        seqs, plain, st = _scan_specs(views, rw, nc, not rev)
        in_specs += seqs + [st]
        arrays += [t[0] for t in views] + [states]
    out_specs = []
    for rev in (False, True):
        out_specs += [_scan_specs([], rw, nc, not rev)[1]] * 6

    def both(refs_f, refs_b):
        return [jnp.concatenate([_split_pairs(f[...]), _split_pairs(b[...])], axis=0) for f, b in zip(refs_f, refs_b)]

    def body(*refs):
        ds_ref = refs[28]

        @pl.when(pl.program_id(0) == 0)
        def _():
            ds_ref[...] = jnp.zeros_like(ds_ref)

        s0 = jnp.concatenate([refs[7][0], refs[15][0]], axis=0)
        _, vjp = jax.vjp(functools.partial(_rwkv_chunk, None), s0, *both(refs[:6], refs[8:14]))
        (dy,) = both(refs[6:7], refs[14:15])
        grads = vjp((dy, ds_ref[...]))
        ds_ref[...] = grads[0]
        for o_f, o_b, gval in zip(refs[16:22], refs[22:28], grads[1:]):
            o_f[...] = _merge_pairs(gval[:pairs])
            o_b[...] = _merge_pairs(gval[pairs:])

    return pl.pallas_call(
        body, name=name, grid=(nc,), in_specs=in_specs, out_specs=out_specs,
        out_shape=[jax.ShapeDtypeStruct((S, rw), F32)] * 12,
        scratch_shapes=[pltpu.VMEM((2 * pairs, LANES, LANES), F32)],
        compiler_params=_cparams(("arbitrary",)),
    )(*arrays)


def _shift_lerp(x_view, mu, d=None, into=None, *, name):
    arr, off, width = x_view
    S = arr.shape[0]
    cb = _pick(width, 512)
    assert off % cb == 0

    def cshift(t):
        rows = lax.broadcasted_iota(jnp.int32, t.shape, 0)
        prev = jnp.where(rows == 0, 0.0, pltpu.roll(t, 1, 0))
        nxt = jnp.where(rows == S - 1, 0.0, pltpu.roll(t, S - 1, 0))
        return 0.5 * (prev + nxt)

    def fwd_body(x_ref, mu_ref, o_ref):
        x = x_ref[...]
        o_ref[...] = x + mu_ref[...] * (cshift(x) - x)

    def bwd_body(x_ref, mu_ref, d_ref, _, dx_ref, dmu_ref):
        x, m, dd = x_ref[...], mu_ref[...], d_ref[...]
        gm = m * dd
        dx_ref[...] = (dd - gm + cshift(gm)).astype(dx_ref.dtype)
        dmu_ref[...] = jnp.sum(dd * (cshift(x) - x), axis=0, keepdims=True)

    x_spec = pl.BlockSpec((S, cb), lambda j: (0, off // cb + j))
    blk = pl.BlockSpec((S, cb), lambda j: (0, j))
    vec = pl.BlockSpec((1, cb), lambda j: (0, j))
    if d is None:
        return pl.pallas_call(
            fwd_body, name=name, grid=(width // cb,), in_specs=[x_spec, vec], out_specs=blk,
            out_shape=jax.ShapeDtypeStruct((S, width), F32), compiler_params=_cparams(("parallel",)),
        )(arr, mu)
    buf, first = into
    assert first % cb == 0
    return pl.pallas_call(
        bwd_body, name=name, grid=(width // cb,),
        in_specs=[x_spec, vec, blk, pl.BlockSpec(memory_space=pl.ANY)],
        out_specs=[pl.BlockSpec((S, cb), lambda j: (0, first // cb + j)), vec],
        out_shape=[jax.ShapeDtypeStruct(buf.shape, buf.dtype), jax.ShapeDtypeStruct((1, width), F32)],
        input_output_aliases={3: 0}, compiler_params=_cparams(("parallel",)),
    )(arr, mu, d, buf)


def _attention_fwd(qfull, kv, kr, hm, scale, *, tq, name):
    S = qfull.shape[0]
    nt = (((1,), (1,)), ((), ()))

    def body(q_ref, kn_ref, kr_ref, v_ref, o_ref, lse_ref, k_scr):
        _head_keys(kn_ref, kr_ref, k_scr)
        s = lax.dot_general(q_ref[...], k_scr[...], nt, preferred_element_type=F32)
        m = jnp.max(s, axis=-1, keepdims=True)
        p = jnp.exp((s - m) * scale)
        l = jnp.sum(p, axis=-1, keepdims=True)
        o_ref[...] = jnp.dot(p.astype(BF16), v_ref[...], preferred_element_type=F32) * (1.0 / l)
        lse_ref[...] = jnp.broadcast_to(m * scale + jnp.log(l), lse_ref.shape)

    oblk = pl.BlockSpec((tq, VDIM), lambda h, i: (i, h))
    return pl.pallas_call(
        body, name=name, grid=(hm, S // tq),
        in_specs=[pl.BlockSpec((tq, QHEAD), lambda h, i: (i, h)),
                  pl.BlockSpec((S, NOPE), lambda h, i: (0, h)),
                  pl.BlockSpec((S, LANES), lambda h, i: (0, 0)),
                  pl.BlockSpec((S, VDIM), lambda h, i: (0, hm + h))],
        out_specs=[oblk, oblk],
        out_shape=[jax.ShapeDtypeStruct((S, hm * VDIM), F32)] * 2,
        scratch_shapes=[pltpu.VMEM((S, QHEAD), BF16)],
        compiler_params=_cparams(("parallel", "arbitrary")),
    )(qfull, kv, kr, kv)


def _head_keys(kn_ref, kr_ref, k_scr):
    @pl.when(pl.program_id(1) == 0)
    def _():
        k_scr[:, :NOPE] = kn_ref[...]
        k_scr[:, NOPE:] = kr_ref[...]


def _attention_bwd(qfull, kv, kr, o, lse, d_o, hm, scale, *, tq, name):
    S = qfull.shape[0]
    tq = min(tq, S)
    nq = S // tq
    tn = (((0,), (0,)), ((), ()))
    nt = (((1,), (1,)), ((), ()))

    def body(q_ref, kn_ref, kr_ref, v_ref, o_ref, lse_ref, do_ref, dq_ref, dk_ref, dv_ref, k_scr):
        _head_keys(kn_ref, kr_ref, k_scr)
        s = lax.dot_general(q_ref[...], k_scr[...], nt, preferred_element_type=F32)
        p = jnp.exp(s * scale - lse_ref[:, 0:1])
        d_out = do_ref[...]
        delta = jnp.sum(d_out * o_ref[...], axis=-1, keepdims=True)
        d_out = d_out.astype(BF16)
        dp = lax.dot_general(d_out, v_ref[...], nt, preferred_element_type=F32)
        ds = (p * (dp - delta)).astype(BF16)
        dq_ref[...] = jnp.dot(ds, k_scr[...], preferred_element_type=F32) * scale
        dv = lax.dot_general(p.astype(BF16), d_out, tn, preferred_element_type=F32)
        dk = lax.dot_general(ds, q_ref[...], tn, preferred_element_type=F32)
        i = pl.program_id(1)
        for ref, val in ((dk_ref, dk), (dv_ref, dv)):
            @pl.when(i == 0)
            def _(ref=ref, val=val):
                ref[...] = val

            @pl.when(i > 0)
            def _(ref=ref, val=val):
                ref[...] += val

        @pl.when(i == nq - 1)
        def _():
            dk_ref[...] = dk_ref[...] * scale

    qblk = pl.BlockSpec((tq, QHEAD), lambda h, i: (i, h))
    oblk = pl.BlockSpec((tq, VDIM), lambda h, i: (i, h))
    return pl.pallas_call(
        body, name=name, grid=(hm, nq),
        in_specs=[qblk,
                  pl.BlockSpec((S, NOPE), lambda h, i: (0, h)),
                  pl.BlockSpec((S, LANES), lambda h, i: (0, 0)),
                  pl.BlockSpec((S, VDIM), lambda h, i: (0, hm + h)),
                  oblk, oblk, oblk],
        out_specs=[qblk, pl.BlockSpec((S, QHEAD), lambda h, i: (0, h)), pl.BlockSpec((S, VDIM), lambda h, i: (0, h))],
        out_shape=[jax.ShapeDtypeStruct((S, hm * QHEAD), F32), jax.ShapeDtypeStruct((S, hm * QHEAD), F32),
                   jax.ShapeDtypeStruct((S, hm * VDIM), F32)],
        scratch_shapes=[pltpu.VMEM((S, QHEAD), BF16)],
        compiler_params=_cparams(("parallel", "arbitrary")),
    )(qfull, kv, kr, kv, o, lse, d_o)


def _layout(D, MW, RW, TAIL, QR, KVR):
    names = ["gate_m", "gate_r", "z_m", "z_r", "q_a", "kv_a", "r", "k", "v", "tail"]
    widths = [D, D, MW, RW, QR, KVR, RW, RW, RW, TAIL]
    offs, o = {}, 0
    for nme, w in zip(names, widths):
        assert o % w == 0, (nme, o, w)
        offs[nme] = (o, w)
        o += w
    return offs, o


def _local_grads(x, target, W, dims, exchange=None):
    S, D = x.shape
    hm, hr, hn, rank = dims["hm"], dims["hr"], dims["hn"], dims["rank"]
    MW, RW = hm * VDIM, hr * hn
    TAIL = dims["TAIL"]
    QR, KVR = W["mla_q_norm"].shape[1], W["mla_kv_norm"].shape[1]
    lay, d_in = _layout(D, MW, RW, TAIL, QR, KVR)
    T = 256
    scale = (NOPE + ROPE) ** -0.5
    col = lambda arr, nme: _view(arr, *lay[nme])

    pos = jnp.arange(S, dtype=F32)
    inv_freq = jnp.power(ROPE_THETA, -jnp.arange(0, ROPE, 2, dtype=F32) / ROPE)
    ang = pos[:, None] * inv_freq[None, :]
    zpad = jnp.zeros((S, LANES - ROPE), F32)
    cosx = jnp.concatenate([jnp.cos(ang), jnp.cos(ang), zpad], axis=1)
    sinx = jnp.concatenate([jnp.sin(ang), jnp.sin(ang), zpad], axis=1)
    ri, ci = jnp.arange(LANES)[:, None], jnp.arange(LANES)[None, :]
    half = ROPE // 2
    rot = (jnp.where((ri == ci - half) & (ci >= half) & (ci < ROPE), 1.0, 0.0)
           - jnp.where((ri == ci + half) & (ci < half), 1.0, 0.0)).astype(BF16)
    seg = (jnp.arange(RW)[:, None] // hn == jnp.arange(LANES)[None, :]).astype(BF16)
    stacked = lambda t: jnp.concatenate([t, t], axis=0)
    rot, rot_t, seg, seg_t = stacked(rot), stacked(rot.T), stacked(seg), stacked(seg.T)

    (h,) = _rowwise(lambda xb, g: (_rms(xb, g),), [x], [W["g_pre"]], [(D, BF16)], tile=2 * T, name="pre_norm")
    if exchange is None:
        proj = _mm(h, W["w_in_t"], tb=True, name="in_proj")
    else:
        proj, *slabs = _mm(h, W["w_in_t"], tb=True, ride=_gather_plan(exchange[0]), name="in_proj")
        W = {**W, **_prepare_rest(dict(zip(_MATS[1:], slabs)), dims)}

    qn, kvn = _rowwise(_f_mla_norm, [col(proj, "q_a"), col(proj, "kv_a")], [W["mla_q_norm"], W["mla_kv_norm"]],
                       [(QR, BF16), (KVR, BF16)], tile=2 * T, name="mla_norm")
    qraw = _mm(qn, W["wq_b_t"], tb=True, name="q_up")
    kv = _mm(kvn, W["wkv_b"], out_dtype=BF16, name="kv_up")
    kr_view = _view(proj, lay["tail"][0], LANES)
    qfull, kr = _rowwise(functools.partial(_f_rope, hm), [qraw, kr_view, cosx, sinx], [rot, rot_t],
                         [(hm * QHEAD, BF16), (LANES, BF16)], tile=2 * T, name="rope")
    o_mla, lse = _attention_fwd(qfull, kv, kr, hm, scale, tq=T, name="attn_fwd")

    shift_view = (proj, lay["r"][0], 3 * RW + TAIL)
    rl = _shift_lerp(shift_view, W["mu"], name="shift_fwd")
    rl_r, rl_k, rl_v = _view(rl, 0, RW), _view(rl, RW, RW), _view(rl, 2 * RW, RW)
    rl_tail = _view(rl, 3 * RW, TAIL)
    pre_params = [W["w0_f"], W["w0_b"], W["a0_f"], W["a0_b"], W["k_k"], W["k_a"], W["w2cat"], W["a2cat"], seg, seg_t]
    pre_fn = functools.partial(_f_rwkv_pre, RW)
    lw_f, lw_b, k_f, k_b, a_n, b_f, b_b = _rowwise(pre_fn, [rl_k, rl_tail], pre_params, [(RW, F32)] * 7, tile=2 * T,
                                                    name="rwkv_pre")
    ops_f = (rl_r, lw_f, k_f, rl_v, a_n, b_f)
    ops_b = (rl_r, lw_b, k_b, rl_v, a_n, b_b)
    y_f, st_f, y_b, st_b = _rwkv_scan_fwd(ops_f, ops_b, RW, name="scan_fwd")

    post_fn = functools.partial(_f_post, hn)
    post_rows = [y_f, y_b, rl_r, k_f, k_b, rl_v, col(proj, "z_r"), o_mla, col(proj, "z_m")]
    post_params = [W["gn_g"], W["gn_b"], W["r_k"], seg, seg_t]
    ymg, yrg = _rowwise(post_fn, post_rows, post_params, [(MW, BF16), (RW, BF16)], tile=T, name="post")
    u_m = _mm(ymg, W["w_br_mla"], name="br_mla")
    u_r = _mm(yrg, W["w_br_rwkv"], name="br_rwkv")
    merge_rows = [u_m, u_r, col(proj, "gate_m"), col(proj, "gate_r")]
    (merged,) = _rowwise(lambda *t: (_f_merge(*t),), merge_rows, [], [(D, BF16)], tile=T, name="merge")
    out = _mm(merged, W["w_out"], name="out_proj")

    def head(ob, xb, tb, g):
        yn, vjp = jax.vjp(_rms, ob, g)
        err = xb + yn - tb
        dy = err * (1.0 / D)
        d_ob, d_g = vjp(dy)
        loss = jnp.broadcast_to(0.5 * jnp.sum(err * err) * (1.0 / D), (1, LANES))
        return dy, d_ob, loss, d_g

    dy, d_out, loss, g_g_post = _rowwise(head, [out, x, target], [W["g_post"]], [(D, F32), (D, BF16)],
                                         [(1, LANES), (1, D)], tile=2 * T, name="head")
    d_merged = _mm(d_out, W["w_out"], tb=True, name="d_merged")
    g_w_out = _mm(merged, d_out, ta=True, out_dtype=BF16, name="g_w_out")

    def merge_bwd(u_m_b, u_r_b, g_m_b, g_r_b, dm):
        _, vjp = jax.vjp(_f_merge, u_m_b, u_r_b, g_m_b, g_r_b)
        du_m, du_r, dg_m, dg_r = vjp(dm)
        return du_m, du_r, jnp.concatenate([dg_m, dg_r], axis=1)

    d_u_m, d_u_r, d_proj = _rowwise(merge_bwd, merge_rows + [d_merged], [],
                                    [(D, BF16), (D, BF16), (2 * D, BF16, (None, d_in, lay["gate_m"][0]))], tile=T,
                                    name="merge_bwd")
    d_ymg = _mm(d_u_m, W["w_br_mla"], tb=True, name="d_ymg")
    d_yrg = _mm(d_u_r, W["w_br_rwkv"], tb=True, name="d_yrg")
    g_w_br_mla = _mm(ymg, d_u_m, ta=True, out_dtype=BF16, name="g_w_br_mla")
    g_w_br_rwkv = _mm(yrg, d_u_r, ta=True, out_dtype=BF16, name="g_w_br_rwkv")

    def post_bwd(*args):
        nr = len(post_rows)
        prim, dm, dr = args[:nr] + args[nr + 2:], args[nr], args[nr + 1]
        _, vjp = jax.vjp(post_fn, *prim)
        g = vjp((dm, dr))
        return g[0], g[2], g[3], g[5], g[7], jnp.concatenate([g[8], g[6]], axis=1), g[9], g[10], g[11]

    (d_y, d_r_bonus, d_k_bonus, d_v_bonus, d_o, d_proj, g_gn_g, g_gn_b, g_r_k) = _rowwise(
        post_bwd, post_rows + [d_ymg, d_yrg], post_params,
        [(RW, F32), (RW, F32), (RW, F32), (RW, F32), (MW, F32), (MW + RW, BF16, (d_proj, d_in, lay["z_m"][0]))],
        [(1, RW)] * 3, tile=T, name="post_bwd")

    dscan = _rwkv_scan_bwd(ops_f, ops_b, st_f, st_b, d_y, RW, name="scan_bwd")
    dsc = {"f": dscan[:6], "b": dscan[6:]}

    d_q_att, d_k_att, d_v_att = _attention_bwd(qfull, kv, kr, o_mla, lse, d_o, hm, scale, tq=4 * T, name="attn_bwd")

    def rope_bwd(qraw_b, kr_in, cos_b, sin_b, dq_b, dk_b, dv_b, rot_b, rot_t_b):
        _, vjp = jax.vjp(lambda q_, k_: _f_rope(hm, q_, k_, cos_b, sin_b, rot_b, rot_t_b), qraw_b, kr_in)
        dkn = jnp.concatenate([dk_b[:, hh * QHEAD:hh * QHEAD + NOPE] for hh in range(hm)], axis=1)
        dkr = dk_b[:, NOPE:QHEAD]
        for hh in range(1, hm):
            dkr = dkr + dk_b[:, hh * QHEAD + NOPE:(hh + 1) * QHEAD]
        d_qraw, d_kr_in = vjp((dq_b, dkr))
        return d_qraw, jnp.concatenate([dkn, dv_b], axis=1), d_kr_in

    d_qraw, d_kv, d_kr_in = _rowwise(rope_bwd, [qraw, kr_view, cosx, sinx, d_q_att, d_k_att, d_v_att],
                                     [rot, rot_t], [(hm * QHEAD, BF16), (2 * MW, BF16), (LANES, F32)], tile=T,
                                     name="rope_bwd")
    d_qnorm = _mm(d_qraw, W["wq_b_t"], name="d_qn")
    d_kvnorm = _mm(d_kv, W["wkv_b"], tb=True, name="d_kvn")
    g_wq_b = _mm(d_qraw, qn, ta=True, out_dtype=BF16, name="g_wq_b")
    g_wkv_b = _mm(kvn, d_kv, ta=True, out_dtype=BF16, name="g_wkv_b")

    def mla_norm_bwd(q_a, kv_a, qg, kvg, dq, dk):
        _, vjp = jax.vjp(_f_mla_norm, q_a, kv_a, qg, kvg)
        d_q_a, d_kv_a, d_qg, d_kvg = vjp((dq, dk))
        return jnp.concatenate([d_q_a, d_kv_a], axis=1), d_qg, d_kvg

    d_proj, g_q_norm, g_kv_norm = _rowwise(
        lambda q_a, kv_a, dq, dk, qg, kvg: mla_norm_bwd(q_a, kv_a, qg, kvg, dq, dk),
        [col(proj, "q_a"), col(proj, "kv_a"), d_qnorm, d_kvnorm], [W["mla_q_norm"], W["mla_kv_norm"]],
        [(QR + KVR, BF16, (d_proj, d_in, lay["q_a"][0]))], [(1, QR), (1, KVR)], tile=2 * T, name="mla_norm_bwd")

    def pre_bwd(k_b_, tail_b, dlwf, dlwb, dkf, dkb, dkbon, daf, dab, dbf, dbb, drf, drb, drbon, dvf, dvb, dvbon,
                dkr, *params):
        w2, a2 = params[6], params[7]
        nt, tn = (((1,), (1,)), ((), ())), (((0,), (0,)), ((), ()))
        split = w2.shape[0]
        th = jnp.tanh(tail_b[:, :split])
        th_b, tail_h = th.astype(BF16), tail_b[:, split:].astype(BF16)
        zw = jnp.dot(th_b, w2, preferred_element_type=F32)
        za = jnp.dot(tail_h, a2, preferred_element_type=F32)
        _, vjp = jax.vjp(functools.partial(_f_rwkv_core, RW), k_b_, zw, za, *params[:6], params[8], params[9])
        g = vjp((dlwf, dlwb, dkf + dkbon, dkb + dkbon, daf + dab, dbf, dbb))
        d_zw, d_za = g[1].astype(BF16), g[2].astype(BF16)
        d_tail = (jnp.concatenate([lax.dot_general(d_zw, w2, nt, preferred_element_type=F32) * (1.0 - th * th),
                                   lax.dot_general(d_za, a2, nt, preferred_element_type=F32)], axis=1)
                  + jnp.concatenate([dkr, jnp.zeros((dkr.shape[0], TAIL - LANES), F32)], axis=1))
        g_w2 = lax.dot_general(th_b, d_zw, tn, preferred_element_type=F32)
        g_a2 = lax.dot_general(tail_h, d_za, tn, preferred_element_type=F32)
        d_rl = jnp.concatenate([drf + drb + drbon, g[0], dvf + dvb + dvbon, d_tail], axis=1)
        return (d_rl,) + tuple(g[3:9]) + (g_w2, g_a2)

    f_, b_ = dsc["f"], dsc["b"]
    pre_bwd_rows = [rl_k, rl_tail, f_[1], b_[1], f_[2], b_[2], d_k_bonus, f_[4], b_[4], f_[5], b_[5],
                    f_[0], b_[0], d_r_bonus, f_[3], b_[3], d_v_bonus, d_kr_in]
    (d_rl, g_w0_f, g_w0_b, g_a0_f, g_a0_b, g_k_k, g_k_a, g_w2cat, g_a2cat) = _rowwise(
        pre_bwd, pre_bwd_rows, pre_params, [(3 * RW + TAIL, F32)],
        [(1, RW)] * 6 + [W["w2cat"].shape, W["a2cat"].shape], tile=T // 2, name="rwkv_pre_bwd")
    d_proj, g_mu = _shift_lerp(shift_view, W["mu"], d_rl, (d_proj, lay["r"][0]), name="shift_bwd")
    small = dict(wq_b=g_wq_b, wkv_b=g_wkv_b, w2cat=g_w2cat, a2cat=g_a2cat, w_br_mla=g_w_br_mla,
                 w_br_rwkv=g_w_br_rwkv, w_out=g_w_out)
    if exchange is None:
        received = None
        g_w_in = _mm(d_proj, h, ta=True, out_dtype=BF16, tn_cap=1024, name="g_w_in")
        d_h = _mm(d_proj, W["w_in_t"], tn_cap=1024, name="d_h")
    else:
        slabs = _restore_rest(small, dims)
        slabs = [slabs[n] for n in _MATS[1:]]
        g_w_in, *got = _mm(d_proj, h, ta=True, out_dtype=BF16, tn_cap=1024, ride=_sibling_swap_plan(slabs),
                           name="g_w_in")
        sums = [_pair_add(exchange[1], s, t, name="pair_add_" + n) for n, s, t in zip(_MATS[1:], slabs, got)]
        g_w_in = _restore_w_in(g_w_in, dims)
        d_h, *received = _mm(d_proj, W["w_in_t"], tn_cap=1024, name="d_h",
                             ride=_join_plans(_chip_exchange_plan(sums), _sibling_swap_plan([g_w_in])))
        small = {}

    def pre_norm_bwd(xb, dyb, dhb, g):
        _, vjp = jax.vjp(_rms, xb, g)
        dx, dg = vjp(dhb)
        return dyb + dx, dg

    if exchange is None:
        grad_x, g_g_pre = _rowwise(pre_norm_bwd, [x, dy, d_h], [W["g_pre"]], [(D, F32)], [(1, D)], tile=2 * T,
                                   name="pre_norm_bwd")
    else:
        sums_w_in = _pair_add(exchange[1], g_w_in, received[-1], name="pair_add_w_in")
        grad_x, g_g_pre, recv_w_in = _rowwise(pre_norm_bwd, [x, dy, d_h], [W["g_pre"]], [(D, F32)], [(1, D)],
                                              tile=2 * T, name="pre_norm_bwd",
                                              ride=_chip_exchange_plan([sums_w_in]))
        received = received[:-1] + [recv_w_in]

    grads = dict(g_pre=g_g_pre, w_in=g_w_in, mla_q_norm=g_q_norm, mla_kv_norm=g_kv_norm, mu=g_mu, w0_f=g_w0_f,
                 w0_b=g_w0_b, a0_f=g_a0_f, a0_b=g_a0_b, k_k=g_k_k, k_a=g_k_a, r_k=g_r_k, gn_g=g_gn_g, gn_b=g_gn_b,
                 g_post=g_g_post, **small)
    return loss[0, 0], grad_x, grads, received


_MATS = ["w_in", "mla_wq_b", "mla_wkv_b", "rwkv_w2_f", "rwkv_w2_b", "rwkv_a2_f", "rwkv_a2_b", "w_br_mla",
         "w_br_rwkv", "w_out"]
_ROW_SHARDED = ("w_out",)
_TRANSPOSED = ("w_in", "mla_wq_b")
_VECS = ["g_pre", "mla_q_norm", "mla_kv_norm", "rwkv_mu", "rwkv_w0_f", "rwkv_w0_b", "rwkv_a0_f", "rwkv_a0_b",
         "rwkv_k_k", "rwkv_k_a", "rwkv_r_k", "rwkv_gn_g", "rwkv_gn_b", "g_post"]
_WEIGHTS = ["g_pre", "w_in", "mla_q_norm", "mla_wq_b", "mla_kv_norm", "mla_wkv_b", "rwkv_mu", "rwkv_w0_f",
            "rwkv_w2_f", "rwkv_w0_b", "rwkv_w2_b", "rwkv_a0_f", "rwkv_a2_f", "rwkv_a0_b", "rwkv_a2_b", "rwkv_k_k",
            "rwkv_k_a", "rwkv_r_k", "rwkv_gn_g", "rwkv_gn_b", "w_br_mla", "w_br_rwkv", "w_out", "g_post"]

def _direct_gather_plan(src):
    def phases(src_refs, out_refs, sem_refs):
        (src_ref,), (out_ref,), sems, local_sem = src_refs, out_refs, sem_refs[:2], sem_refs[2]
        x, y, c = lax.axis_index("x"), lax.axis_index("y"), lax.axis_index("c")
        me = 4 * x + 2 * y + c
        flip = lambda v, bit: (1 - v) if bit else v
        peers = [(flip(x, d & 4), flip(y, d & 2), flip(c, d & 1)) for d in range(1, N_DEV)]
        own = lambda: pltpu.make_async_copy(src_ref, out_ref.at[me], local_sem)
        send = lambda d: _remote(src_ref, out_ref.at[me], sems, d, peers[d])

        def first():
            own().start()
            for d in range(N_DEV - 1):
                send(d).start()

        def last():
            for d, (px, py, pc) in enumerate(peers):
                blk = out_ref.at[4 * px + 2 * py + pc]
                _remote(blk, blk, sems, d, (x, y, c)).wait_recv()
            for d in range(N_DEV - 1):
                send(d).wait_send()
            own().wait()

        return first, (lambda: None), last

    return [src], [jax.ShapeDtypeStruct((N_DEV,) + src.shape, src.dtype)], [(N_DEV - 1,), (N_DEV - 1,), ()], phases


def _remote(src, dst, sems, key, to):
    send_sems, recv_sems = sems
    return pltpu.make_async_remote_copy(src_ref=src, dst_ref=dst, send_sem=send_sems.at[key], recv_sem=recv_sems.at[key],
                                        device_id=to, device_id_type=pl.DeviceIdType.MESH)


def _run_exchange(plan, *, name):
    srcs, out_shapes, sem_shapes, phases = plan
    n, m = len(srcs), len(out_shapes)

    def body(*refs):
        for phase in phases(refs[:n], refs[n:n + m], refs[n + m:]):
            phase()

    return pl.pallas_call(
        body, name=name, out_shape=out_shapes,
        in_specs=[pl.BlockSpec(memory_space=pl.ANY)] * n, out_specs=[pl.BlockSpec(memory_space=pl.ANY)] * m,
        scratch_shapes=[pltpu.SemaphoreType.DMA(s) for s in sem_shapes],
    )(*srcs)


def _join_plans(p, q):
    (srcs_p, outs_p, sems_p, phases_p), (srcs_q, outs_q, sems_q, phases_q) = p, q

    def phases(src_refs, out_refs, sem_refs):
        a = phases_p(src_refs[:len(srcs_p)], out_refs[:len(outs_p)], sem_refs[:len(sems_p)])
        b = phases_q(src_refs[len(srcs_p):], out_refs[len(outs_p):], sem_refs[len(sems_p):])

        def both(fa, fb):
            def run():
                fa()
                fb()
            return run

        return tuple(both(fa, fb) for fa, fb in zip(a, b))

    return list(srcs_p) + list(srcs_q), list(outs_p) + list(outs_q), list(sems_p) + list(sems_q), phases


def _gather_plan(srcs):
    n = len(srcs)

    def phases(src_refs, out_refs, sem_refs):
        sems, local_sems = sem_refs[:2], sem_refs[2]
        x, y, c = lax.axis_index("x"), lax.axis_index("y"), lax.axis_index("c")
        idx = lambda px, py, pc: 4 * px + 2 * py + pc
        me, sibling = (x, y, c), (x, y, 1 - c)
        chips = [(1 - x, y), (x, 1 - y), (1 - x, 1 - y)]
        own = lambda a: pltpu.make_async_copy(src_refs[a], out_refs[a].at[idx(*me)], local_sems.at[a])
        to_sibling = lambda a: _remote(src_refs[a], out_refs[a].at[idx(*me)], sems, (0, a), sibling)
        to_chip = lambda a, j: _remote(src_refs[a], out_refs[a].at[idx(*me)], sems, (1 + j, a), (*chips[j], c))
        landed = lambda a, j: out_refs[a].at[idx(*chips[j], c)]
        passed_on = lambda a, j: _remote(landed(a, j), landed(a, j), sems, (4 + j, a), sibling)

        def first():
            for a in range(n):
                own(a).start()
                to_sibling(a).start()
                for j in range(3):
                    to_chip(a, j).start()

        def middle():
            for j in range(3):
                for a in range(n):
                    _remote(landed(a, j), landed(a, j), sems, (1 + j, a), me).wait_recv()
                    passed_on(a, j).start()

        def last():
            for a in range(n):
                blk = out_refs[a].at[idx(*sibling)]
                _remote(blk, blk, sems, (0, a), me).wait_recv()
                for j in range(3):
                    blk = out_refs[a].at[idx(*chips[j], 1 - c)]
                    _remote(blk, blk, sems, (4 + j, a), me).wait_recv()
            for a in range(n):
                to_sibling(a).wait_send()
                for j in range(3):
                    to_chip(a, j).wait_send()
                    passed_on(a, j).wait_send()
                own(a).wait()

        return first, middle, last

    return srcs, [jax.ShapeDtypeStruct((N_DEV,) + s.shape, s.dtype) for s in srcs], [(7, n), (7, n), (n,)], phases


def _sibling_swap_plan(srcs):
    n = len(srcs)

    def phases(src_refs, out_refs, sems):
        x, y, c = lax.axis_index("x"), lax.axis_index("y"), lax.axis_index("c")
        copies = lambda: [_remote(src_refs[a].at[2 * q + 1 - c], out_refs[a].at[q], sems, (q, a), (x, y, 1 - c))
                          for a in range(n) for q in range(4)]

        def first():
            for cp in copies():
                cp.start()

        def last():
            for cp in copies():
                cp.wait()

        return first, (lambda: None), last

    return srcs, [jax.ShapeDtypeStruct((4,) + s.shape[1:], s.dtype) for s in srcs], [(4, n), (4, n)], phases


def _chip_exchange_plan(srcs):
    n = len(srcs)

    def phases(src_refs, out_refs, sem_refs):
        sems, local_sems = sem_refs[:2], sem_refs[2]
        x, y, c = lax.axis_index("x"), lax.axis_index("y"), lax.axis_index("c")
        mine = 2 * x + y
        chips = [(1 - x, y), (x, 1 - y), (1 - x, 1 - y)]
        own = lambda a: pltpu.make_async_copy(src_refs[a].at[mine], out_refs[a].at[mine], local_sems.at[a])
        send = lambda a, j: _remote(src_refs[a].at[2 * chips[j][0] + chips[j][1]], out_refs[a].at[mine], sems, (j, a),
                                    (*chips[j], c))

        def first():
            for a in range(n):
                own(a).start()
                for j in range(3):
                    send(a, j).start()

        def last():
            for j in range(3):
                for a in range(n):
                    blk = out_refs[a].at[2 * chips[j][0] + chips[j][1]]
                    _remote(blk, blk, sems, (j, a), (x, y, c)).wait_recv()
            for a in range(n):
                for j in range(3):
                    send(a, j).wait_send()
                own(a).wait()

        return first, (lambda: None), last

    return srcs, [jax.ShapeDtypeStruct(s.shape, s.dtype) for s in srcs], [(3, n), (3, n), (n,)], phases


def _pair_add(core, g, got, *, name):
    q, r, c = got.shape
    tr, tc = _tile2d(r, c, cap=1024)

    def body(core_ref, a_ref, b_ref, o_ref):
        o_ref[...] = (a_ref[...].astype(F32) + b_ref[...].astype(F32)).astype(BF16)

    blk = pl.BlockSpec((1, tr, tc), lambda i, j, k, core_ref: (i, j, k))
    mine = pl.BlockSpec((1, tr, tc), lambda i, j, k, core_ref: (2 * i + core_ref[0], j, k))
    return pl.pallas_call(
        body, name=name, out_shape=jax.ShapeDtypeStruct(got.shape, BF16),
        grid_spec=pltpu.PrefetchScalarGridSpec(num_scalar_prefetch=1, grid=(q, r // tr, c // tc),
                                               in_specs=[mine, blk], out_specs=blk),
        compiler_params=_cparams(("parallel", "parallel", "parallel")))(core, g, got)


def _adamw(recv, w, m, v, *, name):
    r, c = w.shape
    n_terms = recv.shape[0]
    tr, tc = _tile2d(r, c)

    def body(g_ref, w_ref, m_ref, v_ref, go_ref, d_ref, mo_ref, vo_ref):
        g = g_ref[0].astype(F32)
        for k in range(1, n_terms):
            g = g + g_ref[k].astype(F32)
        m_new = ADAM_B1 * m_ref[...] + (1.0 - ADAM_B1) * g
        v_new = ADAM_B2 * v_ref[...] + (1.0 - ADAM_B2) * (g * g)
        m_hat = m_new / (1.0 - ADAM_B1 ** ADAM_STEP)
        v_hat = v_new / (1.0 - ADAM_B2 ** ADAM_STEP)
        go_ref[...] = g
        d_ref[...] = -ADAM_LR * (m_hat / (jnp.sqrt(v_hat) + ADAM_EPS) + ADAM_WD * w_ref[...])
        mo_ref[...] = m_new
        vo_ref[...] = v_new

    blk = pl.BlockSpec((tr, tc), lambda i, j: (i, j))
    return pl.pallas_call(
        body, name=name, grid=(r // tr, c // tc),
        in_specs=[pl.BlockSpec((n_terms, tr, tc), lambda i, j: (0, i, j)), blk, blk, blk], out_specs=[blk] * 4,
        out_shape=[jax.ShapeDtypeStruct((r, c), F32)] * 4, compiler_params=_cparams(("parallel", "parallel")),
    )(recv, w, m, v)


def _tile2d(r, c, cap=256):
    if r <= cap:
        return r, c
    for t in range(cap - cap % BF16_ROWS, 0, -BF16_ROWS):
        if r % t == 0:
            return t, c
    return r, _pick(c, cap)


def _pack(pieces):
    total = sum(p.shape[0] for p in pieces)
    pad = (-total) % (8 * LANES)
    flat = jnp.concatenate(list(pieces) + [jnp.zeros((pad,), F32)])
    return flat.reshape(-1, LANES)


def _unpack(flat, sizes):
    flat = flat.reshape(-1)
    out, o = [], 0
    for n in sizes:
        out.append(flat[o:o + n])
        o += n
    return out


def _prepare_weights(full, vec, dims):
    rest = {n: t for n, t in full.items() if n != "w_in"}
    return {"w_in_t": _prepare_w_in(full["w_in"], dims), **_prepare_rest(rest, dims), **_prepare_vectors(vec, dims)}


def _prepare_w_in(slabs, dims):
    D = dims["D"]
    flat = slabs.reshape(-1, D)
    parts, pos = [], 0
    for orig_off, width, perm_off in sorted(dims["segs"], key=lambda t: t[2]):
        if perm_off > pos:
            parts.append(jnp.zeros((perm_off - pos, D), BF16))
        parts.append(flat[orig_off:orig_off + width])
        pos = perm_off + width
    if dims["d_in_perm"] > pos:
        parts.append(jnp.zeros((dims["d_in_perm"] - pos, D), BF16))
    return jnp.concatenate(parts, axis=0)


def _prepare_rest(full, dims):
    hm, hr, hn, rank = dims["hm"], dims["hr"], dims["hn"], dims["rank"]
    QR, KVR = dims["QR"], dims["KVR"]
    RW, TAIL = hr * hn, dims["TAIL"]
    full = {n: (t.reshape(-1, t.shape[2]) if n in _ROW_SHARDED + _TRANSPOSED
                else t.transpose(1, 0, 2).reshape(t.shape[1], -1)) for n, t in full.items()}
    wq = full["mla_wq_b"].reshape(hm, NOPE + ROPE, QR)
    wq = jnp.concatenate([wq, jnp.zeros((hm, QHEAD - NOPE - ROPE, QR), BF16)], axis=1).reshape(hm * QHEAD, QR)
    wkv = full["mla_wkv_b"].reshape(KVR, hm, 2, NOPE).transpose(0, 2, 1, 3).reshape(KVR, 2 * hm * NOPE)
    z = lambda rows: jnp.zeros((rows, RW), BF16)
    f = lambda nme: full[nme]
    split = ROPE + 2 * rank
    assert split % LANES == 0, split
    w2cat = jnp.concatenate([
        jnp.concatenate([z(ROPE), f("rwkv_w2_f"), z(rank)], axis=0),
        jnp.concatenate([z(ROPE + rank), f("rwkv_w2_b")], axis=0)], axis=1)
    a2cat = jnp.concatenate([
        jnp.concatenate([f("rwkv_a2_f"), z(TAIL - split - rank)], axis=0),
        jnp.concatenate([z(rank), f("rwkv_a2_b"), z(TAIL - split - 2 * rank)], axis=0)], axis=1)
    return dict(wq_b_t=wq, wkv_b=wkv, w2cat=w2cat, a2cat=a2cat, w_br_mla=full["w_br_mla"],
                w_br_rwkv=full["w_br_rwkv"], w_out=full["w_out"])


def _prepare_vectors(vec, dims):
    rank, RW, TAIL = dims["rank"], dims["hr"] * dims["hn"], dims["TAIL"]
    mu = vec["rwkv_mu"]
    mu_p = jnp.concatenate([mu[:3 * RW], jnp.zeros((ROPE,), F32), mu[3 * RW:],
                            jnp.zeros((TAIL - ROPE - 4 * rank,), F32)])
    row = lambda t: t.reshape(1, -1)
    return dict(
        mu=row(mu_p), g_pre=row(vec["g_pre"]), g_post=row(vec["g_post"]), mla_q_norm=row(vec["mla_q_norm"]),
        mla_kv_norm=row(vec["mla_kv_norm"]), w0_f=row(vec["rwkv_w0_f"]), w0_b=row(vec["rwkv_w0_b"]),
        a0_f=row(vec["rwkv_a0_f"]), a0_b=row(vec["rwkv_a0_b"]), k_k=row(vec["rwkv_k_k"]), k_a=row(vec["rwkv_k_a"]),
        r_k=row(vec["rwkv_r_k"]), gn_g=row(vec["rwkv_gn_g"]), gn_b=row(vec["rwkv_gn_b"]))


def _restore_grads(g, dims):
    return {"w_in": _restore_w_in(g["w_in"], dims), **_restore_rest(g, dims), **_restore_vectors(g, dims)}


def _restore_w_in(gw, dims):
    parts = [gw[perm_off:perm_off + width] for _, width, perm_off in sorted(dims["segs"])]
    return jnp.concatenate(parts, axis=0).reshape(N_DEV, dims["d_in"] // N_DEV, gw.shape[1])


def _restore_rest(g, dims):
    hm, hr, hn, rank = dims["hm"], dims["hr"], dims["hn"], dims["rank"]
    QR, KVR, RW = dims["QR"], dims["KVR"], hr * hn
    wq = g["wq_b"].reshape(hm, QHEAD, QR)[:, :NOPE + ROPE].reshape(N_DEV, -1, QR)
    wkv = g["wkv_b"].reshape(KVR, 2, hm, NOPE).transpose(0, 2, 1, 3).reshape(KVR, 2 * hm * NOPE)
    lo = lambda t, first, half: t[first:first + rank, half * RW:(half + 1) * RW].astype(BF16)
    cols = lambda t: t.reshape(t.shape[0], N_DEV, -1).transpose(1, 0, 2)
    return dict(
        mla_wq_b=wq, mla_wkv_b=cols(wkv), rwkv_w2_f=cols(lo(g["w2cat"], ROPE, 0)),
        rwkv_w2_b=cols(lo(g["w2cat"], ROPE + rank, 1)), rwkv_a2_f=cols(lo(g["a2cat"], 0, 0)),
        rwkv_a2_b=cols(lo(g["a2cat"], rank, 1)), w_br_mla=cols(g["w_br_mla"]), w_br_rwkv=cols(g["w_br_rwkv"]),
        w_out=g["w_out"].reshape(N_DEV, -1, g["w_out"].shape[1]))


def _restore_vectors(g, dims):
    rank, RW = dims["rank"], dims["hr"] * dims["hn"]
    mu = g["mu"][0]
    out = dict(
        rwkv_mu=jnp.concatenate([mu[:3 * RW], mu[3 * RW + ROPE:3 * RW + ROPE + 4 * rank]]),
        g_pre=g["g_pre"][0], g_post=g["g_post"][0], mla_q_norm=g["mla_q_norm"][0], mla_kv_norm=g["mla_kv_norm"][0],
        rwkv_w0_f=g["w0_f"][0], rwkv_w0_b=g["w0_b"][0], rwkv_a0_f=g["a0_f"][0], rwkv_a0_b=g["a0_b"][0],
        rwkv_k_k=g["k_k"][0], rwkv_k_a=g["k_a"][0], rwkv_r_k=g["r_k"][0], rwkv_gn_g=g["gn_g"][0],
        rwkv_gn_b=g["gn_b"][0])
    return out


def _dims(inp):
    D = inp["x"].shape[-1]
    QR, KVR = inp["mla_q_norm"].shape[0], inp["mla_kv_norm"].shape[0]
    hm = inp["mla_wq_b"].shape[1] * N_DEV // (NOPE + ROPE)
    hr, hn = inp["rwkv_r_k"].shape
    rank = inp["rwkv_w2_f"].shape[0]
    MW, RW = hm * VDIM, hr * hn
    TAIL = -(-(ROPE + 4 * rank) // LANES) * LANES
    orig, o = {}, 0
    for nme, w in (("q_a", QR), ("kv_a", KVR), ("k_rope", ROPE), ("rkv", 3 * RW), ("lora", 4 * rank), ("z_m", MW),
                   ("z_r", RW), ("gate_m", D), ("gate_r", D)):
        orig[nme] = (o, w)
        o += w
    assert o == inp["w_in"].shape[1] * N_DEV
    lay, d_in_perm = _layout(D, MW, RW, TAIL, QR, KVR)
    perm_off = dict(q_a=lay["q_a"][0], kv_a=lay["kv_a"][0], k_rope=lay["tail"][0], rkv=lay["r"][0],
                    lora=lay["tail"][0] + ROPE, z_m=lay["z_m"][0], z_r=lay["z_r"][0], gate_m=lay["gate_m"][0],
                    gate_r=lay["gate_r"][0])
    segs = [(orig[nme][0], orig[nme][1], perm_off[nme]) for nme in orig]
    return dict(D=D, QR=QR, KVR=KVR, hm=hm, hr=hr, hn=hn, rank=rank, TAIL=TAIL, segs=segs, d_in=o,
                d_in_perm=d_in_perm)


def kernel(x, g_pre, w_in, mla_q_norm, mla_wq_b, mla_kv_norm, mla_wkv_b, rwkv_mu, rwkv_w0_f, rwkv_w2_f, rwkv_w0_b, rwkv_w2_b, rwkv_a0_f, rwkv_a2_f, rwkv_a0_b, rwkv_a2_b, rwkv_k_k, rwkv_k_a, rwkv_r_k, rwkv_gn_g, rwkv_gn_b, w_br_mla, w_br_rwkv, w_out, g_post, loss_target, m_g_pre, m_w_in, m_mla_q_norm, m_mla_wq_b, m_mla_kv_norm, m_mla_wkv_b, m_rwkv_mu, m_rwkv_w0_f, m_rwkv_w2_f, m_rwkv_w0_b, m_rwkv_w2_b, m_rwkv_a0_f, m_rwkv_a2_f, m_rwkv_a0_b, m_rwkv_a2_b, m_rwkv_k_k, m_rwkv_k_a, m_rwkv_r_k, m_rwkv_gn_g, m_rwkv_gn_b, m_w_br_mla, m_w_br_rwkv, m_w_out, m_g_post, v_g_pre, v_w_in, v_mla_q_norm, v_mla_wq_b, v_mla_kv_norm, v_mla_wkv_b, v_rwkv_mu, v_rwkv_w0_f, v_rwkv_w2_f, v_rwkv_w0_b, v_rwkv_w2_b, v_rwkv_a0_f, v_rwkv_a2_f, v_rwkv_a0_b, v_rwkv_a2_b, v_rwkv_k_k, v_rwkv_k_a, v_rwkv_r_k, v_rwkv_gn_g, v_rwkv_gn_b, v_w_br_mla, v_w_br_rwkv, v_w_out, v_g_post):
    inp = dict(locals())
    dims = _dims(inp)
    stored = lambda t, n: t.T if n in _TRANSPOSED else t
    assert _MATS[0] == "w_in"
    shards = [stored(inp[n], n).astype(BF16) for n in _MATS]
    core = lax.axis_index("c").astype(jnp.int32).reshape(1)
    (w_in_slabs,) = _run_exchange(_gather_plan(shards[:1]), name="gather_w_in")
    W = {"w_in_t": _prepare_w_in(w_in_slabs, dims), **_prepare_vectors({n: inp[n] for n in _VECS}, dims)}
    loss, grad_x, g, recv_rest = _local_grads(x[0], loss_target[0], W, dims, exchange=(shards[1:], core))

    new = {}
    *recv_rest, recv_w_in = recv_rest
    g = _restore_vectors(g, dims)
    vsizes = [inp[n].size for n in _VECS] + [1]
    vflat = lambda prefix, src, last: _pack([src[prefix + n].reshape(-1) for n in _VECS] + [last])
    one = jnp.zeros((1,), F32)
    (vrecv,) = _run_exchange(_direct_gather_plan(vflat("", g, loss.reshape(1))), name="gather_vector_grads")
    for n, t in zip(_MATS, [recv_w_in] + recv_rest):
        out = _adamw(t, stored(inp[n], n), stored(inp["m_" + n], n), stored(inp["v_" + n], n), name="adamw_" + n)
        new[n] = [stored(o, n) for o in out]

    vout = _adamw(vrecv, vflat("", inp, one), vflat("m_", inp, one), vflat("v_", inp, one), name="adamw_vectors")
    vparts = [_unpack(t, vsizes) for t in vout]
    for i, n in enumerate(_VECS):
        new[n] = [vp[i].reshape(inp[n].shape) for vp in vparts]
    loss = vparts[0][-1].reshape(())

    outs = [loss, grad_x[None]]
    for k in range(4):
        outs += [new[n][k] for n in _WEIGHTS]
    return tuple(outs)
```

```python
import functools
import math

import jax
import jax.numpy as jnp
from jax import lax
from jax.experimental import pallas as pl
from jax.experimental.pallas import tpu as pltpu

F32 = jnp.float32
BF16 = jnp.bfloat16

N_DEV = 8
LANES = 128
BF16_ROWS = 16
NOPE, ROPE, VDIM = 128, 64, 128
QHEAD = 256
ROPE_THETA = 10000.0
NORM_EPS = 1e-6
GN_EPS = 64e-5
CHUNK = 64
SUB = 16
VMEM_LIMIT = 56 * 1024 * 1024

ADAM_LR, ADAM_B1, ADAM_B2, ADAM_EPS, ADAM_WD, ADAM_STEP = 0.001, 0.9, 0.999, 1e-08, 0.01, 10


def _cparams(sem):
    return pltpu.CompilerParams(dimension_semantics=sem, vmem_limit_bytes=VMEM_LIMIT)


def _pick(n, cap):
    if n <= cap:
        return n
    for t in range(cap - cap % LANES, 0, -LANES):
        if n % t == 0:
            return t
    raise ValueError(f"no tile for {n} under {cap}")


def _mm(a, b, *, ta=False, tb=False, out_dtype=F32, name, tm_cap=1024, tn_cap=512, tk_cap=2048, ride=None):
    K, M = a.shape if ta else a.shape[::-1]
    N = b.shape[0] if tb else b.shape[1]
    assert (b.shape[1] if tb else b.shape[0]) == K, (a.shape, b.shape, ta, tb)
    tm, tn, tk = _pick(M, tm_cap), _pick(N, tn_cap), _pick(K, tk_cap)
    nj, nk = N // tn, K // tk
    steps = (M // tm) * nj * nk
    dn = (((0 if ta else 1,), (1 if tb else 0,)), ((), ()))
    srcs, extra_shapes, sem_shapes, phases = ride if ride else ((), (), (), None)
    n_src, n_extra = len(srcs), len(extra_shapes)

    def body(*refs):
        a_ref, b_ref, o_ref = refs[0], refs[1], refs[2 + n_src]
        acc_ref = refs[3 + n_src + n_extra]
        k = pl.program_id(2)
        if ride:
            step = (pl.program_id(0) * nj + pl.program_id(1)) * nk + k
            first, middle, last = phases(refs[2:2 + n_src], refs[3 + n_src:3 + n_src + n_extra],
                                         refs[4 + n_src + n_extra:])
            pl.when(step == 0)(first)
            pl.when(step == (steps * 15) // 16)(middle)
        p = lax.dot_general(a_ref[...], b_ref[...], dn, preferred_element_type=F32)

        @pl.when(k == 0)
        def _():
            acc_ref[...] = p

        @pl.when(k > 0)
        def _():
            acc_ref[...] += p

        @pl.when(k == nk - 1)
        def _():
            o_ref[...] = acc_ref[...].astype(out_dtype)

        if ride:
            pl.when(step == steps - 1)(last)

    a_spec = pl.BlockSpec((tk, tm), lambda i, j, k: (k, i)) if ta else pl.BlockSpec((tm, tk), lambda i, j, k: (i, k))
    b_spec = pl.BlockSpec((tn, tk), lambda i, j, k: (j, k)) if tb else pl.BlockSpec((tk, tn), lambda i, j, k: (k, j))
    hbm = pl.BlockSpec(memory_space=pl.ANY)
    out = pl.pallas_call(
        body, name=name, grid=(M // tm, nj, nk),
        in_specs=[a_spec, b_spec] + [hbm] * n_src,
        out_specs=[pl.BlockSpec((tm, tn), lambda i, j, k: (i, j))] + [hbm] * n_extra,
        out_shape=[jax.ShapeDtypeStruct((M, N), out_dtype)] + list(extra_shapes),
        scratch_shapes=[pltpu.VMEM((tm, tn), F32)] + [pltpu.SemaphoreType.DMA(s) for s in sem_shapes],
        compiler_params=_cparams(("arbitrary",) * 3 if ride else ("parallel", "parallel", "arbitrary")),
    )(a, b, *srcs)
    return out if ride else out[0]


def _view(arr, off, width):
    assert off % width == 0, (off, width)
    return (arr, off // width, width)


def _rowwise(fn, rows, params, out_rows, out_accs=(), *, tile, name, ride=None):
    rows = [r if isinstance(r, tuple) else (r, 0, r.shape[1]) for r in rows]
    S = rows[0][0].shape[0]
    T = min(tile, S)
    assert S % T == 0
    steps = S // T
    n_rows, n_par, n_out, n_acc = len(rows), len(params), len(out_rows), len(out_accs)
    into = [o[2] if len(o) == 3 else None for o in out_rows]
    carried = [t[0] for t in into if t is not None and t[0] is not None]
    srcs, extra_shapes, sem_shapes, phases = ride if ride else ((), (), (), None)
    n_in = n_rows + n_par + len(carried) + len(srcs)

    def body(*refs):
        i = pl.program_id(0)
        if ride:
            first, middle, last = phases(refs[n_in - len(srcs):n_in],
                                         refs[n_in + n_out + n_acc:n_in + n_out + n_acc + len(extra_shapes)],
                                         refs[n_in + n_out + n_acc + len(extra_shapes):])
            pl.when(i == 0)(first)
            pl.when(i == (steps * 15) // 16)(middle)
        ins = [r[...] for r in refs[:n_rows + n_par]]
        outs = fn(*ins)
        out_refs = refs[n_in:n_in + n_out + n_acc]
        for o_ref, val in zip(out_refs[:n_out], outs[:n_out]):
            o_ref[...] = val.astype(o_ref.dtype)
        for o_ref, val in zip(out_refs[n_out:], outs[n_out:]):
            @pl.when(i == 0)
            def _(o_ref=o_ref, val=val):
                o_ref[...] = val

            @pl.when(i > 0)
            def _(o_ref=o_ref, val=val):
                o_ref[...] += val
        if ride:
            pl.when(i == steps - 1)(last)

    in_specs = [pl.BlockSpec((T, w), functools.partial(lambda i, cb: (i, cb), cb=cb)) for _, cb, w in rows]
    in_specs += [pl.BlockSpec(p.shape, lambda i: (0, 0)) for p in params]
    in_specs += [pl.BlockSpec(memory_space=pl.ANY)] * len(carried)
    out_specs, out_shape, aliases = [], [], {}
    for k, (o, t) in enumerate(zip(out_rows, into)):
        w, dt = o[0], o[1]
        if t is None:
            out_specs.append(pl.BlockSpec((T, w), lambda i: (i, 0)))
            out_shape.append(jax.ShapeDtypeStruct((S, w), dt))
            continue
        buf, total, first = t
        assert first % w == 0
        out_specs.append(pl.BlockSpec((T, w), functools.partial(lambda i, cb: (i, cb), cb=first // w)))
        out_shape.append(jax.ShapeDtypeStruct((S, total), dt))
        if buf is not None:
            aliases[n_rows + n_par + len(aliases)] = k
    out_specs += [pl.BlockSpec(s, lambda i: (0, 0)) for s in out_accs]
    out_shape += [jax.ShapeDtypeStruct(s, F32) for s in out_accs]
    hbm = pl.BlockSpec(memory_space=pl.ANY)
    return pl.pallas_call(
        body, name=name, grid=(steps,), in_specs=in_specs + [hbm] * len(srcs),
        out_specs=out_specs + [hbm] * len(extra_shapes), out_shape=out_shape + list(extra_shapes),
        scratch_shapes=[pltpu.SemaphoreType.DMA(s) for s in sem_shapes],
        input_output_aliases=aliases, compiler_params=_cparams(("arbitrary",)),
    )(*[r[0] for r in rows], *params, *carried, *srcs)


def _mm_sel(x, sel2):
    hi = x.astype(BF16)
    lo = (x - hi.astype(F32)).astype(BF16)
    return jnp.dot(jnp.concatenate([hi, lo], axis=1), sel2, preferred_element_type=F32)


@jax.custom_vjp
def _sel(x, sel, sel_t):
    return _mm_sel(x, sel)


def _sel_fwd(x, sel, sel_t):
    return _mm_sel(x, sel), (sel, sel_t)


def _sel_bwd(res, ct):
    sel, sel_t = res
    return _mm_sel(ct, sel_t), jnp.zeros_like(sel), jnp.zeros_like(sel_t)


_sel.defvjp(_sel_fwd, _sel_bwd)


def _rms(x, g):
    return x * lax.rsqrt(jnp.mean(x * x, axis=-1, keepdims=True) + NORM_EPS) * g


def _sigmoid(x):
    return 0.5 * jnp.tanh(0.5 * x) + 0.5


def _silu(x):
    return x * _sigmoid(x)


def _softplus(x):
    return jnp.maximum(x, 0.0) + jnp.log(1.0 + jnp.exp(-jnp.abs(x)))


def _f_mla_norm(q_a, kv_a, qg, kvg):
    return _rms(q_a, qg), _rms(kv_a, kvg)


def _f_rope(hm, qraw, kr_in, cosx, sinx, rot, rot_t):
    def rope(t):
        return t * cosx + _sel(t, rot, rot_t) * sinx
    parts = []
    for h in range(hm):
        parts.append(qraw[:, h * QHEAD:h * QHEAD + NOPE])
        parts.append(rope(qraw[:, h * QHEAD + NOPE:(h + 1) * QHEAD]))
    return jnp.concatenate(parts, axis=1), rope(kr_in)


def _f_rwkv_pre(rw, k, tail, w0f, w0b, a0f, a0b, k_k, k_a, w2cat, a2cat, seg, seg_t):
    split = w2cat.shape[0]
    zw = jnp.dot(jnp.tanh(tail[:, :split]).astype(BF16), w2cat, preferred_element_type=F32)
    za = jnp.dot(tail[:, split:].astype(BF16), a2cat, preferred_element_type=F32)
    return _f_rwkv_core(rw, k, zw, za, w0f, w0b, a0f, a0b, k_k, k_a, seg, seg_t)


def _f_rwkv_core(rw, k, zw, za, w0f, w0b, a0f, a0b, k_k, k_a, seg, seg_t):
    lw_f = -jnp.exp(-_softplus(-(w0f + zw[:, :rw])) - 0.5)
    lw_b = -jnp.exp(-_softplus(-(w0b + zw[:, rw:])) - 0.5)
    a_f = _sigmoid(a0f + za[:, :rw])
    a_b = _sigmoid(a0b + za[:, rw:])
    kk = k * k_k
    nrm = jnp.sqrt(_sel(_sel(kk * kk, seg, seg_t), seg_t, seg))
    kk = kk / jnp.maximum(nrm, 1e-12)
    k_f = k * (1.0 + (a_f - 1.0) * k_a)
    k_b = k * (1.0 + (a_b - 1.0) * k_a)
    return lw_f, lw_b, k_f, k_b, -kk, kk * a_f, kk * a_b


def _f_post(hn, y_f, y_b, r, k_f, k_b, v, z_r, o_mla, z_m, gn_g, gn_b, r_k, seg, seg_t):
    segsum = lambda t: _sel(_sel(t, seg, seg_t), seg_t, seg)
    y = y_f + y_b
    mu = segsum(y) * (1.0 / hn)
    yc = y - mu
    var = segsum(yc * yc) * (1.0 / hn)
    yn = yc * lax.rsqrt(var + GN_EPS) * gn_g + gn_b
    bonus = segsum(r * (k_f + k_b) * r_k) * v
    return o_mla * _silu(z_m), (yn + bonus) * _silu(z_r)


def _f_merge(u_m, u_r, g_m, g_r):
    return _sigmoid(g_m) * u_m + _sigmoid(g_r) * u_r


_NN = ((2,), (1,))
_NT = ((2,), (2,))
_TN = ((1,), (1,))

_SCAN_PASSES = {"cum": 2, "gram": 3, "solve": 1, "apply": 1, "state": 1}


def _hdot_raw(passes, x, y, dims):
    dn = (dims, ((0,), (0,)))
    d = lambda p, q: lax.dot_general(p, q, dn, preferred_element_type=F32)
    xh = x.astype(BF16)
    yh = y.astype(BF16)
    if passes == 1:
        return d(xh, yh)
    yl = (y - yh.astype(F32)).astype(BF16)
    kx, ky = (1 if dims == _TN else 2), (2 if dims == _NT else 1)
    depth = x.shape[kx]
    if all(axis == 1 or depth % LANES == 0 for axis in (kx, ky)):
        if passes == 2:
            return d(jnp.concatenate([xh, xh], axis=kx), jnp.concatenate([yh, yl], axis=ky))
        xl = (x - xh.astype(F32)).astype(BF16)
        return d(jnp.concatenate([xh, xl, xh], axis=kx), jnp.concatenate([yh, yh, yl], axis=ky))
    if passes == 2:
        axis = 1 if dims == _NT else 2
        width = y.shape[axis]
        both = d(xh, jnp.concatenate([yh, yl], axis=axis))
        return both[:, :, :width] + both[:, :, width:]
    xl = (x - xh.astype(F32)).astype(BF16)
    if dims == _TN:
        return d(xh, yh) + d(xh, yl) + d(xl, yh)
    rows = x.shape[1]
    both = d(jnp.concatenate([xh, xl], axis=1), yh)
    return both[:, :rows] + both[:, rows:] + d(xh, yl)


@functools.partial(jax.custom_vjp, nondiff_argnums=(2, 3))
def _hdot_p(x, y, dims, passes):
    return _hdot_raw(passes, x, y, dims)


def _hdot_fwd(x, y, dims, passes):
    return _hdot_raw(passes, x, y, dims), (x, y)


def _hdot_bwd(dims, passes, res, ct):
    x, y = res
    if dims == _NN:
        return _hdot_raw(passes, ct, y, _NT), _hdot_raw(passes, x, ct, _TN)
    if dims == _NT:
        return _hdot_raw(passes, ct, y, _NN), _hdot_raw(passes, ct, x, _TN)
    return _hdot_raw(passes, y, ct, _NT), _hdot_raw(passes, x, ct, _NN)


_hdot_p.defvjp(_hdot_fwd, _hdot_bwd)


def _hdot(x, y, dims, kind):
    return _hdot_p(x, y, dims, _SCAN_PASSES[kind])


def _tri_solve(n_mat, x, length, blocks):
    row = lax.broadcasted_iota(jnp.int32, (length, 2 * length), 0)
    col = lax.broadcasted_iota(jnp.int32, (length, 2 * length), 1)
    col = jnp.where(col >= length, col - length, col)
    eye = (row == col).astype(F32)[None]
    diag_blk = ((row // SUB) == (col // SUB))[None]
    nd = jnp.where(diag_blk, n_mat, 0.0)
    no = n_mat - nd
    dinv = eye + nd
    p = _hdot(nd, blocks(nd), _NN, "solve")
    for k in range(int(math.log2(SUB)) - 1):
        if k == int(math.log2(SUB)) - 2:
            dinv = dinv + _hdot(dinv, blocks(p), _NN, "solve")
        else:
            both = _hdot(jnp.concatenate([dinv, p], axis=1), blocks(p), _NN, "solve")
            dinv, p = dinv + both[:, :length], both[:, length:]
    width = x.shape[2]
    both = _hdot(dinv, jnp.concatenate([blocks(x), blocks(no)], axis=2), _NN, "solve")
    u, q = both[:, :, :width], both[:, :, width:]
    for level in range(int(math.log2(length // SUB))):
        if level == int(math.log2(length // SUB)) - 1:
            u = u + _hdot(q, blocks(u), _NN, "solve")
        else:
            both = _hdot(q, jnp.concatenate([blocks(u), blocks(q)], axis=2), _NN, "solve")
            u, q = u + both[:, :, :width], both[:, :, width:]
    return u


def _rwkv_chunk(rev, s0, r, lw, k, v, a, b):
    pairs, length, width = r.shape
    hn = width // 2
    assert 2 * length == width
    row = lax.broadcasted_iota(jnp.int32, (length, length), 0)
    col = lax.broadcasted_iota(jnp.int32, (length, length), 1)
    row2 = lax.broadcasted_iota(jnp.int32, (length, 2 * length), 0)
    col2 = lax.broadcasted_iota(jnp.int32, (length, 2 * length), 1)
    col2 = jnp.where(col2 >= length, col2 - length, col2)
    if rev is None:
        half = pairs // 2
        back = lax.broadcasted_iota(jnp.int32, (pairs, length, length), 0) >= half
        back2 = lax.broadcasted_iota(jnp.int32, (pairs, length, 2 * length), 0) >= half
        ahead = jnp.where(back, (col - row)[None], (row - col)[None])
        ahead2 = jnp.where(back2, (col2 - row2)[None], (row2 - col2)[None])
        incl, strict2, incl2 = ahead >= 0, ahead2 > 0, ahead2 >= 0
    else:
        incl = ((row <= col) if rev else (row >= col))[None]
        strict2 = ((row2 < col2) if rev else (row2 > col2))[None]
        incl2 = ((row2 <= col2) if rev else (row2 >= col2))[None]
    first = (lax.broadcasted_iota(jnp.int32, (1, 1, width), 2) < hn).astype(F32)
    blocks = lambda t: jnp.concatenate([t * first, t * (1.0 - first)], axis=1)

    t_incl = jnp.broadcast_to(incl.astype(F32), (pairs, length, length))
    cum = _hdot(t_incl, lw, _NN, "cum")
    g = jnp.exp(cum)
    g_inv = jnp.exp(-cum)
    at = a * jnp.exp(cum - lw)
    rt = r * g
    bt = b * g_inv
    kt = k * g_inv
    both_rows = jnp.concatenate([at, rt], axis=1)
    gram = _hdot(both_rows, jnp.concatenate([blocks(bt), blocks(kt)], axis=1), _NT, "gram")
    a_ab = jnp.where(strict2, gram[:, :length, :width], 0.0)
    a_ak = jnp.where(strict2, gram[:, :length, width:], 0.0)
    a_rb = jnp.where(incl2, gram[:, length:, :width], 0.0)
    a_rk = jnp.where(incl2, gram[:, length:, width:], 0.0)
    from_state = _hdot(both_rows, s0, _NT, "apply")
    x = from_state[:, :length] + _hdot(a_ak, blocks(v), _NN, "apply")
    u = _tri_solve(a_ab, x, length, blocks)
    y = from_state[:, length:] + _hdot(jnp.concatenate([a_rb, a_rk], axis=2),
                                       jnp.concatenate([blocks(u), blocks(v)], axis=1), _NN, "apply")
    g_last = jnp.exp(jnp.sum(lw, axis=1, keepdims=True))
    ri = lax.broadcasted_iota(jnp.int32, (width, width), 0)
    ci = lax.broadcasted_iota(jnp.int32, (width, width), 1)
    same_head = ((ri < hn) == (ci < hn))[None]
    upd = _hdot(jnp.concatenate([u, v], axis=1), jnp.concatenate([bt, kt], axis=1), _TN, "state")
    s1 = (s0 + jnp.where(same_head, upd, 0.0)) * g_last
    return y, s1


def _split_pairs(x):
    return jnp.stack([x[:, p * LANES:(p + 1) * LANES] for p in range(x.shape[1] // LANES)])


def _merge_pairs(x):
    return jnp.concatenate([x[p] for p in range(x.shape[0])], axis=1)


def _scan_specs(views, rw, nc, rev):
    cidx = (lambda c: nc - 1 - c) if rev else (lambda c: c)
    seqs = [pl.BlockSpec((CHUNK, rw), functools.partial(lambda c, cb: (cidx(c), cb), cb=cb)) for _, cb, _ in views]
    plain = pl.BlockSpec((CHUNK, rw), lambda c: (cidx(c), 0))
    st = pl.BlockSpec((1, rw // LANES, LANES, LANES), lambda c: (cidx(c), 0, 0, 0))
    return seqs, plain, st


def _as_views(arrs, rw):
    return [t if isinstance(t, tuple) else (t, 0, rw) for t in arrs]


def _rwkv_scan_fwd(ops_f, ops_b, rw, *, name):
    S = _as_views(ops_f, rw)[0][0].shape[0]
    nc, pairs = S // CHUNK, rw // LANES
    in_specs, out_specs, arrays = [], [], []
    for rev, ops in ((False, ops_f), (True, ops_b)):
        views = _as_views(ops, rw)
        seqs, plain, st = _scan_specs(views, rw, nc, rev)
        in_specs += seqs
        out_specs += [plain, st]
        arrays += [t[0] for t in views]

    def both(refs_f, refs_b):
        return [jnp.concatenate([_split_pairs(f[...]), _split_pairs(b[...])], axis=0) for f, b in zip(refs_f, refs_b)]

    def body(*refs):
        (y_f, st_f, y_b, st_b), s_ref = refs[12:16], refs[16]

        @pl.when(pl.program_id(0) == 0)
        def _():
            s_ref[...] = jnp.zeros_like(s_ref)

        s0 = s_ref[...]
        st_f[0] = s0[:pairs]
        st_b[0] = s0[pairs:]
        y, s1 = _rwkv_chunk(None, s0, *both(refs[:6], refs[6:12]))
        y_f[...] = _merge_pairs(y[:pairs])
        y_b[...] = _merge_pairs(y[pairs:])
        s_ref[...] = s1

    return pl.pallas_call(
        body, name=name, grid=(nc,), in_specs=in_specs, out_specs=out_specs,
        out_shape=[jax.ShapeDtypeStruct((S, rw), F32), jax.ShapeDtypeStruct((nc, pairs, LANES, LANES), F32)] * 2,
        scratch_shapes=[pltpu.VMEM((2 * pairs, LANES, LANES), F32)],
        compiler_params=_cparams(("arbitrary",)),
    )(*arrays)


def _rwkv_scan_bwd(ops_f, ops_b, states_f, states_b, dy, rw, *, name):
    S = dy.shape[0]
    nc, pairs = S // CHUNK, rw // LANES
    in_specs, arrays = [], []
    for rev, ops, states in ((False, ops_f, states_f), (True, ops_b, states_b)):
        views = _as_views(list(ops) + [dy], rw)
        seqs, plain, st = _scan_specs(views, rw, nc, not rev)
        in_specs += seqs + [st]
        arrays += [t[0] for t in views] + [states]
    out_specs = []
    for rev in (False, True):
        out_specs += [_scan_specs([], rw, nc, not rev)[1]] * 6

    def both(refs_f, refs_b):
        return [jnp.concatenate([_split_pairs(f[...]), _split_pairs(b[...])], axis=0) for f, b in zip(refs_f, refs_b)]

    def body(*refs):
        ds_ref = refs[28]

        @pl.when(pl.program_id(0) == 0)
        def _():
            ds_ref[...] = jnp.zeros_like(ds_ref)

        s0 = jnp.concatenate([refs[7][0], refs[15][0]], axis=0)
        _, vjp = jax.vjp(functools.partial(_rwkv_chunk, None), s0, *both(refs[:6], refs[8:14]))
        (dy,) = both(refs[6:7], refs[14:15])
        grads = vjp((dy, ds_ref[...]))
        ds_ref[...] = grads[0]
        for o_f, o_b, gval in zip(refs[16:22], refs[22:28], grads[1:]):
            o_f[...] = _merge_pairs(gval[:pairs])
            o_b[...] = _merge_pairs(gval[pairs:])

    return pl.pallas_call(
        body, name=name, grid=(nc,), in_specs=in_specs, out_specs=out_specs,
        out_shape=[jax.ShapeDtypeStruct((S, rw), F32)] * 12,
        scratch_shapes=[pltpu.VMEM((2 * pairs, LANES, LANES), F32)],
        compiler_params=_cparams(("arbitrary",)),
    )(*arrays)


def _shift_lerp(x_view, mu, d=None, into=None, *, name):
    arr, off, width = x_view
    S = arr.shape[0]
    cb = _pick(width, 512)
    assert off % cb == 0

    def cshift(t):
        rows = lax.broadcasted_iota(jnp.int32, t.shape, 0)
        prev = jnp.where(rows == 0, 0.0, pltpu.roll(t, 1, 0))
        nxt = jnp.where(rows == S - 1, 0.0, pltpu.roll(t, S - 1, 0))
        return 0.5 * (prev + nxt)

    def fwd_body(x_ref, mu_ref, o_ref):
        x = x_ref[...]
        o_ref[...] = x + mu_ref[...] * (cshift(x) - x)

    def bwd_body(x_ref, mu_ref, d_ref, _, dx_ref, dmu_ref):
        x, m, dd = x_ref[...], mu_ref[...], d_ref[...]
        gm = m * dd
        dx_ref[...] = (dd - gm + cshift(gm)).astype(dx_ref.dtype)
        dmu_ref[...] = jnp.sum(dd * (cshift(x) - x), axis=0, keepdims=True)

    x_spec = pl.BlockSpec((S, cb), lambda j: (0, off // cb + j))
    blk = pl.BlockSpec((S, cb), lambda j: (0, j))
    vec = pl.BlockSpec((1, cb), lambda j: (0, j))
    if d is None:
        return pl.pallas_call(
            fwd_body, name=name, grid=(width // cb,), in_specs=[x_spec, vec], out_specs=blk,
            out_shape=jax.ShapeDtypeStruct((S, width), F32), compiler_params=_cparams(("parallel",)),
        )(arr, mu)
    buf, first = into
    assert first % cb == 0
    return pl.pallas_call(
        bwd_body, name=name, grid=(width // cb,),
        in_specs=[x_spec, vec, blk, pl.BlockSpec(memory_space=pl.ANY)],
        out_specs=[pl.BlockSpec((S, cb), lambda j: (0, first // cb + j)), vec],
        out_shape=[jax.ShapeDtypeStruct(buf.shape, buf.dtype), jax.ShapeDtypeStruct((1, width), F32)],
        input_output_aliases={3: 0}, compiler_params=_cparams(("parallel",)),
    )(arr, mu, d, buf)


def _attention_fwd(qfull, kv, kr, hm, scale, *, tq, name):
    S = qfull.shape[0]
    nt = (((1,), (1,)), ((), ()))

    def body(q_ref, kn_ref, kr_ref, v_ref, o_ref, lse_ref, k_scr):
        _head_keys(kn_ref, kr_ref, k_scr)
        s = lax.dot_general(q_ref[...], k_scr[...], nt, preferred_element_type=F32)
        m = jnp.max(s, axis=-1, keepdims=True)
        p = jnp.exp((s - m) * scale)
        l = jnp.sum(p, axis=-1, keepdims=True)
        o_ref[...] = jnp.dot(p.astype(BF16), v_ref[...], preferred_element_type=F32) * (1.0 / l)
        lse_ref[...] = jnp.broadcast_to(m * scale + jnp.log(l), lse_ref.shape)

    oblk = pl.BlockSpec((tq, VDIM), lambda h, i: (i, h))
    return pl.pallas_call(
        body, name=name, grid=(hm, S // tq),
        in_specs=[pl.BlockSpec((tq, QHEAD), lambda h, i: (i, h)),
                  pl.BlockSpec((S, NOPE), lambda h, i: (0, h)),
                  pl.BlockSpec((S, LANES), lambda h, i: (0, 0)),
                  pl.BlockSpec((S, VDIM), lambda h, i: (0, hm + h))],
        out_specs=[oblk, oblk],
        out_shape=[jax.ShapeDtypeStruct((S, hm * VDIM), F32)] * 2,
        scratch_shapes=[pltpu.VMEM((S, QHEAD), BF16)],
        compiler_params=_cparams(("parallel", "arbitrary")),
    )(qfull, kv, kr, kv)


def _head_keys(kn_ref, kr_ref, k_scr):
    @pl.when(pl.program_id(1) == 0)
    def _():
        k_scr[:, :NOPE] = kn_ref[...]
        k_scr[:, NOPE:] = kr_ref[...]


def _attention_bwd(qfull, kv, kr, o, lse, d_o, hm, scale, *, tq, name):
    S = qfull.shape[0]
    tq = min(tq, S)
    nq = S // tq
    tn = (((0,), (0,)), ((), ()))
    nt = (((1,), (1,)), ((), ()))

    def body(q_ref, kn_ref, kr_ref, v_ref, o_ref, lse_ref, do_ref, dq_ref, dk_ref, dv_ref, k_scr):
        _head_keys(kn_ref, kr_ref, k_scr)
        s = lax.dot_general(q_ref[...], k_scr[...], nt, preferred_element_type=F32)
        p = jnp.exp(s * scale - lse_ref[:, 0:1])
        d_out = do_ref[...]
        delta = jnp.sum(d_out * o_ref[...], axis=-1, keepdims=True)
        d_out = d_out.astype(BF16)
        dp = lax.dot_general(d_out, v_ref[...], nt, preferred_element_type=F32)
        ds = (p * (dp - delta)).astype(BF16)
        dq_ref[...] = jnp.dot(ds, k_scr[...], preferred_element_type=F32) * scale
        dv = lax.dot_general(p.astype(BF16), d_out, tn, preferred_element_type=F32)
        dk = lax.dot_general(ds, q_ref[...], tn, preferred_element_type=F32)
        i = pl.program_id(1)
        for ref, val in ((dk_ref, dk), (dv_ref, dv)):
            @pl.when(i == 0)
            def _(ref=ref, val=val):
                ref[...] = val

            @pl.when(i > 0)
            def _(ref=ref, val=val):
                ref[...] += val

        @pl.when(i == nq - 1)
        def _():
            dk_ref[...] = dk_ref[...] * scale

    qblk = pl.BlockSpec((tq, QHEAD), lambda h, i: (i, h))
    oblk = pl.BlockSpec((tq, VDIM), lambda h, i: (i, h))
    return pl.pallas_call(
        body, name=name, grid=(hm, nq),
        in_specs=[qblk,
                  pl.BlockSpec((S, NOPE), lambda h, i: (0, h)),
                  pl.BlockSpec((S, LANES), lambda h, i: (0, 0)),
                  pl.BlockSpec((S, VDIM), lambda h, i: (0, hm + h)),
                  oblk, oblk, oblk],
        out_specs=[qblk, pl.BlockSpec((S, QHEAD), lambda h, i: (0, h)), pl.BlockSpec((S, VDIM), lambda h, i: (0, h))],
        out_shape=[jax.ShapeDtypeStruct((S, hm * QHEAD), F32), jax.ShapeDtypeStruct((S, hm * QHEAD), F32),
                   jax.ShapeDtypeStruct((S, hm * VDIM), F32)],
        scratch_shapes=[pltpu.VMEM((S, QHEAD), BF16)],
        compiler_params=_cparams(("parallel", "arbitrary")),
    )(qfull, kv, kr, kv, o, lse, d_o)


def _layout(D, MW, RW, TAIL, QR, KVR):
    names = ["gate_m", "gate_r", "z_m", "z_r", "q_a", "kv_a", "r", "k", "v", "tail"]
    widths = [D, D, MW, RW, QR, KVR, RW, RW, RW, TAIL]
    offs, o = {}, 0
    for nme, w in zip(names, widths):
        assert o % w == 0, (nme, o, w)
        offs[nme] = (o, w)
        o += w
    return offs, o


def _local_grads(x, target, W, dims, exchange=None):
    S, D = x.shape
    hm, hr, hn, rank = dims["hm"], dims["hr"], dims["hn"], dims["rank"]
    MW, RW = hm * VDIM, hr * hn
    TAIL = dims["TAIL"]
    QR, KVR = W["mla_q_norm"].shape[1], W["mla_kv_norm"].shape[1]
    lay, d_in = _layout(D, MW, RW, TAIL, QR, KVR)
    T = 256
    scale = (NOPE + ROPE) ** -0.5
    col = lambda arr, nme: _view(arr, *lay[nme])

    pos = jnp.arange(S, dtype=F32)
    inv_freq = jnp.power(ROPE_THETA, -jnp.arange(0, ROPE, 2, dtype=F32) / ROPE)
    ang = pos[:, None] * inv_freq[None, :]
    zpad = jnp.zeros((S, LANES - ROPE), F32)
    cosx = jnp.concatenate([jnp.cos(ang), jnp.cos(ang), zpad], axis=1)
    sinx = jnp.concatenate([jnp.sin(ang), jnp.sin(ang), zpad], axis=1)
    ri, ci = jnp.arange(LANES)[:, None], jnp.arange(LANES)[None, :]
    half = ROPE // 2
    rot = (jnp.where((ri == ci - half) & (ci >= half) & (ci < ROPE), 1.0, 0.0)
           - jnp.where((ri == ci + half) & (ci < half), 1.0, 0.0)).astype(BF16)
    seg = (jnp.arange(RW)[:, None] // hn == jnp.arange(LANES)[None, :]).astype(BF16)
    stacked = lambda t: jnp.concatenate([t, t], axis=0)
    rot, rot_t, seg, seg_t = stacked(rot), stacked(rot.T), stacked(seg), stacked(seg.T)

    (h,) = _rowwise(lambda xb, g: (_rms(xb, g),), [x], [W["g_pre"]], [(D, BF16)], tile=2 * T, name="pre_norm")
    if exchange is None:
        proj = _mm(h, W["w_in_t"], tb=True, name="in_proj")
    else:
        proj, *slabs = _mm(h, W["w_in_t"], tb=True, ride=_gather_plan(exchange[0]), name="in_proj")
        W = {**W, **_prepare_rest(dict(zip(_MATS[1:], slabs)), dims)}

    qn, kvn = _rowwise(_f_mla_norm, [col(proj, "q_a"), col(proj, "kv_a")], [W["mla_q_norm"], W["mla_kv_norm"]],
                       [(QR, BF16), (KVR, BF16)], tile=2 * T, name="mla_norm")
    qraw = _mm(qn, W["wq_b_t"], tb=True, name="q_up")
    kv = _mm(kvn, W["wkv_b"], out_dtype=BF16, name="kv_up")
    kr_view = _view(proj, lay["tail"][0], LANES)
    qfull, kr = _rowwise(functools.partial(_f_rope, hm), [qraw, kr_view, cosx, sinx], [rot, rot_t],
                         [(hm * QHEAD, BF16), (LANES, BF16)], tile=2 * T, name="rope")
    o_mla, lse = _attention_fwd(qfull, kv, kr, hm, scale, tq=T, name="attn_fwd")

    shift_view = (proj, lay["r"][0], 3 * RW + TAIL)
    rl = _shift_lerp(shift_view, W["mu"], name="shift_fwd")
    rl_r, rl_k, rl_v = _view(rl, 0, RW), _view(rl, RW, RW), _view(rl, 2 * RW, RW)
    rl_tail = _view(rl, 3 * RW, TAIL)
    pre_params = [W["w0_f"], W["w0_b"], W["a0_f"], W["a0_b"], W["k_k"], W["k_a"], W["w2cat"], W["a2cat"], seg, seg_t]
    pre_fn = functools.partial(_f_rwkv_pre, RW)
    lw_f, lw_b, k_f, k_b, a_n, b_f, b_b = _rowwise(pre_fn, [rl_k, rl_tail], pre_params, [(RW, F32)] * 7, tile=2 * T,
                                                    name="rwkv_pre")
    ops_f = (rl_r, lw_f, k_f, rl_v, a_n, b_f)
    ops_b = (rl_r, lw_b, k_b, rl_v, a_n, b_b)
    y_f, st_f, y_b, st_b = _rwkv_scan_fwd(ops_f, ops_b, RW, name="scan_fwd")

    post_fn = functools.partial(_f_post, hn)
    post_rows = [y_f, y_b, rl_r, k_f, k_b, rl_v, col(proj, "z_r"), o_mla, col(proj, "z_m")]
    post_params = [W["gn_g"], W["gn_b"], W["r_k"], seg, seg_t]
    ymg, yrg = _rowwise(post_fn, post_rows, post_params, [(MW, BF16), (RW, BF16)], tile=T, name="post")
    u_m = _mm(ymg, W["w_br_mla"], name="br_mla")
    u_r = _mm(yrg, W["w_br_rwkv"], name="br_rwkv")
    merge_rows = [u_m, u_r, col(proj, "gate_m"), col(proj, "gate_r")]
    (merged,) = _rowwise(lambda *t: (_f_merge(*t),), merge_rows, [], [(D, BF16)], tile=T, name="merge")
    out = _mm(merged, W["w_out"], name="out_proj")

    def head(ob, xb, tb, g):
        yn, vjp = jax.vjp(_rms, ob, g)
        err = xb + yn - tb
        dy = err * (1.0 / D)
        d_ob, d_g = vjp(dy)
        loss = jnp.broadcast_to(0.5 * jnp.sum(err * err) * (1.0 / D), (1, LANES))
        return dy, d_ob, loss, d_g

    dy, d_out, loss, g_g_post = _rowwise(head, [out, x, target], [W["g_post"]], [(D, F32), (D, BF16)],
                                         [(1, LANES), (1, D)], tile=2 * T, name="head")
    d_merged = _mm(d_out, W["w_out"], tb=True, name="d_merged")
    g_w_out = _mm(merged, d_out, ta=True, out_dtype=BF16, name="g_w_out")

    def merge_bwd(u_m_b, u_r_b, g_m_b, g_r_b, dm):
        _, vjp = jax.vjp(_f_merge, u_m_b, u_r_b, g_m_b, g_r_b)
        du_m, du_r, dg_m, dg_r = vjp(dm)
        return du_m, du_r, jnp.concatenate([dg_m, dg_r], axis=1)

    d_u_m, d_u_r, d_proj = _rowwise(merge_bwd, merge_rows + [d_merged], [],
                                    [(D, BF16), (D, BF16), (2 * D, BF16, (None, d_in, lay["gate_m"][0]))], tile=T,
                                    name="merge_bwd")
    d_ymg = _mm(d_u_m, W["w_br_mla"], tb=True, name="d_ymg")
    d_yrg = _mm(d_u_r, W["w_br_rwkv"], tb=True, name="d_yrg")
    g_w_br_mla = _mm(ymg, d_u_m, ta=True, out_dtype=BF16, name="g_w_br_mla")
    g_w_br_rwkv = _mm(yrg, d_u_r, ta=True, out_dtype=BF16, name="g_w_br_rwkv")

    def post_bwd(*args):
        nr = len(post_rows)
        prim, dm, dr = args[:nr] + args[nr + 2:], args[nr], args[nr + 1]
        _, vjp = jax.vjp(post_fn, *prim)
        g = vjp((dm, dr))
        return g[0], g[2], g[3], g[5], g[7], jnp.concatenate([g[8], g[6]], axis=1), g[9], g[10], g[11]

    (d_y, d_r_bonus, d_k_bonus, d_v_bonus, d_o, d_proj, g_gn_g, g_gn_b, g_r_k) = _rowwise(
        post_bwd, post_rows + [d_ymg, d_yrg], post_params,
        [(RW, F32), (RW, F32), (RW, F32), (RW, F32), (MW, F32), (MW + RW, BF16, (d_proj, d_in, lay["z_m"][0]))],
        [(1, RW)] * 3, tile=T, name="post_bwd")

    dscan = _rwkv_scan_bwd(ops_f, ops_b, st_f, st_b, d_y, RW, name="scan_bwd")
    dsc = {"f": dscan[:6], "b": dscan[6:]}

    d_q_att, d_k_att, d_v_att = _attention_bwd(qfull, kv, kr, o_mla, lse, d_o, hm, scale, tq=4 * T, name="attn_bwd")

    def rope_bwd(qraw_b, kr_in, cos_b, sin_b, dq_b, dk_b, dv_b, rot_b, rot_t_b):
        _, vjp = jax.vjp(lambda q_, k_: _f_rope(hm, q_, k_, cos_b, sin_b, rot_b, rot_t_b), qraw_b, kr_in)
        dkn = jnp.concatenate([dk_b[:, hh * QHEAD:hh * QHEAD + NOPE] for hh in range(hm)], axis=1)
        dkr = dk_b[:, NOPE:QHEAD]
        for hh in range(1, hm):
            dkr = dkr + dk_b[:, hh * QHEAD + NOPE:(hh + 1) * QHEAD]
        d_qraw, d_kr_in = vjp((dq_b, dkr))
        return d_qraw, jnp.concatenate([dkn, dv_b], axis=1), d_kr_in

    d_qraw, d_kv, d_kr_in = _rowwise(rope_bwd, [qraw, kr_view, cosx, sinx, d_q_att, d_k_att, d_v_att],
                                     [rot, rot_t], [(hm * QHEAD, BF16), (2 * MW, BF16), (LANES, F32)], tile=T,
                                     name="rope_bwd")
    d_qnorm = _mm(d_qraw, W["wq_b_t"], name="d_qn")
    d_kvnorm = _mm(d_kv, W["wkv_b"], tb=True, name="d_kvn")
    g_wq_b = _mm(d_qraw, qn, ta=True, out_dtype=BF16, name="g_wq_b")
    g_wkv_b = _mm(kvn, d_kv, ta=True, out_dtype=BF16, name="g_wkv_b")

    def mla_norm_bwd(q_a, kv_a, qg, kvg, dq, dk):
        _, vjp = jax.vjp(_f_mla_norm, q_a, kv_a, qg, kvg)
        d_q_a, d_kv_a, d_qg, d_kvg = vjp((dq, dk))
        return jnp.concatenate([d_q_a, d_kv_a], axis=1), d_qg, d_kvg

    d_proj, g_q_norm, g_kv_norm = _rowwise(
        lambda q_a, kv_a, dq, dk, qg, kvg: mla_norm_bwd(q_a, kv_a, qg, kvg, dq, dk),
        [col(proj, "q_a"), col(proj, "kv_a"), d_qnorm, d_kvnorm], [W["mla_q_norm"], W["mla_kv_norm"]],
        [(QR + KVR, BF16, (d_proj, d_in, lay["q_a"][0]))], [(1, QR), (1, KVR)], tile=2 * T, name="mla_norm_bwd")

    def pre_bwd(k_b_, tail_b, dlwf, dlwb, dkf, dkb, dkbon, daf, dab, dbf, dbb, drf, drb, drbon, dvf, dvb, dvbon,
                dkr, *params):
        w2, a2 = params[6], params[7]
        nt, tn = (((1,), (1,)), ((), ())), (((0,), (0,)), ((), ()))
        split = w2.shape[0]
        th = jnp.tanh(tail_b[:, :split])
        th_b, tail_h = th.astype(BF16), tail_b[:, split:].astype(BF16)
        zw = jnp.dot(th_b, w2, preferred_element_type=F32)
        za = jnp.dot(tail_h, a2, preferred_element_type=F32)
        _, vjp = jax.vjp(functools.partial(_f_rwkv_core, RW), k_b_, zw, za, *params[:6], params[8], params[9])
        g = vjp((dlwf, dlwb, dkf + dkbon, dkb + dkbon, daf + dab, dbf, dbb))
        d_zw, d_za = g[1].astype(BF16), g[2].astype(BF16)
        d_tail = (jnp.concatenate([lax.dot_general(d_zw, w2, nt, preferred_element_type=F32) * (1.0 - th * th),
                                   lax.dot_general(d_za, a2, nt, preferred_element_type=F32)], axis=1)
                  + jnp.concatenate([dkr, jnp.zeros((dkr.shape[0], TAIL - LANES), F32)], axis=1))
        g_w2 = lax.dot_general(th_b, d_zw, tn, preferred_element_type=F32)
        g_a2 = lax.dot_general(tail_h, d_za, tn, preferred_element_type=F32)
        d_rl = jnp.concatenate([drf + drb + drbon, g[0], dvf + dvb + dvbon, d_tail], axis=1)
        return (d_rl,) + tuple(g[3:9]) + (g_w2, g_a2)

    f_, b_ = dsc["f"], dsc["b"]
    pre_bwd_rows = [rl_k, rl_tail, f_[1], b_[1], f_[2], b_[2], d_k_bonus, f_[4], b_[4], f_[5], b_[5],
                    f_[0], b_[0], d_r_bonus, f_[3], b_[3], d_v_bonus, d_kr_in]
    (d_rl, g_w0_f, g_w0_b, g_a0_f, g_a0_b, g_k_k, g_k_a, g_w2cat, g_a2cat) = _rowwise(
        pre_bwd, pre_bwd_rows, pre_params, [(3 * RW + TAIL, F32)],
        [(1, RW)] * 6 + [W["w2cat"].shape, W["a2cat"].shape], tile=T // 2, name="rwkv_pre_bwd")
    d_proj, g_mu = _shift_lerp(shift_view, W["mu"], d_rl, (d_proj, lay["r"][0]), name="shift_bwd")
    small = dict(wq_b=g_wq_b, wkv_b=g_wkv_b, w2cat=g_w2cat, a2cat=g_a2cat, w_br_mla=g_w_br_mla,
                 w_br_rwkv=g_w_br_rwkv, w_out=g_w_out)
    if exchange is None:
        received = None
        g_w_in = _mm(d_proj, h, ta=True, out_dtype=BF16, tn_cap=1024, name="g_w_in")
        d_h = _mm(d_proj, W["w_in_t"], tn_cap=1024, name="d_h")
    else:
        slabs = _restore_rest(small, dims)
        slabs = [slabs[n] for n in _MATS[1:]]
        g_w_in, *got = _mm(d_proj, h, ta=True, out_dtype=BF16, tn_cap=1024, ride=_sibling_swap_plan(slabs),
                           name="g_w_in")
        sums = [_pair_add(exchange[1], s, t, name="pair_add_" + n) for n, s, t in zip(_MATS[1:], slabs, got)]
        g_w_in = _restore_w_in(g_w_in, dims)
        d_h, *received = _mm(d_proj, W["w_in_t"], tn_cap=1024, name="d_h",
                             ride=_join_plans(_chip_exchange_plan(sums), _sibling_swap_plan([g_w_in])))
        small = {}

    def pre_norm_bwd(xb, dyb, dhb, g):
        _, vjp = jax.vjp(_rms, xb, g)
        dx, dg = vjp(dhb)
        return dyb + dx, dg

    if exchange is None:
        grad_x, g_g_pre = _rowwise(pre_norm_bwd, [x, dy, d_h], [W["g_pre"]], [(D, F32)], [(1, D)], tile=2 * T,
                                   name="pre_norm_bwd")
    else:
        sums_w_in = _pair_add(exchange[1], g_w_in, received[-1], name="pair_add_w_in")
        grad_x, g_g_pre, recv_w_in = _rowwise(pre_norm_bwd, [x, dy, d_h], [W["g_pre"]], [(D, F32)], [(1, D)],
                                              tile=2 * T, name="pre_norm_bwd",
                                              ride=_chip_exchange_plan([sums_w_in]))
        received = received[:-1] + [recv_w_in]

    grads = dict(g_pre=g_g_pre, w_in=g_w_in, mla_q_norm=g_q_norm, mla_kv_norm=g_kv_norm, mu=g_mu, w0_f=g_w0_f,
                 w0_b=g_w0_b, a0_f=g_a0_f, a0_b=g_a0_b, k_k=g_k_k, k_a=g_k_a, r_k=g_r_k, gn_g=g_gn_g, gn_b=g_gn_b,
                 g_post=g_g_post, **small)
    return loss[0, 0], grad_x, grads, received


_MATS = ["w_in", "mla_wq_b", "mla_wkv_b", "rwkv_w2_f", "rwkv_w2_b", "rwkv_a2_f", "rwkv_a2_b", "w_br_mla",
         "w_br_rwkv", "w_out"]
_ROW_SHARDED = ("w_out",)
_TRANSPOSED = ("w_in", "mla_wq_b")
_VECS = ["g_pre", "mla_q_norm", "mla_kv_norm", "rwkv_mu", "rwkv_w0_f", "rwkv_w0_b", "rwkv_a0_f", "rwkv_a0_b",
         "rwkv_k_k", "rwkv_k_a", "rwkv_r_k", "rwkv_gn_g", "rwkv_gn_b", "g_post"]
_WEIGHTS = ["g_pre", "w_in", "mla_q_norm", "mla_wq_b", "mla_kv_norm", "mla_wkv_b", "rwkv_mu", "rwkv_w0_f",
            "rwkv_w2_f", "rwkv_w0_b", "rwkv_w2_b", "rwkv_a0_f", "rwkv_a2_f", "rwkv_a0_b", "rwkv_a2_b", "rwkv_k_k",
            "rwkv_k_a", "rwkv_r_k", "rwkv_gn_g", "rwkv_gn_b", "w_br_mla", "w_br_rwkv", "w_out", "g_post"]

def _direct_gather_plan(src):
    def phases(src_refs, out_refs, sem_refs):
        (src_ref,), (out_ref,), sems, local_sem = src_refs, out_refs, sem_refs[:2], sem_refs[2]
        x, y, c = lax.axis_index("x"), lax.axis_index("y"), lax.axis_index("c")
        me = 4 * x + 2 * y + c
        flip = lambda v, bit: (1 - v) if bit else v
        peers = [(flip(x, d & 4), flip(y, d & 2), flip(c, d & 1)) for d in range(1, N_DEV)]
        own = lambda: pltpu.make_async_copy(src_ref, out_ref.at[me], local_sem)
        send = lambda d: _remote(src_ref, out_ref.at[me], sems, d, peers[d])

        def first():
            own().start()
            for d in range(N_DEV - 1):
                send(d).start()

        def last():
            for d, (px, py, pc) in enumerate(peers):
                blk = out_ref.at[4 * px + 2 * py + pc]
                _remote(blk, blk, sems, d, (x, y, c)).wait_recv()
            for d in range(N_DEV - 1):
                send(d).wait_send()
            own().wait()

        return first, (lambda: None), last

    return [src], [jax.ShapeDtypeStruct((N_DEV,) + src.shape, src.dtype)], [(N_DEV - 1,), (N_DEV - 1,), ()], phases


def _remote(src, dst, sems, key, to):
    send_sems, recv_sems = sems
    return pltpu.make_async_remote_copy(src_ref=src, dst_ref=dst, send_sem=send_sems.at[key], recv_sem=recv_sems.at[key],
                                        device_id=to, device_id_type=pl.DeviceIdType.MESH)


def _run_exchange(plan, *, name):
    srcs, out_shapes, sem_shapes, phases = plan
    n, m = len(srcs), len(out_shapes)

    def body(*refs):
        for phase in phases(refs[:n], refs[n:n + m], refs[n + m:]):
            phase()

    return pl.pallas_call(
        body, name=name, out_shape=out_shapes,
        in_specs=[pl.BlockSpec(memory_space=pl.ANY)] * n, out_specs=[pl.BlockSpec(memory_space=pl.ANY)] * m,
        scratch_shapes=[pltpu.SemaphoreType.DMA(s) for s in sem_shapes],
    )(*srcs)


def _join_plans(p, q):
    (srcs_p, outs_p, sems_p, phases_p), (srcs_q, outs_q, sems_q, phases_q) = p, q

    def phases(src_refs, out_refs, sem_refs):
        a = phases_p(src_refs[:len(srcs_p)], out_refs[:len(outs_p)], sem_refs[:len(sems_p)])
        b = phases_q(src_refs[len(srcs_p):], out_refs[len(outs_p):], sem_refs[len(sems_p):])

        def both(fa, fb):
            def run():
                fa()
                fb()
            return run

        return tuple(both(fa, fb) for fa, fb in zip(a, b))

    return list(srcs_p) + list(srcs_q), list(outs_p) + list(outs_q), list(sems_p) + list(sems_q), phases


def _gather_plan(srcs):
    n = len(srcs)

    def phases(src_refs, out_refs, sem_refs):
        sems, local_sems = sem_refs[:2], sem_refs[2]
        x, y, c = lax.axis_index("x"), lax.axis_index("y"), lax.axis_index("c")
        idx = lambda px, py, pc: 4 * px + 2 * py + pc
        me, sibling = (x, y, c), (x, y, 1 - c)
        chips = [(1 - x, y), (x, 1 - y), (1 - x, 1 - y)]
        own = lambda a: pltpu.make_async_copy(src_refs[a], out_refs[a].at[idx(*me)], local_sems.at[a])
        to_sibling = lambda a: _remote(src_refs[a], out_refs[a].at[idx(*me)], sems, (0, a), sibling)
        to_chip = lambda a, j: _remote(src_refs[a], out_refs[a].at[idx(*me)], sems, (1 + j, a), (*chips[j], c))
        landed = lambda a, j: out_refs[a].at[idx(*chips[j], c)]
        passed_on = lambda a, j: _remote(landed(a, j), landed(a, j), sems, (4 + j, a), sibling)

        def first():
            for a in range(n):
                own(a).start()
                to_sibling(a).start()
                for j in range(3):
                    to_chip(a, j).start()

        def middle():
            for j in range(3):
                for a in range(n):
                    _remote(landed(a, j), landed(a, j), sems, (1 + j, a), me).wait_recv()
                    passed_on(a, j).start()

        def last():
            for a in range(n):
                blk = out_refs[a].at[idx(*sibling)]
                _remote(blk, blk, sems, (0, a), me).wait_recv()
                for j in range(3):
                    blk = out_refs[a].at[idx(*chips[j], 1 - c)]
                    _remote(blk, blk, sems, (4 + j, a), me).wait_recv()
            for a in range(n):
                to_sibling(a).wait_send()
                for j in range(3):
                    to_chip(a, j).wait_send()
                    passed_on(a, j).wait_send()
                own(a).wait()

        return first, middle, last

    return srcs, [jax.ShapeDtypeStruct((N_DEV,) + s.shape, s.dtype) for s in srcs], [(7, n), (7, n), (n,)], phases


def _sibling_swap_plan(srcs):
    n = len(srcs)

    def phases(src_refs, out_refs, sems):
        x, y, c = lax.axis_index("x"), lax.axis_index("y"), lax.axis_index("c")
        copies = lambda: [_remote(src_refs[a].at[2 * q + 1 - c], out_refs[a].at[q], sems, (q, a), (x, y, 1 - c))
                          for a in range(n) for q in range(4)]

        def first():
            for cp in copies():
                cp.start()

        def last():
            for cp in copies():
                cp.wait()

        return first, (lambda: None), last

    return srcs, [jax.ShapeDtypeStruct((4,) + s.shape[1:], s.dtype) for s in srcs], [(4, n), (4, n)], phases


def _chip_exchange_plan(srcs):
    n = len(srcs)

    def phases(src_refs, out_refs, sem_refs):
        sems, local_sems = sem_refs[:2], sem_refs[2]
        x, y, c = lax.axis_index("x"), lax.axis_index("y"), lax.axis_index("c")
        mine = 2 * x + y
        chips = [(1 - x, y), (x, 1 - y), (1 - x, 1 - y)]
        own = lambda a: pltpu.make_async_copy(src_refs[a].at[mine], out_refs[a].at[mine], local_sems.at[a])
        send = lambda a, j: _remote(src_refs[a].at[2 * chips[j][0] + chips[j][1]], out_refs[a].at[mine], sems, (j, a),
                                    (*chips[j], c))

        def first():
            for a in range(n):
                own(a).start()
                for j in range(3):
                    send(a, j).start()

        def last():
            for j in range(3):
                for a in range(n):
                    blk = out_refs[a].at[2 * chips[j][0] + chips[j][1]]
                    _remote(blk, blk, sems, (j, a), (x, y, c)).wait_recv()
            for a in range(n):
                for j in range(3):
                    send(a, j).wait_send()
                own(a).wait()

        return first, (lambda: None), last

    return srcs, [jax.ShapeDtypeStruct(s.shape, s.dtype) for s in srcs], [(3, n), (3, n), (n,)], phases


def _pair_add(core, g, got, *, name):
    q, r, c = got.shape
    tr, tc = _tile2d(r, c, cap=1024)

    def body(core_ref, a_ref, b_ref, o_ref):
        o_ref[...] = (a_ref[...].astype(F32) + b_ref[...].astype(F32)).astype(BF16)

    blk = pl.BlockSpec((1, tr, tc), lambda i, j, k, core_ref: (i, j, k))
    mine = pl.BlockSpec((1, tr, tc), lambda i, j, k, core_ref: (2 * i + core_ref[0], j, k))
    return pl.pallas_call(
        body, name=name, out_shape=jax.ShapeDtypeStruct(got.shape, BF16),
        grid_spec=pltpu.PrefetchScalarGridSpec(num_scalar_prefetch=1, grid=(q, r // tr, c // tc),
                                               in_specs=[mine, blk], out_specs=blk),
        compiler_params=_cparams(("parallel", "parallel", "parallel")))(core, g, got)


def _adamw(recv, w, m, v, *, name, ride=None):
    r, c = w.shape
    n_terms = recv.shape[0]
    tr, tc = _tile2d(r, c)
    nj = c // tc
    steps = (r // tr) * nj
    srcs, extra_shapes, sem_shapes, phases = ride if ride else ((), (), (), None)
    n_src, n_extra = len(srcs), len(extra_shapes)

    def body(*refs):
        g_ref, w_ref, m_ref, v_ref = refs[:4]
        go_ref, d_ref, mo_ref, vo_ref = refs[4 + n_src:8 + n_src]
        if ride:
            step = pl.program_id(0) * nj + pl.program_id(1)
            first, middle, last = phases(refs[4:4 + n_src], refs[8 + n_src:8 + n_src + n_extra],
                                         refs[8 + n_src + n_extra:])
            pl.when(step == 0)(first)
            pl.when(step == (steps * 15) // 16)(middle)
        g = g_ref[0].astype(F32)
        for k in range(1, n_terms):
            g = g + g_ref[k].astype(F32)
        m_new = ADAM_B1 * m_ref[...] + (1.0 - ADAM_B1) * g
        v_new = ADAM_B2 * v_ref[...] + (1.0 - ADAM_B2) * (g * g)
        m_hat = m_new / (1.0 - ADAM_B1 ** ADAM_STEP)
        v_hat = v_new / (1.0 - ADAM_B2 ** ADAM_STEP)
        go_ref[...] = g
        d_ref[...] = -ADAM_LR * (m_hat / (jnp.sqrt(v_hat) + ADAM_EPS) + ADAM_WD * w_ref[...])
        mo_ref[...] = m_new
        vo_ref[...] = v_new
        if ride:
            pl.when(step == steps - 1)(last)

    blk = pl.BlockSpec((tr, tc), lambda i, j: (i, j))
    hbm = pl.BlockSpec(memory_space=pl.ANY)
    return pl.pallas_call(
        body, name=name, grid=(r // tr, nj),
        in_specs=[pl.BlockSpec((n_terms, tr, tc), lambda i, j: (0, i, j)), blk, blk, blk] + [hbm] * n_src,
        out_specs=[blk] * 4 + [hbm] * n_extra,
        out_shape=[jax.ShapeDtypeStruct((r, c), F32)] * 4 + list(extra_shapes),
        scratch_shapes=[pltpu.SemaphoreType.DMA(s) for s in sem_shapes],
        compiler_params=_cparams(("arbitrary", "arbitrary") if ride else ("parallel", "parallel")),
    )(recv, w, m, v, *srcs)


def _tile2d(r, c, cap=256):
    if r <= cap:
        return r, c
    for t in range(cap - cap % BF16_ROWS, 0, -BF16_ROWS):
        if r % t == 0:
            return t, c
    return r, _pick(c, cap)


def _pack(pieces):
    total = sum(p.shape[0] for p in pieces)
    pad = (-total) % (8 * LANES)
    flat = jnp.concatenate(list(pieces) + [jnp.zeros((pad,), F32)])
    return flat.reshape(-1, LANES)


def _unpack(flat, sizes):
    flat = flat.reshape(-1)
    out, o = [], 0
    for n in sizes:
        out.append(flat[o:o + n])
        o += n
    return out


def _prepare_weights(full, vec, dims):
    rest = {n: t for n, t in full.items() if n != "w_in"}
    return {"w_in_t": _prepare_w_in(full["w_in"], dims), **_prepare_rest(rest, dims), **_prepare_vectors(vec, dims)}


def _prepare_w_in(slabs, dims):
    D = dims["D"]
    flat = slabs.reshape(-1, D)
    parts, pos = [], 0
    for orig_off, width, perm_off in sorted(dims["segs"], key=lambda t: t[2]):
        if perm_off > pos:
            parts.append(jnp.zeros((perm_off - pos, D), BF16))
        parts.append(flat[orig_off:orig_off + width])
        pos = perm_off + width
    if dims["d_in_perm"] > pos:
        parts.append(jnp.zeros((dims["d_in_perm"] - pos, D), BF16))
    return jnp.concatenate(parts, axis=0)


def _prepare_rest(full, dims):
    hm, hr, hn, rank = dims["hm"], dims["hr"], dims["hn"], dims["rank"]
    QR, KVR = dims["QR"], dims["KVR"]
    RW, TAIL = hr * hn, dims["TAIL"]
    full = {n: (t.reshape(-1, t.shape[2]) if n in _ROW_SHARDED + _TRANSPOSED
                else t.transpose(1, 0, 2).reshape(t.shape[1], -1)) for n, t in full.items()}
    wq = full["mla_wq_b"].reshape(hm, NOPE + ROPE, QR)
    wq = jnp.concatenate([wq, jnp.zeros((hm, QHEAD - NOPE - ROPE, QR), BF16)], axis=1).reshape(hm * QHEAD, QR)
    wkv = full["mla_wkv_b"].reshape(KVR, hm, 2, NOPE).transpose(0, 2, 1, 3).reshape(KVR, 2 * hm * NOPE)
    z = lambda rows: jnp.zeros((rows, RW), BF16)
    f = lambda nme: full[nme]
    split = ROPE + 2 * rank
    assert split % LANES == 0, split
    w2cat = jnp.concatenate([
        jnp.concatenate([z(ROPE), f("rwkv_w2_f"), z(rank)], axis=0),
        jnp.concatenate([z(ROPE + rank), f("rwkv_w2_b")], axis=0)], axis=1)
    a2cat = jnp.concatenate([
        jnp.concatenate([f("rwkv_a2_f"), z(TAIL - split - rank)], axis=0),
        jnp.concatenate([z(rank), f("rwkv_a2_b"), z(TAIL - split - 2 * rank)], axis=0)], axis=1)
    return dict(wq_b_t=wq, wkv_b=wkv, w2cat=w2cat, a2cat=a2cat, w_br_mla=full["w_br_mla"],
                w_br_rwkv=full["w_br_rwkv"], w_out=full["w_out"])


def _prepare_vectors(vec, dims):
    rank, RW, TAIL = dims["rank"], dims["hr"] * dims["hn"], dims["TAIL"]
    mu = vec["rwkv_mu"]
    mu_p = jnp.concatenate([mu[:3 * RW], jnp.zeros((ROPE,), F32), mu[3 * RW:],
                            jnp.zeros((TAIL - ROPE - 4 * rank,), F32)])
    row = lambda t: t.reshape(1, -1)
    return dict(
        mu=row(mu_p), g_pre=row(vec["g_pre"]), g_post=row(vec["g_post"]), mla_q_norm=row(vec["mla_q_norm"]),
        mla_kv_norm=row(vec["mla_kv_norm"]), w0_f=row(vec["rwkv_w0_f"]), w0_b=row(vec["rwkv_w0_b"]),
        a0_f=row(vec["rwkv_a0_f"]), a0_b=row(vec["rwkv_a0_b"]), k_k=row(vec["rwkv_k_k"]), k_a=row(vec["rwkv_k_a"]),
        r_k=row(vec["rwkv_r_k"]), gn_g=row(vec["rwkv_gn_g"]), gn_b=row(vec["rwkv_gn_b"]))


def _restore_grads(g, dims):
    return {"w_in": _restore_w_in(g["w_in"], dims), **_restore_rest(g, dims), **_restore_vectors(g, dims)}


def _restore_w_in(gw, dims):
    parts = [gw[perm_off:perm_off + width] for _, width, perm_off in sorted(dims["segs"])]
    return jnp.concatenate(parts, axis=0).reshape(N_DEV, dims["d_in"] // N_DEV, gw.shape[1])


def _restore_rest(g, dims):
    hm, hr, hn, rank = dims["hm"], dims["hr"], dims["hn"], dims["rank"]
    QR, KVR, RW = dims["QR"], dims["KVR"], hr * hn
    wq = g["wq_b"].reshape(hm, QHEAD, QR)[:, :NOPE + ROPE].reshape(N_DEV, -1, QR)
    wkv = g["wkv_b"].reshape(KVR, 2, hm, NOPE).transpose(0, 2, 1, 3).reshape(KVR, 2 * hm * NOPE)
    lo = lambda t, first, half: t[first:first + rank, half * RW:(half + 1) * RW].astype(BF16)
    cols = lambda t: t.reshape(t.shape[0], N_DEV, -1).transpose(1, 0, 2)
    return dict(
        mla_wq_b=wq, mla_wkv_b=cols(wkv), rwkv_w2_f=cols(lo(g["w2cat"], ROPE, 0)),
        rwkv_w2_b=cols(lo(g["w2cat"], ROPE + rank, 1)), rwkv_a2_f=cols(lo(g["a2cat"], 0, 0)),
        rwkv_a2_b=cols(lo(g["a2cat"], rank, 1)), w_br_mla=cols(g["w_br_mla"]), w_br_rwkv=cols(g["w_br_rwkv"]),
        w_out=g["w_out"].reshape(N_DEV, -1, g["w_out"].shape[1]))


def _restore_vectors(g, dims):
    rank, RW = dims["rank"], dims["hr"] * dims["hn"]
    mu = g["mu"][0]
    out = dict(
        rwkv_mu=jnp.concatenate([mu[:3 * RW], mu[3 * RW + ROPE:3 * RW + ROPE + 4 * rank]]),
        g_pre=g["g_pre"][0], g_post=g["g_post"][0], mla_q_norm=g["mla_q_norm"][0], mla_kv_norm=g["mla_kv_norm"][0],
        rwkv_w0_f=g["w0_f"][0], rwkv_w0_b=g["w0_b"][0], rwkv_a0_f=g["a0_f"][0], rwkv_a0_b=g["a0_b"][0],
        rwkv_k_k=g["k_k"][0], rwkv_k_a=g["k_a"][0], rwkv_r_k=g["r_k"][0], rwkv_gn_g=g["gn_g"][0],
        rwkv_gn_b=g["gn_b"][0])
    return out


def _dims(inp):
    D = inp["x"].shape[-1]
    QR, KVR = inp["mla_q_norm"].shape[0], inp["mla_kv_norm"].shape[0]
    hm = inp["mla_wq_b"].shape[1] * N_DEV // (NOPE + ROPE)
    hr, hn = inp["rwkv_r_k"].shape
    rank = inp["rwkv_w2_f"].shape[0]
    MW, RW = hm * VDIM, hr * hn
    TAIL = -(-(ROPE + 4 * rank) // LANES) * LANES
    orig, o = {}, 0
    for nme, w in (("q_a", QR), ("kv_a", KVR), ("k_rope", ROPE), ("rkv", 3 * RW), ("lora", 4 * rank), ("z_m", MW),
                   ("z_r", RW), ("gate_m", D), ("gate_r", D)):
        orig[nme] = (o, w)
        o += w
    assert o == inp["w_in"].shape[1] * N_DEV
    lay, d_in_perm = _layout(D, MW, RW, TAIL, QR, KVR)
    perm_off = dict(q_a=lay["q_a"][0], kv_a=lay["kv_a"][0], k_rope=lay["tail"][0], rkv=lay["r"][0],
                    lora=lay["tail"][0] + ROPE, z_m=lay["z_m"][0], z_r=lay["z_r"][0], gate_m=lay["gate_m"][0],
                    gate_r=lay["gate_r"][0])
    segs = [(orig[nme][0], orig[nme][1], perm_off[nme]) for nme in orig]
    return dict(D=D, QR=QR, KVR=KVR, hm=hm, hr=hr, hn=hn, rank=rank, TAIL=TAIL, segs=segs, d_in=o,
                d_in_perm=d_in_perm)


def kernel(x, g_pre, w_in, mla_q_norm, mla_wq_b, mla_kv_norm, mla_wkv_b, rwkv_mu, rwkv_w0_f, rwkv_w2_f, rwkv_w0_b, rwkv_w2_b, rwkv_a0_f, rwkv_a2_f, rwkv_a0_b, rwkv_a2_b, rwkv_k_k, rwkv_k_a, rwkv_r_k, rwkv_gn_g, rwkv_gn_b, w_br_mla, w_br_rwkv, w_out, g_post, loss_target, m_g_pre, m_w_in, m_mla_q_norm, m_mla_wq_b, m_mla_kv_norm, m_mla_wkv_b, m_rwkv_mu, m_rwkv_w0_f, m_rwkv_w2_f, m_rwkv_w0_b, m_rwkv_w2_b, m_rwkv_a0_f, m_rwkv_a2_f, m_rwkv_a0_b, m_rwkv_a2_b, m_rwkv_k_k, m_rwkv_k_a, m_rwkv_r_k, m_rwkv_gn_g, m_rwkv_gn_b, m_w_br_mla, m_w_br_rwkv, m_w_out, m_g_post, v_g_pre, v_w_in, v_mla_q_norm, v_mla_wq_b, v_mla_kv_norm, v_mla_wkv_b, v_rwkv_mu, v_rwkv_w0_f, v_rwkv_w2_f, v_rwkv_w0_b, v_rwkv_w2_b, v_rwkv_a0_f, v_rwkv_a2_f, v_rwkv_a0_b, v_rwkv_a2_b, v_rwkv_k_k, v_rwkv_k_a, v_rwkv_r_k, v_rwkv_gn_g, v_rwkv_gn_b, v_w_br_mla, v_w_br_rwkv, v_w_out, v_g_post):
    inp = dict(locals())
    dims = _dims(inp)
    stored = lambda t, n: t.T if n in _TRANSPOSED else t
    assert _MATS[0] == "w_in"
    shards = [stored(inp[n], n).astype(BF16) for n in _MATS]
    core = lax.axis_index("c").astype(jnp.int32).reshape(1)
    (w_in_slabs,) = _run_exchange(_gather_plan(shards[:1]), name="gather_w_in")
    W = {"w_in_t": _prepare_w_in(w_in_slabs, dims), **_prepare_vectors({n: inp[n] for n in _VECS}, dims)}
    loss, grad_x, g, recv_rest = _local_grads(x[0], loss_target[0], W, dims, exchange=(shards[1:], core))

    new = {}
    *recv_rest, recv_w_in = recv_rest
    g = _restore_vectors(g, dims)
    vsizes = [inp[n].size for n in _VECS] + [1]
    vflat = lambda prefix, src, last: _pack([src[prefix + n].reshape(-1) for n in _VECS] + [last])
    one = jnp.zeros((1,), F32)
    for n, t in zip(_MATS, [recv_w_in] + recv_rest):
        ride = _direct_gather_plan(vflat("", g, loss.reshape(1))) if n == "w_in" else None
        out = _adamw(t, stored(inp[n], n), stored(inp["m_" + n], n), stored(inp["v_" + n], n), name="adamw_" + n,
                     ride=ride)
        if ride:
            *out, vrecv = out
        new[n] = [stored(o, n) for o in out]

    vout = _adamw(vrecv, vflat("", inp, one), vflat("m_", inp, one), vflat("v_", inp, one), name="adamw_vectors")
    vparts = [_unpack(t, vsizes) for t in vout]
    for i, n in enumerate(_VECS):
        new[n] = [vp[i].reshape(inp[n].shape) for vp in vparts]
    loss = vparts[0][-1].reshape(())

    outs = [loss, grad_x[None]]
    for k in range(4):
        outs += [new[n][k] for n in _WEIGHTS]
    return tuple(outs)
```
